```python
import jax, jax.numpy as jnp
from jax import lax
import numpy as np

D_MODEL = 1024
BATCH = 32
SEQ = 2048
DEPTH = 2

HEAD_DIM = 64
N_ATTN_HEADS = 8
N_KV_HEADS = 2
GQA_GROUP = N_ATTN_HEADS // N_KV_HEADS
ATTN_WIDTH = N_ATTN_HEADS * HEAD_DIM
KV_WIDTH = N_KV_HEADS * HEAD_DIM
ROPE_DIM = HEAD_DIM // 4
ROPE_THETA = 500000.0
DILATED_PATTERNS = ((128, 1), (512, 4), (2048, 16))
ATTN_BLOCK = 128

SSM_HEAD_DIM = 64
SSM_HEADS = 16
SSM_INNER = SSM_HEADS * SSM_HEAD_DIM
SSM_GROUPS = 2
D_STATE = 128
CONV_WIDTH = 4
CHUNK = 128
CONV_CH = SSM_INNER + 2 * SSM_GROUPS * D_STATE

MIX_WIDTH = ATTN_WIDTH + SSM_INNER
Q_END = ATTN_WIDTH
K_END = Q_END + KV_WIDTH
V_END = K_END + KV_WIDTH
Z_END = V_END + SSM_INNER
XBC_END = Z_END + CONV_CH
IN_PROJ = XBC_END + SSM_HEADS

FFN_HIDDEN = ((8 * D_MODEL + 3 * 256 - 1) // (3 * 256)) * 256
EPS = 1e-5

kernel_name = "hybrid_dilated_attn_mamba2_block"


def rmsnorm(x, w):
    xf = x.astype(jnp.float32)
    y = xf * lax.rsqrt(jnp.mean(xf * xf, axis=-1, keepdims=True) + EPS)
    return (y * w.astype(jnp.float32)).astype(x.dtype)


def rotary_tables(positions, dtype):
    inv_freq = ROPE_THETA ** (-jnp.arange(0, ROPE_DIM, 2, dtype=jnp.float32) / ROPE_DIM)
    ang = positions.astype(jnp.float32)[..., None] * inv_freq
    return jnp.cos(ang)[:, :, None, :].astype(dtype), jnp.sin(ang)[:, :, None, :].astype(dtype)


def partial_rotary(t, cos, sin):
    half = ROPE_DIM // 2
    t1, t2, rest = t[..., :half], t[..., half:ROPE_DIM], t[..., ROPE_DIM:]
    return jnp.concatenate([t1 * cos - t2 * sin, t2 * cos + t1 * sin, rest], axis=-1)


def dilated_window_branch(q, k, v, window, dilation):
    bsz, s = q.shape[0], q.shape[1]
    length = s // dilation
    w_d = window // dilation
    nb = -(-length // ATTN_BLOCK)
    lp = nb * ATTN_BLOCK
    qd = q.reshape((bsz, length, dilation) + q.shape[2:])
    kd = k.reshape((bsz, length, dilation) + k.shape[2:])
    vd = v.reshape((bsz, length, dilation) + v.shape[2:])
    qd = jnp.pad(qd, [(0, 0), (0, lp - length)] + [(0, 0)] * (qd.ndim - 2))
    kv_pad = [(0, 0), (ATTN_BLOCK, lp - length)] + [(0, 0)] * (kd.ndim - 2)
    kd = jnp.pad(kd, kv_pad)
    vd = jnp.pad(vd, kv_pad)
    qb = qd.reshape((bsz, nb, ATTN_BLOCK) + qd.shape[2:])
    kb = kd.reshape((bsz, nb + 1, ATTN_BLOCK) + kd.shape[2:])
    vb = vd.reshape((bsz, nb + 1, ATTN_BLOCK) + vd.shape[2:])
    kb = jnp.concatenate([kb[:, :-1], kb[:, 1:]], axis=2)
    vb = jnp.concatenate([vb[:, :-1], vb[:, 1:]], axis=2)
    scores = jnp.einsum('bnqrhgc,bnkrhc->bnrhgqk', qb, kb).astype(jnp.float32)
    qi = jnp.arange(ATTN_BLOCK)[:, None]
    ki = jnp.arange(2 * ATTN_BLOCK)[None, :]
    delta = qi + ATTN_BLOCK - ki
    kpos = jnp.arange(nb)[:, None, None] * ATTN_BLOCK - ATTN_BLOCK + ki[None]
    valid = (delta >= 0)[None] & (delta <= w_d)[None] & (kpos >= 0)
    scores = jnp.where(valid[None, :, None, None, None], scores, -jnp.inf)
    m = jnp.max(scores, axis=-1, keepdims=True)
    p = jnp.exp(scores - m)
    den = jnp.sum(p, axis=-1, keepdims=True)
    o = jnp.einsum('bnrhgqk,bnkrhc->bnrhgqc', p, vb.astype(jnp.float32)) / den
    lse = (m + jnp.log(den))[..., 0]
    o = jnp.transpose(o, (0, 1, 5, 2, 3, 4, 6)).reshape((bsz, lp, dilation) + q.shape[2:])
    lse = jnp.transpose(lse, (0, 1, 5, 2, 3, 4)).reshape((bsz, lp, dilation) + q.shape[2:4])
    o = o[:, :length].reshape(q.shape)
    lse = lse[:, :length].reshape(q.shape[:4])
    return o, lse


def dilated_attention(q, k, v):
    outs, lses = [], []
    for window, dilation in DILATED_PATTERNS:
        o, l = dilated_window_branch(q, k, v, window, dilation)
        outs.append(o)
        lses.append(l)
    wts = jax.nn.softmax(jnp.stack(lses, axis=0), axis=0)
    return jnp.einsum('ibshg,ibshgc->bshgc', wts, jnp.stack(outs, axis=0))


def causal_depthwise_conv(u, w, b):
    y = lax.conv_general_dilated(u, w[:, None, :], window_strides=(1,),
                                 padding=[(CONV_WIDTH - 1, 0)],
                                 dimension_numbers=('NWC', 'WIO', 'NWC'),
                                 feature_group_count=u.shape[-1])
    return y + b


def segsum_exp(a):
    cs = jnp.cumsum(a, axis=-1)
    diff = cs[..., :, None] - cs[..., None, :]
    t = a.shape[-1]
    mask = jnp.tril(jnp.ones((t, t), dtype=bool))
    return jnp.exp(jnp.where(mask, diff, -jnp.inf))


def ssd_chunked(xs, dt, a_neg, bm, cm):
    bsz, s, nh, hp = xs.shape
    nc = s // CHUNK
    e = nh // SSM_GROUPS
    xg = (xs.astype(jnp.float32) * dt[..., None]).reshape(bsz, nc, CHUNK, SSM_GROUPS, e, hp)
    a = jnp.transpose((dt * a_neg).reshape(bsz, nc, CHUNK, SSM_GROUPS, e), (0, 1, 3, 4, 2))
    bc = bm.astype(jnp.float32).reshape(bsz, nc, CHUNK, SSM_GROUPS, D_STATE)
    cc = cm.astype(jnp.float32).reshape(bsz, nc, CHUNK, SSM_GROUPS, D_STATE)
    a_cs = jnp.cumsum(a, axis=-1)
    cb = jnp.einsum('bclgn,bcsgn->bcgls', cc, bc)
    m_mat = cb[:, :, :, None] * segsum_exp(a)
    y_diag = jnp.einsum('bcgels,bcsgep->bclgep', m_mat, xg)
    decay_states = jnp.exp(a_cs[..., -1:] - a_cs)
    states = jnp.einsum('bclgn,bcgel,bclgep->bcgepn', bc, decay_states, xg)
    chunk_decay = jnp.exp(a_cs[..., -1])

    def step(h, inp):
        dec, st = inp
        return h * dec[..., None, None] + st, h

    h0 = jnp.zeros((bsz, SSM_GROUPS, e, hp, D_STATE), jnp.float32)
    _, prev = lax.scan(step, h0, (jnp.moveaxis(chunk_decay, 1, 0), jnp.moveaxis(states, 1, 0)))
    prev_states = jnp.moveaxis(prev, 0, 1)
    y_off = jnp.einsum('bclgn,bcgepn,bcgel->bclgep', cc, prev_states, jnp.exp(a_cs))
    return (y_diag + y_off).reshape(bsz, s, nh, hp)


def hybrid_mixer(h, w_in, conv_w, conv_b, dt_bias, a_log, d_skip, ssm_norm, w_out, cos, sin):
    bsz, s, _ = h.shape
    proj = h @ w_in
    q, k, v, z, xbc, dt = jnp.split(proj, [Q_END, K_END, V_END, Z_END, XBC_END], axis=-1)
    q = partial_rotary(q.reshape(bsz, s, N_ATTN_HEADS, HEAD_DIM), cos, sin)
    k = partial_rotary(k.reshape(bsz, s, N_KV_HEADS, HEAD_DIM), cos, sin)
    v = v.reshape(bsz, s, N_KV_HEADS, HEAD_DIM)
    q = (q * (HEAD_DIM ** -0.5)).reshape(bsz, s, N_KV_HEADS, GQA_GROUP, HEAD_DIM)
    attn = dilated_attention(q, k, v).reshape(bsz, s, ATTN_WIDTH).astype(h.dtype)
    xbc = jax.nn.silu(causal_depthwise_conv(xbc, conv_w, conv_b))
    xs, bm, cm = jnp.split(xbc, [SSM_INNER, SSM_INNER + SSM_GROUPS * D_STATE], axis=-1)
    xs = xs.reshape(bsz, s, SSM_HEADS, SSM_HEAD_DIM)
    bm = bm.reshape(bsz, s, SSM_GROUPS, D_STATE)
    cm = cm.reshape(bsz, s, SSM_GROUPS, D_STATE)
    dt = jax.nn.softplus(dt.astype(jnp.float32) + dt_bias.astype(jnp.float32))
    a_neg = -jnp.exp(a_log.astype(jnp.float32))
    y = ssd_chunked(xs, dt, a_neg, bm, cm) + d_skip.astype(jnp.float32)[:, None] * xs.astype(jnp.float32)
    y = y.reshape(bsz, s, SSM_INNER).astype(h.dtype) * jax.nn.silu(z)
    gsize = SSM_INNER // SSM_GROUPS
    y = rmsnorm(y.reshape(bsz, s, SSM_GROUPS, gsize), ssm_norm.reshape(SSM_GROUPS, gsize))
    y = y.reshape(bsz, s, SSM_INNER)
    return jnp.concatenate([attn, y], axis=-1) @ w_out


def swiglu(h, w_gate, w_up, w_down):
    return (jax.nn.silu(h @ w_gate) * (h @ w_up)) @ w_down


def _fwd_setup_inputs(seed: int = 0) -> dict:
    key = jax.random.key(seed)
    ks = jax.random.split(key, 16)
    f32 = jnp.float32
    x = jax.random.normal(ks[0], (BATCH, SEQ, D_MODEL), f32)
    positions = jnp.broadcast_to(jnp.arange(SEQ, dtype=jnp.int32), (BATCH, SEQ))
    norm_mix = 1.0 + 0.02 * jax.random.normal(ks[1], (DEPTH, D_MODEL), f32)
    w_in = jax.random.normal(ks[2], (DEPTH, D_MODEL, IN_PROJ), f32) * D_MODEL ** -0.5
    conv_w = jax.random.normal(ks[3], (DEPTH, CONV_WIDTH, CONV_CH), f32) * CONV_WIDTH ** -0.5
    conv_b = 0.01 * jax.random.normal(ks[4], (DEPTH, CONV_CH), f32)
    dt0 = jnp.exp(jax.random.uniform(ks[5], (DEPTH, SSM_HEADS), f32, np.log(1e-3), np.log(1e-1)))
    dt_bias = dt0 + jnp.log(-jnp.expm1(-dt0))
    a_log = jnp.log(jax.random.uniform(ks[6], (DEPTH, SSM_HEADS), f32, 1.0, 16.0))
    d_skip = 1.0 + 0.1 * jax.random.normal(ks[7], (DEPTH, SSM_HEADS), f32)
    ssm_norm = 1.0 + 0.02 * jax.random.normal(ks[8], (DEPTH, SSM_INNER), f32)
    w_out = jax.random.normal(ks[9], (DEPTH, MIX_WIDTH, D_MODEL), f32) * MIX_WIDTH ** -0.5
    norm_ffn = 1.0 + 0.02 * jax.random.normal(ks[10], (DEPTH, D_MODEL), f32)
    w_gate = jax.random.normal(ks[11], (DEPTH, D_MODEL, FFN_HIDDEN), f32) * D_MODEL ** -0.5
    w_up = jax.random.normal(ks[12], (DEPTH, D_MODEL, FFN_HIDDEN), f32) * D_MODEL ** -0.5
    w_down = jax.random.normal(ks[13], (DEPTH, FFN_HIDDEN, D_MODEL), f32) * FFN_HIDDEN ** -0.5
    final_norm = 1.0 + 0.02 * jax.random.normal(ks[14], (D_MODEL,), f32)
    return {"x": x, "positions": positions, "norm_mix": norm_mix, "w_in": w_in,
            "conv_w": conv_w, "conv_b": conv_b, "dt_bias": dt_bias, "a_log": a_log,
            "d_skip": d_skip, "ssm_norm": ssm_norm, "w_out": w_out, "norm_ffn": norm_ffn,
            "w_gate": w_gate, "w_up": w_up, "w_down": w_down, "final_norm": final_norm}


def _fwd_reference(x, positions, norm_mix, w_in, conv_w, conv_b, dt_bias, a_log, d_skip, ssm_norm,
              w_out, norm_ffn, w_gate, w_up, w_down, final_norm):
    cos, sin = rotary_tables(positions, x.dtype)
    h = x
    for layer in range(DEPTH):
        h = h + hybrid_mixer(rmsnorm(h, norm_mix[layer]), w_in[layer], conv_w[layer],
                             conv_b[layer], dt_bias[layer], a_log[layer], d_skip[layer],
                             ssm_norm[layer], w_out[layer], cos, sin)
        h = h + swiglu(rmsnorm(h, norm_ffn[layer]), w_gate[layer], w_up[layer], w_down[layer])
    return rmsnorm(h, final_norm)


import jax as _jax
import jax.numpy as _jnp

TWIN_FORMAT = 'train_step'
FWD_PARAMS = ['x', 'positions', 'norm_mix', 'w_in', 'conv_w', 'conv_b', 'dt_bias', 'a_log', 'd_skip', 'ssm_norm', 'w_out', 'norm_ffn', 'w_gate', 'w_up', 'w_down', 'final_norm']
TWIN_WEIGHTS = ['norm_mix', 'w_in', 'conv_w', 'conv_b', 'dt_bias', 'a_log', 'd_skip', 'ssm_norm', 'w_out', 'norm_ffn', 'w_gate', 'w_up', 'w_down', 'final_norm']
TWIN_DIFF_INPUT = 'x'
TWIN_INPUTS = ['x', 'positions', 'norm_mix', 'w_in', 'conv_w', 'conv_b', 'dt_bias', 'a_log', 'd_skip', 'ssm_norm', 'w_out', 'norm_ffn', 'w_gate', 'w_up', 'w_down', 'final_norm', 'loss_target', 'm_norm_mix', 'm_w_in', 'm_conv_w', 'm_conv_b', 'm_dt_bias', 'm_a_log', 'm_d_skip', 'm_ssm_norm', 'm_w_out', 'm_norm_ffn', 'm_w_gate', 'm_w_up', 'm_w_down', 'm_final_norm', 'v_norm_mix', 'v_w_in', 'v_conv_w', 'v_conv_b', 'v_dt_bias', 'v_a_log', 'v_d_skip', 'v_ssm_norm', 'v_w_out', 'v_norm_ffn', 'v_w_gate', 'v_w_up', 'v_w_down', 'v_final_norm']
TWIN_OUTPUTS = ['loss', 'grad_x', 'grad_norm_mix', 'grad_w_in', 'grad_conv_w', 'grad_conv_b', 'grad_dt_bias', 'grad_a_log', 'grad_d_skip', 'grad_ssm_norm', 'grad_w_out', 'grad_norm_ffn', 'grad_w_gate', 'grad_w_up', 'grad_w_down', 'grad_final_norm', 'delta_norm_mix', 'delta_w_in', 'delta_conv_w', 'delta_conv_b', 'delta_dt_bias', 'delta_a_log', 'delta_d_skip', 'delta_ssm_norm', 'delta_w_out', 'delta_norm_ffn', 'delta_w_gate', 'delta_w_up', 'delta_w_down', 'delta_final_norm', 'new_m_norm_mix', 'new_m_w_in', 'new_m_conv_w', 'new_m_conv_b', 'new_m_dt_bias', 'new_m_a_log', 'new_m_d_skip', 'new_m_ssm_norm', 'new_m_w_out', 'new_m_norm_ffn', 'new_m_w_gate', 'new_m_w_up', 'new_m_w_down', 'new_m_final_norm', 'new_v_norm_mix', 'new_v_w_in', 'new_v_conv_w', 'new_v_conv_b', 'new_v_dt_bias', 'new_v_a_log', 'new_v_d_skip', 'new_v_ssm_norm', 'new_v_w_out', 'new_v_norm_ffn', 'new_v_w_gate', 'new_v_w_up', 'new_v_w_down', 'new_v_final_norm']
TWIN_LEAF_KINDS = {'loss': 'loss', 'grad_x': 'grad_x', 'grad_norm_mix': 'grad_w', 'grad_w_in': 'grad_w', 'grad_conv_w': 'grad_w', 'grad_conv_b': 'grad_w', 'grad_dt_bias': 'grad_w', 'grad_a_log': 'grad_w', 'grad_d_skip': 'grad_w', 'grad_ssm_norm': 'grad_w', 'grad_w_out': 'grad_w', 'grad_norm_ffn': 'grad_w', 'grad_w_gate': 'grad_w', 'grad_w_up': 'grad_w', 'grad_w_down': 'grad_w', 'grad_final_norm': 'grad_w', 'delta_norm_mix': 'delta_w', 'delta_w_in': 'delta_w', 'delta_conv_w': 'delta_w', 'delta_conv_b': 'delta_w', 'delta_dt_bias': 'delta_w', 'delta_a_log': 'delta_w', 'delta_d_skip': 'delta_w', 'delta_ssm_norm': 'delta_w', 'delta_w_out': 'delta_w', 'delta_norm_ffn': 'delta_w', 'delta_w_gate': 'delta_w', 'delta_w_up': 'delta_w', 'delta_w_down': 'delta_w', 'delta_final_norm': 'delta_w', 'new_m_norm_mix': 'new_m', 'new_m_w_in': 'new_m', 'new_m_conv_w': 'new_m', 'new_m_conv_b': 'new_m', 'new_m_dt_bias': 'new_m', 'new_m_a_log': 'new_m', 'new_m_d_skip': 'new_m', 'new_m_ssm_norm': 'new_m', 'new_m_w_out': 'new_m', 'new_m_norm_ffn': 'new_m', 'new_m_w_gate': 'new_m', 'new_m_w_up': 'new_m', 'new_m_w_down': 'new_m', 'new_m_final_norm': 'new_m', 'new_v_norm_mix': 'new_v', 'new_v_w_in': 'new_v', 'new_v_conv_w': 'new_v', 'new_v_conv_b': 'new_v', 'new_v_dt_bias': 'new_v', 'new_v_a_log': 'new_v', 'new_v_d_skip': 'new_v', 'new_v_ssm_norm': 'new_v', 'new_v_w_out': 'new_v', 'new_v_norm_ffn': 'new_v', 'new_v_w_gate': 'new_v', 'new_v_w_up': 'new_v', 'new_v_w_down': 'new_v', 'new_v_final_norm': 'new_v'}


def _forward(args):
    return _fwd_reference(*[args[k] for k in FWD_PARAMS])


def _output_shape():
    out = _jax.eval_shape(lambda: _forward(_fwd_setup_inputs(0)))
    return out.shape, out.dtype

N_MICROBATCH = 1
ADAM_LR = 0.001
ADAM_B1 = 0.9
ADAM_B2 = 0.999
ADAM_EPS = 1e-08
ADAM_WD = 0.01
ADAM_STEP = 10
PER_EXAMPLE_BATCH_AXIS = {'x': 0, 'positions': 0, 'loss_target': 0}
SHARED_INPUTS = []
_WEIGHT_DTYPES = {'norm_mix': _jnp.float32, 'w_in': _jnp.float32, 'conv_w': _jnp.float32, 'conv_b': _jnp.float32, 'dt_bias': _jnp.float32, 'a_log': _jnp.float32, 'd_skip': _jnp.float32, 'ssm_norm': _jnp.float32, 'w_out': _jnp.float32, 'norm_ffn': _jnp.float32, 'w_gate': _jnp.float32, 'w_up': _jnp.float32, 'w_down': _jnp.float32, 'final_norm': _jnp.float32}
MOMENT_SCALE = {'norm_mix': 2.760249e-01, 'w_in': 1.473368e-01, 'conv_w': 1.584071e-01, 'conv_b': 2.224589e-01, 'dt_bias': 3.948716e-01, 'a_log': 5.005316e-01, 'd_skip': 8.159636e-01, 'ssm_norm': 1.735080e-01, 'w_out': 1.765994e-01, 'norm_ffn': 1.545493e-01, 'w_gate': 6.712799e-02, 'w_up': 6.518062e-02, 'w_down': 1.077152e-01, 'final_norm': 6.392534e+01}


def _to_microbatches(a, axis):
    t = _jnp.moveaxis(a, axis, 0)
    t = t.reshape((N_MICROBATCH, t.shape[0] // N_MICROBATCH) + t.shape[1:])
    return _jnp.moveaxis(t, 1, axis + 1)


def setup_inputs(seed: int = 0) -> dict:
    inp = _fwd_setup_inputs(seed)
    key = _jax.random.fold_in(_jax.random.key(seed), 7919)
    shape, _ = _output_shape()
    out = dict(inp)
    out["loss_target"] = _jax.random.normal(_jax.random.fold_in(key, 0), shape, _jnp.float32)
    for i, name in enumerate(TWIN_WEIGHTS):
        w = inp[name].astype(_jnp.float32)
        if MOMENT_SCALE is None:
            s = _jnp.sqrt(_jnp.mean(_jnp.square(w)) + 1e-30)
        else:
            s = MOMENT_SCALE[name]
        km, kv = _jax.random.split(_jax.random.fold_in(key, i + 1))
        out[name] = w
        out["m_" + name] = s * _jax.random.normal(km, w.shape, _jnp.float32)
        out["v_" + name] = (s * s) * _jax.random.uniform(kv, w.shape, _jnp.float32, 0.5, 1.5)
    if N_MICROBATCH > 1:
        for name, axis in PER_EXAMPLE_BATCH_AXIS.items():
            out[name] = _to_microbatches(out[name], axis)
    return {'x': out['x'], 'positions': out['positions'], 'norm_mix': out['norm_mix'], 'w_in': out['w_in'], 'conv_w': out['conv_w'], 'conv_b': out['conv_b'], 'dt_bias': out['dt_bias'], 'a_log': out['a_log'], 'd_skip': out['d_skip'], 'ssm_norm': out['ssm_norm'], 'w_out': out['w_out'], 'norm_ffn': out['norm_ffn'], 'w_gate': out['w_gate'], 'w_up': out['w_up'], 'w_down': out['w_down'], 'final_norm': out['final_norm'], 'loss_target': out['loss_target'], 'm_norm_mix': out['m_norm_mix'], 'm_w_in': out['m_w_in'], 'm_conv_w': out['m_conv_w'], 'm_conv_b': out['m_conv_b'], 'm_dt_bias': out['m_dt_bias'], 'm_a_log': out['m_a_log'], 'm_d_skip': out['m_d_skip'], 'm_ssm_norm': out['m_ssm_norm'], 'm_w_out': out['m_w_out'], 'm_norm_ffn': out['m_norm_ffn'], 'm_w_gate': out['m_w_gate'], 'm_w_up': out['m_w_up'], 'm_w_down': out['m_w_down'], 'm_final_norm': out['m_final_norm'], 'v_norm_mix': out['v_norm_mix'], 'v_w_in': out['v_w_in'], 'v_conv_w': out['v_conv_w'], 'v_conv_b': out['v_conv_b'], 'v_dt_bias': out['v_dt_bias'], 'v_a_log': out['v_a_log'], 'v_d_skip': out['v_d_skip'], 'v_ssm_norm': out['v_ssm_norm'], 'v_w_out': out['v_w_out'], 'v_norm_ffn': out['v_norm_ffn'], 'v_w_gate': out['v_w_gate'], 'v_w_up': out['v_w_up'], 'v_w_down': out['v_w_down'], 'v_final_norm': out['v_final_norm']}


def _loss(weights, diff, rest, loss_target):
    with _jax.named_scope("forward"):
        args = {**rest, TWIN_DIFF_INPUT: diff, **{k: w.astype(_WEIGHT_DTYPES[k]) for k, w in weights.items()}}
        y = _forward(args)
    with _jax.named_scope("loss_head"):
        err = _jnp.square(y.astype(_jnp.float32) - loss_target)
        return 0.5 * _jnp.sum(_jnp.mean(err, axis=-1)) if err.ndim else 0.5 * err


def _adamw(w, g, m, v):
    m = ADAM_B1 * m + (1.0 - ADAM_B1) * g
    v = ADAM_B2 * v + (1.0 - ADAM_B2) * _jnp.square(g)
    m_hat = m / (1.0 - ADAM_B1 ** ADAM_STEP)
    v_hat = v / (1.0 - ADAM_B2 ** ADAM_STEP)
    delta = -ADAM_LR * (m_hat / (_jnp.sqrt(v_hat) + ADAM_EPS) + ADAM_WD * w)
    return delta, m, v


def reference(x, positions, norm_mix, w_in, conv_w, conv_b, dt_bias, a_log, d_skip, ssm_norm, w_out, norm_ffn, w_gate, w_up, w_down, final_norm, loss_target, m_norm_mix, m_w_in, m_conv_w, m_conv_b, m_dt_bias, m_a_log, m_d_skip, m_ssm_norm, m_w_out, m_norm_ffn, m_w_gate, m_w_up, m_w_down, m_final_norm, v_norm_mix, v_w_in, v_conv_w, v_conv_b, v_dt_bias, v_a_log, v_d_skip, v_ssm_norm, v_w_out, v_norm_ffn, v_w_gate, v_w_up, v_w_down, v_final_norm):
    given = dict(x=x, positions=positions, norm_mix=norm_mix, w_in=w_in, conv_w=conv_w, conv_b=conv_b, dt_bias=dt_bias, a_log=a_log, d_skip=d_skip, ssm_norm=ssm_norm, w_out=w_out, norm_ffn=norm_ffn, w_gate=w_gate, w_up=w_up, w_down=w_down, final_norm=final_norm, loss_target=loss_target, m_norm_mix=m_norm_mix, m_w_in=m_w_in, m_conv_w=m_conv_w, m_conv_b=m_conv_b, m_dt_bias=m_dt_bias, m_a_log=m_a_log, m_d_skip=m_d_skip, m_ssm_norm=m_ssm_norm, m_w_out=m_w_out, m_norm_ffn=m_norm_ffn, m_w_gate=m_w_gate, m_w_up=m_w_up, m_w_down=m_w_down, m_final_norm=m_final_norm, v_norm_mix=v_norm_mix, v_w_in=v_w_in, v_conv_w=v_conv_w, v_conv_b=v_conv_b, v_dt_bias=v_dt_bias, v_a_log=v_a_log, v_d_skip=v_d_skip, v_ssm_norm=v_ssm_norm, v_w_out=v_w_out, v_norm_ffn=v_norm_ffn, v_w_gate=v_w_gate, v_w_up=v_w_up, v_w_down=v_w_down, v_final_norm=v_final_norm)
    weights = {n: given[n] for n in TWIN_WEIGHTS}
    shared = {n: given[n] for n in SHARED_INPUTS}
    per_example = {n: given[n] for n in ['x', 'positions']}
    grad_fn = _jax.value_and_grad(_loss, argnums=(0, 1))

    def one_microbatch(ex, loss_target):
        ex = dict(ex)
        diff = ex.pop(TWIN_DIFF_INPUT)
        return grad_fn(weights, diff, {**shared, **ex}, loss_target)

    if N_MICROBATCH == 1:
        loss, (grad_w, grad_x) = one_microbatch(per_example, given["loss_target"])
    else:
        def body(carry, xs):
            loss_sum, grad_sum = carry
            l_k, (gw_k, gx_k) = one_microbatch(xs[0], xs[1])
            with _jax.named_scope("update"):
                return (loss_sum + l_k, _jax.tree.map(_jnp.add, grad_sum, gw_k)), gx_k

        init = (_jnp.zeros((), _jnp.float32), _jax.tree.map(_jnp.zeros_like, weights))
        (loss, grad_w), grad_x = _jax.lax.scan(body, init, (per_example, given["loss_target"]))
    with _jax.named_scope("update"):
        delta_w, new_m, new_v = {}, {}, {}
        for n in TWIN_WEIGHTS:
            delta_w[n], new_m[n], new_v[n] = _adamw(weights[n], grad_w[n], given["m_" + n], given["v_" + n])
    return (loss, grad_x, *[grad_w[n] for n in TWIN_WEIGHTS], *[delta_w[n] for n in TWIN_WEIGHTS],
            *[new_m[n] for n in TWIN_WEIGHTS], *[new_v[n] for n in TWIN_WEIGHTS])
```

```python
import functools
import math

import jax
import jax.numpy as jnp
import numpy as np
from jax import lax
from jax.experimental import pallas as pl
from jax.experimental.pallas import tpu as pltpu

f32 = jnp.float32
bf16 = jnp.bfloat16

D_MODEL = 1024
SEQ = 2048
DEPTH = 2
HEAD_DIM = 64
N_ATTN_HEADS = 8
N_KV_HEADS = 2
ATTN_WIDTH = 512
KV_WIDTH = 128
ROPE_DIM = 16
ROPE_THETA = 500000.0
DILATIONS = (1, 4, 16)
ATTN_BLOCK = 128
SSM_HEADS = 16
SSM_INNER = 1024
SSM_GROUPS = 2
D_STATE = 128
CONV_WIDTH = 4
CHUNK = 128
CONV_CH = 1536
MIX_WIDTH = 1536
QKV_WIDTH = ATTN_WIDTH + 2 * KV_WIDTH
Z_OFF = 768
XBC_OFF = 1792
DT_OFF = 3328
IN_PROJ = 3344
IN_PROJ_PAD = 3456
FFN_HIDDEN = 2816
EPS = 1e-5
N_DEV = 8
ADAM_LR = 0.001
ADAM_B1 = 0.9
ADAM_B2 = 0.999
ADAM_EPS = 1e-08
ADAM_WD = 0.01
ADAM_STEP = 10

LANES = 128
SUBLANES = 8
VMEM_LIMIT = 56 * 1024 * 1024

MESH = pl.DeviceIdType.MESH
ANY = pl.BlockSpec(memory_space=pl.ANY)


def _cparams(sem, vmem=None):
    return pltpu.CompilerParams(dimension_semantics=sem, vmem_limit_bytes=vmem or VMEM_LIMIT)


def _sigmoid(x):
    return 1.0 / (1.0 + jnp.exp(-x))


def _silu(x):
    return x * _sigmoid(x)


def _dsilu(x):
    s = _sigmoid(x)
    return s * (1.0 + x * (1.0 - s))


def _softplus(x):
    return jnp.maximum(x, 0.0) + jnp.log(1.0 + jnp.exp(-jnp.abs(x)))


def _dot(a, b, dims, precision=None):
    return lax.dot_general(a, b, (dims, ((), ())), preferred_element_type=f32, precision=precision)


def _nn(a, b, precision=None):
    return _dot(a, b, ((1,), (0,)), precision)


def _nt(a, b):
    return _dot(a, b, ((1,), (1,)))


def _tn(a, b):
    return _dot(a, b, ((0,), (0,)))


def _rowsum8(t):
    n, w = t.shape
    return jnp.sum(t.reshape(n // SUBLANES, SUBLANES, w), axis=0)


def _matmul(a, b, *, mode, n_out=None, b_off=0, a_koff=0, b_koff=0, k_len=None, add=None, out_dtype=f32, tm=1024, tn=512, tk=1024, name):
    if mode == "tn":
        kdim_a, m = a.shape
    else:
        m, kdim_a = a.shape
    kk = k_len if k_len is not None else kdim_a
    n = n_out if n_out is not None else (b.shape[0] if mode == "nt" else b.shape[1])
    tm, tn, tk = min(tm, m), min(tn, n), min(tk, kk)
    assert m % tm == 0 and n % tn == 0 and kk % tk == 0, (name, m, n, kk, tm, tn, tk)
    nk = kk // tk
    if mode == "nn":
        a_spec = pl.BlockSpec((tm, tk), lambda i, j, k: (i, k + a_koff))
        b_spec = pl.BlockSpec((tk, tn), lambda i, j, k: (k + b_koff, j + b_off))
        dims = ((1,), (0,))
    elif mode == "nt":
        a_spec = pl.BlockSpec((tm, tk), lambda i, j, k: (i, k + a_koff))
        b_spec = pl.BlockSpec((tn, tk), lambda i, j, k: (j + b_off, k + b_koff))
        dims = ((1,), (1,))
    else:
        a_spec = pl.BlockSpec((tk, tm), lambda i, j, k: (k + a_koff, i))
        b_spec = pl.BlockSpec((tk, tn), lambda i, j, k: (k + b_koff, j + b_off))
        dims = ((0,), (0,))
    o_spec = pl.BlockSpec((tm, tn), lambda i, j, k: (i, j))
    has_add = add is not None

    def body(*refs):
        if has_add:
            a_ref, b_ref, add_ref, o_ref, acc_ref = refs
        else:
            a_ref, b_ref, o_ref, acc_ref = refs
        k = pl.program_id(2)
        part = _dot(a_ref[...].astype(bf16), b_ref[...].astype(bf16), dims)

        @pl.when(k == 0)
        def _():
            acc_ref[...] = part

        @pl.when(k > 0)
        def _():
            acc_ref[...] += part

        @pl.when(k == nk - 1)
        def _():
            r = acc_ref[...]
            if has_add:
                r = r + add_ref[...]
            o_ref[...] = r.astype(out_dtype)

    in_specs = [a_spec, b_spec] + ([o_spec] if has_add else [])
    args = (a, b) + ((add,) if has_add else ())
    return pl.pallas_call(
        body, name=name, grid=(m // tm, n // tn, nk), in_specs=in_specs, out_specs=o_spec,
        out_shape=jax.ShapeDtypeStruct((m, n), out_dtype), scratch_shapes=[pltpu.VMEM((tm, tn), f32)],
        compiler_params=_cparams(("parallel", "parallel", "arbitrary")),
    )(*args)


def _swiglu_fwd(hn, w_gate, w_up, name, tm=1024, tn=256):
    m, k = hn.shape
    n = w_gate.shape[1]

    def body(a_ref, wg_ref, wu_ref, g_ref, u_ref, act_ref):
        a = a_ref[...]
        g = _nn(a, wg_ref[...])
        u = _nn(a, wu_ref[...])
        g_ref[...] = g
        u_ref[...] = u
        act_ref[...] = (_silu(g) * u).astype(bf16)

    a_spec = pl.BlockSpec((tm, k), lambda i, j: (i, 0))
    w_spec = pl.BlockSpec((k, tn), lambda i, j: (0, j))
    o_spec = pl.BlockSpec((tm, tn), lambda i, j: (i, j))
    return pl.pallas_call(
        body, name=name, grid=(m // tm, n // tn), in_specs=[a_spec, w_spec, w_spec], out_specs=[o_spec, o_spec, o_spec],
        out_shape=[jax.ShapeDtypeStruct((m, n), f32), jax.ShapeDtypeStruct((m, n), f32), jax.ShapeDtypeStruct((m, n), bf16)],
        compiler_params=_cparams(("parallel", "parallel")),
    )(hn, w_gate, w_up)


def _swiglu_bwd(dh, w_down, g, u, name, tm=1024, tn=256):
    m, k = dh.shape
    n = w_down.shape[0]

    def body(a_ref, w_ref, g_ref, u_ref, dg_ref, du_ref):
        dact = _nt(a_ref[...].astype(bf16), w_ref[...])
        gg = g_ref[...]
        dg_ref[...] = (dact * u_ref[...] * _dsilu(gg)).astype(bf16)
        du_ref[...] = (dact * _silu(gg)).astype(bf16)

    a_spec = pl.BlockSpec((tm, k), lambda i, j: (i, 0))
    w_spec = pl.BlockSpec((tn, k), lambda i, j: (j, 0))
    o_spec = pl.BlockSpec((tm, tn), lambda i, j: (i, j))
    return pl.pallas_call(
        body, name=name, grid=(m // tm, n // tn), in_specs=[a_spec, w_spec, o_spec, o_spec], out_specs=[o_spec, o_spec],
        out_shape=[jax.ShapeDtypeStruct((m, n), bf16), jax.ShapeDtypeStruct((m, n), bf16)],
        compiler_params=_cparams(("parallel", "parallel")),
    )(dh, w_down, g, u)


def _rmsnorm_fwd(h, w, name, tm=512):
    m, d = h.shape

    def body(h_ref, w_ref, o_ref):
        x = h_ref[...]
        r = lax.rsqrt(jnp.mean(x * x, axis=-1, keepdims=True) + EPS)
        o_ref[...] = (x * r * w_ref[...]).astype(bf16)

    return pl.pallas_call(
        body, name=name, grid=(m // tm,),
        in_specs=[pl.BlockSpec((tm, d), lambda i: (i, 0)), pl.BlockSpec((1, d), lambda i: (0, 0))],
        out_specs=pl.BlockSpec((tm, d), lambda i: (i, 0)), out_shape=jax.ShapeDtypeStruct((m, d), bf16),
        compiler_params=_cparams(("parallel",)),
    )(h, w)


def _rmsnorm_bwd(dhn, h, w, dres, name, tm=512):
    m, d = h.shape

    def body(dhn_ref, h_ref, w_ref, dres_ref, dh_ref, dw_ref):
        x = h_ref[...]
        r = lax.rsqrt(jnp.mean(x * x, axis=-1, keepdims=True) + EPS)
        xhat = x * r
        dy = dhn_ref[...]
        gw = dy * w_ref[...]
        dh_ref[...] = dres_ref[...] + r * (gw - xhat * jnp.mean(gw * xhat, axis=-1, keepdims=True))
        part = _rowsum8(dy * xhat)

        @pl.when(pl.program_id(0) == 0)
        def _():
            dw_ref[...] = part

        @pl.when(pl.program_id(0) > 0)
        def _():
            dw_ref[...] += part

    row = pl.BlockSpec((tm, d), lambda i: (i, 0))
    return pl.pallas_call(
        body, name=name, grid=(m // tm,),
        in_specs=[row, row, pl.BlockSpec((1, d), lambda i: (0, 0)), row],
        out_specs=[row, pl.BlockSpec((SUBLANES, d), lambda i: (0, 0))],
        out_shape=[jax.ShapeDtypeStruct((m, d), f32), jax.ShapeDtypeStruct((SUBLANES, d), f32)],
        compiler_params=_cparams(("arbitrary",)),
    )(dhn, h, w, dres)


def _final_loss(h, w, target, name, tm=512):
    m, d = h.shape

    def body(h_ref, w_ref, t_ref, dh_ref, loss_ref, dw_ref):
        x = h_ref[...]
        r = lax.rsqrt(jnp.mean(x * x, axis=-1, keepdims=True) + EPS)
        xhat = x * r
        ww = w_ref[...]
        err = xhat * ww - t_ref[...]
        dy = err * (1.0 / d)
        gw = dy * ww
        dh_ref[...] = r * (gw - xhat * jnp.mean(gw * xhat, axis=-1, keepdims=True))
        lpart = _rowsum8(err * err) * (0.5 / d)
        wpart = _rowsum8(dy * xhat)

        @pl.when(pl.program_id(0) == 0)
        def _():
            loss_ref[...] = lpart
            dw_ref[...] = wpart

        @pl.when(pl.program_id(0) > 0)
        def _():
            loss_ref[...] += lpart
            dw_ref[...] += wpart

    row = pl.BlockSpec((tm, d), lambda i: (i, 0))
    acc = pl.BlockSpec((SUBLANES, d), lambda i: (0, 0))
    return pl.pallas_call(
        body, name=name, grid=(m // tm,),
        in_specs=[row, pl.BlockSpec((1, d), lambda i: (0, 0)), row], out_specs=[row, acc, acc],
        out_shape=[jax.ShapeDtypeStruct((m, d), f32), jax.ShapeDtypeStruct((SUBLANES, d), f32), jax.ShapeDtypeStruct((SUBLANES, d), f32)],
        compiler_params=_cparams(("arbitrary",)),
    )(h, w, target)


def _lane_tables():
    f = np.arange(LANES) % HEAD_DIM
    inv = ROPE_THETA ** (-jnp.arange(0, ROPE_DIM, 2, dtype=f32) / ROPE_DIM)
    invf = jnp.where(f < ROPE_DIM, inv[f % (ROPE_DIM // 2)], 0.0).astype(f32)
    return invf.reshape(1, LANES)


def _rope_tables(pos_col, name):
    t = pos_col.shape[0]
    tm = SEQ

    def body(p_ref, f_ref, c_ref, s1_ref, s2_ref):
        ang = p_ref[...].astype(f32) * f_ref[...]
        co, si = jnp.cos(ang), jnp.sin(ang)
        f = lax.broadcasted_iota(jnp.int32, (tm, LANES), 1) % HEAD_DIM
        c_ref[...] = jnp.where(f < ROPE_DIM, co, 1.0)
        s1_ref[...] = jnp.where(f < ROPE_DIM // 2, -si, 0.0)
        s2_ref[...] = jnp.where((f >= ROPE_DIM // 2) & (f < ROPE_DIM), si, 0.0)

    row = pl.BlockSpec((tm, LANES), lambda i: (i, 0))
    return pl.pallas_call(
        body, name=name, grid=(t // tm,),
        in_specs=[pl.BlockSpec((tm, 1), lambda i: (i, 0)), pl.BlockSpec((1, LANES), lambda i: (0, 0))],
        out_specs=[row, row, row], out_shape=[jax.ShapeDtypeStruct((t, LANES), f32)] * 3,
        compiler_params=_cparams(("parallel",)),
    )(pos_col, _lane_tables())


def _rot(x, c, s1, s2):
    return x * c + pltpu.roll(x, LANES - ROPE_DIM // 2, 1) * s1 + pltpu.roll(x, ROPE_DIM // 2, 1) * s2


def _rot_t(g, c, s1, s2):
    return g * c + pltpu.roll(g * s1, ROPE_DIM // 2, 1) + pltpu.roll(g * s2, LANES - ROPE_DIM // 2, 1)


def _dup_head(x, kvh, low):
    a = jnp.where(kvh == 0, x, pltpu.roll(x, HEAD_DIM, 1))
    return jnp.where(low, a, pltpu.roll(a, HEAD_DIM, 1))


def _deinterleave(src_ref, dst_ref, d, dtype):
    length = SEQ // d
    if d == 1:
        dst_ref[...] = src_ref[...].astype(dtype)
    else:
        for r in range(d):
            dst_ref[pl.ds(r * length, length), :] = src_ref[pl.ds(r, length, stride=d), :].astype(dtype)


def _interleave_store(src_ref, dst_ref, d, accumulate):
    length = SEQ // d
    if d == 1:
        if accumulate:
            dst_ref[...] += src_ref[...]
        else:
            dst_ref[...] = src_ref[...]
    else:
        for r in range(d):
            blk = src_ref[pl.ds(r * length, length), :]
            if accumulate:
                dst_ref[pl.ds(r, length, stride=d), :] = dst_ref[pl.ds(r, length, stride=d), :] + blk
            else:
                dst_ref[pl.ds(r, length, stride=d), :] = blk


def _attn_masks():
    qi = lax.broadcasted_iota(jnp.int32, (ATTN_BLOCK, ATTN_BLOCK), 0)
    ki = lax.broadcasted_iota(jnp.int32, (ATTN_BLOCK, ATTN_BLOCK), 1)
    low = lax.broadcasted_iota(jnp.int32, (ATTN_BLOCK, LANES), 1) < HEAD_DIM
    return ki <= qi, ki >= qi, low


NEG_INF = float("-inf")


def _attn_fwd(qkv, tabs, name):
    t = qkv.shape[0]
    nb = t // SEQ
    n_blk = SEQ // ATTN_BLOCK

    def body(q_ref, k_ref, v_ref, c_ref, s1_ref, s2_ref, o_ref, lse_ref,
             qr, kr, vr, qd, kd, vd, ob, lb, o0, o1, o2, l0, l1, l2):
        kvh = pl.program_id(1) // 2
        cur_ok, prev_ok, low = _attn_masks()
        lowfull = lax.broadcasted_iota(jnp.int32, (SEQ, LANES), 1) < HEAD_DIM
        c, s1, s2 = c_ref[...], s1_ref[...], s2_ref[...]
        qr[...] = _rot(q_ref[...], c, s1, s2) * (HEAD_DIM ** -0.5)
        kr[...] = _dup_head(_rot(k_ref[...], c, s1, s2), kvh, lowfull)
        vr[...] = _dup_head(v_ref[...], kvh, lowfull)
        onat, lnat = (o0, o1, o2), (l0, l1, l2)
        for bi, d in enumerate(DILATIONS):
            _deinterleave(qr, qd, d, bf16)
            _deinterleave(kr, kd, d, bf16)
            _deinterleave(vr, vd, d, bf16)
            per_res = n_blk // d

            def block(n, carry):
                start = pl.multiple_of(n * ATTN_BLOCK, ATTN_BLOCK)
                has_prev = (n % per_res) != 0
                pstart = pl.multiple_of(jnp.maximum(n - 1, 0) * ATTN_BLOCK, ATTN_BLOCK)
                qb = qd[pl.ds(start, ATTN_BLOCK), :]
                kc, kp = kd[pl.ds(start, ATTN_BLOCK), :], kd[pl.ds(pstart, ATTN_BLOCK), :]
                vc, vp = vd[pl.ds(start, ATTN_BLOCK), :], vd[pl.ds(pstart, ATTN_BLOCK), :]
                outs, lses = [], []
                for a in range(2):
                    qa = jnp.where(low if a == 0 else ~low, qb, jnp.zeros_like(qb))
                    sc = jnp.where(cur_ok, _nt(qa, kc), NEG_INF)
                    sp = jnp.where(prev_ok & has_prev, _nt(qa, kp), NEG_INF)
                    m = jnp.maximum(jnp.max(sc, axis=1, keepdims=True), jnp.max(sp, axis=1, keepdims=True))
                    pc, pp = jnp.exp(sc - m), jnp.exp(sp - m)
                    den = jnp.sum(pc, axis=1, keepdims=True) + jnp.sum(pp, axis=1, keepdims=True)
                    outs.append((_nn(pc.astype(bf16), vc) + _nn(pp.astype(bf16), vp)) / den)
                    lses.append(m + jnp.log(den))
                ob[pl.ds(start, ATTN_BLOCK), :] = jnp.where(low, outs[0], outs[1])
                lb[pl.ds(start, ATTN_BLOCK), :] = jnp.where(low, lses[0], lses[1])
                return carry

            lax.fori_loop(0, n_blk, block, 0)
            _interleave_store(ob, onat[bi], d, False)
            _interleave_store(lb, lnat[bi], d, False)
        la, lbb, lc = l0[...], l1[...], l2[...]
        lm = jnp.maximum(jnp.maximum(la, lbb), lc)
        wa, wb, wc = jnp.exp(la - lm), jnp.exp(lbb - lm), jnp.exp(lc - lm)
        ws = wa + wb + wc
        o_ref[...] = (wa * o0[...] + wb * o1[...] + wc * o2[...]) / ws
        lse_ref[...] = lm + jnp.log(ws)

    def col(jj):
        return pl.BlockSpec((SEQ, LANES), lambda b, j: (b, jj if jj is not None else j))

    tab = pl.BlockSpec((SEQ, LANES), lambda b, j: (b, 0))
    fs = pltpu.VMEM((SEQ, LANES), f32)
    hs = pltpu.VMEM((SEQ, LANES), bf16)
    return pl.pallas_call(
        body, name=name, grid=(nb, ATTN_WIDTH // LANES),
        in_specs=[col(None), col(ATTN_WIDTH // LANES), col(ATTN_WIDTH // LANES + 1), tab, tab, tab],
        out_specs=[col(None), col(None)],
        out_shape=[jax.ShapeDtypeStruct((t, ATTN_WIDTH), f32), jax.ShapeDtypeStruct((t, ATTN_WIDTH), f32)],
        scratch_shapes=[fs, fs, fs, hs, hs, hs, fs, fs, fs, fs, fs, fs, fs, fs],
        compiler_params=_cparams(("parallel", "parallel")),
    )(qkv, qkv, qkv, *tabs)


def _attn_bwd(qkv, tabs, o, lse, do, name):
    t = qkv.shape[0]
    nb = t // SEQ
    n_blk = SEQ // ATTN_BLOCK
    n_j = ATTN_WIDTH // LANES

    def body(q_ref, k_ref, v_ref, c_ref, s1_ref, s2_ref, o_ref, lse_ref, do_ref, dq_ref, dk_ref, dv_ref,
             qr, kr, vr, dl, qd, kd, vd, dod, lsd, dld, dqd, dkd, dvd, dqa, dka, dva):
        j = pl.program_id(1)
        kvh = j // 2
        cur_ok, prev_ok, low = _attn_masks()
        lowfull = lax.broadcasted_iota(jnp.int32, (SEQ, LANES), 1) < HEAD_DIM
        c, s1, s2 = c_ref[...], s1_ref[...], s2_ref[...]
        qr[...] = _rot(q_ref[...], c, s1, s2) * (HEAD_DIM ** -0.5)
        kr[...] = _dup_head(_rot(k_ref[...], c, s1, s2), kvh, lowfull)
        vr[...] = _dup_head(v_ref[...], kvh, lowfull)
        prod = do_ref[...] * o_ref[...]
        d_lo = jnp.sum(jnp.where(lowfull, prod, 0.0), axis=1, keepdims=True)
        d_hi = jnp.sum(jnp.where(lowfull, 0.0, prod), axis=1, keepdims=True)
        dl[...] = jnp.where(lowfull, d_lo, d_hi)
        dqa[...] = jnp.zeros_like(dqa)
        dka[...] = jnp.zeros_like(dka)
        dva[...] = jnp.zeros_like(dva)
        for d in DILATIONS:
            _deinterleave(qr, qd, d, bf16)
            _deinterleave(kr, kd, d, bf16)
            _deinterleave(vr, vd, d, bf16)
            _deinterleave(do_ref, dod, d, bf16)
            _deinterleave(lse_ref, lsd, d, f32)
            _deinterleave(dl, dld, d, f32)
            dkd[...] = jnp.zeros_like(dkd)
            dvd[...] = jnp.zeros_like(dvd)
            per_res = n_blk // d

            def block(n, carry):
                start = pl.multiple_of(n * ATTN_BLOCK, ATTN_BLOCK)
                has_prev = (n % per_res) != 0
                pstart = pl.multiple_of(jnp.maximum(n - 1, 0) * ATTN_BLOCK, ATTN_BLOCK)
                cur, prev = pl.ds(start, ATTN_BLOCK), pl.ds(pstart, ATTN_BLOCK)
                qb, dob = qd[cur, :], dod[cur, :]
                kc, kp, vc, vp = kd[cur, :], kd[prev, :], vd[cur, :], vd[prev, :]
                lsb, dlb = lsd[cur, :], dld[cur, :]
                dqs = []
                dkc = dkp = dvc = dvp = None
                for a in range(2):
                    sel = low if a == 0 else ~low
                    qa = jnp.where(sel, qb, jnp.zeros_like(qb))
                    doa = jnp.where(sel, dob, jnp.zeros_like(dob))
                    ls = lsb[:, a * HEAD_DIM:a * HEAD_DIM + 1]
                    de = dlb[:, a * HEAD_DIM:a * HEAD_DIM + 1]
                    pc = jnp.exp(jnp.where(cur_ok, _nt(qa, kc), NEG_INF) - ls)
                    pp = jnp.exp(jnp.where(prev_ok & has_prev, _nt(qa, kp), NEG_INF) - ls)
                    dsc = (pc * (_nt(doa, vc) - de)).astype(bf16)
                    dsp = (pp * (_nt(doa, vp) - de)).astype(bf16)
                    dqs.append(_nn(dsc, kc) + _nn(dsp, kp))
                    pcb, ppb = pc.astype(bf16), pp.astype(bf16)
                    t_kc, t_kp, t_vc, t_vp = _tn(dsc, qa), _tn(dsp, qa), _tn(pcb, doa), _tn(ppb, doa)
                    dkc, dkp = (t_kc, t_kp) if a == 0 else (dkc + t_kc, dkp + t_kp)
                    dvc, dvp = (t_vc, t_vp) if a == 0 else (dvc + t_vc, dvp + t_vp)
                dqd[cur, :] = jnp.where(low, dqs[0], dqs[1])
                dkd[cur, :] += dkc
                dvd[cur, :] += dvc
                dkd[prev, :] += dkp
                dvd[prev, :] += dvp
                return carry

            lax.fori_loop(0, n_blk, block, 0)
            _interleave_store(dqd, dqa, d, True)
            _interleave_store(dkd, dka, d, True)
            _interleave_store(dvd, dva, d, True)
        dq_ref[...] = _rot_t(dqa[...] * (HEAD_DIM ** -0.5), c, s1, s2)
        dkf = dka[...]
        dkf = _rot_t(dkf + pltpu.roll(dkf, HEAD_DIM, 1), c, s1, s2)
        dvf = dva[...]
        dvf = dvf + pltpu.roll(dvf, HEAD_DIM, 1)
        mine = (lax.broadcasted_iota(jnp.int32, (SEQ, LANES), 1) // HEAD_DIM) == kvh
        dkc_, dvc_ = jnp.where(mine, dkf, 0.0), jnp.where(mine, dvf, 0.0)

        @pl.when(j == 0)
        def _():
            dk_ref[...] = dkc_
            dv_ref[...] = dvc_

        @pl.when(j > 0)
        def _():
            dk_ref[...] += dkc_
            dv_ref[...] += dvc_

    def col(jj):
        return pl.BlockSpec((SEQ, LANES), lambda b, j: (b, jj if jj is not None else j))

    tab = pl.BlockSpec((SEQ, LANES), lambda b, j: (b, 0))
    fs = pltpu.VMEM((SEQ, LANES), f32)
    hs = pltpu.VMEM((SEQ, LANES), bf16)
    return pl.pallas_call(
        body, name=name, grid=(nb, n_j),
        in_specs=[col(None), col(n_j), col(n_j + 1), tab, tab, tab, col(None), col(None), col(None)],
        out_specs=[col(None), tab, tab],
        out_shape=[jax.ShapeDtypeStruct((t, ATTN_WIDTH), f32), jax.ShapeDtypeStruct((t, LANES), f32), jax.ShapeDtypeStruct((t, LANES), f32)],
        scratch_shapes=[fs, fs, fs, fs, hs, hs, hs, hs, fs, fs, fs, fs, fs, fs, fs, fs],
        compiler_params=_cparams(("parallel", "arbitrary")),
    )(qkv, qkv, qkv, *tabs, o, lse, do)


def _conv_pre(x, w_ref, b_ref, row):
    shifted = [x] + [jnp.where(row >= s, pltpu.roll(x, s, 0), 0.0) for s in range(1, CONV_WIDTH)]
    pre = b_ref[...] + w_ref[CONV_WIDTH - 1:CONV_WIDTH, :] * x
    for s in range(1, CONV_WIDTH):
        pre = pre + w_ref[CONV_WIDTH - 1 - s:CONV_WIDTH - s, :] * shifted[s]
    return pre, shifted


def _conv_fwd(x, w, b, name, tc=512):
    t, ch = x.shape

    def body(x_ref, w_ref, b_ref, o_ref):
        row = lax.broadcasted_iota(jnp.int32, (SEQ, tc), 0)
        pre, _ = _conv_pre(x_ref[...], w_ref, b_ref, row)
        o_ref[...] = _silu(pre)

    xs = pl.BlockSpec((SEQ, tc), lambda i, j: (i, j))
    return pl.pallas_call(
        body, name=name, grid=(t // SEQ, ch // tc),
        in_specs=[xs, pl.BlockSpec((CONV_WIDTH, tc), lambda i, j: (0, j)), pl.BlockSpec((1, tc), lambda i, j: (0, j))],
        out_specs=xs, out_shape=jax.ShapeDtypeStruct((t, ch), f32),
        compiler_params=_cparams(("parallel", "parallel")),
    )(x, w, b)


def _conv_bwd(x, w, b, dact, name, tc=512):
    t, ch = x.shape

    def body(x_ref, w_ref, b_ref, d_ref, dx_ref, dw_ref, db_ref):
        row = lax.broadcasted_iota(jnp.int32, (SEQ, tc), 0)
        pre, shifted = _conv_pre(x_ref[...], w_ref, b_ref, row)
        dpre = d_ref[...] * _dsilu(pre)
        dx = w_ref[CONV_WIDTH - 1:CONV_WIDTH, :] * dpre
        for s in range(1, CONV_WIDTH):
            dx = dx + w_ref[CONV_WIDTH - 1 - s:CONV_WIDTH - s, :] * jnp.where(row < SEQ - s, pltpu.roll(dpre, SEQ - s, 0), 0.0)
        dx_ref[...] = dx
        first = pl.program_id(1) == 0
        parts = [jnp.sum(dpre * shifted[CONV_WIDTH - 1 - k], axis=0, keepdims=True) for k in range(CONV_WIDTH)]
        dbp = jnp.sum(dpre, axis=0, keepdims=True)

        @pl.when(first)
        def _():
            for k in range(CONV_WIDTH):
                dw_ref[k:k + 1, :] = parts[k]
            db_ref[...] = dbp

        @pl.when(jnp.logical_not(first))
        def _():
            for k in range(CONV_WIDTH):
                dw_ref[k:k + 1, :] += parts[k]
            db_ref[...] += dbp

    xs = pl.BlockSpec((SEQ, tc), lambda j, i: (i, j))
    ws = pl.BlockSpec((CONV_WIDTH, tc), lambda j, i: (0, j))
    bs = pl.BlockSpec((1, tc), lambda j, i: (0, j))
    return pl.pallas_call(
        body, name=name, grid=(ch // tc, t // SEQ),
        in_specs=[xs, ws, bs, xs], out_specs=[xs, ws, bs],
        out_shape=[jax.ShapeDtypeStruct((t, ch), f32), jax.ShapeDtypeStruct((CONV_WIDTH, ch), f32), jax.ShapeDtypeStruct((1, ch), f32)],
        compiler_params=_cparams(("parallel", "arbitrary")),
    )(x, w, b, dact)


GROUP_W = SSM_INNER // SSM_GROUPS
HEADS_PER_GROUP = SSM_HEADS // SSM_GROUPS
HI = lax.Precision.HIGHEST


def _ssd_common(xbc_ref, dt_ref, bias_ref, alog_ref):
    r = lax.broadcasted_iota(jnp.int32, (CHUNK, CHUNK), 0)
    cidx = lax.broadcasted_iota(jnp.int32, (CHUNK, CHUNK), 1)
    causal = r >= cidx
    tril = causal.astype(f32)
    expand = (lax.broadcasted_iota(jnp.int32, (CHUNK, SSM_INNER), 0)
              == lax.broadcasted_iota(jnp.int32, (CHUNK, SSM_INNER), 1) // HEAD_DIM).astype(f32)
    head_lane = cidx < SSM_HEADS
    dtp = dt_ref[...] + bias_ref[...]
    dt = jnp.where(head_lane, _softplus(dtp), 0.0)
    a_neg = -jnp.exp(alog_ref[...])
    a = dt * a_neg
    cs = _nn(tril, a, HI)
    dt_e = _nn(dt, expand, HI)
    cs_e = _nn(cs, expand, HI)
    xs = xbc_ref[:, 0:SSM_INNER]
    xg = xs * dt_e
    ecs = jnp.exp(cs_e)
    cs_last = cs_e[CHUNK - 1:CHUNK, :]
    dse = jnp.exp(cs_last - cs_e)
    cde = jnp.exp(cs_last)
    return dict(r=r, cidx=cidx, causal=causal, tril=tril, expand=expand, head_lane=head_lane, dtp=dtp, dt=dt, a_neg=a_neg,
                cs=cs, cst=cs.T, dt_e=dt_e, cs_e=cs_e, xs=xs, xg=xg, ecs=ecs, dse=dse, cde=cde)


def _decay_mat(q, h):
    return jnp.exp(jnp.where(q["causal"], q["cs"][:, h:h + 1] - q["cst"][h:h + 1, :], NEG_INF))


def _gate_norm(y, z, nw):
    y2 = y * _silu(z)
    outs, xhats, rs = [], [], []
    for g in range(SSM_GROUPS):
        sl = slice(g * GROUP_W, (g + 1) * GROUP_W)
        yg = y2[:, sl]
        r = lax.rsqrt(jnp.mean(yg * yg, axis=-1, keepdims=True) + EPS)
        xhats.append(yg * r)
        rs.append(r)
        outs.append(yg * r * nw[:, sl])
    return y2, outs, xhats, rs


def _ssd_fwd(xbc, z, dtp, params, name):
    t = xbc.shape[0]
    n_chunk = SEQ // CHUNK
    low = None

    def body(xbc_ref, z_ref, dt_ref, bias_ref, alog_ref, dskip_ref, nw_ref, yn_ref, y_ref, hs_ref, h_scr):
        @pl.when(pl.program_id(1) == 0)
        def _():
            h_scr[...] = jnp.zeros_like(h_scr)

        q = _ssd_common(xbc_ref, dt_ref, bias_ref, alog_ref)
        low = lax.broadcasted_iota(jnp.int32, (CHUNK, LANES), 1) < HEAD_DIM
        xgb = q["xg"].astype(bf16)
        wst = (q["xg"] * q["dse"]).astype(bf16)
        hs_ref[0] = h_scr[...]
        ys = []
        for g in range(SSM_GROUPS):
            gl = slice(g * GROUP_W, (g + 1) * GROUP_W)
            bg = xbc_ref[:, SSM_INNER + g * D_STATE:SSM_INNER + (g + 1) * D_STATE].astype(bf16)
            cg = xbc_ref[:, SSM_INNER + SSM_GROUPS * D_STATE + g * D_STATE:SSM_INNER + SSM_GROUPS * D_STATE + (g + 1) * D_STATE].astype(bf16)
            cb = _nt(cg, bg)
            hg = h_scr[g]
            yoff = _nn(cg, hg.astype(bf16)) * q["ecs"][:, gl]
            pieces = []
            for i in range(HEADS_PER_GROUP // 2):
                h0 = g * HEADS_PER_GROUP + 2 * i
                xp = xgb[:, h0 * HEAD_DIM:(h0 + 2) * HEAD_DIM]
                m0 = (cb * _decay_mat(q, h0)).astype(bf16)
                m1 = (cb * _decay_mat(q, h0 + 1)).astype(bf16)
                zero = jnp.zeros_like(xp)
                pieces.append(_nn(m0, jnp.where(low, xp, zero)) + _nn(m1, jnp.where(low, zero, xp)))
            ys.append(jnp.concatenate(pieces, axis=1) + yoff + dskip_ref[:, gl] * q["xs"][:, gl])
            h_scr[g] = hg * q["cde"][:, gl] + _tn(bg, wst[:, gl])
        y = jnp.concatenate(ys, axis=1)
        y_ref[...] = y
        _, outs, _, _ = _gate_norm(y, z_ref[...], nw_ref[...])
        yn_ref[...] = jnp.concatenate(outs, axis=1).astype(bf16)

    def rows(w):
        return pl.BlockSpec((CHUNK, w), lambda b, c: (b * n_chunk + c, 0))

    def par(w):
        return pl.BlockSpec((1, w), lambda b, c: (0, 0))

    return pl.pallas_call(
        body, name=name, grid=(t // SEQ, n_chunk),
        in_specs=[rows(CONV_CH), rows(SSM_INNER), rows(LANES), par(LANES), par(LANES), par(SSM_INNER), par(SSM_INNER)],
        out_specs=[rows(SSM_INNER), rows(SSM_INNER), pl.BlockSpec((1, SSM_GROUPS, D_STATE, GROUP_W), lambda b, c: (b * n_chunk + c, 0, 0, 0))],
        out_shape=[jax.ShapeDtypeStruct((t, SSM_INNER), bf16), jax.ShapeDtypeStruct((t, SSM_INNER), f32),
                   jax.ShapeDtypeStruct((t // CHUNK, SSM_GROUPS, D_STATE, GROUP_W), f32)],
        scratch_shapes=[pltpu.VMEM((SSM_GROUPS, D_STATE, GROUP_W), f32)],
        compiler_params=_cparams(("parallel", "arbitrary")),
    )(xbc, z, dtp, *params)


def _ssd_bwd(xbc, z, dtp, y, hs, dyn, params, name):
    t = xbc.shape[0]
    n_chunk = SEQ // CHUNK

    def body(xbc_ref, z_ref, dt_ref, y_ref, hs_ref, dyn_ref, bias_ref, alog_ref, dskip_ref, nw_ref,
             dxbc_ref, dz_ref, ddt_ref, dnw_ref, dds_ref, dal_ref, dbi_ref, dh_scr):
        @pl.when(pl.program_id(1) == 0)
        def _():
            dh_scr[...] = jnp.zeros_like(dh_scr)

        q = _ssd_common(xbc_ref, dt_ref, bias_ref, alog_ref)
        low = lax.broadcasted_iota(jnp.int32, (CHUNK, LANES), 1) < HEAD_DIM
        last_row = lax.broadcasted_iota(jnp.int32, (CHUNK, GROUP_W), 0) == CHUNK - 1
        xs, xg = q["xs"], q["xg"]
        xgb = xg.astype(bf16)
        wf = xg * q["dse"]
        wst = wf.astype(bf16)
        zz = z_ref[...]
        yy = y_ref[...]
        sz = _silu(zz)
        y2, _, xhats, rs = _gate_norm(yy, zz, nw_ref[...])
        dyn_ = dyn_ref[...]
        dy2s, dnws = [], []
        for g in range(SSM_GROUPS):
            gl = slice(g * GROUP_W, (g + 1) * GROUP_W)
            gw = dyn_[:, gl] * nw_ref[:, gl]
            dy2s.append(rs[g] * (gw - xhats[g] * jnp.mean(gw * xhats[g], axis=-1, keepdims=True)))
            dnws.append(_rowsum8(dyn_[:, gl] * xhats[g]))
        dy2 = jnp.concatenate(dy2s, axis=1)
        dy = dy2 * sz
        dz_ref[...] = dy2 * yy * _dsilu(zz)
        dnw_p = jnp.concatenate(dnws, axis=1)
        dds_p = _rowsum8(dy * xs)
        dyb = dy.astype(bf16)
        gfull = (dy * q["ecs"]).astype(bf16)
        dcs_c = jnp.zeros((CHUNK, CHUNK), f32)
        dcs_r = jnp.zeros((CHUNK, CHUNK), f32)
        dcs_e_parts, dxg_parts = [], []
        for g in range(SSM_GROUPS):
            gl = slice(g * GROUP_W, (g + 1) * GROUP_W)
            bsl = slice(SSM_INNER + g * D_STATE, SSM_INNER + (g + 1) * D_STATE)
            csl = slice(SSM_INNER + SSM_GROUPS * D_STATE + g * D_STATE, SSM_INNER + SSM_GROUPS * D_STATE + (g + 1) * D_STATE)
            bg = xbc_ref[:, bsl].astype(bf16)
            cg = xbc_ref[:, csl].astype(bf16)
            cb = _nt(cg, bg)
            hg = hs_ref[0, g]
            hgb = hg.astype(bf16)
            dhn = dh_scr[g]
            dhnb = dhn.astype(bf16)
            yoff = _nn(cg, hgb) * q["ecs"][:, gl]
            dw_ = _nn(bg, dhnb)
            r_e = dw_ * wf[:, gl]
            to_last = jnp.sum(r_e, axis=0, keepdims=True) + jnp.sum(dhn * hg, axis=0, keepdims=True) * q["cde"][:, gl]
            dcs_e_parts.append(dy[:, gl] * yoff - r_e + jnp.where(last_row, to_last, 0.0))
            dcb = jnp.zeros((CHUNK, CHUNK), f32)
            dxg_pairs = []
            for i in range(HEADS_PER_GROUP // 2):
                h0 = g * HEADS_PER_GROUP + 2 * i
                psl = slice(h0 * HEAD_DIM, (h0 + 2) * HEAD_DIM)
                xp = xgb[:, psl]
                dyp = dyb[:, psl]
                zero = jnp.zeros_like(dyp)
                tns = []
                for a in range(2):
                    h = h0 + a
                    lm = _decay_mat(q, h)
                    m = cb * lm
                    dm = _nt(jnp.where(low, dyp, zero) if a == 0 else jnp.where(low, zero, dyp), xp)
                    dcb = dcb + dm * lm
                    nmat = dm * m
                    dcs_c = dcs_c + jnp.where(q["cidx"] == h, jnp.sum(nmat, axis=1, keepdims=True), 0.0)
                    dcs_r = dcs_r + jnp.where(q["r"] == h, jnp.sum(nmat, axis=0, keepdims=True), 0.0)
                    tns.append(_tn(m.astype(bf16), dyp))
                dxg_pairs.append(jnp.where(low, tns[0], tns[1]))
            dxg_parts.append(jnp.concatenate(dxg_pairs, axis=1) + dw_ * q["dse"][:, gl])
            dcbb = dcb.astype(bf16)
            dxbc_ref[:, csl] = _nt(gfull[:, gl], hgb) + _nn(dcbb, bg)
            dxbc_ref[:, bsl] = _nt(wst[:, gl], dhnb) + _tn(dcbb, cg)
            dh_scr[g] = dhn * q["cde"][:, gl] + _tn(cg, gfull[:, gl])
        dxg = jnp.concatenate(dxg_parts, axis=1)
        dcs_e = jnp.concatenate(dcs_e_parts, axis=1)
        dxbc_ref[:, 0:SSM_INNER] = dskip_ref[...] * dy + dxg * q["dt_e"]
        dcs = dcs_c - dcs_r.T + _dot(dcs_e, q["expand"], ((1,), (1,)), HI)
        triu = (q["cidx"] >= q["r"]).astype(f32)
        da = _nn(triu, dcs, HI)
        ddt = _dot(dxg * xs, q["expand"], ((1,), (1,)), HI) + da * q["a_neg"]
        ddtp = jnp.where(q["head_lane"], ddt * _sigmoid(q["dtp"]), 0.0)
        ddt_ref[...] = ddtp
        dal_p = _rowsum8(da * q["dt"]) * q["a_neg"]
        dbi_p = _rowsum8(ddtp)
        first = (pl.program_id(0) == 0) & (pl.program_id(1) == 0)

        @pl.when(first)
        def _():
            dnw_ref[...] = dnw_p
            dds_ref[...] = dds_p
            dal_ref[...] = dal_p
            dbi_ref[...] = dbi_p

        @pl.when(jnp.logical_not(first))
        def _():
            dnw_ref[...] += dnw_p
            dds_ref[...] += dds_p
            dal_ref[...] += dal_p
            dbi_ref[...] += dbi_p

    def rows(w):
        return pl.BlockSpec((CHUNK, w), lambda b, c: (b * n_chunk + n_chunk - 1 - c, 0))

    def par(w):
        return pl.BlockSpec((1, w), lambda b, c: (0, 0))

    def acc(w):
        return pl.BlockSpec((SUBLANES, w), lambda b, c: (0, 0))

    return pl.pallas_call(
        body, name=name, grid=(t // SEQ, n_chunk),
        in_specs=[rows(CONV_CH), rows(SSM_INNER), rows(LANES), rows(SSM_INNER),
                  pl.BlockSpec((1, SSM_GROUPS, D_STATE, GROUP_W), lambda b, c: (b * n_chunk + n_chunk - 1 - c, 0, 0, 0)),
                  rows(SSM_INNER), par(LANES), par(LANES), par(SSM_INNER), par(SSM_INNER)],
        out_specs=[rows(CONV_CH), rows(SSM_INNER), rows(LANES), acc(SSM_INNER), acc(SSM_INNER), acc(LANES), acc(LANES)],
        out_shape=[jax.ShapeDtypeStruct((t, CONV_CH), f32), jax.ShapeDtypeStruct((t, SSM_INNER), f32), jax.ShapeDtypeStruct((t, LANES), f32),
                   jax.ShapeDtypeStruct((SUBLANES, SSM_INNER), f32), jax.ShapeDtypeStruct((SUBLANES, SSM_INNER), f32),
                   jax.ShapeDtypeStruct((SUBLANES, LANES), f32), jax.ShapeDtypeStruct((SUBLANES, LANES), f32)],
        scratch_shapes=[pltpu.VMEM((SSM_GROUPS, D_STATE, GROUP_W), f32)],
        compiler_params=_cparams(("arbitrary", "arbitrary")),
    )(xbc, z, dtp, y, hs, dyn, *params)


def _adamw(g_parts, w, m, v, name, tr=None):
    rows, width = w.shape
    n = len(g_parts)
    if tr is None:
        tr = rows
        for cand in range(min(rows, 512), 7, -8):
            if rows % cand == 0:
                tr = cand
                break

    def body(*refs):
        g_refs, (w_ref, m_ref, v_ref, g_out, d_out, m_out, v_out) = refs[:n], refs[n:]

        def part(i):
            return g_refs[i][...] if g_parts[i][1] is None else g_refs[i][0]

        g = part(0)
        for i in range(1, n):
            g = g + part(i)
        mm = ADAM_B1 * m_ref[...] + (1.0 - ADAM_B1) * g
        vv = ADAM_B2 * v_ref[...] + (1.0 - ADAM_B2) * (g * g)
        m_hat = mm / (1.0 - ADAM_B1 ** ADAM_STEP)
        v_hat = vv / (1.0 - ADAM_B2 ** ADAM_STEP)
        g_out[...] = g
        d_out[...] = -ADAM_LR * (m_hat / (jnp.sqrt(v_hat) + ADAM_EPS) + ADAM_WD * w_ref[...])
        m_out[...] = mm
        v_out[...] = vv

    spec = pl.BlockSpec((tr, width), lambda i: (i, 0))

    def gspec(idx):
        return spec if idx is None else pl.BlockSpec((1, tr, width), lambda i: (idx, i, 0))

    return pl.pallas_call(
        body, name=name, grid=(rows // tr,), in_specs=[gspec(idx) for _, idx in g_parts] + [spec] * 3, out_specs=[spec] * 4,
        out_shape=[jax.ShapeDtypeStruct((rows, width), f32)] * 4, compiler_params=_cparams(("parallel",)),
    )(*[a for a, _ in g_parts], w, m, v)


def _add_kept(g, recv, core, name, tr=304):
    nblk, _, rows, width = g.shape

    def body(c_ref, g_ref, r_ref, o_ref):
        o_ref[0] = g_ref[0, 0] + r_ref[0]

    grid_spec = pltpu.PrefetchScalarGridSpec(
        num_scalar_prefetch=1, grid=(nblk, rows // tr),
        in_specs=[pl.BlockSpec((1, 1, tr, width), lambda i, j, c: (i, c[0], j, 0)), pl.BlockSpec((1, tr, width), lambda i, j, c: (i, j, 0))],
        out_specs=pl.BlockSpec((1, tr, width), lambda i, j, c: (i, j, 0)))
    return pl.pallas_call(
        body, name=name, grid_spec=grid_spec, out_shape=jax.ShapeDtypeStruct((nblk, rows, width), f32),
        compiler_params=_cparams(("parallel", "parallel")),
    )(core, g, recv)


def _me():
    return lax.axis_index("x"), lax.axis_index("y"), lax.axis_index("c")


def _allgather_two_level(shards, name):
    n = len(shards)
    per = 7

    def body(*refs):
        ins, outs = refs[:n], refs[n:2 * n]
        send_sems, recv_sems, local_sems = refs[2 * n:]
        x, y, c = _me()
        me, sibling = (x, y, c), (x, y, 1 - c)
        chips = [(1 - x, y), (x, 1 - y), (1 - x, 1 - y)]

        def slot(a, p):
            return outs[a].at[4 * p[0] + 2 * p[1] + p[2]]

        def copy(a, k, block, to, src=None):
            return pltpu.make_async_remote_copy(
                src_ref=slot(a, block) if src is None else src, dst_ref=slot(a, block),
                send_sem=send_sems.at[a * per + k], recv_sem=recv_sems.at[a * per + k], device_id=to, device_id_type=MESH)

        mine = [pltpu.make_async_copy(ins[a], slot(a, me), local_sems.at[a]) for a in range(n)]
        for cp in mine:
            cp.start()
        first = []
        for a in range(n):
            first.append(copy(a, 0, me, sibling, src=ins[a]))
            first += [copy(a, 1 + j, me, (*chip, c), src=ins[a]) for j, chip in enumerate(chips)]
        for cp in first:
            cp.start()
        passed = []
        for j, chip in enumerate(chips):
            for a in range(n):
                copy(a, 1 + j, (*chip, c), me).wait_recv()
                fwd = copy(a, 4 + j, (*chip, c), sibling)
                fwd.start()
                passed.append(fwd)
        for a in range(n):
            copy(a, 0, sibling, me).wait_recv()
            for j, chip in enumerate(chips):
                copy(a, 4 + j, (*chip, 1 - c), me).wait_recv()
        for cp in first + passed:
            cp.wait_send()
        for cp in mine:
            cp.wait()

    return pl.pallas_call(
        body, name=name, in_specs=[ANY] * n, out_specs=[ANY] * n,
        out_shape=[jax.ShapeDtypeStruct((N_DEV,) + s.shape, s.dtype) for s in shards],
        scratch_shapes=[pltpu.SemaphoreType.DMA((n * per,)), pltpu.SemaphoreType.DMA((n * per,)), pltpu.SemaphoreType.DMA((n,))],
    )(*shards)


def _allgather_direct(row, name):
    def body(in_ref, out_ref, send_sems, recv_sems, local_sem):
        x, y, c = _me()
        mine = out_ref.at[4 * x + 2 * y + c]
        local = pltpu.make_async_copy(in_ref, mine, local_sem)
        local.start()
        sends = []
        for k in range(1, N_DEV):
            px, py, pc = x ^ (k >> 2), y ^ ((k >> 1) & 1), c ^ (k & 1)
            sends.append(pltpu.make_async_remote_copy(
                src_ref=in_ref, dst_ref=mine, send_sem=send_sems.at[k - 1], recv_sem=recv_sems.at[k - 1],
                device_id=(px, py, pc), device_id_type=MESH))
        for cp in sends:
            cp.start()
        for k in range(1, N_DEV):
            px, py, pc = x ^ (k >> 2), y ^ ((k >> 1) & 1), c ^ (k & 1)
            theirs = out_ref.at[4 * px + 2 * py + pc]
            pltpu.make_async_remote_copy(
                src_ref=in_ref, dst_ref=theirs, send_sem=send_sems.at[k - 1], recv_sem=recv_sems.at[k - 1],
                device_id=(px, py, pc), device_id_type=MESH).wait_recv()
        for cp in sends:
            cp.wait_send()
        local.wait()

    return pl.pallas_call(
        body, name=name, in_specs=[ANY], out_specs=ANY, out_shape=jax.ShapeDtypeStruct((N_DEV,) + row.shape, row.dtype),
        scratch_shapes=[pltpu.SemaphoreType.DMA((N_DEV - 1,)), pltpu.SemaphoreType.DMA((N_DEV - 1,)), pltpu.SemaphoreType.DMA],
    )(row)


def _exchange_sibling(g, name):
    n_chip = g.shape[0]

    def body(g_ref, out_ref, send_sems, recv_sems):
        x, y, c = _me()
        copies = [pltpu.make_async_remote_copy(
            src_ref=g_ref.at[i, 1 - c], dst_ref=out_ref.at[i], send_sem=send_sems.at[i], recv_sem=recv_sems.at[i],
            device_id=(x, y, 1 - c), device_id_type=MESH) for i in range(n_chip)]
        for cp in copies:
            cp.start()
        for cp in copies:
            cp.wait_recv()
        for cp in copies:
            cp.wait_send()

    return pl.pallas_call(
        body, name=name, in_specs=[ANY], out_specs=ANY, out_shape=jax.ShapeDtypeStruct((n_chip,) + g.shape[2:], g.dtype),
        scratch_shapes=[pltpu.SemaphoreType.DMA((n_chip,)), pltpu.SemaphoreType.DMA((n_chip,))],
    )(g)


def _exchange_chips(p, name):
    def body(p_ref, out_ref, send_sems, recv_sems):
        x, y, c = _me()
        chips = [(1 - x, y), (x, 1 - y), (1 - x, 1 - y)]
        copies = [pltpu.make_async_remote_copy(
            src_ref=p_ref.at[2 * cx + cy], dst_ref=out_ref.at[k], send_sem=send_sems.at[k], recv_sem=recv_sems.at[k],
            device_id=(cx, cy, c), device_id_type=MESH) for k, (cx, cy) in enumerate(chips)]
        for cp in copies:
            cp.start()
        for cp in copies:
            cp.wait_recv()
        for cp in copies:
            cp.wait_send()

    return pl.pallas_call(
        body, name=name, in_specs=[ANY], out_specs=ANY, out_shape=jax.ShapeDtypeStruct((3,) + p.shape[1:], p.dtype),
        scratch_shapes=[pltpu.SemaphoreType.DMA((3,)), pltpu.SemaphoreType.DMA((3,))],
    )(p)


def _row(v, width=None):
    v = v.reshape(1, -1).astype(f32)
    if width is not None and v.shape[1] < width:
        v = jnp.pad(v, ((0, 0), (0, width - v.shape[1])))
    return v


def _layer_params(p, l):
    return dict(
        norm_mix=_row(p["norm_mix"][l]), norm_ffn=_row(p["norm_ffn"][l]), conv_w=p["conv_w"][l], conv_b=_row(p["conv_b"][l]),
        ssd=(_row(p["dt_bias"][l], LANES), _row(p["a_log"][l], LANES), _row(jnp.repeat(p["d_skip"][l], HEAD_DIM)), _row(p["ssm_norm"][l])))


def _layer_fwd(h, big, sp, tabs, l):
    tag = f"l{l}_"
    w_in, w_out, w_gate, w_up, w_down = big
    hn = _rmsnorm_fwd(h, sp["norm_mix"], tag + "norm_mix")
    qkv = _matmul(hn, w_in, mode="nn", n_out=QKV_WIDTH, tn=256, b_off=0, name=tag + "proj_qkv")
    z = _matmul(hn, w_in, mode="nn", n_out=SSM_INNER, tn=256, b_off=Z_OFF // 256, name=tag + "proj_z")
    xbc_pre = _matmul(hn, w_in, mode="nn", n_out=CONV_CH, tn=256, b_off=XBC_OFF // 256, name=tag + "proj_xbc")
    dtp = _matmul(hn, w_in, mode="nn", n_out=LANES, tn=LANES, b_off=DT_OFF // LANES, name=tag + "proj_dt")
    o, lse = _attn_fwd(qkv, tabs, tag + "attn_fwd")
    xbc = _conv_fwd(xbc_pre, sp["conv_w"], sp["conv_b"], tag + "conv_fwd")
    yn, y, hs = _ssd_fwd(xbc, z, dtp, sp["ssd"], tag + "ssd_fwd")
    t1 = _matmul(o, w_out, mode="nn", k_len=ATTN_WIDTH, tk=512, add=h, name=tag + "out_attn")
    h2 = _matmul(yn, w_out, mode="nn", k_len=SSM_INNER, tk=512, b_koff=1, add=t1, name=tag + "out_ssm")
    hn2 = _rmsnorm_fwd(h2, sp["norm_ffn"], tag + "norm_ffn")
    g, u, act = _swiglu_fwd(hn2, w_gate, w_up, tag + "ffn_up")
    h3 = _matmul(act, w_down, mode="nn", tk=1408, add=h2, name=tag + "ffn_down")
    saved = dict(h=h, hn=hn, qkv=qkv, z=z, xbc_pre=xbc_pre, dtp=dtp, o=o, lse=lse, xbc=xbc, yn=yn, y=y, hs=hs, h2=h2, hn2=hn2, g=g, u=u, act=act)
    return h3, saved


def _layer_bwd(dh3, s, big, sp, tabs, l):
    tag = f"l{l}_"
    w_in, w_out, w_gate, w_up, w_down = big
    dg, du = _swiglu_bwd(dh3, w_down, s["g"], s["u"], tag + "ffn_down_bwd")
    dw_down = _matmul(s["act"], dh3, mode="tn", tm=1408, tn=512, name=tag + "dw_down")
    dhn2 = _matmul(dg, w_gate, mode="nt", tk=1408, name=tag + "ffn_gate_bwd")
    dhn2 = _matmul(du, w_up, mode="nt", tk=1408, add=dhn2, name=tag + "ffn_up_bwd")
    dw_gate = _matmul(s["hn2"], dg, mode="tn", tm=512, tn=1408, name=tag + "dw_gate")
    dw_up = _matmul(s["hn2"], du, mode="tn", tm=512, tn=1408, name=tag + "dw_up")
    dh2, dnf = _rmsnorm_bwd(dhn2, s["h2"], sp["norm_ffn"], dh3, tag + "norm_ffn_bwd")
    d_o = _matmul(dh2, w_out, mode="nt", n_out=ATTN_WIDTH, tn=512, b_off=0, name=tag + "out_attn_bwd")
    dyn = _matmul(dh2, w_out, mode="nt", n_out=SSM_INNER, tn=512, b_off=1, name=tag + "out_ssm_bwd")
    dw_out = jnp.concatenate([_matmul(s["o"], dh2, mode="tn", tm=512, tn=512, name=tag + "dw_out_attn"),
                              _matmul(s["yn"], dh2, mode="tn", tm=512, tn=512, name=tag + "dw_out_ssm")], axis=0)
    dxbc, dz, ddtp, dnw, dds, dal, dbi = _ssd_bwd(s["xbc"], s["z"], s["dtp"], s["y"], s["hs"], dyn, sp["ssd"], tag + "ssd_bwd")
    dxbc_pre, dconv_w, dconv_b = _conv_bwd(s["xbc_pre"], sp["conv_w"], sp["conv_b"], dxbc, tag + "conv_bwd")
    dq, dk, dv = _attn_bwd(s["qkv"], tabs, s["o"], s["lse"], d_o, tag + "attn_bwd")
    dproj = jnp.concatenate([dq.astype(bf16), dk.astype(bf16), dv.astype(bf16), dz.astype(bf16), dxbc_pre.astype(bf16), ddtp.astype(bf16)], axis=1)
    dhn = _matmul(dproj, w_in, mode="nt", tk=1152, name=tag + "proj_bwd")
    dw_in = _matmul(s["hn"], dproj, mode="tn", tm=512, tn=1152, name=tag + "dw_in")[:, :IN_PROJ]
    dh, dnm = _rmsnorm_bwd(dhn, s["h"], sp["norm_mix"], dh2, tag + "norm_mix_bwd")
    grads = dict(
        norm_mix=dnm.sum(0), w_in=dw_in, conv_w=dconv_w, conv_b=dconv_b[0], dt_bias=dbi.sum(0)[:SSM_HEADS], a_log=dal.sum(0)[:SSM_HEADS],
        d_skip=dds.sum(0).reshape(SSM_HEADS, HEAD_DIM).sum(1), ssm_norm=dnw.sum(0), w_out=dw_out, norm_ffn=dnf.sum(0),
        w_gate=dw_gate, w_up=dw_up, w_down=dw_down)
    return dh, grads


def _local_step(x, positions, target, p, bigs):
    tabs = _rope_tables(positions.reshape(-1, 1), "rope_tables")
    h = x
    saved, sps = [], []
    for l in range(DEPTH):
        sps.append(_layer_params(p, l))
        h, s = _layer_fwd(h, bigs[l], sps[l], tabs, l)
        saved.append(s)
    dh, loss_parts, dfn = _final_loss(h, _row(p["final_norm"]), target, "final_loss")
    layer_grads = [None] * DEPTH
    for l in reversed(range(DEPTH)):
        dh, layer_grads[l] = _layer_bwd(dh, saved[l], bigs[l], sps[l], tabs, l)
    grads = {k: jnp.stack([layer_grads[l][k] for l in range(DEPTH)]) for k in layer_grads[0]}
    grads["final_norm"] = dfn.sum(0)
    return jnp.sum(loss_parts), dh, grads


BIG = ("w_in", "w_out", "w_gate", "w_up", "w_down")
COL_SHARDED = ("w_in", "w_gate", "w_up")
SMALL = ("norm_mix", "conv_b", "dt_bias", "a_log", "d_skip", "ssm_norm", "norm_ffn", "final_norm")
WEIGHTS = ("norm_mix", "w_in", "conv_w", "conv_b", "dt_bias", "a_log", "d_skip", "ssm_norm", "w_out", "norm_ffn", "w_gate", "w_up", "w_down", "final_norm")
PACK_W = 1024
SMALL_ROWS = 88
CONVW_ROWS = 96
CONVW_SHARD_ROWS = 16


def _full_from_gathered(name, g):
    _, depth, a, b = g.shape
    if name in COL_SHARDED:
        return jnp.transpose(g, (1, 2, 0, 3)).reshape(depth, a, N_DEV * b)
    return jnp.transpose(g, (1, 0, 2, 3)).reshape(depth, N_DEV * a, b)


def _by_device(name, full):
    depth, a, b = full.shape
    if name in COL_SHARDED:
        t = jnp.transpose(full.reshape(depth, a, N_DEV, b // N_DEV), (2, 0, 1, 3))
    else:
        t = jnp.transpose(full.reshape(depth, N_DEV, a // N_DEV, b), (1, 0, 2, 3))
    return t.reshape(N_DEV, -1)


def _pack_rows(parts, rows, width):
    flat = jnp.concatenate([q.reshape(-1) for q in parts])
    return jnp.pad(flat, (0, rows * width - flat.shape[0])).reshape(rows, width)


def _unpack(flat, like):
    out, off = [], 0
    for q in like:
        out.append(flat[off:off + q.size].reshape(q.shape))
        off += q.size
    return out


def kernel(x, positions, norm_mix, w_in, conv_w, conv_b, dt_bias, a_log, d_skip, ssm_norm, w_out, norm_ffn, w_gate, w_up, w_down, final_norm, loss_target, m_norm_mix, m_w_in, m_conv_w, m_conv_b, m_dt_bias, m_a_log, m_d_skip, m_ssm_norm, m_w_out, m_norm_ffn, m_w_gate, m_w_up, m_w_down, m_final_norm, v_norm_mix, v_w_in, v_conv_w, v_conv_b, v_dt_bias, v_a_log, v_d_skip, v_ssm_norm, v_w_out, v_norm_ffn, v_w_gate, v_w_up, v_w_down, v_final_norm):
    w = dict(norm_mix=norm_mix, w_in=w_in, conv_w=conv_w, conv_b=conv_b, dt_bias=dt_bias, a_log=a_log, d_skip=d_skip, ssm_norm=ssm_norm,
             w_out=w_out, norm_ffn=norm_ffn, w_gate=w_gate, w_up=w_up, w_down=w_down, final_norm=final_norm)
    m = dict(norm_mix=m_norm_mix, w_in=m_w_in, conv_w=m_conv_w, conv_b=m_conv_b, dt_bias=m_dt_bias, a_log=m_a_log, d_skip=m_d_skip,
             ssm_norm=m_ssm_norm, w_out=m_w_out, norm_ffn=m_norm_ffn, w_gate=m_w_gate, w_up=m_w_up, w_down=m_w_down, final_norm=m_final_norm)
    v = dict(norm_mix=v_norm_mix, w_in=v_w_in, conv_w=v_conv_w, conv_b=v_conv_b, dt_bias=v_dt_bias, a_log=v_a_log, d_skip=v_d_skip,
             ssm_norm=v_ssm_norm, w_out=v_w_out, norm_ffn=v_norm_ffn, w_gate=v_w_gate, w_up=v_w_up, w_down=v_w_down, final_norm=v_final_norm)
    ax, ay, ac = lax.axis_index("x"), lax.axis_index("y"), lax.axis_index("c")
    dev = 4 * ax + 2 * ay + ac

    gathered = _allgather_two_level([w[k].astype(bf16) for k in BIG] + [w["conv_w"]], "gather_weights")
    full = {k: _full_from_gathered(k, g) for k, g in zip(BIG, gathered[:len(BIG)])}
    full["w_in"] = jnp.pad(full["w_in"], ((0, 0), (0, 0), (0, IN_PROJ_PAD - IN_PROJ)))
    p = {k: w[k] for k in SMALL}
    p["conv_w"] = jnp.transpose(gathered[-1], (1, 2, 0, 3)).reshape(DEPTH, CONV_WIDTH, CONV_CH)
    bigs = [tuple(full[k][l] for k in BIG) for l in range(DEPTH)]

    t = x.shape[0] * x.shape[1]
    loss_local, dx, grads = _local_step(x.reshape(t, D_MODEL), positions.reshape(t), loss_target.reshape(t, D_MODEL), p, bigs)

    n_big = sum(w[k].size for k in BIG)
    pack_rows = -(-n_big // (PACK_W * 16)) * 16
    by_dev = jnp.concatenate([_by_device(k, grads[k]) for k in BIG], axis=1)
    by_dev = jnp.pad(by_dev, ((0, 0), (0, pack_rows * PACK_W - n_big))).reshape(N_DEV // 2, 2, pack_rows, PACK_W)
    from_sibling = _exchange_sibling(by_dev, "scatter_sibling")
    chip_sums = _add_kept(by_dev, from_sibling, ac.reshape(1).astype(jnp.int32), "scatter_add")
    from_chips = _exchange_chips(chip_sums, "scatter_chips")
    own = lax.dynamic_index_in_dim(chip_sums, 2 * ax + ay, 0, keepdims=False)
    big_like = [w[k] for k in BIG]
    g_big, d_big, m_big, v_big = _adamw(
        [(own, None), (from_chips, 0), (from_chips, 1), (from_chips, 2)],
        _pack_rows(big_like, pack_rows, PACK_W), _pack_rows([m[k] for k in BIG], pack_rows, PACK_W),
        _pack_rows([v[k] for k in BIG], pack_rows, PACK_W), "adamw_matrices")
    out_g = dict(zip(BIG, _unpack(g_big.reshape(-1), big_like)))
    out_d = dict(zip(BIG, _unpack(d_big.reshape(-1), big_like)))
    out_m = dict(zip(BIG, _unpack(m_big.reshape(-1), big_like)))
    out_v = dict(zip(BIG, _unpack(v_big.reshape(-1), big_like)))

    small_like = [w[k] for k in SMALL]
    small_pack = jnp.concatenate([_pack_rows([grads[k] for k in SMALL], SMALL_ROWS, LANES), _pack_rows([grads["conv_w"]], CONVW_ROWS, LANES)], axis=0)
    parts = _allgather_direct(small_pack, "gather_small_grads")
    g_s, d_s, m_s, v_s = _adamw(
        [(parts[i, :SMALL_ROWS], None) for i in range(N_DEV)], _pack_rows(small_like, SMALL_ROWS, LANES),
        _pack_rows([m[k] for k in SMALL], SMALL_ROWS, LANES), _pack_rows([v[k] for k in SMALL], SMALL_ROWS, LANES), "adamw_replicated")
    for dst, src in ((out_g, g_s), (out_d, d_s), (out_m, m_s), (out_v, v_s)):
        dst.update(zip(SMALL, _unpack(src.reshape(-1), small_like)))
    shard_w = conv_w.shape[-1]
    conv_parts = parts[:, SMALL_ROWS:].reshape(N_DEV, DEPTH, CONV_WIDTH, CONV_CH)
    conv_mine = lax.dynamic_slice_in_dim(conv_parts, dev * shard_w, shard_w, axis=3)
    g_c, d_c, m_c, v_c = _adamw(
        [(_pack_rows([conv_mine[i]], CONVW_SHARD_ROWS, LANES), None) for i in range(N_DEV)], _pack_rows([conv_w], CONVW_SHARD_ROWS, LANES),
        _pack_rows([m["conv_w"]], CONVW_SHARD_ROWS, LANES), _pack_rows([v["conv_w"]], CONVW_SHARD_ROWS, LANES), "adamw_conv_w")
    for dst, src in ((out_g, g_c), (out_d, d_c), (out_m, m_c), (out_v, v_c)):
        dst["conv_w"] = src.reshape(-1)[:conv_w.size].reshape(conv_w.shape)

    loss = lax.psum(loss_local, ("x", "y", "c"))
    return (loss, dx.reshape(x.shape), *[out_g[k] for k in WEIGHTS], *[out_d[k] for k in WEIGHTS],
            *[out_m[k] for k in WEIGHTS], *[out_v[k] for k in WEIGHTS])
```

```python
import functools
import math

import jax
import jax.numpy as jnp
import numpy as np
from jax import lax
from jax.experimental import pallas as pl
from jax.experimental.pallas import tpu as pltpu

f32 = jnp.float32
bf16 = jnp.bfloat16

D_MODEL = 1024
SEQ = 2048
DEPTH = 2
HEAD_DIM = 64
N_ATTN_HEADS = 8
N_KV_HEADS = 2
ATTN_WIDTH = 512
KV_WIDTH = 128
ROPE_DIM = 16
ROPE_THETA = 500000.0
DILATIONS = (1, 4, 16)
ATTN_BLOCK = 128
SSM_HEADS = 16
SSM_INNER = 1024
SSM_GROUPS = 2
D_STATE = 128
CONV_WIDTH = 4
CHUNK = 128
CONV_CH = 1536
MIX_WIDTH = 1536
QKV_WIDTH = ATTN_WIDTH + 2 * KV_WIDTH
Z_OFF = 768
XBC_OFF = 1792
DT_OFF = 3328
IN_PROJ = 3344
IN_PROJ_PAD = 3456
FFN_HIDDEN = 2816
EPS = 1e-5
N_DEV = 8
ADAM_LR = 0.001
ADAM_B1 = 0.9
ADAM_B2 = 0.999
ADAM_EPS = 1e-08
ADAM_WD = 0.01
ADAM_STEP = 10

LANES = 128
SUBLANES = 8
VMEM_LIMIT = 56 * 1024 * 1024

MESH = pl.DeviceIdType.MESH
ANY = pl.BlockSpec(memory_space=pl.ANY)


def _cparams(sem, vmem=None):
    return pltpu.CompilerParams(dimension_semantics=sem, vmem_limit_bytes=vmem or VMEM_LIMIT)


def _sigmoid(x):
    return 1.0 / (1.0 + jnp.exp(-x))


def _silu(x):
    return x * _sigmoid(x)


def _dsilu(x):
    s = _sigmoid(x)
    return s * (1.0 + x * (1.0 - s))


def _softplus(x):
    return jnp.maximum(x, 0.0) + jnp.log(1.0 + jnp.exp(-jnp.abs(x)))


def _dot(a, b, dims, precision=None):
    return lax.dot_general(a, b, (dims, ((), ())), preferred_element_type=f32, precision=precision)


def _nn(a, b, precision=None):
    return _dot(a, b, ((1,), (0,)), precision)


def _nt(a, b):
    return _dot(a, b, ((1,), (1,)))


def _tn(a, b):
    return _dot(a, b, ((0,), (0,)))


def _rowsum8(t):
    n, w = t.shape
    return jnp.sum(t.reshape(n // SUBLANES, SUBLANES, w), axis=0)


def _matmul(a, b, *, mode, n_out=None, b_off=0, a_koff=0, b_koff=0, k_len=None, add=None, out_dtype=f32, tm=1024, tn=512, tk=1024, name):
    if mode == "tn":
        kdim_a, m = a.shape
    else:
        m, kdim_a = a.shape
    kk = k_len if k_len is not None else kdim_a
    n = n_out if n_out is not None else (b.shape[0] if mode == "nt" else b.shape[1])
    tm, tn, tk = min(tm, m), min(tn, n), min(tk, kk)
    assert m % tm == 0 and n % tn == 0 and kk % tk == 0, (name, m, n, kk, tm, tn, tk)
    nk = kk // tk
    if mode == "nn":
        a_spec = pl.BlockSpec((tm, tk), lambda i, j, k: (i, k + a_koff))
        b_spec = pl.BlockSpec((tk, tn), lambda i, j, k: (k + b_koff, j + b_off))
        dims = ((1,), (0,))
    elif mode == "nt":
        a_spec = pl.BlockSpec((tm, tk), lambda i, j, k: (i, k + a_koff))
        b_spec = pl.BlockSpec((tn, tk), lambda i, j, k: (j + b_off, k + b_koff))
        dims = ((1,), (1,))
    else:
        a_spec = pl.BlockSpec((tk, tm), lambda i, j, k: (k + a_koff, i))
        b_spec = pl.BlockSpec((tk, tn), lambda i, j, k: (k + b_koff, j + b_off))
        dims = ((0,), (0,))
    o_spec = pl.BlockSpec((tm, tn), lambda i, j, k: (i, j))
    has_add = add is not None

    def body(*refs):
        if has_add:
            a_ref, b_ref, add_ref, o_ref, acc_ref = refs
        else:
            a_ref, b_ref, o_ref, acc_ref = refs
        k = pl.program_id(2)
        part = _dot(a_ref[...].astype(bf16), b_ref[...].astype(bf16), dims)

        @pl.when(k == 0)
        def _():
            acc_ref[...] = part

        @pl.when(k > 0)
        def _():
            acc_ref[...] += part

        @pl.when(k == nk - 1)
        def _():
            r = acc_ref[...]
            if has_add:
                r = r + add_ref[...]
            o_ref[...] = r.astype(out_dtype)

    in_specs = [a_spec, b_spec] + ([o_spec] if has_add else [])
    args = (a, b) + ((add,) if has_add else ())
    return pl.pallas_call(
        body, name=name, grid=(m // tm, n // tn, nk), in_specs=in_specs, out_specs=o_spec,
        out_shape=jax.ShapeDtypeStruct((m, n), out_dtype), scratch_shapes=[pltpu.VMEM((tm, tn), f32)],
        compiler_params=_cparams(("parallel", "parallel", "arbitrary")),
    )(*args)


def _swiglu_fwd(hn, w_gate, w_up, name, tm=1024, tn=256):
    m, k = hn.shape
    n = w_gate.shape[1]

    def body(a_ref, wg_ref, wu_ref, g_ref, u_ref, act_ref):
        a = a_ref[...]
        g = _nn(a, wg_ref[...])
        u = _nn(a, wu_ref[...])
        g_ref[...] = g
        u_ref[...] = u
        act_ref[...] = (_silu(g) * u).astype(bf16)

    a_spec = pl.BlockSpec((tm, k), lambda i, j: (i, 0))
    w_spec = pl.BlockSpec((k, tn), lambda i, j: (0, j))
    o_spec = pl.BlockSpec((tm, tn), lambda i, j: (i, j))
    return pl.pallas_call(
        body, name=name, grid=(m // tm, n // tn), in_specs=[a_spec, w_spec, w_spec], out_specs=[o_spec, o_spec, o_spec],
        out_shape=[jax.ShapeDtypeStruct((m, n), f32), jax.ShapeDtypeStruct((m, n), f32), jax.ShapeDtypeStruct((m, n), bf16)],
        compiler_params=_cparams(("parallel", "parallel")),
    )(hn, w_gate, w_up)


def _swiglu_bwd(dh, w_down, g, u, name, tm=1024, tn=256):
    m, k = dh.shape
    n = w_down.shape[0]

    def body(a_ref, w_ref, g_ref, u_ref, dg_ref, du_ref):
        dact = _nt(a_ref[...].astype(bf16), w_ref[...])
        gg = g_ref[...]
        dg_ref[...] = (dact * u_ref[...] * _dsilu(gg)).astype(bf16)
        du_ref[...] = (dact * _silu(gg)).astype(bf16)

    a_spec = pl.BlockSpec((tm, k), lambda i, j: (i, 0))
    w_spec = pl.BlockSpec((tn, k), lambda i, j: (j, 0))
    o_spec = pl.BlockSpec((tm, tn), lambda i, j: (i, j))
    return pl.pallas_call(
        body, name=name, grid=(m // tm, n // tn), in_specs=[a_spec, w_spec, o_spec, o_spec], out_specs=[o_spec, o_spec],
        out_shape=[jax.ShapeDtypeStruct((m, n), bf16), jax.ShapeDtypeStruct((m, n), bf16)],
        compiler_params=_cparams(("parallel", "parallel")),
    )(dh, w_down, g, u)


def _rmsnorm_fwd(h, w, name, tm=512):
    m, d = h.shape

    def body(h_ref, w_ref, o_ref):
        x = h_ref[...]
        r = lax.rsqrt(jnp.mean(x * x, axis=-1, keepdims=True) + EPS)
        o_ref[...] = (x * r * w_ref[...]).astype(bf16)

    return pl.pallas_call(
        body, name=name, grid=(m // tm,),
        in_specs=[pl.BlockSpec((tm, d), lambda i: (i, 0)), pl.BlockSpec((1, d), lambda i: (0, 0))],
        out_specs=pl.BlockSpec((tm, d), lambda i: (i, 0)), out_shape=jax.ShapeDtypeStruct((m, d), bf16),
        compiler_params=_cparams(("parallel",)),
    )(h, w)


def _rmsnorm_bwd(dhn, h, w, dres, name, tm=512):
    m, d = h.shape

    def body(dhn_ref, h_ref, w_ref, dres_ref, dh_ref, dw_ref):
        x = h_ref[...]
        r = lax.rsqrt(jnp.mean(x * x, axis=-1, keepdims=True) + EPS)
        xhat = x * r
        dy = dhn_ref[...]
        gw = dy * w_ref[...]
        dh_ref[...] = dres_ref[...] + r * (gw - xhat * jnp.mean(gw * xhat, axis=-1, keepdims=True))
        part = _rowsum8(dy * xhat)

        @pl.when(pl.program_id(0) == 0)
        def _():
            dw_ref[...] = part

        @pl.when(pl.program_id(0) > 0)
        def _():
            dw_ref[...] += part

    row = pl.BlockSpec((tm, d), lambda i: (i, 0))
    return pl.pallas_call(
        body, name=name, grid=(m // tm,),
        in_specs=[row, row, pl.BlockSpec((1, d), lambda i: (0, 0)), row],
        out_specs=[row, pl.BlockSpec((SUBLANES, d), lambda i: (0, 0))],
        out_shape=[jax.ShapeDtypeStruct((m, d), f32), jax.ShapeDtypeStruct((SUBLANES, d), f32)],
        compiler_params=_cparams(("arbitrary",)),
    )(dhn, h, w, dres)


def _final_loss(h, w, target, name, tm=512):
    m, d = h.shape

    def body(h_ref, w_ref, t_ref, dh_ref, loss_ref, dw_ref):
        x = h_ref[...]
        r = lax.rsqrt(jnp.mean(x * x, axis=-1, keepdims=True) + EPS)
        xhat = x * r
        ww = w_ref[...]
        err = xhat * ww - t_ref[...]
        dy = err * (1.0 / d)
        gw = dy * ww
        dh_ref[...] = r * (gw - xhat * jnp.mean(gw * xhat, axis=-1, keepdims=True))
        lpart = _rowsum8(err * err) * (0.5 / d)
        wpart = _rowsum8(dy * xhat)

        @pl.when(pl.program_id(0) == 0)
        def _():
            loss_ref[...] = lpart
            dw_ref[...] = wpart

        @pl.when(pl.program_id(0) > 0)
        def _():
            loss_ref[...] += lpart
            dw_ref[...] += wpart

    row = pl.BlockSpec((tm, d), lambda i: (i, 0))
    acc = pl.BlockSpec((SUBLANES, d), lambda i: (0, 0))
    return pl.pallas_call(
        body, name=name, grid=(m // tm,),
        in_specs=[row, pl.BlockSpec((1, d), lambda i: (0, 0)), row], out_specs=[row, acc, acc],
        out_shape=[jax.ShapeDtypeStruct((m, d), f32), jax.ShapeDtypeStruct((SUBLANES, d), f32), jax.ShapeDtypeStruct((SUBLANES, d), f32)],
        compiler_params=_cparams(("arbitrary",)),
    )(h, w, target)


def _lane_tables():
    f = np.arange(LANES) % HEAD_DIM
    inv = ROPE_THETA ** (-jnp.arange(0, ROPE_DIM, 2, dtype=f32) / ROPE_DIM)
    invf = jnp.where(f < ROPE_DIM, inv[f % (ROPE_DIM // 2)], 0.0).astype(f32)
    return invf.reshape(1, LANES)


def _rope_tables(pos_col, name):
    t = pos_col.shape[0]
    tm = SEQ

    def body(p_ref, f_ref, c_ref, s1_ref, s2_ref):
        ang = p_ref[...].astype(f32) * f_ref[...]
        co, si = jnp.cos(ang), jnp.sin(ang)
        f = lax.broadcasted_iota(jnp.int32, (tm, LANES), 1) % HEAD_DIM
        c_ref[...] = jnp.where(f < ROPE_DIM, co, 1.0)
        s1_ref[...] = jnp.where(f < ROPE_DIM // 2, -si, 0.0)
        s2_ref[...] = jnp.where((f >= ROPE_DIM // 2) & (f < ROPE_DIM), si, 0.0)

    row = pl.BlockSpec((tm, LANES), lambda i: (i, 0))
    return pl.pallas_call(
        body, name=name, grid=(t // tm,),
        in_specs=[pl.BlockSpec((tm, 1), lambda i: (i, 0)), pl.BlockSpec((1, LANES), lambda i: (0, 0))],
        out_specs=[row, row, row], out_shape=[jax.ShapeDtypeStruct((t, LANES), f32)] * 3,
        compiler_params=_cparams(("parallel",)),
    )(pos_col, _lane_tables())


def _rot(x, c, s1, s2):
    return x * c + pltpu.roll(x, LANES - ROPE_DIM // 2, 1) * s1 + pltpu.roll(x, ROPE_DIM // 2, 1) * s2


def _rot_t(g, c, s1, s2):
    return g * c + pltpu.roll(g * s1, ROPE_DIM // 2, 1) + pltpu.roll(g * s2, LANES - ROPE_DIM // 2, 1)


def _dup_head(x, kvh, low):
    a = jnp.where(kvh == 0, x, pltpu.roll(x, HEAD_DIM, 1))
    return jnp.where(low, a, pltpu.roll(a, HEAD_DIM, 1))


def _deinterleave(src_ref, dst_ref, d, dtype):
    length = SEQ // d
    if d == 1:
        dst_ref[...] = src_ref[...].astype(dtype)
    else:
        for r in range(d):
            dst_ref[pl.ds(r * length, length), :] = src_ref[pl.ds(r, length, stride=d), :].astype(dtype)


def _interleave_store(src_ref, dst_ref, d, accumulate):
    length = SEQ // d
    if d == 1:
        if accumulate:
            dst_ref[...] += src_ref[...]
        else:
            dst_ref[...] = src_ref[...]
    else:
        for r in range(d):
            blk = src_ref[pl.ds(r * length, length), :]
            if accumulate:
                dst_ref[pl.ds(r, length, stride=d), :] = dst_ref[pl.ds(r, length, stride=d), :] + blk
            else:
                dst_ref[pl.ds(r, length, stride=d), :] = blk


def _attn_masks():
    qi = lax.broadcasted_iota(jnp.int32, (ATTN_BLOCK, ATTN_BLOCK), 0)
    ki = lax.broadcasted_iota(jnp.int32, (ATTN_BLOCK, ATTN_BLOCK), 1)
    low = lax.broadcasted_iota(jnp.int32, (ATTN_BLOCK, LANES), 1) < HEAD_DIM
    return ki <= qi, ki >= qi, low


NEG_INF = float("-inf")


def _attn_fwd(qkv, tabs, name):
    t = qkv.shape[0]
    nb = t // SEQ
    n_blk = SEQ // ATTN_BLOCK

    def body(q_ref, k_ref, v_ref, c_ref, s1_ref, s2_ref, o_ref, lse_ref,
             qr, kr, vr, qd, kd, vd, ob, lb, o0, o1, o2, l0, l1, l2):
        kvh = pl.program_id(1) // 2
        cur_ok, prev_ok, low = _attn_masks()
        lowfull = lax.broadcasted_iota(jnp.int32, (SEQ, LANES), 1) < HEAD_DIM
        c, s1, s2 = c_ref[...], s1_ref[...], s2_ref[...]
        qr[...] = _rot(q_ref[...], c, s1, s2) * (HEAD_DIM ** -0.5)
        kr[...] = _dup_head(_rot(k_ref[...], c, s1, s2), kvh, lowfull)
        vr[...] = _dup_head(v_ref[...], kvh, lowfull)
        onat, lnat = (o0, o1, o2), (l0, l1, l2)
        for bi, d in enumerate(DILATIONS):
            _deinterleave(qr, qd, d, bf16)
            _deinterleave(kr, kd, d, bf16)
            _deinterleave(vr, vd, d, bf16)
            per_res = n_blk // d

            def block(n, carry):
                start = pl.multiple_of(n * ATTN_BLOCK, ATTN_BLOCK)
                has_prev = (n % per_res) != 0
                pstart = pl.multiple_of(jnp.maximum(n - 1, 0) * ATTN_BLOCK, ATTN_BLOCK)
                qb = qd[pl.ds(start, ATTN_BLOCK), :]
                kc, kp = kd[pl.ds(start, ATTN_BLOCK), :], kd[pl.ds(pstart, ATTN_BLOCK), :]
                vc, vp = vd[pl.ds(start, ATTN_BLOCK), :], vd[pl.ds(pstart, ATTN_BLOCK), :]
                outs, lses = [], []
                for a in range(2):
                    qa = jnp.where(low if a == 0 else ~low, qb, jnp.zeros_like(qb))
                    sc = jnp.where(cur_ok, _nt(qa, kc), NEG_INF)
                    sp = jnp.where(prev_ok & has_prev, _nt(qa, kp), NEG_INF)
                    m = jnp.maximum(jnp.max(sc, axis=1, keepdims=True), jnp.max(sp, axis=1, keepdims=True))
                    pc, pp = jnp.exp(sc - m), jnp.exp(sp - m)
                    den = jnp.sum(pc, axis=1, keepdims=True) + jnp.sum(pp, axis=1, keepdims=True)
                    outs.append((_nn(pc.astype(bf16), vc) + _nn(pp.astype(bf16), vp)) / den)
                    lses.append(m + jnp.log(den))
                ob[pl.ds(start, ATTN_BLOCK), :] = jnp.where(low, outs[0], outs[1])
                lb[pl.ds(start, ATTN_BLOCK), :] = jnp.where(low, lses[0], lses[1])
                return carry

            lax.fori_loop(0, n_blk, block, 0)
            _interleave_store(ob, onat[bi], d, False)
            _interleave_store(lb, lnat[bi], d, False)
        la, lbb, lc = l0[...], l1[...], l2[...]
        lm = jnp.maximum(jnp.maximum(la, lbb), lc)
        wa, wb, wc = jnp.exp(la - lm), jnp.exp(lbb - lm), jnp.exp(lc - lm)
        ws = wa + wb + wc
        o_ref[...] = (wa * o0[...] + wb * o1[...] + wc * o2[...]) / ws
        lse_ref[...] = lm + jnp.log(ws)

    def col(jj):
        return pl.BlockSpec((SEQ, LANES), lambda b, j: (b, jj if jj is not None else j))

    tab = pl.BlockSpec((SEQ, LANES), lambda b, j: (b, 0))
    fs = pltpu.VMEM((SEQ, LANES), f32)
    hs = pltpu.VMEM((SEQ, LANES), bf16)
    return pl.pallas_call(
        body, name=name, grid=(nb, ATTN_WIDTH // LANES),
        in_specs=[col(None), col(ATTN_WIDTH // LANES), col(ATTN_WIDTH // LANES + 1), tab, tab, tab],
        out_specs=[col(None), col(None)],
        out_shape=[jax.ShapeDtypeStruct((t, ATTN_WIDTH), f32), jax.ShapeDtypeStruct((t, ATTN_WIDTH), f32)],
        scratch_shapes=[fs, fs, fs, hs, hs, hs, fs, fs, fs, fs, fs, fs, fs, fs],
        compiler_params=_cparams(("parallel", "parallel")),
    )(qkv, qkv, qkv, *tabs)


def _attn_bwd(qkv, tabs, o, lse, do, name):
    t = qkv.shape[0]
    nb = t // SEQ
    n_blk = SEQ // ATTN_BLOCK
    n_j = ATTN_WIDTH // LANES

    def body(q_ref, k_ref, v_ref, c_ref, s1_ref, s2_ref, o_ref, lse_ref, do_ref, dq_ref, dk_ref, dv_ref,
             qr, kr, vr, dl, qd, kd, vd, dod, lsd, dld, dqd, dkd, dvd, dqa, dka, dva):
        j = pl.program_id(1)
        kvh = j // 2
        cur_ok, prev_ok, low = _attn_masks()
        lowfull = lax.broadcasted_iota(jnp.int32, (SEQ, LANES), 1) < HEAD_DIM
        c, s1, s2 = c_ref[...], s1_ref[...], s2_ref[...]
        qr[...] = _rot(q_ref[...], c, s1, s2) * (HEAD_DIM ** -0.5)
        kr[...] = _dup_head(_rot(k_ref[...], c, s1, s2), kvh, lowfull)
        vr[...] = _dup_head(v_ref[...], kvh, lowfull)
        prod = do_ref[...] * o_ref[...]
        d_lo = jnp.sum(jnp.where(lowfull, prod, 0.0), axis=1, keepdims=True)
        d_hi = jnp.sum(jnp.where(lowfull, 0.0, prod), axis=1, keepdims=True)
        dl[...] = jnp.where(lowfull, d_lo, d_hi)
        dqa[...] = jnp.zeros_like(dqa)
        dka[...] = jnp.zeros_like(dka)
        dva[...] = jnp.zeros_like(dva)
        for d in DILATIONS:
            _deinterleave(qr, qd, d, bf16)
            _deinterleave(kr, kd, d, bf16)
            _deinterleave(vr, vd, d, bf16)
            _deinterleave(do_ref, dod, d, bf16)
            _deinterleave(lse_ref, lsd, d, f32)
            _deinterleave(dl, dld, d, f32)
            dkd[...] = jnp.zeros_like(dkd)
            dvd[...] = jnp.zeros_like(dvd)
            per_res = n_blk // d

            def block(n, carry):
                start = pl.multiple_of(n * ATTN_BLOCK, ATTN_BLOCK)
                has_prev = (n % per_res) != 0
                pstart = pl.multiple_of(jnp.maximum(n - 1, 0) * ATTN_BLOCK, ATTN_BLOCK)
                cur, prev = pl.ds(start, ATTN_BLOCK), pl.ds(pstart, ATTN_BLOCK)
                qb, dob = qd[cur, :], dod[cur, :]
                kc, kp, vc, vp = kd[cur, :], kd[prev, :], vd[cur, :], vd[prev, :]
                lsb, dlb = lsd[cur, :], dld[cur, :]
                dqs = []
                dkc = dkp = dvc = dvp = None
                for a in range(2):
                    sel = low if a == 0 else ~low
                    qa = jnp.where(sel, qb, jnp.zeros_like(qb))
                    doa = jnp.where(sel, dob, jnp.zeros_like(dob))
                    ls = lsb[:, a * HEAD_DIM:a * HEAD_DIM + 1]
                    de = dlb[:, a * HEAD_DIM:a * HEAD_DIM + 1]
                    pc = jnp.exp(jnp.where(cur_ok, _nt(qa, kc), NEG_INF) - ls)
                    pp = jnp.exp(jnp.where(prev_ok & has_prev, _nt(qa, kp), NEG_INF) - ls)
                    dsc = (pc * (_nt(doa, vc) - de)).astype(bf16)
                    dsp = (pp * (_nt(doa, vp) - de)).astype(bf16)
                    dqs.append(_nn(dsc, kc) + _nn(dsp, kp))
                    pcb, ppb = pc.astype(bf16), pp.astype(bf16)
                    t_kc, t_kp, t_vc, t_vp = _tn(dsc, qa), _tn(dsp, qa), _tn(pcb, doa), _tn(ppb, doa)
                    dkc, dkp = (t_kc, t_kp) if a == 0 else (dkc + t_kc, dkp + t_kp)
                    dvc, dvp = (t_vc, t_vp) if a == 0 else (dvc + t_vc, dvp + t_vp)
                dqd[cur, :] = jnp.where(low, dqs[0], dqs[1])
                dkd[cur, :] += dkc
                dvd[cur, :] += dvc
                dkd[prev, :] += dkp
                dvd[prev, :] += dvp
                return carry

            lax.fori_loop(0, n_blk, block, 0)
            _interleave_store(dqd, dqa, d, True)
            _interleave_store(dkd, dka, d, True)
            _interleave_store(dvd, dva, d, True)
        dq_ref[...] = _rot_t(dqa[...] * (HEAD_DIM ** -0.5), c, s1, s2)
        dkf = dka[...]
        dkf = _rot_t(dkf + pltpu.roll(dkf, HEAD_DIM, 1), c, s1, s2)
        dvf = dva[...]
        dvf = dvf + pltpu.roll(dvf, HEAD_DIM, 1)
        mine = (lax.broadcasted_iota(jnp.int32, (SEQ, LANES), 1) // HEAD_DIM) == kvh
        dkc_, dvc_ = jnp.where(mine, dkf, 0.0), jnp.where(mine, dvf, 0.0)

        @pl.when(j == 0)
        def _():
            dk_ref[...] = dkc_
            dv_ref[...] = dvc_

        @pl.when(j > 0)
        def _():
            dk_ref[...] += dkc_
            dv_ref[...] += dvc_

    def col(jj):
        return pl.BlockSpec((SEQ, LANES), lambda b, j: (b, jj if jj is not None else j))

    tab = pl.BlockSpec((SEQ, LANES), lambda b, j: (b, 0))
    fs = pltpu.VMEM((SEQ, LANES), f32)
    hs = pltpu.VMEM((SEQ, LANES), bf16)
    return pl.pallas_call(
        body, name=name, grid=(nb, n_j),
        in_specs=[col(None), col(n_j), col(n_j + 1), tab, tab, tab, col(None), col(None), col(None)],
        out_specs=[col(None), tab, tab],
        out_shape=[jax.ShapeDtypeStruct((t, ATTN_WIDTH), f32), jax.ShapeDtypeStruct((t, LANES), f32), jax.ShapeDtypeStruct((t, LANES), f32)],
        scratch_shapes=[fs, fs, fs, fs, hs, hs, hs, hs, fs, fs, fs, fs, fs, fs, fs, fs],
        compiler_params=_cparams(("parallel", "arbitrary")),
    )(qkv, qkv, qkv, *tabs, o, lse, do)


def _conv_pre(x, w_ref, b_ref, row):
    shifted = [x] + [jnp.where(row >= s, pltpu.roll(x, s, 0), 0.0) for s in range(1, CONV_WIDTH)]
    pre = b_ref[...] + w_ref[CONV_WIDTH - 1:CONV_WIDTH, :] * x
    for s in range(1, CONV_WIDTH):
        pre = pre + w_ref[CONV_WIDTH - 1 - s:CONV_WIDTH - s, :] * shifted[s]
    return pre, shifted


def _conv_fwd(x, w, b, name, tc=512):
    t, ch = x.shape

    def body(x_ref, w_ref, b_ref, o_ref):
        row = lax.broadcasted_iota(jnp.int32, (SEQ, tc), 0)
        pre, _ = _conv_pre(x_ref[...], w_ref, b_ref, row)
        o_ref[...] = _silu(pre)

    xs = pl.BlockSpec((SEQ, tc), lambda i, j: (i, j))
    return pl.pallas_call(
        body, name=name, grid=(t // SEQ, ch // tc),
        in_specs=[xs, pl.BlockSpec((CONV_WIDTH, tc), lambda i, j: (0, j)), pl.BlockSpec((1, tc), lambda i, j: (0, j))],
        out_specs=xs, out_shape=jax.ShapeDtypeStruct((t, ch), f32),
        compiler_params=_cparams(("parallel", "parallel")),
    )(x, w, b)


def _conv_bwd(x, w, b, dact, name, tc=512):
    t, ch = x.shape

    def body(x_ref, w_ref, b_ref, d_ref, dx_ref, dw_ref, db_ref):
        row = lax.broadcasted_iota(jnp.int32, (SEQ, tc), 0)
        pre, shifted = _conv_pre(x_ref[...], w_ref, b_ref, row)
        dpre = d_ref[...] * _dsilu(pre)
        dx = w_ref[CONV_WIDTH - 1:CONV_WIDTH, :] * dpre
        for s in range(1, CONV_WIDTH):
            dx = dx + w_ref[CONV_WIDTH - 1 - s:CONV_WIDTH - s, :] * jnp.where(row < SEQ - s, pltpu.roll(dpre, SEQ - s, 0), 0.0)
        dx_ref[...] = dx
        first = pl.program_id(1) == 0
        parts = [jnp.sum(dpre * shifted[CONV_WIDTH - 1 - k], axis=0, keepdims=True) for k in range(CONV_WIDTH)]
        dbp = jnp.sum(dpre, axis=0, keepdims=True)

        @pl.when(first)
        def _():
            for k in range(CONV_WIDTH):
                dw_ref[k:k + 1, :] = parts[k]
            db_ref[...] = dbp

        @pl.when(jnp.logical_not(first))
        def _():
            for k in range(CONV_WIDTH):
                dw_ref[k:k + 1, :] += parts[k]
            db_ref[...] += dbp

    xs = pl.BlockSpec((SEQ, tc), lambda j, i: (i, j))
    ws = pl.BlockSpec((CONV_WIDTH, tc), lambda j, i: (0, j))
    bs = pl.BlockSpec((1, tc), lambda j, i: (0, j))
    return pl.pallas_call(
        body, name=name, grid=(ch // tc, t // SEQ),
        in_specs=[xs, ws, bs, xs], out_specs=[xs, ws, bs],
        out_shape=[jax.ShapeDtypeStruct((t, ch), f32), jax.ShapeDtypeStruct((CONV_WIDTH, ch), f32), jax.ShapeDtypeStruct((1, ch), f32)],
        compiler_params=_cparams(("parallel", "arbitrary")),
    )(x, w, b, dact)


GROUP_W = SSM_INNER // SSM_GROUPS
HEADS_PER_GROUP = SSM_HEADS // SSM_GROUPS
HI = lax.Precision.HIGHEST


def _ssd_common(xbc_ref, dt_ref, bias_ref, alog_ref):
    r = lax.broadcasted_iota(jnp.int32, (CHUNK, CHUNK), 0)
    cidx = lax.broadcasted_iota(jnp.int32, (CHUNK, CHUNK), 1)
    causal = r >= cidx
    tril = causal.astype(f32)
    expand = (lax.broadcasted_iota(jnp.int32, (CHUNK, SSM_INNER), 0)
              == lax.broadcasted_iota(jnp.int32, (CHUNK, SSM_INNER), 1) // HEAD_DIM).astype(f32)
    head_lane = cidx < SSM_HEADS
    dtp = dt_ref[...] + bias_ref[...]
    dt = jnp.where(head_lane, _softplus(dtp), 0.0)
    a_neg = -jnp.exp(alog_ref[...])
    a = dt * a_neg
    cs = _nn(tril, a, HI)
    dt_e = _nn(dt, expand, HI)
    cs_e = _nn(cs, expand, HI)
    xs = xbc_ref[:, 0:SSM_INNER]
    xg = xs * dt_e
    ecs = jnp.exp(cs_e)
    cs_last = cs_e[CHUNK - 1:CHUNK, :]
    dse = jnp.exp(cs_last - cs_e)
    cde = jnp.exp(cs_last)
    return dict(r=r, cidx=cidx, causal=causal, tril=tril, expand=expand, head_lane=head_lane, dtp=dtp, dt=dt, a_neg=a_neg,
                cs=cs, cst=cs.T, dt_e=dt_e, cs_e=cs_e, xs=xs, xg=xg, ecs=ecs, dse=dse, cde=cde)


def _decay_mat(q, h):
    return jnp.exp(jnp.where(q["causal"], q["cs"][:, h:h + 1] - q["cst"][h:h + 1, :], NEG_INF))


def _gate_norm(y, z, nw):
    y2 = y * _silu(z)
    outs, xhats, rs = [], [], []
    for g in range(SSM_GROUPS):
        sl = slice(g * GROUP_W, (g + 1) * GROUP_W)
        yg = y2[:, sl]
        r = lax.rsqrt(jnp.mean(yg * yg, axis=-1, keepdims=True) + EPS)
        xhats.append(yg * r)
        rs.append(r)
        outs.append(yg * r * nw[:, sl])
    return y2, outs, xhats, rs


def _ssd_fwd(xbc, z, dtp, params, name):
    t = xbc.shape[0]
    n_chunk = SEQ // CHUNK
    low = None

    def body(xbc_ref, z_ref, dt_ref, bias_ref, alog_ref, dskip_ref, nw_ref, yn_ref, y_ref, hs_ref, h_scr):
        @pl.when(pl.program_id(1) == 0)
        def _():
            h_scr[...] = jnp.zeros_like(h_scr)

        q = _ssd_common(xbc_ref, dt_ref, bias_ref, alog_ref)
        low = lax.broadcasted_iota(jnp.int32, (CHUNK, LANES), 1) < HEAD_DIM
        xgb = q["xg"].astype(bf16)
        wst = (q["xg"] * q["dse"]).astype(bf16)
        hs_ref[0] = h_scr[...]
        ys = []
        for g in range(SSM_GROUPS):
            gl = slice(g * GROUP_W, (g + 1) * GROUP_W)
            bg = xbc_ref[:, SSM_INNER + g * D_STATE:SSM_INNER + (g + 1) * D_STATE].astype(bf16)
            cg = xbc_ref[:, SSM_INNER + SSM_GROUPS * D_STATE + g * D_STATE:SSM_INNER + SSM_GROUPS * D_STATE + (g + 1) * D_STATE].astype(bf16)
            cb = _nt(cg, bg)
            hg = h_scr[g]
            yoff = _nn(cg, hg.astype(bf16)) * q["ecs"][:, gl]
            pieces = []
            for i in range(HEADS_PER_GROUP // 2):
                h0 = g * HEADS_PER_GROUP + 2 * i
                xp = xgb[:, h0 * HEAD_DIM:(h0 + 2) * HEAD_DIM]
                m0 = (cb * _decay_mat(q, h0)).astype(bf16)
                m1 = (cb * _decay_mat(q, h0 + 1)).astype(bf16)
                zero = jnp.zeros_like(xp)
                pieces.append(_nn(m0, jnp.where(low, xp, zero)) + _nn(m1, jnp.where(low, zero, xp)))
            ys.append(jnp.concatenate(pieces, axis=1) + yoff + dskip_ref[:, gl] * q["xs"][:, gl])
            h_scr[g] = hg * q["cde"][:, gl] + _tn(bg, wst[:, gl])
        y = jnp.concatenate(ys, axis=1)
        y_ref[...] = y
        _, outs, _, _ = _gate_norm(y, z_ref[...], nw_ref[...])
        yn_ref[...] = jnp.concatenate(outs, axis=1).astype(bf16)

    def rows(w):
        return pl.BlockSpec((CHUNK, w), lambda b, c: (b * n_chunk + c, 0))

    def par(w):
        return pl.BlockSpec((1, w), lambda b, c: (0, 0))

    return pl.pallas_call(
        body, name=name, grid=(t // SEQ, n_chunk),
        in_specs=[rows(CONV_CH), rows(SSM_INNER), rows(LANES), par(LANES), par(LANES), par(SSM_INNER), par(SSM_INNER)],
        out_specs=[rows(SSM_INNER), rows(SSM_INNER), pl.BlockSpec((1, SSM_GROUPS, D_STATE, GROUP_W), lambda b, c: (b * n_chunk + c, 0, 0, 0))],
        out_shape=[jax.ShapeDtypeStruct((t, SSM_INNER), bf16), jax.ShapeDtypeStruct((t, SSM_INNER), f32),
                   jax.ShapeDtypeStruct((t // CHUNK, SSM_GROUPS, D_STATE, GROUP_W), f32)],
        scratch_shapes=[pltpu.VMEM((SSM_GROUPS, D_STATE, GROUP_W), f32)],
        compiler_params=_cparams(("parallel", "arbitrary")),
    )(xbc, z, dtp, *params)


def _ssd_bwd(xbc, z, dtp, y, hs, dyn, params, name):
    t = xbc.shape[0]
    n_chunk = SEQ // CHUNK

    def body(xbc_ref, z_ref, dt_ref, y_ref, hs_ref, dyn_ref, bias_ref, alog_ref, dskip_ref, nw_ref,
             dxbc_ref, dz_ref, ddt_ref, dnw_ref, dds_ref, dal_ref, dbi_ref, dh_scr):
        @pl.when(pl.program_id(1) == 0)
        def _():
            dh_scr[...] = jnp.zeros_like(dh_scr)

        q = _ssd_common(xbc_ref, dt_ref, bias_ref, alog_ref)
        low = lax.broadcasted_iota(jnp.int32, (CHUNK, LANES), 1) < HEAD_DIM
        last_row = lax.broadcasted_iota(jnp.int32, (CHUNK, GROUP_W), 0) == CHUNK - 1
        xs, xg = q["xs"], q["xg"]
        xgb = xg.astype(bf16)
        wf = xg * q["dse"]
        wst = wf.astype(bf16)
        zz = z_ref[...]
        yy = y_ref[...]
        sz = _silu(zz)
        y2, _, xhats, rs = _gate_norm(yy, zz, nw_ref[...])
        dyn_ = dyn_ref[...]
        dy2s, dnws = [], []
        for g in range(SSM_GROUPS):
            gl = slice(g * GROUP_W, (g + 1) * GROUP_W)
            gw = dyn_[:, gl] * nw_ref[:, gl]
            dy2s.append(rs[g] * (gw - xhats[g] * jnp.mean(gw * xhats[g], axis=-1, keepdims=True)))
            dnws.append(_rowsum8(dyn_[:, gl] * xhats[g]))
        dy2 = jnp.concatenate(dy2s, axis=1)
        dy = dy2 * sz
        dz_ref[...] = dy2 * yy * _dsilu(zz)
        dnw_p = jnp.concatenate(dnws, axis=1)
        dds_p = _rowsum8(dy * xs)
        dyb = dy.astype(bf16)
        gfull = (dy * q["ecs"]).astype(bf16)
        dcs_c = jnp.zeros((CHUNK, CHUNK), f32)
        dcs_r = jnp.zeros((CHUNK, CHUNK), f32)
        dcs_e_parts, dxg_parts = [], []
        for g in range(SSM_GROUPS):
            gl = slice(g * GROUP_W, (g + 1) * GROUP_W)
            bsl = slice(SSM_INNER + g * D_STATE, SSM_INNER + (g + 1) * D_STATE)
            csl = slice(SSM_INNER + SSM_GROUPS * D_STATE + g * D_STATE, SSM_INNER + SSM_GROUPS * D_STATE + (g + 1) * D_STATE)
            bg = xbc_ref[:, bsl].astype(bf16)
            cg = xbc_ref[:, csl].astype(bf16)
            cb = _nt(cg, bg)
            hg = hs_ref[0, g]
            hgb = hg.astype(bf16)
            dhn = dh_scr[g]
            dhnb = dhn.astype(bf16)
            yoff = _nn(cg, hgb) * q["ecs"][:, gl]
            dw_ = _nn(bg, dhnb)
            r_e = dw_ * wf[:, gl]
            to_last = jnp.sum(r_e, axis=0, keepdims=True) + jnp.sum(dhn * hg, axis=0, keepdims=True) * q["cde"][:, gl]
            dcs_e_parts.append(dy[:, gl] * yoff - r_e + jnp.where(last_row, to_last, 0.0))
            dcb = jnp.zeros((CHUNK, CHUNK), f32)
            dxg_pairs = []
            for i in range(HEADS_PER_GROUP // 2):
                h0 = g * HEADS_PER_GROUP + 2 * i
                psl = slice(h0 * HEAD_DIM, (h0 + 2) * HEAD_DIM)
                xp = xgb[:, psl]
                dyp = dyb[:, psl]
                zero = jnp.zeros_like(dyp)
                tns = []
                for a in range(2):
                    h = h0 + a
                    lm = _decay_mat(q, h)
                    m = cb * lm
                    dm = _nt(jnp.where(low, dyp, zero) if a == 0 else jnp.where(low, zero, dyp), xp)
                    dcb = dcb + dm * lm
                    nmat = dm * m
                    dcs_c = dcs_c + jnp.where(q["cidx"] == h, jnp.sum(nmat, axis=1, keepdims=True), 0.0)
                    dcs_r = dcs_r + jnp.where(q["r"] == h, jnp.sum(nmat, axis=0, keepdims=True), 0.0)
                    tns.append(_tn(m.astype(bf16), dyp))
                dxg_pairs.append(jnp.where(low, tns[0], tns[1]))
            dxg_parts.append(jnp.concatenate(dxg_pairs, axis=1) + dw_ * q["dse"][:, gl])
            dcbb = dcb.astype(bf16)
            dxbc_ref[:, csl] = _nt(gfull[:, gl], hgb) + _nn(dcbb, bg)
            dxbc_ref[:, bsl] = _nt(wst[:, gl], dhnb) + _tn(dcbb, cg)
            dh_scr[g] = dhn * q["cde"][:, gl] + _tn(cg, gfull[:, gl])
        dxg = jnp.concatenate(dxg_parts, axis=1)
        dcs_e = jnp.concatenate(dcs_e_parts, axis=1)
        dxbc_ref[:, 0:SSM_INNER] = dskip_ref[...] * dy + dxg * q["dt_e"]
        dcs = dcs_c - dcs_r.T + _dot(dcs_e, q["expand"], ((1,), (1,)), HI)
        triu = (q["cidx"] >= q["r"]).astype(f32)
        da = _nn(triu, dcs, HI)
        ddt = _dot(dxg * xs, q["expand"], ((1,), (1,)), HI) + da * q["a_neg"]
        ddtp = jnp.where(q["head_lane"], ddt * _sigmoid(q["dtp"]), 0.0)
        ddt_ref[...] = ddtp
        dal_p = _rowsum8(da * q["dt"]) * q["a_neg"]
        dbi_p = _rowsum8(ddtp)
        first = (pl.program_id(0) == 0) & (pl.program_id(1) == 0)

        @pl.when(first)
        def _():
            dnw_ref[...] = dnw_p
            dds_ref[...] = dds_p
            dal_ref[...] = dal_p
            dbi_ref[...] = dbi_p

        @pl.when(jnp.logical_not(first))
        def _():
            dnw_ref[...] += dnw_p
            dds_ref[...] += dds_p
            dal_ref[...] += dal_p
            dbi_ref[...] += dbi_p

    def rows(w):
        return pl.BlockSpec((CHUNK, w), lambda b, c: (b * n_chunk + n_chunk - 1 - c, 0))

    def par(w):
        return pl.BlockSpec((1, w), lambda b, c: (0, 0))

    def acc(w):
        return pl.BlockSpec((SUBLANES, w), lambda b, c: (0, 0))

    return pl.pallas_call(
        body, name=name, grid=(t // SEQ, n_chunk),
        in_specs=[rows(CONV_CH), rows(SSM_INNER), rows(LANES), rows(SSM_INNER),
                  pl.BlockSpec((1, SSM_GROUPS, D_STATE, GROUP_W), lambda b, c: (b * n_chunk + n_chunk - 1 - c, 0, 0, 0)),
                  rows(SSM_INNER), par(LANES), par(LANES), par(SSM_INNER), par(SSM_INNER)],
        out_specs=[rows(CONV_CH), rows(SSM_INNER), rows(LANES), acc(SSM_INNER), acc(SSM_INNER), acc(LANES), acc(LANES)],
        out_shape=[jax.ShapeDtypeStruct((t, CONV_CH), f32), jax.ShapeDtypeStruct((t, SSM_INNER), f32), jax.ShapeDtypeStruct((t, LANES), f32),
                   jax.ShapeDtypeStruct((SUBLANES, SSM_INNER), f32), jax.ShapeDtypeStruct((SUBLANES, SSM_INNER), f32),
                   jax.ShapeDtypeStruct((SUBLANES, LANES), f32), jax.ShapeDtypeStruct((SUBLANES, LANES), f32)],
        scratch_shapes=[pltpu.VMEM((SSM_GROUPS, D_STATE, GROUP_W), f32)],
        compiler_params=_cparams(("arbitrary", "arbitrary")),
    )(xbc, z, dtp, y, hs, dyn, *params)


def _adamw(g_parts, w, m, v, name, tr=None):
    rows, width = w.shape
    n = len(g_parts)
    if tr is None:
        tr = _row_tile(rows)

    def body(*refs):
        g_refs, (w_ref, m_ref, v_ref, g_out, d_out, m_out, v_out) = refs[:n], refs[n:]

        def part(i):
            return g_refs[i][...] if g_parts[i][1] is None else g_refs[i][0]

        g = part(0)
        for i in range(1, n):
            g = g + part(i)
        mm = ADAM_B1 * m_ref[...] + (1.0 - ADAM_B1) * g
        vv = ADAM_B2 * v_ref[...] + (1.0 - ADAM_B2) * (g * g)
        m_hat = mm / (1.0 - ADAM_B1 ** ADAM_STEP)
        v_hat = vv / (1.0 - ADAM_B2 ** ADAM_STEP)
        g_out[...] = g
        d_out[...] = -ADAM_LR * (m_hat / (jnp.sqrt(v_hat) + ADAM_EPS) + ADAM_WD * w_ref[...])
        m_out[...] = mm
        v_out[...] = vv

    spec = pl.BlockSpec((tr, width), lambda i: (i, 0))

    def gspec(idx):
        return spec if idx is None else pl.BlockSpec((1, tr, width), lambda i: (idx, i, 0))

    return pl.pallas_call(
        body, name=name, grid=(rows // tr,), in_specs=[gspec(idx) for _, idx in g_parts] + [spec] * 3, out_specs=[spec] * 4,
        out_shape=[jax.ShapeDtypeStruct((rows, width), f32)] * 4, compiler_params=_cparams(("parallel",)),
    )(*[a for a, _ in g_parts], w, m, v)


def _row_tile(rows, cap=512):
    for cand in range(min(rows, cap) // SUBLANES * SUBLANES, 0, -SUBLANES):
        if rows % cand == 0:
            return cand
    return rows


def _cols_from_devices(g, width, name):
    n_dev, depth, a, b = g.shape

    def body(g_ref, o_ref):
        for i in range(n_dev):
            o_ref[0, :, i * b:(i + 1) * b] = g_ref[i, 0]
        if width > n_dev * b:
            o_ref[0, :, n_dev * b:width] = jnp.zeros((a, width - n_dev * b), o_ref.dtype)

    return pl.pallas_call(
        body, name=name, grid=(depth,), in_specs=[pl.BlockSpec((n_dev, 1, a, b), lambda l: (0, l, 0, 0))],
        out_specs=pl.BlockSpec((1, a, width), lambda l: (l, 0, 0)), out_shape=jax.ShapeDtypeStruct((depth, a, width), g.dtype),
        compiler_params=_cparams(("parallel",)),
    )(g)


def _devices_from_cols(per_layer, b, name, tr=256):
    depth = len(per_layer)
    a, width = per_layer[0].shape

    def body(*refs):
        o_ref = refs[depth]
        for l in range(depth):
            for i in range(N_DEV):
                o_ref[i, l] = refs[l][:, i * b:(i + 1) * b]

    return pl.pallas_call(
        body, name=name, grid=(a // tr,), in_specs=[pl.BlockSpec((tr, width), lambda r: (r, 0))] * depth,
        out_specs=pl.BlockSpec((N_DEV, depth, tr, b), lambda r: (0, 0, r, 0)),
        out_shape=jax.ShapeDtypeStruct((N_DEV, depth, a, b), per_layer[0].dtype), compiler_params=_cparams(("parallel",)),
    )(*per_layer)


def _add_kept(g, recv, core, name):
    nblk, _, rows, width = g.shape
    tr = _row_tile(rows)

    def body(c_ref, g_ref, r_ref, o_ref):
        o_ref[0] = g_ref[0, 0] + r_ref[0]

    grid_spec = pltpu.PrefetchScalarGridSpec(
        num_scalar_prefetch=1, grid=(nblk, rows // tr),
        in_specs=[pl.BlockSpec((1, 1, tr, width), lambda i, j, c: (i, c[0], j, 0)), pl.BlockSpec((1, tr, width), lambda i, j, c: (i, j, 0))],
        out_specs=pl.BlockSpec((1, tr, width), lambda i, j, c: (i, j, 0)))
    return pl.pallas_call(
        body, name=name, grid_spec=grid_spec, out_shape=jax.ShapeDtypeStruct((nblk, rows, width), f32),
        compiler_params=_cparams(("parallel", "parallel")),
    )(core, g, recv)


def _me():
    return lax.axis_index("x"), lax.axis_index("y"), lax.axis_index("c")


def _allgather_two_level(shards, name):
    n = len(shards)
    per = 7

    def body(*refs):
        ins, outs = refs[:n], refs[n:2 * n]
        send_sems, recv_sems, local_sems = refs[2 * n:]
        x, y, c = _me()
        me, sibling = (x, y, c), (x, y, 1 - c)
        chips = [(1 - x, y), (x, 1 - y), (1 - x, 1 - y)]

        def slot(a, p):
            return outs[a].at[4 * p[0] + 2 * p[1] + p[2]]

        def copy(a, k, block, to, src=None):
            return pltpu.make_async_remote_copy(
                src_ref=slot(a, block) if src is None else src, dst_ref=slot(a, block),
                send_sem=send_sems.at[a * per + k], recv_sem=recv_sems.at[a * per + k], device_id=to, device_id_type=MESH)

        mine = [pltpu.make_async_copy(ins[a], slot(a, me), local_sems.at[a]) for a in range(n)]
        for cp in mine:
            cp.start()
        first = []
        for a in range(n):
            first.append(copy(a, 0, me, sibling, src=ins[a]))
            first += [copy(a, 1 + j, me, (*chip, c), src=ins[a]) for j, chip in enumerate(chips)]
        for cp in first:
            cp.start()
        passed = []
        for j, chip in enumerate(chips):
            for a in range(n):
                copy(a, 1 + j, (*chip, c), me).wait_recv()
                fwd = copy(a, 4 + j, (*chip, c), sibling)
                fwd.start()
                passed.append(fwd)
        for a in range(n):
            copy(a, 0, sibling, me).wait_recv()
            for j, chip in enumerate(chips):
                copy(a, 4 + j, (*chip, 1 - c), me).wait_recv()
        for cp in first + passed:
            cp.wait_send()
        for cp in mine:
            cp.wait()

    return pl.pallas_call(
        body, name=name, in_specs=[ANY] * n, out_specs=[ANY] * n,
        out_shape=[jax.ShapeDtypeStruct((N_DEV,) + s.shape, s.dtype) for s in shards],
        scratch_shapes=[pltpu.SemaphoreType.DMA((n * per,)), pltpu.SemaphoreType.DMA((n * per,)), pltpu.SemaphoreType.DMA((n,))],
    )(*shards)


def _allgather_direct(row, name):
    def body(in_ref, out_ref, send_sems, recv_sems, local_sem):
        x, y, c = _me()
        mine = out_ref.at[4 * x + 2 * y + c]
        local = pltpu.make_async_copy(in_ref, mine, local_sem)
        local.start()
        sends = []
        for k in range(1, N_DEV):
            px, py, pc = x ^ (k >> 2), y ^ ((k >> 1) & 1), c ^ (k & 1)
            sends.append(pltpu.make_async_remote_copy(
                src_ref=in_ref, dst_ref=mine, send_sem=send_sems.at[k - 1], recv_sem=recv_sems.at[k - 1],
                device_id=(px, py, pc), device_id_type=MESH))
        for cp in sends:
            cp.start()
        for k in range(1, N_DEV):
            px, py, pc = x ^ (k >> 2), y ^ ((k >> 1) & 1), c ^ (k & 1)
            theirs = out_ref.at[4 * px + 2 * py + pc]
            pltpu.make_async_remote_copy(
                src_ref=in_ref, dst_ref=theirs, send_sem=send_sems.at[k - 1], recv_sem=recv_sems.at[k - 1],
                device_id=(px, py, pc), device_id_type=MESH).wait_recv()
        for cp in sends:
            cp.wait_send()
        local.wait()

    return pl.pallas_call(
        body, name=name, in_specs=[ANY], out_specs=ANY, out_shape=jax.ShapeDtypeStruct((N_DEV,) + row.shape, row.dtype),
        scratch_shapes=[pltpu.SemaphoreType.DMA((N_DEV - 1,)), pltpu.SemaphoreType.DMA((N_DEV - 1,)), pltpu.SemaphoreType.DMA],
    )(row)


N_CHIP = N_DEV // 2


def _exchange_sibling(gs, name):
    n = len(gs)

    def body(*refs):
        ins, outs = refs[:n], refs[n:2 * n]
        send_sems, recv_sems = refs[2 * n:]
        x, y, c = _me()
        copies = [pltpu.make_async_remote_copy(
            src_ref=ins[a].at[i, 1 - c], dst_ref=outs[a].at[i], send_sem=send_sems.at[a * N_CHIP + i], recv_sem=recv_sems.at[a * N_CHIP + i],
            device_id=(x, y, 1 - c), device_id_type=MESH) for a in range(n) for i in range(N_CHIP)]
        for cp in copies:
            cp.start()
        for cp in copies:
            cp.wait_recv()
        for cp in copies:
            cp.wait_send()

    return pl.pallas_call(
        body, name=name, in_specs=[ANY] * n, out_specs=[ANY] * n,
        out_shape=[jax.ShapeDtypeStruct((N_CHIP,) + g.shape[2:], g.dtype) for g in gs],
        scratch_shapes=[pltpu.SemaphoreType.DMA((n * N_CHIP,)), pltpu.SemaphoreType.DMA((n * N_CHIP,))],
    )(*gs)


def _exchange_chips(ps, name):
    n = len(ps)

    def body(*refs):
        ins, outs = refs[:n], refs[n:2 * n]
        send_sems, recv_sems = refs[2 * n:]
        x, y, c = _me()
        chips = [(1 - x, y), (x, 1 - y), (1 - x, 1 - y)]
        copies = [pltpu.make_async_remote_copy(
            src_ref=ins[a].at[2 * cx + cy], dst_ref=outs[a].at[k], send_sem=send_sems.at[a * 3 + k], recv_sem=recv_sems.at[a * 3 + k],
            device_id=(cx, cy, c), device_id_type=MESH) for a in range(n) for k, (cx, cy) in enumerate(chips)]
        for cp in copies:
            cp.start()
        for cp in copies:
            cp.wait_recv()
        for cp in copies:
            cp.wait_send()

    return pl.pallas_call(
        body, name=name, in_specs=[ANY] * n, out_specs=[ANY] * n,
        out_shape=[jax.ShapeDtypeStruct((3,) + p.shape[1:], p.dtype) for p in ps],
        scratch_shapes=[pltpu.SemaphoreType.DMA((n * 3,)), pltpu.SemaphoreType.DMA((n * 3,))],
    )(*ps)


def _row(v, width=None):
    v = v.reshape(1, -1).astype(f32)
    if width is not None and v.shape[1] < width:
        v = jnp.pad(v, ((0, 0), (0, width - v.shape[1])))
    return v


def _layer_params(p, l):
    return dict(
        norm_mix=_row(p["norm_mix"][l]), norm_ffn=_row(p["norm_ffn"][l]), conv_w=p["conv_w"][l], conv_b=_row(p["conv_b"][l]),
        ssd=(_row(p["dt_bias"][l], LANES), _row(p["a_log"][l], LANES), _row(jnp.repeat(p["d_skip"][l], HEAD_DIM)), _row(p["ssm_norm"][l])))


def _layer_fwd(h, big, sp, tabs, l):
    tag = f"l{l}_"
    w_in, w_out, w_gate, w_up, w_down = big
    hn = _rmsnorm_fwd(h, sp["norm_mix"], tag + "norm_mix")
    qkv = _matmul(hn, w_in, mode="nn", n_out=QKV_WIDTH, tn=256, b_off=0, name=tag + "proj_qkv")
    z = _matmul(hn, w_in, mode="nn", n_out=SSM_INNER, tn=256, b_off=Z_OFF // 256, name=tag + "proj_z")
    xbc_pre = _matmul(hn, w_in, mode="nn", n_out=CONV_CH, tn=256, b_off=XBC_OFF // 256, name=tag + "proj_xbc")
    dtp = _matmul(hn, w_in, mode="nn", n_out=LANES, tn=LANES, b_off=DT_OFF // LANES, name=tag + "proj_dt")
    o, lse = _attn_fwd(qkv, tabs, tag + "attn_fwd")
    xbc = _conv_fwd(xbc_pre, sp["conv_w"], sp["conv_b"], tag + "conv_fwd")
    yn, y, hs = _ssd_fwd(xbc, z, dtp, sp["ssd"], tag + "ssd_fwd")
    t1 = _matmul(o, w_out, mode="nn", k_len=ATTN_WIDTH, tk=512, add=h, name=tag + "out_attn")
    h2 = _matmul(yn, w_out, mode="nn", k_len=SSM_INNER, tk=512, b_koff=1, add=t1, name=tag + "out_ssm")
    hn2 = _rmsnorm_fwd(h2, sp["norm_ffn"], tag + "norm_ffn")
    g, u, act = _swiglu_fwd(hn2, w_gate, w_up, tag + "ffn_up")
    h3 = _matmul(act, w_down, mode="nn", tk=1408, add=h2, name=tag + "ffn_down")
    saved = dict(h=h, hn=hn, qkv=qkv, z=z, xbc_pre=xbc_pre, dtp=dtp, o=o, lse=lse, xbc=xbc, yn=yn, y=y, hs=hs, h2=h2, hn2=hn2, g=g, u=u, act=act)
    return h3, saved


def _layer_bwd(dh3, s, big, sp, tabs, l):
    tag = f"l{l}_"
    w_in, w_out, w_gate, w_up, w_down = big
    dg, du = _swiglu_bwd(dh3, w_down, s["g"], s["u"], tag + "ffn_down_bwd")
    dw_down = _matmul(s["act"], dh3, mode="tn", tm=1408, tn=512, name=tag + "dw_down")
    dhn2 = _matmul(dg, w_gate, mode="nt", tk=1408, name=tag + "ffn_gate_bwd")
    dhn2 = _matmul(du, w_up, mode="nt", tk=1408, add=dhn2, name=tag + "ffn_up_bwd")
    dw_gate = _matmul(s["hn2"], dg, mode="tn", tm=512, tn=1408, name=tag + "dw_gate")
    dw_up = _matmul(s["hn2"], du, mode="tn", tm=512, tn=1408, name=tag + "dw_up")
    dh2, dnf = _rmsnorm_bwd(dhn2, s["h2"], sp["norm_ffn"], dh3, tag + "norm_ffn_bwd")
    d_o = _matmul(dh2, w_out, mode="nt", n_out=ATTN_WIDTH, tn=512, b_off=0, name=tag + "out_attn_bwd")
    dyn = _matmul(dh2, w_out, mode="nt", n_out=SSM_INNER, tn=512, b_off=1, name=tag + "out_ssm_bwd")
    dw_out = jnp.concatenate([_matmul(s["o"], dh2, mode="tn", tm=512, tn=512, name=tag + "dw_out_attn"),
                              _matmul(s["yn"], dh2, mode="tn", tm=512, tn=512, name=tag + "dw_out_ssm")], axis=0)
    dxbc, dz, ddtp, dnw, dds, dal, dbi = _ssd_bwd(s["xbc"], s["z"], s["dtp"], s["y"], s["hs"], dyn, sp["ssd"], tag + "ssd_bwd")
    dxbc_pre, dconv_w, dconv_b = _conv_bwd(s["xbc_pre"], sp["conv_w"], sp["conv_b"], dxbc, tag + "conv_bwd")
    dq, dk, dv = _attn_bwd(s["qkv"], tabs, s["o"], s["lse"], d_o, tag + "attn_bwd")
    dproj = jnp.concatenate([dq.astype(bf16), dk.astype(bf16), dv.astype(bf16), dz.astype(bf16), dxbc_pre.astype(bf16), ddtp.astype(bf16)], axis=1)
    dhn = _matmul(dproj, w_in, mode="nt", tk=1152, name=tag + "proj_bwd")
    dw_in = _matmul(s["hn"], dproj, mode="tn", tm=512, tn=1152, name=tag + "dw_in")
    dh, dnm = _rmsnorm_bwd(dhn, s["h"], sp["norm_mix"], dh2, tag + "norm_mix_bwd")
    grads = dict(
        norm_mix=dnm.sum(0), w_in=dw_in, conv_w=dconv_w, conv_b=dconv_b[0], dt_bias=dbi.sum(0)[:SSM_HEADS], a_log=dal.sum(0)[:SSM_HEADS],
        d_skip=dds.sum(0).reshape(SSM_HEADS, HEAD_DIM).sum(1), ssm_norm=dnw.sum(0), w_out=dw_out, norm_ffn=dnf.sum(0),
        w_gate=dw_gate, w_up=dw_up, w_down=dw_down)
    return dh, grads


def _local_step(x, positions, target, p, bigs):
    tabs = _rope_tables(positions.reshape(-1, 1), "rope_tables")
    h = x
    saved, sps = [], []
    for l in range(DEPTH):
        sps.append(_layer_params(p, l))
        h, s = _layer_fwd(h, bigs[l], sps[l], tabs, l)
        saved.append(s)
    dh, loss_parts, dfn = _final_loss(h, _row(p["final_norm"]), target, "final_loss")
    layer_grads = [None] * DEPTH
    for l in reversed(range(DEPTH)):
        dh, layer_grads[l] = _layer_bwd(dh, saved[l], bigs[l], sps[l], tabs, l)
    grads = {k: [layer_grads[l][k] for l in range(DEPTH)] for k in layer_grads[0]}
    grads["final_norm"] = dfn.sum(0)
    return jnp.sum(loss_parts), dh, grads


BIG = ("w_in", "w_out", "w_gate", "w_up", "w_down")
COL_SHARDED = ("w_in", "w_gate", "w_up")
SMALL = ("norm_mix", "conv_b", "dt_bias", "a_log", "d_skip", "ssm_norm", "norm_ffn", "final_norm")
WEIGHTS = ("norm_mix", "w_in", "conv_w", "conv_b", "dt_bias", "a_log", "d_skip", "ssm_norm", "w_out", "norm_ffn", "w_gate", "w_up", "w_down", "final_norm")
PACK_W = 1024
SMALL_ROWS = 88
CONVW_ROWS = 96
CONVW_SHARD_ROWS = 16


def _full_from_gathered(name, g):
    _, depth, a, b = g.shape
    if name in COL_SHARDED:
        return _cols_from_devices(g, IN_PROJ_PAD if name == "w_in" else N_DEV * b, "cols_" + name)
    return jnp.transpose(g, (1, 0, 2, 3)).reshape(depth, N_DEV * a, b)


def _by_device(name, per_layer, shard_shape):
    depth, a, b = shard_shape
    if name in COL_SHARDED:
        t = _devices_from_cols(per_layer, b, "devs_" + name)
    else:
        t = jnp.stack([q.reshape(N_DEV, a, b) for q in per_layer], axis=1)
    return t.reshape(N_CHIP, 2, depth * a, b)


def _pack_rows(parts, rows, width):
    flat = jnp.concatenate([q.reshape(-1) for q in parts])
    return jnp.pad(flat, (0, rows * width - flat.shape[0])).reshape(rows, width)


def _unpack(flat, like):
    out, off = [], 0
    for q in like:
        out.append(flat[off:off + q.size].reshape(q.shape))
        off += q.size
    return out


def kernel(x, positions, norm_mix, w_in, conv_w, conv_b, dt_bias, a_log, d_skip, ssm_norm, w_out, norm_ffn, w_gate, w_up, w_down, final_norm, loss_target, m_norm_mix, m_w_in, m_conv_w, m_conv_b, m_dt_bias, m_a_log, m_d_skip, m_ssm_norm, m_w_out, m_norm_ffn, m_w_gate, m_w_up, m_w_down, m_final_norm, v_norm_mix, v_w_in, v_conv_w, v_conv_b, v_dt_bias, v_a_log, v_d_skip, v_ssm_norm, v_w_out, v_norm_ffn, v_w_gate, v_w_up, v_w_down, v_final_norm):
    w = dict(norm_mix=norm_mix, w_in=w_in, conv_w=conv_w, conv_b=conv_b, dt_bias=dt_bias, a_log=a_log, d_skip=d_skip, ssm_norm=ssm_norm,
             w_out=w_out, norm_ffn=norm_ffn, w_gate=w_gate, w_up=w_up, w_down=w_down, final_norm=final_norm)
    m = dict(norm_mix=m_norm_mix, w_in=m_w_in, conv_w=m_conv_w, conv_b=m_conv_b, dt_bias=m_dt_bias, a_log=m_a_log, d_skip=m_d_skip,
             ssm_norm=m_ssm_norm, w_out=m_w_out, norm_ffn=m_norm_ffn, w_gate=m_w_gate, w_up=m_w_up, w_down=m_w_down, final_norm=m_final_norm)
    v = dict(norm_mix=v_norm_mix, w_in=v_w_in, conv_w=v_conv_w, conv_b=v_conv_b, dt_bias=v_dt_bias, a_log=v_a_log, d_skip=v_d_skip,
             ssm_norm=v_ssm_norm, w_out=v_w_out, norm_ffn=v_norm_ffn, w_gate=v_w_gate, w_up=v_w_up, w_down=v_w_down, final_norm=v_final_norm)
    ax, ay, ac = lax.axis_index("x"), lax.axis_index("y"), lax.axis_index("c")
    dev = 4 * ax + 2 * ay + ac

    gathered = _allgather_two_level([w[k].astype(bf16) for k in BIG] + [w["conv_w"]], "gather_weights")
    full = {k: _full_from_gathered(k, g) for k, g in zip(BIG, gathered[:len(BIG)])}
    p = {k: w[k] for k in SMALL}
    p["conv_w"] = jnp.transpose(gathered[-1], (1, 2, 0, 3)).reshape(DEPTH, CONV_WIDTH, CONV_CH)
    bigs = [tuple(full[k][l] for k in BIG) for l in range(DEPTH)]

    t = x.shape[0] * x.shape[1]
    loss_local, dx, grads = _local_step(x.reshape(t, D_MODEL), positions.reshape(t), loss_target.reshape(t, D_MODEL), p, bigs)

    core = ac.reshape(1).astype(jnp.int32)
    by_dev = [_by_device(k, grads[k], w[k].shape) for k in BIG]
    from_sibling = _exchange_sibling(by_dev, "scatter_sibling")
    chip_sums = [_add_kept(g, r, core, "scatter_add_" + k) for k, g, r in zip(BIG, by_dev, from_sibling)]
    from_chips = _exchange_chips(chip_sums, "scatter_chips")
    out_g, out_d, out_m, out_v = {}, {}, {}, {}
    for k, cs, fc in zip(BIG, chip_sums, from_chips):
        own = lax.dynamic_index_in_dim(cs, 2 * ax + ay, 0, keepdims=False)
        rows2d = (cs.shape[1], cs.shape[2])
        res = _adamw([(own, None), (fc, 0), (fc, 1), (fc, 2)], w[k].reshape(rows2d), m[k].reshape(rows2d), v[k].reshape(rows2d), "adamw_" + k)
        for dst, src in zip((out_g, out_d, out_m, out_v), res):
            dst[k] = src.reshape(w[k].shape)

    small_like = [w[k] for k in SMALL]
    small_grads = [jnp.stack(grads[k]) if k != "final_norm" else grads[k] for k in SMALL]
    small_pack = jnp.concatenate([_pack_rows(small_grads, SMALL_ROWS, LANES), _pack_rows([jnp.stack(grads["conv_w"])], CONVW_ROWS, LANES)], axis=0)
    parts = _allgather_direct(small_pack, "gather_small_grads")
    g_s, d_s, m_s, v_s = _adamw(
        [(parts[i, :SMALL_ROWS], None) for i in range(N_DEV)], _pack_rows(small_like, SMALL_ROWS, LANES),
        _pack_rows([m[k] for k in SMALL], SMALL_ROWS, LANES), _pack_rows([v[k] for k in SMALL], SMALL_ROWS, LANES), "adamw_replicated")
    for dst, src in ((out_g, g_s), (out_d, d_s), (out_m, m_s), (out_v, v_s)):
        dst.update(zip(SMALL, _unpack(src.reshape(-1), small_like)))
    shard_w = conv_w.shape[-1]
    conv_parts = parts[:, SMALL_ROWS:].reshape(N_DEV, DEPTH, CONV_WIDTH, CONV_CH)
    conv_mine = lax.dynamic_slice_in_dim(conv_parts, dev * shard_w, shard_w, axis=3)
    g_c, d_c, m_c, v_c = _adamw(
        [(_pack_rows([conv_mine[i]], CONVW_SHARD_ROWS, LANES), None) for i in range(N_DEV)], _pack_rows([conv_w], CONVW_SHARD_ROWS, LANES),
        _pack_rows([m["conv_w"]], CONVW_SHARD_ROWS, LANES), _pack_rows([v["conv_w"]], CONVW_SHARD_ROWS, LANES), "adamw_conv_w")
    for dst, src in ((out_g, g_c), (out_d, d_c), (out_m, m_c), (out_v, v_c)):
        dst["conv_w"] = src.reshape(-1)[:conv_w.size].reshape(conv_w.shape)

    loss = lax.psum(loss_local, ("x", "y", "c"))
    return (loss, dx.reshape(x.shape), *[out_g[k] for k in WEIGHTS], *[out_d[k] for k in WEIGHTS],
            *[out_m[k] for k in WEIGHTS], *[out_v[k] for k in WEIGHTS])
```

```python
import functools
import math

import jax
import jax.numpy as jnp
import numpy as np
from jax import lax
from jax.experimental import pallas as pl
from jax.experimental.pallas import tpu as pltpu

f32 = jnp.float32
bf16 = jnp.bfloat16

D_MODEL = 1024
SEQ = 2048
DEPTH = 2
HEAD_DIM = 64
N_ATTN_HEADS = 8
N_KV_HEADS = 2
ATTN_WIDTH = 512
KV_WIDTH = 128
ROPE_DIM = 16
ROPE_THETA = 500000.0
DILATIONS = (1, 4, 16)
ATTN_BLOCK = 128
SSM_HEADS = 16
SSM_INNER = 1024
SSM_GROUPS = 2
D_STATE = 128
CONV_WIDTH = 4
CHUNK = 128
CONV_CH = 1536
MIX_WIDTH = 1536
QKV_WIDTH = ATTN_WIDTH + 2 * KV_WIDTH
Z_OFF = 768
XBC_OFF = 1792
DT_OFF = 3328
IN_PROJ = 3344
IN_PROJ_PAD = 3456
FFN_HIDDEN = 2816
EPS = 1e-5
N_DEV = 8
ADAM_LR = 0.001
ADAM_B1 = 0.9
ADAM_B2 = 0.999
ADAM_EPS = 1e-08
ADAM_WD = 0.01
ADAM_STEP = 10

LANES = 128
SUBLANES = 8
VMEM_LIMIT = 56 * 1024 * 1024

MESH = pl.DeviceIdType.MESH
ANY = pl.BlockSpec(memory_space=pl.ANY)


def _cparams(sem, vmem=None):
    return pltpu.CompilerParams(dimension_semantics=sem, vmem_limit_bytes=vmem or VMEM_LIMIT)


def _sigmoid(x):
    return 1.0 / (1.0 + jnp.exp(-x))


def _silu(x):
    return x * _sigmoid(x)


def _dsilu(x):
    s = _sigmoid(x)
    return s * (1.0 + x * (1.0 - s))


def _softplus(x):
    return jnp.maximum(x, 0.0) + jnp.log(1.0 + jnp.exp(-jnp.abs(x)))


def _dot(a, b, dims, precision=None):
    return lax.dot_general(a, b, (dims, ((), ())), preferred_element_type=f32, precision=precision)


def _nn(a, b, precision=None):
    return _dot(a, b, ((1,), (0,)), precision)


def _nt(a, b):
    return _dot(a, b, ((1,), (1,)))


def _tn(a, b):
    return _dot(a, b, ((0,), (0,)))


def _rowsum8(t):
    n, w = t.shape
    return jnp.sum(t.reshape(n // SUBLANES, SUBLANES, w), axis=0)


def _matmul(a, b, *, mode, n_out=None, b_off=0, a_koff=0, b_koff=0, k_len=None, add=None, out_dtype=f32, tm=2048, tn=512, tk=1024, name):
    if mode == "tn":
        kdim_a, m = a.shape
    else:
        m, kdim_a = a.shape
    kk = k_len if k_len is not None else kdim_a
    n = n_out if n_out is not None else (b.shape[0] if mode == "nt" else b.shape[1])
    tm, tn, tk = min(tm, m), min(tn, n), min(tk, kk)
    assert m % tm == 0 and n % tn == 0 and kk % tk == 0, (name, m, n, kk, tm, tn, tk)
    nk = kk // tk
    if mode == "nn":
        a_spec = pl.BlockSpec((tm, tk), lambda i, j, k: (i, k + a_koff))
        b_spec = pl.BlockSpec((tk, tn), lambda i, j, k: (k + b_koff, j + b_off))
        dims = ((1,), (0,))
    elif mode == "nt":
        a_spec = pl.BlockSpec((tm, tk), lambda i, j, k: (i, k + a_koff))
        b_spec = pl.BlockSpec((tn, tk), lambda i, j, k: (j + b_off, k + b_koff))
        dims = ((1,), (1,))
    else:
        a_spec = pl.BlockSpec((tk, tm), lambda i, j, k: (k + a_koff, i))
        b_spec = pl.BlockSpec((tk, tn), lambda i, j, k: (k + b_koff, j + b_off))
        dims = ((0,), (0,))
    o_spec = pl.BlockSpec((tm, tn), lambda i, j, k: (i, j))
    has_add = add is not None

    def body(*refs):
        if has_add:
            a_ref, b_ref, add_ref, o_ref, acc_ref = refs
        else:
            a_ref, b_ref, o_ref, acc_ref = refs
        k = pl.program_id(2)
        part = _dot(a_ref[...].astype(bf16), b_ref[...].astype(bf16), dims)

        @pl.when(k == 0)
        def _():
            acc_ref[...] = part

        @pl.when(k > 0)
        def _():
            acc_ref[...] += part

        @pl.when(k == nk - 1)
        def _():
            r = acc_ref[...]
            if has_add:
                r = r + add_ref[...]
            o_ref[...] = r.astype(out_dtype)

    in_specs = [a_spec, b_spec] + ([o_spec] if has_add else [])
    args = (a, b) + ((add,) if has_add else ())
    return pl.pallas_call(
        body, name=name, grid=(m // tm, n // tn, nk), in_specs=in_specs, out_specs=o_spec,
        out_shape=jax.ShapeDtypeStruct((m, n), out_dtype), scratch_shapes=[pltpu.VMEM((tm, tn), f32)],
        compiler_params=_cparams(("parallel", "parallel", "arbitrary")),
    )(*args)


def _swiglu_fwd(hn, w_gate, w_up, name, tm=2048, tn=256):
    m, k = hn.shape
    n = w_gate.shape[1]

    def body(a_ref, wg_ref, wu_ref, g_ref, u_ref, act_ref):
        a = a_ref[...]
        g = _nn(a, wg_ref[...])
        u = _nn(a, wu_ref[...])
        g_ref[...] = g.astype(bf16)
        u_ref[...] = u.astype(bf16)
        act_ref[...] = (_silu(g) * u).astype(bf16)

    a_spec = pl.BlockSpec((tm, k), lambda i, j: (i, 0))
    w_spec = pl.BlockSpec((k, tn), lambda i, j: (0, j))
    o_spec = pl.BlockSpec((tm, tn), lambda i, j: (i, j))
    return pl.pallas_call(
        body, name=name, grid=(m // tm, n // tn), in_specs=[a_spec, w_spec, w_spec], out_specs=[o_spec, o_spec, o_spec],
        out_shape=[jax.ShapeDtypeStruct((m, n), bf16)] * 3,
        compiler_params=_cparams(("parallel", "parallel")),
    )(hn, w_gate, w_up)


def _swiglu_bwd(dh, w_down, g, u, name, tm=2048, tn=256):
    m, k = dh.shape
    n = w_down.shape[0]

    def body(a_ref, w_ref, g_ref, u_ref, dg_ref, du_ref):
        dact = _nt(a_ref[...].astype(bf16), w_ref[...])
        gg = g_ref[...].astype(f32)
        dg_ref[...] = (dact * u_ref[...].astype(f32) * _dsilu(gg)).astype(bf16)
        du_ref[...] = (dact * _silu(gg)).astype(bf16)

    a_spec = pl.BlockSpec((tm, k), lambda i, j: (i, 0))
    w_spec = pl.BlockSpec((tn, k), lambda i, j: (j, 0))
    o_spec = pl.BlockSpec((tm, tn), lambda i, j: (i, j))
    return pl.pallas_call(
        body, name=name, grid=(m // tm, n // tn), in_specs=[a_spec, w_spec, o_spec, o_spec], out_specs=[o_spec, o_spec],
        out_shape=[jax.ShapeDtypeStruct((m, n), bf16), jax.ShapeDtypeStruct((m, n), bf16)],
        compiler_params=_cparams(("parallel", "parallel")),
    )(dh, w_down, g, u)


def _rmsnorm_fwd(h, w, name, tm=512):
    m, d = h.shape

    def body(h_ref, w_ref, o_ref):
        x = h_ref[...]
        r = lax.rsqrt(jnp.mean(x * x, axis=-1, keepdims=True) + EPS)
        o_ref[...] = (x * r * w_ref[...]).astype(bf16)

    return pl.pallas_call(
        body, name=name, grid=(m // tm,),
        in_specs=[pl.BlockSpec((tm, d), lambda i: (i, 0)), pl.BlockSpec((1, d), lambda i: (0, 0))],
        out_specs=pl.BlockSpec((tm, d), lambda i: (i, 0)), out_shape=jax.ShapeDtypeStruct((m, d), bf16),
        compiler_params=_cparams(("parallel",)),
    )(h, w)


def _rmsnorm_bwd(dhn, h, w, dres, name, tm=512):
    m, d = h.shape

    def body(dhn_ref, h_ref, w_ref, dres_ref, dh_ref, dw_ref):
        x = h_ref[...]
        r = lax.rsqrt(jnp.mean(x * x, axis=-1, keepdims=True) + EPS)
        xhat = x * r
        dy = dhn_ref[...]
        gw = dy * w_ref[...]
        dh_ref[...] = dres_ref[...] + r * (gw - xhat * jnp.mean(gw * xhat, axis=-1, keepdims=True))
        part = _rowsum8(dy * xhat)

        @pl.when(pl.program_id(0) == 0)
        def _():
            dw_ref[...] = part

        @pl.when(pl.program_id(0) > 0)
        def _():
            dw_ref[...] += part

    row = pl.BlockSpec((tm, d), lambda i: (i, 0))
    return pl.pallas_call(
        body, name=name, grid=(m // tm,),
        in_specs=[row, row, pl.BlockSpec((1, d), lambda i: (0, 0)), row],
        out_specs=[row, pl.BlockSpec((SUBLANES, d), lambda i: (0, 0))],
        out_shape=[jax.ShapeDtypeStruct((m, d), f32), jax.ShapeDtypeStruct((SUBLANES, d), f32)],
        compiler_params=_cparams(("arbitrary",)),
    )(dhn, h, w, dres)


def _final_loss(h, w, target, name, tm=512):
    m, d = h.shape

    def body(h_ref, w_ref, t_ref, dh_ref, loss_ref, dw_ref):
        x = h_ref[...]
        r = lax.rsqrt(jnp.mean(x * x, axis=-1, keepdims=True) + EPS)
        xhat = x * r
        ww = w_ref[...]
        err = xhat * ww - t_ref[...]
        dy = err * (1.0 / d)
        gw = dy * ww
        dh_ref[...] = r * (gw - xhat * jnp.mean(gw * xhat, axis=-1, keepdims=True))
        lpart = _rowsum8(err * err) * (0.5 / d)
        wpart = _rowsum8(dy * xhat)

        @pl.when(pl.program_id(0) == 0)
        def _():
            loss_ref[...] = lpart
            dw_ref[...] = wpart

        @pl.when(pl.program_id(0) > 0)
        def _():
            loss_ref[...] += lpart
            dw_ref[...] += wpart

    row = pl.BlockSpec((tm, d), lambda i: (i, 0))
    acc = pl.BlockSpec((SUBLANES, d), lambda i: (0, 0))
    return pl.pallas_call(
        body, name=name, grid=(m // tm,),
        in_specs=[row, pl.BlockSpec((1, d), lambda i: (0, 0)), row], out_specs=[row, acc, acc],
        out_shape=[jax.ShapeDtypeStruct((m, d), f32), jax.ShapeDtypeStruct((SUBLANES, d), f32), jax.ShapeDtypeStruct((SUBLANES, d), f32)],
        compiler_params=_cparams(("arbitrary",)),
    )(h, w, target)


def _lane_tables():
    f = np.arange(LANES) % HEAD_DIM
    inv = ROPE_THETA ** (-jnp.arange(0, ROPE_DIM, 2, dtype=f32) / ROPE_DIM)
    invf = jnp.where(f < ROPE_DIM, inv[f % (ROPE_DIM // 2)], 0.0).astype(f32)
    return invf.reshape(1, LANES)


def _rope_tables(pos_col, name):
    t = pos_col.shape[0]
    tm = SEQ

    def body(p_ref, f_ref, c_ref, s1_ref, s2_ref):
        ang = p_ref[...].astype(f32) * f_ref[...]
        co, si = jnp.cos(ang), jnp.sin(ang)
        f = lax.broadcasted_iota(jnp.int32, (tm, LANES), 1) % HEAD_DIM
        c_ref[...] = jnp.where(f < ROPE_DIM, co, 1.0)
        s1_ref[...] = jnp.where(f < ROPE_DIM // 2, -si, 0.0)
        s2_ref[...] = jnp.where((f >= ROPE_DIM // 2) & (f < ROPE_DIM), si, 0.0)

    row = pl.BlockSpec((tm, LANES), lambda i: (i, 0))
    return pl.pallas_call(
        body, name=name, grid=(t // tm,),
        in_specs=[pl.BlockSpec((tm, 1), lambda i: (i, 0)), pl.BlockSpec((1, LANES), lambda i: (0, 0))],
        out_specs=[row, row, row], out_shape=[jax.ShapeDtypeStruct((t, LANES), f32)] * 3,
        compiler_params=_cparams(("parallel",)),
    )(pos_col, _lane_tables())


def _rot(x, c, s1, s2):
    return x * c + pltpu.roll(x, LANES - ROPE_DIM // 2, 1) * s1 + pltpu.roll(x, ROPE_DIM // 2, 1) * s2


def _rot_t(g, c, s1, s2):
    return g * c + pltpu.roll(g * s1, ROPE_DIM // 2, 1) + pltpu.roll(g * s2, LANES - ROPE_DIM // 2, 1)


def _dup_head(x, kvh, low):
    a = jnp.where(kvh == 0, x, pltpu.roll(x, HEAD_DIM, 1))
    return jnp.where(low, a, pltpu.roll(a, HEAD_DIM, 1))


def _deinterleave(src_ref, dst_ref, d, dtype):
    length = SEQ // d
    if d == 1:
        dst_ref[...] = src_ref[...].astype(dtype)
    else:
        for r in range(d):
            dst_ref[pl.ds(r * length, length), :] = src_ref[pl.ds(r, length, stride=d), :].astype(dtype)


def _interleave_store(src_ref, dst_ref, d, accumulate):
    length = SEQ // d
    if d == 1:
        if accumulate:
            dst_ref[...] += src_ref[...]
        else:
            dst_ref[...] = src_ref[...]
    else:
        for r in range(d):
            blk = src_ref[pl.ds(r * length, length), :]
            if accumulate:
                dst_ref[pl.ds(r, length, stride=d), :] = dst_ref[pl.ds(r, length, stride=d), :] + blk
            else:
                dst_ref[pl.ds(r, length, stride=d), :] = blk


def _attn_masks():
    qi = lax.broadcasted_iota(jnp.int32, (ATTN_BLOCK, ATTN_BLOCK), 0)
    ki = lax.broadcasted_iota(jnp.int32, (ATTN_BLOCK, ATTN_BLOCK), 1)
    low = lax.broadcasted_iota(jnp.int32, (ATTN_BLOCK, LANES), 1) < HEAD_DIM
    return ki <= qi, ki >= qi, low


NEG_INF = float("-inf")
ATTN_UNROLL = 2


def _attn_fwd(qkv, tabs, name):
    t = qkv.shape[0]
    nb = t // SEQ
    n_blk = SEQ // ATTN_BLOCK

    def body(q_ref, k_ref, v_ref, c_ref, s1_ref, s2_ref, o_ref, lse_ref,
             qr, kr, vr, qd, kd, vd, ob, lb, o0, o1, o2, l0, l1, l2):
        kvh = pl.program_id(1) // 2
        cur_ok, prev_ok, low = _attn_masks()
        lowfull = lax.broadcasted_iota(jnp.int32, (SEQ, LANES), 1) < HEAD_DIM
        c, s1, s2 = c_ref[...], s1_ref[...], s2_ref[...]
        qr[...] = _rot(q_ref[...], c, s1, s2) * (HEAD_DIM ** -0.5)
        kr[...] = _dup_head(_rot(k_ref[...], c, s1, s2), kvh, lowfull)
        vr[...] = _dup_head(v_ref[...], kvh, lowfull)
        onat, lnat = (o0, o1, o2), (l0, l1, l2)
        for bi, d in enumerate(DILATIONS):
            _deinterleave(qr, qd, d, bf16)
            _deinterleave(kr, kd, d, bf16)
            _deinterleave(vr, vd, d, bf16)
            per_res = n_blk // d
            use_prev = per_res > 1

            def block(n, carry):
                start = pl.multiple_of(n * ATTN_BLOCK, ATTN_BLOCK)
                has_prev = (n % per_res) != 0
                pstart = pl.multiple_of(jnp.maximum(n - 1, 0) * ATTN_BLOCK, ATTN_BLOCK)
                qb = qd[pl.ds(start, ATTN_BLOCK), :]
                kc, vc = kd[pl.ds(start, ATTN_BLOCK), :], vd[pl.ds(start, ATTN_BLOCK), :]
                if use_prev:
                    kp, vp = kd[pl.ds(pstart, ATTN_BLOCK), :], vd[pl.ds(pstart, ATTN_BLOCK), :]
                outs, lses = [], []
                for a in range(2):
                    qa = jnp.where(low if a == 0 else ~low, qb, jnp.zeros_like(qb))
                    sc = jnp.where(cur_ok, _nt(qa, kc), NEG_INF)
                    if use_prev:
                        sp = jnp.where(prev_ok & has_prev, _nt(qa, kp), NEG_INF)
                        m = jnp.max(jnp.maximum(sc, sp), axis=1, keepdims=True)
                        pc, pp = jnp.exp(sc - m), jnp.exp(sp - m)
                        den = jnp.sum(pc + pp, axis=1, keepdims=True)
                        acc = _nn(pc.astype(bf16), vc) + _nn(pp.astype(bf16), vp)
                    else:
                        m = jnp.max(sc, axis=1, keepdims=True)
                        pc = jnp.exp(sc - m)
                        den = jnp.sum(pc, axis=1, keepdims=True)
                        acc = _nn(pc.astype(bf16), vc)
                    outs.append(acc * (1.0 / den))
                    lses.append(m + jnp.log(den))
                ob[pl.ds(start, ATTN_BLOCK), :] = jnp.where(low, outs[0], outs[1])
                lb[pl.ds(start, ATTN_BLOCK), :] = jnp.where(low, lses[0], lses[1])
                return carry

            lax.fori_loop(0, n_blk, block, 0, unroll=ATTN_UNROLL)
            _interleave_store(ob, onat[bi], d, False)
            _interleave_store(lb, lnat[bi], d, False)
        la, lbb, lc = l0[...], l1[...], l2[...]
        lm = jnp.maximum(jnp.maximum(la, lbb), lc)
        wa, wb, wc = jnp.exp(la - lm), jnp.exp(lbb - lm), jnp.exp(lc - lm)
        ws = wa + wb + wc
        o_ref[...] = (wa * o0[...] + wb * o1[...] + wc * o2[...]) / ws
        lse_ref[...] = lm + jnp.log(ws)

    def col(jj):
        return pl.BlockSpec((SEQ, LANES), lambda b, j: (b, jj if jj is not None else j))

    tab = pl.BlockSpec((SEQ, LANES), lambda b, j: (b, 0))
    fs = pltpu.VMEM((SEQ, LANES), f32)
    hs = pltpu.VMEM((SEQ, LANES), bf16)
    return pl.pallas_call(
        body, name=name, grid=(nb, ATTN_WIDTH // LANES),
        in_specs=[col(None), col(ATTN_WIDTH // LANES), col(ATTN_WIDTH // LANES + 1), tab, tab, tab],
        out_specs=[col(None), col(None)],
        out_shape=[jax.ShapeDtypeStruct((t, ATTN_WIDTH), f32), jax.ShapeDtypeStruct((t, ATTN_WIDTH), f32)],
        scratch_shapes=[fs, fs, fs, hs, hs, hs, fs, fs, fs, fs, fs, fs, fs, fs],
        compiler_params=_cparams(("parallel", "parallel")),
    )(qkv, qkv, qkv, *tabs)


def _attn_bwd(qkv, tabs, o, lse, do, name):
    t = qkv.shape[0]
    nb = t // SEQ
    n_blk = SEQ // ATTN_BLOCK
    n_j = ATTN_WIDTH // LANES

    def body(q_ref, k_ref, v_ref, c_ref, s1_ref, s2_ref, o_ref, lse_ref, do_ref, dq_ref, dk_ref, dv_ref,
             qr, kr, vr, dl, qd, kd, vd, dod, lsd, dld, dqd, dkd, dvd, dqa, dka, dva):
        j = pl.program_id(1)
        kvh = j // 2
        cur_ok, prev_ok, low = _attn_masks()
        lowfull = lax.broadcasted_iota(jnp.int32, (SEQ, LANES), 1) < HEAD_DIM
        c, s1, s2 = c_ref[...], s1_ref[...], s2_ref[...]
        qr[...] = _rot(q_ref[...], c, s1, s2) * (HEAD_DIM ** -0.5)
        kr[...] = _dup_head(_rot(k_ref[...], c, s1, s2), kvh, lowfull)
        vr[...] = _dup_head(v_ref[...], kvh, lowfull)
        prod = do_ref[...] * o_ref[...]
        d_lo = jnp.sum(jnp.where(lowfull, prod, 0.0), axis=1, keepdims=True)
        d_hi = jnp.sum(jnp.where(lowfull, 0.0, prod), axis=1, keepdims=True)
        dl[...] = jnp.where(lowfull, d_lo, d_hi)
        dqa[...] = jnp.zeros_like(dqa)
        dka[...] = jnp.zeros_like(dka)
        dva[...] = jnp.zeros_like(dva)
        for d in DILATIONS:
            _deinterleave(qr, qd, d, bf16)
            _deinterleave(kr, kd, d, bf16)
            _deinterleave(vr, vd, d, bf16)
            _deinterleave(do_ref, dod, d, bf16)
            _deinterleave(lse_ref, lsd, d, f32)
            _deinterleave(dl, dld, d, f32)
            dkd[...] = jnp.zeros_like(dkd)
            dvd[...] = jnp.zeros_like(dvd)
            per_res = n_blk // d
            use_prev = per_res > 1

            def block(n, carry):
                start = pl.multiple_of(n * ATTN_BLOCK, ATTN_BLOCK)
                has_prev = (n % per_res) != 0
                pstart = pl.multiple_of(jnp.maximum(n - 1, 0) * ATTN_BLOCK, ATTN_BLOCK)
                cur, prev = pl.ds(start, ATTN_BLOCK), pl.ds(pstart, ATTN_BLOCK)
                qb, dob = qd[cur, :], dod[cur, :]
                kc, vc = kd[cur, :], vd[cur, :]
                if use_prev:
                    kp, vp = kd[prev, :], vd[prev, :]
                lsb, dlb = lsd[cur, :], dld[cur, :]
                dqs, qas, doas, dscs, dsps, pcs, pps = [], [], [], [], [], [], []
                for a in range(2):
                    sel = low if a == 0 else ~low
                    qa = jnp.where(sel, qb, jnp.zeros_like(qb))
                    doa = jnp.where(sel, dob, jnp.zeros_like(dob))
                    ls = lsb[:, a * HEAD_DIM:a * HEAD_DIM + 1]
                    de = dlb[:, a * HEAD_DIM:a * HEAD_DIM + 1]
                    pc = jnp.exp(jnp.where(cur_ok, _nt(qa, kc), NEG_INF) - ls)
                    dsc = (pc * (_nt(doa, vc) - de)).astype(bf16)
                    dq = _nn(dsc, kc)
                    if use_prev:
                        pp = jnp.exp(jnp.where(prev_ok & has_prev, _nt(qa, kp), NEG_INF) - ls)
                        dsp = (pp * (_nt(doa, vp) - de)).astype(bf16)
                        dq = dq + _nn(dsp, kp)
                        dsps.append(dsp)
                        pps.append(pp.astype(bf16))
                    dqs.append(dq)
                    qas.append(qa)
                    doas.append(doa)
                    dscs.append(dsc)
                    pcs.append(pc.astype(bf16))
                q2, do2 = jnp.concatenate(qas, axis=0), jnp.concatenate(doas, axis=0)
                dqd[cur, :] = jnp.where(low, dqs[0], dqs[1])
                dkd[cur, :] += _tn(jnp.concatenate(dscs, axis=0), q2)
                dvd[cur, :] += _tn(jnp.concatenate(pcs, axis=0), do2)
                if use_prev:
                    dkd[prev, :] += _tn(jnp.concatenate(dsps, axis=0), q2)
                    dvd[prev, :] += _tn(jnp.concatenate(pps, axis=0), do2)
                return carry

            lax.fori_loop(0, n_blk, block, 0, unroll=ATTN_UNROLL)
            _interleave_store(dqd, dqa, d, True)
            _interleave_store(dkd, dka, d, True)
            _interleave_store(dvd, dva, d, True)
        dq_ref[...] = _rot_t(dqa[...] * (HEAD_DIM ** -0.5), c, s1, s2)
        dkf = dka[...]
        dkf = _rot_t(dkf + pltpu.roll(dkf, HEAD_DIM, 1), c, s1, s2)
        dvf = dva[...]
        dvf = dvf + pltpu.roll(dvf, HEAD_DIM, 1)
        mine = (lax.broadcasted_iota(jnp.int32, (SEQ, LANES), 1) // HEAD_DIM) == kvh
        dkc_, dvc_ = jnp.where(mine, dkf, 0.0), jnp.where(mine, dvf, 0.0)

        @pl.when(j == 0)
        def _():
            dk_ref[...] = dkc_
            dv_ref[...] = dvc_

        @pl.when(j > 0)
        def _():
            dk_ref[...] += dkc_
            dv_ref[...] += dvc_

    def col(jj):
        return pl.BlockSpec((SEQ, LANES), lambda b, j: (b, jj if jj is not None else j))

    tab = pl.BlockSpec((SEQ, LANES), lambda b, j: (b, 0))
    fs = pltpu.VMEM((SEQ, LANES), f32)
    hs = pltpu.VMEM((SEQ, LANES), bf16)
    return pl.pallas_call(
        body, name=name, grid=(nb, n_j),
        in_specs=[col(None), col(n_j), col(n_j + 1), tab, tab, tab, col(None), col(None), col(None)],
        out_specs=[col(None), tab, tab],
        out_shape=[jax.ShapeDtypeStruct((t, ATTN_WIDTH), f32), jax.ShapeDtypeStruct((t, LANES), f32), jax.ShapeDtypeStruct((t, LANES), f32)],
        scratch_shapes=[fs, fs, fs, fs, hs, hs, hs, hs, fs, fs, fs, fs, fs, fs, fs, fs],
        compiler_params=_cparams(("parallel", "arbitrary")),
    )(qkv, qkv, qkv, *tabs, o, lse, do)


def _conv_pre(x, w_ref, b_ref, row):
    shifted = [x] + [jnp.where(row >= s, pltpu.roll(x, s, 0), 0.0) for s in range(1, CONV_WIDTH)]
    pre = b_ref[...] + w_ref[CONV_WIDTH - 1:CONV_WIDTH, :] * x
    for s in range(1, CONV_WIDTH):
        pre = pre + w_ref[CONV_WIDTH - 1 - s:CONV_WIDTH - s, :] * shifted[s]
    return pre, shifted


def _conv_fwd(x, w, b, name, tc=512):
    t, ch = x.shape

    def body(x_ref, w_ref, b_ref, o_ref):
        row = lax.broadcasted_iota(jnp.int32, (SEQ, tc), 0)
        pre, _ = _conv_pre(x_ref[...], w_ref, b_ref, row)
        o_ref[...] = _silu(pre)

    xs = pl.BlockSpec((SEQ, tc), lambda i, j: (i, j))
    return pl.pallas_call(
        body, name=name, grid=(t // SEQ, ch // tc),
        in_specs=[xs, pl.BlockSpec((CONV_WIDTH, tc), lambda i, j: (0, j)), pl.BlockSpec((1, tc), lambda i, j: (0, j))],
        out_specs=xs, out_shape=jax.ShapeDtypeStruct((t, ch), f32),
        compiler_params=_cparams(("parallel", "parallel")),
    )(x, w, b)


def _conv_bwd(x, w, b, dact, name, tc=512):
    t, ch = x.shape

    def body(x_ref, w_ref, b_ref, d_ref, dx_ref, dw_ref, db_ref):
        row = lax.broadcasted_iota(jnp.int32, (SEQ, tc), 0)
        pre, shifted = _conv_pre(x_ref[...], w_ref, b_ref, row)
        dpre = d_ref[...] * _dsilu(pre)
        dx = w_ref[CONV_WIDTH - 1:CONV_WIDTH, :] * dpre
        for s in range(1, CONV_WIDTH):
            dx = dx + w_ref[CONV_WIDTH - 1 - s:CONV_WIDTH - s, :] * jnp.where(row < SEQ - s, pltpu.roll(dpre, SEQ - s, 0), 0.0)
        dx_ref[...] = dx
        first = pl.program_id(1) == 0
        parts = [jnp.sum(dpre * shifted[CONV_WIDTH - 1 - k], axis=0, keepdims=True) for k in range(CONV_WIDTH)]
        dbp = jnp.sum(dpre, axis=0, keepdims=True)

        @pl.when(first)
        def _():
            for k in range(CONV_WIDTH):
                dw_ref[k:k + 1, :] = parts[k]
            db_ref[...] = dbp

        @pl.when(jnp.logical_not(first))
        def _():
            for k in range(CONV_WIDTH):
                dw_ref[k:k + 1, :] += parts[k]
            db_ref[...] += dbp

    xs = pl.BlockSpec((SEQ, tc), lambda j, i: (i, j))
    ws = pl.BlockSpec((CONV_WIDTH, tc), lambda j, i: (0, j))
    bs = pl.BlockSpec((1, tc), lambda j, i: (0, j))
    return pl.pallas_call(
        body, name=name, grid=(ch // tc, t // SEQ),
        in_specs=[xs, ws, bs, xs], out_specs=[xs, ws, bs],
        out_shape=[jax.ShapeDtypeStruct((t, ch), f32), jax.ShapeDtypeStruct((CONV_WIDTH, ch), f32), jax.ShapeDtypeStruct((1, ch), f32)],
        compiler_params=_cparams(("parallel", "arbitrary")),
    )(x, w, b, dact)


GROUP_W = SSM_INNER // SSM_GROUPS
HEADS_PER_GROUP = SSM_HEADS // SSM_GROUPS
HI = lax.Precision.HIGHEST


def _ssd_common(xbc_ref, dt_ref, bias_ref, alog_ref):
    r = lax.broadcasted_iota(jnp.int32, (CHUNK, CHUNK), 0)
    cidx = lax.broadcasted_iota(jnp.int32, (CHUNK, CHUNK), 1)
    causal = r >= cidx
    tril = causal.astype(f32)
    expand = (lax.broadcasted_iota(jnp.int32, (CHUNK, SSM_INNER), 0)
              == lax.broadcasted_iota(jnp.int32, (CHUNK, SSM_INNER), 1) // HEAD_DIM).astype(f32)
    head_lane = cidx < SSM_HEADS
    dtp = dt_ref[...] + bias_ref[...]
    dt = jnp.where(head_lane, _softplus(dtp), 0.0)
    a_neg = -jnp.exp(alog_ref[...])
    a = dt * a_neg
    cs = _nn(tril, a, HI)
    dt_e = _nn(dt, expand, HI)
    cs_e = _nn(cs, expand, HI)
    xs = xbc_ref[:, 0:SSM_INNER]
    xg = xs * dt_e
    ecs = jnp.exp(cs_e)
    cs_last = cs_e[CHUNK - 1:CHUNK, :]
    dse = jnp.exp(cs_last - cs_e)
    cde = jnp.exp(cs_last)
    return dict(r=r, cidx=cidx, causal=causal, tril=tril, expand=expand, head_lane=head_lane, dtp=dtp, dt=dt, a_neg=a_neg,
                cs=cs, cst=cs.T, dt_e=dt_e, cs_e=cs_e, xs=xs, xg=xg, ecs=ecs, dse=dse, cde=cde)


def _decay_mat(q, h):
    return jnp.exp(jnp.where(q["causal"], q["cs"][:, h:h + 1] - q["cst"][h:h + 1, :], NEG_INF))


def _gate_norm(y, z, nw):
    y2 = y * _silu(z)
    outs, xhats, rs = [], [], []
    for g in range(SSM_GROUPS):
        sl = slice(g * GROUP_W, (g + 1) * GROUP_W)
        yg = y2[:, sl]
        r = lax.rsqrt(jnp.mean(yg * yg, axis=-1, keepdims=True) + EPS)
        xhats.append(yg * r)
        rs.append(r)
        outs.append(yg * r * nw[:, sl])
    return y2, outs, xhats, rs


def _ssd_fwd(xbc, z, dtp, params, name):
    t = xbc.shape[0]
    n_chunk = SEQ // CHUNK
    low = None

    def body(xbc_ref, z_ref, dt_ref, bias_ref, alog_ref, dskip_ref, nw_ref, yn_ref, y_ref, hs_ref, h_scr):
        @pl.when(pl.program_id(1) == 0)
        def _():
            h_scr[...] = jnp.zeros_like(h_scr)

        q = _ssd_common(xbc_ref, dt_ref, bias_ref, alog_ref)
        low = lax.broadcasted_iota(jnp.int32, (CHUNK, LANES), 1) < HEAD_DIM
        xgb = q["xg"].astype(bf16)
        wst = (q["xg"] * q["dse"]).astype(bf16)
        hs_ref[0] = h_scr[...]
        ys = []
        for g in range(SSM_GROUPS):
            gl = slice(g * GROUP_W, (g + 1) * GROUP_W)
            bg = xbc_ref[:, SSM_INNER + g * D_STATE:SSM_INNER + (g + 1) * D_STATE].astype(bf16)
            cg = xbc_ref[:, SSM_INNER + SSM_GROUPS * D_STATE + g * D_STATE:SSM_INNER + SSM_GROUPS * D_STATE + (g + 1) * D_STATE].astype(bf16)
            cb = _nt(cg, bg)
            hg = h_scr[g]
            yoff = _nn(cg, hg.astype(bf16)) * q["ecs"][:, gl]
            pieces = []
            for i in range(HEADS_PER_GROUP // 2):
                h0 = g * HEADS_PER_GROUP + 2 * i
                xp = xgb[:, h0 * HEAD_DIM:(h0 + 2) * HEAD_DIM]
                m0 = (cb * _decay_mat(q, h0)).astype(bf16)
                m1 = (cb * _decay_mat(q, h0 + 1)).astype(bf16)
                zero = jnp.zeros_like(xp)
                pieces.append(_nn(m0, jnp.where(low, xp, zero)) + _nn(m1, jnp.where(low, zero, xp)))
            ys.append(jnp.concatenate(pieces, axis=1) + yoff + dskip_ref[:, gl] * q["xs"][:, gl])
            h_scr[g] = hg * q["cde"][:, gl] + _tn(bg, wst[:, gl])
        y = jnp.concatenate(ys, axis=1)
        y_ref[...] = y
        _, outs, _, _ = _gate_norm(y, z_ref[...], nw_ref[...])
        yn_ref[...] = jnp.concatenate(outs, axis=1).astype(bf16)

    def rows(w):
        return pl.BlockSpec((CHUNK, w), lambda b, c: (b * n_chunk + c, 0))

    def par(w):
        return pl.BlockSpec((1, w), lambda b, c: (0, 0))

    return pl.pallas_call(
        body, name=name, grid=(t // SEQ, n_chunk),
        in_specs=[rows(CONV_CH), rows(SSM_INNER), rows(LANES), par(LANES), par(LANES), par(SSM_INNER), par(SSM_INNER)],
        out_specs=[rows(SSM_INNER), rows(SSM_INNER), pl.BlockSpec((1, SSM_GROUPS, D_STATE, GROUP_W), lambda b, c: (b * n_chunk + c, 0, 0, 0))],
        out_shape=[jax.ShapeDtypeStruct((t, SSM_INNER), bf16), jax.ShapeDtypeStruct((t, SSM_INNER), f32),
                   jax.ShapeDtypeStruct((t // CHUNK, SSM_GROUPS, D_STATE, GROUP_W), f32)],
        scratch_shapes=[pltpu.VMEM((SSM_GROUPS, D_STATE, GROUP_W), f32)],
        compiler_params=_cparams(("parallel", "arbitrary")),
    )(xbc, z, dtp, *params)


def _ssd_bwd(xbc, z, dtp, y, hs, dyn, params, name):
    t = xbc.shape[0]
    n_chunk = SEQ // CHUNK

    def body(xbc_ref, z_ref, dt_ref, y_ref, hs_ref, dyn_ref, bias_ref, alog_ref, dskip_ref, nw_ref,
             dxbc_ref, dz_ref, ddt_ref, dnw_ref, dds_ref, dal_ref, dbi_ref, dh_scr):
        @pl.when(pl.program_id(1) == 0)
        def _():
            dh_scr[...] = jnp.zeros_like(dh_scr)

        q = _ssd_common(xbc_ref, dt_ref, bias_ref, alog_ref)
        low = lax.broadcasted_iota(jnp.int32, (CHUNK, LANES), 1) < HEAD_DIM
        last_row = lax.broadcasted_iota(jnp.int32, (CHUNK, GROUP_W), 0) == CHUNK - 1
        xs, xg = q["xs"], q["xg"]
        xgb = xg.astype(bf16)
        wf = xg * q["dse"]
        wst = wf.astype(bf16)
        zz = z_ref[...]
        yy = y_ref[...]
        sz = _silu(zz)
        y2, _, xhats, rs = _gate_norm(yy, zz, nw_ref[...])
        dyn_ = dyn_ref[...]
        dy2s, dnws = [], []
        for g in range(SSM_GROUPS):
            gl = slice(g * GROUP_W, (g + 1) * GROUP_W)
            gw = dyn_[:, gl] * nw_ref[:, gl]
            dy2s.append(rs[g] * (gw - xhats[g] * jnp.mean(gw * xhats[g], axis=-1, keepdims=True)))
            dnws.append(_rowsum8(dyn_[:, gl] * xhats[g]))
        dy2 = jnp.concatenate(dy2s, axis=1)
        dy = dy2 * sz
        dz_ref[...] = dy2 * yy * _dsilu(zz)
        dnw_p = jnp.concatenate(dnws, axis=1)
        dds_p = _rowsum8(dy * xs)
        dyb = dy.astype(bf16)
        gfull = (dy * q["ecs"]).astype(bf16)
        dcs_c = jnp.zeros((CHUNK, CHUNK), f32)
        dcs_r = jnp.zeros((CHUNK, CHUNK), f32)
        dcs_e_parts, dxg_parts = [], []
        for g in range(SSM_GROUPS):
            gl = slice(g * GROUP_W, (g + 1) * GROUP_W)
            bsl = slice(SSM_INNER + g * D_STATE, SSM_INNER + (g + 1) * D_STATE)
            csl = slice(SSM_INNER + SSM_GROUPS * D_STATE + g * D_STATE, SSM_INNER + SSM_GROUPS * D_STATE + (g + 1) * D_STATE)
            bg = xbc_ref[:, bsl].astype(bf16)
            cg = xbc_ref[:, csl].astype(bf16)
            cb = _nt(cg, bg)
            hg = hs_ref[0, g]
            hgb = hg.astype(bf16)
            dhn = dh_scr[g]
            dhnb = dhn.astype(bf16)
            yoff = _nn(cg, hgb) * q["ecs"][:, gl]
            dw_ = _nn(bg, dhnb)
            r_e = dw_ * wf[:, gl]
            to_last = jnp.sum(r_e, axis=0, keepdims=True) + jnp.sum(dhn * hg, axis=0, keepdims=True) * q["cde"][:, gl]
            dcs_e_parts.append(dy[:, gl] * yoff - r_e + jnp.where(last_row, to_last, 0.0))
            dcb = jnp.zeros((CHUNK, CHUNK), f32)
            dxg_pairs = []
            for i in range(HEADS_PER_GROUP // 2):
                h0 = g * HEADS_PER_GROUP + 2 * i
                psl = slice(h0 * HEAD_DIM, (h0 + 2) * HEAD_DIM)
                xp = xgb[:, psl]
                dyp = dyb[:, psl]
                zero = jnp.zeros_like(dyp)
                tns = []
                for a in range(2):
                    h = h0 + a
                    lm = _decay_mat(q, h)
                    m = cb * lm
                    dm = _nt(jnp.where(low, dyp, zero) if a == 0 else jnp.where(low, zero, dyp), xp)
                    dcb = dcb + dm * lm
                    nmat = dm * m
                    dcs_c = dcs_c + jnp.where(q["cidx"] == h, jnp.sum(nmat, axis=1, keepdims=True), 0.0)
                    dcs_r = dcs_r + jnp.where(q["r"] == h, jnp.sum(nmat, axis=0, keepdims=True), 0.0)
                    tns.append(_tn(m.astype(bf16), dyp))
                dxg_pairs.append(jnp.where(low, tns[0], tns[1]))
            dxg_parts.append(jnp.concatenate(dxg_pairs, axis=1) + dw_ * q["dse"][:, gl])
            dcbb = dcb.astype(bf16)
            dxbc_ref[:, csl] = _nt(gfull[:, gl], hgb) + _nn(dcbb, bg)
            dxbc_ref[:, bsl] = _nt(wst[:, gl], dhnb) + _tn(dcbb, cg)
            dh_scr[g] = dhn * q["cde"][:, gl] + _tn(cg, gfull[:, gl])
        dxg = jnp.concatenate(dxg_parts, axis=1)
        dcs_e = jnp.concatenate(dcs_e_parts, axis=1)
        dxbc_ref[:, 0:SSM_INNER] = dskip_ref[...] * dy + dxg * q["dt_e"]
        dcs = dcs_c - dcs_r.T + _dot(dcs_e, q["expand"], ((1,), (1,)), HI)
        triu = (q["cidx"] >= q["r"]).astype(f32)
        da = _nn(triu, dcs, HI)
        ddt = _dot(dxg * xs, q["expand"], ((1,), (1,)), HI) + da * q["a_neg"]
        ddtp = jnp.where(q["head_lane"], ddt * _sigmoid(q["dtp"]), 0.0)
        ddt_ref[...] = ddtp
        dal_p = _rowsum8(da * q["dt"]) * q["a_neg"]
        dbi_p = _rowsum8(ddtp)
        first = (pl.program_id(0) == 0) & (pl.program_id(1) == 0)

        @pl.when(first)
        def _():
            dnw_ref[...] = dnw_p
            dds_ref[...] = dds_p
            dal_ref[...] = dal_p
            dbi_ref[...] = dbi_p

        @pl.when(jnp.logical_not(first))
        def _():
            dnw_ref[...] += dnw_p
            dds_ref[...] += dds_p
            dal_ref[...] += dal_p
            dbi_ref[...] += dbi_p

    def rows(w):
        return pl.BlockSpec((CHUNK, w), lambda b, c: (b * n_chunk + n_chunk - 1 - c, 0))

    def par(w):
        return pl.BlockSpec((1, w), lambda b, c: (0, 0))

    def acc(w):
        return pl.BlockSpec((SUBLANES, w), lambda b, c: (0, 0))

    return pl.pallas_call(
        body, name=name, grid=(t // SEQ, n_chunk),
        in_specs=[rows(CONV_CH), rows(SSM_INNER), rows(LANES), rows(SSM_INNER),
                  pl.BlockSpec((1, SSM_GROUPS, D_STATE, GROUP_W), lambda b, c: (b * n_chunk + n_chunk - 1 - c, 0, 0, 0)),
                  rows(SSM_INNER), par(LANES), par(LANES), par(SSM_INNER), par(SSM_INNER)],
        out_specs=[rows(CONV_CH), rows(SSM_INNER), rows(LANES), acc(SSM_INNER), acc(SSM_INNER), acc(LANES), acc(LANES)],
        out_shape=[jax.ShapeDtypeStruct((t, CONV_CH), f32), jax.ShapeDtypeStruct((t, SSM_INNER), f32), jax.ShapeDtypeStruct((t, LANES), f32),
                   jax.ShapeDtypeStruct((SUBLANES, SSM_INNER), f32), jax.ShapeDtypeStruct((SUBLANES, SSM_INNER), f32),
                   jax.ShapeDtypeStruct((SUBLANES, LANES), f32), jax.ShapeDtypeStruct((SUBLANES, LANES), f32)],
        scratch_shapes=[pltpu.VMEM((SSM_GROUPS, D_STATE, GROUP_W), f32)],
        compiler_params=_cparams(("arbitrary", "arbitrary")),
    )(xbc, z, dtp, y, hs, dyn, *params)


def _adamw(g_parts, w, m, v, name, tr=None):
    rows, width = w.shape
    n = len(g_parts)
    if tr is None:
        tr = _row_tile(rows)

    def body(*refs):
        g_refs, (w_ref, m_ref, v_ref, g_out, d_out, m_out, v_out) = refs[:n], refs[n:]

        def part(i):
            return (g_refs[i][...] if g_parts[i][1] is None else g_refs[i][0]).astype(f32)

        g = part(0)
        for i in range(1, n):
            g = g + part(i)
        mm = ADAM_B1 * m_ref[...] + (1.0 - ADAM_B1) * g
        vv = ADAM_B2 * v_ref[...] + (1.0 - ADAM_B2) * (g * g)
        m_hat = mm / (1.0 - ADAM_B1 ** ADAM_STEP)
        v_hat = vv / (1.0 - ADAM_B2 ** ADAM_STEP)
        g_out[...] = g
        d_out[...] = -ADAM_LR * (m_hat / (jnp.sqrt(v_hat) + ADAM_EPS) + ADAM_WD * w_ref[...])
        m_out[...] = mm
        v_out[...] = vv

    spec = pl.BlockSpec((tr, width), lambda i: (i, 0))

    def gspec(idx):
        return spec if idx is None else pl.BlockSpec((1, tr, width), lambda i: (idx, i, 0))

    return pl.pallas_call(
        body, name=name, grid=(rows // tr,), in_specs=[gspec(idx) for _, idx in g_parts] + [spec] * 3, out_specs=[spec] * 4,
        out_shape=[jax.ShapeDtypeStruct((rows, width), f32)] * 4, compiler_params=_cparams(("parallel",)),
    )(*[a for a, _ in g_parts], w, m, v)


def _row_tile(rows, cap=512):
    for cand in range(min(rows, cap) // SUBLANES * SUBLANES, 0, -SUBLANES):
        if rows % cand == 0:
            return cand
    return rows


def _cols_from_devices(g, width, name):
    n_dev, depth, a, b = g.shape

    def body(g_ref, o_ref):
        for i in range(n_dev):
            o_ref[0, :, i * b:(i + 1) * b] = g_ref[i, 0]
        if width > n_dev * b:
            o_ref[0, :, n_dev * b:width] = jnp.zeros((a, width - n_dev * b), o_ref.dtype)

    return pl.pallas_call(
        body, name=name, grid=(depth,), in_specs=[pl.BlockSpec((n_dev, 1, a, b), lambda l: (0, l, 0, 0))],
        out_specs=pl.BlockSpec((1, a, width), lambda l: (l, 0, 0)), out_shape=jax.ShapeDtypeStruct((depth, a, width), g.dtype),
        compiler_params=_cparams(("parallel",)),
    )(g)


def _devices_from_cols(per_layer, b, name, tr=256):
    depth = len(per_layer)
    a, width = per_layer[0].shape

    def body(*refs):
        o_ref = refs[depth]
        for l in range(depth):
            for i in range(N_DEV):
                o_ref[i, l] = refs[l][:, i * b:(i + 1) * b]

    return pl.pallas_call(
        body, name=name, grid=(a // tr,), in_specs=[pl.BlockSpec((tr, width), lambda r: (r, 0))] * depth,
        out_specs=pl.BlockSpec((N_DEV, depth, tr, b), lambda r: (0, 0, r, 0)),
        out_shape=jax.ShapeDtypeStruct((N_DEV, depth, a, b), per_layer[0].dtype), compiler_params=_cparams(("parallel",)),
    )(*per_layer)


def _add_kept(g, recv, core, name, out_dtype=bf16):
    nblk, _, rows, width = g.shape
    tr = _row_tile(rows)

    def body(c_ref, g_ref, r_ref, o_ref):
        o_ref[0] = (g_ref[0, 0] + r_ref[0]).astype(out_dtype)

    grid_spec = pltpu.PrefetchScalarGridSpec(
        num_scalar_prefetch=1, grid=(nblk, rows // tr),
        in_specs=[pl.BlockSpec((1, 1, tr, width), lambda i, j, c: (i, c[0], j, 0)), pl.BlockSpec((1, tr, width), lambda i, j, c: (i, j, 0))],
        out_specs=pl.BlockSpec((1, tr, width), lambda i, j, c: (i, j, 0)))
    return pl.pallas_call(
        body, name=name, grid_spec=grid_spec, out_shape=jax.ShapeDtypeStruct((nblk, rows, width), out_dtype),
        compiler_params=_cparams(("parallel", "parallel")),
    )(core, g, recv)


def _me():
    return lax.axis_index("x"), lax.axis_index("y"), lax.axis_index("c")


def _allgather_two_level(shards, name):
    n = len(shards)
    per = 7

    def body(*refs):
        ins, outs = refs[:n], refs[n:2 * n]
        send_sems, recv_sems, local_sems = refs[2 * n:]
        x, y, c = _me()
        me, sibling = (x, y, c), (x, y, 1 - c)
        chips = [(1 - x, y), (x, 1 - y), (1 - x, 1 - y)]

        def slot(a, p):
            return outs[a].at[4 * p[0] + 2 * p[1] + p[2]]

        def copy(a, k, block, to, src=None):
            return pltpu.make_async_remote_copy(
                src_ref=slot(a, block) if src is None else src, dst_ref=slot(a, block),
                send_sem=send_sems.at[a * per + k], recv_sem=recv_sems.at[a * per + k], device_id=to, device_id_type=MESH)

        mine = [pltpu.make_async_copy(ins[a], slot(a, me), local_sems.at[a]) for a in range(n)]
        for cp in mine:
            cp.start()
        first = []
        for a in range(n):
            first.append(copy(a, 0, me, sibling, src=ins[a]))
            first += [copy(a, 1 + j, me, (*chip, c), src=ins[a]) for j, chip in enumerate(chips)]
        for cp in first:
            cp.start()
        passed = []
        for j, chip in enumerate(chips):
            for a in range(n):
                copy(a, 1 + j, (*chip, c), me).wait_recv()
                fwd = copy(a, 4 + j, (*chip, c), sibling)
                fwd.start()
                passed.append(fwd)
        for a in range(n):
            copy(a, 0, sibling, me).wait_recv()
            for j, chip in enumerate(chips):
                copy(a, 4 + j, (*chip, 1 - c), me).wait_recv()
        for cp in first + passed:
            cp.wait_send()
        for cp in mine:
            cp.wait()

    return pl.pallas_call(
        body, name=name, in_specs=[ANY] * n, out_specs=[ANY] * n,
        out_shape=[jax.ShapeDtypeStruct((N_DEV,) + s.shape, s.dtype) for s in shards],
        scratch_shapes=[pltpu.SemaphoreType.DMA((n * per,)), pltpu.SemaphoreType.DMA((n * per,)), pltpu.SemaphoreType.DMA((n,))],
    )(*shards)


def _allgather_direct(row, name):
    def body(in_ref, out_ref, send_sems, recv_sems, local_sem):
        x, y, c = _me()
        mine = out_ref.at[4 * x + 2 * y + c]
        local = pltpu.make_async_copy(in_ref, mine, local_sem)
        local.start()
        sends = []
        for k in range(1, N_DEV):
            px, py, pc = x ^ (k >> 2), y ^ ((k >> 1) & 1), c ^ (k & 1)
            sends.append(pltpu.make_async_remote_copy(
                src_ref=in_ref, dst_ref=mine, send_sem=send_sems.at[k - 1], recv_sem=recv_sems.at[k - 1],
                device_id=(px, py, pc), device_id_type=MESH))
        for cp in sends:
            cp.start()
        for k in range(1, N_DEV):
            px, py, pc = x ^ (k >> 2), y ^ ((k >> 1) & 1), c ^ (k & 1)
            theirs = out_ref.at[4 * px + 2 * py + pc]
            pltpu.make_async_remote_copy(
                src_ref=in_ref, dst_ref=theirs, send_sem=send_sems.at[k - 1], recv_sem=recv_sems.at[k - 1],
                device_id=(px, py, pc), device_id_type=MESH).wait_recv()
        for cp in sends:
            cp.wait_send()
        local.wait()

    return pl.pallas_call(
        body, name=name, in_specs=[ANY], out_specs=ANY, out_shape=jax.ShapeDtypeStruct((N_DEV,) + row.shape, row.dtype),
        scratch_shapes=[pltpu.SemaphoreType.DMA((N_DEV - 1,)), pltpu.SemaphoreType.DMA((N_DEV - 1,)), pltpu.SemaphoreType.DMA],
    )(row)


N_CHIP = N_DEV // 2


def _exchange_sibling(gs, name):
    n = len(gs)

    def body(*refs):
        ins, outs = refs[:n], refs[n:2 * n]
        send_sems, recv_sems = refs[2 * n:]
        x, y, c = _me()
        copies = [pltpu.make_async_remote_copy(
            src_ref=ins[a].at[i, 1 - c], dst_ref=outs[a].at[i], send_sem=send_sems.at[a * N_CHIP + i], recv_sem=recv_sems.at[a * N_CHIP + i],
            device_id=(x, y, 1 - c), device_id_type=MESH) for a in range(n) for i in range(N_CHIP)]
        for cp in copies:
            cp.start()
        for cp in copies:
            cp.wait_recv()
        for cp in copies:
            cp.wait_send()

    return pl.pallas_call(
        body, name=name, in_specs=[ANY] * n, out_specs=[ANY] * n,
        out_shape=[jax.ShapeDtypeStruct((N_CHIP,) + g.shape[2:], g.dtype) for g in gs],
        scratch_shapes=[pltpu.SemaphoreType.DMA((n * N_CHIP,)), pltpu.SemaphoreType.DMA((n * N_CHIP,))],
    )(*gs)


def _exchange_chips(ps, name):
    n = len(ps)

    def body(*refs):
        ins, outs = refs[:n], refs[n:2 * n]
        send_sems, recv_sems = refs[2 * n:]
        x, y, c = _me()
        chips = [(1 - x, y), (x, 1 - y), (1 - x, 1 - y)]
        copies = [pltpu.make_async_remote_copy(
            src_ref=ins[a].at[2 * cx + cy], dst_ref=outs[a].at[k], send_sem=send_sems.at[a * 3 + k], recv_sem=recv_sems.at[a * 3 + k],
            device_id=(cx, cy, c), device_id_type=MESH) for a in range(n) for k, (cx, cy) in enumerate(chips)]
        for cp in copies:
            cp.start()
        for cp in copies:
            cp.wait_recv()
        for cp in copies:
            cp.wait_send()

    return pl.pallas_call(
        body, name=name, in_specs=[ANY] * n, out_specs=[ANY] * n,
        out_shape=[jax.ShapeDtypeStruct((3,) + p.shape[1:], p.dtype) for p in ps],
        scratch_shapes=[pltpu.SemaphoreType.DMA((n * 3,)), pltpu.SemaphoreType.DMA((n * 3,))],
    )(*ps)


def _row(v, width=None):
    v = v.reshape(1, -1).astype(f32)
    if width is not None and v.shape[1] < width:
        v = jnp.pad(v, ((0, 0), (0, width - v.shape[1])))
    return v


def _layer_params(p, l):
    return dict(
        norm_mix=_row(p["norm_mix"][l]), norm_ffn=_row(p["norm_ffn"][l]), conv_w=p["conv_w"][l], conv_b=_row(p["conv_b"][l]),
        ssd=(_row(p["dt_bias"][l], LANES), _row(p["a_log"][l], LANES), _row(jnp.repeat(p["d_skip"][l], HEAD_DIM)), _row(p["ssm_norm"][l])))


def _layer_fwd(h, big, sp, tabs, l):
    tag = f"l{l}_"
    w_in, w_out, w_gate, w_up, w_down = big
    hn = _rmsnorm_fwd(h, sp["norm_mix"], tag + "norm_mix")
    qkv = _matmul(hn, w_in, mode="nn", n_out=QKV_WIDTH, tn=256, b_off=0, name=tag + "proj_qkv")
    z = _matmul(hn, w_in, mode="nn", n_out=SSM_INNER, tn=256, b_off=Z_OFF // 256, name=tag + "proj_z")
    xbc_pre = _matmul(hn, w_in, mode="nn", n_out=CONV_CH, tn=256, b_off=XBC_OFF // 256, name=tag + "proj_xbc")
    dtp = _matmul(hn, w_in, mode="nn", n_out=LANES, tn=LANES, b_off=DT_OFF // LANES, name=tag + "proj_dt")
    o, lse = _attn_fwd(qkv, tabs, tag + "attn_fwd")
    xbc = _conv_fwd(xbc_pre, sp["conv_w"], sp["conv_b"], tag + "conv_fwd")
    yn, y, hs = _ssd_fwd(xbc, z, dtp, sp["ssd"], tag + "ssd_fwd")
    t1 = _matmul(o, w_out, mode="nn", k_len=ATTN_WIDTH, tk=512, add=h, name=tag + "out_attn")
    h2 = _matmul(yn, w_out, mode="nn", k_len=SSM_INNER, tk=512, b_koff=1, add=t1, name=tag + "out_ssm")
    hn2 = _rmsnorm_fwd(h2, sp["norm_ffn"], tag + "norm_ffn")
    g, u, act = _swiglu_fwd(hn2, w_gate, w_up, tag + "ffn_up")
    h3 = _matmul(act, w_down, mode="nn", tk=1408, add=h2, name=tag + "ffn_down")
    saved = dict(h=h, hn=hn, qkv=qkv, z=z, xbc_pre=xbc_pre, dtp=dtp, o=o, lse=lse, xbc=xbc, yn=yn, y=y, hs=hs, h2=h2, hn2=hn2, g=g, u=u, act=act)
    return h3, saved


def _layer_bwd(dh3, s, big, sp, tabs, l):
    tag = f"l{l}_"
    w_in, w_out, w_gate, w_up, w_down = big
    dg, du = _swiglu_bwd(dh3, w_down, s["g"], s["u"], tag + "ffn_down_bwd")
    dw_down = _matmul(s["act"], dh3, mode="tn", tm=1408, tn=512, tk=2048, name=tag + "dw_down")
    dhn2 = _matmul(dg, w_gate, mode="nt", tk=1408, name=tag + "ffn_gate_bwd")
    dhn2 = _matmul(du, w_up, mode="nt", tk=1408, add=dhn2, name=tag + "ffn_up_bwd")
    dw_gate = _matmul(s["hn2"], dg, mode="tn", tm=512, tn=1408, tk=2048, name=tag + "dw_gate")
    dw_up = _matmul(s["hn2"], du, mode="tn", tm=512, tn=1408, tk=2048, name=tag + "dw_up")
    dh2, dnf = _rmsnorm_bwd(dhn2, s["h2"], sp["norm_ffn"], dh3, tag + "norm_ffn_bwd")
    d_o = _matmul(dh2, w_out, mode="nt", n_out=ATTN_WIDTH, tn=512, b_off=0, name=tag + "out_attn_bwd")
    dyn = _matmul(dh2, w_out, mode="nt", n_out=SSM_INNER, tn=512, b_off=1, name=tag + "out_ssm_bwd")
    dw_out = jnp.concatenate([_matmul(s["o"], dh2, mode="tn", tm=512, tn=512, tk=2048, name=tag + "dw_out_attn"),
                              _matmul(s["yn"], dh2, mode="tn", tm=512, tn=512, tk=2048, name=tag + "dw_out_ssm")], axis=0)
    dxbc, dz, ddtp, dnw, dds, dal, dbi = _ssd_bwd(s["xbc"], s["z"], s["dtp"], s["y"], s["hs"], dyn, sp["ssd"], tag + "ssd_bwd")
    dxbc_pre, dconv_w, dconv_b = _conv_bwd(s["xbc_pre"], sp["conv_w"], sp["conv_b"], dxbc, tag + "conv_bwd")
    dq, dk, dv = _attn_bwd(s["qkv"], tabs, s["o"], s["lse"], d_o, tag + "attn_bwd")
    dproj = jnp.concatenate([dq.astype(bf16), dk.astype(bf16), dv.astype(bf16), dz.astype(bf16), dxbc_pre.astype(bf16), ddtp.astype(bf16)], axis=1)
    dhn = _matmul(dproj, w_in, mode="nt", tk=1152, name=tag + "proj_bwd")
    dw_in = _matmul(s["hn"], dproj, mode="tn", tm=512, tn=1152, tk=2048, name=tag + "dw_in")
    dh, dnm = _rmsnorm_bwd(dhn, s["h"], sp["norm_mix"], dh2, tag + "norm_mix_bwd")
    grads = dict(
        norm_mix=dnm.sum(0), w_in=dw_in, conv_w=dconv_w, conv_b=dconv_b[0], dt_bias=dbi.sum(0)[:SSM_HEADS], a_log=dal.sum(0)[:SSM_HEADS],
        d_skip=dds.sum(0).reshape(SSM_HEADS, HEAD_DIM).sum(1), ssm_norm=dnw.sum(0), w_out=dw_out, norm_ffn=dnf.sum(0),
        w_gate=dw_gate, w_up=dw_up, w_down=dw_down)
    return dh, grads


def _local_step(x, positions, target, p, bigs):
    tabs = _rope_tables(positions.reshape(-1, 1), "rope_tables")
    h = x
    saved, sps = [], []
    for l in range(DEPTH):
        sps.append(_layer_params(p, l))
        h, s = _layer_fwd(h, bigs[l], sps[l], tabs, l)
        saved.append(s)
    dh, loss_parts, dfn = _final_loss(h, _row(p["final_norm"]), target, "final_loss")
    layer_grads = [None] * DEPTH
    for l in reversed(range(DEPTH)):
        dh, layer_grads[l] = _layer_bwd(dh, saved[l], bigs[l], sps[l], tabs, l)
    grads = {k: [layer_grads[l][k] for l in range(DEPTH)] for k in layer_grads[0]}
    grads["final_norm"] = dfn.sum(0)
    return jnp.sum(loss_parts), dh, grads


BIG = ("w_in", "w_out", "w_gate", "w_up", "w_down")
COL_SHARDED = ("w_in", "w_gate", "w_up")
SMALL = ("norm_mix", "conv_b", "dt_bias", "a_log", "d_skip", "ssm_norm", "norm_ffn", "final_norm")
WEIGHTS = ("norm_mix", "w_in", "conv_w", "conv_b", "dt_bias", "a_log", "d_skip", "ssm_norm", "w_out", "norm_ffn", "w_gate", "w_up", "w_down", "final_norm")
PACK_W = 1024
SMALL_ROWS = 88
CONVW_ROWS = 96
CONVW_SHARD_ROWS = 16


def _full_from_gathered(name, g):
    _, depth, a, b = g.shape
    if name in COL_SHARDED:
        return _cols_from_devices(g, IN_PROJ_PAD if name == "w_in" else N_DEV * b, "cols_" + name)
    return jnp.transpose(g, (1, 0, 2, 3)).reshape(depth, N_DEV * a, b)


def _by_device(name, per_layer, shard_shape):
    depth, a, b = shard_shape
    if name in COL_SHARDED:
        t = _devices_from_cols(per_layer, b, "devs_" + name)
    else:
        t = jnp.stack([q.reshape(N_DEV, a, b) for q in per_layer], axis=1)
    return t.reshape(N_CHIP, 2, depth * a, b)


def _pack_rows(parts, rows, width):
    flat = jnp.concatenate([q.reshape(-1) for q in parts])
    return jnp.pad(flat, (0, rows * width - flat.shape[0])).reshape(rows, width)


def _unpack(flat, like):
    out, off = [], 0
    for q in like:
        out.append(flat[off:off + q.size].reshape(q.shape))
        off += q.size
    return out


def kernel(x, positions, norm_mix, w_in, conv_w, conv_b, dt_bias, a_log, d_skip, ssm_norm, w_out, norm_ffn, w_gate, w_up, w_down, final_norm, loss_target, m_norm_mix, m_w_in, m_conv_w, m_conv_b, m_dt_bias, m_a_log, m_d_skip, m_ssm_norm, m_w_out, m_norm_ffn, m_w_gate, m_w_up, m_w_down, m_final_norm, v_norm_mix, v_w_in, v_conv_w, v_conv_b, v_dt_bias, v_a_log, v_d_skip, v_ssm_norm, v_w_out, v_norm_ffn, v_w_gate, v_w_up, v_w_down, v_final_norm):
    w = dict(norm_mix=norm_mix, w_in=w_in, conv_w=conv_w, conv_b=conv_b, dt_bias=dt_bias, a_log=a_log, d_skip=d_skip, ssm_norm=ssm_norm,
             w_out=w_out, norm_ffn=norm_ffn, w_gate=w_gate, w_up=w_up, w_down=w_down, final_norm=final_norm)
    m = dict(norm_mix=m_norm_mix, w_in=m_w_in, conv_w=m_conv_w, conv_b=m_conv_b, dt_bias=m_dt_bias, a_log=m_a_log, d_skip=m_d_skip,
             ssm_norm=m_ssm_norm, w_out=m_w_out, norm_ffn=m_norm_ffn, w_gate=m_w_gate, w_up=m_w_up, w_down=m_w_down, final_norm=m_final_norm)
    v = dict(norm_mix=v_norm_mix, w_in=v_w_in, conv_w=v_conv_w, conv_b=v_conv_b, dt_bias=v_dt_bias, a_log=v_a_log, d_skip=v_d_skip,
             ssm_norm=v_ssm_norm, w_out=v_w_out, norm_ffn=v_norm_ffn, w_gate=v_w_gate, w_up=v_w_up, w_down=v_w_down, final_norm=v_final_norm)
    ax, ay, ac = lax.axis_index("x"), lax.axis_index("y"), lax.axis_index("c")
    dev = 4 * ax + 2 * ay + ac

    gathered = _allgather_two_level([w[k].astype(bf16) for k in BIG] + [w["conv_w"]], "gather_weights")
    full = {k: _full_from_gathered(k, g) for k, g in zip(BIG, gathered[:len(BIG)])}
    p = {k: w[k] for k in SMALL}
    p["conv_w"] = jnp.transpose(gathered[-1], (1, 2, 0, 3)).reshape(DEPTH, CONV_WIDTH, CONV_CH)
    bigs = [tuple(full[k][l] for k in BIG) for l in range(DEPTH)]

    t = x.shape[0] * x.shape[1]
    loss_local, dx, grads = _local_step(x.reshape(t, D_MODEL), positions.reshape(t), loss_target.reshape(t, D_MODEL), p, bigs)

    core = ac.reshape(1).astype(jnp.int32)
    by_dev = [_by_device(k, grads[k], w[k].shape) for k in BIG]
    from_sibling = _exchange_sibling(by_dev, "scatter_sibling")
    chip_sums = [_add_kept(g, r, core, "scatter_add_" + k) for k, g, r in zip(BIG, by_dev, from_sibling)]
    from_chips = _exchange_chips(chip_sums, "scatter_chips")
    out_g, out_d, out_m, out_v = {}, {}, {}, {}
    for k, cs, fc in zip(BIG, chip_sums, from_chips):
        own = lax.dynamic_index_in_dim(cs, 2 * ax + ay, 0, keepdims=False)
        rows2d = (cs.shape[1], cs.shape[2])
        res = _adamw([(own, None), (fc, 0), (fc, 1), (fc, 2)], w[k].reshape(rows2d), m[k].reshape(rows2d), v[k].reshape(rows2d), "adamw_" + k)
        for dst, src in zip((out_g, out_d, out_m, out_v), res):
            dst[k] = src.reshape(w[k].shape)

    small_like = [w[k] for k in SMALL]
    small_grads = [jnp.stack(grads[k]) if k != "final_norm" else grads[k] for k in SMALL]
    small_pack = jnp.concatenate([_pack_rows(small_grads, SMALL_ROWS, LANES), _pack_rows([jnp.stack(grads["conv_w"])], CONVW_ROWS, LANES)], axis=0)
    parts = _allgather_direct(small_pack, "gather_small_grads")
    g_s, d_s, m_s, v_s = _adamw(
        [(parts[i, :SMALL_ROWS], None) for i in range(N_DEV)], _pack_rows(small_like, SMALL_ROWS, LANES),
        _pack_rows([m[k] for k in SMALL], SMALL_ROWS, LANES), _pack_rows([v[k] for k in SMALL], SMALL_ROWS, LANES), "adamw_replicated")
    for dst, src in ((out_g, g_s), (out_d, d_s), (out_m, m_s), (out_v, v_s)):
        dst.update(zip(SMALL, _unpack(src.reshape(-1), small_like)))
    shard_w = conv_w.shape[-1]
    conv_parts = parts[:, SMALL_ROWS:].reshape(N_DEV, DEPTH, CONV_WIDTH, CONV_CH)
    conv_mine = lax.dynamic_slice_in_dim(conv_parts, dev * shard_w, shard_w, axis=3)
    g_c, d_c, m_c, v_c = _adamw(
        [(_pack_rows([conv_mine[i]], CONVW_SHARD_ROWS, LANES), None) for i in range(N_DEV)], _pack_rows([conv_w], CONVW_SHARD_ROWS, LANES),
        _pack_rows([m["conv_w"]], CONVW_SHARD_ROWS, LANES), _pack_rows([v["conv_w"]], CONVW_SHARD_ROWS, LANES), "adamw_conv_w")
    for dst, src in ((out_g, g_c), (out_d, d_c), (out_m, m_c), (out_v, v_c)):
        dst["conv_w"] = src.reshape(-1)[:conv_w.size].reshape(conv_w.shape)

    loss = lax.psum(loss_local, ("x", "y", "c"))
    return (loss, dx.reshape(x.shape), *[out_g[k] for k in WEIGHTS], *[out_d[k] for k in WEIGHTS],
            *[out_m[k] for k in WEIGHTS], *[out_v[k] for k in WEIGHTS])
```

```python
import functools
import math

import jax
import jax.numpy as jnp
import numpy as np
from jax import lax
from jax.experimental import pallas as pl
from jax.experimental.pallas import tpu as pltpu

f32 = jnp.float32
bf16 = jnp.bfloat16

D_MODEL = 1024
SEQ = 2048
DEPTH = 2
HEAD_DIM = 64
N_ATTN_HEADS = 8
N_KV_HEADS = 2
ATTN_WIDTH = 512
KV_WIDTH = 128
ROPE_DIM = 16
ROPE_THETA = 500000.0
DILATIONS = (1, 4, 16)
ATTN_BLOCK = 128
SSM_HEADS = 16
SSM_INNER = 1024
SSM_GROUPS = 2
D_STATE = 128
CONV_WIDTH = 4
CHUNK = 128
CONV_CH = 1536
MIX_WIDTH = 1536
QKV_WIDTH = ATTN_WIDTH + 2 * KV_WIDTH
Z_OFF = 768
XBC_OFF = 1792
DT_OFF = 3328
IN_PROJ = 3344
IN_PROJ_PAD = 3456
FFN_HIDDEN = 2816
EPS = 1e-5
N_DEV = 8
ADAM_LR = 0.001
ADAM_B1 = 0.9
ADAM_B2 = 0.999
ADAM_EPS = 1e-08
ADAM_WD = 0.01
ADAM_STEP = 10

LANES = 128
SUBLANES = 8
VMEM_LIMIT = 56 * 1024 * 1024

MESH = pl.DeviceIdType.MESH
ANY = pl.BlockSpec(memory_space=pl.ANY)


def _cparams(sem, vmem=None):
    return pltpu.CompilerParams(dimension_semantics=sem, vmem_limit_bytes=vmem or VMEM_LIMIT)


def _sigmoid(x):
    return 1.0 / (1.0 + jnp.exp(-x))


def _silu(x):
    return x * _sigmoid(x)


def _dsilu(x):
    s = _sigmoid(x)
    return s * (1.0 + x * (1.0 - s))


def _softplus(x):
    return jnp.maximum(x, 0.0) + jnp.log(1.0 + jnp.exp(-jnp.abs(x)))


def _dot(a, b, dims, precision=None):
    return lax.dot_general(a, b, (dims, ((), ())), preferred_element_type=f32, precision=precision)


def _nn(a, b, precision=None):
    return _dot(a, b, ((1,), (0,)), precision)


def _nt(a, b):
    return _dot(a, b, ((1,), (1,)))


def _tn(a, b):
    return _dot(a, b, ((0,), (0,)))


def _rowsum8(t):
    n, w = t.shape
    return jnp.sum(t.reshape(n // SUBLANES, SUBLANES, w), axis=0)


def _matmul(a, b, *, mode, n_out=None, b_off=0, a_koff=0, b_koff=0, k_len=None, add=None, out_dtype=f32, tm=2048, tn=512, tk=1024, name):
    if mode == "tn":
        kdim_a, m = a.shape
    else:
        m, kdim_a = a.shape
    kk = k_len if k_len is not None else kdim_a
    n = n_out if n_out is not None else (b.shape[0] if mode == "nt" else b.shape[1])
    tm, tn, tk = min(tm, m), min(tn, n), min(tk, kk)
    assert m % tm == 0 and n % tn == 0 and kk % tk == 0, (name, m, n, kk, tm, tn, tk)
    nk = kk // tk
    if mode == "nn":
        a_spec = pl.BlockSpec((tm, tk), lambda i, j, k: (i, k + a_koff))
        b_spec = pl.BlockSpec((tk, tn), lambda i, j, k: (k + b_koff, j + b_off))
        dims = ((1,), (0,))
    elif mode == "nt":
        a_spec = pl.BlockSpec((tm, tk), lambda i, j, k: (i, k + a_koff))
        b_spec = pl.BlockSpec((tn, tk), lambda i, j, k: (j + b_off, k + b_koff))
        dims = ((1,), (1,))
    else:
        a_spec = pl.BlockSpec((tk, tm), lambda i, j, k: (k + a_koff, i))
        b_spec = pl.BlockSpec((tk, tn), lambda i, j, k: (k + b_koff, j + b_off))
        dims = ((0,), (0,))
    o_spec = pl.BlockSpec((tm, tn), lambda i, j, k: (i, j))
    has_add = add is not None

    def body(*refs):
        if has_add:
            a_ref, b_ref, add_ref, o_ref, acc_ref = refs
        else:
            a_ref, b_ref, o_ref, acc_ref = refs
        k = pl.program_id(2)
        part = _dot(a_ref[...].astype(bf16), b_ref[...].astype(bf16), dims)

        @pl.when(k == 0)
        def _():
            acc_ref[...] = part

        @pl.when(k > 0)
        def _():
            acc_ref[...] += part

        @pl.when(k == nk - 1)
        def _():
            r = acc_ref[...]
            if has_add:
                r = r + add_ref[...]
            o_ref[...] = r.astype(out_dtype)

    in_specs = [a_spec, b_spec] + ([o_spec] if has_add else [])
    args = (a, b) + ((add,) if has_add else ())
    return pl.pallas_call(
        body, name=name, grid=(m // tm, n // tn, nk), in_specs=in_specs, out_specs=o_spec,
        out_shape=jax.ShapeDtypeStruct((m, n), out_dtype), scratch_shapes=[pltpu.VMEM((tm, tn), f32)],
        compiler_params=_cparams(("parallel", "parallel", "arbitrary")),
    )(*args)


def _swiglu_fwd(hn, w_gate, w_up, name, tm=2048, tn=256):
    m, k = hn.shape
    n = w_gate.shape[1]

    def body(a_ref, wg_ref, wu_ref, g_ref, u_ref, act_ref):
        a = a_ref[...]
        g = _nn(a, wg_ref[...])
        u = _nn(a, wu_ref[...])
        g_ref[...] = g.astype(bf16)
        u_ref[...] = u.astype(bf16)
        act_ref[...] = (_silu(g) * u).astype(bf16)

    a_spec = pl.BlockSpec((tm, k), lambda i, j: (i, 0))
    w_spec = pl.BlockSpec((k, tn), lambda i, j: (0, j))
    o_spec = pl.BlockSpec((tm, tn), lambda i, j: (i, j))
    return pl.pallas_call(
        body, name=name, grid=(m // tm, n // tn), in_specs=[a_spec, w_spec, w_spec], out_specs=[o_spec, o_spec, o_spec],
        out_shape=[jax.ShapeDtypeStruct((m, n), bf16)] * 3,
        compiler_params=_cparams(("parallel", "parallel")),
    )(hn, w_gate, w_up)


def _swiglu_bwd(dh, w_down, g, u, name, tm=2048, tn=256):
    m, k = dh.shape
    n = w_down.shape[0]

    def body(a_ref, w_ref, g_ref, u_ref, dg_ref, du_ref):
        dact = _nt(a_ref[...].astype(bf16), w_ref[...])
        gg = g_ref[...].astype(f32)
        dg_ref[...] = (dact * u_ref[...].astype(f32) * _dsilu(gg)).astype(bf16)
        du_ref[...] = (dact * _silu(gg)).astype(bf16)

    a_spec = pl.BlockSpec((tm, k), lambda i, j: (i, 0))
    w_spec = pl.BlockSpec((tn, k), lambda i, j: (j, 0))
    o_spec = pl.BlockSpec((tm, tn), lambda i, j: (i, j))
    return pl.pallas_call(
        body, name=name, grid=(m // tm, n // tn), in_specs=[a_spec, w_spec, o_spec, o_spec], out_specs=[o_spec, o_spec],
        out_shape=[jax.ShapeDtypeStruct((m, n), bf16), jax.ShapeDtypeStruct((m, n), bf16)],
        compiler_params=_cparams(("parallel", "parallel")),
    )(dh, w_down, g, u)


def _rmsnorm_fwd(h, w, name, tm=512):
    m, d = h.shape

    def body(h_ref, w_ref, o_ref):
        x = h_ref[...]
        r = lax.rsqrt(jnp.mean(x * x, axis=-1, keepdims=True) + EPS)
        o_ref[...] = (x * r * w_ref[...]).astype(bf16)

    return pl.pallas_call(
        body, name=name, grid=(m // tm,),
        in_specs=[pl.BlockSpec((tm, d), lambda i: (i, 0)), pl.BlockSpec((1, d), lambda i: (0, 0))],
        out_specs=pl.BlockSpec((tm, d), lambda i: (i, 0)), out_shape=jax.ShapeDtypeStruct((m, d), bf16),
        compiler_params=_cparams(("parallel",)),
    )(h, w)


def _rmsnorm_bwd(dhn, h, w, dres, name, tm=512):
    m, d = h.shape

    def body(dhn_ref, h_ref, w_ref, dres_ref, dh_ref, dw_ref):
        x = h_ref[...]
        r = lax.rsqrt(jnp.mean(x * x, axis=-1, keepdims=True) + EPS)
        xhat = x * r
        dy = dhn_ref[...]
        gw = dy * w_ref[...]
        dh_ref[...] = dres_ref[...] + r * (gw - xhat * jnp.mean(gw * xhat, axis=-1, keepdims=True))
        part = _rowsum8(dy * xhat)

        @pl.when(pl.program_id(0) == 0)
        def _():
            dw_ref[...] = part

        @pl.when(pl.program_id(0) > 0)
        def _():
            dw_ref[...] += part

    row = pl.BlockSpec((tm, d), lambda i: (i, 0))
    return pl.pallas_call(
        body, name=name, grid=(m // tm,),
        in_specs=[row, row, pl.BlockSpec((1, d), lambda i: (0, 0)), row],
        out_specs=[row, pl.BlockSpec((SUBLANES, d), lambda i: (0, 0))],
        out_shape=[jax.ShapeDtypeStruct((m, d), f32), jax.ShapeDtypeStruct((SUBLANES, d), f32)],
        compiler_params=_cparams(("arbitrary",)),
    )(dhn, h, w, dres)


def _final_loss(h, w, target, name, tm=512):
    m, d = h.shape

    def body(h_ref, w_ref, t_ref, dh_ref, loss_ref, dw_ref):
        x = h_ref[...]
        r = lax.rsqrt(jnp.mean(x * x, axis=-1, keepdims=True) + EPS)
        xhat = x * r
        ww = w_ref[...]
        err = xhat * ww - t_ref[...]
        dy = err * (1.0 / d)
        gw = dy * ww
        dh_ref[...] = r * (gw - xhat * jnp.mean(gw * xhat, axis=-1, keepdims=True))
        lpart = _rowsum8(err * err) * (0.5 / d)
        wpart = _rowsum8(dy * xhat)

        @pl.when(pl.program_id(0) == 0)
        def _():
            loss_ref[...] = lpart
            dw_ref[...] = wpart

        @pl.when(pl.program_id(0) > 0)
        def _():
            loss_ref[...] += lpart
            dw_ref[...] += wpart

    row = pl.BlockSpec((tm, d), lambda i: (i, 0))
    acc = pl.BlockSpec((SUBLANES, d), lambda i: (0, 0))
    return pl.pallas_call(
        body, name=name, grid=(m // tm,),
        in_specs=[row, pl.BlockSpec((1, d), lambda i: (0, 0)), row], out_specs=[row, acc, acc],
        out_shape=[jax.ShapeDtypeStruct((m, d), f32), jax.ShapeDtypeStruct((SUBLANES, d), f32), jax.ShapeDtypeStruct((SUBLANES, d), f32)],
        compiler_params=_cparams(("arbitrary",)),
    )(h, w, target)


def _lane_tables():
    f = np.arange(LANES) % HEAD_DIM
    inv = ROPE_THETA ** (-jnp.arange(0, ROPE_DIM, 2, dtype=f32) / ROPE_DIM)
    invf = jnp.where(f < ROPE_DIM, inv[f % (ROPE_DIM // 2)], 0.0).astype(f32)
    return invf.reshape(1, LANES)


def _rope_tables(pos_col, name):
    t = pos_col.shape[0]
    tm = SEQ

    def body(p_ref, f_ref, c_ref, s1_ref, s2_ref):
        ang = p_ref[...].astype(f32) * f_ref[...]
        co, si = jnp.cos(ang), jnp.sin(ang)
        f = lax.broadcasted_iota(jnp.int32, (tm, LANES), 1) % HEAD_DIM
        c_ref[...] = jnp.where(f < ROPE_DIM, co, 1.0)
        s1_ref[...] = jnp.where(f < ROPE_DIM // 2, -si, 0.0)
        s2_ref[...] = jnp.where((f >= ROPE_DIM // 2) & (f < ROPE_DIM), si, 0.0)

    row = pl.BlockSpec((tm, LANES), lambda i: (i, 0))
    return pl.pallas_call(
        body, name=name, grid=(t // tm,),
        in_specs=[pl.BlockSpec((tm, 1), lambda i: (i, 0)), pl.BlockSpec((1, LANES), lambda i: (0, 0))],
        out_specs=[row, row, row], out_shape=[jax.ShapeDtypeStruct((t, LANES), f32)] * 3,
        compiler_params=_cparams(("parallel",)),
    )(pos_col, _lane_tables())


def _rot(x, c, s1, s2):
    return x * c + pltpu.roll(x, LANES - ROPE_DIM // 2, 1) * s1 + pltpu.roll(x, ROPE_DIM // 2, 1) * s2


def _rot_t(g, c, s1, s2):
    return g * c + pltpu.roll(g * s1, ROPE_DIM // 2, 1) + pltpu.roll(g * s2, LANES - ROPE_DIM // 2, 1)


def _dup_head(x, kvh, low):
    a = jnp.where(kvh == 0, x, pltpu.roll(x, HEAD_DIM, 1))
    return jnp.where(low, a, pltpu.roll(a, HEAD_DIM, 1))


def _deinterleave(src_ref, dst_ref, d, dtype):
    length = SEQ // d
    if d == 1:
        dst_ref[...] = src_ref[...].astype(dtype)
    else:
        for r in range(d):
            dst_ref[pl.ds(r * length, length), :] = src_ref[pl.ds(r, length, stride=d), :].astype(dtype)


def _interleave_store(src_ref, dst_ref, d, accumulate):
    length = SEQ // d
    if d == 1:
        if accumulate:
            dst_ref[...] += src_ref[...]
        else:
            dst_ref[...] = src_ref[...]
    else:
        for r in range(d):
            blk = src_ref[pl.ds(r * length, length), :]
            if accumulate:
                dst_ref[pl.ds(r, length, stride=d), :] = dst_ref[pl.ds(r, length, stride=d), :] + blk
            else:
                dst_ref[pl.ds(r, length, stride=d), :] = blk


def _attn_masks():
    qi = lax.broadcasted_iota(jnp.int32, (ATTN_BLOCK, ATTN_BLOCK), 0)
    ki = lax.broadcasted_iota(jnp.int32, (ATTN_BLOCK, ATTN_BLOCK), 1)
    low = lax.broadcasted_iota(jnp.int32, (ATTN_BLOCK, LANES), 1) < HEAD_DIM
    return ki <= qi, ki >= qi, low


NEG_INF = float("-inf")
ATTN_UNROLL = 4


def _attn_fwd(qkv, tabs, name):
    t = qkv.shape[0]
    nb = t // SEQ
    n_blk = SEQ // ATTN_BLOCK

    def body(q_ref, k_ref, v_ref, c_ref, s1_ref, s2_ref, o_ref, lse_ref,
             qr, kr, vr, qd, kd, vd, ob, lb, o0, o1, o2, l0, l1, l2, ss):
        kvh = pl.program_id(1) // 2
        cur_ok, prev_ok, low = _attn_masks()
        lowfull = lax.broadcasted_iota(jnp.int32, (SEQ, LANES), 1) < HEAD_DIM
        c, s1, s2 = c_ref[...], s1_ref[...], s2_ref[...]
        qr[...] = _rot(q_ref[...], c, s1, s2) * (HEAD_DIM ** -0.5)
        kr[...] = _dup_head(_rot(k_ref[...], c, s1, s2), kvh, lowfull)
        vr[...] = _dup_head(v_ref[...], kvh, lowfull)
        onat, lnat = (o0, o1, o2), (l0, l1, l2)
        for bi, d in enumerate(DILATIONS):
            _deinterleave(qr, qd, d, bf16)
            _deinterleave(kr, kd, d, bf16)
            _deinterleave(vr, vd, d, bf16)
            per_res = n_blk // d
            use_prev = per_res > 1

            def scores(n, carry):
                start = pl.multiple_of(n * ATTN_BLOCK, ATTN_BLOCK)
                has_prev = (n % per_res) != 0
                pstart = pl.multiple_of(jnp.maximum(n - 1, 0) * ATTN_BLOCK, ATTN_BLOCK)
                qb = qd[pl.ds(start, ATTN_BLOCK), :]
                kc = kd[pl.ds(start, ATTN_BLOCK), :]
                if use_prev:
                    kp = kd[pl.ds(pstart, ATTN_BLOCK), :]
                for a in range(2):
                    qa = jnp.where(low if a == 0 else ~low, qb, jnp.zeros_like(qb))
                    ss[2 * n + a, :, 0:ATTN_BLOCK] = jnp.where(cur_ok, _nt(qa, kc), NEG_INF)
                    if use_prev:
                        ss[2 * n + a, :, ATTN_BLOCK:2 * ATTN_BLOCK] = jnp.where(prev_ok & has_prev, _nt(qa, kp), NEG_INF)
                return carry

            def softmax_pv(n, carry):
                start = pl.multiple_of(n * ATTN_BLOCK, ATTN_BLOCK)
                pstart = pl.multiple_of(jnp.maximum(n - 1, 0) * ATTN_BLOCK, ATTN_BLOCK)
                vc = vd[pl.ds(start, ATTN_BLOCK), :]
                if use_prev:
                    vp = vd[pl.ds(pstart, ATTN_BLOCK), :]
                outs, lses = [], []
                for a in range(2):
                    sc = ss[2 * n + a, :, 0:ATTN_BLOCK]
                    if use_prev:
                        sp = ss[2 * n + a, :, ATTN_BLOCK:2 * ATTN_BLOCK]
                        m = jnp.max(jnp.maximum(sc, sp), axis=1, keepdims=True)
                        pc, pp = jnp.exp(sc - m), jnp.exp(sp - m)
                        den = jnp.sum(pc + pp, axis=1, keepdims=True)
                        acc = _nn(pc.astype(bf16), vc) + _nn(pp.astype(bf16), vp)
                    else:
                        m = jnp.max(sc, axis=1, keepdims=True)
                        pc = jnp.exp(sc - m)
                        den = jnp.sum(pc, axis=1, keepdims=True)
                        acc = _nn(pc.astype(bf16), vc)
                    outs.append(acc * (1.0 / den))
                    lses.append(m + jnp.log(den))
                ob[pl.ds(start, ATTN_BLOCK), :] = jnp.where(low, outs[0], outs[1])
                lb[pl.ds(start, ATTN_BLOCK), :] = jnp.where(low, lses[0], lses[1])
                return carry

            lax.fori_loop(0, n_blk, scores, 0, unroll=ATTN_UNROLL)
            lax.fori_loop(0, n_blk, softmax_pv, 0, unroll=ATTN_UNROLL)
            _interleave_store(ob, onat[bi], d, False)
            _interleave_store(lb, lnat[bi], d, False)
        la, lbb, lc = l0[...], l1[...], l2[...]
        lm = jnp.maximum(jnp.maximum(la, lbb), lc)
        wa, wb, wc = jnp.exp(la - lm), jnp.exp(lbb - lm), jnp.exp(lc - lm)
        ws = wa + wb + wc
        o_ref[...] = (wa * o0[...] + wb * o1[...] + wc * o2[...]) / ws
        lse_ref[...] = lm + jnp.log(ws)

    def col(jj):
        return pl.BlockSpec((SEQ, LANES), lambda b, j: (b, jj if jj is not None else j))

    tab = pl.BlockSpec((SEQ, LANES), lambda b, j: (b, 0))
    fs = pltpu.VMEM((SEQ, LANES), f32)
    hs = pltpu.VMEM((SEQ, LANES), bf16)
    return pl.pallas_call(
        body, name=name, grid=(nb, ATTN_WIDTH // LANES),
        in_specs=[col(None), col(ATTN_WIDTH // LANES), col(ATTN_WIDTH // LANES + 1), tab, tab, tab],
        out_specs=[col(None), col(None)],
        out_shape=[jax.ShapeDtypeStruct((t, ATTN_WIDTH), f32), jax.ShapeDtypeStruct((t, ATTN_WIDTH), f32)],
        scratch_shapes=[fs, fs, fs, hs, hs, hs, fs, fs, fs, fs, fs, fs, fs, fs, pltpu.VMEM((2 * n_blk, ATTN_BLOCK, 2 * ATTN_BLOCK), f32)],
        compiler_params=_cparams(("parallel", "parallel")),
    )(qkv, qkv, qkv, *tabs)


def _attn_bwd(qkv, tabs, o, lse, do, name):
    t = qkv.shape[0]
    nb = t // SEQ
    n_blk = SEQ // ATTN_BLOCK
    n_j = ATTN_WIDTH // LANES

    def body(q_ref, k_ref, v_ref, c_ref, s1_ref, s2_ref, o_ref, lse_ref, do_ref, dq_ref, dk_ref, dv_ref,
             qr, kr, vr, dl, qd, kd, vd, dod, lsd, dld, dqd, dkd, dvd, dqa, dka, dva, pb, dsb):
        j = pl.program_id(1)
        pb[2 * n_blk:2 * n_blk + 2] = jnp.zeros((2, ATTN_BLOCK, 2 * ATTN_BLOCK), bf16)
        dsb[2 * n_blk:2 * n_blk + 2] = jnp.zeros((2, ATTN_BLOCK, 2 * ATTN_BLOCK), bf16)
        kvh = j // 2
        cur_ok, prev_ok, low = _attn_masks()
        lowfull = lax.broadcasted_iota(jnp.int32, (SEQ, LANES), 1) < HEAD_DIM
        c, s1, s2 = c_ref[...], s1_ref[...], s2_ref[...]
        qr[...] = _rot(q_ref[...], c, s1, s2) * (HEAD_DIM ** -0.5)
        kr[...] = _dup_head(_rot(k_ref[...], c, s1, s2), kvh, lowfull)
        vr[...] = _dup_head(v_ref[...], kvh, lowfull)
        prod = do_ref[...] * o_ref[...]
        d_lo = jnp.sum(jnp.where(lowfull, prod, 0.0), axis=1, keepdims=True)
        d_hi = jnp.sum(jnp.where(lowfull, 0.0, prod), axis=1, keepdims=True)
        dl[...] = jnp.where(lowfull, d_lo, d_hi)
        dqa[...] = jnp.zeros_like(dqa)
        dka[...] = jnp.zeros_like(dka)
        dva[...] = jnp.zeros_like(dva)
        for d in DILATIONS:
            _deinterleave(qr, qd, d, bf16)
            _deinterleave(kr, kd, d, bf16)
            _deinterleave(vr, vd, d, bf16)
            _deinterleave(do_ref, dod, d, bf16)
            _deinterleave(lse_ref, lsd, d, f32)
            _deinterleave(dl, dld, d, f32)
            per_res = n_blk // d
            use_prev = per_res > 1
            curl, prevl = slice(0, ATTN_BLOCK), slice(ATTN_BLOCK, 2 * ATTN_BLOCK)

            def halves(x):
                zero = jnp.zeros_like(x)
                return jnp.where(low, x, zero), jnp.where(low, zero, x)

            def probs(n, carry):
                start = pl.multiple_of(n * ATTN_BLOCK, ATTN_BLOCK)
                has_prev = (n % per_res) != 0
                pstart = pl.multiple_of(jnp.maximum(n - 1, 0) * ATTN_BLOCK, ATTN_BLOCK)
                cur, prev = pl.ds(start, ATTN_BLOCK), pl.ds(pstart, ATTN_BLOCK)
                qas, doas = halves(qd[cur, :]), halves(dod[cur, :])
                kc, vc = kd[cur, :], vd[cur, :]
                if use_prev:
                    kp, vp = kd[prev, :], vd[prev, :]
                lsb, dlb = lsd[cur, :], dld[cur, :]
                for a in range(2):
                    ls = lsb[:, a * HEAD_DIM:a * HEAD_DIM + 1]
                    de = dlb[:, a * HEAD_DIM:a * HEAD_DIM + 1]
                    pc = jnp.exp(jnp.where(cur_ok, _nt(qas[a], kc), NEG_INF) - ls)
                    pb[2 * n + a, :, curl] = pc.astype(bf16)
                    dsb[2 * n + a, :, curl] = (pc * (_nt(doas[a], vc) - de)).astype(bf16)
                    if use_prev:
                        pp = jnp.exp(jnp.where(prev_ok & has_prev, _nt(qas[a], kp), NEG_INF) - ls)
                        pb[2 * n + a, :, prevl] = pp.astype(bf16)
                        dsb[2 * n + a, :, prevl] = (pp * (_nt(doas[a], vp) - de)).astype(bf16)
                return carry

            def grads(n, carry):
                start = pl.multiple_of(n * ATTN_BLOCK, ATTN_BLOCK)
                pstart = pl.multiple_of(jnp.maximum(n - 1, 0) * ATTN_BLOCK, ATTN_BLOCK)
                nstart = pl.multiple_of(jnp.minimum(n + 1, n_blk - 1) * ATTN_BLOCK, ATTN_BLOCK)
                cur, prev, nxt = pl.ds(start, ATTN_BLOCK), pl.ds(pstart, ATTN_BLOCK), pl.ds(nstart, ATTN_BLOCK)
                kc = kd[cur, :]
                dqs = [_nn(dsb[2 * n + a, :, curl], kc) for a in range(2)]
                q_rows, do_rows = list(halves(qd[cur, :])), list(halves(dod[cur, :]))
                ds_rows, p_rows = [dsb[2 * n + a, :, curl] for a in range(2)], [pb[2 * n + a, :, curl] for a in range(2)]
                if use_prev:
                    kp = kd[prev, :]
                    dqs = [dqs[a] + _nn(dsb[2 * n + a, :, prevl], kp) for a in range(2)]
                    q_rows += list(halves(qd[nxt, :]))
                    do_rows += list(halves(dod[nxt, :]))
                    ds_rows += [dsb[2 * n + 2 + a, :, prevl] for a in range(2)]
                    p_rows += [pb[2 * n + 2 + a, :, prevl] for a in range(2)]
                dqd[cur, :] = jnp.where(low, dqs[0], dqs[1])
                dkd[cur, :] = _tn(jnp.concatenate(ds_rows, axis=0), jnp.concatenate(q_rows, axis=0))
                dvd[cur, :] = _tn(jnp.concatenate(p_rows, axis=0), jnp.concatenate(do_rows, axis=0))
                return carry

            lax.fori_loop(0, n_blk, probs, 0, unroll=ATTN_UNROLL)
            lax.fori_loop(0, n_blk, grads, 0, unroll=ATTN_UNROLL)
            _interleave_store(dqd, dqa, d, True)
            _interleave_store(dkd, dka, d, True)
            _interleave_store(dvd, dva, d, True)
        dq_ref[...] = _rot_t(dqa[...] * (HEAD_DIM ** -0.5), c, s1, s2)
        dkf = dka[...]
        dkf = _rot_t(dkf + pltpu.roll(dkf, HEAD_DIM, 1), c, s1, s2)
        dvf = dva[...]
        dvf = dvf + pltpu.roll(dvf, HEAD_DIM, 1)
        mine = (lax.broadcasted_iota(jnp.int32, (SEQ, LANES), 1) // HEAD_DIM) == kvh
        dkc_, dvc_ = jnp.where(mine, dkf, 0.0), jnp.where(mine, dvf, 0.0)

        @pl.when(j == 0)
        def _():
            dk_ref[...] = dkc_
            dv_ref[...] = dvc_

        @pl.when(j > 0)
        def _():
            dk_ref[...] += dkc_
            dv_ref[...] += dvc_

    def col(jj):
        return pl.BlockSpec((SEQ, LANES), lambda b, j: (b, jj if jj is not None else j))

    tab = pl.BlockSpec((SEQ, LANES), lambda b, j: (b, 0))
    fs = pltpu.VMEM((SEQ, LANES), f32)
    hs = pltpu.VMEM((SEQ, LANES), bf16)
    return pl.pallas_call(
        body, name=name, grid=(nb, n_j),
        in_specs=[col(None), col(n_j), col(n_j + 1), tab, tab, tab, col(None), col(None), col(None)],
        out_specs=[col(None), tab, tab],
        out_shape=[jax.ShapeDtypeStruct((t, ATTN_WIDTH), f32), jax.ShapeDtypeStruct((t, LANES), f32), jax.ShapeDtypeStruct((t, LANES), f32)],
        scratch_shapes=[fs, fs, fs, fs, hs, hs, hs, hs, fs, fs, fs, fs, fs, fs, fs, fs,
                        pltpu.VMEM((2 * n_blk + 2, ATTN_BLOCK, 2 * ATTN_BLOCK), bf16), pltpu.VMEM((2 * n_blk + 2, ATTN_BLOCK, 2 * ATTN_BLOCK), bf16)],
        compiler_params=_cparams(("parallel", "arbitrary")),
    )(qkv, qkv, qkv, *tabs, o, lse, do)


def _conv_pre(x, w_ref, b_ref, row):
    shifted = [x] + [jnp.where(row >= s, pltpu.roll(x, s, 0), 0.0) for s in range(1, CONV_WIDTH)]
    pre = b_ref[...] + w_ref[CONV_WIDTH - 1:CONV_WIDTH, :] * x
    for s in range(1, CONV_WIDTH):
        pre = pre + w_ref[CONV_WIDTH - 1 - s:CONV_WIDTH - s, :] * shifted[s]
    return pre, shifted


def _conv_fwd(x, w, b, name, tc=512):
    t, ch = x.shape

    def body(x_ref, w_ref, b_ref, o_ref):
        row = lax.broadcasted_iota(jnp.int32, (SEQ, tc), 0)
        pre, _ = _conv_pre(x_ref[...], w_ref, b_ref, row)
        o_ref[...] = _silu(pre)

    xs = pl.BlockSpec((SEQ, tc), lambda i, j: (i, j))
    return pl.pallas_call(
        body, name=name, grid=(t // SEQ, ch // tc),
        in_specs=[xs, pl.BlockSpec((CONV_WIDTH, tc), lambda i, j: (0, j)), pl.BlockSpec((1, tc), lambda i, j: (0, j))],
        out_specs=xs, out_shape=jax.ShapeDtypeStruct((t, ch), f32),
        compiler_params=_cparams(("parallel", "parallel")),
    )(x, w, b)


def _conv_bwd(x, w, b, dact, name, tc=512):
    t, ch = x.shape

    def body(x_ref, w_ref, b_ref, d_ref, dx_ref, dw_ref, db_ref):
        row = lax.broadcasted_iota(jnp.int32, (SEQ, tc), 0)
        pre, shifted = _conv_pre(x_ref[...], w_ref, b_ref, row)
        dpre = d_ref[...] * _dsilu(pre)
        dx = w_ref[CONV_WIDTH - 1:CONV_WIDTH, :] * dpre
        for s in range(1, CONV_WIDTH):
            dx = dx + w_ref[CONV_WIDTH - 1 - s:CONV_WIDTH - s, :] * jnp.where(row < SEQ - s, pltpu.roll(dpre, SEQ - s, 0), 0.0)
        dx_ref[...] = dx
        first = pl.program_id(1) == 0
        parts = [jnp.sum(dpre * shifted[CONV_WIDTH - 1 - k], axis=0, keepdims=True) for k in range(CONV_WIDTH)]
        dbp = jnp.sum(dpre, axis=0, keepdims=True)

        @pl.when(first)
        def _():
            for k in range(CONV_WIDTH):
                dw_ref[k:k + 1, :] = parts[k]
            db_ref[...] = dbp

        @pl.when(jnp.logical_not(first))
        def _():
            for k in range(CONV_WIDTH):
                dw_ref[k:k + 1, :] += parts[k]
            db_ref[...] += dbp

    xs = pl.BlockSpec((SEQ, tc), lambda j, i: (i, j))
    ws = pl.BlockSpec((CONV_WIDTH, tc), lambda j, i: (0, j))
    bs = pl.BlockSpec((1, tc), lambda j, i: (0, j))
    return pl.pallas_call(
        body, name=name, grid=(ch // tc, t // SEQ),
        in_specs=[xs, ws, bs, xs], out_specs=[xs, ws, bs],
        out_shape=[jax.ShapeDtypeStruct((t, ch), f32), jax.ShapeDtypeStruct((CONV_WIDTH, ch), f32), jax.ShapeDtypeStruct((1, ch), f32)],
        compiler_params=_cparams(("parallel", "arbitrary")),
    )(x, w, b, dact)


GROUP_W = SSM_INNER // SSM_GROUPS
HEADS_PER_GROUP = SSM_HEADS // SSM_GROUPS
HI = lax.Precision.HIGHEST


def _ssd_common(xbc_ref, dt_ref, bias_ref, alog_ref):
    r = lax.broadcasted_iota(jnp.int32, (CHUNK, CHUNK), 0)
    cidx = lax.broadcasted_iota(jnp.int32, (CHUNK, CHUNK), 1)
    causal = r >= cidx
    tril = causal.astype(f32)
    expand = (lax.broadcasted_iota(jnp.int32, (CHUNK, SSM_INNER), 0)
              == lax.broadcasted_iota(jnp.int32, (CHUNK, SSM_INNER), 1) // HEAD_DIM).astype(f32)
    head_lane = cidx < SSM_HEADS
    dtp = dt_ref[...] + bias_ref[...]
    dt = jnp.where(head_lane, _softplus(dtp), 0.0)
    a_neg = -jnp.exp(alog_ref[...])
    a = dt * a_neg
    cs = _nn(tril, a, HI)
    dt_e = _nn(dt, expand, HI)
    cs_e = _nn(cs, expand, HI)
    xs = xbc_ref[:, 0:SSM_INNER]
    xg = xs * dt_e
    ecs = jnp.exp(cs_e)
    cs_last = cs_e[CHUNK - 1:CHUNK, :]
    dse = jnp.exp(cs_last - cs_e)
    cde = jnp.exp(cs_last)
    return dict(r=r, cidx=cidx, causal=causal, tril=tril, expand=expand, head_lane=head_lane, dtp=dtp, dt=dt, a_neg=a_neg,
                cs=cs, cst=cs.T, dt_e=dt_e, cs_e=cs_e, xs=xs, xg=xg, ecs=ecs, dse=dse, cde=cde)


def _decay_mat(q, h):
    return jnp.exp(jnp.where(q["causal"], q["cs"][:, h:h + 1] - q["cst"][h:h + 1, :], NEG_INF))


def _gate_norm(y, z, nw):
    y2 = y * _silu(z)
    outs, xhats, rs = [], [], []
    for g in range(SSM_GROUPS):
        sl = slice(g * GROUP_W, (g + 1) * GROUP_W)
        yg = y2[:, sl]
        r = lax.rsqrt(jnp.mean(yg * yg, axis=-1, keepdims=True) + EPS)
        xhats.append(yg * r)
        rs.append(r)
        outs.append(yg * r * nw[:, sl])
    return y2, outs, xhats, rs


def _ssd_fwd(xbc, z, dtp, params, name):
    t = xbc.shape[0]
    n_chunk = SEQ // CHUNK
    low = None

    def body(xbc_ref, z_ref, dt_ref, bias_ref, alog_ref, dskip_ref, nw_ref, yn_ref, y_ref, hs_ref, h_scr):
        @pl.when(pl.program_id(1) == 0)
        def _():
            h_scr[...] = jnp.zeros_like(h_scr)

        q = _ssd_common(xbc_ref, dt_ref, bias_ref, alog_ref)
        low = lax.broadcasted_iota(jnp.int32, (CHUNK, LANES), 1) < HEAD_DIM
        xgb = q["xg"].astype(bf16)
        wst = (q["xg"] * q["dse"]).astype(bf16)
        hs_ref[0] = h_scr[...]
        ys = []
        for g in range(SSM_GROUPS):
            gl = slice(g * GROUP_W, (g + 1) * GROUP_W)
            bg = xbc_ref[:, SSM_INNER + g * D_STATE:SSM_INNER + (g + 1) * D_STATE].astype(bf16)
            cg = xbc_ref[:, SSM_INNER + SSM_GROUPS * D_STATE + g * D_STATE:SSM_INNER + SSM_GROUPS * D_STATE + (g + 1) * D_STATE].astype(bf16)
            cb = _nt(cg, bg)
            hg = h_scr[g]
            yoff = _nn(cg, hg.astype(bf16)) * q["ecs"][:, gl]
            pieces = []
            for i in range(HEADS_PER_GROUP // 2):
                h0 = g * HEADS_PER_GROUP + 2 * i
                xp = xgb[:, h0 * HEAD_DIM:(h0 + 2) * HEAD_DIM]
                m0 = (cb * _decay_mat(q, h0)).astype(bf16)
                m1 = (cb * _decay_mat(q, h0 + 1)).astype(bf16)
                zero = jnp.zeros_like(xp)
                pieces.append(_nn(m0, jnp.where(low, xp, zero)) + _nn(m1, jnp.where(low, zero, xp)))
            ys.append(jnp.concatenate(pieces, axis=1) + yoff + dskip_ref[:, gl] * q["xs"][:, gl])
            h_scr[g] = hg * q["cde"][:, gl] + _tn(bg, wst[:, gl])
        y = jnp.concatenate(ys, axis=1)
        y_ref[...] = y
        _, outs, _, _ = _gate_norm(y, z_ref[...], nw_ref[...])
        yn_ref[...] = jnp.concatenate(outs, axis=1).astype(bf16)

    def rows(w):
        return pl.BlockSpec((CHUNK, w), lambda b, c: (b * n_chunk + c, 0))

    def par(w):
        return pl.BlockSpec((1, w), lambda b, c: (0, 0))

    return pl.pallas_call(
        body, name=name, grid=(t // SEQ, n_chunk),
        in_specs=[rows(CONV_CH), rows(SSM_INNER), rows(LANES), par(LANES), par(LANES), par(SSM_INNER), par(SSM_INNER)],
        out_specs=[rows(SSM_INNER), rows(SSM_INNER), pl.BlockSpec((1, SSM_GROUPS, D_STATE, GROUP_W), lambda b, c: (b * n_chunk + c, 0, 0, 0))],
        out_shape=[jax.ShapeDtypeStruct((t, SSM_INNER), bf16), jax.ShapeDtypeStruct((t, SSM_INNER), f32),
                   jax.ShapeDtypeStruct((t // CHUNK, SSM_GROUPS, D_STATE, GROUP_W), f32)],
        scratch_shapes=[pltpu.VMEM((SSM_GROUPS, D_STATE, GROUP_W), f32)],
        compiler_params=_cparams(("parallel", "arbitrary")),
    )(xbc, z, dtp, *params)


def _ssd_bwd(xbc, z, dtp, y, hs, dyn, params, name):
    t = xbc.shape[0]
    n_chunk = SEQ // CHUNK

    def body(xbc_ref, z_ref, dt_ref, y_ref, hs_ref, dyn_ref, bias_ref, alog_ref, dskip_ref, nw_ref,
             dxbc_ref, dz_ref, ddt_ref, dnw_ref, dds_ref, dal_ref, dbi_ref, dh_scr):
        @pl.when(pl.program_id(1) == 0)
        def _():
            dh_scr[...] = jnp.zeros_like(dh_scr)

        q = _ssd_common(xbc_ref, dt_ref, bias_ref, alog_ref)
        low = lax.broadcasted_iota(jnp.int32, (CHUNK, LANES), 1) < HEAD_DIM
        last_row = lax.broadcasted_iota(jnp.int32, (CHUNK, GROUP_W), 0) == CHUNK - 1
        xs, xg = q["xs"], q["xg"]
        xgb = xg.astype(bf16)
        wf = xg * q["dse"]
        wst = wf.astype(bf16)
        zz = z_ref[...]
        yy = y_ref[...]
        sz = _silu(zz)
        y2, _, xhats, rs = _gate_norm(yy, zz, nw_ref[...])
        dyn_ = dyn_ref[...]
        dy2s, dnws = [], []
        for g in range(SSM_GROUPS):
            gl = slice(g * GROUP_W, (g + 1) * GROUP_W)
            gw = dyn_[:, gl] * nw_ref[:, gl]
            dy2s.append(rs[g] * (gw - xhats[g] * jnp.mean(gw * xhats[g], axis=-1, keepdims=True)))
            dnws.append(_rowsum8(dyn_[:, gl] * xhats[g]))
        dy2 = jnp.concatenate(dy2s, axis=1)
        dy = dy2 * sz
        dz_ref[...] = dy2 * yy * _dsilu(zz)
        dnw_p = jnp.concatenate(dnws, axis=1)
        dds_p = _rowsum8(dy * xs)
        dyb = dy.astype(bf16)
        gfull = (dy * q["ecs"]).astype(bf16)
        dcs_c = jnp.zeros((CHUNK, CHUNK), f32)
        dcs_r = jnp.zeros((CHUNK, CHUNK), f32)
        dcs_e_parts, dxg_parts = [], []
        for g in range(SSM_GROUPS):
            gl = slice(g * GROUP_W, (g + 1) * GROUP_W)
            bsl = slice(SSM_INNER + g * D_STATE, SSM_INNER + (g + 1) * D_STATE)
            csl = slice(SSM_INNER + SSM_GROUPS * D_STATE + g * D_STATE, SSM_INNER + SSM_GROUPS * D_STATE + (g + 1) * D_STATE)
            bg = xbc_ref[:, bsl].astype(bf16)
            cg = xbc_ref[:, csl].astype(bf16)
            cb = _nt(cg, bg)
            hg = hs_ref[0, g]
            hgb = hg.astype(bf16)
            dhn = dh_scr[g]
            dhnb = dhn.astype(bf16)
            yoff = _nn(cg, hgb) * q["ecs"][:, gl]
            dw_ = _nn(bg, dhnb)
            r_e = dw_ * wf[:, gl]
            to_last = jnp.sum(r_e, axis=0, keepdims=True) + jnp.sum(dhn * hg, axis=0, keepdims=True) * q["cde"][:, gl]
            dcs_e_parts.append(dy[:, gl] * yoff - r_e + jnp.where(last_row, to_last, 0.0))
            dcb = jnp.zeros((CHUNK, CHUNK), f32)
            dxg_pairs = []
            for i in range(HEADS_PER_GROUP // 2):
                h0 = g * HEADS_PER_GROUP + 2 * i
                psl = slice(h0 * HEAD_DIM, (h0 + 2) * HEAD_DIM)
                xp = xgb[:, psl]
                dyp = dyb[:, psl]
                zero = jnp.zeros_like(dyp)
                tns = []
                for a in range(2):
                    h = h0 + a
                    lm = _decay_mat(q, h)
                    m = cb * lm
                    dm = _nt(jnp.where(low, dyp, zero) if a == 0 else jnp.where(low, zero, dyp), xp)
                    dcb = dcb + dm * lm
                    nmat = dm * m
                    dcs_c = dcs_c + jnp.where(q["cidx"] == h, jnp.sum(nmat, axis=1, keepdims=True), 0.0)
                    dcs_r = dcs_r + jnp.where(q["r"] == h, jnp.sum(nmat, axis=0, keepdims=True), 0.0)
                    tns.append(_tn(m.astype(bf16), dyp))
                dxg_pairs.append(jnp.where(low, tns[0], tns[1]))
            dxg_parts.append(jnp.concatenate(dxg_pairs, axis=1) + dw_ * q["dse"][:, gl])
            dcbb = dcb.astype(bf16)
            dxbc_ref[:, csl] = _nt(gfull[:, gl], hgb) + _nn(dcbb, bg)
            dxbc_ref[:, bsl] = _nt(wst[:, gl], dhnb) + _tn(dcbb, cg)
            dh_scr[g] = dhn * q["cde"][:, gl] + _tn(cg, gfull[:, gl])
        dxg = jnp.concatenate(dxg_parts, axis=1)
        dcs_e = jnp.concatenate(dcs_e_parts, axis=1)
        dxbc_ref[:, 0:SSM_INNER] = dskip_ref[...] * dy + dxg * q["dt_e"]
        dcs = dcs_c - dcs_r.T + _dot(dcs_e, q["expand"], ((1,), (1,)), HI)
        triu = (q["cidx"] >= q["r"]).astype(f32)
        da = _nn(triu, dcs, HI)
        ddt = _dot(dxg * xs, q["expand"], ((1,), (1,)), HI) + da * q["a_neg"]
        ddtp = jnp.where(q["head_lane"], ddt * _sigmoid(q["dtp"]), 0.0)
        ddt_ref[...] = ddtp
        dal_p = _rowsum8(da * q["dt"]) * q["a_neg"]
        dbi_p = _rowsum8(ddtp)
        first = (pl.program_id(0) == 0) & (pl.program_id(1) == 0)

        @pl.when(first)
        def _():
            dnw_ref[...] = dnw_p
            dds_ref[...] = dds_p
            dal_ref[...] = dal_p
            dbi_ref[...] = dbi_p

        @pl.when(jnp.logical_not(first))
        def _():
            dnw_ref[...] += dnw_p
            dds_ref[...] += dds_p
            dal_ref[...] += dal_p
            dbi_ref[...] += dbi_p

    def rows(w):
        return pl.BlockSpec((CHUNK, w), lambda b, c: (b * n_chunk + n_chunk - 1 - c, 0))

    def par(w):
        return pl.BlockSpec((1, w), lambda b, c: (0, 0))

    def acc(w):
        return pl.BlockSpec((SUBLANES, w), lambda b, c: (0, 0))

    return pl.pallas_call(
        body, name=name, grid=(t // SEQ, n_chunk),
        in_specs=[rows(CONV_CH), rows(SSM_INNER), rows(LANES), rows(SSM_INNER),
                  pl.BlockSpec((1, SSM_GROUPS, D_STATE, GROUP_W), lambda b, c: (b * n_chunk + n_chunk - 1 - c, 0, 0, 0)),
                  rows(SSM_INNER), par(LANES), par(LANES), par(SSM_INNER), par(SSM_INNER)],
        out_specs=[rows(CONV_CH), rows(SSM_INNER), rows(LANES), acc(SSM_INNER), acc(SSM_INNER), acc(LANES), acc(LANES)],
        out_shape=[jax.ShapeDtypeStruct((t, CONV_CH), f32), jax.ShapeDtypeStruct((t, SSM_INNER), f32), jax.ShapeDtypeStruct((t, LANES), f32),
                   jax.ShapeDtypeStruct((SUBLANES, SSM_INNER), f32), jax.ShapeDtypeStruct((SUBLANES, SSM_INNER), f32),
                   jax.ShapeDtypeStruct((SUBLANES, LANES), f32), jax.ShapeDtypeStruct((SUBLANES, LANES), f32)],
        scratch_shapes=[pltpu.VMEM((SSM_GROUPS, D_STATE, GROUP_W), f32)],
        compiler_params=_cparams(("arbitrary", "arbitrary")),
    )(xbc, z, dtp, y, hs, dyn, *params)


def _adamw(g_parts, w, m, v, name, tr=None):
    rows, width = w.shape
    n = len(g_parts)
    if tr is None:
        tr = _row_tile(rows)

    def body(*refs):
        g_refs, (w_ref, m_ref, v_ref, g_out, d_out, m_out, v_out) = refs[:n], refs[n:]

        def part(i):
            return (g_refs[i][...] if g_parts[i][1] is None else g_refs[i][0]).astype(f32)

        g = part(0)
        for i in range(1, n):
            g = g + part(i)
        mm = ADAM_B1 * m_ref[...] + (1.0 - ADAM_B1) * g
        vv = ADAM_B2 * v_ref[...] + (1.0 - ADAM_B2) * (g * g)
        m_hat = mm / (1.0 - ADAM_B1 ** ADAM_STEP)
        v_hat = vv / (1.0 - ADAM_B2 ** ADAM_STEP)
        g_out[...] = g
        d_out[...] = -ADAM_LR * (m_hat / (jnp.sqrt(v_hat) + ADAM_EPS) + ADAM_WD * w_ref[...])
        m_out[...] = mm
        v_out[...] = vv

    spec = pl.BlockSpec((tr, width), lambda i: (i, 0))

    def gspec(idx):
        return spec if idx is None else pl.BlockSpec((1, tr, width), lambda i: (idx, i, 0))

    return pl.pallas_call(
        body, name=name, grid=(rows // tr,), in_specs=[gspec(idx) for _, idx in g_parts] + [spec] * 3, out_specs=[spec] * 4,
        out_shape=[jax.ShapeDtypeStruct((rows, width), f32)] * 4, compiler_params=_cparams(("parallel",)),
    )(*[a for a, _ in g_parts], w, m, v)


def _row_tile(rows, cap=512):
    for cand in range(min(rows, cap) // SUBLANES * SUBLANES, 0, -SUBLANES):
        if rows % cand == 0:
            return cand
    return rows


def _cols_from_devices(g, width, name):
    n_dev, depth, a, b = g.shape

    def body(g_ref, o_ref):
        for i in range(n_dev):
            o_ref[0, :, i * b:(i + 1) * b] = g_ref[i, 0]
        if width > n_dev * b:
            o_ref[0, :, n_dev * b:width] = jnp.zeros((a, width - n_dev * b), o_ref.dtype)

    return pl.pallas_call(
        body, name=name, grid=(depth,), in_specs=[pl.BlockSpec((n_dev, 1, a, b), lambda l: (0, l, 0, 0))],
        out_specs=pl.BlockSpec((1, a, width), lambda l: (l, 0, 0)), out_shape=jax.ShapeDtypeStruct((depth, a, width), g.dtype),
        compiler_params=_cparams(("parallel",)),
    )(g)


def _devices_from_cols(per_layer, b, name, tr=256):
    depth = len(per_layer)
    a, width = per_layer[0].shape

    def body(*refs):
        o_ref = refs[depth]
        for l in range(depth):
            for i in range(N_DEV):
                o_ref[i, l] = refs[l][:, i * b:(i + 1) * b]

    return pl.pallas_call(
        body, name=name, grid=(a // tr,), in_specs=[pl.BlockSpec((tr, width), lambda r: (r, 0))] * depth,
        out_specs=pl.BlockSpec((N_DEV, depth, tr, b), lambda r: (0, 0, r, 0)),
        out_shape=jax.ShapeDtypeStruct((N_DEV, depth, a, b), per_layer[0].dtype), compiler_params=_cparams(("parallel",)),
    )(*per_layer)


def _add_kept(g, recv, core, name, out_dtype=bf16):
    nblk, _, rows, width = g.shape
    tr = _row_tile(rows)

    def body(c_ref, g_ref, r_ref, o_ref):
        o_ref[0] = (g_ref[0, 0] + r_ref[0]).astype(out_dtype)

    grid_spec = pltpu.PrefetchScalarGridSpec(
        num_scalar_prefetch=1, grid=(nblk, rows // tr),
        in_specs=[pl.BlockSpec((1, 1, tr, width), lambda i, j, c: (i, c[0], j, 0)), pl.BlockSpec((1, tr, width), lambda i, j, c: (i, j, 0))],
        out_specs=pl.BlockSpec((1, tr, width), lambda i, j, c: (i, j, 0)))
    return pl.pallas_call(
        body, name=name, grid_spec=grid_spec, out_shape=jax.ShapeDtypeStruct((nblk, rows, width), out_dtype),
        compiler_params=_cparams(("parallel", "parallel")),
    )(core, g, recv)


def _me():
    return lax.axis_index("x"), lax.axis_index("y"), lax.axis_index("c")


def _allgather_two_level(shards, name):
    n = len(shards)
    per = 7

    def body(*refs):
        ins, outs = refs[:n], refs[n:2 * n]
        send_sems, recv_sems, local_sems = refs[2 * n:]
        x, y, c = _me()
        me, sibling = (x, y, c), (x, y, 1 - c)
        chips = [(1 - x, y), (x, 1 - y), (1 - x, 1 - y)]

        def slot(a, p):
            return outs[a].at[4 * p[0] + 2 * p[1] + p[2]]

        def copy(a, k, block, to, src=None):
            return pltpu.make_async_remote_copy(
                src_ref=slot(a, block) if src is None else src, dst_ref=slot(a, block),
                send_sem=send_sems.at[a * per + k], recv_sem=recv_sems.at[a * per + k], device_id=to, device_id_type=MESH)

        mine = [pltpu.make_async_copy(ins[a], slot(a, me), local_sems.at[a]) for a in range(n)]
        for cp in mine:
            cp.start()
        first = []
        for a in range(n):
            first.append(copy(a, 0, me, sibling, src=ins[a]))
            first += [copy(a, 1 + j, me, (*chip, c), src=ins[a]) for j, chip in enumerate(chips)]
        for cp in first:
            cp.start()
        passed = []
        for j, chip in enumerate(chips):
            for a in range(n):
                copy(a, 1 + j, (*chip, c), me).wait_recv()
                fwd = copy(a, 4 + j, (*chip, c), sibling)
                fwd.start()
                passed.append(fwd)
        for a in range(n):
            copy(a, 0, sibling, me).wait_recv()
            for j, chip in enumerate(chips):
                copy(a, 4 + j, (*chip, 1 - c), me).wait_recv()
        for cp in first + passed:
            cp.wait_send()
        for cp in mine:
            cp.wait()

    return pl.pallas_call(
        body, name=name, in_specs=[ANY] * n, out_specs=[ANY] * n,
        out_shape=[jax.ShapeDtypeStruct((N_DEV,) + s.shape, s.dtype) for s in shards],
        scratch_shapes=[pltpu.SemaphoreType.DMA((n * per,)), pltpu.SemaphoreType.DMA((n * per,)), pltpu.SemaphoreType.DMA((n,))],
    )(*shards)


def _allgather_direct(row, name):
    def body(in_ref, out_ref, send_sems, recv_sems, local_sem):
        x, y, c = _me()
        mine = out_ref.at[4 * x + 2 * y + c]
        local = pltpu.make_async_copy(in_ref, mine, local_sem)
        local.start()
        sends = []
        for k in range(1, N_DEV):
            px, py, pc = x ^ (k >> 2), y ^ ((k >> 1) & 1), c ^ (k & 1)
            sends.append(pltpu.make_async_remote_copy(
                src_ref=in_ref, dst_ref=mine, send_sem=send_sems.at[k - 1], recv_sem=recv_sems.at[k - 1],
                device_id=(px, py, pc), device_id_type=MESH))
        for cp in sends:
            cp.start()
        for k in range(1, N_DEV):
            px, py, pc = x ^ (k >> 2), y ^ ((k >> 1) & 1), c ^ (k & 1)
            theirs = out_ref.at[4 * px + 2 * py + pc]
            pltpu.make_async_remote_copy(
                src_ref=in_ref, dst_ref=theirs, send_sem=send_sems.at[k - 1], recv_sem=recv_sems.at[k - 1],
                device_id=(px, py, pc), device_id_type=MESH).wait_recv()
        for cp in sends:
            cp.wait_send()
        local.wait()

    return pl.pallas_call(
        body, name=name, in_specs=[ANY], out_specs=ANY, out_shape=jax.ShapeDtypeStruct((N_DEV,) + row.shape, row.dtype),
        scratch_shapes=[pltpu.SemaphoreType.DMA((N_DEV - 1,)), pltpu.SemaphoreType.DMA((N_DEV - 1,)), pltpu.SemaphoreType.DMA],
    )(row)


N_CHIP = N_DEV // 2


def _exchange_sibling(gs, name):
    n = len(gs)

    def body(*refs):
        ins, outs = refs[:n], refs[n:2 * n]
        send_sems, recv_sems = refs[2 * n:]
        x, y, c = _me()
        copies = [pltpu.make_async_remote_copy(
            src_ref=ins[a].at[i, 1 - c], dst_ref=outs[a].at[i], send_sem=send_sems.at[a * N_CHIP + i], recv_sem=recv_sems.at[a * N_CHIP + i],
            device_id=(x, y, 1 - c), device_id_type=MESH) for a in range(n) for i in range(N_CHIP)]
        for cp in copies:
            cp.start()
        for cp in copies:
            cp.wait_recv()
        for cp in copies:
            cp.wait_send()

    return pl.pallas_call(
        body, name=name, in_specs=[ANY] * n, out_specs=[ANY] * n,
        out_shape=[jax.ShapeDtypeStruct((N_CHIP,) + g.shape[2:], g.dtype) for g in gs],
        scratch_shapes=[pltpu.SemaphoreType.DMA((n * N_CHIP,)), pltpu.SemaphoreType.DMA((n * N_CHIP,))],
    )(*gs)


def _exchange_chips(ps, name):
    n = len(ps)

    def body(*refs):
        ins, outs = refs[:n], refs[n:2 * n]
        send_sems, recv_sems = refs[2 * n:]
        x, y, c = _me()
        chips = [(1 - x, y), (x, 1 - y), (1 - x, 1 - y)]
        copies = [pltpu.make_async_remote_copy(
            src_ref=ins[a].at[2 * cx + cy], dst_ref=outs[a].at[k], send_sem=send_sems.at[a * 3 + k], recv_sem=recv_sems.at[a * 3 + k],
            device_id=(cx, cy, c), device_id_type=MESH) for a in range(n) for k, (cx, cy) in enumerate(chips)]
        for cp in copies:
            cp.start()
        for cp in copies:
            cp.wait_recv()
        for cp in copies:
            cp.wait_send()

    return pl.pallas_call(
        body, name=name, in_specs=[ANY] * n, out_specs=[ANY] * n,
        out_shape=[jax.ShapeDtypeStruct((3,) + p.shape[1:], p.dtype) for p in ps],
        scratch_shapes=[pltpu.SemaphoreType.DMA((n * 3,)), pltpu.SemaphoreType.DMA((n * 3,))],
    )(*ps)


def _row(v, width=None):
    v = v.reshape(1, -1).astype(f32)
    if width is not None and v.shape[1] < width:
        v = jnp.pad(v, ((0, 0), (0, width - v.shape[1])))
    return v


def _layer_params(p, l):
    return dict(
        norm_mix=_row(p["norm_mix"][l]), norm_ffn=_row(p["norm_ffn"][l]), conv_w=p["conv_w"][l], conv_b=_row(p["conv_b"][l]),
        ssd=(_row(p["dt_bias"][l], LANES), _row(p["a_log"][l], LANES), _row(jnp.repeat(p["d_skip"][l], HEAD_DIM)), _row(p["ssm_norm"][l])))


def _layer_fwd(h, big, sp, tabs, l):
    tag = f"l{l}_"
    w_in, w_out, w_gate, w_up, w_down = big
    hn = _rmsnorm_fwd(h, sp["norm_mix"], tag + "norm_mix")
    qkv = _matmul(hn, w_in, mode="nn", n_out=QKV_WIDTH, tn=256, b_off=0, name=tag + "proj_qkv")
    z = _matmul(hn, w_in, mode="nn", n_out=SSM_INNER, tn=256, b_off=Z_OFF // 256, name=tag + "proj_z")
    xbc_pre = _matmul(hn, w_in, mode="nn", n_out=CONV_CH, tn=256, b_off=XBC_OFF // 256, name=tag + "proj_xbc")
    dtp = _matmul(hn, w_in, mode="nn", n_out=LANES, tn=LANES, b_off=DT_OFF // LANES, name=tag + "proj_dt")
    o, lse = _attn_fwd(qkv, tabs, tag + "attn_fwd")
    xbc = _conv_fwd(xbc_pre, sp["conv_w"], sp["conv_b"], tag + "conv_fwd")
    yn, y, hs = _ssd_fwd(xbc, z, dtp, sp["ssd"], tag + "ssd_fwd")
    t1 = _matmul(o, w_out, mode="nn", k_len=ATTN_WIDTH, tk=512, add=h, name=tag + "out_attn")
    h2 = _matmul(yn, w_out, mode="nn", k_len=SSM_INNER, tk=512, b_koff=1, add=t1, name=tag + "out_ssm")
    hn2 = _rmsnorm_fwd(h2, sp["norm_ffn"], tag + "norm_ffn")
    g, u, act = _swiglu_fwd(hn2, w_gate, w_up, tag + "ffn_up")
    h3 = _matmul(act, w_down, mode="nn", tk=1408, add=h2, name=tag + "ffn_down")
    saved = dict(h=h, hn=hn, qkv=qkv, z=z, xbc_pre=xbc_pre, dtp=dtp, o=o, lse=lse, xbc=xbc, yn=yn, y=y, hs=hs, h2=h2, hn2=hn2, g=g, u=u, act=act)
    return h3, saved


def _layer_bwd(dh3, s, big, sp, tabs, l):
    tag = f"l{l}_"
    w_in, w_out, w_gate, w_up, w_down = big
    dg, du = _swiglu_bwd(dh3, w_down, s["g"], s["u"], tag + "ffn_down_bwd")
    dw_down = _matmul(s["act"], dh3, mode="tn", tm=1408, tn=512, tk=2048, name=tag + "dw_down")
    dhn2 = _matmul(dg, w_gate, mode="nt", tk=1408, name=tag + "ffn_gate_bwd")
    dhn2 = _matmul(du, w_up, mode="nt", tk=1408, add=dhn2, name=tag + "ffn_up_bwd")
    dw_gate = _matmul(s["hn2"], dg, mode="tn", tm=512, tn=1408, tk=2048, name=tag + "dw_gate")
    dw_up = _matmul(s["hn2"], du, mode="tn", tm=512, tn=1408, tk=2048, name=tag + "dw_up")
    dh2, dnf = _rmsnorm_bwd(dhn2, s["h2"], sp["norm_ffn"], dh3, tag + "norm_ffn_bwd")
    d_o = _matmul(dh2, w_out, mode="nt", n_out=ATTN_WIDTH, tn=512, b_off=0, name=tag + "out_attn_bwd")
    dyn = _matmul(dh2, w_out, mode="nt", n_out=SSM_INNER, tn=512, b_off=1, name=tag + "out_ssm_bwd")
    dw_out = jnp.concatenate([_matmul(s["o"], dh2, mode="tn", tm=512, tn=512, tk=2048, name=tag + "dw_out_attn"),
                              _matmul(s["yn"], dh2, mode="tn", tm=512, tn=512, tk=2048, name=tag + "dw_out_ssm")], axis=0)
    dxbc, dz, ddtp, dnw, dds, dal, dbi = _ssd_bwd(s["xbc"], s["z"], s["dtp"], s["y"], s["hs"], dyn, sp["ssd"], tag + "ssd_bwd")
    dxbc_pre, dconv_w, dconv_b = _conv_bwd(s["xbc_pre"], sp["conv_w"], sp["conv_b"], dxbc, tag + "conv_bwd")
    dq, dk, dv = _attn_bwd(s["qkv"], tabs, s["o"], s["lse"], d_o, tag + "attn_bwd")
    dproj = jnp.concatenate([dq.astype(bf16), dk.astype(bf16), dv.astype(bf16), dz.astype(bf16), dxbc_pre.astype(bf16), ddtp.astype(bf16)], axis=1)
    dhn = _matmul(dproj, w_in, mode="nt", tk=1152, name=tag + "proj_bwd")
    dw_in = _matmul(s["hn"], dproj, mode="tn", tm=512, tn=1152, tk=2048, name=tag + "dw_in")
    dh, dnm = _rmsnorm_bwd(dhn, s["h"], sp["norm_mix"], dh2, tag + "norm_mix_bwd")
    grads = dict(
        norm_mix=dnm.sum(0), w_in=dw_in, conv_w=dconv_w, conv_b=dconv_b[0], dt_bias=dbi.sum(0)[:SSM_HEADS], a_log=dal.sum(0)[:SSM_HEADS],
        d_skip=dds.sum(0).reshape(SSM_HEADS, HEAD_DIM).sum(1), ssm_norm=dnw.sum(0), w_out=dw_out, norm_ffn=dnf.sum(0),
        w_gate=dw_gate, w_up=dw_up, w_down=dw_down)
    return dh, grads


def _local_step(x, positions, target, p, bigs):
    tabs = _rope_tables(positions.reshape(-1, 1), "rope_tables")
    h = x
    saved, sps = [], []
    for l in range(DEPTH):
        sps.append(_layer_params(p, l))
        h, s = _layer_fwd(h, bigs[l], sps[l], tabs, l)
        saved.append(s)
    dh, loss_parts, dfn = _final_loss(h, _row(p["final_norm"]), target, "final_loss")
    layer_grads = [None] * DEPTH
    for l in reversed(range(DEPTH)):
        dh, layer_grads[l] = _layer_bwd(dh, saved[l], bigs[l], sps[l], tabs, l)
    grads = {k: [layer_grads[l][k] for l in range(DEPTH)] for k in layer_grads[0]}
    grads["final_norm"] = dfn.sum(0)
    return jnp.sum(loss_parts), dh, grads


BIG = ("w_in", "w_out", "w_gate", "w_up", "w_down")
COL_SHARDED = ("w_in", "w_gate", "w_up")
SMALL = ("norm_mix", "conv_b", "dt_bias", "a_log", "d_skip", "ssm_norm", "norm_ffn", "final_norm")
WEIGHTS = ("norm_mix", "w_in", "conv_w", "conv_b", "dt_bias", "a_log", "d_skip", "ssm_norm", "w_out", "norm_ffn", "w_gate", "w_up", "w_down", "final_norm")
PACK_W = 1024
SMALL_ROWS = 88
CONVW_ROWS = 96
CONVW_SHARD_ROWS = 16


def _full_from_gathered(name, g):
    _, depth, a, b = g.shape
    if name in COL_SHARDED:
        return _cols_from_devices(g, IN_PROJ_PAD if name == "w_in" else N_DEV * b, "cols_" + name)
    return jnp.transpose(g, (1, 0, 2, 3)).reshape(depth, N_DEV * a, b)


def _by_device(name, per_layer, shard_shape):
    depth, a, b = shard_shape
    if name in COL_SHARDED:
        t = _devices_from_cols(per_layer, b, "devs_" + name)
    else:
        t = jnp.stack([q.reshape(N_DEV, a, b) for q in per_layer], axis=1)
    return t.reshape(N_CHIP, 2, depth * a, b)


def _pack_rows(parts, rows, width):
    flat = jnp.concatenate([q.reshape(-1) for q in parts])
    return jnp.pad(flat, (0, rows * width - flat.shape[0])).reshape(rows, width)


def _unpack(flat, like):
    out, off = [], 0
    for q in like:
        out.append(flat[off:off + q.size].reshape(q.shape))
        off += q.size
    return out


def kernel(x, positions, norm_mix, w_in, conv_w, conv_b, dt_bias, a_log, d_skip, ssm_norm, w_out, norm_ffn, w_gate, w_up, w_down, final_norm, loss_target, m_norm_mix, m_w_in, m_conv_w, m_conv_b, m_dt_bias, m_a_log, m_d_skip, m_ssm_norm, m_w_out, m_norm_ffn, m_w_gate, m_w_up, m_w_down, m_final_norm, v_norm_mix, v_w_in, v_conv_w, v_conv_b, v_dt_bias, v_a_log, v_d_skip, v_ssm_norm, v_w_out, v_norm_ffn, v_w_gate, v_w_up, v_w_down, v_final_norm):
    w = dict(norm_mix=norm_mix, w_in=w_in, conv_w=conv_w, conv_b=conv_b, dt_bias=dt_bias, a_log=a_log, d_skip=d_skip, ssm_norm=ssm_norm,
             w_out=w_out, norm_ffn=norm_ffn, w_gate=w_gate, w_up=w_up, w_down=w_down, final_norm=final_norm)
    m = dict(norm_mix=m_norm_mix, w_in=m_w_in, conv_w=m_conv_w, conv_b=m_conv_b, dt_bias=m_dt_bias, a_log=m_a_log, d_skip=m_d_skip,
             ssm_norm=m_ssm_norm, w_out=m_w_out, norm_ffn=m_norm_ffn, w_gate=m_w_gate, w_up=m_w_up, w_down=m_w_down, final_norm=m_final_norm)
    v = dict(norm_mix=v_norm_mix, w_in=v_w_in, conv_w=v_conv_w, conv_b=v_conv_b, dt_bias=v_dt_bias, a_log=v_a_log, d_skip=v_d_skip,
             ssm_norm=v_ssm_norm, w_out=v_w_out, norm_ffn=v_norm_ffn, w_gate=v_w_gate, w_up=v_w_up, w_down=v_w_down, final_norm=v_final_norm)
    ax, ay, ac = lax.axis_index("x"), lax.axis_index("y"), lax.axis_index("c")
    dev = 4 * ax + 2 * ay + ac

    gathered = _allgather_two_level([w[k].astype(bf16) for k in BIG] + [w["conv_w"]], "gather_weights")
    full = {k: _full_from_gathered(k, g) for k, g in zip(BIG, gathered[:len(BIG)])}
    p = {k: w[k] for k in SMALL}
    p["conv_w"] = jnp.transpose(gathered[-1], (1, 2, 0, 3)).reshape(DEPTH, CONV_WIDTH, CONV_CH)
    bigs = [tuple(full[k][l] for k in BIG) for l in range(DEPTH)]

    t = x.shape[0] * x.shape[1]
    loss_local, dx, grads = _local_step(x.reshape(t, D_MODEL), positions.reshape(t), loss_target.reshape(t, D_MODEL), p, bigs)

    core = ac.reshape(1).astype(jnp.int32)
    by_dev = [_by_device(k, grads[k], w[k].shape) for k in BIG]
    from_sibling = _exchange_sibling(by_dev, "scatter_sibling")
    chip_sums = [_add_kept(g, r, core, "scatter_add_" + k) for k, g, r in zip(BIG, by_dev, from_sibling)]
    from_chips = _exchange_chips(chip_sums, "scatter_chips")
    out_g, out_d, out_m, out_v = {}, {}, {}, {}
    for k, cs, fc in zip(BIG, chip_sums, from_chips):
        own = lax.dynamic_index_in_dim(cs, 2 * ax + ay, 0, keepdims=False)
        rows2d = (cs.shape[1], cs.shape[2])
        res = _adamw([(own, None), (fc, 0), (fc, 1), (fc, 2)], w[k].reshape(rows2d), m[k].reshape(rows2d), v[k].reshape(rows2d), "adamw_" + k)
        for dst, src in zip((out_g, out_d, out_m, out_v), res):
            dst[k] = src.reshape(w[k].shape)

    small_like = [w[k] for k in SMALL]
    small_grads = [jnp.stack(grads[k]) if k != "final_norm" else grads[k] for k in SMALL]
    small_pack = jnp.concatenate([_pack_rows(small_grads, SMALL_ROWS, LANES), _pack_rows([jnp.stack(grads["conv_w"])], CONVW_ROWS, LANES)], axis=0)
    parts = _allgather_direct(small_pack, "gather_small_grads")
    g_s, d_s, m_s, v_s = _adamw(
        [(parts[i, :SMALL_ROWS], None) for i in range(N_DEV)], _pack_rows(small_like, SMALL_ROWS, LANES),
        _pack_rows([m[k] for k in SMALL], SMALL_ROWS, LANES), _pack_rows([v[k] for k in SMALL], SMALL_ROWS, LANES), "adamw_replicated")
    for dst, src in ((out_g, g_s), (out_d, d_s), (out_m, m_s), (out_v, v_s)):
        dst.update(zip(SMALL, _unpack(src.reshape(-1), small_like)))
    shard_w = conv_w.shape[-1]
    conv_parts = parts[:, SMALL_ROWS:].reshape(N_DEV, DEPTH, CONV_WIDTH, CONV_CH)
    conv_mine = lax.dynamic_slice_in_dim(conv_parts, dev * shard_w, shard_w, axis=3)
    g_c, d_c, m_c, v_c = _adamw(
        [(_pack_rows([conv_mine[i]], CONVW_SHARD_ROWS, LANES), None) for i in range(N_DEV)], _pack_rows([conv_w], CONVW_SHARD_ROWS, LANES),
        _pack_rows([m["conv_w"]], CONVW_SHARD_ROWS, LANES), _pack_rows([v["conv_w"]], CONVW_SHARD_ROWS, LANES), "adamw_conv_w")
    for dst, src in ((out_g, g_c), (out_d, d_c), (out_m, m_c), (out_v, v_c)):
        dst["conv_w"] = src.reshape(-1)[:conv_w.size].reshape(conv_w.shape)

    loss = lax.psum(loss_local, ("x", "y", "c"))
    return (loss, dx.reshape(x.shape), *[out_g[k] for k in WEIGHTS], *[out_d[k] for k in WEIGHTS],
            *[out_m[k] for k in WEIGHTS], *[out_v[k] for k in WEIGHTS])
```

```python
import functools
import math

import jax
import jax.numpy as jnp
import numpy as np
from jax import lax
from jax.experimental import pallas as pl
from jax.experimental.pallas import tpu as pltpu

f32 = jnp.float32
bf16 = jnp.bfloat16

D_MODEL = 1024
SEQ = 2048
DEPTH = 2
HEAD_DIM = 64
N_ATTN_HEADS = 8
N_KV_HEADS = 2
ATTN_WIDTH = 512
KV_WIDTH = 128
ROPE_DIM = 16
ROPE_THETA = 500000.0
DILATIONS = (1, 4, 16)
ATTN_BLOCK = 128
SSM_HEADS = 16
SSM_INNER = 1024
SSM_GROUPS = 2
D_STATE = 128
CONV_WIDTH = 4
CHUNK = 128
CONV_CH = 1536
MIX_WIDTH = 1536
QKV_WIDTH = ATTN_WIDTH + 2 * KV_WIDTH
Z_OFF = 768
XBC_OFF = 1792
DT_OFF = 3328
IN_PROJ = 3344
IN_PROJ_PAD = 3456
FFN_HIDDEN = 2816
EPS = 1e-5
N_DEV = 8
ADAM_LR = 0.001
ADAM_B1 = 0.9
ADAM_B2 = 0.999
ADAM_EPS = 1e-08
ADAM_WD = 0.01
ADAM_STEP = 10

LANES = 128
SUBLANES = 8
VMEM_LIMIT = 56 * 1024 * 1024

MESH = pl.DeviceIdType.MESH
ANY = pl.BlockSpec(memory_space=pl.ANY)


def _cparams(sem, vmem=None):
    return pltpu.CompilerParams(dimension_semantics=sem, vmem_limit_bytes=vmem or VMEM_LIMIT)


def _sigmoid(x):
    return 1.0 / (1.0 + jnp.exp(-x))


def _silu(x):
    return x * _sigmoid(x)


def _dsilu(x):
    s = _sigmoid(x)
    return s * (1.0 + x * (1.0 - s))


def _softplus(x):
    return jnp.maximum(x, 0.0) + jnp.log(1.0 + jnp.exp(-jnp.abs(x)))


def _dot(a, b, dims, precision=None):
    return lax.dot_general(a, b, (dims, ((), ())), preferred_element_type=f32, precision=precision)


def _nn(a, b, precision=None):
    return _dot(a, b, ((1,), (0,)), precision)


def _nt(a, b):
    return _dot(a, b, ((1,), (1,)))


def _tn(a, b):
    return _dot(a, b, ((0,), (0,)))


def _rowsum8(t):
    n, w = t.shape
    return jnp.sum(t.reshape(n // SUBLANES, SUBLANES, w), axis=0)


def _matmul(a, b, *, mode, n_out=None, b_off=0, a_koff=0, b_koff=0, k_len=None, add=None, out_dtype=f32, tm=2048, tn=512, tk=1024, name):
    if mode == "tn":
        kdim_a, m = a.shape
    else:
        m, kdim_a = a.shape
    kk = k_len if k_len is not None else kdim_a
    n = n_out if n_out is not None else (b.shape[0] if mode == "nt" else b.shape[1])
    tm, tn, tk = min(tm, m), min(tn, n), min(tk, kk)
    assert m % tm == 0 and n % tn == 0 and kk % tk == 0, (name, m, n, kk, tm, tn, tk)
    nk = kk // tk
    if mode == "nn":
        a_spec = pl.BlockSpec((tm, tk), lambda i, j, k: (i, k + a_koff))
        b_spec = pl.BlockSpec((tk, tn), lambda i, j, k: (k + b_koff, j + b_off))
        dims = ((1,), (0,))
    elif mode == "nt":
        a_spec = pl.BlockSpec((tm, tk), lambda i, j, k: (i, k + a_koff))
        b_spec = pl.BlockSpec((tn, tk), lambda i, j, k: (j + b_off, k + b_koff))
        dims = ((1,), (1,))
    else:
        a_spec = pl.BlockSpec((tk, tm), lambda i, j, k: (k + a_koff, i))
        b_spec = pl.BlockSpec((tk, tn), lambda i, j, k: (k + b_koff, j + b_off))
        dims = ((0,), (0,))
    o_spec = pl.BlockSpec((tm, tn), lambda i, j, k: (i, j))
    has_add = add is not None

    def body(*refs):
        if has_add:
            a_ref, b_ref, add_ref, o_ref, acc_ref = refs
        else:
            a_ref, b_ref, o_ref, acc_ref = refs
        k = pl.program_id(2)
        part = _dot(a_ref[...].astype(bf16), b_ref[...].astype(bf16), dims)

        @pl.when(k == 0)
        def _():
            acc_ref[...] = part

        @pl.when(k > 0)
        def _():
            acc_ref[...] += part

        @pl.when(k == nk - 1)
        def _():
            r = acc_ref[...]
            if has_add:
                r = r + add_ref[...]
            o_ref[...] = r.astype(out_dtype)

    in_specs = [a_spec, b_spec] + ([o_spec] if has_add else [])
    args = (a, b) + ((add,) if has_add else ())
    return pl.pallas_call(
        body, name=name, grid=(m // tm, n // tn, nk), in_specs=in_specs, out_specs=o_spec,
        out_shape=jax.ShapeDtypeStruct((m, n), out_dtype), scratch_shapes=[pltpu.VMEM((tm, tn), f32)],
        compiler_params=_cparams(("parallel", "parallel", "arbitrary")),
    )(*args)


def _swiglu_fwd(hn, w_gate, w_up, name, tm=2048, tn=256):
    m, k = hn.shape
    n = w_gate.shape[1]

    def body(a_ref, wg_ref, wu_ref, g_ref, u_ref, act_ref):
        a = a_ref[...]
        g = _nn(a, wg_ref[...])
        u = _nn(a, wu_ref[...])
        g_ref[...] = g.astype(bf16)
        u_ref[...] = u.astype(bf16)
        act_ref[...] = (_silu(g) * u).astype(bf16)

    a_spec = pl.BlockSpec((tm, k), lambda i, j: (i, 0))
    w_spec = pl.BlockSpec((k, tn), lambda i, j: (0, j))
    o_spec = pl.BlockSpec((tm, tn), lambda i, j: (i, j))
    return pl.pallas_call(
        body, name=name, grid=(m // tm, n // tn), in_specs=[a_spec, w_spec, w_spec], out_specs=[o_spec, o_spec, o_spec],
        out_shape=[jax.ShapeDtypeStruct((m, n), bf16)] * 3,
        compiler_params=_cparams(("parallel", "parallel")),
    )(hn, w_gate, w_up)


def _swiglu_bwd(dh, w_down, g, u, name, tm=2048, tn=256):
    m, k = dh.shape
    n = w_down.shape[0]

    def body(a_ref, w_ref, g_ref, u_ref, dg_ref, du_ref):
        dact = _nt(a_ref[...].astype(bf16), w_ref[...])
        gg = g_ref[...].astype(f32)
        dg_ref[...] = (dact * u_ref[...].astype(f32) * _dsilu(gg)).astype(bf16)
        du_ref[...] = (dact * _silu(gg)).astype(bf16)

    a_spec = pl.BlockSpec((tm, k), lambda i, j: (i, 0))
    w_spec = pl.BlockSpec((tn, k), lambda i, j: (j, 0))
    o_spec = pl.BlockSpec((tm, tn), lambda i, j: (i, j))
    return pl.pallas_call(
        body, name=name, grid=(m // tm, n // tn), in_specs=[a_spec, w_spec, o_spec, o_spec], out_specs=[o_spec, o_spec],
        out_shape=[jax.ShapeDtypeStruct((m, n), bf16), jax.ShapeDtypeStruct((m, n), bf16)],
        compiler_params=_cparams(("parallel", "parallel")),
    )(dh, w_down, g, u)


def _rmsnorm_fwd(h, w, name, tm=512):
    m, d = h.shape

    def body(h_ref, w_ref, o_ref):
        x = h_ref[...]
        r = lax.rsqrt(jnp.mean(x * x, axis=-1, keepdims=True) + EPS)
        o_ref[...] = (x * r * w_ref[...]).astype(bf16)

    return pl.pallas_call(
        body, name=name, grid=(m // tm,),
        in_specs=[pl.BlockSpec((tm, d), lambda i: (i, 0)), pl.BlockSpec((1, d), lambda i: (0, 0))],
        out_specs=pl.BlockSpec((tm, d), lambda i: (i, 0)), out_shape=jax.ShapeDtypeStruct((m, d), bf16),
        compiler_params=_cparams(("parallel",)),
    )(h, w)


def _rmsnorm_bwd(dhn, h, w, dres, name, tm=512):
    m, d = h.shape

    def body(dhn_ref, h_ref, w_ref, dres_ref, dh_ref, dw_ref):
        x = h_ref[...]
        r = lax.rsqrt(jnp.mean(x * x, axis=-1, keepdims=True) + EPS)
        xhat = x * r
        dy = dhn_ref[...]
        gw = dy * w_ref[...]
        dh_ref[...] = dres_ref[...] + r * (gw - xhat * jnp.mean(gw * xhat, axis=-1, keepdims=True))
        part = _rowsum8(dy * xhat)

        @pl.when(pl.program_id(0) == 0)
        def _():
            dw_ref[...] = part

        @pl.when(pl.program_id(0) > 0)
        def _():
            dw_ref[...] += part

    row = pl.BlockSpec((tm, d), lambda i: (i, 0))
    return pl.pallas_call(
        body, name=name, grid=(m // tm,),
        in_specs=[row, row, pl.BlockSpec((1, d), lambda i: (0, 0)), row],
        out_specs=[row, pl.BlockSpec((SUBLANES, d), lambda i: (0, 0))],
        out_shape=[jax.ShapeDtypeStruct((m, d), f32), jax.ShapeDtypeStruct((SUBLANES, d), f32)],
        compiler_params=_cparams(("arbitrary",)),
    )(dhn, h, w, dres)


def _final_loss(h, w, target, name, tm=512):
    m, d = h.shape

    def body(h_ref, w_ref, t_ref, dh_ref, loss_ref, dw_ref):
        x = h_ref[...]
        r = lax.rsqrt(jnp.mean(x * x, axis=-1, keepdims=True) + EPS)
        xhat = x * r
        ww = w_ref[...]
        err = xhat * ww - t_ref[...]
        dy = err * (1.0 / d)
        gw = dy * ww
        dh_ref[...] = r * (gw - xhat * jnp.mean(gw * xhat, axis=-1, keepdims=True))
        lpart = _rowsum8(err * err) * (0.5 / d)
        wpart = _rowsum8(dy * xhat)

        @pl.when(pl.program_id(0) == 0)
        def _():
            loss_ref[...] = lpart
            dw_ref[...] = wpart

        @pl.when(pl.program_id(0) > 0)
        def _():
            loss_ref[...] += lpart
            dw_ref[...] += wpart

    row = pl.BlockSpec((tm, d), lambda i: (i, 0))
    acc = pl.BlockSpec((SUBLANES, d), lambda i: (0, 0))
    return pl.pallas_call(
        body, name=name, grid=(m // tm,),
        in_specs=[row, pl.BlockSpec((1, d), lambda i: (0, 0)), row], out_specs=[row, acc, acc],
        out_shape=[jax.ShapeDtypeStruct((m, d), f32), jax.ShapeDtypeStruct((SUBLANES, d), f32), jax.ShapeDtypeStruct((SUBLANES, d), f32)],
        compiler_params=_cparams(("arbitrary",)),
    )(h, w, target)


def _lane_tables():
    f = np.arange(LANES) % HEAD_DIM
    inv = ROPE_THETA ** (-jnp.arange(0, ROPE_DIM, 2, dtype=f32) / ROPE_DIM)
    invf = jnp.where(f < ROPE_DIM, inv[f % (ROPE_DIM // 2)], 0.0).astype(f32)
    return invf.reshape(1, LANES)


def _rope_tables(pos_col, name):
    t = pos_col.shape[0]
    tm = SEQ

    def body(p_ref, f_ref, c_ref, s1_ref, s2_ref):
        ang = p_ref[...].astype(f32) * f_ref[...]
        co, si = jnp.cos(ang), jnp.sin(ang)
        f = lax.broadcasted_iota(jnp.int32, (tm, LANES), 1) % HEAD_DIM
        c_ref[...] = jnp.where(f < ROPE_DIM, co, 1.0)
        s1_ref[...] = jnp.where(f < ROPE_DIM // 2, -si, 0.0)
        s2_ref[...] = jnp.where((f >= ROPE_DIM // 2) & (f < ROPE_DIM), si, 0.0)

    row = pl.BlockSpec((tm, LANES), lambda i: (i, 0))
    return pl.pallas_call(
        body, name=name, grid=(t // tm,),
        in_specs=[pl.BlockSpec((tm, 1), lambda i: (i, 0)), pl.BlockSpec((1, LANES), lambda i: (0, 0))],
        out_specs=[row, row, row], out_shape=[jax.ShapeDtypeStruct((t, LANES), f32)] * 3,
        compiler_params=_cparams(("parallel",)),
    )(pos_col, _lane_tables())


def _rot(x, c, s1, s2):
    return x * c + pltpu.roll(x, LANES - ROPE_DIM // 2, 1) * s1 + pltpu.roll(x, ROPE_DIM // 2, 1) * s2


def _rot_t(g, c, s1, s2):
    return g * c + pltpu.roll(g * s1, ROPE_DIM // 2, 1) + pltpu.roll(g * s2, LANES - ROPE_DIM // 2, 1)


def _dup_head(x, kvh, low):
    a = jnp.where(kvh == 0, x, pltpu.roll(x, HEAD_DIM, 1))
    return jnp.where(low, a, pltpu.roll(a, HEAD_DIM, 1))


def _deinterleave(src_ref, dst_ref, d, dtype):
    length = SEQ // d
    if d == 1:
        dst_ref[...] = src_ref[...].astype(dtype)
    else:
        for r in range(d):
            dst_ref[pl.ds(r * length, length), :] = src_ref[pl.ds(r, length, stride=d), :].astype(dtype)


def _interleave_store(src_ref, dst_ref, d, accumulate):
    length = SEQ // d
    if d == 1:
        if accumulate:
            dst_ref[...] += src_ref[...]
        else:
            dst_ref[...] = src_ref[...]
    else:
        for r in range(d):
            blk = src_ref[pl.ds(r * length, length), :]
            if accumulate:
                dst_ref[pl.ds(r, length, stride=d), :] = dst_ref[pl.ds(r, length, stride=d), :] + blk
            else:
                dst_ref[pl.ds(r, length, stride=d), :] = blk


def _attn_masks():
    qi = lax.broadcasted_iota(jnp.int32, (ATTN_BLOCK, ATTN_BLOCK), 0)
    ki = lax.broadcasted_iota(jnp.int32, (ATTN_BLOCK, ATTN_BLOCK), 1)
    low = lax.broadcasted_iota(jnp.int32, (ATTN_BLOCK, LANES), 1) < HEAD_DIM
    return ki <= qi, ki >= qi, low


NEG_INF = float("-inf")
ATTN_UNROLL = 4


def _attn_fwd(qkv, tabs, name):
    t = qkv.shape[0]
    nb = t // SEQ
    n_blk = SEQ // ATTN_BLOCK

    def body(q_ref, k_ref, v_ref, c_ref, s1_ref, s2_ref, o_ref, lse_ref,
             qr, kr, vr, qd, kd, vd, ob, lb, o0, o1, o2, l0, l1, l2, ss):
        kvh = pl.program_id(1) // 2
        cur_ok, prev_ok, low = _attn_masks()
        lowfull = lax.broadcasted_iota(jnp.int32, (SEQ, LANES), 1) < HEAD_DIM
        c, s1, s2 = c_ref[...], s1_ref[...], s2_ref[...]
        qr[...] = _rot(q_ref[...], c, s1, s2) * (HEAD_DIM ** -0.5)
        kr[...] = _dup_head(_rot(k_ref[...], c, s1, s2), kvh, lowfull)
        vr[...] = _dup_head(v_ref[...], kvh, lowfull)
        onat, lnat = (o0, o1, o2), (l0, l1, l2)
        for bi, d in enumerate(DILATIONS):
            _deinterleave(qr, qd, d, bf16)
            _deinterleave(kr, kd, d, bf16)
            _deinterleave(vr, vd, d, bf16)
            per_res = n_blk // d
            use_prev = per_res > 1

            def scores(n, carry):
                start = pl.multiple_of(n * ATTN_BLOCK, ATTN_BLOCK)
                has_prev = (n % per_res) != 0
                pstart = pl.multiple_of(jnp.maximum(n - 1, 0) * ATTN_BLOCK, ATTN_BLOCK)
                qb = qd[pl.ds(start, ATTN_BLOCK), :]
                kc = kd[pl.ds(start, ATTN_BLOCK), :]
                if use_prev:
                    kp = kd[pl.ds(pstart, ATTN_BLOCK), :]
                for a in range(2):
                    qa = jnp.where(low if a == 0 else ~low, qb, jnp.zeros_like(qb))
                    ss[2 * n + a, :, 0:ATTN_BLOCK] = jnp.where(cur_ok, _nt(qa, kc), NEG_INF)
                    if use_prev:
                        ss[2 * n + a, :, ATTN_BLOCK:2 * ATTN_BLOCK] = jnp.where(prev_ok & has_prev, _nt(qa, kp), NEG_INF)
                return carry

            def softmax_pv(n, carry):
                start = pl.multiple_of(n * ATTN_BLOCK, ATTN_BLOCK)
                pstart = pl.multiple_of(jnp.maximum(n - 1, 0) * ATTN_BLOCK, ATTN_BLOCK)
                vc = vd[pl.ds(start, ATTN_BLOCK), :]
                if use_prev:
                    vp = vd[pl.ds(pstart, ATTN_BLOCK), :]
                outs, lses = [], []
                for a in range(2):
                    sc = ss[2 * n + a, :, 0:ATTN_BLOCK]
                    if use_prev:
                        sp = ss[2 * n + a, :, ATTN_BLOCK:2 * ATTN_BLOCK]
                        m = jnp.max(jnp.maximum(sc, sp), axis=1, keepdims=True)
                        pc, pp = jnp.exp(sc - m), jnp.exp(sp - m)
                        den = jnp.sum(pc + pp, axis=1, keepdims=True)
                        acc = _nn(pc.astype(bf16), vc) + _nn(pp.astype(bf16), vp)
                    else:
                        m = jnp.max(sc, axis=1, keepdims=True)
                        pc = jnp.exp(sc - m)
                        den = jnp.sum(pc, axis=1, keepdims=True)
                        acc = _nn(pc.astype(bf16), vc)
                    outs.append(acc * (1.0 / den))
                    lses.append(m + jnp.log(den))
                ob[pl.ds(start, ATTN_BLOCK), :] = jnp.where(low, outs[0], outs[1])
                lb[pl.ds(start, ATTN_BLOCK), :] = jnp.where(low, lses[0], lses[1])
                return carry

            lax.fori_loop(0, n_blk, scores, 0, unroll=ATTN_UNROLL)
            lax.fori_loop(0, n_blk, softmax_pv, 0, unroll=ATTN_UNROLL)
            _interleave_store(ob, onat[bi], d, False)
            _interleave_store(lb, lnat[bi], d, False)
        la, lbb, lc = l0[...], l1[...], l2[...]
        lm = jnp.maximum(jnp.maximum(la, lbb), lc)
        wa, wb, wc = jnp.exp(la - lm), jnp.exp(lbb - lm), jnp.exp(lc - lm)
        ws = wa + wb + wc
        o_ref[...] = (wa * o0[...] + wb * o1[...] + wc * o2[...]) / ws
        lse_ref[...] = lm + jnp.log(ws)

    def col(jj):
        return pl.BlockSpec((SEQ, LANES), lambda b, j: (b, jj if jj is not None else j))

    tab = pl.BlockSpec((SEQ, LANES), lambda b, j: (b, 0))
    fs = pltpu.VMEM((SEQ, LANES), f32)
    hs = pltpu.VMEM((SEQ, LANES), bf16)
    return pl.pallas_call(
        body, name=name, grid=(nb, ATTN_WIDTH // LANES),
        in_specs=[col(None), col(ATTN_WIDTH // LANES), col(ATTN_WIDTH // LANES + 1), tab, tab, tab],
        out_specs=[col(None), col(None)],
        out_shape=[jax.ShapeDtypeStruct((t, ATTN_WIDTH), f32), jax.ShapeDtypeStruct((t, ATTN_WIDTH), f32)],
        scratch_shapes=[fs, fs, fs, hs, hs, hs, fs, fs, fs, fs, fs, fs, fs, fs, pltpu.VMEM((2 * n_blk, ATTN_BLOCK, 2 * ATTN_BLOCK), f32)],
        compiler_params=_cparams(("parallel", "parallel")),
    )(qkv, qkv, qkv, *tabs)


def _attn_bwd(qkv, tabs, o, lse, do, name):
    t = qkv.shape[0]
    nb = t // SEQ
    n_blk = SEQ // ATTN_BLOCK
    n_j = ATTN_WIDTH // LANES

    def body(q_ref, k_ref, v_ref, c_ref, s1_ref, s2_ref, o_ref, lse_ref, do_ref, dq_ref, dk_ref, dv_ref,
             qr, kr, vr, dl, qd, kd, vd, dod, lsd, dld, dqd, dkd, dvd, dqa, dka, dva, pb, dsb):
        j = pl.program_id(1)
        pb[2 * n_blk:2 * n_blk + 2] = jnp.zeros((2, ATTN_BLOCK, 2 * ATTN_BLOCK), bf16)
        dsb[2 * n_blk:2 * n_blk + 2] = jnp.zeros((2, ATTN_BLOCK, 2 * ATTN_BLOCK), bf16)
        kvh = j // 2
        cur_ok, prev_ok, low = _attn_masks()
        lowfull = lax.broadcasted_iota(jnp.int32, (SEQ, LANES), 1) < HEAD_DIM
        c, s1, s2 = c_ref[...], s1_ref[...], s2_ref[...]
        qr[...] = _rot(q_ref[...], c, s1, s2) * (HEAD_DIM ** -0.5)
        kr[...] = _dup_head(_rot(k_ref[...], c, s1, s2), kvh, lowfull)
        vr[...] = _dup_head(v_ref[...], kvh, lowfull)
        prod = do_ref[...] * o_ref[...]
        d_lo = jnp.sum(jnp.where(lowfull, prod, 0.0), axis=1, keepdims=True)
        d_hi = jnp.sum(jnp.where(lowfull, 0.0, prod), axis=1, keepdims=True)
        dl[...] = jnp.where(lowfull, d_lo, d_hi)
        dqa[...] = jnp.zeros_like(dqa)
        dka[...] = jnp.zeros_like(dka)
        dva[...] = jnp.zeros_like(dva)
        for d in DILATIONS:
            _deinterleave(qr, qd, d, bf16)
            _deinterleave(kr, kd, d, bf16)
            _deinterleave(vr, vd, d, bf16)
            _deinterleave(do_ref, dod, d, bf16)
            _deinterleave(lse_ref, lsd, d, f32)
            _deinterleave(dl, dld, d, f32)
            per_res = n_blk // d
            use_prev = per_res > 1
            curl, prevl = slice(0, ATTN_BLOCK), slice(ATTN_BLOCK, 2 * ATTN_BLOCK)

            def halves(x):
                zero = jnp.zeros_like(x)
                return jnp.where(low, x, zero), jnp.where(low, zero, x)

            def probs(n, carry):
                start = pl.multiple_of(n * ATTN_BLOCK, ATTN_BLOCK)
                has_prev = (n % per_res) != 0
                pstart = pl.multiple_of(jnp.maximum(n - 1, 0) * ATTN_BLOCK, ATTN_BLOCK)
                cur, prev = pl.ds(start, ATTN_BLOCK), pl.ds(pstart, ATTN_BLOCK)
                qas, doas = halves(qd[cur, :]), halves(dod[cur, :])
                kc, vc = kd[cur, :], vd[cur, :]
                if use_prev:
                    kp, vp = kd[prev, :], vd[prev, :]
                lsb, dlb = lsd[cur, :], dld[cur, :]
                for a in range(2):
                    ls = lsb[:, a * HEAD_DIM:a * HEAD_DIM + 1]
                    de = dlb[:, a * HEAD_DIM:a * HEAD_DIM + 1]
                    pc = jnp.exp(jnp.where(cur_ok, _nt(qas[a], kc), NEG_INF) - ls)
                    pb[2 * n + a, :, curl] = pc.astype(bf16)
                    dsb[2 * n + a, :, curl] = (pc * (_nt(doas[a], vc) - de)).astype(bf16)
                    if use_prev:
                        pp = jnp.exp(jnp.where(prev_ok & has_prev, _nt(qas[a], kp), NEG_INF) - ls)
                        pb[2 * n + a, :, prevl] = pp.astype(bf16)
                        dsb[2 * n + a, :, prevl] = (pp * (_nt(doas[a], vp) - de)).astype(bf16)
                return carry

            def grads(n, carry):
                start = pl.multiple_of(n * ATTN_BLOCK, ATTN_BLOCK)
                pstart = pl.multiple_of(jnp.maximum(n - 1, 0) * ATTN_BLOCK, ATTN_BLOCK)
                nstart = pl.multiple_of(jnp.minimum(n + 1, n_blk - 1) * ATTN_BLOCK, ATTN_BLOCK)
                cur, prev, nxt = pl.ds(start, ATTN_BLOCK), pl.ds(pstart, ATTN_BLOCK), pl.ds(nstart, ATTN_BLOCK)
                kc = kd[cur, :]
                dqs = [_nn(dsb[2 * n + a, :, curl], kc) for a in range(2)]
                q_rows, do_rows = list(halves(qd[cur, :])), list(halves(dod[cur, :]))
                ds_rows, p_rows = [dsb[2 * n + a, :, curl] for a in range(2)], [pb[2 * n + a, :, curl] for a in range(2)]
                if use_prev:
                    kp = kd[prev, :]
                    dqs = [dqs[a] + _nn(dsb[2 * n + a, :, prevl], kp) for a in range(2)]
                    q_rows += list(halves(qd[nxt, :]))
                    do_rows += list(halves(dod[nxt, :]))
                    ds_rows += [dsb[2 * n + 2 + a, :, prevl] for a in range(2)]
                    p_rows += [pb[2 * n + 2 + a, :, prevl] for a in range(2)]
                dqd[cur, :] = jnp.where(low, dqs[0], dqs[1])
                dkd[cur, :] = _tn(jnp.concatenate(ds_rows, axis=0), jnp.concatenate(q_rows, axis=0))
                dvd[cur, :] = _tn(jnp.concatenate(p_rows, axis=0), jnp.concatenate(do_rows, axis=0))
                return carry

            lax.fori_loop(0, n_blk, probs, 0, unroll=ATTN_UNROLL)
            lax.fori_loop(0, n_blk, grads, 0, unroll=ATTN_UNROLL)
            _interleave_store(dqd, dqa, d, True)
            _interleave_store(dkd, dka, d, True)
            _interleave_store(dvd, dva, d, True)
        dq_ref[...] = _rot_t(dqa[...] * (HEAD_DIM ** -0.5), c, s1, s2)
        dkf = dka[...]
        dkf = _rot_t(dkf + pltpu.roll(dkf, HEAD_DIM, 1), c, s1, s2)
        dvf = dva[...]
        dvf = dvf + pltpu.roll(dvf, HEAD_DIM, 1)
        mine = (lax.broadcasted_iota(jnp.int32, (SEQ, LANES), 1) // HEAD_DIM) == kvh
        dkc_, dvc_ = jnp.where(mine, dkf, 0.0), jnp.where(mine, dvf, 0.0)

        @pl.when(j == 0)
        def _():
            dk_ref[...] = dkc_
            dv_ref[...] = dvc_

        @pl.when(j > 0)
        def _():
            dk_ref[...] += dkc_
            dv_ref[...] += dvc_

    def col(jj):
        return pl.BlockSpec((SEQ, LANES), lambda b, j: (b, jj if jj is not None else j))

    tab = pl.BlockSpec((SEQ, LANES), lambda b, j: (b, 0))
    fs = pltpu.VMEM((SEQ, LANES), f32)
    hs = pltpu.VMEM((SEQ, LANES), bf16)
    return pl.pallas_call(
        body, name=name, grid=(nb, n_j),
        in_specs=[col(None), col(n_j), col(n_j + 1), tab, tab, tab, col(None), col(None), col(None)],
        out_specs=[col(None), tab, tab],
        out_shape=[jax.ShapeDtypeStruct((t, ATTN_WIDTH), f32), jax.ShapeDtypeStruct((t, LANES), f32), jax.ShapeDtypeStruct((t, LANES), f32)],
        scratch_shapes=[fs, fs, fs, fs, hs, hs, hs, hs, fs, fs, fs, fs, fs, fs, fs, fs,
                        pltpu.VMEM((2 * n_blk + 2, ATTN_BLOCK, 2 * ATTN_BLOCK), bf16), pltpu.VMEM((2 * n_blk + 2, ATTN_BLOCK, 2 * ATTN_BLOCK), bf16)],
        compiler_params=_cparams(("parallel", "arbitrary")),
    )(qkv, qkv, qkv, *tabs, o, lse, do)


def _conv_pre(x, w_ref, b_ref, row):
    shifted = [x] + [jnp.where(row >= s, pltpu.roll(x, s, 0), 0.0) for s in range(1, CONV_WIDTH)]
    pre = b_ref[...] + w_ref[CONV_WIDTH - 1:CONV_WIDTH, :] * x
    for s in range(1, CONV_WIDTH):
        pre = pre + w_ref[CONV_WIDTH - 1 - s:CONV_WIDTH - s, :] * shifted[s]
    return pre, shifted


def _conv_fwd(x, w, b, name, tc=512):
    t, ch = x.shape

    def body(x_ref, w_ref, b_ref, o_ref):
        row = lax.broadcasted_iota(jnp.int32, (SEQ, tc), 0)
        pre, _ = _conv_pre(x_ref[...], w_ref, b_ref, row)
        o_ref[...] = _silu(pre)

    xs = pl.BlockSpec((SEQ, tc), lambda i, j: (i, j))
    return pl.pallas_call(
        body, name=name, grid=(t // SEQ, ch // tc),
        in_specs=[xs, pl.BlockSpec((CONV_WIDTH, tc), lambda i, j: (0, j)), pl.BlockSpec((1, tc), lambda i, j: (0, j))],
        out_specs=xs, out_shape=jax.ShapeDtypeStruct((t, ch), f32),
        compiler_params=_cparams(("parallel", "parallel")),
    )(x, w, b)


def _conv_bwd(x, w, b, dact, name, tc=512):
    t, ch = x.shape

    def body(x_ref, w_ref, b_ref, d_ref, dx_ref, dw_ref, db_ref):
        row = lax.broadcasted_iota(jnp.int32, (SEQ, tc), 0)
        pre, shifted = _conv_pre(x_ref[...], w_ref, b_ref, row)
        dpre = d_ref[...] * _dsilu(pre)
        dx = w_ref[CONV_WIDTH - 1:CONV_WIDTH, :] * dpre
        for s in range(1, CONV_WIDTH):
            dx = dx + w_ref[CONV_WIDTH - 1 - s:CONV_WIDTH - s, :] * jnp.where(row < SEQ - s, pltpu.roll(dpre, SEQ - s, 0), 0.0)
        dx_ref[...] = dx
        first = pl.program_id(1) == 0
        parts = [jnp.sum(dpre * shifted[CONV_WIDTH - 1 - k], axis=0, keepdims=True) for k in range(CONV_WIDTH)]
        dbp = jnp.sum(dpre, axis=0, keepdims=True)

        @pl.when(first)
        def _():
            for k in range(CONV_WIDTH):
                dw_ref[k:k + 1, :] = parts[k]
            db_ref[...] = dbp

        @pl.when(jnp.logical_not(first))
        def _():
            for k in range(CONV_WIDTH):
                dw_ref[k:k + 1, :] += parts[k]
            db_ref[...] += dbp

    xs = pl.BlockSpec((SEQ, tc), lambda j, i: (i, j))
    ws = pl.BlockSpec((CONV_WIDTH, tc), lambda j, i: (0, j))
    bs = pl.BlockSpec((1, tc), lambda j, i: (0, j))
    return pl.pallas_call(
        body, name=name, grid=(ch // tc, t // SEQ),
        in_specs=[xs, ws, bs, xs], out_specs=[xs, ws, bs],
        out_shape=[jax.ShapeDtypeStruct((t, ch), f32), jax.ShapeDtypeStruct((CONV_WIDTH, ch), f32), jax.ShapeDtypeStruct((1, ch), f32)],
        compiler_params=_cparams(("parallel", "arbitrary")),
    )(x, w, b, dact)


GROUP_W = SSM_INNER // SSM_GROUPS
HEADS_PER_GROUP = SSM_HEADS // SSM_GROUPS
HI = lax.Precision.HIGHEST


def _ssd_common(xbc_ref, dt_ref, bias_ref, alog_ref):
    r = lax.broadcasted_iota(jnp.int32, (CHUNK, CHUNK), 0)
    cidx = lax.broadcasted_iota(jnp.int32, (CHUNK, CHUNK), 1)
    causal = r >= cidx
    tril = causal.astype(f32)
    expand = (lax.broadcasted_iota(jnp.int32, (CHUNK, SSM_INNER), 0)
              == lax.broadcasted_iota(jnp.int32, (CHUNK, SSM_INNER), 1) // HEAD_DIM).astype(f32)
    head_lane = cidx < SSM_HEADS
    dtp = dt_ref[...] + bias_ref[...]
    dt = jnp.where(head_lane, _softplus(dtp), 0.0)
    a_neg = -jnp.exp(alog_ref[...])
    a = dt * a_neg
    cs = _nn(tril, a, HI)
    dt_e = _nn(dt, expand, HI)
    cs_e = _nn(cs, expand, HI)
    xs = xbc_ref[:, 0:SSM_INNER]
    xg = xs * dt_e
    ecs = jnp.exp(cs_e)
    cs_last = cs_e[CHUNK - 1:CHUNK, :]
    dse = jnp.exp(cs_last - cs_e)
    cde = jnp.exp(cs_last)
    return dict(r=r, cidx=cidx, causal=causal, tril=tril, expand=expand, head_lane=head_lane, dtp=dtp, dt=dt, a_neg=a_neg,
                cs=cs, cst=cs.T, dt_e=dt_e, cs_e=cs_e, xs=xs, xg=xg, ecs=ecs, dse=dse, cde=cde)


def _decay_mat(q, h):
    return jnp.exp(jnp.where(q["causal"], q["cs"][:, h:h + 1] - q["cst"][h:h + 1, :], NEG_INF))


def _gate_norm(y, z, nw):
    y2 = y * _silu(z)
    outs, xhats, rs = [], [], []
    for g in range(SSM_GROUPS):
        sl = slice(g * GROUP_W, (g + 1) * GROUP_W)
        yg = y2[:, sl]
        r = lax.rsqrt(jnp.mean(yg * yg, axis=-1, keepdims=True) + EPS)
        xhats.append(yg * r)
        rs.append(r)
        outs.append(yg * r * nw[:, sl])
    return y2, outs, xhats, rs


def _ssd_fwd(xbc, z, dtp, params, name):
    t = xbc.shape[0]
    n_chunk = SEQ // CHUNK
    low = None

    def body(xbc_ref, z_ref, dt_ref, bias_ref, alog_ref, dskip_ref, nw_ref, yn_ref, y_ref, hs_ref, h_scr):
        @pl.when(pl.program_id(1) == 0)
        def _():
            h_scr[...] = jnp.zeros_like(h_scr)

        q = _ssd_common(xbc_ref, dt_ref, bias_ref, alog_ref)
        low = lax.broadcasted_iota(jnp.int32, (CHUNK, LANES), 1) < HEAD_DIM
        xgb = q["xg"].astype(bf16)
        wst = (q["xg"] * q["dse"]).astype(bf16)
        hs_ref[0] = h_scr[...]
        ys = []
        for g in range(SSM_GROUPS):
            gl = slice(g * GROUP_W, (g + 1) * GROUP_W)
            bg = xbc_ref[:, SSM_INNER + g * D_STATE:SSM_INNER + (g + 1) * D_STATE].astype(bf16)
            cg = xbc_ref[:, SSM_INNER + SSM_GROUPS * D_STATE + g * D_STATE:SSM_INNER + SSM_GROUPS * D_STATE + (g + 1) * D_STATE].astype(bf16)
            cb = _nt(cg, bg)
            hg = h_scr[g]
            yoff = _nn(cg, hg.astype(bf16)) * q["ecs"][:, gl]
            pieces = []
            for i in range(HEADS_PER_GROUP // 2):
                h0 = g * HEADS_PER_GROUP + 2 * i
                xp = xgb[:, h0 * HEAD_DIM:(h0 + 2) * HEAD_DIM]
                m0 = (cb * _decay_mat(q, h0)).astype(bf16)
                m1 = (cb * _decay_mat(q, h0 + 1)).astype(bf16)
                zero = jnp.zeros_like(xp)
                pieces.append(_nn(m0, jnp.where(low, xp, zero)) + _nn(m1, jnp.where(low, zero, xp)))
            ys.append(jnp.concatenate(pieces, axis=1) + yoff + dskip_ref[:, gl] * q["xs"][:, gl])
            h_scr[g] = hg * q["cde"][:, gl] + _tn(bg, wst[:, gl])
        y = jnp.concatenate(ys, axis=1)
        y_ref[...] = y
        _, outs, _, _ = _gate_norm(y, z_ref[...], nw_ref[...])
        yn_ref[...] = jnp.concatenate(outs, axis=1).astype(bf16)

    def rows(w):
        return pl.BlockSpec((CHUNK, w), lambda b, c: (b * n_chunk + c, 0))

    def par(w):
        return pl.BlockSpec((1, w), lambda b, c: (0, 0))

    return pl.pallas_call(
        body, name=name, grid=(t // SEQ, n_chunk),
        in_specs=[rows(CONV_CH), rows(SSM_INNER), rows(LANES), par(LANES), par(LANES), par(SSM_INNER), par(SSM_INNER)],
        out_specs=[rows(SSM_INNER), rows(SSM_INNER), pl.BlockSpec((1, SSM_GROUPS, D_STATE, GROUP_W), lambda b, c: (b * n_chunk + c, 0, 0, 0))],
        out_shape=[jax.ShapeDtypeStruct((t, SSM_INNER), bf16), jax.ShapeDtypeStruct((t, SSM_INNER), f32),
                   jax.ShapeDtypeStruct((t // CHUNK, SSM_GROUPS, D_STATE, GROUP_W), f32)],
        scratch_shapes=[pltpu.VMEM((SSM_GROUPS, D_STATE, GROUP_W), f32)],
        compiler_params=_cparams(("parallel", "arbitrary")),
    )(xbc, z, dtp, *params)


def _ssd_bwd(xbc, z, dtp, y, hs, dyn, params, name):
    t = xbc.shape[0]
    n_chunk = SEQ // CHUNK

    def body(xbc_ref, z_ref, dt_ref, y_ref, hs_ref, dyn_ref, bias_ref, alog_ref, dskip_ref, nw_ref,
             dxbc_ref, dz_ref, ddt_ref, dnw_ref, dds_ref, dal_ref, dbi_ref, dh_scr):
        @pl.when(pl.program_id(1) == 0)
        def _():
            dh_scr[...] = jnp.zeros_like(dh_scr)

        q = _ssd_common(xbc_ref, dt_ref, bias_ref, alog_ref)
        low = lax.broadcasted_iota(jnp.int32, (CHUNK, LANES), 1) < HEAD_DIM
        last_row = lax.broadcasted_iota(jnp.int32, (CHUNK, GROUP_W), 0) == CHUNK - 1
        xs, xg = q["xs"], q["xg"]
        xgb = xg.astype(bf16)
        wf = xg * q["dse"]
        wst = wf.astype(bf16)
        zz = z_ref[...]
        yy = y_ref[...]
        sz = _silu(zz)
        y2, _, xhats, rs = _gate_norm(yy, zz, nw_ref[...])
        dyn_ = dyn_ref[...]
        dy2s, dnws = [], []
        for g in range(SSM_GROUPS):
            gl = slice(g * GROUP_W, (g + 1) * GROUP_W)
            gw = dyn_[:, gl] * nw_ref[:, gl]
            dy2s.append(rs[g] * (gw - xhats[g] * jnp.mean(gw * xhats[g], axis=-1, keepdims=True)))
            dnws.append(_rowsum8(dyn_[:, gl] * xhats[g]))
        dy2 = jnp.concatenate(dy2s, axis=1)
        dy = dy2 * sz
        dz_ref[...] = dy2 * yy * _dsilu(zz)
        dnw_p = jnp.concatenate(dnws, axis=1)
        dds_p = _rowsum8(dy * xs)
        dyb = dy.astype(bf16)
        gfull = (dy * q["ecs"]).astype(bf16)
        dcs_c = jnp.zeros((CHUNK, CHUNK), f32)
        dcs_r = jnp.zeros((CHUNK, CHUNK), f32)
        dcs_e_parts, dxg_parts = [], []
        for g in range(SSM_GROUPS):
            gl = slice(g * GROUP_W, (g + 1) * GROUP_W)
            bsl = slice(SSM_INNER + g * D_STATE, SSM_INNER + (g + 1) * D_STATE)
            csl = slice(SSM_INNER + SSM_GROUPS * D_STATE + g * D_STATE, SSM_INNER + SSM_GROUPS * D_STATE + (g + 1) * D_STATE)
            bg = xbc_ref[:, bsl].astype(bf16)
            cg = xbc_ref[:, csl].astype(bf16)
            cb = _nt(cg, bg)
            hg = hs_ref[0, g]
            hgb = hg.astype(bf16)
            dhn = dh_scr[g]
            dhnb = dhn.astype(bf16)
            yoff = _nn(cg, hgb) * q["ecs"][:, gl]
            dw_ = _nn(bg, dhnb)
            r_e = dw_ * wf[:, gl]
            to_last = jnp.sum(r_e, axis=0, keepdims=True) + jnp.sum(dhn * hg, axis=0, keepdims=True) * q["cde"][:, gl]
            dcs_e_parts.append(dy[:, gl] * yoff - r_e + jnp.where(last_row, to_last, 0.0))
            dcb = jnp.zeros((CHUNK, CHUNK), f32)
            dxg_pairs = []
            for i in range(HEADS_PER_GROUP // 2):
                h0 = g * HEADS_PER_GROUP + 2 * i
                psl = slice(h0 * HEAD_DIM, (h0 + 2) * HEAD_DIM)
                xp = xgb[:, psl]
                dyp = dyb[:, psl]
                zero = jnp.zeros_like(dyp)
                tns = []
                for a in range(2):
                    h = h0 + a
                    lm = _decay_mat(q, h)
                    m = cb * lm
                    dm = _nt(jnp.where(low, dyp, zero) if a == 0 else jnp.where(low, zero, dyp), xp)
                    dcb = dcb + dm * lm
                    nmat = dm * m
                    dcs_c = dcs_c + jnp.where(q["cidx"] == h, jnp.sum(nmat, axis=1, keepdims=True), 0.0)
                    dcs_r = dcs_r + jnp.where(q["r"] == h, jnp.sum(nmat, axis=0, keepdims=True), 0.0)
                    tns.append(_tn(m.astype(bf16), dyp))
                dxg_pairs.append(jnp.where(low, tns[0], tns[1]))
            dxg_parts.append(jnp.concatenate(dxg_pairs, axis=1) + dw_ * q["dse"][:, gl])
            dcbb = dcb.astype(bf16)
            dxbc_ref[:, csl] = _nt(gfull[:, gl], hgb) + _nn(dcbb, bg)
            dxbc_ref[:, bsl] = _nt(wst[:, gl], dhnb) + _tn(dcbb, cg)
            dh_scr[g] = dhn * q["cde"][:, gl] + _tn(cg, gfull[:, gl])
        dxg = jnp.concatenate(dxg_parts, axis=1)
        dcs_e = jnp.concatenate(dcs_e_parts, axis=1)
        dxbc_ref[:, 0:SSM_INNER] = dskip_ref[...] * dy + dxg * q["dt_e"]
        dcs = dcs_c - dcs_r.T + _dot(dcs_e, q["expand"], ((1,), (1,)), HI)
        triu = (q["cidx"] >= q["r"]).astype(f32)
        da = _nn(triu, dcs, HI)
        ddt = _dot(dxg * xs, q["expand"], ((1,), (1,)), HI) + da * q["a_neg"]
        ddtp = jnp.where(q["head_lane"], ddt * _sigmoid(q["dtp"]), 0.0)
        ddt_ref[...] = ddtp
        dal_p = _rowsum8(da * q["dt"]) * q["a_neg"]
        dbi_p = _rowsum8(ddtp)
        first = (pl.program_id(0) == 0) & (pl.program_id(1) == 0)

        @pl.when(first)
        def _():
            dnw_ref[...] = dnw_p
            dds_ref[...] = dds_p
            dal_ref[...] = dal_p
            dbi_ref[...] = dbi_p

        @pl.when(jnp.logical_not(first))
        def _():
            dnw_ref[...] += dnw_p
            dds_ref[...] += dds_p
            dal_ref[...] += dal_p
            dbi_ref[...] += dbi_p

    def rows(w):
        return pl.BlockSpec((CHUNK, w), lambda b, c: (b * n_chunk + n_chunk - 1 - c, 0))

    def par(w):
        return pl.BlockSpec((1, w), lambda b, c: (0, 0))

    def acc(w):
        return pl.BlockSpec((SUBLANES, w), lambda b, c: (0, 0))

    return pl.pallas_call(
        body, name=name, grid=(t // SEQ, n_chunk),
        in_specs=[rows(CONV_CH), rows(SSM_INNER), rows(LANES), rows(SSM_INNER),
                  pl.BlockSpec((1, SSM_GROUPS, D_STATE, GROUP_W), lambda b, c: (b * n_chunk + n_chunk - 1 - c, 0, 0, 0)),
                  rows(SSM_INNER), par(LANES), par(LANES), par(SSM_INNER), par(SSM_INNER)],
        out_specs=[rows(CONV_CH), rows(SSM_INNER), rows(LANES), acc(SSM_INNER), acc(SSM_INNER), acc(LANES), acc(LANES)],
        out_shape=[jax.ShapeDtypeStruct((t, CONV_CH), f32), jax.ShapeDtypeStruct((t, SSM_INNER), f32), jax.ShapeDtypeStruct((t, LANES), f32),
                   jax.ShapeDtypeStruct((SUBLANES, SSM_INNER), f32), jax.ShapeDtypeStruct((SUBLANES, SSM_INNER), f32),
                   jax.ShapeDtypeStruct((SUBLANES, LANES), f32), jax.ShapeDtypeStruct((SUBLANES, LANES), f32)],
        scratch_shapes=[pltpu.VMEM((SSM_GROUPS, D_STATE, GROUP_W), f32)],
        compiler_params=_cparams(("arbitrary", "arbitrary")),
    )(xbc, z, dtp, y, hs, dyn, *params)


def _adamw(g_parts, w, m, v, name, layer=None):
    rows, width = w.shape[-2:]
    n = len(g_parts)
    tr = _row_tile(rows)

    def body(*refs):
        g_refs, (w_ref, m_ref, v_ref, g_out, d_out, m_out, v_out) = refs[:n], refs[n:]

        def part(i):
            return (g_refs[i][...] if g_parts[i][1] is None else g_refs[i][0]).astype(f32)

        def state(ref):
            return ref[...] if layer is None else ref[0]

        g = part(0)
        for i in range(1, n):
            g = g + part(i)
        mm = ADAM_B1 * state(m_ref) + (1.0 - ADAM_B1) * g
        vv = ADAM_B2 * state(v_ref) + (1.0 - ADAM_B2) * (g * g)
        m_hat = mm / (1.0 - ADAM_B1 ** ADAM_STEP)
        v_hat = vv / (1.0 - ADAM_B2 ** ADAM_STEP)
        g_out[...] = g
        d_out[...] = -ADAM_LR * (m_hat / (jnp.sqrt(v_hat) + ADAM_EPS) + ADAM_WD * state(w_ref))
        m_out[...] = mm
        v_out[...] = vv

    spec = pl.BlockSpec((tr, width), lambda i: (i, 0))

    def lead(idx):
        return spec if idx is None else pl.BlockSpec((1, tr, width), lambda i: (idx, i, 0))

    return pl.pallas_call(
        body, name=name, grid=(rows // tr,), in_specs=[lead(idx) for _, idx in g_parts] + [lead(layer)] * 3, out_specs=[spec] * 4,
        out_shape=[jax.ShapeDtypeStruct((rows, width), f32)] * 4, compiler_params=_cparams(("parallel",)),
    )(*[a for a, _ in g_parts], w, m, v)


def _row_tile(rows, cap=512):
    for cand in range(min(rows, cap) // SUBLANES * SUBLANES, 0, -SUBLANES):
        if rows % cand == 0:
            return cand
    return rows


def _cols_from_devices(g, width, name):
    n_dev, depth, a, b = g.shape

    def body(g_ref, o_ref):
        for i in range(n_dev):
            o_ref[0, :, i * b:(i + 1) * b] = g_ref[i, 0]
        if width > n_dev * b:
            o_ref[0, :, n_dev * b:width] = jnp.zeros((a, width - n_dev * b), o_ref.dtype)

    return pl.pallas_call(
        body, name=name, grid=(depth,), in_specs=[pl.BlockSpec((n_dev, 1, a, b), lambda l: (0, l, 0, 0))],
        out_specs=pl.BlockSpec((1, a, width), lambda l: (l, 0, 0)), out_shape=jax.ShapeDtypeStruct((depth, a, width), g.dtype),
        compiler_params=_cparams(("parallel",)),
    )(g)


def _devices_from_cols(per_layer, b, name, tr=256):
    depth = len(per_layer)
    a, width = per_layer[0].shape

    def body(*refs):
        o_ref = refs[depth]
        for l in range(depth):
            for i in range(N_DEV):
                o_ref[i, l] = refs[l][:, i * b:(i + 1) * b]

    return pl.pallas_call(
        body, name=name, grid=(a // tr,), in_specs=[pl.BlockSpec((tr, width), lambda r: (r, 0))] * depth,
        out_specs=pl.BlockSpec((N_DEV, depth, tr, b), lambda r: (0, 0, r, 0)),
        out_shape=jax.ShapeDtypeStruct((N_DEV, depth, a, b), per_layer[0].dtype), compiler_params=_cparams(("parallel",)),
    )(*per_layer)


def _add_kept(g, recv, core, name, out_dtype=bf16):
    nblk, _, rows, width = g.shape
    tr = _row_tile(rows)

    def body(c_ref, g_ref, r_ref, o_ref):
        o_ref[0] = (g_ref[0, 0] + r_ref[0]).astype(out_dtype)

    grid_spec = pltpu.PrefetchScalarGridSpec(
        num_scalar_prefetch=1, grid=(nblk, rows // tr),
        in_specs=[pl.BlockSpec((1, 1, tr, width), lambda i, j, c: (i, c[0], j, 0)), pl.BlockSpec((1, tr, width), lambda i, j, c: (i, j, 0))],
        out_specs=pl.BlockSpec((1, tr, width), lambda i, j, c: (i, j, 0)))
    return pl.pallas_call(
        body, name=name, grid_spec=grid_spec, out_shape=jax.ShapeDtypeStruct((nblk, rows, width), out_dtype),
        compiler_params=_cparams(("parallel", "parallel")),
    )(core, g, recv)


def _me():
    return lax.axis_index("x"), lax.axis_index("y"), lax.axis_index("c")


def _allgather_two_level(shards, name):
    n = len(shards)
    per = 7

    def body(*refs):
        ins, outs = refs[:n], refs[n:2 * n]
        send_sems, recv_sems, local_sems = refs[2 * n:]
        x, y, c = _me()
        me, sibling = (x, y, c), (x, y, 1 - c)
        chips = [(1 - x, y), (x, 1 - y), (1 - x, 1 - y)]

        def slot(a, p):
            return outs[a].at[4 * p[0] + 2 * p[1] + p[2]]

        def copy(a, k, block, to, src=None):
            return pltpu.make_async_remote_copy(
                src_ref=slot(a, block) if src is None else src, dst_ref=slot(a, block),
                send_sem=send_sems.at[a * per + k], recv_sem=recv_sems.at[a * per + k], device_id=to, device_id_type=MESH)

        mine = [pltpu.make_async_copy(ins[a], slot(a, me), local_sems.at[a]) for a in range(n)]
        for cp in mine:
            cp.start()
        first = []
        for a in range(n):
            first.append(copy(a, 0, me, sibling, src=ins[a]))
            first += [copy(a, 1 + j, me, (*chip, c), src=ins[a]) for j, chip in enumerate(chips)]
        for cp in first:
            cp.start()
        passed = []
        for j, chip in enumerate(chips):
            for a in range(n):
                copy(a, 1 + j, (*chip, c), me).wait_recv()
                fwd = copy(a, 4 + j, (*chip, c), sibling)
                fwd.start()
                passed.append(fwd)
        for a in range(n):
            copy(a, 0, sibling, me).wait_recv()
            for j, chip in enumerate(chips):
                copy(a, 4 + j, (*chip, 1 - c), me).wait_recv()
        for cp in first + passed:
            cp.wait_send()
        for cp in mine:
            cp.wait()

    return pl.pallas_call(
        body, name=name, in_specs=[ANY] * n, out_specs=[ANY] * n,
        out_shape=[jax.ShapeDtypeStruct((N_DEV,) + s.shape, s.dtype) for s in shards],
        scratch_shapes=[pltpu.SemaphoreType.DMA((n * per,)), pltpu.SemaphoreType.DMA((n * per,)), pltpu.SemaphoreType.DMA((n,))],
    )(*shards)


def _allgather_direct(row, name):
    def body(in_ref, out_ref, send_sems, recv_sems, local_sem):
        x, y, c = _me()
        mine = out_ref.at[4 * x + 2 * y + c]
        local = pltpu.make_async_copy(in_ref, mine, local_sem)
        local.start()
        sends = []
        for k in range(1, N_DEV):
            px, py, pc = x ^ (k >> 2), y ^ ((k >> 1) & 1), c ^ (k & 1)
            sends.append(pltpu.make_async_remote_copy(
                src_ref=in_ref, dst_ref=mine, send_sem=send_sems.at[k - 1], recv_sem=recv_sems.at[k - 1],
                device_id=(px, py, pc), device_id_type=MESH))
        for cp in sends:
            cp.start()
        for k in range(1, N_DEV):
            px, py, pc = x ^ (k >> 2), y ^ ((k >> 1) & 1), c ^ (k & 1)
            theirs = out_ref.at[4 * px + 2 * py + pc]
            pltpu.make_async_remote_copy(
                src_ref=in_ref, dst_ref=theirs, send_sem=send_sems.at[k - 1], recv_sem=recv_sems.at[k - 1],
                device_id=(px, py, pc), device_id_type=MESH).wait_recv()
        for cp in sends:
            cp.wait_send()
        local.wait()

    return pl.pallas_call(
        body, name=name, in_specs=[ANY], out_specs=ANY, out_shape=jax.ShapeDtypeStruct((N_DEV,) + row.shape, row.dtype),
        scratch_shapes=[pltpu.SemaphoreType.DMA((N_DEV - 1,)), pltpu.SemaphoreType.DMA((N_DEV - 1,)), pltpu.SemaphoreType.DMA],
    )(row)


N_CHIP = N_DEV // 2
HBM = pl.BlockSpec(memory_space=pltpu.HBM)
SEM = pl.BlockSpec(memory_space=pltpu.SEMAPHORE)
EFFECT = pltpu.SideEffectType.DATAFLOW_SIDE_EFFECTING


def _peer(k):
    x, y, c = _me()
    return x ^ (k >> 2), y ^ ((k >> 1) & 1), c ^ (k & 1)


def _direct_copies(srcs, lands, send_sems, recv_sems, per_peer):
    x, y, c = _me()
    me = 4 * x + 2 * y + c
    copies = []
    for a in range(len(srcs)):
        for k in range(1, N_DEV):
            px, py, pc = _peer(k)
            piece = srcs[a].at[4 * px + 2 * py + pc] if per_peer else srcs[a]
            copies.append(pltpu.make_async_remote_copy(
                src_ref=piece, dst_ref=lands[a].at[me], send_sem=send_sems.at[a * (N_DEV - 1) + k - 1],
                recv_sem=recv_sems.at[a * (N_DEV - 1) + k - 1], device_id=(px, py, pc), device_id_type=MESH))
    return copies


def _direct_start(srcs, lands, per_peer, name):
    n = len(srcs)
    n_sem = n * (N_DEV - 1)

    def body(*refs):
        src_refs, land_refs = refs[:n], refs[n:2 * n]
        send_sems, recv_sems = refs[2 * n], refs[2 * n + 1]
        token = refs[-1]
        for cp in _direct_copies(src_refs, land_refs, send_sems, recv_sems, per_peer):
            cp.start()
        token[...] = jnp.zeros_like(token)

    outs = pl.pallas_call(
        body, name=name,
        out_shape=(pltpu.SemaphoreType.DMA((n_sem,)), pltpu.SemaphoreType.DMA((n_sem,)),
                   *[pltpu.HBM(s.shape, s.dtype) for s in srcs], *[pltpu.HBM(s.shape, s.dtype) for s in lands],
                   jax.ShapeDtypeStruct((SUBLANES, LANES), f32)),
        in_specs=[HBM] * (2 * n), out_specs=(SEM, SEM, *[HBM] * (2 * n), pl.BlockSpec(memory_space=pltpu.VMEM)),
        input_output_aliases={i: 2 + i for i in range(2 * n)},
        compiler_params=pltpu.CompilerParams(has_side_effects=EFFECT),
    )(*[pltpu.with_memory_space_constraint(s, pltpu.HBM) for s in srcs], *[pltpu.with_memory_space_constraint(s, pltpu.HBM) for s in lands])
    return outs[0], outs[1], outs[2:2 + n], outs[2 + n:2 + 2 * n], outs[-1]


def _direct_wait(send_sems, recv_sems, srcs, lands, after, per_peer, name):
    n = len(srcs)

    def body(*refs):
        src_refs, land_refs = refs[:n], refs[n:2 * n]
        s_sems, r_sems = refs[2 * n], refs[2 * n + 1]
        for cp in _direct_copies(src_refs, land_refs, s_sems, r_sems, per_peer):
            cp.wait_send()
            cp.wait_recv()

    outs = pl.pallas_call(
        body, name=name,
        out_shape=tuple(pltpu.HBM(s.shape, s.dtype) for s in list(srcs) + list(lands)),
        in_specs=[HBM] * (2 * n) + [SEM, SEM, ANY], out_specs=tuple([HBM] * (2 * n)),
        input_output_aliases={i: i for i in range(2 * n)},
        compiler_params=pltpu.CompilerParams(has_side_effects=EFFECT),
    )(*srcs, *lands, send_sems, recv_sems, after)
    return outs[n:]


def _exchange_sibling(gs, name):
    n = len(gs)

    def body(*refs):
        ins, outs = refs[:n], refs[n:2 * n]
        send_sems, recv_sems = refs[2 * n:]
        x, y, c = _me()
        copies = [pltpu.make_async_remote_copy(
            src_ref=ins[a].at[i, 1 - c], dst_ref=outs[a].at[i], send_sem=send_sems.at[a * N_CHIP + i], recv_sem=recv_sems.at[a * N_CHIP + i],
            device_id=(x, y, 1 - c), device_id_type=MESH) for a in range(n) for i in range(N_CHIP)]
        for cp in copies:
            cp.start()
        for cp in copies:
            cp.wait_recv()
        for cp in copies:
            cp.wait_send()

    return pl.pallas_call(
        body, name=name, in_specs=[ANY] * n, out_specs=[ANY] * n,
        out_shape=[jax.ShapeDtypeStruct((N_CHIP,) + g.shape[2:], g.dtype) for g in gs],
        scratch_shapes=[pltpu.SemaphoreType.DMA((n * N_CHIP,)), pltpu.SemaphoreType.DMA((n * N_CHIP,))],
    )(*gs)


def _exchange_chips(ps, name):
    n = len(ps)

    def body(*refs):
        ins, outs = refs[:n], refs[n:2 * n]
        send_sems, recv_sems = refs[2 * n:]
        x, y, c = _me()
        chips = [(1 - x, y), (x, 1 - y), (1 - x, 1 - y)]
        copies = [pltpu.make_async_remote_copy(
            src_ref=ins[a].at[2 * cx + cy], dst_ref=outs[a].at[k], send_sem=send_sems.at[a * 3 + k], recv_sem=recv_sems.at[a * 3 + k],
            device_id=(cx, cy, c), device_id_type=MESH) for a in range(n) for k, (cx, cy) in enumerate(chips)]
        for cp in copies:
            cp.start()
        for cp in copies:
            cp.wait_recv()
        for cp in copies:
            cp.wait_send()

    return pl.pallas_call(
        body, name=name, in_specs=[ANY] * n, out_specs=[ANY] * n,
        out_shape=[jax.ShapeDtypeStruct((3,) + p.shape[1:], p.dtype) for p in ps],
        scratch_shapes=[pltpu.SemaphoreType.DMA((n * 3,)), pltpu.SemaphoreType.DMA((n * 3,))],
    )(*ps)


def _row(v, width=None):
    v = v.reshape(1, -1).astype(f32)
    if width is not None and v.shape[1] < width:
        v = jnp.pad(v, ((0, 0), (0, width - v.shape[1])))
    return v


def _layer_params(p, l):
    return dict(
        norm_mix=_row(p["norm_mix"][l]), norm_ffn=_row(p["norm_ffn"][l]), conv_w=p["conv_w"][l], conv_b=_row(p["conv_b"][l]),
        ssd=(_row(p["dt_bias"][l], LANES), _row(p["a_log"][l], LANES), _row(jnp.repeat(p["d_skip"][l], HEAD_DIM)), _row(p["ssm_norm"][l])))


def _layer_fwd(h, big, sp, tabs, l):
    tag = f"l{l}_"
    w_in, w_out, w_gate, w_up, w_down = big
    hn = _rmsnorm_fwd(h, sp["norm_mix"], tag + "norm_mix")
    qkv = _matmul(hn, w_in, mode="nn", n_out=QKV_WIDTH, tn=256, b_off=0, name=tag + "proj_qkv")
    z = _matmul(hn, w_in, mode="nn", n_out=SSM_INNER, tn=256, b_off=Z_OFF // 256, name=tag + "proj_z")
    xbc_pre = _matmul(hn, w_in, mode="nn", n_out=CONV_CH, tn=256, b_off=XBC_OFF // 256, name=tag + "proj_xbc")
    dtp = _matmul(hn, w_in, mode="nn", n_out=LANES, tn=LANES, b_off=DT_OFF // LANES, name=tag + "proj_dt")
    o, lse = _attn_fwd(qkv, tabs, tag + "attn_fwd")
    xbc = _conv_fwd(xbc_pre, sp["conv_w"], sp["conv_b"], tag + "conv_fwd")
    yn, y, hs = _ssd_fwd(xbc, z, dtp, sp["ssd"], tag + "ssd_fwd")
    t1 = _matmul(o, w_out, mode="nn", k_len=ATTN_WIDTH, tk=512, add=h, name=tag + "out_attn")
    h2 = _matmul(yn, w_out, mode="nn", k_len=SSM_INNER, tk=512, b_koff=1, add=t1, name=tag + "out_ssm")
    hn2 = _rmsnorm_fwd(h2, sp["norm_ffn"], tag + "norm_ffn")
    g, u, act = _swiglu_fwd(hn2, w_gate, w_up, tag + "ffn_up")
    h3 = _matmul(act, w_down, mode="nn", tk=1408, add=h2, name=tag + "ffn_down")
    saved = dict(h=h, hn=hn, qkv=qkv, z=z, xbc_pre=xbc_pre, dtp=dtp, o=o, lse=lse, xbc=xbc, yn=yn, y=y, hs=hs, h2=h2, hn2=hn2, g=g, u=u, act=act)
    return h3, saved


def _layer_bwd(dh3, s, big, sp, tabs, l, gd=f32):
    tag = f"l{l}_"
    w_in, w_out, w_gate, w_up, w_down = big
    dg, du = _swiglu_bwd(dh3, w_down, s["g"], s["u"], tag + "ffn_down_bwd")
    dw_down = _matmul(s["act"], dh3, mode="tn", tm=1408, tn=512, tk=2048, out_dtype=gd, name=tag + "dw_down")
    dhn2 = _matmul(dg, w_gate, mode="nt", tk=1408, name=tag + "ffn_gate_bwd")
    dhn2 = _matmul(du, w_up, mode="nt", tk=1408, add=dhn2, name=tag + "ffn_up_bwd")
    dw_gate = _matmul(s["hn2"], dg, mode="tn", tm=512, tn=1408, tk=2048, out_dtype=gd, name=tag + "dw_gate")
    dw_up = _matmul(s["hn2"], du, mode="tn", tm=512, tn=1408, tk=2048, out_dtype=gd, name=tag + "dw_up")
    dh2, dnf = _rmsnorm_bwd(dhn2, s["h2"], sp["norm_ffn"], dh3, tag + "norm_ffn_bwd")
    d_o = _matmul(dh2, w_out, mode="nt", n_out=ATTN_WIDTH, tn=512, b_off=0, name=tag + "out_attn_bwd")
    dyn = _matmul(dh2, w_out, mode="nt", n_out=SSM_INNER, tn=512, b_off=1, name=tag + "out_ssm_bwd")
    dw_out = jnp.concatenate([_matmul(s["o"], dh2, mode="tn", tm=512, tn=512, tk=2048, out_dtype=gd, name=tag + "dw_out_attn"),
                              _matmul(s["yn"], dh2, mode="tn", tm=512, tn=512, tk=2048, out_dtype=gd, name=tag + "dw_out_ssm")], axis=0)
    dxbc, dz, ddtp, dnw, dds, dal, dbi = _ssd_bwd(s["xbc"], s["z"], s["dtp"], s["y"], s["hs"], dyn, sp["ssd"], tag + "ssd_bwd")
    dxbc_pre, dconv_w, dconv_b = _conv_bwd(s["xbc_pre"], sp["conv_w"], sp["conv_b"], dxbc, tag + "conv_bwd")
    dq, dk, dv = _attn_bwd(s["qkv"], tabs, s["o"], s["lse"], d_o, tag + "attn_bwd")
    dproj = jnp.concatenate([dq.astype(bf16), dk.astype(bf16), dv.astype(bf16), dz.astype(bf16), dxbc_pre.astype(bf16), ddtp.astype(bf16)], axis=1)
    dhn = _matmul(dproj, w_in, mode="nt", tk=1152, name=tag + "proj_bwd")
    dw_in = _matmul(s["hn"], dproj, mode="tn", tm=512, tn=1152, tk=2048, out_dtype=gd, name=tag + "dw_in")
    dh, dnm = _rmsnorm_bwd(dhn, s["h"], sp["norm_mix"], dh2, tag + "norm_mix_bwd")
    grads = dict(
        norm_mix=dnm.sum(0), w_in=dw_in, conv_w=dconv_w, conv_b=dconv_b[0], dt_bias=dbi.sum(0)[:SSM_HEADS], a_log=dal.sum(0)[:SSM_HEADS],
        d_skip=dds.sum(0).reshape(SSM_HEADS, HEAD_DIM).sum(1), ssm_norm=dnw.sum(0), w_out=dw_out, norm_ffn=dnf.sum(0),
        w_gate=dw_gate, w_up=dw_up, w_down=dw_down)
    return dh, grads


def _local_step(x, positions, target, p, bigs):
    tabs = _rope_tables(positions.reshape(-1, 1), "rope_tables")
    h = x
    saved, sps = [], []
    for l in range(DEPTH):
        sps.append(_layer_params(p, l))
        h, s = _layer_fwd(h, bigs[l], sps[l], tabs, l)
        saved.append(s)
    dh, loss_parts, dfn = _final_loss(h, _row(p["final_norm"]), target, "final_loss")
    layer_grads = [None] * DEPTH
    for l in reversed(range(DEPTH)):
        dh, layer_grads[l] = _layer_bwd(dh, saved[l], bigs[l], sps[l], tabs, l)
    grads = {k: [layer_grads[l][k] for l in range(DEPTH)] for k in layer_grads[0]}
    grads["final_norm"] = dfn.sum(0)
    return jnp.sum(loss_parts), dh, grads


BIG = ("w_in", "w_out", "w_gate", "w_up", "w_down")
COL_SHARDED = ("w_in", "w_gate", "w_up")
SMALL = ("norm_mix", "conv_b", "dt_bias", "a_log", "d_skip", "ssm_norm", "norm_ffn", "final_norm")
WEIGHTS = ("norm_mix", "w_in", "conv_w", "conv_b", "dt_bias", "a_log", "d_skip", "ssm_norm", "w_out", "norm_ffn", "w_gate", "w_up", "w_down", "final_norm")
PACK_W = 1024
SMALL_ROWS = 88
CONVW_ROWS = 96
CONVW_SHARD_ROWS = 16


def _full_from_gathered(name, g, l):
    _, a, b = g.shape
    if name in COL_SHARDED:
        width = IN_PROJ_PAD if name == "w_in" else N_DEV * b
        return _cols_from_devices(g.reshape(N_DEV, 1, a, b), width, f"cols_l{l}_{name}").reshape(a, width)
    return g.reshape(N_DEV * a, b)


def _by_device(name, full, shard_shape, l):
    a, b = shard_shape
    if name in COL_SHARDED:
        return _devices_from_cols([full], b, f"devs_l{l}_{name}").reshape(N_CHIP, 2, a, b)
    return full.reshape(N_CHIP, 2, a, b)


def _pack_rows(parts, rows, width):
    flat = jnp.concatenate([q.reshape(-1) for q in parts])
    return jnp.pad(flat, (0, rows * width - flat.shape[0])).reshape(rows, width)


def _unpack(flat, like):
    out, off = [], 0
    for q in like:
        out.append(flat[off:off + q.size].reshape(q.shape))
        off += q.size
    return out


def kernel(x, positions, norm_mix, w_in, conv_w, conv_b, dt_bias, a_log, d_skip, ssm_norm, w_out, norm_ffn, w_gate, w_up, w_down, final_norm, loss_target, m_norm_mix, m_w_in, m_conv_w, m_conv_b, m_dt_bias, m_a_log, m_d_skip, m_ssm_norm, m_w_out, m_norm_ffn, m_w_gate, m_w_up, m_w_down, m_final_norm, v_norm_mix, v_w_in, v_conv_w, v_conv_b, v_dt_bias, v_a_log, v_d_skip, v_ssm_norm, v_w_out, v_norm_ffn, v_w_gate, v_w_up, v_w_down, v_final_norm):
    w = dict(norm_mix=norm_mix, w_in=w_in, conv_w=conv_w, conv_b=conv_b, dt_bias=dt_bias, a_log=a_log, d_skip=d_skip, ssm_norm=ssm_norm,
             w_out=w_out, norm_ffn=norm_ffn, w_gate=w_gate, w_up=w_up, w_down=w_down, final_norm=final_norm)
    m = dict(norm_mix=m_norm_mix, w_in=m_w_in, conv_w=m_conv_w, conv_b=m_conv_b, dt_bias=m_dt_bias, a_log=m_a_log, d_skip=m_d_skip,
             ssm_norm=m_ssm_norm, w_out=m_w_out, norm_ffn=m_norm_ffn, w_gate=m_w_gate, w_up=m_w_up, w_down=m_w_down, final_norm=m_final_norm)
    v = dict(norm_mix=v_norm_mix, w_in=v_w_in, conv_w=v_conv_w, conv_b=v_conv_b, dt_bias=v_dt_bias, a_log=v_a_log, d_skip=v_d_skip,
             ssm_norm=v_ssm_norm, w_out=v_w_out, norm_ffn=v_norm_ffn, w_gate=v_w_gate, w_up=v_w_up, w_down=v_w_down, final_norm=v_final_norm)
    ax, ay, ac = lax.axis_index("x"), lax.axis_index("y"), lax.axis_index("c")
    dev = 4 * ax + 2 * ay + ac

    assert DEPTH == 2
    t = x.shape[0] * x.shape[1]
    xf, target = x.reshape(t, D_MODEL), loss_target.reshape(t, D_MODEL)

    def own_slot(block):
        return lax.dynamic_update_slice(jnp.zeros((N_DEV,) + block.shape[1:], block.dtype), block, (dev,) + (0,) * (block.ndim - 1))

    shards1 = [w[k][1].astype(bf16) for k in BIG]
    g_send, g_recv, shards1, lands1, g_token = _direct_start(shards1, [own_slot(s[None]) for s in shards1], False, "gather_l1_start")
    tie = g_token[0, 0]
    gathered0 = _allgather_two_level([(w["w_in"][0] + tie).astype(bf16)] + [w[k][0].astype(bf16) for k in BIG[1:]] + [w["conv_w"]], "gather_l0")
    p = {k: w[k] for k in SMALL}
    p["conv_w"] = jnp.transpose(gathered0[-1], (1, 2, 0, 3)).reshape(DEPTH, CONV_WIDTH, CONV_CH)
    sp0, sp1 = _layer_params(p, 0), _layer_params(p, 1)
    bigs0 = tuple(_full_from_gathered(k, g, 0) for k, g in zip(BIG, gathered0))

    tabs = _rope_tables(positions.reshape(t, 1), "rope_tables")
    h1, saved0 = _layer_fwd(xf, bigs0, sp0, tabs, 0)
    lands1 = _direct_wait(g_send, g_recv, shards1, lands1, h1, False, "gather_l1_wait")
    bigs1 = tuple(_full_from_gathered(k, g, 1) for k, g in zip(BIG, lands1))
    h2, saved1 = _layer_fwd(h1, bigs1, sp1, tabs, 1)
    dh, loss_parts, dfn = _final_loss(h2, _row(p["final_norm"]), target, "final_loss")
    loss_local = jnp.sum(loss_parts)

    dh, grads1 = _layer_bwd(dh, saved1, bigs1, sp1, tabs, 1, gd=bf16)
    by_dev1 = [_by_device(k, grads1[k], w[k].shape[1:], 1).reshape((N_DEV,) + w[k].shape[1:]) for k in BIG]
    lands_s = [own_slot(lax.dynamic_slice_in_dim(g, dev, 1, 0)) for g in by_dev1]
    s_send, s_recv, by_dev1, lands_s, s_token = _direct_start(by_dev1, lands_s, True, "scatter_l1_start")
    dx, grads0 = _layer_bwd(dh, saved0, bigs0, dict(sp0, norm_ffn=sp0["norm_ffn"] + s_token[0, 0]), tabs, 0)
    lands_s = _direct_wait(s_send, s_recv, by_dev1, lands_s, dx, True, "scatter_l1_wait")
    core = ac.reshape(1).astype(jnp.int32)
    by_dev0 = [_by_device(k, grads0[k], w[k].shape[1:], 0) for k in BIG]
    from_sibling = _exchange_sibling(by_dev0, "scatter_l0_sibling")
    chip_sums = [_add_kept(g, r, core, "scatter_l0_add_" + k) for k, g, r in zip(BIG, by_dev0, from_sibling)]
    from_chips = _exchange_chips(chip_sums, "scatter_l0_chips")
    out_g, out_d, out_m, out_v = {}, {}, {}, {}
    for k, cs, fc, ls in zip(BIG, chip_sums, from_chips, lands_s):
        own = lax.dynamic_index_in_dim(cs, 2 * ax + ay, 0, keepdims=False)
        res0 = _adamw([(own, None), (fc, 0), (fc, 1), (fc, 2)], w[k], m[k], v[k], "adamw_l0_" + k, layer=0)
        res1 = _adamw([(ls, i) for i in range(N_DEV)], w[k], m[k], v[k], "adamw_l1_" + k, layer=1)
        for dst, r0, r1 in zip((out_g, out_d, out_m, out_v), res0, res1):
            dst[k] = jnp.stack([r0, r1])
    grads = {k: [grads0[k], grads1[k]] for k in grads0 if k not in BIG}
    grads["final_norm"] = dfn.sum(0)

    small_like = [w[k] for k in SMALL]
    small_grads = [jnp.stack(grads[k]) if k != "final_norm" else grads[k] for k in SMALL]
    small_pack = jnp.concatenate([_pack_rows(small_grads, SMALL_ROWS, LANES), _pack_rows([jnp.stack(grads["conv_w"])], CONVW_ROWS, LANES)], axis=0)
    parts = _allgather_direct(small_pack, "gather_small_grads")
    g_s, d_s, m_s, v_s = _adamw(
        [(parts[i, :SMALL_ROWS], None) for i in range(N_DEV)], _pack_rows(small_like, SMALL_ROWS, LANES),
        _pack_rows([m[k] for k in SMALL], SMALL_ROWS, LANES), _pack_rows([v[k] for k in SMALL], SMALL_ROWS, LANES), "adamw_replicated")
    for dst, src in ((out_g, g_s), (out_d, d_s), (out_m, m_s), (out_v, v_s)):
        dst.update(zip(SMALL, _unpack(src.reshape(-1), small_like)))
    shard_w = conv_w.shape[-1]
    conv_parts = parts[:, SMALL_ROWS:].reshape(N_DEV, DEPTH, CONV_WIDTH, CONV_CH)
    conv_mine = lax.dynamic_slice_in_dim(conv_parts, dev * shard_w, shard_w, axis=3)
    g_c, d_c, m_c, v_c = _adamw(
        [(_pack_rows([conv_mine[i]], CONVW_SHARD_ROWS, LANES), None) for i in range(N_DEV)], _pack_rows([conv_w], CONVW_SHARD_ROWS, LANES),
        _pack_rows([m["conv_w"]], CONVW_SHARD_ROWS, LANES), _pack_rows([v["conv_w"]], CONVW_SHARD_ROWS, LANES), "adamw_conv_w")
    for dst, src in ((out_g, g_c), (out_d, d_c), (out_m, m_c), (out_v, v_c)):
        dst["conv_w"] = src.reshape(-1)[:conv_w.size].reshape(conv_w.shape)

    loss = lax.psum(loss_local, ("x", "y", "c"))
    return (loss, dx.reshape(x.shape), *[out_g[k] for k in WEIGHTS], *[out_d[k] for k in WEIGHTS],
            *[out_m[k] for k in WEIGHTS], *[out_v[k] for k in WEIGHTS])
```

```python
import functools
import math

import jax
import jax.numpy as jnp
import numpy as np
from jax import lax
from jax.experimental import pallas as pl
from jax.experimental.pallas import tpu as pltpu

f32 = jnp.float32
bf16 = jnp.bfloat16

D_MODEL = 1024
SEQ = 2048
DEPTH = 2
HEAD_DIM = 64
N_ATTN_HEADS = 8
N_KV_HEADS = 2
ATTN_WIDTH = 512
KV_WIDTH = 128
ROPE_DIM = 16
ROPE_THETA = 500000.0
DILATIONS = (1, 4, 16)
ATTN_BLOCK = 128
SSM_HEADS = 16
SSM_INNER = 1024
SSM_GROUPS = 2
D_STATE = 128
CONV_WIDTH = 4
CHUNK = 128
CONV_CH = 1536
MIX_WIDTH = 1536
QKV_WIDTH = ATTN_WIDTH + 2 * KV_WIDTH
Z_OFF = 768
XBC_OFF = 1792
DT_OFF = 3328
IN_PROJ = 3344
IN_PROJ_PAD = 3456
FFN_HIDDEN = 2816
EPS = 1e-5
N_DEV = 8
ADAM_LR = 0.001
ADAM_B1 = 0.9
ADAM_B2 = 0.999
ADAM_EPS = 1e-08
ADAM_WD = 0.01
ADAM_STEP = 10

LANES = 128
SUBLANES = 8
VMEM_LIMIT = 56 * 1024 * 1024

MESH = pl.DeviceIdType.MESH
ANY = pl.BlockSpec(memory_space=pl.ANY)


def _cparams(sem, vmem=None):
    return pltpu.CompilerParams(dimension_semantics=sem, vmem_limit_bytes=vmem or VMEM_LIMIT)


def _sigmoid(x):
    return 1.0 / (1.0 + jnp.exp(-x))


def _silu(x):
    return x * _sigmoid(x)


def _dsilu(x):
    s = _sigmoid(x)
    return s * (1.0 + x * (1.0 - s))


def _softplus(x):
    return jnp.maximum(x, 0.0) + jnp.log(1.0 + jnp.exp(-jnp.abs(x)))


def _dot(a, b, dims, precision=None):
    return lax.dot_general(a, b, (dims, ((), ())), preferred_element_type=f32, precision=precision)


def _nn(a, b, precision=None):
    return _dot(a, b, ((1,), (0,)), precision)


def _nt(a, b):
    return _dot(a, b, ((1,), (1,)))


def _tn(a, b):
    return _dot(a, b, ((0,), (0,)))


def _rowsum8(t):
    n, w = t.shape
    return jnp.sum(t.reshape(n // SUBLANES, SUBLANES, w), axis=0)


def _matmul(a, b, *, mode, n_out=None, b_off=0, a_koff=0, b_koff=0, k_len=None, add=None, out_dtype=f32, tm=2048, tn=512, tk=1024, name):
    if mode == "tn":
        kdim_a, m = a.shape
    else:
        m, kdim_a = a.shape
    kk = k_len if k_len is not None else kdim_a
    n = n_out if n_out is not None else (b.shape[0] if mode == "nt" else b.shape[1])
    tm, tn, tk = min(tm, m), min(tn, n), min(tk, kk)
    assert m % tm == 0 and n % tn == 0 and kk % tk == 0, (name, m, n, kk, tm, tn, tk)
    nk = kk // tk
    if mode == "nn":
        a_spec = pl.BlockSpec((tm, tk), lambda i, j, k: (i, k + a_koff))
        b_spec = pl.BlockSpec((tk, tn), lambda i, j, k: (k + b_koff, j + b_off))
        dims = ((1,), (0,))
    elif mode == "nt":
        a_spec = pl.BlockSpec((tm, tk), lambda i, j, k: (i, k + a_koff))
        b_spec = pl.BlockSpec((tn, tk), lambda i, j, k: (j + b_off, k + b_koff))
        dims = ((1,), (1,))
    else:
        a_spec = pl.BlockSpec((tk, tm), lambda i, j, k: (k + a_koff, i))
        b_spec = pl.BlockSpec((tk, tn), lambda i, j, k: (k + b_koff, j + b_off))
        dims = ((0,), (0,))
    o_spec = pl.BlockSpec((tm, tn), lambda i, j, k: (i, j))
    has_add = add is not None

    def body(*refs):
        if has_add:
            a_ref, b_ref, add_ref, o_ref, acc_ref = refs
        else:
            a_ref, b_ref, o_ref, acc_ref = refs
        k = pl.program_id(2)
        part = _dot(a_ref[...].astype(bf16), b_ref[...].astype(bf16), dims)

        @pl.when(k == 0)
        def _():
            acc_ref[...] = part

        @pl.when(k > 0)
        def _():
            acc_ref[...] += part

        @pl.when(k == nk - 1)
        def _():
            r = acc_ref[...]
            if has_add:
                r = r + add_ref[...]
            o_ref[...] = r.astype(out_dtype)

    in_specs = [a_spec, b_spec] + ([o_spec] if has_add else [])
    args = (a, b) + ((add,) if has_add else ())
    return pl.pallas_call(
        body, name=name, grid=(m // tm, n // tn, nk), in_specs=in_specs, out_specs=o_spec,
        out_shape=jax.ShapeDtypeStruct((m, n), out_dtype), scratch_shapes=[pltpu.VMEM((tm, tn), f32)],
        compiler_params=_cparams(("parallel", "parallel", "arbitrary")),
    )(*args)


def _swiglu_fwd(hn, w_gate, w_up, name, tm=2048, tn=256):
    m, k = hn.shape
    n = w_gate.shape[1]

    def body(a_ref, wg_ref, wu_ref, g_ref, u_ref, act_ref):
        a = a_ref[...]
        g = _nn(a, wg_ref[...])
        u = _nn(a, wu_ref[...])
        g_ref[...] = g.astype(bf16)
        u_ref[...] = u.astype(bf16)
        act_ref[...] = (_silu(g) * u).astype(bf16)

    a_spec = pl.BlockSpec((tm, k), lambda i, j: (i, 0))
    w_spec = pl.BlockSpec((k, tn), lambda i, j: (0, j))
    o_spec = pl.BlockSpec((tm, tn), lambda i, j: (i, j))
    return pl.pallas_call(
        body, name=name, grid=(m // tm, n // tn), in_specs=[a_spec, w_spec, w_spec], out_specs=[o_spec, o_spec, o_spec],
        out_shape=[jax.ShapeDtypeStruct((m, n), bf16)] * 3,
        compiler_params=_cparams(("parallel", "parallel")),
    )(hn, w_gate, w_up)


def _swiglu_bwd(dh, w_down, g, u, name, tm=2048, tn=256):
    m, k = dh.shape
    n = w_down.shape[0]

    def body(a_ref, w_ref, g_ref, u_ref, dg_ref, du_ref):
        dact = _nt(a_ref[...].astype(bf16), w_ref[...])
        gg = g_ref[...].astype(f32)
        dg_ref[...] = (dact * u_ref[...].astype(f32) * _dsilu(gg)).astype(bf16)
        du_ref[...] = (dact * _silu(gg)).astype(bf16)

    a_spec = pl.BlockSpec((tm, k), lambda i, j: (i, 0))
    w_spec = pl.BlockSpec((tn, k), lambda i, j: (j, 0))
    o_spec = pl.BlockSpec((tm, tn), lambda i, j: (i, j))
    return pl.pallas_call(
        body, name=name, grid=(m // tm, n // tn), in_specs=[a_spec, w_spec, o_spec, o_spec], out_specs=[o_spec, o_spec],
        out_shape=[jax.ShapeDtypeStruct((m, n), bf16), jax.ShapeDtypeStruct((m, n), bf16)],
        compiler_params=_cparams(("parallel", "parallel")),
    )(dh, w_down, g, u)


def _rmsnorm_fwd(h, w, name, tm=512):
    m, d = h.shape

    def body(h_ref, w_ref, o_ref):
        x = h_ref[...]
        r = lax.rsqrt(jnp.mean(x * x, axis=-1, keepdims=True) + EPS)
        o_ref[...] = (x * r * w_ref[...]).astype(bf16)

    return pl.pallas_call(
        body, name=name, grid=(m // tm,),
        in_specs=[pl.BlockSpec((tm, d), lambda i: (i, 0)), pl.BlockSpec((1, d), lambda i: (0, 0))],
        out_specs=pl.BlockSpec((tm, d), lambda i: (i, 0)), out_shape=jax.ShapeDtypeStruct((m, d), bf16),
        compiler_params=_cparams(("parallel",)),
    )(h, w)


def _rmsnorm_bwd(dhn, h, w, dres, name, tm=512):
    m, d = h.shape

    def body(dhn_ref, h_ref, w_ref, dres_ref, dh_ref, dw_ref):
        x = h_ref[...]
        r = lax.rsqrt(jnp.mean(x * x, axis=-1, keepdims=True) + EPS)
        xhat = x * r
        dy = dhn_ref[...]
        gw = dy * w_ref[...]
        dh_ref[...] = dres_ref[...] + r * (gw - xhat * jnp.mean(gw * xhat, axis=-1, keepdims=True))
        part = _rowsum8(dy * xhat)

        @pl.when(pl.program_id(0) == 0)
        def _():
            dw_ref[...] = part

        @pl.when(pl.program_id(0) > 0)
        def _():
            dw_ref[...] += part

    row = pl.BlockSpec((tm, d), lambda i: (i, 0))
    return pl.pallas_call(
        body, name=name, grid=(m // tm,),
        in_specs=[row, row, pl.BlockSpec((1, d), lambda i: (0, 0)), row],
        out_specs=[row, pl.BlockSpec((SUBLANES, d), lambda i: (0, 0))],
        out_shape=[jax.ShapeDtypeStruct((m, d), f32), jax.ShapeDtypeStruct((SUBLANES, d), f32)],
        compiler_params=_cparams(("arbitrary",)),
    )(dhn, h, w, dres)


def _final_loss(h, w, target, name, tm=512):
    m, d = h.shape

    def body(h_ref, w_ref, t_ref, dh_ref, loss_ref, dw_ref):
        x = h_ref[...]
        r = lax.rsqrt(jnp.mean(x * x, axis=-1, keepdims=True) + EPS)
        xhat = x * r
        ww = w_ref[...]
        err = xhat * ww - t_ref[...]
        dy = err * (1.0 / d)
        gw = dy * ww
        dh_ref[...] = r * (gw - xhat * jnp.mean(gw * xhat, axis=-1, keepdims=True))
        lpart = _rowsum8(err * err) * (0.5 / d)
        wpart = _rowsum8(dy * xhat)

        @pl.when(pl.program_id(0) == 0)
        def _():
            loss_ref[...] = lpart
            dw_ref[...] = wpart

        @pl.when(pl.program_id(0) > 0)
        def _():
            loss_ref[...] += lpart
            dw_ref[...] += wpart

    row = pl.BlockSpec((tm, d), lambda i: (i, 0))
    acc = pl.BlockSpec((SUBLANES, d), lambda i: (0, 0))
    return pl.pallas_call(
        body, name=name, grid=(m // tm,),
        in_specs=[row, pl.BlockSpec((1, d), lambda i: (0, 0)), row], out_specs=[row, acc, acc],
        out_shape=[jax.ShapeDtypeStruct((m, d), f32), jax.ShapeDtypeStruct((SUBLANES, d), f32), jax.ShapeDtypeStruct((SUBLANES, d), f32)],
        compiler_params=_cparams(("arbitrary",)),
    )(h, w, target)


def _lane_tables():
    f = np.arange(LANES) % HEAD_DIM
    inv = ROPE_THETA ** (-jnp.arange(0, ROPE_DIM, 2, dtype=f32) / ROPE_DIM)
    invf = jnp.where(f < ROPE_DIM, inv[f % (ROPE_DIM // 2)], 0.0).astype(f32)
    return invf.reshape(1, LANES)


def _rope_tables(pos_col, name):
    t = pos_col.shape[0]
    tm = SEQ

    def body(p_ref, f_ref, c_ref, s1_ref, s2_ref):
        ang = p_ref[...].astype(f32) * f_ref[...]
        co, si = jnp.cos(ang), jnp.sin(ang)
        f = lax.broadcasted_iota(jnp.int32, (tm, LANES), 1) % HEAD_DIM
        c_ref[...] = jnp.where(f < ROPE_DIM, co, 1.0)
        s1_ref[...] = jnp.where(f < ROPE_DIM // 2, -si, 0.0)
        s2_ref[...] = jnp.where((f >= ROPE_DIM // 2) & (f < ROPE_DIM), si, 0.0)

    row = pl.BlockSpec((tm, LANES), lambda i: (i, 0))
    return pl.pallas_call(
        body, name=name, grid=(t // tm,),
        in_specs=[pl.BlockSpec((tm, 1), lambda i: (i, 0)), pl.BlockSpec((1, LANES), lambda i: (0, 0))],
        out_specs=[row, row, row], out_shape=[jax.ShapeDtypeStruct((t, LANES), f32)] * 3,
        compiler_params=_cparams(("parallel",)),
    )(pos_col, _lane_tables())


def _rot(x, c, s1, s2):
    return x * c + pltpu.roll(x, LANES - ROPE_DIM // 2, 1) * s1 + pltpu.roll(x, ROPE_DIM // 2, 1) * s2


def _rot_t(g, c, s1, s2):
    return g * c + pltpu.roll(g * s1, ROPE_DIM // 2, 1) + pltpu.roll(g * s2, LANES - ROPE_DIM // 2, 1)


def _dup_head(x, kvh, low):
    a = jnp.where(kvh == 0, x, pltpu.roll(x, HEAD_DIM, 1))
    return jnp.where(low, a, pltpu.roll(a, HEAD_DIM, 1))


def _deinterleave(src_ref, dst_ref, d, dtype):
    length = SEQ // d
    if d == 1:
        dst_ref[...] = src_ref[...].astype(dtype)
    else:
        for r in range(d):
            dst_ref[pl.ds(r * length, length), :] = src_ref[pl.ds(r, length, stride=d), :].astype(dtype)


def _interleave_store(src_ref, dst_ref, d, accumulate):
    length = SEQ // d
    if d == 1:
        if accumulate:
            dst_ref[...] += src_ref[...]
        else:
            dst_ref[...] = src_ref[...]
    else:
        for r in range(d):
            blk = src_ref[pl.ds(r * length, length), :]
            if accumulate:
                dst_ref[pl.ds(r, length, stride=d), :] = dst_ref[pl.ds(r, length, stride=d), :] + blk
            else:
                dst_ref[pl.ds(r, length, stride=d), :] = blk


def _attn_masks():
    qi = lax.broadcasted_iota(jnp.int32, (ATTN_BLOCK, ATTN_BLOCK), 0)
    ki = lax.broadcasted_iota(jnp.int32, (ATTN_BLOCK, ATTN_BLOCK), 1)
    low = lax.broadcasted_iota(jnp.int32, (ATTN_BLOCK, LANES), 1) < HEAD_DIM
    return ki <= qi, ki >= qi, low


NEG_INF = float("-inf")
ATTN_UNROLL = 4


def _attn_fwd(qkv, tabs, name):
    t = qkv.shape[0]
    nb = t // SEQ
    n_blk = SEQ // ATTN_BLOCK

    def body(q_ref, k_ref, v_ref, c_ref, s1_ref, s2_ref, o_ref, lse_ref,
             qr, kr, vr, qd, kd, vd, ob, lb, o0, o1, o2, l0, l1, l2, ss):
        kvh = pl.program_id(1) // 2
        cur_ok, prev_ok, low = _attn_masks()
        lowfull = lax.broadcasted_iota(jnp.int32, (SEQ, LANES), 1) < HEAD_DIM
        c, s1, s2 = c_ref[...], s1_ref[...], s2_ref[...]
        qr[...] = _rot(q_ref[...], c, s1, s2) * (HEAD_DIM ** -0.5)
        kr[...] = _dup_head(_rot(k_ref[...], c, s1, s2), kvh, lowfull)
        vr[...] = _dup_head(v_ref[...], kvh, lowfull)
        onat, lnat = (o0, o1, o2), (l0, l1, l2)
        for bi, d in enumerate(DILATIONS):
            _deinterleave(qr, qd, d, bf16)
            _deinterleave(kr, kd, d, bf16)
            _deinterleave(vr, vd, d, bf16)
            per_res = n_blk // d
            use_prev = per_res > 1

            def scores(n, carry):
                start = pl.multiple_of(n * ATTN_BLOCK, ATTN_BLOCK)
                has_prev = (n % per_res) != 0
                pstart = pl.multiple_of(jnp.maximum(n - 1, 0) * ATTN_BLOCK, ATTN_BLOCK)
                qb = qd[pl.ds(start, ATTN_BLOCK), :]
                kc = kd[pl.ds(start, ATTN_BLOCK), :]
                if use_prev:
                    kp = kd[pl.ds(pstart, ATTN_BLOCK), :]
                for a in range(2):
                    qa = jnp.where(low if a == 0 else ~low, qb, jnp.zeros_like(qb))
                    ss[2 * n + a, :, 0:ATTN_BLOCK] = jnp.where(cur_ok, _nt(qa, kc), NEG_INF)
                    if use_prev:
                        ss[2 * n + a, :, ATTN_BLOCK:2 * ATTN_BLOCK] = jnp.where(prev_ok & has_prev, _nt(qa, kp), NEG_INF)
                return carry

            def softmax_pv(n, carry):
                start = pl.multiple_of(n * ATTN_BLOCK, ATTN_BLOCK)
                pstart = pl.multiple_of(jnp.maximum(n - 1, 0) * ATTN_BLOCK, ATTN_BLOCK)
                vc = vd[pl.ds(start, ATTN_BLOCK), :]
                if use_prev:
                    vp = vd[pl.ds(pstart, ATTN_BLOCK), :]
                outs, lses = [], []
                for a in range(2):
                    sc = ss[2 * n + a, :, 0:ATTN_BLOCK]
                    if use_prev:
                        sp = ss[2 * n + a, :, ATTN_BLOCK:2 * ATTN_BLOCK]
                        m = jnp.max(jnp.maximum(sc, sp), axis=1, keepdims=True)
                        pc, pp = jnp.exp(sc - m), jnp.exp(sp - m)
                        den = jnp.sum(pc + pp, axis=1, keepdims=True)
                        acc = _nn(pc.astype(bf16), vc) + _nn(pp.astype(bf16), vp)
                    else:
                        m = jnp.max(sc, axis=1, keepdims=True)
                        pc = jnp.exp(sc - m)
                        den = jnp.sum(pc, axis=1, keepdims=True)
                        acc = _nn(pc.astype(bf16), vc)
                    outs.append(acc * (1.0 / den))
                    lses.append(m + jnp.log(den))
                ob[pl.ds(start, ATTN_BLOCK), :] = jnp.where(low, outs[0], outs[1])
                lb[pl.ds(start, ATTN_BLOCK), :] = jnp.where(low, lses[0], lses[1])
                return carry

            lax.fori_loop(0, n_blk, scores, 0, unroll=ATTN_UNROLL)
            lax.fori_loop(0, n_blk, softmax_pv, 0, unroll=ATTN_UNROLL)
            _interleave_store(ob, onat[bi], d, False)
            _interleave_store(lb, lnat[bi], d, False)
        la, lbb, lc = l0[...], l1[...], l2[...]
        lm = jnp.maximum(jnp.maximum(la, lbb), lc)
        wa, wb, wc = jnp.exp(la - lm), jnp.exp(lbb - lm), jnp.exp(lc - lm)
        ws = wa + wb + wc
        o_ref[...] = (wa * o0[...] + wb * o1[...] + wc * o2[...]) / ws
        lse_ref[...] = lm + jnp.log(ws)

    def col(jj):
        return pl.BlockSpec((SEQ, LANES), lambda b, j: (b, jj if jj is not None else j))

    tab = pl.BlockSpec((SEQ, LANES), lambda b, j: (b, 0))
    fs = pltpu.VMEM((SEQ, LANES), f32)
    hs = pltpu.VMEM((SEQ, LANES), bf16)
    return pl.pallas_call(
        body, name=name, grid=(nb, ATTN_WIDTH // LANES),
        in_specs=[col(None), col(ATTN_WIDTH // LANES), col(ATTN_WIDTH // LANES + 1), tab, tab, tab],
        out_specs=[col(None), col(None)],
        out_shape=[jax.ShapeDtypeStruct((t, ATTN_WIDTH), f32), jax.ShapeDtypeStruct((t, ATTN_WIDTH), f32)],
        scratch_shapes=[fs, fs, fs, hs, hs, hs, fs, fs, fs, fs, fs, fs, fs, fs, pltpu.VMEM((2 * n_blk, ATTN_BLOCK, 2 * ATTN_BLOCK), f32)],
        compiler_params=_cparams(("parallel", "parallel")),
    )(qkv, qkv, qkv, *tabs)


def _attn_bwd(qkv, tabs, o, lse, do, name):
    t = qkv.shape[0]
    nb = t // SEQ
    n_blk = SEQ // ATTN_BLOCK
    n_j = ATTN_WIDTH // LANES

    def body(q_ref, k_ref, v_ref, c_ref, s1_ref, s2_ref, o_ref, lse_ref, do_ref, dq_ref, dk_ref, dv_ref,
             qr, kr, vr, dl, qd, kd, vd, dod, lsd, dld, dqd, dkd, dvd, dqa, dka, dva, pb, dsb):
        j = pl.program_id(1)
        pb[2 * n_blk:2 * n_blk + 2] = jnp.zeros((2, ATTN_BLOCK, 2 * ATTN_BLOCK), bf16)
        dsb[2 * n_blk:2 * n_blk + 2] = jnp.zeros((2, ATTN_BLOCK, 2 * ATTN_BLOCK), bf16)
        kvh = j // 2
        cur_ok, prev_ok, low = _attn_masks()
        lowfull = lax.broadcasted_iota(jnp.int32, (SEQ, LANES), 1) < HEAD_DIM
        c, s1, s2 = c_ref[...], s1_ref[...], s2_ref[...]
        qr[...] = _rot(q_ref[...], c, s1, s2) * (HEAD_DIM ** -0.5)
        kr[...] = _dup_head(_rot(k_ref[...], c, s1, s2), kvh, lowfull)
        vr[...] = _dup_head(v_ref[...], kvh, lowfull)
        prod = do_ref[...] * o_ref[...]
        d_lo = jnp.sum(jnp.where(lowfull, prod, 0.0), axis=1, keepdims=True)
        d_hi = jnp.sum(jnp.where(lowfull, 0.0, prod), axis=1, keepdims=True)
        dl[...] = jnp.where(lowfull, d_lo, d_hi)
        dqa[...] = jnp.zeros_like(dqa)
        dka[...] = jnp.zeros_like(dka)
        dva[...] = jnp.zeros_like(dva)
        for d in DILATIONS:
            _deinterleave(qr, qd, d, bf16)
            _deinterleave(kr, kd, d, bf16)
            _deinterleave(vr, vd, d, bf16)
            _deinterleave(do_ref, dod, d, bf16)
            _deinterleave(lse_ref, lsd, d, f32)
            _deinterleave(dl, dld, d, f32)
            per_res = n_blk // d
            use_prev = per_res > 1
            curl, prevl = slice(0, ATTN_BLOCK), slice(ATTN_BLOCK, 2 * ATTN_BLOCK)

            def halves(x):
                zero = jnp.zeros_like(x)
                return jnp.where(low, x, zero), jnp.where(low, zero, x)

            def probs(n, carry):
                start = pl.multiple_of(n * ATTN_BLOCK, ATTN_BLOCK)
                has_prev = (n % per_res) != 0
                pstart = pl.multiple_of(jnp.maximum(n - 1, 0) * ATTN_BLOCK, ATTN_BLOCK)
                cur, prev = pl.ds(start, ATTN_BLOCK), pl.ds(pstart, ATTN_BLOCK)
                qas, doas = halves(qd[cur, :]), halves(dod[cur, :])
                kc, vc = kd[cur, :], vd[cur, :]
                if use_prev:
                    kp, vp = kd[prev, :], vd[prev, :]
                lsb, dlb = lsd[cur, :], dld[cur, :]
                for a in range(2):
                    ls = lsb[:, a * HEAD_DIM:a * HEAD_DIM + 1]
                    de = dlb[:, a * HEAD_DIM:a * HEAD_DIM + 1]
                    pc = jnp.exp(jnp.where(cur_ok, _nt(qas[a], kc), NEG_INF) - ls)
                    pb[2 * n + a, :, curl] = pc.astype(bf16)
                    dsb[2 * n + a, :, curl] = (pc * (_nt(doas[a], vc) - de)).astype(bf16)
                    if use_prev:
                        pp = jnp.exp(jnp.where(prev_ok & has_prev, _nt(qas[a], kp), NEG_INF) - ls)
                        pb[2 * n + a, :, prevl] = pp.astype(bf16)
                        dsb[2 * n + a, :, prevl] = (pp * (_nt(doas[a], vp) - de)).astype(bf16)
                return carry

            def grads(n, carry):
                start = pl.multiple_of(n * ATTN_BLOCK, ATTN_BLOCK)
                pstart = pl.multiple_of(jnp.maximum(n - 1, 0) * ATTN_BLOCK, ATTN_BLOCK)
                nstart = pl.multiple_of(jnp.minimum(n + 1, n_blk - 1) * ATTN_BLOCK, ATTN_BLOCK)
                cur, prev, nxt = pl.ds(start, ATTN_BLOCK), pl.ds(pstart, ATTN_BLOCK), pl.ds(nstart, ATTN_BLOCK)
                kc = kd[cur, :]
                dqs = [_nn(dsb[2 * n + a, :, curl], kc) for a in range(2)]
                q_rows, do_rows = list(halves(qd[cur, :])), list(halves(dod[cur, :]))
                ds_rows, p_rows = [dsb[2 * n + a, :, curl] for a in range(2)], [pb[2 * n + a, :, curl] for a in range(2)]
                if use_prev:
                    kp = kd[prev, :]
                    dqs = [dqs[a] + _nn(dsb[2 * n + a, :, prevl], kp) for a in range(2)]
                    q_rows += list(halves(qd[nxt, :]))
                    do_rows += list(halves(dod[nxt, :]))
                    ds_rows += [dsb[2 * n + 2 + a, :, prevl] for a in range(2)]
                    p_rows += [pb[2 * n + 2 + a, :, prevl] for a in range(2)]
                dqd[cur, :] = jnp.where(low, dqs[0], dqs[1])
                dkd[cur, :] = _tn(jnp.concatenate(ds_rows, axis=0), jnp.concatenate(q_rows, axis=0))
                dvd[cur, :] = _tn(jnp.concatenate(p_rows, axis=0), jnp.concatenate(do_rows, axis=0))
                return carry

            lax.fori_loop(0, n_blk, probs, 0, unroll=ATTN_UNROLL)
            lax.fori_loop(0, n_blk, grads, 0, unroll=ATTN_UNROLL)
            _interleave_store(dqd, dqa, d, True)
            _interleave_store(dkd, dka, d, True)
            _interleave_store(dvd, dva, d, True)
        dq_ref[...] = _rot_t(dqa[...] * (HEAD_DIM ** -0.5), c, s1, s2)
        dkf = dka[...]
        dkf = _rot_t(dkf + pltpu.roll(dkf, HEAD_DIM, 1), c, s1, s2)
        dvf = dva[...]
        dvf = dvf + pltpu.roll(dvf, HEAD_DIM, 1)
        mine = (lax.broadcasted_iota(jnp.int32, (SEQ, LANES), 1) // HEAD_DIM) == kvh
        dkc_, dvc_ = jnp.where(mine, dkf, 0.0), jnp.where(mine, dvf, 0.0)

        @pl.when(j == 0)
        def _():
            dk_ref[...] = dkc_
            dv_ref[...] = dvc_

        @pl.when(j > 0)
        def _():
            dk_ref[...] += dkc_
            dv_ref[...] += dvc_

    def col(jj):
        return pl.BlockSpec((SEQ, LANES), lambda b, j: (b, jj if jj is not None else j))

    tab = pl.BlockSpec((SEQ, LANES), lambda b, j: (b, 0))
    fs = pltpu.VMEM((SEQ, LANES), f32)
    hs = pltpu.VMEM((SEQ, LANES), bf16)
    return pl.pallas_call(
        body, name=name, grid=(nb, n_j),
        in_specs=[col(None), col(n_j), col(n_j + 1), tab, tab, tab, col(None), col(None), col(None)],
        out_specs=[col(None), tab, tab],
        out_shape=[jax.ShapeDtypeStruct((t, ATTN_WIDTH), f32), jax.ShapeDtypeStruct((t, LANES), f32), jax.ShapeDtypeStruct((t, LANES), f32)],
        scratch_shapes=[fs, fs, fs, fs, hs, hs, hs, hs, fs, fs, fs, fs, fs, fs, fs, fs,
                        pltpu.VMEM((2 * n_blk + 2, ATTN_BLOCK, 2 * ATTN_BLOCK), bf16), pltpu.VMEM((2 * n_blk + 2, ATTN_BLOCK, 2 * ATTN_BLOCK), bf16)],
        compiler_params=_cparams(("parallel", "arbitrary")),
    )(qkv, qkv, qkv, *tabs, o, lse, do)


def _conv_pre(x, w_ref, b_ref, row):
    shifted = [x] + [jnp.where(row >= s, pltpu.roll(x, s, 0), 0.0) for s in range(1, CONV_WIDTH)]
    pre = b_ref[...] + w_ref[CONV_WIDTH - 1:CONV_WIDTH, :] * x
    for s in range(1, CONV_WIDTH):
        pre = pre + w_ref[CONV_WIDTH - 1 - s:CONV_WIDTH - s, :] * shifted[s]
    return pre, shifted


def _conv_fwd(x, w, b, name, tc=512):
    t, ch = x.shape

    def body(x_ref, w_ref, b_ref, o_ref):
        row = lax.broadcasted_iota(jnp.int32, (SEQ, tc), 0)
        pre, _ = _conv_pre(x_ref[...], w_ref, b_ref, row)
        o_ref[...] = _silu(pre)

    xs = pl.BlockSpec((SEQ, tc), lambda i, j: (i, j))
    return pl.pallas_call(
        body, name=name, grid=(t // SEQ, ch // tc),
        in_specs=[xs, pl.BlockSpec((CONV_WIDTH, tc), lambda i, j: (0, j)), pl.BlockSpec((1, tc), lambda i, j: (0, j))],
        out_specs=xs, out_shape=jax.ShapeDtypeStruct((t, ch), f32),
        compiler_params=_cparams(("parallel", "parallel")),
    )(x, w, b)


def _conv_bwd(x, w, b, dact, name, tc=512):
    t, ch = x.shape

    def body(x_ref, w_ref, b_ref, d_ref, dx_ref, dw_ref, db_ref):
        row = lax.broadcasted_iota(jnp.int32, (SEQ, tc), 0)
        pre, shifted = _conv_pre(x_ref[...], w_ref, b_ref, row)
        dpre = d_ref[...] * _dsilu(pre)
        dx = w_ref[CONV_WIDTH - 1:CONV_WIDTH, :] * dpre
        for s in range(1, CONV_WIDTH):
            dx = dx + w_ref[CONV_WIDTH - 1 - s:CONV_WIDTH - s, :] * jnp.where(row < SEQ - s, pltpu.roll(dpre, SEQ - s, 0), 0.0)
        dx_ref[...] = dx
        first = pl.program_id(1) == 0
        parts = [jnp.sum(dpre * shifted[CONV_WIDTH - 1 - k], axis=0, keepdims=True) for k in range(CONV_WIDTH)]
        dbp = jnp.sum(dpre, axis=0, keepdims=True)

        @pl.when(first)
        def _():
            for k in range(CONV_WIDTH):
                dw_ref[k:k + 1, :] = parts[k]
            db_ref[...] = dbp

        @pl.when(jnp.logical_not(first))
        def _():
            for k in range(CONV_WIDTH):
                dw_ref[k:k + 1, :] += parts[k]
            db_ref[...] += dbp

    xs = pl.BlockSpec((SEQ, tc), lambda j, i: (i, j))
    ws = pl.BlockSpec((CONV_WIDTH, tc), lambda j, i: (0, j))
    bs = pl.BlockSpec((1, tc), lambda j, i: (0, j))
    return pl.pallas_call(
        body, name=name, grid=(ch // tc, t // SEQ),
        in_specs=[xs, ws, bs, xs], out_specs=[xs, ws, bs],
        out_shape=[jax.ShapeDtypeStruct((t, ch), f32), jax.ShapeDtypeStruct((CONV_WIDTH, ch), f32), jax.ShapeDtypeStruct((1, ch), f32)],
        compiler_params=_cparams(("parallel", "arbitrary")),
    )(x, w, b, dact)


GROUP_W = SSM_INNER // SSM_GROUPS
HEADS_PER_GROUP = SSM_HEADS // SSM_GROUPS
HI = lax.Precision.HIGHEST


def _ssd_common(xbc_ref, dt_ref, bias_ref, alog_ref):
    r = lax.broadcasted_iota(jnp.int32, (CHUNK, CHUNK), 0)
    cidx = lax.broadcasted_iota(jnp.int32, (CHUNK, CHUNK), 1)
    causal = r >= cidx
    tril = causal.astype(f32)
    expand = (lax.broadcasted_iota(jnp.int32, (CHUNK, SSM_INNER), 0)
              == lax.broadcasted_iota(jnp.int32, (CHUNK, SSM_INNER), 1) // HEAD_DIM).astype(f32)
    head_lane = cidx < SSM_HEADS
    dtp = dt_ref[...] + bias_ref[...]
    dt = jnp.where(head_lane, _softplus(dtp), 0.0)
    a_neg = -jnp.exp(alog_ref[...])
    a = dt * a_neg
    cs = _nn(tril, a, HI)
    dt_e = _nn(dt, expand, HI)
    cs_e = _nn(cs, expand, HI)
    xs = xbc_ref[:, 0:SSM_INNER]
    xg = xs * dt_e
    ecs = jnp.exp(cs_e)
    cs_last = cs_e[CHUNK - 1:CHUNK, :]
    dse = jnp.exp(cs_last - cs_e)
    cde = jnp.exp(cs_last)
    return dict(r=r, cidx=cidx, causal=causal, tril=tril, expand=expand, head_lane=head_lane, dtp=dtp, dt=dt, a_neg=a_neg,
                cs=cs, cst=cs.T, dt_e=dt_e, cs_e=cs_e, xs=xs, xg=xg, ecs=ecs, dse=dse, cde=cde)


def _decay_mat(q, h):
    return jnp.exp(jnp.where(q["causal"], q["cs"][:, h:h + 1] - q["cst"][h:h + 1, :], NEG_INF))


def _gate_norm(y, z, nw):
    y2 = y * _silu(z)
    outs, xhats, rs = [], [], []
    for g in range(SSM_GROUPS):
        sl = slice(g * GROUP_W, (g + 1) * GROUP_W)
        yg = y2[:, sl]
        r = lax.rsqrt(jnp.mean(yg * yg, axis=-1, keepdims=True) + EPS)
        xhats.append(yg * r)
        rs.append(r)
        outs.append(yg * r * nw[:, sl])
    return y2, outs, xhats, rs


def _ssd_fwd(xbc, z, dtp, params, name):
    t = xbc.shape[0]
    n_chunk = SEQ // CHUNK
    low = None

    def body(xbc_ref, z_ref, dt_ref, bias_ref, alog_ref, dskip_ref, nw_ref, yn_ref, y_ref, hs_ref, h_scr):
        @pl.when(pl.program_id(1) == 0)
        def _():
            h_scr[...] = jnp.zeros_like(h_scr)

        q = _ssd_common(xbc_ref, dt_ref, bias_ref, alog_ref)
        low = lax.broadcasted_iota(jnp.int32, (CHUNK, LANES), 1) < HEAD_DIM
        xgb = q["xg"].astype(bf16)
        wst = (q["xg"] * q["dse"]).astype(bf16)
        hs_ref[0] = h_scr[...]
        ys = []
        for g in range(SSM_GROUPS):
            gl = slice(g * GROUP_W, (g + 1) * GROUP_W)
            bg = xbc_ref[:, SSM_INNER + g * D_STATE:SSM_INNER + (g + 1) * D_STATE].astype(bf16)
            cg = xbc_ref[:, SSM_INNER + SSM_GROUPS * D_STATE + g * D_STATE:SSM_INNER + SSM_GROUPS * D_STATE + (g + 1) * D_STATE].astype(bf16)
            cb = _nt(cg, bg)
            hg = h_scr[g]
            yoff = _nn(cg, hg.astype(bf16)) * q["ecs"][:, gl]
            pieces = []
            for i in range(HEADS_PER_GROUP // 2):
                h0 = g * HEADS_PER_GROUP + 2 * i
                xp = xgb[:, h0 * HEAD_DIM:(h0 + 2) * HEAD_DIM]
                m0 = (cb * _decay_mat(q, h0)).astype(bf16)
                m1 = (cb * _decay_mat(q, h0 + 1)).astype(bf16)
                zero = jnp.zeros_like(xp)
                pieces.append(_nn(m0, jnp.where(low, xp, zero)) + _nn(m1, jnp.where(low, zero, xp)))
            ys.append(jnp.concatenate(pieces, axis=1) + yoff + dskip_ref[:, gl] * q["xs"][:, gl])
            h_scr[g] = hg * q["cde"][:, gl] + _tn(bg, wst[:, gl])
        y = jnp.concatenate(ys, axis=1)
        y_ref[...] = y
        _, outs, _, _ = _gate_norm(y, z_ref[...], nw_ref[...])
        yn_ref[...] = jnp.concatenate(outs, axis=1).astype(bf16)

    def rows(w):
        return pl.BlockSpec((CHUNK, w), lambda b, c: (b * n_chunk + c, 0))

    def par(w):
        return pl.BlockSpec((1, w), lambda b, c: (0, 0))

    return pl.pallas_call(
        body, name=name, grid=(t // SEQ, n_chunk),
        in_specs=[rows(CONV_CH), rows(SSM_INNER), rows(LANES), par(LANES), par(LANES), par(SSM_INNER), par(SSM_INNER)],
        out_specs=[rows(SSM_INNER), rows(SSM_INNER), pl.BlockSpec((1, SSM_GROUPS, D_STATE, GROUP_W), lambda b, c: (b * n_chunk + c, 0, 0, 0))],
        out_shape=[jax.ShapeDtypeStruct((t, SSM_INNER), bf16), jax.ShapeDtypeStruct((t, SSM_INNER), f32),
                   jax.ShapeDtypeStruct((t // CHUNK, SSM_GROUPS, D_STATE, GROUP_W), f32)],
        scratch_shapes=[pltpu.VMEM((SSM_GROUPS, D_STATE, GROUP_W), f32)],
        compiler_params=_cparams(("parallel", "arbitrary")),
    )(xbc, z, dtp, *params)


def _ssd_bwd(xbc, z, dtp, y, hs, dyn, params, name):
    t = xbc.shape[0]
    n_chunk = SEQ // CHUNK

    def body(xbc_ref, z_ref, dt_ref, y_ref, hs_ref, dyn_ref, bias_ref, alog_ref, dskip_ref, nw_ref,
             dxbc_ref, dz_ref, ddt_ref, dnw_ref, dds_ref, dal_ref, dbi_ref, dh_scr):
        @pl.when(pl.program_id(1) == 0)
        def _():
            dh_scr[...] = jnp.zeros_like(dh_scr)

        q = _ssd_common(xbc_ref, dt_ref, bias_ref, alog_ref)
        low = lax.broadcasted_iota(jnp.int32, (CHUNK, LANES), 1) < HEAD_DIM
        last_row = lax.broadcasted_iota(jnp.int32, (CHUNK, GROUP_W), 0) == CHUNK - 1
        xs, xg = q["xs"], q["xg"]
        xgb = xg.astype(bf16)
        wf = xg * q["dse"]
        wst = wf.astype(bf16)
        zz = z_ref[...]
        yy = y_ref[...]
        sz = _silu(zz)
        y2, _, xhats, rs = _gate_norm(yy, zz, nw_ref[...])
        dyn_ = dyn_ref[...]
        dy2s, dnws = [], []
        for g in range(SSM_GROUPS):
            gl = slice(g * GROUP_W, (g + 1) * GROUP_W)
            gw = dyn_[:, gl] * nw_ref[:, gl]
            dy2s.append(rs[g] * (gw - xhats[g] * jnp.mean(gw * xhats[g], axis=-1, keepdims=True)))
            dnws.append(_rowsum8(dyn_[:, gl] * xhats[g]))
        dy2 = jnp.concatenate(dy2s, axis=1)
        dy = dy2 * sz
        dz_ref[...] = dy2 * yy * _dsilu(zz)
        dnw_p = jnp.concatenate(dnws, axis=1)
        dds_p = _rowsum8(dy * xs)
        dyb = dy.astype(bf16)
        gfull = (dy * q["ecs"]).astype(bf16)
        dcs_c = jnp.zeros((CHUNK, CHUNK), f32)
        dcs_r = jnp.zeros((CHUNK, CHUNK), f32)
        dcs_e_parts, dxg_parts = [], []
        for g in range(SSM_GROUPS):
            gl = slice(g * GROUP_W, (g + 1) * GROUP_W)
            bsl = slice(SSM_INNER + g * D_STATE, SSM_INNER + (g + 1) * D_STATE)
            csl = slice(SSM_INNER + SSM_GROUPS * D_STATE + g * D_STATE, SSM_INNER + SSM_GROUPS * D_STATE + (g + 1) * D_STATE)
            bg = xbc_ref[:, bsl].astype(bf16)
            cg = xbc_ref[:, csl].astype(bf16)
            cb = _nt(cg, bg)
            hg = hs_ref[0, g]
            hgb = hg.astype(bf16)
            dhn = dh_scr[g]
            dhnb = dhn.astype(bf16)
            yoff = _nn(cg, hgb) * q["ecs"][:, gl]
            dw_ = _nn(bg, dhnb)
            r_e = dw_ * wf[:, gl]
            to_last = jnp.sum(r_e, axis=0, keepdims=True) + jnp.sum(dhn * hg, axis=0, keepdims=True) * q["cde"][:, gl]
            dcs_e_parts.append(dy[:, gl] * yoff - r_e + jnp.where(last_row, to_last, 0.0))
            dcb = jnp.zeros((CHUNK, CHUNK), f32)
            dxg_pairs = []
            for i in range(HEADS_PER_GROUP // 2):
                h0 = g * HEADS_PER_GROUP + 2 * i
                psl = slice(h0 * HEAD_DIM, (h0 + 2) * HEAD_DIM)
                xp = xgb[:, psl]
                dyp = dyb[:, psl]
                zero = jnp.zeros_like(dyp)
                tns = []
                for a in range(2):
                    h = h0 + a
                    lm = _decay_mat(q, h)
                    m = cb * lm
                    dm = _nt(jnp.where(low, dyp, zero) if a == 0 else jnp.where(low, zero, dyp), xp)
                    dcb = dcb + dm * lm
                    nmat = dm * m
                    dcs_c = dcs_c + jnp.where(q["cidx"] == h, jnp.sum(nmat, axis=1, keepdims=True), 0.0)
                    dcs_r = dcs_r + jnp.where(q["r"] == h, jnp.sum(nmat, axis=0, keepdims=True), 0.0)
                    tns.append(_tn(m.astype(bf16), dyp))
                dxg_pairs.append(jnp.where(low, tns[0], tns[1]))
            dxg_parts.append(jnp.concatenate(dxg_pairs, axis=1) + dw_ * q["dse"][:, gl])
            dcbb = dcb.astype(bf16)
            dxbc_ref[:, csl] = _nt(gfull[:, gl], hgb) + _nn(dcbb, bg)
            dxbc_ref[:, bsl] = _nt(wst[:, gl], dhnb) + _tn(dcbb, cg)
            dh_scr[g] = dhn * q["cde"][:, gl] + _tn(cg, gfull[:, gl])
        dxg = jnp.concatenate(dxg_parts, axis=1)
        dcs_e = jnp.concatenate(dcs_e_parts, axis=1)
        dxbc_ref[:, 0:SSM_INNER] = dskip_ref[...] * dy + dxg * q["dt_e"]
        dcs = dcs_c - dcs_r.T + _dot(dcs_e, q["expand"], ((1,), (1,)), HI)
        triu = (q["cidx"] >= q["r"]).astype(f32)
        da = _nn(triu, dcs, HI)
        ddt = _dot(dxg * xs, q["expand"], ((1,), (1,)), HI) + da * q["a_neg"]
        ddtp = jnp.where(q["head_lane"], ddt * _sigmoid(q["dtp"]), 0.0)
        ddt_ref[...] = ddtp
        dal_p = _rowsum8(da * q["dt"]) * q["a_neg"]
        dbi_p = _rowsum8(ddtp)
        first = (pl.program_id(0) == 0) & (pl.program_id(1) == 0)

        @pl.when(first)
        def _():
            dnw_ref[...] = dnw_p
            dds_ref[...] = dds_p
            dal_ref[...] = dal_p
            dbi_ref[...] = dbi_p

        @pl.when(jnp.logical_not(first))
        def _():
            dnw_ref[...] += dnw_p
            dds_ref[...] += dds_p
            dal_ref[...] += dal_p
            dbi_ref[...] += dbi_p

    def rows(w):
        return pl.BlockSpec((CHUNK, w), lambda b, c: (b * n_chunk + n_chunk - 1 - c, 0))

    def par(w):
        return pl.BlockSpec((1, w), lambda b, c: (0, 0))

    def acc(w):
        return pl.BlockSpec((SUBLANES, w), lambda b, c: (0, 0))

    return pl.pallas_call(
        body, name=name, grid=(t // SEQ, n_chunk),
        in_specs=[rows(CONV_CH), rows(SSM_INNER), rows(LANES), rows(SSM_INNER),
                  pl.BlockSpec((1, SSM_GROUPS, D_STATE, GROUP_W), lambda b, c: (b * n_chunk + n_chunk - 1 - c, 0, 0, 0)),
                  rows(SSM_INNER), par(LANES), par(LANES), par(SSM_INNER), par(SSM_INNER)],
        out_specs=[rows(CONV_CH), rows(SSM_INNER), rows(LANES), acc(SSM_INNER), acc(SSM_INNER), acc(LANES), acc(LANES)],
        out_shape=[jax.ShapeDtypeStruct((t, CONV_CH), f32), jax.ShapeDtypeStruct((t, SSM_INNER), f32), jax.ShapeDtypeStruct((t, LANES), f32),
                   jax.ShapeDtypeStruct((SUBLANES, SSM_INNER), f32), jax.ShapeDtypeStruct((SUBLANES, SSM_INNER), f32),
                   jax.ShapeDtypeStruct((SUBLANES, LANES), f32), jax.ShapeDtypeStruct((SUBLANES, LANES), f32)],
        scratch_shapes=[pltpu.VMEM((SSM_GROUPS, D_STATE, GROUP_W), f32)],
        compiler_params=_cparams(("arbitrary", "arbitrary")),
    )(xbc, z, dtp, y, hs, dyn, *params)


def _adamw(g_parts, w, m, v, name, layer=None):
    rows, width = w.shape[-2:]
    n = len(g_parts)
    tr = _row_tile(rows)

    def body(*refs):
        g_refs, (w_ref, m_ref, v_ref, g_out, d_out, m_out, v_out) = refs[:n], refs[n:]

        def part(i):
            return (g_refs[i][...] if g_parts[i][1] is None else g_refs[i][0]).astype(f32)

        def state(ref):
            return ref[...] if layer is None else ref[0]

        g = part(0)
        for i in range(1, n):
            g = g + part(i)
        mm = ADAM_B1 * state(m_ref) + (1.0 - ADAM_B1) * g
        vv = ADAM_B2 * state(v_ref) + (1.0 - ADAM_B2) * (g * g)
        m_hat = mm / (1.0 - ADAM_B1 ** ADAM_STEP)
        v_hat = vv / (1.0 - ADAM_B2 ** ADAM_STEP)
        g_out[...] = g
        d_out[...] = -ADAM_LR * (m_hat / (jnp.sqrt(v_hat) + ADAM_EPS) + ADAM_WD * state(w_ref))
        m_out[...] = mm
        v_out[...] = vv

    spec = pl.BlockSpec((tr, width), lambda i: (i, 0))

    def lead(idx):
        return spec if idx is None else pl.BlockSpec((1, tr, width), lambda i: (idx, i, 0))

    return pl.pallas_call(
        body, name=name, grid=(rows // tr,), in_specs=[lead(idx) for _, idx in g_parts] + [lead(layer)] * 3, out_specs=[spec] * 4,
        out_shape=[jax.ShapeDtypeStruct((rows, width), f32)] * 4, compiler_params=_cparams(("parallel",)),
    )(*[a for a, _ in g_parts], w, m, v)


def _row_tile(rows, cap=512):
    for cand in range(min(rows, cap) // SUBLANES * SUBLANES, 0, -SUBLANES):
        if rows % cand == 0:
            return cand
    return rows


def _cols_from_devices(g, width, name):
    n_dev, depth, a, b = g.shape

    def body(g_ref, o_ref):
        for i in range(n_dev):
            o_ref[0, :, i * b:(i + 1) * b] = g_ref[i, 0]
        if width > n_dev * b:
            o_ref[0, :, n_dev * b:width] = jnp.zeros((a, width - n_dev * b), o_ref.dtype)

    return pl.pallas_call(
        body, name=name, grid=(depth,), in_specs=[pl.BlockSpec((n_dev, 1, a, b), lambda l: (0, l, 0, 0))],
        out_specs=pl.BlockSpec((1, a, width), lambda l: (l, 0, 0)), out_shape=jax.ShapeDtypeStruct((depth, a, width), g.dtype),
        compiler_params=_cparams(("parallel",)),
    )(g)


def _devices_from_cols(per_layer, b, name, tr=256):
    depth = len(per_layer)
    a, width = per_layer[0].shape

    def body(*refs):
        o_ref = refs[depth]
        for l in range(depth):
            for i in range(N_DEV):
                o_ref[i, l] = refs[l][:, i * b:(i + 1) * b]

    return pl.pallas_call(
        body, name=name, grid=(a // tr,), in_specs=[pl.BlockSpec((tr, width), lambda r: (r, 0))] * depth,
        out_specs=pl.BlockSpec((N_DEV, depth, tr, b), lambda r: (0, 0, r, 0)),
        out_shape=jax.ShapeDtypeStruct((N_DEV, depth, a, b), per_layer[0].dtype), compiler_params=_cparams(("parallel",)),
    )(*per_layer)


def _add_kept(g, recv, core, name, out_dtype=bf16):
    nblk, _, rows, width = g.shape
    tr = _row_tile(rows)

    def body(c_ref, g_ref, r_ref, o_ref):
        o_ref[0] = (g_ref[0, 0] + r_ref[0]).astype(out_dtype)

    grid_spec = pltpu.PrefetchScalarGridSpec(
        num_scalar_prefetch=1, grid=(nblk, rows // tr),
        in_specs=[pl.BlockSpec((1, 1, tr, width), lambda i, j, c: (i, c[0], j, 0)), pl.BlockSpec((1, tr, width), lambda i, j, c: (i, j, 0))],
        out_specs=pl.BlockSpec((1, tr, width), lambda i, j, c: (i, j, 0)))
    return pl.pallas_call(
        body, name=name, grid_spec=grid_spec, out_shape=jax.ShapeDtypeStruct((nblk, rows, width), out_dtype),
        compiler_params=_cparams(("parallel", "parallel")),
    )(core, g, recv)


def _me():
    return lax.axis_index("x"), lax.axis_index("y"), lax.axis_index("c")


def _allgather_two_level(shards, name):
    n = len(shards)
    per = 7

    def body(*refs):
        ins, outs, token = refs[:n], refs[n:2 * n], refs[2 * n]
        send_sems, recv_sems, local_sems = refs[2 * n + 1:]
        token[...] = jnp.zeros_like(token)
        x, y, c = _me()
        me, sibling = (x, y, c), (x, y, 1 - c)
        chips = [(1 - x, y), (x, 1 - y), (1 - x, 1 - y)]

        def slot(a, p):
            return outs[a].at[4 * p[0] + 2 * p[1] + p[2]]

        def copy(a, k, block, to, src=None):
            return pltpu.make_async_remote_copy(
                src_ref=slot(a, block) if src is None else src, dst_ref=slot(a, block),
                send_sem=send_sems.at[a * per + k], recv_sem=recv_sems.at[a * per + k], device_id=to, device_id_type=MESH)

        mine = [pltpu.make_async_copy(ins[a], slot(a, me), local_sems.at[a]) for a in range(n)]
        for cp in mine:
            cp.start()
        first = []
        for a in range(n):
            first.append(copy(a, 0, me, sibling, src=ins[a]))
            first += [copy(a, 1 + j, me, (*chip, c), src=ins[a]) for j, chip in enumerate(chips)]
        for cp in first:
            cp.start()
        passed = []
        for j, chip in enumerate(chips):
            for a in range(n):
                copy(a, 1 + j, (*chip, c), me).wait_recv()
                fwd = copy(a, 4 + j, (*chip, c), sibling)
                fwd.start()
                passed.append(fwd)
        for a in range(n):
            copy(a, 0, sibling, me).wait_recv()
            for j, chip in enumerate(chips):
                copy(a, 4 + j, (*chip, 1 - c), me).wait_recv()
        for cp in first + passed:
            cp.wait_send()
        for cp in mine:
            cp.wait()

    outs = pl.pallas_call(
        body, name=name, in_specs=[ANY] * n, out_specs=[ANY] * n + [pl.BlockSpec(memory_space=pltpu.VMEM)],
        out_shape=[jax.ShapeDtypeStruct((N_DEV,) + s.shape, s.dtype) for s in shards] + [jax.ShapeDtypeStruct((SUBLANES, LANES), f32)],
        scratch_shapes=[pltpu.SemaphoreType.DMA((n * per,)), pltpu.SemaphoreType.DMA((n * per,)), pltpu.SemaphoreType.DMA((n,))],
    )(*shards)
    return outs[:n], outs[n]


def _allgather_direct(row, name):
    def body(in_ref, out_ref, send_sems, recv_sems, local_sem):
        x, y, c = _me()
        mine = out_ref.at[4 * x + 2 * y + c]
        local = pltpu.make_async_copy(in_ref, mine, local_sem)
        local.start()
        sends = []
        for k in range(1, N_DEV):
            px, py, pc = x ^ (k >> 2), y ^ ((k >> 1) & 1), c ^ (k & 1)
            sends.append(pltpu.make_async_remote_copy(
                src_ref=in_ref, dst_ref=mine, send_sem=send_sems.at[k - 1], recv_sem=recv_sems.at[k - 1],
                device_id=(px, py, pc), device_id_type=MESH))
        for cp in sends:
            cp.start()
        for k in range(1, N_DEV):
            px, py, pc = x ^ (k >> 2), y ^ ((k >> 1) & 1), c ^ (k & 1)
            theirs = out_ref.at[4 * px + 2 * py + pc]
            pltpu.make_async_remote_copy(
                src_ref=in_ref, dst_ref=theirs, send_sem=send_sems.at[k - 1], recv_sem=recv_sems.at[k - 1],
                device_id=(px, py, pc), device_id_type=MESH).wait_recv()
        for cp in sends:
            cp.wait_send()
        local.wait()

    return pl.pallas_call(
        body, name=name, in_specs=[ANY], out_specs=ANY, out_shape=jax.ShapeDtypeStruct((N_DEV,) + row.shape, row.dtype),
        scratch_shapes=[pltpu.SemaphoreType.DMA((N_DEV - 1,)), pltpu.SemaphoreType.DMA((N_DEV - 1,)), pltpu.SemaphoreType.DMA],
    )(row)


N_CHIP = N_DEV // 2
HBM = pl.BlockSpec(memory_space=pltpu.HBM)
SEM = pl.BlockSpec(memory_space=pltpu.SEMAPHORE)
EFFECT = pltpu.SideEffectType.DATAFLOW_SIDE_EFFECTING


def _peer(k):
    x, y, c = _me()
    return x ^ (k >> 2), y ^ ((k >> 1) & 1), c ^ (k & 1)


def _direct_copies(srcs, lands, send_sems, recv_sems, per_peer):
    x, y, c = _me()
    me = 4 * x + 2 * y + c
    copies = []
    for a in range(len(srcs)):
        for k in range(1, N_DEV):
            px, py, pc = _peer(k)
            piece = srcs[a].at[4 * px + 2 * py + pc] if per_peer else srcs[a]
            copies.append(pltpu.make_async_remote_copy(
                src_ref=piece, dst_ref=lands[a].at[me], send_sem=send_sems.at[a * (N_DEV - 1) + k - 1],
                recv_sem=recv_sems.at[a * (N_DEV - 1) + k - 1], device_id=(px, py, pc), device_id_type=MESH))
    return copies


def _direct_start(srcs, lands, per_peer, name):
    n = len(srcs)
    n_sem = n * (N_DEV - 1)

    def body(*refs):
        src_refs, land_refs = refs[:n], refs[n:2 * n]
        send_sems, recv_sems = refs[2 * n], refs[2 * n + 1]
        token = refs[-1]
        for cp in _direct_copies(src_refs, land_refs, send_sems, recv_sems, per_peer):
            cp.start()
        token[...] = jnp.zeros_like(token)

    outs = pl.pallas_call(
        body, name=name,
        out_shape=(pltpu.SemaphoreType.DMA((n_sem,)), pltpu.SemaphoreType.DMA((n_sem,)),
                   *[pltpu.HBM(s.shape, s.dtype) for s in srcs], *[pltpu.HBM(s.shape, s.dtype) for s in lands],
                   jax.ShapeDtypeStruct((SUBLANES, LANES), f32)),
        in_specs=[HBM] * (2 * n), out_specs=(SEM, SEM, *[HBM] * (2 * n), pl.BlockSpec(memory_space=pltpu.VMEM)),
        input_output_aliases={i: 2 + i for i in range(2 * n)},
        compiler_params=pltpu.CompilerParams(has_side_effects=EFFECT),
    )(*[pltpu.with_memory_space_constraint(s, pltpu.HBM) for s in srcs], *[pltpu.with_memory_space_constraint(s, pltpu.HBM) for s in lands])
    return outs[0], outs[1], outs[2:2 + n], outs[2 + n:2 + 2 * n], outs[-1]


def _direct_wait(send_sems, recv_sems, srcs, lands, after, per_peer, name):
    n = len(srcs)

    def body(*refs):
        src_refs, land_refs = refs[:n], refs[n:2 * n]
        s_sems, r_sems = refs[2 * n], refs[2 * n + 1]
        for cp in _direct_copies(src_refs, land_refs, s_sems, r_sems, per_peer):
            cp.wait_send()
            cp.wait_recv()

    outs = pl.pallas_call(
        body, name=name,
        out_shape=tuple(pltpu.HBM(s.shape, s.dtype) for s in list(srcs) + list(lands)),
        in_specs=[HBM] * (2 * n) + [SEM, SEM, ANY], out_specs=tuple([HBM] * (2 * n)),
        input_output_aliases={i: i for i in range(2 * n)},
        compiler_params=pltpu.CompilerParams(has_side_effects=EFFECT),
    )(*srcs, *lands, send_sems, recv_sems, after)
    return outs[n:]


def _exchange_sibling(gs, name):
    n = len(gs)

    def body(*refs):
        ins, outs = refs[:n], refs[n:2 * n]
        send_sems, recv_sems = refs[2 * n:]
        x, y, c = _me()
        copies = [pltpu.make_async_remote_copy(
            src_ref=ins[a].at[i, 1 - c], dst_ref=outs[a].at[i], send_sem=send_sems.at[a * N_CHIP + i], recv_sem=recv_sems.at[a * N_CHIP + i],
            device_id=(x, y, 1 - c), device_id_type=MESH) for a in range(n) for i in range(N_CHIP)]
        for cp in copies:
            cp.start()
        for cp in copies:
            cp.wait_recv()
        for cp in copies:
            cp.wait_send()

    return pl.pallas_call(
        body, name=name, in_specs=[ANY] * n, out_specs=[ANY] * n,
        out_shape=[jax.ShapeDtypeStruct((N_CHIP,) + g.shape[2:], g.dtype) for g in gs],
        scratch_shapes=[pltpu.SemaphoreType.DMA((n * N_CHIP,)), pltpu.SemaphoreType.DMA((n * N_CHIP,))],
    )(*gs)


def _exchange_chips(ps, name):
    n = len(ps)

    def body(*refs):
        ins, outs = refs[:n], refs[n:2 * n]
        send_sems, recv_sems = refs[2 * n:]
        x, y, c = _me()
        chips = [(1 - x, y), (x, 1 - y), (1 - x, 1 - y)]
        copies = [pltpu.make_async_remote_copy(
            src_ref=ins[a].at[2 * cx + cy], dst_ref=outs[a].at[k], send_sem=send_sems.at[a * 3 + k], recv_sem=recv_sems.at[a * 3 + k],
            device_id=(cx, cy, c), device_id_type=MESH) for a in range(n) for k, (cx, cy) in enumerate(chips)]
        for cp in copies:
            cp.start()
        for cp in copies:
            cp.wait_recv()
        for cp in copies:
            cp.wait_send()

    return pl.pallas_call(
        body, name=name, in_specs=[ANY] * n, out_specs=[ANY] * n,
        out_shape=[jax.ShapeDtypeStruct((3,) + p.shape[1:], p.dtype) for p in ps],
        scratch_shapes=[pltpu.SemaphoreType.DMA((n * 3,)), pltpu.SemaphoreType.DMA((n * 3,))],
    )(*ps)


def _row(v, width=None):
    v = v.reshape(1, -1).astype(f32)
    if width is not None and v.shape[1] < width:
        v = jnp.pad(v, ((0, 0), (0, width - v.shape[1])))
    return v


def _layer_params(p, l):
    return dict(
        norm_mix=_row(p["norm_mix"][l]), norm_ffn=_row(p["norm_ffn"][l]), conv_w=p["conv_w"][l], conv_b=_row(p["conv_b"][l]),
        ssd=(_row(p["dt_bias"][l], LANES), _row(p["a_log"][l], LANES), _row(jnp.repeat(p["d_skip"][l], HEAD_DIM)), _row(p["ssm_norm"][l])))


def _layer_fwd(h, big, sp, tabs, l):
    tag = f"l{l}_"
    w_in, w_out, w_gate, w_up, w_down = big
    hn = _rmsnorm_fwd(h, sp["norm_mix"], tag + "norm_mix")
    qkv = _matmul(hn, w_in, mode="nn", n_out=QKV_WIDTH, tn=256, b_off=0, name=tag + "proj_qkv")
    z = _matmul(hn, w_in, mode="nn", n_out=SSM_INNER, tn=256, b_off=Z_OFF // 256, name=tag + "proj_z")
    xbc_pre = _matmul(hn, w_in, mode="nn", n_out=CONV_CH, tn=256, b_off=XBC_OFF // 256, name=tag + "proj_xbc")
    dtp = _matmul(hn, w_in, mode="nn", n_out=LANES, tn=LANES, b_off=DT_OFF // LANES, name=tag + "proj_dt")
    o, lse = _attn_fwd(qkv, tabs, tag + "attn_fwd")
    xbc = _conv_fwd(xbc_pre, sp["conv_w"], sp["conv_b"], tag + "conv_fwd")
    yn, y, hs = _ssd_fwd(xbc, z, dtp, sp["ssd"], tag + "ssd_fwd")
    t1 = _matmul(o, w_out, mode="nn", k_len=ATTN_WIDTH, tk=512, add=h, name=tag + "out_attn")
    h2 = _matmul(yn, w_out, mode="nn", k_len=SSM_INNER, tk=512, b_koff=1, add=t1, name=tag + "out_ssm")
    hn2 = _rmsnorm_fwd(h2, sp["norm_ffn"], tag + "norm_ffn")
    g, u, act = _swiglu_fwd(hn2, w_gate, w_up, tag + "ffn_up")
    h3 = _matmul(act, w_down, mode="nn", tk=1408, add=h2, name=tag + "ffn_down")
    saved = dict(h=h, hn=hn, qkv=qkv, z=z, xbc_pre=xbc_pre, dtp=dtp, o=o, lse=lse, xbc=xbc, yn=yn, y=y, hs=hs, h2=h2, hn2=hn2, g=g, u=u, act=act)
    return h3, saved


def _layer_bwd(dh3, s, big, sp, tabs, l, gd=f32):
    tag = f"l{l}_"
    w_in, w_out, w_gate, w_up, w_down = big
    dg, du = _swiglu_bwd(dh3, w_down, s["g"], s["u"], tag + "ffn_down_bwd")
    dw_down = _matmul(s["act"], dh3, mode="tn", tm=1408, tn=512, tk=2048, out_dtype=gd, name=tag + "dw_down")
    dhn2 = _matmul(dg, w_gate, mode="nt", tk=1408, name=tag + "ffn_gate_bwd")
    dhn2 = _matmul(du, w_up, mode="nt", tk=1408, add=dhn2, name=tag + "ffn_up_bwd")
    dw_gate = _matmul(s["hn2"], dg, mode="tn", tm=512, tn=1408, tk=2048, out_dtype=gd, name=tag + "dw_gate")
    dw_up = _matmul(s["hn2"], du, mode="tn", tm=512, tn=1408, tk=2048, out_dtype=gd, name=tag + "dw_up")
    dh2, dnf = _rmsnorm_bwd(dhn2, s["h2"], sp["norm_ffn"], dh3, tag + "norm_ffn_bwd")
    d_o = _matmul(dh2, w_out, mode="nt", n_out=ATTN_WIDTH, tn=512, b_off=0, name=tag + "out_attn_bwd")
    dyn = _matmul(dh2, w_out, mode="nt", n_out=SSM_INNER, tn=512, b_off=1, name=tag + "out_ssm_bwd")
    dw_out = jnp.concatenate([_matmul(s["o"], dh2, mode="tn", tm=512, tn=512, tk=2048, out_dtype=gd, name=tag + "dw_out_attn"),
                              _matmul(s["yn"], dh2, mode="tn", tm=512, tn=512, tk=2048, out_dtype=gd, name=tag + "dw_out_ssm")], axis=0)
    dxbc, dz, ddtp, dnw, dds, dal, dbi = _ssd_bwd(s["xbc"], s["z"], s["dtp"], s["y"], s["hs"], dyn, sp["ssd"], tag + "ssd_bwd")
    dxbc_pre, dconv_w, dconv_b = _conv_bwd(s["xbc_pre"], sp["conv_w"], sp["conv_b"], dxbc, tag + "conv_bwd")
    dq, dk, dv = _attn_bwd(s["qkv"], tabs, s["o"], s["lse"], d_o, tag + "attn_bwd")
    dproj = jnp.concatenate([dq.astype(bf16), dk.astype(bf16), dv.astype(bf16), dz.astype(bf16), dxbc_pre.astype(bf16), ddtp.astype(bf16)], axis=1)
    dhn = _matmul(dproj, w_in, mode="nt", tk=1152, name=tag + "proj_bwd")
    dw_in = _matmul(s["hn"], dproj, mode="tn", tm=512, tn=1152, tk=2048, out_dtype=gd, name=tag + "dw_in")
    dh, dnm = _rmsnorm_bwd(dhn, s["h"], sp["norm_mix"], dh2, tag + "norm_mix_bwd")
    grads = dict(
        norm_mix=dnm.sum(0), w_in=dw_in, conv_w=dconv_w, conv_b=dconv_b[0], dt_bias=dbi.sum(0)[:SSM_HEADS], a_log=dal.sum(0)[:SSM_HEADS],
        d_skip=dds.sum(0).reshape(SSM_HEADS, HEAD_DIM).sum(1), ssm_norm=dnw.sum(0), w_out=dw_out, norm_ffn=dnf.sum(0),
        w_gate=dw_gate, w_up=dw_up, w_down=dw_down)
    return dh, grads


def _local_step(x, positions, target, p, bigs):
    tabs = _rope_tables(positions.reshape(-1, 1), "rope_tables")
    h = x
    saved, sps = [], []
    for l in range(DEPTH):
        sps.append(_layer_params(p, l))
        h, s = _layer_fwd(h, bigs[l], sps[l], tabs, l)
        saved.append(s)
    dh, loss_parts, dfn = _final_loss(h, _row(p["final_norm"]), target, "final_loss")
    layer_grads = [None] * DEPTH
    for l in reversed(range(DEPTH)):
        dh, layer_grads[l] = _layer_bwd(dh, saved[l], bigs[l], sps[l], tabs, l)
    grads = {k: [layer_grads[l][k] for l in range(DEPTH)] for k in layer_grads[0]}
    grads["final_norm"] = dfn.sum(0)
    return jnp.sum(loss_parts), dh, grads


BIG = ("w_in", "w_out", "w_gate", "w_up", "w_down")
COL_SHARDED = ("w_in", "w_gate", "w_up")
SMALL = ("norm_mix", "conv_b", "dt_bias", "a_log", "d_skip", "ssm_norm", "norm_ffn", "final_norm")
WEIGHTS = ("norm_mix", "w_in", "conv_w", "conv_b", "dt_bias", "a_log", "d_skip", "ssm_norm", "w_out", "norm_ffn", "w_gate", "w_up", "w_down", "final_norm")
PACK_W = 1024
SMALL_ROWS = 88
CONVW_ROWS = 96
CONVW_SHARD_ROWS = 16


def _full_from_gathered(name, g, l):
    _, a, b = g.shape
    if name in COL_SHARDED:
        width = IN_PROJ_PAD if name == "w_in" else N_DEV * b
        return _cols_from_devices(g.reshape(N_DEV, 1, a, b), width, f"cols_l{l}_{name}").reshape(a, width)
    return g.reshape(N_DEV * a, b)


def _by_device(name, full, shard_shape, l):
    a, b = shard_shape
    if name in COL_SHARDED:
        return _devices_from_cols([full], b, f"devs_l{l}_{name}").reshape(N_CHIP, 2, a, b)
    return full.reshape(N_CHIP, 2, a, b)


def _pack_rows(parts, rows, width):
    flat = jnp.concatenate([q.reshape(-1) for q in parts])
    return jnp.pad(flat, (0, rows * width - flat.shape[0])).reshape(rows, width)


def _unpack(flat, like):
    out, off = [], 0
    for q in like:
        out.append(flat[off:off + q.size].reshape(q.shape))
        off += q.size
    return out


def kernel(x, positions, norm_mix, w_in, conv_w, conv_b, dt_bias, a_log, d_skip, ssm_norm, w_out, norm_ffn, w_gate, w_up, w_down, final_norm, loss_target, m_norm_mix, m_w_in, m_conv_w, m_conv_b, m_dt_bias, m_a_log, m_d_skip, m_ssm_norm, m_w_out, m_norm_ffn, m_w_gate, m_w_up, m_w_down, m_final_norm, v_norm_mix, v_w_in, v_conv_w, v_conv_b, v_dt_bias, v_a_log, v_d_skip, v_ssm_norm, v_w_out, v_norm_ffn, v_w_gate, v_w_up, v_w_down, v_final_norm):
    w = dict(norm_mix=norm_mix, w_in=w_in, conv_w=conv_w, conv_b=conv_b, dt_bias=dt_bias, a_log=a_log, d_skip=d_skip, ssm_norm=ssm_norm,
             w_out=w_out, norm_ffn=norm_ffn, w_gate=w_gate, w_up=w_up, w_down=w_down, final_norm=final_norm)
    m = dict(norm_mix=m_norm_mix, w_in=m_w_in, conv_w=m_conv_w, conv_b=m_conv_b, dt_bias=m_dt_bias, a_log=m_a_log, d_skip=m_d_skip,
             ssm_norm=m_ssm_norm, w_out=m_w_out, norm_ffn=m_norm_ffn, w_gate=m_w_gate, w_up=m_w_up, w_down=m_w_down, final_norm=m_final_norm)
    v = dict(norm_mix=v_norm_mix, w_in=v_w_in, conv_w=v_conv_w, conv_b=v_conv_b, dt_bias=v_dt_bias, a_log=v_a_log, d_skip=v_d_skip,
             ssm_norm=v_ssm_norm, w_out=v_w_out, norm_ffn=v_norm_ffn, w_gate=v_w_gate, w_up=v_w_up, w_down=v_w_down, final_norm=v_final_norm)
    ax, ay, ac = lax.axis_index("x"), lax.axis_index("y"), lax.axis_index("c")
    dev = 4 * ax + 2 * ay + ac

    assert DEPTH == 2
    t = x.shape[0] * x.shape[1]
    xf, target = x.reshape(t, D_MODEL), loss_target.reshape(t, D_MODEL)

    def own_slot(block):
        return lax.dynamic_update_slice(jnp.zeros((N_DEV,) + block.shape[1:], block.dtype), block, (dev,) + (0,) * (block.ndim - 1))

    gathered0, tie0 = _allgather_two_level([w[k][0].astype(bf16) for k in BIG] + [w["conv_w"]], "gather_l0")
    shards1 = [(w["w_in"][1] + tie0[0, 0]).astype(bf16)] + [w[k][1].astype(bf16) for k in BIG[1:]]
    g_send, g_recv, shards1, lands1, tie1 = _direct_start(shards1, [own_slot(s[None]) for s in shards1], False, "gather_l1_start")
    p = {k: w[k] for k in SMALL}
    p["norm_mix"] = p["norm_mix"] + tie1[0, 0]
    p["conv_w"] = jnp.transpose(gathered0[-1], (1, 2, 0, 3)).reshape(DEPTH, CONV_WIDTH, CONV_CH)
    sp0, sp1 = _layer_params(p, 0), _layer_params(p, 1)
    bigs0 = tuple(_full_from_gathered(k, g, 0) for k, g in zip(BIG, gathered0))

    tabs = _rope_tables(positions.reshape(t, 1), "rope_tables")
    h1, saved0 = _layer_fwd(xf, bigs0, sp0, tabs, 0)
    lands1 = _direct_wait(g_send, g_recv, shards1, lands1, h1, False, "gather_l1_wait")
    bigs1 = tuple(_full_from_gathered(k, g, 1) for k, g in zip(BIG, lands1))
    h2, saved1 = _layer_fwd(h1, bigs1, sp1, tabs, 1)
    dh, loss_parts, dfn = _final_loss(h2, _row(p["final_norm"]), target, "final_loss")
    loss_local = jnp.sum(loss_parts)

    dh, grads1 = _layer_bwd(dh, saved1, bigs1, sp1, tabs, 1, gd=bf16)
    by_dev1 = [_by_device(k, grads1[k], w[k].shape[1:], 1).reshape((N_DEV,) + w[k].shape[1:]) for k in BIG]
    lands_s = [own_slot(lax.dynamic_slice_in_dim(g, dev, 1, 0)) for g in by_dev1]
    s_send, s_recv, by_dev1, lands_s, s_token = _direct_start(by_dev1, lands_s, True, "scatter_l1_start")
    dx, grads0 = _layer_bwd(dh, saved0, bigs0, dict(sp0, norm_ffn=sp0["norm_ffn"] + s_token[0, 0]), tabs, 0)
    lands_s = _direct_wait(s_send, s_recv, by_dev1, lands_s, dx, True, "scatter_l1_wait")
    core = ac.reshape(1).astype(jnp.int32)
    by_dev0 = [_by_device(k, grads0[k], w[k].shape[1:], 0) for k in BIG]
    from_sibling = _exchange_sibling(by_dev0, "scatter_l0_sibling")
    chip_sums = [_add_kept(g, r, core, "scatter_l0_add_" + k) for k, g, r in zip(BIG, by_dev0, from_sibling)]
    from_chips = _exchange_chips(chip_sums, "scatter_l0_chips")
    out_g, out_d, out_m, out_v = {}, {}, {}, {}
    for k, cs, fc, ls in zip(BIG, chip_sums, from_chips, lands_s):
        own = lax.dynamic_index_in_dim(cs, 2 * ax + ay, 0, keepdims=False)
        res0 = _adamw([(own, None), (fc, 0), (fc, 1), (fc, 2)], w[k], m[k], v[k], "adamw_l0_" + k, layer=0)
        res1 = _adamw([(ls, i) for i in range(N_DEV)], w[k], m[k], v[k], "adamw_l1_" + k, layer=1)
        for dst, r0, r1 in zip((out_g, out_d, out_m, out_v), res0, res1):
            dst[k] = jnp.stack([r0, r1])
    grads = {k: [grads0[k], grads1[k]] for k in grads0 if k not in BIG}
    grads["final_norm"] = dfn.sum(0)

    small_like = [w[k] for k in SMALL]
    small_grads = [jnp.stack(grads[k]) if k != "final_norm" else grads[k] for k in SMALL]
    small_pack = jnp.concatenate([_pack_rows(small_grads, SMALL_ROWS, LANES), _pack_rows([jnp.stack(grads["conv_w"])], CONVW_ROWS, LANES)], axis=0)
    parts = _allgather_direct(small_pack, "gather_small_grads")
    g_s, d_s, m_s, v_s = _adamw(
        [(parts[i, :SMALL_ROWS], None) for i in range(N_DEV)], _pack_rows(small_like, SMALL_ROWS, LANES),
        _pack_rows([m[k] for k in SMALL], SMALL_ROWS, LANES), _pack_rows([v[k] for k in SMALL], SMALL_ROWS, LANES), "adamw_replicated")
    for dst, src in ((out_g, g_s), (out_d, d_s), (out_m, m_s), (out_v, v_s)):
        dst.update(zip(SMALL, _unpack(src.reshape(-1), small_like)))
    shard_w = conv_w.shape[-1]
    conv_parts = parts[:, SMALL_ROWS:].reshape(N_DEV, DEPTH, CONV_WIDTH, CONV_CH)
    conv_mine = lax.dynamic_slice_in_dim(conv_parts, dev * shard_w, shard_w, axis=3)
    g_c, d_c, m_c, v_c = _adamw(
        [(_pack_rows([conv_mine[i]], CONVW_SHARD_ROWS, LANES), None) for i in range(N_DEV)], _pack_rows([conv_w], CONVW_SHARD_ROWS, LANES),
        _pack_rows([m["conv_w"]], CONVW_SHARD_ROWS, LANES), _pack_rows([v["conv_w"]], CONVW_SHARD_ROWS, LANES), "adamw_conv_w")
    for dst, src in ((out_g, g_c), (out_d, d_c), (out_m, m_c), (out_v, v_c)):
        dst["conv_w"] = src.reshape(-1)[:conv_w.size].reshape(conv_w.shape)

    loss = lax.psum(loss_local, ("x", "y", "c"))
    return (loss, dx.reshape(x.shape), *[out_g[k] for k in WEIGHTS], *[out_d[k] for k in WEIGHTS],
            *[out_m[k] for k in WEIGHTS], *[out_v[k] for k in WEIGHTS])
```

```python
import functools
import math

import jax
import jax.numpy as jnp
import numpy as np
from jax import lax
from jax.experimental import pallas as pl
from jax.experimental.pallas import tpu as pltpu

f32 = jnp.float32
bf16 = jnp.bfloat16

D_MODEL = 1024
SEQ = 2048
DEPTH = 2
HEAD_DIM = 64
N_ATTN_HEADS = 8
N_KV_HEADS = 2
ATTN_WIDTH = 512
KV_WIDTH = 128
ROPE_DIM = 16
ROPE_THETA = 500000.0
DILATIONS = (1, 4, 16)
ATTN_BLOCK = 128
SSM_HEADS = 16
SSM_INNER = 1024
SSM_GROUPS = 2
D_STATE = 128
CONV_WIDTH = 4
CHUNK = 128
CONV_CH = 1536
MIX_WIDTH = 1536
QKV_WIDTH = ATTN_WIDTH + 2 * KV_WIDTH
Z_OFF = 768
XBC_OFF = 1792
DT_OFF = 3328
IN_PROJ = 3344
IN_PROJ_PAD = 3456
FFN_HIDDEN = 2816
EPS = 1e-5
N_DEV = 8
ADAM_LR = 0.001
ADAM_B1 = 0.9
ADAM_B2 = 0.999
ADAM_EPS = 1e-08
ADAM_WD = 0.01
ADAM_STEP = 10

LANES = 128
SUBLANES = 8
VMEM_LIMIT = 56 * 1024 * 1024

MESH = pl.DeviceIdType.MESH
ANY = pl.BlockSpec(memory_space=pl.ANY)


def _cparams(sem, vmem=None):
    return pltpu.CompilerParams(dimension_semantics=sem, vmem_limit_bytes=vmem or VMEM_LIMIT)


def _sigmoid(x):
    return 1.0 / (1.0 + jnp.exp(-x))


def _silu(x):
    return x * _sigmoid(x)


def _dsilu(x):
    s = _sigmoid(x)
    return s * (1.0 + x * (1.0 - s))


def _softplus(x):
    return jnp.maximum(x, 0.0) + jnp.log(1.0 + jnp.exp(-jnp.abs(x)))


def _dot(a, b, dims, precision=None):
    return lax.dot_general(a, b, (dims, ((), ())), preferred_element_type=f32, precision=precision)


def _nn(a, b, precision=None):
    return _dot(a, b, ((1,), (0,)), precision)


def _nt(a, b):
    return _dot(a, b, ((1,), (1,)))


def _tn(a, b):
    return _dot(a, b, ((0,), (0,)))


def _rowsum8(t):
    n, w = t.shape
    return jnp.sum(t.reshape(n // SUBLANES, SUBLANES, w), axis=0)


def _matmul(a, b, *, mode, n_out=None, b_off=0, a_koff=0, b_koff=0, k_len=None, add=None, out_dtype=f32, tm=2048, tn=512, tk=1024, name):
    if mode == "tn":
        kdim_a, m = a.shape
    else:
        m, kdim_a = a.shape
    kk = k_len if k_len is not None else kdim_a
    n = n_out if n_out is not None else (b.shape[0] if mode == "nt" else b.shape[1])
    tm, tn, tk = min(tm, m), min(tn, n), min(tk, kk)
    assert m % tm == 0 and n % tn == 0 and kk % tk == 0, (name, m, n, kk, tm, tn, tk)
    nk = kk // tk
    if mode == "nn":
        a_spec = pl.BlockSpec((tm, tk), lambda i, j, k: (i, k + a_koff))
        b_spec = pl.BlockSpec((tk, tn), lambda i, j, k: (k + b_koff, j + b_off))
        dims = ((1,), (0,))
    elif mode == "nt":
        a_spec = pl.BlockSpec((tm, tk), lambda i, j, k: (i, k + a_koff))
        b_spec = pl.BlockSpec((tn, tk), lambda i, j, k: (j + b_off, k + b_koff))
        dims = ((1,), (1,))
    else:
        a_spec = pl.BlockSpec((tk, tm), lambda i, j, k: (k + a_koff, i))
        b_spec = pl.BlockSpec((tk, tn), lambda i, j, k: (k + b_koff, j + b_off))
        dims = ((0,), (0,))
    o_spec = pl.BlockSpec((tm, tn), lambda i, j, k: (i, j))
    has_add = add is not None

    def body(*refs):
        if has_add:
            a_ref, b_ref, add_ref, o_ref, acc_ref = refs
        else:
            a_ref, b_ref, o_ref, acc_ref = refs
        k = pl.program_id(2)
        part = _dot(a_ref[...].astype(bf16), b_ref[...].astype(bf16), dims)

        @pl.when(k == 0)
        def _():
            acc_ref[...] = part

        @pl.when(k > 0)
        def _():
            acc_ref[...] += part

        @pl.when(k == nk - 1)
        def _():
            r = acc_ref[...]
            if has_add:
                r = r + add_ref[...]
            o_ref[...] = r.astype(out_dtype)

    in_specs = [a_spec, b_spec] + ([o_spec] if has_add else [])
    args = (a, b) + ((add,) if has_add else ())
    return pl.pallas_call(
        body, name=name, grid=(m // tm, n // tn, nk), in_specs=in_specs, out_specs=o_spec,
        out_shape=jax.ShapeDtypeStruct((m, n), out_dtype), scratch_shapes=[pltpu.VMEM((tm, tn), f32)],
        compiler_params=_cparams(("parallel", "parallel", "arbitrary")),
    )(*args)


def _swiglu_fwd(hn, w_gate, w_up, name, tm=2048, tn=256):
    m, k = hn.shape
    n = w_gate.shape[1]

    def body(a_ref, wg_ref, wu_ref, g_ref, u_ref, act_ref):
        a = a_ref[...]
        g = _nn(a, wg_ref[...])
        u = _nn(a, wu_ref[...])
        g_ref[...] = g.astype(bf16)
        u_ref[...] = u.astype(bf16)
        act_ref[...] = (_silu(g) * u).astype(bf16)

    a_spec = pl.BlockSpec((tm, k), lambda i, j: (i, 0))
    w_spec = pl.BlockSpec((k, tn), lambda i, j: (0, j))
    o_spec = pl.BlockSpec((tm, tn), lambda i, j: (i, j))
    return pl.pallas_call(
        body, name=name, grid=(m // tm, n // tn), in_specs=[a_spec, w_spec, w_spec], out_specs=[o_spec, o_spec, o_spec],
        out_shape=[jax.ShapeDtypeStruct((m, n), bf16)] * 3,
        compiler_params=_cparams(("parallel", "parallel")),
    )(hn, w_gate, w_up)


def _swiglu_bwd(dh, w_down, g, u, name, tm=2048, tn=256):
    m, k = dh.shape
    n = w_down.shape[0]

    def body(a_ref, w_ref, g_ref, u_ref, dg_ref, du_ref):
        dact = _nt(a_ref[...].astype(bf16), w_ref[...])
        gg = g_ref[...].astype(f32)
        dg_ref[...] = (dact * u_ref[...].astype(f32) * _dsilu(gg)).astype(bf16)
        du_ref[...] = (dact * _silu(gg)).astype(bf16)

    a_spec = pl.BlockSpec((tm, k), lambda i, j: (i, 0))
    w_spec = pl.BlockSpec((tn, k), lambda i, j: (j, 0))
    o_spec = pl.BlockSpec((tm, tn), lambda i, j: (i, j))
    return pl.pallas_call(
        body, name=name, grid=(m // tm, n // tn), in_specs=[a_spec, w_spec, o_spec, o_spec], out_specs=[o_spec, o_spec],
        out_shape=[jax.ShapeDtypeStruct((m, n), bf16), jax.ShapeDtypeStruct((m, n), bf16)],
        compiler_params=_cparams(("parallel", "parallel")),
    )(dh, w_down, g, u)


def _rmsnorm_fwd(h, w, name, tm=512):
    m, d = h.shape

    def body(h_ref, w_ref, o_ref):
        x = h_ref[...]
        r = lax.rsqrt(jnp.mean(x * x, axis=-1, keepdims=True) + EPS)
        o_ref[...] = (x * r * w_ref[...]).astype(bf16)

    return pl.pallas_call(
        body, name=name, grid=(m // tm,),
        in_specs=[pl.BlockSpec((tm, d), lambda i: (i, 0)), pl.BlockSpec((1, d), lambda i: (0, 0))],
        out_specs=pl.BlockSpec((tm, d), lambda i: (i, 0)), out_shape=jax.ShapeDtypeStruct((m, d), bf16),
        compiler_params=_cparams(("parallel",)),
    )(h, w)


def _rmsnorm_bwd(dhn, h, w, dres, name, tm=512):
    m, d = h.shape

    def body(dhn_ref, h_ref, w_ref, dres_ref, dh_ref, dw_ref):
        x = h_ref[...]
        r = lax.rsqrt(jnp.mean(x * x, axis=-1, keepdims=True) + EPS)
        xhat = x * r
        dy = dhn_ref[...]
        gw = dy * w_ref[...]
        dh_ref[...] = dres_ref[...] + r * (gw - xhat * jnp.mean(gw * xhat, axis=-1, keepdims=True))
        part = _rowsum8(dy * xhat)

        @pl.when(pl.program_id(0) == 0)
        def _():
            dw_ref[...] = part

        @pl.when(pl.program_id(0) > 0)
        def _():
            dw_ref[...] += part

    row = pl.BlockSpec((tm, d), lambda i: (i, 0))
    return pl.pallas_call(
        body, name=name, grid=(m // tm,),
        in_specs=[row, row, pl.BlockSpec((1, d), lambda i: (0, 0)), row],
        out_specs=[row, pl.BlockSpec((SUBLANES, d), lambda i: (0, 0))],
        out_shape=[jax.ShapeDtypeStruct((m, d), f32), jax.ShapeDtypeStruct((SUBLANES, d), f32)],
        compiler_params=_cparams(("arbitrary",)),
    )(dhn, h, w, dres)


def _final_loss(h, w, target, name, tm=512):
    m, d = h.shape

    def body(h_ref, w_ref, t_ref, dh_ref, loss_ref, dw_ref):
        x = h_ref[...]
        r = lax.rsqrt(jnp.mean(x * x, axis=-1, keepdims=True) + EPS)
        xhat = x * r
        ww = w_ref[...]
        err = xhat * ww - t_ref[...]
        dy = err * (1.0 / d)
        gw = dy * ww
        dh_ref[...] = r * (gw - xhat * jnp.mean(gw * xhat, axis=-1, keepdims=True))
        lpart = _rowsum8(err * err) * (0.5 / d)
        wpart = _rowsum8(dy * xhat)

        @pl.when(pl.program_id(0) == 0)
        def _():
            loss_ref[...] = lpart
            dw_ref[...] = wpart

        @pl.when(pl.program_id(0) > 0)
        def _():
            loss_ref[...] += lpart
            dw_ref[...] += wpart

    row = pl.BlockSpec((tm, d), lambda i: (i, 0))
    acc = pl.BlockSpec((SUBLANES, d), lambda i: (0, 0))
    return pl.pallas_call(
        body, name=name, grid=(m // tm,),
        in_specs=[row, pl.BlockSpec((1, d), lambda i: (0, 0)), row], out_specs=[row, acc, acc],
        out_shape=[jax.ShapeDtypeStruct((m, d), f32), jax.ShapeDtypeStruct((SUBLANES, d), f32), jax.ShapeDtypeStruct((SUBLANES, d), f32)],
        compiler_params=_cparams(("arbitrary",)),
    )(h, w, target)


def _lane_tables():
    f = np.arange(LANES) % HEAD_DIM
    inv = ROPE_THETA ** (-jnp.arange(0, ROPE_DIM, 2, dtype=f32) / ROPE_DIM)
    invf = jnp.where(f < ROPE_DIM, inv[f % (ROPE_DIM // 2)], 0.0).astype(f32)
    return invf.reshape(1, LANES)


def _rope_tables(pos_col, name):
    t = pos_col.shape[0]
    tm = SEQ

    def body(p_ref, f_ref, c_ref, s1_ref, s2_ref):
        ang = p_ref[...].astype(f32) * f_ref[...]
        co, si = jnp.cos(ang), jnp.sin(ang)
        f = lax.broadcasted_iota(jnp.int32, (tm, LANES), 1) % HEAD_DIM
        c_ref[...] = jnp.where(f < ROPE_DIM, co, 1.0)
        s1_ref[...] = jnp.where(f < ROPE_DIM // 2, -si, 0.0)
        s2_ref[...] = jnp.where((f >= ROPE_DIM // 2) & (f < ROPE_DIM), si, 0.0)

    row = pl.BlockSpec((tm, LANES), lambda i: (i, 0))
    return pl.pallas_call(
        body, name=name, grid=(t // tm,),
        in_specs=[pl.BlockSpec((tm, 1), lambda i: (i, 0)), pl.BlockSpec((1, LANES), lambda i: (0, 0))],
        out_specs=[row, row, row], out_shape=[jax.ShapeDtypeStruct((t, LANES), f32)] * 3,
        compiler_params=_cparams(("parallel",)),
    )(pos_col, _lane_tables())


def _rot(x, c, s1, s2):
    return x * c + pltpu.roll(x, LANES - ROPE_DIM // 2, 1) * s1 + pltpu.roll(x, ROPE_DIM // 2, 1) * s2


def _rot_t(g, c, s1, s2):
    return g * c + pltpu.roll(g * s1, ROPE_DIM // 2, 1) + pltpu.roll(g * s2, LANES - ROPE_DIM // 2, 1)


def _dup_head(x, kvh, low):
    a = jnp.where(kvh == 0, x, pltpu.roll(x, HEAD_DIM, 1))
    return jnp.where(low, a, pltpu.roll(a, HEAD_DIM, 1))


def _deinterleave(src_ref, dst_ref, d, dtype):
    length = SEQ // d
    if d == 1:
        dst_ref[...] = src_ref[...].astype(dtype)
    else:
        for r in range(d):
            dst_ref[pl.ds(r * length, length), :] = src_ref[pl.ds(r, length, stride=d), :].astype(dtype)


def _interleave_store(src_ref, dst_ref, d, accumulate):
    length = SEQ // d
    if d == 1:
        if accumulate:
            dst_ref[...] += src_ref[...]
        else:
            dst_ref[...] = src_ref[...]
    else:
        for r in range(d):
            blk = src_ref[pl.ds(r * length, length), :]
            if accumulate:
                dst_ref[pl.ds(r, length, stride=d), :] = dst_ref[pl.ds(r, length, stride=d), :] + blk
            else:
                dst_ref[pl.ds(r, length, stride=d), :] = blk


def _attn_masks():
    qi = lax.broadcasted_iota(jnp.int32, (ATTN_BLOCK, ATTN_BLOCK), 0)
    ki = lax.broadcasted_iota(jnp.int32, (ATTN_BLOCK, ATTN_BLOCK), 1)
    low = lax.broadcasted_iota(jnp.int32, (ATTN_BLOCK, LANES), 1) < HEAD_DIM
    return ki <= qi, ki >= qi, low


NEG_INF = float("-inf")
ATTN_UNROLL = 4


def _attn_fwd(qkv, tabs, name):
    t = qkv.shape[0]
    nb = t // SEQ
    n_blk = SEQ // ATTN_BLOCK

    def body(q_ref, k_ref, v_ref, c_ref, s1_ref, s2_ref, o_ref, lse_ref,
             qr, kr, vr, qd, kd, vd, ob, lb, o0, o1, o2, l0, l1, l2, ss):
        kvh = pl.program_id(1) // 2
        cur_ok, prev_ok, low = _attn_masks()
        lowfull = lax.broadcasted_iota(jnp.int32, (SEQ, LANES), 1) < HEAD_DIM
        c, s1, s2 = c_ref[...], s1_ref[...], s2_ref[...]
        qr[...] = _rot(q_ref[...], c, s1, s2) * (HEAD_DIM ** -0.5)
        kr[...] = _dup_head(_rot(k_ref[...], c, s1, s2), kvh, lowfull)
        vr[...] = _dup_head(v_ref[...], kvh, lowfull)
        onat, lnat = (o0, o1, o2), (l0, l1, l2)
        for bi, d in enumerate(DILATIONS):
            _deinterleave(qr, qd, d, bf16)
            _deinterleave(kr, kd, d, bf16)
            _deinterleave(vr, vd, d, bf16)
            per_res = n_blk // d
            use_prev = per_res > 1

            def scores(n, carry):
                start = pl.multiple_of(n * ATTN_BLOCK, ATTN_BLOCK)
                has_prev = (n % per_res) != 0
                pstart = pl.multiple_of(jnp.maximum(n - 1, 0) * ATTN_BLOCK, ATTN_BLOCK)
                qb = qd[pl.ds(start, ATTN_BLOCK), :]
                kc = kd[pl.ds(start, ATTN_BLOCK), :]
                if use_prev:
                    kp = kd[pl.ds(pstart, ATTN_BLOCK), :]
                for a in range(2):
                    qa = jnp.where(low if a == 0 else ~low, qb, jnp.zeros_like(qb))
                    ss[2 * n + a, :, 0:ATTN_BLOCK] = jnp.where(cur_ok, _nt(qa, kc), NEG_INF)
                    if use_prev:
                        ss[2 * n + a, :, ATTN_BLOCK:2 * ATTN_BLOCK] = jnp.where(prev_ok & has_prev, _nt(qa, kp), NEG_INF)
                return carry

            def softmax_pv(n, carry):
                start = pl.multiple_of(n * ATTN_BLOCK, ATTN_BLOCK)
                pstart = pl.multiple_of(jnp.maximum(n - 1, 0) * ATTN_BLOCK, ATTN_BLOCK)
                vc = vd[pl.ds(start, ATTN_BLOCK), :]
                if use_prev:
                    vp = vd[pl.ds(pstart, ATTN_BLOCK), :]
                outs, lses = [], []
                for a in range(2):
                    sc = ss[2 * n + a, :, 0:ATTN_BLOCK]
                    if use_prev:
                        sp = ss[2 * n + a, :, ATTN_BLOCK:2 * ATTN_BLOCK]
                        m = jnp.max(jnp.maximum(sc, sp), axis=1, keepdims=True)
                        pc, pp = jnp.exp(sc - m), jnp.exp(sp - m)
                        den = jnp.sum(pc + pp, axis=1, keepdims=True)
                        acc = _nn(pc.astype(bf16), vc) + _nn(pp.astype(bf16), vp)
                    else:
                        m = jnp.max(sc, axis=1, keepdims=True)
                        pc = jnp.exp(sc - m)
                        den = jnp.sum(pc, axis=1, keepdims=True)
                        acc = _nn(pc.astype(bf16), vc)
                    outs.append(acc * (1.0 / den))
                    lses.append(m + jnp.log(den))
                ob[pl.ds(start, ATTN_BLOCK), :] = jnp.where(low, outs[0], outs[1])
                lb[pl.ds(start, ATTN_BLOCK), :] = jnp.where(low, lses[0], lses[1])
                return carry

            lax.fori_loop(0, n_blk, scores, 0, unroll=ATTN_UNROLL)
            lax.fori_loop(0, n_blk, softmax_pv, 0, unroll=ATTN_UNROLL)
            _interleave_store(ob, onat[bi], d, False)
            _interleave_store(lb, lnat[bi], d, False)
        la, lbb, lc = l0[...], l1[...], l2[...]
        lm = jnp.maximum(jnp.maximum(la, lbb), lc)
        wa, wb, wc = jnp.exp(la - lm), jnp.exp(lbb - lm), jnp.exp(lc - lm)
        ws = wa + wb + wc
        o_ref[...] = (wa * o0[...] + wb * o1[...] + wc * o2[...]) / ws
        lse_ref[...] = lm + jnp.log(ws)

    def col(jj):
        return pl.BlockSpec((SEQ, LANES), lambda b, j: (b, jj if jj is not None else j))

    tab = pl.BlockSpec((SEQ, LANES), lambda b, j: (b, 0))
    fs = pltpu.VMEM((SEQ, LANES), f32)
    hs = pltpu.VMEM((SEQ, LANES), bf16)
    return pl.pallas_call(
        body, name=name, grid=(nb, ATTN_WIDTH // LANES),
        in_specs=[col(None), col(ATTN_WIDTH // LANES), col(ATTN_WIDTH // LANES + 1), tab, tab, tab],
        out_specs=[col(None), col(None)],
        out_shape=[jax.ShapeDtypeStruct((t, ATTN_WIDTH), f32), jax.ShapeDtypeStruct((t, ATTN_WIDTH), f32)],
        scratch_shapes=[fs, fs, fs, hs, hs, hs, fs, fs, fs, fs, fs, fs, fs, fs, pltpu.VMEM((2 * n_blk, ATTN_BLOCK, 2 * ATTN_BLOCK), f32)],
        compiler_params=_cparams(("parallel", "parallel")),
    )(qkv, qkv, qkv, *tabs)


def _attn_bwd(qkv, tabs, o, lse, do, name):
    t = qkv.shape[0]
    nb = t // SEQ
    n_blk = SEQ // ATTN_BLOCK
    n_j = ATTN_WIDTH // LANES

    def body(q_ref, k_ref, v_ref, c_ref, s1_ref, s2_ref, o_ref, lse_ref, do_ref, dq_ref, dk_ref, dv_ref,
             qr, kr, vr, dl, qd, kd, vd, dod, lsd, dld, dqd, dkd, dvd, dqa, dka, dva, pb, dsb):
        j = pl.program_id(1)
        pb[2 * n_blk:2 * n_blk + 2] = jnp.zeros((2, ATTN_BLOCK, 2 * ATTN_BLOCK), bf16)
        dsb[2 * n_blk:2 * n_blk + 2] = jnp.zeros((2, ATTN_BLOCK, 2 * ATTN_BLOCK), bf16)
        kvh = j // 2
        cur_ok, prev_ok, low = _attn_masks()
        lowfull = lax.broadcasted_iota(jnp.int32, (SEQ, LANES), 1) < HEAD_DIM
        c, s1, s2 = c_ref[...], s1_ref[...], s2_ref[...]
        qr[...] = _rot(q_ref[...], c, s1, s2) * (HEAD_DIM ** -0.5)
        kr[...] = _dup_head(_rot(k_ref[...], c, s1, s2), kvh, lowfull)
        vr[...] = _dup_head(v_ref[...], kvh, lowfull)
        prod = do_ref[...] * o_ref[...]
        d_lo = jnp.sum(jnp.where(lowfull, prod, 0.0), axis=1, keepdims=True)
        d_hi = jnp.sum(jnp.where(lowfull, 0.0, prod), axis=1, keepdims=True)
        dl[...] = jnp.where(lowfull, d_lo, d_hi)
        dqa[...] = jnp.zeros_like(dqa)
        dka[...] = jnp.zeros_like(dka)
        dva[...] = jnp.zeros_like(dva)
        for d in DILATIONS:
            _deinterleave(qr, qd, d, bf16)
            _deinterleave(kr, kd, d, bf16)
            _deinterleave(vr, vd, d, bf16)
            _deinterleave(do_ref, dod, d, bf16)
            _deinterleave(lse_ref, lsd, d, f32)
            _deinterleave(dl, dld, d, f32)
            per_res = n_blk // d
            use_prev = per_res > 1
            curl, prevl = slice(0, ATTN_BLOCK), slice(ATTN_BLOCK, 2 * ATTN_BLOCK)

            def halves(x):
                zero = jnp.zeros_like(x)
                return jnp.where(low, x, zero), jnp.where(low, zero, x)

            def probs(n, carry):
                start = pl.multiple_of(n * ATTN_BLOCK, ATTN_BLOCK)
                has_prev = (n % per_res) != 0
                pstart = pl.multiple_of(jnp.maximum(n - 1, 0) * ATTN_BLOCK, ATTN_BLOCK)
                cur, prev = pl.ds(start, ATTN_BLOCK), pl.ds(pstart, ATTN_BLOCK)
                qas, doas = halves(qd[cur, :]), halves(dod[cur, :])
                kc, vc = kd[cur, :], vd[cur, :]
                if use_prev:
                    kp, vp = kd[prev, :], vd[prev, :]
                lsb, dlb = lsd[cur, :], dld[cur, :]
                for a in range(2):
                    ls = lsb[:, a * HEAD_DIM:a * HEAD_DIM + 1]
                    de = dlb[:, a * HEAD_DIM:a * HEAD_DIM + 1]
                    pc = jnp.exp(jnp.where(cur_ok, _nt(qas[a], kc), NEG_INF) - ls)
                    pb[2 * n + a, :, curl] = pc.astype(bf16)
                    dsb[2 * n + a, :, curl] = (pc * (_nt(doas[a], vc) - de)).astype(bf16)
                    if use_prev:
                        pp = jnp.exp(jnp.where(prev_ok & has_prev, _nt(qas[a], kp), NEG_INF) - ls)
                        pb[2 * n + a, :, prevl] = pp.astype(bf16)
                        dsb[2 * n + a, :, prevl] = (pp * (_nt(doas[a], vp) - de)).astype(bf16)
                return carry

            def grads(n, carry):
                start = pl.multiple_of(n * ATTN_BLOCK, ATTN_BLOCK)
                pstart = pl.multiple_of(jnp.maximum(n - 1, 0) * ATTN_BLOCK, ATTN_BLOCK)
                nstart = pl.multiple_of(jnp.minimum(n + 1, n_blk - 1) * ATTN_BLOCK, ATTN_BLOCK)
                cur, prev, nxt = pl.ds(start, ATTN_BLOCK), pl.ds(pstart, ATTN_BLOCK), pl.ds(nstart, ATTN_BLOCK)
                kc = kd[cur, :]
                dqs = [_nn(dsb[2 * n + a, :, curl], kc) for a in range(2)]
                q_rows, do_rows = list(halves(qd[cur, :])), list(halves(dod[cur, :]))
                ds_rows, p_rows = [dsb[2 * n + a, :, curl] for a in range(2)], [pb[2 * n + a, :, curl] for a in range(2)]
                if use_prev:
                    kp = kd[prev, :]
                    dqs = [dqs[a] + _nn(dsb[2 * n + a, :, prevl], kp) for a in range(2)]
                    q_rows += list(halves(qd[nxt, :]))
                    do_rows += list(halves(dod[nxt, :]))
                    ds_rows += [dsb[2 * n + 2 + a, :, prevl] for a in range(2)]
                    p_rows += [pb[2 * n + 2 + a, :, prevl] for a in range(2)]
                dqd[cur, :] = jnp.where(low, dqs[0], dqs[1])
                dkd[cur, :] = _tn(jnp.concatenate(ds_rows, axis=0), jnp.concatenate(q_rows, axis=0))
                dvd[cur, :] = _tn(jnp.concatenate(p_rows, axis=0), jnp.concatenate(do_rows, axis=0))
                return carry

            lax.fori_loop(0, n_blk, probs, 0, unroll=ATTN_UNROLL)
            lax.fori_loop(0, n_blk, grads, 0, unroll=ATTN_UNROLL)
            _interleave_store(dqd, dqa, d, True)
            _interleave_store(dkd, dka, d, True)
            _interleave_store(dvd, dva, d, True)
        dq_ref[...] = _rot_t(dqa[...] * (HEAD_DIM ** -0.5), c, s1, s2)
        dkf = dka[...]
        dkf = _rot_t(dkf + pltpu.roll(dkf, HEAD_DIM, 1), c, s1, s2)
        dvf = dva[...]
        dvf = dvf + pltpu.roll(dvf, HEAD_DIM, 1)
        mine = (lax.broadcasted_iota(jnp.int32, (SEQ, LANES), 1) // HEAD_DIM) == kvh
        dkc_, dvc_ = jnp.where(mine, dkf, 0.0), jnp.where(mine, dvf, 0.0)

        @pl.when(j == 0)
        def _():
            dk_ref[...] = dkc_
            dv_ref[...] = dvc_

        @pl.when(j > 0)
        def _():
            dk_ref[...] += dkc_
            dv_ref[...] += dvc_

    def col(jj):
        return pl.BlockSpec((SEQ, LANES), lambda b, j: (b, jj if jj is not None else j))

    tab = pl.BlockSpec((SEQ, LANES), lambda b, j: (b, 0))
    fs = pltpu.VMEM((SEQ, LANES), f32)
    hs = pltpu.VMEM((SEQ, LANES), bf16)
    return pl.pallas_call(
        body, name=name, grid=(nb, n_j),
        in_specs=[col(None), col(n_j), col(n_j + 1), tab, tab, tab, col(None), col(None), col(None)],
        out_specs=[col(None), tab, tab],
        out_shape=[jax.ShapeDtypeStruct((t, ATTN_WIDTH), f32), jax.ShapeDtypeStruct((t, LANES), f32), jax.ShapeDtypeStruct((t, LANES), f32)],
        scratch_shapes=[fs, fs, fs, fs, hs, hs, hs, hs, fs, fs, fs, fs, fs, fs, fs, fs,
                        pltpu.VMEM((2 * n_blk + 2, ATTN_BLOCK, 2 * ATTN_BLOCK), bf16), pltpu.VMEM((2 * n_blk + 2, ATTN_BLOCK, 2 * ATTN_BLOCK), bf16)],
        compiler_params=_cparams(("parallel", "arbitrary")),
    )(qkv, qkv, qkv, *tabs, o, lse, do)


def _conv_pre(x, w_ref, b_ref, row):
    shifted = [x] + [jnp.where(row >= s, pltpu.roll(x, s, 0), 0.0) for s in range(1, CONV_WIDTH)]
    pre = b_ref[...] + w_ref[CONV_WIDTH - 1:CONV_WIDTH, :] * x
    for s in range(1, CONV_WIDTH):
        pre = pre + w_ref[CONV_WIDTH - 1 - s:CONV_WIDTH - s, :] * shifted[s]
    return pre, shifted


def _conv_fwd(x, w, b, name, tc=512):
    t, ch = x.shape

    def body(x_ref, w_ref, b_ref, o_ref):
        row = lax.broadcasted_iota(jnp.int32, (SEQ, tc), 0)
        pre, _ = _conv_pre(x_ref[...], w_ref, b_ref, row)
        o_ref[...] = _silu(pre)

    xs = pl.BlockSpec((SEQ, tc), lambda i, j: (i, j))
    return pl.pallas_call(
        body, name=name, grid=(t // SEQ, ch // tc),
        in_specs=[xs, pl.BlockSpec((CONV_WIDTH, tc), lambda i, j: (0, j)), pl.BlockSpec((1, tc), lambda i, j: (0, j))],
        out_specs=xs, out_shape=jax.ShapeDtypeStruct((t, ch), f32),
        compiler_params=_cparams(("parallel", "parallel")),
    )(x, w, b)


def _conv_bwd(x, w, b, dact, name, tc=512):
    t, ch = x.shape

    def body(x_ref, w_ref, b_ref, d_ref, dx_ref, dw_ref, db_ref):
        row = lax.broadcasted_iota(jnp.int32, (SEQ, tc), 0)
        pre, shifted = _conv_pre(x_ref[...], w_ref, b_ref, row)
        dpre = d_ref[...] * _dsilu(pre)
        dx = w_ref[CONV_WIDTH - 1:CONV_WIDTH, :] * dpre
        for s in range(1, CONV_WIDTH):
            dx = dx + w_ref[CONV_WIDTH - 1 - s:CONV_WIDTH - s, :] * jnp.where(row < SEQ - s, pltpu.roll(dpre, SEQ - s, 0), 0.0)
        dx_ref[...] = dx
        first = pl.program_id(1) == 0
        parts = [jnp.sum(dpre * shifted[CONV_WIDTH - 1 - k], axis=0, keepdims=True) for k in range(CONV_WIDTH)]
        dbp = jnp.sum(dpre, axis=0, keepdims=True)

        @pl.when(first)
        def _():
            for k in range(CONV_WIDTH):
                dw_ref[k:k + 1, :] = parts[k]
            db_ref[...] = dbp

        @pl.when(jnp.logical_not(first))
        def _():
            for k in range(CONV_WIDTH):
                dw_ref[k:k + 1, :] += parts[k]
            db_ref[...] += dbp

    xs = pl.BlockSpec((SEQ, tc), lambda j, i: (i, j))
    ws = pl.BlockSpec((CONV_WIDTH, tc), lambda j, i: (0, j))
    bs = pl.BlockSpec((1, tc), lambda j, i: (0, j))
    return pl.pallas_call(
        body, name=name, grid=(ch // tc, t // SEQ),
        in_specs=[xs, ws, bs, xs], out_specs=[xs, ws, bs],
        out_shape=[jax.ShapeDtypeStruct((t, ch), f32), jax.ShapeDtypeStruct((CONV_WIDTH, ch), f32), jax.ShapeDtypeStruct((1, ch), f32)],
        compiler_params=_cparams(("parallel", "arbitrary")),
    )(x, w, b, dact)


GROUP_W = SSM_INNER // SSM_GROUPS
HEADS_PER_GROUP = SSM_HEADS // SSM_GROUPS
HI = lax.Precision.HIGHEST


def _ssd_common(xbc_ref, dt_ref, bias_ref, alog_ref):
    r = lax.broadcasted_iota(jnp.int32, (CHUNK, CHUNK), 0)
    cidx = lax.broadcasted_iota(jnp.int32, (CHUNK, CHUNK), 1)
    causal = r >= cidx
    tril = causal.astype(f32)
    expand = (lax.broadcasted_iota(jnp.int32, (CHUNK, SSM_INNER), 0)
              == lax.broadcasted_iota(jnp.int32, (CHUNK, SSM_INNER), 1) // HEAD_DIM).astype(f32)
    head_lane = cidx < SSM_HEADS
    dtp = dt_ref[...] + bias_ref[...]
    dt = jnp.where(head_lane, _softplus(dtp), 0.0)
    a_neg = -jnp.exp(alog_ref[...])
    a = dt * a_neg
    cs = _nn(tril, a, HI)
    dt_e = _nn(dt, expand, HI)
    cs_e = _nn(cs, expand, HI)
    xs = xbc_ref[:, 0:SSM_INNER]
    xg = xs * dt_e
    ecs = jnp.exp(cs_e)
    cs_last = cs_e[CHUNK - 1:CHUNK, :]
    dse = jnp.exp(cs_last - cs_e)
    cde = jnp.exp(cs_last)
    return dict(r=r, cidx=cidx, causal=causal, tril=tril, expand=expand, head_lane=head_lane, dtp=dtp, dt=dt, a_neg=a_neg,
                cs=cs, cst=cs.T, dt_e=dt_e, cs_e=cs_e, xs=xs, xg=xg, ecs=ecs, dse=dse, cde=cde)


def _decay_mat(q, h):
    return jnp.exp(jnp.where(q["causal"], q["cs"][:, h:h + 1] - q["cst"][h:h + 1, :], NEG_INF))


def _gate_norm(y, z, nw):
    y2 = y * _silu(z)
    outs, xhats, rs = [], [], []
    for g in range(SSM_GROUPS):
        sl = slice(g * GROUP_W, (g + 1) * GROUP_W)
        yg = y2[:, sl]
        r = lax.rsqrt(jnp.mean(yg * yg, axis=-1, keepdims=True) + EPS)
        xhats.append(yg * r)
        rs.append(r)
        outs.append(yg * r * nw[:, sl])
    return y2, outs, xhats, rs


def _ssd_fwd(xbc, z, dtp, params, name):
    t = xbc.shape[0]
    n_chunk = SEQ // CHUNK
    low = None

    def body(xbc_ref, z_ref, dt_ref, bias_ref, alog_ref, dskip_ref, nw_ref, yn_ref, y_ref, hs_ref, h_scr):
        @pl.when(pl.program_id(1) == 0)
        def _():
            h_scr[...] = jnp.zeros_like(h_scr)

        q = _ssd_common(xbc_ref, dt_ref, bias_ref, alog_ref)
        low = lax.broadcasted_iota(jnp.int32, (CHUNK, LANES), 1) < HEAD_DIM
        xgb = q["xg"].astype(bf16)
        wst = (q["xg"] * q["dse"]).astype(bf16)
        hs_ref[0] = h_scr[...]
        ys = []
        for g in range(SSM_GROUPS):
            gl = slice(g * GROUP_W, (g + 1) * GROUP_W)
            bg = xbc_ref[:, SSM_INNER + g * D_STATE:SSM_INNER + (g + 1) * D_STATE].astype(bf16)
            cg = xbc_ref[:, SSM_INNER + SSM_GROUPS * D_STATE + g * D_STATE:SSM_INNER + SSM_GROUPS * D_STATE + (g + 1) * D_STATE].astype(bf16)
            cb = _nt(cg, bg)
            hg = h_scr[g]
            yoff = _nn(cg, hg.astype(bf16)) * q["ecs"][:, gl]
            pieces = []
            for i in range(HEADS_PER_GROUP // 2):
                h0 = g * HEADS_PER_GROUP + 2 * i
                xp = xgb[:, h0 * HEAD_DIM:(h0 + 2) * HEAD_DIM]
                m0 = (cb * _decay_mat(q, h0)).astype(bf16)
                m1 = (cb * _decay_mat(q, h0 + 1)).astype(bf16)
                zero = jnp.zeros_like(xp)
                pieces.append(_nn(m0, jnp.where(low, xp, zero)) + _nn(m1, jnp.where(low, zero, xp)))
            ys.append(jnp.concatenate(pieces, axis=1) + yoff + dskip_ref[:, gl] * q["xs"][:, gl])
            h_scr[g] = hg * q["cde"][:, gl] + _tn(bg, wst[:, gl])
        y = jnp.concatenate(ys, axis=1)
        y_ref[...] = y
        _, outs, _, _ = _gate_norm(y, z_ref[...], nw_ref[...])
        yn_ref[...] = jnp.concatenate(outs, axis=1).astype(bf16)

    def rows(w):
        return pl.BlockSpec((CHUNK, w), lambda b, c: (b * n_chunk + c, 0))

    def par(w):
        return pl.BlockSpec((1, w), lambda b, c: (0, 0))

    return pl.pallas_call(
        body, name=name, grid=(t // SEQ, n_chunk),
        in_specs=[rows(CONV_CH), rows(SSM_INNER), rows(LANES), par(LANES), par(LANES), par(SSM_INNER), par(SSM_INNER)],
        out_specs=[rows(SSM_INNER), rows(SSM_INNER), pl.BlockSpec((1, SSM_GROUPS, D_STATE, GROUP_W), lambda b, c: (b * n_chunk + c, 0, 0, 0))],
        out_shape=[jax.ShapeDtypeStruct((t, SSM_INNER), bf16), jax.ShapeDtypeStruct((t, SSM_INNER), f32),
                   jax.ShapeDtypeStruct((t // CHUNK, SSM_GROUPS, D_STATE, GROUP_W), f32)],
        scratch_shapes=[pltpu.VMEM((SSM_GROUPS, D_STATE, GROUP_W), f32)],
        compiler_params=_cparams(("parallel", "arbitrary")),
    )(xbc, z, dtp, *params)


def _ssd_bwd(xbc, z, dtp, y, hs, dyn, params, name):
    t = xbc.shape[0]
    n_chunk = SEQ // CHUNK

    def body(xbc_ref, z_ref, dt_ref, y_ref, hs_ref, dyn_ref, bias_ref, alog_ref, dskip_ref, nw_ref,
             dxbc_ref, dz_ref, ddt_ref, dnw_ref, dds_ref, dal_ref, dbi_ref, dh_scr):
        @pl.when(pl.program_id(1) == 0)
        def _():
            dh_scr[...] = jnp.zeros_like(dh_scr)

        q = _ssd_common(xbc_ref, dt_ref, bias_ref, alog_ref)
        low = lax.broadcasted_iota(jnp.int32, (CHUNK, LANES), 1) < HEAD_DIM
        last_row = lax.broadcasted_iota(jnp.int32, (CHUNK, GROUP_W), 0) == CHUNK - 1
        xs, xg = q["xs"], q["xg"]
        xgb = xg.astype(bf16)
        wf = xg * q["dse"]
        wst = wf.astype(bf16)
        zz = z_ref[...]
        yy = y_ref[...]
        sz = _silu(zz)
        y2, _, xhats, rs = _gate_norm(yy, zz, nw_ref[...])
        dyn_ = dyn_ref[...]
        dy2s, dnws = [], []
        for g in range(SSM_GROUPS):
            gl = slice(g * GROUP_W, (g + 1) * GROUP_W)
            gw = dyn_[:, gl] * nw_ref[:, gl]
            dy2s.append(rs[g] * (gw - xhats[g] * jnp.mean(gw * xhats[g], axis=-1, keepdims=True)))
            dnws.append(_rowsum8(dyn_[:, gl] * xhats[g]))
        dy2 = jnp.concatenate(dy2s, axis=1)
        dy = dy2 * sz
        dz_ref[...] = dy2 * yy * _dsilu(zz)
        dnw_p = jnp.concatenate(dnws, axis=1)
        dds_p = _rowsum8(dy * xs)
        dyb = dy.astype(bf16)
        gfull = (dy * q["ecs"]).astype(bf16)
        dcs_c = jnp.zeros((CHUNK, CHUNK), f32)
        dcs_r = jnp.zeros((CHUNK, CHUNK), f32)
        dcs_e_parts, dxg_parts = [], []
        for g in range(SSM_GROUPS):
            gl = slice(g * GROUP_W, (g + 1) * GROUP_W)
            bsl = slice(SSM_INNER + g * D_STATE, SSM_INNER + (g + 1) * D_STATE)
            csl = slice(SSM_INNER + SSM_GROUPS * D_STATE + g * D_STATE, SSM_INNER + SSM_GROUPS * D_STATE + (g + 1) * D_STATE)
            bg = xbc_ref[:, bsl].astype(bf16)
            cg = xbc_ref[:, csl].astype(bf16)
            cb = _nt(cg, bg)
            hg = hs_ref[0, g]
            hgb = hg.astype(bf16)
            dhn = dh_scr[g]
            dhnb = dhn.astype(bf16)
            yoff = _nn(cg, hgb) * q["ecs"][:, gl]
            dw_ = _nn(bg, dhnb)
            r_e = dw_ * wf[:, gl]
            to_last = jnp.sum(r_e, axis=0, keepdims=True) + jnp.sum(dhn * hg, axis=0, keepdims=True) * q["cde"][:, gl]
            dcs_e_parts.append(dy[:, gl] * yoff - r_e + jnp.where(last_row, to_last, 0.0))
            dcb = jnp.zeros((CHUNK, CHUNK), f32)
            dxg_pairs = []
            for i in range(HEADS_PER_GROUP // 2):
                h0 = g * HEADS_PER_GROUP + 2 * i
                psl = slice(h0 * HEAD_DIM, (h0 + 2) * HEAD_DIM)
                xp = xgb[:, psl]
                dyp = dyb[:, psl]
                zero = jnp.zeros_like(dyp)
                tns = []
                for a in range(2):
                    h = h0 + a
                    lm = _decay_mat(q, h)
                    m = cb * lm
                    dm = _nt(jnp.where(low, dyp, zero) if a == 0 else jnp.where(low, zero, dyp), xp)
                    dcb = dcb + dm * lm
                    nmat = dm * m
                    dcs_c = dcs_c + jnp.where(q["cidx"] == h, jnp.sum(nmat, axis=1, keepdims=True), 0.0)
                    dcs_r = dcs_r + jnp.where(q["r"] == h, jnp.sum(nmat, axis=0, keepdims=True), 0.0)
                    tns.append(_tn(m.astype(bf16), dyp))
                dxg_pairs.append(jnp.where(low, tns[0], tns[1]))
            dxg_parts.append(jnp.concatenate(dxg_pairs, axis=1) + dw_ * q["dse"][:, gl])
            dcbb = dcb.astype(bf16)
            dxbc_ref[:, csl] = _nt(gfull[:, gl], hgb) + _nn(dcbb, bg)
            dxbc_ref[:, bsl] = _nt(wst[:, gl], dhnb) + _tn(dcbb, cg)
            dh_scr[g] = dhn * q["cde"][:, gl] + _tn(cg, gfull[:, gl])
        dxg = jnp.concatenate(dxg_parts, axis=1)
        dcs_e = jnp.concatenate(dcs_e_parts, axis=1)
        dxbc_ref[:, 0:SSM_INNER] = dskip_ref[...] * dy + dxg * q["dt_e"]
        dcs = dcs_c - dcs_r.T + _dot(dcs_e, q["expand"], ((1,), (1,)), HI)
        triu = (q["cidx"] >= q["r"]).astype(f32)
        da = _nn(triu, dcs, HI)
        ddt = _dot(dxg * xs, q["expand"], ((1,), (1,)), HI) + da * q["a_neg"]
        ddtp = jnp.where(q["head_lane"], ddt * _sigmoid(q["dtp"]), 0.0)
        ddt_ref[...] = ddtp
        dal_p = _rowsum8(da * q["dt"]) * q["a_neg"]
        dbi_p = _rowsum8(ddtp)
        first = (pl.program_id(0) == 0) & (pl.program_id(1) == 0)

        @pl.when(first)
        def _():
            dnw_ref[...] = dnw_p
            dds_ref[...] = dds_p
            dal_ref[...] = dal_p
            dbi_ref[...] = dbi_p

        @pl.when(jnp.logical_not(first))
        def _():
            dnw_ref[...] += dnw_p
            dds_ref[...] += dds_p
            dal_ref[...] += dal_p
            dbi_ref[...] += dbi_p

    def rows(w):
        return pl.BlockSpec((CHUNK, w), lambda b, c: (b * n_chunk + n_chunk - 1 - c, 0))

    def par(w):
        return pl.BlockSpec((1, w), lambda b, c: (0, 0))

    def acc(w):
        return pl.BlockSpec((SUBLANES, w), lambda b, c: (0, 0))

    return pl.pallas_call(
        body, name=name, grid=(t // SEQ, n_chunk),
        in_specs=[rows(CONV_CH), rows(SSM_INNER), rows(LANES), rows(SSM_INNER),
                  pl.BlockSpec((1, SSM_GROUPS, D_STATE, GROUP_W), lambda b, c: (b * n_chunk + n_chunk - 1 - c, 0, 0, 0)),
                  rows(SSM_INNER), par(LANES), par(LANES), par(SSM_INNER), par(SSM_INNER)],
        out_specs=[rows(CONV_CH), rows(SSM_INNER), rows(LANES), acc(SSM_INNER), acc(SSM_INNER), acc(LANES), acc(LANES)],
        out_shape=[jax.ShapeDtypeStruct((t, CONV_CH), f32), jax.ShapeDtypeStruct((t, SSM_INNER), f32), jax.ShapeDtypeStruct((t, LANES), f32),
                   jax.ShapeDtypeStruct((SUBLANES, SSM_INNER), f32), jax.ShapeDtypeStruct((SUBLANES, SSM_INNER), f32),
                   jax.ShapeDtypeStruct((SUBLANES, LANES), f32), jax.ShapeDtypeStruct((SUBLANES, LANES), f32)],
        scratch_shapes=[pltpu.VMEM((SSM_GROUPS, D_STATE, GROUP_W), f32)],
        compiler_params=_cparams(("arbitrary", "arbitrary")),
    )(xbc, z, dtp, y, hs, dyn, *params)


def _adamw(g_parts, w, m, v, name, layer=None):
    rows, width = w.shape[-2:]
    n = len(g_parts)
    tr = _row_tile(rows)

    def body(*refs):
        g_refs, (w_ref, m_ref, v_ref, g_out, d_out, m_out, v_out) = refs[:n], refs[n:]

        def part(i):
            return (g_refs[i][...] if g_parts[i][1] is None else g_refs[i][0]).astype(f32)

        def state(ref):
            return ref[...] if layer is None else ref[0]

        g = part(0)
        for i in range(1, n):
            g = g + part(i)
        mm = ADAM_B1 * state(m_ref) + (1.0 - ADAM_B1) * g
        vv = ADAM_B2 * state(v_ref) + (1.0 - ADAM_B2) * (g * g)
        m_hat = mm / (1.0 - ADAM_B1 ** ADAM_STEP)
        v_hat = vv / (1.0 - ADAM_B2 ** ADAM_STEP)
        g_out[...] = g
        d_out[...] = -ADAM_LR * (m_hat / (jnp.sqrt(v_hat) + ADAM_EPS) + ADAM_WD * state(w_ref))
        m_out[...] = mm
        v_out[...] = vv

    spec = pl.BlockSpec((tr, width), lambda i: (i, 0))

    def lead(idx):
        return spec if idx is None else pl.BlockSpec((1, tr, width), lambda i: (idx, i, 0))

    return pl.pallas_call(
        body, name=name, grid=(rows // tr,), in_specs=[lead(idx) for _, idx in g_parts] + [lead(layer)] * 3, out_specs=[spec] * 4,
        out_shape=[jax.ShapeDtypeStruct((rows, width), f32)] * 4, compiler_params=_cparams(("parallel",)),
    )(*[a for a, _ in g_parts], w, m, v)


def _row_tile(rows, cap=512):
    for cand in range(min(rows, cap) // SUBLANES * SUBLANES, 0, -SUBLANES):
        if rows % cand == 0:
            return cand
    return rows


def _cols_from_devices(g, width, name):
    n_dev, depth, a, b = g.shape

    def body(g_ref, o_ref):
        for i in range(n_dev):
            o_ref[0, :, i * b:(i + 1) * b] = g_ref[i, 0]
        if width > n_dev * b:
            o_ref[0, :, n_dev * b:width] = jnp.zeros((a, width - n_dev * b), o_ref.dtype)

    return pl.pallas_call(
        body, name=name, grid=(depth,), in_specs=[pl.BlockSpec((n_dev, 1, a, b), lambda l: (0, l, 0, 0))],
        out_specs=pl.BlockSpec((1, a, width), lambda l: (l, 0, 0)), out_shape=jax.ShapeDtypeStruct((depth, a, width), g.dtype),
        compiler_params=_cparams(("parallel",)),
    )(g)


def _devices_from_cols(per_layer, b, name, tr=256):
    depth = len(per_layer)
    a, width = per_layer[0].shape

    def body(*refs):
        o_ref = refs[depth]
        for l in range(depth):
            for i in range(N_DEV):
                o_ref[i, l] = refs[l][:, i * b:(i + 1) * b]

    return pl.pallas_call(
        body, name=name, grid=(a // tr,), in_specs=[pl.BlockSpec((tr, width), lambda r: (r, 0))] * depth,
        out_specs=pl.BlockSpec((N_DEV, depth, tr, b), lambda r: (0, 0, r, 0)),
        out_shape=jax.ShapeDtypeStruct((N_DEV, depth, a, b), per_layer[0].dtype), compiler_params=_cparams(("parallel",)),
    )(*per_layer)


def _me():
    return lax.axis_index("x"), lax.axis_index("y"), lax.axis_index("c")


def _allgather_two_level(shards, name):
    n = len(shards)
    per = 7

    def body(*refs):
        ins, outs, token = refs[:n], refs[n:2 * n], refs[2 * n]
        send_sems, recv_sems, local_sems = refs[2 * n + 1:]
        token[...] = jnp.zeros_like(token)
        x, y, c = _me()
        me, sibling = (x, y, c), (x, y, 1 - c)
        chips = [(1 - x, y), (x, 1 - y), (1 - x, 1 - y)]

        def slot(a, p):
            return outs[a].at[4 * p[0] + 2 * p[1] + p[2]]

        def copy(a, k, block, to, src=None):
            return pltpu.make_async_remote_copy(
                src_ref=slot(a, block) if src is None else src, dst_ref=slot(a, block),
                send_sem=send_sems.at[a * per + k], recv_sem=recv_sems.at[a * per + k], device_id=to, device_id_type=MESH)

        mine = [pltpu.make_async_copy(ins[a], slot(a, me), local_sems.at[a]) for a in range(n)]
        for cp in mine:
            cp.start()
        first = []
        for a in range(n):
            first.append(copy(a, 0, me, sibling, src=ins[a]))
            first += [copy(a, 1 + j, me, (*chip, c), src=ins[a]) for j, chip in enumerate(chips)]
        for cp in first:
            cp.start()
        passed = []
        for j, chip in enumerate(chips):
            for a in range(n):
                copy(a, 1 + j, (*chip, c), me).wait_recv()
                fwd = copy(a, 4 + j, (*chip, c), sibling)
                fwd.start()
                passed.append(fwd)
        for a in range(n):
            copy(a, 0, sibling, me).wait_recv()
            for j, chip in enumerate(chips):
                copy(a, 4 + j, (*chip, 1 - c), me).wait_recv()
        for cp in first + passed:
            cp.wait_send()
        for cp in mine:
            cp.wait()

    outs = pl.pallas_call(
        body, name=name, in_specs=[ANY] * n, out_specs=[ANY] * n + [pl.BlockSpec(memory_space=pltpu.VMEM)],
        out_shape=[jax.ShapeDtypeStruct((N_DEV,) + s.shape, s.dtype) for s in shards] + [jax.ShapeDtypeStruct((SUBLANES, LANES), f32)],
        scratch_shapes=[pltpu.SemaphoreType.DMA((n * per,)), pltpu.SemaphoreType.DMA((n * per,)), pltpu.SemaphoreType.DMA((n,))],
    )(*shards)
    return outs[:n], outs[n]


def _allgather_direct(row, name):
    def body(in_ref, out_ref, send_sems, recv_sems, local_sem):
        x, y, c = _me()
        mine = out_ref.at[4 * x + 2 * y + c]
        local = pltpu.make_async_copy(in_ref, mine, local_sem)
        local.start()
        sends = []
        for k in range(1, N_DEV):
            px, py, pc = x ^ (k >> 2), y ^ ((k >> 1) & 1), c ^ (k & 1)
            sends.append(pltpu.make_async_remote_copy(
                src_ref=in_ref, dst_ref=mine, send_sem=send_sems.at[k - 1], recv_sem=recv_sems.at[k - 1],
                device_id=(px, py, pc), device_id_type=MESH))
        for cp in sends:
            cp.start()
        for k in range(1, N_DEV):
            px, py, pc = x ^ (k >> 2), y ^ ((k >> 1) & 1), c ^ (k & 1)
            theirs = out_ref.at[4 * px + 2 * py + pc]
            pltpu.make_async_remote_copy(
                src_ref=in_ref, dst_ref=theirs, send_sem=send_sems.at[k - 1], recv_sem=recv_sems.at[k - 1],
                device_id=(px, py, pc), device_id_type=MESH).wait_recv()
        for cp in sends:
            cp.wait_send()
        local.wait()

    return pl.pallas_call(
        body, name=name, in_specs=[ANY], out_specs=ANY, out_shape=jax.ShapeDtypeStruct((N_DEV,) + row.shape, row.dtype),
        scratch_shapes=[pltpu.SemaphoreType.DMA((N_DEV - 1,)), pltpu.SemaphoreType.DMA((N_DEV - 1,)), pltpu.SemaphoreType.DMA],
    )(row)


N_CHIP = N_DEV // 2
HBM = pl.BlockSpec(memory_space=pltpu.HBM)
SEM = pl.BlockSpec(memory_space=pltpu.SEMAPHORE)
EFFECT = pltpu.SideEffectType.DATAFLOW_SIDE_EFFECTING


def _peer(k):
    x, y, c = _me()
    return x ^ (k >> 2), y ^ ((k >> 1) & 1), c ^ (k & 1)


def _direct_copies(srcs, lands, send_sems, recv_sems, per_peer):
    x, y, c = _me()
    me = 4 * x + 2 * y + c
    copies = []
    for a in range(len(srcs)):
        for k in range(1, N_DEV):
            px, py, pc = _peer(k)
            piece = srcs[a].at[4 * px + 2 * py + pc] if per_peer else srcs[a]
            copies.append(pltpu.make_async_remote_copy(
                src_ref=piece, dst_ref=lands[a].at[me], send_sem=send_sems.at[a * (N_DEV - 1) + k - 1],
                recv_sem=recv_sems.at[a * (N_DEV - 1) + k - 1], device_id=(px, py, pc), device_id_type=MESH))
    return copies


def _direct_start(srcs, lands, per_peer, name):
    n = len(srcs)
    n_sem = n * (N_DEV - 1)

    def body(*refs):
        src_refs, land_refs = refs[:n], refs[n:2 * n]
        send_sems, recv_sems = refs[2 * n], refs[2 * n + 1]
        token = refs[-1]
        for cp in _direct_copies(src_refs, land_refs, send_sems, recv_sems, per_peer):
            cp.start()
        token[...] = jnp.zeros_like(token)

    outs = pl.pallas_call(
        body, name=name,
        out_shape=(pltpu.SemaphoreType.DMA((n_sem,)), pltpu.SemaphoreType.DMA((n_sem,)),
                   *[pltpu.HBM(s.shape, s.dtype) for s in srcs], *[pltpu.HBM(s.shape, s.dtype) for s in lands],
                   jax.ShapeDtypeStruct((SUBLANES, LANES), f32)),
        in_specs=[HBM] * (2 * n), out_specs=(SEM, SEM, *[HBM] * (2 * n), pl.BlockSpec(memory_space=pltpu.VMEM)),
        input_output_aliases={i: 2 + i for i in range(2 * n)},
        compiler_params=pltpu.CompilerParams(has_side_effects=EFFECT),
    )(*[pltpu.with_memory_space_constraint(s, pltpu.HBM) for s in srcs], *[pltpu.with_memory_space_constraint(s, pltpu.HBM) for s in lands])
    return outs[0], outs[1], outs[2:2 + n], outs[2 + n:2 + 2 * n], outs[-1]


def _direct_wait(send_sems, recv_sems, srcs, lands, after, per_peer, name):
    n = len(srcs)

    def body(*refs):
        src_refs, land_refs = refs[:n], refs[n:2 * n]
        s_sems, r_sems = refs[2 * n], refs[2 * n + 1]
        for cp in _direct_copies(src_refs, land_refs, s_sems, r_sems, per_peer):
            cp.wait_send()
            cp.wait_recv()

    outs = pl.pallas_call(
        body, name=name,
        out_shape=tuple(pltpu.HBM(s.shape, s.dtype) for s in list(srcs) + list(lands)),
        in_specs=[HBM] * (2 * n) + [SEM, SEM, ANY], out_specs=tuple([HBM] * (2 * n)),
        input_output_aliases={i: i for i in range(2 * n)},
        compiler_params=pltpu.CompilerParams(has_side_effects=EFFECT),
    )(*srcs, *lands, send_sems, recv_sems, after)
    return outs[n:]


def _row(v, width=None):
    v = v.reshape(1, -1).astype(f32)
    if width is not None and v.shape[1] < width:
        v = jnp.pad(v, ((0, 0), (0, width - v.shape[1])))
    return v


def _layer_params(p, l):
    return dict(
        norm_mix=_row(p["norm_mix"][l]), norm_ffn=_row(p["norm_ffn"][l]), conv_w=p["conv_w"][l], conv_b=_row(p["conv_b"][l]),
        ssd=(_row(p["dt_bias"][l], LANES), _row(p["a_log"][l], LANES), _row(jnp.repeat(p["d_skip"][l], HEAD_DIM)), _row(p["ssm_norm"][l])))


def _layer_fwd(h, w_in, rest, sp, tabs, l):
    tag = f"l{l}_"
    hn = _rmsnorm_fwd(h, sp["norm_mix"], tag + "norm_mix")
    qkv = _matmul(hn, w_in, mode="nn", n_out=QKV_WIDTH, tn=256, b_off=0, name=tag + "proj_qkv")
    z = _matmul(hn, w_in, mode="nn", n_out=SSM_INNER, tn=256, b_off=Z_OFF // 256, name=tag + "proj_z")
    xbc_pre = _matmul(hn, w_in, mode="nn", n_out=CONV_CH, tn=256, b_off=XBC_OFF // 256, name=tag + "proj_xbc")
    dtp = _matmul(hn, w_in, mode="nn", n_out=LANES, tn=LANES, b_off=DT_OFF // LANES, name=tag + "proj_dt")
    o, lse = _attn_fwd(qkv, tabs, tag + "attn_fwd")
    xbc = _conv_fwd(xbc_pre, sp["conv_w"], sp["conv_b"], tag + "conv_fwd")
    yn, y, hs = _ssd_fwd(xbc, z, dtp, sp["ssd"], tag + "ssd_fwd")
    w_out, w_gate, w_up, w_down = rest(yn) if callable(rest) else rest
    t1 = _matmul(o, w_out, mode="nn", k_len=ATTN_WIDTH, tk=512, add=h, name=tag + "out_attn")
    h2 = _matmul(yn, w_out, mode="nn", k_len=SSM_INNER, tk=512, b_koff=1, add=t1, name=tag + "out_ssm")
    hn2 = _rmsnorm_fwd(h2, sp["norm_ffn"], tag + "norm_ffn")
    g, u, act = _swiglu_fwd(hn2, w_gate, w_up, tag + "ffn_up")
    h3 = _matmul(act, w_down, mode="nn", tk=1408, add=h2, name=tag + "ffn_down")
    saved = dict(h=h, hn=hn, qkv=qkv, z=z, xbc_pre=xbc_pre, dtp=dtp, o=o, lse=lse, xbc=xbc, yn=yn, y=y, hs=hs, h2=h2, hn2=hn2, g=g, u=u, act=act,
                 rest=(w_out, w_gate, w_up, w_down))
    return h3, saved


def _layer_bwd(dh3, s, big, sp, tabs, l, gd=f32, after_ffn=None):
    tag = f"l{l}_"
    w_in, w_out, w_gate, w_up, w_down = big
    dg, du = _swiglu_bwd(dh3, w_down, s["g"], s["u"], tag + "ffn_down_bwd")
    dw_down = _matmul(s["act"], dh3, mode="tn", tm=1408, tn=512, tk=2048, out_dtype=gd, name=tag + "dw_down")
    dhn2 = _matmul(dg, w_gate, mode="nt", tk=1408, name=tag + "ffn_gate_bwd")
    dhn2 = _matmul(du, w_up, mode="nt", tk=1408, add=dhn2, name=tag + "ffn_up_bwd")
    dw_gate = _matmul(s["hn2"], dg, mode="tn", tm=512, tn=1408, tk=2048, out_dtype=gd, name=tag + "dw_gate")
    dw_up = _matmul(s["hn2"], du, mode="tn", tm=512, tn=1408, tk=2048, out_dtype=gd, name=tag + "dw_up")
    norm_ffn = sp["norm_ffn"] if after_ffn is None else sp["norm_ffn"] + after_ffn(dict(w_gate=dw_gate, w_up=dw_up, w_down=dw_down))
    dh2, dnf = _rmsnorm_bwd(dhn2, s["h2"], norm_ffn, dh3, tag + "norm_ffn_bwd")
    d_o = _matmul(dh2, w_out, mode="nt", n_out=ATTN_WIDTH, tn=512, b_off=0, name=tag + "out_attn_bwd")
    dyn = _matmul(dh2, w_out, mode="nt", n_out=SSM_INNER, tn=512, b_off=1, name=tag + "out_ssm_bwd")
    dw_out = jnp.concatenate([_matmul(s["o"], dh2, mode="tn", tm=512, tn=512, tk=2048, out_dtype=gd, name=tag + "dw_out_attn"),
                              _matmul(s["yn"], dh2, mode="tn", tm=512, tn=512, tk=2048, out_dtype=gd, name=tag + "dw_out_ssm")], axis=0)
    dxbc, dz, ddtp, dnw, dds, dal, dbi = _ssd_bwd(s["xbc"], s["z"], s["dtp"], s["y"], s["hs"], dyn, sp["ssd"], tag + "ssd_bwd")
    dxbc_pre, dconv_w, dconv_b = _conv_bwd(s["xbc_pre"], sp["conv_w"], sp["conv_b"], dxbc, tag + "conv_bwd")
    dq, dk, dv = _attn_bwd(s["qkv"], tabs, s["o"], s["lse"], d_o, tag + "attn_bwd")
    dproj = jnp.concatenate([dq.astype(bf16), dk.astype(bf16), dv.astype(bf16), dz.astype(bf16), dxbc_pre.astype(bf16), ddtp.astype(bf16)], axis=1)
    dhn = _matmul(dproj, w_in, mode="nt", tk=1152, name=tag + "proj_bwd")
    dw_in = _matmul(s["hn"], dproj, mode="tn", tm=512, tn=1152, tk=2048, out_dtype=gd, name=tag + "dw_in")
    dh, dnm = _rmsnorm_bwd(dhn, s["h"], sp["norm_mix"], dh2, tag + "norm_mix_bwd")
    grads = dict(
        norm_mix=dnm.sum(0), w_in=dw_in, conv_w=dconv_w, conv_b=dconv_b[0], dt_bias=dbi.sum(0)[:SSM_HEADS], a_log=dal.sum(0)[:SSM_HEADS],
        d_skip=dds.sum(0).reshape(SSM_HEADS, HEAD_DIM).sum(1), ssm_norm=dnw.sum(0), w_out=dw_out, norm_ffn=dnf.sum(0),
        w_gate=dw_gate, w_up=dw_up, w_down=dw_down)
    return dh, grads


def _local_step(x, positions, target, p, bigs):
    tabs = _rope_tables(positions.reshape(-1, 1), "rope_tables")
    h = x
    saved, sps = [], []
    for l in range(DEPTH):
        sps.append(_layer_params(p, l))
        h, s = _layer_fwd(h, bigs[l][0], bigs[l][1:], sps[l], tabs, l)
        saved.append(s)
    dh, loss_parts, dfn = _final_loss(h, _row(p["final_norm"]), target, "final_loss")
    layer_grads = [None] * DEPTH
    for l in reversed(range(DEPTH)):
        dh, layer_grads[l] = _layer_bwd(dh, saved[l], bigs[l], sps[l], tabs, l)
    grads = {k: [layer_grads[l][k] for l in range(DEPTH)] for k in layer_grads[0]}
    grads["final_norm"] = dfn.sum(0)
    return jnp.sum(loss_parts), dh, grads


BIG = ("w_in", "w_out", "w_gate", "w_up", "w_down")
REST = BIG[1:]
FFN = ("w_gate", "w_up", "w_down")
MIX = ("w_in", "w_out")
COL_SHARDED = ("w_in", "w_gate", "w_up")
SMALL = ("norm_mix", "conv_b", "dt_bias", "a_log", "d_skip", "ssm_norm", "norm_ffn", "final_norm")
WEIGHTS = ("norm_mix", "w_in", "conv_w", "conv_b", "dt_bias", "a_log", "d_skip", "ssm_norm", "w_out", "norm_ffn", "w_gate", "w_up", "w_down", "final_norm")
PACK_W = 1024
SMALL_ROWS = 88
CONVW_ROWS = 96
CONVW_SHARD_ROWS = 16


def _full_from_gathered(name, g, l):
    _, a, b = g.shape
    if name in COL_SHARDED:
        width = IN_PROJ_PAD if name == "w_in" else N_DEV * b
        return _cols_from_devices(g.reshape(N_DEV, 1, a, b), width, f"cols_l{l}_{name}").reshape(a, width)
    return g.reshape(N_DEV * a, b)


def _by_device(name, full, shard_shape, l):
    a, b = shard_shape
    if name in COL_SHARDED:
        return _devices_from_cols([full], b, f"devs_l{l}_{name}").reshape(N_CHIP, 2, a, b)
    return full.reshape(N_CHIP, 2, a, b)


def _pack_rows(parts, rows, width):
    flat = jnp.concatenate([q.reshape(-1) for q in parts])
    return jnp.pad(flat, (0, rows * width - flat.shape[0])).reshape(rows, width)


def _unpack(flat, like):
    out, off = [], 0
    for q in like:
        out.append(flat[off:off + q.size].reshape(q.shape))
        off += q.size
    return out


def kernel(x, positions, norm_mix, w_in, conv_w, conv_b, dt_bias, a_log, d_skip, ssm_norm, w_out, norm_ffn, w_gate, w_up, w_down, final_norm, loss_target, m_norm_mix, m_w_in, m_conv_w, m_conv_b, m_dt_bias, m_a_log, m_d_skip, m_ssm_norm, m_w_out, m_norm_ffn, m_w_gate, m_w_up, m_w_down, m_final_norm, v_norm_mix, v_w_in, v_conv_w, v_conv_b, v_dt_bias, v_a_log, v_d_skip, v_ssm_norm, v_w_out, v_norm_ffn, v_w_gate, v_w_up, v_w_down, v_final_norm):
    w = dict(norm_mix=norm_mix, w_in=w_in, conv_w=conv_w, conv_b=conv_b, dt_bias=dt_bias, a_log=a_log, d_skip=d_skip, ssm_norm=ssm_norm,
             w_out=w_out, norm_ffn=norm_ffn, w_gate=w_gate, w_up=w_up, w_down=w_down, final_norm=final_norm)
    m = dict(norm_mix=m_norm_mix, w_in=m_w_in, conv_w=m_conv_w, conv_b=m_conv_b, dt_bias=m_dt_bias, a_log=m_a_log, d_skip=m_d_skip,
             ssm_norm=m_ssm_norm, w_out=m_w_out, norm_ffn=m_norm_ffn, w_gate=m_w_gate, w_up=m_w_up, w_down=m_w_down, final_norm=m_final_norm)
    v = dict(norm_mix=v_norm_mix, w_in=v_w_in, conv_w=v_conv_w, conv_b=v_conv_b, dt_bias=v_dt_bias, a_log=v_a_log, d_skip=v_d_skip,
             ssm_norm=v_ssm_norm, w_out=v_w_out, norm_ffn=v_norm_ffn, w_gate=v_w_gate, w_up=v_w_up, w_down=v_w_down, final_norm=v_final_norm)
    ax, ay, ac = lax.axis_index("x"), lax.axis_index("y"), lax.axis_index("c")
    dev = 4 * ax + 2 * ay + ac

    assert DEPTH == 2
    t = x.shape[0] * x.shape[1]
    xf, target = x.reshape(t, D_MODEL), loss_target.reshape(t, D_MODEL)

    def own_slot(block):
        return lax.dynamic_update_slice(jnp.zeros((N_DEV,) + block.shape[1:], block.dtype), block, (dev,) + (0,) * (block.ndim - 1))

    def gather_start(keys, l, tie, name):
        shards = [(w[keys[0]][l] + tie).astype(bf16)] + [w[k][l].astype(bf16) for k in keys[1:]]
        return _direct_start(shards, [own_slot(s[None]) for s in shards], False, name)

    def scatter_start(keys, grads_l, l, name):
        by_dev = [_by_device(k, grads_l[k], w[k].shape[1:], l).reshape((N_DEV,) + w[k].shape[1:]) for k in keys]
        return _direct_start(by_dev, [own_slot(lax.dynamic_slice_in_dim(g, dev, 1, 0)) for g in by_dev], True, name)

    (g_in0, conv_all), tie = _allgather_two_level([w["w_in"][0].astype(bf16), w["conv_w"]], "gather_l0_w_in")
    rest0_copy = gather_start(REST, 0, tie[0, 0], "gather_l0_rest_start")
    l1_copy = gather_start(BIG, 1, rest0_copy[4][0, 0], "gather_l1_start")
    p = {k: w[k] for k in SMALL}
    p["norm_mix"] = p["norm_mix"] + l1_copy[4][0, 0]
    p["conv_w"] = jnp.transpose(conv_all, (1, 2, 0, 3)).reshape(DEPTH, CONV_WIDTH, CONV_CH)
    sp0, sp1 = _layer_params(p, 0), _layer_params(p, 1)

    def rest0(after):
        lands = _direct_wait(*rest0_copy[:4], after, False, "gather_l0_rest_wait")
        return tuple(_full_from_gathered(k, g, 0) for k, g in zip(REST, lands))

    tabs = _rope_tables(positions.reshape(t, 1), "rope_tables")
    w_in0 = _full_from_gathered("w_in", g_in0, 0)
    h1, saved0 = _layer_fwd(xf, w_in0, rest0, sp0, tabs, 0)
    lands1 = _direct_wait(*l1_copy[:4], h1, False, "gather_l1_wait")
    bigs1 = tuple(_full_from_gathered(k, g, 1) for k, g in zip(BIG, lands1))
    h2, saved1 = _layer_fwd(h1, bigs1[0], bigs1[1:], sp1, tabs, 1)
    dh, loss_parts, dfn = _final_loss(h2, _row(p["final_norm"]), target, "final_loss")
    loss_local = jnp.sum(loss_parts)

    dh, grads1 = _layer_bwd(dh, saved1, bigs1, sp1, tabs, 1, gd=bf16)
    l1_grads = scatter_start(BIG, grads1, 1, "scatter_l1_start")
    w_out0, w_gate0, w_up0, w_down0 = saved0["rest"]
    bigs0 = (w_in0, w_out0, w_gate0, w_up0, w_down0 + l1_grads[4][0, 0].astype(bf16))
    ffn0_grads = []

    def after_ffn(grads_ffn):
        ffn0_grads.append(scatter_start(FFN, grads_ffn, 0, "scatter_l0_ffn_start"))
        return ffn0_grads[0][4][0, 0]

    dx, grads0 = _layer_bwd(dh, saved0, bigs0, sp0, tabs, 0, gd=bf16, after_ffn=after_ffn)
    mix0_grads = scatter_start(MIX, grads0, 0, "scatter_l0_mix_start")
    landed = {(k, 1): g for k, g in zip(BIG, _direct_wait(*l1_grads[:4], dx, True, "scatter_l1_wait"))}
    landed.update({(k, 0): g for k, g in zip(FFN, _direct_wait(*ffn0_grads[0][:4], dx, True, "scatter_l0_ffn_wait"))})
    landed.update({(k, 0): g for k, g in zip(MIX, _direct_wait(*mix0_grads[:4], mix0_grads[4], True, "scatter_l0_mix_wait"))})
    out_g, out_d, out_m, out_v = {}, {}, {}, {}
    for k in BIG:
        res = [_adamw([(landed[k, l], i) for i in range(N_DEV)], w[k], m[k], v[k], f"adamw_l{l}_{k}", layer=l) for l in range(DEPTH)]
        for dst, r0, r1 in zip((out_g, out_d, out_m, out_v), res[0], res[1]):
            dst[k] = jnp.stack([r0, r1])
    grads = {k: [grads0[k], grads1[k]] for k in grads0 if k not in BIG}
    grads["final_norm"] = dfn.sum(0)

    small_like = [w[k] for k in SMALL]
    small_grads = [jnp.stack(grads[k]) if k != "final_norm" else grads[k] for k in SMALL]
    small_pack = jnp.concatenate([_pack_rows(small_grads, SMALL_ROWS, LANES), _pack_rows([jnp.stack(grads["conv_w"])], CONVW_ROWS, LANES)], axis=0)
    parts = _allgather_direct(small_pack, "gather_small_grads")
    g_s, d_s, m_s, v_s = _adamw(
        [(parts[i, :SMALL_ROWS], None) for i in range(N_DEV)], _pack_rows(small_like, SMALL_ROWS, LANES),
        _pack_rows([m[k] for k in SMALL], SMALL_ROWS, LANES), _pack_rows([v[k] for k in SMALL], SMALL_ROWS, LANES), "adamw_replicated")
    for dst, src in ((out_g, g_s), (out_d, d_s), (out_m, m_s), (out_v, v_s)):
        dst.update(zip(SMALL, _unpack(src.reshape(-1), small_like)))
    shard_w = conv_w.shape[-1]
    conv_parts = parts[:, SMALL_ROWS:].reshape(N_DEV, DEPTH, CONV_WIDTH, CONV_CH)
    conv_mine = lax.dynamic_slice_in_dim(conv_parts, dev * shard_w, shard_w, axis=3)
    g_c, d_c, m_c, v_c = _adamw(
        [(_pack_rows([conv_mine[i]], CONVW_SHARD_ROWS, LANES), None) for i in range(N_DEV)], _pack_rows([conv_w], CONVW_SHARD_ROWS, LANES),
        _pack_rows([m["conv_w"]], CONVW_SHARD_ROWS, LANES), _pack_rows([v["conv_w"]], CONVW_SHARD_ROWS, LANES), "adamw_conv_w")
    for dst, src in ((out_g, g_c), (out_d, d_c), (out_m, m_c), (out_v, v_c)):
        dst["conv_w"] = src.reshape(-1)[:conv_w.size].reshape(conv_w.shape)

    loss = lax.psum(loss_local, ("x", "y", "c"))
    return (loss, dx.reshape(x.shape), *[out_g[k] for k in WEIGHTS], *[out_d[k] for k in WEIGHTS],
            *[out_m[k] for k in WEIGHTS], *[out_v[k] for k in WEIGHTS])
```

```python
import functools
import math

import jax
import jax.numpy as jnp
import numpy as np
from jax import lax
from jax.experimental import pallas as pl
from jax.experimental.pallas import tpu as pltpu

f32 = jnp.float32
bf16 = jnp.bfloat16

D_MODEL = 1024
SEQ = 2048
DEPTH = 2
HEAD_DIM = 64
N_ATTN_HEADS = 8
N_KV_HEADS = 2
ATTN_WIDTH = 512
KV_WIDTH = 128
ROPE_DIM = 16
ROPE_THETA = 500000.0
DILATIONS = (1, 4, 16)
ATTN_BLOCK = 128
SSM_HEADS = 16
SSM_INNER = 1024
SSM_GROUPS = 2
D_STATE = 128
CONV_WIDTH = 4
CHUNK = 128
CONV_CH = 1536
MIX_WIDTH = 1536
QKV_WIDTH = ATTN_WIDTH + 2 * KV_WIDTH
Z_OFF = 768
XBC_OFF = 1792
DT_OFF = 3328
IN_PROJ = 3344
IN_PROJ_PAD = 3456
FFN_HIDDEN = 2816
EPS = 1e-5
N_DEV = 8
ADAM_LR = 0.001
ADAM_B1 = 0.9
ADAM_B2 = 0.999
ADAM_EPS = 1e-08
ADAM_WD = 0.01
ADAM_STEP = 10

LANES = 128
SUBLANES = 8
VMEM_LIMIT = 56 * 1024 * 1024

MESH = pl.DeviceIdType.MESH
ANY = pl.BlockSpec(memory_space=pl.ANY)


def _cparams(sem, vmem=None):
    return pltpu.CompilerParams(dimension_semantics=sem, vmem_limit_bytes=vmem or VMEM_LIMIT)


def _sigmoid(x):
    return 1.0 / (1.0 + jnp.exp(-x))


def _silu(x):
    return x * _sigmoid(x)


def _dsilu(x):
    s = _sigmoid(x)
    return s * (1.0 + x * (1.0 - s))


def _silu_and_grad(x):
    s = _sigmoid(x)
    return x * s, s * (1.0 + x * (1.0 - s))


def _softplus(x):
    return jnp.maximum(x, 0.0) + jnp.log(1.0 + jnp.exp(-jnp.abs(x)))


def _dot(a, b, dims, precision=None):
    return lax.dot_general(a, b, (dims, ((), ())), preferred_element_type=f32, precision=precision)


def _nn(a, b, precision=None):
    return _dot(a, b, ((1,), (0,)), precision)


def _nt(a, b):
    return _dot(a, b, ((1,), (1,)))


def _tn(a, b):
    return _dot(a, b, ((0,), (0,)))


def _rowsum8(t):
    n, w = t.shape
    return jnp.sum(t.reshape(n // SUBLANES, SUBLANES, w), axis=0)


def _matmul(a, b, *, mode, n_out=None, b_off=0, a_koff=0, b_koff=0, k_len=None, add=None, out_dtype=f32, tm=2048, tn=512, tk=1024, name):
    if mode == "tn":
        kdim_a, m = a.shape
    else:
        m, kdim_a = a.shape
    kk = k_len if k_len is not None else kdim_a
    n = n_out if n_out is not None else (b.shape[0] if mode == "nt" else b.shape[1])
    tm, tn, tk = min(tm, m), min(tn, n), min(tk, kk)
    assert m % tm == 0 and n % tn == 0 and kk % tk == 0, (name, m, n, kk, tm, tn, tk)
    nk = kk // tk
    if mode == "nn":
        a_spec = pl.BlockSpec((tm, tk), lambda i, j, k: (i, k + a_koff))
        b_spec = pl.BlockSpec((tk, tn), lambda i, j, k: (k + b_koff, j + b_off))
        dims = ((1,), (0,))
    elif mode == "nt":
        a_spec = pl.BlockSpec((tm, tk), lambda i, j, k: (i, k + a_koff))
        b_spec = pl.BlockSpec((tn, tk), lambda i, j, k: (j + b_off, k + b_koff))
        dims = ((1,), (1,))
    else:
        a_spec = pl.BlockSpec((tk, tm), lambda i, j, k: (k + a_koff, i))
        b_spec = pl.BlockSpec((tk, tn), lambda i, j, k: (k + b_koff, j + b_off))
        dims = ((0,), (0,))
    o_spec = pl.BlockSpec((tm, tn), lambda i, j, k: (i, j))
    has_add = add is not None

    def body(*refs):
        if has_add:
            a_ref, b_ref, add_ref, o_ref, acc_ref = refs
        else:
            a_ref, b_ref, o_ref, acc_ref = refs
        k = pl.program_id(2)
        part = _dot(a_ref[...].astype(bf16), b_ref[...].astype(bf16), dims)

        @pl.when(k == 0)
        def _():
            acc_ref[...] = part

        @pl.when(k > 0)
        def _():
            acc_ref[...] += part

        @pl.when(k == nk - 1)
        def _():
            r = acc_ref[...]
            if has_add:
                r = r + add_ref[...]
            o_ref[...] = r.astype(out_dtype)

    in_specs = [a_spec, b_spec] + ([o_spec] if has_add else [])
    args = (a, b) + ((add,) if has_add else ())
    return pl.pallas_call(
        body, name=name, grid=(m // tm, n // tn, nk), in_specs=in_specs, out_specs=o_spec,
        out_shape=jax.ShapeDtypeStruct((m, n), out_dtype), scratch_shapes=[pltpu.VMEM((tm, tn), f32)],
        compiler_params=_cparams(("parallel", "parallel", "arbitrary")),
    )(*args)


def _out_proj(o, yn, w_out, h, name, tm=2048, tn=512):
    m, kb = o.shape
    n = w_out.shape[1]
    n_y = yn.shape[1] // kb
    assert yn.shape[1] % kb == 0 and w_out.shape[0] == kb * (1 + n_y)

    def body(*refs):
        o_ref, y_refs, w_refs, h_ref, out_ref = refs[0], refs[1:1 + n_y], refs[1 + n_y:2 + 2 * n_y], refs[-2], refs[-1]
        acc = h_ref[...] + _nn(o_ref[...].astype(bf16), w_refs[0][...])
        for y_ref, w_ref in zip(y_refs, w_refs[1:]):
            acc = acc + _nn(y_ref[...], w_ref[...])
        out_ref[...] = acc

    res = pl.BlockSpec((tm, tn), lambda i, j: (i, j))

    def a_blk(c):
        return pl.BlockSpec((tm, kb), lambda i, j: (i, c))

    def w_blk(r):
        return pl.BlockSpec((kb, tn), lambda i, j: (r, j))

    return pl.pallas_call(
        body, name=name, grid=(m // tm, n // tn),
        in_specs=[a_blk(0)] + [a_blk(c) for c in range(n_y)] + [w_blk(r) for r in range(1 + n_y)] + [res],
        out_specs=res, out_shape=jax.ShapeDtypeStruct((m, n), f32), compiler_params=_cparams(("parallel", "parallel")),
    )(o, *[yn] * n_y, *[w_out] * (1 + n_y), h)


def _swiglu_fwd(hn, w_gate, w_up, name, tm=2048, tn=256):
    m, k = hn.shape
    n = w_gate.shape[1]

    def body(a_ref, wg_ref, wu_ref, g_ref, u_ref, act_ref):
        a = a_ref[...]
        g = _nn(a, wg_ref[...])
        u = _nn(a, wu_ref[...])
        g_ref[...] = g.astype(bf16)
        u_ref[...] = u.astype(bf16)
        act_ref[...] = (_silu(g) * u).astype(bf16)

    a_spec = pl.BlockSpec((tm, k), lambda i, j: (i, 0))
    w_spec = pl.BlockSpec((k, tn), lambda i, j: (0, j))
    o_spec = pl.BlockSpec((tm, tn), lambda i, j: (i, j))
    return pl.pallas_call(
        body, name=name, grid=(m // tm, n // tn), in_specs=[a_spec, w_spec, w_spec], out_specs=[o_spec, o_spec, o_spec],
        out_shape=[jax.ShapeDtypeStruct((m, n), bf16)] * 3,
        compiler_params=_cparams(("parallel", "parallel")),
    )(hn, w_gate, w_up)


def _swiglu_bwd(dh, w_down, g, u, name, tm=2048, tn=256):
    m, k = dh.shape
    n = w_down.shape[0]

    def body(a_ref, w_ref, g_ref, u_ref, dg_ref, du_ref):
        dact = _nt(a_ref[...].astype(bf16), w_ref[...])
        act, dact_dg = _silu_and_grad(g_ref[...].astype(f32))
        dg_ref[...] = (dact * u_ref[...].astype(f32) * dact_dg).astype(bf16)
        du_ref[...] = (dact * act).astype(bf16)

    a_spec = pl.BlockSpec((tm, k), lambda i, j: (i, 0))
    w_spec = pl.BlockSpec((tn, k), lambda i, j: (j, 0))
    o_spec = pl.BlockSpec((tm, tn), lambda i, j: (i, j))
    return pl.pallas_call(
        body, name=name, grid=(m // tm, n // tn), in_specs=[a_spec, w_spec, o_spec, o_spec], out_specs=[o_spec, o_spec],
        out_shape=[jax.ShapeDtypeStruct((m, n), bf16), jax.ShapeDtypeStruct((m, n), bf16)],
        compiler_params=_cparams(("parallel", "parallel")),
    )(dh, w_down, g, u)


def _rmsnorm_fwd(h, w, name, tm=512):
    m, d = h.shape

    def body(h_ref, w_ref, o_ref):
        x = h_ref[...]
        r = lax.rsqrt(jnp.mean(x * x, axis=-1, keepdims=True) + EPS)
        o_ref[...] = (x * r * w_ref[...]).astype(bf16)

    return pl.pallas_call(
        body, name=name, grid=(m // tm,),
        in_specs=[pl.BlockSpec((tm, d), lambda i: (i, 0)), pl.BlockSpec((1, d), lambda i: (0, 0))],
        out_specs=pl.BlockSpec((tm, d), lambda i: (i, 0)), out_shape=jax.ShapeDtypeStruct((m, d), bf16),
        compiler_params=_cparams(("parallel",)),
    )(h, w)


def _rmsnorm_bwd(dhn, h, w, dres, name, tm=512):
    m, d = h.shape

    def body(dhn_ref, h_ref, w_ref, dres_ref, dh_ref, dw_ref):
        x = h_ref[...]
        r = lax.rsqrt(jnp.mean(x * x, axis=-1, keepdims=True) + EPS)
        xhat = x * r
        dy = dhn_ref[...]
        gw = dy * w_ref[...]
        dh_ref[...] = dres_ref[...] + r * (gw - xhat * jnp.mean(gw * xhat, axis=-1, keepdims=True))
        part = _rowsum8(dy * xhat)

        @pl.when(pl.program_id(0) == 0)
        def _():
            dw_ref[...] = part

        @pl.when(pl.program_id(0) > 0)
        def _():
            dw_ref[...] += part

    row = pl.BlockSpec((tm, d), lambda i: (i, 0))
    return pl.pallas_call(
        body, name=name, grid=(m // tm,),
        in_specs=[row, row, pl.BlockSpec((1, d), lambda i: (0, 0)), row],
        out_specs=[row, pl.BlockSpec((SUBLANES, d), lambda i: (0, 0))],
        out_shape=[jax.ShapeDtypeStruct((m, d), f32), jax.ShapeDtypeStruct((SUBLANES, d), f32)],
        compiler_params=_cparams(("arbitrary",)),
    )(dhn, h, w, dres)


def _final_loss(h, w, target, name, tm=512):
    m, d = h.shape

    def body(h_ref, w_ref, t_ref, dh_ref, loss_ref, dw_ref):
        x = h_ref[...]
        r = lax.rsqrt(jnp.mean(x * x, axis=-1, keepdims=True) + EPS)
        xhat = x * r
        ww = w_ref[...]
        err = xhat * ww - t_ref[...]
        dy = err * (1.0 / d)
        gw = dy * ww
        dh_ref[...] = r * (gw - xhat * jnp.mean(gw * xhat, axis=-1, keepdims=True))
        lpart = _rowsum8(err * err) * (0.5 / d)
        wpart = _rowsum8(dy * xhat)

        @pl.when(pl.program_id(0) == 0)
        def _():
            loss_ref[...] = lpart
            dw_ref[...] = wpart

        @pl.when(pl.program_id(0) > 0)
        def _():
            loss_ref[...] += lpart
            dw_ref[...] += wpart

    row = pl.BlockSpec((tm, d), lambda i: (i, 0))
    acc = pl.BlockSpec((SUBLANES, d), lambda i: (0, 0))
    return pl.pallas_call(
        body, name=name, grid=(m // tm,),
        in_specs=[row, pl.BlockSpec((1, d), lambda i: (0, 0)), row], out_specs=[row, acc, acc],
        out_shape=[jax.ShapeDtypeStruct((m, d), f32), jax.ShapeDtypeStruct((SUBLANES, d), f32), jax.ShapeDtypeStruct((SUBLANES, d), f32)],
        compiler_params=_cparams(("arbitrary",)),
    )(h, w, target)


def _lane_tables():
    f = np.arange(LANES) % HEAD_DIM
    inv = ROPE_THETA ** (-jnp.arange(0, ROPE_DIM, 2, dtype=f32) / ROPE_DIM)
    invf = jnp.where(f < ROPE_DIM, inv[f % (ROPE_DIM // 2)], 0.0).astype(f32)
    return invf.reshape(1, LANES)


def _rope_tables(pos_col, name):
    t = pos_col.shape[0]
    tm = SEQ

    def body(p_ref, f_ref, c_ref, s1_ref, s2_ref):
        ang = p_ref[...].astype(f32) * f_ref[...]
        co, si = jnp.cos(ang), jnp.sin(ang)
        f = lax.broadcasted_iota(jnp.int32, (tm, LANES), 1) % HEAD_DIM
        c_ref[...] = jnp.where(f < ROPE_DIM, co, 1.0)
        s1_ref[...] = jnp.where(f < ROPE_DIM // 2, -si, 0.0)
        s2_ref[...] = jnp.where((f >= ROPE_DIM // 2) & (f < ROPE_DIM), si, 0.0)

    row = pl.BlockSpec((tm, LANES), lambda i: (i, 0))
    return pl.pallas_call(
        body, name=name, grid=(t // tm,),
        in_specs=[pl.BlockSpec((tm, 1), lambda i: (i, 0)), pl.BlockSpec((1, LANES), lambda i: (0, 0))],
        out_specs=[row, row, row], out_shape=[jax.ShapeDtypeStruct((t, LANES), f32)] * 3,
        compiler_params=_cparams(("parallel",)),
    )(pos_col, _lane_tables())


def _rot(x, c, s1, s2):
    return x * c + pltpu.roll(x, LANES - ROPE_DIM // 2, 1) * s1 + pltpu.roll(x, ROPE_DIM // 2, 1) * s2


def _rot_t(g, c, s1, s2):
    return g * c + pltpu.roll(g * s1, ROPE_DIM // 2, 1) + pltpu.roll(g * s2, LANES - ROPE_DIM // 2, 1)


def _dup_head(x, kvh, low):
    a = jnp.where(kvh == 0, x, pltpu.roll(x, HEAD_DIM, 1))
    return jnp.where(low, a, pltpu.roll(a, HEAD_DIM, 1))


def _deinterleave(src_ref, dst_ref, d, dtype):
    length = SEQ // d
    if d == 1:
        dst_ref[...] = src_ref[...].astype(dtype)
    else:
        for r in range(d):
            dst_ref[pl.ds(r * length, length), :] = src_ref[pl.ds(r, length, stride=d), :].astype(dtype)


def _interleave_store(src_ref, dst_ref, d, accumulate):
    length = SEQ // d
    if d == 1:
        if accumulate:
            dst_ref[...] += src_ref[...]
        else:
            dst_ref[...] = src_ref[...]
    else:
        for r in range(d):
            blk = src_ref[pl.ds(r * length, length), :]
            if accumulate:
                dst_ref[pl.ds(r, length, stride=d), :] = dst_ref[pl.ds(r, length, stride=d), :] + blk
            else:
                dst_ref[pl.ds(r, length, stride=d), :] = blk


def _attn_masks():
    qi = lax.broadcasted_iota(jnp.int32, (ATTN_BLOCK, ATTN_BLOCK), 0)
    ki = lax.broadcasted_iota(jnp.int32, (ATTN_BLOCK, ATTN_BLOCK), 1)
    low = lax.broadcasted_iota(jnp.int32, (ATTN_BLOCK, LANES), 1) < HEAD_DIM
    return ki <= qi, ki >= qi, low


NEG_INF = float("-inf")
ATTN_UNROLL = 4


def _attn_fwd(qkv, tabs, name):
    t = qkv.shape[0]
    nb = t // SEQ
    n_blk = SEQ // ATTN_BLOCK

    def body(q_ref, k_ref, v_ref, c_ref, s1_ref, s2_ref, o_ref, lse_ref,
             qr, kr, vr, qd, kd, vd, ob, lb, o0, o1, o2, l0, l1, l2, ss):
        kvh = pl.program_id(1) // 2
        cur_ok, prev_ok, low = _attn_masks()
        lowfull = lax.broadcasted_iota(jnp.int32, (SEQ, LANES), 1) < HEAD_DIM
        c, s1, s2 = c_ref[...], s1_ref[...], s2_ref[...]
        qr[...] = _rot(q_ref[...], c, s1, s2) * (HEAD_DIM ** -0.5)
        kr[...] = _dup_head(_rot(k_ref[...], c, s1, s2), kvh, lowfull)
        vr[...] = _dup_head(v_ref[...], kvh, lowfull)
        onat, lnat = (o0, o1, o2), (l0, l1, l2)
        for bi, d in enumerate(DILATIONS):
            _deinterleave(qr, qd, d, bf16)
            _deinterleave(kr, kd, d, bf16)
            _deinterleave(vr, vd, d, bf16)
            per_res = n_blk // d
            use_prev = per_res > 1

            def scores(n, carry):
                start = pl.multiple_of(n * ATTN_BLOCK, ATTN_BLOCK)
                has_prev = (n % per_res) != 0
                pstart = pl.multiple_of(jnp.maximum(n - 1, 0) * ATTN_BLOCK, ATTN_BLOCK)
                qb = qd[pl.ds(start, ATTN_BLOCK), :]
                kc = kd[pl.ds(start, ATTN_BLOCK), :]
                if use_prev:
                    kp = kd[pl.ds(pstart, ATTN_BLOCK), :]
                for a in range(2):
                    qa = jnp.where(low if a == 0 else ~low, qb, jnp.zeros_like(qb))
                    ss[2 * n + a, :, 0:ATTN_BLOCK] = jnp.where(cur_ok, _nt(qa, kc), NEG_INF)
                    if use_prev:
                        ss[2 * n + a, :, ATTN_BLOCK:2 * ATTN_BLOCK] = jnp.where(prev_ok & has_prev, _nt(qa, kp), NEG_INF)
                return carry

            def softmax_pv(n, carry):
                start = pl.multiple_of(n * ATTN_BLOCK, ATTN_BLOCK)
                pstart = pl.multiple_of(jnp.maximum(n - 1, 0) * ATTN_BLOCK, ATTN_BLOCK)
                vc = vd[pl.ds(start, ATTN_BLOCK), :]
                if use_prev:
                    vp = vd[pl.ds(pstart, ATTN_BLOCK), :]
                outs, lses = [], []
                for a in range(2):
                    sc = ss[2 * n + a, :, 0:ATTN_BLOCK]
                    if use_prev:
                        sp = ss[2 * n + a, :, ATTN_BLOCK:2 * ATTN_BLOCK]
                        m = jnp.max(jnp.maximum(sc, sp), axis=1, keepdims=True)
                        pc, pp = jnp.exp(sc - m), jnp.exp(sp - m)
                        den = jnp.sum(pc + pp, axis=1, keepdims=True)
                        acc = _nn(pc.astype(bf16), vc) + _nn(pp.astype(bf16), vp)
                    else:
                        m = jnp.max(sc, axis=1, keepdims=True)
                        pc = jnp.exp(sc - m)
                        den = jnp.sum(pc, axis=1, keepdims=True)
                        acc = _nn(pc.astype(bf16), vc)
                    outs.append(acc * (1.0 / den))
                    lses.append(m + jnp.log(den))
                ob[pl.ds(start, ATTN_BLOCK), :] = jnp.where(low, outs[0], outs[1])
                lb[pl.ds(start, ATTN_BLOCK), :] = jnp.where(low, lses[0], lses[1])
                return carry

            lax.fori_loop(0, n_blk, scores, 0, unroll=ATTN_UNROLL)
            lax.fori_loop(0, n_blk, softmax_pv, 0, unroll=ATTN_UNROLL)
            _interleave_store(ob, onat[bi], d, False)
            _interleave_store(lb, lnat[bi], d, False)
        la, lbb, lc = l0[...], l1[...], l2[...]
        lm = jnp.maximum(jnp.maximum(la, lbb), lc)
        wa, wb, wc = jnp.exp(la - lm), jnp.exp(lbb - lm), jnp.exp(lc - lm)
        ws = wa + wb + wc
        o_ref[...] = (wa * o0[...] + wb * o1[...] + wc * o2[...]) / ws
        lse_ref[...] = lm + jnp.log(ws)

    def col(jj):
        return pl.BlockSpec((SEQ, LANES), lambda b, j: (b, jj if jj is not None else j))

    tab = pl.BlockSpec((SEQ, LANES), lambda b, j: (b, 0))
    fs = pltpu.VMEM((SEQ, LANES), f32)
    hs = pltpu.VMEM((SEQ, LANES), bf16)
    return pl.pallas_call(
        body, name=name, grid=(nb, ATTN_WIDTH // LANES),
        in_specs=[col(None), col(ATTN_WIDTH // LANES), col(ATTN_WIDTH // LANES + 1), tab, tab, tab],
        out_specs=[col(None), col(None)],
        out_shape=[jax.ShapeDtypeStruct((t, ATTN_WIDTH), f32), jax.ShapeDtypeStruct((t, ATTN_WIDTH), f32)],
        scratch_shapes=[fs, fs, fs, hs, hs, hs, fs, fs, fs, fs, fs, fs, fs, fs, pltpu.VMEM((2 * n_blk, ATTN_BLOCK, 2 * ATTN_BLOCK), f32)],
        compiler_params=_cparams(("parallel", "parallel")),
    )(qkv, qkv, qkv, *tabs)


def _attn_bwd(qkv, tabs, o, lse, do, name):
    t = qkv.shape[0]
    nb = t // SEQ
    n_blk = SEQ // ATTN_BLOCK
    n_j = ATTN_WIDTH // LANES

    def body(q_ref, k_ref, v_ref, c_ref, s1_ref, s2_ref, o_ref, lse_ref, do_ref, dq_ref, dk_ref, dv_ref,
             qr, kr, vr, dl, qd, kd, vd, dod, lsd, dld, dqd, dkd, dvd, dqa, dka, dva, pb, dsb, dk_acc, dv_acc):
        j = pl.program_id(1)
        pb[2 * n_blk:2 * n_blk + 2] = jnp.zeros((2, ATTN_BLOCK, 2 * ATTN_BLOCK), bf16)
        dsb[2 * n_blk:2 * n_blk + 2] = jnp.zeros((2, ATTN_BLOCK, 2 * ATTN_BLOCK), bf16)
        kvh = j // 2
        cur_ok, prev_ok, low = _attn_masks()
        lowfull = lax.broadcasted_iota(jnp.int32, (SEQ, LANES), 1) < HEAD_DIM
        c, s1, s2 = c_ref[...], s1_ref[...], s2_ref[...]
        qr[...] = _rot(q_ref[...], c, s1, s2) * (HEAD_DIM ** -0.5)
        kr[...] = _dup_head(_rot(k_ref[...], c, s1, s2), kvh, lowfull)
        vr[...] = _dup_head(v_ref[...], kvh, lowfull)
        prod = do_ref[...] * o_ref[...]
        d_lo = jnp.sum(jnp.where(lowfull, prod, 0.0), axis=1, keepdims=True)
        d_hi = jnp.sum(jnp.where(lowfull, 0.0, prod), axis=1, keepdims=True)
        dl[...] = jnp.where(lowfull, d_lo, d_hi)
        dqa[...] = jnp.zeros_like(dqa)
        dka[...] = jnp.zeros_like(dka)
        dva[...] = jnp.zeros_like(dva)
        for d in DILATIONS:
            _deinterleave(qr, qd, d, bf16)
            _deinterleave(kr, kd, d, bf16)
            _deinterleave(vr, vd, d, bf16)
            _deinterleave(do_ref, dod, d, bf16)
            _deinterleave(lse_ref, lsd, d, f32)
            _deinterleave(dl, dld, d, f32)
            per_res = n_blk // d
            use_prev = per_res > 1
            curl, prevl = slice(0, ATTN_BLOCK), slice(ATTN_BLOCK, 2 * ATTN_BLOCK)

            def halves(x):
                zero = jnp.zeros_like(x)
                return jnp.where(low, x, zero), jnp.where(low, zero, x)

            def probs(n, carry):
                start = pl.multiple_of(n * ATTN_BLOCK, ATTN_BLOCK)
                has_prev = (n % per_res) != 0
                pstart = pl.multiple_of(jnp.maximum(n - 1, 0) * ATTN_BLOCK, ATTN_BLOCK)
                cur, prev = pl.ds(start, ATTN_BLOCK), pl.ds(pstart, ATTN_BLOCK)
                qas, doas = halves(qd[cur, :]), halves(dod[cur, :])
                kc, vc = kd[cur, :], vd[cur, :]
                if use_prev:
                    kp, vp = kd[prev, :], vd[prev, :]
                lsb, dlb = lsd[cur, :], dld[cur, :]
                for a in range(2):
                    ls = lsb[:, a * HEAD_DIM:a * HEAD_DIM + 1]
                    de = dlb[:, a * HEAD_DIM:a * HEAD_DIM + 1]
                    pc = jnp.exp(jnp.where(cur_ok, _nt(qas[a], kc), NEG_INF) - ls)
                    pb[2 * n + a, :, curl] = pc.astype(bf16)
                    dsb[2 * n + a, :, curl] = (pc * (_nt(doas[a], vc) - de)).astype(bf16)
                    if use_prev:
                        pp = jnp.exp(jnp.where(prev_ok & has_prev, _nt(qas[a], kp), NEG_INF) - ls)
                        pb[2 * n + a, :, prevl] = pp.astype(bf16)
                        dsb[2 * n + a, :, prevl] = (pp * (_nt(doas[a], vp) - de)).astype(bf16)
                return carry

            def grads(n, carry):
                start = pl.multiple_of(n * ATTN_BLOCK, ATTN_BLOCK)
                pstart = pl.multiple_of(jnp.maximum(n - 1, 0) * ATTN_BLOCK, ATTN_BLOCK)
                nstart = pl.multiple_of(jnp.minimum(n + 1, n_blk - 1) * ATTN_BLOCK, ATTN_BLOCK)
                cur, prev, nxt = pl.ds(start, ATTN_BLOCK), pl.ds(pstart, ATTN_BLOCK), pl.ds(nstart, ATTN_BLOCK)
                kc = kd[cur, :]
                dqs = [_nn(dsb[2 * n + a, :, curl], kc) for a in range(2)]
                q_rows, do_rows = list(halves(qd[cur, :])), list(halves(dod[cur, :]))
                ds_rows, p_rows = [dsb[2 * n + a, :, curl] for a in range(2)], [pb[2 * n + a, :, curl] for a in range(2)]
                if use_prev:
                    kp = kd[prev, :]
                    dqs = [dqs[a] + _nn(dsb[2 * n + a, :, prevl], kp) for a in range(2)]
                    q_rows += list(halves(qd[nxt, :]))
                    do_rows += list(halves(dod[nxt, :]))
                    ds_rows += [dsb[2 * n + 2 + a, :, prevl] for a in range(2)]
                    p_rows += [pb[2 * n + 2 + a, :, prevl] for a in range(2)]
                dqd[cur, :] = jnp.where(low, dqs[0], dqs[1])
                dkd[cur, :] = _tn(jnp.concatenate(ds_rows, axis=0), jnp.concatenate(q_rows, axis=0))
                dvd[cur, :] = _tn(jnp.concatenate(p_rows, axis=0), jnp.concatenate(do_rows, axis=0))
                return carry

            lax.fori_loop(0, n_blk, probs, 0, unroll=ATTN_UNROLL)
            lax.fori_loop(0, n_blk, grads, 0, unroll=ATTN_UNROLL)
            _interleave_store(dqd, dqa, d, True)
            _interleave_store(dkd, dka, d, True)
            _interleave_store(dvd, dva, d, True)
        dq_ref[...] = _rot_t(dqa[...] * (HEAD_DIM ** -0.5), c, s1, s2).astype(bf16)
        dkf = dka[...]
        dkf = _rot_t(dkf + pltpu.roll(dkf, HEAD_DIM, 1), c, s1, s2)
        dvf = dva[...]
        dvf = dvf + pltpu.roll(dvf, HEAD_DIM, 1)
        mine = (lax.broadcasted_iota(jnp.int32, (SEQ, LANES), 1) // HEAD_DIM) == kvh
        dkc_, dvc_ = jnp.where(mine, dkf, 0.0), jnp.where(mine, dvf, 0.0)

        @pl.when(j == 0)
        def _():
            dk_acc[...] = dkc_
            dv_acc[...] = dvc_

        @pl.when(j > 0)
        def _():
            dk_acc[...] += dkc_
            dv_acc[...] += dvc_

        @pl.when(j == n_j - 1)
        def _():
            dk_ref[...] = dk_acc[...].astype(bf16)
            dv_ref[...] = dv_acc[...].astype(bf16)

    def col(jj):
        return pl.BlockSpec((SEQ, LANES), lambda b, j: (b, jj if jj is not None else j))

    tab = pl.BlockSpec((SEQ, LANES), lambda b, j: (b, 0))
    fs = pltpu.VMEM((SEQ, LANES), f32)
    hs = pltpu.VMEM((SEQ, LANES), bf16)
    return pl.pallas_call(
        body, name=name, grid=(nb, n_j),
        in_specs=[col(None), col(n_j), col(n_j + 1), tab, tab, tab, col(None), col(None), col(None)],
        out_specs=[col(None), tab, tab],
        out_shape=[jax.ShapeDtypeStruct((t, ATTN_WIDTH), bf16), jax.ShapeDtypeStruct((t, LANES), bf16), jax.ShapeDtypeStruct((t, LANES), bf16)],
        scratch_shapes=[fs, fs, fs, fs, hs, hs, hs, hs, fs, fs, fs, fs, fs, fs, fs, fs,
                        pltpu.VMEM((2 * n_blk + 2, ATTN_BLOCK, 2 * ATTN_BLOCK), bf16), pltpu.VMEM((2 * n_blk + 2, ATTN_BLOCK, 2 * ATTN_BLOCK), bf16), fs, fs],
        compiler_params=_cparams(("parallel", "arbitrary")),
    )(qkv, qkv, qkv, *tabs, o, lse, do)


def _conv_pre(x, w_ref, b_ref, row):
    shifted = [x] + [jnp.where(row >= s, pltpu.roll(x, s, 0), 0.0) for s in range(1, CONV_WIDTH)]
    pre = b_ref[...] + w_ref[CONV_WIDTH - 1:CONV_WIDTH, :] * x
    for s in range(1, CONV_WIDTH):
        pre = pre + w_ref[CONV_WIDTH - 1 - s:CONV_WIDTH - s, :] * shifted[s]
    return pre, shifted


def _conv_fwd(x, w, b, name, tc=512):
    t, ch = x.shape

    def body(x_ref, w_ref, b_ref, o_ref):
        row = lax.broadcasted_iota(jnp.int32, (SEQ, tc), 0)
        pre, _ = _conv_pre(x_ref[...], w_ref, b_ref, row)
        o_ref[...] = _silu(pre)

    xs = pl.BlockSpec((SEQ, tc), lambda i, j: (i, j))
    return pl.pallas_call(
        body, name=name, grid=(t // SEQ, ch // tc),
        in_specs=[xs, pl.BlockSpec((CONV_WIDTH, tc), lambda i, j: (0, j)), pl.BlockSpec((1, tc), lambda i, j: (0, j))],
        out_specs=xs, out_shape=jax.ShapeDtypeStruct((t, ch), f32),
        compiler_params=_cparams(("parallel", "parallel")),
    )(x, w, b)


def _conv_bwd(x, w, b, dact, name, tc=512):
    t, ch = x.shape

    def body(x_ref, w_ref, b_ref, d_ref, dx_ref, dw_ref, db_ref):
        row = lax.broadcasted_iota(jnp.int32, (SEQ, tc), 0)
        pre, shifted = _conv_pre(x_ref[...], w_ref, b_ref, row)
        dpre = d_ref[...] * _dsilu(pre)
        dx = w_ref[CONV_WIDTH - 1:CONV_WIDTH, :] * dpre
        for s in range(1, CONV_WIDTH):
            dx = dx + w_ref[CONV_WIDTH - 1 - s:CONV_WIDTH - s, :] * jnp.where(row < SEQ - s, pltpu.roll(dpre, SEQ - s, 0), 0.0)
        dx_ref[...] = dx.astype(bf16)
        first = pl.program_id(1) == 0
        parts = [jnp.sum(dpre * shifted[CONV_WIDTH - 1 - k], axis=0, keepdims=True) for k in range(CONV_WIDTH)]
        dbp = jnp.sum(dpre, axis=0, keepdims=True)

        @pl.when(first)
        def _():
            for k in range(CONV_WIDTH):
                dw_ref[k:k + 1, :] = parts[k]
            db_ref[...] = dbp

        @pl.when(jnp.logical_not(first))
        def _():
            for k in range(CONV_WIDTH):
                dw_ref[k:k + 1, :] += parts[k]
            db_ref[...] += dbp

    xs = pl.BlockSpec((SEQ, tc), lambda j, i: (i, j))
    ws = pl.BlockSpec((CONV_WIDTH, tc), lambda j, i: (0, j))
    bs = pl.BlockSpec((1, tc), lambda j, i: (0, j))
    return pl.pallas_call(
        body, name=name, grid=(ch // tc, t // SEQ),
        in_specs=[xs, ws, bs, xs], out_specs=[xs, ws, bs],
        out_shape=[jax.ShapeDtypeStruct((t, ch), bf16), jax.ShapeDtypeStruct((CONV_WIDTH, ch), f32), jax.ShapeDtypeStruct((1, ch), f32)],
        compiler_params=_cparams(("parallel", "arbitrary")),
    )(x, w, b, dact)


GROUP_W = SSM_INNER // SSM_GROUPS
HEADS_PER_GROUP = SSM_HEADS // SSM_GROUPS


def _split3(x):
    hi = x.astype(bf16)
    r1 = x - hi.astype(f32)
    mid = r1.astype(bf16)
    lo = (r1 - mid.astype(f32)).astype(bf16)
    return hi, mid, lo


def _dot_exact(x, sel, dims, x_is_lhs=True):
    parts = _split3(x)
    if x_is_lhs:
        return _dot(parts[0], sel, dims) + _dot(parts[1], sel, dims) + _dot(parts[2], sel, dims)
    return _dot(sel, parts[0], dims) + _dot(sel, parts[1], dims) + _dot(sel, parts[2], dims)


def _ssd_common(xbc_ref, dt_ref, bias_ref, alog_ref):
    r = lax.broadcasted_iota(jnp.int32, (CHUNK, CHUNK), 0)
    cidx = lax.broadcasted_iota(jnp.int32, (CHUNK, CHUNK), 1)
    causal = r >= cidx
    tril = causal.astype(bf16)
    expand = (lax.broadcasted_iota(jnp.int32, (CHUNK, SSM_INNER), 0)
              == lax.broadcasted_iota(jnp.int32, (CHUNK, SSM_INNER), 1) // HEAD_DIM).astype(bf16)
    head_lane = cidx < SSM_HEADS
    dtp = dt_ref[...] + bias_ref[...]
    dt = jnp.where(head_lane, _softplus(dtp), 0.0)
    a_neg = -jnp.exp(alog_ref[...])
    a = dt * a_neg
    nn_dims = ((1,), (0,))
    cs = _dot_exact(a, tril, nn_dims, x_is_lhs=False)
    dt_e = _dot_exact(dt, expand, nn_dims)
    cs_e = _dot_exact(cs, expand, nn_dims)
    xs = xbc_ref[:, 0:SSM_INNER]
    xg = xs * dt_e
    ecs = jnp.exp(cs_e)
    cs_last = cs_e[CHUNK - 1:CHUNK, :]
    dse = jnp.exp(cs_last - cs_e)
    cde = jnp.exp(cs_last)
    return dict(r=r, cidx=cidx, causal=causal, tril=tril, expand=expand, head_lane=head_lane, dtp=dtp, dt=dt, a_neg=a_neg,
                cs=cs, cst=cs.T, dt_e=dt_e, cs_e=cs_e, xs=xs, xg=xg, ecs=ecs, dse=dse, cde=cde)


def _decay_mat(q, h):
    return jnp.exp(jnp.where(q["causal"], q["cs"][:, h:h + 1] - q["cst"][h:h + 1, :], NEG_INF))


def _gate_norm(y, z, nw, gate=None):
    y2 = y * (_silu(z) if gate is None else gate)
    outs, xhats, rs = [], [], []
    for g in range(SSM_GROUPS):
        sl = slice(g * GROUP_W, (g + 1) * GROUP_W)
        yg = y2[:, sl]
        r = lax.rsqrt(jnp.mean(yg * yg, axis=-1, keepdims=True) + EPS)
        xhats.append(yg * r)
        rs.append(r)
        outs.append(yg * r * nw[:, sl])
    return y2, outs, xhats, rs


def _ssd_fwd(xbc, z, dtp, params, name):
    t = xbc.shape[0]
    n_chunk = SEQ // CHUNK
    low = None

    def body(xbc_ref, z_ref, dt_ref, bias_ref, alog_ref, dskip_ref, nw_ref, yn_ref, y_ref, hs_ref, h_scr):
        @pl.when(pl.program_id(1) == 0)
        def _():
            h_scr[...] = jnp.zeros_like(h_scr)

        q = _ssd_common(xbc_ref, dt_ref, bias_ref, alog_ref)
        low = lax.broadcasted_iota(jnp.int32, (CHUNK, LANES), 1) < HEAD_DIM
        xgb = q["xg"].astype(bf16)
        wst = (q["xg"] * q["dse"]).astype(bf16)
        hs_ref[0] = h_scr[...]
        ys = []
        for g in range(SSM_GROUPS):
            gl = slice(g * GROUP_W, (g + 1) * GROUP_W)
            bg = xbc_ref[:, SSM_INNER + g * D_STATE:SSM_INNER + (g + 1) * D_STATE].astype(bf16)
            cg = xbc_ref[:, SSM_INNER + SSM_GROUPS * D_STATE + g * D_STATE:SSM_INNER + SSM_GROUPS * D_STATE + (g + 1) * D_STATE].astype(bf16)
            cb = _nt(cg, bg)
            hg = h_scr[g]
            yoff = _nn(cg, hg.astype(bf16)) * q["ecs"][:, gl]
            pieces = []
            for i in range(HEADS_PER_GROUP // 2):
                h0 = g * HEADS_PER_GROUP + 2 * i
                xp = xgb[:, h0 * HEAD_DIM:(h0 + 2) * HEAD_DIM]
                m0 = (cb * _decay_mat(q, h0)).astype(bf16)
                m1 = (cb * _decay_mat(q, h0 + 1)).astype(bf16)
                zero = jnp.zeros_like(xp)
                pieces.append(_nn(m0, jnp.where(low, xp, zero)) + _nn(m1, jnp.where(low, zero, xp)))
            ys.append(jnp.concatenate(pieces, axis=1) + yoff + dskip_ref[:, gl] * q["xs"][:, gl])
            h_scr[g] = hg * q["cde"][:, gl] + _tn(bg, wst[:, gl])
        y = jnp.concatenate(ys, axis=1)
        y_ref[...] = y
        _, outs, _, _ = _gate_norm(y, z_ref[...], nw_ref[...])
        yn_ref[...] = jnp.concatenate(outs, axis=1).astype(bf16)

    def rows(w):
        return pl.BlockSpec((CHUNK, w), lambda b, c: (b * n_chunk + c, 0))

    def par(w):
        return pl.BlockSpec((1, w), lambda b, c: (0, 0))

    return pl.pallas_call(
        body, name=name, grid=(t // SEQ, n_chunk),
        in_specs=[rows(CONV_CH), rows(SSM_INNER), rows(LANES), par(LANES), par(LANES), par(SSM_INNER), par(SSM_INNER)],
        out_specs=[rows(SSM_INNER), rows(SSM_INNER), pl.BlockSpec((1, SSM_GROUPS, D_STATE, GROUP_W), lambda b, c: (b * n_chunk + c, 0, 0, 0))],
        out_shape=[jax.ShapeDtypeStruct((t, SSM_INNER), bf16), jax.ShapeDtypeStruct((t, SSM_INNER), f32),
                   jax.ShapeDtypeStruct((t // CHUNK, SSM_GROUPS, D_STATE, GROUP_W), f32)],
        scratch_shapes=[pltpu.VMEM((SSM_GROUPS, D_STATE, GROUP_W), f32)],
        compiler_params=_cparams(("parallel", "arbitrary")),
    )(xbc, z, dtp, *params)


def _ssd_bwd(xbc, z, dtp, y, hs, dyn, params, name):
    t = xbc.shape[0]
    n_chunk = SEQ // CHUNK

    def body(xbc_ref, z_ref, dt_ref, y_ref, hs_ref, dyn_ref, bias_ref, alog_ref, dskip_ref, nw_ref,
             dxbc_ref, dz_ref, ddt_ref, dnw_ref, dds_ref, dal_ref, dbi_ref, dh_scr):
        @pl.when(pl.program_id(1) == 0)
        def _():
            dh_scr[...] = jnp.zeros_like(dh_scr)

        q = _ssd_common(xbc_ref, dt_ref, bias_ref, alog_ref)
        low = lax.broadcasted_iota(jnp.int32, (CHUNK, LANES), 1) < HEAD_DIM
        last_row = lax.broadcasted_iota(jnp.int32, (CHUNK, GROUP_W), 0) == CHUNK - 1
        xs, xg = q["xs"], q["xg"]
        xgb = xg.astype(bf16)
        wf = xg * q["dse"]
        wst = wf.astype(bf16)
        zz = z_ref[...]
        yy = y_ref[...]
        sz, dsz = _silu_and_grad(zz)
        y2, _, xhats, rs = _gate_norm(yy, zz, nw_ref[...], gate=sz)
        dyn_ = dyn_ref[...]
        dy2s, dnws = [], []
        for g in range(SSM_GROUPS):
            gl = slice(g * GROUP_W, (g + 1) * GROUP_W)
            gw = dyn_[:, gl] * nw_ref[:, gl]
            dy2s.append(rs[g] * (gw - xhats[g] * jnp.mean(gw * xhats[g], axis=-1, keepdims=True)))
            dnws.append(_rowsum8(dyn_[:, gl] * xhats[g]))
        dy2 = jnp.concatenate(dy2s, axis=1)
        dy = dy2 * sz
        dz_ref[...] = (dy2 * yy * dsz).astype(bf16)
        dnw_p = jnp.concatenate(dnws, axis=1)
        dds_p = _rowsum8(dy * xs)
        dyb = dy.astype(bf16)
        gfull = (dy * q["ecs"]).astype(bf16)
        dcs_c = jnp.zeros((CHUNK, CHUNK), f32)
        dcs_r = jnp.zeros((CHUNK, CHUNK), f32)
        dcs_e_parts, dxg_parts = [], []
        for g in range(SSM_GROUPS):
            gl = slice(g * GROUP_W, (g + 1) * GROUP_W)
            bsl = slice(SSM_INNER + g * D_STATE, SSM_INNER + (g + 1) * D_STATE)
            csl = slice(SSM_INNER + SSM_GROUPS * D_STATE + g * D_STATE, SSM_INNER + SSM_GROUPS * D_STATE + (g + 1) * D_STATE)
            bg = xbc_ref[:, bsl].astype(bf16)
            cg = xbc_ref[:, csl].astype(bf16)
            cb = _nt(cg, bg)
            hg = hs_ref[0, g]
            hgb = hg.astype(bf16)
            dhn = dh_scr[g]
            dhnb = dhn.astype(bf16)
            yoff = _nn(cg, hgb) * q["ecs"][:, gl]
            dw_ = _nn(bg, dhnb)
            r_e = dw_ * wf[:, gl]
            to_last = jnp.sum(r_e, axis=0, keepdims=True) + jnp.sum(dhn * hg, axis=0, keepdims=True) * q["cde"][:, gl]
            dcs_e_parts.append(dy[:, gl] * yoff - r_e + jnp.where(last_row, to_last, 0.0))
            dcb = jnp.zeros((CHUNK, CHUNK), f32)
            dxg_pairs = []
            for i in range(HEADS_PER_GROUP // 2):
                h0 = g * HEADS_PER_GROUP + 2 * i
                psl = slice(h0 * HEAD_DIM, (h0 + 2) * HEAD_DIM)
                xp = xgb[:, psl]
                dyp = dyb[:, psl]
                zero = jnp.zeros_like(dyp)
                tns = []
                for a in range(2):
                    h = h0 + a
                    lm = _decay_mat(q, h)
                    m = cb * lm
                    dm = _nt(jnp.where(low, dyp, zero) if a == 0 else jnp.where(low, zero, dyp), xp)
                    dcb = dcb + dm * lm
                    nmat = dm * m
                    dcs_c = dcs_c + jnp.where(q["cidx"] == h, jnp.sum(nmat, axis=1, keepdims=True), 0.0)
                    dcs_r = dcs_r + jnp.where(q["r"] == h, jnp.sum(nmat, axis=0, keepdims=True), 0.0)
                    tns.append(_tn(m.astype(bf16), dyp))
                dxg_pairs.append(jnp.where(low, tns[0], tns[1]))
            dxg_parts.append(jnp.concatenate(dxg_pairs, axis=1) + dw_ * q["dse"][:, gl])
            dcbb = dcb.astype(bf16)
            dxbc_ref[:, csl] = _nt(gfull[:, gl], hgb) + _nn(dcbb, bg)
            dxbc_ref[:, bsl] = _nt(wst[:, gl], dhnb) + _tn(dcbb, cg)
            dh_scr[g] = dhn * q["cde"][:, gl] + _tn(cg, gfull[:, gl])
        dxg = jnp.concatenate(dxg_parts, axis=1)
        dcs_e = jnp.concatenate(dcs_e_parts, axis=1)
        dxbc_ref[:, 0:SSM_INNER] = dskip_ref[...] * dy + dxg * q["dt_e"]
        dcs = dcs_c - dcs_r.T + _dot_exact(dcs_e, q["expand"], ((1,), (1,)))
        triu = (q["cidx"] >= q["r"]).astype(bf16)
        da = _dot_exact(dcs, triu, ((1,), (0,)), x_is_lhs=False)
        ddt = _dot_exact(dxg * xs, q["expand"], ((1,), (1,))) + da * q["a_neg"]
        ddtp = jnp.where(q["head_lane"], ddt * _sigmoid(q["dtp"]), 0.0)
        ddt_ref[...] = ddtp.astype(bf16)
        dal_p = _rowsum8(da * q["dt"]) * q["a_neg"]
        dbi_p = _rowsum8(ddtp)
        first = (pl.program_id(0) == 0) & (pl.program_id(1) == 0)

        @pl.when(first)
        def _():
            dnw_ref[...] = dnw_p
            dds_ref[...] = dds_p
            dal_ref[...] = dal_p
            dbi_ref[...] = dbi_p

        @pl.when(jnp.logical_not(first))
        def _():
            dnw_ref[...] += dnw_p
            dds_ref[...] += dds_p
            dal_ref[...] += dal_p
            dbi_ref[...] += dbi_p

    def rows(w):
        return pl.BlockSpec((CHUNK, w), lambda b, c: (b * n_chunk + n_chunk - 1 - c, 0))

    def par(w):
        return pl.BlockSpec((1, w), lambda b, c: (0, 0))

    def acc(w):
        return pl.BlockSpec((SUBLANES, w), lambda b, c: (0, 0))

    return pl.pallas_call(
        body, name=name, grid=(t // SEQ, n_chunk),
        in_specs=[rows(CONV_CH), rows(SSM_INNER), rows(LANES), rows(SSM_INNER),
                  pl.BlockSpec((1, SSM_GROUPS, D_STATE, GROUP_W), lambda b, c: (b * n_chunk + n_chunk - 1 - c, 0, 0, 0)),
                  rows(SSM_INNER), par(LANES), par(LANES), par(SSM_INNER), par(SSM_INNER)],
        out_specs=[rows(CONV_CH), rows(SSM_INNER), rows(LANES), acc(SSM_INNER), acc(SSM_INNER), acc(LANES), acc(LANES)],
        out_shape=[jax.ShapeDtypeStruct((t, CONV_CH), f32), jax.ShapeDtypeStruct((t, SSM_INNER), bf16), jax.ShapeDtypeStruct((t, LANES), bf16),
                   jax.ShapeDtypeStruct((SUBLANES, SSM_INNER), f32), jax.ShapeDtypeStruct((SUBLANES, SSM_INNER), f32),
                   jax.ShapeDtypeStruct((SUBLANES, LANES), f32), jax.ShapeDtypeStruct((SUBLANES, LANES), f32)],
        scratch_shapes=[pltpu.VMEM((SSM_GROUPS, D_STATE, GROUP_W), f32)],
        compiler_params=_cparams(("arbitrary", "arbitrary")),
    )(xbc, z, dtp, y, hs, dyn, *params)


def _adamw(g_parts, w, m, v, name, layer=None):
    rows, width = w.shape[-2:]
    n = len(g_parts)
    tr = _row_tile(rows)

    def body(*refs):
        g_refs, (w_ref, m_ref, v_ref, g_out, d_out, m_out, v_out) = refs[:n], refs[n:]

        def part(i):
            return (g_refs[i][...] if g_parts[i][1] is None else g_refs[i][0]).astype(f32)

        def state(ref):
            return ref[...] if layer is None else ref[0]

        g = part(0)
        for i in range(1, n):
            g = g + part(i)
        mm = ADAM_B1 * state(m_ref) + (1.0 - ADAM_B1) * g
        vv = ADAM_B2 * state(v_ref) + (1.0 - ADAM_B2) * (g * g)
        m_hat = mm / (1.0 - ADAM_B1 ** ADAM_STEP)
        v_hat = vv / (1.0 - ADAM_B2 ** ADAM_STEP)
        g_out[...] = g
        d_out[...] = -ADAM_LR * (m_hat / (jnp.sqrt(v_hat) + ADAM_EPS) + ADAM_WD * state(w_ref))
        m_out[...] = mm
        v_out[...] = vv

    spec = pl.BlockSpec((tr, width), lambda i: (i, 0))

    def lead(idx):
        return spec if idx is None else pl.BlockSpec((1, tr, width), lambda i: (idx, i, 0))

    return pl.pallas_call(
        body, name=name, grid=(rows // tr,), in_specs=[lead(idx) for _, idx in g_parts] + [lead(layer)] * 3, out_specs=[spec] * 4,
        out_shape=[jax.ShapeDtypeStruct((rows, width), f32)] * 4, compiler_params=_cparams(("parallel",)),
    )(*[a for a, _ in g_parts], w, m, v)


def _row_tile(rows, cap=512):
    for cand in range(min(rows, cap) // SUBLANES * SUBLANES, 0, -SUBLANES):
        if rows % cand == 0:
            return cand
    return rows


def _cols_from_devices(g, width, name):
    n_dev, depth, a, b = g.shape

    def body(g_ref, o_ref):
        for i in range(n_dev):
            o_ref[0, :, i * b:(i + 1) * b] = g_ref[i, 0]
        if width > n_dev * b:
            o_ref[0, :, n_dev * b:width] = jnp.zeros((a, width - n_dev * b), o_ref.dtype)

    return pl.pallas_call(
        body, name=name, grid=(depth,), in_specs=[pl.BlockSpec((n_dev, 1, a, b), lambda l: (0, l, 0, 0))],
        out_specs=pl.BlockSpec((1, a, width), lambda l: (l, 0, 0)), out_shape=jax.ShapeDtypeStruct((depth, a, width), g.dtype),
        compiler_params=_cparams(("parallel",)),
    )(g)


def _devices_from_cols(per_layer, b, name, tr=256):
    depth = len(per_layer)
    a, width = per_layer[0].shape

    def body(*refs):
        o_ref = refs[depth]
        for l in range(depth):
            for i in range(N_DEV):
                o_ref[i, l] = refs[l][:, i * b:(i + 1) * b]

    return pl.pallas_call(
        body, name=name, grid=(a // tr,), in_specs=[pl.BlockSpec((tr, width), lambda r: (r, 0))] * depth,
        out_specs=pl.BlockSpec((N_DEV, depth, tr, b), lambda r: (0, 0, r, 0)),
        out_shape=jax.ShapeDtypeStruct((N_DEV, depth, a, b), per_layer[0].dtype), compiler_params=_cparams(("parallel",)),
    )(*per_layer)


def _me():
    return lax.axis_index("x"), lax.axis_index("y"), lax.axis_index("c")


def _allgather_two_level(shards, name):
    n = len(shards)
    per = 7

    def body(*refs):
        ins, outs, token = refs[:n], refs[n:2 * n], refs[2 * n]
        send_sems, recv_sems, local_sems = refs[2 * n + 1:]
        token[...] = jnp.zeros_like(token)
        x, y, c = _me()
        me, sibling = (x, y, c), (x, y, 1 - c)
        chips = [(1 - x, y), (x, 1 - y), (1 - x, 1 - y)]

        def slot(a, p):
            return outs[a].at[4 * p[0] + 2 * p[1] + p[2]]

        def copy(a, k, block, to, src=None):
            return pltpu.make_async_remote_copy(
                src_ref=slot(a, block) if src is None else src, dst_ref=slot(a, block),
                send_sem=send_sems.at[a * per + k], recv_sem=recv_sems.at[a * per + k], device_id=to, device_id_type=MESH)

        mine = [pltpu.make_async_copy(ins[a], slot(a, me), local_sems.at[a]) for a in range(n)]
        for cp in mine:
            cp.start()
        first = []
        for a in range(n):
            first.append(copy(a, 0, me, sibling, src=ins[a]))
            first += [copy(a, 1 + j, me, (*chip, c), src=ins[a]) for j, chip in enumerate(chips)]
        for cp in first:
            cp.start()
        passed = []
        for j, chip in enumerate(chips):
            for a in range(n):
                copy(a, 1 + j, (*chip, c), me).wait_recv()
                fwd = copy(a, 4 + j, (*chip, c), sibling)
                fwd.start()
                passed.append(fwd)
        for a in range(n):
            copy(a, 0, sibling, me).wait_recv()
            for j, chip in enumerate(chips):
                copy(a, 4 + j, (*chip, 1 - c), me).wait_recv()
        for cp in first + passed:
            cp.wait_send()
        for cp in mine:
            cp.wait()

    outs = pl.pallas_call(
        body, name=name, in_specs=[ANY] * n, out_specs=[ANY] * n + [pl.BlockSpec(memory_space=pltpu.VMEM)],
        out_shape=[jax.ShapeDtypeStruct((N_DEV,) + s.shape, s.dtype) for s in shards] + [jax.ShapeDtypeStruct((SUBLANES, LANES), f32)],
        scratch_shapes=[pltpu.SemaphoreType.DMA((n * per,)), pltpu.SemaphoreType.DMA((n * per,)), pltpu.SemaphoreType.DMA((n,))],
    )(*shards)
    return outs[:n], outs[n]


def _allgather_direct(row, name):
    def body(in_ref, out_ref, send_sems, recv_sems, local_sem):
        x, y, c = _me()
        mine = out_ref.at[4 * x + 2 * y + c]
        local = pltpu.make_async_copy(in_ref, mine, local_sem)
        local.start()
        sends = []
        for k in range(1, N_DEV):
            px, py, pc = x ^ (k >> 2), y ^ ((k >> 1) & 1), c ^ (k & 1)
            sends.append(pltpu.make_async_remote_copy(
                src_ref=in_ref, dst_ref=mine, send_sem=send_sems.at[k - 1], recv_sem=recv_sems.at[k - 1],
                device_id=(px, py, pc), device_id_type=MESH))
        for cp in sends:
            cp.start()
        for k in range(1, N_DEV):
            px, py, pc = x ^ (k >> 2), y ^ ((k >> 1) & 1), c ^ (k & 1)
            theirs = out_ref.at[4 * px + 2 * py + pc]
            pltpu.make_async_remote_copy(
                src_ref=in_ref, dst_ref=theirs, send_sem=send_sems.at[k - 1], recv_sem=recv_sems.at[k - 1],
                device_id=(px, py, pc), device_id_type=MESH).wait_recv()
        for cp in sends:
            cp.wait_send()
        local.wait()

    return pl.pallas_call(
        body, name=name, in_specs=[ANY], out_specs=ANY, out_shape=jax.ShapeDtypeStruct((N_DEV,) + row.shape, row.dtype),
        scratch_shapes=[pltpu.SemaphoreType.DMA((N_DEV - 1,)), pltpu.SemaphoreType.DMA((N_DEV - 1,)), pltpu.SemaphoreType.DMA],
    )(row)


N_CHIP = N_DEV // 2
HBM = pl.BlockSpec(memory_space=pltpu.HBM)
SEM = pl.BlockSpec(memory_space=pltpu.SEMAPHORE)
EFFECT = pltpu.SideEffectType.DATAFLOW_SIDE_EFFECTING


def _peer(k):
    x, y, c = _me()
    return x ^ (k >> 2), y ^ ((k >> 1) & 1), c ^ (k & 1)


def _direct_copies(srcs, lands, send_sems, recv_sems, per_peer):
    x, y, c = _me()
    me = 4 * x + 2 * y + c
    copies = []
    for a in range(len(srcs)):
        for k in range(1, N_DEV):
            px, py, pc = _peer(k)
            piece = srcs[a].at[4 * px + 2 * py + pc] if per_peer else srcs[a]
            copies.append(pltpu.make_async_remote_copy(
                src_ref=piece, dst_ref=lands[a].at[me], send_sem=send_sems.at[a * (N_DEV - 1) + k - 1],
                recv_sem=recv_sems.at[a * (N_DEV - 1) + k - 1], device_id=(px, py, pc), device_id_type=MESH))
    return copies


def _direct_start(srcs, lands, per_peer, name):
    n = len(srcs)
    n_sem = n * (N_DEV - 1)

    def body(*refs):
        src_refs, land_refs = refs[:n], refs[n:2 * n]
        send_sems, recv_sems = refs[2 * n], refs[2 * n + 1]
        token = refs[-1]
        for cp in _direct_copies(src_refs, land_refs, send_sems, recv_sems, per_peer):
            cp.start()
        token[...] = jnp.zeros_like(token)

    outs = pl.pallas_call(
        body, name=name,
        out_shape=(pltpu.SemaphoreType.DMA((n_sem,)), pltpu.SemaphoreType.DMA((n_sem,)),
                   *[pltpu.HBM(s.shape, s.dtype) for s in srcs], *[pltpu.HBM(s.shape, s.dtype) for s in lands],
                   jax.ShapeDtypeStruct((SUBLANES, LANES), f32)),
        in_specs=[HBM] * (2 * n), out_specs=(SEM, SEM, *[HBM] * (2 * n), pl.BlockSpec(memory_space=pltpu.VMEM)),
        input_output_aliases={i: 2 + i for i in range(2 * n)},
        compiler_params=pltpu.CompilerParams(has_side_effects=EFFECT),
    )(*[pltpu.with_memory_space_constraint(s, pltpu.HBM) for s in srcs], *[pltpu.with_memory_space_constraint(s, pltpu.HBM) for s in lands])
    return outs[0], outs[1], outs[2:2 + n], outs[2 + n:2 + 2 * n], outs[-1]


def _direct_wait(send_sems, recv_sems, srcs, lands, after, per_peer, name):
    n = len(srcs)

    def body(*refs):
        src_refs, land_refs = refs[:n], refs[n:2 * n]
        s_sems, r_sems = refs[2 * n], refs[2 * n + 1]
        for cp in _direct_copies(src_refs, land_refs, s_sems, r_sems, per_peer):
            cp.wait_send()
            cp.wait_recv()

    outs = pl.pallas_call(
        body, name=name,
        out_shape=tuple(pltpu.HBM(s.shape, s.dtype) for s in list(srcs) + list(lands)),
        in_specs=[HBM] * (2 * n) + [SEM, SEM, ANY], out_specs=tuple([HBM] * (2 * n)),
        input_output_aliases={i: i for i in range(2 * n)},
        compiler_params=pltpu.CompilerParams(has_side_effects=EFFECT),
    )(*srcs, *lands, send_sems, recv_sems, after)
    return outs[n:]


def _row(v, width=None):
    v = v.reshape(1, -1).astype(f32)
    if width is not None and v.shape[1] < width:
        v = jnp.pad(v, ((0, 0), (0, width - v.shape[1])))
    return v


def _layer_params(p, l):
    return dict(
        norm_mix=_row(p["norm_mix"][l]), norm_ffn=_row(p["norm_ffn"][l]), conv_w=p["conv_w"][l], conv_b=_row(p["conv_b"][l]),
        ssd=(_row(p["dt_bias"][l], LANES), _row(p["a_log"][l], LANES), _row(jnp.repeat(p["d_skip"][l], HEAD_DIM)), _row(p["ssm_norm"][l])))


def _layer_fwd(h, w_in, rest, sp, tabs, l):
    tag = f"l{l}_"
    hn = _rmsnorm_fwd(h, sp["norm_mix"], tag + "norm_mix")
    qkv = _matmul(hn, w_in, mode="nn", n_out=QKV_WIDTH, tn=256, b_off=0, name=tag + "proj_qkv")
    z = _matmul(hn, w_in, mode="nn", n_out=SSM_INNER, tn=256, b_off=Z_OFF // 256, name=tag + "proj_z")
    xbc_pre = _matmul(hn, w_in, mode="nn", n_out=CONV_CH, tn=256, b_off=XBC_OFF // 256, name=tag + "proj_xbc")
    dtp = _matmul(hn, w_in, mode="nn", n_out=LANES, tn=LANES, b_off=DT_OFF // LANES, name=tag + "proj_dt")
    o, lse = _attn_fwd(qkv, tabs, tag + "attn_fwd")
    xbc = _conv_fwd(xbc_pre, sp["conv_w"], sp["conv_b"], tag + "conv_fwd")
    yn, y, hs = _ssd_fwd(xbc, z, dtp, sp["ssd"], tag + "ssd_fwd")
    w_out, w_gate, w_up, w_down = rest(yn) if callable(rest) else rest
    h2 = _out_proj(o, yn, w_out, h, tag + "out_proj")
    hn2 = _rmsnorm_fwd(h2, sp["norm_ffn"], tag + "norm_ffn")
    g, u, act = _swiglu_fwd(hn2, w_gate, w_up, tag + "ffn_up")
    h3 = _matmul(act, w_down, mode="nn", tk=1408, add=h2, name=tag + "ffn_down")
    saved = dict(h=h, hn=hn, qkv=qkv, z=z, xbc_pre=xbc_pre, dtp=dtp, o=o, lse=lse, xbc=xbc, yn=yn, y=y, hs=hs, h2=h2, hn2=hn2, g=g, u=u, act=act,
                 rest=(w_out, w_gate, w_up, w_down))
    return h3, saved


def _layer_bwd(dh3, s, big, sp, tabs, l, gd=f32, after_ffn=None):
    tag = f"l{l}_"
    w_in, w_out, w_gate, w_up, w_down = big
    dg, du = _swiglu_bwd(dh3, w_down, s["g"], s["u"], tag + "ffn_down_bwd")
    dw_down = _matmul(s["act"], dh3, mode="tn", tm=1408, tn=512, tk=2048, out_dtype=gd, name=tag + "dw_down")
    dhn2 = _matmul(dg, w_gate, mode="nt", tk=1408, name=tag + "ffn_gate_bwd")
    dhn2 = _matmul(du, w_up, mode="nt", tk=1408, add=dhn2, name=tag + "ffn_up_bwd")
    dw_gate = _matmul(s["hn2"], dg, mode="tn", tm=512, tn=1408, tk=2048, out_dtype=gd, name=tag + "dw_gate")
    dw_up = _matmul(s["hn2"], du, mode="tn", tm=512, tn=1408, tk=2048, out_dtype=gd, name=tag + "dw_up")
    norm_ffn = sp["norm_ffn"] if after_ffn is None else sp["norm_ffn"] + after_ffn(dict(w_gate=dw_gate, w_up=dw_up, w_down=dw_down))
    dh2, dnf = _rmsnorm_bwd(dhn2, s["h2"], norm_ffn, dh3, tag + "norm_ffn_bwd")
    d_o = _matmul(dh2, w_out, mode="nt", n_out=ATTN_WIDTH, tn=512, b_off=0, name=tag + "out_attn_bwd")
    dyn = _matmul(dh2, w_out, mode="nt", n_out=SSM_INNER, tn=512, b_off=1, name=tag + "out_ssm_bwd")
    dw_out = jnp.concatenate([_matmul(s["o"], dh2, mode="tn", tm=512, tn=512, tk=2048, out_dtype=gd, name=tag + "dw_out_attn"),
                              _matmul(s["yn"], dh2, mode="tn", tm=512, tn=512, tk=2048, out_dtype=gd, name=tag + "dw_out_ssm")], axis=0)
    dxbc, dz, ddtp, dnw, dds, dal, dbi = _ssd_bwd(s["xbc"], s["z"], s["dtp"], s["y"], s["hs"], dyn, sp["ssd"], tag + "ssd_bwd")
    dxbc_pre, dconv_w, dconv_b = _conv_bwd(s["xbc_pre"], sp["conv_w"], sp["conv_b"], dxbc, tag + "conv_bwd")
    dq, dk, dv = _attn_bwd(s["qkv"], tabs, s["o"], s["lse"], d_o, tag + "attn_bwd")
    dproj = jnp.concatenate([dq, dk, dv, dz, dxbc_pre, ddtp], axis=1)
    dhn = _matmul(dproj, w_in, mode="nt", tk=1152, name=tag + "proj_bwd")
    dw_in = _matmul(s["hn"], dproj, mode="tn", tm=512, tn=1152, tk=2048, out_dtype=gd, name=tag + "dw_in")
    dh, dnm = _rmsnorm_bwd(dhn, s["h"], sp["norm_mix"], dh2, tag + "norm_mix_bwd")
    grads = dict(
        norm_mix=dnm.sum(0), w_in=dw_in, conv_w=dconv_w, conv_b=dconv_b[0], dt_bias=dbi.sum(0)[:SSM_HEADS], a_log=dal.sum(0)[:SSM_HEADS],
        d_skip=dds.sum(0).reshape(SSM_HEADS, HEAD_DIM).sum(1), ssm_norm=dnw.sum(0), w_out=dw_out, norm_ffn=dnf.sum(0),
        w_gate=dw_gate, w_up=dw_up, w_down=dw_down)
    return dh, grads


def _local_step(x, positions, target, p, bigs):
    tabs = _rope_tables(positions.reshape(-1, 1), "rope_tables")
    h = x
    saved, sps = [], []
    for l in range(DEPTH):
        sps.append(_layer_params(p, l))
        h, s = _layer_fwd(h, bigs[l][0], bigs[l][1:], sps[l], tabs, l)
        saved.append(s)
    dh, loss_parts, dfn = _final_loss(h, _row(p["final_norm"]), target, "final_loss")
    layer_grads = [None] * DEPTH
    for l in reversed(range(DEPTH)):
        dh, layer_grads[l] = _layer_bwd(dh, saved[l], bigs[l], sps[l], tabs, l)
    grads = {k: [layer_grads[l][k] for l in range(DEPTH)] for k in layer_grads[0]}
    grads["final_norm"] = dfn.sum(0)
    return jnp.sum(loss_parts), dh, grads


BIG = ("w_in", "w_out", "w_gate", "w_up", "w_down")
REST = BIG[1:]
FFN = ("w_gate", "w_up", "w_down")
MIX = ("w_in", "w_out")
COL_SHARDED = ("w_in", "w_gate", "w_up")
SMALL = ("norm_mix", "conv_b", "dt_bias", "a_log", "d_skip", "ssm_norm", "norm_ffn", "final_norm")
WEIGHTS = ("norm_mix", "w_in", "conv_w", "conv_b", "dt_bias", "a_log", "d_skip", "ssm_norm", "w_out", "norm_ffn", "w_gate", "w_up", "w_down", "final_norm")
PACK_W = 1024
SMALL_ROWS = 88
CONVW_ROWS = 96
CONVW_SHARD_ROWS = 16


def _full_from_gathered(name, g, l):
    _, a, b = g.shape
    if name in COL_SHARDED:
        width = IN_PROJ_PAD if name == "w_in" else N_DEV * b
        return _cols_from_devices(g.reshape(N_DEV, 1, a, b), width, f"cols_l{l}_{name}").reshape(a, width)
    return g.reshape(N_DEV * a, b)


def _by_device(name, full, shard_shape, l):
    a, b = shard_shape
    if name in COL_SHARDED:
        return _devices_from_cols([full], b, f"devs_l{l}_{name}").reshape(N_CHIP, 2, a, b)
    return full.reshape(N_CHIP, 2, a, b)


def _pack_rows(parts, rows, width):
    flat = jnp.concatenate([q.reshape(-1) for q in parts])
    return jnp.pad(flat, (0, rows * width - flat.shape[0])).reshape(rows, width)


def _unpack(flat, like):
    out, off = [], 0
    for q in like:
        out.append(flat[off:off + q.size].reshape(q.shape))
        off += q.size
    return out


def kernel(x, positions, norm_mix, w_in, conv_w, conv_b, dt_bias, a_log, d_skip, ssm_norm, w_out, norm_ffn, w_gate, w_up, w_down, final_norm, loss_target, m_norm_mix, m_w_in, m_conv_w, m_conv_b, m_dt_bias, m_a_log, m_d_skip, m_ssm_norm, m_w_out, m_norm_ffn, m_w_gate, m_w_up, m_w_down, m_final_norm, v_norm_mix, v_w_in, v_conv_w, v_conv_b, v_dt_bias, v_a_log, v_d_skip, v_ssm_norm, v_w_out, v_norm_ffn, v_w_gate, v_w_up, v_w_down, v_final_norm):
    w = dict(norm_mix=norm_mix, w_in=w_in, conv_w=conv_w, conv_b=conv_b, dt_bias=dt_bias, a_log=a_log, d_skip=d_skip, ssm_norm=ssm_norm,
             w_out=w_out, norm_ffn=norm_ffn, w_gate=w_gate, w_up=w_up, w_down=w_down, final_norm=final_norm)
    m = dict(norm_mix=m_norm_mix, w_in=m_w_in, conv_w=m_conv_w, conv_b=m_conv_b, dt_bias=m_dt_bias, a_log=m_a_log, d_skip=m_d_skip,
             ssm_norm=m_ssm_norm, w_out=m_w_out, norm_ffn=m_norm_ffn, w_gate=m_w_gate, w_up=m_w_up, w_down=m_w_down, final_norm=m_final_norm)
    v = dict(norm_mix=v_norm_mix, w_in=v_w_in, conv_w=v_conv_w, conv_b=v_conv_b, dt_bias=v_dt_bias, a_log=v_a_log, d_skip=v_d_skip,
             ssm_norm=v_ssm_norm, w_out=v_w_out, norm_ffn=v_norm_ffn, w_gate=v_w_gate, w_up=v_w_up, w_down=v_w_down, final_norm=v_final_norm)
    ax, ay, ac = lax.axis_index("x"), lax.axis_index("y"), lax.axis_index("c")
    dev = 4 * ax + 2 * ay + ac

    assert DEPTH == 2
    t = x.shape[0] * x.shape[1]
    xf, target = x.reshape(t, D_MODEL), loss_target.reshape(t, D_MODEL)

    def own_slot(block):
        return lax.dynamic_update_slice(jnp.zeros((N_DEV,) + block.shape[1:], block.dtype), block, (dev,) + (0,) * (block.ndim - 1))

    def gather_start(keys, l, tie, name):
        shards = [(w[keys[0]][l] + tie).astype(bf16)] + [w[k][l].astype(bf16) for k in keys[1:]]
        return _direct_start(shards, [own_slot(s[None]) for s in shards], False, name)

    def scatter_start(keys, grads_l, l, name):
        by_dev = [_by_device(k, grads_l[k], w[k].shape[1:], l).reshape((N_DEV,) + w[k].shape[1:]) for k in keys]
        return _direct_start(by_dev, [own_slot(lax.dynamic_slice_in_dim(g, dev, 1, 0)) for g in by_dev], True, name)

    (g_in0, conv_all), tie = _allgather_two_level([w["w_in"][0].astype(bf16), w["conv_w"]], "gather_l0_w_in")
    rest0_copy = gather_start(REST, 0, tie[0, 0], "gather_l0_rest_start")
    l1_copy = gather_start(BIG, 1, rest0_copy[4][0, 0], "gather_l1_start")
    p = {k: w[k] for k in SMALL}
    p["norm_mix"] = p["norm_mix"] + l1_copy[4][0, 0]
    p["conv_w"] = jnp.transpose(conv_all, (1, 2, 0, 3)).reshape(DEPTH, CONV_WIDTH, CONV_CH)
    sp0, sp1 = _layer_params(p, 0), _layer_params(p, 1)

    def rest0(after):
        lands = _direct_wait(*rest0_copy[:4], after, False, "gather_l0_rest_wait")
        return tuple(_full_from_gathered(k, g, 0) for k, g in zip(REST, lands))

    tabs = _rope_tables(positions.reshape(t, 1), "rope_tables")
    w_in0 = _full_from_gathered("w_in", g_in0, 0)
    h1, saved0 = _layer_fwd(xf, w_in0, rest0, sp0, tabs, 0)
    lands1 = _direct_wait(*l1_copy[:4], h1, False, "gather_l1_wait")
    bigs1 = tuple(_full_from_gathered(k, g, 1) for k, g in zip(BIG, lands1))
    h2, saved1 = _layer_fwd(h1, bigs1[0], bigs1[1:], sp1, tabs, 1)
    dh, loss_parts, dfn = _final_loss(h2, _row(p["final_norm"]), target, "final_loss")
    loss_local = jnp.sum(loss_parts)

    dh, grads1 = _layer_bwd(dh, saved1, bigs1, sp1, tabs, 1, gd=bf16)
    l1_grads = scatter_start(BIG, grads1, 1, "scatter_l1_start")
    w_out0, w_gate0, w_up0, w_down0 = saved0["rest"]
    bigs0 = (w_in0, w_out0, w_gate0, w_up0, w_down0 + l1_grads[4][0, 0].astype(bf16))
    ffn0_grads = []

    def after_ffn(grads_ffn):
        ffn0_grads.append(scatter_start(FFN, grads_ffn, 0, "scatter_l0_ffn_start"))
        return ffn0_grads[0][4][0, 0]

    dx, grads0 = _layer_bwd(dh, saved0, bigs0, sp0, tabs, 0, gd=bf16, after_ffn=after_ffn)
    mix0_grads = scatter_start(MIX, grads0, 0, "scatter_l0_mix_start")
    landed = {(k, 1): g for k, g in zip(BIG, _direct_wait(*l1_grads[:4], dx, True, "scatter_l1_wait"))}
    landed.update({(k, 0): g for k, g in zip(FFN, _direct_wait(*ffn0_grads[0][:4], dx, True, "scatter_l0_ffn_wait"))})
    landed.update({(k, 0): g for k, g in zip(MIX, _direct_wait(*mix0_grads[:4], mix0_grads[4], True, "scatter_l0_mix_wait"))})
    out_g, out_d, out_m, out_v = {}, {}, {}, {}
    for k in BIG:
        res = [_adamw([(landed[k, l], i) for i in range(N_DEV)], w[k], m[k], v[k], f"adamw_l{l}_{k}", layer=l) for l in range(DEPTH)]
        for dst, r0, r1 in zip((out_g, out_d, out_m, out_v), res[0], res[1]):
            dst[k] = jnp.stack([r0, r1])
    grads = {k: [grads0[k], grads1[k]] for k in grads0 if k not in BIG}
    grads["final_norm"] = dfn.sum(0)

    small_like = [w[k] for k in SMALL]
    small_grads = [jnp.stack(grads[k]) if k != "final_norm" else grads[k] for k in SMALL]
    small_pack = jnp.concatenate([_pack_rows(small_grads, SMALL_ROWS, LANES), _pack_rows([jnp.stack(grads["conv_w"])], CONVW_ROWS, LANES)], axis=0)
    parts = _allgather_direct(small_pack, "gather_small_grads")
    g_s, d_s, m_s, v_s = _adamw(
        [(parts[i, :SMALL_ROWS], None) for i in range(N_DEV)], _pack_rows(small_like, SMALL_ROWS, LANES),
        _pack_rows([m[k] for k in SMALL], SMALL_ROWS, LANES), _pack_rows([v[k] for k in SMALL], SMALL_ROWS, LANES), "adamw_replicated")
    for dst, src in ((out_g, g_s), (out_d, d_s), (out_m, m_s), (out_v, v_s)):
        dst.update(zip(SMALL, _unpack(src.reshape(-1), small_like)))
    shard_w = conv_w.shape[-1]
    conv_parts = parts[:, SMALL_ROWS:].reshape(N_DEV, DEPTH, CONV_WIDTH, CONV_CH)
    conv_mine = lax.dynamic_slice_in_dim(conv_parts, dev * shard_w, shard_w, axis=3)
    g_c, d_c, m_c, v_c = _adamw(
        [(_pack_rows([conv_mine[i]], CONVW_SHARD_ROWS, LANES), None) for i in range(N_DEV)], _pack_rows([conv_w], CONVW_SHARD_ROWS, LANES),
        _pack_rows([m["conv_w"]], CONVW_SHARD_ROWS, LANES), _pack_rows([v["conv_w"]], CONVW_SHARD_ROWS, LANES), "adamw_conv_w")
    for dst, src in ((out_g, g_c), (out_d, d_c), (out_m, m_c), (out_v, v_c)):
        dst["conv_w"] = src.reshape(-1)[:conv_w.size].reshape(conv_w.shape)

    loss = lax.psum(loss_local, ("x", "y", "c"))
    return (loss, dx.reshape(x.shape), *[out_g[k] for k in WEIGHTS], *[out_d[k] for k in WEIGHTS],
            *[out_m[k] for k in WEIGHTS], *[out_v[k] for k in WEIGHTS])
```

```python
import functools
import math

import jax
import jax.numpy as jnp
import numpy as np
from jax import lax
from jax.experimental import pallas as pl
from jax.experimental.pallas import tpu as pltpu

f32 = jnp.float32
bf16 = jnp.bfloat16

D_MODEL = 1024
SEQ = 2048
DEPTH = 2
HEAD_DIM = 64
N_ATTN_HEADS = 8
N_KV_HEADS = 2
ATTN_WIDTH = 512
KV_WIDTH = 128
ROPE_DIM = 16
ROPE_THETA = 500000.0
DILATIONS = (1, 4, 16)
ATTN_BLOCK = 128
SSM_HEADS = 16
SSM_INNER = 1024
SSM_GROUPS = 2
D_STATE = 128
CONV_WIDTH = 4
CHUNK = 128
CONV_CH = 1536
MIX_WIDTH = 1536
QKV_WIDTH = ATTN_WIDTH + 2 * KV_WIDTH
Z_OFF = 768
XBC_OFF = 1792
DT_OFF = 3328
IN_PROJ = 3344
IN_PROJ_PAD = 3456
FFN_HIDDEN = 2816
EPS = 1e-5
N_DEV = 8
ADAM_LR = 0.001
ADAM_B1 = 0.9
ADAM_B2 = 0.999
ADAM_EPS = 1e-08
ADAM_WD = 0.01
ADAM_STEP = 10

LANES = 128
SUBLANES = 8
VMEM_LIMIT = 56 * 1024 * 1024

MESH = pl.DeviceIdType.MESH
ANY = pl.BlockSpec(memory_space=pl.ANY)


def _cparams(sem, vmem=None):
    return pltpu.CompilerParams(dimension_semantics=sem, vmem_limit_bytes=vmem or VMEM_LIMIT)


def _sigmoid(x):
    return 1.0 / (1.0 + jnp.exp(-x))


def _silu(x):
    return x * _sigmoid(x)


def _dsilu(x):
    s = _sigmoid(x)
    return s * (1.0 + x * (1.0 - s))


def _silu_and_grad(x):
    s = _sigmoid(x)
    return x * s, s * (1.0 + x * (1.0 - s))


def _softplus(x):
    return jnp.maximum(x, 0.0) + jnp.log(1.0 + jnp.exp(-jnp.abs(x)))


def _dot(a, b, dims, precision=None):
    return lax.dot_general(a, b, (dims, ((), ())), preferred_element_type=f32, precision=precision)


def _nn(a, b, precision=None):
    return _dot(a, b, ((1,), (0,)), precision)


def _nt(a, b):
    return _dot(a, b, ((1,), (1,)))


def _tn(a, b):
    return _dot(a, b, ((0,), (0,)))


def _rowsum8(t):
    n, w = t.shape
    return jnp.sum(t.reshape(n // SUBLANES, SUBLANES, w), axis=0)


def _matmul(a, b, *, mode, n_out=None, b_off=0, a_koff=0, b_koff=0, k_len=None, add=None, out_dtype=f32, tm=2048, tn=512, tk=1024, name):
    if mode == "tn":
        kdim_a, m = a.shape
    else:
        m, kdim_a = a.shape
    kk = k_len if k_len is not None else kdim_a
    n = n_out if n_out is not None else (b.shape[0] if mode == "nt" else b.shape[1])
    tm, tn, tk = min(tm, m), min(tn, n), min(tk, kk)
    assert m % tm == 0 and n % tn == 0 and kk % tk == 0, (name, m, n, kk, tm, tn, tk)
    nk = kk // tk
    if mode == "nn":
        a_spec = pl.BlockSpec((tm, tk), lambda i, j, k: (i, k + a_koff))
        b_spec = pl.BlockSpec((tk, tn), lambda i, j, k: (k + b_koff, j + b_off))
        dims = ((1,), (0,))
    elif mode == "nt":
        a_spec = pl.BlockSpec((tm, tk), lambda i, j, k: (i, k + a_koff))
        b_spec = pl.BlockSpec((tn, tk), lambda i, j, k: (j + b_off, k + b_koff))
        dims = ((1,), (1,))
    else:
        a_spec = pl.BlockSpec((tk, tm), lambda i, j, k: (k + a_koff, i))
        b_spec = pl.BlockSpec((tk, tn), lambda i, j, k: (k + b_koff, j + b_off))
        dims = ((0,), (0,))
    o_spec = pl.BlockSpec((tm, tn), lambda i, j, k: (i, j))
    has_add = add is not None

    def body(*refs):
        if has_add:
            a_ref, b_ref, add_ref, o_ref, acc_ref = refs
        else:
            a_ref, b_ref, o_ref, acc_ref = refs
        k = pl.program_id(2)
        part = _dot(a_ref[...].astype(bf16), b_ref[...].astype(bf16), dims)

        @pl.when(k == 0)
        def _():
            acc_ref[...] = part

        @pl.when(k > 0)
        def _():
            acc_ref[...] += part

        @pl.when(k == nk - 1)
        def _():
            r = acc_ref[...]
            if has_add:
                r = r + add_ref[...]
            o_ref[...] = r.astype(out_dtype)

    in_specs = [a_spec, b_spec] + ([o_spec] if has_add else [])
    args = (a, b) + ((add,) if has_add else ())
    return pl.pallas_call(
        body, name=name, grid=(m // tm, n // tn, nk), in_specs=in_specs, out_specs=o_spec,
        out_shape=jax.ShapeDtypeStruct((m, n), out_dtype), scratch_shapes=[pltpu.VMEM((tm, tn), f32)],
        compiler_params=_cparams(("parallel", "parallel", "arbitrary")),
    )(*args)


def _out_proj(o, yn, w_out, h, name, tm=2048, tn=512):
    m, kb = o.shape
    n = w_out.shape[1]
    n_y = yn.shape[1] // kb
    assert yn.shape[1] % kb == 0 and w_out.shape[0] == kb * (1 + n_y)

    def body(*refs):
        o_ref, y_refs, w_refs, h_ref, out_ref = refs[0], refs[1:1 + n_y], refs[1 + n_y:2 + 2 * n_y], refs[-2], refs[-1]
        acc = h_ref[...] + _nn(o_ref[...].astype(bf16), w_refs[0][...])
        for y_ref, w_ref in zip(y_refs, w_refs[1:]):
            acc = acc + _nn(y_ref[...], w_ref[...])
        out_ref[...] = acc

    res = pl.BlockSpec((tm, tn), lambda i, j: (i, j))

    def a_blk(c):
        return pl.BlockSpec((tm, kb), lambda i, j: (i, c))

    def w_blk(r):
        return pl.BlockSpec((kb, tn), lambda i, j: (r, j))

    return pl.pallas_call(
        body, name=name, grid=(m // tm, n // tn),
        in_specs=[a_blk(0)] + [a_blk(c) for c in range(n_y)] + [w_blk(r) for r in range(1 + n_y)] + [res],
        out_specs=res, out_shape=jax.ShapeDtypeStruct((m, n), f32), compiler_params=_cparams(("parallel", "parallel")),
    )(o, *[yn] * n_y, *[w_out] * (1 + n_y), h)


def _swiglu_fwd(hn, w_gate, w_up, name, tm=2048, tn=256):
    m, k = hn.shape
    n = w_gate.shape[1]

    def body(a_ref, wg_ref, wu_ref, g_ref, u_ref, act_ref):
        a = a_ref[...]
        g = _nn(a, wg_ref[...])
        u = _nn(a, wu_ref[...])
        sg, dsg = _silu_and_grad(g)
        g_ref[...] = (u * dsg).astype(bf16)
        u_ref[...] = sg.astype(bf16)
        act_ref[...] = (sg * u).astype(bf16)

    a_spec = pl.BlockSpec((tm, k), lambda i, j: (i, 0))
    w_spec = pl.BlockSpec((k, tn), lambda i, j: (0, j))
    o_spec = pl.BlockSpec((tm, tn), lambda i, j: (i, j))
    return pl.pallas_call(
        body, name=name, grid=(m // tm, n // tn), in_specs=[a_spec, w_spec, w_spec], out_specs=[o_spec, o_spec, o_spec],
        out_shape=[jax.ShapeDtypeStruct((m, n), bf16)] * 3,
        compiler_params=_cparams(("parallel", "parallel")),
    )(hn, w_gate, w_up)


def _swiglu_bwd(dh, w_down, g, u, name, tm=2048, tn=256):
    m, k = dh.shape
    n = w_down.shape[0]

    def body(a_ref, w_ref, g_ref, u_ref, dg_ref, du_ref):
        dact = _nt(a_ref[...].astype(bf16), w_ref[...])
        dg_ref[...] = (dact * g_ref[...].astype(f32)).astype(bf16)
        du_ref[...] = (dact * u_ref[...].astype(f32)).astype(bf16)

    a_spec = pl.BlockSpec((tm, k), lambda i, j: (i, 0))
    w_spec = pl.BlockSpec((tn, k), lambda i, j: (j, 0))
    o_spec = pl.BlockSpec((tm, tn), lambda i, j: (i, j))
    return pl.pallas_call(
        body, name=name, grid=(m // tm, n // tn), in_specs=[a_spec, w_spec, o_spec, o_spec], out_specs=[o_spec, o_spec],
        out_shape=[jax.ShapeDtypeStruct((m, n), bf16), jax.ShapeDtypeStruct((m, n), bf16)],
        compiler_params=_cparams(("parallel", "parallel")),
    )(dh, w_down, g, u)


def _rmsnorm_fwd(h, w, name, tm=512):
    m, d = h.shape

    def body(h_ref, w_ref, o_ref):
        x = h_ref[...]
        r = lax.rsqrt(jnp.mean(x * x, axis=-1, keepdims=True) + EPS)
        o_ref[...] = (x * r * w_ref[...]).astype(bf16)

    return pl.pallas_call(
        body, name=name, grid=(m // tm,),
        in_specs=[pl.BlockSpec((tm, d), lambda i: (i, 0)), pl.BlockSpec((1, d), lambda i: (0, 0))],
        out_specs=pl.BlockSpec((tm, d), lambda i: (i, 0)), out_shape=jax.ShapeDtypeStruct((m, d), bf16),
        compiler_params=_cparams(("parallel",)),
    )(h, w)


def _rmsnorm_bwd(dhn, h, w, dres, name, tm=512):
    m, d = h.shape

    def body(dhn_ref, h_ref, w_ref, dres_ref, dh_ref, dw_ref):
        x = h_ref[...]
        r = lax.rsqrt(jnp.mean(x * x, axis=-1, keepdims=True) + EPS)
        xhat = x * r
        dy = dhn_ref[...]
        gw = dy * w_ref[...]
        dh_ref[...] = dres_ref[...] + r * (gw - xhat * jnp.mean(gw * xhat, axis=-1, keepdims=True))
        part = _rowsum8(dy * xhat)

        @pl.when(pl.program_id(0) == 0)
        def _():
            dw_ref[...] = part

        @pl.when(pl.program_id(0) > 0)
        def _():
            dw_ref[...] += part

    row = pl.BlockSpec((tm, d), lambda i: (i, 0))
    return pl.pallas_call(
        body, name=name, grid=(m // tm,),
        in_specs=[row, row, pl.BlockSpec((1, d), lambda i: (0, 0)), row],
        out_specs=[row, pl.BlockSpec((SUBLANES, d), lambda i: (0, 0))],
        out_shape=[jax.ShapeDtypeStruct((m, d), f32), jax.ShapeDtypeStruct((SUBLANES, d), f32)],
        compiler_params=_cparams(("arbitrary",)),
    )(dhn, h, w, dres)


def _nt_norm_bwd(pairs, h, w, dres, name, tm=1024, tk=704):
    m, d = h.shape
    steps = [p[0].shape[1] // tk for p in pairs]
    assert all(p[0].shape[1] % tk == 0 for p in pairs), (name, tk)
    starts = [sum(steps[:i]) for i in range(len(pairs))]
    nk = sum(steps)
    n_p = len(pairs)

    def body(*refs):
        ab = refs[:2 * n_p]
        h_ref, w_ref, dres_ref, dh_ref, dw_ref, acc_ref = refs[2 * n_p:]
        i, k = pl.program_id(0), pl.program_id(1)

        @pl.when(k == 0)
        def _():
            acc_ref[...] = jnp.zeros_like(acc_ref)

        for p in range(n_p):
            @pl.when((k >= starts[p]) & (k < starts[p] + steps[p]))
            def _(p=p):
                acc_ref[...] += _nt(ab[2 * p][...], ab[2 * p + 1][...])

        @pl.when(k == nk - 1)
        def _():
            x = h_ref[...]
            r = lax.rsqrt(jnp.mean(x * x, axis=-1, keepdims=True) + EPS)
            xhat = x * r
            dy = acc_ref[...]
            gw = dy * w_ref[...]
            dh_ref[...] = dres_ref[...] + r * (gw - xhat * jnp.mean(gw * xhat, axis=-1, keepdims=True))
            part = _rowsum8(dy * xhat)

            @pl.when(i == 0)
            def _():
                dw_ref[...] = part

            @pl.when(i > 0)
            def _():
                dw_ref[...] += part

    def clamp(k, p):
        return jnp.clip(k - starts[p], 0, steps[p] - 1)

    in_specs = []
    for p in range(n_p):
        in_specs += [pl.BlockSpec((tm, tk), lambda i, k, p=p: (i, clamp(k, p))), pl.BlockSpec((d, tk), lambda i, k, p=p: (0, clamp(k, p)))]
    row = pl.BlockSpec((tm, d), lambda i, k: (i, 0))
    in_specs += [row, pl.BlockSpec((1, d), lambda i, k: (0, 0)), row]
    return pl.pallas_call(
        body, name=name, grid=(m // tm, nk), in_specs=in_specs,
        out_specs=[row, pl.BlockSpec((SUBLANES, d), lambda i, k: (0, 0))],
        out_shape=[jax.ShapeDtypeStruct((m, d), f32), jax.ShapeDtypeStruct((SUBLANES, d), f32)],
        scratch_shapes=[pltpu.VMEM((tm, d), f32)], compiler_params=_cparams(("arbitrary", "arbitrary")),
    )(*[t for p in pairs for t in p], h, w, dres)


def _final_loss(h, w, target, name, tm=512):
    m, d = h.shape

    def body(h_ref, w_ref, t_ref, dh_ref, loss_ref, dw_ref):
        x = h_ref[...]
        r = lax.rsqrt(jnp.mean(x * x, axis=-1, keepdims=True) + EPS)
        xhat = x * r
        ww = w_ref[...]
        err = xhat * ww - t_ref[...]
        dy = err * (1.0 / d)
        gw = dy * ww
        dh_ref[...] = r * (gw - xhat * jnp.mean(gw * xhat, axis=-1, keepdims=True))
        lpart = _rowsum8(err * err) * (0.5 / d)
        wpart = _rowsum8(dy * xhat)

        @pl.when(pl.program_id(0) == 0)
        def _():
            loss_ref[...] = lpart
            dw_ref[...] = wpart

        @pl.when(pl.program_id(0) > 0)
        def _():
            loss_ref[...] += lpart
            dw_ref[...] += wpart

    row = pl.BlockSpec((tm, d), lambda i: (i, 0))
    acc = pl.BlockSpec((SUBLANES, d), lambda i: (0, 0))
    return pl.pallas_call(
        body, name=name, grid=(m // tm,),
        in_specs=[row, pl.BlockSpec((1, d), lambda i: (0, 0)), row], out_specs=[row, acc, acc],
        out_shape=[jax.ShapeDtypeStruct((m, d), f32), jax.ShapeDtypeStruct((SUBLANES, d), f32), jax.ShapeDtypeStruct((SUBLANES, d), f32)],
        compiler_params=_cparams(("arbitrary",)),
    )(h, w, target)


def _lane_tables():
    f = np.arange(LANES) % HEAD_DIM
    inv = ROPE_THETA ** (-jnp.arange(0, ROPE_DIM, 2, dtype=f32) / ROPE_DIM)
    invf = jnp.where(f < ROPE_DIM, inv[f % (ROPE_DIM // 2)], 0.0).astype(f32)
    return invf.reshape(1, LANES)


def _rope_tables(pos_col, name):
    t = pos_col.shape[0]
    tm = SEQ

    def body(p_ref, f_ref, c_ref, s1_ref, s2_ref):
        ang = p_ref[...].astype(f32) * f_ref[...]
        co, si = jnp.cos(ang), jnp.sin(ang)
        f = lax.broadcasted_iota(jnp.int32, (tm, LANES), 1) % HEAD_DIM
        c_ref[...] = jnp.where(f < ROPE_DIM, co, 1.0)
        s1_ref[...] = jnp.where(f < ROPE_DIM // 2, -si, 0.0)
        s2_ref[...] = jnp.where((f >= ROPE_DIM // 2) & (f < ROPE_DIM), si, 0.0)

    row = pl.BlockSpec((tm, LANES), lambda i: (i, 0))
    return pl.pallas_call(
        body, name=name, grid=(t // tm,),
        in_specs=[pl.BlockSpec((tm, 1), lambda i: (i, 0)), pl.BlockSpec((1, LANES), lambda i: (0, 0))],
        out_specs=[row, row, row], out_shape=[jax.ShapeDtypeStruct((t, LANES), f32)] * 3,
        compiler_params=_cparams(("parallel",)),
    )(pos_col, _lane_tables())


def _rot(x, c, s1, s2):
    return x * c + pltpu.roll(x, LANES - ROPE_DIM // 2, 1) * s1 + pltpu.roll(x, ROPE_DIM // 2, 1) * s2


def _rot_t(g, c, s1, s2):
    return g * c + pltpu.roll(g * s1, ROPE_DIM // 2, 1) + pltpu.roll(g * s2, LANES - ROPE_DIM // 2, 1)


def _dup_head(x, kvh, low):
    a = jnp.where(kvh == 0, x, pltpu.roll(x, HEAD_DIM, 1))
    return jnp.where(low, a, pltpu.roll(a, HEAD_DIM, 1))


def _deinterleave(src_ref, dst_ref, d, dtype):
    length = SEQ // d
    if d == 1:
        dst_ref[...] = src_ref[...].astype(dtype)
    else:
        for r in range(d):
            dst_ref[pl.ds(r * length, length), :] = src_ref[pl.ds(r, length, stride=d), :].astype(dtype)


def _interleave_store(src_ref, dst_ref, d, accumulate):
    length = SEQ // d
    if d == 1:
        if accumulate:
            dst_ref[...] += src_ref[...]
        else:
            dst_ref[...] = src_ref[...]
    else:
        for r in range(d):
            blk = src_ref[pl.ds(r * length, length), :]
            if accumulate:
                dst_ref[pl.ds(r, length, stride=d), :] = dst_ref[pl.ds(r, length, stride=d), :] + blk
            else:
                dst_ref[pl.ds(r, length, stride=d), :] = blk


def _attn_masks():
    qi = lax.broadcasted_iota(jnp.int32, (ATTN_BLOCK, ATTN_BLOCK), 0)
    ki = lax.broadcasted_iota(jnp.int32, (ATTN_BLOCK, ATTN_BLOCK), 1)
    low = lax.broadcasted_iota(jnp.int32, (ATTN_BLOCK, LANES), 1) < HEAD_DIM
    return ki <= qi, ki >= qi, low


NEG_INF = float("-inf")
ATTN_UNROLL = 4


def _attn_fwd(qkv, tabs, name):
    t = qkv.shape[0]
    nb = t // SEQ
    n_blk = SEQ // ATTN_BLOCK

    def body(q_ref, k_ref, v_ref, c_ref, s1_ref, s2_ref, o_ref, lse_ref,
             qr, kr, vr, qd, kd, vd, ob, lb, o0, o1, o2, l0, l1, l2, ss):
        kvh = pl.program_id(1) // 2
        cur_ok, prev_ok, low = _attn_masks()
        lowfull = lax.broadcasted_iota(jnp.int32, (SEQ, LANES), 1) < HEAD_DIM
        c, s1, s2 = c_ref[...], s1_ref[...], s2_ref[...]
        qr[...] = _rot(q_ref[...], c, s1, s2) * (HEAD_DIM ** -0.5)
        kr[...] = _dup_head(_rot(k_ref[...], c, s1, s2), kvh, lowfull)
        vr[...] = _dup_head(v_ref[...], kvh, lowfull)
        onat, lnat = (o0, o1, o2), (l0, l1, l2)
        for bi, d in enumerate(DILATIONS):
            _deinterleave(qr, qd, d, bf16)
            _deinterleave(kr, kd, d, bf16)
            _deinterleave(vr, vd, d, bf16)
            per_res = n_blk // d
            use_prev = per_res > 1

            def scores(n, carry):
                start = pl.multiple_of(n * ATTN_BLOCK, ATTN_BLOCK)
                has_prev = (n % per_res) != 0
                pstart = pl.multiple_of(jnp.maximum(n - 1, 0) * ATTN_BLOCK, ATTN_BLOCK)
                qb = qd[pl.ds(start, ATTN_BLOCK), :]
                kc = kd[pl.ds(start, ATTN_BLOCK), :]
                if use_prev:
                    kp = kd[pl.ds(pstart, ATTN_BLOCK), :]
                for a in range(2):
                    qa = jnp.where(low if a == 0 else ~low, qb, jnp.zeros_like(qb))
                    ss[2 * n + a, :, 0:ATTN_BLOCK] = jnp.where(cur_ok, _nt(qa, kc), NEG_INF)
                    if use_prev:
                        ss[2 * n + a, :, ATTN_BLOCK:2 * ATTN_BLOCK] = jnp.where(prev_ok & has_prev, _nt(qa, kp), NEG_INF)
                return carry

            def softmax_pv(n, carry):
                start = pl.multiple_of(n * ATTN_BLOCK, ATTN_BLOCK)
                pstart = pl.multiple_of(jnp.maximum(n - 1, 0) * ATTN_BLOCK, ATTN_BLOCK)
                vc = vd[pl.ds(start, ATTN_BLOCK), :]
                if use_prev:
                    vp = vd[pl.ds(pstart, ATTN_BLOCK), :]
                outs, lses = [], []
                for a in range(2):
                    sc = ss[2 * n + a, :, 0:ATTN_BLOCK]
                    if use_prev:
                        sp = ss[2 * n + a, :, ATTN_BLOCK:2 * ATTN_BLOCK]
                        m = jnp.max(jnp.maximum(sc, sp), axis=1, keepdims=True)
                        pc, pp = jnp.exp(sc - m), jnp.exp(sp - m)
                        den = jnp.sum(pc + pp, axis=1, keepdims=True)
                        acc = _nn(pc.astype(bf16), vc) + _nn(pp.astype(bf16), vp)
                    else:
                        m = jnp.max(sc, axis=1, keepdims=True)
                        pc = jnp.exp(sc - m)
                        den = jnp.sum(pc, axis=1, keepdims=True)
                        acc = _nn(pc.astype(bf16), vc)
                    outs.append(acc * (1.0 / den))
                    lses.append(m + jnp.log(den))
                ob[pl.ds(start, ATTN_BLOCK), :] = jnp.where(low, outs[0], outs[1])
                lb[pl.ds(start, ATTN_BLOCK), :] = jnp.where(low, lses[0], lses[1])
                return carry

            lax.fori_loop(0, n_blk, scores, 0, unroll=ATTN_UNROLL)
            lax.fori_loop(0, n_blk, softmax_pv, 0, unroll=ATTN_UNROLL)
            _interleave_store(ob, onat[bi], d, False)
            _interleave_store(lb, lnat[bi], d, False)
        la, lbb, lc = l0[...], l1[...], l2[...]
        lm = jnp.maximum(jnp.maximum(la, lbb), lc)
        wa, wb, wc = jnp.exp(la - lm), jnp.exp(lbb - lm), jnp.exp(lc - lm)
        ws = wa + wb + wc
        o_ref[...] = (wa * o0[...] + wb * o1[...] + wc * o2[...]) / ws
        lse_ref[...] = lm + jnp.log(ws)

    def col(jj):
        return pl.BlockSpec((SEQ, LANES), lambda b, j: (b, jj if jj is not None else j))

    tab = pl.BlockSpec((SEQ, LANES), lambda b, j: (b, 0))
    fs = pltpu.VMEM((SEQ, LANES), f32)
    hs = pltpu.VMEM((SEQ, LANES), bf16)
    return pl.pallas_call(
        body, name=name, grid=(nb, ATTN_WIDTH // LANES),
        in_specs=[col(None), col(ATTN_WIDTH // LANES), col(ATTN_WIDTH // LANES + 1), tab, tab, tab],
        out_specs=[col(None), col(None)],
        out_shape=[jax.ShapeDtypeStruct((t, ATTN_WIDTH), f32), jax.ShapeDtypeStruct((t, ATTN_WIDTH), f32)],
        scratch_shapes=[fs, fs, fs, hs, hs, hs, fs, fs, fs, fs, fs, fs, fs, fs, pltpu.VMEM((2 * n_blk, ATTN_BLOCK, 2 * ATTN_BLOCK), f32)],
        compiler_params=_cparams(("parallel", "parallel")),
    )(qkv, qkv, qkv, *tabs)


def _attn_bwd(qkv, tabs, o, lse, do, name):
    t = qkv.shape[0]
    nb = t // SEQ
    n_blk = SEQ // ATTN_BLOCK
    n_j = ATTN_WIDTH // LANES

    def body(q_ref, k_ref, v_ref, c_ref, s1_ref, s2_ref, o_ref, lse_ref, do_ref, dq_ref, dk_ref, dv_ref,
             qr, kr, vr, dl, qd, kd, vd, dod, lsd, dld, dqd, dkd, dvd, dqa, dka, dva, pb, dsb, dk_acc, dv_acc):
        j = pl.program_id(1)
        pb[2 * n_blk:2 * n_blk + 2] = jnp.zeros((2, ATTN_BLOCK, 2 * ATTN_BLOCK), bf16)
        dsb[2 * n_blk:2 * n_blk + 2] = jnp.zeros((2, ATTN_BLOCK, 2 * ATTN_BLOCK), bf16)
        kvh = j // 2
        cur_ok, prev_ok, low = _attn_masks()
        lowfull = lax.broadcasted_iota(jnp.int32, (SEQ, LANES), 1) < HEAD_DIM
        c, s1, s2 = c_ref[...], s1_ref[...], s2_ref[...]
        qr[...] = _rot(q_ref[...], c, s1, s2) * (HEAD_DIM ** -0.5)
        kr[...] = _dup_head(_rot(k_ref[...], c, s1, s2), kvh, lowfull)
        vr[...] = _dup_head(v_ref[...], kvh, lowfull)
        prod = do_ref[...] * o_ref[...]
        d_lo = jnp.sum(jnp.where(lowfull, prod, 0.0), axis=1, keepdims=True)
        d_hi = jnp.sum(jnp.where(lowfull, 0.0, prod), axis=1, keepdims=True)
        dl[...] = jnp.where(lowfull, d_lo, d_hi)
        dqa[...] = jnp.zeros_like(dqa)
        dka[...] = jnp.zeros_like(dka)
        dva[...] = jnp.zeros_like(dva)
        for d in DILATIONS:
            _deinterleave(qr, qd, d, bf16)
            _deinterleave(kr, kd, d, bf16)
            _deinterleave(vr, vd, d, bf16)
            _deinterleave(do_ref, dod, d, bf16)
            _deinterleave(lse_ref, lsd, d, f32)
            _deinterleave(dl, dld, d, f32)
            per_res = n_blk // d
            use_prev = per_res > 1
            curl, prevl = slice(0, ATTN_BLOCK), slice(ATTN_BLOCK, 2 * ATTN_BLOCK)

            def halves(x):
                zero = jnp.zeros_like(x)
                return jnp.where(low, x, zero), jnp.where(low, zero, x)

            def probs(n, carry):
                start = pl.multiple_of(n * ATTN_BLOCK, ATTN_BLOCK)
                has_prev = (n % per_res) != 0
                pstart = pl.multiple_of(jnp.maximum(n - 1, 0) * ATTN_BLOCK, ATTN_BLOCK)
                cur, prev = pl.ds(start, ATTN_BLOCK), pl.ds(pstart, ATTN_BLOCK)
                qas, doas = halves(qd[cur, :]), halves(dod[cur, :])
                kc, vc = kd[cur, :], vd[cur, :]
                if use_prev:
                    kp, vp = kd[prev, :], vd[prev, :]
                lsb, dlb = lsd[cur, :], dld[cur, :]
                for a in range(2):
                    ls = lsb[:, a * HEAD_DIM:a * HEAD_DIM + 1]
                    de = dlb[:, a * HEAD_DIM:a * HEAD_DIM + 1]
                    pc = jnp.exp(jnp.where(cur_ok, _nt(qas[a], kc), NEG_INF) - ls)
                    pb[2 * n + a, :, curl] = pc.astype(bf16)
                    dsb[2 * n + a, :, curl] = (pc * (_nt(doas[a], vc) - de)).astype(bf16)
                    if use_prev:
                        pp = jnp.exp(jnp.where(prev_ok & has_prev, _nt(qas[a], kp), NEG_INF) - ls)
                        pb[2 * n + a, :, prevl] = pp.astype(bf16)
                        dsb[2 * n + a, :, prevl] = (pp * (_nt(doas[a], vp) - de)).astype(bf16)
                return carry

            def grads(n, carry):
                start = pl.multiple_of(n * ATTN_BLOCK, ATTN_BLOCK)
                pstart = pl.multiple_of(jnp.maximum(n - 1, 0) * ATTN_BLOCK, ATTN_BLOCK)
                nstart = pl.multiple_of(jnp.minimum(n + 1, n_blk - 1) * ATTN_BLOCK, ATTN_BLOCK)
                cur, prev, nxt = pl.ds(start, ATTN_BLOCK), pl.ds(pstart, ATTN_BLOCK), pl.ds(nstart, ATTN_BLOCK)
                kc = kd[cur, :]
                dqs = [_nn(dsb[2 * n + a, :, curl], kc) for a in range(2)]
                q_rows, do_rows = list(halves(qd[cur, :])), list(halves(dod[cur, :]))
                ds_rows, p_rows = [dsb[2 * n + a, :, curl] for a in range(2)], [pb[2 * n + a, :, curl] for a in range(2)]
                if use_prev:
                    kp = kd[prev, :]
                    dqs = [dqs[a] + _nn(dsb[2 * n + a, :, prevl], kp) for a in range(2)]
                    q_rows += list(halves(qd[nxt, :]))
                    do_rows += list(halves(dod[nxt, :]))
                    ds_rows += [dsb[2 * n + 2 + a, :, prevl] for a in range(2)]
                    p_rows += [pb[2 * n + 2 + a, :, prevl] for a in range(2)]
                dqd[cur, :] = jnp.where(low, dqs[0], dqs[1])
                dkd[cur, :] = _tn(jnp.concatenate(ds_rows, axis=0), jnp.concatenate(q_rows, axis=0))
                dvd[cur, :] = _tn(jnp.concatenate(p_rows, axis=0), jnp.concatenate(do_rows, axis=0))
                return carry

            lax.fori_loop(0, n_blk, probs, 0, unroll=ATTN_UNROLL)
            lax.fori_loop(0, n_blk, grads, 0, unroll=ATTN_UNROLL)
            _interleave_store(dqd, dqa, d, True)
            _interleave_store(dkd, dka, d, True)
            _interleave_store(dvd, dva, d, True)
        dq_ref[...] = _rot_t(dqa[...] * (HEAD_DIM ** -0.5), c, s1, s2).astype(bf16)
        dkf = dka[...]
        dkf = _rot_t(dkf + pltpu.roll(dkf, HEAD_DIM, 1), c, s1, s2)
        dvf = dva[...]
        dvf = dvf + pltpu.roll(dvf, HEAD_DIM, 1)
        mine = (lax.broadcasted_iota(jnp.int32, (SEQ, LANES), 1) // HEAD_DIM) == kvh
        dkc_, dvc_ = jnp.where(mine, dkf, 0.0), jnp.where(mine, dvf, 0.0)

        @pl.when(j == 0)
        def _():
            dk_acc[...] = dkc_
            dv_acc[...] = dvc_

        @pl.when(j > 0)
        def _():
            dk_acc[...] += dkc_
            dv_acc[...] += dvc_

        @pl.when(j == n_j - 1)
        def _():
            dk_ref[...] = dk_acc[...].astype(bf16)
            dv_ref[...] = dv_acc[...].astype(bf16)

    def col(jj):
        return pl.BlockSpec((SEQ, LANES), lambda b, j: (b, jj if jj is not None else j))

    tab = pl.BlockSpec((SEQ, LANES), lambda b, j: (b, 0))
    fs = pltpu.VMEM((SEQ, LANES), f32)
    hs = pltpu.VMEM((SEQ, LANES), bf16)
    return pl.pallas_call(
        body, name=name, grid=(nb, n_j),
        in_specs=[col(None), col(n_j), col(n_j + 1), tab, tab, tab, col(None), col(None), col(None)],
        out_specs=[col(None), tab, tab],
        out_shape=[jax.ShapeDtypeStruct((t, ATTN_WIDTH), bf16), jax.ShapeDtypeStruct((t, LANES), bf16), jax.ShapeDtypeStruct((t, LANES), bf16)],
        scratch_shapes=[fs, fs, fs, fs, hs, hs, hs, hs, fs, fs, fs, fs, fs, fs, fs, fs,
                        pltpu.VMEM((2 * n_blk + 2, ATTN_BLOCK, 2 * ATTN_BLOCK), bf16), pltpu.VMEM((2 * n_blk + 2, ATTN_BLOCK, 2 * ATTN_BLOCK), bf16), fs, fs],
        compiler_params=_cparams(("parallel", "arbitrary")),
    )(qkv, qkv, qkv, *tabs, o, lse, do)


def _conv_pre(x, w_ref, b_ref, row):
    shifted = [x] + [jnp.where(row >= s, pltpu.roll(x, s, 0), 0.0) for s in range(1, CONV_WIDTH)]
    pre = b_ref[...] + w_ref[CONV_WIDTH - 1:CONV_WIDTH, :] * x
    for s in range(1, CONV_WIDTH):
        pre = pre + w_ref[CONV_WIDTH - 1 - s:CONV_WIDTH - s, :] * shifted[s]
    return pre, shifted


def _conv_fwd(x, w, b, name, tc=512):
    t, ch = x.shape

    def body(x_ref, w_ref, b_ref, o_ref):
        row = lax.broadcasted_iota(jnp.int32, (SEQ, tc), 0)
        pre, _ = _conv_pre(x_ref[...], w_ref, b_ref, row)
        o_ref[...] = _silu(pre)

    xs = pl.BlockSpec((SEQ, tc), lambda i, j: (i, j))
    return pl.pallas_call(
        body, name=name, grid=(t // SEQ, ch // tc),
        in_specs=[xs, pl.BlockSpec((CONV_WIDTH, tc), lambda i, j: (0, j)), pl.BlockSpec((1, tc), lambda i, j: (0, j))],
        out_specs=xs, out_shape=jax.ShapeDtypeStruct((t, ch), f32),
        compiler_params=_cparams(("parallel", "parallel")),
    )(x, w, b)


def _conv_bwd(x, w, b, dact, name, tc=512):
    t, ch = x.shape

    def body(x_ref, w_ref, b_ref, d_ref, dx_ref, dw_ref, db_ref):
        row = lax.broadcasted_iota(jnp.int32, (SEQ, tc), 0)
        pre, shifted = _conv_pre(x_ref[...], w_ref, b_ref, row)
        dpre = d_ref[...] * _dsilu(pre)
        dx = w_ref[CONV_WIDTH - 1:CONV_WIDTH, :] * dpre
        for s in range(1, CONV_WIDTH):
            dx = dx + w_ref[CONV_WIDTH - 1 - s:CONV_WIDTH - s, :] * jnp.where(row < SEQ - s, pltpu.roll(dpre, SEQ - s, 0), 0.0)
        dx_ref[...] = dx.astype(bf16)
        first = pl.program_id(1) == 0
        parts = [jnp.sum(dpre * shifted[CONV_WIDTH - 1 - k], axis=0, keepdims=True) for k in range(CONV_WIDTH)]
        dbp = jnp.sum(dpre, axis=0, keepdims=True)

        @pl.when(first)
        def _():
            for k in range(CONV_WIDTH):
                dw_ref[k:k + 1, :] = parts[k]
            db_ref[...] = dbp

        @pl.when(jnp.logical_not(first))
        def _():
            for k in range(CONV_WIDTH):
                dw_ref[k:k + 1, :] += parts[k]
            db_ref[...] += dbp

    xs = pl.BlockSpec((SEQ, tc), lambda j, i: (i, j))
    ws = pl.BlockSpec((CONV_WIDTH, tc), lambda j, i: (0, j))
    bs = pl.BlockSpec((1, tc), lambda j, i: (0, j))
    return pl.pallas_call(
        body, name=name, grid=(ch // tc, t // SEQ),
        in_specs=[xs, ws, bs, xs], out_specs=[xs, ws, bs],
        out_shape=[jax.ShapeDtypeStruct((t, ch), bf16), jax.ShapeDtypeStruct((CONV_WIDTH, ch), f32), jax.ShapeDtypeStruct((1, ch), f32)],
        compiler_params=_cparams(("parallel", "arbitrary")),
    )(x, w, b, dact)


GROUP_W = SSM_INNER // SSM_GROUPS
HEADS_PER_GROUP = SSM_HEADS // SSM_GROUPS


def _split3(x):
    hi = x.astype(bf16)
    r1 = x - hi.astype(f32)
    mid = r1.astype(bf16)
    lo = (r1 - mid.astype(f32)).astype(bf16)
    return hi, mid, lo


def _dot_exact(x, sel, dims, x_is_lhs=True):
    parts = _split3(x)
    if x_is_lhs:
        return _dot(parts[0], sel, dims) + _dot(parts[1], sel, dims) + _dot(parts[2], sel, dims)
    return _dot(sel, parts[0], dims) + _dot(sel, parts[1], dims) + _dot(sel, parts[2], dims)


def _ssd_common(xbc_ref, dt_ref, bias_ref, alog_ref):
    r = lax.broadcasted_iota(jnp.int32, (CHUNK, CHUNK), 0)
    cidx = lax.broadcasted_iota(jnp.int32, (CHUNK, CHUNK), 1)
    causal = r >= cidx
    tril = causal.astype(bf16)
    expand = (lax.broadcasted_iota(jnp.int32, (CHUNK, SSM_INNER), 0)
              == lax.broadcasted_iota(jnp.int32, (CHUNK, SSM_INNER), 1) // HEAD_DIM).astype(bf16)
    head_lane = cidx < SSM_HEADS
    dtp = dt_ref[...] + bias_ref[...]
    dt = jnp.where(head_lane, _softplus(dtp), 0.0)
    a_neg = -jnp.exp(alog_ref[...])
    a = dt * a_neg
    nn_dims = ((1,), (0,))
    cs = _dot_exact(a, tril, nn_dims, x_is_lhs=False)
    dt_e = _dot_exact(dt, expand, nn_dims)
    cs_e = _dot_exact(cs, expand, nn_dims)
    xs = xbc_ref[:, 0:SSM_INNER]
    xg = xs * dt_e
    ecs = jnp.exp(cs_e)
    cs_last = cs_e[CHUNK - 1:CHUNK, :]
    dse = jnp.exp(cs_last - cs_e)
    cde = jnp.exp(cs_last)
    return dict(r=r, cidx=cidx, causal=causal, tril=tril, expand=expand, head_lane=head_lane, dtp=dtp, dt=dt, a_neg=a_neg,
                cs=cs, cst=cs.T, dt_e=dt_e, cs_e=cs_e, xs=xs, xg=xg, ecs=ecs, dse=dse, cde=cde)


def _decay_mat(q, h):
    return jnp.exp(jnp.where(q["causal"], q["cs"][:, h:h + 1] - q["cst"][h:h + 1, :], NEG_INF))


def _gate_norm(y, z, nw, gate=None):
    y2 = y * (_silu(z) if gate is None else gate)
    outs, xhats, rs = [], [], []
    for g in range(SSM_GROUPS):
        sl = slice(g * GROUP_W, (g + 1) * GROUP_W)
        yg = y2[:, sl]
        r = lax.rsqrt(jnp.mean(yg * yg, axis=-1, keepdims=True) + EPS)
        xhats.append(yg * r)
        rs.append(r)
        outs.append(yg * r * nw[:, sl])
    return y2, outs, xhats, rs


def _ssd_fwd(xbc, z, dtp, params, name):
    t = xbc.shape[0]
    n_chunk = SEQ // CHUNK
    low = None

    def body(xbc_ref, z_ref, dt_ref, bias_ref, alog_ref, dskip_ref, nw_ref, yn_ref, y_ref, hs_ref, h_scr):
        @pl.when(pl.program_id(1) == 0)
        def _():
            h_scr[...] = jnp.zeros_like(h_scr)

        q = _ssd_common(xbc_ref, dt_ref, bias_ref, alog_ref)
        low = lax.broadcasted_iota(jnp.int32, (CHUNK, LANES), 1) < HEAD_DIM
        xgb = q["xg"].astype(bf16)
        wst = (q["xg"] * q["dse"]).astype(bf16)
        hs_ref[0] = h_scr[...]
        ys = []
        for g in range(SSM_GROUPS):
            gl = slice(g * GROUP_W, (g + 1) * GROUP_W)
            bg = xbc_ref[:, SSM_INNER + g * D_STATE:SSM_INNER + (g + 1) * D_STATE].astype(bf16)
            cg = xbc_ref[:, SSM_INNER + SSM_GROUPS * D_STATE + g * D_STATE:SSM_INNER + SSM_GROUPS * D_STATE + (g + 1) * D_STATE].astype(bf16)
            cb = _nt(cg, bg)
            hg = h_scr[g]
            yoff = _nn(cg, hg.astype(bf16)) * q["ecs"][:, gl]
            pieces = []
            for i in range(HEADS_PER_GROUP // 2):
                h0 = g * HEADS_PER_GROUP + 2 * i
                xp = xgb[:, h0 * HEAD_DIM:(h0 + 2) * HEAD_DIM]
                m0 = (cb * _decay_mat(q, h0)).astype(bf16)
                m1 = (cb * _decay_mat(q, h0 + 1)).astype(bf16)
                zero = jnp.zeros_like(xp)
                pieces.append(_nn(m0, jnp.where(low, xp, zero)) + _nn(m1, jnp.where(low, zero, xp)))
            ys.append(jnp.concatenate(pieces, axis=1) + yoff + dskip_ref[:, gl] * q["xs"][:, gl])
            h_scr[g] = hg * q["cde"][:, gl] + _tn(bg, wst[:, gl])
        y = jnp.concatenate(ys, axis=1)
        y_ref[...] = y
        _, outs, _, _ = _gate_norm(y, z_ref[...], nw_ref[...])
        yn_ref[...] = jnp.concatenate(outs, axis=1).astype(bf16)

    def rows(w):
        return pl.BlockSpec((CHUNK, w), lambda b, c: (b * n_chunk + c, 0))

    def par(w):
        return pl.BlockSpec((1, w), lambda b, c: (0, 0))

    return pl.pallas_call(
        body, name=name, grid=(t // SEQ, n_chunk),
        in_specs=[rows(CONV_CH), rows(SSM_INNER), rows(LANES), par(LANES), par(LANES), par(SSM_INNER), par(SSM_INNER)],
        out_specs=[rows(SSM_INNER), rows(SSM_INNER), pl.BlockSpec((1, SSM_GROUPS, D_STATE, GROUP_W), lambda b, c: (b * n_chunk + c, 0, 0, 0))],
        out_shape=[jax.ShapeDtypeStruct((t, SSM_INNER), bf16), jax.ShapeDtypeStruct((t, SSM_INNER), f32),
                   jax.ShapeDtypeStruct((t // CHUNK, SSM_GROUPS, D_STATE, GROUP_W), f32)],
        scratch_shapes=[pltpu.VMEM((SSM_GROUPS, D_STATE, GROUP_W), f32)],
        compiler_params=_cparams(("parallel", "arbitrary")),
    )(xbc, z, dtp, *params)


def _ssd_bwd(xbc, z, dtp, y, hs, dyn, params, name):
    t = xbc.shape[0]
    n_chunk = SEQ // CHUNK

    def body(xbc_ref, z_ref, dt_ref, y_ref, hs_ref, dyn_ref, bias_ref, alog_ref, dskip_ref, nw_ref,
             dxbc_ref, dz_ref, ddt_ref, dnw_ref, dds_ref, dal_ref, dbi_ref, dh_scr):
        @pl.when(pl.program_id(1) == 0)
        def _():
            dh_scr[...] = jnp.zeros_like(dh_scr)

        q = _ssd_common(xbc_ref, dt_ref, bias_ref, alog_ref)
        low = lax.broadcasted_iota(jnp.int32, (CHUNK, LANES), 1) < HEAD_DIM
        last_row = lax.broadcasted_iota(jnp.int32, (CHUNK, GROUP_W), 0) == CHUNK - 1
        xs, xg = q["xs"], q["xg"]
        xgb = xg.astype(bf16)
        wf = xg * q["dse"]
        wst = wf.astype(bf16)
        zz = z_ref[...]
        yy = y_ref[...]
        sz, dsz = _silu_and_grad(zz)
        y2, _, xhats, rs = _gate_norm(yy, zz, nw_ref[...], gate=sz)
        dyn_ = dyn_ref[...]
        dy2s, dnws = [], []
        for g in range(SSM_GROUPS):
            gl = slice(g * GROUP_W, (g + 1) * GROUP_W)
            gw = dyn_[:, gl] * nw_ref[:, gl]
            dy2s.append(rs[g] * (gw - xhats[g] * jnp.mean(gw * xhats[g], axis=-1, keepdims=True)))
            dnws.append(_rowsum8(dyn_[:, gl] * xhats[g]))
        dy2 = jnp.concatenate(dy2s, axis=1)
        dy = dy2 * sz
        dz_ref[...] = (dy2 * yy * dsz).astype(bf16)
        dnw_p = jnp.concatenate(dnws, axis=1)
        dds_p = _rowsum8(dy * xs)
        dyb = dy.astype(bf16)
        gfull = (dy * q["ecs"]).astype(bf16)
        dcs_c = jnp.zeros((CHUNK, CHUNK), f32)
        dcs_r = jnp.zeros((CHUNK, CHUNK), f32)
        dcs_e_parts, dxg_parts = [], []
        for g in range(SSM_GROUPS):
            gl = slice(g * GROUP_W, (g + 1) * GROUP_W)
            bsl = slice(SSM_INNER + g * D_STATE, SSM_INNER + (g + 1) * D_STATE)
            csl = slice(SSM_INNER + SSM_GROUPS * D_STATE + g * D_STATE, SSM_INNER + SSM_GROUPS * D_STATE + (g + 1) * D_STATE)
            bg = xbc_ref[:, bsl].astype(bf16)
            cg = xbc_ref[:, csl].astype(bf16)
            cb = _nt(cg, bg)
            hg = hs_ref[0, g]
            hgb = hg.astype(bf16)
            dhn = dh_scr[g]
            dhnb = dhn.astype(bf16)
            yoff = _nn(cg, hgb) * q["ecs"][:, gl]
            dw_ = _nn(bg, dhnb)
            r_e = dw_ * wf[:, gl]
            to_last = jnp.sum(r_e, axis=0, keepdims=True) + jnp.sum(dhn * hg, axis=0, keepdims=True) * q["cde"][:, gl]
            dcs_e_parts.append(dy[:, gl] * yoff - r_e + jnp.where(last_row, to_last, 0.0))
            dcb = jnp.zeros((CHUNK, CHUNK), f32)
            dxg_pairs = []
            for i in range(HEADS_PER_GROUP // 2):
                h0 = g * HEADS_PER_GROUP + 2 * i
                psl = slice(h0 * HEAD_DIM, (h0 + 2) * HEAD_DIM)
                xp = xgb[:, psl]
                dyp = dyb[:, psl]
                zero = jnp.zeros_like(dyp)
                tns = []
                for a in range(2):
                    h = h0 + a
                    lm = _decay_mat(q, h)
                    m = cb * lm
                    dm = _nt(jnp.where(low, dyp, zero) if a == 0 else jnp.where(low, zero, dyp), xp)
                    dcb = dcb + dm * lm
                    nmat = dm * m
                    dcs_c = dcs_c + jnp.where(q["cidx"] == h, jnp.sum(nmat, axis=1, keepdims=True), 0.0)
                    dcs_r = dcs_r + jnp.where(q["r"] == h, jnp.sum(nmat, axis=0, keepdims=True), 0.0)
                    tns.append(_tn(m.astype(bf16), dyp))
                dxg_pairs.append(jnp.where(low, tns[0], tns[1]))
            dxg_parts.append(jnp.concatenate(dxg_pairs, axis=1) + dw_ * q["dse"][:, gl])
            dcbb = dcb.astype(bf16)
            dxbc_ref[:, csl] = _nt(gfull[:, gl], hgb) + _nn(dcbb, bg)
            dxbc_ref[:, bsl] = _nt(wst[:, gl], dhnb) + _tn(dcbb, cg)
            dh_scr[g] = dhn * q["cde"][:, gl] + _tn(cg, gfull[:, gl])
        dxg = jnp.concatenate(dxg_parts, axis=1)
        dcs_e = jnp.concatenate(dcs_e_parts, axis=1)
        dxbc_ref[:, 0:SSM_INNER] = dskip_ref[...] * dy + dxg * q["dt_e"]
        dcs = dcs_c - dcs_r.T + _dot_exact(dcs_e, q["expand"], ((1,), (1,)))
        triu = (q["cidx"] >= q["r"]).astype(bf16)
        da = _dot_exact(dcs, triu, ((1,), (0,)), x_is_lhs=False)
        ddt = _dot_exact(dxg * xs, q["expand"], ((1,), (1,))) + da * q["a_neg"]
        ddtp = jnp.where(q["head_lane"], ddt * _sigmoid(q["dtp"]), 0.0)
        ddt_ref[...] = ddtp.astype(bf16)
        dal_p = _rowsum8(da * q["dt"]) * q["a_neg"]
        dbi_p = _rowsum8(ddtp)
        first = (pl.program_id(0) == 0) & (pl.program_id(1) == 0)

        @pl.when(first)
        def _():
            dnw_ref[...] = dnw_p
            dds_ref[...] = dds_p
            dal_ref[...] = dal_p
            dbi_ref[...] = dbi_p

        @pl.when(jnp.logical_not(first))
        def _():
            dnw_ref[...] += dnw_p
            dds_ref[...] += dds_p
            dal_ref[...] += dal_p
            dbi_ref[...] += dbi_p

    def rows(w):
        return pl.BlockSpec((CHUNK, w), lambda b, c: (b * n_chunk + n_chunk - 1 - c, 0))

    def par(w):
        return pl.BlockSpec((1, w), lambda b, c: (0, 0))

    def acc(w):
        return pl.BlockSpec((SUBLANES, w), lambda b, c: (0, 0))

    return pl.pallas_call(
        body, name=name, grid=(t // SEQ, n_chunk),
        in_specs=[rows(CONV_CH), rows(SSM_INNER), rows(LANES), rows(SSM_INNER),
                  pl.BlockSpec((1, SSM_GROUPS, D_STATE, GROUP_W), lambda b, c: (b * n_chunk + n_chunk - 1 - c, 0, 0, 0)),
                  rows(SSM_INNER), par(LANES), par(LANES), par(SSM_INNER), par(SSM_INNER)],
        out_specs=[rows(CONV_CH), rows(SSM_INNER), rows(LANES), acc(SSM_INNER), acc(SSM_INNER), acc(LANES), acc(LANES)],
        out_shape=[jax.ShapeDtypeStruct((t, CONV_CH), f32), jax.ShapeDtypeStruct((t, SSM_INNER), bf16), jax.ShapeDtypeStruct((t, LANES), bf16),
                   jax.ShapeDtypeStruct((SUBLANES, SSM_INNER), f32), jax.ShapeDtypeStruct((SUBLANES, SSM_INNER), f32),
                   jax.ShapeDtypeStruct((SUBLANES, LANES), f32), jax.ShapeDtypeStruct((SUBLANES, LANES), f32)],
        scratch_shapes=[pltpu.VMEM((SSM_GROUPS, D_STATE, GROUP_W), f32)],
        compiler_params=_cparams(("arbitrary", "arbitrary")),
    )(xbc, z, dtp, y, hs, dyn, *params)


def _adamw(g_parts, w, m, v, name, layer=None):
    rows, width = w.shape[-2:]
    n = len(g_parts)
    tr = _row_tile(rows)

    def body(*refs):
        g_refs, (w_ref, m_ref, v_ref, g_out, d_out, m_out, v_out) = refs[:n], refs[n:]

        def part(i):
            return (g_refs[i][...] if g_parts[i][1] is None else g_refs[i][0]).astype(f32)

        def state(ref):
            return ref[...] if layer is None else ref[0]

        g = part(0)
        for i in range(1, n):
            g = g + part(i)
        mm = ADAM_B1 * state(m_ref) + (1.0 - ADAM_B1) * g
        vv = ADAM_B2 * state(v_ref) + (1.0 - ADAM_B2) * (g * g)
        m_hat = mm / (1.0 - ADAM_B1 ** ADAM_STEP)
        v_hat = vv / (1.0 - ADAM_B2 ** ADAM_STEP)
        g_out[...] = g
        d_out[...] = -ADAM_LR * (m_hat / (jnp.sqrt(v_hat) + ADAM_EPS) + ADAM_WD * state(w_ref))
        m_out[...] = mm
        v_out[...] = vv

    spec = pl.BlockSpec((tr, width), lambda i: (i, 0))

    def lead(idx):
        return spec if idx is None else pl.BlockSpec((1, tr, width), lambda i: (idx, i, 0))

    return pl.pallas_call(
        body, name=name, grid=(rows // tr,), in_specs=[lead(idx) for _, idx in g_parts] + [lead(layer)] * 3, out_specs=[spec] * 4,
        out_shape=[jax.ShapeDtypeStruct((rows, width), f32)] * 4, compiler_params=_cparams(("parallel",)),
    )(*[a for a, _ in g_parts], w, m, v)


def _row_tile(rows, cap=512):
    for cand in range(min(rows, cap) // SUBLANES * SUBLANES, 0, -SUBLANES):
        if rows % cand == 0:
            return cand
    return rows


def _cols_from_devices(g, width, name):
    n_dev, depth, a, b = g.shape

    def body(g_ref, o_ref):
        for i in range(n_dev):
            o_ref[0, :, i * b:(i + 1) * b] = g_ref[i, 0]
        if width > n_dev * b:
            o_ref[0, :, n_dev * b:width] = jnp.zeros((a, width - n_dev * b), o_ref.dtype)

    return pl.pallas_call(
        body, name=name, grid=(depth,), in_specs=[pl.BlockSpec((n_dev, 1, a, b), lambda l: (0, l, 0, 0))],
        out_specs=pl.BlockSpec((1, a, width), lambda l: (l, 0, 0)), out_shape=jax.ShapeDtypeStruct((depth, a, width), g.dtype),
        compiler_params=_cparams(("parallel",)),
    )(g)


def _devices_from_cols(per_layer, b, name, tr=256):
    depth = len(per_layer)
    a, width = per_layer[0].shape

    def body(*refs):
        o_ref = refs[depth]
        for l in range(depth):
            for i in range(N_DEV):
                o_ref[i, l] = refs[l][:, i * b:(i + 1) * b]

    return pl.pallas_call(
        body, name=name, grid=(a // tr,), in_specs=[pl.BlockSpec((tr, width), lambda r: (r, 0))] * depth,
        out_specs=pl.BlockSpec((N_DEV, depth, tr, b), lambda r: (0, 0, r, 0)),
        out_shape=jax.ShapeDtypeStruct((N_DEV, depth, a, b), per_layer[0].dtype), compiler_params=_cparams(("parallel",)),
    )(*per_layer)


def _me():
    return lax.axis_index("x"), lax.axis_index("y"), lax.axis_index("c")


def _allgather_two_level(shards, name):
    n = len(shards)
    per = 7

    def body(*refs):
        ins, outs, token = refs[:n], refs[n:2 * n], refs[2 * n]
        send_sems, recv_sems, local_sems = refs[2 * n + 1:]
        token[...] = jnp.zeros_like(token)
        x, y, c = _me()
        me, sibling = (x, y, c), (x, y, 1 - c)
        chips = [(1 - x, y), (x, 1 - y), (1 - x, 1 - y)]

        def slot(a, p):
            return outs[a].at[4 * p[0] + 2 * p[1] + p[2]]

        def copy(a, k, block, to, src=None):
            return pltpu.make_async_remote_copy(
                src_ref=slot(a, block) if src is None else src, dst_ref=slot(a, block),
                send_sem=send_sems.at[a * per + k], recv_sem=recv_sems.at[a * per + k], device_id=to, device_id_type=MESH)

        mine = [pltpu.make_async_copy(ins[a], slot(a, me), local_sems.at[a]) for a in range(n)]
        for cp in mine:
            cp.start()
        first = []
        for a in range(n):
            first.append(copy(a, 0, me, sibling, src=ins[a]))
            first += [copy(a, 1 + j, me, (*chip, c), src=ins[a]) for j, chip in enumerate(chips)]
        for cp in first:
            cp.start()
        passed = []
        for j, chip in enumerate(chips):
            for a in range(n):
                copy(a, 1 + j, (*chip, c), me).wait_recv()
                fwd = copy(a, 4 + j, (*chip, c), sibling)
                fwd.start()
                passed.append(fwd)
        for a in range(n):
            copy(a, 0, sibling, me).wait_recv()
            for j, chip in enumerate(chips):
                copy(a, 4 + j, (*chip, 1 - c), me).wait_recv()
        for cp in first + passed:
            cp.wait_send()
        for cp in mine:
            cp.wait()

    outs = pl.pallas_call(
        body, name=name, in_specs=[ANY] * n, out_specs=[ANY] * n + [pl.BlockSpec(memory_space=pltpu.VMEM)],
        out_shape=[jax.ShapeDtypeStruct((N_DEV,) + s.shape, s.dtype) for s in shards] + [jax.ShapeDtypeStruct((SUBLANES, LANES), f32)],
        scratch_shapes=[pltpu.SemaphoreType.DMA((n * per,)), pltpu.SemaphoreType.DMA((n * per,)), pltpu.SemaphoreType.DMA((n,))],
    )(*shards)
    return outs[:n], outs[n]


def _allgather_direct(row, name):
    def body(in_ref, out_ref, send_sems, recv_sems, local_sem):
        x, y, c = _me()
        mine = out_ref.at[4 * x + 2 * y + c]
        local = pltpu.make_async_copy(in_ref, mine, local_sem)
        local.start()
        sends = []
        for k in range(1, N_DEV):
            px, py, pc = x ^ (k >> 2), y ^ ((k >> 1) & 1), c ^ (k & 1)
            sends.append(pltpu.make_async_remote_copy(
                src_ref=in_ref, dst_ref=mine, send_sem=send_sems.at[k - 1], recv_sem=recv_sems.at[k - 1],
                device_id=(px, py, pc), device_id_type=MESH))
        for cp in sends:
            cp.start()
        for k in range(1, N_DEV):
            px, py, pc = x ^ (k >> 2), y ^ ((k >> 1) & 1), c ^ (k & 1)
            theirs = out_ref.at[4 * px + 2 * py + pc]
            pltpu.make_async_remote_copy(
                src_ref=in_ref, dst_ref=theirs, send_sem=send_sems.at[k - 1], recv_sem=recv_sems.at[k - 1],
                device_id=(px, py, pc), device_id_type=MESH).wait_recv()
        for cp in sends:
            cp.wait_send()
        local.wait()

    return pl.pallas_call(
        body, name=name, in_specs=[ANY], out_specs=ANY, out_shape=jax.ShapeDtypeStruct((N_DEV,) + row.shape, row.dtype),
        scratch_shapes=[pltpu.SemaphoreType.DMA((N_DEV - 1,)), pltpu.SemaphoreType.DMA((N_DEV - 1,)), pltpu.SemaphoreType.DMA],
    )(row)


N_CHIP = N_DEV // 2
HBM = pl.BlockSpec(memory_space=pltpu.HBM)
SEM = pl.BlockSpec(memory_space=pltpu.SEMAPHORE)
EFFECT = pltpu.SideEffectType.DATAFLOW_SIDE_EFFECTING


def _peer(k):
    x, y, c = _me()
    return x ^ (k >> 2), y ^ ((k >> 1) & 1), c ^ (k & 1)


def _direct_copies(srcs, lands, send_sems, recv_sems, per_peer):
    x, y, c = _me()
    me = 4 * x + 2 * y + c
    copies = []
    for a in range(len(srcs)):
        for k in range(1, N_DEV):
            px, py, pc = _peer(k)
            piece = srcs[a].at[4 * px + 2 * py + pc] if per_peer else srcs[a]
            copies.append(pltpu.make_async_remote_copy(
                src_ref=piece, dst_ref=lands[a].at[me], send_sem=send_sems.at[a * (N_DEV - 1) + k - 1],
                recv_sem=recv_sems.at[a * (N_DEV - 1) + k - 1], device_id=(px, py, pc), device_id_type=MESH))
    return copies


def _direct_start(srcs, lands, per_peer, name):
    n = len(srcs)
    n_sem = n * (N_DEV - 1)

    def body(*refs):
        src_refs, land_refs = refs[:n], refs[n:2 * n]
        send_sems, recv_sems = refs[2 * n], refs[2 * n + 1]
        token = refs[-1]
        for cp in _direct_copies(src_refs, land_refs, send_sems, recv_sems, per_peer):
            cp.start()
        token[...] = jnp.zeros_like(token)

    outs = pl.pallas_call(
        body, name=name,
        out_shape=(pltpu.SemaphoreType.DMA((n_sem,)), pltpu.SemaphoreType.DMA((n_sem,)),
                   *[pltpu.HBM(s.shape, s.dtype) for s in srcs], *[pltpu.HBM(s.shape, s.dtype) for s in lands],
                   jax.ShapeDtypeStruct((SUBLANES, LANES), f32)),
        in_specs=[HBM] * (2 * n), out_specs=(SEM, SEM, *[HBM] * (2 * n), pl.BlockSpec(memory_space=pltpu.VMEM)),
        input_output_aliases={i: 2 + i for i in range(2 * n)},
        compiler_params=pltpu.CompilerParams(has_side_effects=EFFECT),
    )(*[pltpu.with_memory_space_constraint(s, pltpu.HBM) for s in srcs], *[pltpu.with_memory_space_constraint(s, pltpu.HBM) for s in lands])
    return outs[0], outs[1], outs[2:2 + n], outs[2 + n:2 + 2 * n], outs[-1]


def _direct_wait(send_sems, recv_sems, srcs, lands, after, per_peer, name):
    n = len(srcs)

    def body(*refs):
        src_refs, land_refs = refs[:n], refs[n:2 * n]
        s_sems, r_sems = refs[2 * n], refs[2 * n + 1]
        for cp in _direct_copies(src_refs, land_refs, s_sems, r_sems, per_peer):
            cp.wait_send()
            cp.wait_recv()

    outs = pl.pallas_call(
        body, name=name,
        out_shape=tuple(pltpu.HBM(s.shape, s.dtype) for s in list(srcs) + list(lands)),
        in_specs=[HBM] * (2 * n) + [SEM, SEM, ANY], out_specs=tuple([HBM] * (2 * n)),
        input_output_aliases={i: i for i in range(2 * n)},
        compiler_params=pltpu.CompilerParams(has_side_effects=EFFECT),
    )(*srcs, *lands, send_sems, recv_sems, after)
    return outs[n:]


def _row(v, width=None):
    v = v.reshape(1, -1).astype(f32)
    if width is not None and v.shape[1] < width:
        v = jnp.pad(v, ((0, 0), (0, width - v.shape[1])))
    return v


def _layer_params(p, l):
    return dict(
        norm_mix=_row(p["norm_mix"][l]), norm_ffn=_row(p["norm_ffn"][l]), conv_w=p["conv_w"][l], conv_b=_row(p["conv_b"][l]),
        ssd=(_row(p["dt_bias"][l], LANES), _row(p["a_log"][l], LANES), _row(jnp.repeat(p["d_skip"][l], HEAD_DIM)), _row(p["ssm_norm"][l])))


def _layer_fwd(h, w_in, rest, sp, tabs, l):
    tag = f"l{l}_"
    hn = _rmsnorm_fwd(h, sp["norm_mix"], tag + "norm_mix")
    qkv = _matmul(hn, w_in, mode="nn", n_out=QKV_WIDTH, tn=256, b_off=0, name=tag + "proj_qkv")
    z = _matmul(hn, w_in, mode="nn", n_out=SSM_INNER, tn=256, b_off=Z_OFF // 256, name=tag + "proj_z")
    xbc_pre = _matmul(hn, w_in, mode="nn", n_out=CONV_CH, tn=256, b_off=XBC_OFF // 256, name=tag + "proj_xbc")
    dtp = _matmul(hn, w_in, mode="nn", n_out=LANES, tn=LANES, b_off=DT_OFF // LANES, name=tag + "proj_dt")
    o, lse = _attn_fwd(qkv, tabs, tag + "attn_fwd")
    xbc = _conv_fwd(xbc_pre, sp["conv_w"], sp["conv_b"], tag + "conv_fwd")
    yn, y, hs = _ssd_fwd(xbc, z, dtp, sp["ssd"], tag + "ssd_fwd")
    w_out, w_gate, w_up, w_down = rest(yn) if callable(rest) else rest
    h2 = _out_proj(o, yn, w_out, h, tag + "out_proj")
    hn2 = _rmsnorm_fwd(h2, sp["norm_ffn"], tag + "norm_ffn")
    g, u, act = _swiglu_fwd(hn2, w_gate, w_up, tag + "ffn_up")
    h3 = _matmul(act, w_down, mode="nn", tk=1408, add=h2, name=tag + "ffn_down")
    saved = dict(h=h, hn=hn, qkv=qkv, z=z, xbc_pre=xbc_pre, dtp=dtp, o=o, lse=lse, xbc=xbc, yn=yn, y=y, hs=hs, h2=h2, hn2=hn2, g=g, u=u, act=act,
                 rest=(w_out, w_gate, w_up, w_down))
    return h3, saved


def _layer_bwd(dh3, s, big, sp, tabs, l, gd=f32, after_ffn=None):
    tag = f"l{l}_"
    w_in, w_out, w_gate, w_up, w_down = big
    dg, du = _swiglu_bwd(dh3, w_down, s["g"], s["u"], tag + "ffn_down_bwd")
    dw_down = _matmul(s["act"], dh3, mode="tn", tm=1408, tn=512, tk=2048, out_dtype=gd, name=tag + "dw_down")
    dw_gate = _matmul(s["hn2"], dg, mode="tn", tm=512, tn=1408, tk=2048, out_dtype=gd, name=tag + "dw_gate")
    dw_up = _matmul(s["hn2"], du, mode="tn", tm=512, tn=1408, tk=2048, out_dtype=gd, name=tag + "dw_up")
    norm_ffn = sp["norm_ffn"] if after_ffn is None else sp["norm_ffn"] + after_ffn(dict(w_gate=dw_gate, w_up=dw_up, w_down=dw_down))
    dh2, dnf = _nt_norm_bwd([(dg, w_gate), (du, w_up)], s["h2"], norm_ffn, dh3, tag + "ffn_up_bwd_norm", tk=1408)
    d_o = _matmul(dh2, w_out, mode="nt", n_out=ATTN_WIDTH, tn=512, b_off=0, name=tag + "out_attn_bwd")
    dyn = _matmul(dh2, w_out, mode="nt", n_out=SSM_INNER, tn=512, b_off=1, name=tag + "out_ssm_bwd")
    dw_out = jnp.concatenate([_matmul(s["o"], dh2, mode="tn", tm=512, tn=512, tk=2048, out_dtype=gd, name=tag + "dw_out_attn"),
                              _matmul(s["yn"], dh2, mode="tn", tm=512, tn=512, tk=2048, out_dtype=gd, name=tag + "dw_out_ssm")], axis=0)
    dxbc, dz, ddtp, dnw, dds, dal, dbi = _ssd_bwd(s["xbc"], s["z"], s["dtp"], s["y"], s["hs"], dyn, sp["ssd"], tag + "ssd_bwd")
    dxbc_pre, dconv_w, dconv_b = _conv_bwd(s["xbc_pre"], sp["conv_w"], sp["conv_b"], dxbc, tag + "conv_bwd")
    dq, dk, dv = _attn_bwd(s["qkv"], tabs, s["o"], s["lse"], d_o, tag + "attn_bwd")
    dproj = jnp.concatenate([dq, dk, dv, dz, dxbc_pre, ddtp], axis=1)
    dw_in = _matmul(s["hn"], dproj, mode="tn", tm=512, tn=1152, tk=2048, out_dtype=gd, name=tag + "dw_in")
    dh, dnm = _nt_norm_bwd([(dproj, w_in)], s["h"], sp["norm_mix"], dh2, tag + "proj_bwd_norm", tk=1152)
    grads = dict(
        norm_mix=dnm.sum(0), w_in=dw_in, conv_w=dconv_w, conv_b=dconv_b[0], dt_bias=dbi.sum(0)[:SSM_HEADS], a_log=dal.sum(0)[:SSM_HEADS],
        d_skip=dds.sum(0).reshape(SSM_HEADS, HEAD_DIM).sum(1), ssm_norm=dnw.sum(0), w_out=dw_out, norm_ffn=dnf.sum(0),
        w_gate=dw_gate, w_up=dw_up, w_down=dw_down)
    return dh, grads


def _local_step(x, positions, target, p, bigs):
    tabs = _rope_tables(positions.reshape(-1, 1), "rope_tables")
    h = x
    saved, sps = [], []
    for l in range(DEPTH):
        sps.append(_layer_params(p, l))
        h, s = _layer_fwd(h, bigs[l][0], bigs[l][1:], sps[l], tabs, l)
        saved.append(s)
    dh, loss_parts, dfn = _final_loss(h, _row(p["final_norm"]), target, "final_loss")
    layer_grads = [None] * DEPTH
    for l in reversed(range(DEPTH)):
        dh, layer_grads[l] = _layer_bwd(dh, saved[l], bigs[l], sps[l], tabs, l)
    grads = {k: [layer_grads[l][k] for l in range(DEPTH)] for k in layer_grads[0]}
    grads["final_norm"] = dfn.sum(0)
    return jnp.sum(loss_parts), dh, grads


BIG = ("w_in", "w_out", "w_gate", "w_up", "w_down")
REST = BIG[1:]
FFN = ("w_gate", "w_up", "w_down")
MIX = ("w_in", "w_out")
COL_SHARDED = ("w_in", "w_gate", "w_up")
SMALL = ("norm_mix", "conv_b", "dt_bias", "a_log", "d_skip", "ssm_norm", "norm_ffn", "final_norm")
WEIGHTS = ("norm_mix", "w_in", "conv_w", "conv_b", "dt_bias", "a_log", "d_skip", "ssm_norm", "w_out", "norm_ffn", "w_gate", "w_up", "w_down", "final_norm")
PACK_W = 1024
SMALL_ROWS = 88
CONVW_ROWS = 96
CONVW_SHARD_ROWS = 16


def _full_from_gathered(name, g, l):
    _, a, b = g.shape
    if name in COL_SHARDED:
        width = IN_PROJ_PAD if name == "w_in" else N_DEV * b
        return _cols_from_devices(g.reshape(N_DEV, 1, a, b), width, f"cols_l{l}_{name}").reshape(a, width)
    return g.reshape(N_DEV * a, b)


def _by_device(name, full, shard_shape, l):
    a, b = shard_shape
    if name in COL_SHARDED:
        return _devices_from_cols([full], b, f"devs_l{l}_{name}").reshape(N_CHIP, 2, a, b)
    return full.reshape(N_CHIP, 2, a, b)


def _pack_rows(parts, rows, width):
    flat = jnp.concatenate([q.reshape(-1) for q in parts])
    return jnp.pad(flat, (0, rows * width - flat.shape[0])).reshape(rows, width)


def _unpack(flat, like):
    out, off = [], 0
    for q in like:
        out.append(flat[off:off + q.size].reshape(q.shape))
        off += q.size
    return out


def kernel(x, positions, norm_mix, w_in, conv_w, conv_b, dt_bias, a_log, d_skip, ssm_norm, w_out, norm_ffn, w_gate, w_up, w_down, final_norm, loss_target, m_norm_mix, m_w_in, m_conv_w, m_conv_b, m_dt_bias, m_a_log, m_d_skip, m_ssm_norm, m_w_out, m_norm_ffn, m_w_gate, m_w_up, m_w_down, m_final_norm, v_norm_mix, v_w_in, v_conv_w, v_conv_b, v_dt_bias, v_a_log, v_d_skip, v_ssm_norm, v_w_out, v_norm_ffn, v_w_gate, v_w_up, v_w_down, v_final_norm):
    w = dict(norm_mix=norm_mix, w_in=w_in, conv_w=conv_w, conv_b=conv_b, dt_bias=dt_bias, a_log=a_log, d_skip=d_skip, ssm_norm=ssm_norm,
             w_out=w_out, norm_ffn=norm_ffn, w_gate=w_gate, w_up=w_up, w_down=w_down, final_norm=final_norm)
    m = dict(norm_mix=m_norm_mix, w_in=m_w_in, conv_w=m_conv_w, conv_b=m_conv_b, dt_bias=m_dt_bias, a_log=m_a_log, d_skip=m_d_skip,
             ssm_norm=m_ssm_norm, w_out=m_w_out, norm_ffn=m_norm_ffn, w_gate=m_w_gate, w_up=m_w_up, w_down=m_w_down, final_norm=m_final_norm)
    v = dict(norm_mix=v_norm_mix, w_in=v_w_in, conv_w=v_conv_w, conv_b=v_conv_b, dt_bias=v_dt_bias, a_log=v_a_log, d_skip=v_d_skip,
             ssm_norm=v_ssm_norm, w_out=v_w_out, norm_ffn=v_norm_ffn, w_gate=v_w_gate, w_up=v_w_up, w_down=v_w_down, final_norm=v_final_norm)
    ax, ay, ac = lax.axis_index("x"), lax.axis_index("y"), lax.axis_index("c")
    dev = 4 * ax + 2 * ay + ac

    assert DEPTH == 2
    t = x.shape[0] * x.shape[1]
    xf, target = x.reshape(t, D_MODEL), loss_target.reshape(t, D_MODEL)

    def own_slot(block):
        return jnp.broadcast_to(block, (N_DEV,) + block.shape[1:])

    def gather_start(keys, l, tie, name):
        shards = [(w[keys[0]][l] + tie).astype(bf16)] + [w[k][l].astype(bf16) for k in keys[1:]]
        return _direct_start(shards, [own_slot(s[None]) for s in shards], False, name)

    def scatter_start(keys, grads_l, l, name):
        by_dev = [_by_device(k, grads_l[k], w[k].shape[1:], l).reshape((N_DEV,) + w[k].shape[1:]) for k in keys]
        return _direct_start(by_dev, [own_slot(lax.dynamic_slice_in_dim(g, dev, 1, 0)) for g in by_dev], True, name)

    (g_in0, conv_all), tie = _allgather_two_level([w["w_in"][0].astype(bf16), w["conv_w"]], "gather_l0_w_in")
    rest0_copy = gather_start(REST, 0, tie[0, 0], "gather_l0_rest_start")
    l1_copy = gather_start(BIG, 1, rest0_copy[4][0, 0], "gather_l1_start")
    p = {k: w[k] for k in SMALL}
    p["norm_mix"] = p["norm_mix"] + l1_copy[4][0, 0]
    p["conv_w"] = jnp.transpose(conv_all, (1, 2, 0, 3)).reshape(DEPTH, CONV_WIDTH, CONV_CH)
    sp0, sp1 = _layer_params(p, 0), _layer_params(p, 1)

    def rest0(after):
        lands = _direct_wait(*rest0_copy[:4], after, False, "gather_l0_rest_wait")
        return tuple(_full_from_gathered(k, g, 0) for k, g in zip(REST, lands))

    tabs = _rope_tables(positions.reshape(t, 1), "rope_tables")
    w_in0 = _full_from_gathered("w_in", g_in0, 0)
    h1, saved0 = _layer_fwd(xf, w_in0, rest0, sp0, tabs, 0)
    lands1 = _direct_wait(*l1_copy[:4], h1, False, "gather_l1_wait")
    bigs1 = tuple(_full_from_gathered(k, g, 1) for k, g in zip(BIG, lands1))
    h2, saved1 = _layer_fwd(h1, bigs1[0], bigs1[1:], sp1, tabs, 1)
    dh, loss_parts, dfn = _final_loss(h2, _row(p["final_norm"]), target, "final_loss")
    loss_local = jnp.sum(loss_parts)

    dh, grads1 = _layer_bwd(dh, saved1, bigs1, sp1, tabs, 1, gd=bf16)
    l1_grads = scatter_start(BIG, grads1, 1, "scatter_l1_start")
    w_out0, w_gate0, w_up0, w_down0 = saved0["rest"]
    bigs0 = (w_in0, w_out0, w_gate0, w_up0, w_down0 + l1_grads[4][0, 0].astype(bf16))
    ffn0_grads = []

    def after_ffn(grads_ffn):
        ffn0_grads.append(scatter_start(FFN, grads_ffn, 0, "scatter_l0_ffn_start"))
        return ffn0_grads[0][4][0, 0]

    dx, grads0 = _layer_bwd(dh, saved0, bigs0, sp0, tabs, 0, gd=bf16, after_ffn=after_ffn)
    mix0_grads = scatter_start(MIX, grads0, 0, "scatter_l0_mix_start")
    landed = {(k, 1): g for k, g in zip(BIG, _direct_wait(*l1_grads[:4], dx, True, "scatter_l1_wait"))}
    landed.update({(k, 0): g for k, g in zip(FFN, _direct_wait(*ffn0_grads[0][:4], dx, True, "scatter_l0_ffn_wait"))})
    landed.update({(k, 0): g for k, g in zip(MIX, _direct_wait(*mix0_grads[:4], mix0_grads[4], True, "scatter_l0_mix_wait"))})
    out_g, out_d, out_m, out_v = {}, {}, {}, {}
    for k in BIG:
        res = [_adamw([(landed[k, l], i) for i in range(N_DEV)], w[k], m[k], v[k], f"adamw_l{l}_{k}", layer=l) for l in range(DEPTH)]
        for dst, r0, r1 in zip((out_g, out_d, out_m, out_v), res[0], res[1]):
            dst[k] = jnp.stack([r0, r1])
    grads = {k: [grads0[k], grads1[k]] for k in grads0 if k not in BIG}
    grads["final_norm"] = dfn.sum(0)

    small_like = [w[k] for k in SMALL]
    small_grads = [jnp.stack(grads[k]) if k != "final_norm" else grads[k] for k in SMALL]
    small_pack = jnp.concatenate([_pack_rows(small_grads, SMALL_ROWS, LANES), _pack_rows([jnp.stack(grads["conv_w"])], CONVW_ROWS, LANES)], axis=0)
    parts = _allgather_direct(small_pack, "gather_small_grads")
    g_s, d_s, m_s, v_s = _adamw(
        [(parts[i, :SMALL_ROWS], None) for i in range(N_DEV)], _pack_rows(small_like, SMALL_ROWS, LANES),
        _pack_rows([m[k] for k in SMALL], SMALL_ROWS, LANES), _pack_rows([v[k] for k in SMALL], SMALL_ROWS, LANES), "adamw_replicated")
    for dst, src in ((out_g, g_s), (out_d, d_s), (out_m, m_s), (out_v, v_s)):
        dst.update(zip(SMALL, _unpack(src.reshape(-1), small_like)))
    shard_w = conv_w.shape[-1]
    conv_parts = parts[:, SMALL_ROWS:].reshape(N_DEV, DEPTH, CONV_WIDTH, CONV_CH)
    conv_mine = lax.dynamic_slice_in_dim(conv_parts, dev * shard_w, shard_w, axis=3)
    g_c, d_c, m_c, v_c = _adamw(
        [(_pack_rows([conv_mine[i]], CONVW_SHARD_ROWS, LANES), None) for i in range(N_DEV)], _pack_rows([conv_w], CONVW_SHARD_ROWS, LANES),
        _pack_rows([m["conv_w"]], CONVW_SHARD_ROWS, LANES), _pack_rows([v["conv_w"]], CONVW_SHARD_ROWS, LANES), "adamw_conv_w")
    for dst, src in ((out_g, g_c), (out_d, d_c), (out_m, m_c), (out_v, v_c)):
        dst["conv_w"] = src.reshape(-1)[:conv_w.size].reshape(conv_w.shape)

    loss = lax.psum(loss_local, ("x", "y", "c"))
    return (loss, dx.reshape(x.shape), *[out_g[k] for k in WEIGHTS], *[out_d[k] for k in WEIGHTS],
            *[out_m[k] for k in WEIGHTS], *[out_v[k] for k in WEIGHTS])
```

```python
import functools
import math

import jax
import jax.numpy as jnp
import numpy as np
from jax import lax
from jax.experimental import pallas as pl
from jax.experimental.pallas import tpu as pltpu

f32 = jnp.float32
bf16 = jnp.bfloat16

D_MODEL = 1024
SEQ = 2048
DEPTH = 2
HEAD_DIM = 64
N_ATTN_HEADS = 8
N_KV_HEADS = 2
ATTN_WIDTH = 512
KV_WIDTH = 128
ROPE_DIM = 16
ROPE_THETA = 500000.0
DILATIONS = (1, 4, 16)
ATTN_BLOCK = 128
SSM_HEADS = 16
SSM_INNER = 1024
SSM_GROUPS = 2
D_STATE = 128
CONV_WIDTH = 4
CHUNK = 128
CONV_CH = 1536
MIX_WIDTH = 1536
QKV_WIDTH = ATTN_WIDTH + 2 * KV_WIDTH
Z_OFF = 768
XBC_OFF = 1792
DT_OFF = 3328
IN_PROJ = 3344
IN_PROJ_PAD = 3456
FFN_HIDDEN = 2816
EPS = 1e-5
N_DEV = 8
ADAM_LR = 0.001
ADAM_B1 = 0.9
ADAM_B2 = 0.999
ADAM_EPS = 1e-08
ADAM_WD = 0.01
ADAM_STEP = 10

LANES = 128
SUBLANES = 8
VMEM_LIMIT = 56 * 1024 * 1024

MESH = pl.DeviceIdType.MESH
ANY = pl.BlockSpec(memory_space=pl.ANY)


def _cparams(sem, vmem=None):
    return pltpu.CompilerParams(dimension_semantics=sem, vmem_limit_bytes=vmem or VMEM_LIMIT)


def _sigmoid(x):
    return 1.0 / (1.0 + jnp.exp(-x))


def _silu(x):
    return x * _sigmoid(x)


def _dsilu(x):
    s = _sigmoid(x)
    return s * (1.0 + x * (1.0 - s))


def _silu_and_grad(x):
    s = _sigmoid(x)
    return x * s, s * (1.0 + x * (1.0 - s))


def _softplus(x):
    return jnp.maximum(x, 0.0) + jnp.log(1.0 + jnp.exp(-jnp.abs(x)))


def _dot(a, b, dims, precision=None):
    return lax.dot_general(a, b, (dims, ((), ())), preferred_element_type=f32, precision=precision)


def _nn(a, b, precision=None):
    return _dot(a, b, ((1,), (0,)), precision)


def _nt(a, b):
    return _dot(a, b, ((1,), (1,)))


def _tn(a, b):
    return _dot(a, b, ((0,), (0,)))


def _rowsum8(t):
    n, w = t.shape
    return jnp.sum(t.reshape(n // SUBLANES, SUBLANES, w), axis=0)


def _matmul(a, b, *, mode, n_out=None, b_off=0, a_koff=0, b_koff=0, k_len=None, add=None, out_dtype=f32, tm=2048, tn=512, tk=1024, name):
    if mode == "tn":
        kdim_a, m = a.shape
    else:
        m, kdim_a = a.shape
    kk = k_len if k_len is not None else kdim_a
    n = n_out if n_out is not None else (b.shape[0] if mode == "nt" else b.shape[1])
    tm, tn, tk = min(tm, m), min(tn, n), min(tk, kk)
    assert m % tm == 0 and n % tn == 0 and kk % tk == 0, (name, m, n, kk, tm, tn, tk)
    nk = kk // tk
    if mode == "nn":
        a_spec = pl.BlockSpec((tm, tk), lambda i, j, k: (i, k + a_koff))
        b_spec = pl.BlockSpec((tk, tn), lambda i, j, k: (k + b_koff, j + b_off))
        dims = ((1,), (0,))
    elif mode == "nt":
        a_spec = pl.BlockSpec((tm, tk), lambda i, j, k: (i, k + a_koff))
        b_spec = pl.BlockSpec((tn, tk), lambda i, j, k: (j + b_off, k + b_koff))
        dims = ((1,), (1,))
    else:
        a_spec = pl.BlockSpec((tk, tm), lambda i, j, k: (k + a_koff, i))
        b_spec = pl.BlockSpec((tk, tn), lambda i, j, k: (k + b_koff, j + b_off))
        dims = ((0,), (0,))
    o_spec = pl.BlockSpec((tm, tn), lambda i, j, k: (i, j))
    has_add = add is not None

    def body(*refs):
        if has_add:
            a_ref, b_ref, add_ref, o_ref, acc_ref = refs
        else:
            a_ref, b_ref, o_ref, acc_ref = refs
        k = pl.program_id(2)
        part = _dot(a_ref[...].astype(bf16), b_ref[...].astype(bf16), dims)

        @pl.when(k == 0)
        def _():
            acc_ref[...] = part

        @pl.when(k > 0)
        def _():
            acc_ref[...] += part

        @pl.when(k == nk - 1)
        def _():
            r = acc_ref[...]
            if has_add:
                r = r + add_ref[...]
            o_ref[...] = r.astype(out_dtype)

    in_specs = [a_spec, b_spec] + ([o_spec] if has_add else [])
    args = (a, b) + ((add,) if has_add else ())
    return pl.pallas_call(
        body, name=name, grid=(m // tm, n // tn, nk), in_specs=in_specs, out_specs=o_spec,
        out_shape=jax.ShapeDtypeStruct((m, n), out_dtype), scratch_shapes=[pltpu.VMEM((tm, tn), f32)],
        compiler_params=_cparams(("parallel", "parallel", "arbitrary")),
    )(*args)


def _out_proj(o, yn, w_out, h, name, tm=2048, tn=512):
    m, kb = o.shape
    n = w_out.shape[1]
    n_y = yn.shape[1] // kb
    assert yn.shape[1] % kb == 0 and w_out.shape[0] == kb * (1 + n_y)

    def body(*refs):
        o_ref, y_refs, w_refs, h_ref, out_ref = refs[0], refs[1:1 + n_y], refs[1 + n_y:2 + 2 * n_y], refs[-2], refs[-1]
        acc = h_ref[...] + _nn(o_ref[...].astype(bf16), w_refs[0][...])
        for y_ref, w_ref in zip(y_refs, w_refs[1:]):
            acc = acc + _nn(y_ref[...], w_ref[...])
        out_ref[...] = acc

    res = pl.BlockSpec((tm, tn), lambda i, j: (i, j))

    def a_blk(c):
        return pl.BlockSpec((tm, kb), lambda i, j: (i, c))

    def w_blk(r):
        return pl.BlockSpec((kb, tn), lambda i, j: (r, j))

    return pl.pallas_call(
        body, name=name, grid=(m // tm, n // tn),
        in_specs=[a_blk(0)] + [a_blk(c) for c in range(n_y)] + [w_blk(r) for r in range(1 + n_y)] + [res],
        out_specs=res, out_shape=jax.ShapeDtypeStruct((m, n), f32), compiler_params=_cparams(("parallel", "parallel")),
    )(o, *[yn] * n_y, *[w_out] * (1 + n_y), h)


def _swiglu_fwd(hn, w_gate, w_up, name, tm=2048, tn=256):
    m, k = hn.shape
    n = w_gate.shape[1]

    def body(a_ref, wg_ref, wu_ref, g_ref, u_ref, act_ref):
        a = a_ref[...]
        g = _nn(a, wg_ref[...])
        u = _nn(a, wu_ref[...])
        sg, dsg = _silu_and_grad(g)
        g_ref[...] = (u * dsg).astype(bf16)
        u_ref[...] = sg.astype(bf16)
        act_ref[...] = (sg * u).astype(bf16)

    a_spec = pl.BlockSpec((tm, k), lambda i, j: (i, 0))
    w_spec = pl.BlockSpec((k, tn), lambda i, j: (0, j))
    o_spec = pl.BlockSpec((tm, tn), lambda i, j: (i, j))
    return pl.pallas_call(
        body, name=name, grid=(m // tm, n // tn), in_specs=[a_spec, w_spec, w_spec], out_specs=[o_spec, o_spec, o_spec],
        out_shape=[jax.ShapeDtypeStruct((m, n), bf16)] * 3,
        compiler_params=_cparams(("parallel", "parallel")),
    )(hn, w_gate, w_up)


def _swiglu_bwd(dh, w_down, g, u, name, tm=2048, tn=256):
    m, k = dh.shape
    n = w_down.shape[0]

    def body(a_ref, w_ref, g_ref, u_ref, dg_ref, du_ref):
        dact = _nt(a_ref[...].astype(bf16), w_ref[...])
        dg_ref[...] = (dact * g_ref[...].astype(f32)).astype(bf16)
        du_ref[...] = (dact * u_ref[...].astype(f32)).astype(bf16)

    a_spec = pl.BlockSpec((tm, k), lambda i, j: (i, 0))
    w_spec = pl.BlockSpec((tn, k), lambda i, j: (j, 0))
    o_spec = pl.BlockSpec((tm, tn), lambda i, j: (i, j))
    return pl.pallas_call(
        body, name=name, grid=(m // tm, n // tn), in_specs=[a_spec, w_spec, o_spec, o_spec], out_specs=[o_spec, o_spec],
        out_shape=[jax.ShapeDtypeStruct((m, n), bf16), jax.ShapeDtypeStruct((m, n), bf16)],
        compiler_params=_cparams(("parallel", "parallel")),
    )(dh, w_down, g, u)


def _rmsnorm_fwd(h, w, name, tm=512):
    m, d = h.shape

    def body(h_ref, w_ref, o_ref):
        x = h_ref[...]
        r = lax.rsqrt(jnp.mean(x * x, axis=-1, keepdims=True) + EPS)
        o_ref[...] = (x * r * w_ref[...]).astype(bf16)

    return pl.pallas_call(
        body, name=name, grid=(m // tm,),
        in_specs=[pl.BlockSpec((tm, d), lambda i: (i, 0)), pl.BlockSpec((1, d), lambda i: (0, 0))],
        out_specs=pl.BlockSpec((tm, d), lambda i: (i, 0)), out_shape=jax.ShapeDtypeStruct((m, d), bf16),
        compiler_params=_cparams(("parallel",)),
    )(h, w)


def _rmsnorm_bwd(dhn, h, w, dres, name, tm=512):
    m, d = h.shape

    def body(dhn_ref, h_ref, w_ref, dres_ref, dh_ref, dw_ref):
        x = h_ref[...]
        r = lax.rsqrt(jnp.mean(x * x, axis=-1, keepdims=True) + EPS)
        xhat = x * r
        dy = dhn_ref[...]
        gw = dy * w_ref[...]
        dh_ref[...] = dres_ref[...] + r * (gw - xhat * jnp.mean(gw * xhat, axis=-1, keepdims=True))
        part = _rowsum8(dy * xhat)

        @pl.when(pl.program_id(0) == 0)
        def _():
            dw_ref[...] = part

        @pl.when(pl.program_id(0) > 0)
        def _():
            dw_ref[...] += part

    row = pl.BlockSpec((tm, d), lambda i: (i, 0))
    return pl.pallas_call(
        body, name=name, grid=(m // tm,),
        in_specs=[row, row, pl.BlockSpec((1, d), lambda i: (0, 0)), row],
        out_specs=[row, pl.BlockSpec((SUBLANES, d), lambda i: (0, 0))],
        out_shape=[jax.ShapeDtypeStruct((m, d), f32), jax.ShapeDtypeStruct((SUBLANES, d), f32)],
        compiler_params=_cparams(("arbitrary",)),
    )(dhn, h, w, dres)


def _nt_norm_bwd(pairs, h, w, dres, name, tm=1024, tk=704):
    m, d = h.shape
    steps = [p[0].shape[1] // tk for p in pairs]
    assert all(p[0].shape[1] % tk == 0 for p in pairs), (name, tk)
    starts = [sum(steps[:i]) for i in range(len(pairs))]
    nk = sum(steps)
    n_p = len(pairs)

    def body(*refs):
        ab = refs[:2 * n_p]
        h_ref, w_ref, dres_ref, dh_ref, dw_ref, acc_ref = refs[2 * n_p:]
        i, k = pl.program_id(0), pl.program_id(1)

        @pl.when(k == 0)
        def _():
            acc_ref[...] = jnp.zeros_like(acc_ref)

        for p in range(n_p):
            @pl.when((k >= starts[p]) & (k < starts[p] + steps[p]))
            def _(p=p):
                acc_ref[...] += _nt(ab[2 * p][...], ab[2 * p + 1][...])

        @pl.when(k == nk - 1)
        def _():
            x = h_ref[...]
            r = lax.rsqrt(jnp.mean(x * x, axis=-1, keepdims=True) + EPS)
            xhat = x * r
            dy = acc_ref[...]
            gw = dy * w_ref[...]
            dh_ref[...] = dres_ref[...] + r * (gw - xhat * jnp.mean(gw * xhat, axis=-1, keepdims=True))
            part = _rowsum8(dy * xhat)

            @pl.when(i == 0)
            def _():
                dw_ref[...] = part

            @pl.when(i > 0)
            def _():
                dw_ref[...] += part

    def clamp(k, p):
        return jnp.clip(k - starts[p], 0, steps[p] - 1)

    in_specs = []
    for p in range(n_p):
        in_specs += [pl.BlockSpec((tm, tk), lambda i, k, p=p: (i, clamp(k, p))), pl.BlockSpec((d, tk), lambda i, k, p=p: (0, clamp(k, p)))]
    row = pl.BlockSpec((tm, d), lambda i, k: (i, 0))
    in_specs += [row, pl.BlockSpec((1, d), lambda i, k: (0, 0)), row]
    return pl.pallas_call(
        body, name=name, grid=(m // tm, nk), in_specs=in_specs,
        out_specs=[row, pl.BlockSpec((SUBLANES, d), lambda i, k: (0, 0))],
        out_shape=[jax.ShapeDtypeStruct((m, d), f32), jax.ShapeDtypeStruct((SUBLANES, d), f32)],
        scratch_shapes=[pltpu.VMEM((tm, d), f32)], compiler_params=_cparams(("arbitrary", "arbitrary")),
    )(*[t for p in pairs for t in p], h, w, dres)


def _final_loss(h, w, target, name, tm=512):
    m, d = h.shape

    def body(h_ref, w_ref, t_ref, dh_ref, loss_ref, dw_ref):
        x = h_ref[...]
        r = lax.rsqrt(jnp.mean(x * x, axis=-1, keepdims=True) + EPS)
        xhat = x * r
        ww = w_ref[...]
        err = xhat * ww - t_ref[...]
        dy = err * (1.0 / d)
        gw = dy * ww
        dh_ref[...] = r * (gw - xhat * jnp.mean(gw * xhat, axis=-1, keepdims=True))
        lpart = _rowsum8(err * err) * (0.5 / d)
        wpart = _rowsum8(dy * xhat)

        @pl.when(pl.program_id(0) == 0)
        def _():
            loss_ref[...] = lpart
            dw_ref[...] = wpart

        @pl.when(pl.program_id(0) > 0)
        def _():
            loss_ref[...] += lpart
            dw_ref[...] += wpart

    row = pl.BlockSpec((tm, d), lambda i: (i, 0))
    acc = pl.BlockSpec((SUBLANES, d), lambda i: (0, 0))
    return pl.pallas_call(
        body, name=name, grid=(m // tm,),
        in_specs=[row, pl.BlockSpec((1, d), lambda i: (0, 0)), row], out_specs=[row, acc, acc],
        out_shape=[jax.ShapeDtypeStruct((m, d), f32), jax.ShapeDtypeStruct((SUBLANES, d), f32), jax.ShapeDtypeStruct((SUBLANES, d), f32)],
        compiler_params=_cparams(("arbitrary",)),
    )(h, w, target)


def _lane_tables():
    f = np.arange(LANES) % HEAD_DIM
    inv = ROPE_THETA ** (-jnp.arange(0, ROPE_DIM, 2, dtype=f32) / ROPE_DIM)
    invf = jnp.where(f < ROPE_DIM, inv[f % (ROPE_DIM // 2)], 0.0).astype(f32)
    return invf.reshape(1, LANES)


def _rope_tables(pos_col, name):
    t = pos_col.shape[0]
    tm = SEQ

    def body(p_ref, f_ref, c_ref, s1_ref, s2_ref):
        ang = p_ref[...].astype(f32) * f_ref[...]
        co, si = jnp.cos(ang), jnp.sin(ang)
        f = lax.broadcasted_iota(jnp.int32, (tm, LANES), 1) % HEAD_DIM
        c_ref[...] = jnp.where(f < ROPE_DIM, co, 1.0)
        s1_ref[...] = jnp.where(f < ROPE_DIM // 2, -si, 0.0)
        s2_ref[...] = jnp.where((f >= ROPE_DIM // 2) & (f < ROPE_DIM), si, 0.0)

    row = pl.BlockSpec((tm, LANES), lambda i: (i, 0))
    return pl.pallas_call(
        body, name=name, grid=(t // tm,),
        in_specs=[pl.BlockSpec((tm, 1), lambda i: (i, 0)), pl.BlockSpec((1, LANES), lambda i: (0, 0))],
        out_specs=[row, row, row], out_shape=[jax.ShapeDtypeStruct((t, LANES), f32)] * 3,
        compiler_params=_cparams(("parallel",)),
    )(pos_col, _lane_tables())


def _rot(x, c, s1, s2):
    return x * c + pltpu.roll(x, LANES - ROPE_DIM // 2, 1) * s1 + pltpu.roll(x, ROPE_DIM // 2, 1) * s2


def _rot_t(g, c, s1, s2):
    return g * c + pltpu.roll(g * s1, ROPE_DIM // 2, 1) + pltpu.roll(g * s2, LANES - ROPE_DIM // 2, 1)


def _dup_head(x, kvh, low):
    a = jnp.where(kvh == 0, x, pltpu.roll(x, HEAD_DIM, 1))
    return jnp.where(low, a, pltpu.roll(a, HEAD_DIM, 1))


def _deinterleave(src_ref, dst_ref, d, dtype):
    length = SEQ // d
    if d == 1:
        dst_ref[...] = src_ref[...].astype(dtype)
    else:
        for r in range(d):
            dst_ref[pl.ds(r * length, length), :] = src_ref[pl.ds(r, length, stride=d), :].astype(dtype)


def _interleave_store(src_ref, dst_ref, d, accumulate):
    length = SEQ // d
    if d == 1:
        if accumulate:
            dst_ref[...] += src_ref[...]
        else:
            dst_ref[...] = src_ref[...]
    else:
        for r in range(d):
            blk = src_ref[pl.ds(r * length, length), :]
            if accumulate:
                dst_ref[pl.ds(r, length, stride=d), :] = dst_ref[pl.ds(r, length, stride=d), :] + blk
            else:
                dst_ref[pl.ds(r, length, stride=d), :] = blk


def _attn_masks():
    qi = lax.broadcasted_iota(jnp.int32, (ATTN_BLOCK, ATTN_BLOCK), 0)
    ki = lax.broadcasted_iota(jnp.int32, (ATTN_BLOCK, ATTN_BLOCK), 1)
    low = lax.broadcasted_iota(jnp.int32, (ATTN_BLOCK, LANES), 1) < HEAD_DIM
    return ki <= qi, ki >= qi, low


NEG_INF = float("-inf")
ATTN_UNROLL = 4


N_BRANCH = len(DILATIONS)


def _attn_prep(qkv, tabs, name):
    t = qkv.shape[0]
    nb = t // SEQ
    n_j = ATTN_WIDTH // LANES

    def q_body(q_ref, c_ref, s1_ref, s2_ref, out_ref, xr):
        xr[...] = _rot(q_ref[...], c_ref[...], s1_ref[...], s2_ref[...]) * (HEAD_DIM ** -0.5)
        for bi, d in enumerate(DILATIONS):
            _deinterleave(xr, out_ref.at[bi], d, bf16)

    def kv_body(x_ref, c_ref, s1_ref, s2_ref, out_ref, xr):
        lowfull = lax.broadcasted_iota(jnp.int32, (SEQ, LANES), 1) < HEAD_DIM
        x = x_ref[...]
        x = jnp.where(pl.program_id(1) == 0, _rot(x, c_ref[...], s1_ref[...], s2_ref[...]), x)
        for kvh in range(N_KV_HEADS):
            xr[...] = _dup_head(x, kvh, lowfull)
            for bi, d in enumerate(DILATIONS):
                length = SEQ // d
                for r in range(d):
                    rows = xr[...] if d == 1 else xr[pl.ds(r, length, stride=d), :]
                    out_ref[0, bi, pl.ds(r * length, length), kvh * LANES:(kvh + 1) * LANES] = rows.astype(bf16)

    tab = pl.BlockSpec((SEQ, LANES), lambda b, j: (b, 0))
    q = pl.pallas_call(
        q_body, name=name + "_q", grid=(nb, n_j),
        in_specs=[pl.BlockSpec((SEQ, LANES), lambda b, j: (b, j)), tab, tab, tab],
        out_specs=pl.BlockSpec((N_BRANCH, SEQ, LANES), lambda b, j: (0, b, j)),
        out_shape=jax.ShapeDtypeStruct((N_BRANCH, t, ATTN_WIDTH), bf16), scratch_shapes=[pltpu.VMEM((SEQ, LANES), f32)],
        compiler_params=_cparams(("parallel", "parallel")),
    )(qkv, *tabs)
    kv = pl.pallas_call(
        kv_body, name=name + "_kv", grid=(nb, 2),
        in_specs=[pl.BlockSpec((SEQ, LANES), lambda b, j: (b, n_j + j)), tab, tab, tab],
        out_specs=pl.BlockSpec((1, N_BRANCH, SEQ, N_KV_HEADS * LANES), lambda b, j: (j, 0, b, 0)),
        out_shape=jax.ShapeDtypeStruct((2, N_BRANCH, t, N_KV_HEADS * LANES), bf16), scratch_shapes=[pltpu.VMEM((SEQ, LANES), f32)],
        compiler_params=_cparams(("parallel", "parallel")),
    )(qkv, *tabs)
    return q, kv


def _attn_fwd(prep, name):
    q_all, kv_all = prep
    t = q_all.shape[1]
    nb = t // SEQ
    n_blk = SEQ // ATTN_BLOCK

    def body(q_ref, k_ref, v_ref, o_ref, lse_ref, ob, lb, o0, o1, o2, l0, l1, l2, ss):
        cur_ok, prev_ok, low = _attn_masks()
        onat, lnat = (o0, o1, o2), (l0, l1, l2)
        for bi, d in enumerate(DILATIONS):
            qd, kd, vd = q_ref.at[bi], k_ref.at[0, bi], v_ref.at[0, bi]
            per_res = n_blk // d
            use_prev = per_res > 1

            def scores(n, carry):
                start = pl.multiple_of(n * ATTN_BLOCK, ATTN_BLOCK)
                has_prev = (n % per_res) != 0
                pstart = pl.multiple_of(jnp.maximum(n - 1, 0) * ATTN_BLOCK, ATTN_BLOCK)
                qb = qd[pl.ds(start, ATTN_BLOCK), :]
                kc = kd[pl.ds(start, ATTN_BLOCK), :]
                if use_prev:
                    kp = kd[pl.ds(pstart, ATTN_BLOCK), :]
                for a in range(2):
                    qa = jnp.where(low if a == 0 else ~low, qb, jnp.zeros_like(qb))
                    ss[2 * n + a, :, 0:ATTN_BLOCK] = jnp.where(cur_ok, _nt(qa, kc), NEG_INF)
                    if use_prev:
                        ss[2 * n + a, :, ATTN_BLOCK:2 * ATTN_BLOCK] = jnp.where(prev_ok & has_prev, _nt(qa, kp), NEG_INF)
                return carry

            def softmax_pv(n, carry):
                start = pl.multiple_of(n * ATTN_BLOCK, ATTN_BLOCK)
                pstart = pl.multiple_of(jnp.maximum(n - 1, 0) * ATTN_BLOCK, ATTN_BLOCK)
                vc = vd[pl.ds(start, ATTN_BLOCK), :]
                if use_prev:
                    vp = vd[pl.ds(pstart, ATTN_BLOCK), :]
                outs, lses = [], []
                for a in range(2):
                    sc = ss[2 * n + a, :, 0:ATTN_BLOCK]
                    if use_prev:
                        sp = ss[2 * n + a, :, ATTN_BLOCK:2 * ATTN_BLOCK]
                        m = jnp.max(jnp.maximum(sc, sp), axis=1, keepdims=True)
                        pc, pp = jnp.exp(sc - m), jnp.exp(sp - m)
                        den = jnp.sum(pc + pp, axis=1, keepdims=True)
                        acc = _nn(pc.astype(bf16), vc) + _nn(pp.astype(bf16), vp)
                    else:
                        m = jnp.max(sc, axis=1, keepdims=True)
                        pc = jnp.exp(sc - m)
                        den = jnp.sum(pc, axis=1, keepdims=True)
                        acc = _nn(pc.astype(bf16), vc)
                    outs.append(acc * (1.0 / den))
                    lses.append(m + jnp.log(den))
                ob[pl.ds(start, ATTN_BLOCK), :] = jnp.where(low, outs[0], outs[1])
                lb[pl.ds(start, ATTN_BLOCK), :] = jnp.where(low, lses[0], lses[1])
                return carry

            lax.fori_loop(0, n_blk, scores, 0, unroll=ATTN_UNROLL)
            lax.fori_loop(0, n_blk, softmax_pv, 0, unroll=ATTN_UNROLL)
            _interleave_store(ob, onat[bi], d, False)
            _interleave_store(lb, lnat[bi], d, False)
        la, lbb, lc = l0[...], l1[...], l2[...]
        lm = jnp.maximum(jnp.maximum(la, lbb), lc)
        wa, wb, wc = jnp.exp(la - lm), jnp.exp(lbb - lm), jnp.exp(lc - lm)
        ws = wa + wb + wc
        o_ref[...] = (wa * o0[...] + wb * o1[...] + wc * o2[...]) / ws
        lse_ref[...] = lm + jnp.log(ws)

    def col(jj):
        return pl.BlockSpec((SEQ, LANES), lambda b, j: (b, jj if jj is not None else j))

    fs = pltpu.VMEM((SEQ, LANES), f32)
    return pl.pallas_call(
        body, name=name, grid=(nb, ATTN_WIDTH // LANES),
        in_specs=[pl.BlockSpec((N_BRANCH, SEQ, LANES), lambda b, j: (0, b, j)),
                  pl.BlockSpec((1, N_BRANCH, SEQ, LANES), lambda b, j: (0, 0, b, j // 2)),
                  pl.BlockSpec((1, N_BRANCH, SEQ, LANES), lambda b, j: (1, 0, b, j // 2))],
        out_specs=[col(None), col(None)],
        out_shape=[jax.ShapeDtypeStruct((t, ATTN_WIDTH), f32), jax.ShapeDtypeStruct((t, ATTN_WIDTH), f32)],
        scratch_shapes=[fs, fs, fs, fs, fs, fs, fs, fs, pltpu.VMEM((2 * n_blk, ATTN_BLOCK, 2 * ATTN_BLOCK), f32)],
        compiler_params=_cparams(("parallel", "parallel")),
    )(q_all, kv_all, kv_all)


def _attn_bwd(prep, tabs, o, lse, do, name):
    q_all, kv_all = prep
    t = q_all.shape[1]
    nb = t // SEQ
    n_blk = SEQ // ATTN_BLOCK
    n_j = ATTN_WIDTH // LANES

    def body(q_ref, k_ref, v_ref, c_ref, s1_ref, s2_ref, o_ref, lse_ref, do_ref, dq_ref, dk_ref, dv_ref,
             dl, dod, lsd, dld, dqd, dkd, dvd, dqa, dka, dva, pb, dsb, dk_acc, dv_acc):
        j = pl.program_id(1)
        pb[2 * n_blk:2 * n_blk + 2] = jnp.zeros((2, ATTN_BLOCK, 2 * ATTN_BLOCK), bf16)
        dsb[2 * n_blk:2 * n_blk + 2] = jnp.zeros((2, ATTN_BLOCK, 2 * ATTN_BLOCK), bf16)
        kvh = j // 2
        cur_ok, prev_ok, low = _attn_masks()
        lowfull = lax.broadcasted_iota(jnp.int32, (SEQ, LANES), 1) < HEAD_DIM
        c, s1, s2 = c_ref[...], s1_ref[...], s2_ref[...]
        prod = do_ref[...] * o_ref[...]
        d_lo = jnp.sum(jnp.where(lowfull, prod, 0.0), axis=1, keepdims=True)
        d_hi = jnp.sum(jnp.where(lowfull, 0.0, prod), axis=1, keepdims=True)
        dl[...] = jnp.where(lowfull, d_lo, d_hi)
        dqa[...] = jnp.zeros_like(dqa)
        dka[...] = jnp.zeros_like(dka)
        dva[...] = jnp.zeros_like(dva)
        for bi, d in enumerate(DILATIONS):
            qd, kd, vd = q_ref.at[bi], k_ref.at[0, bi], v_ref.at[0, bi]
            _deinterleave(do_ref, dod, d, bf16)
            _deinterleave(lse_ref, lsd, d, f32)
            _deinterleave(dl, dld, d, f32)
            per_res = n_blk // d
            use_prev = per_res > 1
            curl, prevl = slice(0, ATTN_BLOCK), slice(ATTN_BLOCK, 2 * ATTN_BLOCK)

            def halves(x):
                zero = jnp.zeros_like(x)
                return jnp.where(low, x, zero), jnp.where(low, zero, x)

            def probs(n, carry):
                start = pl.multiple_of(n * ATTN_BLOCK, ATTN_BLOCK)
                has_prev = (n % per_res) != 0
                pstart = pl.multiple_of(jnp.maximum(n - 1, 0) * ATTN_BLOCK, ATTN_BLOCK)
                cur, prev = pl.ds(start, ATTN_BLOCK), pl.ds(pstart, ATTN_BLOCK)
                qas, doas = halves(qd[cur, :]), halves(dod[cur, :])
                kc, vc = kd[cur, :], vd[cur, :]
                if use_prev:
                    kp, vp = kd[prev, :], vd[prev, :]
                lsb, dlb = lsd[cur, :], dld[cur, :]
                for a in range(2):
                    ls = lsb[:, a * HEAD_DIM:a * HEAD_DIM + 1]
                    de = dlb[:, a * HEAD_DIM:a * HEAD_DIM + 1]
                    pc = jnp.exp(jnp.where(cur_ok, _nt(qas[a], kc), NEG_INF) - ls)
                    pb[2 * n + a, :, curl] = pc.astype(bf16)
                    dsb[2 * n + a, :, curl] = (pc * (_nt(doas[a], vc) - de)).astype(bf16)
                    if use_prev:
                        pp = jnp.exp(jnp.where(prev_ok & has_prev, _nt(qas[a], kp), NEG_INF) - ls)
                        pb[2 * n + a, :, prevl] = pp.astype(bf16)
                        dsb[2 * n + a, :, prevl] = (pp * (_nt(doas[a], vp) - de)).astype(bf16)
                return carry

            def grads(n, carry):
                start = pl.multiple_of(n * ATTN_BLOCK, ATTN_BLOCK)
                pstart = pl.multiple_of(jnp.maximum(n - 1, 0) * ATTN_BLOCK, ATTN_BLOCK)
                nstart = pl.multiple_of(jnp.minimum(n + 1, n_blk - 1) * ATTN_BLOCK, ATTN_BLOCK)
                cur, prev, nxt = pl.ds(start, ATTN_BLOCK), pl.ds(pstart, ATTN_BLOCK), pl.ds(nstart, ATTN_BLOCK)
                kc = kd[cur, :]
                dqs = [_nn(dsb[2 * n + a, :, curl], kc) for a in range(2)]
                q_rows, do_rows = list(halves(qd[cur, :])), list(halves(dod[cur, :]))
                ds_rows, p_rows = [dsb[2 * n + a, :, curl] for a in range(2)], [pb[2 * n + a, :, curl] for a in range(2)]
                if use_prev:
                    kp = kd[prev, :]
                    dqs = [dqs[a] + _nn(dsb[2 * n + a, :, prevl], kp) for a in range(2)]
                    q_rows += list(halves(qd[nxt, :]))
                    do_rows += list(halves(dod[nxt, :]))
                    ds_rows += [dsb[2 * n + 2 + a, :, prevl] for a in range(2)]
                    p_rows += [pb[2 * n + 2 + a, :, prevl] for a in range(2)]
                dqd[cur, :] = jnp.where(low, dqs[0], dqs[1])
                dkd[cur, :] = _tn(jnp.concatenate(ds_rows, axis=0), jnp.concatenate(q_rows, axis=0))
                dvd[cur, :] = _tn(jnp.concatenate(p_rows, axis=0), jnp.concatenate(do_rows, axis=0))
                return carry

            lax.fori_loop(0, n_blk, probs, 0, unroll=ATTN_UNROLL)
            lax.fori_loop(0, n_blk, grads, 0, unroll=ATTN_UNROLL)
            _interleave_store(dqd, dqa, d, True)
            _interleave_store(dkd, dka, d, True)
            _interleave_store(dvd, dva, d, True)
        dq_ref[...] = _rot_t(dqa[...] * (HEAD_DIM ** -0.5), c, s1, s2).astype(bf16)
        dkf = dka[...]
        dkf = _rot_t(dkf + pltpu.roll(dkf, HEAD_DIM, 1), c, s1, s2)
        dvf = dva[...]
        dvf = dvf + pltpu.roll(dvf, HEAD_DIM, 1)
        mine = (lax.broadcasted_iota(jnp.int32, (SEQ, LANES), 1) // HEAD_DIM) == kvh
        dkc_, dvc_ = jnp.where(mine, dkf, 0.0), jnp.where(mine, dvf, 0.0)

        @pl.when(j == 0)
        def _():
            dk_acc[...] = dkc_
            dv_acc[...] = dvc_

        @pl.when(j > 0)
        def _():
            dk_acc[...] += dkc_
            dv_acc[...] += dvc_

        @pl.when(j == n_j - 1)
        def _():
            dk_ref[...] = dk_acc[...].astype(bf16)
            dv_ref[...] = dv_acc[...].astype(bf16)

    def col(jj):
        return pl.BlockSpec((SEQ, LANES), lambda b, j: (b, jj if jj is not None else j))

    tab = pl.BlockSpec((SEQ, LANES), lambda b, j: (b, 0))
    fs = pltpu.VMEM((SEQ, LANES), f32)
    hs = pltpu.VMEM((SEQ, LANES), bf16)
    return pl.pallas_call(
        body, name=name, grid=(nb, n_j),
        in_specs=[pl.BlockSpec((N_BRANCH, SEQ, LANES), lambda b, j: (0, b, j)),
                  pl.BlockSpec((1, N_BRANCH, SEQ, LANES), lambda b, j: (0, 0, b, j // 2)),
                  pl.BlockSpec((1, N_BRANCH, SEQ, LANES), lambda b, j: (1, 0, b, j // 2)),
                  tab, tab, tab, col(None), col(None), col(None)],
        out_specs=[col(None), tab, tab],
        out_shape=[jax.ShapeDtypeStruct((t, ATTN_WIDTH), bf16), jax.ShapeDtypeStruct((t, LANES), bf16), jax.ShapeDtypeStruct((t, LANES), bf16)],
        scratch_shapes=[fs, hs, fs, fs, fs, fs, fs, fs, fs, fs,
                        pltpu.VMEM((2 * n_blk + 2, ATTN_BLOCK, 2 * ATTN_BLOCK), bf16), pltpu.VMEM((2 * n_blk + 2, ATTN_BLOCK, 2 * ATTN_BLOCK), bf16), fs, fs],
        compiler_params=_cparams(("parallel", "arbitrary")),
    )(q_all, kv_all, kv_all, *tabs, o, lse, do)


def _conv_pre(x, w_ref, b_ref, row):
    shifted = [x] + [jnp.where(row >= s, pltpu.roll(x, s, 0), 0.0) for s in range(1, CONV_WIDTH)]
    pre = b_ref[...] + w_ref[CONV_WIDTH - 1:CONV_WIDTH, :] * x
    for s in range(1, CONV_WIDTH):
        pre = pre + w_ref[CONV_WIDTH - 1 - s:CONV_WIDTH - s, :] * shifted[s]
    return pre, shifted


def _conv_fwd(x, w, b, name, tc=512):
    t, ch = x.shape

    def body(x_ref, w_ref, b_ref, o_ref):
        row = lax.broadcasted_iota(jnp.int32, (SEQ, tc), 0)
        pre, _ = _conv_pre(x_ref[...], w_ref, b_ref, row)
        o_ref[...] = _silu(pre)

    xs = pl.BlockSpec((SEQ, tc), lambda i, j: (i, j))
    return pl.pallas_call(
        body, name=name, grid=(t // SEQ, ch // tc),
        in_specs=[xs, pl.BlockSpec((CONV_WIDTH, tc), lambda i, j: (0, j)), pl.BlockSpec((1, tc), lambda i, j: (0, j))],
        out_specs=xs, out_shape=jax.ShapeDtypeStruct((t, ch), f32),
        compiler_params=_cparams(("parallel", "parallel")),
    )(x, w, b)


def _conv_bwd(x, w, b, dact, name, tc=512):
    t, ch = x.shape

    def body(x_ref, w_ref, b_ref, d_ref, dx_ref, dw_ref, db_ref):
        row = lax.broadcasted_iota(jnp.int32, (SEQ, tc), 0)
        pre, shifted = _conv_pre(x_ref[...], w_ref, b_ref, row)
        dpre = d_ref[...] * _dsilu(pre)
        dx = w_ref[CONV_WIDTH - 1:CONV_WIDTH, :] * dpre
        for s in range(1, CONV_WIDTH):
            dx = dx + w_ref[CONV_WIDTH - 1 - s:CONV_WIDTH - s, :] * jnp.where(row < SEQ - s, pltpu.roll(dpre, SEQ - s, 0), 0.0)
        dx_ref[...] = dx.astype(bf16)
        first = pl.program_id(1) == 0
        parts = [jnp.sum(dpre * shifted[CONV_WIDTH - 1 - k], axis=0, keepdims=True) for k in range(CONV_WIDTH)]
        dbp = jnp.sum(dpre, axis=0, keepdims=True)

        @pl.when(first)
        def _():
            for k in range(CONV_WIDTH):
                dw_ref[k:k + 1, :] = parts[k]
            db_ref[...] = dbp

        @pl.when(jnp.logical_not(first))
        def _():
            for k in range(CONV_WIDTH):
                dw_ref[k:k + 1, :] += parts[k]
            db_ref[...] += dbp

    xs = pl.BlockSpec((SEQ, tc), lambda j, i: (i, j))
    ws = pl.BlockSpec((CONV_WIDTH, tc), lambda j, i: (0, j))
    bs = pl.BlockSpec((1, tc), lambda j, i: (0, j))
    return pl.pallas_call(
        body, name=name, grid=(ch // tc, t // SEQ),
        in_specs=[xs, ws, bs, xs], out_specs=[xs, ws, bs],
        out_shape=[jax.ShapeDtypeStruct((t, ch), bf16), jax.ShapeDtypeStruct((CONV_WIDTH, ch), f32), jax.ShapeDtypeStruct((1, ch), f32)],
        compiler_params=_cparams(("parallel", "arbitrary")),
    )(x, w, b, dact)


GROUP_W = SSM_INNER // SSM_GROUPS
HEADS_PER_GROUP = SSM_HEADS // SSM_GROUPS


def _split3(x):
    hi = x.astype(bf16)
    r1 = x - hi.astype(f32)
    mid = r1.astype(bf16)
    lo = (r1 - mid.astype(f32)).astype(bf16)
    return hi, mid, lo


def _dot_exact(x, sel, dims, x_is_lhs=True):
    parts = _split3(x)
    if x_is_lhs:
        return _dot(parts[0], sel, dims) + _dot(parts[1], sel, dims) + _dot(parts[2], sel, dims)
    return _dot(sel, parts[0], dims) + _dot(sel, parts[1], dims) + _dot(sel, parts[2], dims)


def _ssd_common(xbc_ref, dt_ref, bias_ref, alog_ref):
    r = lax.broadcasted_iota(jnp.int32, (CHUNK, CHUNK), 0)
    cidx = lax.broadcasted_iota(jnp.int32, (CHUNK, CHUNK), 1)
    causal = r >= cidx
    tril = causal.astype(bf16)
    expand = (lax.broadcasted_iota(jnp.int32, (CHUNK, SSM_INNER), 0)
              == lax.broadcasted_iota(jnp.int32, (CHUNK, SSM_INNER), 1) // HEAD_DIM).astype(bf16)
    head_lane = cidx < SSM_HEADS
    dtp = dt_ref[...] + bias_ref[...]
    dt = jnp.where(head_lane, _softplus(dtp), 0.0)
    a_neg = -jnp.exp(alog_ref[...])
    a = dt * a_neg
    nn_dims = ((1,), (0,))
    cs = _dot_exact(a, tril, nn_dims, x_is_lhs=False)
    dt_e = _dot_exact(dt, expand, nn_dims)
    cs_e = _dot_exact(cs, expand, nn_dims)
    xs = xbc_ref[:, 0:SSM_INNER]
    xg = xs * dt_e
    ecs = jnp.exp(cs_e)
    cs_last = cs_e[CHUNK - 1:CHUNK, :]
    dse = jnp.exp(cs_last - cs_e)
    cde = jnp.exp(cs_last)
    return dict(r=r, cidx=cidx, causal=causal, tril=tril, expand=expand, head_lane=head_lane, dtp=dtp, dt=dt, a_neg=a_neg,
                cs=cs, cst=cs.T, dt_e=dt_e, cs_e=cs_e, xs=xs, xg=xg, ecs=ecs, dse=dse, cde=cde)


def _decay_mat(q, h):
    return jnp.exp(jnp.where(q["causal"], q["cs"][:, h:h + 1] - q["cst"][h:h + 1, :], NEG_INF))


def _gate_norm(y, z, nw, gate=None):
    y2 = y * (_silu(z) if gate is None else gate)
    outs, xhats, rs = [], [], []
    for g in range(SSM_GROUPS):
        sl = slice(g * GROUP_W, (g + 1) * GROUP_W)
        yg = y2[:, sl]
        r = lax.rsqrt(jnp.mean(yg * yg, axis=-1, keepdims=True) + EPS)
        xhats.append(yg * r)
        rs.append(r)
        outs.append(yg * r * nw[:, sl])
    return y2, outs, xhats, rs


def _ssd_fwd(xbc, z, dtp, params, name):
    t = xbc.shape[0]
    n_chunk = SEQ // CHUNK
    low = None

    def body(xbc_ref, z_ref, dt_ref, bias_ref, alog_ref, dskip_ref, nw_ref, yn_ref, y_ref, hs_ref, h_scr):
        @pl.when(pl.program_id(1) == 0)
        def _():
            h_scr[...] = jnp.zeros_like(h_scr)

        q = _ssd_common(xbc_ref, dt_ref, bias_ref, alog_ref)
        low = lax.broadcasted_iota(jnp.int32, (CHUNK, LANES), 1) < HEAD_DIM
        xgb = q["xg"].astype(bf16)
        wst = (q["xg"] * q["dse"]).astype(bf16)
        hs_ref[0] = h_scr[...]
        ys = []
        for g in range(SSM_GROUPS):
            gl = slice(g * GROUP_W, (g + 1) * GROUP_W)
            bg = xbc_ref[:, SSM_INNER + g * D_STATE:SSM_INNER + (g + 1) * D_STATE].astype(bf16)
            cg = xbc_ref[:, SSM_INNER + SSM_GROUPS * D_STATE + g * D_STATE:SSM_INNER + SSM_GROUPS * D_STATE + (g + 1) * D_STATE].astype(bf16)
            cb = _nt(cg, bg)
            hg = h_scr[g]
            yoff = _nn(cg, hg.astype(bf16)) * q["ecs"][:, gl]
            pieces = []
            for i in range(HEADS_PER_GROUP // 2):
                h0 = g * HEADS_PER_GROUP + 2 * i
                xp = xgb[:, h0 * HEAD_DIM:(h0 + 2) * HEAD_DIM]
                m0 = (cb * _decay_mat(q, h0)).astype(bf16)
                m1 = (cb * _decay_mat(q, h0 + 1)).astype(bf16)
                zero = jnp.zeros_like(xp)
                pieces.append(_nn(m0, jnp.where(low, xp, zero)) + _nn(m1, jnp.where(low, zero, xp)))
            ys.append(jnp.concatenate(pieces, axis=1) + yoff + dskip_ref[:, gl] * q["xs"][:, gl])
            h_scr[g] = hg * q["cde"][:, gl] + _tn(bg, wst[:, gl])
        y = jnp.concatenate(ys, axis=1)
        y_ref[...] = y
        _, outs, _, _ = _gate_norm(y, z_ref[...], nw_ref[...])
        yn_ref[...] = jnp.concatenate(outs, axis=1).astype(bf16)

    def rows(w):
        return pl.BlockSpec((CHUNK, w), lambda b, c: (b * n_chunk + c, 0))

    def par(w):
        return pl.BlockSpec((1, w), lambda b, c: (0, 0))

    return pl.pallas_call(
        body, name=name, grid=(t // SEQ, n_chunk),
        in_specs=[rows(CONV_CH), rows(SSM_INNER), rows(LANES), par(LANES), par(LANES), par(SSM_INNER), par(SSM_INNER)],
        out_specs=[rows(SSM_INNER), rows(SSM_INNER), pl.BlockSpec((1, SSM_GROUPS, D_STATE, GROUP_W), lambda b, c: (b * n_chunk + c, 0, 0, 0))],
        out_shape=[jax.ShapeDtypeStruct((t, SSM_INNER), bf16), jax.ShapeDtypeStruct((t, SSM_INNER), f32),
                   jax.ShapeDtypeStruct((t // CHUNK, SSM_GROUPS, D_STATE, GROUP_W), f32)],
        scratch_shapes=[pltpu.VMEM((SSM_GROUPS, D_STATE, GROUP_W), f32)],
        compiler_params=_cparams(("parallel", "arbitrary")),
    )(xbc, z, dtp, *params)


def _ssd_bwd(xbc, z, dtp, y, hs, dyn, params, name):
    t = xbc.shape[0]
    n_chunk = SEQ // CHUNK

    def body(xbc_ref, z_ref, dt_ref, y_ref, hs_ref, dyn_ref, bias_ref, alog_ref, dskip_ref, nw_ref,
             dxbc_ref, dz_ref, ddt_ref, dnw_ref, dds_ref, dal_ref, dbi_ref, dh_scr):
        @pl.when(pl.program_id(1) == 0)
        def _():
            dh_scr[...] = jnp.zeros_like(dh_scr)

        q = _ssd_common(xbc_ref, dt_ref, bias_ref, alog_ref)
        low = lax.broadcasted_iota(jnp.int32, (CHUNK, LANES), 1) < HEAD_DIM
        last_row = lax.broadcasted_iota(jnp.int32, (CHUNK, GROUP_W), 0) == CHUNK - 1
        xs, xg = q["xs"], q["xg"]
        xgb = xg.astype(bf16)
        wf = xg * q["dse"]
        wst = wf.astype(bf16)
        zz = z_ref[...]
        yy = y_ref[...]
        sz, dsz = _silu_and_grad(zz)
        y2, _, xhats, rs = _gate_norm(yy, zz, nw_ref[...], gate=sz)
        dyn_ = dyn_ref[...]
        dy2s, dnws = [], []
        for g in range(SSM_GROUPS):
            gl = slice(g * GROUP_W, (g + 1) * GROUP_W)
            gw = dyn_[:, gl] * nw_ref[:, gl]
            dy2s.append(rs[g] * (gw - xhats[g] * jnp.mean(gw * xhats[g], axis=-1, keepdims=True)))
            dnws.append(_rowsum8(dyn_[:, gl] * xhats[g]))
        dy2 = jnp.concatenate(dy2s, axis=1)
        dy = dy2 * sz
        dz_ref[...] = (dy2 * yy * dsz).astype(bf16)
        dnw_p = jnp.concatenate(dnws, axis=1)
        dds_p = _rowsum8(dy * xs)
        dyb = dy.astype(bf16)
        gfull = (dy * q["ecs"]).astype(bf16)
        dcs_c = jnp.zeros((CHUNK, CHUNK), f32)
        dcs_r = jnp.zeros((CHUNK, CHUNK), f32)
        dcs_e_parts, dxg_parts = [], []
        for g in range(SSM_GROUPS):
            gl = slice(g * GROUP_W, (g + 1) * GROUP_W)
            bsl = slice(SSM_INNER + g * D_STATE, SSM_INNER + (g + 1) * D_STATE)
            csl = slice(SSM_INNER + SSM_GROUPS * D_STATE + g * D_STATE, SSM_INNER + SSM_GROUPS * D_STATE + (g + 1) * D_STATE)
            bg = xbc_ref[:, bsl].astype(bf16)
            cg = xbc_ref[:, csl].astype(bf16)
            cb = _nt(cg, bg)
            hg = hs_ref[0, g]
            hgb = hg.astype(bf16)
            dhn = dh_scr[g]
            dhnb = dhn.astype(bf16)
            yoff = _nn(cg, hgb) * q["ecs"][:, gl]
            dw_ = _nn(bg, dhnb)
            r_e = dw_ * wf[:, gl]
            to_last = jnp.sum(r_e, axis=0, keepdims=True) + jnp.sum(dhn * hg, axis=0, keepdims=True) * q["cde"][:, gl]
            dcs_e_parts.append(dy[:, gl] * yoff - r_e + jnp.where(last_row, to_last, 0.0))
            dcb = jnp.zeros((CHUNK, CHUNK), f32)
            dxg_pairs = []
            for i in range(HEADS_PER_GROUP // 2):
                h0 = g * HEADS_PER_GROUP + 2 * i
                psl = slice(h0 * HEAD_DIM, (h0 + 2) * HEAD_DIM)
                xp = xgb[:, psl]
                dyp = dyb[:, psl]
                zero = jnp.zeros_like(dyp)
                tns = []
                for a in range(2):
                    h = h0 + a
                    lm = _decay_mat(q, h)
                    m = cb * lm
                    dm = _nt(jnp.where(low, dyp, zero) if a == 0 else jnp.where(low, zero, dyp), xp)
                    dcb = dcb + dm * lm
                    nmat = dm * m
                    dcs_c = dcs_c + jnp.where(q["cidx"] == h, jnp.sum(nmat, axis=1, keepdims=True), 0.0)
                    dcs_r = dcs_r + jnp.where(q["r"] == h, jnp.sum(nmat, axis=0, keepdims=True), 0.0)
                    tns.append(_tn(m.astype(bf16), dyp))
                dxg_pairs.append(jnp.where(low, tns[0], tns[1]))
            dxg_parts.append(jnp.concatenate(dxg_pairs, axis=1) + dw_ * q["dse"][:, gl])
            dcbb = dcb.astype(bf16)
            dxbc_ref[:, csl] = _nt(gfull[:, gl], hgb) + _nn(dcbb, bg)
            dxbc_ref[:, bsl] = _nt(wst[:, gl], dhnb) + _tn(dcbb, cg)
            dh_scr[g] = dhn * q["cde"][:, gl] + _tn(cg, gfull[:, gl])
        dxg = jnp.concatenate(dxg_parts, axis=1)
        dcs_e = jnp.concatenate(dcs_e_parts, axis=1)
        dxbc_ref[:, 0:SSM_INNER] = dskip_ref[...] * dy + dxg * q["dt_e"]
        dcs = dcs_c - dcs_r.T + _dot_exact(dcs_e, q["expand"], ((1,), (1,)))
        triu = (q["cidx"] >= q["r"]).astype(bf16)
        da = _dot_exact(dcs, triu, ((1,), (0,)), x_is_lhs=False)
        ddt = _dot_exact(dxg * xs, q["expand"], ((1,), (1,))) + da * q["a_neg"]
        ddtp = jnp.where(q["head_lane"], ddt * _sigmoid(q["dtp"]), 0.0)
        ddt_ref[...] = ddtp.astype(bf16)
        dal_p = _rowsum8(da * q["dt"]) * q["a_neg"]
        dbi_p = _rowsum8(ddtp)
        first = (pl.program_id(0) == 0) & (pl.program_id(1) == 0)

        @pl.when(first)
        def _():
            dnw_ref[...] = dnw_p
            dds_ref[...] = dds_p
            dal_ref[...] = dal_p
            dbi_ref[...] = dbi_p

        @pl.when(jnp.logical_not(first))
        def _():
            dnw_ref[...] += dnw_p
            dds_ref[...] += dds_p
            dal_ref[...] += dal_p
            dbi_ref[...] += dbi_p

    def rows(w):
        return pl.BlockSpec((CHUNK, w), lambda b, c: (b * n_chunk + n_chunk - 1 - c, 0))

    def par(w):
        return pl.BlockSpec((1, w), lambda b, c: (0, 0))

    def acc(w):
        return pl.BlockSpec((SUBLANES, w), lambda b, c: (0, 0))

    return pl.pallas_call(
        body, name=name, grid=(t // SEQ, n_chunk),
        in_specs=[rows(CONV_CH), rows(SSM_INNER), rows(LANES), rows(SSM_INNER),
                  pl.BlockSpec((1, SSM_GROUPS, D_STATE, GROUP_W), lambda b, c: (b * n_chunk + n_chunk - 1 - c, 0, 0, 0)),
                  rows(SSM_INNER), par(LANES), par(LANES), par(SSM_INNER), par(SSM_INNER)],
        out_specs=[rows(CONV_CH), rows(SSM_INNER), rows(LANES), acc(SSM_INNER), acc(SSM_INNER), acc(LANES), acc(LANES)],
        out_shape=[jax.ShapeDtypeStruct((t, CONV_CH), f32), jax.ShapeDtypeStruct((t, SSM_INNER), bf16), jax.ShapeDtypeStruct((t, LANES), bf16),
                   jax.ShapeDtypeStruct((SUBLANES, SSM_INNER), f32), jax.ShapeDtypeStruct((SUBLANES, SSM_INNER), f32),
                   jax.ShapeDtypeStruct((SUBLANES, LANES), f32), jax.ShapeDtypeStruct((SUBLANES, LANES), f32)],
        scratch_shapes=[pltpu.VMEM((SSM_GROUPS, D_STATE, GROUP_W), f32)],
        compiler_params=_cparams(("arbitrary", "arbitrary")),
    )(xbc, z, dtp, y, hs, dyn, *params)


def _adamw(g_parts, w, m, v, name, layer=None):
    rows, width = w.shape[-2:]
    n = len(g_parts)
    tr = _row_tile(rows)

    def body(*refs):
        g_refs, (w_ref, m_ref, v_ref, g_out, d_out, m_out, v_out) = refs[:n], refs[n:]

        def part(i):
            return (g_refs[i][...] if g_parts[i][1] is None else g_refs[i][0]).astype(f32)

        def state(ref):
            return ref[...] if layer is None else ref[0]

        g = part(0)
        for i in range(1, n):
            g = g + part(i)
        mm = ADAM_B1 * state(m_ref) + (1.0 - ADAM_B1) * g
        vv = ADAM_B2 * state(v_ref) + (1.0 - ADAM_B2) * (g * g)
        m_hat = mm / (1.0 - ADAM_B1 ** ADAM_STEP)
        v_hat = vv / (1.0 - ADAM_B2 ** ADAM_STEP)
        g_out[...] = g
        d_out[...] = -ADAM_LR * (m_hat / (jnp.sqrt(v_hat) + ADAM_EPS) + ADAM_WD * state(w_ref))
        m_out[...] = mm
        v_out[...] = vv

    spec = pl.BlockSpec((tr, width), lambda i: (i, 0))

    def lead(idx):
        return spec if idx is None else pl.BlockSpec((1, tr, width), lambda i: (idx, i, 0))

    return pl.pallas_call(
        body, name=name, grid=(rows // tr,), in_specs=[lead(idx) for _, idx in g_parts] + [lead(layer)] * 3, out_specs=[spec] * 4,
        out_shape=[jax.ShapeDtypeStruct((rows, width), f32)] * 4, compiler_params=_cparams(("parallel",)),
    )(*[a for a, _ in g_parts], w, m, v)


def _row_tile(rows, cap=512):
    for cand in range(min(rows, cap) // SUBLANES * SUBLANES, 0, -SUBLANES):
        if rows % cand == 0:
            return cand
    return rows


def _cols_from_devices(g, width, name):
    n_dev, depth, a, b = g.shape

    def body(g_ref, o_ref):
        for i in range(n_dev):
            o_ref[0, :, i * b:(i + 1) * b] = g_ref[i, 0]
        if width > n_dev * b:
            o_ref[0, :, n_dev * b:width] = jnp.zeros((a, width - n_dev * b), o_ref.dtype)

    return pl.pallas_call(
        body, name=name, grid=(depth,), in_specs=[pl.BlockSpec((n_dev, 1, a, b), lambda l: (0, l, 0, 0))],
        out_specs=pl.BlockSpec((1, a, width), lambda l: (l, 0, 0)), out_shape=jax.ShapeDtypeStruct((depth, a, width), g.dtype),
        compiler_params=_cparams(("parallel",)),
    )(g)


def _devices_from_cols(per_layer, b, name, tr=256):
    depth = len(per_layer)
    a, width = per_layer[0].shape

    def body(*refs):
        o_ref = refs[depth]
        for l in range(depth):
            for i in range(N_DEV):
                o_ref[i, l] = refs[l][:, i * b:(i + 1) * b]

    return pl.pallas_call(
        body, name=name, grid=(a // tr,), in_specs=[pl.BlockSpec((tr, width), lambda r: (r, 0))] * depth,
        out_specs=pl.BlockSpec((N_DEV, depth, tr, b), lambda r: (0, 0, r, 0)),
        out_shape=jax.ShapeDtypeStruct((N_DEV, depth, a, b), per_layer[0].dtype), compiler_params=_cparams(("parallel",)),
    )(*per_layer)


def _me():
    return lax.axis_index("x"), lax.axis_index("y"), lax.axis_index("c")


def _allgather_two_level(shards, name):
    n = len(shards)
    per = 7

    def body(*refs):
        ins, outs, token = refs[:n], refs[n:2 * n], refs[2 * n]
        send_sems, recv_sems, local_sems = refs[2 * n + 1:]
        token[...] = jnp.zeros_like(token)
        x, y, c = _me()
        me, sibling = (x, y, c), (x, y, 1 - c)
        chips = [(1 - x, y), (x, 1 - y), (1 - x, 1 - y)]

        def slot(a, p):
            return outs[a].at[4 * p[0] + 2 * p[1] + p[2]]

        def copy(a, k, block, to, src=None):
            return pltpu.make_async_remote_copy(
                src_ref=slot(a, block) if src is None else src, dst_ref=slot(a, block),
                send_sem=send_sems.at[a * per + k], recv_sem=recv_sems.at[a * per + k], device_id=to, device_id_type=MESH)

        mine = [pltpu.make_async_copy(ins[a], slot(a, me), local_sems.at[a]) for a in range(n)]
        for cp in mine:
            cp.start()
        first = []
        for a in range(n):
            first.append(copy(a, 0, me, sibling, src=ins[a]))
            first += [copy(a, 1 + j, me, (*chip, c), src=ins[a]) for j, chip in enumerate(chips)]
        for cp in first:
            cp.start()
        passed = []
        for j, chip in enumerate(chips):
            for a in range(n):
                copy(a, 1 + j, (*chip, c), me).wait_recv()
                fwd = copy(a, 4 + j, (*chip, c), sibling)
                fwd.start()
                passed.append(fwd)
        for a in range(n):
            copy(a, 0, sibling, me).wait_recv()
            for j, chip in enumerate(chips):
                copy(a, 4 + j, (*chip, 1 - c), me).wait_recv()
        for cp in first + passed:
            cp.wait_send()
        for cp in mine:
            cp.wait()

    outs = pl.pallas_call(
        body, name=name, in_specs=[ANY] * n, out_specs=[ANY] * n + [pl.BlockSpec(memory_space=pltpu.VMEM)],
        out_shape=[jax.ShapeDtypeStruct((N_DEV,) + s.shape, s.dtype) for s in shards] + [jax.ShapeDtypeStruct((SUBLANES, LANES), f32)],
        scratch_shapes=[pltpu.SemaphoreType.DMA((n * per,)), pltpu.SemaphoreType.DMA((n * per,)), pltpu.SemaphoreType.DMA((n,))],
    )(*shards)
    return outs[:n], outs[n]


def _allgather_direct(row, name):
    def body(in_ref, out_ref, send_sems, recv_sems, local_sem):
        x, y, c = _me()
        mine = out_ref.at[4 * x + 2 * y + c]
        local = pltpu.make_async_copy(in_ref, mine, local_sem)
        local.start()
        sends = []
        for k in range(1, N_DEV):
            px, py, pc = x ^ (k >> 2), y ^ ((k >> 1) & 1), c ^ (k & 1)
            sends.append(pltpu.make_async_remote_copy(
                src_ref=in_ref, dst_ref=mine, send_sem=send_sems.at[k - 1], recv_sem=recv_sems.at[k - 1],
                device_id=(px, py, pc), device_id_type=MESH))
        for cp in sends:
            cp.start()
        for k in range(1, N_DEV):
            px, py, pc = x ^ (k >> 2), y ^ ((k >> 1) & 1), c ^ (k & 1)
            theirs = out_ref.at[4 * px + 2 * py + pc]
            pltpu.make_async_remote_copy(
                src_ref=in_ref, dst_ref=theirs, send_sem=send_sems.at[k - 1], recv_sem=recv_sems.at[k - 1],
                device_id=(px, py, pc), device_id_type=MESH).wait_recv()
        for cp in sends:
            cp.wait_send()
        local.wait()

    return pl.pallas_call(
        body, name=name, in_specs=[ANY], out_specs=ANY, out_shape=jax.ShapeDtypeStruct((N_DEV,) + row.shape, row.dtype),
        scratch_shapes=[pltpu.SemaphoreType.DMA((N_DEV - 1,)), pltpu.SemaphoreType.DMA((N_DEV - 1,)), pltpu.SemaphoreType.DMA],
    )(row)


N_CHIP = N_DEV // 2
HBM = pl.BlockSpec(memory_space=pltpu.HBM)
SEM = pl.BlockSpec(memory_space=pltpu.SEMAPHORE)
EFFECT = pltpu.SideEffectType.DATAFLOW_SIDE_EFFECTING


def _peer(k):
    x, y, c = _me()
    return x ^ (k >> 2), y ^ ((k >> 1) & 1), c ^ (k & 1)


def _direct_copies(srcs, lands, send_sems, recv_sems, per_peer):
    x, y, c = _me()
    me = 4 * x + 2 * y + c
    copies = []
    for a in range(len(srcs)):
        for k in range(1, N_DEV):
            px, py, pc = _peer(k)
            piece = srcs[a].at[4 * px + 2 * py + pc] if per_peer else srcs[a]
            copies.append(pltpu.make_async_remote_copy(
                src_ref=piece, dst_ref=lands[a].at[me], send_sem=send_sems.at[a * (N_DEV - 1) + k - 1],
                recv_sem=recv_sems.at[a * (N_DEV - 1) + k - 1], device_id=(px, py, pc), device_id_type=MESH))
    return copies


def _direct_start(srcs, lands, per_peer, name):
    n = len(srcs)
    n_sem = n * (N_DEV - 1)

    def body(*refs):
        src_refs, land_refs = refs[:n], refs[n:2 * n]
        send_sems, recv_sems = refs[2 * n], refs[2 * n + 1]
        token = refs[-1]
        for cp in _direct_copies(src_refs, land_refs, send_sems, recv_sems, per_peer):
            cp.start()
        token[...] = jnp.zeros_like(token)

    outs = pl.pallas_call(
        body, name=name,
        out_shape=(pltpu.SemaphoreType.DMA((n_sem,)), pltpu.SemaphoreType.DMA((n_sem,)),
                   *[pltpu.HBM(s.shape, s.dtype) for s in srcs], *[pltpu.HBM(s.shape, s.dtype) for s in lands],
                   jax.ShapeDtypeStruct((SUBLANES, LANES), f32)),
        in_specs=[HBM] * (2 * n), out_specs=(SEM, SEM, *[HBM] * (2 * n), pl.BlockSpec(memory_space=pltpu.VMEM)),
        input_output_aliases={i: 2 + i for i in range(2 * n)},
        compiler_params=pltpu.CompilerParams(has_side_effects=EFFECT),
    )(*[pltpu.with_memory_space_constraint(s, pltpu.HBM) for s in srcs], *[pltpu.with_memory_space_constraint(s, pltpu.HBM) for s in lands])
    return outs[0], outs[1], outs[2:2 + n], outs[2 + n:2 + 2 * n], outs[-1]


def _direct_wait(send_sems, recv_sems, srcs, lands, after, per_peer, name):
    n = len(srcs)

    def body(*refs):
        src_refs, land_refs = refs[:n], refs[n:2 * n]
        s_sems, r_sems = refs[2 * n], refs[2 * n + 1]
        for cp in _direct_copies(src_refs, land_refs, s_sems, r_sems, per_peer):
            cp.wait_send()
            cp.wait_recv()

    outs = pl.pallas_call(
        body, name=name,
        out_shape=tuple(pltpu.HBM(s.shape, s.dtype) for s in list(srcs) + list(lands)),
        in_specs=[HBM] * (2 * n) + [SEM, SEM, ANY], out_specs=tuple([HBM] * (2 * n)),
        input_output_aliases={i: i for i in range(2 * n)},
        compiler_params=pltpu.CompilerParams(has_side_effects=EFFECT),
    )(*srcs, *lands, send_sems, recv_sems, after)
    return outs[n:]


def _row(v, width=None):
    v = v.reshape(1, -1).astype(f32)
    if width is not None and v.shape[1] < width:
        v = jnp.pad(v, ((0, 0), (0, width - v.shape[1])))
    return v


def _layer_params(p, l):
    return dict(
        norm_mix=_row(p["norm_mix"][l]), norm_ffn=_row(p["norm_ffn"][l]), conv_w=p["conv_w"][l], conv_b=_row(p["conv_b"][l]),
        ssd=(_row(p["dt_bias"][l], LANES), _row(p["a_log"][l], LANES), _row(jnp.repeat(p["d_skip"][l], HEAD_DIM)), _row(p["ssm_norm"][l])))


def _layer_fwd(h, w_in, rest, sp, tabs, l):
    tag = f"l{l}_"
    hn = _rmsnorm_fwd(h, sp["norm_mix"], tag + "norm_mix")
    qkv = _matmul(hn, w_in, mode="nn", n_out=QKV_WIDTH, tn=256, b_off=0, name=tag + "proj_qkv")
    z = _matmul(hn, w_in, mode="nn", n_out=SSM_INNER, tn=256, b_off=Z_OFF // 256, name=tag + "proj_z")
    xbc_pre = _matmul(hn, w_in, mode="nn", n_out=CONV_CH, tn=256, b_off=XBC_OFF // 256, name=tag + "proj_xbc")
    dtp = _matmul(hn, w_in, mode="nn", n_out=LANES, tn=LANES, b_off=DT_OFF // LANES, name=tag + "proj_dt")
    prep = _attn_prep(qkv, tabs, tag + "attn_prep")
    o, lse = _attn_fwd(prep, tag + "attn_fwd")
    xbc = _conv_fwd(xbc_pre, sp["conv_w"], sp["conv_b"], tag + "conv_fwd")
    yn, y, hs = _ssd_fwd(xbc, z, dtp, sp["ssd"], tag + "ssd_fwd")
    w_out, w_gate, w_up, w_down = rest(yn) if callable(rest) else rest
    h2 = _out_proj(o, yn, w_out, h, tag + "out_proj")
    hn2 = _rmsnorm_fwd(h2, sp["norm_ffn"], tag + "norm_ffn")
    g, u, act = _swiglu_fwd(hn2, w_gate, w_up, tag + "ffn_up")
    h3 = _matmul(act, w_down, mode="nn", tk=1408, add=h2, name=tag + "ffn_down")
    saved = dict(h=h, hn=hn, prep=prep, z=z, xbc_pre=xbc_pre, dtp=dtp, o=o, lse=lse, xbc=xbc, yn=yn, y=y, hs=hs, h2=h2, hn2=hn2, g=g, u=u, act=act,
                 rest=(w_out, w_gate, w_up, w_down))
    return h3, saved


def _layer_bwd(dh3, s, big, sp, tabs, l, gd=f32, after_ffn=None):
    tag = f"l{l}_"
    w_in, w_out, w_gate, w_up, w_down = big
    dg, du = _swiglu_bwd(dh3, w_down, s["g"], s["u"], tag + "ffn_down_bwd")
    dw_down = _matmul(s["act"], dh3, mode="tn", tm=1408, tn=512, tk=2048, out_dtype=gd, name=tag + "dw_down")
    dw_gate = _matmul(s["hn2"], dg, mode="tn", tm=512, tn=1408, tk=2048, out_dtype=gd, name=tag + "dw_gate")
    dw_up = _matmul(s["hn2"], du, mode="tn", tm=512, tn=1408, tk=2048, out_dtype=gd, name=tag + "dw_up")
    norm_ffn = sp["norm_ffn"] if after_ffn is None else sp["norm_ffn"] + after_ffn(dict(w_gate=dw_gate, w_up=dw_up, w_down=dw_down))
    dh2, dnf = _nt_norm_bwd([(dg, w_gate), (du, w_up)], s["h2"], norm_ffn, dh3, tag + "ffn_up_bwd_norm", tk=1408)
    d_o = _matmul(dh2, w_out, mode="nt", n_out=ATTN_WIDTH, tn=512, b_off=0, name=tag + "out_attn_bwd")
    dyn = _matmul(dh2, w_out, mode="nt", n_out=SSM_INNER, tn=512, b_off=1, name=tag + "out_ssm_bwd")
    dw_out = jnp.concatenate([_matmul(s["o"], dh2, mode="tn", tm=512, tn=512, tk=2048, out_dtype=gd, name=tag + "dw_out_attn"),
                              _matmul(s["yn"], dh2, mode="tn", tm=512, tn=512, tk=2048, out_dtype=gd, name=tag + "dw_out_ssm")], axis=0)
    dxbc, dz, ddtp, dnw, dds, dal, dbi = _ssd_bwd(s["xbc"], s["z"], s["dtp"], s["y"], s["hs"], dyn, sp["ssd"], tag + "ssd_bwd")
    dxbc_pre, dconv_w, dconv_b = _conv_bwd(s["xbc_pre"], sp["conv_w"], sp["conv_b"], dxbc, tag + "conv_bwd")
    dq, dk, dv = _attn_bwd(s["prep"], tabs, s["o"], s["lse"], d_o, tag + "attn_bwd")
    dproj = jnp.concatenate([dq, dk, dv, dz, dxbc_pre, ddtp], axis=1)
    dw_in = _matmul(s["hn"], dproj, mode="tn", tm=512, tn=1152, tk=2048, out_dtype=gd, name=tag + "dw_in")
    dh, dnm = _nt_norm_bwd([(dproj, w_in)], s["h"], sp["norm_mix"], dh2, tag + "proj_bwd_norm", tk=1152)
    grads = dict(
        norm_mix=dnm.sum(0), w_in=dw_in, conv_w=dconv_w, conv_b=dconv_b[0], dt_bias=dbi.sum(0)[:SSM_HEADS], a_log=dal.sum(0)[:SSM_HEADS],
        d_skip=dds.sum(0).reshape(SSM_HEADS, HEAD_DIM).sum(1), ssm_norm=dnw.sum(0), w_out=dw_out, norm_ffn=dnf.sum(0),
        w_gate=dw_gate, w_up=dw_up, w_down=dw_down)
    return dh, grads


def _local_step(x, positions, target, p, bigs):
    tabs = _rope_tables(positions.reshape(-1, 1), "rope_tables")
    h = x
    saved, sps = [], []
    for l in range(DEPTH):
        sps.append(_layer_params(p, l))
        h, s = _layer_fwd(h, bigs[l][0], bigs[l][1:], sps[l], tabs, l)
        saved.append(s)
    dh, loss_parts, dfn = _final_loss(h, _row(p["final_norm"]), target, "final_loss")
    layer_grads = [None] * DEPTH
    for l in reversed(range(DEPTH)):
        dh, layer_grads[l] = _layer_bwd(dh, saved[l], bigs[l], sps[l], tabs, l)
    grads = {k: [layer_grads[l][k] for l in range(DEPTH)] for k in layer_grads[0]}
    grads["final_norm"] = dfn.sum(0)
    return jnp.sum(loss_parts), dh, grads


BIG = ("w_in", "w_out", "w_gate", "w_up", "w_down")
REST = BIG[1:]
FFN = ("w_gate", "w_up", "w_down")
MIX = ("w_in", "w_out")
COL_SHARDED = ("w_in", "w_gate", "w_up")
SMALL = ("norm_mix", "conv_b", "dt_bias", "a_log", "d_skip", "ssm_norm", "norm_ffn", "final_norm")
WEIGHTS = ("norm_mix", "w_in", "conv_w", "conv_b", "dt_bias", "a_log", "d_skip", "ssm_norm", "w_out", "norm_ffn", "w_gate", "w_up", "w_down", "final_norm")
PACK_W = 1024
SMALL_ROWS = 88
CONVW_ROWS = 96
CONVW_SHARD_ROWS = 16


def _full_from_gathered(name, g, l):
    _, a, b = g.shape
    if name in COL_SHARDED:
        width = IN_PROJ_PAD if name == "w_in" else N_DEV * b
        return _cols_from_devices(g.reshape(N_DEV, 1, a, b), width, f"cols_l{l}_{name}").reshape(a, width)
    return g.reshape(N_DEV * a, b)


def _by_device(name, full, shard_shape, l):
    a, b = shard_shape
    if name in COL_SHARDED:
        return _devices_from_cols([full], b, f"devs_l{l}_{name}").reshape(N_CHIP, 2, a, b)
    return full.reshape(N_CHIP, 2, a, b)


def _pack_rows(parts, rows, width):
    flat = jnp.concatenate([q.reshape(-1) for q in parts])
    return jnp.pad(flat, (0, rows * width - flat.shape[0])).reshape(rows, width)


def _unpack(flat, like):
    out, off = [], 0
    for q in like:
        out.append(flat[off:off + q.size].reshape(q.shape))
        off += q.size
    return out


def kernel(x, positions, norm_mix, w_in, conv_w, conv_b, dt_bias, a_log, d_skip, ssm_norm, w_out, norm_ffn, w_gate, w_up, w_down, final_norm, loss_target, m_norm_mix, m_w_in, m_conv_w, m_conv_b, m_dt_bias, m_a_log, m_d_skip, m_ssm_norm, m_w_out, m_norm_ffn, m_w_gate, m_w_up, m_w_down, m_final_norm, v_norm_mix, v_w_in, v_conv_w, v_conv_b, v_dt_bias, v_a_log, v_d_skip, v_ssm_norm, v_w_out, v_norm_ffn, v_w_gate, v_w_up, v_w_down, v_final_norm):
    w = dict(norm_mix=norm_mix, w_in=w_in, conv_w=conv_w, conv_b=conv_b, dt_bias=dt_bias, a_log=a_log, d_skip=d_skip, ssm_norm=ssm_norm,
             w_out=w_out, norm_ffn=norm_ffn, w_gate=w_gate, w_up=w_up, w_down=w_down, final_norm=final_norm)
    m = dict(norm_mix=m_norm_mix, w_in=m_w_in, conv_w=m_conv_w, conv_b=m_conv_b, dt_bias=m_dt_bias, a_log=m_a_log, d_skip=m_d_skip,
             ssm_norm=m_ssm_norm, w_out=m_w_out, norm_ffn=m_norm_ffn, w_gate=m_w_gate, w_up=m_w_up, w_down=m_w_down, final_norm=m_final_norm)
    v = dict(norm_mix=v_norm_mix, w_in=v_w_in, conv_w=v_conv_w, conv_b=v_conv_b, dt_bias=v_dt_bias, a_log=v_a_log, d_skip=v_d_skip,
             ssm_norm=v_ssm_norm, w_out=v_w_out, norm_ffn=v_norm_ffn, w_gate=v_w_gate, w_up=v_w_up, w_down=v_w_down, final_norm=v_final_norm)
    ax, ay, ac = lax.axis_index("x"), lax.axis_index("y"), lax.axis_index("c")
    dev = 4 * ax + 2 * ay + ac

    assert DEPTH == 2
    t = x.shape[0] * x.shape[1]
    xf, target = x.reshape(t, D_MODEL), loss_target.reshape(t, D_MODEL)

    def own_slot(block):
        return jnp.broadcast_to(block, (N_DEV,) + block.shape[1:])

    def gather_start(keys, l, tie, name):
        shards = [(w[keys[0]][l] + tie).astype(bf16)] + [w[k][l].astype(bf16) for k in keys[1:]]
        return _direct_start(shards, [own_slot(s[None]) for s in shards], False, name)

    def scatter_start(keys, grads_l, l, name):
        by_dev = [_by_device(k, grads_l[k], w[k].shape[1:], l).reshape((N_DEV,) + w[k].shape[1:]) for k in keys]
        return _direct_start(by_dev, [own_slot(lax.dynamic_slice_in_dim(g, dev, 1, 0)) for g in by_dev], True, name)

    (g_in0, conv_all), tie = _allgather_two_level([w["w_in"][0].astype(bf16), w["conv_w"]], "gather_l0_w_in")
    rest0_copy = gather_start(REST, 0, tie[0, 0], "gather_l0_rest_start")
    l1_copy = gather_start(BIG, 1, rest0_copy[4][0, 0], "gather_l1_start")
    p = {k: w[k] for k in SMALL}
    p["norm_mix"] = p["norm_mix"] + l1_copy[4][0, 0]
    p["conv_w"] = jnp.transpose(conv_all, (1, 2, 0, 3)).reshape(DEPTH, CONV_WIDTH, CONV_CH)
    sp0, sp1 = _layer_params(p, 0), _layer_params(p, 1)

    def rest0(after):
        lands = _direct_wait(*rest0_copy[:4], after, False, "gather_l0_rest_wait")
        return tuple(_full_from_gathered(k, g, 0) for k, g in zip(REST, lands))

    tabs = _rope_tables(positions.reshape(t, 1), "rope_tables")
    w_in0 = _full_from_gathered("w_in", g_in0, 0)
    h1, saved0 = _layer_fwd(xf, w_in0, rest0, sp0, tabs, 0)
    lands1 = _direct_wait(*l1_copy[:4], h1, False, "gather_l1_wait")
    bigs1 = tuple(_full_from_gathered(k, g, 1) for k, g in zip(BIG, lands1))
    h2, saved1 = _layer_fwd(h1, bigs1[0], bigs1[1:], sp1, tabs, 1)
    dh, loss_parts, dfn = _final_loss(h2, _row(p["final_norm"]), target, "final_loss")
    loss_local = jnp.sum(loss_parts)

    dh, grads1 = _layer_bwd(dh, saved1, bigs1, sp1, tabs, 1, gd=bf16)
    l1_grads = scatter_start(BIG, grads1, 1, "scatter_l1_start")
    w_out0, w_gate0, w_up0, w_down0 = saved0["rest"]
    bigs0 = (w_in0, w_out0, w_gate0, w_up0, w_down0 + l1_grads[4][0, 0].astype(bf16))
    ffn0_grads = []

    def after_ffn(grads_ffn):
        ffn0_grads.append(scatter_start(FFN, grads_ffn, 0, "scatter_l0_ffn_start"))
        return ffn0_grads[0][4][0, 0]

    dx, grads0 = _layer_bwd(dh, saved0, bigs0, sp0, tabs, 0, gd=bf16, after_ffn=after_ffn)
    mix0_grads = scatter_start(MIX, grads0, 0, "scatter_l0_mix_start")
    landed = {(k, 1): g for k, g in zip(BIG, _direct_wait(*l1_grads[:4], dx, True, "scatter_l1_wait"))}
    landed.update({(k, 0): g for k, g in zip(FFN, _direct_wait(*ffn0_grads[0][:4], dx, True, "scatter_l0_ffn_wait"))})
    landed.update({(k, 0): g for k, g in zip(MIX, _direct_wait(*mix0_grads[:4], mix0_grads[4], True, "scatter_l0_mix_wait"))})
    out_g, out_d, out_m, out_v = {}, {}, {}, {}
    for k in BIG:
        res = [_adamw([(landed[k, l], i) for i in range(N_DEV)], w[k], m[k], v[k], f"adamw_l{l}_{k}", layer=l) for l in range(DEPTH)]
        for dst, r0, r1 in zip((out_g, out_d, out_m, out_v), res[0], res[1]):
            dst[k] = jnp.stack([r0, r1])
    grads = {k: [grads0[k], grads1[k]] for k in grads0 if k not in BIG}
    grads["final_norm"] = dfn.sum(0)

    small_like = [w[k] for k in SMALL]
    small_grads = [jnp.stack(grads[k]) if k != "final_norm" else grads[k] for k in SMALL]
    small_pack = jnp.concatenate([_pack_rows(small_grads, SMALL_ROWS, LANES), _pack_rows([jnp.stack(grads["conv_w"])], CONVW_ROWS, LANES)], axis=0)
    parts = _allgather_direct(small_pack, "gather_small_grads")
    g_s, d_s, m_s, v_s = _adamw(
        [(parts[i, :SMALL_ROWS], None) for i in range(N_DEV)], _pack_rows(small_like, SMALL_ROWS, LANES),
        _pack_rows([m[k] for k in SMALL], SMALL_ROWS, LANES), _pack_rows([v[k] for k in SMALL], SMALL_ROWS, LANES), "adamw_replicated")
    for dst, src in ((out_g, g_s), (out_d, d_s), (out_m, m_s), (out_v, v_s)):
        dst.update(zip(SMALL, _unpack(src.reshape(-1), small_like)))
    shard_w = conv_w.shape[-1]
    conv_parts = parts[:, SMALL_ROWS:].reshape(N_DEV, DEPTH, CONV_WIDTH, CONV_CH)
    conv_mine = lax.dynamic_slice_in_dim(conv_parts, dev * shard_w, shard_w, axis=3)
    g_c, d_c, m_c, v_c = _adamw(
        [(_pack_rows([conv_mine[i]], CONVW_SHARD_ROWS, LANES), None) for i in range(N_DEV)], _pack_rows([conv_w], CONVW_SHARD_ROWS, LANES),
        _pack_rows([m["conv_w"]], CONVW_SHARD_ROWS, LANES), _pack_rows([v["conv_w"]], CONVW_SHARD_ROWS, LANES), "adamw_conv_w")
    for dst, src in ((out_g, g_c), (out_d, d_c), (out_m, m_c), (out_v, v_c)):
        dst["conv_w"] = src.reshape(-1)[:conv_w.size].reshape(conv_w.shape)

    loss = lax.psum(loss_local, ("x", "y", "c"))
    return (loss, dx.reshape(x.shape), *[out_g[k] for k in WEIGHTS], *[out_d[k] for k in WEIGHTS],
            *[out_m[k] for k in WEIGHTS], *[out_v[k] for k in WEIGHTS])
```

```python
import functools
import math

import jax
import jax.numpy as jnp
import numpy as np
from jax import lax
from jax.experimental import pallas as pl
from jax.experimental.pallas import tpu as pltpu

f32 = jnp.float32
bf16 = jnp.bfloat16

D_MODEL = 1024
SEQ = 2048
DEPTH = 2
HEAD_DIM = 64
N_ATTN_HEADS = 8
N_KV_HEADS = 2
ATTN_WIDTH = 512
KV_WIDTH = 128
ROPE_DIM = 16
ROPE_THETA = 500000.0
DILATIONS = (1, 4, 16)
ATTN_BLOCK = 128
SSM_HEADS = 16
SSM_INNER = 1024
SSM_GROUPS = 2
D_STATE = 128
CONV_WIDTH = 4
CHUNK = 128
CONV_CH = 1536
MIX_WIDTH = 1536
QKV_WIDTH = ATTN_WIDTH + 2 * KV_WIDTH
Z_OFF = 768
XBC_OFF = 1792
DT_OFF = 3328
IN_PROJ = 3344
IN_PROJ_PAD = 3456
FFN_HIDDEN = 2816
EPS = 1e-5
N_DEV = 8
ADAM_LR = 0.001
ADAM_B1 = 0.9
ADAM_B2 = 0.999
ADAM_EPS = 1e-08
ADAM_WD = 0.01
ADAM_STEP = 10

LANES = 128
SUBLANES = 8
VMEM_LIMIT = 56 * 1024 * 1024

MESH = pl.DeviceIdType.MESH
ANY = pl.BlockSpec(memory_space=pl.ANY)


def _cparams(sem, vmem=None):
    return pltpu.CompilerParams(dimension_semantics=sem, vmem_limit_bytes=vmem or VMEM_LIMIT)


def _sigmoid(x):
    return 1.0 / (1.0 + jnp.exp(-x))


def _silu(x):
    return x * _sigmoid(x)


def _dsilu(x):
    s = _sigmoid(x)
    return s * (1.0 + x * (1.0 - s))


def _silu_and_grad(x):
    s = _sigmoid(x)
    return x * s, s * (1.0 + x * (1.0 - s))


def _softplus(x):
    return jnp.maximum(x, 0.0) + jnp.log(1.0 + jnp.exp(-jnp.abs(x)))


def _dot(a, b, dims, precision=None):
    return lax.dot_general(a, b, (dims, ((), ())), preferred_element_type=f32, precision=precision)


def _nn(a, b, precision=None):
    return _dot(a, b, ((1,), (0,)), precision)


def _nt(a, b):
    return _dot(a, b, ((1,), (1,)))


def _tn(a, b):
    return _dot(a, b, ((0,), (0,)))


def _rowsum8(t):
    n, w = t.shape
    return jnp.sum(t.reshape(n // SUBLANES, SUBLANES, w), axis=0)


def _matmul(a, b, *, mode, n_out=None, b_off=0, a_koff=0, b_koff=0, k_len=None, add=None, out_dtype=f32, tm=2048, tn=512, tk=1024, name):
    if mode == "tn":
        kdim_a, m = a.shape
    else:
        m, kdim_a = a.shape
    kk = k_len if k_len is not None else kdim_a
    n = n_out if n_out is not None else (b.shape[0] if mode == "nt" else b.shape[1])
    tm, tn, tk = min(tm, m), min(tn, n), min(tk, kk)
    assert m % tm == 0 and n % tn == 0 and kk % tk == 0, (name, m, n, kk, tm, tn, tk)
    nk = kk // tk
    if mode == "nn":
        a_spec = pl.BlockSpec((tm, tk), lambda i, j, k: (i, k + a_koff))
        b_spec = pl.BlockSpec((tk, tn), lambda i, j, k: (k + b_koff, j + b_off))
        dims = ((1,), (0,))
    elif mode == "nt":
        a_spec = pl.BlockSpec((tm, tk), lambda i, j, k: (i, k + a_koff))
        b_spec = pl.BlockSpec((tn, tk), lambda i, j, k: (j + b_off, k + b_koff))
        dims = ((1,), (1,))
    else:
        a_spec = pl.BlockSpec((tk, tm), lambda i, j, k: (k + a_koff, i))
        b_spec = pl.BlockSpec((tk, tn), lambda i, j, k: (k + b_koff, j + b_off))
        dims = ((0,), (0,))
    o_spec = pl.BlockSpec((tm, tn), lambda i, j, k: (i, j))
    has_add = add is not None

    def body(*refs):
        if has_add:
            a_ref, b_ref, add_ref, o_ref, acc_ref = refs
        else:
            a_ref, b_ref, o_ref, acc_ref = refs
        k = pl.program_id(2)
        part = _dot(a_ref[...].astype(bf16), b_ref[...].astype(bf16), dims)

        @pl.when(k == 0)
        def _():
            acc_ref[...] = part

        @pl.when(k > 0)
        def _():
            acc_ref[...] += part

        @pl.when(k == nk - 1)
        def _():
            r = acc_ref[...]
            if has_add:
                r = r + add_ref[...]
            o_ref[...] = r.astype(out_dtype)

    in_specs = [a_spec, b_spec] + ([o_spec] if has_add else [])
    args = (a, b) + ((add,) if has_add else ())
    return pl.pallas_call(
        body, name=name, grid=(m // tm, n // tn, nk), in_specs=in_specs, out_specs=o_spec,
        out_shape=jax.ShapeDtypeStruct((m, n), out_dtype), scratch_shapes=[pltpu.VMEM((tm, tn), f32)],
        compiler_params=_cparams(("parallel", "parallel", "arbitrary")),
    )(*args)


def _out_proj(o, yn, w_out, h, name, tm=2048, tn=512):
    m, kb = o.shape
    n = w_out.shape[1]
    n_y = yn.shape[1] // kb
    assert yn.shape[1] % kb == 0 and w_out.shape[0] == kb * (1 + n_y)

    def body(*refs):
        o_ref, y_refs, w_refs, h_ref, out_ref = refs[0], refs[1:1 + n_y], refs[1 + n_y:2 + 2 * n_y], refs[-2], refs[-1]
        acc = h_ref[...] + _nn(o_ref[...].astype(bf16), w_refs[0][...])
        for y_ref, w_ref in zip(y_refs, w_refs[1:]):
            acc = acc + _nn(y_ref[...], w_ref[...])
        out_ref[...] = acc

    res = pl.BlockSpec((tm, tn), lambda i, j: (i, j))

    def a_blk(c):
        return pl.BlockSpec((tm, kb), lambda i, j: (i, c))

    def w_blk(r):
        return pl.BlockSpec((kb, tn), lambda i, j: (r, j))

    return pl.pallas_call(
        body, name=name, grid=(m // tm, n // tn),
        in_specs=[a_blk(0)] + [a_blk(c) for c in range(n_y)] + [w_blk(r) for r in range(1 + n_y)] + [res],
        out_specs=res, out_shape=jax.ShapeDtypeStruct((m, n), f32), compiler_params=_cparams(("parallel", "parallel")),
    )(o, *[yn] * n_y, *[w_out] * (1 + n_y), h)


def _swiglu_fwd(hn, w_gate, w_up, name, tm=2048, tn=256):
    m, k = hn.shape
    n = w_gate.shape[1]

    def body(a_ref, wg_ref, wu_ref, g_ref, u_ref, act_ref):
        a = a_ref[...]
        g = _nn(a, wg_ref[...])
        u = _nn(a, wu_ref[...])
        sg, dsg = _silu_and_grad(g)
        g_ref[...] = (u * dsg).astype(bf16)
        u_ref[...] = sg.astype(bf16)
        act_ref[...] = (sg * u).astype(bf16)

    a_spec = pl.BlockSpec((tm, k), lambda i, j: (i, 0))
    w_spec = pl.BlockSpec((k, tn), lambda i, j: (0, j))
    o_spec = pl.BlockSpec((tm, tn), lambda i, j: (i, j))
    return pl.pallas_call(
        body, name=name, grid=(m // tm, n // tn), in_specs=[a_spec, w_spec, w_spec], out_specs=[o_spec, o_spec, o_spec],
        out_shape=[jax.ShapeDtypeStruct((m, n), bf16)] * 3,
        compiler_params=_cparams(("parallel", "parallel")),
    )(hn, w_gate, w_up)


def _swiglu_bwd(dh, w_down, g, u, name, tm=2048, tn=256):
    m, k = dh.shape
    n = w_down.shape[0]

    def body(a_ref, w_ref, g_ref, u_ref, dg_ref, du_ref):
        dact = _nt(a_ref[...].astype(bf16), w_ref[...])
        dg_ref[...] = (dact * g_ref[...].astype(f32)).astype(bf16)
        du_ref[...] = (dact * u_ref[...].astype(f32)).astype(bf16)

    a_spec = pl.BlockSpec((tm, k), lambda i, j: (i, 0))
    w_spec = pl.BlockSpec((tn, k), lambda i, j: (j, 0))
    o_spec = pl.BlockSpec((tm, tn), lambda i, j: (i, j))
    return pl.pallas_call(
        body, name=name, grid=(m // tm, n // tn), in_specs=[a_spec, w_spec, o_spec, o_spec], out_specs=[o_spec, o_spec],
        out_shape=[jax.ShapeDtypeStruct((m, n), bf16), jax.ShapeDtypeStruct((m, n), bf16)],
        compiler_params=_cparams(("parallel", "parallel")),
    )(dh, w_down, g, u)


def _rmsnorm_fwd(h, w, name, tm=512):
    m, d = h.shape

    def body(h_ref, w_ref, o_ref):
        x = h_ref[...]
        r = lax.rsqrt(jnp.mean(x * x, axis=-1, keepdims=True) + EPS)
        o_ref[...] = (x * r * w_ref[...]).astype(bf16)

    return pl.pallas_call(
        body, name=name, grid=(m // tm,),
        in_specs=[pl.BlockSpec((tm, d), lambda i: (i, 0)), pl.BlockSpec((1, d), lambda i: (0, 0))],
        out_specs=pl.BlockSpec((tm, d), lambda i: (i, 0)), out_shape=jax.ShapeDtypeStruct((m, d), bf16),
        compiler_params=_cparams(("parallel",)),
    )(h, w)


def _rmsnorm_bwd(dhn, h, w, dres, name, tm=512):
    m, d = h.shape

    def body(dhn_ref, h_ref, w_ref, dres_ref, dh_ref, dw_ref):
        x = h_ref[...]
        r = lax.rsqrt(jnp.mean(x * x, axis=-1, keepdims=True) + EPS)
        xhat = x * r
        dy = dhn_ref[...]
        gw = dy * w_ref[...]
        dh_ref[...] = dres_ref[...] + r * (gw - xhat * jnp.mean(gw * xhat, axis=-1, keepdims=True))
        part = _rowsum8(dy * xhat)

        @pl.when(pl.program_id(0) == 0)
        def _():
            dw_ref[...] = part

        @pl.when(pl.program_id(0) > 0)
        def _():
            dw_ref[...] += part

    row = pl.BlockSpec((tm, d), lambda i: (i, 0))
    return pl.pallas_call(
        body, name=name, grid=(m // tm,),
        in_specs=[row, row, pl.BlockSpec((1, d), lambda i: (0, 0)), row],
        out_specs=[row, pl.BlockSpec((SUBLANES, d), lambda i: (0, 0))],
        out_shape=[jax.ShapeDtypeStruct((m, d), f32), jax.ShapeDtypeStruct((SUBLANES, d), f32)],
        compiler_params=_cparams(("arbitrary",)),
    )(dhn, h, w, dres)


def _nt_norm_bwd(pairs, h, w, dres, name, tm=1024, tk=704):
    m, d = h.shape
    steps = [p[0].shape[1] // tk for p in pairs]
    assert all(p[0].shape[1] % tk == 0 for p in pairs), (name, tk)
    starts = [sum(steps[:i]) for i in range(len(pairs))]
    nk = sum(steps)
    n_p = len(pairs)

    def body(*refs):
        ab = refs[:2 * n_p]
        h_ref, w_ref, dres_ref, dh_ref, dw_ref, acc_ref = refs[2 * n_p:]
        i, k = pl.program_id(0), pl.program_id(1)

        @pl.when(k == 0)
        def _():
            acc_ref[...] = jnp.zeros_like(acc_ref)

        for p in range(n_p):
            @pl.when((k >= starts[p]) & (k < starts[p] + steps[p]))
            def _(p=p):
                acc_ref[...] += _nt(ab[2 * p][...], ab[2 * p + 1][...])

        @pl.when(k == nk - 1)
        def _():
            x = h_ref[...]
            r = lax.rsqrt(jnp.mean(x * x, axis=-1, keepdims=True) + EPS)
            xhat = x * r
            dy = acc_ref[...]
            gw = dy * w_ref[...]
            dh_ref[...] = dres_ref[...] + r * (gw - xhat * jnp.mean(gw * xhat, axis=-1, keepdims=True))
            part = _rowsum8(dy * xhat)

            @pl.when(i == 0)
            def _():
                dw_ref[...] = part

            @pl.when(i > 0)
            def _():
                dw_ref[...] += part

    def clamp(k, p):
        return jnp.clip(k - starts[p], 0, steps[p] - 1)

    in_specs = []
    for p in range(n_p):
        in_specs += [pl.BlockSpec((tm, tk), lambda i, k, p=p: (i, clamp(k, p))), pl.BlockSpec((d, tk), lambda i, k, p=p: (0, clamp(k, p)))]
    row = pl.BlockSpec((tm, d), lambda i, k: (i, 0))
    in_specs += [row, pl.BlockSpec((1, d), lambda i, k: (0, 0)), row]
    return pl.pallas_call(
        body, name=name, grid=(m // tm, nk), in_specs=in_specs,
        out_specs=[row, pl.BlockSpec((SUBLANES, d), lambda i, k: (0, 0))],
        out_shape=[jax.ShapeDtypeStruct((m, d), f32), jax.ShapeDtypeStruct((SUBLANES, d), f32)],
        scratch_shapes=[pltpu.VMEM((tm, d), f32)], compiler_params=_cparams(("arbitrary", "arbitrary")),
    )(*[t for p in pairs for t in p], h, w, dres)


def _final_loss(h, w, target, name, tm=512):
    m, d = h.shape

    def body(h_ref, w_ref, t_ref, dh_ref, loss_ref, dw_ref):
        x = h_ref[...]
        r = lax.rsqrt(jnp.mean(x * x, axis=-1, keepdims=True) + EPS)
        xhat = x * r
        ww = w_ref[...]
        err = xhat * ww - t_ref[...]
        dy = err * (1.0 / d)
        gw = dy * ww
        dh_ref[...] = r * (gw - xhat * jnp.mean(gw * xhat, axis=-1, keepdims=True))
        lpart = _rowsum8(err * err) * (0.5 / d)
        wpart = _rowsum8(dy * xhat)

        @pl.when(pl.program_id(0) == 0)
        def _():
            loss_ref[...] = lpart
            dw_ref[...] = wpart

        @pl.when(pl.program_id(0) > 0)
        def _():
            loss_ref[...] += lpart
            dw_ref[...] += wpart

    row = pl.BlockSpec((tm, d), lambda i: (i, 0))
    acc = pl.BlockSpec((SUBLANES, d), lambda i: (0, 0))
    return pl.pallas_call(
        body, name=name, grid=(m // tm,),
        in_specs=[row, pl.BlockSpec((1, d), lambda i: (0, 0)), row], out_specs=[row, acc, acc],
        out_shape=[jax.ShapeDtypeStruct((m, d), f32), jax.ShapeDtypeStruct((SUBLANES, d), f32), jax.ShapeDtypeStruct((SUBLANES, d), f32)],
        compiler_params=_cparams(("arbitrary",)),
    )(h, w, target)


def _lane_tables():
    f = np.arange(LANES) % HEAD_DIM
    inv = ROPE_THETA ** (-jnp.arange(0, ROPE_DIM, 2, dtype=f32) / ROPE_DIM)
    invf = jnp.where(f < ROPE_DIM, inv[f % (ROPE_DIM // 2)], 0.0).astype(f32)
    return invf.reshape(1, LANES)


def _rope_tables(pos_col, name):
    t = pos_col.shape[0]
    tm = SEQ

    def body(p_ref, f_ref, c_ref, s1_ref, s2_ref):
        ang = p_ref[...].astype(f32) * f_ref[...]
        co, si = jnp.cos(ang), jnp.sin(ang)
        f = lax.broadcasted_iota(jnp.int32, (tm, LANES), 1) % HEAD_DIM
        c_ref[...] = jnp.where(f < ROPE_DIM, co, 1.0)
        s1_ref[...] = jnp.where(f < ROPE_DIM // 2, -si, 0.0)
        s2_ref[...] = jnp.where((f >= ROPE_DIM // 2) & (f < ROPE_DIM), si, 0.0)

    row = pl.BlockSpec((tm, LANES), lambda i: (i, 0))
    return pl.pallas_call(
        body, name=name, grid=(t // tm,),
        in_specs=[pl.BlockSpec((tm, 1), lambda i: (i, 0)), pl.BlockSpec((1, LANES), lambda i: (0, 0))],
        out_specs=[row, row, row], out_shape=[jax.ShapeDtypeStruct((t, LANES), f32)] * 3,
        compiler_params=_cparams(("parallel",)),
    )(pos_col, _lane_tables())


def _rot(x, c, s1, s2):
    return x * c + pltpu.roll(x, LANES - ROPE_DIM // 2, 1) * s1 + pltpu.roll(x, ROPE_DIM // 2, 1) * s2


def _rot_t(g, c, s1, s2):
    return g * c + pltpu.roll(g * s1, ROPE_DIM // 2, 1) + pltpu.roll(g * s2, LANES - ROPE_DIM // 2, 1)


def _dup_head(x, kvh, low):
    a = jnp.where(kvh == 0, x, pltpu.roll(x, HEAD_DIM, 1))
    return jnp.where(low, a, pltpu.roll(a, HEAD_DIM, 1))


def _deinterleave(src_ref, dst_ref, d, dtype):
    length = SEQ // d
    if d == 1:
        dst_ref[...] = src_ref[...].astype(dtype)
    else:
        for r in range(d):
            dst_ref[pl.ds(r * length, length), :] = src_ref[pl.ds(r, length, stride=d), :].astype(dtype)


def _interleave_store(src_ref, dst_ref, d, accumulate):
    length = SEQ // d
    if d == 1:
        if accumulate:
            dst_ref[...] += src_ref[...]
        else:
            dst_ref[...] = src_ref[...]
    else:
        for r in range(d):
            blk = src_ref[pl.ds(r * length, length), :]
            if accumulate:
                dst_ref[pl.ds(r, length, stride=d), :] = dst_ref[pl.ds(r, length, stride=d), :] + blk
            else:
                dst_ref[pl.ds(r, length, stride=d), :] = blk


def _attn_masks():
    qi = lax.broadcasted_iota(jnp.int32, (ATTN_BLOCK, ATTN_BLOCK), 0)
    ki = lax.broadcasted_iota(jnp.int32, (ATTN_BLOCK, ATTN_BLOCK), 1)
    low = lax.broadcasted_iota(jnp.int32, (ATTN_BLOCK, LANES), 1) < HEAD_DIM
    return ki <= qi, ki >= qi, low


NEG_INF = float("-inf")
ATTN_UNROLL = 4


N_BRANCH = len(DILATIONS)


def _attn_prep(qkv, tabs, name):
    t = qkv.shape[0]
    nb = t // SEQ
    n_j = ATTN_WIDTH // LANES

    def q_body(q_ref, c_ref, s1_ref, s2_ref, out_ref, xr):
        xr[...] = _rot(q_ref[...], c_ref[...], s1_ref[...], s2_ref[...]) * (HEAD_DIM ** -0.5)
        for bi, d in enumerate(DILATIONS):
            _deinterleave(xr, out_ref.at[bi], d, bf16)

    def kv_body(x_ref, c_ref, s1_ref, s2_ref, out_ref, xr):
        lowfull = lax.broadcasted_iota(jnp.int32, (SEQ, LANES), 1) < HEAD_DIM
        x = x_ref[...]
        x = jnp.where(pl.program_id(1) == 0, _rot(x, c_ref[...], s1_ref[...], s2_ref[...]), x)
        for kvh in range(N_KV_HEADS):
            xr[...] = _dup_head(x, kvh, lowfull)
            for bi, d in enumerate(DILATIONS):
                length = SEQ // d
                for r in range(d):
                    rows = xr[...] if d == 1 else xr[pl.ds(r, length, stride=d), :]
                    out_ref[0, bi, pl.ds(r * length, length), kvh * LANES:(kvh + 1) * LANES] = rows.astype(bf16)

    tab = pl.BlockSpec((SEQ, LANES), lambda b, j: (b, 0))
    q = pl.pallas_call(
        q_body, name=name + "_q", grid=(nb, n_j),
        in_specs=[pl.BlockSpec((SEQ, LANES), lambda b, j: (b, j)), tab, tab, tab],
        out_specs=pl.BlockSpec((N_BRANCH, SEQ, LANES), lambda b, j: (0, b, j)),
        out_shape=jax.ShapeDtypeStruct((N_BRANCH, t, ATTN_WIDTH), bf16), scratch_shapes=[pltpu.VMEM((SEQ, LANES), f32)],
        compiler_params=_cparams(("parallel", "parallel")),
    )(qkv, *tabs)
    kv = pl.pallas_call(
        kv_body, name=name + "_kv", grid=(nb, 2),
        in_specs=[pl.BlockSpec((SEQ, LANES), lambda b, j: (b, n_j + j)), tab, tab, tab],
        out_specs=pl.BlockSpec((1, N_BRANCH, SEQ, N_KV_HEADS * LANES), lambda b, j: (j, 0, b, 0)),
        out_shape=jax.ShapeDtypeStruct((2, N_BRANCH, t, N_KV_HEADS * LANES), bf16), scratch_shapes=[pltpu.VMEM((SEQ, LANES), f32)],
        compiler_params=_cparams(("parallel", "parallel")),
    )(qkv, *tabs)
    return q, kv


def _attn_fwd(prep, name):
    q_all, kv_all = prep
    t = q_all.shape[1]
    nb = t // SEQ
    n_blk = SEQ // ATTN_BLOCK

    def body(q_ref, k_ref, v_ref, o_ref, lse_ref, ob, lb, o0, o1, o2, l0, l1, l2, ss):
        cur_ok, prev_ok, low = _attn_masks()
        onat, lnat = (o0, o1, o2), (l0, l1, l2)
        for bi, d in enumerate(DILATIONS):
            qd, kd, vd = q_ref.at[bi], k_ref.at[0, bi], v_ref.at[0, bi]
            per_res = n_blk // d
            use_prev = per_res > 1

            def scores(n, carry):
                start = pl.multiple_of(n * ATTN_BLOCK, ATTN_BLOCK)
                has_prev = (n % per_res) != 0
                pstart = pl.multiple_of(jnp.maximum(n - 1, 0) * ATTN_BLOCK, ATTN_BLOCK)
                qb = qd[pl.ds(start, ATTN_BLOCK), :]
                kc = kd[pl.ds(start, ATTN_BLOCK), :]
                if use_prev:
                    kp = kd[pl.ds(pstart, ATTN_BLOCK), :]
                for a in range(2):
                    qa = jnp.where(low if a == 0 else ~low, qb, jnp.zeros_like(qb))
                    ss[2 * n + a, :, 0:ATTN_BLOCK] = jnp.where(cur_ok, _nt(qa, kc), NEG_INF)
                    if use_prev:
                        ss[2 * n + a, :, ATTN_BLOCK:2 * ATTN_BLOCK] = jnp.where(prev_ok & has_prev, _nt(qa, kp), NEG_INF)
                return carry

            def softmax_pv(n, carry):
                start = pl.multiple_of(n * ATTN_BLOCK, ATTN_BLOCK)
                pstart = pl.multiple_of(jnp.maximum(n - 1, 0) * ATTN_BLOCK, ATTN_BLOCK)
                vc = vd[pl.ds(start, ATTN_BLOCK), :]
                if use_prev:
                    vp = vd[pl.ds(pstart, ATTN_BLOCK), :]
                outs, lses = [], []
                for a in range(2):
                    sc = ss[2 * n + a, :, 0:ATTN_BLOCK]
                    if use_prev:
                        sp = ss[2 * n + a, :, ATTN_BLOCK:2 * ATTN_BLOCK]
                        m = jnp.max(jnp.maximum(sc, sp), axis=1, keepdims=True)
                        pc, pp = jnp.exp(sc - m), jnp.exp(sp - m)
                        den = jnp.sum(pc + pp, axis=1, keepdims=True)
                        acc = _nn(pc.astype(bf16), vc) + _nn(pp.astype(bf16), vp)
                    else:
                        m = jnp.max(sc, axis=1, keepdims=True)
                        pc = jnp.exp(sc - m)
                        den = jnp.sum(pc, axis=1, keepdims=True)
                        acc = _nn(pc.astype(bf16), vc)
                    outs.append(acc * (1.0 / den))
                    lses.append(m + jnp.log(den))
                ob[pl.ds(start, ATTN_BLOCK), :] = jnp.where(low, outs[0], outs[1])
                lb[pl.ds(start, ATTN_BLOCK), :] = jnp.where(low, lses[0], lses[1])
                return carry

            lax.fori_loop(0, n_blk, scores, 0, unroll=ATTN_UNROLL)
            lax.fori_loop(0, n_blk, softmax_pv, 0, unroll=ATTN_UNROLL)
            _interleave_store(ob, onat[bi], d, False)
            _interleave_store(lb, lnat[bi], d, False)
        la, lbb, lc = l0[...], l1[...], l2[...]
        lm = jnp.maximum(jnp.maximum(la, lbb), lc)
        wa, wb, wc = jnp.exp(la - lm), jnp.exp(lbb - lm), jnp.exp(lc - lm)
        ws = wa + wb + wc
        o_ref[...] = (wa * o0[...] + wb * o1[...] + wc * o2[...]) / ws
        lse_ref[...] = lm + jnp.log(ws)

    def col(jj):
        return pl.BlockSpec((SEQ, LANES), lambda b, j: (b, jj if jj is not None else j))

    fs = pltpu.VMEM((SEQ, LANES), f32)
    return pl.pallas_call(
        body, name=name, grid=(nb, ATTN_WIDTH // LANES),
        in_specs=[pl.BlockSpec((N_BRANCH, SEQ, LANES), lambda b, j: (0, b, j)),
                  pl.BlockSpec((1, N_BRANCH, SEQ, LANES), lambda b, j: (0, 0, b, j // 2)),
                  pl.BlockSpec((1, N_BRANCH, SEQ, LANES), lambda b, j: (1, 0, b, j // 2))],
        out_specs=[col(None), col(None)],
        out_shape=[jax.ShapeDtypeStruct((t, ATTN_WIDTH), f32), jax.ShapeDtypeStruct((t, ATTN_WIDTH), f32)],
        scratch_shapes=[fs, fs, fs, fs, fs, fs, fs, fs, pltpu.VMEM((2 * n_blk, ATTN_BLOCK, 2 * ATTN_BLOCK), f32)],
        compiler_params=_cparams(("parallel", "parallel")),
    )(q_all, kv_all, kv_all)


def _attn_bwd(prep, tabs, o, lse, do, name):
    q_all, kv_all = prep
    t = q_all.shape[1]
    nb = t // SEQ
    n_blk = SEQ // ATTN_BLOCK
    n_j = ATTN_WIDTH // LANES

    def body(q_ref, k_ref, v_ref, c_ref, s1_ref, s2_ref, o_ref, lse_ref, do_ref, dq_ref, dk_ref, dv_ref,
             dl, dod, lsd, dld, dqd, dkd, dvd, dqa, dka, dva, pb, dsb, dk_acc, dv_acc):
        j = pl.program_id(1)
        pb[2 * n_blk:2 * n_blk + 2] = jnp.zeros((2, ATTN_BLOCK, 2 * ATTN_BLOCK), bf16)
        dsb[2 * n_blk:2 * n_blk + 2] = jnp.zeros((2, ATTN_BLOCK, 2 * ATTN_BLOCK), bf16)
        kvh = j // 2
        cur_ok, prev_ok, low = _attn_masks()
        lowfull = lax.broadcasted_iota(jnp.int32, (SEQ, LANES), 1) < HEAD_DIM
        c, s1, s2 = c_ref[...], s1_ref[...], s2_ref[...]
        prod = do_ref[...] * o_ref[...]
        d_lo = jnp.sum(jnp.where(lowfull, prod, 0.0), axis=1, keepdims=True)
        d_hi = jnp.sum(jnp.where(lowfull, 0.0, prod), axis=1, keepdims=True)
        dl[...] = jnp.where(lowfull, d_lo, d_hi)
        dqa[...] = jnp.zeros_like(dqa)
        dka[...] = jnp.zeros_like(dka)
        dva[...] = jnp.zeros_like(dva)
        for bi, d in enumerate(DILATIONS):
            qd, kd, vd = q_ref.at[bi], k_ref.at[0, bi], v_ref.at[0, bi]
            _deinterleave(do_ref, dod, d, bf16)
            _deinterleave(lse_ref, lsd, d, f32)
            _deinterleave(dl, dld, d, f32)
            per_res = n_blk // d
            use_prev = per_res > 1
            curl, prevl = slice(0, ATTN_BLOCK), slice(ATTN_BLOCK, 2 * ATTN_BLOCK)

            def halves(x):
                zero = jnp.zeros_like(x)
                return jnp.where(low, x, zero), jnp.where(low, zero, x)

            def probs(n, carry):
                start = pl.multiple_of(n * ATTN_BLOCK, ATTN_BLOCK)
                has_prev = (n % per_res) != 0
                pstart = pl.multiple_of(jnp.maximum(n - 1, 0) * ATTN_BLOCK, ATTN_BLOCK)
                cur, prev = pl.ds(start, ATTN_BLOCK), pl.ds(pstart, ATTN_BLOCK)
                qas, doas = halves(qd[cur, :]), halves(dod[cur, :])
                kc, vc = kd[cur, :], vd[cur, :]
                if use_prev:
                    kp, vp = kd[prev, :], vd[prev, :]
                lsb, dlb = lsd[cur, :], dld[cur, :]
                for a in range(2):
                    ls = lsb[:, a * HEAD_DIM:a * HEAD_DIM + 1]
                    de = dlb[:, a * HEAD_DIM:a * HEAD_DIM + 1]
                    pc = jnp.exp(jnp.where(cur_ok, _nt(qas[a], kc), NEG_INF) - ls)
                    pb[2 * n + a, :, curl] = pc.astype(bf16)
                    dsb[2 * n + a, :, curl] = (pc * (_nt(doas[a], vc) - de)).astype(bf16)
                    if use_prev:
                        pp = jnp.exp(jnp.where(prev_ok & has_prev, _nt(qas[a], kp), NEG_INF) - ls)
                        pb[2 * n + a, :, prevl] = pp.astype(bf16)
                        dsb[2 * n + a, :, prevl] = (pp * (_nt(doas[a], vp) - de)).astype(bf16)
                return carry

            def grads(n, carry):
                start = pl.multiple_of(n * ATTN_BLOCK, ATTN_BLOCK)
                pstart = pl.multiple_of(jnp.maximum(n - 1, 0) * ATTN_BLOCK, ATTN_BLOCK)
                nstart = pl.multiple_of(jnp.minimum(n + 1, n_blk - 1) * ATTN_BLOCK, ATTN_BLOCK)
                cur, prev, nxt = pl.ds(start, ATTN_BLOCK), pl.ds(pstart, ATTN_BLOCK), pl.ds(nstart, ATTN_BLOCK)
                kc = kd[cur, :]
                dqs = [_nn(dsb[2 * n + a, :, curl], kc) for a in range(2)]
                q_rows, do_rows = list(halves(qd[cur, :])), list(halves(dod[cur, :]))
                ds_rows, p_rows = [dsb[2 * n + a, :, curl] for a in range(2)], [pb[2 * n + a, :, curl] for a in range(2)]
                if use_prev:
                    kp = kd[prev, :]
                    dqs = [dqs[a] + _nn(dsb[2 * n + a, :, prevl], kp) for a in range(2)]
                    q_rows += list(halves(qd[nxt, :]))
                    do_rows += list(halves(dod[nxt, :]))
                    ds_rows += [dsb[2 * n + 2 + a, :, prevl] for a in range(2)]
                    p_rows += [pb[2 * n + 2 + a, :, prevl] for a in range(2)]
                dqd[cur, :] = jnp.where(low, dqs[0], dqs[1])
                dkd[cur, :] = _tn(jnp.concatenate(ds_rows, axis=0), jnp.concatenate(q_rows, axis=0))
                dvd[cur, :] = _tn(jnp.concatenate(p_rows, axis=0), jnp.concatenate(do_rows, axis=0))
                return carry

            lax.fori_loop(0, n_blk, probs, 0, unroll=ATTN_UNROLL)
            lax.fori_loop(0, n_blk, grads, 0, unroll=ATTN_UNROLL)
            _interleave_store(dqd, dqa, d, True)
            _interleave_store(dkd, dka, d, True)
            _interleave_store(dvd, dva, d, True)
        dq_ref[...] = _rot_t(dqa[...] * (HEAD_DIM ** -0.5), c, s1, s2).astype(bf16)
        dkf = dka[...]
        dkf = _rot_t(dkf + pltpu.roll(dkf, HEAD_DIM, 1), c, s1, s2)
        dvf = dva[...]
        dvf = dvf + pltpu.roll(dvf, HEAD_DIM, 1)
        mine = (lax.broadcasted_iota(jnp.int32, (SEQ, LANES), 1) // HEAD_DIM) == kvh
        dkc_, dvc_ = jnp.where(mine, dkf, 0.0), jnp.where(mine, dvf, 0.0)

        @pl.when(j == 0)
        def _():
            dk_acc[...] = dkc_
            dv_acc[...] = dvc_

        @pl.when(j > 0)
        def _():
            dk_acc[...] += dkc_
            dv_acc[...] += dvc_

        @pl.when(j == n_j - 1)
        def _():
            dk_ref[...] = dk_acc[...].astype(bf16)
            dv_ref[...] = dv_acc[...].astype(bf16)

    def col(jj):
        return pl.BlockSpec((SEQ, LANES), lambda b, j: (b, jj if jj is not None else j))

    tab = pl.BlockSpec((SEQ, LANES), lambda b, j: (b, 0))
    fs = pltpu.VMEM((SEQ, LANES), f32)
    hs = pltpu.VMEM((SEQ, LANES), bf16)
    return pl.pallas_call(
        body, name=name, grid=(nb, n_j),
        in_specs=[pl.BlockSpec((N_BRANCH, SEQ, LANES), lambda b, j: (0, b, j)),
                  pl.BlockSpec((1, N_BRANCH, SEQ, LANES), lambda b, j: (0, 0, b, j // 2)),
                  pl.BlockSpec((1, N_BRANCH, SEQ, LANES), lambda b, j: (1, 0, b, j // 2)),
                  tab, tab, tab, col(None), col(None), col(None)],
        out_specs=[col(None), tab, tab],
        out_shape=[jax.ShapeDtypeStruct((t, ATTN_WIDTH), bf16), jax.ShapeDtypeStruct((t, LANES), bf16), jax.ShapeDtypeStruct((t, LANES), bf16)],
        scratch_shapes=[fs, hs, fs, fs, fs, fs, fs, fs, fs, fs,
                        pltpu.VMEM((2 * n_blk + 2, ATTN_BLOCK, 2 * ATTN_BLOCK), bf16), pltpu.VMEM((2 * n_blk + 2, ATTN_BLOCK, 2 * ATTN_BLOCK), bf16), fs, fs],
        compiler_params=_cparams(("parallel", "arbitrary")),
    )(q_all, kv_all, kv_all, *tabs, o, lse, do)


def _conv_pre(x, w_ref, b_ref, row):
    shifted = [x] + [jnp.where(row >= s, pltpu.roll(x, s, 0), 0.0) for s in range(1, CONV_WIDTH)]
    pre = b_ref[...] + w_ref[CONV_WIDTH - 1:CONV_WIDTH, :] * x
    for s in range(1, CONV_WIDTH):
        pre = pre + w_ref[CONV_WIDTH - 1 - s:CONV_WIDTH - s, :] * shifted[s]
    return pre, shifted


def _conv_fwd(x, w, b, name, tc=512):
    t, ch = x.shape

    def body(x_ref, w_ref, b_ref, o_ref):
        row = lax.broadcasted_iota(jnp.int32, (SEQ, tc), 0)
        pre, _ = _conv_pre(x_ref[...], w_ref, b_ref, row)
        o_ref[...] = _silu(pre)

    xs = pl.BlockSpec((SEQ, tc), lambda i, j: (i, j))
    return pl.pallas_call(
        body, name=name, grid=(t // SEQ, ch // tc),
        in_specs=[xs, pl.BlockSpec((CONV_WIDTH, tc), lambda i, j: (0, j)), pl.BlockSpec((1, tc), lambda i, j: (0, j))],
        out_specs=xs, out_shape=jax.ShapeDtypeStruct((t, ch), f32),
        compiler_params=_cparams(("parallel", "parallel")),
    )(x, w, b)


def _conv_bwd(x, w, b, dact, name, tc=512):
    t, ch = x.shape

    def body(x_ref, w_ref, b_ref, d_ref, dx_ref, dw_ref, db_ref):
        row = lax.broadcasted_iota(jnp.int32, (SEQ, tc), 0)
        pre, shifted = _conv_pre(x_ref[...], w_ref, b_ref, row)
        dpre = d_ref[...] * _dsilu(pre)
        dx = w_ref[CONV_WIDTH - 1:CONV_WIDTH, :] * dpre
        for s in range(1, CONV_WIDTH):
            dx = dx + w_ref[CONV_WIDTH - 1 - s:CONV_WIDTH - s, :] * jnp.where(row < SEQ - s, pltpu.roll(dpre, SEQ - s, 0), 0.0)
        dx_ref[...] = dx.astype(bf16)
        first = pl.program_id(1) == 0
        parts = [jnp.sum(dpre * shifted[CONV_WIDTH - 1 - k], axis=0, keepdims=True) for k in range(CONV_WIDTH)]
        dbp = jnp.sum(dpre, axis=0, keepdims=True)

        @pl.when(first)
        def _():
            for k in range(CONV_WIDTH):
                dw_ref[k:k + 1, :] = parts[k]
            db_ref[...] = dbp

        @pl.when(jnp.logical_not(first))
        def _():
            for k in range(CONV_WIDTH):
                dw_ref[k:k + 1, :] += parts[k]
            db_ref[...] += dbp

    xs = pl.BlockSpec((SEQ, tc), lambda j, i: (i, j))
    ws = pl.BlockSpec((CONV_WIDTH, tc), lambda j, i: (0, j))
    bs = pl.BlockSpec((1, tc), lambda j, i: (0, j))
    return pl.pallas_call(
        body, name=name, grid=(ch // tc, t // SEQ),
        in_specs=[xs, ws, bs, xs], out_specs=[xs, ws, bs],
        out_shape=[jax.ShapeDtypeStruct((t, ch), bf16), jax.ShapeDtypeStruct((CONV_WIDTH, ch), f32), jax.ShapeDtypeStruct((1, ch), f32)],
        compiler_params=_cparams(("parallel", "arbitrary")),
    )(x, w, b, dact)


GROUP_W = SSM_INNER // SSM_GROUPS
HEADS_PER_GROUP = SSM_HEADS // SSM_GROUPS


def _split3(x):
    hi = x.astype(bf16)
    r1 = x - hi.astype(f32)
    mid = r1.astype(bf16)
    lo = (r1 - mid.astype(f32)).astype(bf16)
    return hi, mid, lo


def _dot_exact(x, sel, dims, x_is_lhs=True):
    parts = _split3(x)
    if x_is_lhs:
        return _dot(parts[0], sel, dims) + _dot(parts[1], sel, dims) + _dot(parts[2], sel, dims)
    return _dot(sel, parts[0], dims) + _dot(sel, parts[1], dims) + _dot(sel, parts[2], dims)


def _ssd_common(xbc_ref, dt_ref, bias_ref, alog_ref):
    r = lax.broadcasted_iota(jnp.int32, (CHUNK, CHUNK), 0)
    cidx = lax.broadcasted_iota(jnp.int32, (CHUNK, CHUNK), 1)
    causal = r >= cidx
    tril = causal.astype(bf16)
    expand = (lax.broadcasted_iota(jnp.int32, (CHUNK, SSM_INNER), 0)
              == lax.broadcasted_iota(jnp.int32, (CHUNK, SSM_INNER), 1) // HEAD_DIM).astype(bf16)
    head_lane = cidx < SSM_HEADS
    dtp = dt_ref[...] + bias_ref[...]
    dt = jnp.where(head_lane, _softplus(dtp), 0.0)
    a_neg = -jnp.exp(alog_ref[...])
    a = dt * a_neg
    nn_dims = ((1,), (0,))
    cs = _dot_exact(a, tril, nn_dims, x_is_lhs=False)
    dt_e = _dot_exact(dt, expand, nn_dims)
    cs_e = _dot_exact(cs, expand, nn_dims)
    xs = xbc_ref[:, 0:SSM_INNER]
    xg = xs * dt_e
    ecs = jnp.exp(cs_e)
    cs_last = cs_e[CHUNK - 1:CHUNK, :]
    dse = jnp.exp(cs_last - cs_e)
    cde = jnp.exp(cs_last)
    return dict(r=r, cidx=cidx, causal=causal, tril=tril, expand=expand, head_lane=head_lane, dtp=dtp, dt=dt, a_neg=a_neg,
                cs=cs, cst=cs.T, dt_e=dt_e, cs_e=cs_e, xs=xs, xg=xg, ecs=ecs, dse=dse, cde=cde)


def _decay_mat(q, h):
    return jnp.exp(jnp.where(q["causal"], q["cs"][:, h:h + 1] - q["cst"][h:h + 1, :], NEG_INF))


def _gate_norm(y, z, nw, gate=None):
    y2 = y * (_silu(z) if gate is None else gate)
    outs, xhats, rs = [], [], []
    for g in range(SSM_GROUPS):
        sl = slice(g * GROUP_W, (g + 1) * GROUP_W)
        yg = y2[:, sl]
        r = lax.rsqrt(jnp.mean(yg * yg, axis=-1, keepdims=True) + EPS)
        xhats.append(yg * r)
        rs.append(r)
        outs.append(yg * r * nw[:, sl])
    return y2, outs, xhats, rs


def _ssd_fwd(xbc, z, dtp, params, name):
    t = xbc.shape[0]
    n_chunk = SEQ // CHUNK
    low = None

    def body(xbc_ref, z_ref, dt_ref, bias_ref, alog_ref, dskip_ref, nw_ref, yn_ref, y_ref, hs_ref, h_scr):
        @pl.when(pl.program_id(1) == 0)
        def _():
            h_scr[...] = jnp.zeros_like(h_scr)

        q = _ssd_common(xbc_ref, dt_ref, bias_ref, alog_ref)
        low = lax.broadcasted_iota(jnp.int32, (CHUNK, LANES), 1) < HEAD_DIM
        xgb = q["xg"].astype(bf16)
        wst = (q["xg"] * q["dse"]).astype(bf16)
        hs_ref[0] = h_scr[...]
        ys = []
        for g in range(SSM_GROUPS):
            gl = slice(g * GROUP_W, (g + 1) * GROUP_W)
            bg = xbc_ref[:, SSM_INNER + g * D_STATE:SSM_INNER + (g + 1) * D_STATE].astype(bf16)
            cg = xbc_ref[:, SSM_INNER + SSM_GROUPS * D_STATE + g * D_STATE:SSM_INNER + SSM_GROUPS * D_STATE + (g + 1) * D_STATE].astype(bf16)
            cb = _nt(cg, bg)
            hg = h_scr[g]
            yoff = _nn(cg, hg.astype(bf16)) * q["ecs"][:, gl]
            pieces = []
            for i in range(HEADS_PER_GROUP // 2):
                h0 = g * HEADS_PER_GROUP + 2 * i
                xp = xgb[:, h0 * HEAD_DIM:(h0 + 2) * HEAD_DIM]
                m0 = (cb * _decay_mat(q, h0)).astype(bf16)
                m1 = (cb * _decay_mat(q, h0 + 1)).astype(bf16)
                zero = jnp.zeros_like(xp)
                pieces.append(_nn(m0, jnp.where(low, xp, zero)) + _nn(m1, jnp.where(low, zero, xp)))
            ys.append(jnp.concatenate(pieces, axis=1) + yoff + dskip_ref[:, gl] * q["xs"][:, gl])
            h_scr[g] = hg * q["cde"][:, gl] + _tn(bg, wst[:, gl])
        y = jnp.concatenate(ys, axis=1)
        y_ref[...] = y
        _, outs, _, _ = _gate_norm(y, z_ref[...], nw_ref[...])
        yn_ref[...] = jnp.concatenate(outs, axis=1).astype(bf16)

    def rows(w):
        return pl.BlockSpec((CHUNK, w), lambda b, c: (b * n_chunk + c, 0))

    def par(w):
        return pl.BlockSpec((1, w), lambda b, c: (0, 0))

    return pl.pallas_call(
        body, name=name, grid=(t // SEQ, n_chunk),
        in_specs=[rows(CONV_CH), rows(SSM_INNER), rows(LANES), par(LANES), par(LANES), par(SSM_INNER), par(SSM_INNER)],
        out_specs=[rows(SSM_INNER), rows(SSM_INNER), pl.BlockSpec((1, SSM_GROUPS, D_STATE, GROUP_W), lambda b, c: (b * n_chunk + c, 0, 0, 0))],
        out_shape=[jax.ShapeDtypeStruct((t, SSM_INNER), bf16), jax.ShapeDtypeStruct((t, SSM_INNER), f32),
                   jax.ShapeDtypeStruct((t // CHUNK, SSM_GROUPS, D_STATE, GROUP_W), f32)],
        scratch_shapes=[pltpu.VMEM((SSM_GROUPS, D_STATE, GROUP_W), f32)],
        compiler_params=_cparams(("parallel", "arbitrary")),
    )(xbc, z, dtp, *params)


def _ssd_bwd(xbc, z, dtp, y, hs, dyn, params, name):
    t = xbc.shape[0]
    n_chunk = SEQ // CHUNK

    def body(xbc_ref, z_ref, dt_ref, y_ref, hs_ref, dyn_ref, bias_ref, alog_ref, dskip_ref, nw_ref,
             dxbc_ref, dz_ref, ddt_ref, dnw_ref, dds_ref, dal_ref, dbi_ref, dh_scr):
        @pl.when(pl.program_id(1) == 0)
        def _():
            dh_scr[...] = jnp.zeros_like(dh_scr)

        q = _ssd_common(xbc_ref, dt_ref, bias_ref, alog_ref)
        low = lax.broadcasted_iota(jnp.int32, (CHUNK, LANES), 1) < HEAD_DIM
        last_row = lax.broadcasted_iota(jnp.int32, (CHUNK, GROUP_W), 0) == CHUNK - 1
        xs, xg = q["xs"], q["xg"]
        xgb = xg.astype(bf16)
        wf = xg * q["dse"]
        wst = wf.astype(bf16)
        zz = z_ref[...]
        yy = y_ref[...]
        sz, dsz = _silu_and_grad(zz)
        y2, _, xhats, rs = _gate_norm(yy, zz, nw_ref[...], gate=sz)
        dyn_ = dyn_ref[...]
        dy2s, dnws = [], []
        for g in range(SSM_GROUPS):
            gl = slice(g * GROUP_W, (g + 1) * GROUP_W)
            gw = dyn_[:, gl] * nw_ref[:, gl]
            dy2s.append(rs[g] * (gw - xhats[g] * jnp.mean(gw * xhats[g], axis=-1, keepdims=True)))
            dnws.append(_rowsum8(dyn_[:, gl] * xhats[g]))
        dy2 = jnp.concatenate(dy2s, axis=1)
        dy = dy2 * sz
        dz_ref[...] = (dy2 * yy * dsz).astype(bf16)
        dnw_p = jnp.concatenate(dnws, axis=1)
        dds_p = _rowsum8(dy * xs)
        dyb = dy.astype(bf16)
        gfull = (dy * q["ecs"]).astype(bf16)
        dcs_c = jnp.zeros((CHUNK, CHUNK), f32)
        dcs_r = jnp.zeros((CHUNK, CHUNK), f32)
        dcs_e_parts, dxg_parts = [], []
        for g in range(SSM_GROUPS):
            gl = slice(g * GROUP_W, (g + 1) * GROUP_W)
            bsl = slice(SSM_INNER + g * D_STATE, SSM_INNER + (g + 1) * D_STATE)
            csl = slice(SSM_INNER + SSM_GROUPS * D_STATE + g * D_STATE, SSM_INNER + SSM_GROUPS * D_STATE + (g + 1) * D_STATE)
            bg = xbc_ref[:, bsl].astype(bf16)
            cg = xbc_ref[:, csl].astype(bf16)
            cb = _nt(cg, bg)
            hg = hs_ref[0, g]
            hgb = hg.astype(bf16)
            dhn = dh_scr[g]
            dhnb = dhn.astype(bf16)
            yoff = _nn(cg, hgb) * q["ecs"][:, gl]
            dw_ = _nn(bg, dhnb)
            r_e = dw_ * wf[:, gl]
            to_last = jnp.sum(r_e, axis=0, keepdims=True) + jnp.sum(dhn * hg, axis=0, keepdims=True) * q["cde"][:, gl]
            dcs_e_parts.append(dy[:, gl] * yoff - r_e + jnp.where(last_row, to_last, 0.0))
            dcb = jnp.zeros((CHUNK, CHUNK), f32)
            dxg_pairs = []
            for i in range(HEADS_PER_GROUP // 2):
                h0 = g * HEADS_PER_GROUP + 2 * i
                psl = slice(h0 * HEAD_DIM, (h0 + 2) * HEAD_DIM)
                xp = xgb[:, psl]
                dyp = dyb[:, psl]
                zero = jnp.zeros_like(dyp)
                tns = []
                for a in range(2):
                    h = h0 + a
                    lm = _decay_mat(q, h)
                    m = cb * lm
                    dm = _nt(jnp.where(low, dyp, zero) if a == 0 else jnp.where(low, zero, dyp), xp)
                    dcb = dcb + dm * lm
                    nmat = dm * m
                    dcs_c = dcs_c + jnp.where(q["cidx"] == h, jnp.sum(nmat, axis=1, keepdims=True), 0.0)
                    dcs_r = dcs_r + jnp.where(q["r"] == h, jnp.sum(nmat, axis=0, keepdims=True), 0.0)
                    tns.append(_tn(m.astype(bf16), dyp))
                dxg_pairs.append(jnp.where(low, tns[0], tns[1]))
            dxg_parts.append(jnp.concatenate(dxg_pairs, axis=1) + dw_ * q["dse"][:, gl])
            dcbb = dcb.astype(bf16)
            dxbc_ref[:, csl] = _nt(gfull[:, gl], hgb) + _nn(dcbb, bg)
            dxbc_ref[:, bsl] = _nt(wst[:, gl], dhnb) + _tn(dcbb, cg)
            dh_scr[g] = dhn * q["cde"][:, gl] + _tn(cg, gfull[:, gl])
        dxg = jnp.concatenate(dxg_parts, axis=1)
        dcs_e = jnp.concatenate(dcs_e_parts, axis=1)
        dxbc_ref[:, 0:SSM_INNER] = dskip_ref[...] * dy + dxg * q["dt_e"]
        dcs = dcs_c - dcs_r.T + _dot_exact(dcs_e, q["expand"], ((1,), (1,)))
        triu = (q["cidx"] >= q["r"]).astype(bf16)
        da = _dot_exact(dcs, triu, ((1,), (0,)), x_is_lhs=False)
        ddt = _dot_exact(dxg * xs, q["expand"], ((1,), (1,))) + da * q["a_neg"]
        ddtp = jnp.where(q["head_lane"], ddt * _sigmoid(q["dtp"]), 0.0)
        ddt_ref[...] = ddtp.astype(bf16)
        dal_p = _rowsum8(da * q["dt"]) * q["a_neg"]
        dbi_p = _rowsum8(ddtp)
        first = (pl.program_id(0) == 0) & (pl.program_id(1) == 0)

        @pl.when(first)
        def _():
            dnw_ref[...] = dnw_p
            dds_ref[...] = dds_p
            dal_ref[...] = dal_p
            dbi_ref[...] = dbi_p

        @pl.when(jnp.logical_not(first))
        def _():
            dnw_ref[...] += dnw_p
            dds_ref[...] += dds_p
            dal_ref[...] += dal_p
            dbi_ref[...] += dbi_p

    def rows(w):
        return pl.BlockSpec((CHUNK, w), lambda b, c: (b * n_chunk + n_chunk - 1 - c, 0))

    def par(w):
        return pl.BlockSpec((1, w), lambda b, c: (0, 0))

    def acc(w):
        return pl.BlockSpec((SUBLANES, w), lambda b, c: (0, 0))

    return pl.pallas_call(
        body, name=name, grid=(t // SEQ, n_chunk),
        in_specs=[rows(CONV_CH), rows(SSM_INNER), rows(LANES), rows(SSM_INNER),
                  pl.BlockSpec((1, SSM_GROUPS, D_STATE, GROUP_W), lambda b, c: (b * n_chunk + n_chunk - 1 - c, 0, 0, 0)),
                  rows(SSM_INNER), par(LANES), par(LANES), par(SSM_INNER), par(SSM_INNER)],
        out_specs=[rows(CONV_CH), rows(SSM_INNER), rows(LANES), acc(SSM_INNER), acc(SSM_INNER), acc(LANES), acc(LANES)],
        out_shape=[jax.ShapeDtypeStruct((t, CONV_CH), f32), jax.ShapeDtypeStruct((t, SSM_INNER), bf16), jax.ShapeDtypeStruct((t, LANES), bf16),
                   jax.ShapeDtypeStruct((SUBLANES, SSM_INNER), f32), jax.ShapeDtypeStruct((SUBLANES, SSM_INNER), f32),
                   jax.ShapeDtypeStruct((SUBLANES, LANES), f32), jax.ShapeDtypeStruct((SUBLANES, LANES), f32)],
        scratch_shapes=[pltpu.VMEM((SSM_GROUPS, D_STATE, GROUP_W), f32)],
        compiler_params=_cparams(("arbitrary", "arbitrary")),
    )(xbc, z, dtp, y, hs, dyn, *params)


def _adamw_update(g, w, m, v):
    mm = ADAM_B1 * m + (1.0 - ADAM_B1) * g
    vv = ADAM_B2 * v + (1.0 - ADAM_B2) * (g * g)
    m_hat = mm / (1.0 - ADAM_B1 ** ADAM_STEP)
    v_hat = vv / (1.0 - ADAM_B2 ** ADAM_STEP)
    return -ADAM_LR * (m_hat / (jnp.sqrt(v_hat) + ADAM_EPS) + ADAM_WD * w), mm, vv


def _adamw(g_parts, w, m, v, name):
    rows, width = w.shape
    n = len(g_parts)
    tr = _row_tile(rows)

    def body(*refs):
        g_refs, (w_ref, m_ref, v_ref, g_out, d_out, m_out, v_out) = refs[:n], refs[n:]
        g = g_refs[0][...].astype(f32)
        for r in g_refs[1:]:
            g = g + r[...].astype(f32)
        g_out[...] = g
        d_out[...], m_out[...], v_out[...] = _adamw_update(g, w_ref[...], m_ref[...], v_ref[...])

    spec = pl.BlockSpec((tr, width), lambda i: (i, 0))
    return pl.pallas_call(
        body, name=name, grid=(rows // tr,), in_specs=[spec] * (n + 3), out_specs=[spec] * 4,
        out_shape=[jax.ShapeDtypeStruct((rows, width), f32)] * 4, compiler_params=_cparams(("parallel",)),
    )(*g_parts, w, m, v)


def _adamw_layers(landed, w, m, v, name):
    depth, rows, width = w.shape
    tr = _row_tile(rows)
    n_i = rows // tr

    def body(*refs):
        part_refs, (w_ref, m_ref, v_ref, g_out, d_out, m_out, v_out) = refs[:depth * N_DEV], refs[depth * N_DEV:]
        for l in range(depth):
            @pl.when(pl.program_id(0) == l)
            def _(l=l):
                g = part_refs[l * N_DEV][0].astype(f32)
                for r in part_refs[l * N_DEV + 1:(l + 1) * N_DEV]:
                    g = g + r[0].astype(f32)
                g_out[0] = g
                d_out[0], m_out[0], v_out[0] = _adamw_update(g, w_ref[0], m_ref[0], v_ref[0])

    def part_spec(l, p):
        return pl.BlockSpec((1, tr, width), lambda ll, i: (p, jnp.where(ll == l, i, jnp.where(ll < l, 0, n_i - 1)), 0))

    state = pl.BlockSpec((1, tr, width), lambda ll, i: (ll, i, 0))
    return pl.pallas_call(
        body, name=name, grid=(depth, n_i),
        in_specs=[part_spec(l, p) for l in range(depth) for p in range(N_DEV)] + [state] * 3, out_specs=[state] * 4,
        out_shape=[jax.ShapeDtypeStruct(w.shape, f32)] * 4, compiler_params=_cparams(("arbitrary", "arbitrary")),
    )(*[landed[l] for l in range(depth) for _ in range(N_DEV)], w, m, v)


def _row_tile(rows, cap=512):
    for cand in range(min(rows, cap) // SUBLANES * SUBLANES, 0, -SUBLANES):
        if rows % cand == 0:
            return cand
    return rows


def _cols_from_devices(g, width, name):
    n_dev, depth, a, b = g.shape

    def body(g_ref, o_ref):
        for i in range(n_dev):
            o_ref[0, :, i * b:(i + 1) * b] = g_ref[i, 0]
        if width > n_dev * b:
            o_ref[0, :, n_dev * b:width] = jnp.zeros((a, width - n_dev * b), o_ref.dtype)

    return pl.pallas_call(
        body, name=name, grid=(depth,), in_specs=[pl.BlockSpec((n_dev, 1, a, b), lambda l: (0, l, 0, 0))],
        out_specs=pl.BlockSpec((1, a, width), lambda l: (l, 0, 0)), out_shape=jax.ShapeDtypeStruct((depth, a, width), g.dtype),
        compiler_params=_cparams(("parallel",)),
    )(g)


def _devices_from_cols(per_layer, b, name, tr=256):
    depth = len(per_layer)
    a, width = per_layer[0].shape

    def body(*refs):
        o_ref = refs[depth]
        for l in range(depth):
            for i in range(N_DEV):
                o_ref[i, l] = refs[l][:, i * b:(i + 1) * b]

    return pl.pallas_call(
        body, name=name, grid=(a // tr,), in_specs=[pl.BlockSpec((tr, width), lambda r: (r, 0))] * depth,
        out_specs=pl.BlockSpec((N_DEV, depth, tr, b), lambda r: (0, 0, r, 0)),
        out_shape=jax.ShapeDtypeStruct((N_DEV, depth, a, b), per_layer[0].dtype), compiler_params=_cparams(("parallel",)),
    )(*per_layer)


def _me():
    return lax.axis_index("x"), lax.axis_index("y"), lax.axis_index("c")


def _allgather_two_level(shards, name):
    n = len(shards)
    per = 7

    def body(*refs):
        ins, outs, token = refs[:n], refs[n:2 * n], refs[2 * n]
        send_sems, recv_sems, local_sems = refs[2 * n + 1:]
        token[...] = jnp.zeros_like(token)
        x, y, c = _me()
        me, sibling = (x, y, c), (x, y, 1 - c)
        chips = [(1 - x, y), (x, 1 - y), (1 - x, 1 - y)]

        def slot(a, p):
            return outs[a].at[4 * p[0] + 2 * p[1] + p[2]]

        def copy(a, k, block, to, src=None):
            return pltpu.make_async_remote_copy(
                src_ref=slot(a, block) if src is None else src, dst_ref=slot(a, block),
                send_sem=send_sems.at[a * per + k], recv_sem=recv_sems.at[a * per + k], device_id=to, device_id_type=MESH)

        mine = [pltpu.make_async_copy(ins[a], slot(a, me), local_sems.at[a]) for a in range(n)]
        for cp in mine:
            cp.start()
        first = []
        for a in range(n):
            first.append(copy(a, 0, me, sibling, src=ins[a]))
            first += [copy(a, 1 + j, me, (*chip, c), src=ins[a]) for j, chip in enumerate(chips)]
        for cp in first:
            cp.start()
        passed = []
        for j, chip in enumerate(chips):
            for a in range(n):
                copy(a, 1 + j, (*chip, c), me).wait_recv()
                fwd = copy(a, 4 + j, (*chip, c), sibling)
                fwd.start()
                passed.append(fwd)
        for a in range(n):
            copy(a, 0, sibling, me).wait_recv()
            for j, chip in enumerate(chips):
                copy(a, 4 + j, (*chip, 1 - c), me).wait_recv()
        for cp in first + passed:
            cp.wait_send()
        for cp in mine:
            cp.wait()

    outs = pl.pallas_call(
        body, name=name, in_specs=[ANY] * n, out_specs=[ANY] * n + [pl.BlockSpec(memory_space=pltpu.VMEM)],
        out_shape=[jax.ShapeDtypeStruct((N_DEV,) + s.shape, s.dtype) for s in shards] + [jax.ShapeDtypeStruct((SUBLANES, LANES), f32)],
        scratch_shapes=[pltpu.SemaphoreType.DMA((n * per,)), pltpu.SemaphoreType.DMA((n * per,)), pltpu.SemaphoreType.DMA((n,))],
    )(*shards)
    return outs[:n], outs[n]


def _allgather_direct(row, name):
    def body(in_ref, out_ref, send_sems, recv_sems, local_sem):
        x, y, c = _me()
        mine = out_ref.at[4 * x + 2 * y + c]
        local = pltpu.make_async_copy(in_ref, mine, local_sem)
        local.start()
        sends = []
        for k in range(1, N_DEV):
            px, py, pc = x ^ (k >> 2), y ^ ((k >> 1) & 1), c ^ (k & 1)
            sends.append(pltpu.make_async_remote_copy(
                src_ref=in_ref, dst_ref=mine, send_sem=send_sems.at[k - 1], recv_sem=recv_sems.at[k - 1],
                device_id=(px, py, pc), device_id_type=MESH))
        for cp in sends:
            cp.start()
        for k in range(1, N_DEV):
            px, py, pc = x ^ (k >> 2), y ^ ((k >> 1) & 1), c ^ (k & 1)
            theirs = out_ref.at[4 * px + 2 * py + pc]
            pltpu.make_async_remote_copy(
                src_ref=in_ref, dst_ref=theirs, send_sem=send_sems.at[k - 1], recv_sem=recv_sems.at[k - 1],
                device_id=(px, py, pc), device_id_type=MESH).wait_recv()
        for cp in sends:
            cp.wait_send()
        local.wait()

    return pl.pallas_call(
        body, name=name, in_specs=[ANY], out_specs=ANY, out_shape=jax.ShapeDtypeStruct((N_DEV,) + row.shape, row.dtype),
        scratch_shapes=[pltpu.SemaphoreType.DMA((N_DEV - 1,)), pltpu.SemaphoreType.DMA((N_DEV - 1,)), pltpu.SemaphoreType.DMA],
    )(row)


N_CHIP = N_DEV // 2
HBM = pl.BlockSpec(memory_space=pltpu.HBM)
SEM = pl.BlockSpec(memory_space=pltpu.SEMAPHORE)
EFFECT = pltpu.SideEffectType.DATAFLOW_SIDE_EFFECTING


def _peer(k):
    x, y, c = _me()
    return x ^ (k >> 2), y ^ ((k >> 1) & 1), c ^ (k & 1)


def _direct_copies(srcs, lands, send_sems, recv_sems, per_peer):
    x, y, c = _me()
    me = 4 * x + 2 * y + c
    copies = []
    for a in range(len(srcs)):
        for k in range(1, N_DEV):
            px, py, pc = _peer(k)
            piece = srcs[a].at[4 * px + 2 * py + pc] if per_peer else srcs[a]
            copies.append(pltpu.make_async_remote_copy(
                src_ref=piece, dst_ref=lands[a].at[me], send_sem=send_sems.at[a * (N_DEV - 1) + k - 1],
                recv_sem=recv_sems.at[a * (N_DEV - 1) + k - 1], device_id=(px, py, pc), device_id_type=MESH))
    return copies


def _direct_start(srcs, lands, per_peer, name):
    n = len(srcs)
    n_sem = n * (N_DEV - 1)

    def body(*refs):
        src_refs, land_refs = refs[:n], refs[n:2 * n]
        send_sems, recv_sems = refs[2 * n], refs[2 * n + 1]
        token = refs[-1]
        for cp in _direct_copies(src_refs, land_refs, send_sems, recv_sems, per_peer):
            cp.start()
        token[...] = jnp.zeros_like(token)

    outs = pl.pallas_call(
        body, name=name,
        out_shape=(pltpu.SemaphoreType.DMA((n_sem,)), pltpu.SemaphoreType.DMA((n_sem,)),
                   *[pltpu.HBM(s.shape, s.dtype) for s in srcs], *[pltpu.HBM(s.shape, s.dtype) for s in lands],
                   jax.ShapeDtypeStruct((SUBLANES, LANES), f32)),
        in_specs=[HBM] * (2 * n), out_specs=(SEM, SEM, *[HBM] * (2 * n), pl.BlockSpec(memory_space=pltpu.VMEM)),
        input_output_aliases={i: 2 + i for i in range(2 * n)},
        compiler_params=pltpu.CompilerParams(has_side_effects=EFFECT),
    )(*[pltpu.with_memory_space_constraint(s, pltpu.HBM) for s in srcs], *[pltpu.with_memory_space_constraint(s, pltpu.HBM) for s in lands])
    return outs[0], outs[1], outs[2:2 + n], outs[2 + n:2 + 2 * n], outs[-1]


def _direct_wait(send_sems, recv_sems, srcs, lands, after, per_peer, name):
    n = len(srcs)

    def body(*refs):
        src_refs, land_refs = refs[:n], refs[n:2 * n]
        s_sems, r_sems = refs[2 * n], refs[2 * n + 1]
        for cp in _direct_copies(src_refs, land_refs, s_sems, r_sems, per_peer):
            cp.wait_send()
            cp.wait_recv()

    outs = pl.pallas_call(
        body, name=name,
        out_shape=tuple(pltpu.HBM(s.shape, s.dtype) for s in list(srcs) + list(lands)),
        in_specs=[HBM] * (2 * n) + [SEM, SEM, ANY], out_specs=tuple([HBM] * (2 * n)),
        input_output_aliases={i: i for i in range(2 * n)},
        compiler_params=pltpu.CompilerParams(has_side_effects=EFFECT),
    )(*srcs, *lands, send_sems, recv_sems, after)
    return outs[n:]


def _row(v, width=None):
    v = v.reshape(1, -1).astype(f32)
    if width is not None and v.shape[1] < width:
        v = jnp.pad(v, ((0, 0), (0, width - v.shape[1])))
    return v


def _layer_params(p, l):
    return dict(
        norm_mix=_row(p["norm_mix"][l]), norm_ffn=_row(p["norm_ffn"][l]), conv_w=p["conv_w"][l], conv_b=_row(p["conv_b"][l]),
        ssd=(_row(p["dt_bias"][l], LANES), _row(p["a_log"][l], LANES), _row(jnp.repeat(p["d_skip"][l], HEAD_DIM)), _row(p["ssm_norm"][l])))


def _layer_fwd(h, w_in, rest, sp, tabs, l):
    tag = f"l{l}_"
    hn = _rmsnorm_fwd(h, sp["norm_mix"], tag + "norm_mix")
    qkv = _matmul(hn, w_in, mode="nn", n_out=QKV_WIDTH, tn=256, b_off=0, name=tag + "proj_qkv")
    z = _matmul(hn, w_in, mode="nn", n_out=SSM_INNER, tn=256, b_off=Z_OFF // 256, name=tag + "proj_z")
    xbc_pre = _matmul(hn, w_in, mode="nn", n_out=CONV_CH, tn=256, b_off=XBC_OFF // 256, name=tag + "proj_xbc")
    dtp = _matmul(hn, w_in, mode="nn", n_out=LANES, tn=LANES, b_off=DT_OFF // LANES, name=tag + "proj_dt")
    prep = _attn_prep(qkv, tabs, tag + "attn_prep")
    o, lse = _attn_fwd(prep, tag + "attn_fwd")
    xbc = _conv_fwd(xbc_pre, sp["conv_w"], sp["conv_b"], tag + "conv_fwd")
    yn, y, hs = _ssd_fwd(xbc, z, dtp, sp["ssd"], tag + "ssd_fwd")
    w_out, w_gate, w_up, w_down = rest(yn) if callable(rest) else rest
    h2 = _out_proj(o, yn, w_out, h, tag + "out_proj")
    hn2 = _rmsnorm_fwd(h2, sp["norm_ffn"], tag + "norm_ffn")
    g, u, act = _swiglu_fwd(hn2, w_gate, w_up, tag + "ffn_up")
    h3 = _matmul(act, w_down, mode="nn", tk=1408, add=h2, name=tag + "ffn_down")
    saved = dict(h=h, hn=hn, prep=prep, z=z, xbc_pre=xbc_pre, dtp=dtp, o=o, lse=lse, xbc=xbc, yn=yn, y=y, hs=hs, h2=h2, hn2=hn2, g=g, u=u, act=act,
                 rest=(w_out, w_gate, w_up, w_down))
    return h3, saved


def _layer_bwd(dh3, s, big, sp, tabs, l, gd=f32, after_ffn=None):
    tag = f"l{l}_"
    w_in, w_out, w_gate, w_up, w_down = big
    dg, du = _swiglu_bwd(dh3, w_down, s["g"], s["u"], tag + "ffn_down_bwd")
    dw_down = _matmul(s["act"], dh3, mode="tn", tm=1408, tn=512, tk=2048, out_dtype=gd, name=tag + "dw_down")
    dw_gate = _matmul(s["hn2"], dg, mode="tn", tm=512, tn=1408, tk=2048, out_dtype=gd, name=tag + "dw_gate")
    dw_up = _matmul(s["hn2"], du, mode="tn", tm=512, tn=1408, tk=2048, out_dtype=gd, name=tag + "dw_up")
    norm_ffn = sp["norm_ffn"] if after_ffn is None else sp["norm_ffn"] + after_ffn(dict(w_gate=dw_gate, w_up=dw_up, w_down=dw_down))
    dh2, dnf = _nt_norm_bwd([(dg, w_gate), (du, w_up)], s["h2"], norm_ffn, dh3, tag + "ffn_up_bwd_norm", tk=1408)
    d_o = _matmul(dh2, w_out, mode="nt", n_out=ATTN_WIDTH, tn=512, b_off=0, name=tag + "out_attn_bwd")
    dyn = _matmul(dh2, w_out, mode="nt", n_out=SSM_INNER, tn=512, b_off=1, name=tag + "out_ssm_bwd")
    dw_out = jnp.concatenate([_matmul(s["o"], dh2, mode="tn", tm=512, tn=512, tk=2048, out_dtype=gd, name=tag + "dw_out_attn"),
                              _matmul(s["yn"], dh2, mode="tn", tm=512, tn=512, tk=2048, out_dtype=gd, name=tag + "dw_out_ssm")], axis=0)
    dxbc, dz, ddtp, dnw, dds, dal, dbi = _ssd_bwd(s["xbc"], s["z"], s["dtp"], s["y"], s["hs"], dyn, sp["ssd"], tag + "ssd_bwd")
    dxbc_pre, dconv_w, dconv_b = _conv_bwd(s["xbc_pre"], sp["conv_w"], sp["conv_b"], dxbc, tag + "conv_bwd")
    dq, dk, dv = _attn_bwd(s["prep"], tabs, s["o"], s["lse"], d_o, tag + "attn_bwd")
    dproj = jnp.concatenate([dq, dk, dv, dz, dxbc_pre, ddtp], axis=1)
    dw_in = _matmul(s["hn"], dproj, mode="tn", tm=512, tn=1152, tk=2048, out_dtype=gd, name=tag + "dw_in")
    dh, dnm = _nt_norm_bwd([(dproj, w_in)], s["h"], sp["norm_mix"], dh2, tag + "proj_bwd_norm", tk=1152)
    grads = dict(
        norm_mix=dnm.sum(0), w_in=dw_in, conv_w=dconv_w, conv_b=dconv_b[0], dt_bias=dbi.sum(0)[:SSM_HEADS], a_log=dal.sum(0)[:SSM_HEADS],
        d_skip=dds.sum(0).reshape(SSM_HEADS, HEAD_DIM).sum(1), ssm_norm=dnw.sum(0), w_out=dw_out, norm_ffn=dnf.sum(0),
        w_gate=dw_gate, w_up=dw_up, w_down=dw_down)
    return dh, grads


def _local_step(x, positions, target, p, bigs):
    tabs = _rope_tables(positions.reshape(-1, 1), "rope_tables")
    h = x
    saved, sps = [], []
    for l in range(DEPTH):
        sps.append(_layer_params(p, l))
        h, s = _layer_fwd(h, bigs[l][0], bigs[l][1:], sps[l], tabs, l)
        saved.append(s)
    dh, loss_parts, dfn = _final_loss(h, _row(p["final_norm"]), target, "final_loss")
    layer_grads = [None] * DEPTH
    for l in reversed(range(DEPTH)):
        dh, layer_grads[l] = _layer_bwd(dh, saved[l], bigs[l], sps[l], tabs, l)
    grads = {k: [layer_grads[l][k] for l in range(DEPTH)] for k in layer_grads[0]}
    grads["final_norm"] = dfn.sum(0)
    return jnp.sum(loss_parts), dh, grads


BIG = ("w_in", "w_out", "w_gate", "w_up", "w_down")
REST = BIG[1:]
FFN = ("w_gate", "w_up", "w_down")
MIX = ("w_in", "w_out")
COL_SHARDED = ("w_in", "w_gate", "w_up")
SMALL = ("norm_mix", "conv_b", "dt_bias", "a_log", "d_skip", "ssm_norm", "norm_ffn", "final_norm")
WEIGHTS = ("norm_mix", "w_in", "conv_w", "conv_b", "dt_bias", "a_log", "d_skip", "ssm_norm", "w_out", "norm_ffn", "w_gate", "w_up", "w_down", "final_norm")
PACK_W = 1024
SMALL_ROWS = 88
CONVW_ROWS = 96
CONVW_SHARD_ROWS = 16


def _full_from_gathered(name, g, l):
    _, a, b = g.shape
    if name in COL_SHARDED:
        width = IN_PROJ_PAD if name == "w_in" else N_DEV * b
        return _cols_from_devices(g.reshape(N_DEV, 1, a, b), width, f"cols_l{l}_{name}").reshape(a, width)
    return g.reshape(N_DEV * a, b)


def _by_device(name, full, shard_shape, l):
    a, b = shard_shape
    if name in COL_SHARDED:
        return _devices_from_cols([full], b, f"devs_l{l}_{name}").reshape(N_CHIP, 2, a, b)
    return full.reshape(N_CHIP, 2, a, b)


def _pack_rows(parts, rows, width):
    flat = jnp.concatenate([q.reshape(-1) for q in parts])
    return jnp.pad(flat, (0, rows * width - flat.shape[0])).reshape(rows, width)


def _unpack(flat, like):
    out, off = [], 0
    for q in like:
        out.append(flat[off:off + q.size].reshape(q.shape))
        off += q.size
    return out


def kernel(x, positions, norm_mix, w_in, conv_w, conv_b, dt_bias, a_log, d_skip, ssm_norm, w_out, norm_ffn, w_gate, w_up, w_down, final_norm, loss_target, m_norm_mix, m_w_in, m_conv_w, m_conv_b, m_dt_bias, m_a_log, m_d_skip, m_ssm_norm, m_w_out, m_norm_ffn, m_w_gate, m_w_up, m_w_down, m_final_norm, v_norm_mix, v_w_in, v_conv_w, v_conv_b, v_dt_bias, v_a_log, v_d_skip, v_ssm_norm, v_w_out, v_norm_ffn, v_w_gate, v_w_up, v_w_down, v_final_norm):
    w = dict(norm_mix=norm_mix, w_in=w_in, conv_w=conv_w, conv_b=conv_b, dt_bias=dt_bias, a_log=a_log, d_skip=d_skip, ssm_norm=ssm_norm,
             w_out=w_out, norm_ffn=norm_ffn, w_gate=w_gate, w_up=w_up, w_down=w_down, final_norm=final_norm)
    m = dict(norm_mix=m_norm_mix, w_in=m_w_in, conv_w=m_conv_w, conv_b=m_conv_b, dt_bias=m_dt_bias, a_log=m_a_log, d_skip=m_d_skip,
             ssm_norm=m_ssm_norm, w_out=m_w_out, norm_ffn=m_norm_ffn, w_gate=m_w_gate, w_up=m_w_up, w_down=m_w_down, final_norm=m_final_norm)
    v = dict(norm_mix=v_norm_mix, w_in=v_w_in, conv_w=v_conv_w, conv_b=v_conv_b, dt_bias=v_dt_bias, a_log=v_a_log, d_skip=v_d_skip,
             ssm_norm=v_ssm_norm, w_out=v_w_out, norm_ffn=v_norm_ffn, w_gate=v_w_gate, w_up=v_w_up, w_down=v_w_down, final_norm=v_final_norm)
    ax, ay, ac = lax.axis_index("x"), lax.axis_index("y"), lax.axis_index("c")
    dev = 4 * ax + 2 * ay + ac

    assert DEPTH == 2
    t = x.shape[0] * x.shape[1]
    xf, target = x.reshape(t, D_MODEL), loss_target.reshape(t, D_MODEL)

    def own_slot(block):
        return jnp.broadcast_to(block, (N_DEV,) + block.shape[1:])

    def gather_start(keys, l, tie, name):
        shards = [(w[keys[0]][l] + tie).astype(bf16)] + [w[k][l].astype(bf16) for k in keys[1:]]
        return _direct_start(shards, [own_slot(s[None]) for s in shards], False, name)

    def scatter_start(keys, grads_l, l, name):
        by_dev = [_by_device(k, grads_l[k], w[k].shape[1:], l).reshape((N_DEV,) + w[k].shape[1:]) for k in keys]
        return _direct_start(by_dev, [own_slot(lax.dynamic_slice_in_dim(g, dev, 1, 0)) for g in by_dev], True, name)

    (g_in0, conv_all), tie = _allgather_two_level([w["w_in"][0].astype(bf16), w["conv_w"]], "gather_l0_w_in")
    rest0_copy = gather_start(REST, 0, tie[0, 0], "gather_l0_rest_start")
    l1_copy = gather_start(BIG, 1, rest0_copy[4][0, 0], "gather_l1_start")
    p = {k: w[k] for k in SMALL}
    p["norm_mix"] = p["norm_mix"] + l1_copy[4][0, 0]
    p["conv_w"] = jnp.transpose(conv_all, (1, 2, 0, 3)).reshape(DEPTH, CONV_WIDTH, CONV_CH)
    sp0, sp1 = _layer_params(p, 0), _layer_params(p, 1)

    def rest0(after):
        lands = _direct_wait(*rest0_copy[:4], after, False, "gather_l0_rest_wait")
        return tuple(_full_from_gathered(k, g, 0) for k, g in zip(REST, lands))

    tabs = _rope_tables(positions.reshape(t, 1), "rope_tables")
    w_in0 = _full_from_gathered("w_in", g_in0, 0)
    h1, saved0 = _layer_fwd(xf, w_in0, rest0, sp0, tabs, 0)
    lands1 = _direct_wait(*l1_copy[:4], h1, False, "gather_l1_wait")
    bigs1 = tuple(_full_from_gathered(k, g, 1) for k, g in zip(BIG, lands1))
    h2, saved1 = _layer_fwd(h1, bigs1[0], bigs1[1:], sp1, tabs, 1)
    dh, loss_parts, dfn = _final_loss(h2, _row(p["final_norm"]), target, "final_loss")
    loss_local = jnp.sum(loss_parts)

    dh, grads1 = _layer_bwd(dh, saved1, bigs1, sp1, tabs, 1, gd=bf16)
    l1_grads = scatter_start(BIG, grads1, 1, "scatter_l1_start")
    w_out0, w_gate0, w_up0, w_down0 = saved0["rest"]
    bigs0 = (w_in0, w_out0, w_gate0, w_up0, w_down0 + l1_grads[4][0, 0].astype(bf16))
    ffn0_grads = []

    def after_ffn(grads_ffn):
        ffn0_grads.append(scatter_start(FFN, grads_ffn, 0, "scatter_l0_ffn_start"))
        return ffn0_grads[0][4][0, 0]

    dx, grads0 = _layer_bwd(dh, saved0, bigs0, sp0, tabs, 0, gd=bf16, after_ffn=after_ffn)
    mix0_grads = scatter_start(MIX, grads0, 0, "scatter_l0_mix_start")
    landed = {(k, 1): g for k, g in zip(BIG, _direct_wait(*l1_grads[:4], dx, True, "scatter_l1_wait"))}
    landed.update({(k, 0): g for k, g in zip(FFN, _direct_wait(*ffn0_grads[0][:4], dx, True, "scatter_l0_ffn_wait"))})
    landed.update({(k, 0): g for k, g in zip(MIX, _direct_wait(*mix0_grads[:4], mix0_grads[4], True, "scatter_l0_mix_wait"))})
    out_g, out_d, out_m, out_v = {}, {}, {}, {}
    for k in BIG:
        res = _adamw_layers([landed[k, l] for l in range(DEPTH)], w[k], m[k], v[k], "adamw_" + k)
        for dst, r in zip((out_g, out_d, out_m, out_v), res):
            dst[k] = r
    grads = {k: [grads0[k], grads1[k]] for k in grads0 if k not in BIG}
    grads["final_norm"] = dfn.sum(0)

    small_like = [w[k] for k in SMALL]
    small_grads = [jnp.stack(grads[k]) if k != "final_norm" else grads[k] for k in SMALL]
    small_pack = jnp.concatenate([_pack_rows(small_grads, SMALL_ROWS, LANES), _pack_rows([jnp.stack(grads["conv_w"])], CONVW_ROWS, LANES)], axis=0)
    parts = _allgather_direct(small_pack, "gather_small_grads")
    g_s, d_s, m_s, v_s = _adamw(
        [parts[i, :SMALL_ROWS] for i in range(N_DEV)], _pack_rows(small_like, SMALL_ROWS, LANES),
        _pack_rows([m[k] for k in SMALL], SMALL_ROWS, LANES), _pack_rows([v[k] for k in SMALL], SMALL_ROWS, LANES), "adamw_replicated")
    for dst, src in ((out_g, g_s), (out_d, d_s), (out_m, m_s), (out_v, v_s)):
        dst.update(zip(SMALL, _unpack(src.reshape(-1), small_like)))
    shard_w = conv_w.shape[-1]
    conv_parts = parts[:, SMALL_ROWS:].reshape(N_DEV, DEPTH, CONV_WIDTH, CONV_CH)
    conv_mine = lax.dynamic_slice_in_dim(conv_parts, dev * shard_w, shard_w, axis=3)
    g_c, d_c, m_c, v_c = _adamw(
        [_pack_rows([conv_mine[i]], CONVW_SHARD_ROWS, LANES) for i in range(N_DEV)], _pack_rows([conv_w], CONVW_SHARD_ROWS, LANES),
        _pack_rows([m["conv_w"]], CONVW_SHARD_ROWS, LANES), _pack_rows([v["conv_w"]], CONVW_SHARD_ROWS, LANES), "adamw_conv_w")
    for dst, src in ((out_g, g_c), (out_d, d_c), (out_m, m_c), (out_v, v_c)):
        dst["conv_w"] = src.reshape(-1)[:conv_w.size].reshape(conv_w.shape)

    loss = lax.psum(loss_local, ("x", "y", "c"))
    return (loss, dx.reshape(x.shape), *[out_g[k] for k in WEIGHTS], *[out_d[k] for k in WEIGHTS],
            *[out_m[k] for k in WEIGHTS], *[out_v[k] for k in WEIGHTS])
```

```python
import functools
import math

import jax
import jax.numpy as jnp
import numpy as np
from jax import lax
from jax.experimental import pallas as pl
from jax.experimental.pallas import tpu as pltpu

f32 = jnp.float32
bf16 = jnp.bfloat16

D_MODEL = 1024
SEQ = 2048
DEPTH = 2
HEAD_DIM = 64
N_ATTN_HEADS = 8
N_KV_HEADS = 2
ATTN_WIDTH = 512
KV_WIDTH = 128
ROPE_DIM = 16
ROPE_THETA = 500000.0
DILATIONS = (1, 4, 16)
ATTN_BLOCK = 128
SSM_HEADS = 16
SSM_INNER = 1024
SSM_GROUPS = 2
D_STATE = 128
CONV_WIDTH = 4
CHUNK = 128
CONV_CH = 1536
MIX_WIDTH = 1536
QKV_WIDTH = ATTN_WIDTH + 2 * KV_WIDTH
Z_OFF = 768
XBC_OFF = 1792
DT_OFF = 3328
IN_PROJ = 3344
IN_PROJ_PAD = 3456
FFN_HIDDEN = 2816
EPS = 1e-5
N_DEV = 8
ADAM_LR = 0.001
ADAM_B1 = 0.9
ADAM_B2 = 0.999
ADAM_EPS = 1e-08
ADAM_WD = 0.01
ADAM_STEP = 10

LANES = 128
SUBLANES = 8
VMEM_LIMIT = 56 * 1024 * 1024

MESH = pl.DeviceIdType.MESH
ANY = pl.BlockSpec(memory_space=pl.ANY)


def _cparams(sem, vmem=None):
    return pltpu.CompilerParams(dimension_semantics=sem, vmem_limit_bytes=vmem or VMEM_LIMIT)


def _sigmoid(x):
    return 1.0 / (1.0 + jnp.exp(-x))


def _silu(x):
    return x * _sigmoid(x)


def _dsilu(x):
    s = _sigmoid(x)
    return s * (1.0 + x * (1.0 - s))


def _silu_and_grad(x):
    s = _sigmoid(x)
    return x * s, s * (1.0 + x * (1.0 - s))


def _softplus(x):
    return jnp.maximum(x, 0.0) + jnp.log(1.0 + jnp.exp(-jnp.abs(x)))


def _dot(a, b, dims, precision=None):
    return lax.dot_general(a, b, (dims, ((), ())), preferred_element_type=f32, precision=precision)


def _nn(a, b, precision=None):
    return _dot(a, b, ((1,), (0,)), precision)


def _nt(a, b):
    return _dot(a, b, ((1,), (1,)))


def _tn(a, b):
    return _dot(a, b, ((0,), (0,)))


def _rowsum8(t):
    n, w = t.shape
    return jnp.sum(t.reshape(n // SUBLANES, SUBLANES, w), axis=0)


def _matmul(a, b, *, mode, n_out=None, b_off=0, a_koff=0, b_koff=0, k_len=None, add=None, out_dtype=f32, tm=2048, tn=512, tk=1024, name):
    if mode == "tn":
        kdim_a, m = a.shape
    else:
        m, kdim_a = a.shape
    kk = k_len if k_len is not None else kdim_a
    n = n_out if n_out is not None else (b.shape[0] if mode == "nt" else b.shape[1])
    tm, tn, tk = min(tm, m), min(tn, n), min(tk, kk)
    assert m % tm == 0 and n % tn == 0 and kk % tk == 0, (name, m, n, kk, tm, tn, tk)
    nk = kk // tk
    if mode == "nn":
        a_spec = pl.BlockSpec((tm, tk), lambda i, j, k: (i, k + a_koff))
        b_spec = pl.BlockSpec((tk, tn), lambda i, j, k: (k + b_koff, j + b_off))
        dims = ((1,), (0,))
    elif mode == "nt":
        a_spec = pl.BlockSpec((tm, tk), lambda i, j, k: (i, k + a_koff))
        b_spec = pl.BlockSpec((tn, tk), lambda i, j, k: (j + b_off, k + b_koff))
        dims = ((1,), (1,))
    else:
        a_spec = pl.BlockSpec((tk, tm), lambda i, j, k: (k + a_koff, i))
        b_spec = pl.BlockSpec((tk, tn), lambda i, j, k: (k + b_koff, j + b_off))
        dims = ((0,), (0,))
    o_spec = pl.BlockSpec((tm, tn), lambda i, j, k: (i, j))
    has_add = add is not None

    def body(*refs):
        if has_add:
            a_ref, b_ref, add_ref, o_ref, acc_ref = refs
        else:
            a_ref, b_ref, o_ref, acc_ref = refs
        k = pl.program_id(2)
        part = _dot(a_ref[...].astype(bf16), b_ref[...].astype(bf16), dims)

        @pl.when(k == 0)
        def _():
            acc_ref[...] = part

        @pl.when(k > 0)
        def _():
            acc_ref[...] += part

        @pl.when(k == nk - 1)
        def _():
            r = acc_ref[...]
            if has_add:
                r = r + add_ref[...]
            o_ref[...] = r.astype(out_dtype)

    in_specs = [a_spec, b_spec] + ([o_spec] if has_add else [])
    args = (a, b) + ((add,) if has_add else ())
    return pl.pallas_call(
        body, name=name, grid=(m // tm, n // tn, nk), in_specs=in_specs, out_specs=o_spec,
        out_shape=jax.ShapeDtypeStruct((m, n), out_dtype), scratch_shapes=[pltpu.VMEM((tm, tn), f32)],
        compiler_params=_cparams(("parallel", "parallel", "arbitrary")),
    )(*args)


def _out_proj(o, yn, w_out, h, name, tm=2048, tn=512):
    m, kb = o.shape
    n = w_out.shape[1]
    n_y = yn.shape[1] // kb
    assert yn.shape[1] % kb == 0 and w_out.shape[0] == kb * (1 + n_y)

    def body(*refs):
        o_ref, y_refs, w_refs, h_ref, out_ref = refs[0], refs[1:1 + n_y], refs[1 + n_y:2 + 2 * n_y], refs[-2], refs[-1]
        acc = h_ref[...] + _nn(o_ref[...].astype(bf16), w_refs[0][...])
        for y_ref, w_ref in zip(y_refs, w_refs[1:]):
            acc = acc + _nn(y_ref[...], w_ref[...])
        out_ref[...] = acc

    res = pl.BlockSpec((tm, tn), lambda i, j: (i, j))

    def a_blk(c):
        return pl.BlockSpec((tm, kb), lambda i, j: (i, c))

    def w_blk(r):
        return pl.BlockSpec((kb, tn), lambda i, j: (r, j))

    return pl.pallas_call(
        body, name=name, grid=(m // tm, n // tn),
        in_specs=[a_blk(0)] + [a_blk(c) for c in range(n_y)] + [w_blk(r) for r in range(1 + n_y)] + [res],
        out_specs=res, out_shape=jax.ShapeDtypeStruct((m, n), f32), compiler_params=_cparams(("parallel", "parallel")),
    )(o, *[yn] * n_y, *[w_out] * (1 + n_y), h)


def _swiglu_fwd(hn, w_gate, w_up, name, tm=2048, tn=256):
    m, k = hn.shape
    n = w_gate.shape[1]

    def body(a_ref, wg_ref, wu_ref, g_ref, u_ref, act_ref):
        a = a_ref[...]
        g = _nn(a, wg_ref[...])
        u = _nn(a, wu_ref[...])
        sg, dsg = _silu_and_grad(g)
        g_ref[...] = (u * dsg).astype(bf16)
        u_ref[...] = sg.astype(bf16)
        act_ref[...] = (sg * u).astype(bf16)

    a_spec = pl.BlockSpec((tm, k), lambda i, j: (i, 0))
    w_spec = pl.BlockSpec((k, tn), lambda i, j: (0, j))
    o_spec = pl.BlockSpec((tm, tn), lambda i, j: (i, j))
    return pl.pallas_call(
        body, name=name, grid=(m // tm, n // tn), in_specs=[a_spec, w_spec, w_spec], out_specs=[o_spec, o_spec, o_spec],
        out_shape=[jax.ShapeDtypeStruct((m, n), bf16)] * 3,
        compiler_params=_cparams(("parallel", "parallel")),
    )(hn, w_gate, w_up)


def _swiglu_bwd(dh, w_down, g, u, name, tm=2048, tn=256):
    m, k = dh.shape
    n = w_down.shape[0]

    def body(a_ref, w_ref, g_ref, u_ref, dg_ref, du_ref):
        dact = _nt(a_ref[...].astype(bf16), w_ref[...])
        dg_ref[...] = (dact * g_ref[...].astype(f32)).astype(bf16)
        du_ref[...] = (dact * u_ref[...].astype(f32)).astype(bf16)

    a_spec = pl.BlockSpec((tm, k), lambda i, j: (i, 0))
    w_spec = pl.BlockSpec((tn, k), lambda i, j: (j, 0))
    o_spec = pl.BlockSpec((tm, tn), lambda i, j: (i, j))
    return pl.pallas_call(
        body, name=name, grid=(m // tm, n // tn), in_specs=[a_spec, w_spec, o_spec, o_spec], out_specs=[o_spec, o_spec],
        out_shape=[jax.ShapeDtypeStruct((m, n), bf16), jax.ShapeDtypeStruct((m, n), bf16)],
        compiler_params=_cparams(("parallel", "parallel")),
    )(dh, w_down, g, u)


def _rmsnorm_fwd(h, w, name, tm=512):
    m, d = h.shape

    def body(h_ref, w_ref, o_ref):
        x = h_ref[...]
        r = lax.rsqrt(jnp.mean(x * x, axis=-1, keepdims=True) + EPS)
        o_ref[...] = (x * r * w_ref[...]).astype(bf16)

    return pl.pallas_call(
        body, name=name, grid=(m // tm,),
        in_specs=[pl.BlockSpec((tm, d), lambda i: (i, 0)), pl.BlockSpec((1, d), lambda i: (0, 0))],
        out_specs=pl.BlockSpec((tm, d), lambda i: (i, 0)), out_shape=jax.ShapeDtypeStruct((m, d), bf16),
        compiler_params=_cparams(("parallel",)),
    )(h, w)


def _rmsnorm_bwd(dhn, h, w, dres, name, tm=512):
    m, d = h.shape

    def body(dhn_ref, h_ref, w_ref, dres_ref, dh_ref, dw_ref):
        x = h_ref[...]
        r = lax.rsqrt(jnp.mean(x * x, axis=-1, keepdims=True) + EPS)
        xhat = x * r
        dy = dhn_ref[...]
        gw = dy * w_ref[...]
        dh_ref[...] = dres_ref[...] + r * (gw - xhat * jnp.mean(gw * xhat, axis=-1, keepdims=True))
        part = _rowsum8(dy * xhat)

        @pl.when(pl.program_id(0) == 0)
        def _():
            dw_ref[...] = part

        @pl.when(pl.program_id(0) > 0)
        def _():
            dw_ref[...] += part

    row = pl.BlockSpec((tm, d), lambda i: (i, 0))
    return pl.pallas_call(
        body, name=name, grid=(m // tm,),
        in_specs=[row, row, pl.BlockSpec((1, d), lambda i: (0, 0)), row],
        out_specs=[row, pl.BlockSpec((SUBLANES, d), lambda i: (0, 0))],
        out_shape=[jax.ShapeDtypeStruct((m, d), f32), jax.ShapeDtypeStruct((SUBLANES, d), f32)],
        compiler_params=_cparams(("arbitrary",)),
    )(dhn, h, w, dres)


def _nt_norm_bwd(pairs, h, w, dres, name, tm=1024, tk=704):
    m, d = h.shape
    steps = [p[0].shape[1] // tk for p in pairs]
    assert all(p[0].shape[1] % tk == 0 for p in pairs), (name, tk)
    starts = [sum(steps[:i]) for i in range(len(pairs))]
    nk = sum(steps)
    n_p = len(pairs)

    def body(*refs):
        ab = refs[:2 * n_p]
        h_ref, w_ref, dres_ref, dh_ref, dw_ref, acc_ref = refs[2 * n_p:]
        i, k = pl.program_id(0), pl.program_id(1)

        @pl.when(k == 0)
        def _():
            acc_ref[...] = jnp.zeros_like(acc_ref)

        for p in range(n_p):
            @pl.when((k >= starts[p]) & (k < starts[p] + steps[p]))
            def _(p=p):
                acc_ref[...] += _nt(ab[2 * p][...], ab[2 * p + 1][...])

        @pl.when(k == nk - 1)
        def _():
            x = h_ref[...]
            r = lax.rsqrt(jnp.mean(x * x, axis=-1, keepdims=True) + EPS)
            xhat = x * r
            dy = acc_ref[...]
            gw = dy * w_ref[...]
            dh_ref[...] = dres_ref[...] + r * (gw - xhat * jnp.mean(gw * xhat, axis=-1, keepdims=True))
            part = _rowsum8(dy * xhat)

            @pl.when(i == 0)
            def _():
                dw_ref[...] = part

            @pl.when(i > 0)
            def _():
                dw_ref[...] += part

    def clamp(k, p):
        return jnp.clip(k - starts[p], 0, steps[p] - 1)

    in_specs = []
    for p in range(n_p):
        in_specs += [pl.BlockSpec((tm, tk), lambda i, k, p=p: (i, clamp(k, p))), pl.BlockSpec((d, tk), lambda i, k, p=p: (0, clamp(k, p)))]
    row = pl.BlockSpec((tm, d), lambda i, k: (i, 0))
    in_specs += [row, pl.BlockSpec((1, d), lambda i, k: (0, 0)), row]
    return pl.pallas_call(
        body, name=name, grid=(m // tm, nk), in_specs=in_specs,
        out_specs=[row, pl.BlockSpec((SUBLANES, d), lambda i, k: (0, 0))],
        out_shape=[jax.ShapeDtypeStruct((m, d), f32), jax.ShapeDtypeStruct((SUBLANES, d), f32)],
        scratch_shapes=[pltpu.VMEM((tm, d), f32)], compiler_params=_cparams(("arbitrary", "arbitrary")),
    )(*[t for p in pairs for t in p], h, w, dres)


def _final_loss(h, w, target, name, tm=512):
    m, d = h.shape

    def body(h_ref, w_ref, t_ref, dh_ref, loss_ref, dw_ref):
        x = h_ref[...]
        r = lax.rsqrt(jnp.mean(x * x, axis=-1, keepdims=True) + EPS)
        xhat = x * r
        ww = w_ref[...]
        err = xhat * ww - t_ref[...]
        dy = err * (1.0 / d)
        gw = dy * ww
        dh_ref[...] = r * (gw - xhat * jnp.mean(gw * xhat, axis=-1, keepdims=True))
        lpart = _rowsum8(err * err) * (0.5 / d)
        wpart = _rowsum8(dy * xhat)

        @pl.when(pl.program_id(0) == 0)
        def _():
            loss_ref[...] = lpart
            dw_ref[...] = wpart

        @pl.when(pl.program_id(0) > 0)
        def _():
            loss_ref[...] += lpart
            dw_ref[...] += wpart

    row = pl.BlockSpec((tm, d), lambda i: (i, 0))
    acc = pl.BlockSpec((SUBLANES, d), lambda i: (0, 0))
    return pl.pallas_call(
        body, name=name, grid=(m // tm,),
        in_specs=[row, pl.BlockSpec((1, d), lambda i: (0, 0)), row], out_specs=[row, acc, acc],
        out_shape=[jax.ShapeDtypeStruct((m, d), f32), jax.ShapeDtypeStruct((SUBLANES, d), f32), jax.ShapeDtypeStruct((SUBLANES, d), f32)],
        compiler_params=_cparams(("arbitrary",)),
    )(h, w, target)


def _lane_tables():
    f = np.arange(LANES) % HEAD_DIM
    inv = ROPE_THETA ** (-jnp.arange(0, ROPE_DIM, 2, dtype=f32) / ROPE_DIM)
    invf = jnp.where(f < ROPE_DIM, inv[f % (ROPE_DIM // 2)], 0.0).astype(f32)
    return invf.reshape(1, LANES)


def _rope_tables(pos_col, name):
    t = pos_col.shape[0]
    tm = SEQ

    def body(p_ref, f_ref, c_ref, s1_ref, s2_ref):
        ang = p_ref[...].astype(f32) * f_ref[...]
        co, si = jnp.cos(ang), jnp.sin(ang)
        f = lax.broadcasted_iota(jnp.int32, (tm, LANES), 1) % HEAD_DIM
        c_ref[...] = jnp.where(f < ROPE_DIM, co, 1.0)
        s1_ref[...] = jnp.where(f < ROPE_DIM // 2, -si, 0.0)
        s2_ref[...] = jnp.where((f >= ROPE_DIM // 2) & (f < ROPE_DIM), si, 0.0)

    row = pl.BlockSpec((tm, LANES), lambda i: (i, 0))
    return pl.pallas_call(
        body, name=name, grid=(t // tm,),
        in_specs=[pl.BlockSpec((tm, 1), lambda i: (i, 0)), pl.BlockSpec((1, LANES), lambda i: (0, 0))],
        out_specs=[row, row, row], out_shape=[jax.ShapeDtypeStruct((t, LANES), f32)] * 3,
        compiler_params=_cparams(("parallel",)),
    )(pos_col, _lane_tables())


def _rot(x, c, s1, s2):
    return x * c + pltpu.roll(x, LANES - ROPE_DIM // 2, 1) * s1 + pltpu.roll(x, ROPE_DIM // 2, 1) * s2


def _rot_t(g, c, s1, s2):
    return g * c + pltpu.roll(g * s1, ROPE_DIM // 2, 1) + pltpu.roll(g * s2, LANES - ROPE_DIM // 2, 1)


def _dup_head(x, kvh, low):
    a = jnp.where(kvh == 0, x, pltpu.roll(x, HEAD_DIM, 1))
    return jnp.where(low, a, pltpu.roll(a, HEAD_DIM, 1))


def _deinterleave(src_ref, dst_ref, d, dtype):
    length = SEQ // d
    if d == 1:
        dst_ref[...] = src_ref[...].astype(dtype)
    else:
        for r in range(d):
            dst_ref[pl.ds(r * length, length), :] = src_ref[pl.ds(r, length, stride=d), :].astype(dtype)


def _interleave_store(src_ref, dst_ref, d, accumulate):
    length = SEQ // d
    if d == 1:
        if accumulate:
            dst_ref[...] += src_ref[...]
        else:
            dst_ref[...] = src_ref[...]
    else:
        for r in range(d):
            blk = src_ref[pl.ds(r * length, length), :]
            if accumulate:
                dst_ref[pl.ds(r, length, stride=d), :] = dst_ref[pl.ds(r, length, stride=d), :] + blk
            else:
                dst_ref[pl.ds(r, length, stride=d), :] = blk


def _attn_masks():
    qi = lax.broadcasted_iota(jnp.int32, (ATTN_BLOCK, ATTN_BLOCK), 0)
    ki = lax.broadcasted_iota(jnp.int32, (ATTN_BLOCK, ATTN_BLOCK), 1)
    low = lax.broadcasted_iota(jnp.int32, (ATTN_BLOCK, LANES), 1) < HEAD_DIM
    return ki <= qi, ki >= qi, low


NEG_INF = float("-inf")
ATTN_UNROLL = 4


N_BRANCH = len(DILATIONS)


def _attn_prep(qkv, tabs, name):
    t = qkv.shape[0]
    nb = t // SEQ
    n_j = ATTN_WIDTH // LANES

    def q_body(q_ref, c_ref, s1_ref, s2_ref, out_ref, xr):
        xr[...] = _rot(q_ref[...], c_ref[...], s1_ref[...], s2_ref[...]) * (HEAD_DIM ** -0.5)
        for bi, d in enumerate(DILATIONS):
            _deinterleave(xr, out_ref.at[bi], d, bf16)

    def kv_body(x_ref, c_ref, s1_ref, s2_ref, out_ref, xr):
        lowfull = lax.broadcasted_iota(jnp.int32, (SEQ, LANES), 1) < HEAD_DIM
        x = x_ref[...]
        x = jnp.where(pl.program_id(1) == 0, _rot(x, c_ref[...], s1_ref[...], s2_ref[...]), x)
        for kvh in range(N_KV_HEADS):
            xr[...] = _dup_head(x, kvh, lowfull)
            for bi, d in enumerate(DILATIONS):
                length = SEQ // d
                for r in range(d):
                    rows = xr[...] if d == 1 else xr[pl.ds(r, length, stride=d), :]
                    out_ref[0, bi, pl.ds(r * length, length), kvh * LANES:(kvh + 1) * LANES] = rows.astype(bf16)

    tab = pl.BlockSpec((SEQ, LANES), lambda b, j: (b, 0))
    q = pl.pallas_call(
        q_body, name=name + "_q", grid=(nb, n_j),
        in_specs=[pl.BlockSpec((SEQ, LANES), lambda b, j: (b, j)), tab, tab, tab],
        out_specs=pl.BlockSpec((N_BRANCH, SEQ, LANES), lambda b, j: (0, b, j)),
        out_shape=jax.ShapeDtypeStruct((N_BRANCH, t, ATTN_WIDTH), bf16), scratch_shapes=[pltpu.VMEM((SEQ, LANES), f32)],
        compiler_params=_cparams(("parallel", "parallel")),
    )(qkv, *tabs)
    kv = pl.pallas_call(
        kv_body, name=name + "_kv", grid=(nb, 2),
        in_specs=[pl.BlockSpec((SEQ, LANES), lambda b, j: (b, n_j + j)), tab, tab, tab],
        out_specs=pl.BlockSpec((1, N_BRANCH, SEQ, N_KV_HEADS * LANES), lambda b, j: (j, 0, b, 0)),
        out_shape=jax.ShapeDtypeStruct((2, N_BRANCH, t, N_KV_HEADS * LANES), bf16), scratch_shapes=[pltpu.VMEM((SEQ, LANES), f32)],
        compiler_params=_cparams(("parallel", "parallel")),
    )(qkv, *tabs)
    return q, kv


def _attn_fwd(prep, name):
    q_all, kv_all = prep
    t = q_all.shape[1]
    nb = t // SEQ
    n_blk = SEQ // ATTN_BLOCK

    def body(q_ref, k_ref, v_ref, o_ref, lse_ref, ob, lb, o0, o1, o2, l0, l1, l2, ss):
        cur_ok, prev_ok, low = _attn_masks()
        onat, lnat = (o0, o1, o2), (l0, l1, l2)
        for bi, d in enumerate(DILATIONS):
            qd, kd, vd = q_ref.at[bi], k_ref.at[0, bi], v_ref.at[0, bi]
            per_res = n_blk // d
            use_prev = per_res > 1

            def scores(n, carry):
                start = pl.multiple_of(n * ATTN_BLOCK, ATTN_BLOCK)
                has_prev = (n % per_res) != 0
                pstart = pl.multiple_of(jnp.maximum(n - 1, 0) * ATTN_BLOCK, ATTN_BLOCK)
                qb = qd[pl.ds(start, ATTN_BLOCK), :]
                kc = kd[pl.ds(start, ATTN_BLOCK), :]
                if use_prev:
                    kp = kd[pl.ds(pstart, ATTN_BLOCK), :]
                for a in range(2):
                    qa = jnp.where(low if a == 0 else ~low, qb, jnp.zeros_like(qb))
                    ss[2 * n + a, :, 0:ATTN_BLOCK] = jnp.where(cur_ok, _nt(qa, kc), NEG_INF)
                    if use_prev:
                        ss[2 * n + a, :, ATTN_BLOCK:2 * ATTN_BLOCK] = jnp.where(prev_ok & has_prev, _nt(qa, kp), NEG_INF)
                return carry

            def softmax_pv(n, carry):
                start = pl.multiple_of(n * ATTN_BLOCK, ATTN_BLOCK)
                pstart = pl.multiple_of(jnp.maximum(n - 1, 0) * ATTN_BLOCK, ATTN_BLOCK)
                vc = vd[pl.ds(start, ATTN_BLOCK), :]
                if use_prev:
                    vp = vd[pl.ds(pstart, ATTN_BLOCK), :]
                outs, lses = [], []
                for a in range(2):
                    sc = ss[2 * n + a, :, 0:ATTN_BLOCK]
                    if use_prev:
                        sp = ss[2 * n + a, :, ATTN_BLOCK:2 * ATTN_BLOCK]
                        m = jnp.max(jnp.maximum(sc, sp), axis=1, keepdims=True)
                        pc, pp = jnp.exp(sc - m), jnp.exp(sp - m)
                        den = jnp.sum(pc + pp, axis=1, keepdims=True)
                        acc = _nn(pc.astype(bf16), vc) + _nn(pp.astype(bf16), vp)
                    else:
                        m = jnp.max(sc, axis=1, keepdims=True)
                        pc = jnp.exp(sc - m)
                        den = jnp.sum(pc, axis=1, keepdims=True)
                        acc = _nn(pc.astype(bf16), vc)
                    outs.append(acc * (1.0 / den))
                    lses.append(m + jnp.log(den))
                ob[pl.ds(start, ATTN_BLOCK), :] = jnp.where(low, outs[0], outs[1])
                lb[pl.ds(start, ATTN_BLOCK), :] = jnp.where(low, lses[0], lses[1])
                return carry

            lax.fori_loop(0, n_blk, scores, 0, unroll=ATTN_UNROLL)
            lax.fori_loop(0, n_blk, softmax_pv, 0, unroll=ATTN_UNROLL)
            _interleave_store(ob, onat[bi], d, False)
            _interleave_store(lb, lnat[bi], d, False)
        la, lbb, lc = l0[...], l1[...], l2[...]
        lm = jnp.maximum(jnp.maximum(la, lbb), lc)
        wa, wb, wc = jnp.exp(la - lm), jnp.exp(lbb - lm), jnp.exp(lc - lm)
        ws = wa + wb + wc
        o_ref[...] = (wa * o0[...] + wb * o1[...] + wc * o2[...]) / ws
        lse_ref[...] = lm + jnp.log(ws)

    def col(jj):
        return pl.BlockSpec((SEQ, LANES), lambda b, j: (b, jj if jj is not None else j))

    fs = pltpu.VMEM((SEQ, LANES), f32)
    return pl.pallas_call(
        body, name=name, grid=(nb, ATTN_WIDTH // LANES),
        in_specs=[pl.BlockSpec((N_BRANCH, SEQ, LANES), lambda b, j: (0, b, j)),
                  pl.BlockSpec((1, N_BRANCH, SEQ, LANES), lambda b, j: (0, 0, b, j // 2)),
                  pl.BlockSpec((1, N_BRANCH, SEQ, LANES), lambda b, j: (1, 0, b, j // 2))],
        out_specs=[col(None), col(None)],
        out_shape=[jax.ShapeDtypeStruct((t, ATTN_WIDTH), f32), jax.ShapeDtypeStruct((t, ATTN_WIDTH), f32)],
        scratch_shapes=[fs, fs, fs, fs, fs, fs, fs, fs, pltpu.VMEM((2 * n_blk, ATTN_BLOCK, 2 * ATTN_BLOCK), f32)],
        compiler_params=_cparams(("parallel", "parallel")),
    )(q_all, kv_all, kv_all)


def _attn_bwd(prep, tabs, o, lse, do, name):
    q_all, kv_all = prep
    t = q_all.shape[1]
    nb = t // SEQ
    n_blk = SEQ // ATTN_BLOCK
    n_j = ATTN_WIDTH // LANES

    def body(q_ref, k_ref, v_ref, c_ref, s1_ref, s2_ref, o_ref, lse_ref, do_ref, dq_ref, dk_ref, dv_ref,
             dl, dod, lsd, dld, dqd, dkd, dvd, dqa, dka, dva, pb, dsb, dk_acc, dv_acc):
        j = pl.program_id(1)
        pb[2 * n_blk:2 * n_blk + 2] = jnp.zeros((2, ATTN_BLOCK, 2 * ATTN_BLOCK), bf16)
        dsb[2 * n_blk:2 * n_blk + 2] = jnp.zeros((2, ATTN_BLOCK, 2 * ATTN_BLOCK), bf16)
        kvh = j // 2
        cur_ok, prev_ok, low = _attn_masks()
        lowfull = lax.broadcasted_iota(jnp.int32, (SEQ, LANES), 1) < HEAD_DIM
        c, s1, s2 = c_ref[...], s1_ref[...], s2_ref[...]
        prod = do_ref[...] * o_ref[...]
        d_lo = jnp.sum(jnp.where(lowfull, prod, 0.0), axis=1, keepdims=True)
        d_hi = jnp.sum(jnp.where(lowfull, 0.0, prod), axis=1, keepdims=True)
        dl[...] = jnp.where(lowfull, d_lo, d_hi)
        dqa[...] = jnp.zeros_like(dqa)
        dka[...] = jnp.zeros_like(dka)
        dva[...] = jnp.zeros_like(dva)
        for bi, d in enumerate(DILATIONS):
            qd, kd, vd = q_ref.at[bi], k_ref.at[0, bi], v_ref.at[0, bi]
            _deinterleave(do_ref, dod, d, bf16)
            _deinterleave(lse_ref, lsd, d, f32)
            _deinterleave(dl, dld, d, f32)
            per_res = n_blk // d
            use_prev = per_res > 1
            curl, prevl = slice(0, ATTN_BLOCK), slice(ATTN_BLOCK, 2 * ATTN_BLOCK)

            def halves(x):
                zero = jnp.zeros_like(x)
                return jnp.where(low, x, zero), jnp.where(low, zero, x)

            def probs(n, carry):
                start = pl.multiple_of(n * ATTN_BLOCK, ATTN_BLOCK)
                has_prev = (n % per_res) != 0
                pstart = pl.multiple_of(jnp.maximum(n - 1, 0) * ATTN_BLOCK, ATTN_BLOCK)
                cur, prev = pl.ds(start, ATTN_BLOCK), pl.ds(pstart, ATTN_BLOCK)
                qas, doas = halves(qd[cur, :]), halves(dod[cur, :])
                kc, vc = kd[cur, :], vd[cur, :]
                if use_prev:
                    kp, vp = kd[prev, :], vd[prev, :]
                lsb, dlb = lsd[cur, :], dld[cur, :]
                for a in range(2):
                    ls = lsb[:, a * HEAD_DIM:a * HEAD_DIM + 1]
                    de = dlb[:, a * HEAD_DIM:a * HEAD_DIM + 1]
                    pc = jnp.exp(jnp.where(cur_ok, _nt(qas[a], kc), NEG_INF) - ls)
                    pb[2 * n + a, :, curl] = pc.astype(bf16)
                    dsb[2 * n + a, :, curl] = (pc * (_nt(doas[a], vc) - de)).astype(bf16)
                    if use_prev:
                        pp = jnp.exp(jnp.where(prev_ok & has_prev, _nt(qas[a], kp), NEG_INF) - ls)
                        pb[2 * n + a, :, prevl] = pp.astype(bf16)
                        dsb[2 * n + a, :, prevl] = (pp * (_nt(doas[a], vp) - de)).astype(bf16)
                return carry

            def grads(n, carry):
                start = pl.multiple_of(n * ATTN_BLOCK, ATTN_BLOCK)
                pstart = pl.multiple_of(jnp.maximum(n - 1, 0) * ATTN_BLOCK, ATTN_BLOCK)
                nstart = pl.multiple_of(jnp.minimum(n + 1, n_blk - 1) * ATTN_BLOCK, ATTN_BLOCK)
                cur, prev, nxt = pl.ds(start, ATTN_BLOCK), pl.ds(pstart, ATTN_BLOCK), pl.ds(nstart, ATTN_BLOCK)
                kc = kd[cur, :]
                dqs = [_nn(dsb[2 * n + a, :, curl], kc) for a in range(2)]
                q_rows, do_rows = list(halves(qd[cur, :])), list(halves(dod[cur, :]))
                ds_rows, p_rows = [dsb[2 * n + a, :, curl] for a in range(2)], [pb[2 * n + a, :, curl] for a in range(2)]
                if use_prev:
                    kp = kd[prev, :]
                    dqs = [dqs[a] + _nn(dsb[2 * n + a, :, prevl], kp) for a in range(2)]
                    q_rows += list(halves(qd[nxt, :]))
                    do_rows += list(halves(dod[nxt, :]))
                    ds_rows += [dsb[2 * n + 2 + a, :, prevl] for a in range(2)]
                    p_rows += [pb[2 * n + 2 + a, :, prevl] for a in range(2)]
                dqd[cur, :] = jnp.where(low, dqs[0], dqs[1])
                dkd[cur, :] = _tn(jnp.concatenate(ds_rows, axis=0), jnp.concatenate(q_rows, axis=0))
                dvd[cur, :] = _tn(jnp.concatenate(p_rows, axis=0), jnp.concatenate(do_rows, axis=0))
                return carry

            lax.fori_loop(0, n_blk, probs, 0, unroll=ATTN_UNROLL)
            lax.fori_loop(0, n_blk, grads, 0, unroll=ATTN_UNROLL)
            _interleave_store(dqd, dqa, d, True)
            _interleave_store(dkd, dka, d, True)
            _interleave_store(dvd, dva, d, True)
        dq_ref[...] = _rot_t(dqa[...] * (HEAD_DIM ** -0.5), c, s1, s2).astype(bf16)
        dkf = dka[...]
        dkf = _rot_t(dkf + pltpu.roll(dkf, HEAD_DIM, 1), c, s1, s2)
        dvf = dva[...]
        dvf = dvf + pltpu.roll(dvf, HEAD_DIM, 1)
        mine = (lax.broadcasted_iota(jnp.int32, (SEQ, LANES), 1) // HEAD_DIM) == kvh
        dkc_, dvc_ = jnp.where(mine, dkf, 0.0), jnp.where(mine, dvf, 0.0)

        @pl.when(j == 0)
        def _():
            dk_acc[...] = dkc_
            dv_acc[...] = dvc_

        @pl.when(j > 0)
        def _():
            dk_acc[...] += dkc_
            dv_acc[...] += dvc_

        @pl.when(j == n_j - 1)
        def _():
            dk_ref[...] = dk_acc[...].astype(bf16)
            dv_ref[...] = dv_acc[...].astype(bf16)

    def col(jj):
        return pl.BlockSpec((SEQ, LANES), lambda b, j: (b, jj if jj is not None else j))

    tab = pl.BlockSpec((SEQ, LANES), lambda b, j: (b, 0))
    fs = pltpu.VMEM((SEQ, LANES), f32)
    hs = pltpu.VMEM((SEQ, LANES), bf16)
    return pl.pallas_call(
        body, name=name, grid=(nb, n_j),
        in_specs=[pl.BlockSpec((N_BRANCH, SEQ, LANES), lambda b, j: (0, b, j)),
                  pl.BlockSpec((1, N_BRANCH, SEQ, LANES), lambda b, j: (0, 0, b, j // 2)),
                  pl.BlockSpec((1, N_BRANCH, SEQ, LANES), lambda b, j: (1, 0, b, j // 2)),
                  tab, tab, tab, col(None), col(None), col(None)],
        out_specs=[col(None), tab, tab],
        out_shape=[jax.ShapeDtypeStruct((t, ATTN_WIDTH), bf16), jax.ShapeDtypeStruct((t, LANES), bf16), jax.ShapeDtypeStruct((t, LANES), bf16)],
        scratch_shapes=[fs, hs, fs, fs, fs, fs, fs, fs, fs, fs,
                        pltpu.VMEM((2 * n_blk + 2, ATTN_BLOCK, 2 * ATTN_BLOCK), bf16), pltpu.VMEM((2 * n_blk + 2, ATTN_BLOCK, 2 * ATTN_BLOCK), bf16), fs, fs],
        compiler_params=_cparams(("parallel", "arbitrary")),
    )(q_all, kv_all, kv_all, *tabs, o, lse, do)


def _conv_pre(x, w_ref, b_ref, row):
    shifted = [x] + [jnp.where(row >= s, pltpu.roll(x, s, 0), 0.0) for s in range(1, CONV_WIDTH)]
    pre = b_ref[...] + w_ref[CONV_WIDTH - 1:CONV_WIDTH, :] * x
    for s in range(1, CONV_WIDTH):
        pre = pre + w_ref[CONV_WIDTH - 1 - s:CONV_WIDTH - s, :] * shifted[s]
    return pre, shifted


def _conv_fwd(x, w, b, name, tc=512):
    t, ch = x.shape

    def body(x_ref, w_ref, b_ref, o_ref):
        row = lax.broadcasted_iota(jnp.int32, (SEQ, tc), 0)
        pre, _ = _conv_pre(x_ref[...], w_ref, b_ref, row)
        o_ref[...] = _silu(pre)

    xs = pl.BlockSpec((SEQ, tc), lambda i, j: (i, j))
    return pl.pallas_call(
        body, name=name, grid=(t // SEQ, ch // tc),
        in_specs=[xs, pl.BlockSpec((CONV_WIDTH, tc), lambda i, j: (0, j)), pl.BlockSpec((1, tc), lambda i, j: (0, j))],
        out_specs=xs, out_shape=jax.ShapeDtypeStruct((t, ch), f32),
        compiler_params=_cparams(("parallel", "parallel")),
    )(x, w, b)


def _conv_bwd(x, w, b, dact, name, tc=512):
    t, ch = x.shape

    def body(x_ref, w_ref, b_ref, d_ref, dx_ref, dw_ref, db_ref):
        row = lax.broadcasted_iota(jnp.int32, (SEQ, tc), 0)
        pre, shifted = _conv_pre(x_ref[...], w_ref, b_ref, row)
        dpre = d_ref[...] * _dsilu(pre)
        dx = w_ref[CONV_WIDTH - 1:CONV_WIDTH, :] * dpre
        for s in range(1, CONV_WIDTH):
            dx = dx + w_ref[CONV_WIDTH - 1 - s:CONV_WIDTH - s, :] * jnp.where(row < SEQ - s, pltpu.roll(dpre, SEQ - s, 0), 0.0)
        dx_ref[...] = dx.astype(bf16)
        first = pl.program_id(1) == 0
        parts = [jnp.sum(dpre * shifted[CONV_WIDTH - 1 - k], axis=0, keepdims=True) for k in range(CONV_WIDTH)]
        dbp = jnp.sum(dpre, axis=0, keepdims=True)

        @pl.when(first)
        def _():
            for k in range(CONV_WIDTH):
                dw_ref[k:k + 1, :] = parts[k]
            db_ref[...] = dbp

        @pl.when(jnp.logical_not(first))
        def _():
            for k in range(CONV_WIDTH):
                dw_ref[k:k + 1, :] += parts[k]
            db_ref[...] += dbp

    xs = pl.BlockSpec((SEQ, tc), lambda j, i: (i, j))
    ws = pl.BlockSpec((CONV_WIDTH, tc), lambda j, i: (0, j))
    bs = pl.BlockSpec((1, tc), lambda j, i: (0, j))
    return pl.pallas_call(
        body, name=name, grid=(ch // tc, t // SEQ),
        in_specs=[xs, ws, bs, xs], out_specs=[xs, ws, bs],
        out_shape=[jax.ShapeDtypeStruct((t, ch), bf16), jax.ShapeDtypeStruct((CONV_WIDTH, ch), f32), jax.ShapeDtypeStruct((1, ch), f32)],
        compiler_params=_cparams(("parallel", "arbitrary")),
    )(x, w, b, dact)


GROUP_W = SSM_INNER // SSM_GROUPS
HEADS_PER_GROUP = SSM_HEADS // SSM_GROUPS


def _split3(x):
    hi = x.astype(bf16)
    r1 = x - hi.astype(f32)
    mid = r1.astype(bf16)
    lo = (r1 - mid.astype(f32)).astype(bf16)
    return hi, mid, lo


def _dot_exact(x, sel, dims, x_is_lhs=True):
    parts = _split3(x)
    if x_is_lhs:
        return _dot(parts[0], sel, dims) + _dot(parts[1], sel, dims) + _dot(parts[2], sel, dims)
    return _dot(sel, parts[0], dims) + _dot(sel, parts[1], dims) + _dot(sel, parts[2], dims)


def _ssd_common(xbc_ref, dt_ref, bias_ref, alog_ref):
    r = lax.broadcasted_iota(jnp.int32, (CHUNK, CHUNK), 0)
    cidx = lax.broadcasted_iota(jnp.int32, (CHUNK, CHUNK), 1)
    causal = r >= cidx
    tril = causal.astype(bf16)
    expand = (lax.broadcasted_iota(jnp.int32, (CHUNK, SSM_INNER), 0)
              == lax.broadcasted_iota(jnp.int32, (CHUNK, SSM_INNER), 1) // HEAD_DIM).astype(bf16)
    head_lane = cidx < SSM_HEADS
    dtp = dt_ref[...] + bias_ref[...]
    dt = jnp.where(head_lane, _softplus(dtp), 0.0)
    a_neg = -jnp.exp(alog_ref[...])
    a = dt * a_neg
    nn_dims = ((1,), (0,))
    cs = _dot_exact(a, tril, nn_dims, x_is_lhs=False)
    dt_e = _dot_exact(dt, expand, nn_dims)
    cs_e = _dot_exact(cs, expand, nn_dims)
    xs = xbc_ref[:, 0:SSM_INNER]
    xg = xs * dt_e
    ecs = jnp.exp(cs_e)
    cs_last = cs_e[CHUNK - 1:CHUNK, :]
    dse = jnp.exp(cs_last - cs_e)
    cde = jnp.exp(cs_last)
    return dict(r=r, cidx=cidx, causal=causal, tril=tril, expand=expand, head_lane=head_lane, dtp=dtp, dt=dt, a_neg=a_neg,
                cs=cs, cst=cs.T, dt_e=dt_e, cs_e=cs_e, xs=xs, xg=xg, ecs=ecs, dse=dse, cde=cde)


def _decay_mat(q, h):
    return jnp.exp(jnp.where(q["causal"], q["cs"][:, h:h + 1] - q["cst"][h:h + 1, :], NEG_INF))


def _gate_norm(y, z, nw, gate=None):
    y2 = y * (_silu(z) if gate is None else gate)
    outs, xhats, rs = [], [], []
    for g in range(SSM_GROUPS):
        sl = slice(g * GROUP_W, (g + 1) * GROUP_W)
        yg = y2[:, sl]
        r = lax.rsqrt(jnp.mean(yg * yg, axis=-1, keepdims=True) + EPS)
        xhats.append(yg * r)
        rs.append(r)
        outs.append(yg * r * nw[:, sl])
    return y2, outs, xhats, rs


def _ssd_fwd(xbc, z, dtp, params, name):
    t = xbc.shape[0]
    n_chunk = SEQ // CHUNK
    low = None

    def body(xbc_ref, z_ref, dt_ref, bias_ref, alog_ref, dskip_ref, nw_ref, yn_ref, y_ref, hs_ref, h_scr):
        @pl.when(pl.program_id(1) == 0)
        def _():
            h_scr[...] = jnp.zeros_like(h_scr)

        q = _ssd_common(xbc_ref, dt_ref, bias_ref, alog_ref)
        low = lax.broadcasted_iota(jnp.int32, (CHUNK, LANES), 1) < HEAD_DIM
        xgb = q["xg"].astype(bf16)
        wst = (q["xg"] * q["dse"]).astype(bf16)
        hs_ref[0] = h_scr[...]
        ys = []
        for g in range(SSM_GROUPS):
            gl = slice(g * GROUP_W, (g + 1) * GROUP_W)
            bg = xbc_ref[:, SSM_INNER + g * D_STATE:SSM_INNER + (g + 1) * D_STATE].astype(bf16)
            cg = xbc_ref[:, SSM_INNER + SSM_GROUPS * D_STATE + g * D_STATE:SSM_INNER + SSM_GROUPS * D_STATE + (g + 1) * D_STATE].astype(bf16)
            cb = _nt(cg, bg)
            hg = h_scr[g]
            yoff = _nn(cg, hg.astype(bf16)) * q["ecs"][:, gl]
            pieces = []
            for i in range(HEADS_PER_GROUP // 2):
                h0 = g * HEADS_PER_GROUP + 2 * i
                xp = xgb[:, h0 * HEAD_DIM:(h0 + 2) * HEAD_DIM]
                m0 = (cb * _decay_mat(q, h0)).astype(bf16)
                m1 = (cb * _decay_mat(q, h0 + 1)).astype(bf16)
                zero = jnp.zeros_like(xp)
                pieces.append(_nn(m0, jnp.where(low, xp, zero)) + _nn(m1, jnp.where(low, zero, xp)))
            ys.append(jnp.concatenate(pieces, axis=1) + yoff + dskip_ref[:, gl] * q["xs"][:, gl])
            h_scr[g] = hg * q["cde"][:, gl] + _tn(bg, wst[:, gl])
        y = jnp.concatenate(ys, axis=1)
        y_ref[...] = y
        _, outs, _, _ = _gate_norm(y, z_ref[...], nw_ref[...])
        yn_ref[...] = jnp.concatenate(outs, axis=1).astype(bf16)

    def rows(w):
        return pl.BlockSpec((CHUNK, w), lambda b, c: (b * n_chunk + c, 0))

    def par(w):
        return pl.BlockSpec((1, w), lambda b, c: (0, 0))

    return pl.pallas_call(
        body, name=name, grid=(t // SEQ, n_chunk),
        in_specs=[rows(CONV_CH), rows(SSM_INNER), rows(LANES), par(LANES), par(LANES), par(SSM_INNER), par(SSM_INNER)],
        out_specs=[rows(SSM_INNER), rows(SSM_INNER), pl.BlockSpec((1, SSM_GROUPS, D_STATE, GROUP_W), lambda b, c: (b * n_chunk + c, 0, 0, 0))],
        out_shape=[jax.ShapeDtypeStruct((t, SSM_INNER), bf16), jax.ShapeDtypeStruct((t, SSM_INNER), f32),
                   jax.ShapeDtypeStruct((t // CHUNK, SSM_GROUPS, D_STATE, GROUP_W), f32)],
        scratch_shapes=[pltpu.VMEM((SSM_GROUPS, D_STATE, GROUP_W), f32)],
        compiler_params=_cparams(("parallel", "arbitrary")),
    )(xbc, z, dtp, *params)


def _ssd_bwd(xbc, z, dtp, y, hs, dyn, params, name):
    t = xbc.shape[0]
    n_chunk = SEQ // CHUNK

    def body(xbc_ref, z_ref, dt_ref, y_ref, hs_ref, dyn_ref, bias_ref, alog_ref, dskip_ref, nw_ref,
             dxbc_ref, dz_ref, ddt_ref, dnw_ref, dds_ref, dal_ref, dbi_ref, dh_scr):
        @pl.when(pl.program_id(1) == 0)
        def _():
            dh_scr[...] = jnp.zeros_like(dh_scr)

        q = _ssd_common(xbc_ref, dt_ref, bias_ref, alog_ref)
        low = lax.broadcasted_iota(jnp.int32, (CHUNK, LANES), 1) < HEAD_DIM
        last_row = lax.broadcasted_iota(jnp.int32, (CHUNK, GROUP_W), 0) == CHUNK - 1
        xs, xg = q["xs"], q["xg"]
        xgb = xg.astype(bf16)
        wf = xg * q["dse"]
        wst = wf.astype(bf16)
        zz = z_ref[...]
        yy = y_ref[...]
        sz, dsz = _silu_and_grad(zz)
        y2, _, xhats, rs = _gate_norm(yy, zz, nw_ref[...], gate=sz)
        dyn_ = dyn_ref[...]
        dy2s, dnws = [], []
        for g in range(SSM_GROUPS):
            gl = slice(g * GROUP_W, (g + 1) * GROUP_W)
            gw = dyn_[:, gl] * nw_ref[:, gl]
            dy2s.append(rs[g] * (gw - xhats[g] * jnp.mean(gw * xhats[g], axis=-1, keepdims=True)))
            dnws.append(_rowsum8(dyn_[:, gl] * xhats[g]))
        dy2 = jnp.concatenate(dy2s, axis=1)
        dy = dy2 * sz
        dz_ref[...] = (dy2 * yy * dsz).astype(bf16)
        dnw_p = jnp.concatenate(dnws, axis=1)
        dds_p = _rowsum8(dy * xs)
        dyb = dy.astype(bf16)
        gfull = (dy * q["ecs"]).astype(bf16)
        dcs_c = jnp.zeros((CHUNK, CHUNK), f32)
        dcs_r = jnp.zeros((CHUNK, CHUNK), f32)
        dcs_e_parts, dxg_parts = [], []
        for g in range(SSM_GROUPS):
            gl = slice(g * GROUP_W, (g + 1) * GROUP_W)
            bsl = slice(SSM_INNER + g * D_STATE, SSM_INNER + (g + 1) * D_STATE)
            csl = slice(SSM_INNER + SSM_GROUPS * D_STATE + g * D_STATE, SSM_INNER + SSM_GROUPS * D_STATE + (g + 1) * D_STATE)
            bg = xbc_ref[:, bsl].astype(bf16)
            cg = xbc_ref[:, csl].astype(bf16)
            cb = _nt(cg, bg)
            hg = hs_ref[0, g]
            hgb = hg.astype(bf16)
            dhn = dh_scr[g]
            dhnb = dhn.astype(bf16)
            yoff = _nn(cg, hgb) * q["ecs"][:, gl]
            dw_ = _nn(bg, dhnb)
            r_e = dw_ * wf[:, gl]
            to_last = jnp.sum(r_e, axis=0, keepdims=True) + jnp.sum(dhn * hg, axis=0, keepdims=True) * q["cde"][:, gl]
            dcs_e_parts.append(dy[:, gl] * yoff - r_e + jnp.where(last_row, to_last, 0.0))
            dcb = jnp.zeros((CHUNK, CHUNK), f32)
            dxg_pairs = []
            for i in range(HEADS_PER_GROUP // 2):
                h0 = g * HEADS_PER_GROUP + 2 * i
                psl = slice(h0 * HEAD_DIM, (h0 + 2) * HEAD_DIM)
                xp = xgb[:, psl]
                dyp = dyb[:, psl]
                zero = jnp.zeros_like(dyp)
                tns = []
                for a in range(2):
                    h = h0 + a
                    lm = _decay_mat(q, h)
                    m = cb * lm
                    dm = _nt(jnp.where(low, dyp, zero) if a == 0 else jnp.where(low, zero, dyp), xp)
                    dcb = dcb + dm * lm
                    nmat = dm * m
                    dcs_c = dcs_c + jnp.where(q["cidx"] == h, jnp.sum(nmat, axis=1, keepdims=True), 0.0)
                    dcs_r = dcs_r + jnp.where(q["r"] == h, jnp.sum(nmat, axis=0, keepdims=True), 0.0)
                    tns.append(_tn(m.astype(bf16), dyp))
                dxg_pairs.append(jnp.where(low, tns[0], tns[1]))
            dxg_parts.append(jnp.concatenate(dxg_pairs, axis=1) + dw_ * q["dse"][:, gl])
            dcbb = dcb.astype(bf16)
            dxbc_ref[:, csl] = _nt(gfull[:, gl], hgb) + _nn(dcbb, bg)
            dxbc_ref[:, bsl] = _nt(wst[:, gl], dhnb) + _tn(dcbb, cg)
            dh_scr[g] = dhn * q["cde"][:, gl] + _tn(cg, gfull[:, gl])
        dxg = jnp.concatenate(dxg_parts, axis=1)
        dcs_e = jnp.concatenate(dcs_e_parts, axis=1)
        dxbc_ref[:, 0:SSM_INNER] = dskip_ref[...] * dy + dxg * q["dt_e"]
        dcs = dcs_c - dcs_r.T + _dot_exact(dcs_e, q["expand"], ((1,), (1,)))
        triu = (q["cidx"] >= q["r"]).astype(bf16)
        da = _dot_exact(dcs, triu, ((1,), (0,)), x_is_lhs=False)
        ddt = _dot_exact(dxg * xs, q["expand"], ((1,), (1,))) + da * q["a_neg"]
        ddtp = jnp.where(q["head_lane"], ddt * _sigmoid(q["dtp"]), 0.0)
        ddt_ref[...] = ddtp.astype(bf16)
        dal_p = _rowsum8(da * q["dt"]) * q["a_neg"]
        dbi_p = _rowsum8(ddtp)
        first = (pl.program_id(0) == 0) & (pl.program_id(1) == 0)

        @pl.when(first)
        def _():
            dnw_ref[...] = dnw_p
            dds_ref[...] = dds_p
            dal_ref[...] = dal_p
            dbi_ref[...] = dbi_p

        @pl.when(jnp.logical_not(first))
        def _():
            dnw_ref[...] += dnw_p
            dds_ref[...] += dds_p
            dal_ref[...] += dal_p
            dbi_ref[...] += dbi_p

    def rows(w):
        return pl.BlockSpec((CHUNK, w), lambda b, c: (b * n_chunk + n_chunk - 1 - c, 0))

    def par(w):
        return pl.BlockSpec((1, w), lambda b, c: (0, 0))

    def acc(w):
        return pl.BlockSpec((SUBLANES, w), lambda b, c: (0, 0))

    return pl.pallas_call(
        body, name=name, grid=(t // SEQ, n_chunk),
        in_specs=[rows(CONV_CH), rows(SSM_INNER), rows(LANES), rows(SSM_INNER),
                  pl.BlockSpec((1, SSM_GROUPS, D_STATE, GROUP_W), lambda b, c: (b * n_chunk + n_chunk - 1 - c, 0, 0, 0)),
                  rows(SSM_INNER), par(LANES), par(LANES), par(SSM_INNER), par(SSM_INNER)],
        out_specs=[rows(CONV_CH), rows(SSM_INNER), rows(LANES), acc(SSM_INNER), acc(SSM_INNER), acc(LANES), acc(LANES)],
        out_shape=[jax.ShapeDtypeStruct((t, CONV_CH), f32), jax.ShapeDtypeStruct((t, SSM_INNER), bf16), jax.ShapeDtypeStruct((t, LANES), bf16),
                   jax.ShapeDtypeStruct((SUBLANES, SSM_INNER), f32), jax.ShapeDtypeStruct((SUBLANES, SSM_INNER), f32),
                   jax.ShapeDtypeStruct((SUBLANES, LANES), f32), jax.ShapeDtypeStruct((SUBLANES, LANES), f32)],
        scratch_shapes=[pltpu.VMEM((SSM_GROUPS, D_STATE, GROUP_W), f32)],
        compiler_params=_cparams(("arbitrary", "arbitrary")),
    )(xbc, z, dtp, y, hs, dyn, *params)


def _adamw_update(g, w, m, v):
    mm = ADAM_B1 * m + (1.0 - ADAM_B1) * g
    vv = ADAM_B2 * v + (1.0 - ADAM_B2) * (g * g)
    m_hat = mm / (1.0 - ADAM_B1 ** ADAM_STEP)
    v_hat = vv / (1.0 - ADAM_B2 ** ADAM_STEP)
    return -ADAM_LR * (m_hat / (jnp.sqrt(v_hat) + ADAM_EPS) + ADAM_WD * w), mm, vv


def _adamw(g_parts, w, m, v, name):
    rows, width = w.shape
    n = len(g_parts)
    tr = _row_tile(rows)

    def body(*refs):
        g_refs, (w_ref, m_ref, v_ref, g_out, d_out, m_out, v_out) = refs[:n], refs[n:]
        g = g_refs[0][...].astype(f32)
        for r in g_refs[1:]:
            g = g + r[...].astype(f32)
        g_out[...] = g
        d_out[...], m_out[...], v_out[...] = _adamw_update(g, w_ref[...], m_ref[...], v_ref[...])

    spec = pl.BlockSpec((tr, width), lambda i: (i, 0))
    return pl.pallas_call(
        body, name=name, grid=(rows // tr,), in_specs=[spec] * (n + 3), out_specs=[spec] * 4,
        out_shape=[jax.ShapeDtypeStruct((rows, width), f32)] * 4, compiler_params=_cparams(("parallel",)),
    )(*g_parts, w, m, v)


def _adamw_layers(landed, w, m, v, name):
    depth, rows, width = w.shape
    tr = _row_tile(rows)
    n_i = rows // tr

    def body(*refs):
        part_refs, (w_ref, m_ref, v_ref, g_out, d_out, m_out, v_out) = refs[:depth * N_DEV], refs[depth * N_DEV:]
        for l in range(depth):
            @pl.when(pl.program_id(0) == l)
            def _(l=l):
                g = part_refs[l * N_DEV][0].astype(f32)
                for r in part_refs[l * N_DEV + 1:(l + 1) * N_DEV]:
                    g = g + r[0].astype(f32)
                g_out[0] = g
                d_out[0], m_out[0], v_out[0] = _adamw_update(g, w_ref[0], m_ref[0], v_ref[0])

    def part_spec(l, p):
        return pl.BlockSpec((1, tr, width), lambda ll, i: (p, jnp.where(ll == l, i, jnp.where(ll < l, 0, n_i - 1)), 0))

    state = pl.BlockSpec((1, tr, width), lambda ll, i: (ll, i, 0))
    return pl.pallas_call(
        body, name=name, grid=(depth, n_i),
        in_specs=[part_spec(l, p) for l in range(depth) for p in range(N_DEV)] + [state] * 3, out_specs=[state] * 4,
        out_shape=[jax.ShapeDtypeStruct(w.shape, f32)] * 4, compiler_params=_cparams(("arbitrary", "arbitrary")),
    )(*[landed[l] for l in range(depth) for _ in range(N_DEV)], w, m, v)


def _row_tile(rows, cap=512):
    for cand in range(min(rows, cap) // SUBLANES * SUBLANES, 0, -SUBLANES):
        if rows % cand == 0:
            return cand
    return rows


def _cols_from_devices(g, width, name):
    n_dev, depth, a, b = g.shape

    def body(g_ref, o_ref):
        for i in range(n_dev):
            o_ref[0, :, i * b:(i + 1) * b] = g_ref[i, 0]
        if width > n_dev * b:
            o_ref[0, :, n_dev * b:width] = jnp.zeros((a, width - n_dev * b), o_ref.dtype)

    return pl.pallas_call(
        body, name=name, grid=(depth,), in_specs=[pl.BlockSpec((n_dev, 1, a, b), lambda l: (0, l, 0, 0))],
        out_specs=pl.BlockSpec((1, a, width), lambda l: (l, 0, 0)), out_shape=jax.ShapeDtypeStruct((depth, a, width), g.dtype),
        compiler_params=_cparams(("parallel",)),
    )(g)


def _devices_from_cols(per_layer, b, name, tr=256):
    depth = len(per_layer)
    a, width = per_layer[0].shape

    def body(*refs):
        o_ref = refs[depth]
        for l in range(depth):
            for i in range(N_DEV):
                o_ref[i, l] = refs[l][:, i * b:(i + 1) * b]

    return pl.pallas_call(
        body, name=name, grid=(a // tr,), in_specs=[pl.BlockSpec((tr, width), lambda r: (r, 0))] * depth,
        out_specs=pl.BlockSpec((N_DEV, depth, tr, b), lambda r: (0, 0, r, 0)),
        out_shape=jax.ShapeDtypeStruct((N_DEV, depth, a, b), per_layer[0].dtype), compiler_params=_cparams(("parallel",)),
    )(*per_layer)


def _me():
    return lax.axis_index("x"), lax.axis_index("y"), lax.axis_index("c")


def _allgather_two_level(shards, name):
    n = len(shards)
    per = 7

    def body(*refs):
        ins, outs, token = refs[:n], refs[n:2 * n], refs[2 * n]
        send_sems, recv_sems, local_sems = refs[2 * n + 1:]
        token[...] = jnp.zeros_like(token)
        x, y, c = _me()
        me, sibling = (x, y, c), (x, y, 1 - c)
        chips = [(1 - x, y), (x, 1 - y), (1 - x, 1 - y)]

        def slot(a, p):
            return outs[a].at[4 * p[0] + 2 * p[1] + p[2]]

        def copy(a, k, block, to, src=None):
            return pltpu.make_async_remote_copy(
                src_ref=slot(a, block) if src is None else src, dst_ref=slot(a, block),
                send_sem=send_sems.at[a * per + k], recv_sem=recv_sems.at[a * per + k], device_id=to, device_id_type=MESH)

        mine = [pltpu.make_async_copy(ins[a], slot(a, me), local_sems.at[a]) for a in range(n)]
        for cp in mine:
            cp.start()
        first = []
        for a in range(n):
            first.append(copy(a, 0, me, sibling, src=ins[a]))
            first += [copy(a, 1 + j, me, (*chip, c), src=ins[a]) for j, chip in enumerate(chips)]
        for cp in first:
            cp.start()
        passed = []
        for j, chip in enumerate(chips):
            for a in range(n):
                copy(a, 1 + j, (*chip, c), me).wait_recv()
                fwd = copy(a, 4 + j, (*chip, c), sibling)
                fwd.start()
                passed.append(fwd)
        for a in range(n):
            copy(a, 0, sibling, me).wait_recv()
            for j, chip in enumerate(chips):
                copy(a, 4 + j, (*chip, 1 - c), me).wait_recv()
        for cp in first + passed:
            cp.wait_send()
        for cp in mine:
            cp.wait()

    outs = pl.pallas_call(
        body, name=name, in_specs=[ANY] * n, out_specs=[ANY] * n + [pl.BlockSpec(memory_space=pltpu.VMEM)],
        out_shape=[jax.ShapeDtypeStruct((N_DEV,) + s.shape, s.dtype) for s in shards] + [jax.ShapeDtypeStruct((SUBLANES, LANES), f32)],
        scratch_shapes=[pltpu.SemaphoreType.DMA((n * per,)), pltpu.SemaphoreType.DMA((n * per,)), pltpu.SemaphoreType.DMA((n,))],
    )(*shards)
    return outs[:n], outs[n]


def _allgather_direct(row, name):
    def body(in_ref, out_ref, send_sems, recv_sems, local_sem):
        x, y, c = _me()
        mine = out_ref.at[4 * x + 2 * y + c]
        local = pltpu.make_async_copy(in_ref, mine, local_sem)
        local.start()
        sends = []
        for k in range(1, N_DEV):
            px, py, pc = x ^ (k >> 2), y ^ ((k >> 1) & 1), c ^ (k & 1)
            sends.append(pltpu.make_async_remote_copy(
                src_ref=in_ref, dst_ref=mine, send_sem=send_sems.at[k - 1], recv_sem=recv_sems.at[k - 1],
                device_id=(px, py, pc), device_id_type=MESH))
        for cp in sends:
            cp.start()
        for k in range(1, N_DEV):
            px, py, pc = x ^ (k >> 2), y ^ ((k >> 1) & 1), c ^ (k & 1)
            theirs = out_ref.at[4 * px + 2 * py + pc]
            pltpu.make_async_remote_copy(
                src_ref=in_ref, dst_ref=theirs, send_sem=send_sems.at[k - 1], recv_sem=recv_sems.at[k - 1],
                device_id=(px, py, pc), device_id_type=MESH).wait_recv()
        for cp in sends:
            cp.wait_send()
        local.wait()

    return pl.pallas_call(
        body, name=name, in_specs=[ANY], out_specs=ANY, out_shape=jax.ShapeDtypeStruct((N_DEV,) + row.shape, row.dtype),
        scratch_shapes=[pltpu.SemaphoreType.DMA((N_DEV - 1,)), pltpu.SemaphoreType.DMA((N_DEV - 1,)), pltpu.SemaphoreType.DMA],
    )(row)


N_CHIP = N_DEV // 2
HBM = pl.BlockSpec(memory_space=pltpu.HBM)
SEM = pl.BlockSpec(memory_space=pltpu.SEMAPHORE)
EFFECT = pltpu.SideEffectType.DATAFLOW_SIDE_EFFECTING


def _peer(k):
    x, y, c = _me()
    return x ^ (k >> 2), y ^ ((k >> 1) & 1), c ^ (k & 1)


def _direct_copies(srcs, lands, send_sems, recv_sems, per_peer):
    x, y, c = _me()
    me = 4 * x + 2 * y + c
    copies = []
    for a in range(len(srcs)):
        for k in range(1, N_DEV):
            px, py, pc = _peer(k)
            piece = srcs[a].at[4 * px + 2 * py + pc] if per_peer else srcs[a]
            copies.append(pltpu.make_async_remote_copy(
                src_ref=piece, dst_ref=lands[a].at[me], send_sem=send_sems.at[a * (N_DEV - 1) + k - 1],
                recv_sem=recv_sems.at[a * (N_DEV - 1) + k - 1], device_id=(px, py, pc), device_id_type=MESH))
    return copies


def _direct_start(srcs, lands, per_peer, name):
    n = len(srcs)
    n_sem = n * (N_DEV - 1)

    def body(*refs):
        src_refs, land_refs = refs[:n], refs[n:2 * n]
        send_sems, recv_sems = refs[2 * n], refs[2 * n + 1]
        token = refs[-1]
        for cp in _direct_copies(src_refs, land_refs, send_sems, recv_sems, per_peer):
            cp.start()
        token[...] = jnp.zeros_like(token)

    outs = pl.pallas_call(
        body, name=name,
        out_shape=(pltpu.SemaphoreType.DMA((n_sem,)), pltpu.SemaphoreType.DMA((n_sem,)),
                   *[pltpu.HBM(s.shape, s.dtype) for s in srcs], *[pltpu.HBM(s.shape, s.dtype) for s in lands],
                   jax.ShapeDtypeStruct((SUBLANES, LANES), f32)),
        in_specs=[HBM] * (2 * n), out_specs=(SEM, SEM, *[HBM] * (2 * n), pl.BlockSpec(memory_space=pltpu.VMEM)),
        input_output_aliases={i: 2 + i for i in range(2 * n)},
        compiler_params=pltpu.CompilerParams(has_side_effects=EFFECT),
    )(*[pltpu.with_memory_space_constraint(s, pltpu.HBM) for s in srcs], *[pltpu.with_memory_space_constraint(s, pltpu.HBM) for s in lands])
    return outs[0], outs[1], outs[2:2 + n], outs[2 + n:2 + 2 * n], outs[-1]


def _direct_wait(send_sems, recv_sems, srcs, lands, after, per_peer, name):
    n = len(srcs)

    def body(*refs):
        src_refs, land_refs = refs[:n], refs[n:2 * n]
        s_sems, r_sems = refs[2 * n], refs[2 * n + 1]
        for cp in _direct_copies(src_refs, land_refs, s_sems, r_sems, per_peer):
            cp.wait_send()
            cp.wait_recv()

    outs = pl.pallas_call(
        body, name=name,
        out_shape=tuple(pltpu.HBM(s.shape, s.dtype) for s in list(srcs) + list(lands)),
        in_specs=[HBM] * (2 * n) + [SEM, SEM, ANY], out_specs=tuple([HBM] * (2 * n)),
        input_output_aliases={i: i for i in range(2 * n)},
        compiler_params=pltpu.CompilerParams(has_side_effects=EFFECT),
    )(*srcs, *lands, send_sems, recv_sems, after)
    return outs[n:]


def _row(v, width=None):
    v = v.reshape(1, -1).astype(f32)
    if width is not None and v.shape[1] < width:
        v = jnp.pad(v, ((0, 0), (0, width - v.shape[1])))
    return v


def _layer_params(p, l):
    return dict(
        norm_mix=_row(p["norm_mix"][l]), norm_ffn=_row(p["norm_ffn"][l]), conv_w=p["conv_w"][l], conv_b=_row(p["conv_b"][l]),
        ssd=(_row(p["dt_bias"][l], LANES), _row(p["a_log"][l], LANES), _row(jnp.repeat(p["d_skip"][l], HEAD_DIM)), _row(p["ssm_norm"][l])))


def _layer_fwd(h, w_in, rest, sp, tabs, l):
    tag = f"l{l}_"
    hn = _rmsnorm_fwd(h, sp["norm_mix"], tag + "norm_mix")
    qkv = _matmul(hn, w_in, mode="nn", n_out=QKV_WIDTH, tn=256, b_off=0, name=tag + "proj_qkv")
    z = _matmul(hn, w_in, mode="nn", n_out=SSM_INNER, tn=256, b_off=Z_OFF // 256, name=tag + "proj_z")
    xbc_pre = _matmul(hn, w_in, mode="nn", n_out=CONV_CH, tn=256, b_off=XBC_OFF // 256, name=tag + "proj_xbc")
    dtp = _matmul(hn, w_in, mode="nn", n_out=LANES, tn=LANES, b_off=DT_OFF // LANES, name=tag + "proj_dt")
    prep = _attn_prep(qkv, tabs, tag + "attn_prep")
    o, lse = _attn_fwd(prep, tag + "attn_fwd")
    xbc = _conv_fwd(xbc_pre, sp["conv_w"], sp["conv_b"], tag + "conv_fwd")
    yn, y, hs = _ssd_fwd(xbc, z, dtp, sp["ssd"], tag + "ssd_fwd")
    w_out, w_gate, w_up, w_down = rest(yn) if callable(rest) else rest
    h2 = _out_proj(o, yn, w_out, h, tag + "out_proj")
    hn2 = _rmsnorm_fwd(h2, sp["norm_ffn"], tag + "norm_ffn")
    g, u, act = _swiglu_fwd(hn2, w_gate, w_up, tag + "ffn_up")
    h3 = _matmul(act, w_down, mode="nn", tk=1408, add=h2, name=tag + "ffn_down")
    saved = dict(h=h, hn=hn, prep=prep, z=z, xbc_pre=xbc_pre, dtp=dtp, o=o, lse=lse, xbc=xbc, yn=yn, y=y, hs=hs, h2=h2, hn2=hn2, g=g, u=u, act=act,
                 rest=(w_out, w_gate, w_up, w_down))
    return h3, saved


def _layer_bwd(dh3, s, big, sp, tabs, l, gd=f32, after_ffn=None):
    tag = f"l{l}_"
    w_in, w_out, w_gate, w_up, w_down = big
    dg, du = _swiglu_bwd(dh3, w_down, s["g"], s["u"], tag + "ffn_down_bwd")
    dw_down = _matmul(s["act"], dh3, mode="tn", tm=1408, tn=512, tk=2048, out_dtype=gd, name=tag + "dw_down")
    dw_gate = _matmul(s["hn2"], dg, mode="tn", tm=512, tn=1408, tk=2048, out_dtype=gd, name=tag + "dw_gate")
    dw_up = _matmul(s["hn2"], du, mode="tn", tm=512, tn=1408, tk=2048, out_dtype=gd, name=tag + "dw_up")
    norm_ffn = sp["norm_ffn"] if after_ffn is None else sp["norm_ffn"] + after_ffn(dict(w_gate=dw_gate, w_up=dw_up, w_down=dw_down))
    dh2, dnf = _nt_norm_bwd([(dg, w_gate), (du, w_up)], s["h2"], norm_ffn, dh3, tag + "ffn_up_bwd_norm", tk=1408)
    d_o = _matmul(dh2, w_out, mode="nt", n_out=ATTN_WIDTH, tn=512, b_off=0, name=tag + "out_attn_bwd")
    dyn = _matmul(dh2, w_out, mode="nt", n_out=SSM_INNER, tn=512, b_off=1, name=tag + "out_ssm_bwd")
    dw_out = jnp.concatenate([_matmul(s["o"], dh2, mode="tn", tm=512, tn=512, tk=2048, out_dtype=gd, name=tag + "dw_out_attn"),
                              _matmul(s["yn"], dh2, mode="tn", tm=512, tn=512, tk=2048, out_dtype=gd, name=tag + "dw_out_ssm")], axis=0)
    dxbc, dz, ddtp, dnw, dds, dal, dbi = _ssd_bwd(s["xbc"], s["z"], s["dtp"], s["y"], s["hs"], dyn, sp["ssd"], tag + "ssd_bwd")
    dxbc_pre, dconv_w, dconv_b = _conv_bwd(s["xbc_pre"], sp["conv_w"], sp["conv_b"], dxbc, tag + "conv_bwd")
    dq, dk, dv = _attn_bwd(s["prep"], tabs, s["o"], s["lse"], d_o, tag + "attn_bwd")
    dproj = jnp.concatenate([dq, dk, dv, dz, dxbc_pre, ddtp], axis=1)
    dw_in = _matmul(s["hn"], dproj, mode="tn", tm=512, tn=1152, tk=2048, out_dtype=gd, name=tag + "dw_in")
    dh, dnm = _nt_norm_bwd([(dproj, w_in)], s["h"], sp["norm_mix"], dh2, tag + "proj_bwd_norm", tk=1152)
    grads = dict(
        norm_mix=dnm.sum(0), w_in=dw_in, conv_w=dconv_w, conv_b=dconv_b[0], dt_bias=dbi.sum(0)[:SSM_HEADS], a_log=dal.sum(0)[:SSM_HEADS],
        d_skip=dds.sum(0).reshape(SSM_HEADS, HEAD_DIM).sum(1), ssm_norm=dnw.sum(0), w_out=dw_out, norm_ffn=dnf.sum(0),
        w_gate=dw_gate, w_up=dw_up, w_down=dw_down)
    return dh, grads


def _local_step(x, positions, target, p, bigs):
    tabs = _rope_tables(positions.reshape(-1, 1), "rope_tables")
    h = x
    saved, sps = [], []
    for l in range(DEPTH):
        sps.append(_layer_params(p, l))
        h, s = _layer_fwd(h, bigs[l][0], bigs[l][1:], sps[l], tabs, l)
        saved.append(s)
    dh, loss_parts, dfn = _final_loss(h, _row(p["final_norm"]), target, "final_loss")
    layer_grads = [None] * DEPTH
    for l in reversed(range(DEPTH)):
        dh, layer_grads[l] = _layer_bwd(dh, saved[l], bigs[l], sps[l], tabs, l)
    grads = {k: [layer_grads[l][k] for l in range(DEPTH)] for k in layer_grads[0]}
    grads["final_norm"] = dfn.sum(0)
    return jnp.sum(loss_parts), dh, grads


BIG = ("w_in", "w_out", "w_gate", "w_up", "w_down")
REST = BIG[1:]
FFN = ("w_gate", "w_up", "w_down")
MIX = ("w_in", "w_out")
COL_SHARDED = ("w_in", "w_gate", "w_up")
SMALL = ("norm_mix", "conv_b", "dt_bias", "a_log", "d_skip", "ssm_norm", "norm_ffn", "final_norm")
WEIGHTS = ("norm_mix", "w_in", "conv_w", "conv_b", "dt_bias", "a_log", "d_skip", "ssm_norm", "w_out", "norm_ffn", "w_gate", "w_up", "w_down", "final_norm")
PACK_W = 1024
SMALL_ROWS = 88
CONVW_ROWS = 96
CONVW_SHARD_ROWS = 16


def _full_from_gathered(name, g, l):
    _, a, b = g.shape
    if name in COL_SHARDED:
        width = IN_PROJ_PAD if name == "w_in" else N_DEV * b
        return _cols_from_devices(g.reshape(N_DEV, 1, a, b), width, f"cols_l{l}_{name}").reshape(a, width)
    return g.reshape(N_DEV * a, b)


def _by_device(name, full, shard_shape, l):
    a, b = shard_shape
    if name in COL_SHARDED:
        return _devices_from_cols([full], b, f"devs_l{l}_{name}").reshape(N_CHIP, 2, a, b)
    return full.reshape(N_CHIP, 2, a, b)


def _pack_rows(parts, rows, width):
    flat = jnp.concatenate([q.reshape(-1) for q in parts])
    return jnp.pad(flat, (0, rows * width - flat.shape[0])).reshape(rows, width)


def _unpack(flat, like):
    out, off = [], 0
    for q in like:
        out.append(flat[off:off + q.size].reshape(q.shape))
        off += q.size
    return out


def kernel(x, positions, norm_mix, w_in, conv_w, conv_b, dt_bias, a_log, d_skip, ssm_norm, w_out, norm_ffn, w_gate, w_up, w_down, final_norm, loss_target, m_norm_mix, m_w_in, m_conv_w, m_conv_b, m_dt_bias, m_a_log, m_d_skip, m_ssm_norm, m_w_out, m_norm_ffn, m_w_gate, m_w_up, m_w_down, m_final_norm, v_norm_mix, v_w_in, v_conv_w, v_conv_b, v_dt_bias, v_a_log, v_d_skip, v_ssm_norm, v_w_out, v_norm_ffn, v_w_gate, v_w_up, v_w_down, v_final_norm):
    w = dict(norm_mix=norm_mix, w_in=w_in, conv_w=conv_w, conv_b=conv_b, dt_bias=dt_bias, a_log=a_log, d_skip=d_skip, ssm_norm=ssm_norm,
             w_out=w_out, norm_ffn=norm_ffn, w_gate=w_gate, w_up=w_up, w_down=w_down, final_norm=final_norm)
    m = dict(norm_mix=m_norm_mix, w_in=m_w_in, conv_w=m_conv_w, conv_b=m_conv_b, dt_bias=m_dt_bias, a_log=m_a_log, d_skip=m_d_skip,
             ssm_norm=m_ssm_norm, w_out=m_w_out, norm_ffn=m_norm_ffn, w_gate=m_w_gate, w_up=m_w_up, w_down=m_w_down, final_norm=m_final_norm)
    v = dict(norm_mix=v_norm_mix, w_in=v_w_in, conv_w=v_conv_w, conv_b=v_conv_b, dt_bias=v_dt_bias, a_log=v_a_log, d_skip=v_d_skip,
             ssm_norm=v_ssm_norm, w_out=v_w_out, norm_ffn=v_norm_ffn, w_gate=v_w_gate, w_up=v_w_up, w_down=v_w_down, final_norm=v_final_norm)
    ax, ay, ac = lax.axis_index("x"), lax.axis_index("y"), lax.axis_index("c")
    dev = 4 * ax + 2 * ay + ac

    assert DEPTH == 2
    t = x.shape[0] * x.shape[1]
    xf, target = x.reshape(t, D_MODEL), loss_target.reshape(t, D_MODEL)

    def own_slot(block):
        return lax.dynamic_update_slice(lax.empty((N_DEV,) + block.shape[1:], block.dtype), block, (dev,) + (0,) * (block.ndim - 1))

    def gather_start(keys, l, tie, name):
        shards = [(w[keys[0]][l] + tie).astype(bf16)] + [w[k][l].astype(bf16) for k in keys[1:]]
        return _direct_start(shards, [own_slot(s[None]) for s in shards], False, name)

    def scatter_start(keys, grads_l, l, name):
        by_dev = [_by_device(k, grads_l[k], w[k].shape[1:], l).reshape((N_DEV,) + w[k].shape[1:]) for k in keys]
        return _direct_start(by_dev, [own_slot(lax.dynamic_slice_in_dim(g, dev, 1, 0)) for g in by_dev], True, name)

    (g_in0, conv_all), tie = _allgather_two_level([w["w_in"][0].astype(bf16), w["conv_w"]], "gather_l0_w_in")
    rest0_copy = gather_start(REST, 0, tie[0, 0], "gather_l0_rest_start")
    l1_copy = gather_start(BIG, 1, rest0_copy[4][0, 0], "gather_l1_start")
    p = {k: w[k] for k in SMALL}
    p["norm_mix"] = p["norm_mix"] + l1_copy[4][0, 0]
    p["conv_w"] = jnp.transpose(conv_all, (1, 2, 0, 3)).reshape(DEPTH, CONV_WIDTH, CONV_CH)
    sp0, sp1 = _layer_params(p, 0), _layer_params(p, 1)

    def rest0(after):
        lands = _direct_wait(*rest0_copy[:4], after, False, "gather_l0_rest_wait")
        return tuple(_full_from_gathered(k, g, 0) for k, g in zip(REST, lands))

    tabs = _rope_tables(positions.reshape(t, 1), "rope_tables")
    w_in0 = _full_from_gathered("w_in", g_in0, 0)
    h1, saved0 = _layer_fwd(xf, w_in0, rest0, sp0, tabs, 0)
    lands1 = _direct_wait(*l1_copy[:4], h1, False, "gather_l1_wait")
    bigs1 = tuple(_full_from_gathered(k, g, 1) for k, g in zip(BIG, lands1))
    h2, saved1 = _layer_fwd(h1, bigs1[0], bigs1[1:], sp1, tabs, 1)
    dh, loss_parts, dfn = _final_loss(h2, _row(p["final_norm"]), target, "final_loss")
    loss_local = jnp.sum(loss_parts)

    dh, grads1 = _layer_bwd(dh, saved1, bigs1, sp1, tabs, 1, gd=bf16)
    l1_grads = scatter_start(BIG, grads1, 1, "scatter_l1_start")
    w_out0, w_gate0, w_up0, w_down0 = saved0["rest"]
    bigs0 = (w_in0, w_out0, w_gate0, w_up0, w_down0 + l1_grads[4][0, 0].astype(bf16))
    ffn0_grads = []

    def after_ffn(grads_ffn):
        ffn0_grads.append(scatter_start(FFN, grads_ffn, 0, "scatter_l0_ffn_start"))
        return ffn0_grads[0][4][0, 0]

    dx, grads0 = _layer_bwd(dh, saved0, bigs0, sp0, tabs, 0, gd=bf16, after_ffn=after_ffn)
    mix0_grads = scatter_start(MIX, grads0, 0, "scatter_l0_mix_start")
    landed = {(k, 1): g for k, g in zip(BIG, _direct_wait(*l1_grads[:4], dx, True, "scatter_l1_wait"))}
    landed.update({(k, 0): g for k, g in zip(FFN, _direct_wait(*ffn0_grads[0][:4], dx, True, "scatter_l0_ffn_wait"))})
    out_g, out_d, out_m, out_v = {}, {}, {}, {}

    def update(keys):
        for k in keys:
            res = _adamw_layers([landed[k, l] for l in range(DEPTH)], w[k], m[k], v[k], "adamw_" + k)
            for dst, r in zip((out_g, out_d, out_m, out_v), res):
                dst[k] = r

    update(FFN)
    grads = {k: [grads0[k], grads1[k]] for k in grads0 if k not in BIG}
    grads["final_norm"] = dfn.sum(0)

    small_like = [w[k] for k in SMALL]
    small_grads = [jnp.stack(grads[k]) if k != "final_norm" else grads[k] for k in SMALL]
    small_pack = jnp.concatenate([_pack_rows(small_grads, SMALL_ROWS, LANES), _pack_rows([jnp.stack(grads["conv_w"])], CONVW_ROWS, LANES)], axis=0)
    parts = _allgather_direct(small_pack, "gather_small_grads")
    g_s, d_s, m_s, v_s = _adamw(
        [parts[i, :SMALL_ROWS] for i in range(N_DEV)], _pack_rows(small_like, SMALL_ROWS, LANES),
        _pack_rows([m[k] for k in SMALL], SMALL_ROWS, LANES), _pack_rows([v[k] for k in SMALL], SMALL_ROWS, LANES), "adamw_replicated")
    for dst, src in ((out_g, g_s), (out_d, d_s), (out_m, m_s), (out_v, v_s)):
        dst.update(zip(SMALL, _unpack(src.reshape(-1), small_like)))
    shard_w = conv_w.shape[-1]
    conv_parts = parts[:, SMALL_ROWS:].reshape(N_DEV, DEPTH, CONV_WIDTH, CONV_CH)
    conv_mine = lax.dynamic_slice_in_dim(conv_parts, dev * shard_w, shard_w, axis=3)
    g_c, d_c, m_c, v_c = _adamw(
        [_pack_rows([conv_mine[i]], CONVW_SHARD_ROWS, LANES) for i in range(N_DEV)], _pack_rows([conv_w], CONVW_SHARD_ROWS, LANES),
        _pack_rows([m["conv_w"]], CONVW_SHARD_ROWS, LANES), _pack_rows([v["conv_w"]], CONVW_SHARD_ROWS, LANES), "adamw_conv_w")
    for dst, src in ((out_g, g_c), (out_d, d_c), (out_m, m_c), (out_v, v_c)):
        dst["conv_w"] = src.reshape(-1)[:conv_w.size].reshape(conv_w.shape)

    landed.update({(k, 0): g for k, g in zip(MIX, _direct_wait(*mix0_grads[:4], v_c, True, "scatter_l0_mix_wait"))})
    update(MIX)

    loss = lax.psum(loss_local, ("x", "y", "c"))
    return (loss, dx.reshape(x.shape), *[out_g[k] for k in WEIGHTS], *[out_d[k] for k in WEIGHTS],
            *[out_m[k] for k in WEIGHTS], *[out_v[k] for k in WEIGHTS])
```

```python
import functools
import math

import jax
import jax.numpy as jnp
import numpy as np
from jax import lax
from jax.experimental import pallas as pl
from jax.experimental.pallas import tpu as pltpu

f32 = jnp.float32
bf16 = jnp.bfloat16

D_MODEL = 1024
SEQ = 2048
DEPTH = 2
HEAD_DIM = 64
N_ATTN_HEADS = 8
N_KV_HEADS = 2
ATTN_WIDTH = 512
KV_WIDTH = 128
ROPE_DIM = 16
ROPE_THETA = 500000.0
DILATIONS = (1, 4, 16)
ATTN_BLOCK = 128
SSM_HEADS = 16
SSM_INNER = 1024
SSM_GROUPS = 2
D_STATE = 128
CONV_WIDTH = 4
CHUNK = 128
CONV_CH = 1536
MIX_WIDTH = 1536
QKV_WIDTH = ATTN_WIDTH + 2 * KV_WIDTH
Z_OFF = 768
XBC_OFF = 1792
DT_OFF = 3328
IN_PROJ = 3344
IN_PROJ_PAD = 3456
FFN_HIDDEN = 2816
EPS = 1e-5
N_DEV = 8
ADAM_LR = 0.001
ADAM_B1 = 0.9
ADAM_B2 = 0.999
ADAM_EPS = 1e-08
ADAM_WD = 0.01
ADAM_STEP = 10

LANES = 128
SUBLANES = 8
VMEM_LIMIT = 56 * 1024 * 1024

MESH = pl.DeviceIdType.MESH
ANY = pl.BlockSpec(memory_space=pl.ANY)


def _cparams(sem, vmem=None):
    return pltpu.CompilerParams(dimension_semantics=sem, vmem_limit_bytes=vmem or VMEM_LIMIT)


def _sigmoid(x):
    return 1.0 / (1.0 + jnp.exp(-x))


def _silu(x):
    return x * _sigmoid(x)


def _dsilu(x):
    s = _sigmoid(x)
    return s * (1.0 + x * (1.0 - s))


def _silu_and_grad(x):
    s = _sigmoid(x)
    return x * s, s * (1.0 + x * (1.0 - s))


def _softplus(x):
    return jnp.maximum(x, 0.0) + jnp.log(1.0 + jnp.exp(-jnp.abs(x)))


def _dot(a, b, dims, precision=None):
    return lax.dot_general(a, b, (dims, ((), ())), preferred_element_type=f32, precision=precision)


def _nn(a, b, precision=None):
    return _dot(a, b, ((1,), (0,)), precision)


def _nt(a, b):
    return _dot(a, b, ((1,), (1,)))


def _tn(a, b):
    return _dot(a, b, ((0,), (0,)))


def _rowsum8(t):
    n, w = t.shape
    return jnp.sum(t.reshape(n // SUBLANES, SUBLANES, w), axis=0)


def _matmul(a, b, *, mode, n_out=None, b_off=0, a_koff=0, b_koff=0, k_len=None, add=None, out_dtype=f32, tm=2048, tn=512, tk=1024, name):
    if mode == "tn":
        kdim_a, m = a.shape
    else:
        m, kdim_a = a.shape
    kk = k_len if k_len is not None else kdim_a
    n = n_out if n_out is not None else (b.shape[0] if mode == "nt" else b.shape[1])
    tm, tn, tk = min(tm, m), min(tn, n), min(tk, kk)
    assert m % tm == 0 and n % tn == 0 and kk % tk == 0, (name, m, n, kk, tm, tn, tk)
    nk = kk // tk
    if mode == "nn":
        a_spec = pl.BlockSpec((tm, tk), lambda i, j, k: (i, k + a_koff))
        b_spec = pl.BlockSpec((tk, tn), lambda i, j, k: (k + b_koff, j + b_off))
        dims = ((1,), (0,))
    elif mode == "nt":
        a_spec = pl.BlockSpec((tm, tk), lambda i, j, k: (i, k + a_koff))
        b_spec = pl.BlockSpec((tn, tk), lambda i, j, k: (j + b_off, k + b_koff))
        dims = ((1,), (1,))
    else:
        a_spec = pl.BlockSpec((tk, tm), lambda i, j, k: (k + a_koff, i))
        b_spec = pl.BlockSpec((tk, tn), lambda i, j, k: (k + b_koff, j + b_off))
        dims = ((0,), (0,))
    o_spec = pl.BlockSpec((tm, tn), lambda i, j, k: (i, j))
    has_add = add is not None

    def body(*refs):
        if has_add:
            a_ref, b_ref, add_ref, o_ref, acc_ref = refs
        else:
            a_ref, b_ref, o_ref, acc_ref = refs
        k = pl.program_id(2)
        part = _dot(a_ref[...].astype(bf16), b_ref[...].astype(bf16), dims)

        @pl.when(k == 0)
        def _():
            acc_ref[...] = part

        @pl.when(k > 0)
        def _():
            acc_ref[...] += part

        @pl.when(k == nk - 1)
        def _():
            r = acc_ref[...]
            if has_add:
                r = r + add_ref[...]
            o_ref[...] = r.astype(out_dtype)

    in_specs = [a_spec, b_spec] + ([o_spec] if has_add else [])
    args = (a, b) + ((add,) if has_add else ())
    return pl.pallas_call(
        body, name=name, grid=(m // tm, n // tn, nk), in_specs=in_specs, out_specs=o_spec,
        out_shape=jax.ShapeDtypeStruct((m, n), out_dtype), scratch_shapes=[pltpu.VMEM((tm, tn), f32)],
        compiler_params=_cparams(("parallel", "parallel", "arbitrary")),
    )(*args)


def _out_proj(o, yn, w_out, h, name, tm=2048, tn=512):
    m, kb = o.shape
    n = w_out.shape[1]
    n_y = yn.shape[1] // kb
    assert yn.shape[1] % kb == 0 and w_out.shape[0] == kb * (1 + n_y)

    def body(*refs):
        o_ref, y_refs, w_refs, h_ref, out_ref = refs[0], refs[1:1 + n_y], refs[1 + n_y:2 + 2 * n_y], refs[-2], refs[-1]
        acc = h_ref[...] + _nn(o_ref[...].astype(bf16), w_refs[0][...])
        for y_ref, w_ref in zip(y_refs, w_refs[1:]):
            acc = acc + _nn(y_ref[...], w_ref[...])
        out_ref[...] = acc

    res = pl.BlockSpec((tm, tn), lambda i, j: (i, j))

    def a_blk(c):
        return pl.BlockSpec((tm, kb), lambda i, j: (i, c))

    def w_blk(r):
        return pl.BlockSpec((kb, tn), lambda i, j: (r, j))

    return pl.pallas_call(
        body, name=name, grid=(m // tm, n // tn),
        in_specs=[a_blk(0)] + [a_blk(c) for c in range(n_y)] + [w_blk(r) for r in range(1 + n_y)] + [res],
        out_specs=res, out_shape=jax.ShapeDtypeStruct((m, n), f32), compiler_params=_cparams(("parallel", "parallel")),
    )(o, *[yn] * n_y, *[w_out] * (1 + n_y), h)


def _swiglu_fwd(hn, w_gate, w_up, name, tm=2048, tn=256):
    m, k = hn.shape
    n = w_gate.shape[1]

    def body(a_ref, wg_ref, wu_ref, g_ref, u_ref, act_ref):
        a = a_ref[...]
        g = _nn(a, wg_ref[...])
        u = _nn(a, wu_ref[...])
        sg, dsg = _silu_and_grad(g)
        g_ref[...] = (u * dsg).astype(bf16)
        u_ref[...] = sg.astype(bf16)
        act_ref[...] = (sg * u).astype(bf16)

    a_spec = pl.BlockSpec((tm, k), lambda i, j: (i, 0))
    w_spec = pl.BlockSpec((k, tn), lambda i, j: (0, j))
    o_spec = pl.BlockSpec((tm, tn), lambda i, j: (i, j))
    return pl.pallas_call(
        body, name=name, grid=(m // tm, n // tn), in_specs=[a_spec, w_spec, w_spec], out_specs=[o_spec, o_spec, o_spec],
        out_shape=[jax.ShapeDtypeStruct((m, n), bf16)] * 3,
        compiler_params=_cparams(("parallel", "parallel")),
    )(hn, w_gate, w_up)


def _swiglu_bwd(dh, w_down, g, u, name, tm=2048, tn=256):
    m, k = dh.shape
    n = w_down.shape[0]

    def body(a_ref, w_ref, g_ref, u_ref, dg_ref, du_ref):
        dact = _nt(a_ref[...].astype(bf16), w_ref[...])
        dg_ref[...] = (dact * g_ref[...].astype(f32)).astype(bf16)
        du_ref[...] = (dact * u_ref[...].astype(f32)).astype(bf16)

    a_spec = pl.BlockSpec((tm, k), lambda i, j: (i, 0))
    w_spec = pl.BlockSpec((tn, k), lambda i, j: (j, 0))
    o_spec = pl.BlockSpec((tm, tn), lambda i, j: (i, j))
    return pl.pallas_call(
        body, name=name, grid=(m // tm, n // tn), in_specs=[a_spec, w_spec, o_spec, o_spec], out_specs=[o_spec, o_spec],
        out_shape=[jax.ShapeDtypeStruct((m, n), bf16), jax.ShapeDtypeStruct((m, n), bf16)],
        compiler_params=_cparams(("parallel", "parallel")),
    )(dh, w_down, g, u)


def _rmsnorm_fwd(h, w, name, tm=512):
    m, d = h.shape

    def body(h_ref, w_ref, o_ref):
        x = h_ref[...]
        r = lax.rsqrt(jnp.mean(x * x, axis=-1, keepdims=True) + EPS)
        o_ref[...] = (x * r * w_ref[...]).astype(bf16)

    return pl.pallas_call(
        body, name=name, grid=(m // tm,),
        in_specs=[pl.BlockSpec((tm, d), lambda i: (i, 0)), pl.BlockSpec((1, d), lambda i: (0, 0))],
        out_specs=pl.BlockSpec((tm, d), lambda i: (i, 0)), out_shape=jax.ShapeDtypeStruct((m, d), bf16),
        compiler_params=_cparams(("parallel",)),
    )(h, w)


def _rmsnorm_bwd(dhn, h, w, dres, name, tm=512):
    m, d = h.shape

    def body(dhn_ref, h_ref, w_ref, dres_ref, dh_ref, dw_ref):
        x = h_ref[...]
        r = lax.rsqrt(jnp.mean(x * x, axis=-1, keepdims=True) + EPS)
        xhat = x * r
        dy = dhn_ref[...]
        gw = dy * w_ref[...]
        dh_ref[...] = dres_ref[...] + r * (gw - xhat * jnp.mean(gw * xhat, axis=-1, keepdims=True))
        part = _rowsum8(dy * xhat)

        @pl.when(pl.program_id(0) == 0)
        def _():
            dw_ref[...] = part

        @pl.when(pl.program_id(0) > 0)
        def _():
            dw_ref[...] += part

    row = pl.BlockSpec((tm, d), lambda i: (i, 0))
    return pl.pallas_call(
        body, name=name, grid=(m // tm,),
        in_specs=[row, row, pl.BlockSpec((1, d), lambda i: (0, 0)), row],
        out_specs=[row, pl.BlockSpec((SUBLANES, d), lambda i: (0, 0))],
        out_shape=[jax.ShapeDtypeStruct((m, d), f32), jax.ShapeDtypeStruct((SUBLANES, d), f32)],
        compiler_params=_cparams(("arbitrary",)),
    )(dhn, h, w, dres)


def _nt_norm_bwd(pairs, h, w, dres, name, tm=1024, tk=704):
    m, d = h.shape
    steps = [p[0].shape[1] // tk for p in pairs]
    assert all(p[0].shape[1] % tk == 0 for p in pairs), (name, tk)
    starts = [sum(steps[:i]) for i in range(len(pairs))]
    nk = sum(steps)
    n_p = len(pairs)

    def body(*refs):
        ab = refs[:2 * n_p]
        h_ref, w_ref, dres_ref, dh_ref, dw_ref, acc_ref = refs[2 * n_p:]
        i, k = pl.program_id(0), pl.program_id(1)

        @pl.when(k == 0)
        def _():
            acc_ref[...] = jnp.zeros_like(acc_ref)

        for p in range(n_p):
            @pl.when((k >= starts[p]) & (k < starts[p] + steps[p]))
            def _(p=p):
                acc_ref[...] += _nt(ab[2 * p][...], ab[2 * p + 1][...])

        @pl.when(k == nk - 1)
        def _():
            x = h_ref[...]
            r = lax.rsqrt(jnp.mean(x * x, axis=-1, keepdims=True) + EPS)
            xhat = x * r
            dy = acc_ref[...]
            gw = dy * w_ref[...]
            dh_ref[...] = dres_ref[...] + r * (gw - xhat * jnp.mean(gw * xhat, axis=-1, keepdims=True))
            part = _rowsum8(dy * xhat)

            @pl.when(i == 0)
            def _():
                dw_ref[...] = part

            @pl.when(i > 0)
            def _():
                dw_ref[...] += part

    def clamp(k, p):
        return jnp.clip(k - starts[p], 0, steps[p] - 1)

    in_specs = []
    for p in range(n_p):
        in_specs += [pl.BlockSpec((tm, tk), lambda i, k, p=p: (i, clamp(k, p))), pl.BlockSpec((d, tk), lambda i, k, p=p: (0, clamp(k, p)))]
    row = pl.BlockSpec((tm, d), lambda i, k: (i, 0))
    in_specs += [row, pl.BlockSpec((1, d), lambda i, k: (0, 0)), row]
    return pl.pallas_call(
        body, name=name, grid=(m // tm, nk), in_specs=in_specs,
        out_specs=[row, pl.BlockSpec((SUBLANES, d), lambda i, k: (0, 0))],
        out_shape=[jax.ShapeDtypeStruct((m, d), f32), jax.ShapeDtypeStruct((SUBLANES, d), f32)],
        scratch_shapes=[pltpu.VMEM((tm, d), f32)], compiler_params=_cparams(("arbitrary", "arbitrary")),
    )(*[t for p in pairs for t in p], h, w, dres)


def _final_loss(h, w, target, name, tm=512):
    m, d = h.shape

    def body(h_ref, w_ref, t_ref, dh_ref, loss_ref, dw_ref):
        x = h_ref[...]
        r = lax.rsqrt(jnp.mean(x * x, axis=-1, keepdims=True) + EPS)
        xhat = x * r
        ww = w_ref[...]
        err = xhat * ww - t_ref[...]
        dy = err * (1.0 / d)
        gw = dy * ww
        dh_ref[...] = r * (gw - xhat * jnp.mean(gw * xhat, axis=-1, keepdims=True))
        lpart = _rowsum8(err * err) * (0.5 / d)
        wpart = _rowsum8(dy * xhat)

        @pl.when(pl.program_id(0) == 0)
        def _():
            loss_ref[...] = lpart
            dw_ref[...] = wpart

        @pl.when(pl.program_id(0) > 0)
        def _():
            loss_ref[...] += lpart
            dw_ref[...] += wpart

    row = pl.BlockSpec((tm, d), lambda i: (i, 0))
    acc = pl.BlockSpec((SUBLANES, d), lambda i: (0, 0))
    return pl.pallas_call(
        body, name=name, grid=(m // tm,),
        in_specs=[row, pl.BlockSpec((1, d), lambda i: (0, 0)), row], out_specs=[row, acc, acc],
        out_shape=[jax.ShapeDtypeStruct((m, d), f32), jax.ShapeDtypeStruct((SUBLANES, d), f32), jax.ShapeDtypeStruct((SUBLANES, d), f32)],
        compiler_params=_cparams(("arbitrary",)),
    )(h, w, target)


def _lane_tables():
    f = np.arange(LANES) % HEAD_DIM
    inv = ROPE_THETA ** (-jnp.arange(0, ROPE_DIM, 2, dtype=f32) / ROPE_DIM)
    invf = jnp.where(f < ROPE_DIM, inv[f % (ROPE_DIM // 2)], 0.0).astype(f32)
    return invf.reshape(1, LANES)


def _rope_tables(pos_col, name):
    t = pos_col.shape[0]
    tm = SEQ

    def body(p_ref, f_ref, c_ref, s1_ref, s2_ref):
        ang = p_ref[...].astype(f32) * f_ref[...]
        co, si = jnp.cos(ang), jnp.sin(ang)
        f = lax.broadcasted_iota(jnp.int32, (tm, LANES), 1) % HEAD_DIM
        c_ref[...] = jnp.where(f < ROPE_DIM, co, 1.0)
        s1_ref[...] = jnp.where(f < ROPE_DIM // 2, -si, 0.0)
        s2_ref[...] = jnp.where((f >= ROPE_DIM // 2) & (f < ROPE_DIM), si, 0.0)

    row = pl.BlockSpec((tm, LANES), lambda i: (i, 0))
    return pl.pallas_call(
        body, name=name, grid=(t // tm,),
        in_specs=[pl.BlockSpec((tm, 1), lambda i: (i, 0)), pl.BlockSpec((1, LANES), lambda i: (0, 0))],
        out_specs=[row, row, row], out_shape=[jax.ShapeDtypeStruct((t, LANES), f32)] * 3,
        compiler_params=_cparams(("parallel",)),
    )(pos_col, _lane_tables())


def _rot(x, c, s1, s2):
    return x * c + pltpu.roll(x, LANES - ROPE_DIM // 2, 1) * s1 + pltpu.roll(x, ROPE_DIM // 2, 1) * s2


def _rot_t(g, c, s1, s2):
    return g * c + pltpu.roll(g * s1, ROPE_DIM // 2, 1) + pltpu.roll(g * s2, LANES - ROPE_DIM // 2, 1)


def _dup_head(x, kvh, low):
    a = jnp.where(kvh == 0, x, pltpu.roll(x, HEAD_DIM, 1))
    return jnp.where(low, a, pltpu.roll(a, HEAD_DIM, 1))


def _deinterleave(src_ref, dst_ref, d, dtype):
    length = SEQ // d
    if d == 1:
        dst_ref[...] = src_ref[...].astype(dtype)
    else:
        for r in range(d):
            dst_ref[pl.ds(r * length, length), :] = src_ref[pl.ds(r, length, stride=d), :].astype(dtype)


def _interleave_store(src_ref, dst_ref, d, accumulate):
    length = SEQ // d
    if d == 1:
        if accumulate:
            dst_ref[...] += src_ref[...]
        else:
            dst_ref[...] = src_ref[...]
    else:
        for r in range(d):
            blk = src_ref[pl.ds(r * length, length), :]
            if accumulate:
                dst_ref[pl.ds(r, length, stride=d), :] = dst_ref[pl.ds(r, length, stride=d), :] + blk
            else:
                dst_ref[pl.ds(r, length, stride=d), :] = blk


def _attn_masks():
    qi = lax.broadcasted_iota(jnp.int32, (ATTN_BLOCK, ATTN_BLOCK), 0)
    ki = lax.broadcasted_iota(jnp.int32, (ATTN_BLOCK, ATTN_BLOCK), 1)
    low = lax.broadcasted_iota(jnp.int32, (ATTN_BLOCK, LANES), 1) < HEAD_DIM
    return ki <= qi, ki >= qi, low


NEG_INF = float("-inf")
ATTN_UNROLL = 4


N_BRANCH = len(DILATIONS)


def _attn_prep(qkv, tabs, name):
    t = qkv.shape[0]
    nb = t // SEQ
    n_j = ATTN_WIDTH // LANES

    def q_body(q_ref, c_ref, s1_ref, s2_ref, out_ref, xr):
        xr[...] = _rot(q_ref[...], c_ref[...], s1_ref[...], s2_ref[...]) * (HEAD_DIM ** -0.5)
        for bi, d in enumerate(DILATIONS):
            _deinterleave(xr, out_ref.at[bi], d, bf16)

    def kv_body(x_ref, c_ref, s1_ref, s2_ref, out_ref, xr):
        lowfull = lax.broadcasted_iota(jnp.int32, (SEQ, LANES), 1) < HEAD_DIM
        x = x_ref[...]
        x = jnp.where(pl.program_id(1) == 0, _rot(x, c_ref[...], s1_ref[...], s2_ref[...]), x)
        for kvh in range(N_KV_HEADS):
            xr[...] = _dup_head(x, kvh, lowfull)
            for bi, d in enumerate(DILATIONS):
                length = SEQ // d
                for r in range(d):
                    rows = xr[...] if d == 1 else xr[pl.ds(r, length, stride=d), :]
                    out_ref[0, bi, pl.ds(r * length, length), kvh * LANES:(kvh + 1) * LANES] = rows.astype(bf16)

    tab = pl.BlockSpec((SEQ, LANES), lambda b, j: (b, 0))
    q = pl.pallas_call(
        q_body, name=name + "_q", grid=(nb, n_j),
        in_specs=[pl.BlockSpec((SEQ, LANES), lambda b, j: (b, j)), tab, tab, tab],
        out_specs=pl.BlockSpec((N_BRANCH, SEQ, LANES), lambda b, j: (0, b, j)),
        out_shape=jax.ShapeDtypeStruct((N_BRANCH, t, ATTN_WIDTH), bf16), scratch_shapes=[pltpu.VMEM((SEQ, LANES), f32)],
        compiler_params=_cparams(("parallel", "parallel")),
    )(qkv, *tabs)
    kv = pl.pallas_call(
        kv_body, name=name + "_kv", grid=(nb, 2),
        in_specs=[pl.BlockSpec((SEQ, LANES), lambda b, j: (b, n_j + j)), tab, tab, tab],
        out_specs=pl.BlockSpec((1, N_BRANCH, SEQ, N_KV_HEADS * LANES), lambda b, j: (j, 0, b, 0)),
        out_shape=jax.ShapeDtypeStruct((2, N_BRANCH, t, N_KV_HEADS * LANES), bf16), scratch_shapes=[pltpu.VMEM((SEQ, LANES), f32)],
        compiler_params=_cparams(("parallel", "parallel")),
    )(qkv, *tabs)
    return q, kv


def _attn_fwd(prep, name):
    q_all, kv_all = prep
    t = q_all.shape[1]
    nb = t // SEQ
    n_blk = SEQ // ATTN_BLOCK

    def body(q_ref, k_ref, v_ref, o_ref, lse_ref, ob, lb, o0, o1, o2, l0, l1, l2, ss):
        cur_ok, prev_ok, low = _attn_masks()
        onat, lnat = (o0, o1, o2), (l0, l1, l2)
        for bi, d in enumerate(DILATIONS):
            qd, kd, vd = q_ref.at[bi], k_ref.at[0, bi], v_ref.at[0, bi]
            per_res = n_blk // d
            use_prev = per_res > 1

            def scores(n, carry):
                start = pl.multiple_of(n * ATTN_BLOCK, ATTN_BLOCK)
                has_prev = (n % per_res) != 0
                pstart = pl.multiple_of(jnp.maximum(n - 1, 0) * ATTN_BLOCK, ATTN_BLOCK)
                qb = qd[pl.ds(start, ATTN_BLOCK), :]
                kc = kd[pl.ds(start, ATTN_BLOCK), :]
                if use_prev:
                    kp = kd[pl.ds(pstart, ATTN_BLOCK), :]
                for a in range(2):
                    qa = jnp.where(low if a == 0 else ~low, qb, jnp.zeros_like(qb))
                    ss[2 * n + a, :, 0:ATTN_BLOCK] = jnp.where(cur_ok, _nt(qa, kc), NEG_INF)
                    if use_prev:
                        ss[2 * n + a, :, ATTN_BLOCK:2 * ATTN_BLOCK] = jnp.where(prev_ok & has_prev, _nt(qa, kp), NEG_INF)
                return carry

            def softmax_pv(n, carry):
                start = pl.multiple_of(n * ATTN_BLOCK, ATTN_BLOCK)
                pstart = pl.multiple_of(jnp.maximum(n - 1, 0) * ATTN_BLOCK, ATTN_BLOCK)
                vc = vd[pl.ds(start, ATTN_BLOCK), :]
                if use_prev:
                    vp = vd[pl.ds(pstart, ATTN_BLOCK), :]
                outs, lses = [], []
                for a in range(2):
                    sc = ss[2 * n + a, :, 0:ATTN_BLOCK]
                    if use_prev:
                        sp = ss[2 * n + a, :, ATTN_BLOCK:2 * ATTN_BLOCK]
                        m = jnp.max(jnp.maximum(sc, sp), axis=1, keepdims=True)
                        pc, pp = jnp.exp(sc - m), jnp.exp(sp - m)
                        den = jnp.sum(pc + pp, axis=1, keepdims=True)
                        acc = _nn(pc.astype(bf16), vc) + _nn(pp.astype(bf16), vp)
                    else:
                        m = jnp.max(sc, axis=1, keepdims=True)
                        pc = jnp.exp(sc - m)
                        den = jnp.sum(pc, axis=1, keepdims=True)
                        acc = _nn(pc.astype(bf16), vc)
                    outs.append(acc * (1.0 / den))
                    lses.append(m + jnp.log(den))
                ob[pl.ds(start, ATTN_BLOCK), :] = jnp.where(low, outs[0], outs[1])
                lb[pl.ds(start, ATTN_BLOCK), :] = jnp.where(low, lses[0], lses[1])
                return carry

            lax.fori_loop(0, n_blk, scores, 0, unroll=ATTN_UNROLL)
            lax.fori_loop(0, n_blk, softmax_pv, 0, unroll=ATTN_UNROLL)
            _interleave_store(ob, onat[bi], d, False)
            _interleave_store(lb, lnat[bi], d, False)
        la, lbb, lc = l0[...], l1[...], l2[...]
        lm = jnp.maximum(jnp.maximum(la, lbb), lc)
        wa, wb, wc = jnp.exp(la - lm), jnp.exp(lbb - lm), jnp.exp(lc - lm)
        ws = wa + wb + wc
        o_ref[...] = (wa * o0[...] + wb * o1[...] + wc * o2[...]) / ws
        lse_ref[...] = lm + jnp.log(ws)

    def col(jj):
        return pl.BlockSpec((SEQ, LANES), lambda b, j: (b, jj if jj is not None else j))

    fs = pltpu.VMEM((SEQ, LANES), f32)
    return pl.pallas_call(
        body, name=name, grid=(nb, ATTN_WIDTH // LANES),
        in_specs=[pl.BlockSpec((N_BRANCH, SEQ, LANES), lambda b, j: (0, b, j)),
                  pl.BlockSpec((1, N_BRANCH, SEQ, LANES), lambda b, j: (0, 0, b, j // 2)),
                  pl.BlockSpec((1, N_BRANCH, SEQ, LANES), lambda b, j: (1, 0, b, j // 2))],
        out_specs=[col(None), col(None)],
        out_shape=[jax.ShapeDtypeStruct((t, ATTN_WIDTH), f32), jax.ShapeDtypeStruct((t, ATTN_WIDTH), f32)],
        scratch_shapes=[fs, fs, fs, fs, fs, fs, fs, fs, pltpu.VMEM((2 * n_blk, ATTN_BLOCK, 2 * ATTN_BLOCK), f32)],
        compiler_params=_cparams(("parallel", "parallel")),
    )(q_all, kv_all, kv_all)


def _attn_bwd(prep, tabs, o, lse, do, name):
    q_all, kv_all = prep
    t = q_all.shape[1]
    nb = t // SEQ
    n_blk = SEQ // ATTN_BLOCK
    n_j = ATTN_WIDTH // LANES

    def body(q_ref, k_ref, v_ref, c_ref, s1_ref, s2_ref, o_ref, lse_ref, do_ref, dq_ref, dk_ref, dv_ref,
             dl, dod, lsd, dld, dqd, dkd, dvd, dqa, dka, dva, pb, dsb, dk_acc, dv_acc):
        j = pl.program_id(1)
        pb[2 * n_blk:2 * n_blk + 2] = jnp.zeros((2, ATTN_BLOCK, 2 * ATTN_BLOCK), bf16)
        dsb[2 * n_blk:2 * n_blk + 2] = jnp.zeros((2, ATTN_BLOCK, 2 * ATTN_BLOCK), bf16)
        kvh = j // 2
        cur_ok, prev_ok, low = _attn_masks()
        lowfull = lax.broadcasted_iota(jnp.int32, (SEQ, LANES), 1) < HEAD_DIM
        c, s1, s2 = c_ref[...], s1_ref[...], s2_ref[...]
        prod = do_ref[...] * o_ref[...]
        d_lo = jnp.sum(jnp.where(lowfull, prod, 0.0), axis=1, keepdims=True)
        d_hi = jnp.sum(jnp.where(lowfull, 0.0, prod), axis=1, keepdims=True)
        dl[...] = jnp.where(lowfull, d_lo, d_hi)
        dqa[...] = jnp.zeros_like(dqa)
        dka[...] = jnp.zeros_like(dka)
        dva[...] = jnp.zeros_like(dva)
        for bi, d in enumerate(DILATIONS):
            qd, kd, vd = q_ref.at[bi], k_ref.at[0, bi], v_ref.at[0, bi]
            _deinterleave(do_ref, dod, d, bf16)
            _deinterleave(lse_ref, lsd, d, f32)
            _deinterleave(dl, dld, d, f32)
            per_res = n_blk // d
            use_prev = per_res > 1
            curl, prevl = slice(0, ATTN_BLOCK), slice(ATTN_BLOCK, 2 * ATTN_BLOCK)

            def halves(x):
                zero = jnp.zeros_like(x)
                return jnp.where(low, x, zero), jnp.where(low, zero, x)

            def probs(n, carry):
                start = pl.multiple_of(n * ATTN_BLOCK, ATTN_BLOCK)
                has_prev = (n % per_res) != 0
                pstart = pl.multiple_of(jnp.maximum(n - 1, 0) * ATTN_BLOCK, ATTN_BLOCK)
                cur, prev = pl.ds(start, ATTN_BLOCK), pl.ds(pstart, ATTN_BLOCK)
                qas, doas = halves(qd[cur, :]), halves(dod[cur, :])
                kc, vc = kd[cur, :], vd[cur, :]
                if use_prev:
                    kp, vp = kd[prev, :], vd[prev, :]
                lsb, dlb = lsd[cur, :], dld[cur, :]
                for a in range(2):
                    ls = lsb[:, a * HEAD_DIM:a * HEAD_DIM + 1]
                    de = dlb[:, a * HEAD_DIM:a * HEAD_DIM + 1]
                    pc = jnp.exp(jnp.where(cur_ok, _nt(qas[a], kc), NEG_INF) - ls)
                    pb[2 * n + a, :, curl] = pc.astype(bf16)
                    dsb[2 * n + a, :, curl] = (pc * (_nt(doas[a], vc) - de)).astype(bf16)
                    if use_prev:
                        pp = jnp.exp(jnp.where(prev_ok & has_prev, _nt(qas[a], kp), NEG_INF) - ls)
                        pb[2 * n + a, :, prevl] = pp.astype(bf16)
                        dsb[2 * n + a, :, prevl] = (pp * (_nt(doas[a], vp) - de)).astype(bf16)
                return carry

            def grads(n, carry):
                start = pl.multiple_of(n * ATTN_BLOCK, ATTN_BLOCK)
                pstart = pl.multiple_of(jnp.maximum(n - 1, 0) * ATTN_BLOCK, ATTN_BLOCK)
                nstart = pl.multiple_of(jnp.minimum(n + 1, n_blk - 1) * ATTN_BLOCK, ATTN_BLOCK)
                cur, prev, nxt = pl.ds(start, ATTN_BLOCK), pl.ds(pstart, ATTN_BLOCK), pl.ds(nstart, ATTN_BLOCK)
                kc = kd[cur, :]
                dqs = [_nn(dsb[2 * n + a, :, curl], kc) for a in range(2)]
                q_rows, do_rows = list(halves(qd[cur, :])), list(halves(dod[cur, :]))
                ds_rows, p_rows = [dsb[2 * n + a, :, curl] for a in range(2)], [pb[2 * n + a, :, curl] for a in range(2)]
                if use_prev:
                    kp = kd[prev, :]
                    dqs = [dqs[a] + _nn(dsb[2 * n + a, :, prevl], kp) for a in range(2)]
                    q_rows += list(halves(qd[nxt, :]))
                    do_rows += list(halves(dod[nxt, :]))
                    ds_rows += [dsb[2 * n + 2 + a, :, prevl] for a in range(2)]
                    p_rows += [pb[2 * n + 2 + a, :, prevl] for a in range(2)]
                dqd[cur, :] = jnp.where(low, dqs[0], dqs[1])
                dkd[cur, :] = _tn(jnp.concatenate(ds_rows, axis=0), jnp.concatenate(q_rows, axis=0))
                dvd[cur, :] = _tn(jnp.concatenate(p_rows, axis=0), jnp.concatenate(do_rows, axis=0))
                return carry

            lax.fori_loop(0, n_blk, probs, 0, unroll=ATTN_UNROLL)
            lax.fori_loop(0, n_blk, grads, 0, unroll=ATTN_UNROLL)
            _interleave_store(dqd, dqa, d, True)
            _interleave_store(dkd, dka, d, True)
            _interleave_store(dvd, dva, d, True)
        dq_ref[...] = _rot_t(dqa[...] * (HEAD_DIM ** -0.5), c, s1, s2).astype(bf16)
        dkf = dka[...]
        dkf = _rot_t(dkf + pltpu.roll(dkf, HEAD_DIM, 1), c, s1, s2)
        dvf = dva[...]
        dvf = dvf + pltpu.roll(dvf, HEAD_DIM, 1)
        mine = (lax.broadcasted_iota(jnp.int32, (SEQ, LANES), 1) // HEAD_DIM) == kvh
        dkc_, dvc_ = jnp.where(mine, dkf, 0.0), jnp.where(mine, dvf, 0.0)

        @pl.when(j == 0)
        def _():
            dk_acc[...] = dkc_
            dv_acc[...] = dvc_

        @pl.when(j > 0)
        def _():
            dk_acc[...] += dkc_
            dv_acc[...] += dvc_

        @pl.when(j == n_j - 1)
        def _():
            dk_ref[...] = dk_acc[...].astype(bf16)
            dv_ref[...] = dv_acc[...].astype(bf16)

    def col(jj):
        return pl.BlockSpec((SEQ, LANES), lambda b, j: (b, jj if jj is not None else j))

    tab = pl.BlockSpec((SEQ, LANES), lambda b, j: (b, 0))
    fs = pltpu.VMEM((SEQ, LANES), f32)
    hs = pltpu.VMEM((SEQ, LANES), bf16)
    return pl.pallas_call(
        body, name=name, grid=(nb, n_j),
        in_specs=[pl.BlockSpec((N_BRANCH, SEQ, LANES), lambda b, j: (0, b, j)),
                  pl.BlockSpec((1, N_BRANCH, SEQ, LANES), lambda b, j: (0, 0, b, j // 2)),
                  pl.BlockSpec((1, N_BRANCH, SEQ, LANES), lambda b, j: (1, 0, b, j // 2)),
                  tab, tab, tab, col(None), col(None), col(None)],
        out_specs=[col(None), tab, tab],
        out_shape=[jax.ShapeDtypeStruct((t, ATTN_WIDTH), bf16), jax.ShapeDtypeStruct((t, LANES), bf16), jax.ShapeDtypeStruct((t, LANES), bf16)],
        scratch_shapes=[fs, hs, fs, fs, fs, fs, fs, fs, fs, fs,
                        pltpu.VMEM((2 * n_blk + 2, ATTN_BLOCK, 2 * ATTN_BLOCK), bf16), pltpu.VMEM((2 * n_blk + 2, ATTN_BLOCK, 2 * ATTN_BLOCK), bf16), fs, fs],
        compiler_params=_cparams(("parallel", "arbitrary")),
    )(q_all, kv_all, kv_all, *tabs, o, lse, do)


def _conv_pre(x, w_ref, b_ref, row):
    shifted = [x] + [jnp.where(row >= s, pltpu.roll(x, s, 0), 0.0) for s in range(1, CONV_WIDTH)]
    pre = b_ref[...] + w_ref[CONV_WIDTH - 1:CONV_WIDTH, :] * x
    for s in range(1, CONV_WIDTH):
        pre = pre + w_ref[CONV_WIDTH - 1 - s:CONV_WIDTH - s, :] * shifted[s]
    return pre, shifted


def _conv_fwd(x, w, b, name, tc=512):
    t, ch = x.shape

    def body(x_ref, w_ref, b_ref, o_ref):
        row = lax.broadcasted_iota(jnp.int32, (SEQ, tc), 0)
        pre, _ = _conv_pre(x_ref[...], w_ref, b_ref, row)
        o_ref[...] = _silu(pre)

    xs = pl.BlockSpec((SEQ, tc), lambda i, j: (i, j))
    return pl.pallas_call(
        body, name=name, grid=(t // SEQ, ch // tc),
        in_specs=[xs, pl.BlockSpec((CONV_WIDTH, tc), lambda i, j: (0, j)), pl.BlockSpec((1, tc), lambda i, j: (0, j))],
        out_specs=xs, out_shape=jax.ShapeDtypeStruct((t, ch), f32),
        compiler_params=_cparams(("parallel", "parallel")),
    )(x, w, b)


def _conv_bwd(x, w, b, dact, name, tc=512):
    t, ch = x.shape

    def body(x_ref, w_ref, b_ref, d_ref, dx_ref, dw_ref, db_ref):
        row = lax.broadcasted_iota(jnp.int32, (SEQ, tc), 0)
        pre, shifted = _conv_pre(x_ref[...], w_ref, b_ref, row)
        dpre = d_ref[...] * _dsilu(pre)
        dx = w_ref[CONV_WIDTH - 1:CONV_WIDTH, :] * dpre
        for s in range(1, CONV_WIDTH):
            dx = dx + w_ref[CONV_WIDTH - 1 - s:CONV_WIDTH - s, :] * jnp.where(row < SEQ - s, pltpu.roll(dpre, SEQ - s, 0), 0.0)
        dx_ref[...] = dx.astype(bf16)
        first = pl.program_id(1) == 0
        parts = [jnp.sum(dpre * shifted[CONV_WIDTH - 1 - k], axis=0, keepdims=True) for k in range(CONV_WIDTH)]
        dbp = jnp.sum(dpre, axis=0, keepdims=True)

        @pl.when(first)
        def _():
            for k in range(CONV_WIDTH):
                dw_ref[k:k + 1, :] = parts[k]
            db_ref[...] = dbp

        @pl.when(jnp.logical_not(first))
        def _():
            for k in range(CONV_WIDTH):
                dw_ref[k:k + 1, :] += parts[k]
            db_ref[...] += dbp

    xs = pl.BlockSpec((SEQ, tc), lambda j, i: (i, j))
    ws = pl.BlockSpec((CONV_WIDTH, tc), lambda j, i: (0, j))
    bs = pl.BlockSpec((1, tc), lambda j, i: (0, j))
    return pl.pallas_call(
        body, name=name, grid=(ch // tc, t // SEQ),
        in_specs=[xs, ws, bs, xs], out_specs=[xs, ws, bs],
        out_shape=[jax.ShapeDtypeStruct((t, ch), bf16), jax.ShapeDtypeStruct((CONV_WIDTH, ch), f32), jax.ShapeDtypeStruct((1, ch), f32)],
        compiler_params=_cparams(("parallel", "arbitrary")),
    )(x, w, b, dact)


GROUP_W = SSM_INNER // SSM_GROUPS
HEADS_PER_GROUP = SSM_HEADS // SSM_GROUPS


def _split3(x):
    hi = x.astype(bf16)
    r1 = x - hi.astype(f32)
    mid = r1.astype(bf16)
    lo = (r1 - mid.astype(f32)).astype(bf16)
    return hi, mid, lo


def _dot_exact(x, sel, dims, x_is_lhs=True):
    parts = _split3(x)
    if x_is_lhs:
        return _dot(parts[0], sel, dims) + _dot(parts[1], sel, dims) + _dot(parts[2], sel, dims)
    return _dot(sel, parts[0], dims) + _dot(sel, parts[1], dims) + _dot(sel, parts[2], dims)


def _ssd_common(xbc_ref, dt_ref, bias_ref, alog_ref):
    r = lax.broadcasted_iota(jnp.int32, (CHUNK, CHUNK), 0)
    cidx = lax.broadcasted_iota(jnp.int32, (CHUNK, CHUNK), 1)
    causal = r >= cidx
    tril = causal.astype(bf16)
    expand = (lax.broadcasted_iota(jnp.int32, (CHUNK, SSM_INNER), 0)
              == lax.broadcasted_iota(jnp.int32, (CHUNK, SSM_INNER), 1) // HEAD_DIM).astype(bf16)
    head_lane = cidx < SSM_HEADS
    dtp = dt_ref[...] + bias_ref[...]
    dt = jnp.where(head_lane, _softplus(dtp), 0.0)
    a_neg = -jnp.exp(alog_ref[...])
    a = dt * a_neg
    nn_dims = ((1,), (0,))
    cs = _dot_exact(a, tril, nn_dims, x_is_lhs=False)
    dt_e = _dot_exact(dt, expand, nn_dims)
    cs_e = _dot_exact(cs, expand, nn_dims)
    xs = xbc_ref[:, 0:SSM_INNER]
    xg = xs * dt_e
    ecs = jnp.exp(cs_e)
    cs_last = cs_e[CHUNK - 1:CHUNK, :]
    dse = jnp.exp(cs_last - cs_e)
    cde = jnp.exp(cs_last)
    return dict(r=r, cidx=cidx, causal=causal, tril=tril, expand=expand, head_lane=head_lane, dtp=dtp, dt=dt, a_neg=a_neg,
                cs=cs, cst=cs.T, dt_e=dt_e, cs_e=cs_e, xs=xs, xg=xg, ecs=ecs, dse=dse, cde=cde)


def _decay_mat(q, h):
    return jnp.exp(jnp.where(q["causal"], q["cs"][:, h:h + 1] - q["cst"][h:h + 1, :], NEG_INF))


def _gate_norm(y, z, nw, gate=None):
    y2 = y * (_silu(z) if gate is None else gate)
    outs, xhats, rs = [], [], []
    for g in range(SSM_GROUPS):
        sl = slice(g * GROUP_W, (g + 1) * GROUP_W)
        yg = y2[:, sl]
        r = lax.rsqrt(jnp.mean(yg * yg, axis=-1, keepdims=True) + EPS)
        xhats.append(yg * r)
        rs.append(r)
        outs.append(yg * r * nw[:, sl])
    return y2, outs, xhats, rs


def _ssd_fwd(xbc, z, dtp, params, name):
    t = xbc.shape[0]
    n_chunk = SEQ // CHUNK
    low = None

    def body(xbc_ref, z_ref, dt_ref, bias_ref, alog_ref, dskip_ref, nw_ref, yn_ref, y_ref, hs_ref, h_scr):
        @pl.when(pl.program_id(1) == 0)
        def _():
            h_scr[...] = jnp.zeros_like(h_scr)

        q = _ssd_common(xbc_ref, dt_ref, bias_ref, alog_ref)
        low = lax.broadcasted_iota(jnp.int32, (CHUNK, LANES), 1) < HEAD_DIM
        xgb = q["xg"].astype(bf16)
        wst = (q["xg"] * q["dse"]).astype(bf16)
        hs_ref[0] = h_scr[...]
        ys = []
        for g in range(SSM_GROUPS):
            gl = slice(g * GROUP_W, (g + 1) * GROUP_W)
            bg = xbc_ref[:, SSM_INNER + g * D_STATE:SSM_INNER + (g + 1) * D_STATE].astype(bf16)
            cg = xbc_ref[:, SSM_INNER + SSM_GROUPS * D_STATE + g * D_STATE:SSM_INNER + SSM_GROUPS * D_STATE + (g + 1) * D_STATE].astype(bf16)
            cb = _nt(cg, bg)
            hg = h_scr[g]
            yoff = _nn(cg, hg.astype(bf16)) * q["ecs"][:, gl]
            pieces = []
            for i in range(HEADS_PER_GROUP // 2):
                h0 = g * HEADS_PER_GROUP + 2 * i
                xp = xgb[:, h0 * HEAD_DIM:(h0 + 2) * HEAD_DIM]
                m0 = (cb * _decay_mat(q, h0)).astype(bf16)
                m1 = (cb * _decay_mat(q, h0 + 1)).astype(bf16)
                zero = jnp.zeros_like(xp)
                pieces.append(_nn(m0, jnp.where(low, xp, zero)) + _nn(m1, jnp.where(low, zero, xp)))
            ys.append(jnp.concatenate(pieces, axis=1) + yoff + dskip_ref[:, gl] * q["xs"][:, gl])
            h_scr[g] = hg * q["cde"][:, gl] + _tn(bg, wst[:, gl])
        y = jnp.concatenate(ys, axis=1)
        y_ref[...] = y
        _, outs, _, _ = _gate_norm(y, z_ref[...], nw_ref[...])
        yn_ref[...] = jnp.concatenate(outs, axis=1).astype(bf16)

    def rows(w):
        return pl.BlockSpec((CHUNK, w), lambda b, c: (b * n_chunk + c, 0))

    def par(w):
        return pl.BlockSpec((1, w), lambda b, c: (0, 0))

    return pl.pallas_call(
        body, name=name, grid=(t // SEQ, n_chunk),
        in_specs=[rows(CONV_CH), rows(SSM_INNER), rows(LANES), par(LANES), par(LANES), par(SSM_INNER), par(SSM_INNER)],
        out_specs=[rows(SSM_INNER), rows(SSM_INNER), pl.BlockSpec((1, SSM_GROUPS, D_STATE, GROUP_W), lambda b, c: (b * n_chunk + c, 0, 0, 0))],
        out_shape=[jax.ShapeDtypeStruct((t, SSM_INNER), bf16), jax.ShapeDtypeStruct((t, SSM_INNER), f32),
                   jax.ShapeDtypeStruct((t // CHUNK, SSM_GROUPS, D_STATE, GROUP_W), f32)],
        scratch_shapes=[pltpu.VMEM((SSM_GROUPS, D_STATE, GROUP_W), f32)],
        compiler_params=_cparams(("parallel", "arbitrary")),
    )(xbc, z, dtp, *params)


def _ssd_bwd(xbc, z, dtp, y, hs, dyn, params, name):
    t = xbc.shape[0]
    n_chunk = SEQ // CHUNK

    def body(xbc_ref, z_ref, dt_ref, y_ref, hs_ref, dyn_ref, bias_ref, alog_ref, dskip_ref, nw_ref,
             dxbc_ref, dz_ref, ddt_ref, dnw_ref, dds_ref, dal_ref, dbi_ref, dh_scr):
        @pl.when(pl.program_id(1) == 0)
        def _():
            dh_scr[...] = jnp.zeros_like(dh_scr)

        q = _ssd_common(xbc_ref, dt_ref, bias_ref, alog_ref)
        low = lax.broadcasted_iota(jnp.int32, (CHUNK, LANES), 1) < HEAD_DIM
        last_row = lax.broadcasted_iota(jnp.int32, (CHUNK, GROUP_W), 0) == CHUNK - 1
        xs, xg = q["xs"], q["xg"]
        xgb = xg.astype(bf16)
        wf = xg * q["dse"]
        wst = wf.astype(bf16)
        zz = z_ref[...]
        yy = y_ref[...]
        sz, dsz = _silu_and_grad(zz)
        y2, _, xhats, rs = _gate_norm(yy, zz, nw_ref[...], gate=sz)
        dyn_ = dyn_ref[...]
        dy2s, dnws = [], []
        for g in range(SSM_GROUPS):
            gl = slice(g * GROUP_W, (g + 1) * GROUP_W)
            gw = dyn_[:, gl] * nw_ref[:, gl]
            dy2s.append(rs[g] * (gw - xhats[g] * jnp.mean(gw * xhats[g], axis=-1, keepdims=True)))
            dnws.append(_rowsum8(dyn_[:, gl] * xhats[g]))
        dy2 = jnp.concatenate(dy2s, axis=1)
        dy = dy2 * sz
        dz_ref[...] = (dy2 * yy * dsz).astype(bf16)
        dnw_p = jnp.concatenate(dnws, axis=1)
        dds_p = _rowsum8(dy * xs)
        dyb = dy.astype(bf16)
        gfull = (dy * q["ecs"]).astype(bf16)
        dcs_c = jnp.zeros((CHUNK, CHUNK), f32)
        dcs_r = jnp.zeros((CHUNK, CHUNK), f32)
        dcs_e_parts, dxg_parts = [], []
        for g in range(SSM_GROUPS):
            gl = slice(g * GROUP_W, (g + 1) * GROUP_W)
            bsl = slice(SSM_INNER + g * D_STATE, SSM_INNER + (g + 1) * D_STATE)
            csl = slice(SSM_INNER + SSM_GROUPS * D_STATE + g * D_STATE, SSM_INNER + SSM_GROUPS * D_STATE + (g + 1) * D_STATE)
            bg = xbc_ref[:, bsl].astype(bf16)
            cg = xbc_ref[:, csl].astype(bf16)
            cb = _nt(cg, bg)
            hg = hs_ref[0, g]
            hgb = hg.astype(bf16)
            dhn = dh_scr[g]
            dhnb = dhn.astype(bf16)
            yoff = _nn(cg, hgb) * q["ecs"][:, gl]
            dw_ = _nn(bg, dhnb)
            r_e = dw_ * wf[:, gl]
            to_last = jnp.sum(r_e, axis=0, keepdims=True) + jnp.sum(dhn * hg, axis=0, keepdims=True) * q["cde"][:, gl]
            dcs_e_parts.append(dy[:, gl] * yoff - r_e + jnp.where(last_row, to_last, 0.0))
            dcb = jnp.zeros((CHUNK, CHUNK), f32)
            dxg_pairs = []
            for i in range(HEADS_PER_GROUP // 2):
                h0 = g * HEADS_PER_GROUP + 2 * i
                psl = slice(h0 * HEAD_DIM, (h0 + 2) * HEAD_DIM)
                xp = xgb[:, psl]
                dyp = dyb[:, psl]
                zero = jnp.zeros_like(dyp)
                tns = []
                for a in range(2):
                    h = h0 + a
                    lm = _decay_mat(q, h)
                    m = cb * lm
                    dm = _nt(jnp.where(low, dyp, zero) if a == 0 else jnp.where(low, zero, dyp), xp)
                    dcb = dcb + dm * lm
                    nmat = dm * m
                    dcs_c = dcs_c + jnp.where(q["cidx"] == h, jnp.sum(nmat, axis=1, keepdims=True), 0.0)
                    dcs_r = dcs_r + jnp.where(q["r"] == h, jnp.sum(nmat, axis=0, keepdims=True), 0.0)
                    tns.append(_tn(m.astype(bf16), dyp))
                dxg_pairs.append(jnp.where(low, tns[0], tns[1]))
            dxg_parts.append(jnp.concatenate(dxg_pairs, axis=1) + dw_ * q["dse"][:, gl])
            dcbb = dcb.astype(bf16)
            dxbc_ref[:, csl] = _nt(gfull[:, gl], hgb) + _nn(dcbb, bg)
            dxbc_ref[:, bsl] = _nt(wst[:, gl], dhnb) + _tn(dcbb, cg)
            dh_scr[g] = dhn * q["cde"][:, gl] + _tn(cg, gfull[:, gl])
        dxg = jnp.concatenate(dxg_parts, axis=1)
        dcs_e = jnp.concatenate(dcs_e_parts, axis=1)
        dxbc_ref[:, 0:SSM_INNER] = dskip_ref[...] * dy + dxg * q["dt_e"]
        dcs = dcs_c - dcs_r.T + _dot_exact(dcs_e, q["expand"], ((1,), (1,)))
        triu = (q["cidx"] >= q["r"]).astype(bf16)
        da = _dot_exact(dcs, triu, ((1,), (0,)), x_is_lhs=False)
        ddt = _dot_exact(dxg * xs, q["expand"], ((1,), (1,))) + da * q["a_neg"]
        ddtp = jnp.where(q["head_lane"], ddt * _sigmoid(q["dtp"]), 0.0)
        ddt_ref[...] = ddtp.astype(bf16)
        dal_p = _rowsum8(da * q["dt"]) * q["a_neg"]
        dbi_p = _rowsum8(ddtp)
        first = (pl.program_id(0) == 0) & (pl.program_id(1) == 0)

        @pl.when(first)
        def _():
            dnw_ref[...] = dnw_p
            dds_ref[...] = dds_p
            dal_ref[...] = dal_p
            dbi_ref[...] = dbi_p

        @pl.when(jnp.logical_not(first))
        def _():
            dnw_ref[...] += dnw_p
            dds_ref[...] += dds_p
            dal_ref[...] += dal_p
            dbi_ref[...] += dbi_p

    def rows(w):
        return pl.BlockSpec((CHUNK, w), lambda b, c: (b * n_chunk + n_chunk - 1 - c, 0))

    def par(w):
        return pl.BlockSpec((1, w), lambda b, c: (0, 0))

    def acc(w):
        return pl.BlockSpec((SUBLANES, w), lambda b, c: (0, 0))

    return pl.pallas_call(
        body, name=name, grid=(t // SEQ, n_chunk),
        in_specs=[rows(CONV_CH), rows(SSM_INNER), rows(LANES), rows(SSM_INNER),
                  pl.BlockSpec((1, SSM_GROUPS, D_STATE, GROUP_W), lambda b, c: (b * n_chunk + n_chunk - 1 - c, 0, 0, 0)),
                  rows(SSM_INNER), par(LANES), par(LANES), par(SSM_INNER), par(SSM_INNER)],
        out_specs=[rows(CONV_CH), rows(SSM_INNER), rows(LANES), acc(SSM_INNER), acc(SSM_INNER), acc(LANES), acc(LANES)],
        out_shape=[jax.ShapeDtypeStruct((t, CONV_CH), f32), jax.ShapeDtypeStruct((t, SSM_INNER), bf16), jax.ShapeDtypeStruct((t, LANES), bf16),
                   jax.ShapeDtypeStruct((SUBLANES, SSM_INNER), f32), jax.ShapeDtypeStruct((SUBLANES, SSM_INNER), f32),
                   jax.ShapeDtypeStruct((SUBLANES, LANES), f32), jax.ShapeDtypeStruct((SUBLANES, LANES), f32)],
        scratch_shapes=[pltpu.VMEM((SSM_GROUPS, D_STATE, GROUP_W), f32)],
        compiler_params=_cparams(("arbitrary", "arbitrary")),
    )(xbc, z, dtp, y, hs, dyn, *params)


def _adamw_update(g, w, m, v):
    mm = ADAM_B1 * m + (1.0 - ADAM_B1) * g
    vv = ADAM_B2 * v + (1.0 - ADAM_B2) * (g * g)
    m_hat = mm / (1.0 - ADAM_B1 ** ADAM_STEP)
    v_hat = vv / (1.0 - ADAM_B2 ** ADAM_STEP)
    return -ADAM_LR * (m_hat / (jnp.sqrt(v_hat) + ADAM_EPS) + ADAM_WD * w), mm, vv


def _adamw(g_parts, w, m, v, name):
    rows, width = w.shape
    n = len(g_parts)
    tr = _row_tile(rows)

    def body(*refs):
        g_refs, (w_ref, m_ref, v_ref, g_out, d_out, m_out, v_out) = refs[:n], refs[n:]
        g = g_refs[0][...].astype(f32)
        for r in g_refs[1:]:
            g = g + r[...].astype(f32)
        g_out[...] = g
        d_out[...], m_out[...], v_out[...] = _adamw_update(g, w_ref[...], m_ref[...], v_ref[...])

    spec = pl.BlockSpec((tr, width), lambda i: (i, 0))
    return pl.pallas_call(
        body, name=name, grid=(rows // tr,), in_specs=[spec] * (n + 3), out_specs=[spec] * 4,
        out_shape=[jax.ShapeDtypeStruct((rows, width), f32)] * 4, compiler_params=_cparams(("parallel",)),
    )(*g_parts, w, m, v)


def _adamw_layers(landed, w, m, v, after, name):
    depth, rows, width = w.shape
    tr = _row_tile(rows)
    n_i = rows // tr

    def body(*refs):
        part_refs, (w_ref, m_ref, v_ref, _, g_out, d_out, m_out, v_out) = refs[:depth * N_DEV], refs[depth * N_DEV:]
        for l in range(depth):
            @pl.when(pl.program_id(0) == l)
            def _(l=l):
                g = part_refs[l * N_DEV][0].astype(f32)
                for r in part_refs[l * N_DEV + 1:(l + 1) * N_DEV]:
                    g = g + r[0].astype(f32)
                g_out[0] = g
                d_out[0], m_out[0], v_out[0] = _adamw_update(g, w_ref[0], m_ref[0], v_ref[0])

    def part_spec(l, p):
        return pl.BlockSpec((1, tr, width), lambda ll, i: (p, jnp.where(ll == l, i, jnp.where(ll < l, 0, n_i - 1)), 0))

    state = pl.BlockSpec((1, tr, width), lambda ll, i: (ll, i, 0))
    return pl.pallas_call(
        body, name=name, grid=(depth, n_i),
        in_specs=[part_spec(l, p) for l in range(depth) for p in range(N_DEV)] + [state] * 3 + [ANY], out_specs=[state] * 4,
        out_shape=[jax.ShapeDtypeStruct(w.shape, f32)] * 4, compiler_params=_cparams(("arbitrary", "arbitrary")),
    )(*[landed[l] for l in range(depth) for _ in range(N_DEV)], w, m, v, after)


def _row_tile(rows, cap=512):
    for cand in range(min(rows, cap) // SUBLANES * SUBLANES, 0, -SUBLANES):
        if rows % cand == 0:
            return cand
    return rows


def _cols_from_devices(g, width, name):
    n_dev, depth, a, b = g.shape

    def body(g_ref, o_ref):
        for i in range(n_dev):
            o_ref[0, :, i * b:(i + 1) * b] = g_ref[i, 0]
        if width > n_dev * b:
            o_ref[0, :, n_dev * b:width] = jnp.zeros((a, width - n_dev * b), o_ref.dtype)

    return pl.pallas_call(
        body, name=name, grid=(depth,), in_specs=[pl.BlockSpec((n_dev, 1, a, b), lambda l: (0, l, 0, 0))],
        out_specs=pl.BlockSpec((1, a, width), lambda l: (l, 0, 0)), out_shape=jax.ShapeDtypeStruct((depth, a, width), g.dtype),
        compiler_params=_cparams(("parallel",)),
    )(g)


def _devices_from_cols(per_layer, b, name, tr=256):
    depth = len(per_layer)
    a, width = per_layer[0].shape

    def body(*refs):
        o_ref = refs[depth]
        for l in range(depth):
            for i in range(N_DEV):
                o_ref[i, l] = refs[l][:, i * b:(i + 1) * b]

    return pl.pallas_call(
        body, name=name, grid=(a // tr,), in_specs=[pl.BlockSpec((tr, width), lambda r: (r, 0))] * depth,
        out_specs=pl.BlockSpec((N_DEV, depth, tr, b), lambda r: (0, 0, r, 0)),
        out_shape=jax.ShapeDtypeStruct((N_DEV, depth, a, b), per_layer[0].dtype), compiler_params=_cparams(("parallel",)),
    )(*per_layer)


def _me():
    return lax.axis_index("x"), lax.axis_index("y"), lax.axis_index("c")


def _allgather_two_level(shards, name):
    n = len(shards)
    per = 7

    def body(*refs):
        ins, outs, token = refs[:n], refs[n:2 * n], refs[2 * n]
        send_sems, recv_sems, local_sems = refs[2 * n + 1:]
        token[...] = jnp.zeros_like(token)
        x, y, c = _me()
        me, sibling = (x, y, c), (x, y, 1 - c)
        chips = [(1 - x, y), (x, 1 - y), (1 - x, 1 - y)]

        def slot(a, p):
            return outs[a].at[4 * p[0] + 2 * p[1] + p[2]]

        def copy(a, k, block, to, src=None):
            return pltpu.make_async_remote_copy(
                src_ref=slot(a, block) if src is None else src, dst_ref=slot(a, block),
                send_sem=send_sems.at[a * per + k], recv_sem=recv_sems.at[a * per + k], device_id=to, device_id_type=MESH)

        mine = [pltpu.make_async_copy(ins[a], slot(a, me), local_sems.at[a]) for a in range(n)]
        for cp in mine:
            cp.start()
        first = []
        for a in range(n):
            first.append(copy(a, 0, me, sibling, src=ins[a]))
            first += [copy(a, 1 + j, me, (*chip, c), src=ins[a]) for j, chip in enumerate(chips)]
        for cp in first:
            cp.start()
        passed = []
        for j, chip in enumerate(chips):
            for a in range(n):
                copy(a, 1 + j, (*chip, c), me).wait_recv()
                fwd = copy(a, 4 + j, (*chip, c), sibling)
                fwd.start()
                passed.append(fwd)
        for a in range(n):
            copy(a, 0, sibling, me).wait_recv()
            for j, chip in enumerate(chips):
                copy(a, 4 + j, (*chip, 1 - c), me).wait_recv()
        for cp in first + passed:
            cp.wait_send()
        for cp in mine:
            cp.wait()

    outs = pl.pallas_call(
        body, name=name, in_specs=[ANY] * n, out_specs=[ANY] * n + [pl.BlockSpec(memory_space=pltpu.VMEM)],
        out_shape=[jax.ShapeDtypeStruct((N_DEV,) + s.shape, s.dtype) for s in shards] + [jax.ShapeDtypeStruct((SUBLANES, LANES), f32)],
        scratch_shapes=[pltpu.SemaphoreType.DMA((n * per,)), pltpu.SemaphoreType.DMA((n * per,)), pltpu.SemaphoreType.DMA((n,))],
    )(*shards)
    return outs[:n], outs[n]


def _allgather_direct(row, name):
    def body(in_ref, out_ref, send_sems, recv_sems, local_sem):
        x, y, c = _me()
        mine = out_ref.at[4 * x + 2 * y + c]
        local = pltpu.make_async_copy(in_ref, mine, local_sem)
        local.start()
        sends = []
        for k in range(1, N_DEV):
            px, py, pc = x ^ (k >> 2), y ^ ((k >> 1) & 1), c ^ (k & 1)
            sends.append(pltpu.make_async_remote_copy(
                src_ref=in_ref, dst_ref=mine, send_sem=send_sems.at[k - 1], recv_sem=recv_sems.at[k - 1],
                device_id=(px, py, pc), device_id_type=MESH))
        for cp in sends:
            cp.start()
        for k in range(1, N_DEV):
            px, py, pc = x ^ (k >> 2), y ^ ((k >> 1) & 1), c ^ (k & 1)
            theirs = out_ref.at[4 * px + 2 * py + pc]
            pltpu.make_async_remote_copy(
                src_ref=in_ref, dst_ref=theirs, send_sem=send_sems.at[k - 1], recv_sem=recv_sems.at[k - 1],
                device_id=(px, py, pc), device_id_type=MESH).wait_recv()
        for cp in sends:
            cp.wait_send()
        local.wait()

    return pl.pallas_call(
        body, name=name, in_specs=[ANY], out_specs=ANY, out_shape=jax.ShapeDtypeStruct((N_DEV,) + row.shape, row.dtype),
        scratch_shapes=[pltpu.SemaphoreType.DMA((N_DEV - 1,)), pltpu.SemaphoreType.DMA((N_DEV - 1,)), pltpu.SemaphoreType.DMA],
    )(row)


N_CHIP = N_DEV // 2
HBM = pl.BlockSpec(memory_space=pltpu.HBM)
SEM = pl.BlockSpec(memory_space=pltpu.SEMAPHORE)
EFFECT = pltpu.SideEffectType.DATAFLOW_SIDE_EFFECTING


def _peer(k):
    x, y, c = _me()
    return x ^ (k >> 2), y ^ ((k >> 1) & 1), c ^ (k & 1)


def _direct_copies(srcs, lands, send_sems, recv_sems, per_peer):
    x, y, c = _me()
    me = 4 * x + 2 * y + c
    copies = []
    for a in range(len(srcs)):
        for k in range(1, N_DEV):
            px, py, pc = _peer(k)
            piece = srcs[a].at[4 * px + 2 * py + pc] if per_peer else srcs[a]
            copies.append(pltpu.make_async_remote_copy(
                src_ref=piece, dst_ref=lands[a].at[me], send_sem=send_sems.at[a * (N_DEV - 1) + k - 1],
                recv_sem=recv_sems.at[a * (N_DEV - 1) + k - 1], device_id=(px, py, pc), device_id_type=MESH))
    return copies


def _direct_start(srcs, lands, per_peer, name):
    n = len(srcs)
    n_sem = n * (N_DEV - 1)

    def body(*refs):
        src_refs, land_refs = refs[:n], refs[n:2 * n]
        send_sems, recv_sems = refs[2 * n], refs[2 * n + 1]
        token = refs[-1]
        for cp in _direct_copies(src_refs, land_refs, send_sems, recv_sems, per_peer):
            cp.start()
        token[...] = jnp.zeros_like(token)

    outs = pl.pallas_call(
        body, name=name,
        out_shape=(pltpu.SemaphoreType.DMA((n_sem,)), pltpu.SemaphoreType.DMA((n_sem,)),
                   *[pltpu.HBM(s.shape, s.dtype) for s in srcs], *[pltpu.HBM(s.shape, s.dtype) for s in lands],
                   jax.ShapeDtypeStruct((SUBLANES, LANES), f32)),
        in_specs=[HBM] * (2 * n), out_specs=(SEM, SEM, *[HBM] * (2 * n), pl.BlockSpec(memory_space=pltpu.VMEM)),
        input_output_aliases={i: 2 + i for i in range(2 * n)},
        compiler_params=pltpu.CompilerParams(has_side_effects=EFFECT),
    )(*[pltpu.with_memory_space_constraint(s, pltpu.HBM) for s in srcs], *[pltpu.with_memory_space_constraint(s, pltpu.HBM) for s in lands])
    return outs[0], outs[1], outs[2:2 + n], outs[2 + n:2 + 2 * n], outs[-1]


def _direct_wait(send_sems, recv_sems, srcs, lands, after, per_peer, name):
    n = len(srcs)

    def body(*refs):
        src_refs, land_refs = refs[:n], refs[n:2 * n]
        s_sems, r_sems = refs[2 * n], refs[2 * n + 1]
        for cp in _direct_copies(src_refs, land_refs, s_sems, r_sems, per_peer):
            cp.wait_send()
            cp.wait_recv()

    outs = pl.pallas_call(
        body, name=name,
        out_shape=tuple(pltpu.HBM(s.shape, s.dtype) for s in list(srcs) + list(lands)),
        in_specs=[HBM] * (2 * n) + [SEM, SEM, ANY], out_specs=tuple([HBM] * (2 * n)),
        input_output_aliases={i: i for i in range(2 * n)},
        compiler_params=pltpu.CompilerParams(has_side_effects=EFFECT),
    )(*srcs, *lands, send_sems, recv_sems, after)
    return outs[n:]


def _row(v, width=None):
    v = v.reshape(1, -1).astype(f32)
    if width is not None and v.shape[1] < width:
        v = jnp.pad(v, ((0, 0), (0, width - v.shape[1])))
    return v


def _layer_params(p, l):
    return dict(
        norm_mix=_row(p["norm_mix"][l]), norm_ffn=_row(p["norm_ffn"][l]), conv_w=p["conv_w"][l], conv_b=_row(p["conv_b"][l]),
        ssd=(_row(p["dt_bias"][l], LANES), _row(p["a_log"][l], LANES), _row(jnp.repeat(p["d_skip"][l], HEAD_DIM)), _row(p["ssm_norm"][l])))


def _layer_fwd(h, w_in, rest, sp, tabs, l):
    tag = f"l{l}_"
    hn = _rmsnorm_fwd(h, sp["norm_mix"], tag + "norm_mix")
    qkv = _matmul(hn, w_in, mode="nn", n_out=QKV_WIDTH, tn=256, b_off=0, name=tag + "proj_qkv")
    z = _matmul(hn, w_in, mode="nn", n_out=SSM_INNER, tn=256, b_off=Z_OFF // 256, name=tag + "proj_z")
    xbc_pre = _matmul(hn, w_in, mode="nn", n_out=CONV_CH, tn=256, b_off=XBC_OFF // 256, name=tag + "proj_xbc")
    dtp = _matmul(hn, w_in, mode="nn", n_out=LANES, tn=LANES, b_off=DT_OFF // LANES, name=tag + "proj_dt")
    prep = _attn_prep(qkv, tabs, tag + "attn_prep")
    o, lse = _attn_fwd(prep, tag + "attn_fwd")
    xbc = _conv_fwd(xbc_pre, sp["conv_w"], sp["conv_b"], tag + "conv_fwd")
    yn, y, hs = _ssd_fwd(xbc, z, dtp, sp["ssd"], tag + "ssd_fwd")
    w_out, w_gate, w_up, w_down = rest(yn) if callable(rest) else rest
    h2 = _out_proj(o, yn, w_out, h, tag + "out_proj")
    hn2 = _rmsnorm_fwd(h2, sp["norm_ffn"], tag + "norm_ffn")
    g, u, act = _swiglu_fwd(hn2, w_gate, w_up, tag + "ffn_up")
    h3 = _matmul(act, w_down, mode="nn", tk=1408, add=h2, name=tag + "ffn_down")
    saved = dict(h=h, hn=hn, prep=prep, z=z, xbc_pre=xbc_pre, dtp=dtp, o=o, lse=lse, xbc=xbc, yn=yn, y=y, hs=hs, h2=h2, hn2=hn2, g=g, u=u, act=act,
                 rest=(w_out, w_gate, w_up, w_down))
    return h3, saved


def _layer_bwd(dh3, s, big, sp, tabs, l, gd=f32, after_ffn=None):
    tag = f"l{l}_"
    w_in, w_out, w_gate, w_up, w_down = big
    dg, du = _swiglu_bwd(dh3, w_down, s["g"], s["u"], tag + "ffn_down_bwd")
    dw_down = _matmul(s["act"], dh3, mode="tn", tm=1408, tn=512, tk=2048, out_dtype=gd, name=tag + "dw_down")
    dw_gate = _matmul(s["hn2"], dg, mode="tn", tm=512, tn=1408, tk=2048, out_dtype=gd, name=tag + "dw_gate")
    dw_up = _matmul(s["hn2"], du, mode="tn", tm=512, tn=1408, tk=2048, out_dtype=gd, name=tag + "dw_up")
    norm_ffn = sp["norm_ffn"] if after_ffn is None else sp["norm_ffn"] + after_ffn(dict(w_gate=dw_gate, w_up=dw_up, w_down=dw_down))
    dh2, dnf = _nt_norm_bwd([(dg, w_gate), (du, w_up)], s["h2"], norm_ffn, dh3, tag + "ffn_up_bwd_norm", tk=1408)
    d_o = _matmul(dh2, w_out, mode="nt", n_out=ATTN_WIDTH, tn=512, b_off=0, name=tag + "out_attn_bwd")
    dyn = _matmul(dh2, w_out, mode="nt", n_out=SSM_INNER, tn=512, b_off=1, name=tag + "out_ssm_bwd")
    dw_out = jnp.concatenate([_matmul(s["o"], dh2, mode="tn", tm=512, tn=512, tk=2048, out_dtype=gd, name=tag + "dw_out_attn"),
                              _matmul(s["yn"], dh2, mode="tn", tm=512, tn=512, tk=2048, out_dtype=gd, name=tag + "dw_out_ssm")], axis=0)
    dxbc, dz, ddtp, dnw, dds, dal, dbi = _ssd_bwd(s["xbc"], s["z"], s["dtp"], s["y"], s["hs"], dyn, sp["ssd"], tag + "ssd_bwd")
    dxbc_pre, dconv_w, dconv_b = _conv_bwd(s["xbc_pre"], sp["conv_w"], sp["conv_b"], dxbc, tag + "conv_bwd")
    dq, dk, dv = _attn_bwd(s["prep"], tabs, s["o"], s["lse"], d_o, tag + "attn_bwd")
    dproj = jnp.concatenate([dq, dk, dv, dz, dxbc_pre, ddtp], axis=1)
    dw_in = _matmul(s["hn"], dproj, mode="tn", tm=512, tn=1152, tk=2048, out_dtype=gd, name=tag + "dw_in")
    dh, dnm = _nt_norm_bwd([(dproj, w_in)], s["h"], sp["norm_mix"], dh2, tag + "proj_bwd_norm", tk=1152)
    grads = dict(
        norm_mix=dnm.sum(0), w_in=dw_in, conv_w=dconv_w, conv_b=dconv_b[0], dt_bias=dbi.sum(0)[:SSM_HEADS], a_log=dal.sum(0)[:SSM_HEADS],
        d_skip=dds.sum(0).reshape(SSM_HEADS, HEAD_DIM).sum(1), ssm_norm=dnw.sum(0), w_out=dw_out, norm_ffn=dnf.sum(0),
        w_gate=dw_gate, w_up=dw_up, w_down=dw_down)
    return dh, grads


def _local_step(x, positions, target, p, bigs):
    tabs = _rope_tables(positions.reshape(-1, 1), "rope_tables")
    h = x
    saved, sps = [], []
    for l in range(DEPTH):
        sps.append(_layer_params(p, l))
        h, s = _layer_fwd(h, bigs[l][0], bigs[l][1:], sps[l], tabs, l)
        saved.append(s)
    dh, loss_parts, dfn = _final_loss(h, _row(p["final_norm"]), target, "final_loss")
    layer_grads = [None] * DEPTH
    for l in reversed(range(DEPTH)):
        dh, layer_grads[l] = _layer_bwd(dh, saved[l], bigs[l], sps[l], tabs, l)
    grads = {k: [layer_grads[l][k] for l in range(DEPTH)] for k in layer_grads[0]}
    grads["final_norm"] = dfn.sum(0)
    return jnp.sum(loss_parts), dh, grads


BIG = ("w_in", "w_out", "w_gate", "w_up", "w_down")
REST = BIG[1:]
FFN = ("w_gate", "w_up", "w_down")
MIX = ("w_in", "w_out")
COL_SHARDED = ("w_in", "w_gate", "w_up")
SMALL = ("norm_mix", "conv_b", "dt_bias", "a_log", "d_skip", "ssm_norm", "norm_ffn", "final_norm")
WEIGHTS = ("norm_mix", "w_in", "conv_w", "conv_b", "dt_bias", "a_log", "d_skip", "ssm_norm", "w_out", "norm_ffn", "w_gate", "w_up", "w_down", "final_norm")
PACK_W = 1024
SMALL_ROWS = 88
CONVW_ROWS = 96
CONVW_SHARD_ROWS = 16


def _full_from_gathered(name, g, l):
    _, a, b = g.shape
    if name in COL_SHARDED:
        width = IN_PROJ_PAD if name == "w_in" else N_DEV * b
        return _cols_from_devices(g.reshape(N_DEV, 1, a, b), width, f"cols_l{l}_{name}").reshape(a, width)
    return g.reshape(N_DEV * a, b)


def _by_device(name, full, shard_shape, l):
    a, b = shard_shape
    if name in COL_SHARDED:
        return _devices_from_cols([full], b, f"devs_l{l}_{name}").reshape(N_CHIP, 2, a, b)
    return full.reshape(N_CHIP, 2, a, b)


def _pack_rows(parts, rows, width):
    flat = jnp.concatenate([q.reshape(-1) for q in parts])
    return jnp.pad(flat, (0, rows * width - flat.shape[0])).reshape(rows, width)


def _unpack(flat, like):
    out, off = [], 0
    for q in like:
        out.append(flat[off:off + q.size].reshape(q.shape))
        off += q.size
    return out


def kernel(x, positions, norm_mix, w_in, conv_w, conv_b, dt_bias, a_log, d_skip, ssm_norm, w_out, norm_ffn, w_gate, w_up, w_down, final_norm, loss_target, m_norm_mix, m_w_in, m_conv_w, m_conv_b, m_dt_bias, m_a_log, m_d_skip, m_ssm_norm, m_w_out, m_norm_ffn, m_w_gate, m_w_up, m_w_down, m_final_norm, v_norm_mix, v_w_in, v_conv_w, v_conv_b, v_dt_bias, v_a_log, v_d_skip, v_ssm_norm, v_w_out, v_norm_ffn, v_w_gate, v_w_up, v_w_down, v_final_norm):
    w = dict(norm_mix=norm_mix, w_in=w_in, conv_w=conv_w, conv_b=conv_b, dt_bias=dt_bias, a_log=a_log, d_skip=d_skip, ssm_norm=ssm_norm,
             w_out=w_out, norm_ffn=norm_ffn, w_gate=w_gate, w_up=w_up, w_down=w_down, final_norm=final_norm)
    m = dict(norm_mix=m_norm_mix, w_in=m_w_in, conv_w=m_conv_w, conv_b=m_conv_b, dt_bias=m_dt_bias, a_log=m_a_log, d_skip=m_d_skip,
             ssm_norm=m_ssm_norm, w_out=m_w_out, norm_ffn=m_norm_ffn, w_gate=m_w_gate, w_up=m_w_up, w_down=m_w_down, final_norm=m_final_norm)
    v = dict(norm_mix=v_norm_mix, w_in=v_w_in, conv_w=v_conv_w, conv_b=v_conv_b, dt_bias=v_dt_bias, a_log=v_a_log, d_skip=v_d_skip,
             ssm_norm=v_ssm_norm, w_out=v_w_out, norm_ffn=v_norm_ffn, w_gate=v_w_gate, w_up=v_w_up, w_down=v_w_down, final_norm=v_final_norm)
    ax, ay, ac = lax.axis_index("x"), lax.axis_index("y"), lax.axis_index("c")
    dev = 4 * ax + 2 * ay + ac

    assert DEPTH == 2
    t = x.shape[0] * x.shape[1]
    xf, target = x.reshape(t, D_MODEL), loss_target.reshape(t, D_MODEL)

    def own_slot(block):
        return lax.dynamic_update_slice(lax.empty((N_DEV,) + block.shape[1:], block.dtype), block, (dev,) + (0,) * (block.ndim - 1))

    def gather_start(keys, l, tie, name):
        shards = [(w[keys[0]][l] + tie).astype(bf16)] + [w[k][l].astype(bf16) for k in keys[1:]]
        return _direct_start(shards, [own_slot(s[None]) for s in shards], False, name)

    def scatter_start(keys, grads_l, l, name):
        by_dev = [_by_device(k, grads_l[k], w[k].shape[1:], l).reshape((N_DEV,) + w[k].shape[1:]) for k in keys]
        return _direct_start(by_dev, [own_slot(lax.dynamic_slice_in_dim(g, dev, 1, 0)) for g in by_dev], True, name)

    (g_in0, conv_all), tie = _allgather_two_level([w["w_in"][0].astype(bf16), w["conv_w"]], "gather_l0_w_in")
    rest0_copy = gather_start(REST, 0, tie[0, 0], "gather_l0_rest_start")
    l1_copy = gather_start(BIG, 1, rest0_copy[4][0, 0], "gather_l1_start")
    p = {k: w[k] for k in SMALL}
    p["norm_mix"] = p["norm_mix"] + l1_copy[4][0, 0]
    p["conv_w"] = jnp.transpose(conv_all, (1, 2, 0, 3)).reshape(DEPTH, CONV_WIDTH, CONV_CH)
    sp0, sp1 = _layer_params(p, 0), _layer_params(p, 1)

    def rest0(after):
        lands = _direct_wait(*rest0_copy[:4], after, False, "gather_l0_rest_wait")
        return tuple(_full_from_gathered(k, g, 0) for k, g in zip(REST, lands))

    tabs = _rope_tables(positions.reshape(t, 1), "rope_tables")
    w_in0 = _full_from_gathered("w_in", g_in0, 0)
    h1, saved0 = _layer_fwd(xf, w_in0, rest0, sp0, tabs, 0)
    lands1 = _direct_wait(*l1_copy[:4], h1, False, "gather_l1_wait")
    bigs1 = tuple(_full_from_gathered(k, g, 1) for k, g in zip(BIG, lands1))
    h2, saved1 = _layer_fwd(h1, bigs1[0], bigs1[1:], sp1, tabs, 1)
    dh, loss_parts, dfn = _final_loss(h2, _row(p["final_norm"]), target, "final_loss")
    loss_local = jnp.sum(loss_parts)

    dh, grads1 = _layer_bwd(dh, saved1, bigs1, sp1, tabs, 1, gd=bf16)
    l1_grads = scatter_start(BIG, grads1, 1, "scatter_l1_start")
    w_out0, w_gate0, w_up0, w_down0 = saved0["rest"]
    bigs0 = (w_in0, w_out0, w_gate0, w_up0, w_down0 + l1_grads[4][0, 0].astype(bf16))
    ffn0_grads = []

    def after_ffn(grads_ffn):
        ffn0_grads.append(scatter_start(FFN, grads_ffn, 0, "scatter_l0_ffn_start"))
        return ffn0_grads[0][4][0, 0]

    dx, grads0 = _layer_bwd(dh, saved0, bigs0, sp0, tabs, 0, gd=bf16, after_ffn=after_ffn)
    mix0_grads = scatter_start(MIX, grads0, 0, "scatter_l0_mix_start")
    landed = {(k, 1): g for k, g in zip(BIG, _direct_wait(*l1_grads[:4], dx, True, "scatter_l1_wait"))}
    landed.update({(k, 0): g for k, g in zip(FFN, _direct_wait(*ffn0_grads[0][:4], dx, True, "scatter_l0_ffn_wait"))})
    out_g, out_d, out_m, out_v = {}, {}, {}, {}

    def update(keys, after):
        for k in keys:
            res = _adamw_layers([landed[k, l] for l in range(DEPTH)], w[k], m[k], v[k], after, "adamw_" + k)
            for dst, r in zip((out_g, out_d, out_m, out_v), res):
                dst[k] = r

    update(FFN, mix0_grads[4])
    grads = {k: [grads0[k], grads1[k]] for k in grads0 if k not in BIG}
    grads["final_norm"] = dfn.sum(0) + mix0_grads[4][0, 0]

    small_like = [w[k] for k in SMALL]
    small_grads = [jnp.stack(grads[k]) if k != "final_norm" else grads[k] for k in SMALL]
    small_pack = jnp.concatenate([_pack_rows(small_grads, SMALL_ROWS, LANES), _pack_rows([jnp.stack(grads["conv_w"])], CONVW_ROWS, LANES)], axis=0)
    parts = _allgather_direct(small_pack, "gather_small_grads")
    g_s, d_s, m_s, v_s = _adamw(
        [parts[i, :SMALL_ROWS] for i in range(N_DEV)], _pack_rows(small_like, SMALL_ROWS, LANES),
        _pack_rows([m[k] for k in SMALL], SMALL_ROWS, LANES), _pack_rows([v[k] for k in SMALL], SMALL_ROWS, LANES), "adamw_replicated")
    for dst, src in ((out_g, g_s), (out_d, d_s), (out_m, m_s), (out_v, v_s)):
        dst.update(zip(SMALL, _unpack(src.reshape(-1), small_like)))
    shard_w = conv_w.shape[-1]
    conv_parts = parts[:, SMALL_ROWS:].reshape(N_DEV, DEPTH, CONV_WIDTH, CONV_CH)
    conv_mine = lax.dynamic_slice_in_dim(conv_parts, dev * shard_w, shard_w, axis=3)
    g_c, d_c, m_c, v_c = _adamw(
        [_pack_rows([conv_mine[i]], CONVW_SHARD_ROWS, LANES) for i in range(N_DEV)], _pack_rows([conv_w], CONVW_SHARD_ROWS, LANES),
        _pack_rows([m["conv_w"]], CONVW_SHARD_ROWS, LANES), _pack_rows([v["conv_w"]], CONVW_SHARD_ROWS, LANES), "adamw_conv_w")
    for dst, src in ((out_g, g_c), (out_d, d_c), (out_m, m_c), (out_v, v_c)):
        dst["conv_w"] = src.reshape(-1)[:conv_w.size].reshape(conv_w.shape)

    landed.update({(k, 0): g for k, g in zip(MIX, _direct_wait(*mix0_grads[:4], v_c + out_v["w_down"][0, :CONVW_SHARD_ROWS, :LANES], True, "scatter_l0_mix_wait"))})
    update(MIX, v_c)

    loss = lax.psum(loss_local, ("x", "y", "c"))
    return (loss, dx.reshape(x.shape), *[out_g[k] for k in WEIGHTS], *[out_d[k] for k in WEIGHTS],
            *[out_m[k] for k in WEIGHTS], *[out_v[k] for k in WEIGHTS])
```

```python
import jax
import jax.numpy as jnp
import numpy as np
from jax import lax
from jax.experimental import pallas as pl
from jax.experimental.pallas import tpu as pltpu

f32 = jnp.float32
bf16 = jnp.bfloat16

D_MODEL = 1024
SEQ = 2048
DEPTH = 2
HEAD_DIM = 64
N_ATTN_HEADS = 8
N_KV_HEADS = 2
ATTN_WIDTH = 512
KV_WIDTH = 128
ROPE_DIM = 16
ROPE_THETA = 500000.0
DILATIONS = (1, 4, 16)
ATTN_BLOCK = 128
SSM_HEADS = 16
SSM_INNER = 1024
SSM_GROUPS = 2
D_STATE = 128
CONV_WIDTH = 4
CHUNK = 128
CONV_CH = 1536
MIX_WIDTH = 1536
QKV_WIDTH = ATTN_WIDTH + 2 * KV_WIDTH
DT_OFF = 3328
IN_PROJ = 3344
IN_PROJ_PAD = 3456
FFN_HIDDEN = 2816
EPS = 1e-5
N_DEV = 8
ADAM_LR = 0.001
ADAM_B1 = 0.9
ADAM_B2 = 0.999
ADAM_EPS = 1e-08
ADAM_WD = 0.01
ADAM_STEP = 10

LANES = 128
SUBLANES = 8
VMEM_LIMIT = 56 * 1024 * 1024

MESH = pl.DeviceIdType.MESH
ANY = pl.BlockSpec(memory_space=pl.ANY)


def _cparams(sem, vmem=None):
    return pltpu.CompilerParams(dimension_semantics=sem, vmem_limit_bytes=vmem or VMEM_LIMIT)


def _sigmoid(x):
    return 1.0 / (1.0 + jnp.exp(-x))


def _silu(x):
    return x * _sigmoid(x)


def _dsilu(x):
    s = _sigmoid(x)
    return s * (1.0 + x * (1.0 - s))


def _silu_and_grad(x):
    s = _sigmoid(x)
    return x * s, s * (1.0 + x * (1.0 - s))


def _softplus(x):
    return jnp.maximum(x, 0.0) + jnp.log(1.0 + jnp.exp(-jnp.abs(x)))


def _dot(a, b, dims, precision=None):
    return lax.dot_general(a, b, (dims, ((), ())), preferred_element_type=f32, precision=precision)


def _nn(a, b, precision=None):
    return _dot(a, b, ((1,), (0,)), precision)


def _nt(a, b):
    return _dot(a, b, ((1,), (1,)))


def _tn(a, b):
    return _dot(a, b, ((0,), (0,)))


def _rowsum8(t):
    n, w = t.shape
    return jnp.sum(t.reshape(n // SUBLANES, SUBLANES, w), axis=0)


def _matmul(a, b, *, mode, n_out=None, b_off=0, add=None, out_dtype=f32, tm=2048, tn=512, tk=1024, name):
    if mode == "tn":
        kk, m = a.shape
    else:
        m, kk = a.shape
    n = n_out if n_out is not None else (b.shape[0] if mode == "nt" else b.shape[1])
    tm, tn, tk = min(tm, m), min(tn, n), min(tk, kk)
    assert m % tm == 0 and n % tn == 0 and kk % tk == 0, (name, m, n, kk, tm, tn, tk)
    nk = kk // tk
    if mode == "nn":
        a_spec = pl.BlockSpec((tm, tk), lambda i, j, k: (i, k))
        b_spec = pl.BlockSpec((tk, tn), lambda i, j, k: (k, j + b_off))
        dims = ((1,), (0,))
    elif mode == "nt":
        a_spec = pl.BlockSpec((tm, tk), lambda i, j, k: (i, k))
        b_spec = pl.BlockSpec((tn, tk), lambda i, j, k: (j + b_off, k))
        dims = ((1,), (1,))
    else:
        a_spec = pl.BlockSpec((tk, tm), lambda i, j, k: (k, i))
        b_spec = pl.BlockSpec((tk, tn), lambda i, j, k: (k, j + b_off))
        dims = ((0,), (0,))
    o_spec = pl.BlockSpec((tm, tn), lambda i, j, k: (i, j))
    has_add = add is not None

    def body(*refs):
        if has_add:
            a_ref, b_ref, add_ref, o_ref, acc_ref = refs
        else:
            a_ref, b_ref, o_ref, acc_ref = refs
        k = pl.program_id(2)
        part = _dot(a_ref[...].astype(bf16), b_ref[...].astype(bf16), dims)

        @pl.when(k == 0)
        def _():
            acc_ref[...] = part

        @pl.when(k > 0)
        def _():
            acc_ref[...] += part

        @pl.when(k == nk - 1)
        def _():
            r = acc_ref[...]
            if has_add:
                r = r + add_ref[...]
            o_ref[...] = r.astype(out_dtype)

    in_specs = [a_spec, b_spec] + ([o_spec] if has_add else [])
    args = (a, b) + ((add,) if has_add else ())
    return pl.pallas_call(
        body, name=name, grid=(m // tm, n // tn, nk), in_specs=in_specs, out_specs=o_spec,
        out_shape=jax.ShapeDtypeStruct((m, n), out_dtype), scratch_shapes=[pltpu.VMEM((tm, tn), f32)],
        compiler_params=_cparams(("parallel", "parallel", "arbitrary")),
    )(*args)


def _in_proj(hn, w_in, widths, name, tm=2048, tn=256):
    m, k = hn.shape
    starts = [sum(widths[:i]) // tn for i in range(len(widths))]
    counts = [wd // tn for wd in widths]
    assert all(wd % tn == 0 for wd in widths)
    n_out = len(widths)

    def body(a_ref, w_ref, *o_refs):
        j = pl.program_id(1)
        acc = _nn(a_ref[...], w_ref[...])
        for s, c, o_ref in zip(starts, counts, o_refs):
            @pl.when((j >= s) & (j < s + c))
            def _(o_ref=o_ref):
                o_ref[...] = acc

    def o_spec(s, c):
        return pl.BlockSpec((tm, tn), lambda i, j: (i, jnp.clip(j - s, 0, c - 1)))

    return pl.pallas_call(
        body, name=name, grid=(m // tm, sum(counts)),
        in_specs=[pl.BlockSpec((tm, k), lambda i, j: (i, 0)), pl.BlockSpec((k, tn), lambda i, j: (0, j))],
        out_specs=[o_spec(s, c) for s, c in zip(starts, counts)],
        out_shape=[jax.ShapeDtypeStruct((m, wd), f32) for wd in widths], compiler_params=_cparams(("parallel", "arbitrary")),
    )(hn, w_in)


def _out_proj(o, yn, w_out, h, name, tm=2048, tn=512):
    m, kb = o.shape
    n = w_out.shape[1]
    n_y = yn.shape[1] // kb
    assert yn.shape[1] % kb == 0 and w_out.shape[0] == kb * (1 + n_y)

    def body(*refs):
        o_ref, y_refs, w_refs, h_ref, out_ref = refs[0], refs[1:1 + n_y], refs[1 + n_y:2 + 2 * n_y], refs[-2], refs[-1]
        acc = h_ref[...] + _nn(o_ref[...].astype(bf16), w_refs[0][...])
        for y_ref, w_ref in zip(y_refs, w_refs[1:]):
            acc = acc + _nn(y_ref[...], w_ref[...])
        out_ref[...] = acc

    res = pl.BlockSpec((tm, tn), lambda i, j: (i, j))

    def a_blk(c):
        return pl.BlockSpec((tm, kb), lambda i, j: (i, c))

    def w_blk(r):
        return pl.BlockSpec((kb, tn), lambda i, j: (r, j))

    return pl.pallas_call(
        body, name=name, grid=(m // tm, n // tn),
        in_specs=[a_blk(0)] + [a_blk(c) for c in range(n_y)] + [w_blk(r) for r in range(1 + n_y)] + [res],
        out_specs=res, out_shape=jax.ShapeDtypeStruct((m, n), f32), compiler_params=_cparams(("parallel", "parallel")),
    )(o, *[yn] * n_y, *[w_out] * (1 + n_y), h)


def _swiglu_fwd(hn, w_gate, w_up, name, tm=2048, tn=256):
    m, k = hn.shape
    n = w_gate.shape[1]

    def body(a_ref, wg_ref, wu_ref, g_ref, u_ref, act_ref):
        a = a_ref[...]
        g = _nn(a, wg_ref[...])
        u = _nn(a, wu_ref[...])
        sg, dsg = _silu_and_grad(g)
        g_ref[...] = (u * dsg).astype(bf16)
        u_ref[...] = sg.astype(bf16)
        act_ref[...] = (sg * u).astype(bf16)

    a_spec = pl.BlockSpec((tm, k), lambda i, j: (i, 0))
    w_spec = pl.BlockSpec((k, tn), lambda i, j: (0, j))
    o_spec = pl.BlockSpec((tm, tn), lambda i, j: (i, j))
    return pl.pallas_call(
        body, name=name, grid=(m // tm, n // tn), in_specs=[a_spec, w_spec, w_spec], out_specs=[o_spec, o_spec, o_spec],
        out_shape=[jax.ShapeDtypeStruct((m, n), bf16)] * 3,
        compiler_params=_cparams(("parallel", "parallel")),
    )(hn, w_gate, w_up)


def _swiglu_bwd(dh, w_down, g, u, name, tm=2048, tn=256):
    m, k = dh.shape
    n = w_down.shape[0]

    def body(a_ref, w_ref, g_ref, u_ref, dg_ref, du_ref):
        dact = _nt(a_ref[...].astype(bf16), w_ref[...])
        dg_ref[...] = (dact * g_ref[...].astype(f32)).astype(bf16)
        du_ref[...] = (dact * u_ref[...].astype(f32)).astype(bf16)

    a_spec = pl.BlockSpec((tm, k), lambda i, j: (i, 0))
    w_spec = pl.BlockSpec((tn, k), lambda i, j: (j, 0))
    o_spec = pl.BlockSpec((tm, tn), lambda i, j: (i, j))
    return pl.pallas_call(
        body, name=name, grid=(m // tm, n // tn), in_specs=[a_spec, w_spec, o_spec, o_spec], out_specs=[o_spec, o_spec],
        out_shape=[jax.ShapeDtypeStruct((m, n), bf16), jax.ShapeDtypeStruct((m, n), bf16)],
        compiler_params=_cparams(("parallel", "parallel")),
    )(dh, w_down, g, u)


def _rmsnorm_fwd(h, w, name, tm=512):
    m, d = h.shape

    def body(h_ref, w_ref, o_ref):
        x = h_ref[...]
        r = lax.rsqrt(jnp.mean(x * x, axis=-1, keepdims=True) + EPS)
        o_ref[...] = (x * r * w_ref[...]).astype(bf16)

    return pl.pallas_call(
        body, name=name, grid=(m // tm,),
        in_specs=[pl.BlockSpec((tm, d), lambda i: (i, 0)), pl.BlockSpec((1, d), lambda i: (0, 0))],
        out_specs=pl.BlockSpec((tm, d), lambda i: (i, 0)), out_shape=jax.ShapeDtypeStruct((m, d), bf16),
        compiler_params=_cparams(("parallel",)),
    )(h, w)


def _nt_norm_bwd(pairs, h, w, dres, name, tk, tm=1024):
    m, d = h.shape
    steps = [p[0].shape[1] // tk for p in pairs]
    assert all(p[0].shape[1] % tk == 0 for p in pairs), (name, tk)
    starts = [sum(steps[:i]) for i in range(len(pairs))]
    nk = sum(steps)
    n_p = len(pairs)

    def body(*refs):
        ab = refs[:2 * n_p]
        h_ref, w_ref, dres_ref, dh_ref, dw_ref, acc_ref = refs[2 * n_p:]
        i, k = pl.program_id(0), pl.program_id(1)

        @pl.when(k == 0)
        def _():
            acc_ref[...] = jnp.zeros_like(acc_ref)

        for p in range(n_p):
            @pl.when((k >= starts[p]) & (k < starts[p] + steps[p]))
            def _(p=p):
                acc_ref[...] += _nt(ab[2 * p][...], ab[2 * p + 1][...])

        @pl.when(k == nk - 1)
        def _():
            x = h_ref[...]
            r = lax.rsqrt(jnp.mean(x * x, axis=-1, keepdims=True) + EPS)
            xhat = x * r
            dy = acc_ref[...]
            gw = dy * w_ref[...]
            dh_ref[...] = dres_ref[...] + r * (gw - xhat * jnp.mean(gw * xhat, axis=-1, keepdims=True))
            part = _rowsum8(dy * xhat)

            @pl.when(i == 0)
            def _():
                dw_ref[...] = part

            @pl.when(i > 0)
            def _():
                dw_ref[...] += part

    def clamp(k, p):
        return jnp.clip(k - starts[p], 0, steps[p] - 1)

    in_specs = []
    for p in range(n_p):
        in_specs += [pl.BlockSpec((tm, tk), lambda i, k, p=p: (i, clamp(k, p))), pl.BlockSpec((d, tk), lambda i, k, p=p: (0, clamp(k, p)))]
    row = pl.BlockSpec((tm, d), lambda i, k: (i, 0))
    in_specs += [row, pl.BlockSpec((1, d), lambda i, k: (0, 0)), row]
    return pl.pallas_call(
        body, name=name, grid=(m // tm, nk), in_specs=in_specs,
        out_specs=[row, pl.BlockSpec((SUBLANES, d), lambda i, k: (0, 0))],
        out_shape=[jax.ShapeDtypeStruct((m, d), f32), jax.ShapeDtypeStruct((SUBLANES, d), f32)],
        scratch_shapes=[pltpu.VMEM((tm, d), f32)], compiler_params=_cparams(("arbitrary", "arbitrary")),
    )(*[t for p in pairs for t in p], h, w, dres)


def _final_loss(h, w, target, name, tm=512):
    m, d = h.shape

    def body(h_ref, w_ref, t_ref, dh_ref, loss_ref, dw_ref):
        x = h_ref[...]
        r = lax.rsqrt(jnp.mean(x * x, axis=-1, keepdims=True) + EPS)
        xhat = x * r
        ww = w_ref[...]
        err = xhat * ww - t_ref[...]
        dy = err * (1.0 / d)
        gw = dy * ww
        dh_ref[...] = r * (gw - xhat * jnp.mean(gw * xhat, axis=-1, keepdims=True))
        lpart = _rowsum8(err * err) * (0.5 / d)
        wpart = _rowsum8(dy * xhat)

        @pl.when(pl.program_id(0) == 0)
        def _():
            loss_ref[...] = lpart
            dw_ref[...] = wpart

        @pl.when(pl.program_id(0) > 0)
        def _():
            loss_ref[...] += lpart
            dw_ref[...] += wpart

    row = pl.BlockSpec((tm, d), lambda i: (i, 0))
    acc = pl.BlockSpec((SUBLANES, d), lambda i: (0, 0))
    return pl.pallas_call(
        body, name=name, grid=(m // tm,),
        in_specs=[row, pl.BlockSpec((1, d), lambda i: (0, 0)), row], out_specs=[row, acc, acc],
        out_shape=[jax.ShapeDtypeStruct((m, d), f32), jax.ShapeDtypeStruct((SUBLANES, d), f32), jax.ShapeDtypeStruct((SUBLANES, d), f32)],
        compiler_params=_cparams(("arbitrary",)),
    )(h, w, target)


def _lane_tables():
    f = np.arange(LANES) % HEAD_DIM
    inv = ROPE_THETA ** (-jnp.arange(0, ROPE_DIM, 2, dtype=f32) / ROPE_DIM)
    invf = jnp.where(f < ROPE_DIM, inv[f % (ROPE_DIM // 2)], 0.0).astype(f32)
    return invf.reshape(1, LANES)


def _rope_tables(pos_col, name):
    t = pos_col.shape[0]
    tm = SEQ

    def body(p_ref, f_ref, c_ref, s1_ref, s2_ref):
        ang = p_ref[...].astype(f32) * f_ref[...]
        co, si = jnp.cos(ang), jnp.sin(ang)
        f = lax.broadcasted_iota(jnp.int32, (tm, LANES), 1) % HEAD_DIM
        c_ref[...] = jnp.where(f < ROPE_DIM, co, 1.0)
        s1_ref[...] = jnp.where(f < ROPE_DIM // 2, -si, 0.0)
        s2_ref[...] = jnp.where((f >= ROPE_DIM // 2) & (f < ROPE_DIM), si, 0.0)

    row = pl.BlockSpec((tm, LANES), lambda i: (i, 0))
    return pl.pallas_call(
        body, name=name, grid=(t // tm,),
        in_specs=[pl.BlockSpec((tm, 1), lambda i: (i, 0)), pl.BlockSpec((1, LANES), lambda i: (0, 0))],
        out_specs=[row, row, row], out_shape=[jax.ShapeDtypeStruct((t, LANES), f32)] * 3,
        compiler_params=_cparams(("parallel",)),
    )(pos_col, _lane_tables())


def _rot(x, c, s1, s2):
    return x * c + pltpu.roll(x, LANES - ROPE_DIM // 2, 1) * s1 + pltpu.roll(x, ROPE_DIM // 2, 1) * s2


def _rot_t(g, c, s1, s2):
    return g * c + pltpu.roll(g * s1, ROPE_DIM // 2, 1) + pltpu.roll(g * s2, LANES - ROPE_DIM // 2, 1)


def _dup_head(x, kvh, low):
    a = jnp.where(kvh == 0, x, pltpu.roll(x, HEAD_DIM, 1))
    return jnp.where(low, a, pltpu.roll(a, HEAD_DIM, 1))


def _deinterleave(src_ref, dst_ref, d, dtype):
    length = SEQ // d
    if d == 1:
        dst_ref[...] = src_ref[...].astype(dtype)
    else:
        for r in range(d):
            dst_ref[pl.ds(r * length, length), :] = src_ref[pl.ds(r, length, stride=d), :].astype(dtype)


def _interleave_store(src_ref, dst_ref, d, accumulate):
    length = SEQ // d
    if d == 1:
        if accumulate:
            dst_ref[...] += src_ref[...]
        else:
            dst_ref[...] = src_ref[...]
    else:
        for r in range(d):
            blk = src_ref[pl.ds(r * length, length), :]
            if accumulate:
                dst_ref[pl.ds(r, length, stride=d), :] = dst_ref[pl.ds(r, length, stride=d), :] + blk
            else:
                dst_ref[pl.ds(r, length, stride=d), :] = blk


def _attn_masks():
    qi = lax.broadcasted_iota(jnp.int32, (ATTN_BLOCK, ATTN_BLOCK), 0)
    ki = lax.broadcasted_iota(jnp.int32, (ATTN_BLOCK, ATTN_BLOCK), 1)
    low = lax.broadcasted_iota(jnp.int32, (ATTN_BLOCK, LANES), 1) < HEAD_DIM
    return ki <= qi, ki >= qi, low


NEG_INF = float("-inf")
ATTN_UNROLL = 4


N_BRANCH = len(DILATIONS)


def _attn_prep(qkv, tabs, name):
    t = qkv.shape[0]
    nb = t // SEQ
    n_j = ATTN_WIDTH // LANES

    def q_body(q_ref, c_ref, s1_ref, s2_ref, out_ref, xr):
        xr[...] = _rot(q_ref[...], c_ref[...], s1_ref[...], s2_ref[...]) * (HEAD_DIM ** -0.5)
        for bi, d in enumerate(DILATIONS):
            _deinterleave(xr, out_ref.at[bi], d, bf16)

    def kv_body(x_ref, c_ref, s1_ref, s2_ref, out_ref, xr):
        lowfull = lax.broadcasted_iota(jnp.int32, (SEQ, LANES), 1) < HEAD_DIM
        x = x_ref[...]
        x = jnp.where(pl.program_id(1) == 0, _rot(x, c_ref[...], s1_ref[...], s2_ref[...]), x)
        for kvh in range(N_KV_HEADS):
            xr[...] = _dup_head(x, kvh, lowfull)
            for bi, d in enumerate(DILATIONS):
                length = SEQ // d
                for r in range(d):
                    rows = xr[...] if d == 1 else xr[pl.ds(r, length, stride=d), :]
                    out_ref[0, bi, pl.ds(r * length, length), kvh * LANES:(kvh + 1) * LANES] = rows.astype(bf16)

    tab = pl.BlockSpec((SEQ, LANES), lambda b, j: (b, 0))
    q = pl.pallas_call(
        q_body, name=name + "_q", grid=(nb, n_j),
        in_specs=[pl.BlockSpec((SEQ, LANES), lambda b, j: (b, j)), tab, tab, tab],
        out_specs=pl.BlockSpec((N_BRANCH, SEQ, LANES), lambda b, j: (0, b, j)),
        out_shape=jax.ShapeDtypeStruct((N_BRANCH, t, ATTN_WIDTH), bf16), scratch_shapes=[pltpu.VMEM((SEQ, LANES), f32)],
        compiler_params=_cparams(("parallel", "parallel")),
    )(qkv, *tabs)
    kv = pl.pallas_call(
        kv_body, name=name + "_kv", grid=(nb, 2),
        in_specs=[pl.BlockSpec((SEQ, LANES), lambda b, j: (b, n_j + j)), tab, tab, tab],
        out_specs=pl.BlockSpec((1, N_BRANCH, SEQ, N_KV_HEADS * LANES), lambda b, j: (j, 0, b, 0)),
        out_shape=jax.ShapeDtypeStruct((2, N_BRANCH, t, N_KV_HEADS * LANES), bf16), scratch_shapes=[pltpu.VMEM((SEQ, LANES), f32)],
        compiler_params=_cparams(("parallel", "parallel")),
    )(qkv, *tabs)
    return q, kv


def _attn_fwd(prep, name):
    q_all, kv_all = prep
    t = q_all.shape[1]
    nb = t // SEQ
    n_blk = SEQ // ATTN_BLOCK

    def body(q_ref, k_ref, v_ref, o_ref, lse_ref, ob, lb, o0, o1, o2, l0, l1, l2, ss):
        cur_ok, prev_ok, low = _attn_masks()
        onat, lnat = (o0, o1, o2), (l0, l1, l2)
        for bi, d in enumerate(DILATIONS):
            qd, kd, vd = q_ref.at[bi], k_ref.at[0, bi], v_ref.at[0, bi]
            per_res = n_blk // d
            use_prev = per_res > 1

            def scores(n, carry):
                start = pl.multiple_of(n * ATTN_BLOCK, ATTN_BLOCK)
                has_prev = (n % per_res) != 0
                pstart = pl.multiple_of(jnp.maximum(n - 1, 0) * ATTN_BLOCK, ATTN_BLOCK)
                qb = qd[pl.ds(start, ATTN_BLOCK), :]
                kc = kd[pl.ds(start, ATTN_BLOCK), :]
                if use_prev:
                    kp = kd[pl.ds(pstart, ATTN_BLOCK), :]
                for a in range(2):
                    qa = jnp.where(low if a == 0 else ~low, qb, jnp.zeros_like(qb))
                    ss[2 * n + a, :, 0:ATTN_BLOCK] = jnp.where(cur_ok, _nt(qa, kc), NEG_INF)
                    if use_prev:
                        ss[2 * n + a, :, ATTN_BLOCK:2 * ATTN_BLOCK] = jnp.where(prev_ok & has_prev, _nt(qa, kp), NEG_INF)
                return carry

            def softmax_pv(n, carry):
                start = pl.multiple_of(n * ATTN_BLOCK, ATTN_BLOCK)
                pstart = pl.multiple_of(jnp.maximum(n - 1, 0) * ATTN_BLOCK, ATTN_BLOCK)
                vc = vd[pl.ds(start, ATTN_BLOCK), :]
                if use_prev:
                    vp = vd[pl.ds(pstart, ATTN_BLOCK), :]
                outs, lses = [], []
                for a in range(2):
                    sc = ss[2 * n + a, :, 0:ATTN_BLOCK]
                    if use_prev:
                        sp = ss[2 * n + a, :, ATTN_BLOCK:2 * ATTN_BLOCK]
                        m = jnp.max(jnp.maximum(sc, sp), axis=1, keepdims=True)
                        pc, pp = jnp.exp(sc - m), jnp.exp(sp - m)
                        den = jnp.sum(pc + pp, axis=1, keepdims=True)
                        acc = _nn(pc.astype(bf16), vc) + _nn(pp.astype(bf16), vp)
                    else:
                        m = jnp.max(sc, axis=1, keepdims=True)
                        pc = jnp.exp(sc - m)
                        den = jnp.sum(pc, axis=1, keepdims=True)
                        acc = _nn(pc.astype(bf16), vc)
                    outs.append(acc * (1.0 / den))
                    lses.append(m + jnp.log(den))
                ob[pl.ds(start, ATTN_BLOCK), :] = jnp.where(low, outs[0], outs[1])
                lb[pl.ds(start, ATTN_BLOCK), :] = jnp.where(low, lses[0], lses[1])
                return carry

            lax.fori_loop(0, n_blk, scores, 0, unroll=ATTN_UNROLL)
            lax.fori_loop(0, n_blk, softmax_pv, 0, unroll=ATTN_UNROLL)
            _interleave_store(ob, onat[bi], d, False)
            _interleave_store(lb, lnat[bi], d, False)
        la, lbb, lc = l0[...], l1[...], l2[...]
        lm = jnp.maximum(jnp.maximum(la, lbb), lc)
        wa, wb, wc = jnp.exp(la - lm), jnp.exp(lbb - lm), jnp.exp(lc - lm)
        ws = wa + wb + wc
        o_ref[...] = (wa * o0[...] + wb * o1[...] + wc * o2[...]) / ws
        lse_ref[...] = lm + jnp.log(ws)

    def col(jj):
        return pl.BlockSpec((SEQ, LANES), lambda b, j: (b, jj if jj is not None else j))

    fs = pltpu.VMEM((SEQ, LANES), f32)
    return pl.pallas_call(
        body, name=name, grid=(nb, ATTN_WIDTH // LANES),
        in_specs=[pl.BlockSpec((N_BRANCH, SEQ, LANES), lambda b, j: (0, b, j)),
                  pl.BlockSpec((1, N_BRANCH, SEQ, LANES), lambda b, j: (0, 0, b, j // 2)),
                  pl.BlockSpec((1, N_BRANCH, SEQ, LANES), lambda b, j: (1, 0, b, j // 2))],
        out_specs=[col(None), col(None)],
        out_shape=[jax.ShapeDtypeStruct((t, ATTN_WIDTH), f32), jax.ShapeDtypeStruct((t, ATTN_WIDTH), f32)],
        scratch_shapes=[fs, fs, fs, fs, fs, fs, fs, fs, pltpu.VMEM((2 * n_blk, ATTN_BLOCK, 2 * ATTN_BLOCK), f32)],
        compiler_params=_cparams(("parallel", "parallel")),
    )(q_all, kv_all, kv_all)


def _attn_bwd(prep, tabs, o, lse, do, name):
    q_all, kv_all = prep
    t = q_all.shape[1]
    nb = t // SEQ
    n_blk = SEQ // ATTN_BLOCK
    n_j = ATTN_WIDTH // LANES

    def body(q_ref, k_ref, v_ref, c_ref, s1_ref, s2_ref, o_ref, lse_ref, do_ref, dq_ref, dk_ref, dv_ref,
             dl, dod, lsd, dld, dqd, dkd, dvd, dqa, dka, dva, pb, dsb, dk_acc, dv_acc):
        j = pl.program_id(1)
        pb[2 * n_blk:2 * n_blk + 2] = jnp.zeros((2, ATTN_BLOCK, 2 * ATTN_BLOCK), bf16)
        dsb[2 * n_blk:2 * n_blk + 2] = jnp.zeros((2, ATTN_BLOCK, 2 * ATTN_BLOCK), bf16)
        kvh = j // 2
        cur_ok, prev_ok, low = _attn_masks()
        lowfull = lax.broadcasted_iota(jnp.int32, (SEQ, LANES), 1) < HEAD_DIM
        c, s1, s2 = c_ref[...], s1_ref[...], s2_ref[...]
        prod = do_ref[...] * o_ref[...]
        d_lo = jnp.sum(jnp.where(lowfull, prod, 0.0), axis=1, keepdims=True)
        d_hi = jnp.sum(jnp.where(lowfull, 0.0, prod), axis=1, keepdims=True)
        dl[...] = jnp.where(lowfull, d_lo, d_hi)
        dqa[...] = jnp.zeros_like(dqa)
        dka[...] = jnp.zeros_like(dka)
        dva[...] = jnp.zeros_like(dva)
        for bi, d in enumerate(DILATIONS):
            qd, kd, vd = q_ref.at[bi], k_ref.at[0, bi], v_ref.at[0, bi]
            _deinterleave(do_ref, dod, d, bf16)
            _deinterleave(lse_ref, lsd, d, f32)
            _deinterleave(dl, dld, d, f32)
            per_res = n_blk // d
            use_prev = per_res > 1
            curl, prevl = slice(0, ATTN_BLOCK), slice(ATTN_BLOCK, 2 * ATTN_BLOCK)

            def halves(x):
                zero = jnp.zeros_like(x)
                return jnp.where(low, x, zero), jnp.where(low, zero, x)

            def probs(n, carry):
                start = pl.multiple_of(n * ATTN_BLOCK, ATTN_BLOCK)
                has_prev = (n % per_res) != 0
                pstart = pl.multiple_of(jnp.maximum(n - 1, 0) * ATTN_BLOCK, ATTN_BLOCK)
                cur, prev = pl.ds(start, ATTN_BLOCK), pl.ds(pstart, ATTN_BLOCK)
                qas, doas = halves(qd[cur, :]), halves(dod[cur, :])
                kc, vc = kd[cur, :], vd[cur, :]
                if use_prev:
                    kp, vp = kd[prev, :], vd[prev, :]
                lsb, dlb = lsd[cur, :], dld[cur, :]
                for a in range(2):
                    ls = lsb[:, a * HEAD_DIM:a * HEAD_DIM + 1]
                    de = dlb[:, a * HEAD_DIM:a * HEAD_DIM + 1]
                    pc = jnp.exp(jnp.where(cur_ok, _nt(qas[a], kc), NEG_INF) - ls)
                    pb[2 * n + a, :, curl] = pc.astype(bf16)
                    dsb[2 * n + a, :, curl] = (pc * (_nt(doas[a], vc) - de)).astype(bf16)
                    if use_prev:
                        pp = jnp.exp(jnp.where(prev_ok & has_prev, _nt(qas[a], kp), NEG_INF) - ls)
                        pb[2 * n + a, :, prevl] = pp.astype(bf16)
                        dsb[2 * n + a, :, prevl] = (pp * (_nt(doas[a], vp) - de)).astype(bf16)
                return carry

            def grads(n, carry):
                start = pl.multiple_of(n * ATTN_BLOCK, ATTN_BLOCK)
                pstart = pl.multiple_of(jnp.maximum(n - 1, 0) * ATTN_BLOCK, ATTN_BLOCK)
                nstart = pl.multiple_of(jnp.minimum(n + 1, n_blk - 1) * ATTN_BLOCK, ATTN_BLOCK)
                cur, prev, nxt = pl.ds(start, ATTN_BLOCK), pl.ds(pstart, ATTN_BLOCK), pl.ds(nstart, ATTN_BLOCK)
                kc = kd[cur, :]
                dqs = [_nn(dsb[2 * n + a, :, curl], kc) for a in range(2)]
                q_rows, do_rows = list(halves(qd[cur, :])), list(halves(dod[cur, :]))
                ds_rows, p_rows = [dsb[2 * n + a, :, curl] for a in range(2)], [pb[2 * n + a, :, curl] for a in range(2)]
                if use_prev:
                    kp = kd[prev, :]
                    dqs = [dqs[a] + _nn(dsb[2 * n + a, :, prevl], kp) for a in range(2)]
                    q_rows += list(halves(qd[nxt, :]))
                    do_rows += list(halves(dod[nxt, :]))
                    ds_rows += [dsb[2 * n + 2 + a, :, prevl] for a in range(2)]
                    p_rows += [pb[2 * n + 2 + a, :, prevl] for a in range(2)]
                dqd[cur, :] = jnp.where(low, dqs[0], dqs[1])
                dkd[cur, :] = _tn(jnp.concatenate(ds_rows, axis=0), jnp.concatenate(q_rows, axis=0))
                dvd[cur, :] = _tn(jnp.concatenate(p_rows, axis=0), jnp.concatenate(do_rows, axis=0))
                return carry

            lax.fori_loop(0, n_blk, probs, 0, unroll=ATTN_UNROLL)
            lax.fori_loop(0, n_blk, grads, 0, unroll=ATTN_UNROLL)
            _interleave_store(dqd, dqa, d, True)
            _interleave_store(dkd, dka, d, True)
            _interleave_store(dvd, dva, d, True)
        dq_ref[...] = _rot_t(dqa[...] * (HEAD_DIM ** -0.5), c, s1, s2).astype(bf16)
        dkf = dka[...]
        dkf = _rot_t(dkf + pltpu.roll(dkf, HEAD_DIM, 1), c, s1, s2)
        dvf = dva[...]
        dvf = dvf + pltpu.roll(dvf, HEAD_DIM, 1)
        mine = (lax.broadcasted_iota(jnp.int32, (SEQ, LANES), 1) // HEAD_DIM) == kvh
        dkc_, dvc_ = jnp.where(mine, dkf, 0.0), jnp.where(mine, dvf, 0.0)

        @pl.when(j == 0)
        def _():
            dk_acc[...] = dkc_
            dv_acc[...] = dvc_

        @pl.when(j > 0)
        def _():
            dk_acc[...] += dkc_
            dv_acc[...] += dvc_

        @pl.when(j == n_j - 1)
        def _():
            dk_ref[...] = dk_acc[...].astype(bf16)
            dv_ref[...] = dv_acc[...].astype(bf16)

    def col(jj):
        return pl.BlockSpec((SEQ, LANES), lambda b, j: (b, jj if jj is not None else j))

    tab = pl.BlockSpec((SEQ, LANES), lambda b, j: (b, 0))
    fs = pltpu.VMEM((SEQ, LANES), f32)
    hs = pltpu.VMEM((SEQ, LANES), bf16)
    return pl.pallas_call(
        body, name=name, grid=(nb, n_j),
        in_specs=[pl.BlockSpec((N_BRANCH, SEQ, LANES), lambda b, j: (0, b, j)),
                  pl.BlockSpec((1, N_BRANCH, SEQ, LANES), lambda b, j: (0, 0, b, j // 2)),
                  pl.BlockSpec((1, N_BRANCH, SEQ, LANES), lambda b, j: (1, 0, b, j // 2)),
                  tab, tab, tab, col(None), col(None), col(None)],
        out_specs=[col(None), tab, tab],
        out_shape=[jax.ShapeDtypeStruct((t, ATTN_WIDTH), bf16), jax.ShapeDtypeStruct((t, LANES), bf16), jax.ShapeDtypeStruct((t, LANES), bf16)],
        scratch_shapes=[fs, hs, fs, fs, fs, fs, fs, fs, fs, fs,
                        pltpu.VMEM((2 * n_blk + 2, ATTN_BLOCK, 2 * ATTN_BLOCK), bf16), pltpu.VMEM((2 * n_blk + 2, ATTN_BLOCK, 2 * ATTN_BLOCK), bf16), fs, fs],
        compiler_params=_cparams(("parallel", "arbitrary")),
    )(q_all, kv_all, kv_all, *tabs, o, lse, do)


def _conv_pre(x, w_ref, b_ref, row):
    shifted = [x] + [jnp.where(row >= s, pltpu.roll(x, s, 0), 0.0) for s in range(1, CONV_WIDTH)]
    pre = b_ref[...] + w_ref[CONV_WIDTH - 1:CONV_WIDTH, :] * x
    for s in range(1, CONV_WIDTH):
        pre = pre + w_ref[CONV_WIDTH - 1 - s:CONV_WIDTH - s, :] * shifted[s]
    return pre, shifted


def _conv_fwd(x, w, b, name, tc=512):
    t, ch = x.shape

    def body(x_ref, w_ref, b_ref, o_ref):
        row = lax.broadcasted_iota(jnp.int32, (SEQ, tc), 0)
        pre, _ = _conv_pre(x_ref[...], w_ref, b_ref, row)
        o_ref[...] = _silu(pre)

    xs = pl.BlockSpec((SEQ, tc), lambda i, j: (i, j))
    return pl.pallas_call(
        body, name=name, grid=(t // SEQ, ch // tc),
        in_specs=[xs, pl.BlockSpec((CONV_WIDTH, tc), lambda i, j: (0, j)), pl.BlockSpec((1, tc), lambda i, j: (0, j))],
        out_specs=xs, out_shape=jax.ShapeDtypeStruct((t, ch), f32),
        compiler_params=_cparams(("parallel", "parallel")),
    )(x, w, b)


def _conv_bwd(x, w, b, dact, name, tc=512):
    t, ch = x.shape

    def body(x_ref, w_ref, b_ref, d_ref, dx_ref, dw_ref, db_ref):
        row = lax.broadcasted_iota(jnp.int32, (SEQ, tc), 0)
        pre, shifted = _conv_pre(x_ref[...], w_ref, b_ref, row)
        dpre = d_ref[...] * _dsilu(pre)
        dx = w_ref[CONV_WIDTH - 1:CONV_WIDTH, :] * dpre
        for s in range(1, CONV_WIDTH):
            dx = dx + w_ref[CONV_WIDTH - 1 - s:CONV_WIDTH - s, :] * jnp.where(row < SEQ - s, pltpu.roll(dpre, SEQ - s, 0), 0.0)
        dx_ref[...] = dx.astype(bf16)
        first = pl.program_id(1) == 0
        parts = [jnp.sum(dpre * shifted[CONV_WIDTH - 1 - k], axis=0, keepdims=True) for k in range(CONV_WIDTH)]
        dbp = jnp.sum(dpre, axis=0, keepdims=True)

        @pl.when(first)
        def _():
            for k in range(CONV_WIDTH):
                dw_ref[k:k + 1, :] = parts[k]
            db_ref[...] = dbp

        @pl.when(jnp.logical_not(first))
        def _():
            for k in range(CONV_WIDTH):
                dw_ref[k:k + 1, :] += parts[k]
            db_ref[...] += dbp

    xs = pl.BlockSpec((SEQ, tc), lambda j, i: (i, j))
    ws = pl.BlockSpec((CONV_WIDTH, tc), lambda j, i: (0, j))
    bs = pl.BlockSpec((1, tc), lambda j, i: (0, j))
    return pl.pallas_call(
        body, name=name, grid=(ch // tc, t // SEQ),
        in_specs=[xs, ws, bs, xs], out_specs=[xs, ws, bs],
        out_shape=[jax.ShapeDtypeStruct((t, ch), bf16), jax.ShapeDtypeStruct((CONV_WIDTH, ch), f32), jax.ShapeDtypeStruct((1, ch), f32)],
        compiler_params=_cparams(("parallel", "arbitrary")),
    )(x, w, b, dact)


GROUP_W = SSM_INNER // SSM_GROUPS
HEADS_PER_GROUP = SSM_HEADS // SSM_GROUPS


def _split3(x):
    hi = x.astype(bf16)
    r1 = x - hi.astype(f32)
    mid = r1.astype(bf16)
    lo = (r1 - mid.astype(f32)).astype(bf16)
    return hi, mid, lo


def _dot_exact(x, sel, dims, x_is_lhs=True):
    parts = _split3(x)
    if x_is_lhs:
        return _dot(parts[0], sel, dims) + _dot(parts[1], sel, dims) + _dot(parts[2], sel, dims)
    return _dot(sel, parts[0], dims) + _dot(sel, parts[1], dims) + _dot(sel, parts[2], dims)


def _ssd_common(xbc_ref, dt_ref, bias_ref, alog_ref):
    r = lax.broadcasted_iota(jnp.int32, (CHUNK, CHUNK), 0)
    cidx = lax.broadcasted_iota(jnp.int32, (CHUNK, CHUNK), 1)
    causal = r >= cidx
    tril = causal.astype(bf16)
    expand = (lax.broadcasted_iota(jnp.int32, (CHUNK, SSM_INNER), 0)
              == lax.broadcasted_iota(jnp.int32, (CHUNK, SSM_INNER), 1) // HEAD_DIM).astype(bf16)
    head_lane = cidx < SSM_HEADS
    dtp = dt_ref[...] + bias_ref[...]
    dt = jnp.where(head_lane, _softplus(dtp), 0.0)
    a_neg = -jnp.exp(alog_ref[...])
    a = dt * a_neg
    nn_dims = ((1,), (0,))
    cs = _dot_exact(a, tril, nn_dims, x_is_lhs=False)
    dt_e = _dot_exact(dt, expand, nn_dims)
    cs_e = _dot_exact(cs, expand, nn_dims)
    xs = xbc_ref[:, 0:SSM_INNER]
    xg = xs * dt_e
    ecs = jnp.exp(cs_e)
    cs_last = cs_e[CHUNK - 1:CHUNK, :]
    dse = jnp.exp(cs_last - cs_e)
    cde = jnp.exp(cs_last)
    return dict(r=r, cidx=cidx, causal=causal, tril=tril, expand=expand, head_lane=head_lane, dtp=dtp, dt=dt, a_neg=a_neg,
                cs=cs, cst=cs.T, dt_e=dt_e, cs_e=cs_e, xs=xs, xg=xg, ecs=ecs, dse=dse, cde=cde)


def _decay_mat(q, h):
    return jnp.exp(jnp.where(q["causal"], q["cs"][:, h:h + 1] - q["cst"][h:h + 1, :], NEG_INF))


def _gate_norm(y, z, nw, gate=None):
    y2 = y * (_silu(z) if gate is None else gate)
    outs, xhats, rs = [], [], []
    for g in range(SSM_GROUPS):
        sl = slice(g * GROUP_W, (g + 1) * GROUP_W)
        yg = y2[:, sl]
        r = lax.rsqrt(jnp.mean(yg * yg, axis=-1, keepdims=True) + EPS)
        xhats.append(yg * r)
        rs.append(r)
        outs.append(yg * r * nw[:, sl])
    return y2, outs, xhats, rs


def _ssd_fwd(xbc, z, dtp, params, name):
    t = xbc.shape[0]
    n_chunk = SEQ // CHUNK
    low = None

    def body(xbc_ref, z_ref, dt_ref, bias_ref, alog_ref, dskip_ref, nw_ref, yn_ref, y_ref, hs_ref, h_scr):
        @pl.when(pl.program_id(1) == 0)
        def _():
            h_scr[...] = jnp.zeros_like(h_scr)

        q = _ssd_common(xbc_ref, dt_ref, bias_ref, alog_ref)
        low = lax.broadcasted_iota(jnp.int32, (CHUNK, LANES), 1) < HEAD_DIM
        xgb = q["xg"].astype(bf16)
        wst = (q["xg"] * q["dse"]).astype(bf16)
        hs_ref[0] = h_scr[...]
        ys = []
        for g in range(SSM_GROUPS):
            gl = slice(g * GROUP_W, (g + 1) * GROUP_W)
            bg = xbc_ref[:, SSM_INNER + g * D_STATE:SSM_INNER + (g + 1) * D_STATE].astype(bf16)
            cg = xbc_ref[:, SSM_INNER + SSM_GROUPS * D_STATE + g * D_STATE:SSM_INNER + SSM_GROUPS * D_STATE + (g + 1) * D_STATE].astype(bf16)
            cb = _nt(cg, bg)
            hg = h_scr[g]
            yoff = _nn(cg, hg.astype(bf16)) * q["ecs"][:, gl]
            pieces = []
            for i in range(HEADS_PER_GROUP // 2):
                h0 = g * HEADS_PER_GROUP + 2 * i
                xp = xgb[:, h0 * HEAD_DIM:(h0 + 2) * HEAD_DIM]
                m0 = (cb * _decay_mat(q, h0)).astype(bf16)
                m1 = (cb * _decay_mat(q, h0 + 1)).astype(bf16)
                zero = jnp.zeros_like(xp)
                pieces.append(_nn(m0, jnp.where(low, xp, zero)) + _nn(m1, jnp.where(low, zero, xp)))
            ys.append(jnp.concatenate(pieces, axis=1) + yoff + dskip_ref[:, gl] * q["xs"][:, gl])
            h_scr[g] = hg * q["cde"][:, gl] + _tn(bg, wst[:, gl])
        y = jnp.concatenate(ys, axis=1)
        y_ref[...] = y
        _, outs, _, _ = _gate_norm(y, z_ref[...], nw_ref[...])
        yn_ref[...] = jnp.concatenate(outs, axis=1).astype(bf16)

    def rows(w):
        return pl.BlockSpec((CHUNK, w), lambda b, c: (b * n_chunk + c, 0))

    def par(w):
        return pl.BlockSpec((1, w), lambda b, c: (0, 0))

    return pl.pallas_call(
        body, name=name, grid=(t // SEQ, n_chunk),
        in_specs=[rows(CONV_CH), rows(SSM_INNER), rows(LANES), par(LANES), par(LANES), par(SSM_INNER), par(SSM_INNER)],
        out_specs=[rows(SSM_INNER), rows(SSM_INNER), pl.BlockSpec((1, SSM_GROUPS, D_STATE, GROUP_W), lambda b, c: (b * n_chunk + c, 0, 0, 0))],
        out_shape=[jax.ShapeDtypeStruct((t, SSM_INNER), bf16), jax.ShapeDtypeStruct((t, SSM_INNER), f32),
                   jax.ShapeDtypeStruct((t // CHUNK, SSM_GROUPS, D_STATE, GROUP_W), f32)],
        scratch_shapes=[pltpu.VMEM((SSM_GROUPS, D_STATE, GROUP_W), f32)],
        compiler_params=_cparams(("parallel", "arbitrary")),
    )(xbc, z, dtp, *params)


def _ssd_bwd(xbc, z, dtp, y, hs, dyn, params, name):
    t = xbc.shape[0]
    n_chunk = SEQ // CHUNK

    def body(xbc_ref, z_ref, dt_ref, y_ref, hs_ref, dyn_ref, bias_ref, alog_ref, dskip_ref, nw_ref,
             dxbc_ref, dz_ref, ddt_ref, dnw_ref, dds_ref, dal_ref, dbi_ref, dh_scr):
        @pl.when(pl.program_id(1) == 0)
        def _():
            dh_scr[...] = jnp.zeros_like(dh_scr)

        q = _ssd_common(xbc_ref, dt_ref, bias_ref, alog_ref)
        low = lax.broadcasted_iota(jnp.int32, (CHUNK, LANES), 1) < HEAD_DIM
        last_row = lax.broadcasted_iota(jnp.int32, (CHUNK, GROUP_W), 0) == CHUNK - 1
        xs, xg = q["xs"], q["xg"]
        xgb = xg.astype(bf16)
        wf = xg * q["dse"]
        wst = wf.astype(bf16)
        zz = z_ref[...]
        yy = y_ref[...]
        sz, dsz = _silu_and_grad(zz)
        y2, _, xhats, rs = _gate_norm(yy, zz, nw_ref[...], gate=sz)
        dyn_ = dyn_ref[...]
        dy2s, dnws = [], []
        for g in range(SSM_GROUPS):
            gl = slice(g * GROUP_W, (g + 1) * GROUP_W)
            gw = dyn_[:, gl] * nw_ref[:, gl]
            dy2s.append(rs[g] * (gw - xhats[g] * jnp.mean(gw * xhats[g], axis=-1, keepdims=True)))
            dnws.append(_rowsum8(dyn_[:, gl] * xhats[g]))
        dy2 = jnp.concatenate(dy2s, axis=1)
        dy = dy2 * sz
        dz_ref[...] = (dy2 * yy * dsz).astype(bf16)
        dnw_p = jnp.concatenate(dnws, axis=1)
        dds_p = _rowsum8(dy * xs)
        dyb = dy.astype(bf16)
        gfull = (dy * q["ecs"]).astype(bf16)
        dcs_c = jnp.zeros((CHUNK, CHUNK), f32)
        dcs_r = jnp.zeros((CHUNK, CHUNK), f32)
        dcs_e_parts, dxg_parts = [], []
        for g in range(SSM_GROUPS):
            gl = slice(g * GROUP_W, (g + 1) * GROUP_W)
            bsl = slice(SSM_INNER + g * D_STATE, SSM_INNER + (g + 1) * D_STATE)
            csl = slice(SSM_INNER + SSM_GROUPS * D_STATE + g * D_STATE, SSM_INNER + SSM_GROUPS * D_STATE + (g + 1) * D_STATE)
            bg = xbc_ref[:, bsl].astype(bf16)
            cg = xbc_ref[:, csl].astype(bf16)
            cb = _nt(cg, bg)
            hg = hs_ref[0, g]
            hgb = hg.astype(bf16)
            dhn = dh_scr[g]
            dhnb = dhn.astype(bf16)
            yoff = _nn(cg, hgb) * q["ecs"][:, gl]
            dw_ = _nn(bg, dhnb)
            r_e = dw_ * wf[:, gl]
            to_last = jnp.sum(r_e, axis=0, keepdims=True) + jnp.sum(dhn * hg, axis=0, keepdims=True) * q["cde"][:, gl]
            dcs_e_parts.append(dy[:, gl] * yoff - r_e + jnp.where(last_row, to_last, 0.0))
            dcb = jnp.zeros((CHUNK, CHUNK), f32)
            dxg_pairs = []
            for i in range(HEADS_PER_GROUP // 2):
                h0 = g * HEADS_PER_GROUP + 2 * i
                psl = slice(h0 * HEAD_DIM, (h0 + 2) * HEAD_DIM)
                xp = xgb[:, psl]
                dyp = dyb[:, psl]
                zero = jnp.zeros_like(dyp)
                tns = []
                for a in range(2):
                    h = h0 + a
                    lm = _decay_mat(q, h)
                    m = cb * lm
                    dm = _nt(jnp.where(low, dyp, zero) if a == 0 else jnp.where(low, zero, dyp), xp)
                    dcb = dcb + dm * lm
                    nmat = dm * m
                    dcs_c = dcs_c + jnp.where(q["cidx"] == h, jnp.sum(nmat, axis=1, keepdims=True), 0.0)
                    dcs_r = dcs_r + jnp.where(q["r"] == h, jnp.sum(nmat, axis=0, keepdims=True), 0.0)
                    tns.append(_tn(m.astype(bf16), dyp))
                dxg_pairs.append(jnp.where(low, tns[0], tns[1]))
            dxg_parts.append(jnp.concatenate(dxg_pairs, axis=1) + dw_ * q["dse"][:, gl])
            dcbb = dcb.astype(bf16)
            dxbc_ref[:, csl] = _nt(gfull[:, gl], hgb) + _nn(dcbb, bg)
            dxbc_ref[:, bsl] = _nt(wst[:, gl], dhnb) + _tn(dcbb, cg)
            dh_scr[g] = dhn * q["cde"][:, gl] + _tn(cg, gfull[:, gl])
        dxg = jnp.concatenate(dxg_parts, axis=1)
        dcs_e = jnp.concatenate(dcs_e_parts, axis=1)
        dxbc_ref[:, 0:SSM_INNER] = dskip_ref[...] * dy + dxg * q["dt_e"]
        dcs = dcs_c - dcs_r.T + _dot_exact(dcs_e, q["expand"], ((1,), (1,)))
        triu = (q["cidx"] >= q["r"]).astype(bf16)
        da = _dot_exact(dcs, triu, ((1,), (0,)), x_is_lhs=False)
        ddt = _dot_exact(dxg * xs, q["expand"], ((1,), (1,))) + da * q["a_neg"]
        ddtp = jnp.where(q["head_lane"], ddt * _sigmoid(q["dtp"]), 0.0)
        ddt_ref[...] = ddtp.astype(bf16)
        dal_p = _rowsum8(da * q["dt"]) * q["a_neg"]
        dbi_p = _rowsum8(ddtp)
        first = (pl.program_id(0) == 0) & (pl.program_id(1) == 0)

        @pl.when(first)
        def _():
            dnw_ref[...] = dnw_p
            dds_ref[...] = dds_p
            dal_ref[...] = dal_p
            dbi_ref[...] = dbi_p

        @pl.when(jnp.logical_not(first))
        def _():
            dnw_ref[...] += dnw_p
            dds_ref[...] += dds_p
            dal_ref[...] += dal_p
            dbi_ref[...] += dbi_p

    def rows(w):
        return pl.BlockSpec((CHUNK, w), lambda b, c: (b * n_chunk + n_chunk - 1 - c, 0))

    def par(w):
        return pl.BlockSpec((1, w), lambda b, c: (0, 0))

    def acc(w):
        return pl.BlockSpec((SUBLANES, w), lambda b, c: (0, 0))

    return pl.pallas_call(
        body, name=name, grid=(t // SEQ, n_chunk),
        in_specs=[rows(CONV_CH), rows(SSM_INNER), rows(LANES), rows(SSM_INNER),
                  pl.BlockSpec((1, SSM_GROUPS, D_STATE, GROUP_W), lambda b, c: (b * n_chunk + n_chunk - 1 - c, 0, 0, 0)),
                  rows(SSM_INNER), par(LANES), par(LANES), par(SSM_INNER), par(SSM_INNER)],
        out_specs=[rows(CONV_CH), rows(SSM_INNER), rows(LANES), acc(SSM_INNER), acc(SSM_INNER), acc(LANES), acc(LANES)],
        out_shape=[jax.ShapeDtypeStruct((t, CONV_CH), f32), jax.ShapeDtypeStruct((t, SSM_INNER), bf16), jax.ShapeDtypeStruct((t, LANES), bf16),
                   jax.ShapeDtypeStruct((SUBLANES, SSM_INNER), f32), jax.ShapeDtypeStruct((SUBLANES, SSM_INNER), f32),
                   jax.ShapeDtypeStruct((SUBLANES, LANES), f32), jax.ShapeDtypeStruct((SUBLANES, LANES), f32)],
        scratch_shapes=[pltpu.VMEM((SSM_GROUPS, D_STATE, GROUP_W), f32)],
        compiler_params=_cparams(("arbitrary", "arbitrary")),
    )(xbc, z, dtp, y, hs, dyn, *params)


def _adamw_update(g, w, m, v):
    mm = ADAM_B1 * m + (1.0 - ADAM_B1) * g
    vv = ADAM_B2 * v + (1.0 - ADAM_B2) * (g * g)
    m_hat = mm / (1.0 - ADAM_B1 ** ADAM_STEP)
    v_hat = vv / (1.0 - ADAM_B2 ** ADAM_STEP)
    return -ADAM_LR * (m_hat / (jnp.sqrt(v_hat) + ADAM_EPS) + ADAM_WD * w), mm, vv


def _adamw(g_parts, w, m, v, name):
    rows, width = w.shape
    n = len(g_parts)
    tr = _row_tile(rows)

    def body(*refs):
        g_refs, (w_ref, m_ref, v_ref, g_out, d_out, m_out, v_out) = refs[:n], refs[n:]
        g = g_refs[0][...].astype(f32)
        for r in g_refs[1:]:
            g = g + r[...].astype(f32)
        g_out[...] = g
        d_out[...], m_out[...], v_out[...] = _adamw_update(g, w_ref[...], m_ref[...], v_ref[...])

    spec = pl.BlockSpec((tr, width), lambda i: (i, 0))
    return pl.pallas_call(
        body, name=name, grid=(rows // tr,), in_specs=[spec] * (n + 3), out_specs=[spec] * 4,
        out_shape=[jax.ShapeDtypeStruct((rows, width), f32)] * 4, compiler_params=_cparams(("parallel",)),
    )(*g_parts, w, m, v)


def _adamw_layers(landed, w, m, v, after, name):
    depth, rows, width = w.shape
    tr = _row_tile(rows)
    n_i = rows // tr

    def body(*refs):
        part_refs, (w_ref, m_ref, v_ref, _, g_out, d_out, m_out, v_out) = refs[:depth * N_DEV], refs[depth * N_DEV:]
        for l in range(depth):
            @pl.when(pl.program_id(0) == l)
            def _(l=l):
                g = part_refs[l * N_DEV][0].astype(f32)
                for r in part_refs[l * N_DEV + 1:(l + 1) * N_DEV]:
                    g = g + r[0].astype(f32)
                g_out[0] = g
                d_out[0], m_out[0], v_out[0] = _adamw_update(g, w_ref[0], m_ref[0], v_ref[0])

    def part_spec(l, p):
        return pl.BlockSpec((1, tr, width), lambda ll, i: (p, jnp.where(ll == l, i, jnp.where(ll < l, 0, n_i - 1)), 0))

    state = pl.BlockSpec((1, tr, width), lambda ll, i: (ll, i, 0))
    return pl.pallas_call(
        body, name=name, grid=(depth, n_i),
        in_specs=[part_spec(l, p) for l in range(depth) for p in range(N_DEV)] + [state] * 3 + [ANY], out_specs=[state] * 4,
        out_shape=[jax.ShapeDtypeStruct(w.shape, f32)] * 4, compiler_params=_cparams(("arbitrary", "arbitrary")),
    )(*[landed[l] for l in range(depth) for _ in range(N_DEV)], w, m, v, after)


def _row_tile(rows, cap=512):
    for cand in range(min(rows, cap) // SUBLANES * SUBLANES, 0, -SUBLANES):
        if rows % cand == 0:
            return cand
    return rows


def _cols_from_devices(g, width, name):
    n_dev, depth, a, b = g.shape

    def body(g_ref, o_ref):
        for i in range(n_dev):
            o_ref[0, :, i * b:(i + 1) * b] = g_ref[i, 0]
        if width > n_dev * b:
            o_ref[0, :, n_dev * b:width] = jnp.zeros((a, width - n_dev * b), o_ref.dtype)

    return pl.pallas_call(
        body, name=name, grid=(depth,), in_specs=[pl.BlockSpec((n_dev, 1, a, b), lambda l: (0, l, 0, 0))],
        out_specs=pl.BlockSpec((1, a, width), lambda l: (l, 0, 0)), out_shape=jax.ShapeDtypeStruct((depth, a, width), g.dtype),
        compiler_params=_cparams(("parallel",)),
    )(g)


def _devices_from_cols(per_layer, b, name, tr=256):
    depth = len(per_layer)
    a, width = per_layer[0].shape

    def body(*refs):
        o_ref = refs[depth]
        for l in range(depth):
            for i in range(N_DEV):
                o_ref[i, l] = refs[l][:, i * b:(i + 1) * b]

    return pl.pallas_call(
        body, name=name, grid=(a // tr,), in_specs=[pl.BlockSpec((tr, width), lambda r: (r, 0))] * depth,
        out_specs=pl.BlockSpec((N_DEV, depth, tr, b), lambda r: (0, 0, r, 0)),
        out_shape=jax.ShapeDtypeStruct((N_DEV, depth, a, b), per_layer[0].dtype), compiler_params=_cparams(("parallel",)),
    )(*per_layer)


def _me():
    return lax.axis_index("x"), lax.axis_index("y"), lax.axis_index("c")


def _allgather_two_level(shards, name):
    n = len(shards)
    per = 7

    def body(*refs):
        ins, outs, token = refs[:n], refs[n:2 * n], refs[2 * n]
        send_sems, recv_sems, local_sems = refs[2 * n + 1:]
        token[...] = jnp.zeros_like(token)
        x, y, c = _me()
        me, sibling = (x, y, c), (x, y, 1 - c)
        chips = [(1 - x, y), (x, 1 - y), (1 - x, 1 - y)]

        def slot(a, p):
            return outs[a].at[4 * p[0] + 2 * p[1] + p[2]]

        def copy(a, k, block, to, src=None):
            return pltpu.make_async_remote_copy(
                src_ref=slot(a, block) if src is None else src, dst_ref=slot(a, block),
                send_sem=send_sems.at[a * per + k], recv_sem=recv_sems.at[a * per + k], device_id=to, device_id_type=MESH)

        mine = [pltpu.make_async_copy(ins[a], slot(a, me), local_sems.at[a]) for a in range(n)]
        for cp in mine:
            cp.start()
        first = []
        for a in range(n):
            first.append(copy(a, 0, me, sibling, src=ins[a]))
            first += [copy(a, 1 + j, me, (*chip, c), src=ins[a]) for j, chip in enumerate(chips)]
        for cp in first:
            cp.start()
        passed = []
        for j, chip in enumerate(chips):
            for a in range(n):
                copy(a, 1 + j, (*chip, c), me).wait_recv()
                fwd = copy(a, 4 + j, (*chip, c), sibling)
                fwd.start()
                passed.append(fwd)
        for a in range(n):
            copy(a, 0, sibling, me).wait_recv()
            for j, chip in enumerate(chips):
                copy(a, 4 + j, (*chip, 1 - c), me).wait_recv()
        for cp in first + passed:
            cp.wait_send()
        for cp in mine:
            cp.wait()

    outs = pl.pallas_call(
        body, name=name, in_specs=[ANY] * n, out_specs=[ANY] * n + [pl.BlockSpec(memory_space=pltpu.VMEM)],
        out_shape=[jax.ShapeDtypeStruct((N_DEV,) + s.shape, s.dtype) for s in shards] + [jax.ShapeDtypeStruct((SUBLANES, LANES), f32)],
        scratch_shapes=[pltpu.SemaphoreType.DMA((n * per,)), pltpu.SemaphoreType.DMA((n * per,)), pltpu.SemaphoreType.DMA((n,))],
    )(*shards)
    return outs[:n], outs[n]


def _allgather_direct(row, name):
    def body(in_ref, out_ref, send_sems, recv_sems, local_sem):
        x, y, c = _me()
        mine = out_ref.at[4 * x + 2 * y + c]
        local = pltpu.make_async_copy(in_ref, mine, local_sem)
        local.start()
        sends = []
        for k in range(1, N_DEV):
            px, py, pc = x ^ (k >> 2), y ^ ((k >> 1) & 1), c ^ (k & 1)
            sends.append(pltpu.make_async_remote_copy(
                src_ref=in_ref, dst_ref=mine, send_sem=send_sems.at[k - 1], recv_sem=recv_sems.at[k - 1],
                device_id=(px, py, pc), device_id_type=MESH))
        for cp in sends:
            cp.start()
        for k in range(1, N_DEV):
            px, py, pc = x ^ (k >> 2), y ^ ((k >> 1) & 1), c ^ (k & 1)
            theirs = out_ref.at[4 * px + 2 * py + pc]
            pltpu.make_async_remote_copy(
                src_ref=in_ref, dst_ref=theirs, send_sem=send_sems.at[k - 1], recv_sem=recv_sems.at[k - 1],
                device_id=(px, py, pc), device_id_type=MESH).wait_recv()
        for cp in sends:
            cp.wait_send()
        local.wait()

    return pl.pallas_call(
        body, name=name, in_specs=[ANY], out_specs=ANY, out_shape=jax.ShapeDtypeStruct((N_DEV,) + row.shape, row.dtype),
        scratch_shapes=[pltpu.SemaphoreType.DMA((N_DEV - 1,)), pltpu.SemaphoreType.DMA((N_DEV - 1,)), pltpu.SemaphoreType.DMA],
    )(row)


N_CHIP = N_DEV // 2
HBM = pl.BlockSpec(memory_space=pltpu.HBM)
SEM = pl.BlockSpec(memory_space=pltpu.SEMAPHORE)
EFFECT = pltpu.SideEffectType.DATAFLOW_SIDE_EFFECTING


def _peer(k):
    x, y, c = _me()
    return x ^ (k >> 2), y ^ ((k >> 1) & 1), c ^ (k & 1)


def _direct_copies(srcs, lands, send_sems, recv_sems, per_peer):
    x, y, c = _me()
    me = 4 * x + 2 * y + c
    copies = []
    for a in range(len(srcs)):
        for k in range(1, N_DEV):
            px, py, pc = _peer(k)
            piece = srcs[a].at[4 * px + 2 * py + pc] if per_peer else srcs[a]
            copies.append(pltpu.make_async_remote_copy(
                src_ref=piece, dst_ref=lands[a].at[me], send_sem=send_sems.at[a * (N_DEV - 1) + k - 1],
                recv_sem=recv_sems.at[a * (N_DEV - 1) + k - 1], device_id=(px, py, pc), device_id_type=MESH))
    return copies


def _direct_start(srcs, lands, per_peer, name):
    n = len(srcs)
    n_sem = n * (N_DEV - 1)

    def body(*refs):
        src_refs, land_refs = refs[:n], refs[n:2 * n]
        send_sems, recv_sems = refs[2 * n], refs[2 * n + 1]
        token = refs[-1]
        for cp in _direct_copies(src_refs, land_refs, send_sems, recv_sems, per_peer):
            cp.start()
        token[...] = jnp.zeros_like(token)

    outs = pl.pallas_call(
        body, name=name,
        out_shape=(pltpu.SemaphoreType.DMA((n_sem,)), pltpu.SemaphoreType.DMA((n_sem,)),
                   *[pltpu.HBM(s.shape, s.dtype) for s in srcs], *[pltpu.HBM(s.shape, s.dtype) for s in lands],
                   jax.ShapeDtypeStruct((SUBLANES, LANES), f32)),
        in_specs=[HBM] * (2 * n), out_specs=(SEM, SEM, *[HBM] * (2 * n), pl.BlockSpec(memory_space=pltpu.VMEM)),
        input_output_aliases={i: 2 + i for i in range(2 * n)},
        compiler_params=pltpu.CompilerParams(has_side_effects=EFFECT),
    )(*[pltpu.with_memory_space_constraint(s, pltpu.HBM) for s in srcs], *[pltpu.with_memory_space_constraint(s, pltpu.HBM) for s in lands])
    return outs[0], outs[1], outs[2:2 + n], outs[2 + n:2 + 2 * n], outs[-1]


def _direct_wait(send_sems, recv_sems, srcs, lands, after, per_peer, name):
    n = len(srcs)

    def body(*refs):
        src_refs, land_refs = refs[:n], refs[n:2 * n]
        s_sems, r_sems = refs[2 * n], refs[2 * n + 1]
        for cp in _direct_copies(src_refs, land_refs, s_sems, r_sems, per_peer):
            cp.wait_send()
            cp.wait_recv()

    outs = pl.pallas_call(
        body, name=name,
        out_shape=tuple(pltpu.HBM(s.shape, s.dtype) for s in list(srcs) + list(lands)),
        in_specs=[HBM] * (2 * n) + [SEM, SEM, ANY], out_specs=tuple([HBM] * (2 * n)),
        input_output_aliases={i: i for i in range(2 * n)},
        compiler_params=pltpu.CompilerParams(has_side_effects=EFFECT),
    )(*srcs, *lands, send_sems, recv_sems, after)
    return outs[n:]


def _row(v, width=None):
    v = v.reshape(1, -1).astype(f32)
    if width is not None and v.shape[1] < width:
        v = jnp.pad(v, ((0, 0), (0, width - v.shape[1])))
    return v


def _layer_params(p, l):
    return dict(
        norm_mix=_row(p["norm_mix"][l]), norm_ffn=_row(p["norm_ffn"][l]), conv_w=p["conv_w"][l], conv_b=_row(p["conv_b"][l]),
        ssd=(_row(p["dt_bias"][l], LANES), _row(p["a_log"][l], LANES), _row(jnp.repeat(p["d_skip"][l], HEAD_DIM)), _row(p["ssm_norm"][l])))


def _layer_fwd(h, w_in, rest, sp, tabs, l):
    tag = f"l{l}_"
    hn = _rmsnorm_fwd(h, sp["norm_mix"], tag + "norm_mix")
    qkv, z, xbc_pre = _in_proj(hn, w_in, (QKV_WIDTH, SSM_INNER, CONV_CH), tag + "proj")
    dtp = _matmul(hn, w_in, mode="nn", n_out=LANES, tn=LANES, b_off=DT_OFF // LANES, name=tag + "proj_dt")
    prep = _attn_prep(qkv, tabs, tag + "attn_prep")
    o, lse = _attn_fwd(prep, tag + "attn_fwd")
    xbc = _conv_fwd(xbc_pre, sp["conv_w"], sp["conv_b"], tag + "conv_fwd")
    yn, y, hs = _ssd_fwd(xbc, z, dtp, sp["ssd"], tag + "ssd_fwd")
    w_out, w_gate, w_up, w_down = rest(yn) if callable(rest) else rest
    h2 = _out_proj(o, yn, w_out, h, tag + "out_proj")
    hn2 = _rmsnorm_fwd(h2, sp["norm_ffn"], tag + "norm_ffn")
    g, u, act = _swiglu_fwd(hn2, w_gate, w_up, tag + "ffn_up")
    h3 = _matmul(act, w_down, mode="nn", tk=1408, add=h2, name=tag + "ffn_down")
    saved = dict(h=h, hn=hn, prep=prep, z=z, xbc_pre=xbc_pre, dtp=dtp, o=o, lse=lse, xbc=xbc, yn=yn, y=y, hs=hs, h2=h2, hn2=hn2, g=g, u=u, act=act,
                 rest=(w_out, w_gate, w_up, w_down))
    return h3, saved


def _layer_bwd(dh3, s, big, sp, tabs, l, gd=f32, after_ffn=None):
    tag = f"l{l}_"
    w_in, w_out, w_gate, w_up, w_down = big
    dg, du = _swiglu_bwd(dh3, w_down, s["g"], s["u"], tag + "ffn_down_bwd")
    dw_down = _matmul(s["act"], dh3, mode="tn", tm=1408, tn=512, tk=2048, out_dtype=gd, name=tag + "dw_down")
    dw_gate = _matmul(s["hn2"], dg, mode="tn", tm=512, tn=1408, tk=2048, out_dtype=gd, name=tag + "dw_gate")
    dw_up = _matmul(s["hn2"], du, mode="tn", tm=512, tn=1408, tk=2048, out_dtype=gd, name=tag + "dw_up")
    norm_ffn = sp["norm_ffn"] if after_ffn is None else sp["norm_ffn"] + after_ffn(dict(w_gate=dw_gate, w_up=dw_up, w_down=dw_down))
    dh2, dnf = _nt_norm_bwd([(dg, w_gate), (du, w_up)], s["h2"], norm_ffn, dh3, tag + "ffn_up_bwd_norm", tk=1408)
    d_o = _matmul(dh2, w_out, mode="nt", n_out=ATTN_WIDTH, tn=512, b_off=0, name=tag + "out_attn_bwd")
    dyn = _matmul(dh2, w_out, mode="nt", n_out=SSM_INNER, tn=512, b_off=1, name=tag + "out_ssm_bwd")
    dw_out = jnp.concatenate([_matmul(s["o"], dh2, mode="tn", tm=512, tn=512, tk=2048, out_dtype=gd, name=tag + "dw_out_attn"),
                              _matmul(s["yn"], dh2, mode="tn", tm=512, tn=512, tk=2048, out_dtype=gd, name=tag + "dw_out_ssm")], axis=0)
    dxbc, dz, ddtp, dnw, dds, dal, dbi = _ssd_bwd(s["xbc"], s["z"], s["dtp"], s["y"], s["hs"], dyn, sp["ssd"], tag + "ssd_bwd")
    dxbc_pre, dconv_w, dconv_b = _conv_bwd(s["xbc_pre"], sp["conv_w"], sp["conv_b"], dxbc, tag + "conv_bwd")
    dq, dk, dv = _attn_bwd(s["prep"], tabs, s["o"], s["lse"], d_o, tag + "attn_bwd")
    dproj = jnp.concatenate([dq, dk, dv, dz, dxbc_pre, ddtp], axis=1)
    dw_in = _matmul(s["hn"], dproj, mode="tn", tm=512, tn=1152, tk=2048, out_dtype=gd, name=tag + "dw_in")
    dh, dnm = _nt_norm_bwd([(dproj, w_in)], s["h"], sp["norm_mix"], dh2, tag + "proj_bwd_norm", tk=1152)
    grads = dict(
        norm_mix=dnm.sum(0), w_in=dw_in, conv_w=dconv_w, conv_b=dconv_b[0], dt_bias=dbi.sum(0)[:SSM_HEADS], a_log=dal.sum(0)[:SSM_HEADS],
        d_skip=dds.sum(0).reshape(SSM_HEADS, HEAD_DIM).sum(1), ssm_norm=dnw.sum(0), w_out=dw_out, norm_ffn=dnf.sum(0),
        w_gate=dw_gate, w_up=dw_up, w_down=dw_down)
    return dh, grads


def _local_step(x, positions, target, p, bigs):
    tabs = _rope_tables(positions.reshape(-1, 1), "rope_tables")
    h = x
    saved, sps = [], []
    for l in range(DEPTH):
        sps.append(_layer_params(p, l))
        h, s = _layer_fwd(h, bigs[l][0], bigs[l][1:], sps[l], tabs, l)
        saved.append(s)
    dh, loss_parts, dfn = _final_loss(h, _row(p["final_norm"]), target, "final_loss")
    layer_grads = [None] * DEPTH
    for l in reversed(range(DEPTH)):
        dh, layer_grads[l] = _layer_bwd(dh, saved[l], bigs[l], sps[l], tabs, l)
    grads = {k: [layer_grads[l][k] for l in range(DEPTH)] for k in layer_grads[0]}
    grads["final_norm"] = dfn.sum(0)
    return jnp.sum(loss_parts), dh, grads


BIG = ("w_in", "w_out", "w_gate", "w_up", "w_down")
REST = BIG[1:]
FFN = ("w_gate", "w_up", "w_down")
MIX = ("w_in", "w_out")
COL_SHARDED = ("w_in", "w_gate", "w_up")
SMALL = ("norm_mix", "conv_b", "dt_bias", "a_log", "d_skip", "ssm_norm", "norm_ffn", "final_norm")
WEIGHTS = ("norm_mix", "w_in", "conv_w", "conv_b", "dt_bias", "a_log", "d_skip", "ssm_norm", "w_out", "norm_ffn", "w_gate", "w_up", "w_down", "final_norm")
SMALL_ROWS = 88
CONVW_ROWS = 96
CONVW_SHARD_ROWS = 16


def _full_from_gathered(name, g, l):
    _, a, b = g.shape
    if name in COL_SHARDED:
        width = IN_PROJ_PAD if name == "w_in" else N_DEV * b
        return _cols_from_devices(g.reshape(N_DEV, 1, a, b), width, f"cols_l{l}_{name}").reshape(a, width)
    return g.reshape(N_DEV * a, b)


def _by_device(name, full, shard_shape, l):
    a, b = shard_shape
    if name in COL_SHARDED:
        return _devices_from_cols([full], b, f"devs_l{l}_{name}").reshape(N_CHIP, 2, a, b)
    return full.reshape(N_CHIP, 2, a, b)


def _pack_rows(parts, rows, width):
    flat = jnp.concatenate([q.reshape(-1) for q in parts])
    return jnp.pad(flat, (0, rows * width - flat.shape[0])).reshape(rows, width)


def _unpack(flat, like):
    out, off = [], 0
    for q in like:
        out.append(flat[off:off + q.size].reshape(q.shape))
        off += q.size
    return out


def kernel(x, positions, norm_mix, w_in, conv_w, conv_b, dt_bias, a_log, d_skip, ssm_norm, w_out, norm_ffn, w_gate, w_up, w_down, final_norm, loss_target, m_norm_mix, m_w_in, m_conv_w, m_conv_b, m_dt_bias, m_a_log, m_d_skip, m_ssm_norm, m_w_out, m_norm_ffn, m_w_gate, m_w_up, m_w_down, m_final_norm, v_norm_mix, v_w_in, v_conv_w, v_conv_b, v_dt_bias, v_a_log, v_d_skip, v_ssm_norm, v_w_out, v_norm_ffn, v_w_gate, v_w_up, v_w_down, v_final_norm):
    w = dict(norm_mix=norm_mix, w_in=w_in, conv_w=conv_w, conv_b=conv_b, dt_bias=dt_bias, a_log=a_log, d_skip=d_skip, ssm_norm=ssm_norm,
             w_out=w_out, norm_ffn=norm_ffn, w_gate=w_gate, w_up=w_up, w_down=w_down, final_norm=final_norm)
    m = dict(norm_mix=m_norm_mix, w_in=m_w_in, conv_w=m_conv_w, conv_b=m_conv_b, dt_bias=m_dt_bias, a_log=m_a_log, d_skip=m_d_skip,
             ssm_norm=m_ssm_norm, w_out=m_w_out, norm_ffn=m_norm_ffn, w_gate=m_w_gate, w_up=m_w_up, w_down=m_w_down, final_norm=m_final_norm)
    v = dict(norm_mix=v_norm_mix, w_in=v_w_in, conv_w=v_conv_w, conv_b=v_conv_b, dt_bias=v_dt_bias, a_log=v_a_log, d_skip=v_d_skip,
             ssm_norm=v_ssm_norm, w_out=v_w_out, norm_ffn=v_norm_ffn, w_gate=v_w_gate, w_up=v_w_up, w_down=v_w_down, final_norm=v_final_norm)
    ax, ay, ac = lax.axis_index("x"), lax.axis_index("y"), lax.axis_index("c")
    dev = 4 * ax + 2 * ay + ac

    assert DEPTH == 2
    t = x.shape[0] * x.shape[1]
    xf, target = x.reshape(t, D_MODEL), loss_target.reshape(t, D_MODEL)

    def own_slot(block):
        return lax.dynamic_update_slice(lax.empty((N_DEV,) + block.shape[1:], block.dtype), block, (dev,) + (0,) * (block.ndim - 1))

    def gather_start(keys, l, tie, name):
        shards = [(w[keys[0]][l] + tie).astype(bf16)] + [w[k][l].astype(bf16) for k in keys[1:]]
        return _direct_start(shards, [own_slot(s[None]) for s in shards], False, name)

    def scatter_start(keys, grads_l, l, name):
        by_dev = [_by_device(k, grads_l[k], w[k].shape[1:], l).reshape((N_DEV,) + w[k].shape[1:]) for k in keys]
        return _direct_start(by_dev, [own_slot(lax.dynamic_slice_in_dim(g, dev, 1, 0)) for g in by_dev], True, name)

    (g_in0, conv_all), tie = _allgather_two_level([w["w_in"][0].astype(bf16), w["conv_w"]], "gather_l0_w_in")
    rest0_copy = gather_start(REST, 0, tie[0, 0], "gather_l0_rest_start")
    l1_copy = gather_start(BIG, 1, rest0_copy[4][0, 0], "gather_l1_start")
    p = {k: w[k] for k in SMALL}
    p["norm_mix"] = p["norm_mix"] + l1_copy[4][0, 0]
    p["conv_w"] = jnp.transpose(conv_all, (1, 2, 0, 3)).reshape(DEPTH, CONV_WIDTH, CONV_CH)
    sp0, sp1 = _layer_params(p, 0), _layer_params(p, 1)

    def rest0(after):
        lands = _direct_wait(*rest0_copy[:4], after, False, "gather_l0_rest_wait")
        return tuple(_full_from_gathered(k, g, 0) for k, g in zip(REST, lands))

    tabs = _rope_tables(positions.reshape(t, 1), "rope_tables")
    w_in0 = _full_from_gathered("w_in", g_in0, 0)
    h1, saved0 = _layer_fwd(xf, w_in0, rest0, sp0, tabs, 0)
    lands1 = _direct_wait(*l1_copy[:4], h1, False, "gather_l1_wait")
    bigs1 = tuple(_full_from_gathered(k, g, 1) for k, g in zip(BIG, lands1))
    h2, saved1 = _layer_fwd(h1, bigs1[0], bigs1[1:], sp1, tabs, 1)
    dh, loss_parts, dfn = _final_loss(h2, _row(p["final_norm"]), target, "final_loss")
    loss_local = jnp.sum(loss_parts)

    dh, grads1 = _layer_bwd(dh, saved1, bigs1, sp1, tabs, 1, gd=bf16)
    l1_grads = scatter_start(BIG, grads1, 1, "scatter_l1_start")
    w_out0, w_gate0, w_up0, w_down0 = saved0["rest"]
    bigs0 = (w_in0, w_out0, w_gate0, w_up0, w_down0 + l1_grads[4][0, 0].astype(bf16))
    ffn0_grads = []

    def after_ffn(grads_ffn):
        ffn0_grads.append(scatter_start(FFN, grads_ffn, 0, "scatter_l0_ffn_start"))
        return ffn0_grads[0][4][0, 0]

    dx, grads0 = _layer_bwd(dh, saved0, bigs0, sp0, tabs, 0, gd=bf16, after_ffn=after_ffn)
    mix0_grads = scatter_start(MIX, grads0, 0, "scatter_l0_mix_start")
    landed = {(k, 1): g for k, g in zip(BIG, _direct_wait(*l1_grads[:4], dx, True, "scatter_l1_wait"))}
    landed.update({(k, 0): g for k, g in zip(FFN, _direct_wait(*ffn0_grads[0][:4], dx, True, "scatter_l0_ffn_wait"))})
    out_g, out_d, out_m, out_v = {}, {}, {}, {}

    def update(keys, after):
        for k in keys:
            res = _adamw_layers([landed[k, l] for l in range(DEPTH)], w[k], m[k], v[k], after, "adamw_" + k)
            for dst, r in zip((out_g, out_d, out_m, out_v), res):
                dst[k] = r

    update(FFN, mix0_grads[4])
    grads = {k: [grads0[k], grads1[k]] for k in grads0 if k not in BIG}
    grads["final_norm"] = dfn.sum(0) + mix0_grads[4][0, 0]

    small_like = [w[k] for k in SMALL]
    small_grads = [jnp.stack(grads[k]) if k != "final_norm" else grads[k] for k in SMALL]
    small_pack = jnp.concatenate([_pack_rows(small_grads, SMALL_ROWS, LANES), _pack_rows([jnp.stack(grads["conv_w"])], CONVW_ROWS, LANES)], axis=0)
    parts = _allgather_direct(small_pack, "gather_small_grads")
    g_s, d_s, m_s, v_s = _adamw(
        [parts[i, :SMALL_ROWS] for i in range(N_DEV)], _pack_rows(small_like, SMALL_ROWS, LANES),
        _pack_rows([m[k] for k in SMALL], SMALL_ROWS, LANES), _pack_rows([v[k] for k in SMALL], SMALL_ROWS, LANES), "adamw_replicated")
    for dst, src in ((out_g, g_s), (out_d, d_s), (out_m, m_s), (out_v, v_s)):
        dst.update(zip(SMALL, _unpack(src.reshape(-1), small_like)))
    shard_w = conv_w.shape[-1]
    conv_parts = parts[:, SMALL_ROWS:].reshape(N_DEV, DEPTH, CONV_WIDTH, CONV_CH)
    conv_mine = lax.dynamic_slice_in_dim(conv_parts, dev * shard_w, shard_w, axis=3)
    g_c, d_c, m_c, v_c = _adamw(
        [_pack_rows([conv_mine[i]], CONVW_SHARD_ROWS, LANES) for i in range(N_DEV)], _pack_rows([conv_w], CONVW_SHARD_ROWS, LANES),
        _pack_rows([m["conv_w"]], CONVW_SHARD_ROWS, LANES), _pack_rows([v["conv_w"]], CONVW_SHARD_ROWS, LANES), "adamw_conv_w")
    for dst, src in ((out_g, g_c), (out_d, d_c), (out_m, m_c), (out_v, v_c)):
        dst["conv_w"] = src.reshape(-1)[:conv_w.size].reshape(conv_w.shape)

    landed.update({(k, 0): g for k, g in zip(MIX, _direct_wait(*mix0_grads[:4], v_c + out_v["w_down"][0, :CONVW_SHARD_ROWS, :LANES], True, "scatter_l0_mix_wait"))})
    update(MIX, v_c)

    loss = lax.psum(loss_local, ("x", "y", "c"))
    return (loss, dx.reshape(x.shape), *[out_g[k] for k in WEIGHTS], *[out_d[k] for k in WEIGHTS],
            *[out_m[k] for k in WEIGHTS], *[out_v[k] for k in WEIGHTS])
```

```python
import jax
import jax.numpy as jnp
import numpy as np
from jax import lax
from jax.experimental import pallas as pl
from jax.experimental.pallas import tpu as pltpu

f32 = jnp.float32
bf16 = jnp.bfloat16

D_MODEL = 1024
SEQ = 2048
DEPTH = 2
HEAD_DIM = 64
N_ATTN_HEADS = 8
N_KV_HEADS = 2
ATTN_WIDTH = 512
KV_WIDTH = 128
ROPE_DIM = 16
ROPE_THETA = 500000.0
DILATIONS = (1, 4, 16)
ATTN_BLOCK = 128
SSM_HEADS = 16
SSM_INNER = 1024
SSM_GROUPS = 2
D_STATE = 128
CONV_WIDTH = 4
CHUNK = 128
CONV_CH = 1536
MIX_WIDTH = 1536
QKV_WIDTH = ATTN_WIDTH + 2 * KV_WIDTH
DT_OFF = 3328
IN_PROJ = 3344
IN_PROJ_PAD = 3456
FFN_HIDDEN = 2816
EPS = 1e-5
N_DEV = 8
ADAM_LR = 0.001
ADAM_B1 = 0.9
ADAM_B2 = 0.999
ADAM_EPS = 1e-08
ADAM_WD = 0.01
ADAM_STEP = 10

LANES = 128
SUBLANES = 8
VMEM_LIMIT = 56 * 1024 * 1024

MESH = pl.DeviceIdType.MESH
ANY = pl.BlockSpec(memory_space=pl.ANY)


def _cparams(sem, vmem=None):
    return pltpu.CompilerParams(dimension_semantics=sem, vmem_limit_bytes=vmem or VMEM_LIMIT)


def _sigmoid(x):
    return 1.0 / (1.0 + jnp.exp(-x))


def _silu(x):
    return x * _sigmoid(x)


def _dsilu(x):
    s = _sigmoid(x)
    return s * (1.0 + x * (1.0 - s))


def _silu_and_grad(x):
    s = _sigmoid(x)
    return x * s, s * (1.0 + x * (1.0 - s))


def _softplus(x):
    return jnp.maximum(x, 0.0) + jnp.log(1.0 + jnp.exp(-jnp.abs(x)))


def _dot(a, b, dims, precision=None):
    return lax.dot_general(a, b, (dims, ((), ())), preferred_element_type=f32, precision=precision)


def _nn(a, b, precision=None):
    return _dot(a, b, ((1,), (0,)), precision)


def _nt(a, b):
    return _dot(a, b, ((1,), (1,)))


def _tn(a, b):
    return _dot(a, b, ((0,), (0,)))


def _rowsum8(t):
    n, w = t.shape
    return jnp.sum(t.reshape(n // SUBLANES, SUBLANES, w), axis=0)


def _matmul(a, b, *, mode, n_out=None, b_off=0, add=None, out_dtype=f32, tm=2048, tn=512, tk=1024, name):
    if mode == "tn":
        kk, m = a.shape
    else:
        m, kk = a.shape
    n = n_out if n_out is not None else (b.shape[0] if mode == "nt" else b.shape[1])
    tm, tn, tk = min(tm, m), min(tn, n), min(tk, kk)
    assert m % tm == 0 and n % tn == 0 and kk % tk == 0, (name, m, n, kk, tm, tn, tk)
    nk = kk // tk
    if mode == "nn":
        a_spec = pl.BlockSpec((tm, tk), lambda i, j, k: (i, k))
        b_spec = pl.BlockSpec((tk, tn), lambda i, j, k: (k, j + b_off))
        dims = ((1,), (0,))
    elif mode == "nt":
        a_spec = pl.BlockSpec((tm, tk), lambda i, j, k: (i, k))
        b_spec = pl.BlockSpec((tn, tk), lambda i, j, k: (j + b_off, k))
        dims = ((1,), (1,))
    else:
        a_spec = pl.BlockSpec((tk, tm), lambda i, j, k: (k, i))
        b_spec = pl.BlockSpec((tk, tn), lambda i, j, k: (k, j + b_off))
        dims = ((0,), (0,))
    o_spec = pl.BlockSpec((tm, tn), lambda i, j, k: (i, j))
    has_add = add is not None

    def body(*refs):
        if has_add:
            a_ref, b_ref, add_ref, o_ref, acc_ref = refs
        else:
            a_ref, b_ref, o_ref, acc_ref = refs
        k = pl.program_id(2)
        part = _dot(a_ref[...].astype(bf16), b_ref[...].astype(bf16), dims)

        @pl.when(k == 0)
        def _():
            acc_ref[...] = part

        @pl.when(k > 0)
        def _():
            acc_ref[...] += part

        @pl.when(k == nk - 1)
        def _():
            r = acc_ref[...]
            if has_add:
                r = r + add_ref[...]
            o_ref[...] = r.astype(out_dtype)

    in_specs = [a_spec, b_spec] + ([o_spec] if has_add else [])
    args = (a, b) + ((add,) if has_add else ())
    return pl.pallas_call(
        body, name=name, grid=(m // tm, n // tn, nk), in_specs=in_specs, out_specs=o_spec,
        out_shape=jax.ShapeDtypeStruct((m, n), out_dtype), scratch_shapes=[pltpu.VMEM((tm, tn), f32)],
        compiler_params=_cparams(("parallel", "parallel", "arbitrary")),
    )(*args)


def _in_proj(hn, w_in, widths, name, tm=2048, tn=256):
    m, k = hn.shape
    starts = [sum(widths[:i]) // tn for i in range(len(widths))]
    counts = [wd // tn for wd in widths]
    assert all(wd % tn == 0 for wd in widths)
    n_out = len(widths)

    def body(a_ref, w_ref, *o_refs):
        j = pl.program_id(1)
        acc = _nn(a_ref[...], w_ref[...])
        for s, c, o_ref in zip(starts, counts, o_refs):
            @pl.when((j >= s) & (j < s + c))
            def _(o_ref=o_ref):
                o_ref[...] = acc

    def o_spec(s, c):
        return pl.BlockSpec((tm, tn), lambda i, j: (i, jnp.clip(j - s, 0, c - 1)))

    return pl.pallas_call(
        body, name=name, grid=(m // tm, sum(counts)),
        in_specs=[pl.BlockSpec((tm, k), lambda i, j: (i, 0)), pl.BlockSpec((k, tn), lambda i, j: (0, j))],
        out_specs=[o_spec(s, c) for s, c in zip(starts, counts)],
        out_shape=[jax.ShapeDtypeStruct((m, wd), f32) for wd in widths], compiler_params=_cparams(("parallel", "arbitrary")),
    )(hn, w_in)


def _out_proj(o, yn, w_out, h, name, tm=2048, tn=512):
    m, kb = o.shape
    n = w_out.shape[1]
    n_y = yn.shape[1] // kb
    assert yn.shape[1] % kb == 0 and w_out.shape[0] == kb * (1 + n_y)

    def body(*refs):
        o_ref, y_refs, w_refs, h_ref, out_ref = refs[0], refs[1:1 + n_y], refs[1 + n_y:2 + 2 * n_y], refs[-2], refs[-1]
        acc = h_ref[...] + _nn(o_ref[...].astype(bf16), w_refs[0][...])
        for y_ref, w_ref in zip(y_refs, w_refs[1:]):
            acc = acc + _nn(y_ref[...], w_ref[...])
        out_ref[...] = acc

    res = pl.BlockSpec((tm, tn), lambda i, j: (i, j))

    def a_blk(c):
        return pl.BlockSpec((tm, kb), lambda i, j: (i, c))

    def w_blk(r):
        return pl.BlockSpec((kb, tn), lambda i, j: (r, j))

    return pl.pallas_call(
        body, name=name, grid=(m // tm, n // tn),
        in_specs=[a_blk(0)] + [a_blk(c) for c in range(n_y)] + [w_blk(r) for r in range(1 + n_y)] + [res],
        out_specs=res, out_shape=jax.ShapeDtypeStruct((m, n), f32), compiler_params=_cparams(("parallel", "parallel")),
    )(o, *[yn] * n_y, *[w_out] * (1 + n_y), h)


def _swiglu_fwd(hn, w_gate, w_up, name, tm=2048, tn=256):
    m, k = hn.shape
    n = w_gate.shape[0]

    def body(a_ref, wg_ref, wu_ref, g_ref, u_ref, act_ref):
        a = a_ref[...]
        g = _nt(a, wg_ref[...])
        u = _nt(a, wu_ref[...])
        sg, dsg = _silu_and_grad(g)
        g_ref[...] = (u * dsg).astype(bf16)
        u_ref[...] = sg.astype(bf16)
        act_ref[...] = (sg * u).astype(bf16)

    a_spec = pl.BlockSpec((tm, k), lambda i, j: (i, 0))
    w_spec = pl.BlockSpec((tn, k), lambda i, j: (j, 0))
    o_spec = pl.BlockSpec((tm, tn), lambda i, j: (i, j))
    return pl.pallas_call(
        body, name=name, grid=(m // tm, n // tn), in_specs=[a_spec, w_spec, w_spec], out_specs=[o_spec, o_spec, o_spec],
        out_shape=[jax.ShapeDtypeStruct((m, n), bf16)] * 3,
        compiler_params=_cparams(("parallel", "parallel")),
    )(hn, w_gate, w_up)


def _swiglu_bwd(dh, w_down, g, u, name, tm=2048, tn=256):
    m, k = dh.shape
    n = w_down.shape[0]

    def body(a_ref, w_ref, g_ref, u_ref, dg_ref, du_ref):
        dact = _nt(a_ref[...].astype(bf16), w_ref[...])
        dg_ref[...] = (dact * g_ref[...].astype(f32)).astype(bf16)
        du_ref[...] = (dact * u_ref[...].astype(f32)).astype(bf16)

    a_spec = pl.BlockSpec((tm, k), lambda i, j: (i, 0))
    w_spec = pl.BlockSpec((tn, k), lambda i, j: (j, 0))
    o_spec = pl.BlockSpec((tm, tn), lambda i, j: (i, j))
    return pl.pallas_call(
        body, name=name, grid=(m // tm, n // tn), in_specs=[a_spec, w_spec, o_spec, o_spec], out_specs=[o_spec, o_spec],
        out_shape=[jax.ShapeDtypeStruct((m, n), bf16), jax.ShapeDtypeStruct((m, n), bf16)],
        compiler_params=_cparams(("parallel", "parallel")),
    )(dh, w_down, g, u)


def _rmsnorm_fwd(h, w, name, tm=512):
    m, d = h.shape

    def body(h_ref, w_ref, o_ref):
        x = h_ref[...]
        r = lax.rsqrt(jnp.mean(x * x, axis=-1, keepdims=True) + EPS)
        o_ref[...] = (x * r * w_ref[...]).astype(bf16)

    return pl.pallas_call(
        body, name=name, grid=(m // tm,),
        in_specs=[pl.BlockSpec((tm, d), lambda i: (i, 0)), pl.BlockSpec((1, d), lambda i: (0, 0))],
        out_specs=pl.BlockSpec((tm, d), lambda i: (i, 0)), out_shape=jax.ShapeDtypeStruct((m, d), bf16),
        compiler_params=_cparams(("parallel",)),
    )(h, w)


def _nt_norm_bwd(pairs, h, w, dres, name, tk, b_is_kd=False, tm=1024):
    m, d = h.shape
    contract = _nn if b_is_kd else _nt
    steps = [p[0].shape[1] // tk for p in pairs]
    assert all(p[0].shape[1] % tk == 0 for p in pairs), (name, tk)
    starts = [sum(steps[:i]) for i in range(len(pairs))]
    nk = sum(steps)
    n_p = len(pairs)

    def body(*refs):
        ab = refs[:2 * n_p]
        h_ref, w_ref, dres_ref, dh_ref, dw_ref, acc_ref = refs[2 * n_p:]
        i, k = pl.program_id(0), pl.program_id(1)

        @pl.when(k == 0)
        def _():
            acc_ref[...] = jnp.zeros_like(acc_ref)

        for p in range(n_p):
            @pl.when((k >= starts[p]) & (k < starts[p] + steps[p]))
            def _(p=p):
                acc_ref[...] += contract(ab[2 * p][...], ab[2 * p + 1][...])

        @pl.when(k == nk - 1)
        def _():
            x = h_ref[...]
            r = lax.rsqrt(jnp.mean(x * x, axis=-1, keepdims=True) + EPS)
            xhat = x * r
            dy = acc_ref[...]
            gw = dy * w_ref[...]
            dh_ref[...] = dres_ref[...] + r * (gw - xhat * jnp.mean(gw * xhat, axis=-1, keepdims=True))
            part = _rowsum8(dy * xhat)

            @pl.when(i == 0)
            def _():
                dw_ref[...] = part

            @pl.when(i > 0)
            def _():
                dw_ref[...] += part

    def clamp(k, p):
        return jnp.clip(k - starts[p], 0, steps[p] - 1)

    in_specs = []
    for p in range(n_p):
        b_spec = (pl.BlockSpec((tk, d), lambda i, k, p=p: (clamp(k, p), 0)) if b_is_kd
                  else pl.BlockSpec((d, tk), lambda i, k, p=p: (0, clamp(k, p))))
        in_specs += [pl.BlockSpec((tm, tk), lambda i, k, p=p: (i, clamp(k, p))), b_spec]
    row = pl.BlockSpec((tm, d), lambda i, k: (i, 0))
    in_specs += [row, pl.BlockSpec((1, d), lambda i, k: (0, 0)), row]
    return pl.pallas_call(
        body, name=name, grid=(m // tm, nk), in_specs=in_specs,
        out_specs=[row, pl.BlockSpec((SUBLANES, d), lambda i, k: (0, 0))],
        out_shape=[jax.ShapeDtypeStruct((m, d), f32), jax.ShapeDtypeStruct((SUBLANES, d), f32)],
        scratch_shapes=[pltpu.VMEM((tm, d), f32)], compiler_params=_cparams(("arbitrary", "arbitrary")),
    )(*[t for p in pairs for t in p], h, w, dres)


def _final_loss(h, w, target, name, tm=512):
    m, d = h.shape

    def body(h_ref, w_ref, t_ref, dh_ref, loss_ref, dw_ref):
        x = h_ref[...]
        r = lax.rsqrt(jnp.mean(x * x, axis=-1, keepdims=True) + EPS)
        xhat = x * r
        ww = w_ref[...]
        err = xhat * ww - t_ref[...]
        dy = err * (1.0 / d)
        gw = dy * ww
        dh_ref[...] = r * (gw - xhat * jnp.mean(gw * xhat, axis=-1, keepdims=True))
        lpart = _rowsum8(err * err) * (0.5 / d)
        wpart = _rowsum8(dy * xhat)

        @pl.when(pl.program_id(0) == 0)
        def _():
            loss_ref[...] = lpart
            dw_ref[...] = wpart

        @pl.when(pl.program_id(0) > 0)
        def _():
            loss_ref[...] += lpart
            dw_ref[...] += wpart

    row = pl.BlockSpec((tm, d), lambda i: (i, 0))
    acc = pl.BlockSpec((SUBLANES, d), lambda i: (0, 0))
    return pl.pallas_call(
        body, name=name, grid=(m // tm,),
        in_specs=[row, pl.BlockSpec((1, d), lambda i: (0, 0)), row], out_specs=[row, acc, acc],
        out_shape=[jax.ShapeDtypeStruct((m, d), f32), jax.ShapeDtypeStruct((SUBLANES, d), f32), jax.ShapeDtypeStruct((SUBLANES, d), f32)],
        compiler_params=_cparams(("arbitrary",)),
    )(h, w, target)


def _lane_tables():
    f = np.arange(LANES) % HEAD_DIM
    inv = ROPE_THETA ** (-jnp.arange(0, ROPE_DIM, 2, dtype=f32) / ROPE_DIM)
    invf = jnp.where(f < ROPE_DIM, inv[f % (ROPE_DIM // 2)], 0.0).astype(f32)
    return invf.reshape(1, LANES)


def _rope_tables(pos_col, name):
    t = pos_col.shape[0]
    tm = SEQ

    def body(p_ref, f_ref, c_ref, s1_ref, s2_ref):
        ang = p_ref[...].astype(f32) * f_ref[...]
        co, si = jnp.cos(ang), jnp.sin(ang)
        f = lax.broadcasted_iota(jnp.int32, (tm, LANES), 1) % HEAD_DIM
        c_ref[...] = jnp.where(f < ROPE_DIM, co, 1.0)
        s1_ref[...] = jnp.where(f < ROPE_DIM // 2, -si, 0.0)
        s2_ref[...] = jnp.where((f >= ROPE_DIM // 2) & (f < ROPE_DIM), si, 0.0)

    row = pl.BlockSpec((tm, LANES), lambda i: (i, 0))
    return pl.pallas_call(
        body, name=name, grid=(t // tm,),
        in_specs=[pl.BlockSpec((tm, 1), lambda i: (i, 0)), pl.BlockSpec((1, LANES), lambda i: (0, 0))],
        out_specs=[row, row, row], out_shape=[jax.ShapeDtypeStruct((t, LANES), f32)] * 3,
        compiler_params=_cparams(("parallel",)),
    )(pos_col, _lane_tables())


def _rot(x, c, s1, s2):
    return x * c + pltpu.roll(x, LANES - ROPE_DIM // 2, 1) * s1 + pltpu.roll(x, ROPE_DIM // 2, 1) * s2


def _rot_t(g, c, s1, s2):
    return g * c + pltpu.roll(g * s1, ROPE_DIM // 2, 1) + pltpu.roll(g * s2, LANES - ROPE_DIM // 2, 1)


def _dup_head(x, kvh, low):
    a = jnp.where(kvh == 0, x, pltpu.roll(x, HEAD_DIM, 1))
    return jnp.where(low, a, pltpu.roll(a, HEAD_DIM, 1))


def _deinterleave(src_ref, dst_ref, d, dtype):
    length = SEQ // d
    if d == 1:
        dst_ref[...] = src_ref[...].astype(dtype)
    else:
        for r in range(d):
            dst_ref[pl.ds(r * length, length), :] = src_ref[pl.ds(r, length, stride=d), :].astype(dtype)


def _interleave_store(src_ref, dst_ref, d, accumulate):
    length = SEQ // d
    if d == 1:
        if accumulate:
            dst_ref[...] += src_ref[...]
        else:
            dst_ref[...] = src_ref[...]
    else:
        for r in range(d):
            blk = src_ref[pl.ds(r * length, length), :]
            if accumulate:
                dst_ref[pl.ds(r, length, stride=d), :] = dst_ref[pl.ds(r, length, stride=d), :] + blk
            else:
                dst_ref[pl.ds(r, length, stride=d), :] = blk


def _attn_masks():
    qi = lax.broadcasted_iota(jnp.int32, (ATTN_BLOCK, ATTN_BLOCK), 0)
    ki = lax.broadcasted_iota(jnp.int32, (ATTN_BLOCK, ATTN_BLOCK), 1)
    low = lax.broadcasted_iota(jnp.int32, (ATTN_BLOCK, LANES), 1) < HEAD_DIM
    return ki <= qi, ki >= qi, low


NEG_INF = float("-inf")
ATTN_UNROLL = 4


N_BRANCH = len(DILATIONS)


def _attn_prep(qkv, tabs, name):
    t = qkv.shape[0]
    nb = t // SEQ
    n_j = ATTN_WIDTH // LANES

    def q_body(q_ref, c_ref, s1_ref, s2_ref, out_ref, xr):
        xr[...] = _rot(q_ref[...], c_ref[...], s1_ref[...], s2_ref[...]) * (HEAD_DIM ** -0.5)
        for bi, d in enumerate(DILATIONS):
            _deinterleave(xr, out_ref.at[bi], d, bf16)

    def kv_body(x_ref, c_ref, s1_ref, s2_ref, out_ref, xr):
        lowfull = lax.broadcasted_iota(jnp.int32, (SEQ, LANES), 1) < HEAD_DIM
        x = x_ref[...]
        x = jnp.where(pl.program_id(1) == 0, _rot(x, c_ref[...], s1_ref[...], s2_ref[...]), x)
        for kvh in range(N_KV_HEADS):
            xr[...] = _dup_head(x, kvh, lowfull)
            for bi, d in enumerate(DILATIONS):
                length = SEQ // d
                for r in range(d):
                    rows = xr[...] if d == 1 else xr[pl.ds(r, length, stride=d), :]
                    out_ref[0, bi, pl.ds(r * length, length), kvh * LANES:(kvh + 1) * LANES] = rows.astype(bf16)

    tab = pl.BlockSpec((SEQ, LANES), lambda b, j: (b, 0))
    q = pl.pallas_call(
        q_body, name=name + "_q", grid=(nb, n_j),
        in_specs=[pl.BlockSpec((SEQ, LANES), lambda b, j: (b, j)), tab, tab, tab],
        out_specs=pl.BlockSpec((N_BRANCH, SEQ, LANES), lambda b, j: (0, b, j)),
        out_shape=jax.ShapeDtypeStruct((N_BRANCH, t, ATTN_WIDTH), bf16), scratch_shapes=[pltpu.VMEM((SEQ, LANES), f32)],
        compiler_params=_cparams(("parallel", "parallel")),
    )(qkv, *tabs)
    kv = pl.pallas_call(
        kv_body, name=name + "_kv", grid=(nb, 2),
        in_specs=[pl.BlockSpec((SEQ, LANES), lambda b, j: (b, n_j + j)), tab, tab, tab],
        out_specs=pl.BlockSpec((1, N_BRANCH, SEQ, N_KV_HEADS * LANES), lambda b, j: (j, 0, b, 0)),
        out_shape=jax.ShapeDtypeStruct((2, N_BRANCH, t, N_KV_HEADS * LANES), bf16), scratch_shapes=[pltpu.VMEM((SEQ, LANES), f32)],
        compiler_params=_cparams(("parallel", "parallel")),
    )(qkv, *tabs)
    return q, kv


def _attn_fwd(prep, name):
    q_all, kv_all = prep
    t = q_all.shape[1]
    nb = t // SEQ
    n_blk = SEQ // ATTN_BLOCK

    def body(q_ref, k_ref, v_ref, o_ref, lse_ref, ob, lb, o0, o1, o2, l0, l1, l2, ss):
        cur_ok, prev_ok, low = _attn_masks()
        onat, lnat = (o0, o1, o2), (l0, l1, l2)
        for bi, d in enumerate(DILATIONS):
            qd, kd, vd = q_ref.at[bi], k_ref.at[0, bi], v_ref.at[0, bi]
            per_res = n_blk // d
            use_prev = per_res > 1

            def scores(n, carry):
                start = pl.multiple_of(n * ATTN_BLOCK, ATTN_BLOCK)
                has_prev = (n % per_res) != 0
                pstart = pl.multiple_of(jnp.maximum(n - 1, 0) * ATTN_BLOCK, ATTN_BLOCK)
                qb = qd[pl.ds(start, ATTN_BLOCK), :]
                kc = kd[pl.ds(start, ATTN_BLOCK), :]
                if use_prev:
                    kp = kd[pl.ds(pstart, ATTN_BLOCK), :]
                for a in range(2):
                    qa = jnp.where(low if a == 0 else ~low, qb, jnp.zeros_like(qb))
                    ss[2 * n + a, :, 0:ATTN_BLOCK] = jnp.where(cur_ok, _nt(qa, kc), NEG_INF)
                    if use_prev:
                        ss[2 * n + a, :, ATTN_BLOCK:2 * ATTN_BLOCK] = jnp.where(prev_ok & has_prev, _nt(qa, kp), NEG_INF)
                return carry

            def softmax_pv(n, carry):
                start = pl.multiple_of(n * ATTN_BLOCK, ATTN_BLOCK)
                pstart = pl.multiple_of(jnp.maximum(n - 1, 0) * ATTN_BLOCK, ATTN_BLOCK)
                vc = vd[pl.ds(start, ATTN_BLOCK), :]
                if use_prev:
                    vp = vd[pl.ds(pstart, ATTN_BLOCK), :]
                outs, lses = [], []
                for a in range(2):
                    sc = ss[2 * n + a, :, 0:ATTN_BLOCK]
                    if use_prev:
                        sp = ss[2 * n + a, :, ATTN_BLOCK:2 * ATTN_BLOCK]
                        m = jnp.max(jnp.maximum(sc, sp), axis=1, keepdims=True)
                        pc, pp = jnp.exp(sc - m), jnp.exp(sp - m)
                        den = jnp.sum(pc + pp, axis=1, keepdims=True)
                        acc = _nn(pc.astype(bf16), vc) + _nn(pp.astype(bf16), vp)
                    else:
                        m = jnp.max(sc, axis=1, keepdims=True)
                        pc = jnp.exp(sc - m)
                        den = jnp.sum(pc, axis=1, keepdims=True)
                        acc = _nn(pc.astype(bf16), vc)
                    outs.append(acc * (1.0 / den))
                    lses.append(m + jnp.log(den))
                ob[pl.ds(start, ATTN_BLOCK), :] = jnp.where(low, outs[0], outs[1])
                lb[pl.ds(start, ATTN_BLOCK), :] = jnp.where(low, lses[0], lses[1])
                return carry

            lax.fori_loop(0, n_blk, scores, 0, unroll=ATTN_UNROLL)
            lax.fori_loop(0, n_blk, softmax_pv, 0, unroll=ATTN_UNROLL)
            _interleave_store(ob, onat[bi], d, False)
            _interleave_store(lb, lnat[bi], d, False)
        la, lbb, lc = l0[...], l1[...], l2[...]
        lm = jnp.maximum(jnp.maximum(la, lbb), lc)
        wa, wb, wc = jnp.exp(la - lm), jnp.exp(lbb - lm), jnp.exp(lc - lm)
        ws = wa + wb + wc
        o_ref[...] = (wa * o0[...] + wb * o1[...] + wc * o2[...]) / ws
        lse_ref[...] = lm + jnp.log(ws)

    def col(jj):
        return pl.BlockSpec((SEQ, LANES), lambda b, j: (b, jj if jj is not None else j))

    fs = pltpu.VMEM((SEQ, LANES), f32)
    return pl.pallas_call(
        body, name=name, grid=(nb, ATTN_WIDTH // LANES),
        in_specs=[pl.BlockSpec((N_BRANCH, SEQ, LANES), lambda b, j: (0, b, j)),
                  pl.BlockSpec((1, N_BRANCH, SEQ, LANES), lambda b, j: (0, 0, b, j // 2)),
                  pl.BlockSpec((1, N_BRANCH, SEQ, LANES), lambda b, j: (1, 0, b, j // 2))],
        out_specs=[col(None), col(None)],
        out_shape=[jax.ShapeDtypeStruct((t, ATTN_WIDTH), f32), jax.ShapeDtypeStruct((t, ATTN_WIDTH), f32)],
        scratch_shapes=[fs, fs, fs, fs, fs, fs, fs, fs, pltpu.VMEM((2 * n_blk, ATTN_BLOCK, 2 * ATTN_BLOCK), f32)],
        compiler_params=_cparams(("parallel", "parallel")),
    )(q_all, kv_all, kv_all)


def _attn_bwd(prep, tabs, o, lse, do, name):
    q_all, kv_all = prep
    t = q_all.shape[1]
    nb = t // SEQ
    n_blk = SEQ // ATTN_BLOCK
    n_j = ATTN_WIDTH // LANES

    def body(q_ref, k_ref, v_ref, c_ref, s1_ref, s2_ref, o_ref, lse_ref, do_ref, dq_ref, dk_ref, dv_ref,
             dl, dod, lsd, dld, dqd, dkd, dvd, dqa, dka, dva, pb, dsb, dk_acc, dv_acc):
        j = pl.program_id(1)
        pb[2 * n_blk:2 * n_blk + 2] = jnp.zeros((2, ATTN_BLOCK, 2 * ATTN_BLOCK), bf16)
        dsb[2 * n_blk:2 * n_blk + 2] = jnp.zeros((2, ATTN_BLOCK, 2 * ATTN_BLOCK), bf16)
        kvh = j // 2
        cur_ok, prev_ok, low = _attn_masks()
        lowfull = lax.broadcasted_iota(jnp.int32, (SEQ, LANES), 1) < HEAD_DIM
        c, s1, s2 = c_ref[...], s1_ref[...], s2_ref[...]
        prod = do_ref[...] * o_ref[...]
        d_lo = jnp.sum(jnp.where(lowfull, prod, 0.0), axis=1, keepdims=True)
        d_hi = jnp.sum(jnp.where(lowfull, 0.0, prod), axis=1, keepdims=True)
        dl[...] = jnp.where(lowfull, d_lo, d_hi)
        dqa[...] = jnp.zeros_like(dqa)
        dka[...] = jnp.zeros_like(dka)
        dva[...] = jnp.zeros_like(dva)
        for bi, d in enumerate(DILATIONS):
            qd, kd, vd = q_ref.at[bi], k_ref.at[0, bi], v_ref.at[0, bi]
            _deinterleave(do_ref, dod, d, bf16)
            _deinterleave(lse_ref, lsd, d, f32)
            _deinterleave(dl, dld, d, f32)
            per_res = n_blk // d
            use_prev = per_res > 1
            curl, prevl = slice(0, ATTN_BLOCK), slice(ATTN_BLOCK, 2 * ATTN_BLOCK)

            def halves(x):
                zero = jnp.zeros_like(x)
                return jnp.where(low, x, zero), jnp.where(low, zero, x)

            def probs(n, carry):
                start = pl.multiple_of(n * ATTN_BLOCK, ATTN_BLOCK)
                has_prev = (n % per_res) != 0
                pstart = pl.multiple_of(jnp.maximum(n - 1, 0) * ATTN_BLOCK, ATTN_BLOCK)
                cur, prev = pl.ds(start, ATTN_BLOCK), pl.ds(pstart, ATTN_BLOCK)
                qas, doas = halves(qd[cur, :]), halves(dod[cur, :])
                kc, vc = kd[cur, :], vd[cur, :]
                if use_prev:
                    kp, vp = kd[prev, :], vd[prev, :]
                lsb, dlb = lsd[cur, :], dld[cur, :]
                for a in range(2):
                    ls = lsb[:, a * HEAD_DIM:a * HEAD_DIM + 1]
                    de = dlb[:, a * HEAD_DIM:a * HEAD_DIM + 1]
                    pc = jnp.exp(jnp.where(cur_ok, _nt(qas[a], kc), NEG_INF) - ls)
                    pb[2 * n + a, :, curl] = pc.astype(bf16)
                    dsb[2 * n + a, :, curl] = (pc * (_nt(doas[a], vc) - de)).astype(bf16)
                    if use_prev:
                        pp = jnp.exp(jnp.where(prev_ok & has_prev, _nt(qas[a], kp), NEG_INF) - ls)
                        pb[2 * n + a, :, prevl] = pp.astype(bf16)
                        dsb[2 * n + a, :, prevl] = (pp * (_nt(doas[a], vp) - de)).astype(bf16)
                return carry

            def grads(n, carry):
                start = pl.multiple_of(n * ATTN_BLOCK, ATTN_BLOCK)
                pstart = pl.multiple_of(jnp.maximum(n - 1, 0) * ATTN_BLOCK, ATTN_BLOCK)
                nstart = pl.multiple_of(jnp.minimum(n + 1, n_blk - 1) * ATTN_BLOCK, ATTN_BLOCK)
                cur, prev, nxt = pl.ds(start, ATTN_BLOCK), pl.ds(pstart, ATTN_BLOCK), pl.ds(nstart, ATTN_BLOCK)
                kc = kd[cur, :]
                dqs = [_nn(dsb[2 * n + a, :, curl], kc) for a in range(2)]
                q_rows, do_rows = list(halves(qd[cur, :])), list(halves(dod[cur, :]))
                ds_rows, p_rows = [dsb[2 * n + a, :, curl] for a in range(2)], [pb[2 * n + a, :, curl] for a in range(2)]
                if use_prev:
                    kp = kd[prev, :]
                    dqs = [dqs[a] + _nn(dsb[2 * n + a, :, prevl], kp) for a in range(2)]
                    q_rows += list(halves(qd[nxt, :]))
                    do_rows += list(halves(dod[nxt, :]))
                    ds_rows += [dsb[2 * n + 2 + a, :, prevl] for a in range(2)]
                    p_rows += [pb[2 * n + 2 + a, :, prevl] for a in range(2)]
                dqd[cur, :] = jnp.where(low, dqs[0], dqs[1])
                dkd[cur, :] = _tn(jnp.concatenate(ds_rows, axis=0), jnp.concatenate(q_rows, axis=0))
                dvd[cur, :] = _tn(jnp.concatenate(p_rows, axis=0), jnp.concatenate(do_rows, axis=0))
                return carry

            lax.fori_loop(0, n_blk, probs, 0, unroll=ATTN_UNROLL)
            lax.fori_loop(0, n_blk, grads, 0, unroll=ATTN_UNROLL)
            _interleave_store(dqd, dqa, d, True)
            _interleave_store(dkd, dka, d, True)
            _interleave_store(dvd, dva, d, True)
        dq_ref[...] = _rot_t(dqa[...] * (HEAD_DIM ** -0.5), c, s1, s2).astype(bf16)
        dkf = dka[...]
        dkf = _rot_t(dkf + pltpu.roll(dkf, HEAD_DIM, 1), c, s1, s2)
        dvf = dva[...]
        dvf = dvf + pltpu.roll(dvf, HEAD_DIM, 1)
        mine = (lax.broadcasted_iota(jnp.int32, (SEQ, LANES), 1) // HEAD_DIM) == kvh
        dkc_, dvc_ = jnp.where(mine, dkf, 0.0), jnp.where(mine, dvf, 0.0)

        @pl.when(j == 0)
        def _():
            dk_acc[...] = dkc_
            dv_acc[...] = dvc_

        @pl.when(j > 0)
        def _():
            dk_acc[...] += dkc_
            dv_acc[...] += dvc_

        @pl.when(j == n_j - 1)
        def _():
            dk_ref[...] = dk_acc[...].astype(bf16)
            dv_ref[...] = dv_acc[...].astype(bf16)

    def col(jj):
        return pl.BlockSpec((SEQ, LANES), lambda b, j: (b, jj if jj is not None else j))

    tab = pl.BlockSpec((SEQ, LANES), lambda b, j: (b, 0))
    fs = pltpu.VMEM((SEQ, LANES), f32)
    hs = pltpu.VMEM((SEQ, LANES), bf16)
    return pl.pallas_call(
        body, name=name, grid=(nb, n_j),
        in_specs=[pl.BlockSpec((N_BRANCH, SEQ, LANES), lambda b, j: (0, b, j)),
                  pl.BlockSpec((1, N_BRANCH, SEQ, LANES), lambda b, j: (0, 0, b, j // 2)),
                  pl.BlockSpec((1, N_BRANCH, SEQ, LANES), lambda b, j: (1, 0, b, j // 2)),
                  tab, tab, tab, col(None), col(None), col(None)],
        out_specs=[col(None), tab, tab],
        out_shape=[jax.ShapeDtypeStruct((t, ATTN_WIDTH), bf16), jax.ShapeDtypeStruct((t, LANES), bf16), jax.ShapeDtypeStruct((t, LANES), bf16)],
        scratch_shapes=[fs, hs, fs, fs, fs, fs, fs, fs, fs, fs,
                        pltpu.VMEM((2 * n_blk + 2, ATTN_BLOCK, 2 * ATTN_BLOCK), bf16), pltpu.VMEM((2 * n_blk + 2, ATTN_BLOCK, 2 * ATTN_BLOCK), bf16), fs, fs],
        compiler_params=_cparams(("parallel", "arbitrary")),
    )(q_all, kv_all, kv_all, *tabs, o, lse, do)


def _tap(w_ref, s):
    return w_ref[CONV_WIDTH - 1 - s:CONV_WIDTH - s, :]


def _conv_pre(x, w_ref, b_ref, row):
    shifted = [x] + [jnp.where(row >= s, pltpu.roll(x, s, 0), 0.0) for s in range(1, CONV_WIDTH)]
    pre = b_ref[...] + _tap(w_ref, 0) * x
    for s in range(1, CONV_WIDTH):
        pre = pre + _tap(w_ref, s) * shifted[s]
    return pre, shifted


def _conv_fwd(x, w, b, name, tc=512):
    t, ch = x.shape

    def body(x_ref, w_ref, b_ref, o_ref):
        row = lax.broadcasted_iota(jnp.int32, (SEQ, tc), 0)
        pre, _ = _conv_pre(x_ref[...], w_ref, b_ref, row)
        o_ref[...] = _silu(pre)

    xs = pl.BlockSpec((SEQ, tc), lambda i, j: (i, j))
    return pl.pallas_call(
        body, name=name, grid=(t // SEQ, ch // tc),
        in_specs=[xs, pl.BlockSpec((CONV_WIDTH, tc), lambda i, j: (0, j)), pl.BlockSpec((1, tc), lambda i, j: (0, j))],
        out_specs=xs, out_shape=jax.ShapeDtypeStruct((t, ch), f32),
        compiler_params=_cparams(("parallel", "parallel")),
    )(x, w, b)


def _conv_bwd(x, w, b, dact, name, tc=512):
    t, ch = x.shape

    def body(x_ref, w_ref, b_ref, d_ref, dx_ref, dw_ref, db_ref):
        row = lax.broadcasted_iota(jnp.int32, (SEQ, tc), 0)
        pre, shifted = _conv_pre(x_ref[...], w_ref, b_ref, row)
        dpre = d_ref[...] * _dsilu(pre)
        dx = _tap(w_ref, 0) * dpre
        for s in range(1, CONV_WIDTH):
            dx = dx + _tap(w_ref, s) * jnp.where(row < SEQ - s, pltpu.roll(dpre, SEQ - s, 0), 0.0)
        dx_ref[...] = dx.astype(bf16)
        first = pl.program_id(1) == 0
        parts = [jnp.sum(dpre * shifted[CONV_WIDTH - 1 - k], axis=0, keepdims=True) for k in range(CONV_WIDTH)]
        dbp = jnp.sum(dpre, axis=0, keepdims=True)

        @pl.when(first)
        def _():
            for k in range(CONV_WIDTH):
                dw_ref[k:k + 1, :] = parts[k]
            db_ref[...] = dbp

        @pl.when(jnp.logical_not(first))
        def _():
            for k in range(CONV_WIDTH):
                dw_ref[k:k + 1, :] += parts[k]
            db_ref[...] += dbp

    xs = pl.BlockSpec((SEQ, tc), lambda j, i: (i, j))
    ws = pl.BlockSpec((CONV_WIDTH, tc), lambda j, i: (0, j))
    bs = pl.BlockSpec((1, tc), lambda j, i: (0, j))
    return pl.pallas_call(
        body, name=name, grid=(ch // tc, t // SEQ),
        in_specs=[xs, ws, bs, xs], out_specs=[xs, ws, bs],
        out_shape=[jax.ShapeDtypeStruct((t, ch), bf16), jax.ShapeDtypeStruct((CONV_WIDTH, ch), f32), jax.ShapeDtypeStruct((1, ch), f32)],
        compiler_params=_cparams(("parallel", "arbitrary")),
    )(x, w, b, dact)


GROUP_W = SSM_INNER // SSM_GROUPS
HEADS_PER_GROUP = SSM_HEADS // SSM_GROUPS


def _split3(x):
    hi = x.astype(bf16)
    r1 = x - hi.astype(f32)
    mid = r1.astype(bf16)
    lo = (r1 - mid.astype(f32)).astype(bf16)
    return hi, mid, lo


def _dot_exact(x, sel, dims, x_is_lhs=True):
    parts = _split3(x)
    if x_is_lhs:
        return _dot(parts[0], sel, dims) + _dot(parts[1], sel, dims) + _dot(parts[2], sel, dims)
    return _dot(sel, parts[0], dims) + _dot(sel, parts[1], dims) + _dot(sel, parts[2], dims)


def _ssd_common(xbc_ref, dt_ref, bias_ref, alog_ref):
    r = lax.broadcasted_iota(jnp.int32, (CHUNK, CHUNK), 0)
    cidx = lax.broadcasted_iota(jnp.int32, (CHUNK, CHUNK), 1)
    causal = r >= cidx
    tril = causal.astype(bf16)
    expand = (lax.broadcasted_iota(jnp.int32, (CHUNK, SSM_INNER), 0)
              == lax.broadcasted_iota(jnp.int32, (CHUNK, SSM_INNER), 1) // HEAD_DIM).astype(bf16)
    head_lane = cidx < SSM_HEADS
    dtp = dt_ref[...] + bias_ref[...]
    dt = jnp.where(head_lane, _softplus(dtp), 0.0)
    a_neg = -jnp.exp(alog_ref[...])
    a = dt * a_neg
    nn_dims = ((1,), (0,))
    cs = _dot_exact(a, tril, nn_dims, x_is_lhs=False)
    dt_e = _dot_exact(dt, expand, nn_dims)
    cs_e = _dot_exact(cs, expand, nn_dims)
    xs = xbc_ref[:, 0:SSM_INNER]
    xg = xs * dt_e
    ecs = jnp.exp(cs_e)
    cs_last = cs_e[CHUNK - 1:CHUNK, :]
    dse = jnp.exp(cs_last - cs_e)
    cde = jnp.exp(cs_last)
    return dict(r=r, cidx=cidx, causal=causal, tril=tril, expand=expand, head_lane=head_lane, dtp=dtp, dt=dt, a_neg=a_neg,
                cs=cs, cst=cs.T, dt_e=dt_e, cs_e=cs_e, xs=xs, xg=xg, ecs=ecs, dse=dse, cde=cde)


def _decay_mat(q, h):
    return jnp.exp(jnp.where(q["causal"], q["cs"][:, h:h + 1] - q["cst"][h:h + 1, :], NEG_INF))


def _gate_norm(y, z, nw, gate=None):
    y2 = y * (_silu(z) if gate is None else gate)
    outs, xhats, rs = [], [], []
    for g in range(SSM_GROUPS):
        sl = slice(g * GROUP_W, (g + 1) * GROUP_W)
        yg = y2[:, sl]
        r = lax.rsqrt(jnp.mean(yg * yg, axis=-1, keepdims=True) + EPS)
        xhats.append(yg * r)
        rs.append(r)
        outs.append(yg * r * nw[:, sl])
    return y2, outs, xhats, rs


def _ssd_fwd(xbc, z, dtp, params, name):
    t = xbc.shape[0]
    n_chunk = SEQ // CHUNK
    low = None

    def body(xbc_ref, z_ref, dt_ref, bias_ref, alog_ref, dskip_ref, nw_ref, yn_ref, y_ref, hs_ref, h_scr):
        @pl.when(pl.program_id(1) == 0)
        def _():
            h_scr[...] = jnp.zeros_like(h_scr)

        q = _ssd_common(xbc_ref, dt_ref, bias_ref, alog_ref)
        low = lax.broadcasted_iota(jnp.int32, (CHUNK, LANES), 1) < HEAD_DIM
        xgb = q["xg"].astype(bf16)
        wst = (q["xg"] * q["dse"]).astype(bf16)
        hs_ref[0] = h_scr[...]
        ys = []
        for g in range(SSM_GROUPS):
            gl = slice(g * GROUP_W, (g + 1) * GROUP_W)
            bg = xbc_ref[:, SSM_INNER + g * D_STATE:SSM_INNER + (g + 1) * D_STATE].astype(bf16)
            cg = xbc_ref[:, SSM_INNER + SSM_GROUPS * D_STATE + g * D_STATE:SSM_INNER + SSM_GROUPS * D_STATE + (g + 1) * D_STATE].astype(bf16)
            cb = _nt(cg, bg)
            hg = h_scr[g]
            yoff = _nn(cg, hg.astype(bf16)) * q["ecs"][:, gl]
            pieces = []
            for i in range(HEADS_PER_GROUP // 2):
                h0 = g * HEADS_PER_GROUP + 2 * i
                xp = xgb[:, h0 * HEAD_DIM:(h0 + 2) * HEAD_DIM]
                m0 = (cb * _decay_mat(q, h0)).astype(bf16)
                m1 = (cb * _decay_mat(q, h0 + 1)).astype(bf16)
                zero = jnp.zeros_like(xp)
                pieces.append(_nn(m0, jnp.where(low, xp, zero)) + _nn(m1, jnp.where(low, zero, xp)))
            ys.append(jnp.concatenate(pieces, axis=1) + yoff + dskip_ref[:, gl] * q["xs"][:, gl])
            h_scr[g] = hg * q["cde"][:, gl] + _tn(bg, wst[:, gl])
        y = jnp.concatenate(ys, axis=1)
        y_ref[...] = y
        _, outs, _, _ = _gate_norm(y, z_ref[...], nw_ref[...])
        yn_ref[...] = jnp.concatenate(outs, axis=1).astype(bf16)

    def rows(w):
        return pl.BlockSpec((CHUNK, w), lambda b, c: (b * n_chunk + c, 0))

    def par(w):
        return pl.BlockSpec((1, w), lambda b, c: (0, 0))

    return pl.pallas_call(
        body, name=name, grid=(t // SEQ, n_chunk),
        in_specs=[rows(CONV_CH), rows(SSM_INNER), rows(LANES), par(LANES), par(LANES), par(SSM_INNER), par(SSM_INNER)],
        out_specs=[rows(SSM_INNER), rows(SSM_INNER), pl.BlockSpec((1, SSM_GROUPS, D_STATE, GROUP_W), lambda b, c: (b * n_chunk + c, 0, 0, 0))],
        out_shape=[jax.ShapeDtypeStruct((t, SSM_INNER), bf16), jax.ShapeDtypeStruct((t, SSM_INNER), f32),
                   jax.ShapeDtypeStruct((t // CHUNK, SSM_GROUPS, D_STATE, GROUP_W), f32)],
        scratch_shapes=[pltpu.VMEM((SSM_GROUPS, D_STATE, GROUP_W), f32)],
        compiler_params=_cparams(("parallel", "arbitrary")),
    )(xbc, z, dtp, *params)


def _ssd_bwd(xbc, z, dtp, y, hs, dyn, params, name):
    t = xbc.shape[0]
    n_chunk = SEQ // CHUNK

    def body(xbc_ref, z_ref, dt_ref, y_ref, hs_ref, dyn_ref, bias_ref, alog_ref, dskip_ref, nw_ref,
             dxbc_ref, dz_ref, ddt_ref, dnw_ref, dds_ref, dal_ref, dbi_ref, dh_scr):
        @pl.when(pl.program_id(1) == 0)
        def _():
            dh_scr[...] = jnp.zeros_like(dh_scr)

        q = _ssd_common(xbc_ref, dt_ref, bias_ref, alog_ref)
        low = lax.broadcasted_iota(jnp.int32, (CHUNK, LANES), 1) < HEAD_DIM
        last_row = lax.broadcasted_iota(jnp.int32, (CHUNK, GROUP_W), 0) == CHUNK - 1
        xs, xg = q["xs"], q["xg"]
        xgb = xg.astype(bf16)
        wf = xg * q["dse"]
        wst = wf.astype(bf16)
        zz = z_ref[...]
        yy = y_ref[...]
        sz, dsz = _silu_and_grad(zz)
        y2, _, xhats, rs = _gate_norm(yy, zz, nw_ref[...], gate=sz)
        dyn_ = dyn_ref[...]
        dy2s, dnws = [], []
        for g in range(SSM_GROUPS):
            gl = slice(g * GROUP_W, (g + 1) * GROUP_W)
            gw = dyn_[:, gl] * nw_ref[:, gl]
            dy2s.append(rs[g] * (gw - xhats[g] * jnp.mean(gw * xhats[g], axis=-1, keepdims=True)))
            dnws.append(_rowsum8(dyn_[:, gl] * xhats[g]))
        dy2 = jnp.concatenate(dy2s, axis=1)
        dy = dy2 * sz
        dz_ref[...] = (dy2 * yy * dsz).astype(bf16)
        dnw_p = jnp.concatenate(dnws, axis=1)
        dds_p = _rowsum8(dy * xs)
        dyb = dy.astype(bf16)
        gfull = (dy * q["ecs"]).astype(bf16)
        dcs_c = jnp.zeros((CHUNK, CHUNK), f32)
        dcs_r = jnp.zeros((CHUNK, CHUNK), f32)
        dcs_e_parts, dxg_parts = [], []
        for g in range(SSM_GROUPS):
            gl = slice(g * GROUP_W, (g + 1) * GROUP_W)
            bsl = slice(SSM_INNER + g * D_STATE, SSM_INNER + (g + 1) * D_STATE)
            csl = slice(SSM_INNER + SSM_GROUPS * D_STATE + g * D_STATE, SSM_INNER + SSM_GROUPS * D_STATE + (g + 1) * D_STATE)
            bg = xbc_ref[:, bsl].astype(bf16)
            cg = xbc_ref[:, csl].astype(bf16)
            cb = _nt(cg, bg)
            hg = hs_ref[0, g]
            hgb = hg.astype(bf16)
            dhn = dh_scr[g]
            dhnb = dhn.astype(bf16)
            yoff = _nn(cg, hgb) * q["ecs"][:, gl]
            dw_ = _nn(bg, dhnb)
            r_e = dw_ * wf[:, gl]
            to_last = jnp.sum(r_e, axis=0, keepdims=True) + jnp.sum(dhn * hg, axis=0, keepdims=True) * q["cde"][:, gl]
            dcs_e_parts.append(dy[:, gl] * yoff - r_e + jnp.where(last_row, to_last, 0.0))
            dcb = jnp.zeros((CHUNK, CHUNK), f32)
            dxg_pairs = []
            for i in range(HEADS_PER_GROUP // 2):
                h0 = g * HEADS_PER_GROUP + 2 * i
                psl = slice(h0 * HEAD_DIM, (h0 + 2) * HEAD_DIM)
                xp = xgb[:, psl]
                dyp = dyb[:, psl]
                zero = jnp.zeros_like(dyp)
                tns = []
                for a in range(2):
                    h = h0 + a
                    lm = _decay_mat(q, h)
                    m = cb * lm
                    dm = _nt(jnp.where(low, dyp, zero) if a == 0 else jnp.where(low, zero, dyp), xp)
                    dcb = dcb + dm * lm
                    nmat = dm * m
                    dcs_c = dcs_c + jnp.where(q["cidx"] == h, jnp.sum(nmat, axis=1, keepdims=True), 0.0)
                    dcs_r = dcs_r + jnp.where(q["r"] == h, jnp.sum(nmat, axis=0, keepdims=True), 0.0)
                    tns.append(_tn(m.astype(bf16), dyp))
                dxg_pairs.append(jnp.where(low, tns[0], tns[1]))
            dxg_parts.append(jnp.concatenate(dxg_pairs, axis=1) + dw_ * q["dse"][:, gl])
            dcbb = dcb.astype(bf16)
            dxbc_ref[:, csl] = _nt(gfull[:, gl], hgb) + _nn(dcbb, bg)
            dxbc_ref[:, bsl] = _nt(wst[:, gl], dhnb) + _tn(dcbb, cg)
            dh_scr[g] = dhn * q["cde"][:, gl] + _tn(cg, gfull[:, gl])
        dxg = jnp.concatenate(dxg_parts, axis=1)
        dcs_e = jnp.concatenate(dcs_e_parts, axis=1)
        dxbc_ref[:, 0:SSM_INNER] = dskip_ref[...] * dy + dxg * q["dt_e"]
        dcs = dcs_c - dcs_r.T + _dot_exact(dcs_e, q["expand"], ((1,), (1,)))
        triu = (q["cidx"] >= q["r"]).astype(bf16)
        da = _dot_exact(dcs, triu, ((1,), (0,)), x_is_lhs=False)
        ddt = _dot_exact(dxg * xs, q["expand"], ((1,), (1,))) + da * q["a_neg"]
        ddtp = jnp.where(q["head_lane"], ddt * _sigmoid(q["dtp"]), 0.0)
        ddt_ref[...] = ddtp.astype(bf16)
        dal_p = _rowsum8(da * q["dt"]) * q["a_neg"]
        dbi_p = _rowsum8(ddtp)
        first = (pl.program_id(0) == 0) & (pl.program_id(1) == 0)

        @pl.when(first)
        def _():
            dnw_ref[...] = dnw_p
            dds_ref[...] = dds_p
            dal_ref[...] = dal_p
            dbi_ref[...] = dbi_p

        @pl.when(jnp.logical_not(first))
        def _():
            dnw_ref[...] += dnw_p
            dds_ref[...] += dds_p
            dal_ref[...] += dal_p
            dbi_ref[...] += dbi_p

    def rows(w):
        return pl.BlockSpec((CHUNK, w), lambda b, c: (b * n_chunk + n_chunk - 1 - c, 0))

    def par(w):
        return pl.BlockSpec((1, w), lambda b, c: (0, 0))

    def acc(w):
        return pl.BlockSpec((SUBLANES, w), lambda b, c: (0, 0))

    return pl.pallas_call(
        body, name=name, grid=(t // SEQ, n_chunk),
        in_specs=[rows(CONV_CH), rows(SSM_INNER), rows(LANES), rows(SSM_INNER),
                  pl.BlockSpec((1, SSM_GROUPS, D_STATE, GROUP_W), lambda b, c: (b * n_chunk + n_chunk - 1 - c, 0, 0, 0)),
                  rows(SSM_INNER), par(LANES), par(LANES), par(SSM_INNER), par(SSM_INNER)],
        out_specs=[rows(CONV_CH), rows(SSM_INNER), rows(LANES), acc(SSM_INNER), acc(SSM_INNER), acc(LANES), acc(LANES)],
        out_shape=[jax.ShapeDtypeStruct((t, CONV_CH), f32), jax.ShapeDtypeStruct((t, SSM_INNER), bf16), jax.ShapeDtypeStruct((t, LANES), bf16),
                   jax.ShapeDtypeStruct((SUBLANES, SSM_INNER), f32), jax.ShapeDtypeStruct((SUBLANES, SSM_INNER), f32),
                   jax.ShapeDtypeStruct((SUBLANES, LANES), f32), jax.ShapeDtypeStruct((SUBLANES, LANES), f32)],
        scratch_shapes=[pltpu.VMEM((SSM_GROUPS, D_STATE, GROUP_W), f32)],
        compiler_params=_cparams(("arbitrary", "arbitrary")),
    )(xbc, z, dtp, y, hs, dyn, *params)


def _adamw_update(g, w, m, v):
    mm = ADAM_B1 * m + (1.0 - ADAM_B1) * g
    vv = ADAM_B2 * v + (1.0 - ADAM_B2) * (g * g)
    m_hat = mm / (1.0 - ADAM_B1 ** ADAM_STEP)
    v_hat = vv / (1.0 - ADAM_B2 ** ADAM_STEP)
    return -ADAM_LR * (m_hat / (jnp.sqrt(v_hat) + ADAM_EPS) + ADAM_WD * w), mm, vv


def _adamw(g_parts, w, m, v, name):
    rows, width = w.shape
    n = len(g_parts)
    tr = _row_tile(rows)

    def body(*refs):
        g_refs, (w_ref, m_ref, v_ref, g_out, d_out, m_out, v_out) = refs[:n], refs[n:]
        g = g_refs[0][...].astype(f32)
        for r in g_refs[1:]:
            g = g + r[...].astype(f32)
        g_out[...] = g
        d_out[...], m_out[...], v_out[...] = _adamw_update(g, w_ref[...], m_ref[...], v_ref[...])

    spec = pl.BlockSpec((tr, width), lambda i: (i, 0))
    return pl.pallas_call(
        body, name=name, grid=(rows // tr,), in_specs=[spec] * (n + 3), out_specs=[spec] * 4,
        out_shape=[jax.ShapeDtypeStruct((rows, width), f32)] * 4, compiler_params=_cparams(("parallel",)),
    )(*g_parts, w, m, v)


def _adamw_layers(landed, w, m, v, after, name, layers_on_columns=False):
    depth = len(landed)
    _, rows, width = landed[0].shape
    tr = _row_tile(rows)
    n_i = rows // tr
    at = (lambda ref: ref) if layers_on_columns else (lambda ref: ref.at[0])

    def body(*refs):
        part_refs, (w_ref, m_ref, v_ref, _, g_out, d_out, m_out, v_out) = refs[:depth * N_DEV], refs[depth * N_DEV:]
        for l in range(depth):
            @pl.when(pl.program_id(0) == l)
            def _(l=l):
                g = part_refs[l * N_DEV][0].astype(f32)
                for r in part_refs[l * N_DEV + 1:(l + 1) * N_DEV]:
                    g = g + r[0].astype(f32)
                at(g_out)[...] = g
                at(d_out)[...], at(m_out)[...], at(v_out)[...] = _adamw_update(g, at(w_ref)[...], at(m_ref)[...], at(v_ref)[...])

    def part_spec(l, p):
        return pl.BlockSpec((1, tr, width), lambda ll, i: (p, jnp.where(ll == l, i, jnp.where(ll < l, 0, n_i - 1)), 0))

    state = (pl.BlockSpec((tr, width), lambda ll, i: (i, ll)) if layers_on_columns
             else pl.BlockSpec((1, tr, width), lambda ll, i: (ll, i, 0)))
    return pl.pallas_call(
        body, name=name, grid=(depth, n_i),
        in_specs=[part_spec(l, p) for l in range(depth) for p in range(N_DEV)] + [state] * 3 + [ANY], out_specs=[state] * 4,
        out_shape=[jax.ShapeDtypeStruct(w.shape, f32)] * 4, compiler_params=_cparams(("arbitrary", "arbitrary")),
    )(*[landed[l] for l in range(depth) for _ in range(N_DEV)], w, m, v, after)


def _row_tile(rows, cap=512):
    for cand in range(min(rows, cap) // SUBLANES * SUBLANES, 0, -SUBLANES):
        if rows % cand == 0:
            return cand
    return rows


def _cols_from_devices(g, width, name):
    n_dev, depth, a, b = g.shape

    def body(g_ref, o_ref):
        for i in range(n_dev):
            o_ref[0, :, i * b:(i + 1) * b] = g_ref[i, 0]
        if width > n_dev * b:
            o_ref[0, :, n_dev * b:width] = jnp.zeros((a, width - n_dev * b), o_ref.dtype)

    return pl.pallas_call(
        body, name=name, grid=(depth,), in_specs=[pl.BlockSpec((n_dev, 1, a, b), lambda l: (0, l, 0, 0))],
        out_specs=pl.BlockSpec((1, a, width), lambda l: (l, 0, 0)), out_shape=jax.ShapeDtypeStruct((depth, a, width), g.dtype),
        compiler_params=_cparams(("parallel",)),
    )(g)


def _devices_from_cols(per_layer, b, name, tr=256):
    depth = len(per_layer)
    a, width = per_layer[0].shape

    def body(*refs):
        o_ref = refs[depth]
        for l in range(depth):
            for i in range(N_DEV):
                o_ref[i, l] = refs[l][:, i * b:(i + 1) * b]

    return pl.pallas_call(
        body, name=name, grid=(a // tr,), in_specs=[pl.BlockSpec((tr, width), lambda r: (r, 0))] * depth,
        out_specs=pl.BlockSpec((N_DEV, depth, tr, b), lambda r: (0, 0, r, 0)),
        out_shape=jax.ShapeDtypeStruct((N_DEV, depth, a, b), per_layer[0].dtype), compiler_params=_cparams(("parallel",)),
    )(*per_layer)


def _me():
    return lax.axis_index("x"), lax.axis_index("y"), lax.axis_index("c")


def _allgather_two_level(shards, name):
    n = len(shards)
    per = 7

    def body(*refs):
        ins, outs, token = refs[:n], refs[n:2 * n], refs[2 * n]
        send_sems, recv_sems, local_sems = refs[2 * n + 1:]
        token[...] = jnp.zeros_like(token)
        x, y, c = _me()
        me, sibling = (x, y, c), (x, y, 1 - c)
        chips = [(1 - x, y), (x, 1 - y), (1 - x, 1 - y)]

        def slot(a, p):
            return outs[a].at[4 * p[0] + 2 * p[1] + p[2]]

        def copy(a, k, block, to, src=None):
            return pltpu.make_async_remote_copy(
                src_ref=slot(a, block) if src is None else src, dst_ref=slot(a, block),
                send_sem=send_sems.at[a * per + k], recv_sem=recv_sems.at[a * per + k], device_id=to, device_id_type=MESH)

        mine = [pltpu.make_async_copy(ins[a], slot(a, me), local_sems.at[a]) for a in range(n)]
        for cp in mine:
            cp.start()
        first = []
        for a in range(n):
            first.append(copy(a, 0, me, sibling, src=ins[a]))
            first += [copy(a, 1 + j, me, (*chip, c), src=ins[a]) for j, chip in enumerate(chips)]
        for cp in first:
            cp.start()
        passed = []
        for j, chip in enumerate(chips):
            for a in range(n):
                copy(a, 1 + j, (*chip, c), me).wait_recv()
                fwd = copy(a, 4 + j, (*chip, c), sibling)
                fwd.start()
                passed.append(fwd)
        for a in range(n):
            copy(a, 0, sibling, me).wait_recv()
            for j, chip in enumerate(chips):
                copy(a, 4 + j, (*chip, 1 - c), me).wait_recv()
        for cp in first + passed:
            cp.wait_send()
        for cp in mine:
            cp.wait()

    outs = pl.pallas_call(
        body, name=name, in_specs=[ANY] * n, out_specs=[ANY] * n + [pl.BlockSpec(memory_space=pltpu.VMEM)],
        out_shape=[jax.ShapeDtypeStruct((N_DEV,) + s.shape, s.dtype) for s in shards] + [jax.ShapeDtypeStruct((SUBLANES, LANES), f32)],
        scratch_shapes=[pltpu.SemaphoreType.DMA((n * per,)), pltpu.SemaphoreType.DMA((n * per,)), pltpu.SemaphoreType.DMA((n,))],
    )(*shards)
    return outs[:n], outs[n]


def _allgather_direct(row, name):
    def body(in_ref, out_ref, send_sems, recv_sems, local_sem):
        x, y, c = _me()
        mine = out_ref.at[4 * x + 2 * y + c]
        local = pltpu.make_async_copy(in_ref, mine, local_sem)
        local.start()
        sends = []
        for k in range(1, N_DEV):
            px, py, pc = x ^ (k >> 2), y ^ ((k >> 1) & 1), c ^ (k & 1)
            sends.append(pltpu.make_async_remote_copy(
                src_ref=in_ref, dst_ref=mine, send_sem=send_sems.at[k - 1], recv_sem=recv_sems.at[k - 1],
                device_id=(px, py, pc), device_id_type=MESH))
        for cp in sends:
            cp.start()
        for k in range(1, N_DEV):
            px, py, pc = x ^ (k >> 2), y ^ ((k >> 1) & 1), c ^ (k & 1)
            theirs = out_ref.at[4 * px + 2 * py + pc]
            pltpu.make_async_remote_copy(
                src_ref=in_ref, dst_ref=theirs, send_sem=send_sems.at[k - 1], recv_sem=recv_sems.at[k - 1],
                device_id=(px, py, pc), device_id_type=MESH).wait_recv()
        for cp in sends:
            cp.wait_send()
        local.wait()

    return pl.pallas_call(
        body, name=name, in_specs=[ANY], out_specs=ANY, out_shape=jax.ShapeDtypeStruct((N_DEV,) + row.shape, row.dtype),
        scratch_shapes=[pltpu.SemaphoreType.DMA((N_DEV - 1,)), pltpu.SemaphoreType.DMA((N_DEV - 1,)), pltpu.SemaphoreType.DMA],
    )(row)


N_CHIP = N_DEV // 2
HBM = pl.BlockSpec(memory_space=pltpu.HBM)
SEM = pl.BlockSpec(memory_space=pltpu.SEMAPHORE)
EFFECT = pltpu.SideEffectType.DATAFLOW_SIDE_EFFECTING


def _peer(k):
    x, y, c = _me()
    return x ^ (k >> 2), y ^ ((k >> 1) & 1), c ^ (k & 1)


def _direct_copies(srcs, lands, send_sems, recv_sems, per_peer):
    x, y, c = _me()
    me = 4 * x + 2 * y + c
    copies = []
    for a in range(len(srcs)):
        for k in range(1, N_DEV):
            px, py, pc = _peer(k)
            piece = srcs[a].at[4 * px + 2 * py + pc] if per_peer else srcs[a]
            copies.append(pltpu.make_async_remote_copy(
                src_ref=piece, dst_ref=lands[a].at[me], send_sem=send_sems.at[a * (N_DEV - 1) + k - 1],
                recv_sem=recv_sems.at[a * (N_DEV - 1) + k - 1], device_id=(px, py, pc), device_id_type=MESH))
    return copies


def _direct_start(srcs, lands, per_peer, name):
    n = len(srcs)
    n_sem = n * (N_DEV - 1)

    def body(*refs):
        src_refs, land_refs = refs[:n], refs[n:2 * n]
        send_sems, recv_sems = refs[2 * n], refs[2 * n + 1]
        token = refs[-1]
        for cp in _direct_copies(src_refs, land_refs, send_sems, recv_sems, per_peer):
            cp.start()
        token[...] = jnp.zeros_like(token)

    outs = pl.pallas_call(
        body, name=name,
        out_shape=(pltpu.SemaphoreType.DMA((n_sem,)), pltpu.SemaphoreType.DMA((n_sem,)),
                   *[pltpu.HBM(s.shape, s.dtype) for s in srcs], *[pltpu.HBM(s.shape, s.dtype) for s in lands],
                   jax.ShapeDtypeStruct((SUBLANES, LANES), f32)),
        in_specs=[HBM] * (2 * n), out_specs=(SEM, SEM, *[HBM] * (2 * n), pl.BlockSpec(memory_space=pltpu.VMEM)),
        input_output_aliases={i: 2 + i for i in range(2 * n)},
        compiler_params=pltpu.CompilerParams(has_side_effects=EFFECT),
    )(*[pltpu.with_memory_space_constraint(s, pltpu.HBM) for s in srcs], *[pltpu.with_memory_space_constraint(s, pltpu.HBM) for s in lands])
    return outs[0], outs[1], outs[2:2 + n], outs[2 + n:2 + 2 * n], outs[-1]


def _direct_wait(send_sems, recv_sems, srcs, lands, after, per_peer, name):
    n = len(srcs)

    def body(*refs):
        src_refs, land_refs = refs[:n], refs[n:2 * n]
        s_sems, r_sems = refs[2 * n], refs[2 * n + 1]
        for cp in _direct_copies(src_refs, land_refs, s_sems, r_sems, per_peer):
            cp.wait_send()
            cp.wait_recv()

    outs = pl.pallas_call(
        body, name=name,
        out_shape=tuple(pltpu.HBM(s.shape, s.dtype) for s in list(srcs) + list(lands)),
        in_specs=[HBM] * (2 * n) + [SEM, SEM, ANY], out_specs=tuple([HBM] * (2 * n)),
        input_output_aliases={i: i for i in range(2 * n)},
        compiler_params=pltpu.CompilerParams(has_side_effects=EFFECT),
    )(*srcs, *lands, send_sems, recv_sems, after)
    return outs[n:]


def _row(v, width=None):
    v = v.reshape(1, -1).astype(f32)
    if width is not None and v.shape[1] < width:
        v = jnp.pad(v, ((0, 0), (0, width - v.shape[1])))
    return v


def _layer_params(p, l):
    return dict(
        norm_mix=_row(p["norm_mix"][l]), norm_ffn=_row(p["norm_ffn"][l]), conv_w=p["conv_w"][l], conv_b=_row(p["conv_b"][l]),
        ssd=(_row(p["dt_bias"][l], LANES), _row(p["a_log"][l], LANES), _row(jnp.repeat(p["d_skip"][l], HEAD_DIM)), _row(p["ssm_norm"][l])))


def _layer_fwd(h, w_in, rest, sp, tabs, l):
    tag = f"l{l}_"
    hn = _rmsnorm_fwd(h, sp["norm_mix"], tag + "norm_mix")
    qkv, z, xbc_pre = _in_proj(hn, w_in, (QKV_WIDTH, SSM_INNER, CONV_CH), tag + "proj")
    dtp = _matmul(hn, w_in, mode="nn", n_out=LANES, tn=LANES, b_off=DT_OFF // LANES, name=tag + "proj_dt")
    prep = _attn_prep(qkv, tabs, tag + "attn_prep")
    o, lse = _attn_fwd(prep, tag + "attn_fwd")
    xbc = _conv_fwd(xbc_pre, sp["conv_w"], sp["conv_b"], tag + "conv_fwd")
    yn, y, hs = _ssd_fwd(xbc, z, dtp, sp["ssd"], tag + "ssd_fwd")
    w_out, w_gate, w_up, w_down = rest(yn) if callable(rest) else rest
    h2 = _out_proj(o, yn, w_out, h, tag + "out_proj")
    hn2 = _rmsnorm_fwd(h2, sp["norm_ffn"], tag + "norm_ffn")
    g, u, act = _swiglu_fwd(hn2, w_gate, w_up, tag + "ffn_up")
    h3 = _matmul(act, w_down, mode="nn", tk=1408, add=h2, name=tag + "ffn_down")
    saved = dict(h=h, hn=hn, prep=prep, z=z, xbc_pre=xbc_pre, dtp=dtp, o=o, lse=lse, xbc=xbc, yn=yn, y=y, hs=hs, h2=h2, hn2=hn2, g=g, u=u, act=act,
                 rest=(w_out, w_gate, w_up, w_down))
    return h3, saved


def _layer_bwd(dh3, s, big, sp, tabs, l, gd=f32, after_ffn=None):
    tag = f"l{l}_"
    w_in, w_out, w_gate, w_up, w_down = big
    dg, du = _swiglu_bwd(dh3, w_down, s["g"], s["u"], tag + "ffn_down_bwd")
    dw_down = _matmul(s["act"], dh3, mode="tn", tm=1408, tn=512, tk=2048, out_dtype=gd, name=tag + "dw_down")
    dw_gate = _matmul(dg, s["hn2"], mode="tn", tm=1408, tn=512, tk=2048, out_dtype=gd, name=tag + "dw_gate")
    dw_up = _matmul(du, s["hn2"], mode="tn", tm=1408, tn=512, tk=2048, out_dtype=gd, name=tag + "dw_up")
    norm_ffn = sp["norm_ffn"] if after_ffn is None else sp["norm_ffn"] + after_ffn(dict(w_gate=dw_gate, w_up=dw_up, w_down=dw_down))
    dh2, dnf = _nt_norm_bwd([(dg, w_gate), (du, w_up)], s["h2"], norm_ffn, dh3, tag + "ffn_up_bwd_norm", tk=1408, b_is_kd=True)
    d_o = _matmul(dh2, w_out, mode="nt", n_out=ATTN_WIDTH, tn=512, b_off=0, name=tag + "out_attn_bwd")
    dyn = _matmul(dh2, w_out, mode="nt", n_out=SSM_INNER, tn=512, b_off=1, name=tag + "out_ssm_bwd")
    dw_out = jnp.concatenate([_matmul(s["o"], dh2, mode="tn", tm=512, tn=512, tk=2048, out_dtype=gd, name=tag + "dw_out_attn"),
                              _matmul(s["yn"], dh2, mode="tn", tm=512, tn=512, tk=2048, out_dtype=gd, name=tag + "dw_out_ssm")], axis=0)
    dxbc, dz, ddtp, dnw, dds, dal, dbi = _ssd_bwd(s["xbc"], s["z"], s["dtp"], s["y"], s["hs"], dyn, sp["ssd"], tag + "ssd_bwd")
    dxbc_pre, dconv_w, dconv_b = _conv_bwd(s["xbc_pre"], sp["conv_w"], sp["conv_b"], dxbc, tag + "conv_bwd")
    dq, dk, dv = _attn_bwd(s["prep"], tabs, s["o"], s["lse"], d_o, tag + "attn_bwd")
    dproj = jnp.concatenate([dq, dk, dv, dz, dxbc_pre, ddtp], axis=1)
    dw_in = _matmul(s["hn"], dproj, mode="tn", tm=512, tn=1152, tk=2048, out_dtype=gd, name=tag + "dw_in")
    dh, dnm = _nt_norm_bwd([(dproj, w_in)], s["h"], sp["norm_mix"], dh2, tag + "proj_bwd_norm", tk=1152)
    grads = dict(
        norm_mix=dnm.sum(0), w_in=dw_in, conv_w=dconv_w, conv_b=dconv_b[0], dt_bias=dbi.sum(0)[:SSM_HEADS], a_log=dal.sum(0)[:SSM_HEADS],
        d_skip=dds.sum(0).reshape(SSM_HEADS, HEAD_DIM).sum(1), ssm_norm=dnw.sum(0), w_out=dw_out, norm_ffn=dnf.sum(0),
        w_gate=dw_gate, w_up=dw_up, w_down=dw_down)
    return dh, grads


def _local_step(x, positions, target, p, bigs):
    tabs = _rope_tables(positions.reshape(-1, 1), "rope_tables")
    h = x
    saved, sps = [], []
    for l in range(DEPTH):
        sps.append(_layer_params(p, l))
        h, s = _layer_fwd(h, bigs[l][0], bigs[l][1:], sps[l], tabs, l)
        saved.append(s)
    dh, loss_parts, dfn = _final_loss(h, _row(p["final_norm"]), target, "final_loss")
    layer_grads = [None] * DEPTH
    for l in reversed(range(DEPTH)):
        dh, layer_grads[l] = _layer_bwd(dh, saved[l], bigs[l], sps[l], tabs, l)
    grads = {k: [layer_grads[l][k] for l in range(DEPTH)] for k in layer_grads[0]}
    grads["final_norm"] = dfn.sum(0)
    return jnp.sum(loss_parts), dh, grads


BIG = ("w_in", "w_out", "w_gate", "w_up", "w_down")
REST = BIG[1:]
FFN = ("w_gate", "w_up", "w_down")
MIX = ("w_in", "w_out")
COL_SHARDED = ("w_in",)
TRANSPOSED = ("w_gate", "w_up")
SMALL = ("norm_mix", "conv_b", "dt_bias", "a_log", "d_skip", "ssm_norm", "norm_ffn", "final_norm")
WEIGHTS = ("norm_mix", "w_in", "conv_w", "conv_b", "dt_bias", "a_log", "d_skip", "ssm_norm", "w_out", "norm_ffn", "w_gate", "w_up", "w_down", "final_norm")
SMALL_ROWS = 88
CONVW_ROWS = 96
CONVW_SHARD_ROWS = 16


def _full_from_gathered(name, g, l):
    _, a, b = g.shape
    if name in COL_SHARDED:
        width = IN_PROJ_PAD if name == "w_in" else N_DEV * b
        return _cols_from_devices(g.reshape(N_DEV, 1, a, b), width, f"cols_l{l}_{name}").reshape(a, width)
    return g.reshape(N_DEV * a, b)


def _by_device(name, full, shard_shape, l):
    a, b = shard_shape
    if name in COL_SHARDED:
        return _devices_from_cols([full], b, f"devs_l{l}_{name}").reshape(N_CHIP, 2, a, b)
    return full.reshape(N_CHIP, 2, a, b)


def _pack_rows(parts, rows, width):
    flat = jnp.concatenate([q.reshape(-1) for q in parts])
    return jnp.pad(flat, (0, rows * width - flat.shape[0])).reshape(rows, width)


def _unpack(flat, like):
    out, off = [], 0
    for q in like:
        out.append(flat[off:off + q.size].reshape(q.shape))
        off += q.size
    return out


def kernel(x, positions, norm_mix, w_in, conv_w, conv_b, dt_bias, a_log, d_skip, ssm_norm, w_out, norm_ffn, w_gate, w_up, w_down, final_norm, loss_target, m_norm_mix, m_w_in, m_conv_w, m_conv_b, m_dt_bias, m_a_log, m_d_skip, m_ssm_norm, m_w_out, m_norm_ffn, m_w_gate, m_w_up, m_w_down, m_final_norm, v_norm_mix, v_w_in, v_conv_w, v_conv_b, v_dt_bias, v_a_log, v_d_skip, v_ssm_norm, v_w_out, v_norm_ffn, v_w_gate, v_w_up, v_w_down, v_final_norm):
    w = dict(norm_mix=norm_mix, w_in=w_in, conv_w=conv_w, conv_b=conv_b, dt_bias=dt_bias, a_log=a_log, d_skip=d_skip, ssm_norm=ssm_norm,
             w_out=w_out, norm_ffn=norm_ffn, w_gate=w_gate, w_up=w_up, w_down=w_down, final_norm=final_norm)
    m = dict(norm_mix=m_norm_mix, w_in=m_w_in, conv_w=m_conv_w, conv_b=m_conv_b, dt_bias=m_dt_bias, a_log=m_a_log, d_skip=m_d_skip,
             ssm_norm=m_ssm_norm, w_out=m_w_out, norm_ffn=m_norm_ffn, w_gate=m_w_gate, w_up=m_w_up, w_down=m_w_down, final_norm=m_final_norm)
    v = dict(norm_mix=v_norm_mix, w_in=v_w_in, conv_w=v_conv_w, conv_b=v_conv_b, dt_bias=v_dt_bias, a_log=v_a_log, d_skip=v_d_skip,
             ssm_norm=v_ssm_norm, w_out=v_w_out, norm_ffn=v_norm_ffn, w_gate=v_w_gate, w_up=v_w_up, w_down=v_w_down, final_norm=v_final_norm)
    ax, ay, ac = lax.axis_index("x"), lax.axis_index("y"), lax.axis_index("c")
    dev = 4 * ax + 2 * ay + ac

    assert DEPTH == 2
    t = x.shape[0] * x.shape[1]
    xf, target = x.reshape(t, D_MODEL), loss_target.reshape(t, D_MODEL)

    def own_slot(block):
        return lax.dynamic_update_slice(lax.empty((N_DEV,) + block.shape[1:], block.dtype), block, (dev,) + (0,) * (block.ndim - 1))

    def layer_shard(arr, k, l):
        return jnp.transpose(arr, (2, 0, 1))[:, l, :] if k in TRANSPOSED else arr[l]

    def gather_start(keys, l, tie, name):
        shards = [(layer_shard(w[keys[0]], keys[0], l) + tie).astype(bf16)] + [layer_shard(w[k], k, l).astype(bf16) for k in keys[1:]]
        return _direct_start(shards, [own_slot(s[None]) for s in shards], False, name)

    def scatter_start(keys, grads_l, l, name):
        shapes = [(w[k].shape[2], w[k].shape[1]) if k in TRANSPOSED else w[k].shape[1:] for k in keys]
        by_dev = [_by_device(k, grads_l[k], sh, l).reshape((N_DEV,) + sh) for k, sh in zip(keys, shapes)]
        return _direct_start(by_dev, [own_slot(lax.dynamic_slice_in_dim(g, dev, 1, 0)) for g in by_dev], True, name)

    (g_in0, conv_all), tie = _allgather_two_level([w["w_in"][0].astype(bf16), w["conv_w"]], "gather_l0_w_in")
    rest0_copy = gather_start(REST, 0, tie[0, 0], "gather_l0_rest_start")
    l1_copy = gather_start(BIG, 1, rest0_copy[4][0, 0], "gather_l1_start")
    p = {k: w[k] for k in SMALL}
    p["norm_mix"] = p["norm_mix"] + l1_copy[4][0, 0]
    p["conv_w"] = jnp.transpose(conv_all, (1, 2, 0, 3)).reshape(DEPTH, CONV_WIDTH, CONV_CH)
    sp0, sp1 = _layer_params(p, 0), _layer_params(p, 1)

    def rest0(after):
        lands = _direct_wait(*rest0_copy[:4], after, False, "gather_l0_rest_wait")
        return tuple(_full_from_gathered(k, g, 0) for k, g in zip(REST, lands))

    tabs = _rope_tables(positions.reshape(t, 1), "rope_tables")
    w_in0 = _full_from_gathered("w_in", g_in0, 0)
    h1, saved0 = _layer_fwd(xf, w_in0, rest0, sp0, tabs, 0)
    lands1 = _direct_wait(*l1_copy[:4], h1, False, "gather_l1_wait")
    bigs1 = tuple(_full_from_gathered(k, g, 1) for k, g in zip(BIG, lands1))
    h2, saved1 = _layer_fwd(h1, bigs1[0], bigs1[1:], sp1, tabs, 1)
    dh, loss_parts, dfn = _final_loss(h2, _row(p["final_norm"]), target, "final_loss")
    loss_local = jnp.sum(loss_parts)

    dh, grads1 = _layer_bwd(dh, saved1, bigs1, sp1, tabs, 1, gd=bf16)
    l1_grads = scatter_start(BIG, grads1, 1, "scatter_l1_start")
    w_out0, w_gate0, w_up0, w_down0 = saved0["rest"]
    bigs0 = (w_in0, w_out0, w_gate0, w_up0, w_down0 + l1_grads[4][0, 0].astype(bf16))
    ffn0_grads = []

    def after_ffn(grads_ffn):
        ffn0_grads.append(scatter_start(FFN, grads_ffn, 0, "scatter_l0_ffn_start"))
        return ffn0_grads[0][4][0, 0]

    dx, grads0 = _layer_bwd(dh, saved0, bigs0, sp0, tabs, 0, gd=bf16, after_ffn=after_ffn)
    mix0_grads = scatter_start(MIX, grads0, 0, "scatter_l0_mix_start")
    landed = {(k, 1): g for k, g in zip(BIG, _direct_wait(*l1_grads[:4], dx, True, "scatter_l1_wait"))}
    landed.update({(k, 0): g for k, g in zip(FFN, _direct_wait(*ffn0_grads[0][:4], dx, True, "scatter_l0_ffn_wait"))})
    out_g, out_d, out_m, out_v = {}, {}, {}, {}

    def update(keys, after):
        for k in keys:
            parts = [landed[k, l] for l in range(DEPTH)]
            if k in TRANSPOSED:
                depth, a, b = w[k].shape
                state = [jnp.transpose(s, (2, 0, 1)).reshape(b, depth * a) for s in (w[k], m[k], v[k])]
                res = _adamw_layers(parts, *state, after, "adamw_" + k, layers_on_columns=True)
                res = [jnp.transpose(r.reshape(b, depth, a), (1, 2, 0)) for r in res]
            else:
                res = _adamw_layers(parts, w[k], m[k], v[k], after, "adamw_" + k)
            for dst, r in zip((out_g, out_d, out_m, out_v), res):
                dst[k] = r

    update(FFN, mix0_grads[4])
    grads = {k: [grads0[k], grads1[k]] for k in grads0 if k not in BIG}
    grads["final_norm"] = dfn.sum(0) + mix0_grads[4][0, 0]

    small_like = [w[k] for k in SMALL]
    small_grads = [jnp.stack(grads[k]) if k != "final_norm" else grads[k] for k in SMALL]
    small_pack = jnp.concatenate([_pack_rows(small_grads, SMALL_ROWS, LANES), _pack_rows([jnp.stack(grads["conv_w"])], CONVW_ROWS, LANES)], axis=0)
    parts = _allgather_direct(small_pack, "gather_small_grads")
    g_s, d_s, m_s, v_s = _adamw(
        [parts[i, :SMALL_ROWS] for i in range(N_DEV)], _pack_rows(small_like, SMALL_ROWS, LANES),
        _pack_rows([m[k] for k in SMALL], SMALL_ROWS, LANES), _pack_rows([v[k] for k in SMALL], SMALL_ROWS, LANES), "adamw_replicated")
    for dst, src in ((out_g, g_s), (out_d, d_s), (out_m, m_s), (out_v, v_s)):
        dst.update(zip(SMALL, _unpack(src.reshape(-1), small_like)))
    shard_w = conv_w.shape[-1]
    conv_parts = parts[:, SMALL_ROWS:].reshape(N_DEV, DEPTH, CONV_WIDTH, CONV_CH)
    conv_mine = lax.dynamic_slice_in_dim(conv_parts, dev * shard_w, shard_w, axis=3)
    g_c, d_c, m_c, v_c = _adamw(
        [_pack_rows([conv_mine[i]], CONVW_SHARD_ROWS, LANES) for i in range(N_DEV)], _pack_rows([conv_w], CONVW_SHARD_ROWS, LANES),
        _pack_rows([m["conv_w"]], CONVW_SHARD_ROWS, LANES), _pack_rows([v["conv_w"]], CONVW_SHARD_ROWS, LANES), "adamw_conv_w")
    for dst, src in ((out_g, g_c), (out_d, d_c), (out_m, m_c), (out_v, v_c)):
        dst["conv_w"] = src.reshape(-1)[:conv_w.size].reshape(conv_w.shape)

    landed.update({(k, 0): g for k, g in zip(MIX, _direct_wait(*mix0_grads[:4], v_c + out_v["w_down"][0, :CONVW_SHARD_ROWS, :LANES], True, "scatter_l0_mix_wait"))})
    update(MIX, v_c)

    loss = lax.psum(loss_local, ("x", "y", "c"))
    return (loss, dx.reshape(x.shape), *[out_g[k] for k in WEIGHTS], *[out_d[k] for k in WEIGHTS],
            *[out_m[k] for k in WEIGHTS], *[out_v[k] for k in WEIGHTS])
```

```python
import jax
import jax.numpy as jnp
import numpy as np
from jax import lax
from jax.experimental import pallas as pl
from jax.experimental.pallas import tpu as pltpu

f32 = jnp.float32
bf16 = jnp.bfloat16

D_MODEL = 1024
SEQ = 2048
DEPTH = 2
HEAD_DIM = 64
N_ATTN_HEADS = 8
N_KV_HEADS = 2
ATTN_WIDTH = 512
KV_WIDTH = 128
ROPE_DIM = 16
ROPE_THETA = 500000.0
DILATIONS = (1, 4, 16)
ATTN_BLOCK = 128
SSM_HEADS = 16
SSM_INNER = 1024
SSM_GROUPS = 2
D_STATE = 128
CONV_WIDTH = 4
CHUNK = 128
CONV_CH = 1536
MIX_WIDTH = 1536
QKV_WIDTH = ATTN_WIDTH + 2 * KV_WIDTH
DT_OFF = 3328
IN_PROJ = 3344
IN_PROJ_PAD = 3456
FFN_HIDDEN = 2816
EPS = 1e-5
N_DEV = 8
ADAM_LR = 0.001
ADAM_B1 = 0.9
ADAM_B2 = 0.999
ADAM_EPS = 1e-08
ADAM_WD = 0.01
ADAM_STEP = 10

LANES = 128
SUBLANES = 8
VMEM_LIMIT = 56 * 1024 * 1024

MESH = pl.DeviceIdType.MESH
ANY = pl.BlockSpec(memory_space=pl.ANY)


def _cparams(sem, vmem=None):
    return pltpu.CompilerParams(dimension_semantics=sem, vmem_limit_bytes=vmem or VMEM_LIMIT)


def _sigmoid(x):
    return 1.0 / (1.0 + jnp.exp(-x))


def _silu(x):
    return x * _sigmoid(x)


def _dsilu(x):
    s = _sigmoid(x)
    return s * (1.0 + x * (1.0 - s))


def _silu_and_grad(x):
    s = _sigmoid(x)
    return x * s, s * (1.0 + x * (1.0 - s))


def _softplus(x):
    return jnp.maximum(x, 0.0) + jnp.log(1.0 + jnp.exp(-jnp.abs(x)))


def _dot(a, b, dims, precision=None):
    return lax.dot_general(a, b, (dims, ((), ())), preferred_element_type=f32, precision=precision)


def _nn(a, b, precision=None):
    return _dot(a, b, ((1,), (0,)), precision)


def _nt(a, b):
    return _dot(a, b, ((1,), (1,)))


def _tn(a, b):
    return _dot(a, b, ((0,), (0,)))


def _rowsum8(t):
    n, w = t.shape
    return jnp.sum(t.reshape(n // SUBLANES, SUBLANES, w), axis=0)


def _matmul(a, b, *, mode, n_out=None, b_off=0, add=None, out_dtype=f32, tm=2048, tn=512, tk=1024, name):
    if mode == "tn":
        kk, m = a.shape
    else:
        m, kk = a.shape
    n = n_out if n_out is not None else (b.shape[0] if mode == "nt" else b.shape[1])
    tm, tn, tk = min(tm, m), min(tn, n), min(tk, kk)
    assert m % tm == 0 and n % tn == 0 and kk % tk == 0, (name, m, n, kk, tm, tn, tk)
    nk = kk // tk
    if mode == "nn":
        a_spec = pl.BlockSpec((tm, tk), lambda i, j, k: (i, k))
        b_spec = pl.BlockSpec((tk, tn), lambda i, j, k: (k, j + b_off))
        dims = ((1,), (0,))
    elif mode == "nt":
        a_spec = pl.BlockSpec((tm, tk), lambda i, j, k: (i, k))
        b_spec = pl.BlockSpec((tn, tk), lambda i, j, k: (j + b_off, k))
        dims = ((1,), (1,))
    else:
        a_spec = pl.BlockSpec((tk, tm), lambda i, j, k: (k, i))
        b_spec = pl.BlockSpec((tk, tn), lambda i, j, k: (k, j + b_off))
        dims = ((0,), (0,))
    o_spec = pl.BlockSpec((tm, tn), lambda i, j, k: (i, j))
    has_add = add is not None

    def body(*refs):
        if has_add:
            a_ref, b_ref, add_ref, o_ref, acc_ref = refs
        else:
            a_ref, b_ref, o_ref, acc_ref = refs
        k = pl.program_id(2)
        part = _dot(a_ref[...].astype(bf16), b_ref[...].astype(bf16), dims)

        @pl.when(k == 0)
        def _():
            acc_ref[...] = part

        @pl.when(k > 0)
        def _():
            acc_ref[...] += part

        @pl.when(k == nk - 1)
        def _():
            r = acc_ref[...]
            if has_add:
                r = r + add_ref[...]
            o_ref[...] = r.astype(out_dtype)

    in_specs = [a_spec, b_spec] + ([o_spec] if has_add else [])
    args = (a, b) + ((add,) if has_add else ())
    return pl.pallas_call(
        body, name=name, grid=(m // tm, n // tn, nk), in_specs=in_specs, out_specs=o_spec,
        out_shape=jax.ShapeDtypeStruct((m, n), out_dtype), scratch_shapes=[pltpu.VMEM((tm, tn), f32)],
        compiler_params=_cparams(("parallel", "parallel", "arbitrary")),
    )(*args)


def _in_proj(hn, w_in, widths, name, tm=2048, tn=256):
    m, k = hn.shape
    starts = [sum(widths[:i]) // tn for i in range(len(widths))]
    counts = [wd // tn for wd in widths]
    assert m % tm == 0 and all(wd % tn == 0 for wd in widths)
    n_out = len(widths)

    def body(a_ref, w_ref, *o_refs):
        j = pl.program_id(1)
        acc = _nn(a_ref[...], w_ref[...])
        for s, c, o_ref in zip(starts, counts, o_refs):
            @pl.when((j >= s) & (j < s + c))
            def _(o_ref=o_ref):
                o_ref[...] = acc

    def o_spec(s, c):
        return pl.BlockSpec((tm, tn), lambda i, j: (i, jnp.clip(j - s, 0, c - 1)))

    return pl.pallas_call(
        body, name=name, grid=(m // tm, sum(counts)),
        in_specs=[pl.BlockSpec((tm, k), lambda i, j: (i, 0)), pl.BlockSpec((k, tn), lambda i, j: (0, j))],
        out_specs=[o_spec(s, c) for s, c in zip(starts, counts)],
        out_shape=[jax.ShapeDtypeStruct((m, wd), f32) for wd in widths], compiler_params=_cparams(("parallel", "arbitrary")),
    )(hn, w_in)


def _out_proj(o, yn, w_out, h, name, tm=2048, tn=512):
    m, kb = o.shape
    n = w_out.shape[1]
    n_y = yn.shape[1] // kb
    assert yn.shape[1] % kb == 0 and w_out.shape[0] == kb * (1 + n_y) and m % tm == 0 and n % tn == 0

    def body(*refs):
        o_ref, y_refs, w_refs, h_ref, out_ref = refs[0], refs[1:1 + n_y], refs[1 + n_y:2 + 2 * n_y], refs[-2], refs[-1]
        acc = h_ref[...] + _nn(o_ref[...].astype(bf16), w_refs[0][...])
        for y_ref, w_ref in zip(y_refs, w_refs[1:]):
            acc = acc + _nn(y_ref[...], w_ref[...])
        out_ref[...] = acc

    res = pl.BlockSpec((tm, tn), lambda i, j: (i, j))

    def a_blk(c):
        return pl.BlockSpec((tm, kb), lambda i, j: (i, c))

    def w_blk(r):
        return pl.BlockSpec((kb, tn), lambda i, j: (r, j))

    return pl.pallas_call(
        body, name=name, grid=(m // tm, n // tn),
        in_specs=[a_blk(0)] + [a_blk(c) for c in range(n_y)] + [w_blk(r) for r in range(1 + n_y)] + [res],
        out_specs=res, out_shape=jax.ShapeDtypeStruct((m, n), f32), compiler_params=_cparams(("parallel", "parallel")),
    )(o, *[yn] * n_y, *[w_out] * (1 + n_y), h)


def _swiglu_fwd(hn, w_gate, w_up, name, tm=2048, tn=256):
    m, k = hn.shape
    n = w_gate.shape[0]
    assert m % tm == 0 and n % tn == 0, (name, m, n, tm, tn)

    def body(a_ref, wg_ref, wu_ref, g_ref, u_ref, act_ref):
        a = a_ref[...]
        g = _nt(a, wg_ref[...])
        u = _nt(a, wu_ref[...])
        sg, dsg = _silu_and_grad(g)
        g_ref[...] = (u * dsg).astype(bf16)
        u_ref[...] = sg.astype(bf16)
        act_ref[...] = (sg * u).astype(bf16)

    a_spec = pl.BlockSpec((tm, k), lambda i, j: (i, 0))
    w_spec = pl.BlockSpec((tn, k), lambda i, j: (j, 0))
    o_spec = pl.BlockSpec((tm, tn), lambda i, j: (i, j))
    return pl.pallas_call(
        body, name=name, grid=(m // tm, n // tn), in_specs=[a_spec, w_spec, w_spec], out_specs=[o_spec, o_spec, o_spec],
        out_shape=[jax.ShapeDtypeStruct((m, n), bf16)] * 3,
        compiler_params=_cparams(("parallel", "parallel")),
    )(hn, w_gate, w_up)


def _swiglu_bwd(dh, w_down, g, u, name, tm=2048, tn=256):
    m, k = dh.shape
    n = w_down.shape[0]
    assert m % tm == 0 and n % tn == 0, (name, m, n, tm, tn)

    def body(a_ref, w_ref, g_ref, u_ref, dg_ref, du_ref):
        dact = _nt(a_ref[...].astype(bf16), w_ref[...])
        dg_ref[...] = (dact * g_ref[...].astype(f32)).astype(bf16)
        du_ref[...] = (dact * u_ref[...].astype(f32)).astype(bf16)

    a_spec = pl.BlockSpec((tm, k), lambda i, j: (i, 0))
    w_spec = pl.BlockSpec((tn, k), lambda i, j: (j, 0))
    o_spec = pl.BlockSpec((tm, tn), lambda i, j: (i, j))
    return pl.pallas_call(
        body, name=name, grid=(m // tm, n // tn), in_specs=[a_spec, w_spec, o_spec, o_spec], out_specs=[o_spec, o_spec],
        out_shape=[jax.ShapeDtypeStruct((m, n), bf16), jax.ShapeDtypeStruct((m, n), bf16)],
        compiler_params=_cparams(("parallel", "parallel")),
    )(dh, w_down, g, u)


def _rmsnorm_fwd(h, w, name, tm=512):
    m, d = h.shape

    def body(h_ref, w_ref, o_ref):
        x = h_ref[...]
        r = lax.rsqrt(jnp.mean(x * x, axis=-1, keepdims=True) + EPS)
        o_ref[...] = (x * r * w_ref[...]).astype(bf16)

    return pl.pallas_call(
        body, name=name, grid=(m // tm,),
        in_specs=[pl.BlockSpec((tm, d), lambda i: (i, 0)), pl.BlockSpec((1, d), lambda i: (0, 0))],
        out_specs=pl.BlockSpec((tm, d), lambda i: (i, 0)), out_shape=jax.ShapeDtypeStruct((m, d), bf16),
        compiler_params=_cparams(("parallel",)),
    )(h, w)


def _nt_norm_bwd(pairs, h, w, dres, name, tk, b_is_kd=False, tm=1024):
    m, d = h.shape
    contract = _nn if b_is_kd else _nt
    steps = [p[0].shape[1] // tk for p in pairs]
    assert all(p[0].shape[1] % tk == 0 for p in pairs), (name, tk)
    starts = [sum(steps[:i]) for i in range(len(pairs))]
    nk = sum(steps)
    n_p = len(pairs)

    def body(*refs):
        ab = refs[:2 * n_p]
        h_ref, w_ref, dres_ref, dh_ref, dhb_ref, dw_ref, acc_ref = refs[2 * n_p:]
        i, k = pl.program_id(0), pl.program_id(1)

        @pl.when(k == 0)
        def _():
            acc_ref[...] = jnp.zeros_like(acc_ref)

        for p in range(n_p):
            @pl.when((k >= starts[p]) & (k < starts[p] + steps[p]))
            def _(p=p):
                acc_ref[...] += contract(ab[2 * p][...], ab[2 * p + 1][...])

        @pl.when(k == nk - 1)
        def _():
            x = h_ref[...]
            r = lax.rsqrt(jnp.mean(x * x, axis=-1, keepdims=True) + EPS)
            xhat = x * r
            dy = acc_ref[...]
            gw = dy * w_ref[...]
            dh = dres_ref[...] + r * (gw - xhat * jnp.mean(gw * xhat, axis=-1, keepdims=True))
            dh_ref[...] = dh
            dhb_ref[...] = dh.astype(bf16)
            part = _rowsum8(dy * xhat)

            @pl.when(i == 0)
            def _():
                dw_ref[...] = part

            @pl.when(i > 0)
            def _():
                dw_ref[...] += part

    def clamp(k, p):
        return jnp.clip(k - starts[p], 0, steps[p] - 1)

    in_specs = []
    for p in range(n_p):
        b_spec = (pl.BlockSpec((tk, d), lambda i, k, p=p: (clamp(k, p), 0)) if b_is_kd
                  else pl.BlockSpec((d, tk), lambda i, k, p=p: (0, clamp(k, p))))
        in_specs += [pl.BlockSpec((tm, tk), lambda i, k, p=p: (i, clamp(k, p))), b_spec]
    row = pl.BlockSpec((tm, d), lambda i, k: (i, 0))
    in_specs += [row, pl.BlockSpec((1, d), lambda i, k: (0, 0)), row]
    return pl.pallas_call(
        body, name=name, grid=(m // tm, nk), in_specs=in_specs,
        out_specs=[row, row, pl.BlockSpec((SUBLANES, d), lambda i, k: (0, 0))],
        out_shape=[jax.ShapeDtypeStruct((m, d), f32), jax.ShapeDtypeStruct((m, d), bf16), jax.ShapeDtypeStruct((SUBLANES, d), f32)],
        scratch_shapes=[pltpu.VMEM((tm, d), f32)], compiler_params=_cparams(("arbitrary", "arbitrary")),
    )(*[t for p in pairs for t in p], h, w, dres)


def _final_loss(h, w, target, name, tm=512):
    m, d = h.shape

    def body(h_ref, w_ref, t_ref, dh_ref, dhb_ref, loss_ref, dw_ref):
        x = h_ref[...]
        r = lax.rsqrt(jnp.mean(x * x, axis=-1, keepdims=True) + EPS)
        xhat = x * r
        ww = w_ref[...]
        err = xhat * ww - t_ref[...]
        dy = err * (1.0 / d)
        gw = dy * ww
        dh = r * (gw - xhat * jnp.mean(gw * xhat, axis=-1, keepdims=True))
        dh_ref[...] = dh
        dhb_ref[...] = dh.astype(bf16)
        lpart = _rowsum8(err * err) * (0.5 / d)
        wpart = _rowsum8(dy * xhat)

        @pl.when(pl.program_id(0) == 0)
        def _():
            loss_ref[...] = lpart
            dw_ref[...] = wpart

        @pl.when(pl.program_id(0) > 0)
        def _():
            loss_ref[...] += lpart
            dw_ref[...] += wpart

    row = pl.BlockSpec((tm, d), lambda i: (i, 0))
    acc = pl.BlockSpec((SUBLANES, d), lambda i: (0, 0))
    return pl.pallas_call(
        body, name=name, grid=(m // tm,),
        in_specs=[row, pl.BlockSpec((1, d), lambda i: (0, 0)), row], out_specs=[row, row, acc, acc],
        out_shape=[jax.ShapeDtypeStruct((m, d), f32), jax.ShapeDtypeStruct((m, d), bf16),
                   jax.ShapeDtypeStruct((SUBLANES, d), f32), jax.ShapeDtypeStruct((SUBLANES, d), f32)],
        compiler_params=_cparams(("arbitrary",)),
    )(h, w, target)


def _lane_tables():
    f = np.arange(LANES) % HEAD_DIM
    inv = ROPE_THETA ** (-jnp.arange(0, ROPE_DIM, 2, dtype=f32) / ROPE_DIM)
    invf = jnp.where(f < ROPE_DIM, inv[f % (ROPE_DIM // 2)], 0.0).astype(f32)
    return invf.reshape(1, LANES)


def _rope_tables(pos_col, name):
    t = pos_col.shape[0]
    tm = SEQ

    def body(p_ref, f_ref, c_ref, s1_ref, s2_ref):
        ang = p_ref[...].astype(f32) * f_ref[...]
        co, si = jnp.cos(ang), jnp.sin(ang)
        f = lax.broadcasted_iota(jnp.int32, (tm, LANES), 1) % HEAD_DIM
        c_ref[...] = jnp.where(f < ROPE_DIM, co, 1.0)
        s1_ref[...] = jnp.where(f < ROPE_DIM // 2, -si, 0.0)
        s2_ref[...] = jnp.where((f >= ROPE_DIM // 2) & (f < ROPE_DIM), si, 0.0)

    row = pl.BlockSpec((tm, LANES), lambda i: (i, 0))
    return pl.pallas_call(
        body, name=name, grid=(t // tm,),
        in_specs=[pl.BlockSpec((tm, 1), lambda i: (i, 0)), pl.BlockSpec((1, LANES), lambda i: (0, 0))],
        out_specs=[row, row, row], out_shape=[jax.ShapeDtypeStruct((t, LANES), f32)] * 3,
        compiler_params=_cparams(("parallel",)),
    )(pos_col, _lane_tables())


def _rot(x, c, s1, s2):
    return x * c + pltpu.roll(x, LANES - ROPE_DIM // 2, 1) * s1 + pltpu.roll(x, ROPE_DIM // 2, 1) * s2


def _rot_t(g, c, s1, s2):
    return g * c + pltpu.roll(g * s1, ROPE_DIM // 2, 1) + pltpu.roll(g * s2, LANES - ROPE_DIM // 2, 1)


def _dup_head(x, kvh, low):
    a = jnp.where(kvh == 0, x, pltpu.roll(x, HEAD_DIM, 1))
    return jnp.where(low, a, pltpu.roll(a, HEAD_DIM, 1))


def _deinterleave(src_ref, dst_ref, d, dtype):
    length = SEQ // d
    if d == 1:
        dst_ref[...] = src_ref[...].astype(dtype)
    else:
        for r in range(d):
            dst_ref[pl.ds(r * length, length), :] = src_ref[pl.ds(r, length, stride=d), :].astype(dtype)


def _interleave_store(src_ref, dst_ref, d, accumulate):
    length = SEQ // d
    if d == 1:
        if accumulate:
            dst_ref[...] += src_ref[...]
        else:
            dst_ref[...] = src_ref[...]
    else:
        for r in range(d):
            blk = src_ref[pl.ds(r * length, length), :]
            if accumulate:
                dst_ref[pl.ds(r, length, stride=d), :] = dst_ref[pl.ds(r, length, stride=d), :] + blk
            else:
                dst_ref[pl.ds(r, length, stride=d), :] = blk


def _attn_masks():
    qi = lax.broadcasted_iota(jnp.int32, (ATTN_BLOCK, ATTN_BLOCK), 0)
    ki = lax.broadcasted_iota(jnp.int32, (ATTN_BLOCK, ATTN_BLOCK), 1)
    low = lax.broadcasted_iota(jnp.int32, (ATTN_BLOCK, LANES), 1) < HEAD_DIM
    return ki <= qi, ki >= qi, low


NEG_INF = float("-inf")
ATTN_UNROLL = 4


N_BRANCH = len(DILATIONS)


def _attn_prep(qkv, tabs, name):
    t = qkv.shape[0]
    nb = t // SEQ
    n_j = ATTN_WIDTH // LANES

    def q_body(q_ref, c_ref, s1_ref, s2_ref, out_ref, xr):
        xr[...] = _rot(q_ref[...], c_ref[...], s1_ref[...], s2_ref[...]) * (HEAD_DIM ** -0.5)
        for bi, d in enumerate(DILATIONS):
            _deinterleave(xr, out_ref.at[bi], d, bf16)

    def kv_body(x_ref, c_ref, s1_ref, s2_ref, out_ref, xr):
        lowfull = lax.broadcasted_iota(jnp.int32, (SEQ, LANES), 1) < HEAD_DIM
        x = x_ref[...]
        x = jnp.where(pl.program_id(1) == 0, _rot(x, c_ref[...], s1_ref[...], s2_ref[...]), x)
        for kvh in range(N_KV_HEADS):
            xr[...] = _dup_head(x, kvh, lowfull)
            for bi, d in enumerate(DILATIONS):
                length = SEQ // d
                for r in range(d):
                    rows = xr[...] if d == 1 else xr[pl.ds(r, length, stride=d), :]
                    out_ref[0, bi, pl.ds(r * length, length), kvh * LANES:(kvh + 1) * LANES] = rows.astype(bf16)

    tab = pl.BlockSpec((SEQ, LANES), lambda b, j: (b, 0))
    q = pl.pallas_call(
        q_body, name=name + "_q", grid=(nb, n_j),
        in_specs=[pl.BlockSpec((SEQ, LANES), lambda b, j: (b, j)), tab, tab, tab],
        out_specs=pl.BlockSpec((N_BRANCH, SEQ, LANES), lambda b, j: (0, b, j)),
        out_shape=jax.ShapeDtypeStruct((N_BRANCH, t, ATTN_WIDTH), bf16), scratch_shapes=[pltpu.VMEM((SEQ, LANES), f32)],
        compiler_params=_cparams(("parallel", "parallel")),
    )(qkv, *tabs)
    kv = pl.pallas_call(
        kv_body, name=name + "_kv", grid=(nb, 2),
        in_specs=[pl.BlockSpec((SEQ, LANES), lambda b, j: (b, n_j + j)), tab, tab, tab],
        out_specs=pl.BlockSpec((1, N_BRANCH, SEQ, N_KV_HEADS * LANES), lambda b, j: (j, 0, b, 0)),
        out_shape=jax.ShapeDtypeStruct((2, N_BRANCH, t, N_KV_HEADS * LANES), bf16), scratch_shapes=[pltpu.VMEM((SEQ, LANES), f32)],
        compiler_params=_cparams(("parallel", "parallel")),
    )(qkv, *tabs)
    return q, kv


def _attn_fwd(prep, name):
    q_all, kv_all = prep
    t = q_all.shape[1]
    nb = t // SEQ
    n_blk = SEQ // ATTN_BLOCK

    def body(q_ref, k_ref, v_ref, o_ref, lse_ref, ob, lb, o0, o1, o2, l0, l1, l2, ss):
        cur_ok, prev_ok, low = _attn_masks()
        onat, lnat = (o0, o1, o2), (l0, l1, l2)
        for bi, d in enumerate(DILATIONS):
            qd, kd, vd = q_ref.at[bi], k_ref.at[0, bi], v_ref.at[0, bi]
            per_res = n_blk // d
            use_prev = per_res > 1

            def scores(n, carry):
                start = pl.multiple_of(n * ATTN_BLOCK, ATTN_BLOCK)
                has_prev = (n % per_res) != 0
                pstart = pl.multiple_of(jnp.maximum(n - 1, 0) * ATTN_BLOCK, ATTN_BLOCK)
                qb = qd[pl.ds(start, ATTN_BLOCK), :]
                kc = kd[pl.ds(start, ATTN_BLOCK), :]
                if use_prev:
                    kp = kd[pl.ds(pstart, ATTN_BLOCK), :]
                for a in range(2):
                    qa = jnp.where(low if a == 0 else ~low, qb, jnp.zeros_like(qb))
                    ss[2 * n + a, :, 0:ATTN_BLOCK] = jnp.where(cur_ok, _nt(qa, kc), NEG_INF)
                    if use_prev:
                        ss[2 * n + a, :, ATTN_BLOCK:2 * ATTN_BLOCK] = jnp.where(prev_ok & has_prev, _nt(qa, kp), NEG_INF)
                return carry

            def softmax_pv(n, carry):
                start = pl.multiple_of(n * ATTN_BLOCK, ATTN_BLOCK)
                pstart = pl.multiple_of(jnp.maximum(n - 1, 0) * ATTN_BLOCK, ATTN_BLOCK)
                vc = vd[pl.ds(start, ATTN_BLOCK), :]
                if use_prev:
                    vp = vd[pl.ds(pstart, ATTN_BLOCK), :]
                outs, lses = [], []
                for a in range(2):
                    sc = ss[2 * n + a, :, 0:ATTN_BLOCK]
                    if use_prev:
                        sp = ss[2 * n + a, :, ATTN_BLOCK:2 * ATTN_BLOCK]
                        m = jnp.max(jnp.maximum(sc, sp), axis=1, keepdims=True)
                        pc, pp = jnp.exp(sc - m), jnp.exp(sp - m)
                        den = jnp.sum(pc + pp, axis=1, keepdims=True)
                        acc = _nn(pc.astype(bf16), vc) + _nn(pp.astype(bf16), vp)
                    else:
                        m = jnp.max(sc, axis=1, keepdims=True)
                        pc = jnp.exp(sc - m)
                        den = jnp.sum(pc, axis=1, keepdims=True)
                        acc = _nn(pc.astype(bf16), vc)
                    outs.append(acc * (1.0 / den))
                    lses.append(m + jnp.log(den))
                ob[pl.ds(start, ATTN_BLOCK), :] = jnp.where(low, outs[0], outs[1])
                lb[pl.ds(start, ATTN_BLOCK), :] = jnp.where(low, lses[0], lses[1])
                return carry

            lax.fori_loop(0, n_blk, scores, 0, unroll=ATTN_UNROLL)
            lax.fori_loop(0, n_blk, softmax_pv, 0, unroll=ATTN_UNROLL)
            _interleave_store(ob, onat[bi], d, False)
            _interleave_store(lb, lnat[bi], d, False)
        la, lbb, lc = l0[...], l1[...], l2[...]
        lm = jnp.maximum(jnp.maximum(la, lbb), lc)
        wa, wb, wc = jnp.exp(la - lm), jnp.exp(lbb - lm), jnp.exp(lc - lm)
        ws = wa + wb + wc
        o_ref[...] = (wa * o0[...] + wb * o1[...] + wc * o2[...]) / ws
        lse_ref[...] = lm + jnp.log(ws)

    def col(jj):
        return pl.BlockSpec((SEQ, LANES), lambda b, j: (b, jj if jj is not None else j))

    fs = pltpu.VMEM((SEQ, LANES), f32)
    return pl.pallas_call(
        body, name=name, grid=(nb, ATTN_WIDTH // LANES),
        in_specs=[pl.BlockSpec((N_BRANCH, SEQ, LANES), lambda b, j: (0, b, j)),
                  pl.BlockSpec((1, N_BRANCH, SEQ, LANES), lambda b, j: (0, 0, b, j // 2)),
                  pl.BlockSpec((1, N_BRANCH, SEQ, LANES), lambda b, j: (1, 0, b, j // 2))],
        out_specs=[col(None), col(None)],
        out_shape=[jax.ShapeDtypeStruct((t, ATTN_WIDTH), f32), jax.ShapeDtypeStruct((t, ATTN_WIDTH), f32)],
        scratch_shapes=[fs, fs, fs, fs, fs, fs, fs, fs, pltpu.VMEM((2 * n_blk, ATTN_BLOCK, 2 * ATTN_BLOCK), f32)],
        compiler_params=_cparams(("parallel", "parallel")),
    )(q_all, kv_all, kv_all)


def _attn_bwd(prep, tabs, o, lse, do, name):
    q_all, kv_all = prep
    t = q_all.shape[1]
    nb = t // SEQ
    n_blk = SEQ // ATTN_BLOCK
    n_j = ATTN_WIDTH // LANES

    def body(q_ref, k_ref, v_ref, c_ref, s1_ref, s2_ref, o_ref, lse_ref, do_ref, dq_ref, dk_ref, dv_ref,
             dl, dod, lsd, dld, dqd, dkd, dvd, dqa, dka, dva, pb, dsb, dk_acc, dv_acc):
        j = pl.program_id(1)
        pb[2 * n_blk:2 * n_blk + 2] = jnp.zeros((2, ATTN_BLOCK, 2 * ATTN_BLOCK), bf16)
        dsb[2 * n_blk:2 * n_blk + 2] = jnp.zeros((2, ATTN_BLOCK, 2 * ATTN_BLOCK), bf16)
        kvh = j // 2
        cur_ok, prev_ok, low = _attn_masks()
        lowfull = lax.broadcasted_iota(jnp.int32, (SEQ, LANES), 1) < HEAD_DIM
        c, s1, s2 = c_ref[...], s1_ref[...], s2_ref[...]
        prod = do_ref[...] * o_ref[...]
        d_lo = jnp.sum(jnp.where(lowfull, prod, 0.0), axis=1, keepdims=True)
        d_hi = jnp.sum(jnp.where(lowfull, 0.0, prod), axis=1, keepdims=True)
        dl[...] = jnp.where(lowfull, d_lo, d_hi)
        dqa[...] = jnp.zeros_like(dqa)
        dka[...] = jnp.zeros_like(dka)
        dva[...] = jnp.zeros_like(dva)
        for bi, d in enumerate(DILATIONS):
            qd, kd, vd = q_ref.at[bi], k_ref.at[0, bi], v_ref.at[0, bi]
            _deinterleave(do_ref, dod, d, bf16)
            _deinterleave(lse_ref, lsd, d, f32)
            _deinterleave(dl, dld, d, f32)
            per_res = n_blk // d
            use_prev = per_res > 1
            curl, prevl = slice(0, ATTN_BLOCK), slice(ATTN_BLOCK, 2 * ATTN_BLOCK)

            def halves(x):
                zero = jnp.zeros_like(x)
                return jnp.where(low, x, zero), jnp.where(low, zero, x)

            def probs(n, carry):
                start = pl.multiple_of(n * ATTN_BLOCK, ATTN_BLOCK)
                has_prev = (n % per_res) != 0
                pstart = pl.multiple_of(jnp.maximum(n - 1, 0) * ATTN_BLOCK, ATTN_BLOCK)
                cur, prev = pl.ds(start, ATTN_BLOCK), pl.ds(pstart, ATTN_BLOCK)
                qas, doas = halves(qd[cur, :]), halves(dod[cur, :])
                kc, vc = kd[cur, :], vd[cur, :]
                if use_prev:
                    kp, vp = kd[prev, :], vd[prev, :]
                lsb, dlb = lsd[cur, :], dld[cur, :]
                for a in range(2):
                    ls = lsb[:, a * HEAD_DIM:a * HEAD_DIM + 1]
                    de = dlb[:, a * HEAD_DIM:a * HEAD_DIM + 1]
                    pc = jnp.exp(jnp.where(cur_ok, _nt(qas[a], kc), NEG_INF) - ls)
                    pb[2 * n + a, :, curl] = pc.astype(bf16)
                    dsb[2 * n + a, :, curl] = (pc * (_nt(doas[a], vc) - de)).astype(bf16)
                    if use_prev:
                        pp = jnp.exp(jnp.where(prev_ok & has_prev, _nt(qas[a], kp), NEG_INF) - ls)
                        pb[2 * n + a, :, prevl] = pp.astype(bf16)
                        dsb[2 * n + a, :, prevl] = (pp * (_nt(doas[a], vp) - de)).astype(bf16)
                return carry

            def grads(n, carry):
                start = pl.multiple_of(n * ATTN_BLOCK, ATTN_BLOCK)
                pstart = pl.multiple_of(jnp.maximum(n - 1, 0) * ATTN_BLOCK, ATTN_BLOCK)
                nstart = pl.multiple_of(jnp.minimum(n + 1, n_blk - 1) * ATTN_BLOCK, ATTN_BLOCK)
                cur, prev, nxt = pl.ds(start, ATTN_BLOCK), pl.ds(pstart, ATTN_BLOCK), pl.ds(nstart, ATTN_BLOCK)
                kc = kd[cur, :]
                dqs = [_nn(dsb[2 * n + a, :, curl], kc) for a in range(2)]
                q_rows, do_rows = list(halves(qd[cur, :])), list(halves(dod[cur, :]))
                ds_rows, p_rows = [dsb[2 * n + a, :, curl] for a in range(2)], [pb[2 * n + a, :, curl] for a in range(2)]
                if use_prev:
                    kp = kd[prev, :]
                    dqs = [dqs[a] + _nn(dsb[2 * n + a, :, prevl], kp) for a in range(2)]
                    q_rows += list(halves(qd[nxt, :]))
                    do_rows += list(halves(dod[nxt, :]))
                    ds_rows += [dsb[2 * n + 2 + a, :, prevl] for a in range(2)]
                    p_rows += [pb[2 * n + 2 + a, :, prevl] for a in range(2)]
                dqd[cur, :] = jnp.where(low, dqs[0], dqs[1])
                dkd[cur, :] = _tn(jnp.concatenate(ds_rows, axis=0), jnp.concatenate(q_rows, axis=0))
                dvd[cur, :] = _tn(jnp.concatenate(p_rows, axis=0), jnp.concatenate(do_rows, axis=0))
                return carry

            lax.fori_loop(0, n_blk, probs, 0, unroll=ATTN_UNROLL)
            lax.fori_loop(0, n_blk, grads, 0, unroll=ATTN_UNROLL)
            _interleave_store(dqd, dqa, d, True)
            _interleave_store(dkd, dka, d, True)
            _interleave_store(dvd, dva, d, True)
        dq_ref[...] = _rot_t(dqa[...] * (HEAD_DIM ** -0.5), c, s1, s2).astype(bf16)
        dkf = dka[...]
        dkf = _rot_t(dkf + pltpu.roll(dkf, HEAD_DIM, 1), c, s1, s2)
        dvf = dva[...]
        dvf = dvf + pltpu.roll(dvf, HEAD_DIM, 1)
        mine = (lax.broadcasted_iota(jnp.int32, (SEQ, LANES), 1) // HEAD_DIM) == kvh
        dkc_, dvc_ = jnp.where(mine, dkf, 0.0), jnp.where(mine, dvf, 0.0)

        @pl.when(j == 0)
        def _():
            dk_acc[...] = dkc_
            dv_acc[...] = dvc_

        @pl.when(j > 0)
        def _():
            dk_acc[...] += dkc_
            dv_acc[...] += dvc_

        @pl.when(j == n_j - 1)
        def _():
            dk_ref[...] = dk_acc[...].astype(bf16)
            dv_ref[...] = dv_acc[...].astype(bf16)

    def col(jj):
        return pl.BlockSpec((SEQ, LANES), lambda b, j: (b, jj if jj is not None else j))

    tab = pl.BlockSpec((SEQ, LANES), lambda b, j: (b, 0))
    fs = pltpu.VMEM((SEQ, LANES), f32)
    hs = pltpu.VMEM((SEQ, LANES), bf16)
    return pl.pallas_call(
        body, name=name, grid=(nb, n_j),
        in_specs=[pl.BlockSpec((N_BRANCH, SEQ, LANES), lambda b, j: (0, b, j)),
                  pl.BlockSpec((1, N_BRANCH, SEQ, LANES), lambda b, j: (0, 0, b, j // 2)),
                  pl.BlockSpec((1, N_BRANCH, SEQ, LANES), lambda b, j: (1, 0, b, j // 2)),
                  tab, tab, tab, col(None), col(None), col(None)],
        out_specs=[col(None), tab, tab],
        out_shape=[jax.ShapeDtypeStruct((t, ATTN_WIDTH), bf16), jax.ShapeDtypeStruct((t, LANES), bf16), jax.ShapeDtypeStruct((t, LANES), bf16)],
        scratch_shapes=[fs, hs, fs, fs, fs, fs, fs, fs, fs, fs,
                        pltpu.VMEM((2 * n_blk + 2, ATTN_BLOCK, 2 * ATTN_BLOCK), bf16), pltpu.VMEM((2 * n_blk + 2, ATTN_BLOCK, 2 * ATTN_BLOCK), bf16), fs, fs],
        compiler_params=_cparams(("parallel", "arbitrary")),
    )(q_all, kv_all, kv_all, *tabs, o, lse, do)


def _tap(w_ref, s):
    return w_ref[CONV_WIDTH - 1 - s:CONV_WIDTH - s, :]


def _conv_pre(x, w_ref, b_ref, row):
    shifted = [x] + [jnp.where(row >= s, pltpu.roll(x, s, 0), 0.0) for s in range(1, CONV_WIDTH)]
    pre = b_ref[...] + _tap(w_ref, 0) * x
    for s in range(1, CONV_WIDTH):
        pre = pre + _tap(w_ref, s) * shifted[s]
    return pre, shifted


def _conv_fwd(x, w, b, name, tc=512):
    t, ch = x.shape

    def body(x_ref, w_ref, b_ref, o_ref):
        row = lax.broadcasted_iota(jnp.int32, (SEQ, tc), 0)
        pre, _ = _conv_pre(x_ref[...], w_ref, b_ref, row)
        o_ref[...] = _silu(pre)

    xs = pl.BlockSpec((SEQ, tc), lambda i, j: (i, j))
    return pl.pallas_call(
        body, name=name, grid=(t // SEQ, ch // tc),
        in_specs=[xs, pl.BlockSpec((CONV_WIDTH, tc), lambda i, j: (0, j)), pl.BlockSpec((1, tc), lambda i, j: (0, j))],
        out_specs=xs, out_shape=jax.ShapeDtypeStruct((t, ch), f32),
        compiler_params=_cparams(("parallel", "parallel")),
    )(x, w, b)


def _conv_bwd(x, w, b, dact, name, tc=512):
    t, ch = x.shape

    def body(x_ref, w_ref, b_ref, d_ref, dx_ref, dw_ref, db_ref):
        row = lax.broadcasted_iota(jnp.int32, (SEQ, tc), 0)
        pre, shifted = _conv_pre(x_ref[...], w_ref, b_ref, row)
        dpre = d_ref[...] * _dsilu(pre)
        dx = _tap(w_ref, 0) * dpre
        for s in range(1, CONV_WIDTH):
            dx = dx + _tap(w_ref, s) * jnp.where(row < SEQ - s, pltpu.roll(dpre, SEQ - s, 0), 0.0)
        dx_ref[...] = dx.astype(bf16)
        first = pl.program_id(1) == 0
        parts = [jnp.sum(dpre * shifted[CONV_WIDTH - 1 - k], axis=0, keepdims=True) for k in range(CONV_WIDTH)]
        dbp = jnp.sum(dpre, axis=0, keepdims=True)

        @pl.when(first)
        def _():
            for k in range(CONV_WIDTH):
                dw_ref[k:k + 1, :] = parts[k]
            db_ref[...] = dbp

        @pl.when(jnp.logical_not(first))
        def _():
            for k in range(CONV_WIDTH):
                dw_ref[k:k + 1, :] += parts[k]
            db_ref[...] += dbp

    xs = pl.BlockSpec((SEQ, tc), lambda j, i: (i, j))
    ws = pl.BlockSpec((CONV_WIDTH, tc), lambda j, i: (0, j))
    bs = pl.BlockSpec((1, tc), lambda j, i: (0, j))
    return pl.pallas_call(
        body, name=name, grid=(ch // tc, t // SEQ),
        in_specs=[xs, ws, bs, xs], out_specs=[xs, ws, bs],
        out_shape=[jax.ShapeDtypeStruct((t, ch), bf16), jax.ShapeDtypeStruct((CONV_WIDTH, ch), f32), jax.ShapeDtypeStruct((1, ch), f32)],
        compiler_params=_cparams(("parallel", "arbitrary")),
    )(x, w, b, dact)


GROUP_W = SSM_INNER // SSM_GROUPS
HEADS_PER_GROUP = SSM_HEADS // SSM_GROUPS


def _split3(x):
    hi = x.astype(bf16)
    r1 = x - hi.astype(f32)
    mid = r1.astype(bf16)
    lo = (r1 - mid.astype(f32)).astype(bf16)
    return hi, mid, lo


def _dot_exact(x, sel, dims, x_is_lhs=True):
    parts = _split3(x)
    if x_is_lhs:
        return _dot(parts[0], sel, dims) + _dot(parts[1], sel, dims) + _dot(parts[2], sel, dims)
    return _dot(sel, parts[0], dims) + _dot(sel, parts[1], dims) + _dot(sel, parts[2], dims)


def _ssd_common(xbc_ref, dt_ref, bias_ref, alog_ref):
    r = lax.broadcasted_iota(jnp.int32, (CHUNK, CHUNK), 0)
    cidx = lax.broadcasted_iota(jnp.int32, (CHUNK, CHUNK), 1)
    causal = r >= cidx
    tril = causal.astype(bf16)
    expand = (lax.broadcasted_iota(jnp.int32, (CHUNK, SSM_INNER), 0)
              == lax.broadcasted_iota(jnp.int32, (CHUNK, SSM_INNER), 1) // HEAD_DIM).astype(bf16)
    head_lane = cidx < SSM_HEADS
    dtp = dt_ref[...] + bias_ref[...]
    dt = jnp.where(head_lane, _softplus(dtp), 0.0)
    a_neg = -jnp.exp(alog_ref[...])
    a = dt * a_neg
    nn_dims = ((1,), (0,))
    cs = _dot_exact(a, tril, nn_dims, x_is_lhs=False)
    dt_e = _dot_exact(dt, expand, nn_dims)
    cs_e = _dot_exact(cs, expand, nn_dims)
    xs = xbc_ref[:, 0:SSM_INNER]
    xg = xs * dt_e
    ecs = jnp.exp(cs_e)
    cs_last = cs_e[CHUNK - 1:CHUNK, :]
    dse = jnp.exp(cs_last - cs_e)
    cde = jnp.exp(cs_last)
    return dict(r=r, cidx=cidx, causal=causal, tril=tril, expand=expand, head_lane=head_lane, dtp=dtp, dt=dt, a_neg=a_neg,
                cs=cs, cst=cs.T, dt_e=dt_e, cs_e=cs_e, xs=xs, xg=xg, ecs=ecs, dse=dse, cde=cde)


def _decay_mat(q, h):
    return jnp.exp(jnp.where(q["causal"], q["cs"][:, h:h + 1] - q["cst"][h:h + 1, :], NEG_INF))


def _gate_norm(y, z, nw, gate=None):
    y2 = y * (_silu(z) if gate is None else gate)
    outs, xhats, rs = [], [], []
    for g in range(SSM_GROUPS):
        sl = slice(g * GROUP_W, (g + 1) * GROUP_W)
        yg = y2[:, sl]
        r = lax.rsqrt(jnp.mean(yg * yg, axis=-1, keepdims=True) + EPS)
        xhats.append(yg * r)
        rs.append(r)
        outs.append(yg * r * nw[:, sl])
    return y2, outs, xhats, rs


def _ssd_fwd(xbc, z, dtp, params, name):
    t = xbc.shape[0]
    n_chunk = SEQ // CHUNK
    low = None

    def body(xbc_ref, z_ref, dt_ref, bias_ref, alog_ref, dskip_ref, nw_ref, yn_ref, y_ref, hs_ref, h_scr):
        @pl.when(pl.program_id(1) == 0)
        def _():
            h_scr[...] = jnp.zeros_like(h_scr)

        q = _ssd_common(xbc_ref, dt_ref, bias_ref, alog_ref)
        low = lax.broadcasted_iota(jnp.int32, (CHUNK, LANES), 1) < HEAD_DIM
        xgb = q["xg"].astype(bf16)
        wst = (q["xg"] * q["dse"]).astype(bf16)
        hs_ref[0] = h_scr[...]
        ys = []
        for g in range(SSM_GROUPS):
            gl = slice(g * GROUP_W, (g + 1) * GROUP_W)
            bg = xbc_ref[:, SSM_INNER + g * D_STATE:SSM_INNER + (g + 1) * D_STATE].astype(bf16)
            cg = xbc_ref[:, SSM_INNER + SSM_GROUPS * D_STATE + g * D_STATE:SSM_INNER + SSM_GROUPS * D_STATE + (g + 1) * D_STATE].astype(bf16)
            cb = _nt(cg, bg)
            hg = h_scr[g]
            yoff = _nn(cg, hg.astype(bf16)) * q["ecs"][:, gl]
            pieces = []
            for i in range(HEADS_PER_GROUP // 2):
                h0 = g * HEADS_PER_GROUP + 2 * i
                xp = xgb[:, h0 * HEAD_DIM:(h0 + 2) * HEAD_DIM]
                m0 = (cb * _decay_mat(q, h0)).astype(bf16)
                m1 = (cb * _decay_mat(q, h0 + 1)).astype(bf16)
                zero = jnp.zeros_like(xp)
                pieces.append(_nn(m0, jnp.where(low, xp, zero)) + _nn(m1, jnp.where(low, zero, xp)))
            ys.append(jnp.concatenate(pieces, axis=1) + yoff + dskip_ref[:, gl] * q["xs"][:, gl])
            h_scr[g] = hg * q["cde"][:, gl] + _tn(bg, wst[:, gl])
        y = jnp.concatenate(ys, axis=1)
        y_ref[...] = y
        _, outs, _, _ = _gate_norm(y, z_ref[...], nw_ref[...])
        yn_ref[...] = jnp.concatenate(outs, axis=1).astype(bf16)

    def rows(w):
        return pl.BlockSpec((CHUNK, w), lambda b, c: (b * n_chunk + c, 0))

    def par(w):
        return pl.BlockSpec((1, w), lambda b, c: (0, 0))

    return pl.pallas_call(
        body, name=name, grid=(t // SEQ, n_chunk),
        in_specs=[rows(CONV_CH), rows(SSM_INNER), rows(LANES), par(LANES), par(LANES), par(SSM_INNER), par(SSM_INNER)],
        out_specs=[rows(SSM_INNER), rows(SSM_INNER), pl.BlockSpec((1, SSM_GROUPS, D_STATE, GROUP_W), lambda b, c: (b * n_chunk + c, 0, 0, 0))],
        out_shape=[jax.ShapeDtypeStruct((t, SSM_INNER), bf16), jax.ShapeDtypeStruct((t, SSM_INNER), f32),
                   jax.ShapeDtypeStruct((t // CHUNK, SSM_GROUPS, D_STATE, GROUP_W), f32)],
        scratch_shapes=[pltpu.VMEM((SSM_GROUPS, D_STATE, GROUP_W), f32)],
        compiler_params=_cparams(("parallel", "arbitrary")),
    )(xbc, z, dtp, *params)


def _ssd_bwd(xbc, z, dtp, y, hs, dyn, params, name):
    t = xbc.shape[0]
    n_chunk = SEQ // CHUNK

    def body(xbc_ref, z_ref, dt_ref, y_ref, hs_ref, dyn_ref, bias_ref, alog_ref, dskip_ref, nw_ref,
             dxbc_ref, dz_ref, ddt_ref, dnw_ref, dds_ref, dal_ref, dbi_ref, dh_scr):
        @pl.when(pl.program_id(1) == 0)
        def _():
            dh_scr[...] = jnp.zeros_like(dh_scr)

        q = _ssd_common(xbc_ref, dt_ref, bias_ref, alog_ref)
        low = lax.broadcasted_iota(jnp.int32, (CHUNK, LANES), 1) < HEAD_DIM
        last_row = lax.broadcasted_iota(jnp.int32, (CHUNK, GROUP_W), 0) == CHUNK - 1
        xs, xg = q["xs"], q["xg"]
        xgb = xg.astype(bf16)
        wf = xg * q["dse"]
        wst = wf.astype(bf16)
        zz = z_ref[...]
        yy = y_ref[...]
        sz, dsz = _silu_and_grad(zz)
        y2, _, xhats, rs = _gate_norm(yy, zz, nw_ref[...], gate=sz)
        dyn_ = dyn_ref[...]
        dy2s, dnws = [], []
        for g in range(SSM_GROUPS):
            gl = slice(g * GROUP_W, (g + 1) * GROUP_W)
            gw = dyn_[:, gl] * nw_ref[:, gl]
            dy2s.append(rs[g] * (gw - xhats[g] * jnp.mean(gw * xhats[g], axis=-1, keepdims=True)))
            dnws.append(_rowsum8(dyn_[:, gl] * xhats[g]))
        dy2 = jnp.concatenate(dy2s, axis=1)
        dy = dy2 * sz
        dz_ref[...] = (dy2 * yy * dsz).astype(bf16)
        dnw_p = jnp.concatenate(dnws, axis=1)
        dds_p = _rowsum8(dy * xs)
        dyb = dy.astype(bf16)
        gfull = (dy * q["ecs"]).astype(bf16)
        dcs_c = jnp.zeros((CHUNK, CHUNK), f32)
        dcs_r = jnp.zeros((CHUNK, CHUNK), f32)
        dcs_e_parts, dxg_parts = [], []
        for g in range(SSM_GROUPS):
            gl = slice(g * GROUP_W, (g + 1) * GROUP_W)
            bsl = slice(SSM_INNER + g * D_STATE, SSM_INNER + (g + 1) * D_STATE)
            csl = slice(SSM_INNER + SSM_GROUPS * D_STATE + g * D_STATE, SSM_INNER + SSM_GROUPS * D_STATE + (g + 1) * D_STATE)
            bg = xbc_ref[:, bsl].astype(bf16)
            cg = xbc_ref[:, csl].astype(bf16)
            cb = _nt(cg, bg)
            hg = hs_ref[0, g]
            hgb = hg.astype(bf16)
            dhn = dh_scr[g]
            dhnb = dhn.astype(bf16)
            yoff = _nn(cg, hgb) * q["ecs"][:, gl]
            dw_ = _nn(bg, dhnb)
            r_e = dw_ * wf[:, gl]
            to_last = jnp.sum(r_e, axis=0, keepdims=True) + jnp.sum(dhn * hg, axis=0, keepdims=True) * q["cde"][:, gl]
            dcs_e_parts.append(dy[:, gl] * yoff - r_e + jnp.where(last_row, to_last, 0.0))
            dcb = jnp.zeros((CHUNK, CHUNK), f32)
            dxg_pairs = []
            for i in range(HEADS_PER_GROUP // 2):
                h0 = g * HEADS_PER_GROUP + 2 * i
                psl = slice(h0 * HEAD_DIM, (h0 + 2) * HEAD_DIM)
                xp = xgb[:, psl]
                dyp = dyb[:, psl]
                zero = jnp.zeros_like(dyp)
                tns = []
                for a in range(2):
                    h = h0 + a
                    lm = _decay_mat(q, h)
                    m = cb * lm
                    dm = _nt(jnp.where(low, dyp, zero) if a == 0 else jnp.where(low, zero, dyp), xp)
                    dcb = dcb + dm * lm
                    nmat = dm * m
                    dcs_c = dcs_c + jnp.where(q["cidx"] == h, jnp.sum(nmat, axis=1, keepdims=True), 0.0)
                    dcs_r = dcs_r + jnp.where(q["r"] == h, jnp.sum(nmat, axis=0, keepdims=True), 0.0)
                    tns.append(_tn(m.astype(bf16), dyp))
                dxg_pairs.append(jnp.where(low, tns[0], tns[1]))
            dxg_parts.append(jnp.concatenate(dxg_pairs, axis=1) + dw_ * q["dse"][:, gl])
            dcbb = dcb.astype(bf16)
            dxbc_ref[:, csl] = _nt(gfull[:, gl], hgb) + _nn(dcbb, bg)
            dxbc_ref[:, bsl] = _nt(wst[:, gl], dhnb) + _tn(dcbb, cg)
            dh_scr[g] = dhn * q["cde"][:, gl] + _tn(cg, gfull[:, gl])
        dxg = jnp.concatenate(dxg_parts, axis=1)
        dcs_e = jnp.concatenate(dcs_e_parts, axis=1)
        dxbc_ref[:, 0:SSM_INNER] = dskip_ref[...] * dy + dxg * q["dt_e"]
        dcs = dcs_c - dcs_r.T + _dot_exact(dcs_e, q["expand"], ((1,), (1,)))
        triu = (q["cidx"] >= q["r"]).astype(bf16)
        da = _dot_exact(dcs, triu, ((1,), (0,)), x_is_lhs=False)
        ddt = _dot_exact(dxg * xs, q["expand"], ((1,), (1,))) + da * q["a_neg"]
        ddtp = jnp.where(q["head_lane"], ddt * _sigmoid(q["dtp"]), 0.0)
        ddt_ref[...] = ddtp.astype(bf16)
        dal_p = _rowsum8(da * q["dt"]) * q["a_neg"]
        dbi_p = _rowsum8(ddtp)
        first = (pl.program_id(0) == 0) & (pl.program_id(1) == 0)

        @pl.when(first)
        def _():
            dnw_ref[...] = dnw_p
            dds_ref[...] = dds_p
            dal_ref[...] = dal_p
            dbi_ref[...] = dbi_p

        @pl.when(jnp.logical_not(first))
        def _():
            dnw_ref[...] += dnw_p
            dds_ref[...] += dds_p
            dal_ref[...] += dal_p
            dbi_ref[...] += dbi_p

    def rows(w):
        return pl.BlockSpec((CHUNK, w), lambda b, c: (b * n_chunk + n_chunk - 1 - c, 0))

    def par(w):
        return pl.BlockSpec((1, w), lambda b, c: (0, 0))

    def acc(w):
        return pl.BlockSpec((SUBLANES, w), lambda b, c: (0, 0))

    return pl.pallas_call(
        body, name=name, grid=(t // SEQ, n_chunk),
        in_specs=[rows(CONV_CH), rows(SSM_INNER), rows(LANES), rows(SSM_INNER),
                  pl.BlockSpec((1, SSM_GROUPS, D_STATE, GROUP_W), lambda b, c: (b * n_chunk + n_chunk - 1 - c, 0, 0, 0)),
                  rows(SSM_INNER), par(LANES), par(LANES), par(SSM_INNER), par(SSM_INNER)],
        out_specs=[rows(CONV_CH), rows(SSM_INNER), rows(LANES), acc(SSM_INNER), acc(SSM_INNER), acc(LANES), acc(LANES)],
        out_shape=[jax.ShapeDtypeStruct((t, CONV_CH), f32), jax.ShapeDtypeStruct((t, SSM_INNER), bf16), jax.ShapeDtypeStruct((t, LANES), bf16),
                   jax.ShapeDtypeStruct((SUBLANES, SSM_INNER), f32), jax.ShapeDtypeStruct((SUBLANES, SSM_INNER), f32),
                   jax.ShapeDtypeStruct((SUBLANES, LANES), f32), jax.ShapeDtypeStruct((SUBLANES, LANES), f32)],
        scratch_shapes=[pltpu.VMEM((SSM_GROUPS, D_STATE, GROUP_W), f32)],
        compiler_params=_cparams(("arbitrary", "arbitrary")),
    )(xbc, z, dtp, y, hs, dyn, *params)


def _adamw_update(g, w, m, v):
    mm = ADAM_B1 * m + (1.0 - ADAM_B1) * g
    vv = ADAM_B2 * v + (1.0 - ADAM_B2) * (g * g)
    m_hat = mm / (1.0 - ADAM_B1 ** ADAM_STEP)
    v_hat = vv / (1.0 - ADAM_B2 ** ADAM_STEP)
    return -ADAM_LR * (m_hat / (jnp.sqrt(v_hat) + ADAM_EPS) + ADAM_WD * w), mm, vv


def _adamw(g_parts, w, m, v, name):
    rows, width = w.shape
    n = len(g_parts)
    tr = _row_tile(rows)

    def body(*refs):
        g_refs, (w_ref, m_ref, v_ref, g_out, d_out, m_out, v_out) = refs[:n], refs[n:]
        g = g_refs[0][...].astype(f32)
        for r in g_refs[1:]:
            g = g + r[...].astype(f32)
        g_out[...] = g
        d_out[...], m_out[...], v_out[...] = _adamw_update(g, w_ref[...], m_ref[...], v_ref[...])

    spec = pl.BlockSpec((tr, width), lambda i: (i, 0))
    return pl.pallas_call(
        body, name=name, grid=(rows // tr,), in_specs=[spec] * (n + 3), out_specs=[spec] * 4,
        out_shape=[jax.ShapeDtypeStruct((rows, width), f32)] * 4, compiler_params=_cparams(("parallel",)),
    )(*g_parts, w, m, v)


def _adamw_layers(landed, w, m, v, after, name, layers_on_columns=False):
    depth = len(landed)
    _, rows, width = landed[0].shape
    tr = _row_tile(rows)
    n_i = rows // tr
    at = (lambda ref: ref) if layers_on_columns else (lambda ref: ref.at[0])

    def body(*refs):
        part_refs, (w_ref, m_ref, v_ref, _, g_out, d_out, m_out, v_out) = refs[:depth * N_DEV], refs[depth * N_DEV:]
        for l in range(depth):
            @pl.when(pl.program_id(0) == l)
            def _(l=l):
                g = part_refs[l * N_DEV][0].astype(f32)
                for r in part_refs[l * N_DEV + 1:(l + 1) * N_DEV]:
                    g = g + r[0].astype(f32)
                at(g_out)[...] = g
                at(d_out)[...], at(m_out)[...], at(v_out)[...] = _adamw_update(g, at(w_ref)[...], at(m_ref)[...], at(v_ref)[...])

    def part_spec(l, p):
        return pl.BlockSpec((1, tr, width), lambda ll, i: (p, jnp.where(ll == l, i, jnp.where(ll < l, 0, n_i - 1)), 0))

    state = (pl.BlockSpec((tr, width), lambda ll, i: (i, ll)) if layers_on_columns
             else pl.BlockSpec((1, tr, width), lambda ll, i: (ll, i, 0)))
    return pl.pallas_call(
        body, name=name, grid=(depth, n_i),
        in_specs=[part_spec(l, p) for l in range(depth) for p in range(N_DEV)] + [state] * 3 + [ANY], out_specs=[state] * 4,
        out_shape=[jax.ShapeDtypeStruct(w.shape, f32)] * 4, compiler_params=_cparams(("arbitrary", "arbitrary")),
    )(*[landed[l] for l in range(depth) for _ in range(N_DEV)], w, m, v, after)


def _row_tile(rows, cap=512):
    for cand in range(min(rows, cap) // SUBLANES * SUBLANES, 0, -SUBLANES):
        if rows % cand == 0:
            return cand
    return rows


def _cols_from_devices(g, width, name):
    n_dev, depth, a, b = g.shape

    def body(g_ref, o_ref):
        for i in range(n_dev):
            o_ref[0, :, i * b:(i + 1) * b] = g_ref[i, 0]
        if width > n_dev * b:
            o_ref[0, :, n_dev * b:width] = jnp.zeros((a, width - n_dev * b), o_ref.dtype)

    return pl.pallas_call(
        body, name=name, grid=(depth,), in_specs=[pl.BlockSpec((n_dev, 1, a, b), lambda l: (0, l, 0, 0))],
        out_specs=pl.BlockSpec((1, a, width), lambda l: (l, 0, 0)), out_shape=jax.ShapeDtypeStruct((depth, a, width), g.dtype),
        compiler_params=_cparams(("parallel",)),
    )(g)


def _devices_from_cols(per_layer, b, name, tr=256):
    depth = len(per_layer)
    a, width = per_layer[0].shape

    def body(*refs):
        o_ref = refs[depth]
        for l in range(depth):
            for i in range(N_DEV):
                o_ref[i, l] = refs[l][:, i * b:(i + 1) * b]

    return pl.pallas_call(
        body, name=name, grid=(a // tr,), in_specs=[pl.BlockSpec((tr, width), lambda r: (r, 0))] * depth,
        out_specs=pl.BlockSpec((N_DEV, depth, tr, b), lambda r: (0, 0, r, 0)),
        out_shape=jax.ShapeDtypeStruct((N_DEV, depth, a, b), per_layer[0].dtype), compiler_params=_cparams(("parallel",)),
    )(*per_layer)


def _me():
    return lax.axis_index("x"), lax.axis_index("y"), lax.axis_index("c")


def _allgather_two_level(shards, name):
    n = len(shards)
    per = 7

    def body(*refs):
        ins, outs, token = refs[:n], refs[n:2 * n], refs[2 * n]
        send_sems, recv_sems, local_sems = refs[2 * n + 1:]
        token[...] = jnp.zeros_like(token)
        x, y, c = _me()
        me, sibling = (x, y, c), (x, y, 1 - c)
        chips = [(1 - x, y), (x, 1 - y), (1 - x, 1 - y)]

        def slot(a, p):
            return outs[a].at[4 * p[0] + 2 * p[1] + p[2]]

        def copy(a, k, block, to, src=None):
            return pltpu.make_async_remote_copy(
                src_ref=slot(a, block) if src is None else src, dst_ref=slot(a, block),
                send_sem=send_sems.at[a * per + k], recv_sem=recv_sems.at[a * per + k], device_id=to, device_id_type=MESH)

        mine = [pltpu.make_async_copy(ins[a], slot(a, me), local_sems.at[a]) for a in range(n)]
        for cp in mine:
            cp.start()
        first = []
        for a in range(n):
            first.append(copy(a, 0, me, sibling, src=ins[a]))
            first += [copy(a, 1 + j, me, (*chip, c), src=ins[a]) for j, chip in enumerate(chips)]
        for cp in first:
            cp.start()
        passed = []
        for j, chip in enumerate(chips):
            for a in range(n):
                copy(a, 1 + j, (*chip, c), me).wait_recv()
                fwd = copy(a, 4 + j, (*chip, c), sibling)
                fwd.start()
                passed.append(fwd)
        for a in range(n):
            copy(a, 0, sibling, me).wait_recv()
            for j, chip in enumerate(chips):
                copy(a, 4 + j, (*chip, 1 - c), me).wait_recv()
        for cp in first + passed:
            cp.wait_send()
        for cp in mine:
            cp.wait()

    outs = pl.pallas_call(
        body, name=name, in_specs=[ANY] * n, out_specs=[ANY] * n + [pl.BlockSpec(memory_space=pltpu.VMEM)],
        out_shape=[jax.ShapeDtypeStruct((N_DEV,) + s.shape, s.dtype) for s in shards] + [jax.ShapeDtypeStruct((SUBLANES, LANES), f32)],
        scratch_shapes=[pltpu.SemaphoreType.DMA((n * per,)), pltpu.SemaphoreType.DMA((n * per,)), pltpu.SemaphoreType.DMA((n,))],
    )(*shards)
    return outs[:n], outs[n]


def _allgather_direct(row, name):
    def body(in_ref, out_ref, send_sems, recv_sems, local_sem):
        x, y, c = _me()
        mine = out_ref.at[4 * x + 2 * y + c]
        local = pltpu.make_async_copy(in_ref, mine, local_sem)
        local.start()
        sends = []
        for k in range(1, N_DEV):
            px, py, pc = x ^ (k >> 2), y ^ ((k >> 1) & 1), c ^ (k & 1)
            sends.append(pltpu.make_async_remote_copy(
                src_ref=in_ref, dst_ref=mine, send_sem=send_sems.at[k - 1], recv_sem=recv_sems.at[k - 1],
                device_id=(px, py, pc), device_id_type=MESH))
        for cp in sends:
            cp.start()
        for k in range(1, N_DEV):
            px, py, pc = x ^ (k >> 2), y ^ ((k >> 1) & 1), c ^ (k & 1)
            theirs = out_ref.at[4 * px + 2 * py + pc]
            pltpu.make_async_remote_copy(
                src_ref=in_ref, dst_ref=theirs, send_sem=send_sems.at[k - 1], recv_sem=recv_sems.at[k - 1],
                device_id=(px, py, pc), device_id_type=MESH).wait_recv()
        for cp in sends:
            cp.wait_send()
        local.wait()

    return pl.pallas_call(
        body, name=name, in_specs=[ANY], out_specs=ANY, out_shape=jax.ShapeDtypeStruct((N_DEV,) + row.shape, row.dtype),
        scratch_shapes=[pltpu.SemaphoreType.DMA((N_DEV - 1,)), pltpu.SemaphoreType.DMA((N_DEV - 1,)), pltpu.SemaphoreType.DMA],
    )(row)


N_CHIP = N_DEV // 2
HBM = pl.BlockSpec(memory_space=pltpu.HBM)
SEM = pl.BlockSpec(memory_space=pltpu.SEMAPHORE)
EFFECT = pltpu.SideEffectType.DATAFLOW_SIDE_EFFECTING


def _peer(k):
    x, y, c = _me()
    return x ^ (k >> 2), y ^ ((k >> 1) & 1), c ^ (k & 1)


def _direct_copies(srcs, lands, send_sems, recv_sems, per_peer):
    x, y, c = _me()
    me = 4 * x + 2 * y + c
    copies = []
    for a in range(len(srcs)):
        for k in range(1, N_DEV):
            px, py, pc = _peer(k)
            piece = srcs[a].at[4 * px + 2 * py + pc] if per_peer else srcs[a]
            copies.append(pltpu.make_async_remote_copy(
                src_ref=piece, dst_ref=lands[a].at[me], send_sem=send_sems.at[a * (N_DEV - 1) + k - 1],
                recv_sem=recv_sems.at[a * (N_DEV - 1) + k - 1], device_id=(px, py, pc), device_id_type=MESH))
    return copies


def _direct_start(srcs, lands, per_peer, name):
    n = len(srcs)
    n_sem = n * (N_DEV - 1)

    def body(*refs):
        src_refs, land_refs = refs[:n], refs[n:2 * n]
        send_sems, recv_sems = refs[2 * n], refs[2 * n + 1]
        token = refs[-1]
        for cp in _direct_copies(src_refs, land_refs, send_sems, recv_sems, per_peer):
            cp.start()
        token[...] = jnp.zeros_like(token)

    outs = pl.pallas_call(
        body, name=name,
        out_shape=(pltpu.SemaphoreType.DMA((n_sem,)), pltpu.SemaphoreType.DMA((n_sem,)),
                   *[pltpu.HBM(s.shape, s.dtype) for s in srcs], *[pltpu.HBM(s.shape, s.dtype) for s in lands],
                   jax.ShapeDtypeStruct((SUBLANES, LANES), f32)),
        in_specs=[HBM] * (2 * n), out_specs=(SEM, SEM, *[HBM] * (2 * n), pl.BlockSpec(memory_space=pltpu.VMEM)),
        input_output_aliases={i: 2 + i for i in range(2 * n)},
        compiler_params=pltpu.CompilerParams(has_side_effects=EFFECT),
    )(*[pltpu.with_memory_space_constraint(s, pltpu.HBM) for s in srcs], *[pltpu.with_memory_space_constraint(s, pltpu.HBM) for s in lands])
    return outs[0], outs[1], outs[2:2 + n], outs[2 + n:2 + 2 * n], outs[-1]


def _direct_wait(send_sems, recv_sems, srcs, lands, after, per_peer, name):
    n = len(srcs)

    def body(*refs):
        src_refs, land_refs = refs[:n], refs[n:2 * n]
        s_sems, r_sems = refs[2 * n], refs[2 * n + 1]
        for cp in _direct_copies(src_refs, land_refs, s_sems, r_sems, per_peer):
            cp.wait_send()
            cp.wait_recv()

    outs = pl.pallas_call(
        body, name=name,
        out_shape=tuple(pltpu.HBM(s.shape, s.dtype) for s in list(srcs) + list(lands)),
        in_specs=[HBM] * (2 * n) + [SEM, SEM, ANY], out_specs=tuple([HBM] * (2 * n)),
        input_output_aliases={i: i for i in range(2 * n)},
        compiler_params=pltpu.CompilerParams(has_side_effects=EFFECT),
    )(*srcs, *lands, send_sems, recv_sems, after)
    return outs[n:]


def _row(v, width=None):
    v = v.reshape(1, -1).astype(f32)
    if width is not None and v.shape[1] < width:
        v = jnp.pad(v, ((0, 0), (0, width - v.shape[1])))
    return v


def _layer_params(p, l):
    return dict(
        norm_mix=_row(p["norm_mix"][l]), norm_ffn=_row(p["norm_ffn"][l]), conv_w=p["conv_w"][l], conv_b=_row(p["conv_b"][l]),
        ssd=(_row(p["dt_bias"][l], LANES), _row(p["a_log"][l], LANES), _row(jnp.repeat(p["d_skip"][l], HEAD_DIM)), _row(p["ssm_norm"][l])))


def _layer_fwd(h, w_in, rest, sp, tabs, l):
    tag = f"l{l}_"
    hn = _rmsnorm_fwd(h, sp["norm_mix"], tag + "norm_mix")
    qkv, z, xbc_pre = _in_proj(hn, w_in, (QKV_WIDTH, SSM_INNER, CONV_CH), tag + "proj")
    dtp = _matmul(hn, w_in, mode="nn", n_out=LANES, tn=LANES, b_off=DT_OFF // LANES, name=tag + "proj_dt")
    prep = _attn_prep(qkv, tabs, tag + "attn_prep")
    o, lse = _attn_fwd(prep, tag + "attn_fwd")
    xbc = _conv_fwd(xbc_pre, sp["conv_w"], sp["conv_b"], tag + "conv_fwd")
    yn, y, hs = _ssd_fwd(xbc, z, dtp, sp["ssd"], tag + "ssd_fwd")
    w_out, w_gate, w_up, w_down = rest(yn) if callable(rest) else rest
    h2 = _out_proj(o, yn, w_out, h, tag + "out_proj")
    hn2 = _rmsnorm_fwd(h2, sp["norm_ffn"], tag + "norm_ffn")
    g, u, act = _swiglu_fwd(hn2, w_gate, w_up, tag + "ffn_up")
    h3 = _matmul(act, w_down, mode="nn", tk=1408, add=h2, name=tag + "ffn_down")
    saved = dict(h=h, hn=hn, prep=prep, z=z, xbc_pre=xbc_pre, dtp=dtp, o=o, lse=lse, xbc=xbc, yn=yn, y=y, hs=hs, h2=h2, hn2=hn2, g=g, u=u, act=act,
                 rest=(w_out, w_gate, w_up, w_down))
    return h3, saved


def _layer_bwd(dh3_pair, s, big, sp, tabs, l, gd=f32, after_ffn=None):
    tag = f"l{l}_"
    dh3, dh3b = dh3_pair
    w_in, w_out, w_gate, w_up, w_down = big
    dg, du = _swiglu_bwd(dh3b, w_down, s["g"], s["u"], tag + "ffn_down_bwd")
    dw_down = _matmul(s["act"], dh3b, mode="tn", tm=1408, tn=512, tk=2048, out_dtype=gd, name=tag + "dw_down")
    dw_gate = _matmul(dg, s["hn2"], mode="tn", tm=1408, tn=512, tk=2048, out_dtype=gd, name=tag + "dw_gate")
    dw_up = _matmul(du, s["hn2"], mode="tn", tm=1408, tn=512, tk=2048, out_dtype=gd, name=tag + "dw_up")
    norm_ffn = sp["norm_ffn"] if after_ffn is None else sp["norm_ffn"] + after_ffn(dict(w_gate=dw_gate, w_up=dw_up, w_down=dw_down))
    dh2, dh2b, dnf = _nt_norm_bwd([(dg, w_gate), (du, w_up)], s["h2"], norm_ffn, dh3, tag + "ffn_up_bwd_norm", tk=1408, b_is_kd=True, tm=512)
    d_o = _matmul(dh2b, w_out, mode="nt", n_out=ATTN_WIDTH, tn=512, b_off=0, name=tag + "out_attn_bwd")
    dyn = _matmul(dh2b, w_out, mode="nt", n_out=SSM_INNER, tn=512, b_off=1, name=tag + "out_ssm_bwd")
    dw_out = jnp.concatenate([_matmul(s["o"], dh2b, mode="tn", tm=512, tn=512, tk=2048, out_dtype=gd, name=tag + "dw_out_attn"),
                              _matmul(s["yn"], dh2b, mode="tn", tm=512, tn=512, tk=2048, out_dtype=gd, name=tag + "dw_out_ssm")], axis=0)
    dxbc, dz, ddtp, dnw, dds, dal, dbi = _ssd_bwd(s["xbc"], s["z"], s["dtp"], s["y"], s["hs"], dyn, sp["ssd"], tag + "ssd_bwd")
    dxbc_pre, dconv_w, dconv_b = _conv_bwd(s["xbc_pre"], sp["conv_w"], sp["conv_b"], dxbc, tag + "conv_bwd")
    dq, dk, dv = _attn_bwd(s["prep"], tabs, s["o"], s["lse"], d_o, tag + "attn_bwd")
    dproj = jnp.concatenate([dq, dk, dv, dz, dxbc_pre, ddtp], axis=1)
    dw_in = _matmul(s["hn"], dproj, mode="tn", tm=512, tn=1152, tk=2048, out_dtype=gd, name=tag + "dw_in")
    dh, dhb, dnm = _nt_norm_bwd([(dproj, w_in)], s["h"], sp["norm_mix"], dh2, tag + "proj_bwd_norm", tk=1152)
    grads = dict(
        norm_mix=dnm.sum(0), w_in=dw_in, conv_w=dconv_w, conv_b=dconv_b[0], dt_bias=dbi.sum(0)[:SSM_HEADS], a_log=dal.sum(0)[:SSM_HEADS],
        d_skip=dds.sum(0).reshape(SSM_HEADS, HEAD_DIM).sum(1), ssm_norm=dnw.sum(0), w_out=dw_out, norm_ffn=dnf.sum(0),
        w_gate=dw_gate, w_up=dw_up, w_down=dw_down)
    return (dh, dhb), grads


def _local_step(x, positions, target, p, bigs):
    tabs = _rope_tables(positions.reshape(-1, 1), "rope_tables")
    h = x
    saved, sps = [], []
    for l in range(DEPTH):
        sps.append(_layer_params(p, l))
        h, s = _layer_fwd(h, bigs[l][0], bigs[l][1:], sps[l], tabs, l)
        saved.append(s)
    dh, dhb, loss_parts, dfn = _final_loss(h, _row(p["final_norm"]), target, "final_loss")
    dh = (dh, dhb)
    layer_grads = [None] * DEPTH
    for l in reversed(range(DEPTH)):
        dh, layer_grads[l] = _layer_bwd(dh, saved[l], bigs[l], sps[l], tabs, l)
    grads = {k: [layer_grads[l][k] for l in range(DEPTH)] for k in layer_grads[0]}
    grads["final_norm"] = dfn.sum(0)
    return jnp.sum(loss_parts), dh[0], grads


BIG = ("w_in", "w_out", "w_gate", "w_up", "w_down")
REST = BIG[1:]
FFN = ("w_gate", "w_up", "w_down")
MIX = ("w_in", "w_out")
COL_SHARDED = ("w_in",)
TRANSPOSED = ("w_gate", "w_up")
SMALL = ("norm_mix", "conv_b", "dt_bias", "a_log", "d_skip", "ssm_norm", "norm_ffn", "final_norm")
WEIGHTS = ("norm_mix", "w_in", "conv_w", "conv_b", "dt_bias", "a_log", "d_skip", "ssm_norm", "w_out", "norm_ffn", "w_gate", "w_up", "w_down", "final_norm")
SMALL_ROWS = 88
CONVW_ROWS = 96
CONVW_SHARD_ROWS = 16


def _full_from_gathered(name, g, l):
    _, a, b = g.shape
    if name in COL_SHARDED:
        width = IN_PROJ_PAD if name == "w_in" else N_DEV * b
        return _cols_from_devices(g.reshape(N_DEV, 1, a, b), width, f"cols_l{l}_{name}").reshape(a, width)
    return g.reshape(N_DEV * a, b)


def _by_device(name, full, shard_shape, l):
    a, b = shard_shape
    if name in COL_SHARDED:
        return _devices_from_cols([full], b, f"devs_l{l}_{name}").reshape(N_CHIP, 2, a, b)
    return full.reshape(N_CHIP, 2, a, b)


def _pack_rows(parts, rows, width):
    flat = jnp.concatenate([q.reshape(-1) for q in parts])
    return jnp.pad(flat, (0, rows * width - flat.shape[0])).reshape(rows, width)


def _unpack(flat, like):
    out, off = [], 0
    for q in like:
        out.append(flat[off:off + q.size].reshape(q.shape))
        off += q.size
    return out


def kernel(x, positions, norm_mix, w_in, conv_w, conv_b, dt_bias, a_log, d_skip, ssm_norm, w_out, norm_ffn, w_gate, w_up, w_down, final_norm, loss_target, m_norm_mix, m_w_in, m_conv_w, m_conv_b, m_dt_bias, m_a_log, m_d_skip, m_ssm_norm, m_w_out, m_norm_ffn, m_w_gate, m_w_up, m_w_down, m_final_norm, v_norm_mix, v_w_in, v_conv_w, v_conv_b, v_dt_bias, v_a_log, v_d_skip, v_ssm_norm, v_w_out, v_norm_ffn, v_w_gate, v_w_up, v_w_down, v_final_norm):
    w = dict(norm_mix=norm_mix, w_in=w_in, conv_w=conv_w, conv_b=conv_b, dt_bias=dt_bias, a_log=a_log, d_skip=d_skip, ssm_norm=ssm_norm,
             w_out=w_out, norm_ffn=norm_ffn, w_gate=w_gate, w_up=w_up, w_down=w_down, final_norm=final_norm)
    m = dict(norm_mix=m_norm_mix, w_in=m_w_in, conv_w=m_conv_w, conv_b=m_conv_b, dt_bias=m_dt_bias, a_log=m_a_log, d_skip=m_d_skip,
             ssm_norm=m_ssm_norm, w_out=m_w_out, norm_ffn=m_norm_ffn, w_gate=m_w_gate, w_up=m_w_up, w_down=m_w_down, final_norm=m_final_norm)
    v = dict(norm_mix=v_norm_mix, w_in=v_w_in, conv_w=v_conv_w, conv_b=v_conv_b, dt_bias=v_dt_bias, a_log=v_a_log, d_skip=v_d_skip,
             ssm_norm=v_ssm_norm, w_out=v_w_out, norm_ffn=v_norm_ffn, w_gate=v_w_gate, w_up=v_w_up, w_down=v_w_down, final_norm=v_final_norm)
    ax, ay, ac = lax.axis_index("x"), lax.axis_index("y"), lax.axis_index("c")
    dev = 4 * ax + 2 * ay + ac

    assert DEPTH == 2
    t = x.shape[0] * x.shape[1]
    xf, target = x.reshape(t, D_MODEL), loss_target.reshape(t, D_MODEL)

    def own_slot(block):
        return lax.dynamic_update_slice(lax.empty((N_DEV,) + block.shape[1:], block.dtype), block, (dev,) + (0,) * (block.ndim - 1))

    def layer_shard(arr, k, l):
        return jnp.transpose(arr, (2, 0, 1))[:, l, :] if k in TRANSPOSED else arr[l]

    def gather_start(keys, l, tie, name):
        shards = [(layer_shard(w[keys[0]], keys[0], l) + tie).astype(bf16)] + [layer_shard(w[k], k, l).astype(bf16) for k in keys[1:]]
        return _direct_start(shards, [own_slot(s[None]) for s in shards], False, name)

    def scatter_start(keys, grads_l, l, name):
        shapes = [(w[k].shape[2], w[k].shape[1]) if k in TRANSPOSED else w[k].shape[1:] for k in keys]
        by_dev = [_by_device(k, grads_l[k], sh, l).reshape((N_DEV,) + sh) for k, sh in zip(keys, shapes)]
        return _direct_start(by_dev, [own_slot(lax.dynamic_slice_in_dim(g, dev, 1, 0)) for g in by_dev], True, name)

    (g_in0, conv_all), tie = _allgather_two_level([w["w_in"][0].astype(bf16), w["conv_w"]], "gather_l0_w_in")
    rest0_copy = gather_start(REST, 0, tie[0, 0], "gather_l0_rest_start")
    l1_copy = gather_start(BIG, 1, rest0_copy[4][0, 0], "gather_l1_start")
    p = {k: w[k] for k in SMALL}
    p["norm_mix"] = p["norm_mix"] + l1_copy[4][0, 0]
    p["conv_w"] = jnp.transpose(conv_all, (1, 2, 0, 3)).reshape(DEPTH, CONV_WIDTH, CONV_CH)
    sp0, sp1 = _layer_params(p, 0), _layer_params(p, 1)

    def rest0(after):
        lands = _direct_wait(*rest0_copy[:4], after, False, "gather_l0_rest_wait")
        return tuple(_full_from_gathered(k, g, 0) for k, g in zip(REST, lands))

    tabs = _rope_tables(positions.reshape(t, 1), "rope_tables")
    w_in0 = _full_from_gathered("w_in", g_in0, 0)
    h1, saved0 = _layer_fwd(xf, w_in0, rest0, sp0, tabs, 0)
    lands1 = _direct_wait(*l1_copy[:4], h1, False, "gather_l1_wait")
    bigs1 = tuple(_full_from_gathered(k, g, 1) for k, g in zip(BIG, lands1))
    h2, saved1 = _layer_fwd(h1, bigs1[0], bigs1[1:], sp1, tabs, 1)
    dh, dhb, loss_parts, dfn = _final_loss(h2, _row(p["final_norm"]), target, "final_loss")
    loss_local = jnp.sum(loss_parts)

    dh, grads1 = _layer_bwd((dh, dhb), saved1, bigs1, sp1, tabs, 1, gd=bf16)
    l1_grads = scatter_start(BIG, grads1, 1, "scatter_l1_start")
    w_out0, w_gate0, w_up0, w_down0 = saved0["rest"]
    bigs0 = (w_in0, w_out0, w_gate0, w_up0, w_down0 + l1_grads[4][0, 0].astype(bf16))
    ffn0_grads = []

    def after_ffn(grads_ffn):
        ffn0_grads.append(scatter_start(FFN, grads_ffn, 0, "scatter_l0_ffn_start"))
        return ffn0_grads[0][4][0, 0]

    (dx, _), grads0 = _layer_bwd(dh, saved0, bigs0, sp0, tabs, 0, gd=bf16, after_ffn=after_ffn)
    mix0_grads = scatter_start(MIX, grads0, 0, "scatter_l0_mix_start")
    landed = {(k, 1): g for k, g in zip(BIG, _direct_wait(*l1_grads[:4], dx, True, "scatter_l1_wait"))}
    landed.update({(k, 0): g for k, g in zip(FFN, _direct_wait(*ffn0_grads[0][:4], dx, True, "scatter_l0_ffn_wait"))})
    out_g, out_d, out_m, out_v = {}, {}, {}, {}

    def update(keys, after):
        for k in keys:
            parts = [landed[k, l] for l in range(DEPTH)]
            if k in TRANSPOSED:
                depth, a, b = w[k].shape
                state = [jnp.transpose(s, (2, 0, 1)).reshape(b, depth * a) for s in (w[k], m[k], v[k])]
                res = _adamw_layers(parts, *state, after, "adamw_" + k, layers_on_columns=True)
                res = [jnp.transpose(r.reshape(b, depth, a), (1, 2, 0)) for r in res]
            else:
                res = _adamw_layers(parts, w[k], m[k], v[k], after, "adamw_" + k)
            for dst, r in zip((out_g, out_d, out_m, out_v), res):
                dst[k] = r

    update(FFN, mix0_grads[4])
    grads = {k: [grads0[k], grads1[k]] for k in grads0 if k not in BIG}
    grads["final_norm"] = dfn.sum(0) + mix0_grads[4][0, 0]

    small_like = [w[k] for k in SMALL]
    small_grads = [jnp.stack(grads[k]) if k != "final_norm" else grads[k] for k in SMALL]
    small_pack = jnp.concatenate([_pack_rows(small_grads, SMALL_ROWS, LANES), _pack_rows([jnp.stack(grads["conv_w"])], CONVW_ROWS, LANES)], axis=0)
    parts = _allgather_direct(small_pack, "gather_small_grads")
    g_s, d_s, m_s, v_s = _adamw(
        [parts[i, :SMALL_ROWS] for i in range(N_DEV)], _pack_rows(small_like, SMALL_ROWS, LANES),
        _pack_rows([m[k] for k in SMALL], SMALL_ROWS, LANES), _pack_rows([v[k] for k in SMALL], SMALL_ROWS, LANES), "adamw_replicated")
    for dst, src in ((out_g, g_s), (out_d, d_s), (out_m, m_s), (out_v, v_s)):
        dst.update(zip(SMALL, _unpack(src.reshape(-1), small_like)))
    shard_w = conv_w.shape[-1]
    conv_parts = parts[:, SMALL_ROWS:].reshape(N_DEV, DEPTH, CONV_WIDTH, CONV_CH)
    conv_mine = lax.dynamic_slice_in_dim(conv_parts, dev * shard_w, shard_w, axis=3)
    g_c, d_c, m_c, v_c = _adamw(
        [_pack_rows([conv_mine[i]], CONVW_SHARD_ROWS, LANES) for i in range(N_DEV)], _pack_rows([conv_w], CONVW_SHARD_ROWS, LANES),
        _pack_rows([m["conv_w"]], CONVW_SHARD_ROWS, LANES), _pack_rows([v["conv_w"]], CONVW_SHARD_ROWS, LANES), "adamw_conv_w")
    for dst, src in ((out_g, g_c), (out_d, d_c), (out_m, m_c), (out_v, v_c)):
        dst["conv_w"] = src.reshape(-1)[:conv_w.size].reshape(conv_w.shape)

    landed.update({(k, 0): g for k, g in zip(MIX, _direct_wait(*mix0_grads[:4], v_c + out_v["w_down"][0, :CONVW_SHARD_ROWS, :LANES], True, "scatter_l0_mix_wait"))})
    update(MIX, v_c)

    loss = lax.psum(loss_local, ("x", "y", "c"))
    return (loss, dx.reshape(x.shape), *[out_g[k] for k in WEIGHTS], *[out_d[k] for k in WEIGHTS],
            *[out_m[k] for k in WEIGHTS], *[out_v[k] for k in WEIGHTS])
```

```python
import jax
import jax.numpy as jnp
import numpy as np
from jax import lax
from jax.experimental import pallas as pl
from jax.experimental.pallas import tpu as pltpu

f32 = jnp.float32
bf16 = jnp.bfloat16

D_MODEL = 1024
SEQ = 2048
DEPTH = 2
HEAD_DIM = 64
N_ATTN_HEADS = 8
N_KV_HEADS = 2
ATTN_WIDTH = 512
KV_WIDTH = 128
ROPE_DIM = 16
ROPE_THETA = 500000.0
DILATIONS = (1, 4, 16)
ATTN_BLOCK = 128
SSM_HEADS = 16
SSM_INNER = 1024
SSM_GROUPS = 2
D_STATE = 128
CONV_WIDTH = 4
CHUNK = 128
CONV_CH = 1536
MIX_WIDTH = 1536
QKV_WIDTH = ATTN_WIDTH + 2 * KV_WIDTH
DT_OFF = 3328
IN_PROJ = 3344
IN_PROJ_PAD = 3456
FFN_HIDDEN = 2816
EPS = 1e-5
N_DEV = 8
ADAM_LR = 0.001
ADAM_B1 = 0.9
ADAM_B2 = 0.999
ADAM_EPS = 1e-08
ADAM_WD = 0.01
ADAM_STEP = 10

LANES = 128
SUBLANES = 8
VMEM_LIMIT = 56 * 1024 * 1024

MESH = pl.DeviceIdType.MESH
ANY = pl.BlockSpec(memory_space=pl.ANY)


def _cparams(sem, vmem=None):
    return pltpu.CompilerParams(dimension_semantics=sem, vmem_limit_bytes=vmem or VMEM_LIMIT)


def _sigmoid(x):
    return 1.0 / (1.0 + jnp.exp(-x))


def _silu(x):
    return x * _sigmoid(x)


def _dsilu(x):
    s = _sigmoid(x)
    return s * (1.0 + x * (1.0 - s))


def _silu_and_grad(x):
    s = _sigmoid(x)
    return x * s, s * (1.0 + x * (1.0 - s))


def _softplus(x):
    return jnp.maximum(x, 0.0) + jnp.log(1.0 + jnp.exp(-jnp.abs(x)))


def _dot(a, b, dims, precision=None):
    return lax.dot_general(a, b, (dims, ((), ())), preferred_element_type=f32, precision=precision)


def _nn(a, b, precision=None):
    return _dot(a, b, ((1,), (0,)), precision)


def _nt(a, b):
    return _dot(a, b, ((1,), (1,)))


def _tn(a, b):
    return _dot(a, b, ((0,), (0,)))


def _rowsum8(t):
    n, w = t.shape
    return jnp.sum(t.reshape(n // SUBLANES, SUBLANES, w), axis=0)


def _matmul(a, b, *, mode, n_out=None, b_off=0, add=None, out_dtype=f32, tm=2048, tn=512, tk=1024, name):
    if mode == "tn":
        kk, m = a.shape
    else:
        m, kk = a.shape
    n = n_out if n_out is not None else (b.shape[0] if mode == "nt" else b.shape[1])
    tm, tn, tk = min(tm, m), min(tn, n), min(tk, kk)
    assert m % tm == 0 and n % tn == 0 and kk % tk == 0, (name, m, n, kk, tm, tn, tk)
    nk = kk // tk
    if mode == "nn":
        a_spec = pl.BlockSpec((tm, tk), lambda i, j, k: (i, k))
        b_spec = pl.BlockSpec((tk, tn), lambda i, j, k: (k, j + b_off))
        dims = ((1,), (0,))
    elif mode == "nt":
        a_spec = pl.BlockSpec((tm, tk), lambda i, j, k: (i, k))
        b_spec = pl.BlockSpec((tn, tk), lambda i, j, k: (j + b_off, k))
        dims = ((1,), (1,))
    else:
        a_spec = pl.BlockSpec((tk, tm), lambda i, j, k: (k, i))
        b_spec = pl.BlockSpec((tk, tn), lambda i, j, k: (k, j + b_off))
        dims = ((0,), (0,))
    o_spec = pl.BlockSpec((tm, tn), lambda i, j, k: (i, j))
    has_add = add is not None

    def body(*refs):
        if has_add:
            a_ref, b_ref, add_ref, o_ref, acc_ref = refs
        else:
            a_ref, b_ref, o_ref, acc_ref = refs
        k = pl.program_id(2)
        part = _dot(a_ref[...].astype(bf16), b_ref[...].astype(bf16), dims)

        @pl.when(k == 0)
        def _():
            acc_ref[...] = part

        @pl.when(k > 0)
        def _():
            acc_ref[...] += part

        @pl.when(k == nk - 1)
        def _():
            r = acc_ref[...]
            if has_add:
                r = r + add_ref[...]
            o_ref[...] = r.astype(out_dtype)

    in_specs = [a_spec, b_spec] + ([o_spec] if has_add else [])
    args = (a, b) + ((add,) if has_add else ())
    return pl.pallas_call(
        body, name=name, grid=(m // tm, n // tn, nk), in_specs=in_specs, out_specs=o_spec,
        out_shape=jax.ShapeDtypeStruct((m, n), out_dtype), scratch_shapes=[pltpu.VMEM((tm, tn), f32)],
        compiler_params=_cparams(("parallel", "parallel", "arbitrary")),
    )(*args)


def _in_proj(hn, w_in, widths, name, tm=2048, tn=256):
    m, k = hn.shape
    starts = [sum(widths[:i]) // tn for i in range(len(widths))]
    counts = [wd // tn for wd in widths]
    assert m % tm == 0 and all(wd % tn == 0 for wd in widths)
    n_out = len(widths)

    def body(a_ref, w_ref, *o_refs):
        j = pl.program_id(1)
        acc = _nn(a_ref[...], w_ref[...])
        for s, c, o_ref in zip(starts, counts, o_refs):
            @pl.when((j >= s) & (j < s + c))
            def _(o_ref=o_ref):
                o_ref[...] = acc

    def o_spec(s, c):
        return pl.BlockSpec((tm, tn), lambda i, j: (i, jnp.clip(j - s, 0, c - 1)))

    return pl.pallas_call(
        body, name=name, grid=(m // tm, sum(counts)),
        in_specs=[pl.BlockSpec((tm, k), lambda i, j: (i, 0)), pl.BlockSpec((k, tn), lambda i, j: (0, j))],
        out_specs=[o_spec(s, c) for s, c in zip(starts, counts)],
        out_shape=[jax.ShapeDtypeStruct((m, wd), f32) for wd in widths], compiler_params=_cparams(("parallel", "arbitrary")),
    )(hn, w_in)


def _out_proj(o, yn, w_out, h, name, tm=2048, tn=512):
    m, kb = o.shape
    n = w_out.shape[1]
    n_y = yn.shape[1] // kb
    assert yn.shape[1] % kb == 0 and w_out.shape[0] == kb * (1 + n_y) and m % tm == 0 and n % tn == 0

    def body(*refs):
        o_ref, y_refs, w_refs, h_ref, out_ref = refs[0], refs[1:1 + n_y], refs[1 + n_y:2 + 2 * n_y], refs[-2], refs[-1]
        acc = h_ref[...] + _nn(o_ref[...].astype(bf16), w_refs[0][...])
        for y_ref, w_ref in zip(y_refs, w_refs[1:]):
            acc = acc + _nn(y_ref[...], w_ref[...])
        out_ref[...] = acc

    res = pl.BlockSpec((tm, tn), lambda i, j: (i, j))

    def a_blk(c):
        return pl.BlockSpec((tm, kb), lambda i, j: (i, c))

    def w_blk(r):
        return pl.BlockSpec((kb, tn), lambda i, j: (r, j))

    return pl.pallas_call(
        body, name=name, grid=(m // tm, n // tn),
        in_specs=[a_blk(0)] + [a_blk(c) for c in range(n_y)] + [w_blk(r) for r in range(1 + n_y)] + [res],
        out_specs=res, out_shape=jax.ShapeDtypeStruct((m, n), f32), compiler_params=_cparams(("parallel", "parallel")),
    )(o, *[yn] * n_y, *[w_out] * (1 + n_y), h)


def _swiglu_fwd(hn, w_gate, w_up, name, tm=2048, tn=256):
    m, k = hn.shape
    n = w_gate.shape[0]
    assert m % tm == 0 and n % tn == 0, (name, m, n, tm, tn)

    def body(a_ref, wg_ref, wu_ref, g_ref, u_ref, act_ref):
        a = a_ref[...]
        g = _nt(a, wg_ref[...])
        u = _nt(a, wu_ref[...])
        sg, dsg = _silu_and_grad(g)
        g_ref[...] = (u * dsg).astype(bf16)
        u_ref[...] = sg.astype(bf16)
        act_ref[...] = (sg * u).astype(bf16)

    a_spec = pl.BlockSpec((tm, k), lambda i, j: (i, 0))
    w_spec = pl.BlockSpec((tn, k), lambda i, j: (j, 0))
    o_spec = pl.BlockSpec((tm, tn), lambda i, j: (i, j))
    return pl.pallas_call(
        body, name=name, grid=(m // tm, n // tn), in_specs=[a_spec, w_spec, w_spec], out_specs=[o_spec, o_spec, o_spec],
        out_shape=[jax.ShapeDtypeStruct((m, n), bf16)] * 3,
        compiler_params=_cparams(("parallel", "parallel")),
    )(hn, w_gate, w_up)


def _swiglu_bwd(dh, w_down, g, u, name, tm=2048, tn=256):
    m, k = dh.shape
    n = w_down.shape[0]
    assert m % tm == 0 and n % tn == 0, (name, m, n, tm, tn)

    def body(a_ref, w_ref, g_ref, u_ref, dg_ref, du_ref):
        dact = _nt(a_ref[...].astype(bf16), w_ref[...])
        dg_ref[...] = (dact * g_ref[...].astype(f32)).astype(bf16)
        du_ref[...] = (dact * u_ref[...].astype(f32)).astype(bf16)

    a_spec = pl.BlockSpec((tm, k), lambda i, j: (i, 0))
    w_spec = pl.BlockSpec((tn, k), lambda i, j: (j, 0))
    o_spec = pl.BlockSpec((tm, tn), lambda i, j: (i, j))
    return pl.pallas_call(
        body, name=name, grid=(m // tm, n // tn), in_specs=[a_spec, w_spec, o_spec, o_spec], out_specs=[o_spec, o_spec],
        out_shape=[jax.ShapeDtypeStruct((m, n), bf16), jax.ShapeDtypeStruct((m, n), bf16)],
        compiler_params=_cparams(("parallel", "parallel")),
    )(dh, w_down, g, u)


def _rmsnorm_fwd(h, w, name, tm=512):
    m, d = h.shape

    def body(h_ref, w_ref, o_ref):
        x = h_ref[...]
        r = lax.rsqrt(jnp.mean(x * x, axis=-1, keepdims=True) + EPS)
        o_ref[...] = (x * r * w_ref[...]).astype(bf16)

    return pl.pallas_call(
        body, name=name, grid=(m // tm,),
        in_specs=[pl.BlockSpec((tm, d), lambda i: (i, 0)), pl.BlockSpec((1, d), lambda i: (0, 0))],
        out_specs=pl.BlockSpec((tm, d), lambda i: (i, 0)), out_shape=jax.ShapeDtypeStruct((m, d), bf16),
        compiler_params=_cparams(("parallel",)),
    )(h, w)


def _nt_norm_bwd(pairs, h, w, dres, name, tk, b_is_kd=False, bf16_copy=True, tm=1024):
    m, d = h.shape
    contract = _nn if b_is_kd else _nt
    steps = [p[0].shape[1] // tk for p in pairs]
    assert all(p[0].shape[1] % tk == 0 for p in pairs), (name, tk)
    starts = [sum(steps[:i]) for i in range(len(pairs))]
    nk = sum(steps)
    n_p = len(pairs)

    def body(*refs):
        ab = refs[:2 * n_p]
        h_ref, w_ref, dres_ref, dh_ref = refs[2 * n_p:2 * n_p + 4]
        dhb_ref = refs[2 * n_p + 4] if bf16_copy else None
        dw_ref, acc_ref = refs[-2:]
        i, k = pl.program_id(0), pl.program_id(1)

        @pl.when(k == 0)
        def _():
            acc_ref[...] = jnp.zeros_like(acc_ref)

        for p in range(n_p):
            @pl.when((k >= starts[p]) & (k < starts[p] + steps[p]))
            def _(p=p):
                acc_ref[...] += contract(ab[2 * p][...], ab[2 * p + 1][...])

        @pl.when(k == nk - 1)
        def _():
            x = h_ref[...]
            r = lax.rsqrt(jnp.mean(x * x, axis=-1, keepdims=True) + EPS)
            xhat = x * r
            dy = acc_ref[...]
            gw = dy * w_ref[...]
            dh = dres_ref[...] + r * (gw - xhat * jnp.mean(gw * xhat, axis=-1, keepdims=True))
            dh_ref[...] = dh
            if bf16_copy:
                dhb_ref[...] = dh.astype(bf16)
            part = _rowsum8(dy * xhat)

            @pl.when(i == 0)
            def _():
                dw_ref[...] = part

            @pl.when(i > 0)
            def _():
                dw_ref[...] += part

    def clamp(k, p):
        return jnp.clip(k - starts[p], 0, steps[p] - 1)

    in_specs = []
    for p in range(n_p):
        b_spec = (pl.BlockSpec((tk, d), lambda i, k, p=p: (clamp(k, p), 0)) if b_is_kd
                  else pl.BlockSpec((d, tk), lambda i, k, p=p: (0, clamp(k, p))))
        in_specs += [pl.BlockSpec((tm, tk), lambda i, k, p=p: (i, clamp(k, p))), b_spec]
    row = pl.BlockSpec((tm, d), lambda i, k: (i, 0))
    in_specs += [row, pl.BlockSpec((1, d), lambda i, k: (0, 0)), row]
    return pl.pallas_call(
        body, name=name, grid=(m // tm, nk), in_specs=in_specs,
        out_specs=[row] + [row] * bf16_copy + [pl.BlockSpec((SUBLANES, d), lambda i, k: (0, 0))],
        out_shape=[jax.ShapeDtypeStruct((m, d), f32)] + [jax.ShapeDtypeStruct((m, d), bf16)] * bf16_copy + [jax.ShapeDtypeStruct((SUBLANES, d), f32)],
        scratch_shapes=[pltpu.VMEM((tm, d), f32)], compiler_params=_cparams(("arbitrary", "arbitrary")),
    )(*[t for p in pairs for t in p], h, w, dres)


def _final_loss(h, w, target, name, tm=512):
    m, d = h.shape

    def body(h_ref, w_ref, t_ref, dh_ref, dhb_ref, loss_ref, dw_ref):
        x = h_ref[...]
        r = lax.rsqrt(jnp.mean(x * x, axis=-1, keepdims=True) + EPS)
        xhat = x * r
        ww = w_ref[...]
        err = xhat * ww - t_ref[...]
        dy = err * (1.0 / d)
        gw = dy * ww
        dh = r * (gw - xhat * jnp.mean(gw * xhat, axis=-1, keepdims=True))
        dh_ref[...] = dh
        dhb_ref[...] = dh.astype(bf16)
        lpart = _rowsum8(err * err) * (0.5 / d)
        wpart = _rowsum8(dy * xhat)

        @pl.when(pl.program_id(0) == 0)
        def _():
            loss_ref[...] = lpart
            dw_ref[...] = wpart

        @pl.when(pl.program_id(0) > 0)
        def _():
            loss_ref[...] += lpart
            dw_ref[...] += wpart

    row = pl.BlockSpec((tm, d), lambda i: (i, 0))
    acc = pl.BlockSpec((SUBLANES, d), lambda i: (0, 0))
    return pl.pallas_call(
        body, name=name, grid=(m // tm,),
        in_specs=[row, pl.BlockSpec((1, d), lambda i: (0, 0)), row], out_specs=[row, row, acc, acc],
        out_shape=[jax.ShapeDtypeStruct((m, d), f32), jax.ShapeDtypeStruct((m, d), bf16),
                   jax.ShapeDtypeStruct((SUBLANES, d), f32), jax.ShapeDtypeStruct((SUBLANES, d), f32)],
        compiler_params=_cparams(("arbitrary",)),
    )(h, w, target)


def _lane_tables():
    f = np.arange(LANES) % HEAD_DIM
    inv = ROPE_THETA ** (-jnp.arange(0, ROPE_DIM, 2, dtype=f32) / ROPE_DIM)
    invf = jnp.where(f < ROPE_DIM, inv[f % (ROPE_DIM // 2)], 0.0).astype(f32)
    return invf.reshape(1, LANES)


def _rope_tables(pos_col, name):
    t = pos_col.shape[0]
    tm = SEQ

    def body(p_ref, f_ref, c_ref, s1_ref, s2_ref):
        ang = p_ref[...].astype(f32) * f_ref[...]
        co, si = jnp.cos(ang), jnp.sin(ang)
        f = lax.broadcasted_iota(jnp.int32, (tm, LANES), 1) % HEAD_DIM
        c_ref[...] = jnp.where(f < ROPE_DIM, co, 1.0)
        s1_ref[...] = jnp.where(f < ROPE_DIM // 2, -si, 0.0)
        s2_ref[...] = jnp.where((f >= ROPE_DIM // 2) & (f < ROPE_DIM), si, 0.0)

    row = pl.BlockSpec((tm, LANES), lambda i: (i, 0))
    return pl.pallas_call(
        body, name=name, grid=(t // tm,),
        in_specs=[pl.BlockSpec((tm, 1), lambda i: (i, 0)), pl.BlockSpec((1, LANES), lambda i: (0, 0))],
        out_specs=[row, row, row], out_shape=[jax.ShapeDtypeStruct((t, LANES), f32)] * 3,
        compiler_params=_cparams(("parallel",)),
    )(pos_col, _lane_tables())


def _rot(x, c, s1, s2):
    return x * c + pltpu.roll(x, LANES - ROPE_DIM // 2, 1) * s1 + pltpu.roll(x, ROPE_DIM // 2, 1) * s2


def _rot_t(g, c, s1, s2):
    return g * c + pltpu.roll(g * s1, ROPE_DIM // 2, 1) + pltpu.roll(g * s2, LANES - ROPE_DIM // 2, 1)


def _dup_head(x, kvh, low):
    a = jnp.where(kvh == 0, x, pltpu.roll(x, HEAD_DIM, 1))
    return jnp.where(low, a, pltpu.roll(a, HEAD_DIM, 1))


def _deinterleave(src_ref, dst_ref, d, dtype):
    length = SEQ // d
    if d == 1:
        dst_ref[...] = src_ref[...].astype(dtype)
    else:
        for r in range(d):
            dst_ref[pl.ds(r * length, length), :] = src_ref[pl.ds(r, length, stride=d), :].astype(dtype)


def _interleave_store(src_ref, dst_ref, d, accumulate):
    length = SEQ // d
    if d == 1:
        if accumulate:
            dst_ref[...] += src_ref[...]
        else:
            dst_ref[...] = src_ref[...]
    else:
        for r in range(d):
            blk = src_ref[pl.ds(r * length, length), :]
            if accumulate:
                dst_ref[pl.ds(r, length, stride=d), :] = dst_ref[pl.ds(r, length, stride=d), :] + blk
            else:
                dst_ref[pl.ds(r, length, stride=d), :] = blk


def _attn_masks():
    qi = lax.broadcasted_iota(jnp.int32, (ATTN_BLOCK, ATTN_BLOCK), 0)
    ki = lax.broadcasted_iota(jnp.int32, (ATTN_BLOCK, ATTN_BLOCK), 1)
    low = lax.broadcasted_iota(jnp.int32, (ATTN_BLOCK, LANES), 1) < HEAD_DIM
    return ki <= qi, ki >= qi, low


NEG_INF = float("-inf")
ATTN_UNROLL = 4


N_BRANCH = len(DILATIONS)


def _attn_prep(qkv, tabs, name):
    t = qkv.shape[0]
    nb = t // SEQ
    n_j = ATTN_WIDTH // LANES

    def q_body(q_ref, c_ref, s1_ref, s2_ref, out_ref, xr):
        xr[...] = _rot(q_ref[...], c_ref[...], s1_ref[...], s2_ref[...]) * (HEAD_DIM ** -0.5)
        for bi, d in enumerate(DILATIONS):
            _deinterleave(xr, out_ref.at[bi], d, bf16)

    def kv_body(x_ref, c_ref, s1_ref, s2_ref, out_ref, xr):
        lowfull = lax.broadcasted_iota(jnp.int32, (SEQ, LANES), 1) < HEAD_DIM
        x = x_ref[...]
        x = jnp.where(pl.program_id(1) == 0, _rot(x, c_ref[...], s1_ref[...], s2_ref[...]), x)
        for kvh in range(N_KV_HEADS):
            xr[...] = _dup_head(x, kvh, lowfull)
            for bi, d in enumerate(DILATIONS):
                length = SEQ // d
                for r in range(d):
                    rows = xr[...] if d == 1 else xr[pl.ds(r, length, stride=d), :]
                    out_ref[0, bi, pl.ds(r * length, length), kvh * LANES:(kvh + 1) * LANES] = rows.astype(bf16)

    tab = pl.BlockSpec((SEQ, LANES), lambda b, j: (b, 0))
    q = pl.pallas_call(
        q_body, name=name + "_q", grid=(nb, n_j),
        in_specs=[pl.BlockSpec((SEQ, LANES), lambda b, j: (b, j)), tab, tab, tab],
        out_specs=pl.BlockSpec((N_BRANCH, SEQ, LANES), lambda b, j: (0, b, j)),
        out_shape=jax.ShapeDtypeStruct((N_BRANCH, t, ATTN_WIDTH), bf16), scratch_shapes=[pltpu.VMEM((SEQ, LANES), f32)],
        compiler_params=_cparams(("parallel", "parallel")),
    )(qkv, *tabs)
    kv = pl.pallas_call(
        kv_body, name=name + "_kv", grid=(nb, 2),
        in_specs=[pl.BlockSpec((SEQ, LANES), lambda b, j: (b, n_j + j)), tab, tab, tab],
        out_specs=pl.BlockSpec((1, N_BRANCH, SEQ, N_KV_HEADS * LANES), lambda b, j: (j, 0, b, 0)),
        out_shape=jax.ShapeDtypeStruct((2, N_BRANCH, t, N_KV_HEADS * LANES), bf16), scratch_shapes=[pltpu.VMEM((SEQ, LANES), f32)],
        compiler_params=_cparams(("parallel", "parallel")),
    )(qkv, *tabs)
    return q, kv


def _attn_fwd(prep, name):
    q_all, kv_all = prep
    t = q_all.shape[1]
    nb = t // SEQ
    n_blk = SEQ // ATTN_BLOCK

    def body(q_ref, k_ref, v_ref, o_ref, lse_ref, ob, lb, o0, o1, o2, l0, l1, l2, ss):
        cur_ok, prev_ok, low = _attn_masks()
        onat, lnat = (o0, o1, o2), (l0, l1, l2)
        for bi, d in enumerate(DILATIONS):
            qd, kd, vd = q_ref.at[bi], k_ref.at[0, bi], v_ref.at[0, bi]
            per_res = n_blk // d
            use_prev = per_res > 1

            def scores(n, carry):
                start = pl.multiple_of(n * ATTN_BLOCK, ATTN_BLOCK)
                has_prev = (n % per_res) != 0
                pstart = pl.multiple_of(jnp.maximum(n - 1, 0) * ATTN_BLOCK, ATTN_BLOCK)
                qb = qd[pl.ds(start, ATTN_BLOCK), :]
                kc = kd[pl.ds(start, ATTN_BLOCK), :]
                if use_prev:
                    kp = kd[pl.ds(pstart, ATTN_BLOCK), :]
                for a in range(2):
                    qa = jnp.where(low if a == 0 else ~low, qb, jnp.zeros_like(qb))
                    ss[2 * n + a, :, 0:ATTN_BLOCK] = jnp.where(cur_ok, _nt(qa, kc), NEG_INF)
                    if use_prev:
                        ss[2 * n + a, :, ATTN_BLOCK:2 * ATTN_BLOCK] = jnp.where(prev_ok & has_prev, _nt(qa, kp), NEG_INF)
                return carry

            def softmax_pv(n, carry):
                start = pl.multiple_of(n * ATTN_BLOCK, ATTN_BLOCK)
                pstart = pl.multiple_of(jnp.maximum(n - 1, 0) * ATTN_BLOCK, ATTN_BLOCK)
                vc = vd[pl.ds(start, ATTN_BLOCK), :]
                if use_prev:
                    vp = vd[pl.ds(pstart, ATTN_BLOCK), :]
                outs, lses = [], []
                for a in range(2):
                    sc = ss[2 * n + a, :, 0:ATTN_BLOCK]
                    if use_prev:
                        sp = ss[2 * n + a, :, ATTN_BLOCK:2 * ATTN_BLOCK]
                        m = jnp.max(jnp.maximum(sc, sp), axis=1, keepdims=True)
                        pc, pp = jnp.exp(sc - m), jnp.exp(sp - m)
                        den = jnp.sum(pc + pp, axis=1, keepdims=True)
                        acc = _nn(pc.astype(bf16), vc) + _nn(pp.astype(bf16), vp)
                    else:
                        m = jnp.max(sc, axis=1, keepdims=True)
                        pc = jnp.exp(sc - m)
                        den = jnp.sum(pc, axis=1, keepdims=True)
                        acc = _nn(pc.astype(bf16), vc)
                    outs.append(acc * (1.0 / den))
                    lses.append(m + jnp.log(den))
                ob[pl.ds(start, ATTN_BLOCK), :] = jnp.where(low, outs[0], outs[1])
                lb[pl.ds(start, ATTN_BLOCK), :] = jnp.where(low, lses[0], lses[1])
                return carry

            lax.fori_loop(0, n_blk, scores, 0, unroll=ATTN_UNROLL)
            lax.fori_loop(0, n_blk, softmax_pv, 0, unroll=ATTN_UNROLL)
            _interleave_store(ob, onat[bi], d, False)
            _interleave_store(lb, lnat[bi], d, False)
        la, lbb, lc = l0[...], l1[...], l2[...]
        lm = jnp.maximum(jnp.maximum(la, lbb), lc)
        wa, wb, wc = jnp.exp(la - lm), jnp.exp(lbb - lm), jnp.exp(lc - lm)
        ws = wa + wb + wc
        o_ref[...] = (wa * o0[...] + wb * o1[...] + wc * o2[...]) / ws
        lse_ref[...] = lm + jnp.log(ws)

    def col(jj):
        return pl.BlockSpec((SEQ, LANES), lambda b, j: (b, jj if jj is not None else j))

    fs = pltpu.VMEM((SEQ, LANES), f32)
    return pl.pallas_call(
        body, name=name, grid=(nb, ATTN_WIDTH // LANES),
        in_specs=[pl.BlockSpec((N_BRANCH, SEQ, LANES), lambda b, j: (0, b, j)),
                  pl.BlockSpec((1, N_BRANCH, SEQ, LANES), lambda b, j: (0, 0, b, j // 2)),
                  pl.BlockSpec((1, N_BRANCH, SEQ, LANES), lambda b, j: (1, 0, b, j // 2))],
        out_specs=[col(None), col(None)],
        out_shape=[jax.ShapeDtypeStruct((t, ATTN_WIDTH), f32), jax.ShapeDtypeStruct((t, ATTN_WIDTH), f32)],
        scratch_shapes=[fs, fs, fs, fs, fs, fs, fs, fs, pltpu.VMEM((2 * n_blk, ATTN_BLOCK, 2 * ATTN_BLOCK), f32)],
        compiler_params=_cparams(("parallel", "parallel")),
    )(q_all, kv_all, kv_all)


def _attn_bwd(prep, tabs, o, lse, do, name):
    q_all, kv_all = prep
    t = q_all.shape[1]
    nb = t // SEQ
    n_blk = SEQ // ATTN_BLOCK
    n_j = ATTN_WIDTH // LANES

    def body(q_ref, k_ref, v_ref, c_ref, s1_ref, s2_ref, o_ref, lse_ref, do_ref, dq_ref, dk_ref, dv_ref,
             dl, dod, lsd, dld, dqd, dkd, dvd, dqa, dka, dva, pb, dsb, dk_acc, dv_acc):
        j = pl.program_id(1)
        pb[2 * n_blk:2 * n_blk + 2] = jnp.zeros((2, ATTN_BLOCK, 2 * ATTN_BLOCK), bf16)
        dsb[2 * n_blk:2 * n_blk + 2] = jnp.zeros((2, ATTN_BLOCK, 2 * ATTN_BLOCK), bf16)
        kvh = j // 2
        cur_ok, prev_ok, low = _attn_masks()
        lowfull = lax.broadcasted_iota(jnp.int32, (SEQ, LANES), 1) < HEAD_DIM
        c, s1, s2 = c_ref[...], s1_ref[...], s2_ref[...]
        prod = do_ref[...] * o_ref[...]
        d_lo = jnp.sum(jnp.where(lowfull, prod, 0.0), axis=1, keepdims=True)
        d_hi = jnp.sum(jnp.where(lowfull, 0.0, prod), axis=1, keepdims=True)
        dl[...] = jnp.where(lowfull, d_lo, d_hi)
        dqa[...] = jnp.zeros_like(dqa)
        dka[...] = jnp.zeros_like(dka)
        dva[...] = jnp.zeros_like(dva)
        for bi, d in enumerate(DILATIONS):
            qd, kd, vd = q_ref.at[bi], k_ref.at[0, bi], v_ref.at[0, bi]
            _deinterleave(do_ref, dod, d, bf16)
            _deinterleave(lse_ref, lsd, d, f32)
            _deinterleave(dl, dld, d, f32)
            per_res = n_blk // d
            use_prev = per_res > 1
            curl, prevl = slice(0, ATTN_BLOCK), slice(ATTN_BLOCK, 2 * ATTN_BLOCK)

            def halves(x):
                zero = jnp.zeros_like(x)
                return jnp.where(low, x, zero), jnp.where(low, zero, x)

            def probs(n, carry):
                start = pl.multiple_of(n * ATTN_BLOCK, ATTN_BLOCK)
                has_prev = (n % per_res) != 0
                pstart = pl.multiple_of(jnp.maximum(n - 1, 0) * ATTN_BLOCK, ATTN_BLOCK)
                cur, prev = pl.ds(start, ATTN_BLOCK), pl.ds(pstart, ATTN_BLOCK)
                qas, doas = halves(qd[cur, :]), halves(dod[cur, :])
                kc, vc = kd[cur, :], vd[cur, :]
                if use_prev:
                    kp, vp = kd[prev, :], vd[prev, :]
                lsb, dlb = lsd[cur, :], dld[cur, :]
                for a in range(2):
                    ls = lsb[:, a * HEAD_DIM:a * HEAD_DIM + 1]
                    de = dlb[:, a * HEAD_DIM:a * HEAD_DIM + 1]
                    pc = jnp.exp(jnp.where(cur_ok, _nt(qas[a], kc), NEG_INF) - ls)
                    pb[2 * n + a, :, curl] = pc.astype(bf16)
                    dsb[2 * n + a, :, curl] = (pc * (_nt(doas[a], vc) - de)).astype(bf16)
                    if use_prev:
                        pp = jnp.exp(jnp.where(prev_ok & has_prev, _nt(qas[a], kp), NEG_INF) - ls)
                        pb[2 * n + a, :, prevl] = pp.astype(bf16)
                        dsb[2 * n + a, :, prevl] = (pp * (_nt(doas[a], vp) - de)).astype(bf16)
                return carry

            def grads(n, carry):
                start = pl.multiple_of(n * ATTN_BLOCK, ATTN_BLOCK)
                pstart = pl.multiple_of(jnp.maximum(n - 1, 0) * ATTN_BLOCK, ATTN_BLOCK)
                nstart = pl.multiple_of(jnp.minimum(n + 1, n_blk - 1) * ATTN_BLOCK, ATTN_BLOCK)
                cur, prev, nxt = pl.ds(start, ATTN_BLOCK), pl.ds(pstart, ATTN_BLOCK), pl.ds(nstart, ATTN_BLOCK)
                kc = kd[cur, :]
                dqs = [_nn(dsb[2 * n + a, :, curl], kc) for a in range(2)]
                q_rows, do_rows = list(halves(qd[cur, :])), list(halves(dod[cur, :]))
                ds_rows, p_rows = [dsb[2 * n + a, :, curl] for a in range(2)], [pb[2 * n + a, :, curl] for a in range(2)]
                if use_prev:
                    kp = kd[prev, :]
                    dqs = [dqs[a] + _nn(dsb[2 * n + a, :, prevl], kp) for a in range(2)]
                    q_rows += list(halves(qd[nxt, :]))
                    do_rows += list(halves(dod[nxt, :]))
                    ds_rows += [dsb[2 * n + 2 + a, :, prevl] for a in range(2)]
                    p_rows += [pb[2 * n + 2 + a, :, prevl] for a in range(2)]
                dqd[cur, :] = jnp.where(low, dqs[0], dqs[1])
                dkd[cur, :] = _tn(jnp.concatenate(ds_rows, axis=0), jnp.concatenate(q_rows, axis=0))
                dvd[cur, :] = _tn(jnp.concatenate(p_rows, axis=0), jnp.concatenate(do_rows, axis=0))
                return carry

            lax.fori_loop(0, n_blk, probs, 0, unroll=ATTN_UNROLL)
            lax.fori_loop(0, n_blk, grads, 0, unroll=ATTN_UNROLL)
            _interleave_store(dqd, dqa, d, True)
            _interleave_store(dkd, dka, d, True)
            _interleave_store(dvd, dva, d, True)
        dq_ref[...] = _rot_t(dqa[...] * (HEAD_DIM ** -0.5), c, s1, s2).astype(bf16)
        dkf = dka[...]
        dkf = _rot_t(dkf + pltpu.roll(dkf, HEAD_DIM, 1), c, s1, s2)
        dvf = dva[...]
        dvf = dvf + pltpu.roll(dvf, HEAD_DIM, 1)
        mine = (lax.broadcasted_iota(jnp.int32, (SEQ, LANES), 1) // HEAD_DIM) == kvh
        dkc_, dvc_ = jnp.where(mine, dkf, 0.0), jnp.where(mine, dvf, 0.0)

        @pl.when(j == 0)
        def _():
            dk_acc[...] = dkc_
            dv_acc[...] = dvc_

        @pl.when(j > 0)
        def _():
            dk_acc[...] += dkc_
            dv_acc[...] += dvc_

        @pl.when(j == n_j - 1)
        def _():
            dk_ref[...] = dk_acc[...].astype(bf16)
            dv_ref[...] = dv_acc[...].astype(bf16)

    def col(jj):
        return pl.BlockSpec((SEQ, LANES), lambda b, j: (b, jj if jj is not None else j))

    tab = pl.BlockSpec((SEQ, LANES), lambda b, j: (b, 0))
    fs = pltpu.VMEM((SEQ, LANES), f32)
    hs = pltpu.VMEM((SEQ, LANES), bf16)
    return pl.pallas_call(
        body, name=name, grid=(nb, n_j),
        in_specs=[pl.BlockSpec((N_BRANCH, SEQ, LANES), lambda b, j: (0, b, j)),
                  pl.BlockSpec((1, N_BRANCH, SEQ, LANES), lambda b, j: (0, 0, b, j // 2)),
                  pl.BlockSpec((1, N_BRANCH, SEQ, LANES), lambda b, j: (1, 0, b, j // 2)),
                  tab, tab, tab, col(None), col(None), col(None)],
        out_specs=[col(None), tab, tab],
        out_shape=[jax.ShapeDtypeStruct((t, ATTN_WIDTH), bf16), jax.ShapeDtypeStruct((t, LANES), bf16), jax.ShapeDtypeStruct((t, LANES), bf16)],
        scratch_shapes=[fs, hs, fs, fs, fs, fs, fs, fs, fs, fs,
                        pltpu.VMEM((2 * n_blk + 2, ATTN_BLOCK, 2 * ATTN_BLOCK), bf16), pltpu.VMEM((2 * n_blk + 2, ATTN_BLOCK, 2 * ATTN_BLOCK), bf16), fs, fs],
        compiler_params=_cparams(("parallel", "arbitrary")),
    )(q_all, kv_all, kv_all, *tabs, o, lse, do)


def _tap(w_ref, s):
    return w_ref[CONV_WIDTH - 1 - s:CONV_WIDTH - s, :]


def _conv_pre(x, w_ref, b_ref, row):
    shifted = [x] + [jnp.where(row >= s, pltpu.roll(x, s, 0), 0.0) for s in range(1, CONV_WIDTH)]
    pre = b_ref[...] + _tap(w_ref, 0) * x
    for s in range(1, CONV_WIDTH):
        pre = pre + _tap(w_ref, s) * shifted[s]
    return pre, shifted


def _conv_fwd(x, w, b, name, tc=512):
    t, ch = x.shape

    def body(x_ref, w_ref, b_ref, o_ref):
        row = lax.broadcasted_iota(jnp.int32, (SEQ, tc), 0)
        pre, _ = _conv_pre(x_ref[...], w_ref, b_ref, row)
        o_ref[...] = _silu(pre)

    xs = pl.BlockSpec((SEQ, tc), lambda i, j: (i, j))
    return pl.pallas_call(
        body, name=name, grid=(t // SEQ, ch // tc),
        in_specs=[xs, pl.BlockSpec((CONV_WIDTH, tc), lambda i, j: (0, j)), pl.BlockSpec((1, tc), lambda i, j: (0, j))],
        out_specs=xs, out_shape=jax.ShapeDtypeStruct((t, ch), f32),
        compiler_params=_cparams(("parallel", "parallel")),
    )(x, w, b)


def _conv_bwd(x, w, b, dact, name, tc=512):
    t, ch = x.shape

    def body(x_ref, w_ref, b_ref, d_ref, dx_ref, dw_ref, db_ref):
        row = lax.broadcasted_iota(jnp.int32, (SEQ, tc), 0)
        pre, shifted = _conv_pre(x_ref[...], w_ref, b_ref, row)
        dpre = d_ref[...] * _dsilu(pre)
        dx = _tap(w_ref, 0) * dpre
        for s in range(1, CONV_WIDTH):
            dx = dx + _tap(w_ref, s) * jnp.where(row < SEQ - s, pltpu.roll(dpre, SEQ - s, 0), 0.0)
        dx_ref[...] = dx.astype(bf16)
        first = pl.program_id(1) == 0
        parts = [jnp.sum(dpre * shifted[CONV_WIDTH - 1 - k], axis=0, keepdims=True) for k in range(CONV_WIDTH)]
        dbp = jnp.sum(dpre, axis=0, keepdims=True)

        @pl.when(first)
        def _():
            for k in range(CONV_WIDTH):
                dw_ref[k:k + 1, :] = parts[k]
            db_ref[...] = dbp

        @pl.when(jnp.logical_not(first))
        def _():
            for k in range(CONV_WIDTH):
                dw_ref[k:k + 1, :] += parts[k]
            db_ref[...] += dbp

    xs = pl.BlockSpec((SEQ, tc), lambda j, i: (i, j))
    ws = pl.BlockSpec((CONV_WIDTH, tc), lambda j, i: (0, j))
    bs = pl.BlockSpec((1, tc), lambda j, i: (0, j))
    return pl.pallas_call(
        body, name=name, grid=(ch // tc, t // SEQ),
        in_specs=[xs, ws, bs, xs], out_specs=[xs, ws, bs],
        out_shape=[jax.ShapeDtypeStruct((t, ch), bf16), jax.ShapeDtypeStruct((CONV_WIDTH, ch), f32), jax.ShapeDtypeStruct((1, ch), f32)],
        compiler_params=_cparams(("parallel", "arbitrary")),
    )(x, w, b, dact)


GROUP_W = SSM_INNER // SSM_GROUPS
HEADS_PER_GROUP = SSM_HEADS // SSM_GROUPS


def _split3(x):
    hi = x.astype(bf16)
    r1 = x - hi.astype(f32)
    mid = r1.astype(bf16)
    lo = (r1 - mid.astype(f32)).astype(bf16)
    return hi, mid, lo


def _dot_exact(x, sel, dims, x_is_lhs=True):
    parts = _split3(x)
    if x_is_lhs:
        return _dot(parts[0], sel, dims) + _dot(parts[1], sel, dims) + _dot(parts[2], sel, dims)
    return _dot(sel, parts[0], dims) + _dot(sel, parts[1], dims) + _dot(sel, parts[2], dims)


def _ssd_common(xbc_ref, dt_ref, bias_ref, alog_ref):
    r = lax.broadcasted_iota(jnp.int32, (CHUNK, CHUNK), 0)
    cidx = lax.broadcasted_iota(jnp.int32, (CHUNK, CHUNK), 1)
    causal = r >= cidx
    tril = causal.astype(bf16)
    expand = (lax.broadcasted_iota(jnp.int32, (CHUNK, SSM_INNER), 0)
              == lax.broadcasted_iota(jnp.int32, (CHUNK, SSM_INNER), 1) // HEAD_DIM).astype(bf16)
    head_lane = cidx < SSM_HEADS
    dtp = dt_ref[...] + bias_ref[...]
    dt = jnp.where(head_lane, _softplus(dtp), 0.0)
    a_neg = -jnp.exp(alog_ref[...])
    a = dt * a_neg
    nn_dims = ((1,), (0,))
    cs = _dot_exact(a, tril, nn_dims, x_is_lhs=False)
    dt_e = _dot_exact(dt, expand, nn_dims)
    cs_e = _dot_exact(cs, expand, nn_dims)
    xs = xbc_ref[:, 0:SSM_INNER]
    xg = xs * dt_e
    ecs = jnp.exp(cs_e)
    cs_last = cs_e[CHUNK - 1:CHUNK, :]
    dse = jnp.exp(cs_last - cs_e)
    cde = jnp.exp(cs_last)
    return dict(r=r, cidx=cidx, causal=causal, tril=tril, expand=expand, head_lane=head_lane, dtp=dtp, dt=dt, a_neg=a_neg,
                cs=cs, cst=cs.T, dt_e=dt_e, cs_e=cs_e, xs=xs, xg=xg, ecs=ecs, dse=dse, cde=cde)


def _decay_mat(q, h):
    return jnp.exp(jnp.where(q["causal"], q["cs"][:, h:h + 1] - q["cst"][h:h + 1, :], NEG_INF))


def _gate_norm(y, z, nw, gate=None):
    y2 = y * (_silu(z) if gate is None else gate)
    outs, xhats, rs = [], [], []
    for g in range(SSM_GROUPS):
        sl = slice(g * GROUP_W, (g + 1) * GROUP_W)
        yg = y2[:, sl]
        r = lax.rsqrt(jnp.mean(yg * yg, axis=-1, keepdims=True) + EPS)
        xhats.append(yg * r)
        rs.append(r)
        outs.append(yg * r * nw[:, sl])
    return y2, outs, xhats, rs


def _ssd_fwd(xbc, z, dtp, params, name):
    t = xbc.shape[0]
    n_chunk = SEQ // CHUNK
    low = None

    def body(xbc_ref, z_ref, dt_ref, bias_ref, alog_ref, dskip_ref, nw_ref, yn_ref, y_ref, hs_ref, h_scr):
        @pl.when(pl.program_id(1) == 0)
        def _():
            h_scr[...] = jnp.zeros_like(h_scr)

        q = _ssd_common(xbc_ref, dt_ref, bias_ref, alog_ref)
        low = lax.broadcasted_iota(jnp.int32, (CHUNK, LANES), 1) < HEAD_DIM
        xgb = q["xg"].astype(bf16)
        wst = (q["xg"] * q["dse"]).astype(bf16)
        hs_ref[0] = h_scr[...]
        ys = []
        for g in range(SSM_GROUPS):
            gl = slice(g * GROUP_W, (g + 1) * GROUP_W)
            bg = xbc_ref[:, SSM_INNER + g * D_STATE:SSM_INNER + (g + 1) * D_STATE].astype(bf16)
            cg = xbc_ref[:, SSM_INNER + SSM_GROUPS * D_STATE + g * D_STATE:SSM_INNER + SSM_GROUPS * D_STATE + (g + 1) * D_STATE].astype(bf16)
            cb = _nt(cg, bg)
            hg = h_scr[g]
            yoff = _nn(cg, hg.astype(bf16)) * q["ecs"][:, gl]
            pieces = []
            for i in range(HEADS_PER_GROUP // 2):
                h0 = g * HEADS_PER_GROUP + 2 * i
                xp = xgb[:, h0 * HEAD_DIM:(h0 + 2) * HEAD_DIM]
                m0 = (cb * _decay_mat(q, h0)).astype(bf16)
                m1 = (cb * _decay_mat(q, h0 + 1)).astype(bf16)
                zero = jnp.zeros_like(xp)
                pieces.append(_nn(m0, jnp.where(low, xp, zero)) + _nn(m1, jnp.where(low, zero, xp)))
            ys.append(jnp.concatenate(pieces, axis=1) + yoff + dskip_ref[:, gl] * q["xs"][:, gl])
            h_scr[g] = hg * q["cde"][:, gl] + _tn(bg, wst[:, gl])
        y = jnp.concatenate(ys, axis=1)
        y_ref[...] = y
        _, outs, _, _ = _gate_norm(y, z_ref[...], nw_ref[...])
        yn_ref[...] = jnp.concatenate(outs, axis=1).astype(bf16)

    def rows(w):
        return pl.BlockSpec((CHUNK, w), lambda b, c: (b * n_chunk + c, 0))

    def par(w):
        return pl.BlockSpec((1, w), lambda b, c: (0, 0))

    return pl.pallas_call(
        body, name=name, grid=(t // SEQ, n_chunk),
        in_specs=[rows(CONV_CH), rows(SSM_INNER), rows(LANES), par(LANES), par(LANES), par(SSM_INNER), par(SSM_INNER)],
        out_specs=[rows(SSM_INNER), rows(SSM_INNER), pl.BlockSpec((1, SSM_GROUPS, D_STATE, GROUP_W), lambda b, c: (b * n_chunk + c, 0, 0, 0))],
        out_shape=[jax.ShapeDtypeStruct((t, SSM_INNER), bf16), jax.ShapeDtypeStruct((t, SSM_INNER), f32),
                   jax.ShapeDtypeStruct((t // CHUNK, SSM_GROUPS, D_STATE, GROUP_W), f32)],
        scratch_shapes=[pltpu.VMEM((SSM_GROUPS, D_STATE, GROUP_W), f32)],
        compiler_params=_cparams(("parallel", "arbitrary")),
    )(xbc, z, dtp, *params)


def _ssd_bwd(xbc, z, dtp, y, hs, dyn, params, name):
    t = xbc.shape[0]
    n_chunk = SEQ // CHUNK

    def body(xbc_ref, z_ref, dt_ref, y_ref, hs_ref, dyn_ref, bias_ref, alog_ref, dskip_ref, nw_ref,
             dxbc_ref, dz_ref, ddt_ref, dnw_ref, dds_ref, dal_ref, dbi_ref, dh_scr):
        @pl.when(pl.program_id(1) == 0)
        def _():
            dh_scr[...] = jnp.zeros_like(dh_scr)

        q = _ssd_common(xbc_ref, dt_ref, bias_ref, alog_ref)
        low = lax.broadcasted_iota(jnp.int32, (CHUNK, LANES), 1) < HEAD_DIM
        last_row = lax.broadcasted_iota(jnp.int32, (CHUNK, GROUP_W), 0) == CHUNK - 1
        xs, xg = q["xs"], q["xg"]
        xgb = xg.astype(bf16)
        wf = xg * q["dse"]
        wst = wf.astype(bf16)
        zz = z_ref[...]
        yy = y_ref[...]
        sz, dsz = _silu_and_grad(zz)
        y2, _, xhats, rs = _gate_norm(yy, zz, nw_ref[...], gate=sz)
        dyn_ = dyn_ref[...]
        dy2s, dnws = [], []
        for g in range(SSM_GROUPS):
            gl = slice(g * GROUP_W, (g + 1) * GROUP_W)
            gw = dyn_[:, gl] * nw_ref[:, gl]
            dy2s.append(rs[g] * (gw - xhats[g] * jnp.mean(gw * xhats[g], axis=-1, keepdims=True)))
            dnws.append(_rowsum8(dyn_[:, gl] * xhats[g]))
        dy2 = jnp.concatenate(dy2s, axis=1)
        dy = dy2 * sz
        dz_ref[...] = (dy2 * yy * dsz).astype(bf16)
        dnw_p = jnp.concatenate(dnws, axis=1)
        dds_p = _rowsum8(dy * xs)
        dyb = dy.astype(bf16)
        gfull = (dy * q["ecs"]).astype(bf16)
        dcs_c = jnp.zeros((CHUNK, CHUNK), f32)
        dcs_r = jnp.zeros((CHUNK, CHUNK), f32)
        dcs_e_parts, dxg_parts = [], []
        for g in range(SSM_GROUPS):
            gl = slice(g * GROUP_W, (g + 1) * GROUP_W)
            bsl = slice(SSM_INNER + g * D_STATE, SSM_INNER + (g + 1) * D_STATE)
            csl = slice(SSM_INNER + SSM_GROUPS * D_STATE + g * D_STATE, SSM_INNER + SSM_GROUPS * D_STATE + (g + 1) * D_STATE)
            bg = xbc_ref[:, bsl].astype(bf16)
            cg = xbc_ref[:, csl].astype(bf16)
            cb = _nt(cg, bg)
            hg = hs_ref[0, g]
            hgb = hg.astype(bf16)
            dhn = dh_scr[g]
            dhnb = dhn.astype(bf16)
            yoff = _nn(cg, hgb) * q["ecs"][:, gl]
            dw_ = _nn(bg, dhnb)
            r_e = dw_ * wf[:, gl]
            to_last = jnp.sum(r_e, axis=0, keepdims=True) + jnp.sum(dhn * hg, axis=0, keepdims=True) * q["cde"][:, gl]
            dcs_e_parts.append(dy[:, gl] * yoff - r_e + jnp.where(last_row, to_last, 0.0))
            dcb = jnp.zeros((CHUNK, CHUNK), f32)
            dxg_pairs = []
            for i in range(HEADS_PER_GROUP // 2):
                h0 = g * HEADS_PER_GROUP + 2 * i
                psl = slice(h0 * HEAD_DIM, (h0 + 2) * HEAD_DIM)
                xp = xgb[:, psl]
                dyp = dyb[:, psl]
                zero = jnp.zeros_like(dyp)
                tns = []
                for a in range(2):
                    h = h0 + a
                    lm = _decay_mat(q, h)
                    m = cb * lm
                    dm = _nt(jnp.where(low, dyp, zero) if a == 0 else jnp.where(low, zero, dyp), xp)
                    dcb = dcb + dm * lm
                    nmat = dm * m
                    dcs_c = dcs_c + jnp.where(q["cidx"] == h, jnp.sum(nmat, axis=1, keepdims=True), 0.0)
                    dcs_r = dcs_r + jnp.where(q["r"] == h, jnp.sum(nmat, axis=0, keepdims=True), 0.0)
                    tns.append(_tn(m.astype(bf16), dyp))
                dxg_pairs.append(jnp.where(low, tns[0], tns[1]))
            dxg_parts.append(jnp.concatenate(dxg_pairs, axis=1) + dw_ * q["dse"][:, gl])
            dcbb = dcb.astype(bf16)
            dxbc_ref[:, csl] = _nt(gfull[:, gl], hgb) + _nn(dcbb, bg)
            dxbc_ref[:, bsl] = _nt(wst[:, gl], dhnb) + _tn(dcbb, cg)
            dh_scr[g] = dhn * q["cde"][:, gl] + _tn(cg, gfull[:, gl])
        dxg = jnp.concatenate(dxg_parts, axis=1)
        dcs_e = jnp.concatenate(dcs_e_parts, axis=1)
        dxbc_ref[:, 0:SSM_INNER] = dskip_ref[...] * dy + dxg * q["dt_e"]
        dcs = dcs_c - dcs_r.T + _dot_exact(dcs_e, q["expand"], ((1,), (1,)))
        triu = (q["cidx"] >= q["r"]).astype(bf16)
        da = _dot_exact(dcs, triu, ((1,), (0,)), x_is_lhs=False)
        ddt = _dot_exact(dxg * xs, q["expand"], ((1,), (1,))) + da * q["a_neg"]
        ddtp = jnp.where(q["head_lane"], ddt * _sigmoid(q["dtp"]), 0.0)
        ddt_ref[...] = ddtp.astype(bf16)
        dal_p = _rowsum8(da * q["dt"]) * q["a_neg"]
        dbi_p = _rowsum8(ddtp)
        first = (pl.program_id(0) == 0) & (pl.program_id(1) == 0)

        @pl.when(first)
        def _():
            dnw_ref[...] = dnw_p
            dds_ref[...] = dds_p
            dal_ref[...] = dal_p
            dbi_ref[...] = dbi_p

        @pl.when(jnp.logical_not(first))
        def _():
            dnw_ref[...] += dnw_p
            dds_ref[...] += dds_p
            dal_ref[...] += dal_p
            dbi_ref[...] += dbi_p

    def rows(w):
        return pl.BlockSpec((CHUNK, w), lambda b, c: (b * n_chunk + n_chunk - 1 - c, 0))

    def par(w):
        return pl.BlockSpec((1, w), lambda b, c: (0, 0))

    def acc(w):
        return pl.BlockSpec((SUBLANES, w), lambda b, c: (0, 0))

    return pl.pallas_call(
        body, name=name, grid=(t // SEQ, n_chunk),
        in_specs=[rows(CONV_CH), rows(SSM_INNER), rows(LANES), rows(SSM_INNER),
                  pl.BlockSpec((1, SSM_GROUPS, D_STATE, GROUP_W), lambda b, c: (b * n_chunk + n_chunk - 1 - c, 0, 0, 0)),
                  rows(SSM_INNER), par(LANES), par(LANES), par(SSM_INNER), par(SSM_INNER)],
        out_specs=[rows(CONV_CH), rows(SSM_INNER), rows(LANES), acc(SSM_INNER), acc(SSM_INNER), acc(LANES), acc(LANES)],
        out_shape=[jax.ShapeDtypeStruct((t, CONV_CH), f32), jax.ShapeDtypeStruct((t, SSM_INNER), bf16), jax.ShapeDtypeStruct((t, LANES), bf16),
                   jax.ShapeDtypeStruct((SUBLANES, SSM_INNER), f32), jax.ShapeDtypeStruct((SUBLANES, SSM_INNER), f32),
                   jax.ShapeDtypeStruct((SUBLANES, LANES), f32), jax.ShapeDtypeStruct((SUBLANES, LANES), f32)],
        scratch_shapes=[pltpu.VMEM((SSM_GROUPS, D_STATE, GROUP_W), f32)],
        compiler_params=_cparams(("arbitrary", "arbitrary")),
    )(xbc, z, dtp, y, hs, dyn, *params)


def _adamw_update(g, w, m, v):
    mm = ADAM_B1 * m + (1.0 - ADAM_B1) * g
    vv = ADAM_B2 * v + (1.0 - ADAM_B2) * (g * g)
    m_hat = mm / (1.0 - ADAM_B1 ** ADAM_STEP)
    v_hat = vv / (1.0 - ADAM_B2 ** ADAM_STEP)
    return -ADAM_LR * (m_hat / (jnp.sqrt(v_hat) + ADAM_EPS) + ADAM_WD * w), mm, vv


def _adamw(g_parts, w, m, v, name):
    rows, width = w.shape
    n = len(g_parts)
    tr = _row_tile(rows)

    def body(*refs):
        g_refs, (w_ref, m_ref, v_ref, g_out, d_out, m_out, v_out) = refs[:n], refs[n:]
        g = g_refs[0][...].astype(f32)
        for r in g_refs[1:]:
            g = g + r[...].astype(f32)
        g_out[...] = g
        d_out[...], m_out[...], v_out[...] = _adamw_update(g, w_ref[...], m_ref[...], v_ref[...])

    spec = pl.BlockSpec((tr, width), lambda i: (i, 0))
    return pl.pallas_call(
        body, name=name, grid=(rows // tr,), in_specs=[spec] * (n + 3), out_specs=[spec] * 4,
        out_shape=[jax.ShapeDtypeStruct((rows, width), f32)] * 4, compiler_params=_cparams(("parallel",)),
    )(*g_parts, w, m, v)


def _adamw_layers(landed, w, m, v, after, name, layers_on_columns=False):
    depth = len(landed)
    _, rows, width = landed[0].shape
    tr = _row_tile(rows)
    n_i = rows // tr
    at = (lambda ref: ref) if layers_on_columns else (lambda ref: ref.at[0])

    def body(*refs):
        part_refs, (w_ref, m_ref, v_ref, _, g_out, d_out, m_out, v_out) = refs[:depth * N_DEV], refs[depth * N_DEV:]
        for l in range(depth):
            @pl.when(pl.program_id(0) == l)
            def _(l=l):
                g = part_refs[l * N_DEV][0].astype(f32)
                for r in part_refs[l * N_DEV + 1:(l + 1) * N_DEV]:
                    g = g + r[0].astype(f32)
                at(g_out)[...] = g
                at(d_out)[...], at(m_out)[...], at(v_out)[...] = _adamw_update(g, at(w_ref)[...], at(m_ref)[...], at(v_ref)[...])

    def part_spec(l, p):
        return pl.BlockSpec((1, tr, width), lambda ll, i: (p, jnp.where(ll == l, i, jnp.where(ll < l, 0, n_i - 1)), 0))

    state = (pl.BlockSpec((tr, width), lambda ll, i: (i, ll)) if layers_on_columns
             else pl.BlockSpec((1, tr, width), lambda ll, i: (ll, i, 0)))
    return pl.pallas_call(
        body, name=name, grid=(depth, n_i),
        in_specs=[part_spec(l, p) for l in range(depth) for p in range(N_DEV)] + [state] * 3 + [ANY], out_specs=[state] * 4,
        out_shape=[jax.ShapeDtypeStruct(w.shape, f32)] * 4, compiler_params=_cparams(("arbitrary", "arbitrary")),
    )(*[landed[l] for l in range(depth) for _ in range(N_DEV)], w, m, v, after)


def _row_tile(rows, cap=512):
    for cand in range(min(rows, cap) // SUBLANES * SUBLANES, 0, -SUBLANES):
        if rows % cand == 0:
            return cand
    return rows


def _cols_from_devices(g, width, name):
    n_dev, depth, a, b = g.shape

    def body(g_ref, o_ref):
        for i in range(n_dev):
            o_ref[0, :, i * b:(i + 1) * b] = g_ref[i, 0]
        if width > n_dev * b:
            o_ref[0, :, n_dev * b:width] = jnp.zeros((a, width - n_dev * b), o_ref.dtype)

    return pl.pallas_call(
        body, name=name, grid=(depth,), in_specs=[pl.BlockSpec((n_dev, 1, a, b), lambda l: (0, l, 0, 0))],
        out_specs=pl.BlockSpec((1, a, width), lambda l: (l, 0, 0)), out_shape=jax.ShapeDtypeStruct((depth, a, width), g.dtype),
        compiler_params=_cparams(("parallel",)),
    )(g)


def _devices_from_cols(per_layer, b, name, tr=256):
    depth = len(per_layer)
    a, width = per_layer[0].shape

    def body(*refs):
        o_ref = refs[depth]
        for l in range(depth):
            for i in range(N_DEV):
                o_ref[i, l] = refs[l][:, i * b:(i + 1) * b]

    return pl.pallas_call(
        body, name=name, grid=(a // tr,), in_specs=[pl.BlockSpec((tr, width), lambda r: (r, 0))] * depth,
        out_specs=pl.BlockSpec((N_DEV, depth, tr, b), lambda r: (0, 0, r, 0)),
        out_shape=jax.ShapeDtypeStruct((N_DEV, depth, a, b), per_layer[0].dtype), compiler_params=_cparams(("parallel",)),
    )(*per_layer)


def _me():
    return lax.axis_index("x"), lax.axis_index("y"), lax.axis_index("c")


def _allgather_two_level(shards, name):
    n = len(shards)
    per = 7

    def body(*refs):
        ins, outs, token = refs[:n], refs[n:2 * n], refs[2 * n]
        send_sems, recv_sems, local_sems = refs[2 * n + 1:]
        token[...] = jnp.zeros_like(token)
        x, y, c = _me()
        me, sibling = (x, y, c), (x, y, 1 - c)
        chips = [(1 - x, y), (x, 1 - y), (1 - x, 1 - y)]

        def slot(a, p):
            return outs[a].at[4 * p[0] + 2 * p[1] + p[2]]

        def copy(a, k, block, to, src=None):
            return pltpu.make_async_remote_copy(
                src_ref=slot(a, block) if src is None else src, dst_ref=slot(a, block),
                send_sem=send_sems.at[a * per + k], recv_sem=recv_sems.at[a * per + k], device_id=to, device_id_type=MESH)

        mine = [pltpu.make_async_copy(ins[a], slot(a, me), local_sems.at[a]) for a in range(n)]
        for cp in mine:
            cp.start()
        first = []
        for a in range(n):
            first.append(copy(a, 0, me, sibling, src=ins[a]))
            first += [copy(a, 1 + j, me, (*chip, c), src=ins[a]) for j, chip in enumerate(chips)]
        for cp in first:
            cp.start()
        passed = []
        for j, chip in enumerate(chips):
            for a in range(n):
                copy(a, 1 + j, (*chip, c), me).wait_recv()
                fwd = copy(a, 4 + j, (*chip, c), sibling)
                fwd.start()
                passed.append(fwd)
        for a in range(n):
            copy(a, 0, sibling, me).wait_recv()
            for j, chip in enumerate(chips):
                copy(a, 4 + j, (*chip, 1 - c), me).wait_recv()
        for cp in first + passed:
            cp.wait_send()
        for cp in mine:
            cp.wait()

    outs = pl.pallas_call(
        body, name=name, in_specs=[ANY] * n, out_specs=[ANY] * n + [pl.BlockSpec(memory_space=pltpu.VMEM)],
        out_shape=[jax.ShapeDtypeStruct((N_DEV,) + s.shape, s.dtype) for s in shards] + [jax.ShapeDtypeStruct((SUBLANES, LANES), f32)],
        scratch_shapes=[pltpu.SemaphoreType.DMA((n * per,)), pltpu.SemaphoreType.DMA((n * per,)), pltpu.SemaphoreType.DMA((n,))],
    )(*shards)
    return outs[:n], outs[n]


def _allgather_direct(row, name):
    def body(in_ref, out_ref, send_sems, recv_sems, local_sem):
        x, y, c = _me()
        mine = out_ref.at[4 * x + 2 * y + c]
        local = pltpu.make_async_copy(in_ref, mine, local_sem)
        local.start()
        sends = []
        for k in range(1, N_DEV):
            px, py, pc = x ^ (k >> 2), y ^ ((k >> 1) & 1), c ^ (k & 1)
            sends.append(pltpu.make_async_remote_copy(
                src_ref=in_ref, dst_ref=mine, send_sem=send_sems.at[k - 1], recv_sem=recv_sems.at[k - 1],
                device_id=(px, py, pc), device_id_type=MESH))
        for cp in sends:
            cp.start()
        for k in range(1, N_DEV):
            px, py, pc = x ^ (k >> 2), y ^ ((k >> 1) & 1), c ^ (k & 1)
            theirs = out_ref.at[4 * px + 2 * py + pc]
            pltpu.make_async_remote_copy(
                src_ref=in_ref, dst_ref=theirs, send_sem=send_sems.at[k - 1], recv_sem=recv_sems.at[k - 1],
                device_id=(px, py, pc), device_id_type=MESH).wait_recv()
        for cp in sends:
            cp.wait_send()
        local.wait()

    return pl.pallas_call(
        body, name=name, in_specs=[ANY], out_specs=ANY, out_shape=jax.ShapeDtypeStruct((N_DEV,) + row.shape, row.dtype),
        scratch_shapes=[pltpu.SemaphoreType.DMA((N_DEV - 1,)), pltpu.SemaphoreType.DMA((N_DEV - 1,)), pltpu.SemaphoreType.DMA],
    )(row)


N_CHIP = N_DEV // 2
HBM = pl.BlockSpec(memory_space=pltpu.HBM)
SEM = pl.BlockSpec(memory_space=pltpu.SEMAPHORE)
EFFECT = pltpu.SideEffectType.DATAFLOW_SIDE_EFFECTING


def _peer(k):
    x, y, c = _me()
    return x ^ (k >> 2), y ^ ((k >> 1) & 1), c ^ (k & 1)


def _direct_copies(srcs, lands, send_sems, recv_sems, per_peer):
    x, y, c = _me()
    me = 4 * x + 2 * y + c
    copies = []
    for a in range(len(srcs)):
        for k in range(1, N_DEV):
            px, py, pc = _peer(k)
            piece = srcs[a].at[4 * px + 2 * py + pc] if per_peer else srcs[a]
            copies.append(pltpu.make_async_remote_copy(
                src_ref=piece, dst_ref=lands[a].at[me], send_sem=send_sems.at[a * (N_DEV - 1) + k - 1],
                recv_sem=recv_sems.at[a * (N_DEV - 1) + k - 1], device_id=(px, py, pc), device_id_type=MESH))
    return copies


def _direct_start(srcs, lands, per_peer, name):
    n = len(srcs)
    n_sem = n * (N_DEV - 1)

    def body(*refs):
        src_refs, land_refs = refs[:n], refs[n:2 * n]
        send_sems, recv_sems = refs[2 * n], refs[2 * n + 1]
        token = refs[-1]
        for cp in _direct_copies(src_refs, land_refs, send_sems, recv_sems, per_peer):
            cp.start()
        token[...] = jnp.zeros_like(token)

    outs = pl.pallas_call(
        body, name=name,
        out_shape=(pltpu.SemaphoreType.DMA((n_sem,)), pltpu.SemaphoreType.DMA((n_sem,)),
                   *[pltpu.HBM(s.shape, s.dtype) for s in srcs], *[pltpu.HBM(s.shape, s.dtype) for s in lands],
                   jax.ShapeDtypeStruct((SUBLANES, LANES), f32)),
        in_specs=[HBM] * (2 * n), out_specs=(SEM, SEM, *[HBM] * (2 * n), pl.BlockSpec(memory_space=pltpu.VMEM)),
        input_output_aliases={i: 2 + i for i in range(2 * n)},
        compiler_params=pltpu.CompilerParams(has_side_effects=EFFECT),
    )(*[pltpu.with_memory_space_constraint(s, pltpu.HBM) for s in srcs], *[pltpu.with_memory_space_constraint(s, pltpu.HBM) for s in lands])
    return outs[0], outs[1], outs[2:2 + n], outs[2 + n:2 + 2 * n], outs[-1]


def _direct_wait(send_sems, recv_sems, srcs, lands, after, per_peer, name):
    n = len(srcs)

    def body(*refs):
        src_refs, land_refs = refs[:n], refs[n:2 * n]
        s_sems, r_sems = refs[2 * n], refs[2 * n + 1]
        for cp in _direct_copies(src_refs, land_refs, s_sems, r_sems, per_peer):
            cp.wait_send()
            cp.wait_recv()

    outs = pl.pallas_call(
        body, name=name,
        out_shape=tuple(pltpu.HBM(s.shape, s.dtype) for s in list(srcs) + list(lands)),
        in_specs=[HBM] * (2 * n) + [SEM, SEM, ANY], out_specs=tuple([HBM] * (2 * n)),
        input_output_aliases={i: i for i in range(2 * n)},
        compiler_params=pltpu.CompilerParams(has_side_effects=EFFECT),
    )(*srcs, *lands, send_sems, recv_sems, after)
    return outs[n:]


def _row(v, width=None):
    v = v.reshape(1, -1).astype(f32)
    if width is not None and v.shape[1] < width:
        v = jnp.pad(v, ((0, 0), (0, width - v.shape[1])))
    return v


def _layer_params(p, l):
    return dict(
        norm_mix=_row(p["norm_mix"][l]), norm_ffn=_row(p["norm_ffn"][l]), conv_w=p["conv_w"][l], conv_b=_row(p["conv_b"][l]),
        ssd=(_row(p["dt_bias"][l], LANES), _row(p["a_log"][l], LANES), _row(jnp.repeat(p["d_skip"][l], HEAD_DIM)), _row(p["ssm_norm"][l])))


def _layer_fwd(h, w_in, rest, sp, tabs, l):
    tag = f"l{l}_"
    hn = _rmsnorm_fwd(h, sp["norm_mix"], tag + "norm_mix")
    qkv, z, xbc_pre = _in_proj(hn, w_in, (QKV_WIDTH, SSM_INNER, CONV_CH), tag + "proj")
    dtp = _matmul(hn, w_in, mode="nn", n_out=LANES, tn=LANES, b_off=DT_OFF // LANES, name=tag + "proj_dt")
    prep = _attn_prep(qkv, tabs, tag + "attn_prep")
    o, lse = _attn_fwd(prep, tag + "attn_fwd")
    xbc = _conv_fwd(xbc_pre, sp["conv_w"], sp["conv_b"], tag + "conv_fwd")
    yn, y, hs = _ssd_fwd(xbc, z, dtp, sp["ssd"], tag + "ssd_fwd")
    w_out, w_gate, w_up, w_down = rest(yn) if callable(rest) else rest
    h2 = _out_proj(o, yn, w_out, h, tag + "out_proj")
    hn2 = _rmsnorm_fwd(h2, sp["norm_ffn"], tag + "norm_ffn")
    g, u, act = _swiglu_fwd(hn2, w_gate, w_up, tag + "ffn_up")
    h3 = _matmul(act, w_down, mode="nn", tk=1408, add=h2, name=tag + "ffn_down")
    saved = dict(h=h, hn=hn, prep=prep, z=z, xbc_pre=xbc_pre, dtp=dtp, o=o, lse=lse, xbc=xbc, yn=yn, y=y, hs=hs, h2=h2, hn2=hn2, g=g, u=u, act=act,
                 rest=(w_out, w_gate, w_up, w_down))
    return h3, saved


def _layer_bwd(dh3_pair, s, big, sp, tabs, l, gd=f32, after_ffn=None):
    tag = f"l{l}_"
    dh3, dh3b = dh3_pair
    w_in, w_out, w_gate, w_up, w_down = big
    dg, du = _swiglu_bwd(dh3b, w_down, s["g"], s["u"], tag + "ffn_down_bwd")
    dw_down = _matmul(s["act"], dh3b, mode="tn", tm=1408, tn=512, tk=2048, out_dtype=gd, name=tag + "dw_down")
    dw_gate = _matmul(dg, s["hn2"], mode="tn", tm=1408, tn=512, tk=2048, out_dtype=gd, name=tag + "dw_gate")
    dw_up = _matmul(du, s["hn2"], mode="tn", tm=1408, tn=512, tk=2048, out_dtype=gd, name=tag + "dw_up")
    norm_ffn = sp["norm_ffn"] if after_ffn is None else sp["norm_ffn"] + after_ffn(dict(w_gate=dw_gate, w_up=dw_up, w_down=dw_down))
    dh2, dnf = _nt_norm_bwd([(dg, w_gate), (du, w_up)], s["h2"], norm_ffn, dh3, tag + "ffn_up_bwd_norm", tk=1408, b_is_kd=True, bf16_copy=False)
    d_o = _matmul(dh2, w_out, mode="nt", n_out=ATTN_WIDTH, tn=512, b_off=0, name=tag + "out_attn_bwd")
    dyn = _matmul(dh2, w_out, mode="nt", n_out=SSM_INNER, tn=512, b_off=1, name=tag + "out_ssm_bwd")
    dw_out = jnp.concatenate([_matmul(s["o"], dh2, mode="tn", tm=512, tn=512, tk=2048, out_dtype=gd, name=tag + "dw_out_attn"),
                              _matmul(s["yn"], dh2, mode="tn", tm=512, tn=512, tk=2048, out_dtype=gd, name=tag + "dw_out_ssm")], axis=0)
    dxbc, dz, ddtp, dnw, dds, dal, dbi = _ssd_bwd(s["xbc"], s["z"], s["dtp"], s["y"], s["hs"], dyn, sp["ssd"], tag + "ssd_bwd")
    dxbc_pre, dconv_w, dconv_b = _conv_bwd(s["xbc_pre"], sp["conv_w"], sp["conv_b"], dxbc, tag + "conv_bwd")
    dq, dk, dv = _attn_bwd(s["prep"], tabs, s["o"], s["lse"], d_o, tag + "attn_bwd")
    dproj = jnp.concatenate([dq, dk, dv, dz, dxbc_pre, ddtp], axis=1)
    dw_in = _matmul(s["hn"], dproj, mode="tn", tm=512, tn=1152, tk=2048, out_dtype=gd, name=tag + "dw_in")
    res = _nt_norm_bwd([(dproj, w_in)], s["h"], sp["norm_mix"], dh2, tag + "proj_bwd_norm", tk=1152, bf16_copy=l > 0)
    dh, dhb, dnm = res if l > 0 else (res[0], None, res[1])
    grads = dict(
        norm_mix=dnm.sum(0), w_in=dw_in, conv_w=dconv_w, conv_b=dconv_b[0], dt_bias=dbi.sum(0)[:SSM_HEADS], a_log=dal.sum(0)[:SSM_HEADS],
        d_skip=dds.sum(0).reshape(SSM_HEADS, HEAD_DIM).sum(1), ssm_norm=dnw.sum(0), w_out=dw_out, norm_ffn=dnf.sum(0),
        w_gate=dw_gate, w_up=dw_up, w_down=dw_down)
    return (dh, dhb), grads


def _local_step(x, positions, target, p, bigs):
    tabs = _rope_tables(positions.reshape(-1, 1), "rope_tables")
    h = x
    saved, sps = [], []
    for l in range(DEPTH):
        sps.append(_layer_params(p, l))
        h, s = _layer_fwd(h, bigs[l][0], bigs[l][1:], sps[l], tabs, l)
        saved.append(s)
    dh, dhb, loss_parts, dfn = _final_loss(h, _row(p["final_norm"]), target, "final_loss")
    dh = (dh, dhb)
    layer_grads = [None] * DEPTH
    for l in reversed(range(DEPTH)):
        dh, layer_grads[l] = _layer_bwd(dh, saved[l], bigs[l], sps[l], tabs, l)
    grads = {k: [layer_grads[l][k] for l in range(DEPTH)] for k in layer_grads[0]}
    grads["final_norm"] = dfn.sum(0)
    return jnp.sum(loss_parts), dh[0], grads


BIG = ("w_in", "w_out", "w_gate", "w_up", "w_down")
REST = BIG[1:]
FFN = ("w_gate", "w_up", "w_down")
MIX = ("w_in", "w_out")
COL_SHARDED = ("w_in",)
TRANSPOSED = ("w_gate", "w_up")
SMALL = ("norm_mix", "conv_b", "dt_bias", "a_log", "d_skip", "ssm_norm", "norm_ffn", "final_norm")
WEIGHTS = ("norm_mix", "w_in", "conv_w", "conv_b", "dt_bias", "a_log", "d_skip", "ssm_norm", "w_out", "norm_ffn", "w_gate", "w_up", "w_down", "final_norm")
SMALL_ROWS = 88
CONVW_ROWS = 96
CONVW_SHARD_ROWS = 16


def _full_from_gathered(name, g, l):
    _, a, b = g.shape
    if name in COL_SHARDED:
        width = IN_PROJ_PAD if name == "w_in" else N_DEV * b
        return _cols_from_devices(g.reshape(N_DEV, 1, a, b), width, f"cols_l{l}_{name}").reshape(a, width)
    return g.reshape(N_DEV * a, b)


def _by_device(name, full, shard_shape, l):
    a, b = shard_shape
    if name in COL_SHARDED:
        return _devices_from_cols([full], b, f"devs_l{l}_{name}").reshape(N_CHIP, 2, a, b)
    return full.reshape(N_CHIP, 2, a, b)


def _pack_rows(parts, rows, width):
    flat = jnp.concatenate([q.reshape(-1) for q in parts])
    return jnp.pad(flat, (0, rows * width - flat.shape[0])).reshape(rows, width)


def _unpack(flat, like):
    out, off = [], 0
    for q in like:
        out.append(flat[off:off + q.size].reshape(q.shape))
        off += q.size
    return out


def kernel(x, positions, norm_mix, w_in, conv_w, conv_b, dt_bias, a_log, d_skip, ssm_norm, w_out, norm_ffn, w_gate, w_up, w_down, final_norm, loss_target, m_norm_mix, m_w_in, m_conv_w, m_conv_b, m_dt_bias, m_a_log, m_d_skip, m_ssm_norm, m_w_out, m_norm_ffn, m_w_gate, m_w_up, m_w_down, m_final_norm, v_norm_mix, v_w_in, v_conv_w, v_conv_b, v_dt_bias, v_a_log, v_d_skip, v_ssm_norm, v_w_out, v_norm_ffn, v_w_gate, v_w_up, v_w_down, v_final_norm):
    w = dict(norm_mix=norm_mix, w_in=w_in, conv_w=conv_w, conv_b=conv_b, dt_bias=dt_bias, a_log=a_log, d_skip=d_skip, ssm_norm=ssm_norm,
             w_out=w_out, norm_ffn=norm_ffn, w_gate=w_gate, w_up=w_up, w_down=w_down, final_norm=final_norm)
    m = dict(norm_mix=m_norm_mix, w_in=m_w_in, conv_w=m_conv_w, conv_b=m_conv_b, dt_bias=m_dt_bias, a_log=m_a_log, d_skip=m_d_skip,
             ssm_norm=m_ssm_norm, w_out=m_w_out, norm_ffn=m_norm_ffn, w_gate=m_w_gate, w_up=m_w_up, w_down=m_w_down, final_norm=m_final_norm)
    v = dict(norm_mix=v_norm_mix, w_in=v_w_in, conv_w=v_conv_w, conv_b=v_conv_b, dt_bias=v_dt_bias, a_log=v_a_log, d_skip=v_d_skip,
             ssm_norm=v_ssm_norm, w_out=v_w_out, norm_ffn=v_norm_ffn, w_gate=v_w_gate, w_up=v_w_up, w_down=v_w_down, final_norm=v_final_norm)
    ax, ay, ac = lax.axis_index("x"), lax.axis_index("y"), lax.axis_index("c")
    dev = 4 * ax + 2 * ay + ac

    assert DEPTH == 2
    t = x.shape[0] * x.shape[1]
    xf, target = x.reshape(t, D_MODEL), loss_target.reshape(t, D_MODEL)

    def own_slot(block):
        return lax.dynamic_update_slice(lax.empty((N_DEV,) + block.shape[1:], block.dtype), block, (dev,) + (0,) * (block.ndim - 1))

    def layer_shard(arr, k, l):
        return jnp.transpose(arr, (2, 0, 1))[:, l, :] if k in TRANSPOSED else arr[l]

    def gather_start(keys, l, tie, name):
        shards = [(layer_shard(w[keys[0]], keys[0], l) + tie).astype(bf16)] + [layer_shard(w[k], k, l).astype(bf16) for k in keys[1:]]
        return _direct_start(shards, [own_slot(s[None]) for s in shards], False, name)

    def scatter_start(keys, grads_l, l, name):
        shapes = [(w[k].shape[2], w[k].shape[1]) if k in TRANSPOSED else w[k].shape[1:] for k in keys]
        by_dev = [_by_device(k, grads_l[k], sh, l).reshape((N_DEV,) + sh) for k, sh in zip(keys, shapes)]
        return _direct_start(by_dev, [own_slot(lax.dynamic_slice_in_dim(g, dev, 1, 0)) for g in by_dev], True, name)

    (g_in0, conv_all), tie = _allgather_two_level([w["w_in"][0].astype(bf16), w["conv_w"]], "gather_l0_w_in")
    rest0_copy = gather_start(REST, 0, tie[0, 0], "gather_l0_rest_start")
    l1_copy = gather_start(BIG, 1, rest0_copy[4][0, 0], "gather_l1_start")
    p = {k: w[k] for k in SMALL}
    p["norm_mix"] = p["norm_mix"] + l1_copy[4][0, 0]
    p["conv_w"] = jnp.transpose(conv_all, (1, 2, 0, 3)).reshape(DEPTH, CONV_WIDTH, CONV_CH)
    sp0, sp1 = _layer_params(p, 0), _layer_params(p, 1)

    def rest0(after):
        lands = _direct_wait(*rest0_copy[:4], after, False, "gather_l0_rest_wait")
        return tuple(_full_from_gathered(k, g, 0) for k, g in zip(REST, lands))

    tabs = _rope_tables(positions.reshape(t, 1), "rope_tables")
    w_in0 = _full_from_gathered("w_in", g_in0, 0)
    h1, saved0 = _layer_fwd(xf, w_in0, rest0, sp0, tabs, 0)
    lands1 = _direct_wait(*l1_copy[:4], h1, False, "gather_l1_wait")
    bigs1 = tuple(_full_from_gathered(k, g, 1) for k, g in zip(BIG, lands1))
    h2, saved1 = _layer_fwd(h1, bigs1[0], bigs1[1:], sp1, tabs, 1)
    dh, dhb, loss_parts, dfn = _final_loss(h2, _row(p["final_norm"]), target, "final_loss")
    loss_local = jnp.sum(loss_parts)

    dh, grads1 = _layer_bwd((dh, dhb), saved1, bigs1, sp1, tabs, 1, gd=bf16)
    l1_grads = scatter_start(BIG, grads1, 1, "scatter_l1_start")
    w_out0, w_gate0, w_up0, w_down0 = saved0["rest"]
    bigs0 = (w_in0, w_out0, w_gate0, w_up0, w_down0 + l1_grads[4][0, 0].astype(bf16))
    ffn0_grads = []

    def after_ffn(grads_ffn):
        ffn0_grads.append(scatter_start(FFN, grads_ffn, 0, "scatter_l0_ffn_start"))
        return ffn0_grads[0][4][0, 0]

    (dx, _), grads0 = _layer_bwd(dh, saved0, bigs0, sp0, tabs, 0, gd=bf16, after_ffn=after_ffn)
    mix0_grads = scatter_start(MIX, grads0, 0, "scatter_l0_mix_start")
    landed = {(k, 1): g for k, g in zip(BIG, _direct_wait(*l1_grads[:4], dx, True, "scatter_l1_wait"))}
    landed.update({(k, 0): g for k, g in zip(FFN, _direct_wait(*ffn0_grads[0][:4], dx, True, "scatter_l0_ffn_wait"))})
    out_g, out_d, out_m, out_v = {}, {}, {}, {}

    def update(keys, after):
        for k in keys:
            parts = [landed[k, l] for l in range(DEPTH)]
            if k in TRANSPOSED:
                depth, a, b = w[k].shape
                state = [jnp.transpose(s, (2, 0, 1)).reshape(b, depth * a) for s in (w[k], m[k], v[k])]
                res = _adamw_layers(parts, *state, after, "adamw_" + k, layers_on_columns=True)
                res = [jnp.transpose(r.reshape(b, depth, a), (1, 2, 0)) for r in res]
            else:
                res = _adamw_layers(parts, w[k], m[k], v[k], after, "adamw_" + k)
            for dst, r in zip((out_g, out_d, out_m, out_v), res):
                dst[k] = r

    update(FFN, mix0_grads[4])
    grads = {k: [grads0[k], grads1[k]] for k in grads0 if k not in BIG}
    grads["final_norm"] = dfn.sum(0) + mix0_grads[4][0, 0]

    small_like = [w[k] for k in SMALL]
    small_grads = [jnp.stack(grads[k]) if k != "final_norm" else grads[k] for k in SMALL]
    small_pack = jnp.concatenate([_pack_rows(small_grads, SMALL_ROWS, LANES), _pack_rows([jnp.stack(grads["conv_w"])], CONVW_ROWS, LANES)], axis=0)
    parts = _allgather_direct(small_pack, "gather_small_grads")
    g_s, d_s, m_s, v_s = _adamw(
        [parts[i, :SMALL_ROWS] for i in range(N_DEV)], _pack_rows(small_like, SMALL_ROWS, LANES),
        _pack_rows([m[k] for k in SMALL], SMALL_ROWS, LANES), _pack_rows([v[k] for k in SMALL], SMALL_ROWS, LANES), "adamw_replicated")
    for dst, src in ((out_g, g_s), (out_d, d_s), (out_m, m_s), (out_v, v_s)):
        dst.update(zip(SMALL, _unpack(src.reshape(-1), small_like)))
    shard_w = conv_w.shape[-1]
    conv_parts = parts[:, SMALL_ROWS:].reshape(N_DEV, DEPTH, CONV_WIDTH, CONV_CH)
    conv_mine = lax.dynamic_slice_in_dim(conv_parts, dev * shard_w, shard_w, axis=3)
    g_c, d_c, m_c, v_c = _adamw(
        [_pack_rows([conv_mine[i]], CONVW_SHARD_ROWS, LANES) for i in range(N_DEV)], _pack_rows([conv_w], CONVW_SHARD_ROWS, LANES),
        _pack_rows([m["conv_w"]], CONVW_SHARD_ROWS, LANES), _pack_rows([v["conv_w"]], CONVW_SHARD_ROWS, LANES), "adamw_conv_w")
    for dst, src in ((out_g, g_c), (out_d, d_c), (out_m, m_c), (out_v, v_c)):
        dst["conv_w"] = src.reshape(-1)[:conv_w.size].reshape(conv_w.shape)

    landed.update({(k, 0): g for k, g in zip(MIX, _direct_wait(*mix0_grads[:4], v_c + out_v["w_down"][0, :CONVW_SHARD_ROWS, :LANES], True, "scatter_l0_mix_wait"))})
    update(MIX, v_c)

    loss = lax.psum(loss_local, ("x", "y", "c"))
    return (loss, dx.reshape(x.shape), *[out_g[k] for k in WEIGHTS], *[out_d[k] for k in WEIGHTS],
            *[out_m[k] for k in WEIGHTS], *[out_v[k] for k in WEIGHTS])
```

```python
import jax
import jax.numpy as jnp
import numpy as np
from jax import lax
from jax.experimental import pallas as pl
from jax.experimental.pallas import tpu as pltpu

f32 = jnp.float32
bf16 = jnp.bfloat16

D_MODEL = 1024
SEQ = 2048
DEPTH = 2
HEAD_DIM = 64
N_ATTN_HEADS = 8
N_KV_HEADS = 2
ATTN_WIDTH = 512
KV_WIDTH = 128
ROPE_DIM = 16
ROPE_THETA = 500000.0
DILATIONS = (1, 4, 16)
ATTN_BLOCK = 128
SSM_HEADS = 16
SSM_INNER = 1024
SSM_GROUPS = 2
D_STATE = 128
CONV_WIDTH = 4
CHUNK = 128
CONV_CH = 1536
MIX_WIDTH = 1536
QKV_WIDTH = ATTN_WIDTH + 2 * KV_WIDTH
DT_OFF = 3328
IN_PROJ = 3344
IN_PROJ_PAD = 3456
FFN_HIDDEN = 2816
EPS = 1e-5
N_DEV = 8
ADAM_LR = 0.001
ADAM_B1 = 0.9
ADAM_B2 = 0.999
ADAM_EPS = 1e-08
ADAM_WD = 0.01
ADAM_STEP = 10

LANES = 128
SUBLANES = 8
VMEM_LIMIT = 56 * 1024 * 1024
VMEM_LIMIT_TWO_PAIRS = 60 * 1024 * 1024

MESH = pl.DeviceIdType.MESH
ANY = pl.BlockSpec(memory_space=pl.ANY)


def _cparams(sem, vmem=None):
    return pltpu.CompilerParams(dimension_semantics=sem, vmem_limit_bytes=vmem or VMEM_LIMIT)


def _sigmoid(x):
    return 1.0 / (1.0 + jnp.exp(-x))


def _silu(x):
    return x * _sigmoid(x)


def _dsilu(x):
    s = _sigmoid(x)
    return s * (1.0 + x * (1.0 - s))


def _silu_and_grad(x):
    s = _sigmoid(x)
    return x * s, s * (1.0 + x * (1.0 - s))


def _softplus(x):
    return jnp.maximum(x, 0.0) + jnp.log(1.0 + jnp.exp(-jnp.abs(x)))


def _dot(a, b, dims, precision=None):
    return lax.dot_general(a, b, (dims, ((), ())), preferred_element_type=f32, precision=precision)


def _nn(a, b, precision=None):
    return _dot(a, b, ((1,), (0,)), precision)


def _nt(a, b):
    return _dot(a, b, ((1,), (1,)))


def _tn(a, b):
    return _dot(a, b, ((0,), (0,)))


def _rowsum8(t):
    n, w = t.shape
    return jnp.sum(t.reshape(n // SUBLANES, SUBLANES, w), axis=0)


def _matmul(a, b, *, mode, n_out=None, b_off=0, add=None, out_dtype=f32, tm=2048, tn=512, tk=1024, name):
    if mode == "tn":
        kk, m = a.shape
    else:
        m, kk = a.shape
    n = n_out if n_out is not None else (b.shape[0] if mode == "nt" else b.shape[1])
    tm, tn, tk = min(tm, m), min(tn, n), min(tk, kk)
    assert m % tm == 0 and n % tn == 0 and kk % tk == 0, (name, m, n, kk, tm, tn, tk)
    nk = kk // tk
    if mode == "nn":
        a_spec = pl.BlockSpec((tm, tk), lambda i, j, k: (i, k))
        b_spec = pl.BlockSpec((tk, tn), lambda i, j, k: (k, j + b_off))
        dims = ((1,), (0,))
    elif mode == "nt":
        a_spec = pl.BlockSpec((tm, tk), lambda i, j, k: (i, k))
        b_spec = pl.BlockSpec((tn, tk), lambda i, j, k: (j + b_off, k))
        dims = ((1,), (1,))
    else:
        a_spec = pl.BlockSpec((tk, tm), lambda i, j, k: (k, i))
        b_spec = pl.BlockSpec((tk, tn), lambda i, j, k: (k, j + b_off))
        dims = ((0,), (0,))
    o_spec = pl.BlockSpec((tm, tn), lambda i, j, k: (i, j))
    has_add = add is not None

    def body(*refs):
        if has_add:
            a_ref, b_ref, add_ref, o_ref, acc_ref = refs
        else:
            a_ref, b_ref, o_ref, acc_ref = refs
        k = pl.program_id(2)
        part = _dot(a_ref[...].astype(bf16), b_ref[...].astype(bf16), dims)

        @pl.when(k == 0)
        def _():
            acc_ref[...] = part

        @pl.when(k > 0)
        def _():
            acc_ref[...] += part

        @pl.when(k == nk - 1)
        def _():
            r = acc_ref[...]
            if has_add:
                r = r + add_ref[...]
            o_ref[...] = r.astype(out_dtype)

    in_specs = [a_spec, b_spec] + ([o_spec] if has_add else [])
    args = (a, b) + ((add,) if has_add else ())
    return pl.pallas_call(
        body, name=name, grid=(m // tm, n // tn, nk), in_specs=in_specs, out_specs=o_spec,
        out_shape=jax.ShapeDtypeStruct((m, n), out_dtype), scratch_shapes=[pltpu.VMEM((tm, tn), f32)],
        compiler_params=_cparams(("parallel", "parallel", "arbitrary")),
    )(*args)


def _in_proj(hn, w_in, widths, name, tm=2048, tn=256):
    m, k = hn.shape
    starts = [sum(widths[:i]) // tn for i in range(len(widths))]
    counts = [wd // tn for wd in widths]
    assert m % tm == 0 and all(wd % tn == 0 for wd in widths)
    n_out = len(widths)

    def body(a_ref, w_ref, *o_refs):
        j = pl.program_id(1)
        acc = _nn(a_ref[...], w_ref[...])
        for s, c, o_ref in zip(starts, counts, o_refs):
            @pl.when((j >= s) & (j < s + c))
            def _(o_ref=o_ref):
                o_ref[...] = acc

    def o_spec(s, c):
        return pl.BlockSpec((tm, tn), lambda i, j: (i, jnp.clip(j - s, 0, c - 1)))

    return pl.pallas_call(
        body, name=name, grid=(m // tm, sum(counts)),
        in_specs=[pl.BlockSpec((tm, k), lambda i, j: (i, 0)), pl.BlockSpec((k, tn), lambda i, j: (0, j))],
        out_specs=[o_spec(s, c) for s, c in zip(starts, counts)],
        out_shape=[jax.ShapeDtypeStruct((m, wd), f32) for wd in widths], compiler_params=_cparams(("parallel", "arbitrary")),
    )(hn, w_in)


def _out_proj(o, yn, w_out, h, name, tm=2048, tn=512):
    m, kb = o.shape
    n = w_out.shape[1]
    n_y = yn.shape[1] // kb
    assert yn.shape[1] % kb == 0 and w_out.shape[0] == kb * (1 + n_y) and m % tm == 0 and n % tn == 0

    def body(*refs):
        o_ref, y_refs, w_refs, h_ref, out_ref = refs[0], refs[1:1 + n_y], refs[1 + n_y:2 + 2 * n_y], refs[-2], refs[-1]
        acc = h_ref[...] + _nn(o_ref[...].astype(bf16), w_refs[0][...])
        for y_ref, w_ref in zip(y_refs, w_refs[1:]):
            acc = acc + _nn(y_ref[...], w_ref[...])
        out_ref[...] = acc

    res = pl.BlockSpec((tm, tn), lambda i, j: (i, j))

    def a_blk(c):
        return pl.BlockSpec((tm, kb), lambda i, j: (i, c))

    def w_blk(r):
        return pl.BlockSpec((kb, tn), lambda i, j: (r, j))

    return pl.pallas_call(
        body, name=name, grid=(m // tm, n // tn),
        in_specs=[a_blk(0)] + [a_blk(c) for c in range(n_y)] + [w_blk(r) for r in range(1 + n_y)] + [res],
        out_specs=res, out_shape=jax.ShapeDtypeStruct((m, n), f32), compiler_params=_cparams(("parallel", "parallel")),
    )(o, *[yn] * n_y, *[w_out] * (1 + n_y), h)


def _swiglu_fwd(hn, w_gate, w_up, name, tm=2048, tn=256):
    m, k = hn.shape
    n = w_gate.shape[0]
    assert m % tm == 0 and n % tn == 0, (name, m, n, tm, tn)

    def body(a_ref, wg_ref, wu_ref, g_ref, u_ref, act_ref):
        a = a_ref[...]
        g = _nt(a, wg_ref[...])
        u = _nt(a, wu_ref[...])
        sg, dsg = _silu_and_grad(g)
        g_ref[...] = (u * dsg).astype(bf16)
        u_ref[...] = sg.astype(bf16)
        act_ref[...] = (sg * u).astype(bf16)

    a_spec = pl.BlockSpec((tm, k), lambda i, j: (i, 0))
    w_spec = pl.BlockSpec((tn, k), lambda i, j: (j, 0))
    o_spec = pl.BlockSpec((tm, tn), lambda i, j: (i, j))
    return pl.pallas_call(
        body, name=name, grid=(m // tm, n // tn), in_specs=[a_spec, w_spec, w_spec], out_specs=[o_spec, o_spec, o_spec],
        out_shape=[jax.ShapeDtypeStruct((m, n), bf16)] * 3,
        compiler_params=_cparams(("parallel", "parallel")),
    )(hn, w_gate, w_up)


def _swiglu_bwd(dh, w_down, g, u, name, tm=2048, tn=256):
    m, k = dh.shape
    n = w_down.shape[0]
    assert m % tm == 0 and n % tn == 0, (name, m, n, tm, tn)

    def body(a_ref, w_ref, g_ref, u_ref, dg_ref, du_ref):
        dact = _nt(a_ref[...].astype(bf16), w_ref[...])
        dg_ref[...] = (dact * g_ref[...].astype(f32)).astype(bf16)
        du_ref[...] = (dact * u_ref[...].astype(f32)).astype(bf16)

    a_spec = pl.BlockSpec((tm, k), lambda i, j: (i, 0))
    w_spec = pl.BlockSpec((tn, k), lambda i, j: (j, 0))
    o_spec = pl.BlockSpec((tm, tn), lambda i, j: (i, j))
    return pl.pallas_call(
        body, name=name, grid=(m // tm, n // tn), in_specs=[a_spec, w_spec, o_spec, o_spec], out_specs=[o_spec, o_spec],
        out_shape=[jax.ShapeDtypeStruct((m, n), bf16), jax.ShapeDtypeStruct((m, n), bf16)],
        compiler_params=_cparams(("parallel", "parallel")),
    )(dh, w_down, g, u)


def _rmsnorm_fwd(h, w, name, tm=512):
    m, d = h.shape

    def body(h_ref, w_ref, o_ref):
        x = h_ref[...]
        r = lax.rsqrt(jnp.mean(x * x, axis=-1, keepdims=True) + EPS)
        o_ref[...] = (x * r * w_ref[...]).astype(bf16)

    return pl.pallas_call(
        body, name=name, grid=(m // tm,),
        in_specs=[pl.BlockSpec((tm, d), lambda i: (i, 0)), pl.BlockSpec((1, d), lambda i: (0, 0))],
        out_specs=pl.BlockSpec((tm, d), lambda i: (i, 0)), out_shape=jax.ShapeDtypeStruct((m, d), bf16),
        compiler_params=_cparams(("parallel",)),
    )(h, w)


def _nt_norm_bwd(pairs, h, w, dres, name, tk, b_is_kd=False, bf16_copy=True, tm=1024, vmem=None):
    m, d = h.shape
    contract = _nn if b_is_kd else _nt
    steps = [p[0].shape[1] // tk for p in pairs]
    assert all(p[0].shape[1] % tk == 0 for p in pairs), (name, tk)
    starts = [sum(steps[:i]) for i in range(len(pairs))]
    nk = sum(steps)
    n_p = len(pairs)

    def body(*refs):
        ab = refs[:2 * n_p]
        h_ref, w_ref, dres_ref, dh_ref = refs[2 * n_p:2 * n_p + 4]
        dhb_ref = refs[2 * n_p + 4] if bf16_copy else None
        dw_ref, acc_ref = refs[-2:]
        i, k = pl.program_id(0), pl.program_id(1)

        @pl.when(k == 0)
        def _():
            acc_ref[...] = jnp.zeros_like(acc_ref)

        for p in range(n_p):
            @pl.when((k >= starts[p]) & (k < starts[p] + steps[p]))
            def _(p=p):
                acc_ref[...] += contract(ab[2 * p][...], ab[2 * p + 1][...])

        @pl.when(k == nk - 1)
        def _():
            x = h_ref[...]
            r = lax.rsqrt(jnp.mean(x * x, axis=-1, keepdims=True) + EPS)
            xhat = x * r
            dy = acc_ref[...]
            gw = dy * w_ref[...]
            dh = dres_ref[...] + r * (gw - xhat * jnp.mean(gw * xhat, axis=-1, keepdims=True))
            dh_ref[...] = dh
            if bf16_copy:
                dhb_ref[...] = dh.astype(bf16)
            part = _rowsum8(dy * xhat)

            @pl.when(i == 0)
            def _():
                dw_ref[...] = part

            @pl.when(i > 0)
            def _():
                dw_ref[...] += part

    def clamp(k, p):
        return jnp.clip(k - starts[p], 0, steps[p] - 1)

    in_specs = []
    for p in range(n_p):
        b_spec = (pl.BlockSpec((tk, d), lambda i, k, p=p: (clamp(k, p), 0)) if b_is_kd
                  else pl.BlockSpec((d, tk), lambda i, k, p=p: (0, clamp(k, p))))
        in_specs += [pl.BlockSpec((tm, tk), lambda i, k, p=p: (i, clamp(k, p))), b_spec]
    row = pl.BlockSpec((tm, d), lambda i, k: (i, 0))
    in_specs += [row, pl.BlockSpec((1, d), lambda i, k: (0, 0)), row]
    return pl.pallas_call(
        body, name=name, grid=(m // tm, nk), in_specs=in_specs,
        out_specs=[row] + [row] * bf16_copy + [pl.BlockSpec((SUBLANES, d), lambda i, k: (0, 0))],
        out_shape=[jax.ShapeDtypeStruct((m, d), f32)] + [jax.ShapeDtypeStruct((m, d), bf16)] * bf16_copy + [jax.ShapeDtypeStruct((SUBLANES, d), f32)],
        scratch_shapes=[pltpu.VMEM((tm, d), f32)], compiler_params=_cparams(("arbitrary", "arbitrary"), vmem),
    )(*[t for p in pairs for t in p], h, w, dres)


def _final_loss(h, w, target, name, tm=512):
    m, d = h.shape

    def body(h_ref, w_ref, t_ref, dh_ref, dhb_ref, loss_ref, dw_ref):
        x = h_ref[...]
        r = lax.rsqrt(jnp.mean(x * x, axis=-1, keepdims=True) + EPS)
        xhat = x * r
        ww = w_ref[...]
        err = xhat * ww - t_ref[...]
        dy = err * (1.0 / d)
        gw = dy * ww
        dh = r * (gw - xhat * jnp.mean(gw * xhat, axis=-1, keepdims=True))
        dh_ref[...] = dh
        dhb_ref[...] = dh.astype(bf16)
        lpart = _rowsum8(err * err) * (0.5 / d)
        wpart = _rowsum8(dy * xhat)

        @pl.when(pl.program_id(0) == 0)
        def _():
            loss_ref[...] = lpart
            dw_ref[...] = wpart

        @pl.when(pl.program_id(0) > 0)
        def _():
            loss_ref[...] += lpart
            dw_ref[...] += wpart

    row = pl.BlockSpec((tm, d), lambda i: (i, 0))
    acc = pl.BlockSpec((SUBLANES, d), lambda i: (0, 0))
    return pl.pallas_call(
        body, name=name, grid=(m // tm,),
        in_specs=[row, pl.BlockSpec((1, d), lambda i: (0, 0)), row], out_specs=[row, row, acc, acc],
        out_shape=[jax.ShapeDtypeStruct((m, d), f32), jax.ShapeDtypeStruct((m, d), bf16),
                   jax.ShapeDtypeStruct((SUBLANES, d), f32), jax.ShapeDtypeStruct((SUBLANES, d), f32)],
        compiler_params=_cparams(("arbitrary",)),
    )(h, w, target)


def _lane_tables():
    f = np.arange(LANES) % HEAD_DIM
    inv = ROPE_THETA ** (-jnp.arange(0, ROPE_DIM, 2, dtype=f32) / ROPE_DIM)
    invf = jnp.where(f < ROPE_DIM, inv[f % (ROPE_DIM // 2)], 0.0).astype(f32)
    return invf.reshape(1, LANES)


def _rope_tables(pos_col, name):
    t = pos_col.shape[0]
    tm = SEQ

    def body(p_ref, f_ref, c_ref, s1_ref, s2_ref):
        ang = p_ref[...].astype(f32) * f_ref[...]
        co, si = jnp.cos(ang), jnp.sin(ang)
        f = lax.broadcasted_iota(jnp.int32, (tm, LANES), 1) % HEAD_DIM
        c_ref[...] = jnp.where(f < ROPE_DIM, co, 1.0)
        s1_ref[...] = jnp.where(f < ROPE_DIM // 2, -si, 0.0)
        s2_ref[...] = jnp.where((f >= ROPE_DIM // 2) & (f < ROPE_DIM), si, 0.0)

    row = pl.BlockSpec((tm, LANES), lambda i: (i, 0))
    return pl.pallas_call(
        body, name=name, grid=(t // tm,),
        in_specs=[pl.BlockSpec((tm, 1), lambda i: (i, 0)), pl.BlockSpec((1, LANES), lambda i: (0, 0))],
        out_specs=[row, row, row], out_shape=[jax.ShapeDtypeStruct((t, LANES), f32)] * 3,
        compiler_params=_cparams(("parallel",)),
    )(pos_col, _lane_tables())


def _rot(x, c, s1, s2):
    return x * c + pltpu.roll(x, LANES - ROPE_DIM // 2, 1) * s1 + pltpu.roll(x, ROPE_DIM // 2, 1) * s2


def _rot_t(g, c, s1, s2):
    return g * c + pltpu.roll(g * s1, ROPE_DIM // 2, 1) + pltpu.roll(g * s2, LANES - ROPE_DIM // 2, 1)


def _dup_head(x, kvh, low):
    a = jnp.where(kvh == 0, x, pltpu.roll(x, HEAD_DIM, 1))
    return jnp.where(low, a, pltpu.roll(a, HEAD_DIM, 1))


def _deinterleave(src_ref, dst_ref, d, dtype):
    length = SEQ // d
    if d == 1:
        dst_ref[...] = src_ref[...].astype(dtype)
    else:
        for r in range(d):
            dst_ref[pl.ds(r * length, length), :] = src_ref[pl.ds(r, length, stride=d), :].astype(dtype)


def _interleave_store(src_ref, dst_ref, d, accumulate):
    length = SEQ // d
    if d == 1:
        if accumulate:
            dst_ref[...] += src_ref[...]
        else:
            dst_ref[...] = src_ref[...]
    else:
        for r in range(d):
            blk = src_ref[pl.ds(r * length, length), :]
            if accumulate:
                dst_ref[pl.ds(r, length, stride=d), :] = dst_ref[pl.ds(r, length, stride=d), :] + blk
            else:
                dst_ref[pl.ds(r, length, stride=d), :] = blk


def _attn_masks():
    qi = lax.broadcasted_iota(jnp.int32, (ATTN_BLOCK, ATTN_BLOCK), 0)
    ki = lax.broadcasted_iota(jnp.int32, (ATTN_BLOCK, ATTN_BLOCK), 1)
    low = lax.broadcasted_iota(jnp.int32, (ATTN_BLOCK, LANES), 1) < HEAD_DIM
    return ki <= qi, ki >= qi, low


NEG_INF = float("-inf")
ATTN_UNROLL = 4


N_BRANCH = len(DILATIONS)


def _attn_prep(qkv, tabs, name):
    t = qkv.shape[0]
    nb = t // SEQ
    n_j = ATTN_WIDTH // LANES

    def q_body(q_ref, c_ref, s1_ref, s2_ref, out_ref, xr):
        xr[...] = _rot(q_ref[...], c_ref[...], s1_ref[...], s2_ref[...]) * (HEAD_DIM ** -0.5)
        for bi, d in enumerate(DILATIONS):
            _deinterleave(xr, out_ref.at[bi], d, bf16)

    def kv_body(x_ref, c_ref, s1_ref, s2_ref, out_ref, xr):
        lowfull = lax.broadcasted_iota(jnp.int32, (SEQ, LANES), 1) < HEAD_DIM
        x = x_ref[...]
        x = jnp.where(pl.program_id(1) == 0, _rot(x, c_ref[...], s1_ref[...], s2_ref[...]), x)
        for kvh in range(N_KV_HEADS):
            xr[...] = _dup_head(x, kvh, lowfull)
            for bi, d in enumerate(DILATIONS):
                length = SEQ // d
                for r in range(d):
                    rows = xr[...] if d == 1 else xr[pl.ds(r, length, stride=d), :]
                    out_ref[0, bi, pl.ds(r * length, length), kvh * LANES:(kvh + 1) * LANES] = rows.astype(bf16)

    tab = pl.BlockSpec((SEQ, LANES), lambda b, j: (b, 0))
    q = pl.pallas_call(
        q_body, name=name + "_q", grid=(nb, n_j),
        in_specs=[pl.BlockSpec((SEQ, LANES), lambda b, j: (b, j)), tab, tab, tab],
        out_specs=pl.BlockSpec((N_BRANCH, SEQ, LANES), lambda b, j: (0, b, j)),
        out_shape=jax.ShapeDtypeStruct((N_BRANCH, t, ATTN_WIDTH), bf16), scratch_shapes=[pltpu.VMEM((SEQ, LANES), f32)],
        compiler_params=_cparams(("parallel", "parallel")),
    )(qkv, *tabs)
    kv = pl.pallas_call(
        kv_body, name=name + "_kv", grid=(nb, 2),
        in_specs=[pl.BlockSpec((SEQ, LANES), lambda b, j: (b, n_j + j)), tab, tab, tab],
        out_specs=pl.BlockSpec((1, N_BRANCH, SEQ, N_KV_HEADS * LANES), lambda b, j: (j, 0, b, 0)),
        out_shape=jax.ShapeDtypeStruct((2, N_BRANCH, t, N_KV_HEADS * LANES), bf16), scratch_shapes=[pltpu.VMEM((SEQ, LANES), f32)],
        compiler_params=_cparams(("parallel", "parallel")),
    )(qkv, *tabs)
    return q, kv


def _attn_fwd(prep, name):
    q_all, kv_all = prep
    t = q_all.shape[1]
    nb = t // SEQ
    n_blk = SEQ // ATTN_BLOCK

    def body(q_ref, k_ref, v_ref, o_ref, lse_ref, ob, lb, o0, o1, o2, l0, l1, l2, ss):
        cur_ok, prev_ok, low = _attn_masks()
        onat, lnat = (o0, o1, o2), (l0, l1, l2)
        for bi, d in enumerate(DILATIONS):
            qd, kd, vd = q_ref.at[bi], k_ref.at[0, bi], v_ref.at[0, bi]
            per_res = n_blk // d
            use_prev = per_res > 1

            def scores(n, carry):
                start = pl.multiple_of(n * ATTN_BLOCK, ATTN_BLOCK)
                has_prev = (n % per_res) != 0
                pstart = pl.multiple_of(jnp.maximum(n - 1, 0) * ATTN_BLOCK, ATTN_BLOCK)
                qb = qd[pl.ds(start, ATTN_BLOCK), :]
                kc = kd[pl.ds(start, ATTN_BLOCK), :]
                if use_prev:
                    kp = kd[pl.ds(pstart, ATTN_BLOCK), :]
                for a in range(2):
                    qa = jnp.where(low if a == 0 else ~low, qb, jnp.zeros_like(qb))
                    ss[2 * n + a, :, 0:ATTN_BLOCK] = jnp.where(cur_ok, _nt(qa, kc), NEG_INF)
                    if use_prev:
                        ss[2 * n + a, :, ATTN_BLOCK:2 * ATTN_BLOCK] = jnp.where(prev_ok & has_prev, _nt(qa, kp), NEG_INF)
                return carry

            def softmax_pv(n, carry):
                start = pl.multiple_of(n * ATTN_BLOCK, ATTN_BLOCK)
                pstart = pl.multiple_of(jnp.maximum(n - 1, 0) * ATTN_BLOCK, ATTN_BLOCK)
                vc = vd[pl.ds(start, ATTN_BLOCK), :]
                if use_prev:
                    vp = vd[pl.ds(pstart, ATTN_BLOCK), :]
                outs, lses = [], []
                for a in range(2):
                    sc = ss[2 * n + a, :, 0:ATTN_BLOCK]
                    if use_prev:
                        sp = ss[2 * n + a, :, ATTN_BLOCK:2 * ATTN_BLOCK]
                        m = jnp.max(jnp.maximum(sc, sp), axis=1, keepdims=True)
                        pc, pp = jnp.exp(sc - m), jnp.exp(sp - m)
                        den = jnp.sum(pc + pp, axis=1, keepdims=True)
                        acc = _nn(pc.astype(bf16), vc) + _nn(pp.astype(bf16), vp)
                    else:
                        m = jnp.max(sc, axis=1, keepdims=True)
                        pc = jnp.exp(sc - m)
                        den = jnp.sum(pc, axis=1, keepdims=True)
                        acc = _nn(pc.astype(bf16), vc)
                    outs.append(acc * (1.0 / den))
                    lses.append(m + jnp.log(den))
                ob[pl.ds(start, ATTN_BLOCK), :] = jnp.where(low, outs[0], outs[1])
                lb[pl.ds(start, ATTN_BLOCK), :] = jnp.where(low, lses[0], lses[1])
                return carry

            lax.fori_loop(0, n_blk, scores, 0, unroll=ATTN_UNROLL)
            lax.fori_loop(0, n_blk, softmax_pv, 0, unroll=ATTN_UNROLL)
            _interleave_store(ob, onat[bi], d, False)
            _interleave_store(lb, lnat[bi], d, False)
        la, lbb, lc = l0[...], l1[...], l2[...]
        lm = jnp.maximum(jnp.maximum(la, lbb), lc)
        wa, wb, wc = jnp.exp(la - lm), jnp.exp(lbb - lm), jnp.exp(lc - lm)
        ws = wa + wb + wc
        o_ref[...] = (wa * o0[...] + wb * o1[...] + wc * o2[...]) / ws
        lse_ref[...] = lm + jnp.log(ws)

    def col(jj):
        return pl.BlockSpec((SEQ, LANES), lambda b, j: (b, jj if jj is not None else j))

    fs = pltpu.VMEM((SEQ, LANES), f32)
    return pl.pallas_call(
        body, name=name, grid=(nb, ATTN_WIDTH // LANES),
        in_specs=[pl.BlockSpec((N_BRANCH, SEQ, LANES), lambda b, j: (0, b, j)),
                  pl.BlockSpec((1, N_BRANCH, SEQ, LANES), lambda b, j: (0, 0, b, j // 2)),
                  pl.BlockSpec((1, N_BRANCH, SEQ, LANES), lambda b, j: (1, 0, b, j // 2))],
        out_specs=[col(None), col(None)],
        out_shape=[jax.ShapeDtypeStruct((t, ATTN_WIDTH), f32), jax.ShapeDtypeStruct((t, ATTN_WIDTH), f32)],
        scratch_shapes=[fs, fs, fs, fs, fs, fs, fs, fs, pltpu.VMEM((2 * n_blk, ATTN_BLOCK, 2 * ATTN_BLOCK), f32)],
        compiler_params=_cparams(("parallel", "parallel")),
    )(q_all, kv_all, kv_all)


def _attn_bwd(prep, tabs, o, lse, do, name):
    q_all, kv_all = prep
    t = q_all.shape[1]
    nb = t // SEQ
    n_blk = SEQ // ATTN_BLOCK
    n_j = ATTN_WIDTH // LANES

    def body(q_ref, k_ref, v_ref, c_ref, s1_ref, s2_ref, o_ref, lse_ref, do_ref, dq_ref, dk_ref, dv_ref,
             dl, dod, lsd, dld, dqd, dkd, dvd, dqa, dka, dva, pb, dsb, dk_acc, dv_acc):
        j = pl.program_id(1)
        pb[2 * n_blk:2 * n_blk + 2] = jnp.zeros((2, ATTN_BLOCK, 2 * ATTN_BLOCK), bf16)
        dsb[2 * n_blk:2 * n_blk + 2] = jnp.zeros((2, ATTN_BLOCK, 2 * ATTN_BLOCK), bf16)
        kvh = j // 2
        cur_ok, prev_ok, low = _attn_masks()
        lowfull = lax.broadcasted_iota(jnp.int32, (SEQ, LANES), 1) < HEAD_DIM
        c, s1, s2 = c_ref[...], s1_ref[...], s2_ref[...]
        prod = do_ref[...] * o_ref[...]
        d_lo = jnp.sum(jnp.where(lowfull, prod, 0.0), axis=1, keepdims=True)
        d_hi = jnp.sum(jnp.where(lowfull, 0.0, prod), axis=1, keepdims=True)
        dl[...] = jnp.where(lowfull, d_lo, d_hi)
        dqa[...] = jnp.zeros_like(dqa)
        dka[...] = jnp.zeros_like(dka)
        dva[...] = jnp.zeros_like(dva)
        for bi, d in enumerate(DILATIONS):
            qd, kd, vd = q_ref.at[bi], k_ref.at[0, bi], v_ref.at[0, bi]
            _deinterleave(do_ref, dod, d, bf16)
            _deinterleave(lse_ref, lsd, d, f32)
            _deinterleave(dl, dld, d, f32)
            per_res = n_blk // d
            use_prev = per_res > 1
            curl, prevl = slice(0, ATTN_BLOCK), slice(ATTN_BLOCK, 2 * ATTN_BLOCK)

            def halves(x):
                zero = jnp.zeros_like(x)
                return jnp.where(low, x, zero), jnp.where(low, zero, x)

            def probs(n, carry):
                start = pl.multiple_of(n * ATTN_BLOCK, ATTN_BLOCK)
                has_prev = (n % per_res) != 0
                pstart = pl.multiple_of(jnp.maximum(n - 1, 0) * ATTN_BLOCK, ATTN_BLOCK)
                cur, prev = pl.ds(start, ATTN_BLOCK), pl.ds(pstart, ATTN_BLOCK)
                qas, doas = halves(qd[cur, :]), halves(dod[cur, :])
                kc, vc = kd[cur, :], vd[cur, :]
                if use_prev:
                    kp, vp = kd[prev, :], vd[prev, :]
                lsb, dlb = lsd[cur, :], dld[cur, :]
                for a in range(2):
                    ls = lsb[:, a * HEAD_DIM:a * HEAD_DIM + 1]
                    de = dlb[:, a * HEAD_DIM:a * HEAD_DIM + 1]
                    pc = jnp.exp(jnp.where(cur_ok, _nt(qas[a], kc), NEG_INF) - ls)
                    pb[2 * n + a, :, curl] = pc.astype(bf16)
                    dsb[2 * n + a, :, curl] = (pc * (_nt(doas[a], vc) - de)).astype(bf16)
                    if use_prev:
                        pp = jnp.exp(jnp.where(prev_ok & has_prev, _nt(qas[a], kp), NEG_INF) - ls)
                        pb[2 * n + a, :, prevl] = pp.astype(bf16)
                        dsb[2 * n + a, :, prevl] = (pp * (_nt(doas[a], vp) - de)).astype(bf16)
                return carry

            def grads(n, carry):
                start = pl.multiple_of(n * ATTN_BLOCK, ATTN_BLOCK)
                pstart = pl.multiple_of(jnp.maximum(n - 1, 0) * ATTN_BLOCK, ATTN_BLOCK)
                nstart = pl.multiple_of(jnp.minimum(n + 1, n_blk - 1) * ATTN_BLOCK, ATTN_BLOCK)
                cur, prev, nxt = pl.ds(start, ATTN_BLOCK), pl.ds(pstart, ATTN_BLOCK), pl.ds(nstart, ATTN_BLOCK)
                kc = kd[cur, :]
                dqs = [_nn(dsb[2 * n + a, :, curl], kc) for a in range(2)]
                q_rows, do_rows = list(halves(qd[cur, :])), list(halves(dod[cur, :]))
                ds_rows, p_rows = [dsb[2 * n + a, :, curl] for a in range(2)], [pb[2 * n + a, :, curl] for a in range(2)]
                if use_prev:
                    kp = kd[prev, :]
                    dqs = [dqs[a] + _nn(dsb[2 * n + a, :, prevl], kp) for a in range(2)]
                    q_rows += list(halves(qd[nxt, :]))
                    do_rows += list(halves(dod[nxt, :]))
                    ds_rows += [dsb[2 * n + 2 + a, :, prevl] for a in range(2)]
                    p_rows += [pb[2 * n + 2 + a, :, prevl] for a in range(2)]
                dqd[cur, :] = jnp.where(low, dqs[0], dqs[1])
                dkd[cur, :] = _tn(jnp.concatenate(ds_rows, axis=0), jnp.concatenate(q_rows, axis=0))
                dvd[cur, :] = _tn(jnp.concatenate(p_rows, axis=0), jnp.concatenate(do_rows, axis=0))
                return carry

            lax.fori_loop(0, n_blk, probs, 0, unroll=ATTN_UNROLL)
            lax.fori_loop(0, n_blk, grads, 0, unroll=ATTN_UNROLL)
            _interleave_store(dqd, dqa, d, True)
            _interleave_store(dkd, dka, d, True)
            _interleave_store(dvd, dva, d, True)
        dq_ref[...] = _rot_t(dqa[...] * (HEAD_DIM ** -0.5), c, s1, s2).astype(bf16)
        dkf = dka[...]
        dkf = _rot_t(dkf + pltpu.roll(dkf, HEAD_DIM, 1), c, s1, s2)
        dvf = dva[...]
        dvf = dvf + pltpu.roll(dvf, HEAD_DIM, 1)
        mine = (lax.broadcasted_iota(jnp.int32, (SEQ, LANES), 1) // HEAD_DIM) == kvh
        dkc_, dvc_ = jnp.where(mine, dkf, 0.0), jnp.where(mine, dvf, 0.0)

        @pl.when(j == 0)
        def _():
            dk_acc[...] = dkc_
            dv_acc[...] = dvc_

        @pl.when(j > 0)
        def _():
            dk_acc[...] += dkc_
            dv_acc[...] += dvc_

        @pl.when(j == n_j - 1)
        def _():
            dk_ref[...] = dk_acc[...].astype(bf16)
            dv_ref[...] = dv_acc[...].astype(bf16)

    def col(jj):
        return pl.BlockSpec((SEQ, LANES), lambda b, j: (b, jj if jj is not None else j))

    tab = pl.BlockSpec((SEQ, LANES), lambda b, j: (b, 0))
    fs = pltpu.VMEM((SEQ, LANES), f32)
    hs = pltpu.VMEM((SEQ, LANES), bf16)
    return pl.pallas_call(
        body, name=name, grid=(nb, n_j),
        in_specs=[pl.BlockSpec((N_BRANCH, SEQ, LANES), lambda b, j: (0, b, j)),
                  pl.BlockSpec((1, N_BRANCH, SEQ, LANES), lambda b, j: (0, 0, b, j // 2)),
                  pl.BlockSpec((1, N_BRANCH, SEQ, LANES), lambda b, j: (1, 0, b, j // 2)),
                  tab, tab, tab, col(None), col(None), col(None)],
        out_specs=[col(None), tab, tab],
        out_shape=[jax.ShapeDtypeStruct((t, ATTN_WIDTH), bf16), jax.ShapeDtypeStruct((t, LANES), bf16), jax.ShapeDtypeStruct((t, LANES), bf16)],
        scratch_shapes=[fs, hs, fs, fs, fs, fs, fs, fs, fs, fs,
                        pltpu.VMEM((2 * n_blk + 2, ATTN_BLOCK, 2 * ATTN_BLOCK), bf16), pltpu.VMEM((2 * n_blk + 2, ATTN_BLOCK, 2 * ATTN_BLOCK), bf16), fs, fs],
        compiler_params=_cparams(("parallel", "arbitrary")),
    )(q_all, kv_all, kv_all, *tabs, o, lse, do)


def _tap(w_ref, s):
    return w_ref[CONV_WIDTH - 1 - s:CONV_WIDTH - s, :]


def _conv_pre(x, w_ref, b_ref, row):
    shifted = [x] + [jnp.where(row >= s, pltpu.roll(x, s, 0), 0.0) for s in range(1, CONV_WIDTH)]
    pre = b_ref[...] + _tap(w_ref, 0) * x
    for s in range(1, CONV_WIDTH):
        pre = pre + _tap(w_ref, s) * shifted[s]
    return pre, shifted


def _conv_fwd(x, w, b, name, tc=512):
    t, ch = x.shape

    def body(x_ref, w_ref, b_ref, o_ref):
        row = lax.broadcasted_iota(jnp.int32, (SEQ, tc), 0)
        pre, _ = _conv_pre(x_ref[...], w_ref, b_ref, row)
        o_ref[...] = _silu(pre)

    xs = pl.BlockSpec((SEQ, tc), lambda i, j: (i, j))
    return pl.pallas_call(
        body, name=name, grid=(t // SEQ, ch // tc),
        in_specs=[xs, pl.BlockSpec((CONV_WIDTH, tc), lambda i, j: (0, j)), pl.BlockSpec((1, tc), lambda i, j: (0, j))],
        out_specs=xs, out_shape=jax.ShapeDtypeStruct((t, ch), f32),
        compiler_params=_cparams(("parallel", "parallel")),
    )(x, w, b)


def _conv_bwd(x, w, b, dact, name, tc=512):
    t, ch = x.shape

    def body(x_ref, w_ref, b_ref, d_ref, dx_ref, dw_ref, db_ref):
        row = lax.broadcasted_iota(jnp.int32, (SEQ, tc), 0)
        pre, shifted = _conv_pre(x_ref[...], w_ref, b_ref, row)
        dpre = d_ref[...] * _dsilu(pre)
        dx = _tap(w_ref, 0) * dpre
        for s in range(1, CONV_WIDTH):
            dx = dx + _tap(w_ref, s) * jnp.where(row < SEQ - s, pltpu.roll(dpre, SEQ - s, 0), 0.0)
        dx_ref[...] = dx.astype(bf16)
        first = pl.program_id(1) == 0
        parts = [jnp.sum(dpre * shifted[CONV_WIDTH - 1 - k], axis=0, keepdims=True) for k in range(CONV_WIDTH)]
        dbp = jnp.sum(dpre, axis=0, keepdims=True)

        @pl.when(first)
        def _():
            for k in range(CONV_WIDTH):
                dw_ref[k:k + 1, :] = parts[k]
            db_ref[...] = dbp

        @pl.when(jnp.logical_not(first))
        def _():
            for k in range(CONV_WIDTH):
                dw_ref[k:k + 1, :] += parts[k]
            db_ref[...] += dbp

    xs = pl.BlockSpec((SEQ, tc), lambda j, i: (i, j))
    ws = pl.BlockSpec((CONV_WIDTH, tc), lambda j, i: (0, j))
    bs = pl.BlockSpec((1, tc), lambda j, i: (0, j))
    return pl.pallas_call(
        body, name=name, grid=(ch // tc, t // SEQ),
        in_specs=[xs, ws, bs, xs], out_specs=[xs, ws, bs],
        out_shape=[jax.ShapeDtypeStruct((t, ch), bf16), jax.ShapeDtypeStruct((CONV_WIDTH, ch), f32), jax.ShapeDtypeStruct((1, ch), f32)],
        compiler_params=_cparams(("parallel", "arbitrary")),
    )(x, w, b, dact)


GROUP_W = SSM_INNER // SSM_GROUPS
HEADS_PER_GROUP = SSM_HEADS // SSM_GROUPS


def _split3(x):
    hi = x.astype(bf16)
    r1 = x - hi.astype(f32)
    mid = r1.astype(bf16)
    lo = (r1 - mid.astype(f32)).astype(bf16)
    return hi, mid, lo


def _dot_exact(x, sel, dims, x_is_lhs=True):
    parts = _split3(x)
    if x_is_lhs:
        return _dot(parts[0], sel, dims) + _dot(parts[1], sel, dims) + _dot(parts[2], sel, dims)
    return _dot(sel, parts[0], dims) + _dot(sel, parts[1], dims) + _dot(sel, parts[2], dims)


def _ssd_common(xbc_ref, dt_ref, bias_ref, alog_ref):
    r = lax.broadcasted_iota(jnp.int32, (CHUNK, CHUNK), 0)
    cidx = lax.broadcasted_iota(jnp.int32, (CHUNK, CHUNK), 1)
    causal = r >= cidx
    tril = causal.astype(bf16)
    expand = (lax.broadcasted_iota(jnp.int32, (CHUNK, SSM_INNER), 0)
              == lax.broadcasted_iota(jnp.int32, (CHUNK, SSM_INNER), 1) // HEAD_DIM).astype(bf16)
    head_lane = cidx < SSM_HEADS
    dtp = dt_ref[...] + bias_ref[...]
    dt = jnp.where(head_lane, _softplus(dtp), 0.0)
    a_neg = -jnp.exp(alog_ref[...])
    a = dt * a_neg
    nn_dims = ((1,), (0,))
    cs = _dot_exact(a, tril, nn_dims, x_is_lhs=False)
    dt_e = _dot_exact(dt, expand, nn_dims)
    cs_e = _dot_exact(cs, expand, nn_dims)
    xs = xbc_ref[:, 0:SSM_INNER]
    xg = xs * dt_e
    ecs = jnp.exp(cs_e)
    cs_last = cs_e[CHUNK - 1:CHUNK, :]
    dse = jnp.exp(cs_last - cs_e)
    cde = jnp.exp(cs_last)
    return dict(r=r, cidx=cidx, causal=causal, tril=tril, expand=expand, head_lane=head_lane, dtp=dtp, dt=dt, a_neg=a_neg,
                cs=cs, cst=cs.T, dt_e=dt_e, cs_e=cs_e, xs=xs, xg=xg, ecs=ecs, dse=dse, cde=cde)


def _decay_mat(q, h):
    return jnp.exp(jnp.where(q["causal"], q["cs"][:, h:h + 1] - q["cst"][h:h + 1, :], NEG_INF))


def _gate_norm(y, z, nw, gate=None):
    y2 = y * (_silu(z) if gate is None else gate)
    outs, xhats, rs = [], [], []
    for g in range(SSM_GROUPS):
        sl = slice(g * GROUP_W, (g + 1) * GROUP_W)
        yg = y2[:, sl]
        r = lax.rsqrt(jnp.mean(yg * yg, axis=-1, keepdims=True) + EPS)
        xhats.append(yg * r)
        rs.append(r)
        outs.append(yg * r * nw[:, sl])
    return y2, outs, xhats, rs


def _ssd_fwd(xbc, z, dtp, params, name):
    t = xbc.shape[0]
    n_chunk = SEQ // CHUNK
    low = None

    def body(xbc_ref, z_ref, dt_ref, bias_ref, alog_ref, dskip_ref, nw_ref, yn_ref, y_ref, hs_ref, h_scr):
        @pl.when(pl.program_id(1) == 0)
        def _():
            h_scr[...] = jnp.zeros_like(h_scr)

        q = _ssd_common(xbc_ref, dt_ref, bias_ref, alog_ref)
        low = lax.broadcasted_iota(jnp.int32, (CHUNK, LANES), 1) < HEAD_DIM
        xgb = q["xg"].astype(bf16)
        wst = (q["xg"] * q["dse"]).astype(bf16)
        hs_ref[0] = h_scr[...]
        ys = []
        for g in range(SSM_GROUPS):
            gl = slice(g * GROUP_W, (g + 1) * GROUP_W)
            bg = xbc_ref[:, SSM_INNER + g * D_STATE:SSM_INNER + (g + 1) * D_STATE].astype(bf16)
            cg = xbc_ref[:, SSM_INNER + SSM_GROUPS * D_STATE + g * D_STATE:SSM_INNER + SSM_GROUPS * D_STATE + (g + 1) * D_STATE].astype(bf16)
            cb = _nt(cg, bg)
            hg = h_scr[g]
            yoff = _nn(cg, hg.astype(bf16)) * q["ecs"][:, gl]
            pieces = []
            for i in range(HEADS_PER_GROUP // 2):
                h0 = g * HEADS_PER_GROUP + 2 * i
                xp = xgb[:, h0 * HEAD_DIM:(h0 + 2) * HEAD_DIM]
                m0 = (cb * _decay_mat(q, h0)).astype(bf16)
                m1 = (cb * _decay_mat(q, h0 + 1)).astype(bf16)
                zero = jnp.zeros_like(xp)
                pieces.append(_nn(m0, jnp.where(low, xp, zero)) + _nn(m1, jnp.where(low, zero, xp)))
            ys.append(jnp.concatenate(pieces, axis=1) + yoff + dskip_ref[:, gl] * q["xs"][:, gl])
            h_scr[g] = hg * q["cde"][:, gl] + _tn(bg, wst[:, gl])
        y = jnp.concatenate(ys, axis=1)
        y_ref[...] = y
        _, outs, _, _ = _gate_norm(y, z_ref[...], nw_ref[...])
        yn_ref[...] = jnp.concatenate(outs, axis=1).astype(bf16)

    def rows(w):
        return pl.BlockSpec((CHUNK, w), lambda b, c: (b * n_chunk + c, 0))

    def par(w):
        return pl.BlockSpec((1, w), lambda b, c: (0, 0))

    return pl.pallas_call(
        body, name=name, grid=(t // SEQ, n_chunk),
        in_specs=[rows(CONV_CH), rows(SSM_INNER), rows(LANES), par(LANES), par(LANES), par(SSM_INNER), par(SSM_INNER)],
        out_specs=[rows(SSM_INNER), rows(SSM_INNER), pl.BlockSpec((1, SSM_GROUPS, D_STATE, GROUP_W), lambda b, c: (b * n_chunk + c, 0, 0, 0))],
        out_shape=[jax.ShapeDtypeStruct((t, SSM_INNER), bf16), jax.ShapeDtypeStruct((t, SSM_INNER), f32),
                   jax.ShapeDtypeStruct((t // CHUNK, SSM_GROUPS, D_STATE, GROUP_W), f32)],
        scratch_shapes=[pltpu.VMEM((SSM_GROUPS, D_STATE, GROUP_W), f32)],
        compiler_params=_cparams(("parallel", "arbitrary")),
    )(xbc, z, dtp, *params)


def _ssd_bwd(xbc, z, dtp, y, hs, dyn, params, name):
    t = xbc.shape[0]
    n_chunk = SEQ // CHUNK

    def body(xbc_ref, z_ref, dt_ref, y_ref, hs_ref, dyn_ref, bias_ref, alog_ref, dskip_ref, nw_ref,
             dxbc_ref, dz_ref, ddt_ref, dnw_ref, dds_ref, dal_ref, dbi_ref, dh_scr):
        @pl.when(pl.program_id(1) == 0)
        def _():
            dh_scr[...] = jnp.zeros_like(dh_scr)

        q = _ssd_common(xbc_ref, dt_ref, bias_ref, alog_ref)
        low = lax.broadcasted_iota(jnp.int32, (CHUNK, LANES), 1) < HEAD_DIM
        last_row = lax.broadcasted_iota(jnp.int32, (CHUNK, GROUP_W), 0) == CHUNK - 1
        xs, xg = q["xs"], q["xg"]
        xgb = xg.astype(bf16)
        wf = xg * q["dse"]
        wst = wf.astype(bf16)
        zz = z_ref[...]
        yy = y_ref[...]
        sz, dsz = _silu_and_grad(zz)
        y2, _, xhats, rs = _gate_norm(yy, zz, nw_ref[...], gate=sz)
        dyn_ = dyn_ref[...]
        dy2s, dnws = [], []
        for g in range(SSM_GROUPS):
            gl = slice(g * GROUP_W, (g + 1) * GROUP_W)
            gw = dyn_[:, gl] * nw_ref[:, gl]
            dy2s.append(rs[g] * (gw - xhats[g] * jnp.mean(gw * xhats[g], axis=-1, keepdims=True)))
            dnws.append(_rowsum8(dyn_[:, gl] * xhats[g]))
        dy2 = jnp.concatenate(dy2s, axis=1)
        dy = dy2 * sz
        dz_ref[...] = (dy2 * yy * dsz).astype(bf16)
        dnw_p = jnp.concatenate(dnws, axis=1)
        dds_p = _rowsum8(dy * xs)
        dyb = dy.astype(bf16)
        gfull = (dy * q["ecs"]).astype(bf16)
        dcs_c = jnp.zeros((CHUNK, CHUNK), f32)
        dcs_r = jnp.zeros((CHUNK, CHUNK), f32)
        dcs_e_parts, dxg_parts = [], []
        for g in range(SSM_GROUPS):
            gl = slice(g * GROUP_W, (g + 1) * GROUP_W)
            bsl = slice(SSM_INNER + g * D_STATE, SSM_INNER + (g + 1) * D_STATE)
            csl = slice(SSM_INNER + SSM_GROUPS * D_STATE + g * D_STATE, SSM_INNER + SSM_GROUPS * D_STATE + (g + 1) * D_STATE)
            bg = xbc_ref[:, bsl].astype(bf16)
            cg = xbc_ref[:, csl].astype(bf16)
            cb = _nt(cg, bg)
            hg = hs_ref[0, g]
            hgb = hg.astype(bf16)
            dhn = dh_scr[g]
            dhnb = dhn.astype(bf16)
            yoff = _nn(cg, hgb) * q["ecs"][:, gl]
            dw_ = _nn(bg, dhnb)
            r_e = dw_ * wf[:, gl]
            to_last = jnp.sum(r_e, axis=0, keepdims=True) + jnp.sum(dhn * hg, axis=0, keepdims=True) * q["cde"][:, gl]
            dcs_e_parts.append(dy[:, gl] * yoff - r_e + jnp.where(last_row, to_last, 0.0))
            dcb = jnp.zeros((CHUNK, CHUNK), f32)
            dxg_pairs = []
            for i in range(HEADS_PER_GROUP // 2):
                h0 = g * HEADS_PER_GROUP + 2 * i
                psl = slice(h0 * HEAD_DIM, (h0 + 2) * HEAD_DIM)
                xp = xgb[:, psl]
                dyp = dyb[:, psl]
                zero = jnp.zeros_like(dyp)
                tns = []
                for a in range(2):
                    h = h0 + a
                    lm = _decay_mat(q, h)
                    m = cb * lm
                    dm = _nt(jnp.where(low, dyp, zero) if a == 0 else jnp.where(low, zero, dyp), xp)
                    dcb = dcb + dm * lm
                    nmat = dm * m
                    dcs_c = dcs_c + jnp.where(q["cidx"] == h, jnp.sum(nmat, axis=1, keepdims=True), 0.0)
                    dcs_r = dcs_r + jnp.where(q["r"] == h, jnp.sum(nmat, axis=0, keepdims=True), 0.0)
                    tns.append(_tn(m.astype(bf16), dyp))
                dxg_pairs.append(jnp.where(low, tns[0], tns[1]))
            dxg_parts.append(jnp.concatenate(dxg_pairs, axis=1) + dw_ * q["dse"][:, gl])
            dcbb = dcb.astype(bf16)
            dxbc_ref[:, csl] = _nt(gfull[:, gl], hgb) + _nn(dcbb, bg)
            dxbc_ref[:, bsl] = _nt(wst[:, gl], dhnb) + _tn(dcbb, cg)
            dh_scr[g] = dhn * q["cde"][:, gl] + _tn(cg, gfull[:, gl])
        dxg = jnp.concatenate(dxg_parts, axis=1)
        dcs_e = jnp.concatenate(dcs_e_parts, axis=1)
        dxbc_ref[:, 0:SSM_INNER] = dskip_ref[...] * dy + dxg * q["dt_e"]
        dcs = dcs_c - dcs_r.T + _dot_exact(dcs_e, q["expand"], ((1,), (1,)))
        triu = (q["cidx"] >= q["r"]).astype(bf16)
        da = _dot_exact(dcs, triu, ((1,), (0,)), x_is_lhs=False)
        ddt = _dot_exact(dxg * xs, q["expand"], ((1,), (1,))) + da * q["a_neg"]
        ddtp = jnp.where(q["head_lane"], ddt * _sigmoid(q["dtp"]), 0.0)
        ddt_ref[...] = ddtp.astype(bf16)
        dal_p = _rowsum8(da * q["dt"]) * q["a_neg"]
        dbi_p = _rowsum8(ddtp)
        first = (pl.program_id(0) == 0) & (pl.program_id(1) == 0)

        @pl.when(first)
        def _():
            dnw_ref[...] = dnw_p
            dds_ref[...] = dds_p
            dal_ref[...] = dal_p
            dbi_ref[...] = dbi_p

        @pl.when(jnp.logical_not(first))
        def _():
            dnw_ref[...] += dnw_p
            dds_ref[...] += dds_p
            dal_ref[...] += dal_p
            dbi_ref[...] += dbi_p

    def rows(w):
        return pl.BlockSpec((CHUNK, w), lambda b, c: (b * n_chunk + n_chunk - 1 - c, 0))

    def par(w):
        return pl.BlockSpec((1, w), lambda b, c: (0, 0))

    def acc(w):
        return pl.BlockSpec((SUBLANES, w), lambda b, c: (0, 0))

    return pl.pallas_call(
        body, name=name, grid=(t // SEQ, n_chunk),
        in_specs=[rows(CONV_CH), rows(SSM_INNER), rows(LANES), rows(SSM_INNER),
                  pl.BlockSpec((1, SSM_GROUPS, D_STATE, GROUP_W), lambda b, c: (b * n_chunk + n_chunk - 1 - c, 0, 0, 0)),
                  rows(SSM_INNER), par(LANES), par(LANES), par(SSM_INNER), par(SSM_INNER)],
        out_specs=[rows(CONV_CH), rows(SSM_INNER), rows(LANES), acc(SSM_INNER), acc(SSM_INNER), acc(LANES), acc(LANES)],
        out_shape=[jax.ShapeDtypeStruct((t, CONV_CH), f32), jax.ShapeDtypeStruct((t, SSM_INNER), bf16), jax.ShapeDtypeStruct((t, LANES), bf16),
                   jax.ShapeDtypeStruct((SUBLANES, SSM_INNER), f32), jax.ShapeDtypeStruct((SUBLANES, SSM_INNER), f32),
                   jax.ShapeDtypeStruct((SUBLANES, LANES), f32), jax.ShapeDtypeStruct((SUBLANES, LANES), f32)],
        scratch_shapes=[pltpu.VMEM((SSM_GROUPS, D_STATE, GROUP_W), f32)],
        compiler_params=_cparams(("arbitrary", "arbitrary")),
    )(xbc, z, dtp, y, hs, dyn, *params)


def _adamw_update(g, w, m, v):
    mm = ADAM_B1 * m + (1.0 - ADAM_B1) * g
    vv = ADAM_B2 * v + (1.0 - ADAM_B2) * (g * g)
    m_hat = mm / (1.0 - ADAM_B1 ** ADAM_STEP)
    v_hat = vv / (1.0 - ADAM_B2 ** ADAM_STEP)
    return -ADAM_LR * (m_hat / (jnp.sqrt(v_hat) + ADAM_EPS) + ADAM_WD * w), mm, vv


def _adamw(g_parts, w, m, v, name):
    rows, width = w.shape
    n = len(g_parts)
    tr = _row_tile(rows)

    def body(*refs):
        g_refs, (w_ref, m_ref, v_ref, g_out, d_out, m_out, v_out) = refs[:n], refs[n:]
        g = g_refs[0][...].astype(f32)
        for r in g_refs[1:]:
            g = g + r[...].astype(f32)
        g_out[...] = g
        d_out[...], m_out[...], v_out[...] = _adamw_update(g, w_ref[...], m_ref[...], v_ref[...])

    spec = pl.BlockSpec((tr, width), lambda i: (i, 0))
    return pl.pallas_call(
        body, name=name, grid=(rows // tr,), in_specs=[spec] * (n + 3), out_specs=[spec] * 4,
        out_shape=[jax.ShapeDtypeStruct((rows, width), f32)] * 4, compiler_params=_cparams(("parallel",)),
    )(*g_parts, w, m, v)


def _adamw_layers(landed, w, m, v, after, name, layers_on_columns=False):
    depth = len(landed)
    _, rows, width = landed[0].shape
    tr = _row_tile(rows)
    n_i = rows // tr
    at = (lambda ref: ref) if layers_on_columns else (lambda ref: ref.at[0])

    def body(*refs):
        part_refs, (w_ref, m_ref, v_ref, _, g_out, d_out, m_out, v_out) = refs[:depth * N_DEV], refs[depth * N_DEV:]
        for l in range(depth):
            @pl.when(pl.program_id(0) == l)
            def _(l=l):
                g = part_refs[l * N_DEV][0].astype(f32)
                for r in part_refs[l * N_DEV + 1:(l + 1) * N_DEV]:
                    g = g + r[0].astype(f32)
                at(g_out)[...] = g
                at(d_out)[...], at(m_out)[...], at(v_out)[...] = _adamw_update(g, at(w_ref)[...], at(m_ref)[...], at(v_ref)[...])

    def part_spec(l, p):
        return pl.BlockSpec((1, tr, width), lambda ll, i: (p, jnp.where(ll == l, i, jnp.where(ll < l, 0, n_i - 1)), 0))

    state = (pl.BlockSpec((tr, width), lambda ll, i: (i, ll)) if layers_on_columns
             else pl.BlockSpec((1, tr, width), lambda ll, i: (ll, i, 0)))
    return pl.pallas_call(
        body, name=name, grid=(depth, n_i),
        in_specs=[part_spec(l, p) for l in range(depth) for p in range(N_DEV)] + [state] * 3 + [ANY], out_specs=[state] * 4,
        out_shape=[jax.ShapeDtypeStruct(w.shape, f32)] * 4, compiler_params=_cparams(("arbitrary", "arbitrary")),
    )(*[landed[l] for l in range(depth) for _ in range(N_DEV)], w, m, v, after)


def _row_tile(rows, cap=512):
    for cand in range(min(rows, cap) // SUBLANES * SUBLANES, 0, -SUBLANES):
        if rows % cand == 0:
            return cand
    return rows


def _cols_from_devices(g, width, name):
    n_dev, depth, a, b = g.shape

    def body(g_ref, o_ref):
        for i in range(n_dev):
            o_ref[0, :, i * b:(i + 1) * b] = g_ref[i, 0]
        if width > n_dev * b:
            o_ref[0, :, n_dev * b:width] = jnp.zeros((a, width - n_dev * b), o_ref.dtype)

    return pl.pallas_call(
        body, name=name, grid=(depth,), in_specs=[pl.BlockSpec((n_dev, 1, a, b), lambda l: (0, l, 0, 0))],
        out_specs=pl.BlockSpec((1, a, width), lambda l: (l, 0, 0)), out_shape=jax.ShapeDtypeStruct((depth, a, width), g.dtype),
        compiler_params=_cparams(("parallel",)),
    )(g)


def _devices_from_cols(per_layer, b, name, tr=256):
    depth = len(per_layer)
    a, width = per_layer[0].shape

    def body(*refs):
        o_ref = refs[depth]
        for l in range(depth):
            for i in range(N_DEV):
                o_ref[i, l] = refs[l][:, i * b:(i + 1) * b]

    return pl.pallas_call(
        body, name=name, grid=(a // tr,), in_specs=[pl.BlockSpec((tr, width), lambda r: (r, 0))] * depth,
        out_specs=pl.BlockSpec((N_DEV, depth, tr, b), lambda r: (0, 0, r, 0)),
        out_shape=jax.ShapeDtypeStruct((N_DEV, depth, a, b), per_layer[0].dtype), compiler_params=_cparams(("parallel",)),
    )(*per_layer)


def _me():
    return lax.axis_index("x"), lax.axis_index("y"), lax.axis_index("c")


def _allgather_two_level(shards, name):
    n = len(shards)
    per = 7

    def body(*refs):
        ins, outs, token = refs[:n], refs[n:2 * n], refs[2 * n]
        send_sems, recv_sems, local_sems = refs[2 * n + 1:]
        token[...] = jnp.zeros_like(token)
        x, y, c = _me()
        me, sibling = (x, y, c), (x, y, 1 - c)
        chips = [(1 - x, y), (x, 1 - y), (1 - x, 1 - y)]

        def slot(a, p):
            return outs[a].at[4 * p[0] + 2 * p[1] + p[2]]

        def copy(a, k, block, to, src=None):
            return pltpu.make_async_remote_copy(
                src_ref=slot(a, block) if src is None else src, dst_ref=slot(a, block),
                send_sem=send_sems.at[a * per + k], recv_sem=recv_sems.at[a * per + k], device_id=to, device_id_type=MESH)

        mine = [pltpu.make_async_copy(ins[a], slot(a, me), local_sems.at[a]) for a in range(n)]
        for cp in mine:
            cp.start()
        first = []
        for a in range(n):
            first.append(copy(a, 0, me, sibling, src=ins[a]))
            first += [copy(a, 1 + j, me, (*chip, c), src=ins[a]) for j, chip in enumerate(chips)]
        for cp in first:
            cp.start()
        passed = []
        for j, chip in enumerate(chips):
            for a in range(n):
                copy(a, 1 + j, (*chip, c), me).wait_recv()
                fwd = copy(a, 4 + j, (*chip, c), sibling)
                fwd.start()
                passed.append(fwd)
        for a in range(n):
            copy(a, 0, sibling, me).wait_recv()
            for j, chip in enumerate(chips):
                copy(a, 4 + j, (*chip, 1 - c), me).wait_recv()
        for cp in first + passed:
            cp.wait_send()
        for cp in mine:
            cp.wait()

    outs = pl.pallas_call(
        body, name=name, in_specs=[ANY] * n, out_specs=[ANY] * n + [pl.BlockSpec(memory_space=pltpu.VMEM)],
        out_shape=[jax.ShapeDtypeStruct((N_DEV,) + s.shape, s.dtype) for s in shards] + [jax.ShapeDtypeStruct((SUBLANES, LANES), f32)],
        scratch_shapes=[pltpu.SemaphoreType.DMA((n * per,)), pltpu.SemaphoreType.DMA((n * per,)), pltpu.SemaphoreType.DMA((n,))],
    )(*shards)
    return outs[:n], outs[n]


def _allgather_direct(row, name):
    def body(in_ref, out_ref, send_sems, recv_sems, local_sem):
        x, y, c = _me()
        mine = out_ref.at[4 * x + 2 * y + c]
        local = pltpu.make_async_copy(in_ref, mine, local_sem)
        local.start()
        sends = []
        for k in range(1, N_DEV):
            px, py, pc = x ^ (k >> 2), y ^ ((k >> 1) & 1), c ^ (k & 1)
            sends.append(pltpu.make_async_remote_copy(
                src_ref=in_ref, dst_ref=mine, send_sem=send_sems.at[k - 1], recv_sem=recv_sems.at[k - 1],
                device_id=(px, py, pc), device_id_type=MESH))
        for cp in sends:
            cp.start()
        for k in range(1, N_DEV):
            px, py, pc = x ^ (k >> 2), y ^ ((k >> 1) & 1), c ^ (k & 1)
            theirs = out_ref.at[4 * px + 2 * py + pc]
            pltpu.make_async_remote_copy(
                src_ref=in_ref, dst_ref=theirs, send_sem=send_sems.at[k - 1], recv_sem=recv_sems.at[k - 1],
                device_id=(px, py, pc), device_id_type=MESH).wait_recv()
        for cp in sends:
            cp.wait_send()
        local.wait()

    return pl.pallas_call(
        body, name=name, in_specs=[ANY], out_specs=ANY, out_shape=jax.ShapeDtypeStruct((N_DEV,) + row.shape, row.dtype),
        scratch_shapes=[pltpu.SemaphoreType.DMA((N_DEV - 1,)), pltpu.SemaphoreType.DMA((N_DEV - 1,)), pltpu.SemaphoreType.DMA],
    )(row)


N_CHIP = N_DEV // 2
HBM = pl.BlockSpec(memory_space=pltpu.HBM)
SEM = pl.BlockSpec(memory_space=pltpu.SEMAPHORE)
EFFECT = pltpu.SideEffectType.DATAFLOW_SIDE_EFFECTING


def _peer(k):
    x, y, c = _me()
    return x ^ (k >> 2), y ^ ((k >> 1) & 1), c ^ (k & 1)


def _direct_copies(srcs, lands, send_sems, recv_sems, per_peer):
    x, y, c = _me()
    me = 4 * x + 2 * y + c
    copies = []
    for a in range(len(srcs)):
        for k in range(1, N_DEV):
            px, py, pc = _peer(k)
            piece = srcs[a].at[4 * px + 2 * py + pc] if per_peer else srcs[a]
            copies.append(pltpu.make_async_remote_copy(
                src_ref=piece, dst_ref=lands[a].at[me], send_sem=send_sems.at[a * (N_DEV - 1) + k - 1],
                recv_sem=recv_sems.at[a * (N_DEV - 1) + k - 1], device_id=(px, py, pc), device_id_type=MESH))
    return copies


def _direct_start(srcs, lands, per_peer, name):
    n = len(srcs)
    n_sem = n * (N_DEV - 1)

    def body(*refs):
        src_refs, land_refs = refs[:n], refs[n:2 * n]
        send_sems, recv_sems = refs[2 * n], refs[2 * n + 1]
        token = refs[-1]
        for cp in _direct_copies(src_refs, land_refs, send_sems, recv_sems, per_peer):
            cp.start()
        token[...] = jnp.zeros_like(token)

    outs = pl.pallas_call(
        body, name=name,
        out_shape=(pltpu.SemaphoreType.DMA((n_sem,)), pltpu.SemaphoreType.DMA((n_sem,)),
                   *[pltpu.HBM(s.shape, s.dtype) for s in srcs], *[pltpu.HBM(s.shape, s.dtype) for s in lands],
                   jax.ShapeDtypeStruct((SUBLANES, LANES), f32)),
        in_specs=[HBM] * (2 * n), out_specs=(SEM, SEM, *[HBM] * (2 * n), pl.BlockSpec(memory_space=pltpu.VMEM)),
        input_output_aliases={i: 2 + i for i in range(2 * n)},
        compiler_params=pltpu.CompilerParams(has_side_effects=EFFECT),
    )(*[pltpu.with_memory_space_constraint(s, pltpu.HBM) for s in srcs], *[pltpu.with_memory_space_constraint(s, pltpu.HBM) for s in lands])
    return outs[0], outs[1], outs[2:2 + n], outs[2 + n:2 + 2 * n], outs[-1]


def _direct_wait(send_sems, recv_sems, srcs, lands, after, per_peer, name):
    n = len(srcs)

    def body(*refs):
        src_refs, land_refs = refs[:n], refs[n:2 * n]
        s_sems, r_sems = refs[2 * n], refs[2 * n + 1]
        for cp in _direct_copies(src_refs, land_refs, s_sems, r_sems, per_peer):
            cp.wait_send()
            cp.wait_recv()

    outs = pl.pallas_call(
        body, name=name,
        out_shape=tuple(pltpu.HBM(s.shape, s.dtype) for s in list(srcs) + list(lands)),
        in_specs=[HBM] * (2 * n) + [SEM, SEM, ANY], out_specs=tuple([HBM] * (2 * n)),
        input_output_aliases={i: i for i in range(2 * n)},
        compiler_params=pltpu.CompilerParams(has_side_effects=EFFECT),
    )(*srcs, *lands, send_sems, recv_sems, after)
    return outs[n:]


def _row(v, width=None):
    v = v.reshape(1, -1).astype(f32)
    if width is not None and v.shape[1] < width:
        v = jnp.pad(v, ((0, 0), (0, width - v.shape[1])))
    return v


def _layer_params(p, l):
    return dict(
        norm_mix=_row(p["norm_mix"][l]), norm_ffn=_row(p["norm_ffn"][l]), conv_w=p["conv_w"][l], conv_b=_row(p["conv_b"][l]),
        ssd=(_row(p["dt_bias"][l], LANES), _row(p["a_log"][l], LANES), _row(jnp.repeat(p["d_skip"][l], HEAD_DIM)), _row(p["ssm_norm"][l])))


def _layer_fwd(h, w_in, rest, sp, tabs, l):
    tag = f"l{l}_"
    hn = _rmsnorm_fwd(h, sp["norm_mix"], tag + "norm_mix")
    qkv, z, xbc_pre = _in_proj(hn, w_in, (QKV_WIDTH, SSM_INNER, CONV_CH), tag + "proj")
    dtp = _matmul(hn, w_in, mode="nn", n_out=LANES, tn=LANES, b_off=DT_OFF // LANES, name=tag + "proj_dt")
    prep = _attn_prep(qkv, tabs, tag + "attn_prep")
    o, lse = _attn_fwd(prep, tag + "attn_fwd")
    xbc = _conv_fwd(xbc_pre, sp["conv_w"], sp["conv_b"], tag + "conv_fwd")
    yn, y, hs = _ssd_fwd(xbc, z, dtp, sp["ssd"], tag + "ssd_fwd")
    w_out, w_gate, w_up, w_down = rest(yn) if callable(rest) else rest
    h2 = _out_proj(o, yn, w_out, h, tag + "out_proj")
    hn2 = _rmsnorm_fwd(h2, sp["norm_ffn"], tag + "norm_ffn")
    g, u, act = _swiglu_fwd(hn2, w_gate, w_up, tag + "ffn_up")
    h3 = _matmul(act, w_down, mode="nn", tk=1408, add=h2, name=tag + "ffn_down")
    saved = dict(h=h, hn=hn, prep=prep, z=z, xbc_pre=xbc_pre, dtp=dtp, o=o, lse=lse, xbc=xbc, yn=yn, y=y, hs=hs, h2=h2, hn2=hn2, g=g, u=u, act=act,
                 rest=(w_out, w_gate, w_up, w_down))
    return h3, saved


def _layer_bwd(dh3_pair, s, big, sp, tabs, l, gd=f32, after_ffn=None):
    tag = f"l{l}_"
    dh3, dh3b = dh3_pair
    w_in, w_out, w_gate, w_up, w_down = big
    dg, du = _swiglu_bwd(dh3b, w_down, s["g"], s["u"], tag + "ffn_down_bwd")
    dw_down = _matmul(s["act"], dh3b, mode="tn", tm=1408, tn=512, tk=2048, out_dtype=gd, name=tag + "dw_down")
    dw_gate = _matmul(dg, s["hn2"], mode="tn", tm=1408, tn=512, tk=2048, out_dtype=gd, name=tag + "dw_gate")
    dw_up = _matmul(du, s["hn2"], mode="tn", tm=1408, tn=512, tk=2048, out_dtype=gd, name=tag + "dw_up")
    norm_ffn = sp["norm_ffn"] if after_ffn is None else sp["norm_ffn"] + after_ffn(dict(w_gate=dw_gate, w_up=dw_up, w_down=dw_down))
    dh2, dh2b, dnf = _nt_norm_bwd([(dg, w_gate), (du, w_up)], s["h2"], norm_ffn, dh3, tag + "ffn_up_bwd_norm", tk=1408, b_is_kd=True,
                                  vmem=VMEM_LIMIT_TWO_PAIRS)
    d_o = _matmul(dh2b, w_out, mode="nt", n_out=ATTN_WIDTH, tn=512, b_off=0, name=tag + "out_attn_bwd")
    dyn = _matmul(dh2b, w_out, mode="nt", n_out=SSM_INNER, tn=512, b_off=1, name=tag + "out_ssm_bwd")
    dw_out = jnp.concatenate([_matmul(s["o"], dh2b, mode="tn", tm=512, tn=512, tk=2048, out_dtype=gd, name=tag + "dw_out_attn"),
                              _matmul(s["yn"], dh2b, mode="tn", tm=512, tn=512, tk=2048, out_dtype=gd, name=tag + "dw_out_ssm")], axis=0)
    dxbc, dz, ddtp, dnw, dds, dal, dbi = _ssd_bwd(s["xbc"], s["z"], s["dtp"], s["y"], s["hs"], dyn, sp["ssd"], tag + "ssd_bwd")
    dxbc_pre, dconv_w, dconv_b = _conv_bwd(s["xbc_pre"], sp["conv_w"], sp["conv_b"], dxbc, tag + "conv_bwd")
    dq, dk, dv = _attn_bwd(s["prep"], tabs, s["o"], s["lse"], d_o, tag + "attn_bwd")
    dproj = jnp.concatenate([dq, dk, dv, dz, dxbc_pre, ddtp], axis=1)
    dw_in = _matmul(s["hn"], dproj, mode="tn", tm=512, tn=1152, tk=2048, out_dtype=gd, name=tag + "dw_in")
    res = _nt_norm_bwd([(dproj, w_in)], s["h"], sp["norm_mix"], dh2, tag + "proj_bwd_norm", tk=1152, bf16_copy=l > 0)
    dh, dhb, dnm = res if l > 0 else (res[0], None, res[1])
    grads = dict(
        norm_mix=dnm.sum(0), w_in=dw_in, conv_w=dconv_w, conv_b=dconv_b[0], dt_bias=dbi.sum(0)[:SSM_HEADS], a_log=dal.sum(0)[:SSM_HEADS],
        d_skip=dds.sum(0).reshape(SSM_HEADS, HEAD_DIM).sum(1), ssm_norm=dnw.sum(0), w_out=dw_out, norm_ffn=dnf.sum(0),
        w_gate=dw_gate, w_up=dw_up, w_down=dw_down)
    return (dh, dhb), grads


def _local_step(x, positions, target, p, bigs):
    tabs = _rope_tables(positions.reshape(-1, 1), "rope_tables")
    h = x
    saved, sps = [], []
    for l in range(DEPTH):
        sps.append(_layer_params(p, l))
        h, s = _layer_fwd(h, bigs[l][0], bigs[l][1:], sps[l], tabs, l)
        saved.append(s)
    dh, dhb, loss_parts, dfn = _final_loss(h, _row(p["final_norm"]), target, "final_loss")
    dh = (dh, dhb)
    layer_grads = [None] * DEPTH
    for l in reversed(range(DEPTH)):
        dh, layer_grads[l] = _layer_bwd(dh, saved[l], bigs[l], sps[l], tabs, l)
    grads = {k: [layer_grads[l][k] for l in range(DEPTH)] for k in layer_grads[0]}
    grads["final_norm"] = dfn.sum(0)
    return jnp.sum(loss_parts), dh[0], grads


BIG = ("w_in", "w_out", "w_gate", "w_up", "w_down")
REST = BIG[1:]
FFN = ("w_gate", "w_up", "w_down")
MIX = ("w_in", "w_out")
COL_SHARDED = ("w_in",)
TRANSPOSED = ("w_gate", "w_up")
SMALL = ("norm_mix", "conv_b", "dt_bias", "a_log", "d_skip", "ssm_norm", "norm_ffn", "final_norm")
WEIGHTS = ("norm_mix", "w_in", "conv_w", "conv_b", "dt_bias", "a_log", "d_skip", "ssm_norm", "w_out", "norm_ffn", "w_gate", "w_up", "w_down", "final_norm")
SMALL_ROWS = 88
CONVW_ROWS = 96
CONVW_SHARD_ROWS = 16


def _full_from_gathered(name, g, l):
    _, a, b = g.shape
    if name in COL_SHARDED:
        width = IN_PROJ_PAD if name == "w_in" else N_DEV * b
        return _cols_from_devices(g.reshape(N_DEV, 1, a, b), width, f"cols_l{l}_{name}").reshape(a, width)
    return g.reshape(N_DEV * a, b)


def _by_device(name, full, shard_shape, l):
    a, b = shard_shape
    if name in COL_SHARDED:
        return _devices_from_cols([full], b, f"devs_l{l}_{name}").reshape(N_CHIP, 2, a, b)
    return full.reshape(N_CHIP, 2, a, b)


def _pack_rows(parts, rows, width):
    flat = jnp.concatenate([q.reshape(-1) for q in parts])
    return jnp.pad(flat, (0, rows * width - flat.shape[0])).reshape(rows, width)


def _unpack(flat, like):
    out, off = [], 0
    for q in like:
        out.append(flat[off:off + q.size].reshape(q.shape))
        off += q.size
    return out


def kernel(x, positions, norm_mix, w_in, conv_w, conv_b, dt_bias, a_log, d_skip, ssm_norm, w_out, norm_ffn, w_gate, w_up, w_down, final_norm, loss_target, m_norm_mix, m_w_in, m_conv_w, m_conv_b, m_dt_bias, m_a_log, m_d_skip, m_ssm_norm, m_w_out, m_norm_ffn, m_w_gate, m_w_up, m_w_down, m_final_norm, v_norm_mix, v_w_in, v_conv_w, v_conv_b, v_dt_bias, v_a_log, v_d_skip, v_ssm_norm, v_w_out, v_norm_ffn, v_w_gate, v_w_up, v_w_down, v_final_norm):
    w = dict(norm_mix=norm_mix, w_in=w_in, conv_w=conv_w, conv_b=conv_b, dt_bias=dt_bias, a_log=a_log, d_skip=d_skip, ssm_norm=ssm_norm,
             w_out=w_out, norm_ffn=norm_ffn, w_gate=w_gate, w_up=w_up, w_down=w_down, final_norm=final_norm)
    m = dict(norm_mix=m_norm_mix, w_in=m_w_in, conv_w=m_conv_w, conv_b=m_conv_b, dt_bias=m_dt_bias, a_log=m_a_log, d_skip=m_d_skip,
             ssm_norm=m_ssm_norm, w_out=m_w_out, norm_ffn=m_norm_ffn, w_gate=m_w_gate, w_up=m_w_up, w_down=m_w_down, final_norm=m_final_norm)
    v = dict(norm_mix=v_norm_mix, w_in=v_w_in, conv_w=v_conv_w, conv_b=v_conv_b, dt_bias=v_dt_bias, a_log=v_a_log, d_skip=v_d_skip,
             ssm_norm=v_ssm_norm, w_out=v_w_out, norm_ffn=v_norm_ffn, w_gate=v_w_gate, w_up=v_w_up, w_down=v_w_down, final_norm=v_final_norm)
    ax, ay, ac = lax.axis_index("x"), lax.axis_index("y"), lax.axis_index("c")
    dev = 4 * ax + 2 * ay + ac

    assert DEPTH == 2
    t = x.shape[0] * x.shape[1]
    xf, target = x.reshape(t, D_MODEL), loss_target.reshape(t, D_MODEL)

    def own_slot(block):
        return lax.dynamic_update_slice(lax.empty((N_DEV,) + block.shape[1:], block.dtype), block, (dev,) + (0,) * (block.ndim - 1))

    def layer_shard(arr, k, l):
        return jnp.transpose(arr, (2, 0, 1))[:, l, :] if k in TRANSPOSED else arr[l]

    def gather_start(keys, l, tie, name):
        shards = [(layer_shard(w[keys[0]], keys[0], l) + tie).astype(bf16)] + [layer_shard(w[k], k, l).astype(bf16) for k in keys[1:]]
        return _direct_start(shards, [own_slot(s[None]) for s in shards], False, name)

    def scatter_start(keys, grads_l, l, name):
        shapes = [(w[k].shape[2], w[k].shape[1]) if k in TRANSPOSED else w[k].shape[1:] for k in keys]
        by_dev = [_by_device(k, grads_l[k], sh, l).reshape((N_DEV,) + sh) for k, sh in zip(keys, shapes)]
        return _direct_start(by_dev, [own_slot(lax.dynamic_slice_in_dim(g, dev, 1, 0)) for g in by_dev], True, name)

    (g_in0, conv_all), tie = _allgather_two_level([w["w_in"][0].astype(bf16), w["conv_w"]], "gather_l0_w_in")
    rest0_copy = gather_start(REST, 0, tie[0, 0], "gather_l0_rest_start")
    l1_copy = gather_start(BIG, 1, rest0_copy[4][0, 0], "gather_l1_start")
    p = {k: w[k] for k in SMALL}
    p["norm_mix"] = p["norm_mix"] + l1_copy[4][0, 0]
    p["conv_w"] = jnp.transpose(conv_all, (1, 2, 0, 3)).reshape(DEPTH, CONV_WIDTH, CONV_CH)
    sp0, sp1 = _layer_params(p, 0), _layer_params(p, 1)

    def rest0(after):
        lands = _direct_wait(*rest0_copy[:4], after, False, "gather_l0_rest_wait")
        return tuple(_full_from_gathered(k, g, 0) for k, g in zip(REST, lands))

    tabs = _rope_tables(positions.reshape(t, 1), "rope_tables")
    w_in0 = _full_from_gathered("w_in", g_in0, 0)
    h1, saved0 = _layer_fwd(xf, w_in0, rest0, sp0, tabs, 0)
    lands1 = _direct_wait(*l1_copy[:4], h1, False, "gather_l1_wait")
    bigs1 = tuple(_full_from_gathered(k, g, 1) for k, g in zip(BIG, lands1))
    h2, saved1 = _layer_fwd(h1, bigs1[0], bigs1[1:], sp1, tabs, 1)
    dh, dhb, loss_parts, dfn = _final_loss(h2, _row(p["final_norm"]), target, "final_loss")
    loss_local = jnp.sum(loss_parts)

    dh, grads1 = _layer_bwd((dh, dhb), saved1, bigs1, sp1, tabs, 1, gd=bf16)
    l1_grads = scatter_start(BIG, grads1, 1, "scatter_l1_start")
    w_out0, w_gate0, w_up0, w_down0 = saved0["rest"]
    bigs0 = (w_in0, w_out0, w_gate0, w_up0, w_down0 + l1_grads[4][0, 0].astype(bf16))
    ffn0_grads = []

    def after_ffn(grads_ffn):
        ffn0_grads.append(scatter_start(FFN, grads_ffn, 0, "scatter_l0_ffn_start"))
        return ffn0_grads[0][4][0, 0]

    (dx, _), grads0 = _layer_bwd(dh, saved0, bigs0, sp0, tabs, 0, gd=bf16, after_ffn=after_ffn)
    mix0_grads = scatter_start(MIX, grads0, 0, "scatter_l0_mix_start")
    landed = {(k, 1): g for k, g in zip(BIG, _direct_wait(*l1_grads[:4], dx, True, "scatter_l1_wait"))}
    landed.update({(k, 0): g for k, g in zip(FFN, _direct_wait(*ffn0_grads[0][:4], dx, True, "scatter_l0_ffn_wait"))})
    out_g, out_d, out_m, out_v = {}, {}, {}, {}

    def update(keys, after):
        for k in keys:
            parts = [landed[k, l] for l in range(DEPTH)]
            if k in TRANSPOSED:
                depth, a, b = w[k].shape
                state = [jnp.transpose(s, (2, 0, 1)).reshape(b, depth * a) for s in (w[k], m[k], v[k])]
                res = _adamw_layers(parts, *state, after, "adamw_" + k, layers_on_columns=True)
                res = [jnp.transpose(r.reshape(b, depth, a), (1, 2, 0)) for r in res]
            else:
                res = _adamw_layers(parts, w[k], m[k], v[k], after, "adamw_" + k)
            for dst, r in zip((out_g, out_d, out_m, out_v), res):
                dst[k] = r

    update(FFN, mix0_grads[4])
    grads = {k: [grads0[k], grads1[k]] for k in grads0 if k not in BIG}
    grads["final_norm"] = dfn.sum(0) + mix0_grads[4][0, 0]

    small_like = [w[k] for k in SMALL]
    small_grads = [jnp.stack(grads[k]) if k != "final_norm" else grads[k] for k in SMALL]
    small_pack = jnp.concatenate([_pack_rows(small_grads, SMALL_ROWS, LANES), _pack_rows([jnp.stack(grads["conv_w"])], CONVW_ROWS, LANES)], axis=0)
    parts = _allgather_direct(small_pack, "gather_small_grads")
    g_s, d_s, m_s, v_s = _adamw(
        [parts[i, :SMALL_ROWS] for i in range(N_DEV)], _pack_rows(small_like, SMALL_ROWS, LANES),
        _pack_rows([m[k] for k in SMALL], SMALL_ROWS, LANES), _pack_rows([v[k] for k in SMALL], SMALL_ROWS, LANES), "adamw_replicated")
    for dst, src in ((out_g, g_s), (out_d, d_s), (out_m, m_s), (out_v, v_s)):
        dst.update(zip(SMALL, _unpack(src.reshape(-1), small_like)))
    shard_w = conv_w.shape[-1]
    conv_parts = parts[:, SMALL_ROWS:].reshape(N_DEV, DEPTH, CONV_WIDTH, CONV_CH)
    conv_mine = lax.dynamic_slice_in_dim(conv_parts, dev * shard_w, shard_w, axis=3)
    g_c, d_c, m_c, v_c = _adamw(
        [_pack_rows([conv_mine[i]], CONVW_SHARD_ROWS, LANES) for i in range(N_DEV)], _pack_rows([conv_w], CONVW_SHARD_ROWS, LANES),
        _pack_rows([m["conv_w"]], CONVW_SHARD_ROWS, LANES), _pack_rows([v["conv_w"]], CONVW_SHARD_ROWS, LANES), "adamw_conv_w")
    for dst, src in ((out_g, g_c), (out_d, d_c), (out_m, m_c), (out_v, v_c)):
        dst["conv_w"] = src.reshape(-1)[:conv_w.size].reshape(conv_w.shape)

    landed.update({(k, 0): g for k, g in zip(MIX, _direct_wait(*mix0_grads[:4], v_c + out_v["w_down"][0, :CONVW_SHARD_ROWS, :LANES], True, "scatter_l0_mix_wait"))})
    update(MIX, v_c)

    loss = lax.psum(loss_local, ("x", "y", "c"))
    return (loss, dx.reshape(x.shape), *[out_g[k] for k in WEIGHTS], *[out_d[k] for k in WEIGHTS],
            *[out_m[k] for k in WEIGHTS], *[out_v[k] for k in WEIGHTS])
```

```python
import jax
import jax.numpy as jnp
import numpy as np
from jax import lax
from jax.experimental import pallas as pl
from jax.experimental.pallas import tpu as pltpu

f32 = jnp.float32
bf16 = jnp.bfloat16

D_MODEL = 1024
SEQ = 2048
DEPTH = 2
HEAD_DIM = 64
N_ATTN_HEADS = 8
N_KV_HEADS = 2
ATTN_WIDTH = 512
KV_WIDTH = 128
ROPE_DIM = 16
ROPE_THETA = 500000.0
DILATIONS = (1, 4, 16)
ATTN_BLOCK = 128
SSM_HEADS = 16
SSM_INNER = 1024
SSM_GROUPS = 2
D_STATE = 128
CONV_WIDTH = 4
CHUNK = 128
CONV_CH = 1536
MIX_WIDTH = 1536
QKV_WIDTH = ATTN_WIDTH + 2 * KV_WIDTH
DT_OFF = 3328
IN_PROJ = 3344
IN_PROJ_PAD = 3456
FFN_HIDDEN = 2816
EPS = 1e-5
N_DEV = 8
ADAM_LR = 0.001
ADAM_B1 = 0.9
ADAM_B2 = 0.999
ADAM_EPS = 1e-08
ADAM_WD = 0.01
ADAM_STEP = 10

LANES = 128
SUBLANES = 8
VMEM_LIMIT = 56 * 1024 * 1024
VMEM_LIMIT_TWO_PAIRS = 60 * 1024 * 1024

MESH = pl.DeviceIdType.MESH
ANY = pl.BlockSpec(memory_space=pl.ANY)


def _cparams(sem, vmem=None):
    return pltpu.CompilerParams(dimension_semantics=sem, vmem_limit_bytes=vmem or VMEM_LIMIT)


def _sigmoid(x):
    return 1.0 / (1.0 + jnp.exp(-x))


def _silu(x):
    return x * _sigmoid(x)


def _dsilu(x):
    s = _sigmoid(x)
    return s * (1.0 + x * (1.0 - s))


def _silu_and_grad(x):
    s = _sigmoid(x)
    return x * s, s * (1.0 + x * (1.0 - s))


def _softplus(x):
    return jnp.maximum(x, 0.0) + jnp.log(1.0 + jnp.exp(-jnp.abs(x)))


def _dot(a, b, dims, precision=None):
    return lax.dot_general(a, b, (dims, ((), ())), preferred_element_type=f32, precision=precision)


def _nn(a, b, precision=None):
    return _dot(a, b, ((1,), (0,)), precision)


def _nt(a, b):
    return _dot(a, b, ((1,), (1,)))


def _tn(a, b):
    return _dot(a, b, ((0,), (0,)))


def _rowsum8(t):
    n, w = t.shape
    return jnp.sum(t.reshape(n // SUBLANES, SUBLANES, w), axis=0)


def _matmul(a, b, *, mode, n_out=None, b_off=0, add=None, out_dtype=f32, tm=2048, tn=512, tk=1024, name):
    if mode == "tn":
        kk, m = a.shape
    else:
        m, kk = a.shape
    n = n_out if n_out is not None else (b.shape[0] if mode == "nt" else b.shape[1])
    tm, tn, tk = min(tm, m), min(tn, n), min(tk, kk)
    assert m % tm == 0 and n % tn == 0 and kk % tk == 0, (name, m, n, kk, tm, tn, tk)
    nk = kk // tk
    if mode == "nn":
        a_spec = pl.BlockSpec((tm, tk), lambda i, j, k: (i, k))
        b_spec = pl.BlockSpec((tk, tn), lambda i, j, k: (k, j + b_off))
        dims = ((1,), (0,))
    elif mode == "nt":
        a_spec = pl.BlockSpec((tm, tk), lambda i, j, k: (i, k))
        b_spec = pl.BlockSpec((tn, tk), lambda i, j, k: (j + b_off, k))
        dims = ((1,), (1,))
    else:
        a_spec = pl.BlockSpec((tk, tm), lambda i, j, k: (k, i))
        b_spec = pl.BlockSpec((tk, tn), lambda i, j, k: (k, j + b_off))
        dims = ((0,), (0,))
    o_spec = pl.BlockSpec((tm, tn), lambda i, j, k: (i, j))
    has_add = add is not None

    def body(*refs):
        if has_add:
            a_ref, b_ref, add_ref, o_ref, acc_ref = refs
        else:
            a_ref, b_ref, o_ref, acc_ref = refs
        k = pl.program_id(2)
        part = _dot(a_ref[...].astype(bf16), b_ref[...].astype(bf16), dims)

        @pl.when(k == 0)
        def _():
            acc_ref[...] = part

        @pl.when(k > 0)
        def _():
            acc_ref[...] += part

        @pl.when(k == nk - 1)
        def _():
            r = acc_ref[...]
            if has_add:
                r = r + add_ref[...]
            o_ref[...] = r.astype(out_dtype)

    in_specs = [a_spec, b_spec] + ([o_spec] if has_add else [])
    args = (a, b) + ((add,) if has_add else ())
    return pl.pallas_call(
        body, name=name, grid=(m // tm, n // tn, nk), in_specs=in_specs, out_specs=o_spec,
        out_shape=jax.ShapeDtypeStruct((m, n), out_dtype), scratch_shapes=[pltpu.VMEM((tm, tn), f32)],
        compiler_params=_cparams(("parallel", "parallel", "arbitrary")),
    )(*args)


def _in_proj(hn, w_in, widths, name, tm=2048, tn=256):
    m, k = hn.shape
    starts = [sum(widths[:i]) // tn for i in range(len(widths))]
    counts = [wd // tn for wd in widths]
    assert m % tm == 0 and all(wd % tn == 0 for wd in widths)
    n_out = len(widths)

    def body(a_ref, w_ref, *o_refs):
        j = pl.program_id(1)
        acc = _nn(a_ref[...], w_ref[...])
        for s, c, o_ref in zip(starts, counts, o_refs):
            @pl.when((j >= s) & (j < s + c))
            def _(o_ref=o_ref):
                o_ref[...] = acc

    def o_spec(s, c):
        return pl.BlockSpec((tm, tn), lambda i, j: (i, jnp.clip(j - s, 0, c - 1)))

    return pl.pallas_call(
        body, name=name, grid=(m // tm, sum(counts)),
        in_specs=[pl.BlockSpec((tm, k), lambda i, j: (i, 0)), pl.BlockSpec((k, tn), lambda i, j: (0, j))],
        out_specs=[o_spec(s, c) for s, c in zip(starts, counts)],
        out_shape=[jax.ShapeDtypeStruct((m, wd), f32) for wd in widths], compiler_params=_cparams(("parallel", "arbitrary")),
    )(hn, w_in)


def _out_proj(o, yn, w_out, h, name, tm=2048, tn=512):
    m, kb = o.shape
    n = w_out.shape[1]
    n_y = yn.shape[1] // kb
    assert yn.shape[1] % kb == 0 and w_out.shape[0] == kb * (1 + n_y) and m % tm == 0 and n % tn == 0

    def body(*refs):
        o_ref, y_refs, w_refs, h_ref, out_ref = refs[0], refs[1:1 + n_y], refs[1 + n_y:2 + 2 * n_y], refs[-2], refs[-1]
        acc = h_ref[...] + _nn(o_ref[...].astype(bf16), w_refs[0][...])
        for y_ref, w_ref in zip(y_refs, w_refs[1:]):
            acc = acc + _nn(y_ref[...], w_ref[...])
        out_ref[...] = acc

    res = pl.BlockSpec((tm, tn), lambda i, j: (i, j))

    def a_blk(c):
        return pl.BlockSpec((tm, kb), lambda i, j: (i, c))

    def w_blk(r):
        return pl.BlockSpec((kb, tn), lambda i, j: (r, j))

    return pl.pallas_call(
        body, name=name, grid=(m // tm, n // tn),
        in_specs=[a_blk(0)] + [a_blk(c) for c in range(n_y)] + [w_blk(r) for r in range(1 + n_y)] + [res],
        out_specs=res, out_shape=jax.ShapeDtypeStruct((m, n), f32), compiler_params=_cparams(("parallel", "parallel")),
    )(o, *[yn] * n_y, *[w_out] * (1 + n_y), h)


def _swiglu_fwd(hn, w_gate, w_up, name, tm=2048, tn=256):
    m, k = hn.shape
    n = w_gate.shape[0]
    assert m % tm == 0 and n % tn == 0, (name, m, n, tm, tn)

    def body(a_ref, wg_ref, wu_ref, g_ref, u_ref, act_ref):
        a = a_ref[...]
        g = _nt(a, wg_ref[...])
        u = _nt(a, wu_ref[...])
        sg, dsg = _silu_and_grad(g)
        g_ref[...] = (u * dsg).astype(bf16)
        u_ref[...] = sg.astype(bf16)
        act_ref[...] = (sg * u).astype(bf16)

    a_spec = pl.BlockSpec((tm, k), lambda i, j: (i, 0))
    w_spec = pl.BlockSpec((tn, k), lambda i, j: (j, 0))
    o_spec = pl.BlockSpec((tm, tn), lambda i, j: (i, j))
    return pl.pallas_call(
        body, name=name, grid=(m // tm, n // tn), in_specs=[a_spec, w_spec, w_spec], out_specs=[o_spec, o_spec, o_spec],
        out_shape=[jax.ShapeDtypeStruct((m, n), bf16)] * 3,
        compiler_params=_cparams(("parallel", "parallel")),
    )(hn, w_gate, w_up)


def _swiglu_bwd(dh, w_down, g, u, name, tm=2048, tn=256):
    m, k = dh.shape
    n = w_down.shape[0]
    assert m % tm == 0 and n % tn == 0, (name, m, n, tm, tn)

    def body(a_ref, w_ref, g_ref, u_ref, dg_ref, du_ref):
        dact = _nt(a_ref[...].astype(bf16), w_ref[...])
        dg_ref[...] = (dact * g_ref[...].astype(f32)).astype(bf16)
        du_ref[...] = (dact * u_ref[...].astype(f32)).astype(bf16)

    a_spec = pl.BlockSpec((tm, k), lambda i, j: (i, 0))
    w_spec = pl.BlockSpec((tn, k), lambda i, j: (j, 0))
    o_spec = pl.BlockSpec((tm, tn), lambda i, j: (i, j))
    return pl.pallas_call(
        body, name=name, grid=(m // tm, n // tn), in_specs=[a_spec, w_spec, o_spec, o_spec], out_specs=[o_spec, o_spec],
        out_shape=[jax.ShapeDtypeStruct((m, n), bf16), jax.ShapeDtypeStruct((m, n), bf16)],
        compiler_params=_cparams(("parallel", "parallel")),
    )(dh, w_down, g, u)


def _rmsnorm_fwd(h, w, name, tm=512):
    m, d = h.shape

    def body(h_ref, w_ref, o_ref):
        x = h_ref[...]
        r = lax.rsqrt(jnp.mean(x * x, axis=-1, keepdims=True) + EPS)
        o_ref[...] = (x * r * w_ref[...]).astype(bf16)

    return pl.pallas_call(
        body, name=name, grid=(m // tm,),
        in_specs=[pl.BlockSpec((tm, d), lambda i: (i, 0)), pl.BlockSpec((1, d), lambda i: (0, 0))],
        out_specs=pl.BlockSpec((tm, d), lambda i: (i, 0)), out_shape=jax.ShapeDtypeStruct((m, d), bf16),
        compiler_params=_cparams(("parallel",)),
    )(h, w)


def _nt_norm_bwd(pairs, h, w, dres, name, tk, b_is_kd=False, bf16_copy=True, tm=1024, vmem=None):
    m, d = h.shape
    contract = _nn if b_is_kd else _nt
    steps = [p[0].shape[1] // tk for p in pairs]
    assert all(p[0].shape[1] % tk == 0 for p in pairs), (name, tk)
    starts = [sum(steps[:i]) for i in range(len(pairs))]
    nk = sum(steps)
    n_p = len(pairs)

    def body(*refs):
        ab = refs[:2 * n_p]
        h_ref, w_ref, dres_ref, dh_ref = refs[2 * n_p:2 * n_p + 4]
        dhb_ref = refs[2 * n_p + 4] if bf16_copy else None
        dw_ref, acc_ref = refs[-2:]
        i, k = pl.program_id(0), pl.program_id(1)

        @pl.when(k == 0)
        def _():
            acc_ref[...] = jnp.zeros_like(acc_ref)

        for p in range(n_p):
            @pl.when((k >= starts[p]) & (k < starts[p] + steps[p]))
            def _(p=p):
                acc_ref[...] += contract(ab[2 * p][...], ab[2 * p + 1][...])

        @pl.when(k == nk - 1)
        def _():
            x = h_ref[...]
            r = lax.rsqrt(jnp.mean(x * x, axis=-1, keepdims=True) + EPS)
            xhat = x * r
            dy = acc_ref[...]
            gw = dy * w_ref[...]
            dh = dres_ref[...] + r * (gw - xhat * jnp.mean(gw * xhat, axis=-1, keepdims=True))
            dh_ref[...] = dh
            if bf16_copy:
                dhb_ref[...] = dh.astype(bf16)
            part = _rowsum8(dy * xhat)

            @pl.when(i == 0)
            def _():
                dw_ref[...] = part

            @pl.when(i > 0)
            def _():
                dw_ref[...] += part

    def clamp(k, p):
        return jnp.clip(k - starts[p], 0, steps[p] - 1)

    in_specs = []
    for p in range(n_p):
        b_spec = (pl.BlockSpec((tk, d), lambda i, k, p=p: (clamp(k, p), 0)) if b_is_kd
                  else pl.BlockSpec((d, tk), lambda i, k, p=p: (0, clamp(k, p))))
        in_specs += [pl.BlockSpec((tm, tk), lambda i, k, p=p: (i, clamp(k, p))), b_spec]
    row = pl.BlockSpec((tm, d), lambda i, k: (i, 0))
    in_specs += [row, pl.BlockSpec((1, d), lambda i, k: (0, 0)), row]
    return pl.pallas_call(
        body, name=name, grid=(m // tm, nk), in_specs=in_specs,
        out_specs=[row] + [row] * bf16_copy + [pl.BlockSpec((SUBLANES, d), lambda i, k: (0, 0))],
        out_shape=[jax.ShapeDtypeStruct((m, d), f32)] + [jax.ShapeDtypeStruct((m, d), bf16)] * bf16_copy + [jax.ShapeDtypeStruct((SUBLANES, d), f32)],
        scratch_shapes=[pltpu.VMEM((tm, d), f32)], compiler_params=_cparams(("arbitrary", "arbitrary"), vmem),
    )(*[t for p in pairs for t in p], h, w, dres)


def _final_loss(h, w, target, name, tm=512):
    m, d = h.shape

    def body(h_ref, w_ref, t_ref, dh_ref, dhb_ref, loss_ref, dw_ref):
        x = h_ref[...]
        r = lax.rsqrt(jnp.mean(x * x, axis=-1, keepdims=True) + EPS)
        xhat = x * r
        ww = w_ref[...]
        err = xhat * ww - t_ref[...]
        dy = err * (1.0 / d)
        gw = dy * ww
        dh = r * (gw - xhat * jnp.mean(gw * xhat, axis=-1, keepdims=True))
        dh_ref[...] = dh
        dhb_ref[...] = dh.astype(bf16)
        lpart = _rowsum8(err * err) * (0.5 / d)
        wpart = _rowsum8(dy * xhat)

        @pl.when(pl.program_id(0) == 0)
        def _():
            loss_ref[...] = lpart
            dw_ref[...] = wpart

        @pl.when(pl.program_id(0) > 0)
        def _():
            loss_ref[...] += lpart
            dw_ref[...] += wpart

    row = pl.BlockSpec((tm, d), lambda i: (i, 0))
    acc = pl.BlockSpec((SUBLANES, d), lambda i: (0, 0))
    return pl.pallas_call(
        body, name=name, grid=(m // tm,),
        in_specs=[row, pl.BlockSpec((1, d), lambda i: (0, 0)), row], out_specs=[row, row, acc, acc],
        out_shape=[jax.ShapeDtypeStruct((m, d), f32), jax.ShapeDtypeStruct((m, d), bf16),
                   jax.ShapeDtypeStruct((SUBLANES, d), f32), jax.ShapeDtypeStruct((SUBLANES, d), f32)],
        compiler_params=_cparams(("arbitrary",)),
    )(h, w, target)


def _lane_tables():
    f = np.arange(LANES) % HEAD_DIM
    inv = ROPE_THETA ** (-jnp.arange(0, ROPE_DIM, 2, dtype=f32) / ROPE_DIM)
    invf = jnp.where(f < ROPE_DIM, inv[f % (ROPE_DIM // 2)], 0.0).astype(f32)
    return invf.reshape(1, LANES)


def _rope_tables(pos_col, name):
    t = pos_col.shape[0]
    tm = SEQ

    def body(p_ref, f_ref, c_ref, s1_ref, s2_ref):
        ang = p_ref[...].astype(f32) * f_ref[...]
        co, si = jnp.cos(ang), jnp.sin(ang)
        f = lax.broadcasted_iota(jnp.int32, (tm, LANES), 1) % HEAD_DIM
        c_ref[...] = jnp.where(f < ROPE_DIM, co, 1.0)
        s1_ref[...] = jnp.where(f < ROPE_DIM // 2, -si, 0.0)
        s2_ref[...] = jnp.where((f >= ROPE_DIM // 2) & (f < ROPE_DIM), si, 0.0)

    row = pl.BlockSpec((tm, LANES), lambda i: (i, 0))
    return pl.pallas_call(
        body, name=name, grid=(t // tm,),
        in_specs=[pl.BlockSpec((tm, 1), lambda i: (i, 0)), pl.BlockSpec((1, LANES), lambda i: (0, 0))],
        out_specs=[row, row, row], out_shape=[jax.ShapeDtypeStruct((t, LANES), f32)] * 3,
        compiler_params=_cparams(("parallel",)),
    )(pos_col, _lane_tables())


def _rot(x, c, s1, s2):
    return x * c + pltpu.roll(x, LANES - ROPE_DIM // 2, 1) * s1 + pltpu.roll(x, ROPE_DIM // 2, 1) * s2


def _rot_t(g, c, s1, s2):
    return g * c + pltpu.roll(g * s1, ROPE_DIM // 2, 1) + pltpu.roll(g * s2, LANES - ROPE_DIM // 2, 1)


def _dup_head(x, kvh, low):
    a = jnp.where(kvh == 0, x, pltpu.roll(x, HEAD_DIM, 1))
    return jnp.where(low, a, pltpu.roll(a, HEAD_DIM, 1))


def _deinterleave(src_ref, dst_ref, d, dtype):
    length = SEQ // d
    if d == 1:
        dst_ref[...] = src_ref[...].astype(dtype)
    else:
        for r in range(d):
            dst_ref[pl.ds(r * length, length), :] = src_ref[pl.ds(r, length, stride=d), :].astype(dtype)


def _interleave_store(src_ref, dst_ref, d, accumulate):
    length = SEQ // d
    if d == 1:
        if accumulate:
            dst_ref[...] += src_ref[...]
        else:
            dst_ref[...] = src_ref[...]
    else:
        for r in range(d):
            blk = src_ref[pl.ds(r * length, length), :]
            if accumulate:
                dst_ref[pl.ds(r, length, stride=d), :] = dst_ref[pl.ds(r, length, stride=d), :] + blk
            else:
                dst_ref[pl.ds(r, length, stride=d), :] = blk


def _attn_masks():
    qi = lax.broadcasted_iota(jnp.int32, (ATTN_BLOCK, ATTN_BLOCK), 0)
    ki = lax.broadcasted_iota(jnp.int32, (ATTN_BLOCK, ATTN_BLOCK), 1)
    low = lax.broadcasted_iota(jnp.int32, (ATTN_BLOCK, LANES), 1) < HEAD_DIM
    return ki <= qi, ki >= qi, low


NEG_INF = float("-inf")
ATTN_UNROLL = 8
SOFTMAX_UNROLL = 4


N_BRANCH = len(DILATIONS)


def _attn_prep(qkv, tabs, name):
    t = qkv.shape[0]
    nb = t // SEQ
    n_j = ATTN_WIDTH // LANES

    def q_body(q_ref, c_ref, s1_ref, s2_ref, out_ref, xr):
        xr[...] = _rot(q_ref[...], c_ref[...], s1_ref[...], s2_ref[...]) * (HEAD_DIM ** -0.5)
        for bi, d in enumerate(DILATIONS):
            _deinterleave(xr, out_ref.at[bi], d, bf16)

    def kv_body(x_ref, c_ref, s1_ref, s2_ref, out_ref, xr):
        lowfull = lax.broadcasted_iota(jnp.int32, (SEQ, LANES), 1) < HEAD_DIM
        x = x_ref[...]
        x = jnp.where(pl.program_id(1) == 0, _rot(x, c_ref[...], s1_ref[...], s2_ref[...]), x)
        for kvh in range(N_KV_HEADS):
            xr[...] = _dup_head(x, kvh, lowfull)
            for bi, d in enumerate(DILATIONS):
                length = SEQ // d
                for r in range(d):
                    rows = xr[...] if d == 1 else xr[pl.ds(r, length, stride=d), :]
                    out_ref[0, bi, pl.ds(r * length, length), kvh * LANES:(kvh + 1) * LANES] = rows.astype(bf16)

    tab = pl.BlockSpec((SEQ, LANES), lambda b, j: (b, 0))
    q = pl.pallas_call(
        q_body, name=name + "_q", grid=(nb, n_j),
        in_specs=[pl.BlockSpec((SEQ, LANES), lambda b, j: (b, j)), tab, tab, tab],
        out_specs=pl.BlockSpec((N_BRANCH, SEQ, LANES), lambda b, j: (0, b, j)),
        out_shape=jax.ShapeDtypeStruct((N_BRANCH, t, ATTN_WIDTH), bf16), scratch_shapes=[pltpu.VMEM((SEQ, LANES), f32)],
        compiler_params=_cparams(("parallel", "parallel")),
    )(qkv, *tabs)
    kv = pl.pallas_call(
        kv_body, name=name + "_kv", grid=(nb, 2),
        in_specs=[pl.BlockSpec((SEQ, LANES), lambda b, j: (b, n_j + j)), tab, tab, tab],
        out_specs=pl.BlockSpec((1, N_BRANCH, SEQ, N_KV_HEADS * LANES), lambda b, j: (j, 0, b, 0)),
        out_shape=jax.ShapeDtypeStruct((2, N_BRANCH, t, N_KV_HEADS * LANES), bf16), scratch_shapes=[pltpu.VMEM((SEQ, LANES), f32)],
        compiler_params=_cparams(("parallel", "parallel")),
    )(qkv, *tabs)
    return q, kv


def _attn_fwd(prep, name):
    q_all, kv_all = prep
    t = q_all.shape[1]
    nb = t // SEQ
    n_blk = SEQ // ATTN_BLOCK

    def body(q_ref, k_ref, v_ref, o_ref, lse_ref, ob, lb, o0, o1, o2, l0, l1, l2, ss):
        cur_ok, prev_ok, low = _attn_masks()
        onat, lnat = (o0, o1, o2), (l0, l1, l2)
        for bi, d in enumerate(DILATIONS):
            qd, kd, vd = q_ref.at[bi], k_ref.at[0, bi], v_ref.at[0, bi]
            per_res = n_blk // d
            use_prev = per_res > 1

            def scores(n, carry):
                start = pl.multiple_of(n * ATTN_BLOCK, ATTN_BLOCK)
                has_prev = (n % per_res) != 0
                pstart = pl.multiple_of(jnp.maximum(n - 1, 0) * ATTN_BLOCK, ATTN_BLOCK)
                qb = qd[pl.ds(start, ATTN_BLOCK), :]
                kc = kd[pl.ds(start, ATTN_BLOCK), :]
                if use_prev:
                    kp = kd[pl.ds(pstart, ATTN_BLOCK), :]
                for a in range(2):
                    qa = jnp.where(low if a == 0 else ~low, qb, jnp.zeros_like(qb))
                    ss[2 * n + a, :, 0:ATTN_BLOCK] = jnp.where(cur_ok, _nt(qa, kc), NEG_INF)
                    if use_prev:
                        ss[2 * n + a, :, ATTN_BLOCK:2 * ATTN_BLOCK] = jnp.where(prev_ok & has_prev, _nt(qa, kp), NEG_INF)
                return carry

            def softmax_pv(n, carry):
                start = pl.multiple_of(n * ATTN_BLOCK, ATTN_BLOCK)
                pstart = pl.multiple_of(jnp.maximum(n - 1, 0) * ATTN_BLOCK, ATTN_BLOCK)
                vc = vd[pl.ds(start, ATTN_BLOCK), :]
                if use_prev:
                    vp = vd[pl.ds(pstart, ATTN_BLOCK), :]
                outs, lses = [], []
                for a in range(2):
                    sc = ss[2 * n + a, :, 0:ATTN_BLOCK]
                    if use_prev:
                        sp = ss[2 * n + a, :, ATTN_BLOCK:2 * ATTN_BLOCK]
                        m = jnp.max(jnp.maximum(sc, sp), axis=1, keepdims=True)
                        pc, pp = jnp.exp(sc - m), jnp.exp(sp - m)
                        den = jnp.sum(pc + pp, axis=1, keepdims=True)
                        acc = _nn(pc.astype(bf16), vc) + _nn(pp.astype(bf16), vp)
                    else:
                        m = jnp.max(sc, axis=1, keepdims=True)
                        pc = jnp.exp(sc - m)
                        den = jnp.sum(pc, axis=1, keepdims=True)
                        acc = _nn(pc.astype(bf16), vc)
                    outs.append(acc * (1.0 / den))
                    lses.append(m + jnp.log(den))
                ob[pl.ds(start, ATTN_BLOCK), :] = jnp.where(low, outs[0], outs[1])
                lb[pl.ds(start, ATTN_BLOCK), :] = jnp.where(low, lses[0], lses[1])
                return carry

            lax.fori_loop(0, n_blk, scores, 0, unroll=ATTN_UNROLL)
            lax.fori_loop(0, n_blk, softmax_pv, 0, unroll=SOFTMAX_UNROLL)
            _interleave_store(ob, onat[bi], d, False)
            _interleave_store(lb, lnat[bi], d, False)
        la, lbb, lc = l0[...], l1[...], l2[...]
        lm = jnp.maximum(jnp.maximum(la, lbb), lc)
        wa, wb, wc = jnp.exp(la - lm), jnp.exp(lbb - lm), jnp.exp(lc - lm)
        ws = wa + wb + wc
        o_ref[...] = (wa * o0[...] + wb * o1[...] + wc * o2[...]) / ws
        lse_ref[...] = lm + jnp.log(ws)

    def col(jj):
        return pl.BlockSpec((SEQ, LANES), lambda b, j: (b, jj if jj is not None else j))

    fs = pltpu.VMEM((SEQ, LANES), f32)
    return pl.pallas_call(
        body, name=name, grid=(nb, ATTN_WIDTH // LANES),
        in_specs=[pl.BlockSpec((N_BRANCH, SEQ, LANES), lambda b, j: (0, b, j)),
                  pl.BlockSpec((1, N_BRANCH, SEQ, LANES), lambda b, j: (0, 0, b, j // 2)),
                  pl.BlockSpec((1, N_BRANCH, SEQ, LANES), lambda b, j: (1, 0, b, j // 2))],
        out_specs=[col(None), col(None)],
        out_shape=[jax.ShapeDtypeStruct((t, ATTN_WIDTH), f32), jax.ShapeDtypeStruct((t, ATTN_WIDTH), f32)],
        scratch_shapes=[fs, fs, fs, fs, fs, fs, fs, fs, pltpu.VMEM((2 * n_blk, ATTN_BLOCK, 2 * ATTN_BLOCK), f32)],
        compiler_params=_cparams(("parallel", "parallel")),
    )(q_all, kv_all, kv_all)


def _attn_bwd(prep, tabs, o, lse, do, name):
    q_all, kv_all = prep
    t = q_all.shape[1]
    nb = t // SEQ
    n_blk = SEQ // ATTN_BLOCK
    n_j = ATTN_WIDTH // LANES

    def body(q_ref, k_ref, v_ref, c_ref, s1_ref, s2_ref, o_ref, lse_ref, do_ref, dq_ref, dk_ref, dv_ref,
             dl, dod, lsd, dld, dqd, dkd, dvd, dqa, dka, dva, pb, dsb, dk_acc, dv_acc):
        j = pl.program_id(1)
        pb[2 * n_blk:2 * n_blk + 2] = jnp.zeros((2, ATTN_BLOCK, 2 * ATTN_BLOCK), bf16)
        dsb[2 * n_blk:2 * n_blk + 2] = jnp.zeros((2, ATTN_BLOCK, 2 * ATTN_BLOCK), bf16)
        kvh = j // 2
        cur_ok, prev_ok, low = _attn_masks()
        lowfull = lax.broadcasted_iota(jnp.int32, (SEQ, LANES), 1) < HEAD_DIM
        c, s1, s2 = c_ref[...], s1_ref[...], s2_ref[...]
        prod = do_ref[...] * o_ref[...]
        d_lo = jnp.sum(jnp.where(lowfull, prod, 0.0), axis=1, keepdims=True)
        d_hi = jnp.sum(jnp.where(lowfull, 0.0, prod), axis=1, keepdims=True)
        dl[...] = jnp.where(lowfull, d_lo, d_hi)
        dqa[...] = jnp.zeros_like(dqa)
        dka[...] = jnp.zeros_like(dka)
        dva[...] = jnp.zeros_like(dva)
        for bi, d in enumerate(DILATIONS):
            qd, kd, vd = q_ref.at[bi], k_ref.at[0, bi], v_ref.at[0, bi]
            _deinterleave(do_ref, dod, d, bf16)
            _deinterleave(lse_ref, lsd, d, f32)
            _deinterleave(dl, dld, d, f32)
            per_res = n_blk // d
            use_prev = per_res > 1
            curl, prevl = slice(0, ATTN_BLOCK), slice(ATTN_BLOCK, 2 * ATTN_BLOCK)

            def halves(x):
                zero = jnp.zeros_like(x)
                return jnp.where(low, x, zero), jnp.where(low, zero, x)

            def probs(n, carry):
                start = pl.multiple_of(n * ATTN_BLOCK, ATTN_BLOCK)
                has_prev = (n % per_res) != 0
                pstart = pl.multiple_of(jnp.maximum(n - 1, 0) * ATTN_BLOCK, ATTN_BLOCK)
                cur, prev = pl.ds(start, ATTN_BLOCK), pl.ds(pstart, ATTN_BLOCK)
                qas, doas = halves(qd[cur, :]), halves(dod[cur, :])
                kc, vc = kd[cur, :], vd[cur, :]
                if use_prev:
                    kp, vp = kd[prev, :], vd[prev, :]
                lsb, dlb = lsd[cur, :], dld[cur, :]
                for a in range(2):
                    ls = lsb[:, a * HEAD_DIM:a * HEAD_DIM + 1]
                    de = dlb[:, a * HEAD_DIM:a * HEAD_DIM + 1]
                    pc = jnp.exp(jnp.where(cur_ok, _nt(qas[a], kc), NEG_INF) - ls)
                    pb[2 * n + a, :, curl] = pc.astype(bf16)
                    dsb[2 * n + a, :, curl] = (pc * (_nt(doas[a], vc) - de)).astype(bf16)
                    if use_prev:
                        pp = jnp.exp(jnp.where(prev_ok & has_prev, _nt(qas[a], kp), NEG_INF) - ls)
                        pb[2 * n + a, :, prevl] = pp.astype(bf16)
                        dsb[2 * n + a, :, prevl] = (pp * (_nt(doas[a], vp) - de)).astype(bf16)
                return carry

            def grads(n, carry):
                start = pl.multiple_of(n * ATTN_BLOCK, ATTN_BLOCK)
                pstart = pl.multiple_of(jnp.maximum(n - 1, 0) * ATTN_BLOCK, ATTN_BLOCK)
                nstart = pl.multiple_of(jnp.minimum(n + 1, n_blk - 1) * ATTN_BLOCK, ATTN_BLOCK)
                cur, prev, nxt = pl.ds(start, ATTN_BLOCK), pl.ds(pstart, ATTN_BLOCK), pl.ds(nstart, ATTN_BLOCK)
                kc = kd[cur, :]
                dqs = [_nn(dsb[2 * n + a, :, curl], kc) for a in range(2)]
                q_rows, do_rows = list(halves(qd[cur, :])), list(halves(dod[cur, :]))
                ds_rows, p_rows = [dsb[2 * n + a, :, curl] for a in range(2)], [pb[2 * n + a, :, curl] for a in range(2)]
                if use_prev:
                    kp = kd[prev, :]
                    dqs = [dqs[a] + _nn(dsb[2 * n + a, :, prevl], kp) for a in range(2)]
                    q_rows += list(halves(qd[nxt, :]))
                    do_rows += list(halves(dod[nxt, :]))
                    ds_rows += [dsb[2 * n + 2 + a, :, prevl] for a in range(2)]
                    p_rows += [pb[2 * n + 2 + a, :, prevl] for a in range(2)]
                dqd[cur, :] = jnp.where(low, dqs[0], dqs[1])
                dkd[cur, :] = _tn(jnp.concatenate(ds_rows, axis=0), jnp.concatenate(q_rows, axis=0))
                dvd[cur, :] = _tn(jnp.concatenate(p_rows, axis=0), jnp.concatenate(do_rows, axis=0))
                return carry

            lax.fori_loop(0, n_blk, probs, 0, unroll=ATTN_UNROLL)
            lax.fori_loop(0, n_blk, grads, 0, unroll=ATTN_UNROLL)
            _interleave_store(dqd, dqa, d, True)
            _interleave_store(dkd, dka, d, True)
            _interleave_store(dvd, dva, d, True)
        dq_ref[...] = _rot_t(dqa[...] * (HEAD_DIM ** -0.5), c, s1, s2).astype(bf16)
        dkf = dka[...]
        dkf = _rot_t(dkf + pltpu.roll(dkf, HEAD_DIM, 1), c, s1, s2)
        dvf = dva[...]
        dvf = dvf + pltpu.roll(dvf, HEAD_DIM, 1)
        mine = (lax.broadcasted_iota(jnp.int32, (SEQ, LANES), 1) // HEAD_DIM) == kvh
        dkc_, dvc_ = jnp.where(mine, dkf, 0.0), jnp.where(mine, dvf, 0.0)

        @pl.when(j == 0)
        def _():
            dk_acc[...] = dkc_
            dv_acc[...] = dvc_

        @pl.when(j > 0)
        def _():
            dk_acc[...] += dkc_
            dv_acc[...] += dvc_

        @pl.when(j == n_j - 1)
        def _():
            dk_ref[...] = dk_acc[...].astype(bf16)
            dv_ref[...] = dv_acc[...].astype(bf16)

    def col(jj):
        return pl.BlockSpec((SEQ, LANES), lambda b, j: (b, jj if jj is not None else j))

    tab = pl.BlockSpec((SEQ, LANES), lambda b, j: (b, 0))
    fs = pltpu.VMEM((SEQ, LANES), f32)
    hs = pltpu.VMEM((SEQ, LANES), bf16)
    return pl.pallas_call(
        body, name=name, grid=(nb, n_j),
        in_specs=[pl.BlockSpec((N_BRANCH, SEQ, LANES), lambda b, j: (0, b, j)),
                  pl.BlockSpec((1, N_BRANCH, SEQ, LANES), lambda b, j: (0, 0, b, j // 2)),
                  pl.BlockSpec((1, N_BRANCH, SEQ, LANES), lambda b, j: (1, 0, b, j // 2)),
                  tab, tab, tab, col(None), col(None), col(None)],
        out_specs=[col(None), tab, tab],
        out_shape=[jax.ShapeDtypeStruct((t, ATTN_WIDTH), bf16), jax.ShapeDtypeStruct((t, LANES), bf16), jax.ShapeDtypeStruct((t, LANES), bf16)],
        scratch_shapes=[fs, hs, fs, fs, fs, fs, fs, fs, fs, fs,
                        pltpu.VMEM((2 * n_blk + 2, ATTN_BLOCK, 2 * ATTN_BLOCK), bf16), pltpu.VMEM((2 * n_blk + 2, ATTN_BLOCK, 2 * ATTN_BLOCK), bf16), fs, fs],
        compiler_params=_cparams(("parallel", "arbitrary")),
    )(q_all, kv_all, kv_all, *tabs, o, lse, do)


def _tap(w_ref, s):
    return w_ref[CONV_WIDTH - 1 - s:CONV_WIDTH - s, :]


def _conv_pre(x, w_ref, b_ref, row):
    shifted = [x] + [jnp.where(row >= s, pltpu.roll(x, s, 0), 0.0) for s in range(1, CONV_WIDTH)]
    pre = b_ref[...] + _tap(w_ref, 0) * x
    for s in range(1, CONV_WIDTH):
        pre = pre + _tap(w_ref, s) * shifted[s]
    return pre, shifted


def _conv_fwd(x, w, b, name, tc=512):
    t, ch = x.shape

    def body(x_ref, w_ref, b_ref, o_ref):
        row = lax.broadcasted_iota(jnp.int32, (SEQ, tc), 0)
        pre, _ = _conv_pre(x_ref[...], w_ref, b_ref, row)
        o_ref[...] = _silu(pre)

    xs = pl.BlockSpec((SEQ, tc), lambda i, j: (i, j))
    return pl.pallas_call(
        body, name=name, grid=(t // SEQ, ch // tc),
        in_specs=[xs, pl.BlockSpec((CONV_WIDTH, tc), lambda i, j: (0, j)), pl.BlockSpec((1, tc), lambda i, j: (0, j))],
        out_specs=xs, out_shape=jax.ShapeDtypeStruct((t, ch), f32),
        compiler_params=_cparams(("parallel", "parallel")),
    )(x, w, b)


def _conv_bwd(x, w, b, dact, name, tc=512):
    t, ch = x.shape

    def body(x_ref, w_ref, b_ref, d_ref, dx_ref, dw_ref, db_ref):
        row = lax.broadcasted_iota(jnp.int32, (SEQ, tc), 0)
        pre, shifted = _conv_pre(x_ref[...], w_ref, b_ref, row)
        dpre = d_ref[...] * _dsilu(pre)
        dx = _tap(w_ref, 0) * dpre
        for s in range(1, CONV_WIDTH):
            dx = dx + _tap(w_ref, s) * jnp.where(row < SEQ - s, pltpu.roll(dpre, SEQ - s, 0), 0.0)
        dx_ref[...] = dx.astype(bf16)
        first = pl.program_id(1) == 0
        parts = [jnp.sum(dpre * shifted[CONV_WIDTH - 1 - k], axis=0, keepdims=True) for k in range(CONV_WIDTH)]
        dbp = jnp.sum(dpre, axis=0, keepdims=True)

        @pl.when(first)
        def _():
            for k in range(CONV_WIDTH):
                dw_ref[k:k + 1, :] = parts[k]
            db_ref[...] = dbp

        @pl.when(jnp.logical_not(first))
        def _():
            for k in range(CONV_WIDTH):
                dw_ref[k:k + 1, :] += parts[k]
            db_ref[...] += dbp

    xs = pl.BlockSpec((SEQ, tc), lambda j, i: (i, j))
    ws = pl.BlockSpec((CONV_WIDTH, tc), lambda j, i: (0, j))
    bs = pl.BlockSpec((1, tc), lambda j, i: (0, j))
    return pl.pallas_call(
        body, name=name, grid=(ch // tc, t // SEQ),
        in_specs=[xs, ws, bs, xs], out_specs=[xs, ws, bs],
        out_shape=[jax.ShapeDtypeStruct((t, ch), bf16), jax.ShapeDtypeStruct((CONV_WIDTH, ch), f32), jax.ShapeDtypeStruct((1, ch), f32)],
        compiler_params=_cparams(("parallel", "arbitrary")),
    )(x, w, b, dact)


GROUP_W = SSM_INNER // SSM_GROUPS
HEADS_PER_GROUP = SSM_HEADS // SSM_GROUPS


def _split3(x):
    hi = x.astype(bf16)
    r1 = x - hi.astype(f32)
    mid = r1.astype(bf16)
    lo = (r1 - mid.astype(f32)).astype(bf16)
    return hi, mid, lo


def _dot_exact(x, sel, dims, x_is_lhs=True):
    parts = _split3(x)
    if x_is_lhs:
        return _dot(parts[0], sel, dims) + _dot(parts[1], sel, dims) + _dot(parts[2], sel, dims)
    return _dot(sel, parts[0], dims) + _dot(sel, parts[1], dims) + _dot(sel, parts[2], dims)


def _ssd_common(xbc_ref, dt_ref, bias_ref, alog_ref):
    r = lax.broadcasted_iota(jnp.int32, (CHUNK, CHUNK), 0)
    cidx = lax.broadcasted_iota(jnp.int32, (CHUNK, CHUNK), 1)
    causal = r >= cidx
    tril = causal.astype(bf16)
    expand = (lax.broadcasted_iota(jnp.int32, (CHUNK, SSM_INNER), 0)
              == lax.broadcasted_iota(jnp.int32, (CHUNK, SSM_INNER), 1) // HEAD_DIM).astype(bf16)
    head_lane = cidx < SSM_HEADS
    dtp = dt_ref[...] + bias_ref[...]
    dt = jnp.where(head_lane, _softplus(dtp), 0.0)
    a_neg = -jnp.exp(alog_ref[...])
    a = dt * a_neg
    nn_dims = ((1,), (0,))
    cs = _dot_exact(a, tril, nn_dims, x_is_lhs=False)
    dt_e = _dot_exact(dt, expand, nn_dims)
    cs_e = _dot_exact(cs, expand, nn_dims)
    xs = xbc_ref[:, 0:SSM_INNER]
    xg = xs * dt_e
    ecs = jnp.exp(cs_e)
    cs_last = cs_e[CHUNK - 1:CHUNK, :]
    dse = jnp.exp(cs_last - cs_e)
    cde = jnp.exp(cs_last)
    return dict(r=r, cidx=cidx, causal=causal, tril=tril, expand=expand, head_lane=head_lane, dtp=dtp, dt=dt, a_neg=a_neg,
                cs=cs, cst=cs.T, dt_e=dt_e, cs_e=cs_e, xs=xs, xg=xg, ecs=ecs, dse=dse, cde=cde)


def _decay_mat(q, h):
    return jnp.exp(jnp.where(q["causal"], q["cs"][:, h:h + 1] - q["cst"][h:h + 1, :], NEG_INF))


def _gate_norm(y, z, nw, gate=None):
    y2 = y * (_silu(z) if gate is None else gate)
    outs, xhats, rs = [], [], []
    for g in range(SSM_GROUPS):
        sl = slice(g * GROUP_W, (g + 1) * GROUP_W)
        yg = y2[:, sl]
        r = lax.rsqrt(jnp.mean(yg * yg, axis=-1, keepdims=True) + EPS)
        xhats.append(yg * r)
        rs.append(r)
        outs.append(yg * r * nw[:, sl])
    return y2, outs, xhats, rs


def _ssd_fwd(xbc, z, dtp, params, name):
    t = xbc.shape[0]
    n_chunk = SEQ // CHUNK
    low = None

    def body(xbc_ref, z_ref, dt_ref, bias_ref, alog_ref, dskip_ref, nw_ref, yn_ref, y_ref, hs_ref, h_scr):
        @pl.when(pl.program_id(1) == 0)
        def _():
            h_scr[...] = jnp.zeros_like(h_scr)

        q = _ssd_common(xbc_ref, dt_ref, bias_ref, alog_ref)
        low = lax.broadcasted_iota(jnp.int32, (CHUNK, LANES), 1) < HEAD_DIM
        xgb = q["xg"].astype(bf16)
        wst = (q["xg"] * q["dse"]).astype(bf16)
        hs_ref[0] = h_scr[...]
        ys = []
        for g in range(SSM_GROUPS):
            gl = slice(g * GROUP_W, (g + 1) * GROUP_W)
            bg = xbc_ref[:, SSM_INNER + g * D_STATE:SSM_INNER + (g + 1) * D_STATE].astype(bf16)
            cg = xbc_ref[:, SSM_INNER + SSM_GROUPS * D_STATE + g * D_STATE:SSM_INNER + SSM_GROUPS * D_STATE + (g + 1) * D_STATE].astype(bf16)
            cb = _nt(cg, bg)
            hg = h_scr[g]
            yoff = _nn(cg, hg.astype(bf16)) * q["ecs"][:, gl]
            pieces = []
            for i in range(HEADS_PER_GROUP // 2):
                h0 = g * HEADS_PER_GROUP + 2 * i
                xp = xgb[:, h0 * HEAD_DIM:(h0 + 2) * HEAD_DIM]
                m0 = (cb * _decay_mat(q, h0)).astype(bf16)
                m1 = (cb * _decay_mat(q, h0 + 1)).astype(bf16)
                zero = jnp.zeros_like(xp)
                pieces.append(_nn(m0, jnp.where(low, xp, zero)) + _nn(m1, jnp.where(low, zero, xp)))
            ys.append(jnp.concatenate(pieces, axis=1) + yoff + dskip_ref[:, gl] * q["xs"][:, gl])
            h_scr[g] = hg * q["cde"][:, gl] + _tn(bg, wst[:, gl])
        y = jnp.concatenate(ys, axis=1)
        y_ref[...] = y
        _, outs, _, _ = _gate_norm(y, z_ref[...], nw_ref[...])
        yn_ref[...] = jnp.concatenate(outs, axis=1).astype(bf16)

    def rows(w):
        return pl.BlockSpec((CHUNK, w), lambda b, c: (b * n_chunk + c, 0))

    def par(w):
        return pl.BlockSpec((1, w), lambda b, c: (0, 0))

    return pl.pallas_call(
        body, name=name, grid=(t // SEQ, n_chunk),
        in_specs=[rows(CONV_CH), rows(SSM_INNER), rows(LANES), par(LANES), par(LANES), par(SSM_INNER), par(SSM_INNER)],
        out_specs=[rows(SSM_INNER), rows(SSM_INNER), pl.BlockSpec((1, SSM_GROUPS, D_STATE, GROUP_W), lambda b, c: (b * n_chunk + c, 0, 0, 0))],
        out_shape=[jax.ShapeDtypeStruct((t, SSM_INNER), bf16), jax.ShapeDtypeStruct((t, SSM_INNER), f32),
                   jax.ShapeDtypeStruct((t // CHUNK, SSM_GROUPS, D_STATE, GROUP_W), f32)],
        scratch_shapes=[pltpu.VMEM((SSM_GROUPS, D_STATE, GROUP_W), f32)],
        compiler_params=_cparams(("parallel", "arbitrary")),
    )(xbc, z, dtp, *params)


def _ssd_bwd(xbc, z, dtp, y, hs, dyn, params, name):
    t = xbc.shape[0]
    n_chunk = SEQ // CHUNK

    def body(xbc_ref, z_ref, dt_ref, y_ref, hs_ref, dyn_ref, bias_ref, alog_ref, dskip_ref, nw_ref,
             dxbc_ref, dz_ref, ddt_ref, dnw_ref, dds_ref, dal_ref, dbi_ref, dh_scr):
        @pl.when(pl.program_id(1) == 0)
        def _():
            dh_scr[...] = jnp.zeros_like(dh_scr)

        q = _ssd_common(xbc_ref, dt_ref, bias_ref, alog_ref)
        low = lax.broadcasted_iota(jnp.int32, (CHUNK, LANES), 1) < HEAD_DIM
        last_row = lax.broadcasted_iota(jnp.int32, (CHUNK, GROUP_W), 0) == CHUNK - 1
        xs, xg = q["xs"], q["xg"]
        xgb = xg.astype(bf16)
        wf = xg * q["dse"]
        wst = wf.astype(bf16)
        zz = z_ref[...]
        yy = y_ref[...]
        sz, dsz = _silu_and_grad(zz)
        y2, _, xhats, rs = _gate_norm(yy, zz, nw_ref[...], gate=sz)
        dyn_ = dyn_ref[...]
        dy2s, dnws = [], []
        for g in range(SSM_GROUPS):
            gl = slice(g * GROUP_W, (g + 1) * GROUP_W)
            gw = dyn_[:, gl] * nw_ref[:, gl]
            dy2s.append(rs[g] * (gw - xhats[g] * jnp.mean(gw * xhats[g], axis=-1, keepdims=True)))
            dnws.append(_rowsum8(dyn_[:, gl] * xhats[g]))
        dy2 = jnp.concatenate(dy2s, axis=1)
        dy = dy2 * sz
        dz_ref[...] = (dy2 * yy * dsz).astype(bf16)
        dnw_p = jnp.concatenate(dnws, axis=1)
        dds_p = _rowsum8(dy * xs)
        dyb = dy.astype(bf16)
        gfull = (dy * q["ecs"]).astype(bf16)
        dcs_c = jnp.zeros((CHUNK, CHUNK), f32)
        dcs_r = jnp.zeros((CHUNK, CHUNK), f32)
        dcs_e_parts, dxg_parts = [], []
        for g in range(SSM_GROUPS):
            gl = slice(g * GROUP_W, (g + 1) * GROUP_W)
            bsl = slice(SSM_INNER + g * D_STATE, SSM_INNER + (g + 1) * D_STATE)
            csl = slice(SSM_INNER + SSM_GROUPS * D_STATE + g * D_STATE, SSM_INNER + SSM_GROUPS * D_STATE + (g + 1) * D_STATE)
            bg = xbc_ref[:, bsl].astype(bf16)
            cg = xbc_ref[:, csl].astype(bf16)
            cb = _nt(cg, bg)
            hg = hs_ref[0, g]
            hgb = hg.astype(bf16)
            dhn = dh_scr[g]
            dhnb = dhn.astype(bf16)
            yoff = _nn(cg, hgb) * q["ecs"][:, gl]
            dw_ = _nn(bg, dhnb)
            r_e = dw_ * wf[:, gl]
            to_last = jnp.sum(r_e, axis=0, keepdims=True) + jnp.sum(dhn * hg, axis=0, keepdims=True) * q["cde"][:, gl]
            dcs_e_parts.append(dy[:, gl] * yoff - r_e + jnp.where(last_row, to_last, 0.0))
            dcb = jnp.zeros((CHUNK, CHUNK), f32)
            dxg_pairs = []
            for i in range(HEADS_PER_GROUP // 2):
                h0 = g * HEADS_PER_GROUP + 2 * i
                psl = slice(h0 * HEAD_DIM, (h0 + 2) * HEAD_DIM)
                xp = xgb[:, psl]
                dyp = dyb[:, psl]
                zero = jnp.zeros_like(dyp)
                tns = []
                for a in range(2):
                    h = h0 + a
                    lm = _decay_mat(q, h)
                    m = cb * lm
                    dm = _nt(jnp.where(low, dyp, zero) if a == 0 else jnp.where(low, zero, dyp), xp)
                    dcb = dcb + dm * lm
                    nmat = dm * m
                    dcs_c = dcs_c + jnp.where(q["cidx"] == h, jnp.sum(nmat, axis=1, keepdims=True), 0.0)
                    dcs_r = dcs_r + jnp.where(q["r"] == h, jnp.sum(nmat, axis=0, keepdims=True), 0.0)
                    tns.append(_tn(m.astype(bf16), dyp))
                dxg_pairs.append(jnp.where(low, tns[0], tns[1]))
            dxg_parts.append(jnp.concatenate(dxg_pairs, axis=1) + dw_ * q["dse"][:, gl])
            dcbb = dcb.astype(bf16)
            dxbc_ref[:, csl] = _nt(gfull[:, gl], hgb) + _nn(dcbb, bg)
            dxbc_ref[:, bsl] = _nt(wst[:, gl], dhnb) + _tn(dcbb, cg)
            dh_scr[g] = dhn * q["cde"][:, gl] + _tn(cg, gfull[:, gl])
        dxg = jnp.concatenate(dxg_parts, axis=1)
        dcs_e = jnp.concatenate(dcs_e_parts, axis=1)
        dxbc_ref[:, 0:SSM_INNER] = dskip_ref[...] * dy + dxg * q["dt_e"]
        dcs = dcs_c - dcs_r.T + _dot_exact(dcs_e, q["expand"], ((1,), (1,)))
        triu = (q["cidx"] >= q["r"]).astype(bf16)
        da = _dot_exact(dcs, triu, ((1,), (0,)), x_is_lhs=False)
        ddt = _dot_exact(dxg * xs, q["expand"], ((1,), (1,))) + da * q["a_neg"]
        ddtp = jnp.where(q["head_lane"], ddt * _sigmoid(q["dtp"]), 0.0)
        ddt_ref[...] = ddtp.astype(bf16)
        dal_p = _rowsum8(da * q["dt"]) * q["a_neg"]
        dbi_p = _rowsum8(ddtp)
        first = (pl.program_id(0) == 0) & (pl.program_id(1) == 0)

        @pl.when(first)
        def _():
            dnw_ref[...] = dnw_p
            dds_ref[...] = dds_p
            dal_ref[...] = dal_p
            dbi_ref[...] = dbi_p

        @pl.when(jnp.logical_not(first))
        def _():
            dnw_ref[...] += dnw_p
            dds_ref[...] += dds_p
            dal_ref[...] += dal_p
            dbi_ref[...] += dbi_p

    def rows(w):
        return pl.BlockSpec((CHUNK, w), lambda b, c: (b * n_chunk + n_chunk - 1 - c, 0))

    def par(w):
        return pl.BlockSpec((1, w), lambda b, c: (0, 0))

    def acc(w):
        return pl.BlockSpec((SUBLANES, w), lambda b, c: (0, 0))

    return pl.pallas_call(
        body, name=name, grid=(t // SEQ, n_chunk),
        in_specs=[rows(CONV_CH), rows(SSM_INNER), rows(LANES), rows(SSM_INNER),
                  pl.BlockSpec((1, SSM_GROUPS, D_STATE, GROUP_W), lambda b, c: (b * n_chunk + n_chunk - 1 - c, 0, 0, 0)),
                  rows(SSM_INNER), par(LANES), par(LANES), par(SSM_INNER), par(SSM_INNER)],
        out_specs=[rows(CONV_CH), rows(SSM_INNER), rows(LANES), acc(SSM_INNER), acc(SSM_INNER), acc(LANES), acc(LANES)],
        out_shape=[jax.ShapeDtypeStruct((t, CONV_CH), f32), jax.ShapeDtypeStruct((t, SSM_INNER), bf16), jax.ShapeDtypeStruct((t, LANES), bf16),
                   jax.ShapeDtypeStruct((SUBLANES, SSM_INNER), f32), jax.ShapeDtypeStruct((SUBLANES, SSM_INNER), f32),
                   jax.ShapeDtypeStruct((SUBLANES, LANES), f32), jax.ShapeDtypeStruct((SUBLANES, LANES), f32)],
        scratch_shapes=[pltpu.VMEM((SSM_GROUPS, D_STATE, GROUP_W), f32)],
        compiler_params=_cparams(("arbitrary", "arbitrary")),
    )(xbc, z, dtp, y, hs, dyn, *params)


def _adamw_update(g, w, m, v):
    mm = ADAM_B1 * m + (1.0 - ADAM_B1) * g
    vv = ADAM_B2 * v + (1.0 - ADAM_B2) * (g * g)
    m_hat = mm / (1.0 - ADAM_B1 ** ADAM_STEP)
    v_hat = vv / (1.0 - ADAM_B2 ** ADAM_STEP)
    return -ADAM_LR * (m_hat / (jnp.sqrt(v_hat) + ADAM_EPS) + ADAM_WD * w), mm, vv


def _adamw(g_parts, w, m, v, name):
    rows, width = w.shape
    n = len(g_parts)
    tr = _row_tile(rows)

    def body(*refs):
        g_refs, (w_ref, m_ref, v_ref, g_out, d_out, m_out, v_out) = refs[:n], refs[n:]
        g = g_refs[0][...].astype(f32)
        for r in g_refs[1:]:
            g = g + r[...].astype(f32)
        g_out[...] = g
        d_out[...], m_out[...], v_out[...] = _adamw_update(g, w_ref[...], m_ref[...], v_ref[...])

    spec = pl.BlockSpec((tr, width), lambda i: (i, 0))
    return pl.pallas_call(
        body, name=name, grid=(rows // tr,), in_specs=[spec] * (n + 3), out_specs=[spec] * 4,
        out_shape=[jax.ShapeDtypeStruct((rows, width), f32)] * 4, compiler_params=_cparams(("parallel",)),
    )(*g_parts, w, m, v)


def _adamw_layers(landed, w, m, v, after, name, layers_on_columns=False):
    depth = len(landed)
    _, rows, width = landed[0].shape
    tr = _row_tile(rows)
    n_i = rows // tr
    at = (lambda ref: ref) if layers_on_columns else (lambda ref: ref.at[0])

    def body(*refs):
        part_refs, (w_ref, m_ref, v_ref, _, g_out, d_out, m_out, v_out) = refs[:depth * N_DEV], refs[depth * N_DEV:]
        for l in range(depth):
            @pl.when(pl.program_id(0) == l)
            def _(l=l):
                g = part_refs[l * N_DEV][0].astype(f32)
                for r in part_refs[l * N_DEV + 1:(l + 1) * N_DEV]:
                    g = g + r[0].astype(f32)
                at(g_out)[...] = g
                at(d_out)[...], at(m_out)[...], at(v_out)[...] = _adamw_update(g, at(w_ref)[...], at(m_ref)[...], at(v_ref)[...])

    def part_spec(l, p):
        return pl.BlockSpec((1, tr, width), lambda ll, i: (p, jnp.where(ll == l, i, jnp.where(ll < l, 0, n_i - 1)), 0))

    state = (pl.BlockSpec((tr, width), lambda ll, i: (i, ll)) if layers_on_columns
             else pl.BlockSpec((1, tr, width), lambda ll, i: (ll, i, 0)))
    return pl.pallas_call(
        body, name=name, grid=(depth, n_i),
        in_specs=[part_spec(l, p) for l in range(depth) for p in range(N_DEV)] + [state] * 3 + [ANY], out_specs=[state] * 4,
        out_shape=[jax.ShapeDtypeStruct(w.shape, f32)] * 4, compiler_params=_cparams(("arbitrary", "arbitrary")),
    )(*[landed[l] for l in range(depth) for _ in range(N_DEV)], w, m, v, after)


def _row_tile(rows, cap=512):
    for cand in range(min(rows, cap) // SUBLANES * SUBLANES, 0, -SUBLANES):
        if rows % cand == 0:
            return cand
    return rows


def _cols_from_devices(g, width, name):
    n_dev, depth, a, b = g.shape

    def body(g_ref, o_ref):
        for i in range(n_dev):
            o_ref[0, :, i * b:(i + 1) * b] = g_ref[i, 0]
        if width > n_dev * b:
            o_ref[0, :, n_dev * b:width] = jnp.zeros((a, width - n_dev * b), o_ref.dtype)

    return pl.pallas_call(
        body, name=name, grid=(depth,), in_specs=[pl.BlockSpec((n_dev, 1, a, b), lambda l: (0, l, 0, 0))],
        out_specs=pl.BlockSpec((1, a, width), lambda l: (l, 0, 0)), out_shape=jax.ShapeDtypeStruct((depth, a, width), g.dtype),
        compiler_params=_cparams(("parallel",)),
    )(g)


def _devices_from_cols(per_layer, b, name, tr=256):
    depth = len(per_layer)
    a, width = per_layer[0].shape

    def body(*refs):
        o_ref = refs[depth]
        for l in range(depth):
            for i in range(N_DEV):
                o_ref[i, l] = refs[l][:, i * b:(i + 1) * b]

    return pl.pallas_call(
        body, name=name, grid=(a // tr,), in_specs=[pl.BlockSpec((tr, width), lambda r: (r, 0))] * depth,
        out_specs=pl.BlockSpec((N_DEV, depth, tr, b), lambda r: (0, 0, r, 0)),
        out_shape=jax.ShapeDtypeStruct((N_DEV, depth, a, b), per_layer[0].dtype), compiler_params=_cparams(("parallel",)),
    )(*per_layer)


def _me():
    return lax.axis_index("x"), lax.axis_index("y"), lax.axis_index("c")


def _allgather_two_level(shards, name):
    n = len(shards)
    per = 7

    def body(*refs):
        ins, outs, token = refs[:n], refs[n:2 * n], refs[2 * n]
        send_sems, recv_sems, local_sems = refs[2 * n + 1:]
        token[...] = jnp.zeros_like(token)
        x, y, c = _me()
        me, sibling = (x, y, c), (x, y, 1 - c)
        chips = [(1 - x, y), (x, 1 - y), (1 - x, 1 - y)]

        def slot(a, p):
            return outs[a].at[4 * p[0] + 2 * p[1] + p[2]]

        def copy(a, k, block, to, src=None):
            return pltpu.make_async_remote_copy(
                src_ref=slot(a, block) if src is None else src, dst_ref=slot(a, block),
                send_sem=send_sems.at[a * per + k], recv_sem=recv_sems.at[a * per + k], device_id=to, device_id_type=MESH)

        mine = [pltpu.make_async_copy(ins[a], slot(a, me), local_sems.at[a]) for a in range(n)]
        for cp in mine:
            cp.start()
        first = []
        for a in range(n):
            first.append(copy(a, 0, me, sibling, src=ins[a]))
            first += [copy(a, 1 + j, me, (*chip, c), src=ins[a]) for j, chip in enumerate(chips)]
        for cp in first:
            cp.start()
        passed = []
        for j, chip in enumerate(chips):
            for a in range(n):
                copy(a, 1 + j, (*chip, c), me).wait_recv()
                fwd = copy(a, 4 + j, (*chip, c), sibling)
                fwd.start()
                passed.append(fwd)
        for a in range(n):
            copy(a, 0, sibling, me).wait_recv()
            for j, chip in enumerate(chips):
                copy(a, 4 + j, (*chip, 1 - c), me).wait_recv()
        for cp in first + passed:
            cp.wait_send()
        for cp in mine:
            cp.wait()

    outs = pl.pallas_call(
        body, name=name, in_specs=[ANY] * n, out_specs=[ANY] * n + [pl.BlockSpec(memory_space=pltpu.VMEM)],
        out_shape=[jax.ShapeDtypeStruct((N_DEV,) + s.shape, s.dtype) for s in shards] + [jax.ShapeDtypeStruct((SUBLANES, LANES), f32)],
        scratch_shapes=[pltpu.SemaphoreType.DMA((n * per,)), pltpu.SemaphoreType.DMA((n * per,)), pltpu.SemaphoreType.DMA((n,))],
    )(*shards)
    return outs[:n], outs[n]


def _allgather_direct(row, name):
    def body(in_ref, out_ref, send_sems, recv_sems, local_sem):
        x, y, c = _me()
        mine = out_ref.at[4 * x + 2 * y + c]
        local = pltpu.make_async_copy(in_ref, mine, local_sem)
        local.start()
        sends = []
        for k in range(1, N_DEV):
            px, py, pc = x ^ (k >> 2), y ^ ((k >> 1) & 1), c ^ (k & 1)
            sends.append(pltpu.make_async_remote_copy(
                src_ref=in_ref, dst_ref=mine, send_sem=send_sems.at[k - 1], recv_sem=recv_sems.at[k - 1],
                device_id=(px, py, pc), device_id_type=MESH))
        for cp in sends:
            cp.start()
        for k in range(1, N_DEV):
            px, py, pc = x ^ (k >> 2), y ^ ((k >> 1) & 1), c ^ (k & 1)
            theirs = out_ref.at[4 * px + 2 * py + pc]
            pltpu.make_async_remote_copy(
                src_ref=in_ref, dst_ref=theirs, send_sem=send_sems.at[k - 1], recv_sem=recv_sems.at[k - 1],
                device_id=(px, py, pc), device_id_type=MESH).wait_recv()
        for cp in sends:
            cp.wait_send()
        local.wait()

    return pl.pallas_call(
        body, name=name, in_specs=[ANY], out_specs=ANY, out_shape=jax.ShapeDtypeStruct((N_DEV,) + row.shape, row.dtype),
        scratch_shapes=[pltpu.SemaphoreType.DMA((N_DEV - 1,)), pltpu.SemaphoreType.DMA((N_DEV - 1,)), pltpu.SemaphoreType.DMA],
    )(row)


N_CHIP = N_DEV // 2
HBM = pl.BlockSpec(memory_space=pltpu.HBM)
SEM = pl.BlockSpec(memory_space=pltpu.SEMAPHORE)
EFFECT = pltpu.SideEffectType.DATAFLOW_SIDE_EFFECTING


def _peer(k):
    x, y, c = _me()
    return x ^ (k >> 2), y ^ ((k >> 1) & 1), c ^ (k & 1)


def _direct_copies(srcs, lands, send_sems, recv_sems, per_peer):
    x, y, c = _me()
    me = 4 * x + 2 * y + c
    copies = []
    for a in range(len(srcs)):
        for k in range(1, N_DEV):
            px, py, pc = _peer(k)
            piece = srcs[a].at[4 * px + 2 * py + pc] if per_peer else srcs[a]
            copies.append(pltpu.make_async_remote_copy(
                src_ref=piece, dst_ref=lands[a].at[me], send_sem=send_sems.at[a * (N_DEV - 1) + k - 1],
                recv_sem=recv_sems.at[a * (N_DEV - 1) + k - 1], device_id=(px, py, pc), device_id_type=MESH))
    return copies


def _direct_start(srcs, lands, per_peer, name):
    n = len(srcs)
    n_sem = n * (N_DEV - 1)

    def body(*refs):
        src_refs, land_refs = refs[:n], refs[n:2 * n]
        send_sems, recv_sems = refs[2 * n], refs[2 * n + 1]
        token = refs[-1]
        for cp in _direct_copies(src_refs, land_refs, send_sems, recv_sems, per_peer):
            cp.start()
        token[...] = jnp.zeros_like(token)

    outs = pl.pallas_call(
        body, name=name,
        out_shape=(pltpu.SemaphoreType.DMA((n_sem,)), pltpu.SemaphoreType.DMA((n_sem,)),
                   *[pltpu.HBM(s.shape, s.dtype) for s in srcs], *[pltpu.HBM(s.shape, s.dtype) for s in lands],
                   jax.ShapeDtypeStruct((SUBLANES, LANES), f32)),
        in_specs=[HBM] * (2 * n), out_specs=(SEM, SEM, *[HBM] * (2 * n), pl.BlockSpec(memory_space=pltpu.VMEM)),
        input_output_aliases={i: 2 + i for i in range(2 * n)},
        compiler_params=pltpu.CompilerParams(has_side_effects=EFFECT),
    )(*[pltpu.with_memory_space_constraint(s, pltpu.HBM) for s in srcs], *[pltpu.with_memory_space_constraint(s, pltpu.HBM) for s in lands])
    return outs[0], outs[1], outs[2:2 + n], outs[2 + n:2 + 2 * n], outs[-1]


def _direct_wait(send_sems, recv_sems, srcs, lands, after, per_peer, name):
    n = len(srcs)

    def body(*refs):
        src_refs, land_refs = refs[:n], refs[n:2 * n]
        s_sems, r_sems = refs[2 * n], refs[2 * n + 1]
        for cp in _direct_copies(src_refs, land_refs, s_sems, r_sems, per_peer):
            cp.wait_send()
            cp.wait_recv()

    outs = pl.pallas_call(
        body, name=name,
        out_shape=tuple(pltpu.HBM(s.shape, s.dtype) for s in list(srcs) + list(lands)),
        in_specs=[HBM] * (2 * n) + [SEM, SEM, ANY], out_specs=tuple([HBM] * (2 * n)),
        input_output_aliases={i: i for i in range(2 * n)},
        compiler_params=pltpu.CompilerParams(has_side_effects=EFFECT),
    )(*srcs, *lands, send_sems, recv_sems, after)
    return outs[n:]


def _row(v, width=None):
    v = v.reshape(1, -1).astype(f32)
    if width is not None and v.shape[1] < width:
        v = jnp.pad(v, ((0, 0), (0, width - v.shape[1])))
    return v


def _layer_params(p, l):
    return dict(
        norm_mix=_row(p["norm_mix"][l]), norm_ffn=_row(p["norm_ffn"][l]), conv_w=p["conv_w"][l], conv_b=_row(p["conv_b"][l]),
        ssd=(_row(p["dt_bias"][l], LANES), _row(p["a_log"][l], LANES), _row(jnp.repeat(p["d_skip"][l], HEAD_DIM)), _row(p["ssm_norm"][l])))


def _layer_fwd(h, w_in, rest, sp, tabs, l):
    tag = f"l{l}_"
    hn = _rmsnorm_fwd(h, sp["norm_mix"], tag + "norm_mix")
    qkv, z, xbc_pre = _in_proj(hn, w_in, (QKV_WIDTH, SSM_INNER, CONV_CH), tag + "proj")
    dtp = _matmul(hn, w_in, mode="nn", n_out=LANES, tn=LANES, b_off=DT_OFF // LANES, name=tag + "proj_dt")
    prep = _attn_prep(qkv, tabs, tag + "attn_prep")
    o, lse = _attn_fwd(prep, tag + "attn_fwd")
    xbc = _conv_fwd(xbc_pre, sp["conv_w"], sp["conv_b"], tag + "conv_fwd")
    yn, y, hs = _ssd_fwd(xbc, z, dtp, sp["ssd"], tag + "ssd_fwd")
    w_out, w_gate, w_up, w_down = rest(yn) if callable(rest) else rest
    h2 = _out_proj(o, yn, w_out, h, tag + "out_proj")
    hn2 = _rmsnorm_fwd(h2, sp["norm_ffn"], tag + "norm_ffn")
    g, u, act = _swiglu_fwd(hn2, w_gate, w_up, tag + "ffn_up")
    h3 = _matmul(act, w_down, mode="nn", tk=1408, add=h2, name=tag + "ffn_down")
    saved = dict(h=h, hn=hn, prep=prep, z=z, xbc_pre=xbc_pre, dtp=dtp, o=o, lse=lse, xbc=xbc, yn=yn, y=y, hs=hs, h2=h2, hn2=hn2, g=g, u=u, act=act,
                 rest=(w_out, w_gate, w_up, w_down))
    return h3, saved


def _layer_bwd(dh3_pair, s, big, sp, tabs, l, gd=f32, after_ffn=None):
    tag = f"l{l}_"
    dh3, dh3b = dh3_pair
    w_in, w_out, w_gate, w_up, w_down = big
    dg, du = _swiglu_bwd(dh3b, w_down, s["g"], s["u"], tag + "ffn_down_bwd")
    dw_down = _matmul(s["act"], dh3b, mode="tn", tm=1408, tn=512, tk=2048, out_dtype=gd, name=tag + "dw_down")
    dw_gate = _matmul(dg, s["hn2"], mode="tn", tm=1408, tn=512, tk=2048, out_dtype=gd, name=tag + "dw_gate")
    dw_up = _matmul(du, s["hn2"], mode="tn", tm=1408, tn=512, tk=2048, out_dtype=gd, name=tag + "dw_up")
    norm_ffn = sp["norm_ffn"] if after_ffn is None else sp["norm_ffn"] + after_ffn(dict(w_gate=dw_gate, w_up=dw_up, w_down=dw_down))
    dh2, dh2b, dnf = _nt_norm_bwd([(dg, w_gate), (du, w_up)], s["h2"], norm_ffn, dh3, tag + "ffn_up_bwd_norm", tk=1408, b_is_kd=True,
                                  vmem=VMEM_LIMIT_TWO_PAIRS)
    d_o = _matmul(dh2b, w_out, mode="nt", n_out=ATTN_WIDTH, tn=512, b_off=0, name=tag + "out_attn_bwd")
    dyn = _matmul(dh2b, w_out, mode="nt", n_out=SSM_INNER, tn=512, b_off=1, name=tag + "out_ssm_bwd")
    dw_out = jnp.concatenate([_matmul(s["o"], dh2b, mode="tn", tm=512, tn=512, tk=2048, out_dtype=gd, name=tag + "dw_out_attn"),
                              _matmul(s["yn"], dh2b, mode="tn", tm=512, tn=512, tk=2048, out_dtype=gd, name=tag + "dw_out_ssm")], axis=0)
    dxbc, dz, ddtp, dnw, dds, dal, dbi = _ssd_bwd(s["xbc"], s["z"], s["dtp"], s["y"], s["hs"], dyn, sp["ssd"], tag + "ssd_bwd")
    dxbc_pre, dconv_w, dconv_b = _conv_bwd(s["xbc_pre"], sp["conv_w"], sp["conv_b"], dxbc, tag + "conv_bwd")
    dq, dk, dv = _attn_bwd(s["prep"], tabs, s["o"], s["lse"], d_o, tag + "attn_bwd")
    dproj = jnp.concatenate([dq, dk, dv, dz, dxbc_pre, ddtp], axis=1)
    dw_in = _matmul(s["hn"], dproj, mode="tn", tm=512, tn=1152, tk=2048, out_dtype=gd, name=tag + "dw_in")
    res = _nt_norm_bwd([(dproj, w_in)], s["h"], sp["norm_mix"], dh2, tag + "proj_bwd_norm", tk=1152, bf16_copy=l > 0)
    dh, dhb, dnm = res if l > 0 else (res[0], None, res[1])
    grads = dict(
        norm_mix=dnm.sum(0), w_in=dw_in, conv_w=dconv_w, conv_b=dconv_b[0], dt_bias=dbi.sum(0)[:SSM_HEADS], a_log=dal.sum(0)[:SSM_HEADS],
        d_skip=dds.sum(0).reshape(SSM_HEADS, HEAD_DIM).sum(1), ssm_norm=dnw.sum(0), w_out=dw_out, norm_ffn=dnf.sum(0),
        w_gate=dw_gate, w_up=dw_up, w_down=dw_down)
    return (dh, dhb), grads


def _local_step(x, positions, target, p, bigs):
    tabs = _rope_tables(positions.reshape(-1, 1), "rope_tables")
    h = x
    saved, sps = [], []
    for l in range(DEPTH):
        sps.append(_layer_params(p, l))
        h, s = _layer_fwd(h, bigs[l][0], bigs[l][1:], sps[l], tabs, l)
        saved.append(s)
    dh, dhb, loss_parts, dfn = _final_loss(h, _row(p["final_norm"]), target, "final_loss")
    dh = (dh, dhb)
    layer_grads = [None] * DEPTH
    for l in reversed(range(DEPTH)):
        dh, layer_grads[l] = _layer_bwd(dh, saved[l], bigs[l], sps[l], tabs, l)
    grads = {k: [layer_grads[l][k] for l in range(DEPTH)] for k in layer_grads[0]}
    grads["final_norm"] = dfn.sum(0)
    return jnp.sum(loss_parts), dh[0], grads


BIG = ("w_in", "w_out", "w_gate", "w_up", "w_down")
REST = BIG[1:]
FFN = ("w_gate", "w_up", "w_down")
MIX = ("w_in", "w_out")
COL_SHARDED = ("w_in",)
TRANSPOSED = ("w_gate", "w_up")
SMALL = ("norm_mix", "conv_b", "dt_bias", "a_log", "d_skip", "ssm_norm", "norm_ffn", "final_norm")
WEIGHTS = ("norm_mix", "w_in", "conv_w", "conv_b", "dt_bias", "a_log", "d_skip", "ssm_norm", "w_out", "norm_ffn", "w_gate", "w_up", "w_down", "final_norm")
SMALL_ROWS = 88
CONVW_ROWS = 96
CONVW_SHARD_ROWS = 16


def _full_from_gathered(name, g, l):
    _, a, b = g.shape
    if name in COL_SHARDED:
        width = IN_PROJ_PAD if name == "w_in" else N_DEV * b
        return _cols_from_devices(g.reshape(N_DEV, 1, a, b), width, f"cols_l{l}_{name}").reshape(a, width)
    return g.reshape(N_DEV * a, b)


def _by_device(name, full, shard_shape, l):
    a, b = shard_shape
    if name in COL_SHARDED:
        return _devices_from_cols([full], b, f"devs_l{l}_{name}").reshape(N_CHIP, 2, a, b)
    return full.reshape(N_CHIP, 2, a, b)


def _pack_rows(parts, rows, width):
    flat = jnp.concatenate([q.reshape(-1) for q in parts])
    return jnp.pad(flat, (0, rows * width - flat.shape[0])).reshape(rows, width)


def _unpack(flat, like):
    out, off = [], 0
    for q in like:
        out.append(flat[off:off + q.size].reshape(q.shape))
        off += q.size
    return out


def kernel(x, positions, norm_mix, w_in, conv_w, conv_b, dt_bias, a_log, d_skip, ssm_norm, w_out, norm_ffn, w_gate, w_up, w_down, final_norm, loss_target, m_norm_mix, m_w_in, m_conv_w, m_conv_b, m_dt_bias, m_a_log, m_d_skip, m_ssm_norm, m_w_out, m_norm_ffn, m_w_gate, m_w_up, m_w_down, m_final_norm, v_norm_mix, v_w_in, v_conv_w, v_conv_b, v_dt_bias, v_a_log, v_d_skip, v_ssm_norm, v_w_out, v_norm_ffn, v_w_gate, v_w_up, v_w_down, v_final_norm):
    w = dict(norm_mix=norm_mix, w_in=w_in, conv_w=conv_w, conv_b=conv_b, dt_bias=dt_bias, a_log=a_log, d_skip=d_skip, ssm_norm=ssm_norm,
             w_out=w_out, norm_ffn=norm_ffn, w_gate=w_gate, w_up=w_up, w_down=w_down, final_norm=final_norm)
    m = dict(norm_mix=m_norm_mix, w_in=m_w_in, conv_w=m_conv_w, conv_b=m_conv_b, dt_bias=m_dt_bias, a_log=m_a_log, d_skip=m_d_skip,
             ssm_norm=m_ssm_norm, w_out=m_w_out, norm_ffn=m_norm_ffn, w_gate=m_w_gate, w_up=m_w_up, w_down=m_w_down, final_norm=m_final_norm)
    v = dict(norm_mix=v_norm_mix, w_in=v_w_in, conv_w=v_conv_w, conv_b=v_conv_b, dt_bias=v_dt_bias, a_log=v_a_log, d_skip=v_d_skip,
             ssm_norm=v_ssm_norm, w_out=v_w_out, norm_ffn=v_norm_ffn, w_gate=v_w_gate, w_up=v_w_up, w_down=v_w_down, final_norm=v_final_norm)
    ax, ay, ac = lax.axis_index("x"), lax.axis_index("y"), lax.axis_index("c")
    dev = 4 * ax + 2 * ay + ac

    assert DEPTH == 2
    t = x.shape[0] * x.shape[1]
    xf, target = x.reshape(t, D_MODEL), loss_target.reshape(t, D_MODEL)

    def own_slot(block):
        return lax.dynamic_update_slice(lax.empty((N_DEV,) + block.shape[1:], block.dtype), block, (dev,) + (0,) * (block.ndim - 1))

    def layer_shard(arr, k, l):
        return jnp.transpose(arr, (2, 0, 1))[:, l, :] if k in TRANSPOSED else arr[l]

    def gather_start(keys, l, tie, name):
        shards = [(layer_shard(w[keys[0]], keys[0], l) + tie).astype(bf16)] + [layer_shard(w[k], k, l).astype(bf16) for k in keys[1:]]
        return _direct_start(shards, [own_slot(s[None]) for s in shards], False, name)

    def scatter_start(keys, grads_l, l, name):
        shapes = [(w[k].shape[2], w[k].shape[1]) if k in TRANSPOSED else w[k].shape[1:] for k in keys]
        by_dev = [_by_device(k, grads_l[k], sh, l).reshape((N_DEV,) + sh) for k, sh in zip(keys, shapes)]
        return _direct_start(by_dev, [own_slot(lax.dynamic_slice_in_dim(g, dev, 1, 0)) for g in by_dev], True, name)

    (g_in0, conv_all), tie = _allgather_two_level([w["w_in"][0].astype(bf16), w["conv_w"]], "gather_l0_w_in")
    rest0_copy = gather_start(REST, 0, tie[0, 0], "gather_l0_rest_start")
    l1_copy = gather_start(BIG, 1, rest0_copy[4][0, 0], "gather_l1_start")
    p = {k: w[k] for k in SMALL}
    p["norm_mix"] = p["norm_mix"] + l1_copy[4][0, 0]
    p["conv_w"] = jnp.transpose(conv_all, (1, 2, 0, 3)).reshape(DEPTH, CONV_WIDTH, CONV_CH)
    sp0, sp1 = _layer_params(p, 0), _layer_params(p, 1)

    def rest0(after):
        lands = _direct_wait(*rest0_copy[:4], after, False, "gather_l0_rest_wait")
        return tuple(_full_from_gathered(k, g, 0) for k, g in zip(REST, lands))

    tabs = _rope_tables(positions.reshape(t, 1), "rope_tables")
    w_in0 = _full_from_gathered("w_in", g_in0, 0)
    h1, saved0 = _layer_fwd(xf, w_in0, rest0, sp0, tabs, 0)
    lands1 = _direct_wait(*l1_copy[:4], h1, False, "gather_l1_wait")
    bigs1 = tuple(_full_from_gathered(k, g, 1) for k, g in zip(BIG, lands1))
    h2, saved1 = _layer_fwd(h1, bigs1[0], bigs1[1:], sp1, tabs, 1)
    dh, dhb, loss_parts, dfn = _final_loss(h2, _row(p["final_norm"]), target, "final_loss")
    loss_local = jnp.sum(loss_parts)

    dh, grads1 = _layer_bwd((dh, dhb), saved1, bigs1, sp1, tabs, 1, gd=bf16)
    l1_grads = scatter_start(BIG, grads1, 1, "scatter_l1_start")
    w_out0, w_gate0, w_up0, w_down0 = saved0["rest"]
    bigs0 = (w_in0, w_out0, w_gate0, w_up0, w_down0 + l1_grads[4][0, 0].astype(bf16))
    ffn0_grads = []

    def after_ffn(grads_ffn):
        ffn0_grads.append(scatter_start(FFN, grads_ffn, 0, "scatter_l0_ffn_start"))
        return ffn0_grads[0][4][0, 0]

    (dx, _), grads0 = _layer_bwd(dh, saved0, bigs0, sp0, tabs, 0, gd=bf16, after_ffn=after_ffn)
    mix0_grads = scatter_start(MIX, grads0, 0, "scatter_l0_mix_start")
    landed = {(k, 1): g for k, g in zip(BIG, _direct_wait(*l1_grads[:4], dx, True, "scatter_l1_wait"))}
    landed.update({(k, 0): g for k, g in zip(FFN, _direct_wait(*ffn0_grads[0][:4], dx, True, "scatter_l0_ffn_wait"))})
    out_g, out_d, out_m, out_v = {}, {}, {}, {}

    def update(keys, after):
        for k in keys:
            parts = [landed[k, l] for l in range(DEPTH)]
            if k in TRANSPOSED:
                depth, a, b = w[k].shape
                state = [jnp.transpose(s, (2, 0, 1)).reshape(b, depth * a) for s in (w[k], m[k], v[k])]
                res = _adamw_layers(parts, *state, after, "adamw_" + k, layers_on_columns=True)
                res = [jnp.transpose(r.reshape(b, depth, a), (1, 2, 0)) for r in res]
            else:
                res = _adamw_layers(parts, w[k], m[k], v[k], after, "adamw_" + k)
            for dst, r in zip((out_g, out_d, out_m, out_v), res):
                dst[k] = r

    update(FFN, mix0_grads[4])
    grads = {k: [grads0[k], grads1[k]] for k in grads0 if k not in BIG}
    grads["final_norm"] = dfn.sum(0) + mix0_grads[4][0, 0]

    small_like = [w[k] for k in SMALL]
    small_grads = [jnp.stack(grads[k]) if k != "final_norm" else grads[k] for k in SMALL]
    small_pack = jnp.concatenate([_pack_rows(small_grads, SMALL_ROWS, LANES), _pack_rows([jnp.stack(grads["conv_w"])], CONVW_ROWS, LANES)], axis=0)
    parts = _allgather_direct(small_pack, "gather_small_grads")
    g_s, d_s, m_s, v_s = _adamw(
        [parts[i, :SMALL_ROWS] for i in range(N_DEV)], _pack_rows(small_like, SMALL_ROWS, LANES),
        _pack_rows([m[k] for k in SMALL], SMALL_ROWS, LANES), _pack_rows([v[k] for k in SMALL], SMALL_ROWS, LANES), "adamw_replicated")
    for dst, src in ((out_g, g_s), (out_d, d_s), (out_m, m_s), (out_v, v_s)):
        dst.update(zip(SMALL, _unpack(src.reshape(-1), small_like)))
    shard_w = conv_w.shape[-1]
    conv_parts = parts[:, SMALL_ROWS:].reshape(N_DEV, DEPTH, CONV_WIDTH, CONV_CH)
    conv_mine = lax.dynamic_slice_in_dim(conv_parts, dev * shard_w, shard_w, axis=3)
    g_c, d_c, m_c, v_c = _adamw(
        [_pack_rows([conv_mine[i]], CONVW_SHARD_ROWS, LANES) for i in range(N_DEV)], _pack_rows([conv_w], CONVW_SHARD_ROWS, LANES),
        _pack_rows([m["conv_w"]], CONVW_SHARD_ROWS, LANES), _pack_rows([v["conv_w"]], CONVW_SHARD_ROWS, LANES), "adamw_conv_w")
    for dst, src in ((out_g, g_c), (out_d, d_c), (out_m, m_c), (out_v, v_c)):
        dst["conv_w"] = src.reshape(-1)[:conv_w.size].reshape(conv_w.shape)

    landed.update({(k, 0): g for k, g in zip(MIX, _direct_wait(*mix0_grads[:4], v_c + out_v["w_down"][0, :CONVW_SHARD_ROWS, :LANES], True, "scatter_l0_mix_wait"))})
    update(MIX, v_c)

    loss = lax.psum(loss_local, ("x", "y", "c"))
    return (loss, dx.reshape(x.shape), *[out_g[k] for k in WEIGHTS], *[out_d[k] for k in WEIGHTS],
            *[out_m[k] for k in WEIGHTS], *[out_v[k] for k in WEIGHTS])
```

```python
import jax
import jax.numpy as jnp
import numpy as np
from jax import lax
from jax.experimental import pallas as pl
from jax.experimental.pallas import tpu as pltpu

f32 = jnp.float32
bf16 = jnp.bfloat16

D_MODEL = 1024
SEQ = 2048
DEPTH = 2
HEAD_DIM = 64
N_ATTN_HEADS = 8
N_KV_HEADS = 2
ATTN_WIDTH = 512
KV_WIDTH = 128
ROPE_DIM = 16
ROPE_THETA = 500000.0
DILATIONS = (1, 4, 16)
ATTN_BLOCK = 128
SSM_HEADS = 16
SSM_INNER = 1024
SSM_GROUPS = 2
D_STATE = 128
CONV_WIDTH = 4
CHUNK = 128
CONV_CH = 1536
MIX_WIDTH = 1536
QKV_WIDTH = ATTN_WIDTH + 2 * KV_WIDTH
DT_OFF = 3328
IN_PROJ = 3344
IN_PROJ_PAD = 3456
FFN_HIDDEN = 2816
EPS = 1e-5
N_DEV = 8
ADAM_LR = 0.001
ADAM_B1 = 0.9
ADAM_B2 = 0.999
ADAM_EPS = 1e-08
ADAM_WD = 0.01
ADAM_STEP = 10

LANES = 128
SUBLANES = 8
VMEM_LIMIT = 56 * 1024 * 1024
VMEM_LIMIT_TWO_PAIRS = 60 * 1024 * 1024

MESH = pl.DeviceIdType.MESH
ANY = pl.BlockSpec(memory_space=pl.ANY)


def _cparams(sem, vmem=None):
    return pltpu.CompilerParams(dimension_semantics=sem, vmem_limit_bytes=vmem or VMEM_LIMIT)


def _sigmoid(x):
    return 1.0 / (1.0 + jnp.exp(-x))


def _silu(x):
    return x * _sigmoid(x)


def _dsilu(x):
    s = _sigmoid(x)
    return s * (1.0 + x * (1.0 - s))


def _silu_and_grad(x):
    s = _sigmoid(x)
    return x * s, s * (1.0 + x * (1.0 - s))


def _softplus(x):
    return jnp.maximum(x, 0.0) + jnp.log(1.0 + jnp.exp(-jnp.abs(x)))


def _dot(a, b, dims, precision=None):
    return lax.dot_general(a, b, (dims, ((), ())), preferred_element_type=f32, precision=precision)


def _nn(a, b, precision=None):
    return _dot(a, b, ((1,), (0,)), precision)


def _nt(a, b):
    return _dot(a, b, ((1,), (1,)))


def _tn(a, b):
    return _dot(a, b, ((0,), (0,)))


def _rowsum8(t):
    n, w = t.shape
    return jnp.sum(t.reshape(n // SUBLANES, SUBLANES, w), axis=0)


def _matmul(a, b, *, mode, n_out=None, b_off=0, add=None, out_dtype=f32, tm=2048, tn=512, tk=1024, name):
    if mode == "tn":
        kk, m = a.shape
    else:
        m, kk = a.shape
    n = n_out if n_out is not None else (b.shape[0] if mode == "nt" else b.shape[1])
    tm, tn, tk = min(tm, m), min(tn, n), min(tk, kk)
    assert m % tm == 0 and n % tn == 0 and kk % tk == 0, (name, m, n, kk, tm, tn, tk)
    nk = kk // tk
    if mode == "nn":
        a_spec = pl.BlockSpec((tm, tk), lambda i, j, k: (i, k))
        b_spec = pl.BlockSpec((tk, tn), lambda i, j, k: (k, j + b_off))
        dims = ((1,), (0,))
    elif mode == "nt":
        a_spec = pl.BlockSpec((tm, tk), lambda i, j, k: (i, k))
        b_spec = pl.BlockSpec((tn, tk), lambda i, j, k: (j + b_off, k))
        dims = ((1,), (1,))
    else:
        a_spec = pl.BlockSpec((tk, tm), lambda i, j, k: (k, i))
        b_spec = pl.BlockSpec((tk, tn), lambda i, j, k: (k, j + b_off))
        dims = ((0,), (0,))
    o_spec = pl.BlockSpec((tm, tn), lambda i, j, k: (i, j))
    has_add = add is not None

    def body(*refs):
        if has_add:
            a_ref, b_ref, add_ref, o_ref, acc_ref = refs
        else:
            a_ref, b_ref, o_ref, acc_ref = refs
        k = pl.program_id(2)
        part = _dot(a_ref[...].astype(bf16), b_ref[...].astype(bf16), dims)

        @pl.when(k == 0)
        def _():
            acc_ref[...] = part

        @pl.when(k > 0)
        def _():
            acc_ref[...] += part

        @pl.when(k == nk - 1)
        def _():
            r = acc_ref[...]
            if has_add:
                r = r + add_ref[...]
            o_ref[...] = r.astype(out_dtype)

    in_specs = [a_spec, b_spec] + ([o_spec] if has_add else [])
    args = (a, b) + ((add,) if has_add else ())
    return pl.pallas_call(
        body, name=name, grid=(m // tm, n // tn, nk), in_specs=in_specs, out_specs=o_spec,
        out_shape=jax.ShapeDtypeStruct((m, n), out_dtype), scratch_shapes=[pltpu.VMEM((tm, tn), f32)],
        compiler_params=_cparams(("parallel", "parallel", "arbitrary")),
    )(*args)


def _in_proj(hn, w_in, widths, name, tm=2048, tn=256):
    m, k = hn.shape
    starts = [sum(widths[:i]) // tn for i in range(len(widths))]
    counts = [wd // tn for wd in widths]
    assert m % tm == 0 and all(wd % tn == 0 for wd in widths)
    n_out = len(widths)

    def body(a_ref, w_ref, *o_refs):
        j = pl.program_id(1)
        acc = _nn(a_ref[...], w_ref[...])
        for s, c, o_ref in zip(starts, counts, o_refs):
            @pl.when((j >= s) & (j < s + c))
            def _(o_ref=o_ref):
                o_ref[...] = acc

    def o_spec(s, c):
        return pl.BlockSpec((tm, tn), lambda i, j: (i, jnp.clip(j - s, 0, c - 1)))

    return pl.pallas_call(
        body, name=name, grid=(m // tm, sum(counts)),
        in_specs=[pl.BlockSpec((tm, k), lambda i, j: (i, 0)), pl.BlockSpec((k, tn), lambda i, j: (0, j))],
        out_specs=[o_spec(s, c) for s, c in zip(starts, counts)],
        out_shape=[jax.ShapeDtypeStruct((m, wd), f32) for wd in widths], compiler_params=_cparams(("parallel", "arbitrary")),
    )(hn, w_in)


def _out_proj(o, yn, w_out, h, name, tm=2048, tn=512):
    m, kb = o.shape
    n = w_out.shape[1]
    n_y = yn.shape[1] // kb
    assert yn.shape[1] % kb == 0 and w_out.shape[0] == kb * (1 + n_y) and m % tm == 0 and n % tn == 0

    def body(*refs):
        o_ref, y_refs, w_refs, h_ref, out_ref = refs[0], refs[1:1 + n_y], refs[1 + n_y:2 + 2 * n_y], refs[-2], refs[-1]
        acc = h_ref[...] + _nn(o_ref[...].astype(bf16), w_refs[0][...])
        for y_ref, w_ref in zip(y_refs, w_refs[1:]):
            acc = acc + _nn(y_ref[...], w_ref[...])
        out_ref[...] = acc

    res = pl.BlockSpec((tm, tn), lambda i, j: (i, j))

    def a_blk(c):
        return pl.BlockSpec((tm, kb), lambda i, j: (i, c))

    def w_blk(r):
        return pl.BlockSpec((kb, tn), lambda i, j: (r, j))

    return pl.pallas_call(
        body, name=name, grid=(m // tm, n // tn),
        in_specs=[a_blk(0)] + [a_blk(c) for c in range(n_y)] + [w_blk(r) for r in range(1 + n_y)] + [res],
        out_specs=res, out_shape=jax.ShapeDtypeStruct((m, n), f32), compiler_params=_cparams(("parallel", "parallel")),
    )(o, *[yn] * n_y, *[w_out] * (1 + n_y), h)


def _swiglu_fwd(hn, w_gate, w_up, name, tm=2048, tn=256):
    m, k = hn.shape
    n = w_gate.shape[0]
    assert m % tm == 0 and n % tn == 0, (name, m, n, tm, tn)

    def body(a_ref, wg_ref, wu_ref, g_ref, u_ref, act_ref):
        a = a_ref[...]
        g = _nt(a, wg_ref[...])
        u = _nt(a, wu_ref[...])
        sg, dsg = _silu_and_grad(g)
        g_ref[...] = (u * dsg).astype(bf16)
        u_ref[...] = sg.astype(bf16)
        act_ref[...] = (sg * u).astype(bf16)

    a_spec = pl.BlockSpec((tm, k), lambda i, j: (i, 0))
    w_spec = pl.BlockSpec((tn, k), lambda i, j: (j, 0))
    o_spec = pl.BlockSpec((tm, tn), lambda i, j: (i, j))
    return pl.pallas_call(
        body, name=name, grid=(m // tm, n // tn), in_specs=[a_spec, w_spec, w_spec], out_specs=[o_spec, o_spec, o_spec],
        out_shape=[jax.ShapeDtypeStruct((m, n), bf16)] * 3,
        compiler_params=_cparams(("parallel", "parallel")),
    )(hn, w_gate, w_up)


def _swiglu_bwd(dh, w_down, g, u, name, tm=2048, tn=256):
    m, k = dh.shape
    n = w_down.shape[0]
    assert m % tm == 0 and n % tn == 0, (name, m, n, tm, tn)

    def body(a_ref, w_ref, g_ref, u_ref, dg_ref, du_ref):
        dact = _nt(a_ref[...].astype(bf16), w_ref[...])
        dg_ref[...] = (dact * g_ref[...].astype(f32)).astype(bf16)
        du_ref[...] = (dact * u_ref[...].astype(f32)).astype(bf16)

    a_spec = pl.BlockSpec((tm, k), lambda i, j: (i, 0))
    w_spec = pl.BlockSpec((tn, k), lambda i, j: (j, 0))
    o_spec = pl.BlockSpec((tm, tn), lambda i, j: (i, j))
    return pl.pallas_call(
        body, name=name, grid=(m // tm, n // tn), in_specs=[a_spec, w_spec, o_spec, o_spec], out_specs=[o_spec, o_spec],
        out_shape=[jax.ShapeDtypeStruct((m, n), bf16), jax.ShapeDtypeStruct((m, n), bf16)],
        compiler_params=_cparams(("parallel", "parallel")),
    )(dh, w_down, g, u)


def _rmsnorm_fwd(h, w, name, tm=512):
    m, d = h.shape

    def body(h_ref, w_ref, o_ref):
        x = h_ref[...]
        r = lax.rsqrt(jnp.mean(x * x, axis=-1, keepdims=True) + EPS)
        o_ref[...] = (x * r * w_ref[...]).astype(bf16)

    return pl.pallas_call(
        body, name=name, grid=(m // tm,),
        in_specs=[pl.BlockSpec((tm, d), lambda i: (i, 0)), pl.BlockSpec((1, d), lambda i: (0, 0))],
        out_specs=pl.BlockSpec((tm, d), lambda i: (i, 0)), out_shape=jax.ShapeDtypeStruct((m, d), bf16),
        compiler_params=_cparams(("parallel",)),
    )(h, w)


def _nt_norm_bwd(pairs, h, w, dres, name, tk, b_is_kd=False, bf16_copy=True, tm=1024, vmem=None):
    m, d = h.shape
    contract = _nn if b_is_kd else _nt
    steps = [p[0].shape[1] // tk for p in pairs]
    assert all(p[0].shape[1] % tk == 0 for p in pairs), (name, tk)
    starts = [sum(steps[:i]) for i in range(len(pairs))]
    nk = sum(steps)
    n_p = len(pairs)

    def body(*refs):
        ab = refs[:2 * n_p]
        h_ref, w_ref, dres_ref, dh_ref = refs[2 * n_p:2 * n_p + 4]
        dhb_ref = refs[2 * n_p + 4] if bf16_copy else None
        dw_ref, acc_ref = refs[-2:]
        i, k = pl.program_id(0), pl.program_id(1)

        @pl.when(k == 0)
        def _():
            acc_ref[...] = jnp.zeros_like(acc_ref)

        for p in range(n_p):
            @pl.when((k >= starts[p]) & (k < starts[p] + steps[p]))
            def _(p=p):
                acc_ref[...] += contract(ab[2 * p][...], ab[2 * p + 1][...])

        @pl.when(k == nk - 1)
        def _():
            x = h_ref[...]
            r = lax.rsqrt(jnp.mean(x * x, axis=-1, keepdims=True) + EPS)
            xhat = x * r
            dy = acc_ref[...]
            gw = dy * w_ref[...]
            dh = dres_ref[...] + r * (gw - xhat * jnp.mean(gw * xhat, axis=-1, keepdims=True))
            dh_ref[...] = dh
            if bf16_copy:
                dhb_ref[...] = dh.astype(bf16)
            part = _rowsum8(dy * xhat)

            @pl.when(i == 0)
            def _():
                dw_ref[...] = part

            @pl.when(i > 0)
            def _():
                dw_ref[...] += part

    def clamp(k, p):
        return jnp.clip(k - starts[p], 0, steps[p] - 1)

    in_specs = []
    for p in range(n_p):
        b_spec = (pl.BlockSpec((tk, d), lambda i, k, p=p: (clamp(k, p), 0)) if b_is_kd
                  else pl.BlockSpec((d, tk), lambda i, k, p=p: (0, clamp(k, p))))
        in_specs += [pl.BlockSpec((tm, tk), lambda i, k, p=p: (i, clamp(k, p))), b_spec]
    row = pl.BlockSpec((tm, d), lambda i, k: (i, 0))
    in_specs += [row, pl.BlockSpec((1, d), lambda i, k: (0, 0)), row]
    return pl.pallas_call(
        body, name=name, grid=(m // tm, nk), in_specs=in_specs,
        out_specs=[row] + [row] * bf16_copy + [pl.BlockSpec((SUBLANES, d), lambda i, k: (0, 0))],
        out_shape=[jax.ShapeDtypeStruct((m, d), f32)] + [jax.ShapeDtypeStruct((m, d), bf16)] * bf16_copy + [jax.ShapeDtypeStruct((SUBLANES, d), f32)],
        scratch_shapes=[pltpu.VMEM((tm, d), f32)], compiler_params=_cparams(("arbitrary", "arbitrary"), vmem),
    )(*[t for p in pairs for t in p], h, w, dres)


def _final_loss(h, w, target, name, tm=512):
    m, d = h.shape

    def body(h_ref, w_ref, t_ref, dh_ref, dhb_ref, loss_ref, dw_ref):
        x = h_ref[...]
        r = lax.rsqrt(jnp.mean(x * x, axis=-1, keepdims=True) + EPS)
        xhat = x * r
        ww = w_ref[...]
        err = xhat * ww - t_ref[...]
        dy = err * (1.0 / d)
        gw = dy * ww
        dh = r * (gw - xhat * jnp.mean(gw * xhat, axis=-1, keepdims=True))
        dh_ref[...] = dh
        dhb_ref[...] = dh.astype(bf16)
        lpart = _rowsum8(err * err) * (0.5 / d)
        wpart = _rowsum8(dy * xhat)

        @pl.when(pl.program_id(0) == 0)
        def _():
            loss_ref[...] = lpart
            dw_ref[...] = wpart

        @pl.when(pl.program_id(0) > 0)
        def _():
            loss_ref[...] += lpart
            dw_ref[...] += wpart

    row = pl.BlockSpec((tm, d), lambda i: (i, 0))
    acc = pl.BlockSpec((SUBLANES, d), lambda i: (0, 0))
    return pl.pallas_call(
        body, name=name, grid=(m // tm,),
        in_specs=[row, pl.BlockSpec((1, d), lambda i: (0, 0)), row], out_specs=[row, row, acc, acc],
        out_shape=[jax.ShapeDtypeStruct((m, d), f32), jax.ShapeDtypeStruct((m, d), bf16),
                   jax.ShapeDtypeStruct((SUBLANES, d), f32), jax.ShapeDtypeStruct((SUBLANES, d), f32)],
        compiler_params=_cparams(("arbitrary",)),
    )(h, w, target)


def _lane_tables():
    f = np.arange(LANES) % HEAD_DIM
    inv = ROPE_THETA ** (-jnp.arange(0, ROPE_DIM, 2, dtype=f32) / ROPE_DIM)
    invf = jnp.where(f < ROPE_DIM, inv[f % (ROPE_DIM // 2)], 0.0).astype(f32)
    return invf.reshape(1, LANES)


def _rope_tables(pos_col, name):
    t = pos_col.shape[0]
    tm = SEQ

    def body(p_ref, f_ref, c_ref, s1_ref, s2_ref):
        ang = p_ref[...].astype(f32) * f_ref[...]
        co, si = jnp.cos(ang), jnp.sin(ang)
        f = lax.broadcasted_iota(jnp.int32, (tm, LANES), 1) % HEAD_DIM
        c_ref[...] = jnp.where(f < ROPE_DIM, co, 1.0)
        s1_ref[...] = jnp.where(f < ROPE_DIM // 2, -si, 0.0)
        s2_ref[...] = jnp.where((f >= ROPE_DIM // 2) & (f < ROPE_DIM), si, 0.0)

    row = pl.BlockSpec((tm, LANES), lambda i: (i, 0))
    return pl.pallas_call(
        body, name=name, grid=(t // tm,),
        in_specs=[pl.BlockSpec((tm, 1), lambda i: (i, 0)), pl.BlockSpec((1, LANES), lambda i: (0, 0))],
        out_specs=[row, row, row], out_shape=[jax.ShapeDtypeStruct((t, LANES), f32)] * 3,
        compiler_params=_cparams(("parallel",)),
    )(pos_col, _lane_tables())


def _rot(x, c, s1, s2):
    return x * c + pltpu.roll(x, LANES - ROPE_DIM // 2, 1) * s1 + pltpu.roll(x, ROPE_DIM // 2, 1) * s2


def _rot_t(g, c, s1, s2):
    return g * c + pltpu.roll(g * s1, ROPE_DIM // 2, 1) + pltpu.roll(g * s2, LANES - ROPE_DIM // 2, 1)


def _dup_head(x, kvh, low):
    a = jnp.where(kvh == 0, x, pltpu.roll(x, HEAD_DIM, 1))
    return jnp.where(low, a, pltpu.roll(a, HEAD_DIM, 1))


def _deinterleave(src_ref, dst_ref, d, dtype):
    length = SEQ // d
    if d == 1:
        dst_ref[...] = src_ref[...].astype(dtype)
    else:
        for r in range(d):
            dst_ref[pl.ds(r * length, length), :] = src_ref[pl.ds(r, length, stride=d), :].astype(dtype)


def _interleave_store(src_ref, dst_ref, d, accumulate):
    length = SEQ // d
    if d == 1:
        if accumulate:
            dst_ref[...] += src_ref[...]
        else:
            dst_ref[...] = src_ref[...]
    else:
        for r in range(d):
            blk = src_ref[pl.ds(r * length, length), :]
            if accumulate:
                dst_ref[pl.ds(r, length, stride=d), :] = dst_ref[pl.ds(r, length, stride=d), :] + blk
            else:
                dst_ref[pl.ds(r, length, stride=d), :] = blk


def _attn_masks():
    qi = lax.broadcasted_iota(jnp.int32, (ATTN_BLOCK, ATTN_BLOCK), 0)
    ki = lax.broadcasted_iota(jnp.int32, (ATTN_BLOCK, ATTN_BLOCK), 1)
    low = lax.broadcasted_iota(jnp.int32, (ATTN_BLOCK, LANES), 1) < HEAD_DIM
    return ki <= qi, ki >= qi, low


NEG_INF = float("-inf")
ATTN_UNROLL = 8
SOFTMAX_UNROLL = 4


N_BRANCH = len(DILATIONS)


def _attn_prep(qkv, tabs, name):
    t = qkv.shape[0]
    nb = t // SEQ
    n_j = ATTN_WIDTH // LANES

    def q_body(q_ref, c_ref, s1_ref, s2_ref, out_ref, xr):
        xr[...] = _rot(q_ref[...], c_ref[...], s1_ref[...], s2_ref[...]) * (HEAD_DIM ** -0.5)
        for bi, d in enumerate(DILATIONS):
            _deinterleave(xr, out_ref.at[bi], d, bf16)

    def kv_body(x_ref, c_ref, s1_ref, s2_ref, out_ref, xr):
        lowfull = lax.broadcasted_iota(jnp.int32, (SEQ, LANES), 1) < HEAD_DIM
        x = x_ref[...]
        x = jnp.where(pl.program_id(1) == 0, _rot(x, c_ref[...], s1_ref[...], s2_ref[...]), x)
        for kvh in range(N_KV_HEADS):
            xr[...] = _dup_head(x, kvh, lowfull)
            for bi, d in enumerate(DILATIONS):
                length = SEQ // d
                for r in range(d):
                    rows = xr[...] if d == 1 else xr[pl.ds(r, length, stride=d), :]
                    out_ref[0, bi, pl.ds(r * length, length), kvh * LANES:(kvh + 1) * LANES] = rows.astype(bf16)

    tab = pl.BlockSpec((SEQ, LANES), lambda b, j: (b, 0))
    q = pl.pallas_call(
        q_body, name=name + "_q", grid=(nb, n_j),
        in_specs=[pl.BlockSpec((SEQ, LANES), lambda b, j: (b, j)), tab, tab, tab],
        out_specs=pl.BlockSpec((N_BRANCH, SEQ, LANES), lambda b, j: (0, b, j)),
        out_shape=jax.ShapeDtypeStruct((N_BRANCH, t, ATTN_WIDTH), bf16), scratch_shapes=[pltpu.VMEM((SEQ, LANES), f32)],
        compiler_params=_cparams(("parallel", "parallel")),
    )(qkv, *tabs)
    kv = pl.pallas_call(
        kv_body, name=name + "_kv", grid=(nb, 2),
        in_specs=[pl.BlockSpec((SEQ, LANES), lambda b, j: (b, n_j + j)), tab, tab, tab],
        out_specs=pl.BlockSpec((1, N_BRANCH, SEQ, N_KV_HEADS * LANES), lambda b, j: (j, 0, b, 0)),
        out_shape=jax.ShapeDtypeStruct((2, N_BRANCH, t, N_KV_HEADS * LANES), bf16), scratch_shapes=[pltpu.VMEM((SEQ, LANES), f32)],
        compiler_params=_cparams(("parallel", "parallel")),
    )(qkv, *tabs)
    return q, kv


def _attn_fwd(prep, name):
    q_all, kv_all = prep
    t = q_all.shape[1]
    nb = t // SEQ
    n_blk = SEQ // ATTN_BLOCK

    def body(q_ref, k_ref, v_ref, o_ref, lse_ref, ob, lb, o0, o1, o2, l0, l1, l2, ss):
        cur_ok, prev_ok, low = _attn_masks()
        onat, lnat = (o0, o1, o2), (l0, l1, l2)
        for bi, d in enumerate(DILATIONS):
            qd, kd, vd = q_ref.at[bi], k_ref.at[0, bi], v_ref.at[0, bi]
            per_res = n_blk // d
            use_prev = per_res > 1

            def scores(n, carry):
                start = pl.multiple_of(n * ATTN_BLOCK, ATTN_BLOCK)
                has_prev = (n % per_res) != 0
                pstart = pl.multiple_of(jnp.maximum(n - 1, 0) * ATTN_BLOCK, ATTN_BLOCK)
                qb = qd[pl.ds(start, ATTN_BLOCK), :]
                kc = kd[pl.ds(start, ATTN_BLOCK), :]
                if use_prev:
                    kp = kd[pl.ds(pstart, ATTN_BLOCK), :]
                for a in range(2):
                    qa = jnp.where(low if a == 0 else ~low, qb, jnp.zeros_like(qb))
                    ss[2 * n + a, :, 0:ATTN_BLOCK] = jnp.where(cur_ok, _nt(qa, kc), NEG_INF)
                    if use_prev:
                        ss[2 * n + a, :, ATTN_BLOCK:2 * ATTN_BLOCK] = jnp.where(prev_ok & has_prev, _nt(qa, kp), NEG_INF)
                return carry

            def softmax_pv(n, carry):
                start = pl.multiple_of(n * ATTN_BLOCK, ATTN_BLOCK)
                pstart = pl.multiple_of(jnp.maximum(n - 1, 0) * ATTN_BLOCK, ATTN_BLOCK)
                vc = vd[pl.ds(start, ATTN_BLOCK), :]
                if use_prev:
                    vp = vd[pl.ds(pstart, ATTN_BLOCK), :]
                outs, lses = [], []
                for a in range(2):
                    sc = ss[2 * n + a, :, 0:ATTN_BLOCK]
                    if use_prev:
                        sp = ss[2 * n + a, :, ATTN_BLOCK:2 * ATTN_BLOCK]
                        m = jnp.max(jnp.maximum(sc, sp), axis=1, keepdims=True)
                        pc, pp = jnp.exp(sc - m), jnp.exp(sp - m)
                        den = jnp.sum(pc + pp, axis=1, keepdims=True)
                        acc = _nn(pc.astype(bf16), vc) + _nn(pp.astype(bf16), vp)
                    else:
                        m = jnp.max(sc, axis=1, keepdims=True)
                        pc = jnp.exp(sc - m)
                        den = jnp.sum(pc, axis=1, keepdims=True)
                        acc = _nn(pc.astype(bf16), vc)
                    outs.append(acc * (1.0 / den))
                    lses.append(m + jnp.log(den))
                ob[pl.ds(start, ATTN_BLOCK), :] = jnp.where(low, outs[0], outs[1])
                lb[pl.ds(start, ATTN_BLOCK), :] = jnp.where(low, lses[0], lses[1])
                return carry

            lax.fori_loop(0, n_blk, scores, 0, unroll=ATTN_UNROLL)
            lax.fori_loop(0, n_blk, softmax_pv, 0, unroll=SOFTMAX_UNROLL)
            _interleave_store(ob, onat[bi], d, False)
            _interleave_store(lb, lnat[bi], d, False)
        la, lbb, lc = l0[...], l1[...], l2[...]
        lm = jnp.maximum(jnp.maximum(la, lbb), lc)
        wa, wb, wc = jnp.exp(la - lm), jnp.exp(lbb - lm), jnp.exp(lc - lm)
        ws = wa + wb + wc
        o_ref[...] = (wa * o0[...] + wb * o1[...] + wc * o2[...]) / ws
        lse_ref[...] = lm + jnp.log(ws)

    def col(jj):
        return pl.BlockSpec((SEQ, LANES), lambda b, j: (b, jj if jj is not None else j))

    fs = pltpu.VMEM((SEQ, LANES), f32)
    return pl.pallas_call(
        body, name=name, grid=(nb, ATTN_WIDTH // LANES),
        in_specs=[pl.BlockSpec((N_BRANCH, SEQ, LANES), lambda b, j: (0, b, j)),
                  pl.BlockSpec((1, N_BRANCH, SEQ, LANES), lambda b, j: (0, 0, b, j // 2)),
                  pl.BlockSpec((1, N_BRANCH, SEQ, LANES), lambda b, j: (1, 0, b, j // 2))],
        out_specs=[col(None), col(None)],
        out_shape=[jax.ShapeDtypeStruct((t, ATTN_WIDTH), f32), jax.ShapeDtypeStruct((t, ATTN_WIDTH), f32)],
        scratch_shapes=[fs, fs, fs, fs, fs, fs, fs, fs, pltpu.VMEM((2 * n_blk, ATTN_BLOCK, 2 * ATTN_BLOCK), f32)],
        compiler_params=_cparams(("parallel", "parallel")),
    )(q_all, kv_all, kv_all)


def _attn_bwd(prep, tabs, o, lse, do, name):
    q_all, kv_all = prep
    t = q_all.shape[1]
    nb = t // SEQ
    n_blk = SEQ // ATTN_BLOCK
    n_j = ATTN_WIDTH // LANES

    def body(q_ref, k_ref, v_ref, c_ref, s1_ref, s2_ref, o_ref, lse_ref, do_ref, dq_ref, dk_ref, dv_ref,
             dl, dod, lsd, dld, dqd, dkd, dvd, dqa, dka, dva, pb, dsb, dk_acc, dv_acc):
        j = pl.program_id(1)
        pb[2 * n_blk:2 * n_blk + 2] = jnp.zeros((2, ATTN_BLOCK, 2 * ATTN_BLOCK), bf16)
        dsb[2 * n_blk:2 * n_blk + 2] = jnp.zeros((2, ATTN_BLOCK, 2 * ATTN_BLOCK), bf16)
        kvh = j // 2
        cur_ok, prev_ok, low = _attn_masks()
        lowfull = lax.broadcasted_iota(jnp.int32, (SEQ, LANES), 1) < HEAD_DIM
        c, s1, s2 = c_ref[...], s1_ref[...], s2_ref[...]
        prod = do_ref[...] * o_ref[...]
        d_lo = jnp.sum(jnp.where(lowfull, prod, 0.0), axis=1, keepdims=True)
        d_hi = jnp.sum(jnp.where(lowfull, 0.0, prod), axis=1, keepdims=True)
        dl[...] = jnp.where(lowfull, d_lo, d_hi)
        dqa[...] = jnp.zeros_like(dqa)
        dka[...] = jnp.zeros_like(dka)
        dva[...] = jnp.zeros_like(dva)
        for bi, d in enumerate(DILATIONS):
            qd, kd, vd = q_ref.at[bi], k_ref.at[0, bi], v_ref.at[0, bi]
            _deinterleave(do_ref, dod, d, bf16)
            _deinterleave(lse_ref, lsd, d, f32)
            _deinterleave(dl, dld, d, f32)
            per_res = n_blk // d
            use_prev = per_res > 1
            curl, prevl = slice(0, ATTN_BLOCK), slice(ATTN_BLOCK, 2 * ATTN_BLOCK)

            def halves(x):
                zero = jnp.zeros_like(x)
                return jnp.where(low, x, zero), jnp.where(low, zero, x)

            def probs(n, carry):
                start = pl.multiple_of(n * ATTN_BLOCK, ATTN_BLOCK)
                has_prev = (n % per_res) != 0
                pstart = pl.multiple_of(jnp.maximum(n - 1, 0) * ATTN_BLOCK, ATTN_BLOCK)
                cur, prev = pl.ds(start, ATTN_BLOCK), pl.ds(pstart, ATTN_BLOCK)
                qas, doas = halves(qd[cur, :]), halves(dod[cur, :])
                kc, vc = kd[cur, :], vd[cur, :]
                if use_prev:
                    kp, vp = kd[prev, :], vd[prev, :]
                lsb, dlb = lsd[cur, :], dld[cur, :]
                for a in range(2):
                    ls = lsb[:, a * HEAD_DIM:a * HEAD_DIM + 1]
                    de = dlb[:, a * HEAD_DIM:a * HEAD_DIM + 1]
                    pc = jnp.exp(jnp.where(cur_ok, _nt(qas[a], kc), NEG_INF) - ls)
                    pb[2 * n + a, :, curl] = pc.astype(bf16)
                    dsb[2 * n + a, :, curl] = (pc * (_nt(doas[a], vc) - de)).astype(bf16)
                    if use_prev:
                        pp = jnp.exp(jnp.where(prev_ok & has_prev, _nt(qas[a], kp), NEG_INF) - ls)
                        pb[2 * n + a, :, prevl] = pp.astype(bf16)
                        dsb[2 * n + a, :, prevl] = (pp * (_nt(doas[a], vp) - de)).astype(bf16)
                return carry

            def grads(n, carry):
                start = pl.multiple_of(n * ATTN_BLOCK, ATTN_BLOCK)
                pstart = pl.multiple_of(jnp.maximum(n - 1, 0) * ATTN_BLOCK, ATTN_BLOCK)
                nstart = pl.multiple_of(jnp.minimum(n + 1, n_blk - 1) * ATTN_BLOCK, ATTN_BLOCK)
                cur, prev, nxt = pl.ds(start, ATTN_BLOCK), pl.ds(pstart, ATTN_BLOCK), pl.ds(nstart, ATTN_BLOCK)
                kc = kd[cur, :]
                dqs = [_nn(dsb[2 * n + a, :, curl], kc) for a in range(2)]
                q_rows, do_rows = list(halves(qd[cur, :])), list(halves(dod[cur, :]))
                ds_rows, p_rows = [dsb[2 * n + a, :, curl] for a in range(2)], [pb[2 * n + a, :, curl] for a in range(2)]
                if use_prev:
                    kp = kd[prev, :]
                    dqs = [dqs[a] + _nn(dsb[2 * n + a, :, prevl], kp) for a in range(2)]
                    q_rows += list(halves(qd[nxt, :]))
                    do_rows += list(halves(dod[nxt, :]))
                    ds_rows += [dsb[2 * n + 2 + a, :, prevl] for a in range(2)]
                    p_rows += [pb[2 * n + 2 + a, :, prevl] for a in range(2)]
                dqd[cur, :] = jnp.where(low, dqs[0], dqs[1])
                dkd[cur, :] = _tn(jnp.concatenate(ds_rows, axis=0), jnp.concatenate(q_rows, axis=0))
                dvd[cur, :] = _tn(jnp.concatenate(p_rows, axis=0), jnp.concatenate(do_rows, axis=0))
                return carry

            lax.fori_loop(0, n_blk, probs, 0, unroll=n_blk)
            lax.fori_loop(0, n_blk, grads, 0, unroll=n_blk)
            _interleave_store(dqd, dqa, d, True)
            _interleave_store(dkd, dka, d, True)
            _interleave_store(dvd, dva, d, True)
        dq_ref[...] = _rot_t(dqa[...] * (HEAD_DIM ** -0.5), c, s1, s2).astype(bf16)
        dkf = dka[...]
        dkf = _rot_t(dkf + pltpu.roll(dkf, HEAD_DIM, 1), c, s1, s2)
        dvf = dva[...]
        dvf = dvf + pltpu.roll(dvf, HEAD_DIM, 1)
        mine = (lax.broadcasted_iota(jnp.int32, (SEQ, LANES), 1) // HEAD_DIM) == kvh
        dkc_, dvc_ = jnp.where(mine, dkf, 0.0), jnp.where(mine, dvf, 0.0)

        @pl.when(j == 0)
        def _():
            dk_acc[...] = dkc_
            dv_acc[...] = dvc_

        @pl.when(j > 0)
        def _():
            dk_acc[...] += dkc_
            dv_acc[...] += dvc_

        @pl.when(j == n_j - 1)
        def _():
            dk_ref[...] = dk_acc[...].astype(bf16)
            dv_ref[...] = dv_acc[...].astype(bf16)

    def col(jj):
        return pl.BlockSpec((SEQ, LANES), lambda b, j: (b, jj if jj is not None else j))

    tab = pl.BlockSpec((SEQ, LANES), lambda b, j: (b, 0))
    fs = pltpu.VMEM((SEQ, LANES), f32)
    hs = pltpu.VMEM((SEQ, LANES), bf16)
    return pl.pallas_call(
        body, name=name, grid=(nb, n_j),
        in_specs=[pl.BlockSpec((N_BRANCH, SEQ, LANES), lambda b, j: (0, b, j)),
                  pl.BlockSpec((1, N_BRANCH, SEQ, LANES), lambda b, j: (0, 0, b, j // 2)),
                  pl.BlockSpec((1, N_BRANCH, SEQ, LANES), lambda b, j: (1, 0, b, j // 2)),
                  tab, tab, tab, col(None), col(None), col(None)],
        out_specs=[col(None), tab, tab],
        out_shape=[jax.ShapeDtypeStruct((t, ATTN_WIDTH), bf16), jax.ShapeDtypeStruct((t, LANES), bf16), jax.ShapeDtypeStruct((t, LANES), bf16)],
        scratch_shapes=[fs, hs, fs, fs, fs, fs, fs, fs, fs, fs,
                        pltpu.VMEM((2 * n_blk + 2, ATTN_BLOCK, 2 * ATTN_BLOCK), bf16), pltpu.VMEM((2 * n_blk + 2, ATTN_BLOCK, 2 * ATTN_BLOCK), bf16), fs, fs],
        compiler_params=_cparams(("parallel", "arbitrary")),
    )(q_all, kv_all, kv_all, *tabs, o, lse, do)


def _tap(w_ref, s):
    return w_ref[CONV_WIDTH - 1 - s:CONV_WIDTH - s, :]


def _conv_pre(x, w_ref, b_ref, row):
    shifted = [x] + [jnp.where(row >= s, pltpu.roll(x, s, 0), 0.0) for s in range(1, CONV_WIDTH)]
    pre = b_ref[...] + _tap(w_ref, 0) * x
    for s in range(1, CONV_WIDTH):
        pre = pre + _tap(w_ref, s) * shifted[s]
    return pre, shifted


def _conv_fwd(x, w, b, name, tc=512):
    t, ch = x.shape

    def body(x_ref, w_ref, b_ref, o_ref):
        row = lax.broadcasted_iota(jnp.int32, (SEQ, tc), 0)
        pre, _ = _conv_pre(x_ref[...], w_ref, b_ref, row)
        o_ref[...] = _silu(pre)

    xs = pl.BlockSpec((SEQ, tc), lambda i, j: (i, j))
    return pl.pallas_call(
        body, name=name, grid=(t // SEQ, ch // tc),
        in_specs=[xs, pl.BlockSpec((CONV_WIDTH, tc), lambda i, j: (0, j)), pl.BlockSpec((1, tc), lambda i, j: (0, j))],
        out_specs=xs, out_shape=jax.ShapeDtypeStruct((t, ch), f32),
        compiler_params=_cparams(("parallel", "parallel")),
    )(x, w, b)


def _conv_bwd(x, w, b, dact, name, tc=512):
    t, ch = x.shape

    def body(x_ref, w_ref, b_ref, d_ref, dx_ref, dw_ref, db_ref):
        row = lax.broadcasted_iota(jnp.int32, (SEQ, tc), 0)
        pre, shifted = _conv_pre(x_ref[...], w_ref, b_ref, row)
        dpre = d_ref[...] * _dsilu(pre)
        dx = _tap(w_ref, 0) * dpre
        for s in range(1, CONV_WIDTH):
            dx = dx + _tap(w_ref, s) * jnp.where(row < SEQ - s, pltpu.roll(dpre, SEQ - s, 0), 0.0)
        dx_ref[...] = dx.astype(bf16)
        first = pl.program_id(1) == 0
        parts = [jnp.sum(dpre * shifted[CONV_WIDTH - 1 - k], axis=0, keepdims=True) for k in range(CONV_WIDTH)]
        dbp = jnp.sum(dpre, axis=0, keepdims=True)

        @pl.when(first)
        def _():
            for k in range(CONV_WIDTH):
                dw_ref[k:k + 1, :] = parts[k]
            db_ref[...] = dbp

        @pl.when(jnp.logical_not(first))
        def _():
            for k in range(CONV_WIDTH):
                dw_ref[k:k + 1, :] += parts[k]
            db_ref[...] += dbp

    xs = pl.BlockSpec((SEQ, tc), lambda j, i: (i, j))
    ws = pl.BlockSpec((CONV_WIDTH, tc), lambda j, i: (0, j))
    bs = pl.BlockSpec((1, tc), lambda j, i: (0, j))
    return pl.pallas_call(
        body, name=name, grid=(ch // tc, t // SEQ),
        in_specs=[xs, ws, bs, xs], out_specs=[xs, ws, bs],
        out_shape=[jax.ShapeDtypeStruct((t, ch), bf16), jax.ShapeDtypeStruct((CONV_WIDTH, ch), f32), jax.ShapeDtypeStruct((1, ch), f32)],
        compiler_params=_cparams(("parallel", "arbitrary")),
    )(x, w, b, dact)


GROUP_W = SSM_INNER // SSM_GROUPS
HEADS_PER_GROUP = SSM_HEADS // SSM_GROUPS


def _split3(x):
    hi = x.astype(bf16)
    r1 = x - hi.astype(f32)
    mid = r1.astype(bf16)
    lo = (r1 - mid.astype(f32)).astype(bf16)
    return hi, mid, lo


def _dot_exact(x, sel, dims, x_is_lhs=True):
    parts = _split3(x)
    if x_is_lhs:
        return _dot(parts[0], sel, dims) + _dot(parts[1], sel, dims) + _dot(parts[2], sel, dims)
    return _dot(sel, parts[0], dims) + _dot(sel, parts[1], dims) + _dot(sel, parts[2], dims)


def _ssd_common(xbc_ref, dt_ref, bias_ref, alog_ref):
    r = lax.broadcasted_iota(jnp.int32, (CHUNK, CHUNK), 0)
    cidx = lax.broadcasted_iota(jnp.int32, (CHUNK, CHUNK), 1)
    causal = r >= cidx
    tril = causal.astype(bf16)
    expand = (lax.broadcasted_iota(jnp.int32, (CHUNK, SSM_INNER), 0)
              == lax.broadcasted_iota(jnp.int32, (CHUNK, SSM_INNER), 1) // HEAD_DIM).astype(bf16)
    head_lane = cidx < SSM_HEADS
    dtp = dt_ref[...] + bias_ref[...]
    dt = jnp.where(head_lane, _softplus(dtp), 0.0)
    a_neg = -jnp.exp(alog_ref[...])
    a = dt * a_neg
    nn_dims = ((1,), (0,))
    cs = _dot_exact(a, tril, nn_dims, x_is_lhs=False)
    dt_e = _dot_exact(dt, expand, nn_dims)
    cs_e = _dot_exact(cs, expand, nn_dims)
    xs = xbc_ref[:, 0:SSM_INNER]
    xg = xs * dt_e
    ecs = jnp.exp(cs_e)
    cs_last = cs_e[CHUNK - 1:CHUNK, :]
    dse = jnp.exp(cs_last - cs_e)
    cde = jnp.exp(cs_last)
    return dict(r=r, cidx=cidx, causal=causal, tril=tril, expand=expand, head_lane=head_lane, dtp=dtp, dt=dt, a_neg=a_neg,
                cs=cs, cst=cs.T, dt_e=dt_e, cs_e=cs_e, xs=xs, xg=xg, ecs=ecs, dse=dse, cde=cde)


def _decay_mat(q, h):
    return jnp.exp(jnp.where(q["causal"], q["cs"][:, h:h + 1] - q["cst"][h:h + 1, :], NEG_INF))


def _gate_norm(y, z, nw, gate=None):
    y2 = y * (_silu(z) if gate is None else gate)
    outs, xhats, rs = [], [], []
    for g in range(SSM_GROUPS):
        sl = slice(g * GROUP_W, (g + 1) * GROUP_W)
        yg = y2[:, sl]
        r = lax.rsqrt(jnp.mean(yg * yg, axis=-1, keepdims=True) + EPS)
        xhats.append(yg * r)
        rs.append(r)
        outs.append(yg * r * nw[:, sl])
    return y2, outs, xhats, rs


def _ssd_fwd(xbc, z, dtp, params, name):
    t = xbc.shape[0]
    n_chunk = SEQ // CHUNK
    low = None

    def body(xbc_ref, z_ref, dt_ref, bias_ref, alog_ref, dskip_ref, nw_ref, yn_ref, y_ref, hs_ref, h_scr):
        @pl.when(pl.program_id(1) == 0)
        def _():
            h_scr[...] = jnp.zeros_like(h_scr)

        q = _ssd_common(xbc_ref, dt_ref, bias_ref, alog_ref)
        low = lax.broadcasted_iota(jnp.int32, (CHUNK, LANES), 1) < HEAD_DIM
        xgb = q["xg"].astype(bf16)
        wst = (q["xg"] * q["dse"]).astype(bf16)
        hs_ref[0] = h_scr[...]
        ys = []
        for g in range(SSM_GROUPS):
            gl = slice(g * GROUP_W, (g + 1) * GROUP_W)
            bg = xbc_ref[:, SSM_INNER + g * D_STATE:SSM_INNER + (g + 1) * D_STATE].astype(bf16)
            cg = xbc_ref[:, SSM_INNER + SSM_GROUPS * D_STATE + g * D_STATE:SSM_INNER + SSM_GROUPS * D_STATE + (g + 1) * D_STATE].astype(bf16)
            cb = _nt(cg, bg)
            hg = h_scr[g]
            yoff = _nn(cg, hg.astype(bf16)) * q["ecs"][:, gl]
            pieces = []
            for i in range(HEADS_PER_GROUP // 2):
                h0 = g * HEADS_PER_GROUP + 2 * i
                xp = xgb[:, h0 * HEAD_DIM:(h0 + 2) * HEAD_DIM]
                m0 = (cb * _decay_mat(q, h0)).astype(bf16)
                m1 = (cb * _decay_mat(q, h0 + 1)).astype(bf16)
                zero = jnp.zeros_like(xp)
                pieces.append(_nn(m0, jnp.where(low, xp, zero)) + _nn(m1, jnp.where(low, zero, xp)))
            ys.append(jnp.concatenate(pieces, axis=1) + yoff + dskip_ref[:, gl] * q["xs"][:, gl])
            h_scr[g] = hg * q["cde"][:, gl] + _tn(bg, wst[:, gl])
        y = jnp.concatenate(ys, axis=1)
        y_ref[...] = y
        _, outs, _, _ = _gate_norm(y, z_ref[...], nw_ref[...])
        yn_ref[...] = jnp.concatenate(outs, axis=1).astype(bf16)

    def rows(w):
        return pl.BlockSpec((CHUNK, w), lambda b, c: (b * n_chunk + c, 0))

    def par(w):
        return pl.BlockSpec((1, w), lambda b, c: (0, 0))

    return pl.pallas_call(
        body, name=name, grid=(t // SEQ, n_chunk),
        in_specs=[rows(CONV_CH), rows(SSM_INNER), rows(LANES), par(LANES), par(LANES), par(SSM_INNER), par(SSM_INNER)],
        out_specs=[rows(SSM_INNER), rows(SSM_INNER), pl.BlockSpec((1, SSM_GROUPS, D_STATE, GROUP_W), lambda b, c: (b * n_chunk + c, 0, 0, 0))],
        out_shape=[jax.ShapeDtypeStruct((t, SSM_INNER), bf16), jax.ShapeDtypeStruct((t, SSM_INNER), f32),
                   jax.ShapeDtypeStruct((t // CHUNK, SSM_GROUPS, D_STATE, GROUP_W), f32)],
        scratch_shapes=[pltpu.VMEM((SSM_GROUPS, D_STATE, GROUP_W), f32)],
        compiler_params=_cparams(("parallel", "arbitrary")),
    )(xbc, z, dtp, *params)


def _ssd_bwd(xbc, z, dtp, y, hs, dyn, params, name):
    t = xbc.shape[0]
    n_chunk = SEQ // CHUNK

    def body(xbc_ref, z_ref, dt_ref, y_ref, hs_ref, dyn_ref, bias_ref, alog_ref, dskip_ref, nw_ref,
             dxbc_ref, dz_ref, ddt_ref, dnw_ref, dds_ref, dal_ref, dbi_ref, dh_scr):
        @pl.when(pl.program_id(1) == 0)
        def _():
            dh_scr[...] = jnp.zeros_like(dh_scr)

        q = _ssd_common(xbc_ref, dt_ref, bias_ref, alog_ref)
        low = lax.broadcasted_iota(jnp.int32, (CHUNK, LANES), 1) < HEAD_DIM
        last_row = lax.broadcasted_iota(jnp.int32, (CHUNK, GROUP_W), 0) == CHUNK - 1
        xs, xg = q["xs"], q["xg"]
        xgb = xg.astype(bf16)
        wf = xg * q["dse"]
        wst = wf.astype(bf16)
        zz = z_ref[...]
        yy = y_ref[...]
        sz, dsz = _silu_and_grad(zz)
        y2, _, xhats, rs = _gate_norm(yy, zz, nw_ref[...], gate=sz)
        dyn_ = dyn_ref[...]
        dy2s, dnws = [], []
        for g in range(SSM_GROUPS):
            gl = slice(g * GROUP_W, (g + 1) * GROUP_W)
            gw = dyn_[:, gl] * nw_ref[:, gl]
            dy2s.append(rs[g] * (gw - xhats[g] * jnp.mean(gw * xhats[g], axis=-1, keepdims=True)))
            dnws.append(_rowsum8(dyn_[:, gl] * xhats[g]))
        dy2 = jnp.concatenate(dy2s, axis=1)
        dy = dy2 * sz
        dz_ref[...] = (dy2 * yy * dsz).astype(bf16)
        dnw_p = jnp.concatenate(dnws, axis=1)
        dds_p = _rowsum8(dy * xs)
        dyb = dy.astype(bf16)
        gfull = (dy * q["ecs"]).astype(bf16)
        dcs_c = jnp.zeros((CHUNK, CHUNK), f32)
        dcs_r = jnp.zeros((CHUNK, CHUNK), f32)
        dcs_e_parts, dxg_parts = [], []
        for g in range(SSM_GROUPS):
            gl = slice(g * GROUP_W, (g + 1) * GROUP_W)
            bsl = slice(SSM_INNER + g * D_STATE, SSM_INNER + (g + 1) * D_STATE)
            csl = slice(SSM_INNER + SSM_GROUPS * D_STATE + g * D_STATE, SSM_INNER + SSM_GROUPS * D_STATE + (g + 1) * D_STATE)
            bg = xbc_ref[:, bsl].astype(bf16)
            cg = xbc_ref[:, csl].astype(bf16)
            cb = _nt(cg, bg)
            hg = hs_ref[0, g]
            hgb = hg.astype(bf16)
            dhn = dh_scr[g]
            dhnb = dhn.astype(bf16)
            yoff = _nn(cg, hgb) * q["ecs"][:, gl]
            dw_ = _nn(bg, dhnb)
            r_e = dw_ * wf[:, gl]
            to_last = jnp.sum(r_e, axis=0, keepdims=True) + jnp.sum(dhn * hg, axis=0, keepdims=True) * q["cde"][:, gl]
            dcs_e_parts.append(dy[:, gl] * yoff - r_e + jnp.where(last_row, to_last, 0.0))
            dcb = jnp.zeros((CHUNK, CHUNK), f32)
            dxg_pairs = []
            for i in range(HEADS_PER_GROUP // 2):
                h0 = g * HEADS_PER_GROUP + 2 * i
                psl = slice(h0 * HEAD_DIM, (h0 + 2) * HEAD_DIM)
                xp = xgb[:, psl]
                dyp = dyb[:, psl]
                zero = jnp.zeros_like(dyp)
                tns = []
                for a in range(2):
                    h = h0 + a
                    lm = _decay_mat(q, h)
                    m = cb * lm
                    dm = _nt(jnp.where(low, dyp, zero) if a == 0 else jnp.where(low, zero, dyp), xp)
                    dcb = dcb + dm * lm
                    nmat = dm * m
                    dcs_c = dcs_c + jnp.where(q["cidx"] == h, jnp.sum(nmat, axis=1, keepdims=True), 0.0)
                    dcs_r = dcs_r + jnp.where(q["r"] == h, jnp.sum(nmat, axis=0, keepdims=True), 0.0)
                    tns.append(_tn(m.astype(bf16), dyp))
                dxg_pairs.append(jnp.where(low, tns[0], tns[1]))
            dxg_parts.append(jnp.concatenate(dxg_pairs, axis=1) + dw_ * q["dse"][:, gl])
            dcbb = dcb.astype(bf16)
            dxbc_ref[:, csl] = _nt(gfull[:, gl], hgb) + _nn(dcbb, bg)
            dxbc_ref[:, bsl] = _nt(wst[:, gl], dhnb) + _tn(dcbb, cg)
            dh_scr[g] = dhn * q["cde"][:, gl] + _tn(cg, gfull[:, gl])
        dxg = jnp.concatenate(dxg_parts, axis=1)
        dcs_e = jnp.concatenate(dcs_e_parts, axis=1)
        dxbc_ref[:, 0:SSM_INNER] = dskip_ref[...] * dy + dxg * q["dt_e"]
        dcs = dcs_c - dcs_r.T + _dot_exact(dcs_e, q["expand"], ((1,), (1,)))
        triu = (q["cidx"] >= q["r"]).astype(bf16)
        da = _dot_exact(dcs, triu, ((1,), (0,)), x_is_lhs=False)
        ddt = _dot_exact(dxg * xs, q["expand"], ((1,), (1,))) + da * q["a_neg"]
        ddtp = jnp.where(q["head_lane"], ddt * _sigmoid(q["dtp"]), 0.0)
        ddt_ref[...] = ddtp.astype(bf16)
        dal_p = _rowsum8(da * q["dt"]) * q["a_neg"]
        dbi_p = _rowsum8(ddtp)
        first = (pl.program_id(0) == 0) & (pl.program_id(1) == 0)

        @pl.when(first)
        def _():
            dnw_ref[...] = dnw_p
            dds_ref[...] = dds_p
            dal_ref[...] = dal_p
            dbi_ref[...] = dbi_p

        @pl.when(jnp.logical_not(first))
        def _():
            dnw_ref[...] += dnw_p
            dds_ref[...] += dds_p
            dal_ref[...] += dal_p
            dbi_ref[...] += dbi_p

    def rows(w):
        return pl.BlockSpec((CHUNK, w), lambda b, c: (b * n_chunk + n_chunk - 1 - c, 0))

    def par(w):
        return pl.BlockSpec((1, w), lambda b, c: (0, 0))

    def acc(w):
        return pl.BlockSpec((SUBLANES, w), lambda b, c: (0, 0))

    return pl.pallas_call(
        body, name=name, grid=(t // SEQ, n_chunk),
        in_specs=[rows(CONV_CH), rows(SSM_INNER), rows(LANES), rows(SSM_INNER),
                  pl.BlockSpec((1, SSM_GROUPS, D_STATE, GROUP_W), lambda b, c: (b * n_chunk + n_chunk - 1 - c, 0, 0, 0)),
                  rows(SSM_INNER), par(LANES), par(LANES), par(SSM_INNER), par(SSM_INNER)],
        out_specs=[rows(CONV_CH), rows(SSM_INNER), rows(LANES), acc(SSM_INNER), acc(SSM_INNER), acc(LANES), acc(LANES)],
        out_shape=[jax.ShapeDtypeStruct((t, CONV_CH), f32), jax.ShapeDtypeStruct((t, SSM_INNER), bf16), jax.ShapeDtypeStruct((t, LANES), bf16),
                   jax.ShapeDtypeStruct((SUBLANES, SSM_INNER), f32), jax.ShapeDtypeStruct((SUBLANES, SSM_INNER), f32),
                   jax.ShapeDtypeStruct((SUBLANES, LANES), f32), jax.ShapeDtypeStruct((SUBLANES, LANES), f32)],
        scratch_shapes=[pltpu.VMEM((SSM_GROUPS, D_STATE, GROUP_W), f32)],
        compiler_params=_cparams(("arbitrary", "arbitrary")),
    )(xbc, z, dtp, y, hs, dyn, *params)


def _adamw_update(g, w, m, v):
    mm = ADAM_B1 * m + (1.0 - ADAM_B1) * g
    vv = ADAM_B2 * v + (1.0 - ADAM_B2) * (g * g)
    m_hat = mm / (1.0 - ADAM_B1 ** ADAM_STEP)
    v_hat = vv / (1.0 - ADAM_B2 ** ADAM_STEP)
    return -ADAM_LR * (m_hat / (jnp.sqrt(v_hat) + ADAM_EPS) + ADAM_WD * w), mm, vv


def _adamw(g_parts, w, m, v, name):
    rows, width = w.shape
    n = len(g_parts)
    tr = _row_tile(rows)

    def body(*refs):
        g_refs, (w_ref, m_ref, v_ref, g_out, d_out, m_out, v_out) = refs[:n], refs[n:]
        g = g_refs[0][...].astype(f32)
        for r in g_refs[1:]:
            g = g + r[...].astype(f32)
        g_out[...] = g
        d_out[...], m_out[...], v_out[...] = _adamw_update(g, w_ref[...], m_ref[...], v_ref[...])

    spec = pl.BlockSpec((tr, width), lambda i: (i, 0))
    return pl.pallas_call(
        body, name=name, grid=(rows // tr,), in_specs=[spec] * (n + 3), out_specs=[spec] * 4,
        out_shape=[jax.ShapeDtypeStruct((rows, width), f32)] * 4, compiler_params=_cparams(("parallel",)),
    )(*g_parts, w, m, v)


def _adamw_layers(landed, w, m, v, after, name, layers_on_columns=False):
    depth = len(landed)
    _, rows, width = landed[0].shape
    tr = _row_tile(rows)
    n_i = rows // tr
    at = (lambda ref: ref) if layers_on_columns else (lambda ref: ref.at[0])

    def body(*refs):
        part_refs, (w_ref, m_ref, v_ref, _, g_out, d_out, m_out, v_out) = refs[:depth * N_DEV], refs[depth * N_DEV:]
        for l in range(depth):
            @pl.when(pl.program_id(0) == l)
            def _(l=l):
                g = part_refs[l * N_DEV][0].astype(f32)
                for r in part_refs[l * N_DEV + 1:(l + 1) * N_DEV]:
                    g = g + r[0].astype(f32)
                at(g_out)[...] = g
                at(d_out)[...], at(m_out)[...], at(v_out)[...] = _adamw_update(g, at(w_ref)[...], at(m_ref)[...], at(v_ref)[...])

    def part_spec(l, p):
        return pl.BlockSpec((1, tr, width), lambda ll, i: (p, jnp.where(ll == l, i, jnp.where(ll < l, 0, n_i - 1)), 0))

    state = (pl.BlockSpec((tr, width), lambda ll, i: (i, ll)) if layers_on_columns
             else pl.BlockSpec((1, tr, width), lambda ll, i: (ll, i, 0)))
    return pl.pallas_call(
        body, name=name, grid=(depth, n_i),
        in_specs=[part_spec(l, p) for l in range(depth) for p in range(N_DEV)] + [state] * 3 + [ANY], out_specs=[state] * 4,
        out_shape=[jax.ShapeDtypeStruct(w.shape, f32)] * 4, compiler_params=_cparams(("arbitrary", "arbitrary")),
    )(*[landed[l] for l in range(depth) for _ in range(N_DEV)], w, m, v, after)


def _row_tile(rows, cap=512):
    for cand in range(min(rows, cap) // SUBLANES * SUBLANES, 0, -SUBLANES):
        if rows % cand == 0:
            return cand
    return rows


def _cols_from_devices(g, width, name):
    n_dev, depth, a, b = g.shape

    def body(g_ref, o_ref):
        for i in range(n_dev):
            o_ref[0, :, i * b:(i + 1) * b] = g_ref[i, 0]
        if width > n_dev * b:
            o_ref[0, :, n_dev * b:width] = jnp.zeros((a, width - n_dev * b), o_ref.dtype)

    return pl.pallas_call(
        body, name=name, grid=(depth,), in_specs=[pl.BlockSpec((n_dev, 1, a, b), lambda l: (0, l, 0, 0))],
        out_specs=pl.BlockSpec((1, a, width), lambda l: (l, 0, 0)), out_shape=jax.ShapeDtypeStruct((depth, a, width), g.dtype),
        compiler_params=_cparams(("parallel",)),
    )(g)


def _devices_from_cols(per_layer, b, name, tr=256):
    depth = len(per_layer)
    a, width = per_layer[0].shape

    def body(*refs):
        o_ref = refs[depth]
        for l in range(depth):
            for i in range(N_DEV):
                o_ref[i, l] = refs[l][:, i * b:(i + 1) * b]

    return pl.pallas_call(
        body, name=name, grid=(a // tr,), in_specs=[pl.BlockSpec((tr, width), lambda r: (r, 0))] * depth,
        out_specs=pl.BlockSpec((N_DEV, depth, tr, b), lambda r: (0, 0, r, 0)),
        out_shape=jax.ShapeDtypeStruct((N_DEV, depth, a, b), per_layer[0].dtype), compiler_params=_cparams(("parallel",)),
    )(*per_layer)


def _me():
    return lax.axis_index("x"), lax.axis_index("y"), lax.axis_index("c")


def _allgather_two_level(shards, name):
    n = len(shards)
    per = 7

    def body(*refs):
        ins, outs, token = refs[:n], refs[n:2 * n], refs[2 * n]
        send_sems, recv_sems, local_sems = refs[2 * n + 1:]
        token[...] = jnp.zeros_like(token)
        x, y, c = _me()
        me, sibling = (x, y, c), (x, y, 1 - c)
        chips = [(1 - x, y), (x, 1 - y), (1 - x, 1 - y)]

        def slot(a, p):
            return outs[a].at[4 * p[0] + 2 * p[1] + p[2]]

        def copy(a, k, block, to, src=None):
            return pltpu.make_async_remote_copy(
                src_ref=slot(a, block) if src is None else src, dst_ref=slot(a, block),
                send_sem=send_sems.at[a * per + k], recv_sem=recv_sems.at[a * per + k], device_id=to, device_id_type=MESH)

        mine = [pltpu.make_async_copy(ins[a], slot(a, me), local_sems.at[a]) for a in range(n)]
        for cp in mine:
            cp.start()
        first = []
        for a in range(n):
            first.append(copy(a, 0, me, sibling, src=ins[a]))
            first += [copy(a, 1 + j, me, (*chip, c), src=ins[a]) for j, chip in enumerate(chips)]
        for cp in first:
            cp.start()
        passed = []
        for j, chip in enumerate(chips):
            for a in range(n):
                copy(a, 1 + j, (*chip, c), me).wait_recv()
                fwd = copy(a, 4 + j, (*chip, c), sibling)
                fwd.start()
                passed.append(fwd)
        for a in range(n):
            copy(a, 0, sibling, me).wait_recv()
            for j, chip in enumerate(chips):
                copy(a, 4 + j, (*chip, 1 - c), me).wait_recv()
        for cp in first + passed:
            cp.wait_send()
        for cp in mine:
            cp.wait()

    outs = pl.pallas_call(
        body, name=name, in_specs=[ANY] * n, out_specs=[ANY] * n + [pl.BlockSpec(memory_space=pltpu.VMEM)],
        out_shape=[jax.ShapeDtypeStruct((N_DEV,) + s.shape, s.dtype) for s in shards] + [jax.ShapeDtypeStruct((SUBLANES, LANES), f32)],
        scratch_shapes=[pltpu.SemaphoreType.DMA((n * per,)), pltpu.SemaphoreType.DMA((n * per,)), pltpu.SemaphoreType.DMA((n,))],
    )(*shards)
    return outs[:n], outs[n]


def _allgather_direct(row, name):
    def body(in_ref, out_ref, send_sems, recv_sems, local_sem):
        x, y, c = _me()
        mine = out_ref.at[4 * x + 2 * y + c]
        local = pltpu.make_async_copy(in_ref, mine, local_sem)
        local.start()
        sends = []
        for k in range(1, N_DEV):
            px, py, pc = x ^ (k >> 2), y ^ ((k >> 1) & 1), c ^ (k & 1)
            sends.append(pltpu.make_async_remote_copy(
                src_ref=in_ref, dst_ref=mine, send_sem=send_sems.at[k - 1], recv_sem=recv_sems.at[k - 1],
                device_id=(px, py, pc), device_id_type=MESH))
        for cp in sends:
            cp.start()
        for k in range(1, N_DEV):
            px, py, pc = x ^ (k >> 2), y ^ ((k >> 1) & 1), c ^ (k & 1)
            theirs = out_ref.at[4 * px + 2 * py + pc]
            pltpu.make_async_remote_copy(
                src_ref=in_ref, dst_ref=theirs, send_sem=send_sems.at[k - 1], recv_sem=recv_sems.at[k - 1],
                device_id=(px, py, pc), device_id_type=MESH).wait_recv()
        for cp in sends:
            cp.wait_send()
        local.wait()

    return pl.pallas_call(
        body, name=name, in_specs=[ANY], out_specs=ANY, out_shape=jax.ShapeDtypeStruct((N_DEV,) + row.shape, row.dtype),
        scratch_shapes=[pltpu.SemaphoreType.DMA((N_DEV - 1,)), pltpu.SemaphoreType.DMA((N_DEV - 1,)), pltpu.SemaphoreType.DMA],
    )(row)


N_CHIP = N_DEV // 2
HBM = pl.BlockSpec(memory_space=pltpu.HBM)
SEM = pl.BlockSpec(memory_space=pltpu.SEMAPHORE)
EFFECT = pltpu.SideEffectType.DATAFLOW_SIDE_EFFECTING


def _peer(k):
    x, y, c = _me()
    return x ^ (k >> 2), y ^ ((k >> 1) & 1), c ^ (k & 1)


def _direct_copies(srcs, lands, send_sems, recv_sems, per_peer):
    x, y, c = _me()
    me = 4 * x + 2 * y + c
    copies = []
    for a in range(len(srcs)):
        for k in range(1, N_DEV):
            px, py, pc = _peer(k)
            piece = srcs[a].at[4 * px + 2 * py + pc] if per_peer else srcs[a]
            copies.append(pltpu.make_async_remote_copy(
                src_ref=piece, dst_ref=lands[a].at[me], send_sem=send_sems.at[a * (N_DEV - 1) + k - 1],
                recv_sem=recv_sems.at[a * (N_DEV - 1) + k - 1], device_id=(px, py, pc), device_id_type=MESH))
    return copies


def _direct_start(srcs, lands, per_peer, name):
    n = len(srcs)
    n_sem = n * (N_DEV - 1)

    def body(*refs):
        src_refs, land_refs = refs[:n], refs[n:2 * n]
        send_sems, recv_sems = refs[2 * n], refs[2 * n + 1]
        token = refs[-1]
        for cp in _direct_copies(src_refs, land_refs, send_sems, recv_sems, per_peer):
            cp.start()
        token[...] = jnp.zeros_like(token)

    outs = pl.pallas_call(
        body, name=name,
        out_shape=(pltpu.SemaphoreType.DMA((n_sem,)), pltpu.SemaphoreType.DMA((n_sem,)),
                   *[pltpu.HBM(s.shape, s.dtype) for s in srcs], *[pltpu.HBM(s.shape, s.dtype) for s in lands],
                   jax.ShapeDtypeStruct((SUBLANES, LANES), f32)),
        in_specs=[HBM] * (2 * n), out_specs=(SEM, SEM, *[HBM] * (2 * n), pl.BlockSpec(memory_space=pltpu.VMEM)),
        input_output_aliases={i: 2 + i for i in range(2 * n)},
        compiler_params=pltpu.CompilerParams(has_side_effects=EFFECT),
    )(*[pltpu.with_memory_space_constraint(s, pltpu.HBM) for s in srcs], *[pltpu.with_memory_space_constraint(s, pltpu.HBM) for s in lands])
    return outs[0], outs[1], outs[2:2 + n], outs[2 + n:2 + 2 * n], outs[-1]


def _direct_wait(send_sems, recv_sems, srcs, lands, after, per_peer, name):
    n = len(srcs)

    def body(*refs):
        src_refs, land_refs = refs[:n], refs[n:2 * n]
        s_sems, r_sems = refs[2 * n], refs[2 * n + 1]
        for cp in _direct_copies(src_refs, land_refs, s_sems, r_sems, per_peer):
            cp.wait_send()
            cp.wait_recv()

    outs = pl.pallas_call(
        body, name=name,
        out_shape=tuple(pltpu.HBM(s.shape, s.dtype) for s in list(srcs) + list(lands)),
        in_specs=[HBM] * (2 * n) + [SEM, SEM, ANY], out_specs=tuple([HBM] * (2 * n)),
        input_output_aliases={i: i for i in range(2 * n)},
        compiler_params=pltpu.CompilerParams(has_side_effects=EFFECT),
    )(*srcs, *lands, send_sems, recv_sems, after)
    return outs[n:]


def _row(v, width=None):
    v = v.reshape(1, -1).astype(f32)
    if width is not None and v.shape[1] < width:
        v = jnp.pad(v, ((0, 0), (0, width - v.shape[1])))
    return v


def _layer_params(p, l):
    return dict(
        norm_mix=_row(p["norm_mix"][l]), norm_ffn=_row(p["norm_ffn"][l]), conv_w=p["conv_w"][l], conv_b=_row(p["conv_b"][l]),
        ssd=(_row(p["dt_bias"][l], LANES), _row(p["a_log"][l], LANES), _row(jnp.repeat(p["d_skip"][l], HEAD_DIM)), _row(p["ssm_norm"][l])))


def _layer_fwd(h, w_in, rest, sp, tabs, l):
    tag = f"l{l}_"
    hn = _rmsnorm_fwd(h, sp["norm_mix"], tag + "norm_mix")
    qkv, z, xbc_pre = _in_proj(hn, w_in, (QKV_WIDTH, SSM_INNER, CONV_CH), tag + "proj")
    dtp = _matmul(hn, w_in, mode="nn", n_out=LANES, tn=LANES, b_off=DT_OFF // LANES, name=tag + "proj_dt")
    prep = _attn_prep(qkv, tabs, tag + "attn_prep")
    o, lse = _attn_fwd(prep, tag + "attn_fwd")
    xbc = _conv_fwd(xbc_pre, sp["conv_w"], sp["conv_b"], tag + "conv_fwd")
    yn, y, hs = _ssd_fwd(xbc, z, dtp, sp["ssd"], tag + "ssd_fwd")
    w_out, w_gate, w_up, w_down = rest(yn) if callable(rest) else rest
    h2 = _out_proj(o, yn, w_out, h, tag + "out_proj")
    hn2 = _rmsnorm_fwd(h2, sp["norm_ffn"], tag + "norm_ffn")
    g, u, act = _swiglu_fwd(hn2, w_gate, w_up, tag + "ffn_up")
    h3 = _matmul(act, w_down, mode="nn", tk=1408, add=h2, name=tag + "ffn_down")
    saved = dict(h=h, hn=hn, prep=prep, z=z, xbc_pre=xbc_pre, dtp=dtp, o=o, lse=lse, xbc=xbc, yn=yn, y=y, hs=hs, h2=h2, hn2=hn2, g=g, u=u, act=act,
                 rest=(w_out, w_gate, w_up, w_down))
    return h3, saved


def _layer_bwd(dh3_pair, s, big, sp, tabs, l, gd=f32, after_ffn=None):
    tag = f"l{l}_"
    dh3, dh3b = dh3_pair
    w_in, w_out, w_gate, w_up, w_down = big
    dg, du = _swiglu_bwd(dh3b, w_down, s["g"], s["u"], tag + "ffn_down_bwd")
    dw_down = _matmul(s["act"], dh3b, mode="tn", tm=1408, tn=512, tk=2048, out_dtype=gd, name=tag + "dw_down")
    dw_gate = _matmul(dg, s["hn2"], mode="tn", tm=1408, tn=512, tk=2048, out_dtype=gd, name=tag + "dw_gate")
    dw_up = _matmul(du, s["hn2"], mode="tn", tm=1408, tn=512, tk=2048, out_dtype=gd, name=tag + "dw_up")
    norm_ffn = sp["norm_ffn"] if after_ffn is None else sp["norm_ffn"] + after_ffn(dict(w_gate=dw_gate, w_up=dw_up, w_down=dw_down))
    dh2, dh2b, dnf = _nt_norm_bwd([(dg, w_gate), (du, w_up)], s["h2"], norm_ffn, dh3, tag + "ffn_up_bwd_norm", tk=1408, b_is_kd=True,
                                  vmem=VMEM_LIMIT_TWO_PAIRS)
    d_o = _matmul(dh2b, w_out, mode="nt", n_out=ATTN_WIDTH, tn=512, b_off=0, name=tag + "out_attn_bwd")
    dyn = _matmul(dh2b, w_out, mode="nt", n_out=SSM_INNER, tn=512, b_off=1, name=tag + "out_ssm_bwd")
    dw_out = jnp.concatenate([_matmul(s["o"], dh2b, mode="tn", tm=512, tn=512, tk=2048, out_dtype=gd, name=tag + "dw_out_attn"),
                              _matmul(s["yn"], dh2b, mode="tn", tm=512, tn=512, tk=2048, out_dtype=gd, name=tag + "dw_out_ssm")], axis=0)
    dxbc, dz, ddtp, dnw, dds, dal, dbi = _ssd_bwd(s["xbc"], s["z"], s["dtp"], s["y"], s["hs"], dyn, sp["ssd"], tag + "ssd_bwd")
    dxbc_pre, dconv_w, dconv_b = _conv_bwd(s["xbc_pre"], sp["conv_w"], sp["conv_b"], dxbc, tag + "conv_bwd")
    dq, dk, dv = _attn_bwd(s["prep"], tabs, s["o"], s["lse"], d_o, tag + "attn_bwd")
    dproj = jnp.concatenate([dq, dk, dv, dz, dxbc_pre, ddtp], axis=1)
    dw_in = _matmul(s["hn"], dproj, mode="tn", tm=512, tn=1152, tk=2048, out_dtype=gd, name=tag + "dw_in")
    res = _nt_norm_bwd([(dproj, w_in)], s["h"], sp["norm_mix"], dh2, tag + "proj_bwd_norm", tk=1152, bf16_copy=l > 0)
    dh, dhb, dnm = res if l > 0 else (res[0], None, res[1])
    grads = dict(
        norm_mix=dnm.sum(0), w_in=dw_in, conv_w=dconv_w, conv_b=dconv_b[0], dt_bias=dbi.sum(0)[:SSM_HEADS], a_log=dal.sum(0)[:SSM_HEADS],
        d_skip=dds.sum(0).reshape(SSM_HEADS, HEAD_DIM).sum(1), ssm_norm=dnw.sum(0), w_out=dw_out, norm_ffn=dnf.sum(0),
        w_gate=dw_gate, w_up=dw_up, w_down=dw_down)
    return (dh, dhb), grads


def _local_step(x, positions, target, p, bigs):
    tabs = _rope_tables(positions.reshape(-1, 1), "rope_tables")
    h = x
    saved, sps = [], []
    for l in range(DEPTH):
        sps.append(_layer_params(p, l))
        h, s = _layer_fwd(h, bigs[l][0], bigs[l][1:], sps[l], tabs, l)
        saved.append(s)
    dh, dhb, loss_parts, dfn = _final_loss(h, _row(p["final_norm"]), target, "final_loss")
    dh = (dh, dhb)
    layer_grads = [None] * DEPTH
    for l in reversed(range(DEPTH)):
        dh, layer_grads[l] = _layer_bwd(dh, saved[l], bigs[l], sps[l], tabs, l)
    grads = {k: [layer_grads[l][k] for l in range(DEPTH)] for k in layer_grads[0]}
    grads["final_norm"] = dfn.sum(0)
    return jnp.sum(loss_parts), dh[0], grads


BIG = ("w_in", "w_out", "w_gate", "w_up", "w_down")
REST = BIG[1:]
FFN = ("w_gate", "w_up", "w_down")
MIX = ("w_in", "w_out")
COL_SHARDED = ("w_in",)
TRANSPOSED = ("w_gate", "w_up")
SMALL = ("norm_mix", "conv_b", "dt_bias", "a_log", "d_skip", "ssm_norm", "norm_ffn", "final_norm")
WEIGHTS = ("norm_mix", "w_in", "conv_w", "conv_b", "dt_bias", "a_log", "d_skip", "ssm_norm", "w_out", "norm_ffn", "w_gate", "w_up", "w_down", "final_norm")
SMALL_ROWS = 88
CONVW_ROWS = 96
CONVW_SHARD_ROWS = 16


def _full_from_gathered(name, g, l):
    _, a, b = g.shape
    if name in COL_SHARDED:
        width = IN_PROJ_PAD if name == "w_in" else N_DEV * b
        return _cols_from_devices(g.reshape(N_DEV, 1, a, b), width, f"cols_l{l}_{name}").reshape(a, width)
    return g.reshape(N_DEV * a, b)


def _by_device(name, full, shard_shape, l):
    a, b = shard_shape
    if name in COL_SHARDED:
        return _devices_from_cols([full], b, f"devs_l{l}_{name}").reshape(N_CHIP, 2, a, b)
    return full.reshape(N_CHIP, 2, a, b)


def _pack_rows(parts, rows, width):
    flat = jnp.concatenate([q.reshape(-1) for q in parts])
    return jnp.pad(flat, (0, rows * width - flat.shape[0])).reshape(rows, width)


def _unpack(flat, like):
    out, off = [], 0
    for q in like:
        out.append(flat[off:off + q.size].reshape(q.shape))
        off += q.size
    return out


def kernel(x, positions, norm_mix, w_in, conv_w, conv_b, dt_bias, a_log, d_skip, ssm_norm, w_out, norm_ffn, w_gate, w_up, w_down, final_norm, loss_target, m_norm_mix, m_w_in, m_conv_w, m_conv_b, m_dt_bias, m_a_log, m_d_skip, m_ssm_norm, m_w_out, m_norm_ffn, m_w_gate, m_w_up, m_w_down, m_final_norm, v_norm_mix, v_w_in, v_conv_w, v_conv_b, v_dt_bias, v_a_log, v_d_skip, v_ssm_norm, v_w_out, v_norm_ffn, v_w_gate, v_w_up, v_w_down, v_final_norm):
    w = dict(norm_mix=norm_mix, w_in=w_in, conv_w=conv_w, conv_b=conv_b, dt_bias=dt_bias, a_log=a_log, d_skip=d_skip, ssm_norm=ssm_norm,
             w_out=w_out, norm_ffn=norm_ffn, w_gate=w_gate, w_up=w_up, w_down=w_down, final_norm=final_norm)
    m = dict(norm_mix=m_norm_mix, w_in=m_w_in, conv_w=m_conv_w, conv_b=m_conv_b, dt_bias=m_dt_bias, a_log=m_a_log, d_skip=m_d_skip,
             ssm_norm=m_ssm_norm, w_out=m_w_out, norm_ffn=m_norm_ffn, w_gate=m_w_gate, w_up=m_w_up, w_down=m_w_down, final_norm=m_final_norm)
    v = dict(norm_mix=v_norm_mix, w_in=v_w_in, conv_w=v_conv_w, conv_b=v_conv_b, dt_bias=v_dt_bias, a_log=v_a_log, d_skip=v_d_skip,
             ssm_norm=v_ssm_norm, w_out=v_w_out, norm_ffn=v_norm_ffn, w_gate=v_w_gate, w_up=v_w_up, w_down=v_w_down, final_norm=v_final_norm)
    ax, ay, ac = lax.axis_index("x"), lax.axis_index("y"), lax.axis_index("c")
    dev = 4 * ax + 2 * ay + ac

    assert DEPTH == 2
    t = x.shape[0] * x.shape[1]
    xf, target = x.reshape(t, D_MODEL), loss_target.reshape(t, D_MODEL)

    def own_slot(block):
        return lax.dynamic_update_slice(lax.empty((N_DEV,) + block.shape[1:], block.dtype), block, (dev,) + (0,) * (block.ndim - 1))

    def layer_shard(arr, k, l):
        return jnp.transpose(arr, (2, 0, 1))[:, l, :] if k in TRANSPOSED else arr[l]

    def gather_start(keys, l, tie, name):
        shards = [(layer_shard(w[keys[0]], keys[0], l) + tie).astype(bf16)] + [layer_shard(w[k], k, l).astype(bf16) for k in keys[1:]]
        return _direct_start(shards, [own_slot(s[None]) for s in shards], False, name)

    def scatter_start(keys, grads_l, l, name):
        shapes = [(w[k].shape[2], w[k].shape[1]) if k in TRANSPOSED else w[k].shape[1:] for k in keys]
        by_dev = [_by_device(k, grads_l[k], sh, l).reshape((N_DEV,) + sh) for k, sh in zip(keys, shapes)]
        return _direct_start(by_dev, [own_slot(lax.dynamic_slice_in_dim(g, dev, 1, 0)) for g in by_dev], True, name)

    (g_in0, conv_all), tie = _allgather_two_level([w["w_in"][0].astype(bf16), w["conv_w"]], "gather_l0_w_in")
    rest0_copy = gather_start(REST, 0, tie[0, 0], "gather_l0_rest_start")
    l1_copy = gather_start(BIG, 1, rest0_copy[4][0, 0], "gather_l1_start")
    p = {k: w[k] for k in SMALL}
    p["norm_mix"] = p["norm_mix"] + l1_copy[4][0, 0]
    p["conv_w"] = jnp.transpose(conv_all, (1, 2, 0, 3)).reshape(DEPTH, CONV_WIDTH, CONV_CH)
    sp0, sp1 = _layer_params(p, 0), _layer_params(p, 1)

    def rest0(after):
        lands = _direct_wait(*rest0_copy[:4], after, False, "gather_l0_rest_wait")
        return tuple(_full_from_gathered(k, g, 0) for k, g in zip(REST, lands))

    tabs = _rope_tables(positions.reshape(t, 1), "rope_tables")
    w_in0 = _full_from_gathered("w_in", g_in0, 0)
    h1, saved0 = _layer_fwd(xf, w_in0, rest0, sp0, tabs, 0)
    lands1 = _direct_wait(*l1_copy[:4], h1, False, "gather_l1_wait")
    bigs1 = tuple(_full_from_gathered(k, g, 1) for k, g in zip(BIG, lands1))
    h2, saved1 = _layer_fwd(h1, bigs1[0], bigs1[1:], sp1, tabs, 1)
    dh, dhb, loss_parts, dfn = _final_loss(h2, _row(p["final_norm"]), target, "final_loss")
    loss_local = jnp.sum(loss_parts)

    dh, grads1 = _layer_bwd((dh, dhb), saved1, bigs1, sp1, tabs, 1, gd=bf16)
    l1_grads = scatter_start(BIG, grads1, 1, "scatter_l1_start")
    w_out0, w_gate0, w_up0, w_down0 = saved0["rest"]
    bigs0 = (w_in0, w_out0, w_gate0, w_up0, w_down0 + l1_grads[4][0, 0].astype(bf16))
    ffn0_grads = []

    def after_ffn(grads_ffn):
        ffn0_grads.append(scatter_start(FFN, grads_ffn, 0, "scatter_l0_ffn_start"))
        return ffn0_grads[0][4][0, 0]

    (dx, _), grads0 = _layer_bwd(dh, saved0, bigs0, sp0, tabs, 0, gd=bf16, after_ffn=after_ffn)
    mix0_grads = scatter_start(MIX, grads0, 0, "scatter_l0_mix_start")
    landed = {(k, 1): g for k, g in zip(BIG, _direct_wait(*l1_grads[:4], dx, True, "scatter_l1_wait"))}
    landed.update({(k, 0): g for k, g in zip(FFN, _direct_wait(*ffn0_grads[0][:4], dx, True, "scatter_l0_ffn_wait"))})
    out_g, out_d, out_m, out_v = {}, {}, {}, {}

    def update(keys, after):
        for k in keys:
            parts = [landed[k, l] for l in range(DEPTH)]
            if k in TRANSPOSED:
                depth, a, b = w[k].shape
                state = [jnp.transpose(s, (2, 0, 1)).reshape(b, depth * a) for s in (w[k], m[k], v[k])]
                res = _adamw_layers(parts, *state, after, "adamw_" + k, layers_on_columns=True)
                res = [jnp.transpose(r.reshape(b, depth, a), (1, 2, 0)) for r in res]
            else:
                res = _adamw_layers(parts, w[k], m[k], v[k], after, "adamw_" + k)
            for dst, r in zip((out_g, out_d, out_m, out_v), res):
                dst[k] = r

    update(FFN, mix0_grads[4])
    grads = {k: [grads0[k], grads1[k]] for k in grads0 if k not in BIG}
    grads["final_norm"] = dfn.sum(0) + mix0_grads[4][0, 0]

    small_like = [w[k] for k in SMALL]
    small_grads = [jnp.stack(grads[k]) if k != "final_norm" else grads[k] for k in SMALL]
    small_pack = jnp.concatenate([_pack_rows(small_grads, SMALL_ROWS, LANES), _pack_rows([jnp.stack(grads["conv_w"])], CONVW_ROWS, LANES)], axis=0)
    parts = _allgather_direct(small_pack, "gather_small_grads")
    g_s, d_s, m_s, v_s = _adamw(
        [parts[i, :SMALL_ROWS] for i in range(N_DEV)], _pack_rows(small_like, SMALL_ROWS, LANES),
        _pack_rows([m[k] for k in SMALL], SMALL_ROWS, LANES), _pack_rows([v[k] for k in SMALL], SMALL_ROWS, LANES), "adamw_replicated")
    for dst, src in ((out_g, g_s), (out_d, d_s), (out_m, m_s), (out_v, v_s)):
        dst.update(zip(SMALL, _unpack(src.reshape(-1), small_like)))
    shard_w = conv_w.shape[-1]
    conv_parts = parts[:, SMALL_ROWS:].reshape(N_DEV, DEPTH, CONV_WIDTH, CONV_CH)
    conv_mine = lax.dynamic_slice_in_dim(conv_parts, dev * shard_w, shard_w, axis=3)
    g_c, d_c, m_c, v_c = _adamw(
        [_pack_rows([conv_mine[i]], CONVW_SHARD_ROWS, LANES) for i in range(N_DEV)], _pack_rows([conv_w], CONVW_SHARD_ROWS, LANES),
        _pack_rows([m["conv_w"]], CONVW_SHARD_ROWS, LANES), _pack_rows([v["conv_w"]], CONVW_SHARD_ROWS, LANES), "adamw_conv_w")
    for dst, src in ((out_g, g_c), (out_d, d_c), (out_m, m_c), (out_v, v_c)):
        dst["conv_w"] = src.reshape(-1)[:conv_w.size].reshape(conv_w.shape)

    landed.update({(k, 0): g for k, g in zip(MIX, _direct_wait(*mix0_grads[:4], v_c + out_v["w_down"][0, :CONVW_SHARD_ROWS, :LANES], True, "scatter_l0_mix_wait"))})
    update(MIX, v_c)

    loss = lax.psum(loss_local, ("x", "y", "c"))
    return (loss, dx.reshape(x.shape), *[out_g[k] for k in WEIGHTS], *[out_d[k] for k in WEIGHTS],
            *[out_m[k] for k in WEIGHTS], *[out_v[k] for k in WEIGHTS])
```

```python
import jax
import jax.numpy as jnp
import numpy as np
from jax import lax
from jax.experimental import pallas as pl
from jax.experimental.pallas import tpu as pltpu

f32 = jnp.float32
bf16 = jnp.bfloat16

D_MODEL = 1024
SEQ = 2048
DEPTH = 2
HEAD_DIM = 64
N_ATTN_HEADS = 8
N_KV_HEADS = 2
ATTN_WIDTH = 512
KV_WIDTH = 128
ROPE_DIM = 16
ROPE_THETA = 500000.0
DILATIONS = (1, 4, 16)
ATTN_BLOCK = 128
SSM_HEADS = 16
SSM_INNER = 1024
SSM_GROUPS = 2
D_STATE = 128
CONV_WIDTH = 4
CHUNK = 128
CONV_CH = 1536
MIX_WIDTH = 1536
QKV_WIDTH = ATTN_WIDTH + 2 * KV_WIDTH
DT_OFF = 3328
IN_PROJ = 3344
IN_PROJ_PAD = 3456
FFN_HIDDEN = 2816
EPS = 1e-5
N_DEV = 8
ADAM_LR = 0.001
ADAM_B1 = 0.9
ADAM_B2 = 0.999
ADAM_EPS = 1e-08
ADAM_WD = 0.01
ADAM_STEP = 10

LANES = 128
SUBLANES = 8
VMEM_LIMIT = 56 * 1024 * 1024
VMEM_LIMIT_TWO_PAIRS = 60 * 1024 * 1024

MESH = pl.DeviceIdType.MESH
ANY = pl.BlockSpec(memory_space=pl.ANY)


def _cparams(sem, vmem=None):
    return pltpu.CompilerParams(dimension_semantics=sem, vmem_limit_bytes=vmem or VMEM_LIMIT)


def _sigmoid(x):
    return 1.0 / (1.0 + jnp.exp(-x))


def _silu(x):
    return x * _sigmoid(x)


def _dsilu(x):
    s = _sigmoid(x)
    return s * (1.0 + x * (1.0 - s))


def _silu_and_grad(x):
    s = _sigmoid(x)
    return x * s, s * (1.0 + x * (1.0 - s))


def _softplus(x):
    return jnp.maximum(x, 0.0) + jnp.log(1.0 + jnp.exp(-jnp.abs(x)))


def _dot(a, b, dims, precision=None):
    return lax.dot_general(a, b, (dims, ((), ())), preferred_element_type=f32, precision=precision)


def _nn(a, b, precision=None):
    return _dot(a, b, ((1,), (0,)), precision)


def _nt(a, b):
    return _dot(a, b, ((1,), (1,)))


def _tn(a, b):
    return _dot(a, b, ((0,), (0,)))


def _rowsum8(t):
    n, w = t.shape
    return jnp.sum(t.reshape(n // SUBLANES, SUBLANES, w), axis=0)


def _matmul(a, b, *, mode, n_out=None, b_off=0, add=None, out_dtype=f32, tm=2048, tn=512, tk=1024, name):
    if mode == "tn":
        kk, m = a.shape
    else:
        m, kk = a.shape
    n = n_out if n_out is not None else (b.shape[0] if mode == "nt" else b.shape[1])
    tm, tn, tk = min(tm, m), min(tn, n), min(tk, kk)
    assert m % tm == 0 and n % tn == 0 and kk % tk == 0, (name, m, n, kk, tm, tn, tk)
    nk = kk // tk
    if mode == "nn":
        a_spec = pl.BlockSpec((tm, tk), lambda i, j, k: (i, k))
        b_spec = pl.BlockSpec((tk, tn), lambda i, j, k: (k, j + b_off))
        dims = ((1,), (0,))
    elif mode == "nt":
        a_spec = pl.BlockSpec((tm, tk), lambda i, j, k: (i, k))
        b_spec = pl.BlockSpec((tn, tk), lambda i, j, k: (j + b_off, k))
        dims = ((1,), (1,))
    else:
        a_spec = pl.BlockSpec((tk, tm), lambda i, j, k: (k, i))
        b_spec = pl.BlockSpec((tk, tn), lambda i, j, k: (k, j + b_off))
        dims = ((0,), (0,))
    o_spec = pl.BlockSpec((tm, tn), lambda i, j, k: (i, j))
    has_add = add is not None

    def body(*refs):
        if has_add:
            a_ref, b_ref, add_ref, o_ref, acc_ref = refs
        else:
            a_ref, b_ref, o_ref, acc_ref = refs
        k = pl.program_id(2)
        part = _dot(a_ref[...].astype(bf16), b_ref[...].astype(bf16), dims)

        @pl.when(k == 0)
        def _():
            acc_ref[...] = part

        @pl.when(k > 0)
        def _():
            acc_ref[...] += part

        @pl.when(k == nk - 1)
        def _():
            r = acc_ref[...]
            if has_add:
                r = r + add_ref[...]
            o_ref[...] = r.astype(out_dtype)

    in_specs = [a_spec, b_spec] + ([o_spec] if has_add else [])
    args = (a, b) + ((add,) if has_add else ())
    return pl.pallas_call(
        body, name=name, grid=(m // tm, n // tn, nk), in_specs=in_specs, out_specs=o_spec,
        out_shape=jax.ShapeDtypeStruct((m, n), out_dtype), scratch_shapes=[pltpu.VMEM((tm, tn), f32)],
        compiler_params=_cparams(("parallel", "parallel", "arbitrary")),
    )(*args)


def _in_proj(hn, w_in, widths, name, tm=2048, tn=256):
    m, k = hn.shape
    starts = [sum(widths[:i]) // tn for i in range(len(widths))]
    counts = [wd // tn for wd in widths]
    assert m % tm == 0 and all(wd % tn == 0 for wd in widths)
    n_out = len(widths)

    def body(a_ref, w_ref, *o_refs):
        j = pl.program_id(1)
        acc = _nn(a_ref[...], w_ref[...])
        for s, c, o_ref in zip(starts, counts, o_refs):
            @pl.when((j >= s) & (j < s + c))
            def _(o_ref=o_ref):
                o_ref[...] = acc

    def o_spec(s, c):
        return pl.BlockSpec((tm, tn), lambda i, j: (i, jnp.clip(j - s, 0, c - 1)))

    return pl.pallas_call(
        body, name=name, grid=(m // tm, sum(counts)),
        in_specs=[pl.BlockSpec((tm, k), lambda i, j: (i, 0)), pl.BlockSpec((k, tn), lambda i, j: (0, j))],
        out_specs=[o_spec(s, c) for s, c in zip(starts, counts)],
        out_shape=[jax.ShapeDtypeStruct((m, wd), f32) for wd in widths], compiler_params=_cparams(("parallel", "arbitrary")),
    )(hn, w_in)


def _out_proj(o, yn, w_out, h, name, tm=2048, tn=512):
    m, kb = o.shape
    n = w_out.shape[1]
    n_y = yn.shape[1] // kb
    assert yn.shape[1] % kb == 0 and w_out.shape[0] == kb * (1 + n_y) and m % tm == 0 and n % tn == 0

    def body(*refs):
        o_ref, y_refs, w_refs, h_ref, out_ref = refs[0], refs[1:1 + n_y], refs[1 + n_y:2 + 2 * n_y], refs[-2], refs[-1]
        acc = h_ref[...] + _nn(o_ref[...].astype(bf16), w_refs[0][...])
        for y_ref, w_ref in zip(y_refs, w_refs[1:]):
            acc = acc + _nn(y_ref[...], w_ref[...])
        out_ref[...] = acc

    res = pl.BlockSpec((tm, tn), lambda i, j: (i, j))

    def a_blk(c):
        return pl.BlockSpec((tm, kb), lambda i, j: (i, c))

    def w_blk(r):
        return pl.BlockSpec((kb, tn), lambda i, j: (r, j))

    return pl.pallas_call(
        body, name=name, grid=(m // tm, n // tn),
        in_specs=[a_blk(0)] + [a_blk(c) for c in range(n_y)] + [w_blk(r) for r in range(1 + n_y)] + [res],
        out_specs=res, out_shape=jax.ShapeDtypeStruct((m, n), f32), compiler_params=_cparams(("parallel", "parallel")),
    )(o, *[yn] * n_y, *[w_out] * (1 + n_y), h)


def _swiglu_fwd(hn, w_gate, w_up, name, tm=2048, tn=256):
    m, k = hn.shape
    n = w_gate.shape[0]
    assert m % tm == 0 and n % tn == 0, (name, m, n, tm, tn)

    def body(a_ref, wg_ref, wu_ref, g_ref, u_ref, act_ref):
        a = a_ref[...]
        g = _nt(a, wg_ref[...])
        u = _nt(a, wu_ref[...])
        sg, dsg = _silu_and_grad(g)
        g_ref[...] = (u * dsg).astype(bf16)
        u_ref[...] = sg.astype(bf16)
        act_ref[...] = (sg * u).astype(bf16)

    a_spec = pl.BlockSpec((tm, k), lambda i, j: (i, 0))
    w_spec = pl.BlockSpec((tn, k), lambda i, j: (j, 0))
    o_spec = pl.BlockSpec((tm, tn), lambda i, j: (i, j))
    return pl.pallas_call(
        body, name=name, grid=(m // tm, n // tn), in_specs=[a_spec, w_spec, w_spec], out_specs=[o_spec, o_spec, o_spec],
        out_shape=[jax.ShapeDtypeStruct((m, n), bf16)] * 3,
        compiler_params=_cparams(("parallel", "parallel")),
    )(hn, w_gate, w_up)


def _swiglu_bwd(dh, w_down, g, u, name, tm=2048, tn=256):
    m, k = dh.shape
    n = w_down.shape[0]
    assert m % tm == 0 and n % tn == 0, (name, m, n, tm, tn)

    def body(a_ref, w_ref, g_ref, u_ref, dg_ref, du_ref):
        dact = _nt(a_ref[...].astype(bf16), w_ref[...])
        dg_ref[...] = (dact * g_ref[...].astype(f32)).astype(bf16)
        du_ref[...] = (dact * u_ref[...].astype(f32)).astype(bf16)

    a_spec = pl.BlockSpec((tm, k), lambda i, j: (i, 0))
    w_spec = pl.BlockSpec((tn, k), lambda i, j: (j, 0))
    o_spec = pl.BlockSpec((tm, tn), lambda i, j: (i, j))
    return pl.pallas_call(
        body, name=name, grid=(m // tm, n // tn), in_specs=[a_spec, w_spec, o_spec, o_spec], out_specs=[o_spec, o_spec],
        out_shape=[jax.ShapeDtypeStruct((m, n), bf16), jax.ShapeDtypeStruct((m, n), bf16)],
        compiler_params=_cparams(("parallel", "parallel")),
    )(dh, w_down, g, u)


def _rmsnorm_fwd(h, w, name, tm=512):
    m, d = h.shape

    def body(h_ref, w_ref, o_ref):
        x = h_ref[...]
        r = lax.rsqrt(jnp.mean(x * x, axis=-1, keepdims=True) + EPS)
        o_ref[...] = (x * r * w_ref[...]).astype(bf16)

    return pl.pallas_call(
        body, name=name, grid=(m // tm,),
        in_specs=[pl.BlockSpec((tm, d), lambda i: (i, 0)), pl.BlockSpec((1, d), lambda i: (0, 0))],
        out_specs=pl.BlockSpec((tm, d), lambda i: (i, 0)), out_shape=jax.ShapeDtypeStruct((m, d), bf16),
        compiler_params=_cparams(("parallel",)),
    )(h, w)


def _nt_norm_bwd(pairs, h, w, dres, name, tk, b_is_kd=False, bf16_copy=True, tm=1024, vmem=None):
    m, d = h.shape
    contract = _nn if b_is_kd else _nt
    steps = [p[0].shape[1] // tk for p in pairs]
    assert all(p[0].shape[1] % tk == 0 for p in pairs), (name, tk)
    starts = [sum(steps[:i]) for i in range(len(pairs))]
    nk = sum(steps)
    n_p = len(pairs)

    def body(*refs):
        ab = refs[:2 * n_p]
        h_ref, w_ref, dres_ref, dh_ref = refs[2 * n_p:2 * n_p + 4]
        dhb_ref = refs[2 * n_p + 4] if bf16_copy else None
        dw_ref, acc_ref = refs[-2:]
        i, k = pl.program_id(0), pl.program_id(1)

        @pl.when(k == 0)
        def _():
            acc_ref[...] = jnp.zeros_like(acc_ref)

        for p in range(n_p):
            @pl.when((k >= starts[p]) & (k < starts[p] + steps[p]))
            def _(p=p):
                acc_ref[...] += contract(ab[2 * p][...], ab[2 * p + 1][...])

        @pl.when(k == nk - 1)
        def _():
            x = h_ref[...]
            r = lax.rsqrt(jnp.mean(x * x, axis=-1, keepdims=True) + EPS)
            xhat = x * r
            dy = acc_ref[...]
            gw = dy * w_ref[...]
            dh = dres_ref[...] + r * (gw - xhat * jnp.mean(gw * xhat, axis=-1, keepdims=True))
            dh_ref[...] = dh
            if bf16_copy:
                dhb_ref[...] = dh.astype(bf16)
            part = _rowsum8(dy * xhat)

            @pl.when(i == 0)
            def _():
                dw_ref[...] = part

            @pl.when(i > 0)
            def _():
                dw_ref[...] += part

    def clamp(k, p):
        return jnp.clip(k - starts[p], 0, steps[p] - 1)

    in_specs = []
    for p in range(n_p):
        b_spec = (pl.BlockSpec((tk, d), lambda i, k, p=p: (clamp(k, p), 0)) if b_is_kd
                  else pl.BlockSpec((d, tk), lambda i, k, p=p: (0, clamp(k, p))))
        in_specs += [pl.BlockSpec((tm, tk), lambda i, k, p=p: (i, clamp(k, p))), b_spec]
    row = pl.BlockSpec((tm, d), lambda i, k: (i, 0))
    in_specs += [row, pl.BlockSpec((1, d), lambda i, k: (0, 0)), row]
    return pl.pallas_call(
        body, name=name, grid=(m // tm, nk), in_specs=in_specs,
        out_specs=[row] + [row] * bf16_copy + [pl.BlockSpec((SUBLANES, d), lambda i, k: (0, 0))],
        out_shape=[jax.ShapeDtypeStruct((m, d), f32)] + [jax.ShapeDtypeStruct((m, d), bf16)] * bf16_copy + [jax.ShapeDtypeStruct((SUBLANES, d), f32)],
        scratch_shapes=[pltpu.VMEM((tm, d), f32)], compiler_params=_cparams(("arbitrary", "arbitrary"), vmem),
    )(*[t for p in pairs for t in p], h, w, dres)


def _final_loss(h, w, target, name, tm=512):
    m, d = h.shape

    def body(h_ref, w_ref, t_ref, dh_ref, dhb_ref, loss_ref, dw_ref):
        x = h_ref[...]
        r = lax.rsqrt(jnp.mean(x * x, axis=-1, keepdims=True) + EPS)
        xhat = x * r
        ww = w_ref[...]
        err = xhat * ww - t_ref[...]
        dy = err * (1.0 / d)
        gw = dy * ww
        dh = r * (gw - xhat * jnp.mean(gw * xhat, axis=-1, keepdims=True))
        dh_ref[...] = dh
        dhb_ref[...] = dh.astype(bf16)
        lpart = _rowsum8(err * err) * (0.5 / d)
        wpart = _rowsum8(dy * xhat)

        @pl.when(pl.program_id(0) == 0)
        def _():
            loss_ref[...] = lpart
            dw_ref[...] = wpart

        @pl.when(pl.program_id(0) > 0)
        def _():
            loss_ref[...] += lpart
            dw_ref[...] += wpart

    row = pl.BlockSpec((tm, d), lambda i: (i, 0))
    acc = pl.BlockSpec((SUBLANES, d), lambda i: (0, 0))
    return pl.pallas_call(
        body, name=name, grid=(m // tm,),
        in_specs=[row, pl.BlockSpec((1, d), lambda i: (0, 0)), row], out_specs=[row, row, acc, acc],
        out_shape=[jax.ShapeDtypeStruct((m, d), f32), jax.ShapeDtypeStruct((m, d), bf16),
                   jax.ShapeDtypeStruct((SUBLANES, d), f32), jax.ShapeDtypeStruct((SUBLANES, d), f32)],
        compiler_params=_cparams(("arbitrary",)),
    )(h, w, target)


def _lane_tables():
    f = np.arange(LANES) % HEAD_DIM
    inv = ROPE_THETA ** (-jnp.arange(0, ROPE_DIM, 2, dtype=f32) / ROPE_DIM)
    invf = jnp.where(f < ROPE_DIM, inv[f % (ROPE_DIM // 2)], 0.0).astype(f32)
    return invf.reshape(1, LANES)


def _rope_tables(pos_col, name):
    t = pos_col.shape[0]
    tm = SEQ

    def body(p_ref, f_ref, c_ref, s1_ref, s2_ref):
        ang = p_ref[...].astype(f32) * f_ref[...]
        co, si = jnp.cos(ang), jnp.sin(ang)
        f = lax.broadcasted_iota(jnp.int32, (tm, LANES), 1) % HEAD_DIM
        c_ref[...] = jnp.where(f < ROPE_DIM, co, 1.0)
        s1_ref[...] = jnp.where(f < ROPE_DIM // 2, -si, 0.0)
        s2_ref[...] = jnp.where((f >= ROPE_DIM // 2) & (f < ROPE_DIM), si, 0.0)

    row = pl.BlockSpec((tm, LANES), lambda i: (i, 0))
    return pl.pallas_call(
        body, name=name, grid=(t // tm,),
        in_specs=[pl.BlockSpec((tm, 1), lambda i: (i, 0)), pl.BlockSpec((1, LANES), lambda i: (0, 0))],
        out_specs=[row, row, row], out_shape=[jax.ShapeDtypeStruct((t, LANES), f32)] * 3,
        compiler_params=_cparams(("parallel",)),
    )(pos_col, _lane_tables())


def _rot(x, c, s1, s2):
    return x * c + pltpu.roll(x, LANES - ROPE_DIM // 2, 1) * s1 + pltpu.roll(x, ROPE_DIM // 2, 1) * s2


def _rot_t(g, c, s1, s2):
    return g * c + pltpu.roll(g * s1, ROPE_DIM // 2, 1) + pltpu.roll(g * s2, LANES - ROPE_DIM // 2, 1)


def _dup_head(x, kvh, low):
    a = jnp.where(kvh == 0, x, pltpu.roll(x, HEAD_DIM, 1))
    return jnp.where(low, a, pltpu.roll(a, HEAD_DIM, 1))


def _deinterleave(src_ref, dst_ref, d, dtype):
    length = SEQ // d
    if d == 1:
        dst_ref[...] = src_ref[...].astype(dtype)
    else:
        for r in range(d):
            dst_ref[pl.ds(r * length, length), :] = src_ref[pl.ds(r, length, stride=d), :].astype(dtype)


def _interleave_store(src_ref, dst_ref, d, accumulate):
    length = SEQ // d
    if d == 1:
        if accumulate:
            dst_ref[...] += src_ref[...]
        else:
            dst_ref[...] = src_ref[...]
    else:
        for r in range(d):
            blk = src_ref[pl.ds(r * length, length), :]
            if accumulate:
                dst_ref[pl.ds(r, length, stride=d), :] = dst_ref[pl.ds(r, length, stride=d), :] + blk
            else:
                dst_ref[pl.ds(r, length, stride=d), :] = blk


def _attn_masks():
    qi = lax.broadcasted_iota(jnp.int32, (ATTN_BLOCK, ATTN_BLOCK), 0)
    ki = lax.broadcasted_iota(jnp.int32, (ATTN_BLOCK, ATTN_BLOCK), 1)
    low = lax.broadcasted_iota(jnp.int32, (ATTN_BLOCK, LANES), 1) < HEAD_DIM
    return ki <= qi, ki >= qi, low


NEG_INF = float("-inf")


N_BRANCH = len(DILATIONS)


def _attn_prep(qkv, tabs, name):
    t = qkv.shape[0]
    nb = t // SEQ
    n_j = ATTN_WIDTH // LANES

    def q_body(q_ref, c_ref, s1_ref, s2_ref, out_ref, xr):
        xr[...] = _rot(q_ref[...], c_ref[...], s1_ref[...], s2_ref[...]) * (HEAD_DIM ** -0.5)
        for bi, d in enumerate(DILATIONS):
            _deinterleave(xr, out_ref.at[bi], d, bf16)

    def kv_body(x_ref, c_ref, s1_ref, s2_ref, out_ref, xr):
        lowfull = lax.broadcasted_iota(jnp.int32, (SEQ, LANES), 1) < HEAD_DIM
        x = x_ref[...]
        x = jnp.where(pl.program_id(1) == 0, _rot(x, c_ref[...], s1_ref[...], s2_ref[...]), x)
        for kvh in range(N_KV_HEADS):
            xr[...] = _dup_head(x, kvh, lowfull)
            for bi, d in enumerate(DILATIONS):
                length = SEQ // d
                for r in range(d):
                    rows = xr[...] if d == 1 else xr[pl.ds(r, length, stride=d), :]
                    out_ref[0, bi, pl.ds(r * length, length), kvh * LANES:(kvh + 1) * LANES] = rows.astype(bf16)

    tab = pl.BlockSpec((SEQ, LANES), lambda b, j: (b, 0))
    q = pl.pallas_call(
        q_body, name=name + "_q", grid=(nb, n_j),
        in_specs=[pl.BlockSpec((SEQ, LANES), lambda b, j: (b, j)), tab, tab, tab],
        out_specs=pl.BlockSpec((N_BRANCH, SEQ, LANES), lambda b, j: (0, b, j)),
        out_shape=jax.ShapeDtypeStruct((N_BRANCH, t, ATTN_WIDTH), bf16), scratch_shapes=[pltpu.VMEM((SEQ, LANES), f32)],
        compiler_params=_cparams(("parallel", "parallel")),
    )(qkv, *tabs)
    kv = pl.pallas_call(
        kv_body, name=name + "_kv", grid=(nb, 2),
        in_specs=[pl.BlockSpec((SEQ, LANES), lambda b, j: (b, n_j + j)), tab, tab, tab],
        out_specs=pl.BlockSpec((1, N_BRANCH, SEQ, N_KV_HEADS * LANES), lambda b, j: (j, 0, b, 0)),
        out_shape=jax.ShapeDtypeStruct((2, N_BRANCH, t, N_KV_HEADS * LANES), bf16), scratch_shapes=[pltpu.VMEM((SEQ, LANES), f32)],
        compiler_params=_cparams(("parallel", "parallel")),
    )(qkv, *tabs)
    return q, kv


def _attn_fwd(prep, name):
    q_all, kv_all = prep
    t = q_all.shape[1]
    nb = t // SEQ
    n_blk = SEQ // ATTN_BLOCK

    def body(q_ref, k_ref, v_ref, o_ref, lse_ref, ob, lb, o0, o1, o2, l0, l1, l2, ss):
        cur_ok, prev_ok, low = _attn_masks()
        onat, lnat = (o0, o1, o2), (l0, l1, l2)
        for bi, d in enumerate(DILATIONS):
            qd, kd, vd = q_ref.at[bi], k_ref.at[0, bi], v_ref.at[0, bi]
            per_res = n_blk // d
            use_prev = per_res > 1

            def scores(n, carry):
                start = pl.multiple_of(n * ATTN_BLOCK, ATTN_BLOCK)
                has_prev = (n % per_res) != 0
                pstart = pl.multiple_of(jnp.maximum(n - 1, 0) * ATTN_BLOCK, ATTN_BLOCK)
                qb = qd[pl.ds(start, ATTN_BLOCK), :]
                kc = kd[pl.ds(start, ATTN_BLOCK), :]
                if use_prev:
                    kp = kd[pl.ds(pstart, ATTN_BLOCK), :]
                for a in range(2):
                    qa = jnp.where(low if a == 0 else ~low, qb, jnp.zeros_like(qb))
                    ss[2 * n + a, :, 0:ATTN_BLOCK] = jnp.where(cur_ok, _nt(qa, kc), NEG_INF)
                    if use_prev:
                        ss[2 * n + a, :, ATTN_BLOCK:2 * ATTN_BLOCK] = jnp.where(prev_ok & has_prev, _nt(qa, kp), NEG_INF)
                return carry

            def softmax_pv(n, carry):
                start = pl.multiple_of(n * ATTN_BLOCK, ATTN_BLOCK)
                pstart = pl.multiple_of(jnp.maximum(n - 1, 0) * ATTN_BLOCK, ATTN_BLOCK)
                vc = vd[pl.ds(start, ATTN_BLOCK), :]
                if use_prev:
                    vp = vd[pl.ds(pstart, ATTN_BLOCK), :]
                outs, lses = [], []
                for a in range(2):
                    sc = ss[2 * n + a, :, 0:ATTN_BLOCK]
                    if use_prev:
                        sp = ss[2 * n + a, :, ATTN_BLOCK:2 * ATTN_BLOCK]
                        m = jnp.max(jnp.maximum(sc, sp), axis=1, keepdims=True)
                        pc, pp = jnp.exp(sc - m), jnp.exp(sp - m)
                        den = jnp.sum(pc + pp, axis=1, keepdims=True)
                        acc = _nn(pc.astype(bf16), vc) + _nn(pp.astype(bf16), vp)
                    else:
                        m = jnp.max(sc, axis=1, keepdims=True)
                        pc = jnp.exp(sc - m)
                        den = jnp.sum(pc, axis=1, keepdims=True)
                        acc = _nn(pc.astype(bf16), vc)
                    outs.append(acc * (1.0 / den))
                    lses.append(m + jnp.log(den))
                ob[pl.ds(start, ATTN_BLOCK), :] = jnp.where(low, outs[0], outs[1])
                lb[pl.ds(start, ATTN_BLOCK), :] = jnp.where(low, lses[0], lses[1])
                return carry

            lax.fori_loop(0, n_blk, scores, 0, unroll=n_blk)
            lax.fori_loop(0, n_blk, softmax_pv, 0, unroll=n_blk)
            _interleave_store(ob, onat[bi], d, False)
            _interleave_store(lb, lnat[bi], d, False)
        la, lbb, lc = l0[...], l1[...], l2[...]
        lm = jnp.maximum(jnp.maximum(la, lbb), lc)
        wa, wb, wc = jnp.exp(la - lm), jnp.exp(lbb - lm), jnp.exp(lc - lm)
        ws = wa + wb + wc
        o_ref[...] = (wa * o0[...] + wb * o1[...] + wc * o2[...]) / ws
        lse_ref[...] = lm + jnp.log(ws)

    def col(jj):
        return pl.BlockSpec((SEQ, LANES), lambda b, j: (b, jj if jj is not None else j))

    fs = pltpu.VMEM((SEQ, LANES), f32)
    return pl.pallas_call(
        body, name=name, grid=(nb, ATTN_WIDTH // LANES),
        in_specs=[pl.BlockSpec((N_BRANCH, SEQ, LANES), lambda b, j: (0, b, j)),
                  pl.BlockSpec((1, N_BRANCH, SEQ, LANES), lambda b, j: (0, 0, b, j // 2)),
                  pl.BlockSpec((1, N_BRANCH, SEQ, LANES), lambda b, j: (1, 0, b, j // 2))],
        out_specs=[col(None), col(None)],
        out_shape=[jax.ShapeDtypeStruct((t, ATTN_WIDTH), f32), jax.ShapeDtypeStruct((t, ATTN_WIDTH), f32)],
        scratch_shapes=[fs, fs, fs, fs, fs, fs, fs, fs, pltpu.VMEM((2 * n_blk, ATTN_BLOCK, 2 * ATTN_BLOCK), f32)],
        compiler_params=_cparams(("parallel", "parallel")),
    )(q_all, kv_all, kv_all)


def _attn_bwd(prep, tabs, o, lse, do, name):
    q_all, kv_all = prep
    t = q_all.shape[1]
    nb = t // SEQ
    n_blk = SEQ // ATTN_BLOCK
    n_j = ATTN_WIDTH // LANES

    def body(q_ref, k_ref, v_ref, c_ref, s1_ref, s2_ref, o_ref, lse_ref, do_ref, dq_ref, dk_ref, dv_ref,
             dl, dod, lsd, dld, dqd, dkd, dvd, dqa, dka, dva, pb, dsb, dk_acc, dv_acc):
        j = pl.program_id(1)
        pb[2 * n_blk:2 * n_blk + 2] = jnp.zeros((2, ATTN_BLOCK, 2 * ATTN_BLOCK), bf16)
        dsb[2 * n_blk:2 * n_blk + 2] = jnp.zeros((2, ATTN_BLOCK, 2 * ATTN_BLOCK), bf16)
        kvh = j // 2
        cur_ok, prev_ok, low = _attn_masks()
        lowfull = lax.broadcasted_iota(jnp.int32, (SEQ, LANES), 1) < HEAD_DIM
        c, s1, s2 = c_ref[...], s1_ref[...], s2_ref[...]
        prod = do_ref[...] * o_ref[...]
        d_lo = jnp.sum(jnp.where(lowfull, prod, 0.0), axis=1, keepdims=True)
        d_hi = jnp.sum(jnp.where(lowfull, 0.0, prod), axis=1, keepdims=True)
        dl[...] = jnp.where(lowfull, d_lo, d_hi)
        dqa[...] = jnp.zeros_like(dqa)
        dka[...] = jnp.zeros_like(dka)
        dva[...] = jnp.zeros_like(dva)
        for bi, d in enumerate(DILATIONS):
            qd, kd, vd = q_ref.at[bi], k_ref.at[0, bi], v_ref.at[0, bi]
            _deinterleave(do_ref, dod, d, bf16)
            _deinterleave(lse_ref, lsd, d, f32)
            _deinterleave(dl, dld, d, f32)
            per_res = n_blk // d
            use_prev = per_res > 1
            curl, prevl = slice(0, ATTN_BLOCK), slice(ATTN_BLOCK, 2 * ATTN_BLOCK)

            def halves(x):
                zero = jnp.zeros_like(x)
                return jnp.where(low, x, zero), jnp.where(low, zero, x)

            def probs(n, carry):
                start = pl.multiple_of(n * ATTN_BLOCK, ATTN_BLOCK)
                has_prev = (n % per_res) != 0
                pstart = pl.multiple_of(jnp.maximum(n - 1, 0) * ATTN_BLOCK, ATTN_BLOCK)
                cur, prev = pl.ds(start, ATTN_BLOCK), pl.ds(pstart, ATTN_BLOCK)
                qas, doas = halves(qd[cur, :]), halves(dod[cur, :])
                kc, vc = kd[cur, :], vd[cur, :]
                if use_prev:
                    kp, vp = kd[prev, :], vd[prev, :]
                lsb, dlb = lsd[cur, :], dld[cur, :]
                for a in range(2):
                    ls = lsb[:, a * HEAD_DIM:a * HEAD_DIM + 1]
                    de = dlb[:, a * HEAD_DIM:a * HEAD_DIM + 1]
                    pc = jnp.exp(jnp.where(cur_ok, _nt(qas[a], kc), NEG_INF) - ls)
                    pb[2 * n + a, :, curl] = pc.astype(bf16)
                    dsb[2 * n + a, :, curl] = (pc * (_nt(doas[a], vc) - de)).astype(bf16)
                    if use_prev:
                        pp = jnp.exp(jnp.where(prev_ok & has_prev, _nt(qas[a], kp), NEG_INF) - ls)
                        pb[2 * n + a, :, prevl] = pp.astype(bf16)
                        dsb[2 * n + a, :, prevl] = (pp * (_nt(doas[a], vp) - de)).astype(bf16)
                return carry

            def grads(n, carry):
                start = pl.multiple_of(n * ATTN_BLOCK, ATTN_BLOCK)
                pstart = pl.multiple_of(jnp.maximum(n - 1, 0) * ATTN_BLOCK, ATTN_BLOCK)
                nstart = pl.multiple_of(jnp.minimum(n + 1, n_blk - 1) * ATTN_BLOCK, ATTN_BLOCK)
                cur, prev, nxt = pl.ds(start, ATTN_BLOCK), pl.ds(pstart, ATTN_BLOCK), pl.ds(nstart, ATTN_BLOCK)
                kc = kd[cur, :]
                dqs = [_nn(dsb[2 * n + a, :, curl], kc) for a in range(2)]
                q_rows, do_rows = list(halves(qd[cur, :])), list(halves(dod[cur, :]))
                ds_rows, p_rows = [dsb[2 * n + a, :, curl] for a in range(2)], [pb[2 * n + a, :, curl] for a in range(2)]
                if use_prev:
                    kp = kd[prev, :]
                    dqs = [dqs[a] + _nn(dsb[2 * n + a, :, prevl], kp) for a in range(2)]
                    q_rows += list(halves(qd[nxt, :]))
                    do_rows += list(halves(dod[nxt, :]))
                    ds_rows += [dsb[2 * n + 2 + a, :, prevl] for a in range(2)]
                    p_rows += [pb[2 * n + 2 + a, :, prevl] for a in range(2)]
                dqd[cur, :] = jnp.where(low, dqs[0], dqs[1])
                dkd[cur, :] = _tn(jnp.concatenate(ds_rows, axis=0), jnp.concatenate(q_rows, axis=0))
                dvd[cur, :] = _tn(jnp.concatenate(p_rows, axis=0), jnp.concatenate(do_rows, axis=0))
                return carry

            lax.fori_loop(0, n_blk, probs, 0, unroll=n_blk)
            lax.fori_loop(0, n_blk, grads, 0, unroll=n_blk)
            _interleave_store(dqd, dqa, d, True)
            _interleave_store(dkd, dka, d, True)
            _interleave_store(dvd, dva, d, True)
        dq_ref[...] = _rot_t(dqa[...] * (HEAD_DIM ** -0.5), c, s1, s2).astype(bf16)
        dkf = dka[...]
        dkf = _rot_t(dkf + pltpu.roll(dkf, HEAD_DIM, 1), c, s1, s2)
        dvf = dva[...]
        dvf = dvf + pltpu.roll(dvf, HEAD_DIM, 1)
        mine = (lax.broadcasted_iota(jnp.int32, (SEQ, LANES), 1) // HEAD_DIM) == kvh
        dkc_, dvc_ = jnp.where(mine, dkf, 0.0), jnp.where(mine, dvf, 0.0)

        @pl.when(j == 0)
        def _():
            dk_acc[...] = dkc_
            dv_acc[...] = dvc_

        @pl.when(j > 0)
        def _():
            dk_acc[...] += dkc_
            dv_acc[...] += dvc_

        @pl.when(j == n_j - 1)
        def _():
            dk_ref[...] = dk_acc[...].astype(bf16)
            dv_ref[...] = dv_acc[...].astype(bf16)

    def col(jj):
        return pl.BlockSpec((SEQ, LANES), lambda b, j: (b, jj if jj is not None else j))

    tab = pl.BlockSpec((SEQ, LANES), lambda b, j: (b, 0))
    fs = pltpu.VMEM((SEQ, LANES), f32)
    hs = pltpu.VMEM((SEQ, LANES), bf16)
    return pl.pallas_call(
        body, name=name, grid=(nb, n_j),
        in_specs=[pl.BlockSpec((N_BRANCH, SEQ, LANES), lambda b, j: (0, b, j)),
                  pl.BlockSpec((1, N_BRANCH, SEQ, LANES), lambda b, j: (0, 0, b, j // 2)),
                  pl.BlockSpec((1, N_BRANCH, SEQ, LANES), lambda b, j: (1, 0, b, j // 2)),
                  tab, tab, tab, col(None), col(None), col(None)],
        out_specs=[col(None), tab, tab],
        out_shape=[jax.ShapeDtypeStruct((t, ATTN_WIDTH), bf16), jax.ShapeDtypeStruct((t, LANES), bf16), jax.ShapeDtypeStruct((t, LANES), bf16)],
        scratch_shapes=[fs, hs, fs, fs, fs, fs, fs, fs, fs, fs,
                        pltpu.VMEM((2 * n_blk + 2, ATTN_BLOCK, 2 * ATTN_BLOCK), bf16), pltpu.VMEM((2 * n_blk + 2, ATTN_BLOCK, 2 * ATTN_BLOCK), bf16), fs, fs],
        compiler_params=_cparams(("parallel", "arbitrary")),
    )(q_all, kv_all, kv_all, *tabs, o, lse, do)


def _tap(w_ref, s):
    return w_ref[CONV_WIDTH - 1 - s:CONV_WIDTH - s, :]


def _conv_pre(x, w_ref, b_ref, row):
    shifted = [x] + [jnp.where(row >= s, pltpu.roll(x, s, 0), 0.0) for s in range(1, CONV_WIDTH)]
    pre = b_ref[...] + _tap(w_ref, 0) * x
    for s in range(1, CONV_WIDTH):
        pre = pre + _tap(w_ref, s) * shifted[s]
    return pre, shifted


def _conv_fwd(x, w, b, name, tc=512):
    t, ch = x.shape

    def body(x_ref, w_ref, b_ref, o_ref):
        row = lax.broadcasted_iota(jnp.int32, (SEQ, tc), 0)
        pre, _ = _conv_pre(x_ref[...], w_ref, b_ref, row)
        o_ref[...] = _silu(pre)

    xs = pl.BlockSpec((SEQ, tc), lambda i, j: (i, j))
    return pl.pallas_call(
        body, name=name, grid=(t // SEQ, ch // tc),
        in_specs=[xs, pl.BlockSpec((CONV_WIDTH, tc), lambda i, j: (0, j)), pl.BlockSpec((1, tc), lambda i, j: (0, j))],
        out_specs=xs, out_shape=jax.ShapeDtypeStruct((t, ch), f32),
        compiler_params=_cparams(("parallel", "parallel")),
    )(x, w, b)


def _conv_bwd(x, w, b, dact, name, tc=512):
    t, ch = x.shape

    def body(x_ref, w_ref, b_ref, d_ref, dx_ref, dw_ref, db_ref):
        row = lax.broadcasted_iota(jnp.int32, (SEQ, tc), 0)
        pre, shifted = _conv_pre(x_ref[...], w_ref, b_ref, row)
        dpre = d_ref[...] * _dsilu(pre)
        dx = _tap(w_ref, 0) * dpre
        for s in range(1, CONV_WIDTH):
            dx = dx + _tap(w_ref, s) * jnp.where(row < SEQ - s, pltpu.roll(dpre, SEQ - s, 0), 0.0)
        dx_ref[...] = dx.astype(bf16)
        first = pl.program_id(1) == 0
        parts = [jnp.sum(dpre * shifted[CONV_WIDTH - 1 - k], axis=0, keepdims=True) for k in range(CONV_WIDTH)]
        dbp = jnp.sum(dpre, axis=0, keepdims=True)

        @pl.when(first)
        def _():
            for k in range(CONV_WIDTH):
                dw_ref[k:k + 1, :] = parts[k]
            db_ref[...] = dbp

        @pl.when(jnp.logical_not(first))
        def _():
            for k in range(CONV_WIDTH):
                dw_ref[k:k + 1, :] += parts[k]
            db_ref[...] += dbp

    xs = pl.BlockSpec((SEQ, tc), lambda j, i: (i, j))
    ws = pl.BlockSpec((CONV_WIDTH, tc), lambda j, i: (0, j))
    bs = pl.BlockSpec((1, tc), lambda j, i: (0, j))
    return pl.pallas_call(
        body, name=name, grid=(ch // tc, t // SEQ),
        in_specs=[xs, ws, bs, xs], out_specs=[xs, ws, bs],
        out_shape=[jax.ShapeDtypeStruct((t, ch), bf16), jax.ShapeDtypeStruct((CONV_WIDTH, ch), f32), jax.ShapeDtypeStruct((1, ch), f32)],
        compiler_params=_cparams(("parallel", "arbitrary")),
    )(x, w, b, dact)


GROUP_W = SSM_INNER // SSM_GROUPS
HEADS_PER_GROUP = SSM_HEADS // SSM_GROUPS


def _split3(x):
    hi = x.astype(bf16)
    r1 = x - hi.astype(f32)
    mid = r1.astype(bf16)
    lo = (r1 - mid.astype(f32)).astype(bf16)
    return hi, mid, lo


def _dot_exact(x, sel, dims, x_is_lhs=True):
    parts = _split3(x)
    if x_is_lhs:
        return _dot(parts[0], sel, dims) + _dot(parts[1], sel, dims) + _dot(parts[2], sel, dims)
    return _dot(sel, parts[0], dims) + _dot(sel, parts[1], dims) + _dot(sel, parts[2], dims)


def _ssd_common(xbc_ref, dt_ref, bias_ref, alog_ref):
    r = lax.broadcasted_iota(jnp.int32, (CHUNK, CHUNK), 0)
    cidx = lax.broadcasted_iota(jnp.int32, (CHUNK, CHUNK), 1)
    causal = r >= cidx
    tril = causal.astype(bf16)
    expand = (lax.broadcasted_iota(jnp.int32, (CHUNK, SSM_INNER), 0)
              == lax.broadcasted_iota(jnp.int32, (CHUNK, SSM_INNER), 1) // HEAD_DIM).astype(bf16)
    head_lane = cidx < SSM_HEADS
    dtp = dt_ref[...] + bias_ref[...]
    dt = jnp.where(head_lane, _softplus(dtp), 0.0)
    a_neg = -jnp.exp(alog_ref[...])
    a = dt * a_neg
    nn_dims = ((1,), (0,))
    cs = _dot_exact(a, tril, nn_dims, x_is_lhs=False)
    dt_e = _dot_exact(dt, expand, nn_dims)
    cs_e = _dot_exact(cs, expand, nn_dims)
    xs = xbc_ref[:, 0:SSM_INNER]
    xg = xs * dt_e
    ecs = jnp.exp(cs_e)
    cs_last = cs_e[CHUNK - 1:CHUNK, :]
    dse = jnp.exp(cs_last - cs_e)
    cde = jnp.exp(cs_last)
    return dict(r=r, cidx=cidx, causal=causal, tril=tril, expand=expand, head_lane=head_lane, dtp=dtp, dt=dt, a_neg=a_neg,
                cs=cs, cst=cs.T, dt_e=dt_e, cs_e=cs_e, xs=xs, xg=xg, ecs=ecs, dse=dse, cde=cde)


def _decay_mat(q, h):
    return jnp.exp(jnp.where(q["causal"], q["cs"][:, h:h + 1] - q["cst"][h:h + 1, :], NEG_INF))


def _gate_norm(y, z, nw, gate=None):
    y2 = y * (_silu(z) if gate is None else gate)
    outs, xhats, rs = [], [], []
    for g in range(SSM_GROUPS):
        sl = slice(g * GROUP_W, (g + 1) * GROUP_W)
        yg = y2[:, sl]
        r = lax.rsqrt(jnp.mean(yg * yg, axis=-1, keepdims=True) + EPS)
        xhats.append(yg * r)
        rs.append(r)
        outs.append(yg * r * nw[:, sl])
    return y2, outs, xhats, rs


def _ssd_fwd(xbc, z, dtp, params, name):
    t = xbc.shape[0]
    n_chunk = SEQ // CHUNK
    low = None

    def body(xbc_ref, z_ref, dt_ref, bias_ref, alog_ref, dskip_ref, nw_ref, yn_ref, y_ref, hs_ref, h_scr):
        @pl.when(pl.program_id(1) == 0)
        def _():
            h_scr[...] = jnp.zeros_like(h_scr)

        q = _ssd_common(xbc_ref, dt_ref, bias_ref, alog_ref)
        low = lax.broadcasted_iota(jnp.int32, (CHUNK, LANES), 1) < HEAD_DIM
        xgb = q["xg"].astype(bf16)
        wst = (q["xg"] * q["dse"]).astype(bf16)
        hs_ref[0] = h_scr[...]
        ys = []
        for g in range(SSM_GROUPS):
            gl = slice(g * GROUP_W, (g + 1) * GROUP_W)
            bg = xbc_ref[:, SSM_INNER + g * D_STATE:SSM_INNER + (g + 1) * D_STATE].astype(bf16)
            cg = xbc_ref[:, SSM_INNER + SSM_GROUPS * D_STATE + g * D_STATE:SSM_INNER + SSM_GROUPS * D_STATE + (g + 1) * D_STATE].astype(bf16)
            cb = _nt(cg, bg)
            hg = h_scr[g]
            yoff = _nn(cg, hg.astype(bf16)) * q["ecs"][:, gl]
            pieces = []
            for i in range(HEADS_PER_GROUP // 2):
                h0 = g * HEADS_PER_GROUP + 2 * i
                xp = xgb[:, h0 * HEAD_DIM:(h0 + 2) * HEAD_DIM]
                m0 = (cb * _decay_mat(q, h0)).astype(bf16)
                m1 = (cb * _decay_mat(q, h0 + 1)).astype(bf16)
                zero = jnp.zeros_like(xp)
                pieces.append(_nn(m0, jnp.where(low, xp, zero)) + _nn(m1, jnp.where(low, zero, xp)))
            ys.append(jnp.concatenate(pieces, axis=1) + yoff + dskip_ref[:, gl] * q["xs"][:, gl])
            h_scr[g] = hg * q["cde"][:, gl] + _tn(bg, wst[:, gl])
        y = jnp.concatenate(ys, axis=1)
        y_ref[...] = y
        _, outs, _, _ = _gate_norm(y, z_ref[...], nw_ref[...])
        yn_ref[...] = jnp.concatenate(outs, axis=1).astype(bf16)

    def rows(w):
        return pl.BlockSpec((CHUNK, w), lambda b, c: (b * n_chunk + c, 0))

    def par(w):
        return pl.BlockSpec((1, w), lambda b, c: (0, 0))

    return pl.pallas_call(
        body, name=name, grid=(t // SEQ, n_chunk),
        in_specs=[rows(CONV_CH), rows(SSM_INNER), rows(LANES), par(LANES), par(LANES), par(SSM_INNER), par(SSM_INNER)],
        out_specs=[rows(SSM_INNER), rows(SSM_INNER), pl.BlockSpec((1, SSM_GROUPS, D_STATE, GROUP_W), lambda b, c: (b * n_chunk + c, 0, 0, 0))],
        out_shape=[jax.ShapeDtypeStruct((t, SSM_INNER), bf16), jax.ShapeDtypeStruct((t, SSM_INNER), f32),
                   jax.ShapeDtypeStruct((t // CHUNK, SSM_GROUPS, D_STATE, GROUP_W), f32)],
        scratch_shapes=[pltpu.VMEM((SSM_GROUPS, D_STATE, GROUP_W), f32)],
        compiler_params=_cparams(("parallel", "arbitrary")),
    )(xbc, z, dtp, *params)


def _ssd_bwd(xbc, z, dtp, y, hs, dyn, params, name):
    t = xbc.shape[0]
    n_chunk = SEQ // CHUNK

    def body(xbc_ref, z_ref, dt_ref, y_ref, hs_ref, dyn_ref, bias_ref, alog_ref, dskip_ref, nw_ref,
             dxbc_ref, dz_ref, ddt_ref, dnw_ref, dds_ref, dal_ref, dbi_ref, dh_scr):
        @pl.when(pl.program_id(1) == 0)
        def _():
            dh_scr[...] = jnp.zeros_like(dh_scr)

        q = _ssd_common(xbc_ref, dt_ref, bias_ref, alog_ref)
        low = lax.broadcasted_iota(jnp.int32, (CHUNK, LANES), 1) < HEAD_DIM
        last_row = lax.broadcasted_iota(jnp.int32, (CHUNK, GROUP_W), 0) == CHUNK - 1
        xs, xg = q["xs"], q["xg"]
        xgb = xg.astype(bf16)
        wf = xg * q["dse"]
        wst = wf.astype(bf16)
        zz = z_ref[...]
        yy = y_ref[...]
        sz, dsz = _silu_and_grad(zz)
        y2, _, xhats, rs = _gate_norm(yy, zz, nw_ref[...], gate=sz)
        dyn_ = dyn_ref[...]
        dy2s, dnws = [], []
        for g in range(SSM_GROUPS):
            gl = slice(g * GROUP_W, (g + 1) * GROUP_W)
            gw = dyn_[:, gl] * nw_ref[:, gl]
            dy2s.append(rs[g] * (gw - xhats[g] * jnp.mean(gw * xhats[g], axis=-1, keepdims=True)))
            dnws.append(_rowsum8(dyn_[:, gl] * xhats[g]))
        dy2 = jnp.concatenate(dy2s, axis=1)
        dy = dy2 * sz
        dz_ref[...] = (dy2 * yy * dsz).astype(bf16)
        dnw_p = jnp.concatenate(dnws, axis=1)
        dds_p = _rowsum8(dy * xs)
        dyb = dy.astype(bf16)
        gfull = (dy * q["ecs"]).astype(bf16)
        dcs_c = jnp.zeros((CHUNK, CHUNK), f32)
        dcs_r = jnp.zeros((CHUNK, CHUNK), f32)
        dcs_e_parts, dxg_parts = [], []
        for g in range(SSM_GROUPS):
            gl = slice(g * GROUP_W, (g + 1) * GROUP_W)
            bsl = slice(SSM_INNER + g * D_STATE, SSM_INNER + (g + 1) * D_STATE)
            csl = slice(SSM_INNER + SSM_GROUPS * D_STATE + g * D_STATE, SSM_INNER + SSM_GROUPS * D_STATE + (g + 1) * D_STATE)
            bg = xbc_ref[:, bsl].astype(bf16)
            cg = xbc_ref[:, csl].astype(bf16)
            cb = _nt(cg, bg)
            hg = hs_ref[0, g]
            hgb = hg.astype(bf16)
            dhn = dh_scr[g]
            dhnb = dhn.astype(bf16)
            yoff = _nn(cg, hgb) * q["ecs"][:, gl]
            dw_ = _nn(bg, dhnb)
            r_e = dw_ * wf[:, gl]
            to_last = jnp.sum(r_e, axis=0, keepdims=True) + jnp.sum(dhn * hg, axis=0, keepdims=True) * q["cde"][:, gl]
            dcs_e_parts.append(dy[:, gl] * yoff - r_e + jnp.where(last_row, to_last, 0.0))
            dcb = jnp.zeros((CHUNK, CHUNK), f32)
            dxg_pairs = []
            for i in range(HEADS_PER_GROUP // 2):
                h0 = g * HEADS_PER_GROUP + 2 * i
                psl = slice(h0 * HEAD_DIM, (h0 + 2) * HEAD_DIM)
                xp = xgb[:, psl]
                dyp = dyb[:, psl]
                zero = jnp.zeros_like(dyp)
                tns = []
                for a in range(2):
                    h = h0 + a
                    lm = _decay_mat(q, h)
                    m = cb * lm
                    dm = _nt(jnp.where(low, dyp, zero) if a == 0 else jnp.where(low, zero, dyp), xp)
                    dcb = dcb + dm * lm
                    nmat = dm * m
                    dcs_c = dcs_c + jnp.where(q["cidx"] == h, jnp.sum(nmat, axis=1, keepdims=True), 0.0)
                    dcs_r = dcs_r + jnp.where(q["r"] == h, jnp.sum(nmat, axis=0, keepdims=True), 0.0)
                    tns.append(_tn(m.astype(bf16), dyp))
                dxg_pairs.append(jnp.where(low, tns[0], tns[1]))
            dxg_parts.append(jnp.concatenate(dxg_pairs, axis=1) + dw_ * q["dse"][:, gl])
            dcbb = dcb.astype(bf16)
            dxbc_ref[:, csl] = _nt(gfull[:, gl], hgb) + _nn(dcbb, bg)
            dxbc_ref[:, bsl] = _nt(wst[:, gl], dhnb) + _tn(dcbb, cg)
            dh_scr[g] = dhn * q["cde"][:, gl] + _tn(cg, gfull[:, gl])
        dxg = jnp.concatenate(dxg_parts, axis=1)
        dcs_e = jnp.concatenate(dcs_e_parts, axis=1)
        dxbc_ref[:, 0:SSM_INNER] = dskip_ref[...] * dy + dxg * q["dt_e"]
        dcs = dcs_c - dcs_r.T + _dot_exact(dcs_e, q["expand"], ((1,), (1,)))
        triu = (q["cidx"] >= q["r"]).astype(bf16)
        da = _dot_exact(dcs, triu, ((1,), (0,)), x_is_lhs=False)
        ddt = _dot_exact(dxg * xs, q["expand"], ((1,), (1,))) + da * q["a_neg"]
        ddtp = jnp.where(q["head_lane"], ddt * _sigmoid(q["dtp"]), 0.0)
        ddt_ref[...] = ddtp.astype(bf16)
        dal_p = _rowsum8(da * q["dt"]) * q["a_neg"]
        dbi_p = _rowsum8(ddtp)
        first = (pl.program_id(0) == 0) & (pl.program_id(1) == 0)

        @pl.when(first)
        def _():
            dnw_ref[...] = dnw_p
            dds_ref[...] = dds_p
            dal_ref[...] = dal_p
            dbi_ref[...] = dbi_p

        @pl.when(jnp.logical_not(first))
        def _():
            dnw_ref[...] += dnw_p
            dds_ref[...] += dds_p
            dal_ref[...] += dal_p
            dbi_ref[...] += dbi_p

    def rows(w):
        return pl.BlockSpec((CHUNK, w), lambda b, c: (b * n_chunk + n_chunk - 1 - c, 0))

    def par(w):
        return pl.BlockSpec((1, w), lambda b, c: (0, 0))

    def acc(w):
        return pl.BlockSpec((SUBLANES, w), lambda b, c: (0, 0))

    return pl.pallas_call(
        body, name=name, grid=(t // SEQ, n_chunk),
        in_specs=[rows(CONV_CH), rows(SSM_INNER), rows(LANES), rows(SSM_INNER),
                  pl.BlockSpec((1, SSM_GROUPS, D_STATE, GROUP_W), lambda b, c: (b * n_chunk + n_chunk - 1 - c, 0, 0, 0)),
                  rows(SSM_INNER), par(LANES), par(LANES), par(SSM_INNER), par(SSM_INNER)],
        out_specs=[rows(CONV_CH), rows(SSM_INNER), rows(LANES), acc(SSM_INNER), acc(SSM_INNER), acc(LANES), acc(LANES)],
        out_shape=[jax.ShapeDtypeStruct((t, CONV_CH), f32), jax.ShapeDtypeStruct((t, SSM_INNER), bf16), jax.ShapeDtypeStruct((t, LANES), bf16),
                   jax.ShapeDtypeStruct((SUBLANES, SSM_INNER), f32), jax.ShapeDtypeStruct((SUBLANES, SSM_INNER), f32),
                   jax.ShapeDtypeStruct((SUBLANES, LANES), f32), jax.ShapeDtypeStruct((SUBLANES, LANES), f32)],
        scratch_shapes=[pltpu.VMEM((SSM_GROUPS, D_STATE, GROUP_W), f32)],
        compiler_params=_cparams(("arbitrary", "arbitrary")),
    )(xbc, z, dtp, y, hs, dyn, *params)


def _adamw_update(g, w, m, v):
    mm = ADAM_B1 * m + (1.0 - ADAM_B1) * g
    vv = ADAM_B2 * v + (1.0 - ADAM_B2) * (g * g)
    m_hat = mm / (1.0 - ADAM_B1 ** ADAM_STEP)
    v_hat = vv / (1.0 - ADAM_B2 ** ADAM_STEP)
    return -ADAM_LR * (m_hat / (jnp.sqrt(v_hat) + ADAM_EPS) + ADAM_WD * w), mm, vv


def _adamw(g_parts, w, m, v, name):
    rows, width = w.shape
    n = len(g_parts)
    tr = _row_tile(rows)

    def body(*refs):
        g_refs, (w_ref, m_ref, v_ref, g_out, d_out, m_out, v_out) = refs[:n], refs[n:]
        g = g_refs[0][...].astype(f32)
        for r in g_refs[1:]:
            g = g + r[...].astype(f32)
        g_out[...] = g
        d_out[...], m_out[...], v_out[...] = _adamw_update(g, w_ref[...], m_ref[...], v_ref[...])

    spec = pl.BlockSpec((tr, width), lambda i: (i, 0))
    return pl.pallas_call(
        body, name=name, grid=(rows // tr,), in_specs=[spec] * (n + 3), out_specs=[spec] * 4,
        out_shape=[jax.ShapeDtypeStruct((rows, width), f32)] * 4, compiler_params=_cparams(("parallel",)),
    )(*g_parts, w, m, v)


def _adamw_layers(landed, w, m, v, after, name, layers_on_columns=False):
    depth = len(landed)
    _, rows, width = landed[0].shape
    tr = _row_tile(rows)
    n_i = rows // tr
    at = (lambda ref: ref) if layers_on_columns else (lambda ref: ref.at[0])

    def body(*refs):
        part_refs, (w_ref, m_ref, v_ref, _, g_out, d_out, m_out, v_out) = refs[:depth * N_DEV], refs[depth * N_DEV:]
        for l in range(depth):
            @pl.when(pl.program_id(0) == l)
            def _(l=l):
                g = part_refs[l * N_DEV][0].astype(f32)
                for r in part_refs[l * N_DEV + 1:(l + 1) * N_DEV]:
                    g = g + r[0].astype(f32)
                at(g_out)[...] = g
                at(d_out)[...], at(m_out)[...], at(v_out)[...] = _adamw_update(g, at(w_ref)[...], at(m_ref)[...], at(v_ref)[...])

    def part_spec(l, p):
        return pl.BlockSpec((1, tr, width), lambda ll, i: (p, jnp.where(ll == l, i, jnp.where(ll < l, 0, n_i - 1)), 0))

    state = (pl.BlockSpec((tr, width), lambda ll, i: (i, ll)) if layers_on_columns
             else pl.BlockSpec((1, tr, width), lambda ll, i: (ll, i, 0)))
    return pl.pallas_call(
        body, name=name, grid=(depth, n_i),
        in_specs=[part_spec(l, p) for l in range(depth) for p in range(N_DEV)] + [state] * 3 + [ANY], out_specs=[state] * 4,
        out_shape=[jax.ShapeDtypeStruct(w.shape, f32)] * 4, compiler_params=_cparams(("arbitrary", "arbitrary")),
    )(*[landed[l] for l in range(depth) for _ in range(N_DEV)], w, m, v, after)


def _row_tile(rows, cap=512):
    for cand in range(min(rows, cap) // SUBLANES * SUBLANES, 0, -SUBLANES):
        if rows % cand == 0:
            return cand
    return rows


def _cols_from_devices(g, width, name):
    n_dev, depth, a, b = g.shape

    def body(g_ref, o_ref):
        for i in range(n_dev):
            o_ref[0, :, i * b:(i + 1) * b] = g_ref[i, 0]
        if width > n_dev * b:
            o_ref[0, :, n_dev * b:width] = jnp.zeros((a, width - n_dev * b), o_ref.dtype)

    return pl.pallas_call(
        body, name=name, grid=(depth,), in_specs=[pl.BlockSpec((n_dev, 1, a, b), lambda l: (0, l, 0, 0))],
        out_specs=pl.BlockSpec((1, a, width), lambda l: (l, 0, 0)), out_shape=jax.ShapeDtypeStruct((depth, a, width), g.dtype),
        compiler_params=_cparams(("parallel",)),
    )(g)


def _devices_from_cols(per_layer, b, name, tr=256):
    depth = len(per_layer)
    a, width = per_layer[0].shape

    def body(*refs):
        o_ref = refs[depth]
        for l in range(depth):
            for i in range(N_DEV):
                o_ref[i, l] = refs[l][:, i * b:(i + 1) * b]

    return pl.pallas_call(
        body, name=name, grid=(a // tr,), in_specs=[pl.BlockSpec((tr, width), lambda r: (r, 0))] * depth,
        out_specs=pl.BlockSpec((N_DEV, depth, tr, b), lambda r: (0, 0, r, 0)),
        out_shape=jax.ShapeDtypeStruct((N_DEV, depth, a, b), per_layer[0].dtype), compiler_params=_cparams(("parallel",)),
    )(*per_layer)


def _me():
    return lax.axis_index("x"), lax.axis_index("y"), lax.axis_index("c")


def _allgather_two_level(shards, name):
    n = len(shards)
    per = 7

    def body(*refs):
        ins, outs, token = refs[:n], refs[n:2 * n], refs[2 * n]
        send_sems, recv_sems, local_sems = refs[2 * n + 1:]
        token[...] = jnp.zeros_like(token)
        x, y, c = _me()
        me, sibling = (x, y, c), (x, y, 1 - c)
        chips = [(1 - x, y), (x, 1 - y), (1 - x, 1 - y)]

        def slot(a, p):
            return outs[a].at[4 * p[0] + 2 * p[1] + p[2]]

        def copy(a, k, block, to, src=None):
            return pltpu.make_async_remote_copy(
                src_ref=slot(a, block) if src is None else src, dst_ref=slot(a, block),
                send_sem=send_sems.at[a * per + k], recv_sem=recv_sems.at[a * per + k], device_id=to, device_id_type=MESH)

        mine = [pltpu.make_async_copy(ins[a], slot(a, me), local_sems.at[a]) for a in range(n)]
        for cp in mine:
            cp.start()
        first = []
        for a in range(n):
            first.append(copy(a, 0, me, sibling, src=ins[a]))
            first += [copy(a, 1 + j, me, (*chip, c), src=ins[a]) for j, chip in enumerate(chips)]
        for cp in first:
            cp.start()
        passed = []
        for j, chip in enumerate(chips):
            for a in range(n):
                copy(a, 1 + j, (*chip, c), me).wait_recv()
                fwd = copy(a, 4 + j, (*chip, c), sibling)
                fwd.start()
                passed.append(fwd)
        for a in range(n):
            copy(a, 0, sibling, me).wait_recv()
            for j, chip in enumerate(chips):
                copy(a, 4 + j, (*chip, 1 - c), me).wait_recv()
        for cp in first + passed:
            cp.wait_send()
        for cp in mine:
            cp.wait()

    outs = pl.pallas_call(
        body, name=name, in_specs=[ANY] * n, out_specs=[ANY] * n + [pl.BlockSpec(memory_space=pltpu.VMEM)],
        out_shape=[jax.ShapeDtypeStruct((N_DEV,) + s.shape, s.dtype) for s in shards] + [jax.ShapeDtypeStruct((SUBLANES, LANES), f32)],
        scratch_shapes=[pltpu.SemaphoreType.DMA((n * per,)), pltpu.SemaphoreType.DMA((n * per,)), pltpu.SemaphoreType.DMA((n,))],
    )(*shards)
    return outs[:n], outs[n]


def _allgather_direct(row, name):
    def body(in_ref, out_ref, send_sems, recv_sems, local_sem):
        x, y, c = _me()
        mine = out_ref.at[4 * x + 2 * y + c]
        local = pltpu.make_async_copy(in_ref, mine, local_sem)
        local.start()
        sends = []
        for k in range(1, N_DEV):
            px, py, pc = x ^ (k >> 2), y ^ ((k >> 1) & 1), c ^ (k & 1)
            sends.append(pltpu.make_async_remote_copy(
                src_ref=in_ref, dst_ref=mine, send_sem=send_sems.at[k - 1], recv_sem=recv_sems.at[k - 1],
                device_id=(px, py, pc), device_id_type=MESH))
        for cp in sends:
            cp.start()
        for k in range(1, N_DEV):
            px, py, pc = x ^ (k >> 2), y ^ ((k >> 1) & 1), c ^ (k & 1)
            theirs = out_ref.at[4 * px + 2 * py + pc]
            pltpu.make_async_remote_copy(
                src_ref=in_ref, dst_ref=theirs, send_sem=send_sems.at[k - 1], recv_sem=recv_sems.at[k - 1],
                device_id=(px, py, pc), device_id_type=MESH).wait_recv()
        for cp in sends:
            cp.wait_send()
        local.wait()

    return pl.pallas_call(
        body, name=name, in_specs=[ANY], out_specs=ANY, out_shape=jax.ShapeDtypeStruct((N_DEV,) + row.shape, row.dtype),
        scratch_shapes=[pltpu.SemaphoreType.DMA((N_DEV - 1,)), pltpu.SemaphoreType.DMA((N_DEV - 1,)), pltpu.SemaphoreType.DMA],
    )(row)


N_CHIP = N_DEV // 2
HBM = pl.BlockSpec(memory_space=pltpu.HBM)
SEM = pl.BlockSpec(memory_space=pltpu.SEMAPHORE)
EFFECT = pltpu.SideEffectType.DATAFLOW_SIDE_EFFECTING


def _peer(k):
    x, y, c = _me()
    return x ^ (k >> 2), y ^ ((k >> 1) & 1), c ^ (k & 1)


def _direct_copies(srcs, lands, send_sems, recv_sems, per_peer):
    x, y, c = _me()
    me = 4 * x + 2 * y + c
    copies = []
    for a in range(len(srcs)):
        for k in range(1, N_DEV):
            px, py, pc = _peer(k)
            piece = srcs[a].at[4 * px + 2 * py + pc] if per_peer else srcs[a]
            copies.append(pltpu.make_async_remote_copy(
                src_ref=piece, dst_ref=lands[a].at[me], send_sem=send_sems.at[a * (N_DEV - 1) + k - 1],
                recv_sem=recv_sems.at[a * (N_DEV - 1) + k - 1], device_id=(px, py, pc), device_id_type=MESH))
    return copies


def _direct_start(srcs, lands, per_peer, name):
    n = len(srcs)
    n_sem = n * (N_DEV - 1)

    def body(*refs):
        src_refs, land_refs = refs[:n], refs[n:2 * n]
        send_sems, recv_sems = refs[2 * n], refs[2 * n + 1]
        token = refs[-1]
        for cp in _direct_copies(src_refs, land_refs, send_sems, recv_sems, per_peer):
            cp.start()
        token[...] = jnp.zeros_like(token)

    outs = pl.pallas_call(
        body, name=name,
        out_shape=(pltpu.SemaphoreType.DMA((n_sem,)), pltpu.SemaphoreType.DMA((n_sem,)),
                   *[pltpu.HBM(s.shape, s.dtype) for s in srcs], *[pltpu.HBM(s.shape, s.dtype) for s in lands],
                   jax.ShapeDtypeStruct((SUBLANES, LANES), f32)),
        in_specs=[HBM] * (2 * n), out_specs=(SEM, SEM, *[HBM] * (2 * n), pl.BlockSpec(memory_space=pltpu.VMEM)),
        input_output_aliases={i: 2 + i for i in range(2 * n)},
        compiler_params=pltpu.CompilerParams(has_side_effects=EFFECT),
    )(*[pltpu.with_memory_space_constraint(s, pltpu.HBM) for s in srcs], *[pltpu.with_memory_space_constraint(s, pltpu.HBM) for s in lands])
    return outs[0], outs[1], outs[2:2 + n], outs[2 + n:2 + 2 * n], outs[-1]


def _direct_wait(send_sems, recv_sems, srcs, lands, after, per_peer, name):
    n = len(srcs)

    def body(*refs):
        src_refs, land_refs = refs[:n], refs[n:2 * n]
        s_sems, r_sems = refs[2 * n], refs[2 * n + 1]
        for cp in _direct_copies(src_refs, land_refs, s_sems, r_sems, per_peer):
            cp.wait_send()
            cp.wait_recv()

    outs = pl.pallas_call(
        body, name=name,
        out_shape=tuple(pltpu.HBM(s.shape, s.dtype) for s in list(srcs) + list(lands)),
        in_specs=[HBM] * (2 * n) + [SEM, SEM, ANY], out_specs=tuple([HBM] * (2 * n)),
        input_output_aliases={i: i for i in range(2 * n)},
        compiler_params=pltpu.CompilerParams(has_side_effects=EFFECT),
    )(*srcs, *lands, send_sems, recv_sems, after)
    return outs[n:]


def _row(v, width=None):
    v = v.reshape(1, -1).astype(f32)
    if width is not None and v.shape[1] < width:
        v = jnp.pad(v, ((0, 0), (0, width - v.shape[1])))
    return v


def _layer_params(p, l):
    return dict(
        norm_mix=_row(p["norm_mix"][l]), norm_ffn=_row(p["norm_ffn"][l]), conv_w=p["conv_w"][l], conv_b=_row(p["conv_b"][l]),
        ssd=(_row(p["dt_bias"][l], LANES), _row(p["a_log"][l], LANES), _row(jnp.repeat(p["d_skip"][l], HEAD_DIM)), _row(p["ssm_norm"][l])))


def _layer_fwd(h, w_in, rest, sp, tabs, l):
    tag = f"l{l}_"
    hn = _rmsnorm_fwd(h, sp["norm_mix"], tag + "norm_mix")
    qkv, z, xbc_pre = _in_proj(hn, w_in, (QKV_WIDTH, SSM_INNER, CONV_CH), tag + "proj")
    dtp = _matmul(hn, w_in, mode="nn", n_out=LANES, tn=LANES, b_off=DT_OFF // LANES, name=tag + "proj_dt")
    prep = _attn_prep(qkv, tabs, tag + "attn_prep")
    o, lse = _attn_fwd(prep, tag + "attn_fwd")
    xbc = _conv_fwd(xbc_pre, sp["conv_w"], sp["conv_b"], tag + "conv_fwd")
    yn, y, hs = _ssd_fwd(xbc, z, dtp, sp["ssd"], tag + "ssd_fwd")
    w_out, w_gate, w_up, w_down = rest(yn) if callable(rest) else rest
    h2 = _out_proj(o, yn, w_out, h, tag + "out_proj")
    hn2 = _rmsnorm_fwd(h2, sp["norm_ffn"], tag + "norm_ffn")
    g, u, act = _swiglu_fwd(hn2, w_gate, w_up, tag + "ffn_up")
    h3 = _matmul(act, w_down, mode="nn", tk=1408, add=h2, name=tag + "ffn_down")
    saved = dict(h=h, hn=hn, prep=prep, z=z, xbc_pre=xbc_pre, dtp=dtp, o=o, lse=lse, xbc=xbc, yn=yn, y=y, hs=hs, h2=h2, hn2=hn2, g=g, u=u, act=act,
                 rest=(w_out, w_gate, w_up, w_down))
    return h3, saved


def _layer_bwd(dh3_pair, s, big, sp, tabs, l, gd=f32, after_ffn=None):
    tag = f"l{l}_"
    dh3, dh3b = dh3_pair
    w_in, w_out, w_gate, w_up, w_down = big
    dg, du = _swiglu_bwd(dh3b, w_down, s["g"], s["u"], tag + "ffn_down_bwd")
    dw_down = _matmul(s["act"], dh3b, mode="tn", tm=1408, tn=512, tk=2048, out_dtype=gd, name=tag + "dw_down")
    dw_gate = _matmul(dg, s["hn2"], mode="tn", tm=1408, tn=512, tk=2048, out_dtype=gd, name=tag + "dw_gate")
    dw_up = _matmul(du, s["hn2"], mode="tn", tm=1408, tn=512, tk=2048, out_dtype=gd, name=tag + "dw_up")
    norm_ffn = sp["norm_ffn"] if after_ffn is None else sp["norm_ffn"] + after_ffn(dict(w_gate=dw_gate, w_up=dw_up, w_down=dw_down))
    dh2, dh2b, dnf = _nt_norm_bwd([(dg, w_gate), (du, w_up)], s["h2"], norm_ffn, dh3, tag + "ffn_up_bwd_norm", tk=1408, b_is_kd=True,
                                  vmem=VMEM_LIMIT_TWO_PAIRS)
    d_o = _matmul(dh2b, w_out, mode="nt", n_out=ATTN_WIDTH, tn=512, b_off=0, name=tag + "out_attn_bwd")
    dyn = _matmul(dh2b, w_out, mode="nt", n_out=SSM_INNER, tn=512, b_off=1, name=tag + "out_ssm_bwd")
    dw_out = jnp.concatenate([_matmul(s["o"], dh2b, mode="tn", tm=512, tn=512, tk=2048, out_dtype=gd, name=tag + "dw_out_attn"),
                              _matmul(s["yn"], dh2b, mode="tn", tm=512, tn=512, tk=2048, out_dtype=gd, name=tag + "dw_out_ssm")], axis=0)
    dxbc, dz, ddtp, dnw, dds, dal, dbi = _ssd_bwd(s["xbc"], s["z"], s["dtp"], s["y"], s["hs"], dyn, sp["ssd"], tag + "ssd_bwd")
    dxbc_pre, dconv_w, dconv_b = _conv_bwd(s["xbc_pre"], sp["conv_w"], sp["conv_b"], dxbc, tag + "conv_bwd")
    dq, dk, dv = _attn_bwd(s["prep"], tabs, s["o"], s["lse"], d_o, tag + "attn_bwd")
    dproj = jnp.concatenate([dq, dk, dv, dz, dxbc_pre, ddtp], axis=1)
    dw_in = _matmul(s["hn"], dproj, mode="tn", tm=512, tn=1152, tk=2048, out_dtype=gd, name=tag + "dw_in")
    res = _nt_norm_bwd([(dproj, w_in)], s["h"], sp["norm_mix"], dh2, tag + "proj_bwd_norm", tk=1152, bf16_copy=l > 0)
    dh, dhb, dnm = res if l > 0 else (res[0], None, res[1])
    grads = dict(
        norm_mix=dnm.sum(0), w_in=dw_in, conv_w=dconv_w, conv_b=dconv_b[0], dt_bias=dbi.sum(0)[:SSM_HEADS], a_log=dal.sum(0)[:SSM_HEADS],
        d_skip=dds.sum(0).reshape(SSM_HEADS, HEAD_DIM).sum(1), ssm_norm=dnw.sum(0), w_out=dw_out, norm_ffn=dnf.sum(0),
        w_gate=dw_gate, w_up=dw_up, w_down=dw_down)
    return (dh, dhb), grads


def _local_step(x, positions, target, p, bigs):
    tabs = _rope_tables(positions.reshape(-1, 1), "rope_tables")
    h = x
    saved, sps = [], []
    for l in range(DEPTH):
        sps.append(_layer_params(p, l))
        h, s = _layer_fwd(h, bigs[l][0], bigs[l][1:], sps[l], tabs, l)
        saved.append(s)
    dh, dhb, loss_parts, dfn = _final_loss(h, _row(p["final_norm"]), target, "final_loss")
    dh = (dh, dhb)
    layer_grads = [None] * DEPTH
    for l in reversed(range(DEPTH)):
        dh, layer_grads[l] = _layer_bwd(dh, saved[l], bigs[l], sps[l], tabs, l)
    grads = {k: [layer_grads[l][k] for l in range(DEPTH)] for k in layer_grads[0]}
    grads["final_norm"] = dfn.sum(0)
    return jnp.sum(loss_parts), dh[0], grads


BIG = ("w_in", "w_out", "w_gate", "w_up", "w_down")
REST = BIG[1:]
FFN = ("w_gate", "w_up", "w_down")
MIX = ("w_in", "w_out")
COL_SHARDED = ("w_in",)
TRANSPOSED = ("w_gate", "w_up")
SMALL = ("norm_mix", "conv_b", "dt_bias", "a_log", "d_skip", "ssm_norm", "norm_ffn", "final_norm")
WEIGHTS = ("norm_mix", "w_in", "conv_w", "conv_b", "dt_bias", "a_log", "d_skip", "ssm_norm", "w_out", "norm_ffn", "w_gate", "w_up", "w_down", "final_norm")
SMALL_ROWS = 88
CONVW_ROWS = 96
CONVW_SHARD_ROWS = 16


def _full_from_gathered(name, g, l):
    _, a, b = g.shape
    if name in COL_SHARDED:
        width = IN_PROJ_PAD if name == "w_in" else N_DEV * b
        return _cols_from_devices(g.reshape(N_DEV, 1, a, b), width, f"cols_l{l}_{name}").reshape(a, width)
    return g.reshape(N_DEV * a, b)


def _by_device(name, full, shard_shape, l):
    a, b = shard_shape
    if name in COL_SHARDED:
        return _devices_from_cols([full], b, f"devs_l{l}_{name}").reshape(N_CHIP, 2, a, b)
    return full.reshape(N_CHIP, 2, a, b)


def _pack_rows(parts, rows, width):
    flat = jnp.concatenate([q.reshape(-1) for q in parts])
    return jnp.pad(flat, (0, rows * width - flat.shape[0])).reshape(rows, width)


def _unpack(flat, like):
    out, off = [], 0
    for q in like:
        out.append(flat[off:off + q.size].reshape(q.shape))
        off += q.size
    return out


def kernel(x, positions, norm_mix, w_in, conv_w, conv_b, dt_bias, a_log, d_skip, ssm_norm, w_out, norm_ffn, w_gate, w_up, w_down, final_norm, loss_target, m_norm_mix, m_w_in, m_conv_w, m_conv_b, m_dt_bias, m_a_log, m_d_skip, m_ssm_norm, m_w_out, m_norm_ffn, m_w_gate, m_w_up, m_w_down, m_final_norm, v_norm_mix, v_w_in, v_conv_w, v_conv_b, v_dt_bias, v_a_log, v_d_skip, v_ssm_norm, v_w_out, v_norm_ffn, v_w_gate, v_w_up, v_w_down, v_final_norm):
    w = dict(norm_mix=norm_mix, w_in=w_in, conv_w=conv_w, conv_b=conv_b, dt_bias=dt_bias, a_log=a_log, d_skip=d_skip, ssm_norm=ssm_norm,
             w_out=w_out, norm_ffn=norm_ffn, w_gate=w_gate, w_up=w_up, w_down=w_down, final_norm=final_norm)
    m = dict(norm_mix=m_norm_mix, w_in=m_w_in, conv_w=m_conv_w, conv_b=m_conv_b, dt_bias=m_dt_bias, a_log=m_a_log, d_skip=m_d_skip,
             ssm_norm=m_ssm_norm, w_out=m_w_out, norm_ffn=m_norm_ffn, w_gate=m_w_gate, w_up=m_w_up, w_down=m_w_down, final_norm=m_final_norm)
    v = dict(norm_mix=v_norm_mix, w_in=v_w_in, conv_w=v_conv_w, conv_b=v_conv_b, dt_bias=v_dt_bias, a_log=v_a_log, d_skip=v_d_skip,
             ssm_norm=v_ssm_norm, w_out=v_w_out, norm_ffn=v_norm_ffn, w_gate=v_w_gate, w_up=v_w_up, w_down=v_w_down, final_norm=v_final_norm)
    ax, ay, ac = lax.axis_index("x"), lax.axis_index("y"), lax.axis_index("c")
    dev = 4 * ax + 2 * ay + ac

    assert DEPTH == 2
    t = x.shape[0] * x.shape[1]
    xf, target = x.reshape(t, D_MODEL), loss_target.reshape(t, D_MODEL)

    def own_slot(block):
        return lax.dynamic_update_slice(lax.empty((N_DEV,) + block.shape[1:], block.dtype), block, (dev,) + (0,) * (block.ndim - 1))

    def layer_shard(arr, k, l):
        return jnp.transpose(arr, (2, 0, 1))[:, l, :] if k in TRANSPOSED else arr[l]

    def gather_start(keys, l, tie, name):
        shards = [(layer_shard(w[keys[0]], keys[0], l) + tie).astype(bf16)] + [layer_shard(w[k], k, l).astype(bf16) for k in keys[1:]]
        return _direct_start(shards, [own_slot(s[None]) for s in shards], False, name)

    def scatter_start(keys, grads_l, l, name):
        shapes = [(w[k].shape[2], w[k].shape[1]) if k in TRANSPOSED else w[k].shape[1:] for k in keys]
        by_dev = [_by_device(k, grads_l[k], sh, l).reshape((N_DEV,) + sh) for k, sh in zip(keys, shapes)]
        return _direct_start(by_dev, [own_slot(lax.dynamic_slice_in_dim(g, dev, 1, 0)) for g in by_dev], True, name)

    (g_in0, conv_all), tie = _allgather_two_level([w["w_in"][0].astype(bf16), w["conv_w"]], "gather_l0_w_in")
    rest0_copy = gather_start(REST, 0, tie[0, 0], "gather_l0_rest_start")
    l1_copy = gather_start(BIG, 1, rest0_copy[4][0, 0], "gather_l1_start")
    p = {k: w[k] for k in SMALL}
    p["norm_mix"] = p["norm_mix"] + l1_copy[4][0, 0]
    p["conv_w"] = jnp.transpose(conv_all, (1, 2, 0, 3)).reshape(DEPTH, CONV_WIDTH, CONV_CH)
    sp0, sp1 = _layer_params(p, 0), _layer_params(p, 1)

    def rest0(after):
        lands = _direct_wait(*rest0_copy[:4], after, False, "gather_l0_rest_wait")
        return tuple(_full_from_gathered(k, g, 0) for k, g in zip(REST, lands))

    tabs = _rope_tables(positions.reshape(t, 1), "rope_tables")
    w_in0 = _full_from_gathered("w_in", g_in0, 0)
    h1, saved0 = _layer_fwd(xf, w_in0, rest0, sp0, tabs, 0)
    lands1 = _direct_wait(*l1_copy[:4], h1, False, "gather_l1_wait")
    bigs1 = tuple(_full_from_gathered(k, g, 1) for k, g in zip(BIG, lands1))
    h2, saved1 = _layer_fwd(h1, bigs1[0], bigs1[1:], sp1, tabs, 1)
    dh, dhb, loss_parts, dfn = _final_loss(h2, _row(p["final_norm"]), target, "final_loss")
    loss_local = jnp.sum(loss_parts)

    dh, grads1 = _layer_bwd((dh, dhb), saved1, bigs1, sp1, tabs, 1, gd=bf16)
    l1_grads = scatter_start(BIG, grads1, 1, "scatter_l1_start")
    w_out0, w_gate0, w_up0, w_down0 = saved0["rest"]
    bigs0 = (w_in0, w_out0, w_gate0, w_up0, w_down0 + l1_grads[4][0, 0].astype(bf16))
    ffn0_grads = []

    def after_ffn(grads_ffn):
        ffn0_grads.append(scatter_start(FFN, grads_ffn, 0, "scatter_l0_ffn_start"))
        return ffn0_grads[0][4][0, 0]

    (dx, _), grads0 = _layer_bwd(dh, saved0, bigs0, sp0, tabs, 0, gd=bf16, after_ffn=after_ffn)
    mix0_grads = scatter_start(MIX, grads0, 0, "scatter_l0_mix_start")
    landed = {(k, 1): g for k, g in zip(BIG, _direct_wait(*l1_grads[:4], dx, True, "scatter_l1_wait"))}
    landed.update({(k, 0): g for k, g in zip(FFN, _direct_wait(*ffn0_grads[0][:4], dx, True, "scatter_l0_ffn_wait"))})
    out_g, out_d, out_m, out_v = {}, {}, {}, {}

    def update(keys, after):
        for k in keys:
            parts = [landed[k, l] for l in range(DEPTH)]
            if k in TRANSPOSED:
                depth, a, b = w[k].shape
                state = [jnp.transpose(s, (2, 0, 1)).reshape(b, depth * a) for s in (w[k], m[k], v[k])]
                res = _adamw_layers(parts, *state, after, "adamw_" + k, layers_on_columns=True)
                res = [jnp.transpose(r.reshape(b, depth, a), (1, 2, 0)) for r in res]
            else:
                res = _adamw_layers(parts, w[k], m[k], v[k], after, "adamw_" + k)
            for dst, r in zip((out_g, out_d, out_m, out_v), res):
                dst[k] = r

    update(FFN, mix0_grads[4])
    grads = {k: [grads0[k], grads1[k]] for k in grads0 if k not in BIG}
    grads["final_norm"] = dfn.sum(0) + mix0_grads[4][0, 0]

    small_like = [w[k] for k in SMALL]
    small_grads = [jnp.stack(grads[k]) if k != "final_norm" else grads[k] for k in SMALL]
    small_pack = jnp.concatenate([_pack_rows(small_grads, SMALL_ROWS, LANES), _pack_rows([jnp.stack(grads["conv_w"])], CONVW_ROWS, LANES)], axis=0)
    parts = _allgather_direct(small_pack, "gather_small_grads")
    g_s, d_s, m_s, v_s = _adamw(
        [parts[i, :SMALL_ROWS] for i in range(N_DEV)], _pack_rows(small_like, SMALL_ROWS, LANES),
        _pack_rows([m[k] for k in SMALL], SMALL_ROWS, LANES), _pack_rows([v[k] for k in SMALL], SMALL_ROWS, LANES), "adamw_replicated")
    for dst, src in ((out_g, g_s), (out_d, d_s), (out_m, m_s), (out_v, v_s)):
        dst.update(zip(SMALL, _unpack(src.reshape(-1), small_like)))
    shard_w = conv_w.shape[-1]
    conv_parts = parts[:, SMALL_ROWS:].reshape(N_DEV, DEPTH, CONV_WIDTH, CONV_CH)
    conv_mine = lax.dynamic_slice_in_dim(conv_parts, dev * shard_w, shard_w, axis=3)
    g_c, d_c, m_c, v_c = _adamw(
        [_pack_rows([conv_mine[i]], CONVW_SHARD_ROWS, LANES) for i in range(N_DEV)], _pack_rows([conv_w], CONVW_SHARD_ROWS, LANES),
        _pack_rows([m["conv_w"]], CONVW_SHARD_ROWS, LANES), _pack_rows([v["conv_w"]], CONVW_SHARD_ROWS, LANES), "adamw_conv_w")
    for dst, src in ((out_g, g_c), (out_d, d_c), (out_m, m_c), (out_v, v_c)):
        dst["conv_w"] = src.reshape(-1)[:conv_w.size].reshape(conv_w.shape)

    landed.update({(k, 0): g for k, g in zip(MIX, _direct_wait(*mix0_grads[:4], v_c + out_v["w_down"][0, :CONVW_SHARD_ROWS, :LANES], True, "scatter_l0_mix_wait"))})
    update(MIX, v_c)

    loss = lax.psum(loss_local, ("x", "y", "c"))
    return (loss, dx.reshape(x.shape), *[out_g[k] for k in WEIGHTS], *[out_d[k] for k in WEIGHTS],
            *[out_m[k] for k in WEIGHTS], *[out_v[k] for k in WEIGHTS])
```

```python
import jax
import jax.numpy as jnp
import numpy as np
from jax import lax
from jax.experimental import pallas as pl
from jax.experimental.pallas import tpu as pltpu

f32 = jnp.float32
bf16 = jnp.bfloat16

D_MODEL = 1024
SEQ = 2048
DEPTH = 2
HEAD_DIM = 64
N_ATTN_HEADS = 8
N_KV_HEADS = 2
ATTN_WIDTH = 512
KV_WIDTH = 128
ROPE_DIM = 16
ROPE_THETA = 500000.0
DILATIONS = (1, 4, 16)
ATTN_BLOCK = 128
SSM_HEADS = 16
SSM_INNER = 1024
SSM_GROUPS = 2
D_STATE = 128
CONV_WIDTH = 4
CHUNK = 128
CONV_CH = 1536
MIX_WIDTH = 1536
QKV_WIDTH = ATTN_WIDTH + 2 * KV_WIDTH
DT_OFF = 3328
IN_PROJ = 3344
IN_PROJ_PAD = 3456
FFN_HIDDEN = 2816
EPS = 1e-5
N_DEV = 8
ADAM_LR = 0.001
ADAM_B1 = 0.9
ADAM_B2 = 0.999
ADAM_EPS = 1e-08
ADAM_WD = 0.01
ADAM_STEP = 10

LANES = 128
SUBLANES = 8
VMEM_LIMIT = 56 * 1024 * 1024
VMEM_LIMIT_TWO_PAIRS = 60 * 1024 * 1024

MESH = pl.DeviceIdType.MESH
ANY = pl.BlockSpec(memory_space=pl.ANY)


def _cparams(sem, vmem=None):
    return pltpu.CompilerParams(dimension_semantics=sem, vmem_limit_bytes=vmem or VMEM_LIMIT)


def _sigmoid(x):
    return 1.0 / (1.0 + jnp.exp(-x))


def _silu(x):
    return x * _sigmoid(x)


def _dsilu(x):
    s = _sigmoid(x)
    return s * (1.0 + x * (1.0 - s))


def _silu_and_grad(x):
    s = _sigmoid(x)
    return x * s, s * (1.0 + x * (1.0 - s))


def _softplus(x):
    return jnp.maximum(x, 0.0) + jnp.log(1.0 + jnp.exp(-jnp.abs(x)))


def _dot(a, b, dims, precision=None):
    return lax.dot_general(a, b, (dims, ((), ())), preferred_element_type=f32, precision=precision)


def _nn(a, b, precision=None):
    return _dot(a, b, ((1,), (0,)), precision)


def _nt(a, b):
    return _dot(a, b, ((1,), (1,)))


def _tn(a, b):
    return _dot(a, b, ((0,), (0,)))


def _rowsum8(t):
    n, w = t.shape
    return jnp.sum(t.reshape(n // SUBLANES, SUBLANES, w), axis=0)


def _matmul(a, b, *, mode, n_out=None, b_off=0, add=None, out_dtype=f32, tm=2048, tn=512, tk=1024, name):
    if mode == "tn":
        kk, m = a.shape
    else:
        m, kk = a.shape
    n = n_out if n_out is not None else (b.shape[0] if mode == "nt" else b.shape[1])
    tm, tn, tk = min(tm, m), min(tn, n), min(tk, kk)
    assert m % tm == 0 and n % tn == 0 and kk % tk == 0, (name, m, n, kk, tm, tn, tk)
    nk = kk // tk
    if mode == "nn":
        a_spec = pl.BlockSpec((tm, tk), lambda i, j, k: (i, k))
        b_spec = pl.BlockSpec((tk, tn), lambda i, j, k: (k, j + b_off))
        dims = ((1,), (0,))
    elif mode == "nt":
        a_spec = pl.BlockSpec((tm, tk), lambda i, j, k: (i, k))
        b_spec = pl.BlockSpec((tn, tk), lambda i, j, k: (j + b_off, k))
        dims = ((1,), (1,))
    else:
        a_spec = pl.BlockSpec((tk, tm), lambda i, j, k: (k, i))
        b_spec = pl.BlockSpec((tk, tn), lambda i, j, k: (k, j + b_off))
        dims = ((0,), (0,))
    o_spec = pl.BlockSpec((tm, tn), lambda i, j, k: (i, j))
    has_add = add is not None

    def body(*refs):
        if has_add:
            a_ref, b_ref, add_ref, o_ref, acc_ref = refs
        else:
            a_ref, b_ref, o_ref, acc_ref = refs
        k = pl.program_id(2)
        part = _dot(a_ref[...].astype(bf16), b_ref[...].astype(bf16), dims)

        @pl.when(k == 0)
        def _():
            acc_ref[...] = part

        @pl.when(k > 0)
        def _():
            acc_ref[...] += part

        @pl.when(k == nk - 1)
        def _():
            r = acc_ref[...]
            if has_add:
                r = r + add_ref[...]
            o_ref[...] = r.astype(out_dtype)

    in_specs = [a_spec, b_spec] + ([o_spec] if has_add else [])
    args = (a, b) + ((add,) if has_add else ())
    return pl.pallas_call(
        body, name=name, grid=(m // tm, n // tn, nk), in_specs=in_specs, out_specs=o_spec,
        out_shape=jax.ShapeDtypeStruct((m, n), out_dtype), scratch_shapes=[pltpu.VMEM((tm, tn), f32)],
        compiler_params=_cparams(("parallel", "parallel", "arbitrary")),
    )(*args)


def _in_proj(hn, w_in, widths, name, tm=2048, tn=256):
    m, k = hn.shape
    starts = [sum(widths[:i]) // tn for i in range(len(widths))]
    counts = [wd // tn for wd in widths]
    assert m % tm == 0 and all(wd % tn == 0 for wd in widths)
    n_out = len(widths)

    def body(a_ref, w_ref, *o_refs):
        j = pl.program_id(1)
        acc = _nn(a_ref[...], w_ref[...])
        for s, c, o_ref in zip(starts, counts, o_refs):
            @pl.when((j >= s) & (j < s + c))
            def _(o_ref=o_ref):
                o_ref[...] = acc

    def o_spec(s, c):
        return pl.BlockSpec((tm, tn), lambda i, j: (i, jnp.clip(j - s, 0, c - 1)))

    return pl.pallas_call(
        body, name=name, grid=(m // tm, sum(counts)),
        in_specs=[pl.BlockSpec((tm, k), lambda i, j: (i, 0)), pl.BlockSpec((k, tn), lambda i, j: (0, j))],
        out_specs=[o_spec(s, c) for s, c in zip(starts, counts)],
        out_shape=[jax.ShapeDtypeStruct((m, wd), f32) for wd in widths], compiler_params=_cparams(("parallel", "arbitrary")),
    )(hn, w_in)


def _out_proj(o, yn, w_out, h, name, tm=2048, tn=512):
    m, kb = o.shape
    n = w_out.shape[1]
    n_y = yn.shape[1] // kb
    assert yn.shape[1] % kb == 0 and w_out.shape[0] == kb * (1 + n_y) and m % tm == 0 and n % tn == 0

    def body(*refs):
        o_ref, y_refs, w_refs, h_ref, out_ref = refs[0], refs[1:1 + n_y], refs[1 + n_y:2 + 2 * n_y], refs[-2], refs[-1]
        acc = h_ref[...] + _nn(o_ref[...].astype(bf16), w_refs[0][...])
        for y_ref, w_ref in zip(y_refs, w_refs[1:]):
            acc = acc + _nn(y_ref[...], w_ref[...])
        out_ref[...] = acc

    res = pl.BlockSpec((tm, tn), lambda i, j: (i, j))

    def a_blk(c):
        return pl.BlockSpec((tm, kb), lambda i, j: (i, c))

    def w_blk(r):
        return pl.BlockSpec((kb, tn), lambda i, j: (r, j))

    return pl.pallas_call(
        body, name=name, grid=(m // tm, n // tn),
        in_specs=[a_blk(0)] + [a_blk(c) for c in range(n_y)] + [w_blk(r) for r in range(1 + n_y)] + [res],
        out_specs=res, out_shape=jax.ShapeDtypeStruct((m, n), f32), compiler_params=_cparams(("parallel", "parallel")),
    )(o, *[yn] * n_y, *[w_out] * (1 + n_y), h)


def _swiglu_fwd(hn, w_gate, w_up, name, tm=2048, tn=256):
    m, k = hn.shape
    n = w_gate.shape[0]
    assert m % tm == 0 and n % tn == 0, (name, m, n, tm, tn)

    def body(a_ref, wg_ref, wu_ref, g_ref, u_ref, act_ref):
        a = a_ref[...]
        g = _nt(a, wg_ref[...])
        u = _nt(a, wu_ref[...])
        sg, dsg = _silu_and_grad(g)
        g_ref[...] = (u * dsg).astype(bf16)
        u_ref[...] = sg.astype(bf16)
        act_ref[...] = (sg * u).astype(bf16)

    a_spec = pl.BlockSpec((tm, k), lambda i, j: (i, 0))
    w_spec = pl.BlockSpec((tn, k), lambda i, j: (j, 0))
    o_spec = pl.BlockSpec((tm, tn), lambda i, j: (i, j))
    return pl.pallas_call(
        body, name=name, grid=(m // tm, n // tn), in_specs=[a_spec, w_spec, w_spec], out_specs=[o_spec, o_spec, o_spec],
        out_shape=[jax.ShapeDtypeStruct((m, n), bf16)] * 3,
        compiler_params=_cparams(("parallel", "parallel")),
    )(hn, w_gate, w_up)


def _swiglu_bwd(dh, w_down, g, u, name, tm=2048, tn=256):
    m, k = dh.shape
    n = w_down.shape[0]
    assert m % tm == 0 and n % tn == 0, (name, m, n, tm, tn)

    def body(a_ref, w_ref, g_ref, u_ref, dg_ref, du_ref):
        dact = _nt(a_ref[...].astype(bf16), w_ref[...])
        dg_ref[...] = (dact * g_ref[...].astype(f32)).astype(bf16)
        du_ref[...] = (dact * u_ref[...].astype(f32)).astype(bf16)

    a_spec = pl.BlockSpec((tm, k), lambda i, j: (i, 0))
    w_spec = pl.BlockSpec((tn, k), lambda i, j: (j, 0))
    o_spec = pl.BlockSpec((tm, tn), lambda i, j: (i, j))
    return pl.pallas_call(
        body, name=name, grid=(m // tm, n // tn), in_specs=[a_spec, w_spec, o_spec, o_spec], out_specs=[o_spec, o_spec],
        out_shape=[jax.ShapeDtypeStruct((m, n), bf16), jax.ShapeDtypeStruct((m, n), bf16)],
        compiler_params=_cparams(("parallel", "parallel")),
    )(dh, w_down, g, u)


def _rmsnorm_fwd(h, w, name, tm=512):
    m, d = h.shape

    def body(h_ref, w_ref, o_ref):
        x = h_ref[...]
        r = lax.rsqrt(jnp.mean(x * x, axis=-1, keepdims=True) + EPS)
        o_ref[...] = (x * r * w_ref[...]).astype(bf16)

    return pl.pallas_call(
        body, name=name, grid=(m // tm,),
        in_specs=[pl.BlockSpec((tm, d), lambda i: (i, 0)), pl.BlockSpec((1, d), lambda i: (0, 0))],
        out_specs=pl.BlockSpec((tm, d), lambda i: (i, 0)), out_shape=jax.ShapeDtypeStruct((m, d), bf16),
        compiler_params=_cparams(("parallel",)),
    )(h, w)


def _nt_norm_bwd(pairs, h, w, dres, name, tk, b_is_kd=False, bf16_copy=True, tm=1024, vmem=None):
    m, d = h.shape
    contract = _nn if b_is_kd else _nt
    steps = [p[0].shape[1] // tk for p in pairs]
    assert all(p[0].shape[1] % tk == 0 for p in pairs), (name, tk)
    starts = [sum(steps[:i]) for i in range(len(pairs))]
    nk = sum(steps)
    n_p = len(pairs)

    def body(*refs):
        ab = refs[:2 * n_p]
        h_ref, w_ref, dres_ref, dh_ref = refs[2 * n_p:2 * n_p + 4]
        dhb_ref = refs[2 * n_p + 4] if bf16_copy else None
        dw_ref, acc_ref = refs[-2:]
        i, k = pl.program_id(0), pl.program_id(1)

        @pl.when(k == 0)
        def _():
            acc_ref[...] = jnp.zeros_like(acc_ref)

        for p in range(n_p):
            @pl.when((k >= starts[p]) & (k < starts[p] + steps[p]))
            def _(p=p):
                acc_ref[...] += contract(ab[2 * p][...], ab[2 * p + 1][...])

        @pl.when(k == nk - 1)
        def _():
            x = h_ref[...]
            r = lax.rsqrt(jnp.mean(x * x, axis=-1, keepdims=True) + EPS)
            xhat = x * r
            dy = acc_ref[...]
            gw = dy * w_ref[...]
            dh = dres_ref[...] + r * (gw - xhat * jnp.mean(gw * xhat, axis=-1, keepdims=True))
            dh_ref[...] = dh
            if bf16_copy:
                dhb_ref[...] = dh.astype(bf16)
            part = _rowsum8(dy * xhat)

            @pl.when(i == 0)
            def _():
                dw_ref[...] = part

            @pl.when(i > 0)
            def _():
                dw_ref[...] += part

    def clamp(k, p):
        return jnp.clip(k - starts[p], 0, steps[p] - 1)

    in_specs = []
    for p in range(n_p):
        b_spec = (pl.BlockSpec((tk, d), lambda i, k, p=p: (clamp(k, p), 0)) if b_is_kd
                  else pl.BlockSpec((d, tk), lambda i, k, p=p: (0, clamp(k, p))))
        in_specs += [pl.BlockSpec((tm, tk), lambda i, k, p=p: (i, clamp(k, p))), b_spec]
    row = pl.BlockSpec((tm, d), lambda i, k: (i, 0))
    in_specs += [row, pl.BlockSpec((1, d), lambda i, k: (0, 0)), row]
    return pl.pallas_call(
        body, name=name, grid=(m // tm, nk), in_specs=in_specs,
        out_specs=[row] + [row] * bf16_copy + [pl.BlockSpec((SUBLANES, d), lambda i, k: (0, 0))],
        out_shape=[jax.ShapeDtypeStruct((m, d), f32)] + [jax.ShapeDtypeStruct((m, d), bf16)] * bf16_copy + [jax.ShapeDtypeStruct((SUBLANES, d), f32)],
        scratch_shapes=[pltpu.VMEM((tm, d), f32)], compiler_params=_cparams(("arbitrary", "arbitrary"), vmem),
    )(*[t for p in pairs for t in p], h, w, dres)


def _final_loss(h, w, target, name, tm=512):
    m, d = h.shape

    def body(h_ref, w_ref, t_ref, dh_ref, dhb_ref, loss_ref, dw_ref):
        x = h_ref[...]
        r = lax.rsqrt(jnp.mean(x * x, axis=-1, keepdims=True) + EPS)
        xhat = x * r
        ww = w_ref[...]
        err = xhat * ww - t_ref[...]
        dy = err * (1.0 / d)
        gw = dy * ww
        dh = r * (gw - xhat * jnp.mean(gw * xhat, axis=-1, keepdims=True))
        dh_ref[...] = dh
        dhb_ref[...] = dh.astype(bf16)
        lpart = _rowsum8(err * err) * (0.5 / d)
        wpart = _rowsum8(dy * xhat)

        @pl.when(pl.program_id(0) == 0)
        def _():
            loss_ref[...] = lpart
            dw_ref[...] = wpart

        @pl.when(pl.program_id(0) > 0)
        def _():
            loss_ref[...] += lpart
            dw_ref[...] += wpart

    row = pl.BlockSpec((tm, d), lambda i: (i, 0))
    acc = pl.BlockSpec((SUBLANES, d), lambda i: (0, 0))
    return pl.pallas_call(
        body, name=name, grid=(m // tm,),
        in_specs=[row, pl.BlockSpec((1, d), lambda i: (0, 0)), row], out_specs=[row, row, acc, acc],
        out_shape=[jax.ShapeDtypeStruct((m, d), f32), jax.ShapeDtypeStruct((m, d), bf16),
                   jax.ShapeDtypeStruct((SUBLANES, d), f32), jax.ShapeDtypeStruct((SUBLANES, d), f32)],
        compiler_params=_cparams(("arbitrary",)),
    )(h, w, target)


def _lane_tables():
    f = np.arange(LANES) % HEAD_DIM
    inv = ROPE_THETA ** (-jnp.arange(0, ROPE_DIM, 2, dtype=f32) / ROPE_DIM)
    invf = jnp.where(f < ROPE_DIM, inv[f % (ROPE_DIM // 2)], 0.0).astype(f32)
    return invf.reshape(1, LANES)


def _rope_tables(pos_col, name):
    t = pos_col.shape[0]
    tm = SEQ

    def body(p_ref, f_ref, c_ref, s1_ref, s2_ref):
        ang = p_ref[...].astype(f32) * f_ref[...]
        co, si = jnp.cos(ang), jnp.sin(ang)
        f = lax.broadcasted_iota(jnp.int32, (tm, LANES), 1) % HEAD_DIM
        c_ref[...] = jnp.where(f < ROPE_DIM, co, 1.0)
        s1_ref[...] = jnp.where(f < ROPE_DIM // 2, -si, 0.0)
        s2_ref[...] = jnp.where((f >= ROPE_DIM // 2) & (f < ROPE_DIM), si, 0.0)

    row = pl.BlockSpec((tm, LANES), lambda i: (i, 0))
    return pl.pallas_call(
        body, name=name, grid=(t // tm,),
        in_specs=[pl.BlockSpec((tm, 1), lambda i: (i, 0)), pl.BlockSpec((1, LANES), lambda i: (0, 0))],
        out_specs=[row, row, row], out_shape=[jax.ShapeDtypeStruct((t, LANES), f32)] * 3,
        compiler_params=_cparams(("parallel",)),
    )(pos_col, _lane_tables())


def _rot(x, c, s1, s2):
    return x * c + pltpu.roll(x, LANES - ROPE_DIM // 2, 1) * s1 + pltpu.roll(x, ROPE_DIM // 2, 1) * s2


def _rot_t(g, c, s1, s2):
    return g * c + pltpu.roll(g * s1, ROPE_DIM // 2, 1) + pltpu.roll(g * s2, LANES - ROPE_DIM // 2, 1)


def _dup_head(x, kvh, low):
    a = jnp.where(kvh == 0, x, pltpu.roll(x, HEAD_DIM, 1))
    return jnp.where(low, a, pltpu.roll(a, HEAD_DIM, 1))


def _deinterleave(src_ref, dst_ref, d, dtype):
    length = SEQ // d
    if d == 1:
        dst_ref[...] = src_ref[...].astype(dtype)
    else:
        for r in range(d):
            dst_ref[pl.ds(r * length, length), :] = src_ref[pl.ds(r, length, stride=d), :].astype(dtype)


def _interleave_store(src_ref, dst_ref, d, accumulate):
    length = SEQ // d
    if d == 1:
        if accumulate:
            dst_ref[...] += src_ref[...]
        else:
            dst_ref[...] = src_ref[...]
    else:
        for r in range(d):
            blk = src_ref[pl.ds(r * length, length), :]
            if accumulate:
                dst_ref[pl.ds(r, length, stride=d), :] = dst_ref[pl.ds(r, length, stride=d), :] + blk
            else:
                dst_ref[pl.ds(r, length, stride=d), :] = blk


def _attn_masks():
    qi = lax.broadcasted_iota(jnp.int32, (ATTN_BLOCK, ATTN_BLOCK), 0)
    ki = lax.broadcasted_iota(jnp.int32, (ATTN_BLOCK, ATTN_BLOCK), 1)
    low = lax.broadcasted_iota(jnp.int32, (ATTN_BLOCK, LANES), 1) < HEAD_DIM
    return ki <= qi, ki >= qi, low


NEG_INF = float("-inf")


N_BRANCH = len(DILATIONS)


def _attn_prep(qkv, tabs, name):
    t = qkv.shape[0]
    nb = t // SEQ
    n_j = ATTN_WIDTH // LANES

    def q_body(q_ref, c_ref, s1_ref, s2_ref, out_ref, xr):
        xr[...] = _rot(q_ref[...], c_ref[...], s1_ref[...], s2_ref[...]) * (HEAD_DIM ** -0.5)
        for bi, d in enumerate(DILATIONS):
            _deinterleave(xr, out_ref.at[bi], d, bf16)

    def kv_body(x_ref, c_ref, s1_ref, s2_ref, out_ref, xr):
        lowfull = lax.broadcasted_iota(jnp.int32, (SEQ, LANES), 1) < HEAD_DIM
        x = x_ref[...]
        x = jnp.where(pl.program_id(1) == 0, _rot(x, c_ref[...], s1_ref[...], s2_ref[...]), x)
        for kvh in range(N_KV_HEADS):
            xr[...] = _dup_head(x, kvh, lowfull)
            for bi, d in enumerate(DILATIONS):
                length = SEQ // d
                for r in range(d):
                    rows = xr[...] if d == 1 else xr[pl.ds(r, length, stride=d), :]
                    out_ref[0, bi, pl.ds(r * length, length), kvh * LANES:(kvh + 1) * LANES] = rows.astype(bf16)

    tab = pl.BlockSpec((SEQ, LANES), lambda b, j: (b, 0))
    q = pl.pallas_call(
        q_body, name=name + "_q", grid=(nb, n_j),
        in_specs=[pl.BlockSpec((SEQ, LANES), lambda b, j: (b, j)), tab, tab, tab],
        out_specs=pl.BlockSpec((N_BRANCH, SEQ, LANES), lambda b, j: (0, b, j)),
        out_shape=jax.ShapeDtypeStruct((N_BRANCH, t, ATTN_WIDTH), bf16), scratch_shapes=[pltpu.VMEM((SEQ, LANES), f32)],
        compiler_params=_cparams(("parallel", "parallel")),
    )(qkv, *tabs)
    kv = pl.pallas_call(
        kv_body, name=name + "_kv", grid=(nb, 2),
        in_specs=[pl.BlockSpec((SEQ, LANES), lambda b, j: (b, n_j + j)), tab, tab, tab],
        out_specs=pl.BlockSpec((1, N_BRANCH, SEQ, N_KV_HEADS * LANES), lambda b, j: (j, 0, b, 0)),
        out_shape=jax.ShapeDtypeStruct((2, N_BRANCH, t, N_KV_HEADS * LANES), bf16), scratch_shapes=[pltpu.VMEM((SEQ, LANES), f32)],
        compiler_params=_cparams(("parallel", "parallel")),
    )(qkv, *tabs)
    return q, kv


def _attn_fwd(prep, name):
    q_all, kv_all = prep
    t = q_all.shape[1]
    nb = t // SEQ
    n_blk = SEQ // ATTN_BLOCK

    def body(q_ref, k_ref, v_ref, o_ref, lse_ref, ob, lb, o0, o1, o2, l0, l1, l2, ss):
        cur_ok, prev_ok, low = _attn_masks()
        onat, lnat = (o0, o1, o2), (l0, l1, l2)
        for bi, d in enumerate(DILATIONS):
            qd, kd, vd = q_ref.at[bi], k_ref.at[0, bi], v_ref.at[0, bi]
            per_res = n_blk // d

            def scores(n):
                cur, prev = pl.ds(n * ATTN_BLOCK, ATTN_BLOCK), pl.ds(max(n - 1, 0) * ATTN_BLOCK, ATTN_BLOCK)
                has_prev = n % per_res != 0
                qb = qd[cur, :]
                kc = kd[cur, :]
                if has_prev:
                    kp = kd[prev, :]
                for a in range(2):
                    qa = jnp.where(low if a == 0 else ~low, qb, jnp.zeros_like(qb))
                    ss[2 * n + a, :, 0:ATTN_BLOCK] = jnp.where(cur_ok, _nt(qa, kc), NEG_INF)
                    if has_prev:
                        ss[2 * n + a, :, ATTN_BLOCK:2 * ATTN_BLOCK] = jnp.where(prev_ok, _nt(qa, kp), NEG_INF)

            def softmax_pv(n):
                cur, prev = pl.ds(n * ATTN_BLOCK, ATTN_BLOCK), pl.ds(max(n - 1, 0) * ATTN_BLOCK, ATTN_BLOCK)
                has_prev = n % per_res != 0
                vc = vd[cur, :]
                if has_prev:
                    vp = vd[prev, :]
                outs, lses = [], []
                for a in range(2):
                    sc = ss[2 * n + a, :, 0:ATTN_BLOCK]
                    if has_prev:
                        sp = ss[2 * n + a, :, ATTN_BLOCK:2 * ATTN_BLOCK]
                        m = jnp.max(jnp.maximum(sc, sp), axis=1, keepdims=True)
                        pc, pp = jnp.exp(sc - m), jnp.exp(sp - m)
                        den = jnp.sum(pc + pp, axis=1, keepdims=True)
                        acc = _nn(pc.astype(bf16), vc) + _nn(pp.astype(bf16), vp)
                    else:
                        m = jnp.max(sc, axis=1, keepdims=True)
                        pc = jnp.exp(sc - m)
                        den = jnp.sum(pc, axis=1, keepdims=True)
                        acc = _nn(pc.astype(bf16), vc)
                    outs.append(acc * (1.0 / den))
                    lses.append(m + jnp.log(den))
                ob[cur, :] = jnp.where(low, outs[0], outs[1])
                lb[cur, :] = jnp.where(low, lses[0], lses[1])

            for n in range(n_blk):
                scores(n)
            for n in range(n_blk):
                softmax_pv(n)
            _interleave_store(ob, onat[bi], d, False)
            _interleave_store(lb, lnat[bi], d, False)
        la, lbb, lc = l0[...], l1[...], l2[...]
        lm = jnp.maximum(jnp.maximum(la, lbb), lc)
        wa, wb, wc = jnp.exp(la - lm), jnp.exp(lbb - lm), jnp.exp(lc - lm)
        ws = wa + wb + wc
        o_ref[...] = (wa * o0[...] + wb * o1[...] + wc * o2[...]) / ws
        lse_ref[...] = lm + jnp.log(ws)

    def col(jj):
        return pl.BlockSpec((SEQ, LANES), lambda b, j: (b, jj if jj is not None else j))

    fs = pltpu.VMEM((SEQ, LANES), f32)
    return pl.pallas_call(
        body, name=name, grid=(nb, ATTN_WIDTH // LANES),
        in_specs=[pl.BlockSpec((N_BRANCH, SEQ, LANES), lambda b, j: (0, b, j)),
                  pl.BlockSpec((1, N_BRANCH, SEQ, LANES), lambda b, j: (0, 0, b, j // 2)),
                  pl.BlockSpec((1, N_BRANCH, SEQ, LANES), lambda b, j: (1, 0, b, j // 2))],
        out_specs=[col(None), col(None)],
        out_shape=[jax.ShapeDtypeStruct((t, ATTN_WIDTH), f32), jax.ShapeDtypeStruct((t, ATTN_WIDTH), f32)],
        scratch_shapes=[fs, fs, fs, fs, fs, fs, fs, fs, pltpu.VMEM((2 * n_blk, ATTN_BLOCK, 2 * ATTN_BLOCK), f32)],
        compiler_params=_cparams(("parallel", "parallel")),
    )(q_all, kv_all, kv_all)


def _attn_bwd(prep, tabs, o, lse, do, name):
    q_all, kv_all = prep
    t = q_all.shape[1]
    nb = t // SEQ
    n_blk = SEQ // ATTN_BLOCK
    n_j = ATTN_WIDTH // LANES

    def body(q_ref, k_ref, v_ref, c_ref, s1_ref, s2_ref, o_ref, lse_ref, do_ref, dq_ref, dk_ref, dv_ref,
             dl, dod, lsd, dld, dqd, dkd, dvd, dqa, dka, dva, pb, dsb, dk_acc, dv_acc):
        j = pl.program_id(1)
        kvh = j // 2
        cur_ok, prev_ok, low = _attn_masks()
        lowfull = lax.broadcasted_iota(jnp.int32, (SEQ, LANES), 1) < HEAD_DIM
        c, s1, s2 = c_ref[...], s1_ref[...], s2_ref[...]
        prod = do_ref[...] * o_ref[...]
        d_lo = jnp.sum(jnp.where(lowfull, prod, 0.0), axis=1, keepdims=True)
        d_hi = jnp.sum(jnp.where(lowfull, 0.0, prod), axis=1, keepdims=True)
        dl[...] = jnp.where(lowfull, d_lo, d_hi)
        dqa[...] = jnp.zeros_like(dqa)
        dka[...] = jnp.zeros_like(dka)
        dva[...] = jnp.zeros_like(dva)
        for bi, d in enumerate(DILATIONS):
            qd, kd, vd = q_ref.at[bi], k_ref.at[0, bi], v_ref.at[0, bi]
            _deinterleave(do_ref, dod, d, bf16)
            _deinterleave(lse_ref, lsd, d, f32)
            _deinterleave(dl, dld, d, f32)
            per_res = n_blk // d
            curl, prevl = slice(0, ATTN_BLOCK), slice(ATTN_BLOCK, 2 * ATTN_BLOCK)

            def halves(x):
                zero = jnp.zeros_like(x)
                return jnp.where(low, x, zero), jnp.where(low, zero, x)

            def blk(n):
                return pl.ds(n * ATTN_BLOCK, ATTN_BLOCK)

            def has_prev(n):
                return n < n_blk and n % per_res != 0

            def probs(n):
                cur = blk(n)
                qas, doas = halves(qd[cur, :]), halves(dod[cur, :])
                kc, vc = kd[cur, :], vd[cur, :]
                if has_prev(n):
                    kp, vp = kd[blk(n - 1), :], vd[blk(n - 1), :]
                lsb, dlb = lsd[cur, :], dld[cur, :]
                for a in range(2):
                    ls = lsb[:, a * HEAD_DIM:a * HEAD_DIM + 1]
                    de = dlb[:, a * HEAD_DIM:a * HEAD_DIM + 1]
                    pc = jnp.exp(jnp.where(cur_ok, _nt(qas[a], kc), NEG_INF) - ls)
                    pb[2 * n + a, :, curl] = pc.astype(bf16)
                    dsb[2 * n + a, :, curl] = (pc * (_nt(doas[a], vc) - de)).astype(bf16)
                    if has_prev(n):
                        pp = jnp.exp(jnp.where(prev_ok, _nt(qas[a], kp), NEG_INF) - ls)
                        pb[2 * n + a, :, prevl] = pp.astype(bf16)
                        dsb[2 * n + a, :, prevl] = (pp * (_nt(doas[a], vp) - de)).astype(bf16)

            def grads(n):
                cur = blk(n)
                kc = kd[cur, :]
                dqs = [_nn(dsb[2 * n + a, :, curl], kc) for a in range(2)]
                q_rows, do_rows = list(halves(qd[cur, :])), list(halves(dod[cur, :]))
                ds_rows, p_rows = [dsb[2 * n + a, :, curl] for a in range(2)], [pb[2 * n + a, :, curl] for a in range(2)]
                if has_prev(n):
                    kp = kd[blk(n - 1), :]
                    dqs = [dqs[a] + _nn(dsb[2 * n + a, :, prevl], kp) for a in range(2)]
                if has_prev(n + 1):
                    q_rows += list(halves(qd[blk(n + 1), :]))
                    do_rows += list(halves(dod[blk(n + 1), :]))
                    ds_rows += [dsb[2 * n + 2 + a, :, prevl] for a in range(2)]
                    p_rows += [pb[2 * n + 2 + a, :, prevl] for a in range(2)]
                dqd[cur, :] = jnp.where(low, dqs[0], dqs[1])
                dkd[cur, :] = _tn(jnp.concatenate(ds_rows, axis=0), jnp.concatenate(q_rows, axis=0))
                dvd[cur, :] = _tn(jnp.concatenate(p_rows, axis=0), jnp.concatenate(do_rows, axis=0))

            for n in range(n_blk):
                probs(n)
            for n in range(n_blk):
                grads(n)
            _interleave_store(dqd, dqa, d, True)
            _interleave_store(dkd, dka, d, True)
            _interleave_store(dvd, dva, d, True)
        dq_ref[...] = _rot_t(dqa[...] * (HEAD_DIM ** -0.5), c, s1, s2).astype(bf16)
        dkf = dka[...]
        dkf = _rot_t(dkf + pltpu.roll(dkf, HEAD_DIM, 1), c, s1, s2)
        dvf = dva[...]
        dvf = dvf + pltpu.roll(dvf, HEAD_DIM, 1)
        mine = (lax.broadcasted_iota(jnp.int32, (SEQ, LANES), 1) // HEAD_DIM) == kvh
        dkc_, dvc_ = jnp.where(mine, dkf, 0.0), jnp.where(mine, dvf, 0.0)

        @pl.when(j == 0)
        def _():
            dk_acc[...] = dkc_
            dv_acc[...] = dvc_

        @pl.when(j > 0)
        def _():
            dk_acc[...] += dkc_
            dv_acc[...] += dvc_

        @pl.when(j == n_j - 1)
        def _():
            dk_ref[...] = dk_acc[...].astype(bf16)
            dv_ref[...] = dv_acc[...].astype(bf16)

    def col(jj):
        return pl.BlockSpec((SEQ, LANES), lambda b, j: (b, jj if jj is not None else j))

    tab = pl.BlockSpec((SEQ, LANES), lambda b, j: (b, 0))
    fs = pltpu.VMEM((SEQ, LANES), f32)
    hs = pltpu.VMEM((SEQ, LANES), bf16)
    return pl.pallas_call(
        body, name=name, grid=(nb, n_j),
        in_specs=[pl.BlockSpec((N_BRANCH, SEQ, LANES), lambda b, j: (0, b, j)),
                  pl.BlockSpec((1, N_BRANCH, SEQ, LANES), lambda b, j: (0, 0, b, j // 2)),
                  pl.BlockSpec((1, N_BRANCH, SEQ, LANES), lambda b, j: (1, 0, b, j // 2)),
                  tab, tab, tab, col(None), col(None), col(None)],
        out_specs=[col(None), tab, tab],
        out_shape=[jax.ShapeDtypeStruct((t, ATTN_WIDTH), bf16), jax.ShapeDtypeStruct((t, LANES), bf16), jax.ShapeDtypeStruct((t, LANES), bf16)],
        scratch_shapes=[fs, hs, fs, fs, fs, fs, fs, fs, fs, fs,
                        pltpu.VMEM((2 * n_blk, ATTN_BLOCK, 2 * ATTN_BLOCK), bf16), pltpu.VMEM((2 * n_blk, ATTN_BLOCK, 2 * ATTN_BLOCK), bf16), fs, fs],
        compiler_params=_cparams(("parallel", "arbitrary")),
    )(q_all, kv_all, kv_all, *tabs, o, lse, do)


def _tap(w_ref, s):
    return w_ref[CONV_WIDTH - 1 - s:CONV_WIDTH - s, :]


def _conv_pre(x, w_ref, b_ref, row):
    shifted = [x] + [jnp.where(row >= s, pltpu.roll(x, s, 0), 0.0) for s in range(1, CONV_WIDTH)]
    pre = b_ref[...] + _tap(w_ref, 0) * x
    for s in range(1, CONV_WIDTH):
        pre = pre + _tap(w_ref, s) * shifted[s]
    return pre, shifted


def _conv_fwd(x, w, b, name, tc=512):
    t, ch = x.shape

    def body(x_ref, w_ref, b_ref, o_ref):
        row = lax.broadcasted_iota(jnp.int32, (SEQ, tc), 0)
        pre, _ = _conv_pre(x_ref[...], w_ref, b_ref, row)
        o_ref[...] = _silu(pre)

    xs = pl.BlockSpec((SEQ, tc), lambda i, j: (i, j))
    return pl.pallas_call(
        body, name=name, grid=(t // SEQ, ch // tc),
        in_specs=[xs, pl.BlockSpec((CONV_WIDTH, tc), lambda i, j: (0, j)), pl.BlockSpec((1, tc), lambda i, j: (0, j))],
        out_specs=xs, out_shape=jax.ShapeDtypeStruct((t, ch), f32),
        compiler_params=_cparams(("parallel", "parallel")),
    )(x, w, b)


def _conv_bwd(x, w, b, dact, name, tc=512):
    t, ch = x.shape

    def body(x_ref, w_ref, b_ref, d_ref, dx_ref, dw_ref, db_ref):
        row = lax.broadcasted_iota(jnp.int32, (SEQ, tc), 0)
        pre, shifted = _conv_pre(x_ref[...], w_ref, b_ref, row)
        dpre = d_ref[...] * _dsilu(pre)
        dx = _tap(w_ref, 0) * dpre
        for s in range(1, CONV_WIDTH):
            dx = dx + _tap(w_ref, s) * jnp.where(row < SEQ - s, pltpu.roll(dpre, SEQ - s, 0), 0.0)
        dx_ref[...] = dx.astype(bf16)
        first = pl.program_id(1) == 0
        parts = [jnp.sum(dpre * shifted[CONV_WIDTH - 1 - k], axis=0, keepdims=True) for k in range(CONV_WIDTH)]
        dbp = jnp.sum(dpre, axis=0, keepdims=True)

        @pl.when(first)
        def _():
            for k in range(CONV_WIDTH):
                dw_ref[k:k + 1, :] = parts[k]
            db_ref[...] = dbp

        @pl.when(jnp.logical_not(first))
        def _():
            for k in range(CONV_WIDTH):
                dw_ref[k:k + 1, :] += parts[k]
            db_ref[...] += dbp

    xs = pl.BlockSpec((SEQ, tc), lambda j, i: (i, j))
    ws = pl.BlockSpec((CONV_WIDTH, tc), lambda j, i: (0, j))
    bs = pl.BlockSpec((1, tc), lambda j, i: (0, j))
    return pl.pallas_call(
        body, name=name, grid=(ch // tc, t // SEQ),
        in_specs=[xs, ws, bs, xs], out_specs=[xs, ws, bs],
        out_shape=[jax.ShapeDtypeStruct((t, ch), bf16), jax.ShapeDtypeStruct((CONV_WIDTH, ch), f32), jax.ShapeDtypeStruct((1, ch), f32)],
        compiler_params=_cparams(("parallel", "arbitrary")),
    )(x, w, b, dact)


GROUP_W = SSM_INNER // SSM_GROUPS
HEADS_PER_GROUP = SSM_HEADS // SSM_GROUPS


def _split3(x):
    hi = x.astype(bf16)
    r1 = x - hi.astype(f32)
    mid = r1.astype(bf16)
    lo = (r1 - mid.astype(f32)).astype(bf16)
    return hi, mid, lo


def _dot_exact(x, sel, dims, x_is_lhs=True):
    parts = _split3(x)
    if x_is_lhs:
        return _dot(parts[0], sel, dims) + _dot(parts[1], sel, dims) + _dot(parts[2], sel, dims)
    return _dot(sel, parts[0], dims) + _dot(sel, parts[1], dims) + _dot(sel, parts[2], dims)


def _ssd_common(xbc_ref, dt_ref, bias_ref, alog_ref):
    r = lax.broadcasted_iota(jnp.int32, (CHUNK, CHUNK), 0)
    cidx = lax.broadcasted_iota(jnp.int32, (CHUNK, CHUNK), 1)
    causal = r >= cidx
    tril = causal.astype(bf16)
    expand = (lax.broadcasted_iota(jnp.int32, (CHUNK, SSM_INNER), 0)
              == lax.broadcasted_iota(jnp.int32, (CHUNK, SSM_INNER), 1) // HEAD_DIM).astype(bf16)
    head_lane = cidx < SSM_HEADS
    dtp = dt_ref[...] + bias_ref[...]
    dt = jnp.where(head_lane, _softplus(dtp), 0.0)
    a_neg = -jnp.exp(alog_ref[...])
    a = dt * a_neg
    nn_dims = ((1,), (0,))
    cs = _dot_exact(a, tril, nn_dims, x_is_lhs=False)
    dt_e = _dot_exact(dt, expand, nn_dims)
    cs_e = _dot_exact(cs, expand, nn_dims)
    xs = xbc_ref[:, 0:SSM_INNER]
    xg = xs * dt_e
    ecs = jnp.exp(cs_e)
    cs_last = cs_e[CHUNK - 1:CHUNK, :]
    dse = jnp.exp(cs_last - cs_e)
    cde = jnp.exp(cs_last)
    return dict(r=r, cidx=cidx, causal=causal, tril=tril, expand=expand, head_lane=head_lane, dtp=dtp, dt=dt, a_neg=a_neg,
                cs=cs, cst=cs.T, dt_e=dt_e, cs_e=cs_e, xs=xs, xg=xg, ecs=ecs, dse=dse, cde=cde)


def _decay_mat(q, h):
    return jnp.exp(jnp.where(q["causal"], q["cs"][:, h:h + 1] - q["cst"][h:h + 1, :], NEG_INF))


def _gate_norm(y, z, nw, gate=None):
    y2 = y * (_silu(z) if gate is None else gate)
    outs, xhats, rs = [], [], []
    for g in range(SSM_GROUPS):
        sl = slice(g * GROUP_W, (g + 1) * GROUP_W)
        yg = y2[:, sl]
        r = lax.rsqrt(jnp.mean(yg * yg, axis=-1, keepdims=True) + EPS)
        xhats.append(yg * r)
        rs.append(r)
        outs.append(yg * r * nw[:, sl])
    return y2, outs, xhats, rs


def _ssd_fwd(xbc, z, dtp, params, name):
    t = xbc.shape[0]
    n_chunk = SEQ // CHUNK
    low = None

    def body(xbc_ref, z_ref, dt_ref, bias_ref, alog_ref, dskip_ref, nw_ref, yn_ref, y_ref, hs_ref, h_scr):
        @pl.when(pl.program_id(1) == 0)
        def _():
            h_scr[...] = jnp.zeros_like(h_scr)

        q = _ssd_common(xbc_ref, dt_ref, bias_ref, alog_ref)
        low = lax.broadcasted_iota(jnp.int32, (CHUNK, LANES), 1) < HEAD_DIM
        xgb = q["xg"].astype(bf16)
        wst = (q["xg"] * q["dse"]).astype(bf16)
        hs_ref[0] = h_scr[...]
        ys = []
        for g in range(SSM_GROUPS):
            gl = slice(g * GROUP_W, (g + 1) * GROUP_W)
            bg = xbc_ref[:, SSM_INNER + g * D_STATE:SSM_INNER + (g + 1) * D_STATE].astype(bf16)
            cg = xbc_ref[:, SSM_INNER + SSM_GROUPS * D_STATE + g * D_STATE:SSM_INNER + SSM_GROUPS * D_STATE + (g + 1) * D_STATE].astype(bf16)
            cb = _nt(cg, bg)
            hg = h_scr[g]
            yoff = _nn(cg, hg.astype(bf16)) * q["ecs"][:, gl]
            pieces = []
            for i in range(HEADS_PER_GROUP // 2):
                h0 = g * HEADS_PER_GROUP + 2 * i
                xp = xgb[:, h0 * HEAD_DIM:(h0 + 2) * HEAD_DIM]
                m0 = (cb * _decay_mat(q, h0)).astype(bf16)
                m1 = (cb * _decay_mat(q, h0 + 1)).astype(bf16)
                zero = jnp.zeros_like(xp)
                pieces.append(_nn(m0, jnp.where(low, xp, zero)) + _nn(m1, jnp.where(low, zero, xp)))
            ys.append(jnp.concatenate(pieces, axis=1) + yoff + dskip_ref[:, gl] * q["xs"][:, gl])
            h_scr[g] = hg * q["cde"][:, gl] + _tn(bg, wst[:, gl])
        y = jnp.concatenate(ys, axis=1)
        y_ref[...] = y
        _, outs, _, _ = _gate_norm(y, z_ref[...], nw_ref[...])
        yn_ref[...] = jnp.concatenate(outs, axis=1).astype(bf16)

    def rows(w):
        return pl.BlockSpec((CHUNK, w), lambda b, c: (b * n_chunk + c, 0))

    def par(w):
        return pl.BlockSpec((1, w), lambda b, c: (0, 0))

    return pl.pallas_call(
        body, name=name, grid=(t // SEQ, n_chunk),
        in_specs=[rows(CONV_CH), rows(SSM_INNER), rows(LANES), par(LANES), par(LANES), par(SSM_INNER), par(SSM_INNER)],
        out_specs=[rows(SSM_INNER), rows(SSM_INNER), pl.BlockSpec((1, SSM_GROUPS, D_STATE, GROUP_W), lambda b, c: (b * n_chunk + c, 0, 0, 0))],
        out_shape=[jax.ShapeDtypeStruct((t, SSM_INNER), bf16), jax.ShapeDtypeStruct((t, SSM_INNER), f32),
                   jax.ShapeDtypeStruct((t // CHUNK, SSM_GROUPS, D_STATE, GROUP_W), f32)],
        scratch_shapes=[pltpu.VMEM((SSM_GROUPS, D_STATE, GROUP_W), f32)],
        compiler_params=_cparams(("parallel", "arbitrary")),
    )(xbc, z, dtp, *params)


def _ssd_bwd(xbc, z, dtp, y, hs, dyn, params, name):
    t = xbc.shape[0]
    n_chunk = SEQ // CHUNK

    def body(xbc_ref, z_ref, dt_ref, y_ref, hs_ref, dyn_ref, bias_ref, alog_ref, dskip_ref, nw_ref,
             dxbc_ref, dz_ref, ddt_ref, dnw_ref, dds_ref, dal_ref, dbi_ref, dh_scr):
        @pl.when(pl.program_id(1) == 0)
        def _():
            dh_scr[...] = jnp.zeros_like(dh_scr)

        q = _ssd_common(xbc_ref, dt_ref, bias_ref, alog_ref)
        low = lax.broadcasted_iota(jnp.int32, (CHUNK, LANES), 1) < HEAD_DIM
        last_row = lax.broadcasted_iota(jnp.int32, (CHUNK, GROUP_W), 0) == CHUNK - 1
        xs, xg = q["xs"], q["xg"]
        xgb = xg.astype(bf16)
        wf = xg * q["dse"]
        wst = wf.astype(bf16)
        zz = z_ref[...]
        yy = y_ref[...]
        sz, dsz = _silu_and_grad(zz)
        y2, _, xhats, rs = _gate_norm(yy, zz, nw_ref[...], gate=sz)
        dyn_ = dyn_ref[...]
        dy2s, dnws = [], []
        for g in range(SSM_GROUPS):
            gl = slice(g * GROUP_W, (g + 1) * GROUP_W)
            gw = dyn_[:, gl] * nw_ref[:, gl]
            dy2s.append(rs[g] * (gw - xhats[g] * jnp.mean(gw * xhats[g], axis=-1, keepdims=True)))
            dnws.append(_rowsum8(dyn_[:, gl] * xhats[g]))
        dy2 = jnp.concatenate(dy2s, axis=1)
        dy = dy2 * sz
        dz_ref[...] = (dy2 * yy * dsz).astype(bf16)
        dnw_p = jnp.concatenate(dnws, axis=1)
        dds_p = _rowsum8(dy * xs)
        dyb = dy.astype(bf16)
        gfull = (dy * q["ecs"]).astype(bf16)
        dcs_c = jnp.zeros((CHUNK, CHUNK), f32)
        dcs_r = jnp.zeros((CHUNK, CHUNK), f32)
        dcs_e_parts, dxg_parts = [], []
        for g in range(SSM_GROUPS):
            gl = slice(g * GROUP_W, (g + 1) * GROUP_W)
            bsl = slice(SSM_INNER + g * D_STATE, SSM_INNER + (g + 1) * D_STATE)
            csl = slice(SSM_INNER + SSM_GROUPS * D_STATE + g * D_STATE, SSM_INNER + SSM_GROUPS * D_STATE + (g + 1) * D_STATE)
            bg = xbc_ref[:, bsl].astype(bf16)
            cg = xbc_ref[:, csl].astype(bf16)
            cb = _nt(cg, bg)
            hg = hs_ref[0, g]
            hgb = hg.astype(bf16)
            dhn = dh_scr[g]
            dhnb = dhn.astype(bf16)
            yoff = _nn(cg, hgb) * q["ecs"][:, gl]
            dw_ = _nn(bg, dhnb)
            r_e = dw_ * wf[:, gl]
            to_last = jnp.sum(r_e, axis=0, keepdims=True) + jnp.sum(dhn * hg, axis=0, keepdims=True) * q["cde"][:, gl]
            dcs_e_parts.append(dy[:, gl] * yoff - r_e + jnp.where(last_row, to_last, 0.0))
            dcb = jnp.zeros((CHUNK, CHUNK), f32)
            dxg_pairs = []
            for i in range(HEADS_PER_GROUP // 2):
                h0 = g * HEADS_PER_GROUP + 2 * i
                psl = slice(h0 * HEAD_DIM, (h0 + 2) * HEAD_DIM)
                xp = xgb[:, psl]
                dyp = dyb[:, psl]
                zero = jnp.zeros_like(dyp)
                tns = []
                for a in range(2):
                    h = h0 + a
                    lm = _decay_mat(q, h)
                    m = cb * lm
                    dm = _nt(jnp.where(low, dyp, zero) if a == 0 else jnp.where(low, zero, dyp), xp)
                    dcb = dcb + dm * lm
                    nmat = dm * m
                    dcs_c = dcs_c + jnp.where(q["cidx"] == h, jnp.sum(nmat, axis=1, keepdims=True), 0.0)
                    dcs_r = dcs_r + jnp.where(q["r"] == h, jnp.sum(nmat, axis=0, keepdims=True), 0.0)
                    tns.append(_tn(m.astype(bf16), dyp))
                dxg_pairs.append(jnp.where(low, tns[0], tns[1]))
            dxg_parts.append(jnp.concatenate(dxg_pairs, axis=1) + dw_ * q["dse"][:, gl])
            dcbb = dcb.astype(bf16)
            dxbc_ref[:, csl] = _nt(gfull[:, gl], hgb) + _nn(dcbb, bg)
            dxbc_ref[:, bsl] = _nt(wst[:, gl], dhnb) + _tn(dcbb, cg)
            dh_scr[g] = dhn * q["cde"][:, gl] + _tn(cg, gfull[:, gl])
        dxg = jnp.concatenate(dxg_parts, axis=1)
        dcs_e = jnp.concatenate(dcs_e_parts, axis=1)
        dxbc_ref[:, 0:SSM_INNER] = dskip_ref[...] * dy + dxg * q["dt_e"]
        dcs = dcs_c - dcs_r.T + _dot_exact(dcs_e, q["expand"], ((1,), (1,)))
        triu = (q["cidx"] >= q["r"]).astype(bf16)
        da = _dot_exact(dcs, triu, ((1,), (0,)), x_is_lhs=False)
        ddt = _dot_exact(dxg * xs, q["expand"], ((1,), (1,))) + da * q["a_neg"]
        ddtp = jnp.where(q["head_lane"], ddt * _sigmoid(q["dtp"]), 0.0)
        ddt_ref[...] = ddtp.astype(bf16)
        dal_p = _rowsum8(da * q["dt"]) * q["a_neg"]
        dbi_p = _rowsum8(ddtp)
        first = (pl.program_id(0) == 0) & (pl.program_id(1) == 0)

        @pl.when(first)
        def _():
            dnw_ref[...] = dnw_p
            dds_ref[...] = dds_p
            dal_ref[...] = dal_p
            dbi_ref[...] = dbi_p

        @pl.when(jnp.logical_not(first))
        def _():
            dnw_ref[...] += dnw_p
            dds_ref[...] += dds_p
            dal_ref[...] += dal_p
            dbi_ref[...] += dbi_p

    def rows(w):
        return pl.BlockSpec((CHUNK, w), lambda b, c: (b * n_chunk + n_chunk - 1 - c, 0))

    def par(w):
        return pl.BlockSpec((1, w), lambda b, c: (0, 0))

    def acc(w):
        return pl.BlockSpec((SUBLANES, w), lambda b, c: (0, 0))

    return pl.pallas_call(
        body, name=name, grid=(t // SEQ, n_chunk),
        in_specs=[rows(CONV_CH), rows(SSM_INNER), rows(LANES), rows(SSM_INNER),
                  pl.BlockSpec((1, SSM_GROUPS, D_STATE, GROUP_W), lambda b, c: (b * n_chunk + n_chunk - 1 - c, 0, 0, 0)),
                  rows(SSM_INNER), par(LANES), par(LANES), par(SSM_INNER), par(SSM_INNER)],
        out_specs=[rows(CONV_CH), rows(SSM_INNER), rows(LANES), acc(SSM_INNER), acc(SSM_INNER), acc(LANES), acc(LANES)],
        out_shape=[jax.ShapeDtypeStruct((t, CONV_CH), f32), jax.ShapeDtypeStruct((t, SSM_INNER), bf16), jax.ShapeDtypeStruct((t, LANES), bf16),
                   jax.ShapeDtypeStruct((SUBLANES, SSM_INNER), f32), jax.ShapeDtypeStruct((SUBLANES, SSM_INNER), f32),
                   jax.ShapeDtypeStruct((SUBLANES, LANES), f32), jax.ShapeDtypeStruct((SUBLANES, LANES), f32)],
        scratch_shapes=[pltpu.VMEM((SSM_GROUPS, D_STATE, GROUP_W), f32)],
        compiler_params=_cparams(("arbitrary", "arbitrary")),
    )(xbc, z, dtp, y, hs, dyn, *params)


def _adamw_update(g, w, m, v):
    mm = ADAM_B1 * m + (1.0 - ADAM_B1) * g
    vv = ADAM_B2 * v + (1.0 - ADAM_B2) * (g * g)
    m_hat = mm / (1.0 - ADAM_B1 ** ADAM_STEP)
    v_hat = vv / (1.0 - ADAM_B2 ** ADAM_STEP)
    return -ADAM_LR * (m_hat / (jnp.sqrt(v_hat) + ADAM_EPS) + ADAM_WD * w), mm, vv


def _adamw(g_parts, w, m, v, name):
    rows, width = w.shape
    n = len(g_parts)
    tr = _row_tile(rows)

    def body(*refs):
        g_refs, (w_ref, m_ref, v_ref, g_out, d_out, m_out, v_out) = refs[:n], refs[n:]
        g = g_refs[0][...].astype(f32)
        for r in g_refs[1:]:
            g = g + r[...].astype(f32)
        g_out[...] = g
        d_out[...], m_out[...], v_out[...] = _adamw_update(g, w_ref[...], m_ref[...], v_ref[...])

    spec = pl.BlockSpec((tr, width), lambda i: (i, 0))
    return pl.pallas_call(
        body, name=name, grid=(rows // tr,), in_specs=[spec] * (n + 3), out_specs=[spec] * 4,
        out_shape=[jax.ShapeDtypeStruct((rows, width), f32)] * 4, compiler_params=_cparams(("parallel",)),
    )(*g_parts, w, m, v)


def _adamw_layers(landed, w, m, v, after, name, layers_on_columns=False):
    depth = len(landed)
    _, rows, width = landed[0].shape
    tr = _row_tile(rows)
    n_i = rows // tr
    at = (lambda ref: ref) if layers_on_columns else (lambda ref: ref.at[0])

    def body(*refs):
        part_refs, (w_ref, m_ref, v_ref, _, g_out, d_out, m_out, v_out) = refs[:depth * N_DEV], refs[depth * N_DEV:]
        for l in range(depth):
            @pl.when(pl.program_id(0) == l)
            def _(l=l):
                g = part_refs[l * N_DEV][0].astype(f32)
                for r in part_refs[l * N_DEV + 1:(l + 1) * N_DEV]:
                    g = g + r[0].astype(f32)
                at(g_out)[...] = g
                at(d_out)[...], at(m_out)[...], at(v_out)[...] = _adamw_update(g, at(w_ref)[...], at(m_ref)[...], at(v_ref)[...])

    def part_spec(l, p):
        return pl.BlockSpec((1, tr, width), lambda ll, i: (p, jnp.where(ll == l, i, jnp.where(ll < l, 0, n_i - 1)), 0))

    state = (pl.BlockSpec((tr, width), lambda ll, i: (i, ll)) if layers_on_columns
             else pl.BlockSpec((1, tr, width), lambda ll, i: (ll, i, 0)))
    return pl.pallas_call(
        body, name=name, grid=(depth, n_i),
        in_specs=[part_spec(l, p) for l in range(depth) for p in range(N_DEV)] + [state] * 3 + [ANY], out_specs=[state] * 4,
        out_shape=[jax.ShapeDtypeStruct(w.shape, f32)] * 4, compiler_params=_cparams(("arbitrary", "arbitrary")),
    )(*[landed[l] for l in range(depth) for _ in range(N_DEV)], w, m, v, after)


def _row_tile(rows, cap=512):
    for cand in range(min(rows, cap) // SUBLANES * SUBLANES, 0, -SUBLANES):
        if rows % cand == 0:
            return cand
    return rows


def _cols_from_devices(g, width, name):
    n_dev, depth, a, b = g.shape

    def body(g_ref, o_ref):
        for i in range(n_dev):
            o_ref[0, :, i * b:(i + 1) * b] = g_ref[i, 0]
        if width > n_dev * b:
            o_ref[0, :, n_dev * b:width] = jnp.zeros((a, width - n_dev * b), o_ref.dtype)

    return pl.pallas_call(
        body, name=name, grid=(depth,), in_specs=[pl.BlockSpec((n_dev, 1, a, b), lambda l: (0, l, 0, 0))],
        out_specs=pl.BlockSpec((1, a, width), lambda l: (l, 0, 0)), out_shape=jax.ShapeDtypeStruct((depth, a, width), g.dtype),
        compiler_params=_cparams(("parallel",)),
    )(g)


def _devices_from_cols(per_layer, b, name, tr=256):
    depth = len(per_layer)
    a, width = per_layer[0].shape

    def body(*refs):
        o_ref = refs[depth]
        for l in range(depth):
            for i in range(N_DEV):
                o_ref[i, l] = refs[l][:, i * b:(i + 1) * b]

    return pl.pallas_call(
        body, name=name, grid=(a // tr,), in_specs=[pl.BlockSpec((tr, width), lambda r: (r, 0))] * depth,
        out_specs=pl.BlockSpec((N_DEV, depth, tr, b), lambda r: (0, 0, r, 0)),
        out_shape=jax.ShapeDtypeStruct((N_DEV, depth, a, b), per_layer[0].dtype), compiler_params=_cparams(("parallel",)),
    )(*per_layer)


def _me():
    return lax.axis_index("x"), lax.axis_index("y"), lax.axis_index("c")


def _allgather_two_level(shards, name):
    n = len(shards)
    per = 7

    def body(*refs):
        ins, outs, token = refs[:n], refs[n:2 * n], refs[2 * n]
        send_sems, recv_sems, local_sems = refs[2 * n + 1:]
        token[...] = jnp.zeros_like(token)
        x, y, c = _me()
        me, sibling = (x, y, c), (x, y, 1 - c)
        chips = [(1 - x, y), (x, 1 - y), (1 - x, 1 - y)]

        def slot(a, p):
            return outs[a].at[4 * p[0] + 2 * p[1] + p[2]]

        def copy(a, k, block, to, src=None):
            return pltpu.make_async_remote_copy(
                src_ref=slot(a, block) if src is None else src, dst_ref=slot(a, block),
                send_sem=send_sems.at[a * per + k], recv_sem=recv_sems.at[a * per + k], device_id=to, device_id_type=MESH)

        mine = [pltpu.make_async_copy(ins[a], slot(a, me), local_sems.at[a]) for a in range(n)]
        for cp in mine:
            cp.start()
        first = []
        for a in range(n):
            first.append(copy(a, 0, me, sibling, src=ins[a]))
            first += [copy(a, 1 + j, me, (*chip, c), src=ins[a]) for j, chip in enumerate(chips)]
        for cp in first:
            cp.start()
        passed = []
        for j, chip in enumerate(chips):
            for a in range(n):
                copy(a, 1 + j, (*chip, c), me).wait_recv()
                fwd = copy(a, 4 + j, (*chip, c), sibling)
                fwd.start()
                passed.append(fwd)
        for a in range(n):
            copy(a, 0, sibling, me).wait_recv()
            for j, chip in enumerate(chips):
                copy(a, 4 + j, (*chip, 1 - c), me).wait_recv()
        for cp in first + passed:
            cp.wait_send()
        for cp in mine:
            cp.wait()

    outs = pl.pallas_call(
        body, name=name, in_specs=[ANY] * n, out_specs=[ANY] * n + [pl.BlockSpec(memory_space=pltpu.VMEM)],
        out_shape=[jax.ShapeDtypeStruct((N_DEV,) + s.shape, s.dtype) for s in shards] + [jax.ShapeDtypeStruct((SUBLANES, LANES), f32)],
        scratch_shapes=[pltpu.SemaphoreType.DMA((n * per,)), pltpu.SemaphoreType.DMA((n * per,)), pltpu.SemaphoreType.DMA((n,))],
    )(*shards)
    return outs[:n], outs[n]


def _allgather_direct(row, name):
    def body(in_ref, out_ref, send_sems, recv_sems, local_sem):
        x, y, c = _me()
        mine = out_ref.at[4 * x + 2 * y + c]
        local = pltpu.make_async_copy(in_ref, mine, local_sem)
        local.start()
        sends = []
        for k in range(1, N_DEV):
            px, py, pc = x ^ (k >> 2), y ^ ((k >> 1) & 1), c ^ (k & 1)
            sends.append(pltpu.make_async_remote_copy(
                src_ref=in_ref, dst_ref=mine, send_sem=send_sems.at[k - 1], recv_sem=recv_sems.at[k - 1],
                device_id=(px, py, pc), device_id_type=MESH))
        for cp in sends:
            cp.start()
        for k in range(1, N_DEV):
            px, py, pc = x ^ (k >> 2), y ^ ((k >> 1) & 1), c ^ (k & 1)
            theirs = out_ref.at[4 * px + 2 * py + pc]
            pltpu.make_async_remote_copy(
                src_ref=in_ref, dst_ref=theirs, send_sem=send_sems.at[k - 1], recv_sem=recv_sems.at[k - 1],
                device_id=(px, py, pc), device_id_type=MESH).wait_recv()
        for cp in sends:
            cp.wait_send()
        local.wait()

    return pl.pallas_call(
        body, name=name, in_specs=[ANY], out_specs=ANY, out_shape=jax.ShapeDtypeStruct((N_DEV,) + row.shape, row.dtype),
        scratch_shapes=[pltpu.SemaphoreType.DMA((N_DEV - 1,)), pltpu.SemaphoreType.DMA((N_DEV - 1,)), pltpu.SemaphoreType.DMA],
    )(row)


N_CHIP = N_DEV // 2
HBM = pl.BlockSpec(memory_space=pltpu.HBM)
SEM = pl.BlockSpec(memory_space=pltpu.SEMAPHORE)
EFFECT = pltpu.SideEffectType.DATAFLOW_SIDE_EFFECTING


def _peer(k):
    x, y, c = _me()
    return x ^ (k >> 2), y ^ ((k >> 1) & 1), c ^ (k & 1)


def _direct_copies(srcs, lands, send_sems, recv_sems, per_peer):
    x, y, c = _me()
    me = 4 * x + 2 * y + c
    copies = []
    for a in range(len(srcs)):
        for k in range(1, N_DEV):
            px, py, pc = _peer(k)
            piece = srcs[a].at[4 * px + 2 * py + pc] if per_peer else srcs[a]
            copies.append(pltpu.make_async_remote_copy(
                src_ref=piece, dst_ref=lands[a].at[me], send_sem=send_sems.at[a * (N_DEV - 1) + k - 1],
                recv_sem=recv_sems.at[a * (N_DEV - 1) + k - 1], device_id=(px, py, pc), device_id_type=MESH))
    return copies


def _direct_start(srcs, lands, per_peer, name):
    n = len(srcs)
    n_sem = n * (N_DEV - 1)

    def body(*refs):
        src_refs, land_refs = refs[:n], refs[n:2 * n]
        send_sems, recv_sems = refs[2 * n], refs[2 * n + 1]
        token = refs[-1]
        for cp in _direct_copies(src_refs, land_refs, send_sems, recv_sems, per_peer):
            cp.start()
        token[...] = jnp.zeros_like(token)

    outs = pl.pallas_call(
        body, name=name,
        out_shape=(pltpu.SemaphoreType.DMA((n_sem,)), pltpu.SemaphoreType.DMA((n_sem,)),
                   *[pltpu.HBM(s.shape, s.dtype) for s in srcs], *[pltpu.HBM(s.shape, s.dtype) for s in lands],
                   jax.ShapeDtypeStruct((SUBLANES, LANES), f32)),
        in_specs=[HBM] * (2 * n), out_specs=(SEM, SEM, *[HBM] * (2 * n), pl.BlockSpec(memory_space=pltpu.VMEM)),
        input_output_aliases={i: 2 + i for i in range(2 * n)},
        compiler_params=pltpu.CompilerParams(has_side_effects=EFFECT),
    )(*[pltpu.with_memory_space_constraint(s, pltpu.HBM) for s in srcs], *[pltpu.with_memory_space_constraint(s, pltpu.HBM) for s in lands])
    return outs[0], outs[1], outs[2:2 + n], outs[2 + n:2 + 2 * n], outs[-1]


def _direct_wait(send_sems, recv_sems, srcs, lands, after, per_peer, name):
    n = len(srcs)

    def body(*refs):
        src_refs, land_refs = refs[:n], refs[n:2 * n]
        s_sems, r_sems = refs[2 * n], refs[2 * n + 1]
        for cp in _direct_copies(src_refs, land_refs, s_sems, r_sems, per_peer):
            cp.wait_send()
            cp.wait_recv()

    outs = pl.pallas_call(
        body, name=name,
        out_shape=tuple(pltpu.HBM(s.shape, s.dtype) for s in list(srcs) + list(lands)),
        in_specs=[HBM] * (2 * n) + [SEM, SEM, ANY], out_specs=tuple([HBM] * (2 * n)),
        input_output_aliases={i: i for i in range(2 * n)},
        compiler_params=pltpu.CompilerParams(has_side_effects=EFFECT),
    )(*srcs, *lands, send_sems, recv_sems, after)
    return outs[n:]


def _row(v, width=None):
    v = v.reshape(1, -1).astype(f32)
    if width is not None and v.shape[1] < width:
        v = jnp.pad(v, ((0, 0), (0, width - v.shape[1])))
    return v


def _layer_params(p, l):
    return dict(
        norm_mix=_row(p["norm_mix"][l]), norm_ffn=_row(p["norm_ffn"][l]), conv_w=p["conv_w"][l], conv_b=_row(p["conv_b"][l]),
        ssd=(_row(p["dt_bias"][l], LANES), _row(p["a_log"][l], LANES), _row(jnp.repeat(p["d_skip"][l], HEAD_DIM)), _row(p["ssm_norm"][l])))


def _layer_fwd(h, w_in, rest, sp, tabs, l):
    tag = f"l{l}_"
    hn = _rmsnorm_fwd(h, sp["norm_mix"], tag + "norm_mix")
    qkv, z, xbc_pre = _in_proj(hn, w_in, (QKV_WIDTH, SSM_INNER, CONV_CH), tag + "proj")
    dtp = _matmul(hn, w_in, mode="nn", n_out=LANES, tn=LANES, b_off=DT_OFF // LANES, name=tag + "proj_dt")
    prep = _attn_prep(qkv, tabs, tag + "attn_prep")
    o, lse = _attn_fwd(prep, tag + "attn_fwd")
    xbc = _conv_fwd(xbc_pre, sp["conv_w"], sp["conv_b"], tag + "conv_fwd")
    yn, y, hs = _ssd_fwd(xbc, z, dtp, sp["ssd"], tag + "ssd_fwd")
    w_out, w_gate, w_up, w_down = rest(yn) if callable(rest) else rest
    h2 = _out_proj(o, yn, w_out, h, tag + "out_proj")
    hn2 = _rmsnorm_fwd(h2, sp["norm_ffn"], tag + "norm_ffn")
    g, u, act = _swiglu_fwd(hn2, w_gate, w_up, tag + "ffn_up")
    h3 = _matmul(act, w_down, mode="nn", tk=1408, add=h2, name=tag + "ffn_down")
    saved = dict(h=h, hn=hn, prep=prep, z=z, xbc_pre=xbc_pre, dtp=dtp, o=o, lse=lse, xbc=xbc, yn=yn, y=y, hs=hs, h2=h2, hn2=hn2, g=g, u=u, act=act,
                 rest=(w_out, w_gate, w_up, w_down))
    return h3, saved


def _layer_bwd(dh3_pair, s, big, sp, tabs, l, gd=f32, after_ffn=None):
    tag = f"l{l}_"
    dh3, dh3b = dh3_pair
    w_in, w_out, w_gate, w_up, w_down = big
    dg, du = _swiglu_bwd(dh3b, w_down, s["g"], s["u"], tag + "ffn_down_bwd")
    dw_down = _matmul(s["act"], dh3b, mode="tn", tm=1408, tn=512, tk=2048, out_dtype=gd, name=tag + "dw_down")
    dw_gate = _matmul(dg, s["hn2"], mode="tn", tm=1408, tn=512, tk=2048, out_dtype=gd, name=tag + "dw_gate")
    dw_up = _matmul(du, s["hn2"], mode="tn", tm=1408, tn=512, tk=2048, out_dtype=gd, name=tag + "dw_up")
    norm_ffn = sp["norm_ffn"] if after_ffn is None else sp["norm_ffn"] + after_ffn(dict(w_gate=dw_gate, w_up=dw_up, w_down=dw_down))
    dh2, dh2b, dnf = _nt_norm_bwd([(dg, w_gate), (du, w_up)], s["h2"], norm_ffn, dh3, tag + "ffn_up_bwd_norm", tk=1408, b_is_kd=True,
                                  vmem=VMEM_LIMIT_TWO_PAIRS)
    d_o = _matmul(dh2b, w_out, mode="nt", n_out=ATTN_WIDTH, tn=512, b_off=0, name=tag + "out_attn_bwd")
    dyn = _matmul(dh2b, w_out, mode="nt", n_out=SSM_INNER, tn=512, b_off=1, name=tag + "out_ssm_bwd")
    dw_out = jnp.concatenate([_matmul(s["o"], dh2b, mode="tn", tm=512, tn=512, tk=2048, out_dtype=gd, name=tag + "dw_out_attn"),
                              _matmul(s["yn"], dh2b, mode="tn", tm=512, tn=512, tk=2048, out_dtype=gd, name=tag + "dw_out_ssm")], axis=0)
    dxbc, dz, ddtp, dnw, dds, dal, dbi = _ssd_bwd(s["xbc"], s["z"], s["dtp"], s["y"], s["hs"], dyn, sp["ssd"], tag + "ssd_bwd")
    dxbc_pre, dconv_w, dconv_b = _conv_bwd(s["xbc_pre"], sp["conv_w"], sp["conv_b"], dxbc, tag + "conv_bwd")
    dq, dk, dv = _attn_bwd(s["prep"], tabs, s["o"], s["lse"], d_o, tag + "attn_bwd")
    dproj = jnp.concatenate([dq, dk, dv, dz, dxbc_pre, ddtp], axis=1)
    dw_in = _matmul(s["hn"], dproj, mode="tn", tm=512, tn=1152, tk=2048, out_dtype=gd, name=tag + "dw_in")
    res = _nt_norm_bwd([(dproj, w_in)], s["h"], sp["norm_mix"], dh2, tag + "proj_bwd_norm", tk=1152, bf16_copy=l > 0)
    dh, dhb, dnm = res if l > 0 else (res[0], None, res[1])
    grads = dict(
        norm_mix=dnm.sum(0), w_in=dw_in, conv_w=dconv_w, conv_b=dconv_b[0], dt_bias=dbi.sum(0)[:SSM_HEADS], a_log=dal.sum(0)[:SSM_HEADS],
        d_skip=dds.sum(0).reshape(SSM_HEADS, HEAD_DIM).sum(1), ssm_norm=dnw.sum(0), w_out=dw_out, norm_ffn=dnf.sum(0),
        w_gate=dw_gate, w_up=dw_up, w_down=dw_down)
    return (dh, dhb), grads


def _local_step(x, positions, target, p, bigs):
    tabs = _rope_tables(positions.reshape(-1, 1), "rope_tables")
    h = x
    saved, sps = [], []
    for l in range(DEPTH):
        sps.append(_layer_params(p, l))
        h, s = _layer_fwd(h, bigs[l][0], bigs[l][1:], sps[l], tabs, l)
        saved.append(s)
    dh, dhb, loss_parts, dfn = _final_loss(h, _row(p["final_norm"]), target, "final_loss")
    dh = (dh, dhb)
    layer_grads = [None] * DEPTH
    for l in reversed(range(DEPTH)):
        dh, layer_grads[l] = _layer_bwd(dh, saved[l], bigs[l], sps[l], tabs, l)
    grads = {k: [layer_grads[l][k] for l in range(DEPTH)] for k in layer_grads[0]}
    grads["final_norm"] = dfn.sum(0)
    return jnp.sum(loss_parts), dh[0], grads


BIG = ("w_in", "w_out", "w_gate", "w_up", "w_down")
REST = BIG[1:]
FFN = ("w_gate", "w_up", "w_down")
MIX = ("w_in", "w_out")
COL_SHARDED = ("w_in",)
TRANSPOSED = ("w_gate", "w_up")
SMALL = ("norm_mix", "conv_b", "dt_bias", "a_log", "d_skip", "ssm_norm", "norm_ffn", "final_norm")
WEIGHTS = ("norm_mix", "w_in", "conv_w", "conv_b", "dt_bias", "a_log", "d_skip", "ssm_norm", "w_out", "norm_ffn", "w_gate", "w_up", "w_down", "final_norm")
SMALL_ROWS = 88
CONVW_ROWS = 96
CONVW_SHARD_ROWS = 16


def _full_from_gathered(name, g, l):
    _, a, b = g.shape
    if name in COL_SHARDED:
        width = IN_PROJ_PAD if name == "w_in" else N_DEV * b
        return _cols_from_devices(g.reshape(N_DEV, 1, a, b), width, f"cols_l{l}_{name}").reshape(a, width)
    return g.reshape(N_DEV * a, b)


def _by_device(name, full, shard_shape, l):
    a, b = shard_shape
    if name in COL_SHARDED:
        return _devices_from_cols([full], b, f"devs_l{l}_{name}").reshape(N_CHIP, 2, a, b)
    return full.reshape(N_CHIP, 2, a, b)


def _pack_rows(parts, rows, width):
    flat = jnp.concatenate([q.reshape(-1) for q in parts])
    return jnp.pad(flat, (0, rows * width - flat.shape[0])).reshape(rows, width)


def _unpack(flat, like):
    out, off = [], 0
    for q in like:
        out.append(flat[off:off + q.size].reshape(q.shape))
        off += q.size
    return out


def kernel(x, positions, norm_mix, w_in, conv_w, conv_b, dt_bias, a_log, d_skip, ssm_norm, w_out, norm_ffn, w_gate, w_up, w_down, final_norm, loss_target, m_norm_mix, m_w_in, m_conv_w, m_conv_b, m_dt_bias, m_a_log, m_d_skip, m_ssm_norm, m_w_out, m_norm_ffn, m_w_gate, m_w_up, m_w_down, m_final_norm, v_norm_mix, v_w_in, v_conv_w, v_conv_b, v_dt_bias, v_a_log, v_d_skip, v_ssm_norm, v_w_out, v_norm_ffn, v_w_gate, v_w_up, v_w_down, v_final_norm):
    w = dict(norm_mix=norm_mix, w_in=w_in, conv_w=conv_w, conv_b=conv_b, dt_bias=dt_bias, a_log=a_log, d_skip=d_skip, ssm_norm=ssm_norm,
             w_out=w_out, norm_ffn=norm_ffn, w_gate=w_gate, w_up=w_up, w_down=w_down, final_norm=final_norm)
    m = dict(norm_mix=m_norm_mix, w_in=m_w_in, conv_w=m_conv_w, conv_b=m_conv_b, dt_bias=m_dt_bias, a_log=m_a_log, d_skip=m_d_skip,
             ssm_norm=m_ssm_norm, w_out=m_w_out, norm_ffn=m_norm_ffn, w_gate=m_w_gate, w_up=m_w_up, w_down=m_w_down, final_norm=m_final_norm)
    v = dict(norm_mix=v_norm_mix, w_in=v_w_in, conv_w=v_conv_w, conv_b=v_conv_b, dt_bias=v_dt_bias, a_log=v_a_log, d_skip=v_d_skip,
             ssm_norm=v_ssm_norm, w_out=v_w_out, norm_ffn=v_norm_ffn, w_gate=v_w_gate, w_up=v_w_up, w_down=v_w_down, final_norm=v_final_norm)
    ax, ay, ac = lax.axis_index("x"), lax.axis_index("y"), lax.axis_index("c")
    dev = 4 * ax + 2 * ay + ac

    assert DEPTH == 2
    t = x.shape[0] * x.shape[1]
    xf, target = x.reshape(t, D_MODEL), loss_target.reshape(t, D_MODEL)

    def own_slot(block):
        return lax.dynamic_update_slice(lax.empty((N_DEV,) + block.shape[1:], block.dtype), block, (dev,) + (0,) * (block.ndim - 1))

    def layer_shard(arr, k, l):
        return jnp.transpose(arr, (2, 0, 1))[:, l, :] if k in TRANSPOSED else arr[l]

    def gather_start(keys, l, tie, name):
        shards = [(layer_shard(w[keys[0]], keys[0], l) + tie).astype(bf16)] + [layer_shard(w[k], k, l).astype(bf16) for k in keys[1:]]
        return _direct_start(shards, [own_slot(s[None]) for s in shards], False, name)

    def scatter_start(keys, grads_l, l, name):
        shapes = [(w[k].shape[2], w[k].shape[1]) if k in TRANSPOSED else w[k].shape[1:] for k in keys]
        by_dev = [_by_device(k, grads_l[k], sh, l).reshape((N_DEV,) + sh) for k, sh in zip(keys, shapes)]
        return _direct_start(by_dev, [own_slot(lax.dynamic_slice_in_dim(g, dev, 1, 0)) for g in by_dev], True, name)

    (g_in0, conv_all), tie = _allgather_two_level([w["w_in"][0].astype(bf16), w["conv_w"]], "gather_l0_w_in")
    rest0_copy = gather_start(REST, 0, tie[0, 0], "gather_l0_rest_start")
    l1_copy = gather_start(BIG, 1, rest0_copy[4][0, 0], "gather_l1_start")
    p = {k: w[k] for k in SMALL}
    p["norm_mix"] = p["norm_mix"] + l1_copy[4][0, 0]
    p["conv_w"] = jnp.transpose(conv_all, (1, 2, 0, 3)).reshape(DEPTH, CONV_WIDTH, CONV_CH)
    sp0, sp1 = _layer_params(p, 0), _layer_params(p, 1)

    def rest0(after):
        lands = _direct_wait(*rest0_copy[:4], after, False, "gather_l0_rest_wait")
        return tuple(_full_from_gathered(k, g, 0) for k, g in zip(REST, lands))

    tabs = _rope_tables(positions.reshape(t, 1), "rope_tables")
    w_in0 = _full_from_gathered("w_in", g_in0, 0)
    h1, saved0 = _layer_fwd(xf, w_in0, rest0, sp0, tabs, 0)
    lands1 = _direct_wait(*l1_copy[:4], h1, False, "gather_l1_wait")
    bigs1 = tuple(_full_from_gathered(k, g, 1) for k, g in zip(BIG, lands1))
    h2, saved1 = _layer_fwd(h1, bigs1[0], bigs1[1:], sp1, tabs, 1)
    dh, dhb, loss_parts, dfn = _final_loss(h2, _row(p["final_norm"]), target, "final_loss")
    loss_local = jnp.sum(loss_parts)

    dh, grads1 = _layer_bwd((dh, dhb), saved1, bigs1, sp1, tabs, 1, gd=bf16)
    l1_grads = scatter_start(BIG, grads1, 1, "scatter_l1_start")
    w_out0, w_gate0, w_up0, w_down0 = saved0["rest"]
    bigs0 = (w_in0, w_out0, w_gate0, w_up0, w_down0 + l1_grads[4][0, 0].astype(bf16))
    ffn0_grads = []

    def after_ffn(grads_ffn):
        ffn0_grads.append(scatter_start(FFN, grads_ffn, 0, "scatter_l0_ffn_start"))
        return ffn0_grads[0][4][0, 0]

    (dx, _), grads0 = _layer_bwd(dh, saved0, bigs0, sp0, tabs, 0, gd=bf16, after_ffn=after_ffn)
    mix0_grads = scatter_start(MIX, grads0, 0, "scatter_l0_mix_start")
    landed = {(k, 1): g for k, g in zip(BIG, _direct_wait(*l1_grads[:4], dx, True, "scatter_l1_wait"))}
    landed.update({(k, 0): g for k, g in zip(FFN, _direct_wait(*ffn0_grads[0][:4], dx, True, "scatter_l0_ffn_wait"))})
    out_g, out_d, out_m, out_v = {}, {}, {}, {}

    def update(keys, after):
        for k in keys:
            parts = [landed[k, l] for l in range(DEPTH)]
            if k in TRANSPOSED:
                depth, a, b = w[k].shape
                state = [jnp.transpose(s, (2, 0, 1)).reshape(b, depth * a) for s in (w[k], m[k], v[k])]
                res = _adamw_layers(parts, *state, after, "adamw_" + k, layers_on_columns=True)
                res = [jnp.transpose(r.reshape(b, depth, a), (1, 2, 0)) for r in res]
            else:
                res = _adamw_layers(parts, w[k], m[k], v[k], after, "adamw_" + k)
            for dst, r in zip((out_g, out_d, out_m, out_v), res):
                dst[k] = r

    update(FFN, mix0_grads[4])
    grads = {k: [grads0[k], grads1[k]] for k in grads0 if k not in BIG}
    grads["final_norm"] = dfn.sum(0) + mix0_grads[4][0, 0]

    small_like = [w[k] for k in SMALL]
    small_grads = [jnp.stack(grads[k]) if k != "final_norm" else grads[k] for k in SMALL]
    small_pack = jnp.concatenate([_pack_rows(small_grads, SMALL_ROWS, LANES), _pack_rows([jnp.stack(grads["conv_w"])], CONVW_ROWS, LANES)], axis=0)
    parts = _allgather_direct(small_pack, "gather_small_grads")
    g_s, d_s, m_s, v_s = _adamw(
        [parts[i, :SMALL_ROWS] for i in range(N_DEV)], _pack_rows(small_like, SMALL_ROWS, LANES),
        _pack_rows([m[k] for k in SMALL], SMALL_ROWS, LANES), _pack_rows([v[k] for k in SMALL], SMALL_ROWS, LANES), "adamw_replicated")
    for dst, src in ((out_g, g_s), (out_d, d_s), (out_m, m_s), (out_v, v_s)):
        dst.update(zip(SMALL, _unpack(src.reshape(-1), small_like)))
    shard_w = conv_w.shape[-1]
    conv_parts = parts[:, SMALL_ROWS:].reshape(N_DEV, DEPTH, CONV_WIDTH, CONV_CH)
    conv_mine = lax.dynamic_slice_in_dim(conv_parts, dev * shard_w, shard_w, axis=3)
    g_c, d_c, m_c, v_c = _adamw(
        [_pack_rows([conv_mine[i]], CONVW_SHARD_ROWS, LANES) for i in range(N_DEV)], _pack_rows([conv_w], CONVW_SHARD_ROWS, LANES),
        _pack_rows([m["conv_w"]], CONVW_SHARD_ROWS, LANES), _pack_rows([v["conv_w"]], CONVW_SHARD_ROWS, LANES), "adamw_conv_w")
    for dst, src in ((out_g, g_c), (out_d, d_c), (out_m, m_c), (out_v, v_c)):
        dst["conv_w"] = src.reshape(-1)[:conv_w.size].reshape(conv_w.shape)

    landed.update({(k, 0): g for k, g in zip(MIX, _direct_wait(*mix0_grads[:4], v_c + out_v["w_down"][0, :CONVW_SHARD_ROWS, :LANES], True, "scatter_l0_mix_wait"))})
    update(MIX, v_c)

    loss = lax.psum(loss_local, ("x", "y", "c"))
    return (loss, dx.reshape(x.shape), *[out_g[k] for k in WEIGHTS], *[out_d[k] for k in WEIGHTS],
            *[out_m[k] for k in WEIGHTS], *[out_v[k] for k in WEIGHTS])
```

```python
import jax
import jax.numpy as jnp
import numpy as np
from jax import lax
from jax.experimental import pallas as pl
from jax.experimental.pallas import tpu as pltpu

f32 = jnp.float32
bf16 = jnp.bfloat16

D_MODEL = 1024
SEQ = 2048
DEPTH = 2
HEAD_DIM = 64
N_ATTN_HEADS = 8
N_KV_HEADS = 2
ATTN_WIDTH = 512
KV_WIDTH = 128
ROPE_DIM = 16
ROPE_THETA = 500000.0
DILATIONS = (1, 4, 16)
ATTN_BLOCK = 128
SSM_HEADS = 16
SSM_INNER = 1024
SSM_GROUPS = 2
D_STATE = 128
CONV_WIDTH = 4
CHUNK = 128
CONV_CH = 1536
MIX_WIDTH = 1536
QKV_WIDTH = ATTN_WIDTH + 2 * KV_WIDTH
DT_OFF = 3328
IN_PROJ = 3344
IN_PROJ_PAD = 3456
FFN_HIDDEN = 2816
EPS = 1e-5
N_DEV = 8
ADAM_LR = 0.001
ADAM_B1 = 0.9
ADAM_B2 = 0.999
ADAM_EPS = 1e-08
ADAM_WD = 0.01
ADAM_STEP = 10

LANES = 128
SUBLANES = 8
VMEM_LIMIT = 56 * 1024 * 1024
VMEM_LIMIT_TWO_PAIRS = 60 * 1024 * 1024

MESH = pl.DeviceIdType.MESH
ANY = pl.BlockSpec(memory_space=pl.ANY)


def _cparams(sem, vmem=None):
    return pltpu.CompilerParams(dimension_semantics=sem, vmem_limit_bytes=vmem or VMEM_LIMIT)


def _sigmoid(x):
    return 1.0 / (1.0 + jnp.exp(-x))


def _silu(x):
    return x * _sigmoid(x)


def _dsilu(x):
    s = _sigmoid(x)
    return s * (1.0 + x * (1.0 - s))


def _silu_and_grad(x):
    s = _sigmoid(x)
    return x * s, s * (1.0 + x * (1.0 - s))


def _softplus(x):
    return jnp.maximum(x, 0.0) + jnp.log(1.0 + jnp.exp(-jnp.abs(x)))


def _dot(a, b, dims, precision=None):
    return lax.dot_general(a, b, (dims, ((), ())), preferred_element_type=f32, precision=precision)


def _nn(a, b, precision=None):
    return _dot(a, b, ((1,), (0,)), precision)


def _nt(a, b):
    return _dot(a, b, ((1,), (1,)))


def _tn(a, b):
    return _dot(a, b, ((0,), (0,)))


def _rowsum8(t):
    n, w = t.shape
    return jnp.sum(t.reshape(n // SUBLANES, SUBLANES, w), axis=0)


def _matmul(a, b, *, mode, n_out=None, b_off=0, add=None, out_dtype=f32, tm=2048, tn=512, tk=1024, name):
    if mode == "tn":
        kk, m = a.shape
    else:
        m, kk = a.shape
    n = n_out if n_out is not None else (b.shape[0] if mode == "nt" else b.shape[1])
    tm, tn, tk = min(tm, m), min(tn, n), min(tk, kk)
    assert m % tm == 0 and n % tn == 0 and kk % tk == 0, (name, m, n, kk, tm, tn, tk)
    nk = kk // tk
    if mode == "nn":
        a_spec = pl.BlockSpec((tm, tk), lambda i, j, k: (i, k))
        b_spec = pl.BlockSpec((tk, tn), lambda i, j, k: (k, j + b_off))
        dims = ((1,), (0,))
    elif mode == "nt":
        a_spec = pl.BlockSpec((tm, tk), lambda i, j, k: (i, k))
        b_spec = pl.BlockSpec((tn, tk), lambda i, j, k: (j + b_off, k))
        dims = ((1,), (1,))
    else:
        a_spec = pl.BlockSpec((tk, tm), lambda i, j, k: (k, i))
        b_spec = pl.BlockSpec((tk, tn), lambda i, j, k: (k, j + b_off))
        dims = ((0,), (0,))
    o_spec = pl.BlockSpec((tm, tn), lambda i, j, k: (i, j))
    has_add = add is not None

    def body(*refs):
        if has_add:
            a_ref, b_ref, add_ref, o_ref, acc_ref = refs
        else:
            a_ref, b_ref, o_ref, acc_ref = refs
        k = pl.program_id(2)
        part = _dot(a_ref[...].astype(bf16), b_ref[...].astype(bf16), dims)

        @pl.when(k == 0)
        def _():
            acc_ref[...] = part

        @pl.when(k > 0)
        def _():
            acc_ref[...] += part

        @pl.when(k == nk - 1)
        def _():
            r = acc_ref[...]
            if has_add:
                r = r + add_ref[...]
            o_ref[...] = r.astype(out_dtype)

    in_specs = [a_spec, b_spec] + ([o_spec] if has_add else [])
    args = (a, b) + ((add,) if has_add else ())
    return pl.pallas_call(
        body, name=name, grid=(m // tm, n // tn, nk), in_specs=in_specs, out_specs=o_spec,
        out_shape=jax.ShapeDtypeStruct((m, n), out_dtype), scratch_shapes=[pltpu.VMEM((tm, tn), f32)],
        compiler_params=_cparams(("parallel", "parallel", "arbitrary")),
    )(*args)


def _in_proj(hn, w_in, widths, name, tm=2048, tn=256):
    m, k = hn.shape
    starts = [sum(widths[:i]) // tn for i in range(len(widths))]
    counts = [wd // tn for wd in widths]
    assert m % tm == 0 and all(wd % tn == 0 for wd in widths)
    n_out = len(widths)

    def body(a_ref, w_ref, *o_refs):
        j = pl.program_id(1)
        acc = _nn(a_ref[...], w_ref[...])
        for s, c, o_ref in zip(starts, counts, o_refs):
            @pl.when((j >= s) & (j < s + c))
            def _(o_ref=o_ref):
                o_ref[...] = acc

    def o_spec(s, c):
        return pl.BlockSpec((tm, tn), lambda i, j: (i, jnp.clip(j - s, 0, c - 1)))

    return pl.pallas_call(
        body, name=name, grid=(m // tm, sum(counts)),
        in_specs=[pl.BlockSpec((tm, k), lambda i, j: (i, 0)), pl.BlockSpec((k, tn), lambda i, j: (0, j))],
        out_specs=[o_spec(s, c) for s, c in zip(starts, counts)],
        out_shape=[jax.ShapeDtypeStruct((m, wd), f32) for wd in widths], compiler_params=_cparams(("parallel", "arbitrary")),
    )(hn, w_in)


def _out_proj(o, yn, w_out, h, name, tm=2048, tn=512):
    m, kb = o.shape
    n = w_out.shape[1]
    n_y = yn.shape[1] // kb
    assert yn.shape[1] % kb == 0 and w_out.shape[0] == kb * (1 + n_y) and m % tm == 0 and n % tn == 0

    def body(*refs):
        o_ref, y_refs, w_refs, h_ref, out_ref = refs[0], refs[1:1 + n_y], refs[1 + n_y:2 + 2 * n_y], refs[-2], refs[-1]
        acc = h_ref[...] + _nn(o_ref[...].astype(bf16), w_refs[0][...])
        for y_ref, w_ref in zip(y_refs, w_refs[1:]):
            acc = acc + _nn(y_ref[...], w_ref[...])
        out_ref[...] = acc

    res = pl.BlockSpec((tm, tn), lambda i, j: (i, j))

    def a_blk(c):
        return pl.BlockSpec((tm, kb), lambda i, j: (i, c))

    def w_blk(r):
        return pl.BlockSpec((kb, tn), lambda i, j: (r, j))

    return pl.pallas_call(
        body, name=name, grid=(m // tm, n // tn),
        in_specs=[a_blk(0)] + [a_blk(c) for c in range(n_y)] + [w_blk(r) for r in range(1 + n_y)] + [res],
        out_specs=res, out_shape=jax.ShapeDtypeStruct((m, n), f32), compiler_params=_cparams(("parallel", "parallel")),
    )(o, *[yn] * n_y, *[w_out] * (1 + n_y), h)


def _swiglu_fwd(hn, w_gate, w_up, name, tm=2048, tn=256):
    m, k = hn.shape
    n = w_gate.shape[0]
    assert m % tm == 0 and n % tn == 0, (name, m, n, tm, tn)

    def body(a_ref, wg_ref, wu_ref, g_ref, u_ref, act_ref):
        a = a_ref[...]
        g = _nt(a, wg_ref[...])
        u = _nt(a, wu_ref[...])
        sg, dsg = _silu_and_grad(g)
        g_ref[...] = (u * dsg).astype(bf16)
        u_ref[...] = sg.astype(bf16)
        act_ref[...] = (sg * u).astype(bf16)

    a_spec = pl.BlockSpec((tm, k), lambda i, j: (i, 0))
    w_spec = pl.BlockSpec((tn, k), lambda i, j: (j, 0))
    o_spec = pl.BlockSpec((tm, tn), lambda i, j: (i, j))
    return pl.pallas_call(
        body, name=name, grid=(m // tm, n // tn), in_specs=[a_spec, w_spec, w_spec], out_specs=[o_spec, o_spec, o_spec],
        out_shape=[jax.ShapeDtypeStruct((m, n), bf16)] * 3,
        compiler_params=_cparams(("parallel", "parallel")),
    )(hn, w_gate, w_up)


def _swiglu_bwd(dh, w_down, g, u, name, tm=2048, tn=256):
    m, k = dh.shape
    n = w_down.shape[0]
    assert m % tm == 0 and n % tn == 0, (name, m, n, tm, tn)

    def body(a_ref, w_ref, g_ref, u_ref, dg_ref, du_ref):
        dact = _nt(a_ref[...].astype(bf16), w_ref[...])
        dg_ref[...] = (dact * g_ref[...].astype(f32)).astype(bf16)
        du_ref[...] = (dact * u_ref[...].astype(f32)).astype(bf16)

    a_spec = pl.BlockSpec((tm, k), lambda i, j: (i, 0))
    w_spec = pl.BlockSpec((tn, k), lambda i, j: (j, 0))
    o_spec = pl.BlockSpec((tm, tn), lambda i, j: (i, j))
    return pl.pallas_call(
        body, name=name, grid=(m // tm, n // tn), in_specs=[a_spec, w_spec, o_spec, o_spec], out_specs=[o_spec, o_spec],
        out_shape=[jax.ShapeDtypeStruct((m, n), bf16), jax.ShapeDtypeStruct((m, n), bf16)],
        compiler_params=_cparams(("parallel", "parallel")),
    )(dh, w_down, g, u)


def _rmsnorm_fwd(h, w, name, tm=512):
    m, d = h.shape

    def body(h_ref, w_ref, o_ref):
        x = h_ref[...]
        r = lax.rsqrt(jnp.mean(x * x, axis=-1, keepdims=True) + EPS)
        o_ref[...] = (x * r * w_ref[...]).astype(bf16)

    return pl.pallas_call(
        body, name=name, grid=(m // tm,),
        in_specs=[pl.BlockSpec((tm, d), lambda i: (i, 0)), pl.BlockSpec((1, d), lambda i: (0, 0))],
        out_specs=pl.BlockSpec((tm, d), lambda i: (i, 0)), out_shape=jax.ShapeDtypeStruct((m, d), bf16),
        compiler_params=_cparams(("parallel",)),
    )(h, w)


def _nt_norm_bwd(pairs, h, w, dres, name, tk, b_is_kd=False, bf16_copy=True, tm=1024, vmem=None):
    m, d = h.shape
    contract = _nn if b_is_kd else _nt
    steps = [p[0].shape[1] // tk for p in pairs]
    assert all(p[0].shape[1] % tk == 0 for p in pairs), (name, tk)
    starts = [sum(steps[:i]) for i in range(len(pairs))]
    nk = sum(steps)
    n_p = len(pairs)

    def body(*refs):
        ab = refs[:2 * n_p]
        h_ref, w_ref, dres_ref, dh_ref = refs[2 * n_p:2 * n_p + 4]
        dhb_ref = refs[2 * n_p + 4] if bf16_copy else None
        dw_ref, acc_ref = refs[-2:]
        i, k = pl.program_id(0), pl.program_id(1)

        @pl.when(k == 0)
        def _():
            acc_ref[...] = jnp.zeros_like(acc_ref)

        for p in range(n_p):
            @pl.when((k >= starts[p]) & (k < starts[p] + steps[p]))
            def _(p=p):
                acc_ref[...] += contract(ab[2 * p][...], ab[2 * p + 1][...])

        @pl.when(k == nk - 1)
        def _():
            x = h_ref[...]
            r = lax.rsqrt(jnp.mean(x * x, axis=-1, keepdims=True) + EPS)
            xhat = x * r
            dy = acc_ref[...]
            gw = dy * w_ref[...]
            dh = dres_ref[...] + r * (gw - xhat * jnp.mean(gw * xhat, axis=-1, keepdims=True))
            dh_ref[...] = dh
            if bf16_copy:
                dhb_ref[...] = dh.astype(bf16)
            part = _rowsum8(dy * xhat)

            @pl.when(i == 0)
            def _():
                dw_ref[...] = part

            @pl.when(i > 0)
            def _():
                dw_ref[...] += part

    def clamp(k, p):
        return jnp.clip(k - starts[p], 0, steps[p] - 1)

    in_specs = []
    for p in range(n_p):
        b_spec = (pl.BlockSpec((tk, d), lambda i, k, p=p: (clamp(k, p), 0)) if b_is_kd
                  else pl.BlockSpec((d, tk), lambda i, k, p=p: (0, clamp(k, p))))
        in_specs += [pl.BlockSpec((tm, tk), lambda i, k, p=p: (i, clamp(k, p))), b_spec]
    row = pl.BlockSpec((tm, d), lambda i, k: (i, 0))
    in_specs += [row, pl.BlockSpec((1, d), lambda i, k: (0, 0)), row]
    return pl.pallas_call(
        body, name=name, grid=(m // tm, nk), in_specs=in_specs,
        out_specs=[row] + [row] * bf16_copy + [pl.BlockSpec((SUBLANES, d), lambda i, k: (0, 0))],
        out_shape=[jax.ShapeDtypeStruct((m, d), f32)] + [jax.ShapeDtypeStruct((m, d), bf16)] * bf16_copy + [jax.ShapeDtypeStruct((SUBLANES, d), f32)],
        scratch_shapes=[pltpu.VMEM((tm, d), f32)], compiler_params=_cparams(("arbitrary", "arbitrary"), vmem),
    )(*[t for p in pairs for t in p], h, w, dres)


def _final_loss(h, w, target, name, tm=512):
    m, d = h.shape

    def body(h_ref, w_ref, t_ref, dh_ref, dhb_ref, loss_ref, dw_ref):
        x = h_ref[...]
        r = lax.rsqrt(jnp.mean(x * x, axis=-1, keepdims=True) + EPS)
        xhat = x * r
        ww = w_ref[...]
        err = xhat * ww - t_ref[...]
        dy = err * (1.0 / d)
        gw = dy * ww
        dh = r * (gw - xhat * jnp.mean(gw * xhat, axis=-1, keepdims=True))
        dh_ref[...] = dh
        dhb_ref[...] = dh.astype(bf16)
        lpart = _rowsum8(err * err) * (0.5 / d)
        wpart = _rowsum8(dy * xhat)

        @pl.when(pl.program_id(0) == 0)
        def _():
            loss_ref[...] = lpart
            dw_ref[...] = wpart

        @pl.when(pl.program_id(0) > 0)
        def _():
            loss_ref[...] += lpart
            dw_ref[...] += wpart

    row = pl.BlockSpec((tm, d), lambda i: (i, 0))
    acc = pl.BlockSpec((SUBLANES, d), lambda i: (0, 0))
    return pl.pallas_call(
        body, name=name, grid=(m // tm,),
        in_specs=[row, pl.BlockSpec((1, d), lambda i: (0, 0)), row], out_specs=[row, row, acc, acc],
        out_shape=[jax.ShapeDtypeStruct((m, d), f32), jax.ShapeDtypeStruct((m, d), bf16),
                   jax.ShapeDtypeStruct((SUBLANES, d), f32), jax.ShapeDtypeStruct((SUBLANES, d), f32)],
        compiler_params=_cparams(("arbitrary",)),
    )(h, w, target)


def _lane_tables():
    f = np.arange(LANES) % HEAD_DIM
    inv = ROPE_THETA ** (-jnp.arange(0, ROPE_DIM, 2, dtype=f32) / ROPE_DIM)
    invf = jnp.where(f < ROPE_DIM, inv[f % (ROPE_DIM // 2)], 0.0).astype(f32)
    return invf.reshape(1, LANES)


def _rope_tables(pos_col, name):
    t = pos_col.shape[0]
    tm = SEQ

    def body(p_ref, f_ref, c_ref, s1_ref, s2_ref):
        ang = p_ref[...].astype(f32) * f_ref[...]
        co, si = jnp.cos(ang), jnp.sin(ang)
        f = lax.broadcasted_iota(jnp.int32, (tm, LANES), 1) % HEAD_DIM
        c_ref[...] = jnp.where(f < ROPE_DIM, co, 1.0)
        s1_ref[...] = jnp.where(f < ROPE_DIM // 2, -si, 0.0)
        s2_ref[...] = jnp.where((f >= ROPE_DIM // 2) & (f < ROPE_DIM), si, 0.0)

    row = pl.BlockSpec((tm, LANES), lambda i: (i, 0))
    return pl.pallas_call(
        body, name=name, grid=(t // tm,),
        in_specs=[pl.BlockSpec((tm, 1), lambda i: (i, 0)), pl.BlockSpec((1, LANES), lambda i: (0, 0))],
        out_specs=[row, row, row], out_shape=[jax.ShapeDtypeStruct((t, LANES), f32)] * 3,
        compiler_params=_cparams(("parallel",)),
    )(pos_col, _lane_tables())


def _rot(x, c, s1, s2):
    return x * c + pltpu.roll(x, LANES - ROPE_DIM // 2, 1) * s1 + pltpu.roll(x, ROPE_DIM // 2, 1) * s2


def _rot_t(g, c, s1, s2):
    return g * c + pltpu.roll(g * s1, ROPE_DIM // 2, 1) + pltpu.roll(g * s2, LANES - ROPE_DIM // 2, 1)


def _dup_head(x, kvh, low):
    a = jnp.where(kvh == 0, x, pltpu.roll(x, HEAD_DIM, 1))
    return jnp.where(low, a, pltpu.roll(a, HEAD_DIM, 1))


def _deinterleave(src_ref, dst_ref, d, dtype):
    length = SEQ // d
    if d == 1:
        dst_ref[...] = src_ref[...].astype(dtype)
    else:
        for r in range(d):
            dst_ref[pl.ds(r * length, length), :] = src_ref[pl.ds(r, length, stride=d), :].astype(dtype)


def _interleave_store(src_ref, dst_ref, d, accumulate):
    length = SEQ // d
    if d == 1:
        if accumulate:
            dst_ref[...] += src_ref[...]
        else:
            dst_ref[...] = src_ref[...]
    else:
        for r in range(d):
            blk = src_ref[pl.ds(r * length, length), :]
            if accumulate:
                dst_ref[pl.ds(r, length, stride=d), :] = dst_ref[pl.ds(r, length, stride=d), :] + blk
            else:
                dst_ref[pl.ds(r, length, stride=d), :] = blk


def _attn_masks():
    qi = lax.broadcasted_iota(jnp.int32, (ATTN_BLOCK, ATTN_BLOCK), 0)
    ki = lax.broadcasted_iota(jnp.int32, (ATTN_BLOCK, ATTN_BLOCK), 1)
    low = lax.broadcasted_iota(jnp.int32, (ATTN_BLOCK, LANES), 1) < HEAD_DIM
    return ki <= qi, ki >= qi, low


NEG_INF = float("-inf")


N_BRANCH = len(DILATIONS)


def _attn_prep(qkv, tabs, name):
    t = qkv.shape[0]
    nb = t // SEQ
    n_j = ATTN_WIDTH // LANES

    def q_body(q_ref, c_ref, s1_ref, s2_ref, out_ref, xr):
        xr[...] = _rot(q_ref[...], c_ref[...], s1_ref[...], s2_ref[...]) * (HEAD_DIM ** -0.5)
        for bi, d in enumerate(DILATIONS):
            _deinterleave(xr, out_ref.at[bi], d, bf16)

    def kv_body(x_ref, c_ref, s1_ref, s2_ref, out_ref, xr):
        lowfull = lax.broadcasted_iota(jnp.int32, (SEQ, LANES), 1) < HEAD_DIM
        x = x_ref[...]
        x = jnp.where(pl.program_id(1) == 0, _rot(x, c_ref[...], s1_ref[...], s2_ref[...]), x)
        for kvh in range(N_KV_HEADS):
            xr[...] = _dup_head(x, kvh, lowfull)
            for bi, d in enumerate(DILATIONS):
                length = SEQ // d
                for r in range(d):
                    rows = xr[...] if d == 1 else xr[pl.ds(r, length, stride=d), :]
                    out_ref[0, bi, pl.ds(r * length, length), kvh * LANES:(kvh + 1) * LANES] = rows.astype(bf16)

    tab = pl.BlockSpec((SEQ, LANES), lambda b, j: (b, 0))
    q = pl.pallas_call(
        q_body, name=name + "_q", grid=(nb, n_j),
        in_specs=[pl.BlockSpec((SEQ, LANES), lambda b, j: (b, j)), tab, tab, tab],
        out_specs=pl.BlockSpec((N_BRANCH, SEQ, LANES), lambda b, j: (0, b, j)),
        out_shape=jax.ShapeDtypeStruct((N_BRANCH, t, ATTN_WIDTH), bf16), scratch_shapes=[pltpu.VMEM((SEQ, LANES), f32)],
        compiler_params=_cparams(("parallel", "parallel")),
    )(qkv, *tabs)
    kv = pl.pallas_call(
        kv_body, name=name + "_kv", grid=(nb, 2),
        in_specs=[pl.BlockSpec((SEQ, LANES), lambda b, j: (b, n_j + j)), tab, tab, tab],
        out_specs=pl.BlockSpec((1, N_BRANCH, SEQ, N_KV_HEADS * LANES), lambda b, j: (j, 0, b, 0)),
        out_shape=jax.ShapeDtypeStruct((2, N_BRANCH, t, N_KV_HEADS * LANES), bf16), scratch_shapes=[pltpu.VMEM((SEQ, LANES), f32)],
        compiler_params=_cparams(("parallel", "parallel")),
    )(qkv, *tabs)
    return q, kv


def _attn_fwd(prep, name):
    q_all, kv_all = prep
    t = q_all.shape[1]
    nb = t // SEQ
    n_blk = SEQ // ATTN_BLOCK

    def body(q_ref, k_ref, v_ref, o_ref, lse_ref, ob, lb, o0, o1, o2, l0, l1, l2, ss):
        cur_ok, prev_ok, low = _attn_masks()
        onat, lnat = (o0, o1, o2), (l0, l1, l2)
        for bi, d in enumerate(DILATIONS):
            qd, kd, vd = q_ref.at[bi], k_ref.at[0, bi], v_ref.at[0, bi]
            per_res = n_blk // d

            def scores(n):
                cur, prev = pl.ds(n * ATTN_BLOCK, ATTN_BLOCK), pl.ds(max(n - 1, 0) * ATTN_BLOCK, ATTN_BLOCK)
                has_prev = n % per_res != 0
                qb = qd[cur, :]
                kc = kd[cur, :]
                if has_prev:
                    kp = kd[prev, :]
                for a in range(2):
                    qa = jnp.where(low if a == 0 else ~low, qb, jnp.zeros_like(qb))
                    ss[2 * n + a, :, 0:ATTN_BLOCK] = jnp.where(cur_ok, _nt(qa, kc), NEG_INF)
                    if has_prev:
                        ss[2 * n + a, :, ATTN_BLOCK:2 * ATTN_BLOCK] = jnp.where(prev_ok, _nt(qa, kp), NEG_INF)

            def softmax_pv(n):
                cur, prev = pl.ds(n * ATTN_BLOCK, ATTN_BLOCK), pl.ds(max(n - 1, 0) * ATTN_BLOCK, ATTN_BLOCK)
                has_prev = n % per_res != 0
                vc = vd[cur, :]
                if has_prev:
                    vp = vd[prev, :]
                outs, lses = [], []
                for a in range(2):
                    sc = ss[2 * n + a, :, 0:ATTN_BLOCK]
                    if has_prev:
                        sp = ss[2 * n + a, :, ATTN_BLOCK:2 * ATTN_BLOCK]
                        m = jnp.max(jnp.maximum(sc, sp), axis=1, keepdims=True)
                        pc, pp = jnp.exp(sc - m), jnp.exp(sp - m)
                        den = jnp.sum(pc + pp, axis=1, keepdims=True)
                        acc = _nn(pc.astype(bf16), vc) + _nn(pp.astype(bf16), vp)
                    else:
                        m = jnp.max(sc, axis=1, keepdims=True)
                        pc = jnp.exp(sc - m)
                        den = jnp.sum(pc, axis=1, keepdims=True)
                        acc = _nn(pc.astype(bf16), vc)
                    outs.append(acc * (1.0 / den))
                    lses.append(m + jnp.log(den))
                ob[cur, :] = jnp.where(low, outs[0], outs[1])
                lb[cur, :] = jnp.where(low, lses[0], lses[1])

            for n in range(n_blk):
                scores(n)
            for n in range(n_blk):
                softmax_pv(n)
            _interleave_store(ob, onat[bi], d, False)
            _interleave_store(lb, lnat[bi], d, False)
        la, lbb, lc = l0[...], l1[...], l2[...]
        lm = jnp.maximum(jnp.maximum(la, lbb), lc)
        wa, wb, wc = jnp.exp(la - lm), jnp.exp(lbb - lm), jnp.exp(lc - lm)
        ws = wa + wb + wc
        o_ref[...] = (wa * o0[...] + wb * o1[...] + wc * o2[...]) / ws
        lse_ref[...] = lm + jnp.log(ws)

    def col(jj):
        return pl.BlockSpec((SEQ, LANES), lambda b, j: (b, jj if jj is not None else j))

    fs = pltpu.VMEM((SEQ, LANES), f32)
    return pl.pallas_call(
        body, name=name, grid=(nb, ATTN_WIDTH // LANES),
        in_specs=[pl.BlockSpec((N_BRANCH, SEQ, LANES), lambda b, j: (0, b, j)),
                  pl.BlockSpec((1, N_BRANCH, SEQ, LANES), lambda b, j: (0, 0, b, j // 2)),
                  pl.BlockSpec((1, N_BRANCH, SEQ, LANES), lambda b, j: (1, 0, b, j // 2))],
        out_specs=[col(None), col(None)],
        out_shape=[jax.ShapeDtypeStruct((t, ATTN_WIDTH), f32), jax.ShapeDtypeStruct((t, ATTN_WIDTH), f32)],
        scratch_shapes=[fs, fs, fs, fs, fs, fs, fs, fs, pltpu.VMEM((2 * n_blk, ATTN_BLOCK, 2 * ATTN_BLOCK), f32)],
        compiler_params=_cparams(("parallel", "parallel")),
    )(q_all, kv_all, kv_all)


def _attn_bwd(prep, tabs, o, lse, do, name):
    q_all, kv_all = prep
    t = q_all.shape[1]
    nb = t // SEQ
    n_blk = SEQ // ATTN_BLOCK
    n_j = ATTN_WIDTH // LANES

    def body(q_ref, k_ref, v_ref, c_ref, s1_ref, s2_ref, o_ref, lse_ref, do_ref, dq_ref, dk_ref, dv_ref,
             dl, dod, lsd, dld, dqd, dkd, dvd, dqa, dka, dva, pb, dsb, dk_acc, dv_acc):
        j = pl.program_id(1)
        kvh = j // 2
        cur_ok, prev_ok, low = _attn_masks()
        lowfull = lax.broadcasted_iota(jnp.int32, (SEQ, LANES), 1) < HEAD_DIM
        c, s1, s2 = c_ref[...], s1_ref[...], s2_ref[...]
        prod = do_ref[...] * o_ref[...]
        d_lo = jnp.sum(jnp.where(lowfull, prod, 0.0), axis=1, keepdims=True)
        d_hi = jnp.sum(jnp.where(lowfull, 0.0, prod), axis=1, keepdims=True)
        dl[...] = jnp.where(lowfull, d_lo, d_hi)
        dqa[...] = jnp.zeros_like(dqa)
        dka[...] = jnp.zeros_like(dka)
        dva[...] = jnp.zeros_like(dva)
        for bi, d in enumerate(DILATIONS):
            qd, kd, vd = q_ref.at[bi], k_ref.at[0, bi], v_ref.at[0, bi]
            _deinterleave(do_ref, dod, d, bf16)
            _deinterleave(lse_ref, lsd, d, f32)
            _deinterleave(dl, dld, d, f32)
            per_res = n_blk // d
            curl, prevl = slice(0, ATTN_BLOCK), slice(ATTN_BLOCK, 2 * ATTN_BLOCK)

            def halves(x):
                zero = jnp.zeros_like(x)
                return jnp.where(low, x, zero), jnp.where(low, zero, x)

            def blk(n):
                return pl.ds(n * ATTN_BLOCK, ATTN_BLOCK)

            def has_prev(n):
                return n < n_blk and n % per_res != 0

            def probs(n):
                cur = blk(n)
                qas, doas = halves(qd[cur, :]), halves(dod[cur, :])
                kc, vc = kd[cur, :], vd[cur, :]
                if has_prev(n):
                    kp, vp = kd[blk(n - 1), :], vd[blk(n - 1), :]
                lsb, dlb = lsd[cur, :], dld[cur, :]
                for a in range(2):
                    ls = lsb[:, a * HEAD_DIM:a * HEAD_DIM + 1]
                    de = dlb[:, a * HEAD_DIM:a * HEAD_DIM + 1]
                    pc = jnp.exp(jnp.where(cur_ok, _nt(qas[a], kc), NEG_INF) - ls)
                    pb[2 * n + a, :, curl] = pc.astype(bf16)
                    dsb[2 * n + a, :, curl] = (pc * (_nt(doas[a], vc) - de)).astype(bf16)
                    if has_prev(n):
                        pp = jnp.exp(jnp.where(prev_ok, _nt(qas[a], kp), NEG_INF) - ls)
                        pb[2 * n + a, :, prevl] = pp.astype(bf16)
                        dsb[2 * n + a, :, prevl] = (pp * (_nt(doas[a], vp) - de)).astype(bf16)

            def grads(n):
                cur = blk(n)
                kc = kd[cur, :]
                dqs = [_nn(dsb[2 * n + a, :, curl], kc) for a in range(2)]
                q_rows, do_rows = list(halves(qd[cur, :])), list(halves(dod[cur, :]))
                ds_rows, p_rows = [dsb[2 * n + a, :, curl] for a in range(2)], [pb[2 * n + a, :, curl] for a in range(2)]
                if has_prev(n):
                    kp = kd[blk(n - 1), :]
                    dqs = [dqs[a] + _nn(dsb[2 * n + a, :, prevl], kp) for a in range(2)]
                if has_prev(n + 1):
                    q_rows += list(halves(qd[blk(n + 1), :]))
                    do_rows += list(halves(dod[blk(n + 1), :]))
                    ds_rows += [dsb[2 * n + 2 + a, :, prevl] for a in range(2)]
                    p_rows += [pb[2 * n + 2 + a, :, prevl] for a in range(2)]
                dqd[cur, :] = jnp.where(low, dqs[0], dqs[1])
                dkd[cur, :] = _tn(jnp.concatenate(ds_rows, axis=0), jnp.concatenate(q_rows, axis=0))
                dvd[cur, :] = _tn(jnp.concatenate(p_rows, axis=0), jnp.concatenate(do_rows, axis=0))

            for n in range(n_blk):
                probs(n)
            for n in range(n_blk):
                grads(n)
            _interleave_store(dqd, dqa, d, True)
            _interleave_store(dkd, dka, d, True)
            _interleave_store(dvd, dva, d, True)
        dq_ref[...] = _rot_t(dqa[...] * (HEAD_DIM ** -0.5), c, s1, s2).astype(bf16)
        dkf = dka[...]
        dkf = _rot_t(dkf + pltpu.roll(dkf, HEAD_DIM, 1), c, s1, s2)
        dvf = dva[...]
        dvf = dvf + pltpu.roll(dvf, HEAD_DIM, 1)
        mine = (lax.broadcasted_iota(jnp.int32, (SEQ, LANES), 1) // HEAD_DIM) == kvh
        dkc_, dvc_ = jnp.where(mine, dkf, 0.0), jnp.where(mine, dvf, 0.0)

        @pl.when(j == 0)
        def _():
            dk_acc[...] = dkc_
            dv_acc[...] = dvc_

        @pl.when(j > 0)
        def _():
            dk_acc[...] += dkc_
            dv_acc[...] += dvc_

        @pl.when(j == n_j - 1)
        def _():
            dk_ref[...] = dk_acc[...].astype(bf16)
            dv_ref[...] = dv_acc[...].astype(bf16)

    def col(jj):
        return pl.BlockSpec((SEQ, LANES), lambda b, j: (b, jj if jj is not None else j))

    tab = pl.BlockSpec((SEQ, LANES), lambda b, j: (b, 0))
    fs = pltpu.VMEM((SEQ, LANES), f32)
    hs = pltpu.VMEM((SEQ, LANES), bf16)
    return pl.pallas_call(
        body, name=name, grid=(nb, n_j),
        in_specs=[pl.BlockSpec((N_BRANCH, SEQ, LANES), lambda b, j: (0, b, j)),
                  pl.BlockSpec((1, N_BRANCH, SEQ, LANES), lambda b, j: (0, 0, b, j // 2)),
                  pl.BlockSpec((1, N_BRANCH, SEQ, LANES), lambda b, j: (1, 0, b, j // 2)),
                  tab, tab, tab, col(None), col(None), col(None)],
        out_specs=[col(None), tab, tab],
        out_shape=[jax.ShapeDtypeStruct((t, ATTN_WIDTH), bf16), jax.ShapeDtypeStruct((t, LANES), bf16), jax.ShapeDtypeStruct((t, LANES), bf16)],
        scratch_shapes=[fs, hs, fs, fs, fs, fs, fs, fs, fs, fs,
                        pltpu.VMEM((2 * n_blk, ATTN_BLOCK, 2 * ATTN_BLOCK), bf16), pltpu.VMEM((2 * n_blk, ATTN_BLOCK, 2 * ATTN_BLOCK), bf16), fs, fs],
        compiler_params=_cparams(("parallel", "arbitrary")),
    )(q_all, kv_all, kv_all, *tabs, o, lse, do)


def _tap(w_ref, s):
    return w_ref[CONV_WIDTH - 1 - s:CONV_WIDTH - s, :]


def _conv_pre(x, w_ref, b_ref, row):
    shifted = [x] + [jnp.where(row >= s, pltpu.roll(x, s, 0), 0.0) for s in range(1, CONV_WIDTH)]
    pre = b_ref[...] + _tap(w_ref, 0) * x
    for s in range(1, CONV_WIDTH):
        pre = pre + _tap(w_ref, s) * shifted[s]
    return pre, shifted


def _conv_fwd(x, w, b, name, tc=512):
    t, ch = x.shape

    def body(x_ref, w_ref, b_ref, o_ref):
        row = lax.broadcasted_iota(jnp.int32, (SEQ, tc), 0)
        pre, _ = _conv_pre(x_ref[...], w_ref, b_ref, row)
        o_ref[...] = _silu(pre)

    xs = pl.BlockSpec((SEQ, tc), lambda i, j: (i, j))
    return pl.pallas_call(
        body, name=name, grid=(t // SEQ, ch // tc),
        in_specs=[xs, pl.BlockSpec((CONV_WIDTH, tc), lambda i, j: (0, j)), pl.BlockSpec((1, tc), lambda i, j: (0, j))],
        out_specs=xs, out_shape=jax.ShapeDtypeStruct((t, ch), f32),
        compiler_params=_cparams(("parallel", "parallel")),
    )(x, w, b)


def _conv_bwd(x, w, b, dact, name, tc=512):
    t, ch = x.shape

    def body(x_ref, w_ref, b_ref, d_ref, dx_ref, dw_ref, db_ref):
        row = lax.broadcasted_iota(jnp.int32, (SEQ, tc), 0)
        pre, shifted = _conv_pre(x_ref[...], w_ref, b_ref, row)
        dpre = d_ref[...] * _dsilu(pre)
        dx = _tap(w_ref, 0) * dpre
        for s in range(1, CONV_WIDTH):
            dx = dx + _tap(w_ref, s) * jnp.where(row < SEQ - s, pltpu.roll(dpre, SEQ - s, 0), 0.0)
        dx_ref[...] = dx.astype(bf16)
        first = pl.program_id(1) == 0
        parts = [jnp.sum(dpre * shifted[CONV_WIDTH - 1 - k], axis=0, keepdims=True) for k in range(CONV_WIDTH)]
        dbp = jnp.sum(dpre, axis=0, keepdims=True)

        @pl.when(first)
        def _():
            for k in range(CONV_WIDTH):
                dw_ref[k:k + 1, :] = parts[k]
            db_ref[...] = dbp

        @pl.when(jnp.logical_not(first))
        def _():
            for k in range(CONV_WIDTH):
                dw_ref[k:k + 1, :] += parts[k]
            db_ref[...] += dbp

    xs = pl.BlockSpec((SEQ, tc), lambda j, i: (i, j))
    ws = pl.BlockSpec((CONV_WIDTH, tc), lambda j, i: (0, j))
    bs = pl.BlockSpec((1, tc), lambda j, i: (0, j))
    return pl.pallas_call(
        body, name=name, grid=(ch // tc, t // SEQ),
        in_specs=[xs, ws, bs, xs], out_specs=[xs, ws, bs],
        out_shape=[jax.ShapeDtypeStruct((t, ch), bf16), jax.ShapeDtypeStruct((CONV_WIDTH, ch), f32), jax.ShapeDtypeStruct((1, ch), f32)],
        compiler_params=_cparams(("parallel", "arbitrary")),
    )(x, w, b, dact)


GROUP_W = SSM_INNER // SSM_GROUPS
HEADS_PER_GROUP = SSM_HEADS // SSM_GROUPS
SSD_CHUNKS_PER_STEP = 4


def _split3(x):
    hi = x.astype(bf16)
    r1 = x - hi.astype(f32)
    mid = r1.astype(bf16)
    lo = (r1 - mid.astype(f32)).astype(bf16)
    return hi, mid, lo


def _dot_exact(x, sel, dims, x_is_lhs=True):
    parts = _split3(x)
    if x_is_lhs:
        return _dot(parts[0], sel, dims) + _dot(parts[1], sel, dims) + _dot(parts[2], sel, dims)
    return _dot(sel, parts[0], dims) + _dot(sel, parts[1], dims) + _dot(sel, parts[2], dims)


def _ssd_common(xbc_ref, dt_ref, bias_ref, alog_ref):
    r = lax.broadcasted_iota(jnp.int32, (CHUNK, CHUNK), 0)
    cidx = lax.broadcasted_iota(jnp.int32, (CHUNK, CHUNK), 1)
    causal = r >= cidx
    tril = causal.astype(bf16)
    expand = (lax.broadcasted_iota(jnp.int32, (CHUNK, SSM_INNER), 0)
              == lax.broadcasted_iota(jnp.int32, (CHUNK, SSM_INNER), 1) // HEAD_DIM).astype(bf16)
    head_lane = cidx < SSM_HEADS
    dtp = dt_ref[...] + bias_ref[...]
    dt = jnp.where(head_lane, _softplus(dtp), 0.0)
    a_neg = -jnp.exp(alog_ref[...])
    a = dt * a_neg
    nn_dims = ((1,), (0,))
    cs = _dot_exact(a, tril, nn_dims, x_is_lhs=False)
    dt_e = _dot_exact(dt, expand, nn_dims)
    cs_e = _dot_exact(cs, expand, nn_dims)
    xs = xbc_ref[:, 0:SSM_INNER]
    xg = xs * dt_e
    ecs = jnp.exp(cs_e)
    cs_last = cs_e[CHUNK - 1:CHUNK, :]
    dse = jnp.exp(cs_last - cs_e)
    cde = jnp.exp(cs_last)
    return dict(r=r, cidx=cidx, causal=causal, tril=tril, expand=expand, head_lane=head_lane, dtp=dtp, dt=dt, a_neg=a_neg,
                cs=cs, cst=cs.T, dt_e=dt_e, cs_e=cs_e, xs=xs, xg=xg, ecs=ecs, dse=dse, cde=cde)


def _decay_mat(q, h):
    return jnp.exp(jnp.where(q["causal"], q["cs"][:, h:h + 1] - q["cst"][h:h + 1, :], NEG_INF))


def _gate_norm(y, z, nw, gate=None):
    y2 = y * (_silu(z) if gate is None else gate)
    outs, xhats, rs = [], [], []
    for g in range(SSM_GROUPS):
        sl = slice(g * GROUP_W, (g + 1) * GROUP_W)
        yg = y2[:, sl]
        r = lax.rsqrt(jnp.mean(yg * yg, axis=-1, keepdims=True) + EPS)
        xhats.append(yg * r)
        rs.append(r)
        outs.append(yg * r * nw[:, sl])
    return y2, outs, xhats, rs


def _ssd_fwd(xbc, z, dtp, params, name):
    t = xbc.shape[0]
    n_chunk = SEQ // CHUNK
    n_step = n_chunk // SSD_CHUNKS_PER_STEP

    def body(xbc_ref, z_ref, dt_ref, bias_ref, alog_ref, dskip_ref, nw_ref, yn_ref, y_ref, hs_ref, h_scr):
        @pl.when(pl.program_id(1) == 0)
        def _():
            h_scr[...] = jnp.zeros_like(h_scr)

        for s in range(SSD_CHUNKS_PER_STEP):
            r = pl.ds(s * CHUNK, CHUNK)
            one_chunk(xbc_ref.at[r], z_ref.at[r], dt_ref.at[r], bias_ref, alog_ref, dskip_ref, nw_ref,
                      yn_ref.at[r], y_ref.at[r], hs_ref.at[pl.ds(s, 1)], h_scr)

    def one_chunk(xbc_ref, z_ref, dt_ref, bias_ref, alog_ref, dskip_ref, nw_ref, yn_ref, y_ref, hs_ref, h_scr):
        q = _ssd_common(xbc_ref, dt_ref, bias_ref, alog_ref)
        low = lax.broadcasted_iota(jnp.int32, (CHUNK, LANES), 1) < HEAD_DIM
        xgb = q["xg"].astype(bf16)
        wst = (q["xg"] * q["dse"]).astype(bf16)
        hs_ref[0] = h_scr[...]
        ys = []
        for g in range(SSM_GROUPS):
            gl = slice(g * GROUP_W, (g + 1) * GROUP_W)
            bg = xbc_ref[:, SSM_INNER + g * D_STATE:SSM_INNER + (g + 1) * D_STATE].astype(bf16)
            cg = xbc_ref[:, SSM_INNER + SSM_GROUPS * D_STATE + g * D_STATE:SSM_INNER + SSM_GROUPS * D_STATE + (g + 1) * D_STATE].astype(bf16)
            cb = _nt(cg, bg)
            hg = h_scr[g]
            yoff = _nn(cg, hg.astype(bf16)) * q["ecs"][:, gl]
            pieces = []
            for i in range(HEADS_PER_GROUP // 2):
                h0 = g * HEADS_PER_GROUP + 2 * i
                xp = xgb[:, h0 * HEAD_DIM:(h0 + 2) * HEAD_DIM]
                m0 = (cb * _decay_mat(q, h0)).astype(bf16)
                m1 = (cb * _decay_mat(q, h0 + 1)).astype(bf16)
                zero = jnp.zeros_like(xp)
                pieces.append(_nn(m0, jnp.where(low, xp, zero)) + _nn(m1, jnp.where(low, zero, xp)))
            ys.append(jnp.concatenate(pieces, axis=1) + yoff + dskip_ref[:, gl] * q["xs"][:, gl])
            h_scr[g] = hg * q["cde"][:, gl] + _tn(bg, wst[:, gl])
        y = jnp.concatenate(ys, axis=1)
        y_ref[...] = y
        _, outs, _, _ = _gate_norm(y, z_ref[...], nw_ref[...])
        yn_ref[...] = jnp.concatenate(outs, axis=1).astype(bf16)

    def rows(w):
        return pl.BlockSpec((SSD_CHUNKS_PER_STEP * CHUNK, w), lambda b, c: (b * n_step + c, 0))

    def par(w):
        return pl.BlockSpec((1, w), lambda b, c: (0, 0))

    return pl.pallas_call(
        body, name=name, grid=(t // SEQ, n_step),
        in_specs=[rows(CONV_CH), rows(SSM_INNER), rows(LANES), par(LANES), par(LANES), par(SSM_INNER), par(SSM_INNER)],
        out_specs=[rows(SSM_INNER), rows(SSM_INNER),
                   pl.BlockSpec((SSD_CHUNKS_PER_STEP, SSM_GROUPS, D_STATE, GROUP_W), lambda b, c: (b * n_step + c, 0, 0, 0))],
        out_shape=[jax.ShapeDtypeStruct((t, SSM_INNER), bf16), jax.ShapeDtypeStruct((t, SSM_INNER), f32),
                   jax.ShapeDtypeStruct((t // CHUNK, SSM_GROUPS, D_STATE, GROUP_W), f32)],
        scratch_shapes=[pltpu.VMEM((SSM_GROUPS, D_STATE, GROUP_W), f32)],
        compiler_params=_cparams(("parallel", "arbitrary")),
    )(xbc, z, dtp, *params)


def _ssd_bwd(xbc, z, dtp, y, hs, dyn, params, name):
    t = xbc.shape[0]
    n_chunk = SEQ // CHUNK
    n_step = n_chunk // SSD_CHUNKS_PER_STEP

    def body(xbc_ref, z_ref, dt_ref, y_ref, hs_ref, dyn_ref, bias_ref, alog_ref, dskip_ref, nw_ref,
             dxbc_ref, dz_ref, ddt_ref, dnw_ref, dds_ref, dal_ref, dbi_ref, dh_scr):
        @pl.when(pl.program_id(1) == 0)
        def _():
            dh_scr[...] = jnp.zeros_like(dh_scr)

        first_step = (pl.program_id(0) == 0) & (pl.program_id(1) == 0)
        for s in reversed(range(SSD_CHUNKS_PER_STEP)):
            r = pl.ds(s * CHUNK, CHUNK)
            one_chunk(xbc_ref.at[r], z_ref.at[r], dt_ref.at[r], y_ref.at[r], hs_ref.at[pl.ds(s, 1)], dyn_ref.at[r],
                      bias_ref, alog_ref, dskip_ref, nw_ref, dxbc_ref.at[r], dz_ref.at[r], ddt_ref.at[r],
                      dnw_ref, dds_ref, dal_ref, dbi_ref, dh_scr, first_step if s == SSD_CHUNKS_PER_STEP - 1 else False)

    def one_chunk(xbc_ref, z_ref, dt_ref, y_ref, hs_ref, dyn_ref, bias_ref, alog_ref, dskip_ref, nw_ref,
                  dxbc_ref, dz_ref, ddt_ref, dnw_ref, dds_ref, dal_ref, dbi_ref, dh_scr, first):
        q = _ssd_common(xbc_ref, dt_ref, bias_ref, alog_ref)
        low = lax.broadcasted_iota(jnp.int32, (CHUNK, LANES), 1) < HEAD_DIM
        last_row = lax.broadcasted_iota(jnp.int32, (CHUNK, GROUP_W), 0) == CHUNK - 1
        xs, xg = q["xs"], q["xg"]
        xgb = xg.astype(bf16)
        wf = xg * q["dse"]
        wst = wf.astype(bf16)
        zz = z_ref[...]
        yy = y_ref[...]
        sz, dsz = _silu_and_grad(zz)
        y2, _, xhats, rs = _gate_norm(yy, zz, nw_ref[...], gate=sz)
        dyn_ = dyn_ref[...]
        dy2s, dnws = [], []
        for g in range(SSM_GROUPS):
            gl = slice(g * GROUP_W, (g + 1) * GROUP_W)
            gw = dyn_[:, gl] * nw_ref[:, gl]
            dy2s.append(rs[g] * (gw - xhats[g] * jnp.mean(gw * xhats[g], axis=-1, keepdims=True)))
            dnws.append(_rowsum8(dyn_[:, gl] * xhats[g]))
        dy2 = jnp.concatenate(dy2s, axis=1)
        dy = dy2 * sz
        dz_ref[...] = (dy2 * yy * dsz).astype(bf16)
        dnw_p = jnp.concatenate(dnws, axis=1)
        dds_p = _rowsum8(dy * xs)
        dyb = dy.astype(bf16)
        gfull = (dy * q["ecs"]).astype(bf16)
        dcs_c = jnp.zeros((CHUNK, CHUNK), f32)
        dcs_r = jnp.zeros((CHUNK, CHUNK), f32)
        dcs_e_parts, dxg_parts = [], []
        for g in range(SSM_GROUPS):
            gl = slice(g * GROUP_W, (g + 1) * GROUP_W)
            bsl = slice(SSM_INNER + g * D_STATE, SSM_INNER + (g + 1) * D_STATE)
            csl = slice(SSM_INNER + SSM_GROUPS * D_STATE + g * D_STATE, SSM_INNER + SSM_GROUPS * D_STATE + (g + 1) * D_STATE)
            bg = xbc_ref[:, bsl].astype(bf16)
            cg = xbc_ref[:, csl].astype(bf16)
            cb = _nt(cg, bg)
            hg = hs_ref[0, g]
            hgb = hg.astype(bf16)
            dhn = dh_scr[g]
            dhnb = dhn.astype(bf16)
            yoff = _nn(cg, hgb) * q["ecs"][:, gl]
            dw_ = _nn(bg, dhnb)
            r_e = dw_ * wf[:, gl]
            to_last = jnp.sum(r_e, axis=0, keepdims=True) + jnp.sum(dhn * hg, axis=0, keepdims=True) * q["cde"][:, gl]
            dcs_e_parts.append(dy[:, gl] * yoff - r_e + jnp.where(last_row, to_last, 0.0))
            dcb = jnp.zeros((CHUNK, CHUNK), f32)
            dxg_pairs = []
            for i in range(HEADS_PER_GROUP // 2):
                h0 = g * HEADS_PER_GROUP + 2 * i
                psl = slice(h0 * HEAD_DIM, (h0 + 2) * HEAD_DIM)
                xp = xgb[:, psl]
                dyp = dyb[:, psl]
                zero = jnp.zeros_like(dyp)
                tns = []
                for a in range(2):
                    h = h0 + a
                    lm = _decay_mat(q, h)
                    m = cb * lm
                    dm = _nt(jnp.where(low, dyp, zero) if a == 0 else jnp.where(low, zero, dyp), xp)
                    dcb = dcb + dm * lm
                    nmat = dm * m
                    dcs_c = dcs_c + jnp.where(q["cidx"] == h, jnp.sum(nmat, axis=1, keepdims=True), 0.0)
                    dcs_r = dcs_r + jnp.where(q["r"] == h, jnp.sum(nmat, axis=0, keepdims=True), 0.0)
                    tns.append(_tn(m.astype(bf16), dyp))
                dxg_pairs.append(jnp.where(low, tns[0], tns[1]))
            dxg_parts.append(jnp.concatenate(dxg_pairs, axis=1) + dw_ * q["dse"][:, gl])
            dcbb = dcb.astype(bf16)
            dxbc_ref[:, csl] = _nt(gfull[:, gl], hgb) + _nn(dcbb, bg)
            dxbc_ref[:, bsl] = _nt(wst[:, gl], dhnb) + _tn(dcbb, cg)
            dh_scr[g] = dhn * q["cde"][:, gl] + _tn(cg, gfull[:, gl])
        dxg = jnp.concatenate(dxg_parts, axis=1)
        dcs_e = jnp.concatenate(dcs_e_parts, axis=1)
        dxbc_ref[:, 0:SSM_INNER] = dskip_ref[...] * dy + dxg * q["dt_e"]
        dcs = dcs_c - dcs_r.T + _dot_exact(dcs_e, q["expand"], ((1,), (1,)))
        triu = (q["cidx"] >= q["r"]).astype(bf16)
        da = _dot_exact(dcs, triu, ((1,), (0,)), x_is_lhs=False)
        ddt = _dot_exact(dxg * xs, q["expand"], ((1,), (1,))) + da * q["a_neg"]
        ddtp = jnp.where(q["head_lane"], ddt * _sigmoid(q["dtp"]), 0.0)
        ddt_ref[...] = ddtp.astype(bf16)
        dal_p = _rowsum8(da * q["dt"]) * q["a_neg"]
        dbi_p = _rowsum8(ddtp)
        def accumulate():
            dnw_ref[...] += dnw_p
            dds_ref[...] += dds_p
            dal_ref[...] += dal_p
            dbi_ref[...] += dbi_p

        if first is False:
            accumulate()
        else:
            @pl.when(first)
            def _():
                dnw_ref[...] = dnw_p
                dds_ref[...] = dds_p
                dal_ref[...] = dal_p
                dbi_ref[...] = dbi_p

            pl.when(jnp.logical_not(first))(accumulate)

    def rows(w):
        return pl.BlockSpec((SSD_CHUNKS_PER_STEP * CHUNK, w), lambda b, c: (b * n_step + n_step - 1 - c, 0))

    def par(w):
        return pl.BlockSpec((1, w), lambda b, c: (0, 0))

    def acc(w):
        return pl.BlockSpec((SUBLANES, w), lambda b, c: (0, 0))

    return pl.pallas_call(
        body, name=name, grid=(t // SEQ, n_step),
        in_specs=[rows(CONV_CH), rows(SSM_INNER), rows(LANES), rows(SSM_INNER),
                  pl.BlockSpec((SSD_CHUNKS_PER_STEP, SSM_GROUPS, D_STATE, GROUP_W), lambda b, c: (b * n_step + n_step - 1 - c, 0, 0, 0)),
                  rows(SSM_INNER), par(LANES), par(LANES), par(SSM_INNER), par(SSM_INNER)],
        out_specs=[rows(CONV_CH), rows(SSM_INNER), rows(LANES), acc(SSM_INNER), acc(SSM_INNER), acc(LANES), acc(LANES)],
        out_shape=[jax.ShapeDtypeStruct((t, CONV_CH), f32), jax.ShapeDtypeStruct((t, SSM_INNER), bf16), jax.ShapeDtypeStruct((t, LANES), bf16),
                   jax.ShapeDtypeStruct((SUBLANES, SSM_INNER), f32), jax.ShapeDtypeStruct((SUBLANES, SSM_INNER), f32),
                   jax.ShapeDtypeStruct((SUBLANES, LANES), f32), jax.ShapeDtypeStruct((SUBLANES, LANES), f32)],
        scratch_shapes=[pltpu.VMEM((SSM_GROUPS, D_STATE, GROUP_W), f32)],
        compiler_params=_cparams(("arbitrary", "arbitrary")),
    )(xbc, z, dtp, y, hs, dyn, *params)


def _adamw_update(g, w, m, v):
    mm = ADAM_B1 * m + (1.0 - ADAM_B1) * g
    vv = ADAM_B2 * v + (1.0 - ADAM_B2) * (g * g)
    m_hat = mm / (1.0 - ADAM_B1 ** ADAM_STEP)
    v_hat = vv / (1.0 - ADAM_B2 ** ADAM_STEP)
    return -ADAM_LR * (m_hat / (jnp.sqrt(v_hat) + ADAM_EPS) + ADAM_WD * w), mm, vv


def _adamw(g_parts, w, m, v, name):
    rows, width = w.shape
    n = len(g_parts)
    tr = _row_tile(rows)

    def body(*refs):
        g_refs, (w_ref, m_ref, v_ref, g_out, d_out, m_out, v_out) = refs[:n], refs[n:]
        g = g_refs[0][...].astype(f32)
        for r in g_refs[1:]:
            g = g + r[...].astype(f32)
        g_out[...] = g
        d_out[...], m_out[...], v_out[...] = _adamw_update(g, w_ref[...], m_ref[...], v_ref[...])

    spec = pl.BlockSpec((tr, width), lambda i: (i, 0))
    return pl.pallas_call(
        body, name=name, grid=(rows // tr,), in_specs=[spec] * (n + 3), out_specs=[spec] * 4,
        out_shape=[jax.ShapeDtypeStruct((rows, width), f32)] * 4, compiler_params=_cparams(("parallel",)),
    )(*g_parts, w, m, v)


def _adamw_layers(landed, w, m, v, after, name, layers_on_columns=False):
    depth = len(landed)
    _, rows, width = landed[0].shape
    tr = _row_tile(rows)
    n_i = rows // tr
    at = (lambda ref: ref) if layers_on_columns else (lambda ref: ref.at[0])

    def body(*refs):
        part_refs, (w_ref, m_ref, v_ref, _, g_out, d_out, m_out, v_out) = refs[:depth * N_DEV], refs[depth * N_DEV:]
        for l in range(depth):
            @pl.when(pl.program_id(0) == l)
            def _(l=l):
                g = part_refs[l * N_DEV][0].astype(f32)
                for r in part_refs[l * N_DEV + 1:(l + 1) * N_DEV]:
                    g = g + r[0].astype(f32)
                at(g_out)[...] = g
                at(d_out)[...], at(m_out)[...], at(v_out)[...] = _adamw_update(g, at(w_ref)[...], at(m_ref)[...], at(v_ref)[...])

    def part_spec(l, p):
        return pl.BlockSpec((1, tr, width), lambda ll, i: (p, jnp.where(ll == l, i, jnp.where(ll < l, 0, n_i - 1)), 0))

    state = (pl.BlockSpec((tr, width), lambda ll, i: (i, ll)) if layers_on_columns
             else pl.BlockSpec((1, tr, width), lambda ll, i: (ll, i, 0)))
    return pl.pallas_call(
        body, name=name, grid=(depth, n_i),
        in_specs=[part_spec(l, p) for l in range(depth) for p in range(N_DEV)] + [state] * 3 + [ANY], out_specs=[state] * 4,
        out_shape=[jax.ShapeDtypeStruct(w.shape, f32)] * 4, compiler_params=_cparams(("arbitrary", "arbitrary")),
    )(*[landed[l] for l in range(depth) for _ in range(N_DEV)], w, m, v, after)


def _row_tile(rows, cap=512):
    for cand in range(min(rows, cap) // SUBLANES * SUBLANES, 0, -SUBLANES):
        if rows % cand == 0:
            return cand
    return rows


def _cols_from_devices(g, width, name):
    n_dev, depth, a, b = g.shape

    def body(g_ref, o_ref):
        for i in range(n_dev):
            o_ref[0, :, i * b:(i + 1) * b] = g_ref[i, 0]
        if width > n_dev * b:
            o_ref[0, :, n_dev * b:width] = jnp.zeros((a, width - n_dev * b), o_ref.dtype)

    return pl.pallas_call(
        body, name=name, grid=(depth,), in_specs=[pl.BlockSpec((n_dev, 1, a, b), lambda l: (0, l, 0, 0))],
        out_specs=pl.BlockSpec((1, a, width), lambda l: (l, 0, 0)), out_shape=jax.ShapeDtypeStruct((depth, a, width), g.dtype),
        compiler_params=_cparams(("parallel",)),
    )(g)


def _devices_from_cols(per_layer, b, name, tr=256):
    depth = len(per_layer)
    a, width = per_layer[0].shape

    def body(*refs):
        o_ref = refs[depth]
        for l in range(depth):
            for i in range(N_DEV):
                o_ref[i, l] = refs[l][:, i * b:(i + 1) * b]

    return pl.pallas_call(
        body, name=name, grid=(a // tr,), in_specs=[pl.BlockSpec((tr, width), lambda r: (r, 0))] * depth,
        out_specs=pl.BlockSpec((N_DEV, depth, tr, b), lambda r: (0, 0, r, 0)),
        out_shape=jax.ShapeDtypeStruct((N_DEV, depth, a, b), per_layer[0].dtype), compiler_params=_cparams(("parallel",)),
    )(*per_layer)


def _me():
    return lax.axis_index("x"), lax.axis_index("y"), lax.axis_index("c")


def _allgather_two_level(shards, name):
    n = len(shards)
    per = 7

    def body(*refs):
        ins, outs, token = refs[:n], refs[n:2 * n], refs[2 * n]
        send_sems, recv_sems, local_sems = refs[2 * n + 1:]
        token[...] = jnp.zeros_like(token)
        x, y, c = _me()
        me, sibling = (x, y, c), (x, y, 1 - c)
        chips = [(1 - x, y), (x, 1 - y), (1 - x, 1 - y)]

        def slot(a, p):
            return outs[a].at[4 * p[0] + 2 * p[1] + p[2]]

        def copy(a, k, block, to, src=None):
            return pltpu.make_async_remote_copy(
                src_ref=slot(a, block) if src is None else src, dst_ref=slot(a, block),
                send_sem=send_sems.at[a * per + k], recv_sem=recv_sems.at[a * per + k], device_id=to, device_id_type=MESH)

        mine = [pltpu.make_async_copy(ins[a], slot(a, me), local_sems.at[a]) for a in range(n)]
        for cp in mine:
            cp.start()
        first = []
        for a in range(n):
            first.append(copy(a, 0, me, sibling, src=ins[a]))
            first += [copy(a, 1 + j, me, (*chip, c), src=ins[a]) for j, chip in enumerate(chips)]
        for cp in first:
            cp.start()
        passed = []
        for j, chip in enumerate(chips):
            for a in range(n):
                copy(a, 1 + j, (*chip, c), me).wait_recv()
                fwd = copy(a, 4 + j, (*chip, c), sibling)
                fwd.start()
                passed.append(fwd)
        for a in range(n):
            copy(a, 0, sibling, me).wait_recv()
            for j, chip in enumerate(chips):
                copy(a, 4 + j, (*chip, 1 - c), me).wait_recv()
        for cp in first + passed:
            cp.wait_send()
        for cp in mine:
            cp.wait()

    outs = pl.pallas_call(
        body, name=name, in_specs=[ANY] * n, out_specs=[ANY] * n + [pl.BlockSpec(memory_space=pltpu.VMEM)],
        out_shape=[jax.ShapeDtypeStruct((N_DEV,) + s.shape, s.dtype) for s in shards] + [jax.ShapeDtypeStruct((SUBLANES, LANES), f32)],
        scratch_shapes=[pltpu.SemaphoreType.DMA((n * per,)), pltpu.SemaphoreType.DMA((n * per,)), pltpu.SemaphoreType.DMA((n,))],
    )(*shards)
    return outs[:n], outs[n]


def _allgather_direct(row, name):
    def body(in_ref, out_ref, send_sems, recv_sems, local_sem):
        x, y, c = _me()
        mine = out_ref.at[4 * x + 2 * y + c]
        local = pltpu.make_async_copy(in_ref, mine, local_sem)
        local.start()
        sends = []
        for k in range(1, N_DEV):
            px, py, pc = x ^ (k >> 2), y ^ ((k >> 1) & 1), c ^ (k & 1)
            sends.append(pltpu.make_async_remote_copy(
                src_ref=in_ref, dst_ref=mine, send_sem=send_sems.at[k - 1], recv_sem=recv_sems.at[k - 1],
                device_id=(px, py, pc), device_id_type=MESH))
        for cp in sends:
            cp.start()
        for k in range(1, N_DEV):
            px, py, pc = x ^ (k >> 2), y ^ ((k >> 1) & 1), c ^ (k & 1)
            theirs = out_ref.at[4 * px + 2 * py + pc]
            pltpu.make_async_remote_copy(
                src_ref=in_ref, dst_ref=theirs, send_sem=send_sems.at[k - 1], recv_sem=recv_sems.at[k - 1],
                device_id=(px, py, pc), device_id_type=MESH).wait_recv()
        for cp in sends:
            cp.wait_send()
        local.wait()

    return pl.pallas_call(
        body, name=name, in_specs=[ANY], out_specs=ANY, out_shape=jax.ShapeDtypeStruct((N_DEV,) + row.shape, row.dtype),
        scratch_shapes=[pltpu.SemaphoreType.DMA((N_DEV - 1,)), pltpu.SemaphoreType.DMA((N_DEV - 1,)), pltpu.SemaphoreType.DMA],
    )(row)


N_CHIP = N_DEV // 2
HBM = pl.BlockSpec(memory_space=pltpu.HBM)
SEM = pl.BlockSpec(memory_space=pltpu.SEMAPHORE)
EFFECT = pltpu.SideEffectType.DATAFLOW_SIDE_EFFECTING


def _peer(k):
    x, y, c = _me()
    return x ^ (k >> 2), y ^ ((k >> 1) & 1), c ^ (k & 1)


def _direct_copies(srcs, lands, send_sems, recv_sems, per_peer):
    x, y, c = _me()
    me = 4 * x + 2 * y + c
    copies = []
    for a in range(len(srcs)):
        for k in range(1, N_DEV):
            px, py, pc = _peer(k)
            piece = srcs[a].at[4 * px + 2 * py + pc] if per_peer else srcs[a]
            copies.append(pltpu.make_async_remote_copy(
                src_ref=piece, dst_ref=lands[a].at[me], send_sem=send_sems.at[a * (N_DEV - 1) + k - 1],
                recv_sem=recv_sems.at[a * (N_DEV - 1) + k - 1], device_id=(px, py, pc), device_id_type=MESH))
    return copies


def _direct_start(srcs, lands, per_peer, name):
    n = len(srcs)
    n_sem = n * (N_DEV - 1)

    def body(*refs):
        src_refs, land_refs = refs[:n], refs[n:2 * n]
        send_sems, recv_sems = refs[2 * n], refs[2 * n + 1]
        token = refs[-1]
        for cp in _direct_copies(src_refs, land_refs, send_sems, recv_sems, per_peer):
            cp.start()
        token[...] = jnp.zeros_like(token)

    outs = pl.pallas_call(
        body, name=name,
        out_shape=(pltpu.SemaphoreType.DMA((n_sem,)), pltpu.SemaphoreType.DMA((n_sem,)),
                   *[pltpu.HBM(s.shape, s.dtype) for s in srcs], *[pltpu.HBM(s.shape, s.dtype) for s in lands],
                   jax.ShapeDtypeStruct((SUBLANES, LANES), f32)),
        in_specs=[HBM] * (2 * n), out_specs=(SEM, SEM, *[HBM] * (2 * n), pl.BlockSpec(memory_space=pltpu.VMEM)),
        input_output_aliases={i: 2 + i for i in range(2 * n)},
        compiler_params=pltpu.CompilerParams(has_side_effects=EFFECT),
    )(*[pltpu.with_memory_space_constraint(s, pltpu.HBM) for s in srcs], *[pltpu.with_memory_space_constraint(s, pltpu.HBM) for s in lands])
    return outs[0], outs[1], outs[2:2 + n], outs[2 + n:2 + 2 * n], outs[-1]


def _direct_wait(send_sems, recv_sems, srcs, lands, after, per_peer, name):
    n = len(srcs)

    def body(*refs):
        src_refs, land_refs = refs[:n], refs[n:2 * n]
        s_sems, r_sems = refs[2 * n], refs[2 * n + 1]
        for cp in _direct_copies(src_refs, land_refs, s_sems, r_sems, per_peer):
            cp.wait_send()
            cp.wait_recv()

    outs = pl.pallas_call(
        body, name=name,
        out_shape=tuple(pltpu.HBM(s.shape, s.dtype) for s in list(srcs) + list(lands)),
        in_specs=[HBM] * (2 * n) + [SEM, SEM, ANY], out_specs=tuple([HBM] * (2 * n)),
        input_output_aliases={i: i for i in range(2 * n)},
        compiler_params=pltpu.CompilerParams(has_side_effects=EFFECT),
    )(*srcs, *lands, send_sems, recv_sems, after)
    return outs[n:]


def _row(v, width=None):
    v = v.reshape(1, -1).astype(f32)
    if width is not None and v.shape[1] < width:
        v = jnp.pad(v, ((0, 0), (0, width - v.shape[1])))
    return v


def _layer_params(p, l):
    return dict(
        norm_mix=_row(p["norm_mix"][l]), norm_ffn=_row(p["norm_ffn"][l]), conv_w=p["conv_w"][l], conv_b=_row(p["conv_b"][l]),
        ssd=(_row(p["dt_bias"][l], LANES), _row(p["a_log"][l], LANES), _row(jnp.repeat(p["d_skip"][l], HEAD_DIM)), _row(p["ssm_norm"][l])))


def _layer_fwd(h, w_in, rest, sp, tabs, l):
    tag = f"l{l}_"
    hn = _rmsnorm_fwd(h, sp["norm_mix"], tag + "norm_mix")
    qkv, z, xbc_pre = _in_proj(hn, w_in, (QKV_WIDTH, SSM_INNER, CONV_CH), tag + "proj")
    dtp = _matmul(hn, w_in, mode="nn", n_out=LANES, tn=LANES, b_off=DT_OFF // LANES, name=tag + "proj_dt")
    prep = _attn_prep(qkv, tabs, tag + "attn_prep")
    o, lse = _attn_fwd(prep, tag + "attn_fwd")
    xbc = _conv_fwd(xbc_pre, sp["conv_w"], sp["conv_b"], tag + "conv_fwd")
    yn, y, hs = _ssd_fwd(xbc, z, dtp, sp["ssd"], tag + "ssd_fwd")
    w_out, w_gate, w_up, w_down = rest(yn) if callable(rest) else rest
    h2 = _out_proj(o, yn, w_out, h, tag + "out_proj")
    hn2 = _rmsnorm_fwd(h2, sp["norm_ffn"], tag + "norm_ffn")
    g, u, act = _swiglu_fwd(hn2, w_gate, w_up, tag + "ffn_up")
    h3 = _matmul(act, w_down, mode="nn", tk=1408, add=h2, name=tag + "ffn_down")
    saved = dict(h=h, hn=hn, prep=prep, z=z, xbc_pre=xbc_pre, dtp=dtp, o=o, lse=lse, xbc=xbc, yn=yn, y=y, hs=hs, h2=h2, hn2=hn2, g=g, u=u, act=act,
                 rest=(w_out, w_gate, w_up, w_down))
    return h3, saved


def _layer_bwd(dh3_pair, s, big, sp, tabs, l, gd=f32, after_ffn=None):
    tag = f"l{l}_"
    dh3, dh3b = dh3_pair
    w_in, w_out, w_gate, w_up, w_down = big
    dg, du = _swiglu_bwd(dh3b, w_down, s["g"], s["u"], tag + "ffn_down_bwd")
    dw_down = _matmul(s["act"], dh3b, mode="tn", tm=1408, tn=512, tk=2048, out_dtype=gd, name=tag + "dw_down")
    dw_gate = _matmul(dg, s["hn2"], mode="tn", tm=1408, tn=512, tk=2048, out_dtype=gd, name=tag + "dw_gate")
    dw_up = _matmul(du, s["hn2"], mode="tn", tm=1408, tn=512, tk=2048, out_dtype=gd, name=tag + "dw_up")
    norm_ffn = sp["norm_ffn"] if after_ffn is None else sp["norm_ffn"] + after_ffn(dict(w_gate=dw_gate, w_up=dw_up, w_down=dw_down))
    dh2, dh2b, dnf = _nt_norm_bwd([(dg, w_gate), (du, w_up)], s["h2"], norm_ffn, dh3, tag + "ffn_up_bwd_norm", tk=1408, b_is_kd=True,
                                  vmem=VMEM_LIMIT_TWO_PAIRS)
    d_o = _matmul(dh2b, w_out, mode="nt", n_out=ATTN_WIDTH, tn=512, b_off=0, name=tag + "out_attn_bwd")
    dyn = _matmul(dh2b, w_out, mode="nt", n_out=SSM_INNER, tn=512, b_off=1, name=tag + "out_ssm_bwd")
    dw_out = jnp.concatenate([_matmul(s["o"], dh2b, mode="tn", tm=512, tn=512, tk=2048, out_dtype=gd, name=tag + "dw_out_attn"),
                              _matmul(s["yn"], dh2b, mode="tn", tm=512, tn=512, tk=2048, out_dtype=gd, name=tag + "dw_out_ssm")], axis=0)
    dxbc, dz, ddtp, dnw, dds, dal, dbi = _ssd_bwd(s["xbc"], s["z"], s["dtp"], s["y"], s["hs"], dyn, sp["ssd"], tag + "ssd_bwd")
    dxbc_pre, dconv_w, dconv_b = _conv_bwd(s["xbc_pre"], sp["conv_w"], sp["conv_b"], dxbc, tag + "conv_bwd")
    dq, dk, dv = _attn_bwd(s["prep"], tabs, s["o"], s["lse"], d_o, tag + "attn_bwd")
    dproj = jnp.concatenate([dq, dk, dv, dz, dxbc_pre, ddtp], axis=1)
    dw_in = _matmul(s["hn"], dproj, mode="tn", tm=512, tn=1152, tk=2048, out_dtype=gd, name=tag + "dw_in")
    res = _nt_norm_bwd([(dproj, w_in)], s["h"], sp["norm_mix"], dh2, tag + "proj_bwd_norm", tk=1152, bf16_copy=l > 0)
    dh, dhb, dnm = res if l > 0 else (res[0], None, res[1])
    grads = dict(
        norm_mix=dnm.sum(0), w_in=dw_in, conv_w=dconv_w, conv_b=dconv_b[0], dt_bias=dbi.sum(0)[:SSM_HEADS], a_log=dal.sum(0)[:SSM_HEADS],
        d_skip=dds.sum(0).reshape(SSM_HEADS, HEAD_DIM).sum(1), ssm_norm=dnw.sum(0), w_out=dw_out, norm_ffn=dnf.sum(0),
        w_gate=dw_gate, w_up=dw_up, w_down=dw_down)
    return (dh, dhb), grads


def _local_step(x, positions, target, p, bigs):
    tabs = _rope_tables(positions.reshape(-1, 1), "rope_tables")
    h = x
    saved, sps = [], []
    for l in range(DEPTH):
        sps.append(_layer_params(p, l))
        h, s = _layer_fwd(h, bigs[l][0], bigs[l][1:], sps[l], tabs, l)
        saved.append(s)
    dh, dhb, loss_parts, dfn = _final_loss(h, _row(p["final_norm"]), target, "final_loss")
    dh = (dh, dhb)
    layer_grads = [None] * DEPTH
    for l in reversed(range(DEPTH)):
        dh, layer_grads[l] = _layer_bwd(dh, saved[l], bigs[l], sps[l], tabs, l)
    grads = {k: [layer_grads[l][k] for l in range(DEPTH)] for k in layer_grads[0]}
    grads["final_norm"] = dfn.sum(0)
    return jnp.sum(loss_parts), dh[0], grads


BIG = ("w_in", "w_out", "w_gate", "w_up", "w_down")
REST = BIG[1:]
FFN = ("w_gate", "w_up", "w_down")
MIX = ("w_in", "w_out")
COL_SHARDED = ("w_in",)
TRANSPOSED = ("w_gate", "w_up")
SMALL = ("norm_mix", "conv_b", "dt_bias", "a_log", "d_skip", "ssm_norm", "norm_ffn", "final_norm")
WEIGHTS = ("norm_mix", "w_in", "conv_w", "conv_b", "dt_bias", "a_log", "d_skip", "ssm_norm", "w_out", "norm_ffn", "w_gate", "w_up", "w_down", "final_norm")
SMALL_ROWS = 88
CONVW_ROWS = 96
CONVW_SHARD_ROWS = 16


def _full_from_gathered(name, g, l):
    _, a, b = g.shape
    if name in COL_SHARDED:
        width = IN_PROJ_PAD if name == "w_in" else N_DEV * b
        return _cols_from_devices(g.reshape(N_DEV, 1, a, b), width, f"cols_l{l}_{name}").reshape(a, width)
    return g.reshape(N_DEV * a, b)


def _by_device(name, full, shard_shape, l):
    a, b = shard_shape
    if name in COL_SHARDED:
        return _devices_from_cols([full], b, f"devs_l{l}_{name}").reshape(N_CHIP, 2, a, b)
    return full.reshape(N_CHIP, 2, a, b)


def _pack_rows(parts, rows, width):
    flat = jnp.concatenate([q.reshape(-1) for q in parts])
    return jnp.pad(flat, (0, rows * width - flat.shape[0])).reshape(rows, width)


def _unpack(flat, like):
    out, off = [], 0
    for q in like:
        out.append(flat[off:off + q.size].reshape(q.shape))
        off += q.size
    return out


def kernel(x, positions, norm_mix, w_in, conv_w, conv_b, dt_bias, a_log, d_skip, ssm_norm, w_out, norm_ffn, w_gate, w_up, w_down, final_norm, loss_target, m_norm_mix, m_w_in, m_conv_w, m_conv_b, m_dt_bias, m_a_log, m_d_skip, m_ssm_norm, m_w_out, m_norm_ffn, m_w_gate, m_w_up, m_w_down, m_final_norm, v_norm_mix, v_w_in, v_conv_w, v_conv_b, v_dt_bias, v_a_log, v_d_skip, v_ssm_norm, v_w_out, v_norm_ffn, v_w_gate, v_w_up, v_w_down, v_final_norm):
    w = dict(norm_mix=norm_mix, w_in=w_in, conv_w=conv_w, conv_b=conv_b, dt_bias=dt_bias, a_log=a_log, d_skip=d_skip, ssm_norm=ssm_norm,
             w_out=w_out, norm_ffn=norm_ffn, w_gate=w_gate, w_up=w_up, w_down=w_down, final_norm=final_norm)
    m = dict(norm_mix=m_norm_mix, w_in=m_w_in, conv_w=m_conv_w, conv_b=m_conv_b, dt_bias=m_dt_bias, a_log=m_a_log, d_skip=m_d_skip,
             ssm_norm=m_ssm_norm, w_out=m_w_out, norm_ffn=m_norm_ffn, w_gate=m_w_gate, w_up=m_w_up, w_down=m_w_down, final_norm=m_final_norm)
    v = dict(norm_mix=v_norm_mix, w_in=v_w_in, conv_w=v_conv_w, conv_b=v_conv_b, dt_bias=v_dt_bias, a_log=v_a_log, d_skip=v_d_skip,
             ssm_norm=v_ssm_norm, w_out=v_w_out, norm_ffn=v_norm_ffn, w_gate=v_w_gate, w_up=v_w_up, w_down=v_w_down, final_norm=v_final_norm)
    ax, ay, ac = lax.axis_index("x"), lax.axis_index("y"), lax.axis_index("c")
    dev = 4 * ax + 2 * ay + ac

    assert DEPTH == 2
    t = x.shape[0] * x.shape[1]
    xf, target = x.reshape(t, D_MODEL), loss_target.reshape(t, D_MODEL)

    def own_slot(block):
        return lax.dynamic_update_slice(lax.empty((N_DEV,) + block.shape[1:], block.dtype), block, (dev,) + (0,) * (block.ndim - 1))

    def layer_shard(arr, k, l):
        return jnp.transpose(arr, (2, 0, 1))[:, l, :] if k in TRANSPOSED else arr[l]

    def gather_start(keys, l, tie, name):
        shards = [(layer_shard(w[keys[0]], keys[0], l) + tie).astype(bf16)] + [layer_shard(w[k], k, l).astype(bf16) for k in keys[1:]]
        return _direct_start(shards, [own_slot(s[None]) for s in shards], False, name)

    def scatter_start(keys, grads_l, l, name):
        shapes = [(w[k].shape[2], w[k].shape[1]) if k in TRANSPOSED else w[k].shape[1:] for k in keys]
        by_dev = [_by_device(k, grads_l[k], sh, l).reshape((N_DEV,) + sh) for k, sh in zip(keys, shapes)]
        return _direct_start(by_dev, [own_slot(lax.dynamic_slice_in_dim(g, dev, 1, 0)) for g in by_dev], True, name)

    (g_in0, conv_all), tie = _allgather_two_level([w["w_in"][0].astype(bf16), w["conv_w"]], "gather_l0_w_in")
    rest0_copy = gather_start(REST, 0, tie[0, 0], "gather_l0_rest_start")
    l1_copy = gather_start(BIG, 1, rest0_copy[4][0, 0], "gather_l1_start")
    p = {k: w[k] for k in SMALL}
    p["norm_mix"] = p["norm_mix"] + l1_copy[4][0, 0]
    p["conv_w"] = jnp.transpose(conv_all, (1, 2, 0, 3)).reshape(DEPTH, CONV_WIDTH, CONV_CH)
    sp0, sp1 = _layer_params(p, 0), _layer_params(p, 1)

    def rest0(after):
        lands = _direct_wait(*rest0_copy[:4], after, False, "gather_l0_rest_wait")
        return tuple(_full_from_gathered(k, g, 0) for k, g in zip(REST, lands))

    tabs = _rope_tables(positions.reshape(t, 1), "rope_tables")
    w_in0 = _full_from_gathered("w_in", g_in0, 0)
    h1, saved0 = _layer_fwd(xf, w_in0, rest0, sp0, tabs, 0)
    lands1 = _direct_wait(*l1_copy[:4], h1, False, "gather_l1_wait")
    bigs1 = tuple(_full_from_gathered(k, g, 1) for k, g in zip(BIG, lands1))
    h2, saved1 = _layer_fwd(h1, bigs1[0], bigs1[1:], sp1, tabs, 1)
    dh, dhb, loss_parts, dfn = _final_loss(h2, _row(p["final_norm"]), target, "final_loss")
    loss_local = jnp.sum(loss_parts)

    dh, grads1 = _layer_bwd((dh, dhb), saved1, bigs1, sp1, tabs, 1, gd=bf16)
    l1_grads = scatter_start(BIG, grads1, 1, "scatter_l1_start")
    w_out0, w_gate0, w_up0, w_down0 = saved0["rest"]
    bigs0 = (w_in0, w_out0, w_gate0, w_up0, w_down0 + l1_grads[4][0, 0].astype(bf16))
    ffn0_grads = []

    def after_ffn(grads_ffn):
        ffn0_grads.append(scatter_start(FFN, grads_ffn, 0, "scatter_l0_ffn_start"))
        return ffn0_grads[0][4][0, 0]

    (dx, _), grads0 = _layer_bwd(dh, saved0, bigs0, sp0, tabs, 0, gd=bf16, after_ffn=after_ffn)
    mix0_grads = scatter_start(MIX, grads0, 0, "scatter_l0_mix_start")
    landed = {(k, 1): g for k, g in zip(BIG, _direct_wait(*l1_grads[:4], dx, True, "scatter_l1_wait"))}
    landed.update({(k, 0): g for k, g in zip(FFN, _direct_wait(*ffn0_grads[0][:4], dx, True, "scatter_l0_ffn_wait"))})
    out_g, out_d, out_m, out_v = {}, {}, {}, {}

    def update(keys, after):
        for k in keys:
            parts = [landed[k, l] for l in range(DEPTH)]
            if k in TRANSPOSED:
                depth, a, b = w[k].shape
                state = [jnp.transpose(s, (2, 0, 1)).reshape(b, depth * a) for s in (w[k], m[k], v[k])]
                res = _adamw_layers(parts, *state, after, "adamw_" + k, layers_on_columns=True)
                res = [jnp.transpose(r.reshape(b, depth, a), (1, 2, 0)) for r in res]
            else:
                res = _adamw_layers(parts, w[k], m[k], v[k], after, "adamw_" + k)
            for dst, r in zip((out_g, out_d, out_m, out_v), res):
                dst[k] = r

    update(FFN, mix0_grads[4])
    grads = {k: [grads0[k], grads1[k]] for k in grads0 if k not in BIG}
    grads["final_norm"] = dfn.sum(0) + mix0_grads[4][0, 0]

    small_like = [w[k] for k in SMALL]
    small_grads = [jnp.stack(grads[k]) if k != "final_norm" else grads[k] for k in SMALL]
    small_pack = jnp.concatenate([_pack_rows(small_grads, SMALL_ROWS, LANES), _pack_rows([jnp.stack(grads["conv_w"])], CONVW_ROWS, LANES)], axis=0)
    parts = _allgather_direct(small_pack, "gather_small_grads")
    g_s, d_s, m_s, v_s = _adamw(
        [parts[i, :SMALL_ROWS] for i in range(N_DEV)], _pack_rows(small_like, SMALL_ROWS, LANES),
        _pack_rows([m[k] for k in SMALL], SMALL_ROWS, LANES), _pack_rows([v[k] for k in SMALL], SMALL_ROWS, LANES), "adamw_replicated")
    for dst, src in ((out_g, g_s), (out_d, d_s), (out_m, m_s), (out_v, v_s)):
        dst.update(zip(SMALL, _unpack(src.reshape(-1), small_like)))
    shard_w = conv_w.shape[-1]
    conv_parts = parts[:, SMALL_ROWS:].reshape(N_DEV, DEPTH, CONV_WIDTH, CONV_CH)
    conv_mine = lax.dynamic_slice_in_dim(conv_parts, dev * shard_w, shard_w, axis=3)
    g_c, d_c, m_c, v_c = _adamw(
        [_pack_rows([conv_mine[i]], CONVW_SHARD_ROWS, LANES) for i in range(N_DEV)], _pack_rows([conv_w], CONVW_SHARD_ROWS, LANES),
        _pack_rows([m["conv_w"]], CONVW_SHARD_ROWS, LANES), _pack_rows([v["conv_w"]], CONVW_SHARD_ROWS, LANES), "adamw_conv_w")
    for dst, src in ((out_g, g_c), (out_d, d_c), (out_m, m_c), (out_v, v_c)):
        dst["conv_w"] = src.reshape(-1)[:conv_w.size].reshape(conv_w.shape)

    landed.update({(k, 0): g for k, g in zip(MIX, _direct_wait(*mix0_grads[:4], v_c + out_v["w_down"][0, :CONVW_SHARD_ROWS, :LANES], True, "scatter_l0_mix_wait"))})
    update(MIX, v_c)

    loss = lax.psum(loss_local, ("x", "y", "c"))
    return (loss, dx.reshape(x.shape), *[out_g[k] for k in WEIGHTS], *[out_d[k] for k in WEIGHTS],
            *[out_m[k] for k in WEIGHTS], *[out_v[k] for k in WEIGHTS])
```

```python
import jax
import jax.numpy as jnp
import numpy as np
from jax import lax
from jax.experimental import pallas as pl
from jax.experimental.pallas import tpu as pltpu

f32 = jnp.float32
bf16 = jnp.bfloat16

D_MODEL = 1024
SEQ = 2048
DEPTH = 2
HEAD_DIM = 64
N_ATTN_HEADS = 8
N_KV_HEADS = 2
ATTN_WIDTH = 512
KV_WIDTH = 128
ROPE_DIM = 16
ROPE_THETA = 500000.0
DILATIONS = (1, 4, 16)
ATTN_BLOCK = 128
SSM_HEADS = 16
SSM_INNER = 1024
SSM_GROUPS = 2
D_STATE = 128
CONV_WIDTH = 4
CHUNK = 128
CONV_CH = 1536
MIX_WIDTH = 1536
QKV_WIDTH = ATTN_WIDTH + 2 * KV_WIDTH
DT_OFF = 3328
IN_PROJ = 3344
IN_PROJ_PAD = 3456
FFN_HIDDEN = 2816
EPS = 1e-5
N_DEV = 8
ADAM_LR = 0.001
ADAM_B1 = 0.9
ADAM_B2 = 0.999
ADAM_EPS = 1e-08
ADAM_WD = 0.01
ADAM_STEP = 10

LANES = 128
SUBLANES = 8
VMEM_LIMIT = 56 * 1024 * 1024
VMEM_LIMIT_TWO_PAIRS = 60 * 1024 * 1024

MESH = pl.DeviceIdType.MESH
ANY = pl.BlockSpec(memory_space=pl.ANY)


def _cparams(sem, vmem=None):
    return pltpu.CompilerParams(dimension_semantics=sem, vmem_limit_bytes=vmem or VMEM_LIMIT)


def _sigmoid(x):
    return 1.0 / (1.0 + jnp.exp(-x))


def _silu(x):
    return x * _sigmoid(x)


def _dsilu(x):
    s = _sigmoid(x)
    return s * (1.0 + x * (1.0 - s))


def _silu_and_grad(x):
    s = _sigmoid(x)
    return x * s, s * (1.0 + x * (1.0 - s))


def _softplus(x):
    return jnp.maximum(x, 0.0) + jnp.log(1.0 + jnp.exp(-jnp.abs(x)))


def _dot(a, b, dims, precision=None):
    return lax.dot_general(a, b, (dims, ((), ())), preferred_element_type=f32, precision=precision)


def _nn(a, b, precision=None):
    return _dot(a, b, ((1,), (0,)), precision)


def _nt(a, b):
    return _dot(a, b, ((1,), (1,)))


def _tn(a, b):
    return _dot(a, b, ((0,), (0,)))


def _rowsum8(t):
    n, w = t.shape
    return jnp.sum(t.reshape(n // SUBLANES, SUBLANES, w), axis=0)


def _matmul(a, b, *, mode, n_out=None, b_off=0, add=None, out_dtype=f32, tm=2048, tn=512, tk=1024, name):
    if mode == "tn":
        kk, m = a.shape
    else:
        m, kk = a.shape
    n = n_out if n_out is not None else (b.shape[0] if mode == "nt" else b.shape[1])
    tm, tn, tk = min(tm, m), min(tn, n), min(tk, kk)
    assert m % tm == 0 and n % tn == 0 and kk % tk == 0, (name, m, n, kk, tm, tn, tk)
    nk = kk // tk
    if mode == "nn":
        a_spec = pl.BlockSpec((tm, tk), lambda i, j, k: (i, k))
        b_spec = pl.BlockSpec((tk, tn), lambda i, j, k: (k, j + b_off))
        dims = ((1,), (0,))
    elif mode == "nt":
        a_spec = pl.BlockSpec((tm, tk), lambda i, j, k: (i, k))
        b_spec = pl.BlockSpec((tn, tk), lambda i, j, k: (j + b_off, k))
        dims = ((1,), (1,))
    else:
        a_spec = pl.BlockSpec((tk, tm), lambda i, j, k: (k, i))
        b_spec = pl.BlockSpec((tk, tn), lambda i, j, k: (k, j + b_off))
        dims = ((0,), (0,))
    o_spec = pl.BlockSpec((tm, tn), lambda i, j, k: (i, j))
    has_add = add is not None

    def body(*refs):
        if has_add:
            a_ref, b_ref, add_ref, o_ref, acc_ref = refs
        else:
            a_ref, b_ref, o_ref, acc_ref = refs
        k = pl.program_id(2)
        part = _dot(a_ref[...].astype(bf16), b_ref[...].astype(bf16), dims)

        @pl.when(k == 0)
        def _():
            acc_ref[...] = part

        @pl.when(k > 0)
        def _():
            acc_ref[...] += part

        @pl.when(k == nk - 1)
        def _():
            r = acc_ref[...]
            if has_add:
                r = r + add_ref[...]
            o_ref[...] = r.astype(out_dtype)

    in_specs = [a_spec, b_spec] + ([o_spec] if has_add else [])
    args = (a, b) + ((add,) if has_add else ())
    return pl.pallas_call(
        body, name=name, grid=(m // tm, n // tn, nk), in_specs=in_specs, out_specs=o_spec,
        out_shape=jax.ShapeDtypeStruct((m, n), out_dtype), scratch_shapes=[pltpu.VMEM((tm, tn), f32)],
        compiler_params=_cparams(("parallel", "parallel", "arbitrary")),
    )(*args)


def _in_proj(hn, w_in, widths, name, tm=2048, tn=256):
    m, k = hn.shape
    starts = [sum(widths[:i]) // tn for i in range(len(widths))]
    counts = [wd // tn for wd in widths]
    assert m % tm == 0 and all(wd % tn == 0 for wd in widths)
    n_out = len(widths)

    def body(a_ref, w_ref, *o_refs):
        j = pl.program_id(1)
        acc = _nn(a_ref[...], w_ref[...])
        for s, c, o_ref in zip(starts, counts, o_refs):
            @pl.when((j >= s) & (j < s + c))
            def _(o_ref=o_ref):
                o_ref[...] = acc

    def o_spec(s, c):
        return pl.BlockSpec((tm, tn), lambda i, j: (i, jnp.clip(j - s, 0, c - 1)))

    return pl.pallas_call(
        body, name=name, grid=(m // tm, sum(counts)),
        in_specs=[pl.BlockSpec((tm, k), lambda i, j: (i, 0)), pl.BlockSpec((k, tn), lambda i, j: (0, j))],
        out_specs=[o_spec(s, c) for s, c in zip(starts, counts)],
        out_shape=[jax.ShapeDtypeStruct((m, wd), f32) for wd in widths], compiler_params=_cparams(("parallel", "arbitrary")),
    )(hn, w_in)


def _out_proj(o, yn, w_out, h, name, tm=2048, tn=512):
    m, kb = o.shape
    n = w_out.shape[1]
    n_y = yn.shape[1] // kb
    assert yn.shape[1] % kb == 0 and w_out.shape[0] == kb * (1 + n_y) and m % tm == 0 and n % tn == 0

    def body(*refs):
        o_ref, y_refs, w_refs, h_ref, out_ref = refs[0], refs[1:1 + n_y], refs[1 + n_y:2 + 2 * n_y], refs[-2], refs[-1]
        acc = h_ref[...] + _nn(o_ref[...].astype(bf16), w_refs[0][...])
        for y_ref, w_ref in zip(y_refs, w_refs[1:]):
            acc = acc + _nn(y_ref[...], w_ref[...])
        out_ref[...] = acc

    res = pl.BlockSpec((tm, tn), lambda i, j: (i, j))

    def a_blk(c):
        return pl.BlockSpec((tm, kb), lambda i, j: (i, c))

    def w_blk(r):
        return pl.BlockSpec((kb, tn), lambda i, j: (r, j))

    return pl.pallas_call(
        body, name=name, grid=(m // tm, n // tn),
        in_specs=[a_blk(0)] + [a_blk(c) for c in range(n_y)] + [w_blk(r) for r in range(1 + n_y)] + [res],
        out_specs=res, out_shape=jax.ShapeDtypeStruct((m, n), f32), compiler_params=_cparams(("parallel", "parallel")),
    )(o, *[yn] * n_y, *[w_out] * (1 + n_y), h)


def _swiglu_fwd(hn, w_gate, w_up, name, tm=2048, tn=256):
    m, k = hn.shape
    n = w_gate.shape[0]
    assert m % tm == 0 and n % tn == 0, (name, m, n, tm, tn)

    def body(a_ref, wg_ref, wu_ref, g_ref, u_ref, act_ref):
        a = a_ref[...]
        g = _nt(a, wg_ref[...])
        u = _nt(a, wu_ref[...])
        sg, dsg = _silu_and_grad(g)
        g_ref[...] = (u * dsg).astype(bf16)
        u_ref[...] = sg.astype(bf16)
        act_ref[...] = (sg * u).astype(bf16)

    a_spec = pl.BlockSpec((tm, k), lambda i, j: (i, 0))
    w_spec = pl.BlockSpec((tn, k), lambda i, j: (j, 0))
    o_spec = pl.BlockSpec((tm, tn), lambda i, j: (i, j))
    return pl.pallas_call(
        body, name=name, grid=(m // tm, n // tn), in_specs=[a_spec, w_spec, w_spec], out_specs=[o_spec, o_spec, o_spec],
        out_shape=[jax.ShapeDtypeStruct((m, n), bf16)] * 3,
        compiler_params=_cparams(("parallel", "parallel")),
    )(hn, w_gate, w_up)


def _swiglu_bwd(dh, w_down, g, u, name, tm=2048, tn=256):
    m, k = dh.shape
    n = w_down.shape[0]
    assert m % tm == 0 and n % tn == 0, (name, m, n, tm, tn)

    def body(a_ref, w_ref, g_ref, u_ref, dg_ref, du_ref):
        dact = _nt(a_ref[...].astype(bf16), w_ref[...])
        dg_ref[...] = (dact * g_ref[...].astype(f32)).astype(bf16)
        du_ref[...] = (dact * u_ref[...].astype(f32)).astype(bf16)

    a_spec = pl.BlockSpec((tm, k), lambda i, j: (i, 0))
    w_spec = pl.BlockSpec((tn, k), lambda i, j: (j, 0))
    o_spec = pl.BlockSpec((tm, tn), lambda i, j: (i, j))
    return pl.pallas_call(
        body, name=name, grid=(m // tm, n // tn), in_specs=[a_spec, w_spec, o_spec, o_spec], out_specs=[o_spec, o_spec],
        out_shape=[jax.ShapeDtypeStruct((m, n), bf16), jax.ShapeDtypeStruct((m, n), bf16)],
        compiler_params=_cparams(("parallel", "parallel")),
    )(dh, w_down, g, u)


def _rmsnorm_fwd(h, w, name, tm=512):
    m, d = h.shape

    def body(h_ref, w_ref, o_ref):
        x = h_ref[...]
        r = lax.rsqrt(jnp.mean(x * x, axis=-1, keepdims=True) + EPS)
        o_ref[...] = (x * r * w_ref[...]).astype(bf16)

    return pl.pallas_call(
        body, name=name, grid=(m // tm,),
        in_specs=[pl.BlockSpec((tm, d), lambda i: (i, 0)), pl.BlockSpec((1, d), lambda i: (0, 0))],
        out_specs=pl.BlockSpec((tm, d), lambda i: (i, 0)), out_shape=jax.ShapeDtypeStruct((m, d), bf16),
        compiler_params=_cparams(("parallel",)),
    )(h, w)


def _nt_norm_bwd(pairs, h, w, dres, name, tk, b_is_kd=False, bf16_copy=True, tm=1024, vmem=None):
    m, d = h.shape
    contract = _nn if b_is_kd else _nt
    steps = [p[0].shape[1] // tk for p in pairs]
    assert all(p[0].shape[1] % tk == 0 for p in pairs), (name, tk)
    starts = [sum(steps[:i]) for i in range(len(pairs))]
    nk = sum(steps)
    n_p = len(pairs)

    def body(*refs):
        ab = refs[:2 * n_p]
        h_ref, w_ref, dres_ref, dh_ref = refs[2 * n_p:2 * n_p + 4]
        dhb_ref = refs[2 * n_p + 4] if bf16_copy else None
        dw_ref, acc_ref = refs[-2:]
        i, k = pl.program_id(0), pl.program_id(1)

        @pl.when(k == 0)
        def _():
            acc_ref[...] = jnp.zeros_like(acc_ref)

        for p in range(n_p):
            @pl.when((k >= starts[p]) & (k < starts[p] + steps[p]))
            def _(p=p):
                acc_ref[...] += contract(ab[2 * p][...], ab[2 * p + 1][...])

        @pl.when(k == nk - 1)
        def _():
            x = h_ref[...]
            r = lax.rsqrt(jnp.mean(x * x, axis=-1, keepdims=True) + EPS)
            xhat = x * r
            dy = acc_ref[...]
            gw = dy * w_ref[...]
            dh = dres_ref[...] + r * (gw - xhat * jnp.mean(gw * xhat, axis=-1, keepdims=True))
            dh_ref[...] = dh
            if bf16_copy:
                dhb_ref[...] = dh.astype(bf16)
            part = _rowsum8(dy * xhat)

            @pl.when(i == 0)
            def _():
                dw_ref[...] = part

            @pl.when(i > 0)
            def _():
                dw_ref[...] += part

    def clamp(k, p):
        return jnp.clip(k - starts[p], 0, steps[p] - 1)

    in_specs = []
    for p in range(n_p):
        b_spec = (pl.BlockSpec((tk, d), lambda i, k, p=p: (clamp(k, p), 0)) if b_is_kd
                  else pl.BlockSpec((d, tk), lambda i, k, p=p: (0, clamp(k, p))))
        in_specs += [pl.BlockSpec((tm, tk), lambda i, k, p=p: (i, clamp(k, p))), b_spec]
    row = pl.BlockSpec((tm, d), lambda i, k: (i, 0))
    in_specs += [row, pl.BlockSpec((1, d), lambda i, k: (0, 0)), row]
    return pl.pallas_call(
        body, name=name, grid=(m // tm, nk), in_specs=in_specs,
        out_specs=[row] + [row] * bf16_copy + [pl.BlockSpec((SUBLANES, d), lambda i, k: (0, 0))],
        out_shape=[jax.ShapeDtypeStruct((m, d), f32)] + [jax.ShapeDtypeStruct((m, d), bf16)] * bf16_copy + [jax.ShapeDtypeStruct((SUBLANES, d), f32)],
        scratch_shapes=[pltpu.VMEM((tm, d), f32)], compiler_params=_cparams(("arbitrary", "arbitrary"), vmem),
    )(*[t for p in pairs for t in p], h, w, dres)


def _final_loss(h, w, target, name, tm=512):
    m, d = h.shape

    def body(h_ref, w_ref, t_ref, dh_ref, dhb_ref, loss_ref, dw_ref):
        x = h_ref[...]
        r = lax.rsqrt(jnp.mean(x * x, axis=-1, keepdims=True) + EPS)
        xhat = x * r
        ww = w_ref[...]
        err = xhat * ww - t_ref[...]
        dy = err * (1.0 / d)
        gw = dy * ww
        dh = r * (gw - xhat * jnp.mean(gw * xhat, axis=-1, keepdims=True))
        dh_ref[...] = dh
        dhb_ref[...] = dh.astype(bf16)
        lpart = _rowsum8(err * err) * (0.5 / d)
        wpart = _rowsum8(dy * xhat)

        @pl.when(pl.program_id(0) == 0)
        def _():
            loss_ref[...] = lpart
            dw_ref[...] = wpart

        @pl.when(pl.program_id(0) > 0)
        def _():
            loss_ref[...] += lpart
            dw_ref[...] += wpart

    row = pl.BlockSpec((tm, d), lambda i: (i, 0))
    acc = pl.BlockSpec((SUBLANES, d), lambda i: (0, 0))
    return pl.pallas_call(
        body, name=name, grid=(m // tm,),
        in_specs=[row, pl.BlockSpec((1, d), lambda i: (0, 0)), row], out_specs=[row, row, acc, acc],
        out_shape=[jax.ShapeDtypeStruct((m, d), f32), jax.ShapeDtypeStruct((m, d), bf16),
                   jax.ShapeDtypeStruct((SUBLANES, d), f32), jax.ShapeDtypeStruct((SUBLANES, d), f32)],
        compiler_params=_cparams(("arbitrary",)),
    )(h, w, target)


def _lane_tables():
    f = np.arange(LANES) % HEAD_DIM
    inv = ROPE_THETA ** (-jnp.arange(0, ROPE_DIM, 2, dtype=f32) / ROPE_DIM)
    invf = jnp.where(f < ROPE_DIM, inv[f % (ROPE_DIM // 2)], 0.0).astype(f32)
    return invf.reshape(1, LANES)


def _rope_tables(pos_col, name):
    t = pos_col.shape[0]
    tm = SEQ

    def body(p_ref, f_ref, c_ref, s1_ref, s2_ref):
        ang = p_ref[...].astype(f32) * f_ref[...]
        co, si = jnp.cos(ang), jnp.sin(ang)
        f = lax.broadcasted_iota(jnp.int32, (tm, LANES), 1) % HEAD_DIM
        c_ref[...] = jnp.where(f < ROPE_DIM, co, 1.0)
        s1_ref[...] = jnp.where(f < ROPE_DIM // 2, -si, 0.0)
        s2_ref[...] = jnp.where((f >= ROPE_DIM // 2) & (f < ROPE_DIM), si, 0.0)

    row = pl.BlockSpec((tm, LANES), lambda i: (i, 0))
    return pl.pallas_call(
        body, name=name, grid=(t // tm,),
        in_specs=[pl.BlockSpec((tm, 1), lambda i: (i, 0)), pl.BlockSpec((1, LANES), lambda i: (0, 0))],
        out_specs=[row, row, row], out_shape=[jax.ShapeDtypeStruct((t, LANES), f32)] * 3,
        compiler_params=_cparams(("parallel",)),
    )(pos_col, _lane_tables())


def _rot(x, c, s1, s2):
    return x * c + pltpu.roll(x, LANES - ROPE_DIM // 2, 1) * s1 + pltpu.roll(x, ROPE_DIM // 2, 1) * s2


def _rot_t(g, c, s1, s2):
    return g * c + pltpu.roll(g * s1, ROPE_DIM // 2, 1) + pltpu.roll(g * s2, LANES - ROPE_DIM // 2, 1)


def _dup_head(x, kvh, low):
    a = jnp.where(kvh == 0, x, pltpu.roll(x, HEAD_DIM, 1))
    return jnp.where(low, a, pltpu.roll(a, HEAD_DIM, 1))


def _deinterleave(src_ref, dst_ref, d, dtype):
    length = SEQ // d
    if d == 1:
        dst_ref[...] = src_ref[...].astype(dtype)
    else:
        for r in range(d):
            dst_ref[pl.ds(r * length, length), :] = src_ref[pl.ds(r, length, stride=d), :].astype(dtype)


def _interleave_store(src_ref, dst_ref, d, accumulate):
    length = SEQ // d
    if d == 1:
        if accumulate:
            dst_ref[...] += src_ref[...]
        else:
            dst_ref[...] = src_ref[...]
    else:
        for r in range(d):
            blk = src_ref[pl.ds(r * length, length), :]
            if accumulate:
                dst_ref[pl.ds(r, length, stride=d), :] = dst_ref[pl.ds(r, length, stride=d), :] + blk
            else:
                dst_ref[pl.ds(r, length, stride=d), :] = blk


def _attn_masks():
    qi = lax.broadcasted_iota(jnp.int32, (ATTN_BLOCK, ATTN_BLOCK), 0)
    ki = lax.broadcasted_iota(jnp.int32, (ATTN_BLOCK, ATTN_BLOCK), 1)
    low = lax.broadcasted_iota(jnp.int32, (ATTN_BLOCK, LANES), 1) < HEAD_DIM
    return ki <= qi, ki >= qi, low


NEG_INF = float("-inf")


N_BRANCH = len(DILATIONS)


def _attn_prep(qkv, tabs, name):
    t = qkv.shape[0]
    nb = t // SEQ
    n_j = ATTN_WIDTH // LANES

    def q_body(q_ref, c_ref, s1_ref, s2_ref, out_ref, xr):
        xr[...] = _rot(q_ref[...], c_ref[...], s1_ref[...], s2_ref[...]) * (HEAD_DIM ** -0.5)
        for bi, d in enumerate(DILATIONS):
            _deinterleave(xr, out_ref.at[bi], d, bf16)

    def kv_body(x_ref, c_ref, s1_ref, s2_ref, out_ref, xr):
        lowfull = lax.broadcasted_iota(jnp.int32, (SEQ, LANES), 1) < HEAD_DIM
        x = x_ref[...]
        x = jnp.where(pl.program_id(1) == 0, _rot(x, c_ref[...], s1_ref[...], s2_ref[...]), x)
        for kvh in range(N_KV_HEADS):
            xr[...] = _dup_head(x, kvh, lowfull)
            for bi, d in enumerate(DILATIONS):
                length = SEQ // d
                for r in range(d):
                    rows = xr[...] if d == 1 else xr[pl.ds(r, length, stride=d), :]
                    out_ref[0, bi, pl.ds(r * length, length), kvh * LANES:(kvh + 1) * LANES] = rows.astype(bf16)

    tab = pl.BlockSpec((SEQ, LANES), lambda b, j: (b, 0))
    q = pl.pallas_call(
        q_body, name=name + "_q", grid=(nb, n_j),
        in_specs=[pl.BlockSpec((SEQ, LANES), lambda b, j: (b, j)), tab, tab, tab],
        out_specs=pl.BlockSpec((N_BRANCH, SEQ, LANES), lambda b, j: (0, b, j)),
        out_shape=jax.ShapeDtypeStruct((N_BRANCH, t, ATTN_WIDTH), bf16), scratch_shapes=[pltpu.VMEM((SEQ, LANES), f32)],
        compiler_params=_cparams(("parallel", "parallel")),
    )(qkv, *tabs)
    kv = pl.pallas_call(
        kv_body, name=name + "_kv", grid=(nb, 2),
        in_specs=[pl.BlockSpec((SEQ, LANES), lambda b, j: (b, n_j + j)), tab, tab, tab],
        out_specs=pl.BlockSpec((1, N_BRANCH, SEQ, N_KV_HEADS * LANES), lambda b, j: (j, 0, b, 0)),
        out_shape=jax.ShapeDtypeStruct((2, N_BRANCH, t, N_KV_HEADS * LANES), bf16), scratch_shapes=[pltpu.VMEM((SEQ, LANES), f32)],
        compiler_params=_cparams(("parallel", "parallel")),
    )(qkv, *tabs)
    return q, kv


def _attn_fwd(prep, name):
    q_all, kv_all = prep
    t = q_all.shape[1]
    nb = t // SEQ
    n_blk = SEQ // ATTN_BLOCK

    def body(q_ref, k_ref, v_ref, o_ref, lse_ref, ob, lb, o0, o1, o2, l0, l1, l2, ss):
        cur_ok, prev_ok, low = _attn_masks()
        onat, lnat = (o0, o1, o2), (l0, l1, l2)
        for bi, d in enumerate(DILATIONS):
            qd, kd, vd = q_ref.at[bi], k_ref.at[0, bi], v_ref.at[0, bi]
            per_res = n_blk // d

            def scores(n):
                cur, prev = pl.ds(n * ATTN_BLOCK, ATTN_BLOCK), pl.ds(max(n - 1, 0) * ATTN_BLOCK, ATTN_BLOCK)
                has_prev = n % per_res != 0
                qb = qd[cur, :]
                kc = kd[cur, :]
                if has_prev:
                    kp = kd[prev, :]
                for a in range(2):
                    qa = jnp.where(low if a == 0 else ~low, qb, jnp.zeros_like(qb))
                    ss[2 * n + a, :, 0:ATTN_BLOCK] = jnp.where(cur_ok, _nt(qa, kc), NEG_INF)
                    if has_prev:
                        ss[2 * n + a, :, ATTN_BLOCK:2 * ATTN_BLOCK] = jnp.where(prev_ok, _nt(qa, kp), NEG_INF)

            def softmax_pv(n):
                cur, prev = pl.ds(n * ATTN_BLOCK, ATTN_BLOCK), pl.ds(max(n - 1, 0) * ATTN_BLOCK, ATTN_BLOCK)
                has_prev = n % per_res != 0
                vc = vd[cur, :]
                if has_prev:
                    vp = vd[prev, :]
                outs, lses = [], []
                for a in range(2):
                    sc = ss[2 * n + a, :, 0:ATTN_BLOCK]
                    if has_prev:
                        sp = ss[2 * n + a, :, ATTN_BLOCK:2 * ATTN_BLOCK]
                        m = jnp.max(jnp.maximum(sc, sp), axis=1, keepdims=True)
                        pc, pp = jnp.exp(sc - m), jnp.exp(sp - m)
                        den = jnp.sum(pc + pp, axis=1, keepdims=True)
                        acc = _nn(pc.astype(bf16), vc) + _nn(pp.astype(bf16), vp)
                    else:
                        m = jnp.max(sc, axis=1, keepdims=True)
                        pc = jnp.exp(sc - m)
                        den = jnp.sum(pc, axis=1, keepdims=True)
                        acc = _nn(pc.astype(bf16), vc)
                    outs.append(acc * (1.0 / den))
                    lses.append(m + jnp.log(den))
                ob[cur, :] = jnp.where(low, outs[0], outs[1])
                lb[cur, :] = jnp.where(low, lses[0], lses[1])

            for n in range(n_blk):
                scores(n)
            for n in range(n_blk):
                softmax_pv(n)
            _interleave_store(ob, onat[bi], d, False)
            _interleave_store(lb, lnat[bi], d, False)
        la, lbb, lc = l0[...], l1[...], l2[...]
        lm = jnp.maximum(jnp.maximum(la, lbb), lc)
        wa, wb, wc = jnp.exp(la - lm), jnp.exp(lbb - lm), jnp.exp(lc - lm)
        ws = wa + wb + wc
        o_ref[...] = (wa * o0[...] + wb * o1[...] + wc * o2[...]) / ws
        lse_ref[...] = lm + jnp.log(ws)

    def col(jj):
        return pl.BlockSpec((SEQ, LANES), lambda b, j: (b, jj if jj is not None else j))

    fs = pltpu.VMEM((SEQ, LANES), f32)
    return pl.pallas_call(
        body, name=name, grid=(nb, ATTN_WIDTH // LANES),
        in_specs=[pl.BlockSpec((N_BRANCH, SEQ, LANES), lambda b, j: (0, b, j)),
                  pl.BlockSpec((1, N_BRANCH, SEQ, LANES), lambda b, j: (0, 0, b, j // 2)),
                  pl.BlockSpec((1, N_BRANCH, SEQ, LANES), lambda b, j: (1, 0, b, j // 2))],
        out_specs=[col(None), col(None)],
        out_shape=[jax.ShapeDtypeStruct((t, ATTN_WIDTH), f32), jax.ShapeDtypeStruct((t, ATTN_WIDTH), f32)],
        scratch_shapes=[fs, fs, fs, fs, fs, fs, fs, fs, pltpu.VMEM((2 * n_blk, ATTN_BLOCK, 2 * ATTN_BLOCK), f32)],
        compiler_params=_cparams(("parallel", "parallel")),
    )(q_all, kv_all, kv_all)


def _attn_bwd(prep, tabs, o, lse, do, name):
    q_all, kv_all = prep
    t = q_all.shape[1]
    nb = t // SEQ
    n_blk = SEQ // ATTN_BLOCK
    n_j = ATTN_WIDTH // LANES

    def body(q_ref, k_ref, v_ref, c_ref, s1_ref, s2_ref, o_ref, lse_ref, do_ref, dq_ref, dk_ref, dv_ref,
             stat, dod, std, dqd, dkd, dvd, dqa, dka, dva, pb, dsb, dk_acc, dv_acc):
        j = pl.program_id(1)
        kvh = j // 2
        cur_ok, prev_ok, low = _attn_masks()
        lane = lax.broadcasted_iota(jnp.int32, (SEQ, LANES), 1)
        lowfull = lane < HEAD_DIM
        c, s1, s2 = c_ref[...], s1_ref[...], s2_ref[...]
        prod = do_ref[...] * o_ref[...]
        d_lo = jnp.sum(jnp.where(lowfull, prod, 0.0), axis=1, keepdims=True)
        d_hi = jnp.sum(jnp.where(lowfull, 0.0, prod), axis=1, keepdims=True)
        stat[...] = jnp.where(lane % HEAD_DIM < HEAD_DIM // 2, lse_ref[...], jnp.where(lowfull, d_lo, d_hi))
        dqa[...] = jnp.zeros_like(dqa)
        dka[...] = jnp.zeros_like(dka)
        dva[...] = jnp.zeros_like(dva)
        for bi, d in enumerate(DILATIONS):
            qd, kd, vd = q_ref.at[bi], k_ref.at[0, bi], v_ref.at[0, bi]
            _deinterleave(do_ref, dod, d, bf16)
            _deinterleave(stat, std, d, f32)
            per_res = n_blk // d
            curl, prevl = slice(0, ATTN_BLOCK), slice(ATTN_BLOCK, 2 * ATTN_BLOCK)

            def halves(x):
                zero = jnp.zeros_like(x)
                return jnp.where(low, x, zero), jnp.where(low, zero, x)

            def blk(n):
                return pl.ds(n * ATTN_BLOCK, ATTN_BLOCK)

            def has_prev(n):
                return n < n_blk and n % per_res != 0

            def probs(n):
                cur = blk(n)
                qas, doas = halves(qd[cur, :]), halves(dod[cur, :])
                kc, vc = kd[cur, :], vd[cur, :]
                if has_prev(n):
                    kp, vp = kd[blk(n - 1), :], vd[blk(n - 1), :]
                stb = std[cur, :]
                for a in range(2):
                    ls = stb[:, a * HEAD_DIM:a * HEAD_DIM + 1]
                    de = stb[:, a * HEAD_DIM + HEAD_DIM // 2:a * HEAD_DIM + HEAD_DIM // 2 + 1]
                    pc = jnp.exp(jnp.where(cur_ok, _nt(qas[a], kc), NEG_INF) - ls)
                    pb[2 * n + a, :, curl] = pc.astype(bf16)
                    dsb[2 * n + a, :, curl] = (pc * (_nt(doas[a], vc) - de)).astype(bf16)
                    if has_prev(n):
                        pp = jnp.exp(jnp.where(prev_ok, _nt(qas[a], kp), NEG_INF) - ls)
                        pb[2 * n + a, :, prevl] = pp.astype(bf16)
                        dsb[2 * n + a, :, prevl] = (pp * (_nt(doas[a], vp) - de)).astype(bf16)

            def grads(n):
                cur = blk(n)
                kc = kd[cur, :]
                dqs = [_nn(dsb[2 * n + a, :, curl], kc) for a in range(2)]
                q_rows, do_rows = list(halves(qd[cur, :])), list(halves(dod[cur, :]))
                ds_rows, p_rows = [dsb[2 * n + a, :, curl] for a in range(2)], [pb[2 * n + a, :, curl] for a in range(2)]
                if has_prev(n):
                    kp = kd[blk(n - 1), :]
                    dqs = [dqs[a] + _nn(dsb[2 * n + a, :, prevl], kp) for a in range(2)]
                if has_prev(n + 1):
                    q_rows += list(halves(qd[blk(n + 1), :]))
                    do_rows += list(halves(dod[blk(n + 1), :]))
                    ds_rows += [dsb[2 * n + 2 + a, :, prevl] for a in range(2)]
                    p_rows += [pb[2 * n + 2 + a, :, prevl] for a in range(2)]
                dqd[cur, :] = jnp.where(low, dqs[0], dqs[1])
                dkd[cur, :] = _tn(jnp.concatenate(ds_rows, axis=0), jnp.concatenate(q_rows, axis=0))
                dvd[cur, :] = _tn(jnp.concatenate(p_rows, axis=0), jnp.concatenate(do_rows, axis=0))

            for n in range(n_blk):
                probs(n)
            for n in range(n_blk):
                grads(n)
            _interleave_store(dqd, dqa, d, True)
            _interleave_store(dkd, dka, d, True)
            _interleave_store(dvd, dva, d, True)
        dq_ref[...] = _rot_t(dqa[...] * (HEAD_DIM ** -0.5), c, s1, s2).astype(bf16)
        dkf = dka[...]
        dkf = _rot_t(dkf + pltpu.roll(dkf, HEAD_DIM, 1), c, s1, s2)
        dvf = dva[...]
        dvf = dvf + pltpu.roll(dvf, HEAD_DIM, 1)
        mine = (lax.broadcasted_iota(jnp.int32, (SEQ, LANES), 1) // HEAD_DIM) == kvh
        dkc_, dvc_ = jnp.where(mine, dkf, 0.0), jnp.where(mine, dvf, 0.0)

        @pl.when(j == 0)
        def _():
            dk_acc[...] = dkc_
            dv_acc[...] = dvc_

        @pl.when(j > 0)
        def _():
            dk_acc[...] += dkc_
            dv_acc[...] += dvc_

        @pl.when(j == n_j - 1)
        def _():
            dk_ref[...] = dk_acc[...].astype(bf16)
            dv_ref[...] = dv_acc[...].astype(bf16)

    def col(jj):
        return pl.BlockSpec((SEQ, LANES), lambda b, j: (b, jj if jj is not None else j))

    tab = pl.BlockSpec((SEQ, LANES), lambda b, j: (b, 0))
    fs = pltpu.VMEM((SEQ, LANES), f32)
    hs = pltpu.VMEM((SEQ, LANES), bf16)
    return pl.pallas_call(
        body, name=name, grid=(nb, n_j),
        in_specs=[pl.BlockSpec((N_BRANCH, SEQ, LANES), lambda b, j: (0, b, j)),
                  pl.BlockSpec((1, N_BRANCH, SEQ, LANES), lambda b, j: (0, 0, b, j // 2)),
                  pl.BlockSpec((1, N_BRANCH, SEQ, LANES), lambda b, j: (1, 0, b, j // 2)),
                  tab, tab, tab, col(None), col(None), col(None)],
        out_specs=[col(None), tab, tab],
        out_shape=[jax.ShapeDtypeStruct((t, ATTN_WIDTH), bf16), jax.ShapeDtypeStruct((t, LANES), bf16), jax.ShapeDtypeStruct((t, LANES), bf16)],
        scratch_shapes=[fs, hs, fs, fs, fs, fs, fs, fs, fs,
                        pltpu.VMEM((2 * n_blk, ATTN_BLOCK, 2 * ATTN_BLOCK), bf16), pltpu.VMEM((2 * n_blk, ATTN_BLOCK, 2 * ATTN_BLOCK), bf16), fs, fs],
        compiler_params=_cparams(("parallel", "arbitrary")),
    )(q_all, kv_all, kv_all, *tabs, o, lse, do)


def _tap(w_ref, s):
    return w_ref[CONV_WIDTH - 1 - s:CONV_WIDTH - s, :]


def _conv_pre(x, w_ref, b_ref, row):
    shifted = [x] + [jnp.where(row >= s, pltpu.roll(x, s, 0), 0.0) for s in range(1, CONV_WIDTH)]
    pre = b_ref[...] + _tap(w_ref, 0) * x
    for s in range(1, CONV_WIDTH):
        pre = pre + _tap(w_ref, s) * shifted[s]
    return pre, shifted


def _conv_fwd(x, w, b, name, tc=512):
    t, ch = x.shape

    def body(x_ref, w_ref, b_ref, o_ref):
        row = lax.broadcasted_iota(jnp.int32, (SEQ, tc), 0)
        pre, _ = _conv_pre(x_ref[...], w_ref, b_ref, row)
        o_ref[...] = _silu(pre)

    xs = pl.BlockSpec((SEQ, tc), lambda i, j: (i, j))
    return pl.pallas_call(
        body, name=name, grid=(t // SEQ, ch // tc),
        in_specs=[xs, pl.BlockSpec((CONV_WIDTH, tc), lambda i, j: (0, j)), pl.BlockSpec((1, tc), lambda i, j: (0, j))],
        out_specs=xs, out_shape=jax.ShapeDtypeStruct((t, ch), f32),
        compiler_params=_cparams(("parallel", "parallel")),
    )(x, w, b)


def _conv_bwd(x, w, b, dact, name, tc=512):
    t, ch = x.shape

    def body(x_ref, w_ref, b_ref, d_ref, dx_ref, dw_ref, db_ref):
        row = lax.broadcasted_iota(jnp.int32, (SEQ, tc), 0)
        pre, shifted = _conv_pre(x_ref[...], w_ref, b_ref, row)
        dpre = d_ref[...] * _dsilu(pre)
        dx = _tap(w_ref, 0) * dpre
        for s in range(1, CONV_WIDTH):
            dx = dx + _tap(w_ref, s) * jnp.where(row < SEQ - s, pltpu.roll(dpre, SEQ - s, 0), 0.0)
        dx_ref[...] = dx.astype(bf16)
        first = pl.program_id(1) == 0
        parts = [jnp.sum(dpre * shifted[CONV_WIDTH - 1 - k], axis=0, keepdims=True) for k in range(CONV_WIDTH)]
        dbp = jnp.sum(dpre, axis=0, keepdims=True)

        @pl.when(first)
        def _():
            for k in range(CONV_WIDTH):
                dw_ref[k:k + 1, :] = parts[k]
            db_ref[...] = dbp

        @pl.when(jnp.logical_not(first))
        def _():
            for k in range(CONV_WIDTH):
                dw_ref[k:k + 1, :] += parts[k]
            db_ref[...] += dbp

    xs = pl.BlockSpec((SEQ, tc), lambda j, i: (i, j))
    ws = pl.BlockSpec((CONV_WIDTH, tc), lambda j, i: (0, j))
    bs = pl.BlockSpec((1, tc), lambda j, i: (0, j))
    return pl.pallas_call(
        body, name=name, grid=(ch // tc, t // SEQ),
        in_specs=[xs, ws, bs, xs], out_specs=[xs, ws, bs],
        out_shape=[jax.ShapeDtypeStruct((t, ch), bf16), jax.ShapeDtypeStruct((CONV_WIDTH, ch), f32), jax.ShapeDtypeStruct((1, ch), f32)],
        compiler_params=_cparams(("parallel", "arbitrary")),
    )(x, w, b, dact)


GROUP_W = SSM_INNER // SSM_GROUPS
HEADS_PER_GROUP = SSM_HEADS // SSM_GROUPS
SSD_CHUNKS_PER_STEP = 4


def _split3(x):
    hi = x.astype(bf16)
    r1 = x - hi.astype(f32)
    mid = r1.astype(bf16)
    lo = (r1 - mid.astype(f32)).astype(bf16)
    return hi, mid, lo


def _dot_exact(x, sel, dims, x_is_lhs=True):
    parts = _split3(x)
    if x_is_lhs:
        return _dot(parts[0], sel, dims) + _dot(parts[1], sel, dims) + _dot(parts[2], sel, dims)
    return _dot(sel, parts[0], dims) + _dot(sel, parts[1], dims) + _dot(sel, parts[2], dims)


def _ssd_common(xbc_ref, dt_ref, bias_ref, alog_ref):
    r = lax.broadcasted_iota(jnp.int32, (CHUNK, CHUNK), 0)
    cidx = lax.broadcasted_iota(jnp.int32, (CHUNK, CHUNK), 1)
    causal = r >= cidx
    tril = causal.astype(bf16)
    expand = (lax.broadcasted_iota(jnp.int32, (CHUNK, SSM_INNER), 0)
              == lax.broadcasted_iota(jnp.int32, (CHUNK, SSM_INNER), 1) // HEAD_DIM).astype(bf16)
    head_lane = cidx < SSM_HEADS
    dtp = dt_ref[...] + bias_ref[...]
    dt = jnp.where(head_lane, _softplus(dtp), 0.0)
    a_neg = -jnp.exp(alog_ref[...])
    a = dt * a_neg
    nn_dims = ((1,), (0,))
    cs = _dot_exact(a, tril, nn_dims, x_is_lhs=False)
    dt_e = _dot_exact(dt, expand, nn_dims)
    cs_e = _dot_exact(cs, expand, nn_dims)
    xs = xbc_ref[:, 0:SSM_INNER]
    xg = xs * dt_e
    ecs = jnp.exp(cs_e)
    cs_last = cs_e[CHUNK - 1:CHUNK, :]
    dse = jnp.exp(cs_last - cs_e)
    cde = jnp.exp(cs_last)
    return dict(r=r, cidx=cidx, causal=causal, tril=tril, expand=expand, head_lane=head_lane, dtp=dtp, dt=dt, a_neg=a_neg,
                cs=cs, cst=cs.T, dt_e=dt_e, cs_e=cs_e, xs=xs, xg=xg, ecs=ecs, dse=dse, cde=cde)


def _decay_mat(q, h):
    return jnp.exp(jnp.where(q["causal"], q["cs"][:, h:h + 1] - q["cst"][h:h + 1, :], NEG_INF))


def _gate_norm(y, z, nw, gate=None):
    y2 = y * (_silu(z) if gate is None else gate)
    outs, xhats, rs = [], [], []
    for g in range(SSM_GROUPS):
        sl = slice(g * GROUP_W, (g + 1) * GROUP_W)
        yg = y2[:, sl]
        r = lax.rsqrt(jnp.mean(yg * yg, axis=-1, keepdims=True) + EPS)
        xhats.append(yg * r)
        rs.append(r)
        outs.append(yg * r * nw[:, sl])
    return y2, outs, xhats, rs


def _ssd_fwd(xbc, z, dtp, params, name):
    t = xbc.shape[0]
    n_chunk = SEQ // CHUNK
    n_step = n_chunk // SSD_CHUNKS_PER_STEP

    def body(xbc_ref, z_ref, dt_ref, bias_ref, alog_ref, dskip_ref, nw_ref, yn_ref, y_ref, hs_ref, h_scr):
        @pl.when(pl.program_id(1) == 0)
        def _():
            h_scr[...] = jnp.zeros_like(h_scr)

        for s in range(SSD_CHUNKS_PER_STEP):
            r = pl.ds(s * CHUNK, CHUNK)
            one_chunk(xbc_ref.at[r], z_ref.at[r], dt_ref.at[r], bias_ref, alog_ref, dskip_ref, nw_ref,
                      yn_ref.at[r], y_ref.at[r], hs_ref.at[pl.ds(s, 1)], h_scr)

    def one_chunk(xbc_ref, z_ref, dt_ref, bias_ref, alog_ref, dskip_ref, nw_ref, yn_ref, y_ref, hs_ref, h_scr):
        q = _ssd_common(xbc_ref, dt_ref, bias_ref, alog_ref)
        low = lax.broadcasted_iota(jnp.int32, (CHUNK, LANES), 1) < HEAD_DIM
        xgb = q["xg"].astype(bf16)
        wst = (q["xg"] * q["dse"]).astype(bf16)
        hs_ref[0] = h_scr[...]
        ys = []
        for g in range(SSM_GROUPS):
            gl = slice(g * GROUP_W, (g + 1) * GROUP_W)
            bg = xbc_ref[:, SSM_INNER + g * D_STATE:SSM_INNER + (g + 1) * D_STATE].astype(bf16)
            cg = xbc_ref[:, SSM_INNER + SSM_GROUPS * D_STATE + g * D_STATE:SSM_INNER + SSM_GROUPS * D_STATE + (g + 1) * D_STATE].astype(bf16)
            cb = _nt(cg, bg)
            hg = h_scr[g]
            yoff = _nn(cg, hg.astype(bf16)) * q["ecs"][:, gl]
            pieces = []
            for i in range(HEADS_PER_GROUP // 2):
                h0 = g * HEADS_PER_GROUP + 2 * i
                xp = xgb[:, h0 * HEAD_DIM:(h0 + 2) * HEAD_DIM]
                m0 = (cb * _decay_mat(q, h0)).astype(bf16)
                m1 = (cb * _decay_mat(q, h0 + 1)).astype(bf16)
                zero = jnp.zeros_like(xp)
                pieces.append(_nn(m0, jnp.where(low, xp, zero)) + _nn(m1, jnp.where(low, zero, xp)))
            ys.append(jnp.concatenate(pieces, axis=1) + yoff + dskip_ref[:, gl] * q["xs"][:, gl])
            h_scr[g] = hg * q["cde"][:, gl] + _tn(bg, wst[:, gl])
        y = jnp.concatenate(ys, axis=1)
        y_ref[...] = y
        _, outs, _, _ = _gate_norm(y, z_ref[...], nw_ref[...])
        yn_ref[...] = jnp.concatenate(outs, axis=1).astype(bf16)

    def rows(w):
        return pl.BlockSpec((SSD_CHUNKS_PER_STEP * CHUNK, w), lambda b, c: (b * n_step + c, 0))

    def par(w):
        return pl.BlockSpec((1, w), lambda b, c: (0, 0))

    return pl.pallas_call(
        body, name=name, grid=(t // SEQ, n_step),
        in_specs=[rows(CONV_CH), rows(SSM_INNER), rows(LANES), par(LANES), par(LANES), par(SSM_INNER), par(SSM_INNER)],
        out_specs=[rows(SSM_INNER), rows(SSM_INNER),
                   pl.BlockSpec((SSD_CHUNKS_PER_STEP, SSM_GROUPS, D_STATE, GROUP_W), lambda b, c: (b * n_step + c, 0, 0, 0))],
        out_shape=[jax.ShapeDtypeStruct((t, SSM_INNER), bf16), jax.ShapeDtypeStruct((t, SSM_INNER), f32),
                   jax.ShapeDtypeStruct((t // CHUNK, SSM_GROUPS, D_STATE, GROUP_W), f32)],
        scratch_shapes=[pltpu.VMEM((SSM_GROUPS, D_STATE, GROUP_W), f32)],
        compiler_params=_cparams(("parallel", "arbitrary")),
    )(xbc, z, dtp, *params)


def _ssd_bwd(xbc, z, dtp, y, hs, dyn, params, name):
    t = xbc.shape[0]
    n_chunk = SEQ // CHUNK
    n_step = n_chunk // SSD_CHUNKS_PER_STEP

    def body(xbc_ref, z_ref, dt_ref, y_ref, hs_ref, dyn_ref, bias_ref, alog_ref, dskip_ref, nw_ref,
             dxbc_ref, dz_ref, ddt_ref, dnw_ref, dds_ref, dal_ref, dbi_ref, dh_scr):
        @pl.when(pl.program_id(1) == 0)
        def _():
            dh_scr[...] = jnp.zeros_like(dh_scr)

        first_step = (pl.program_id(0) == 0) & (pl.program_id(1) == 0)
        for s in reversed(range(SSD_CHUNKS_PER_STEP)):
            r = pl.ds(s * CHUNK, CHUNK)
            one_chunk(xbc_ref.at[r], z_ref.at[r], dt_ref.at[r], y_ref.at[r], hs_ref.at[pl.ds(s, 1)], dyn_ref.at[r],
                      bias_ref, alog_ref, dskip_ref, nw_ref, dxbc_ref.at[r], dz_ref.at[r], ddt_ref.at[r],
                      dnw_ref, dds_ref, dal_ref, dbi_ref, dh_scr, first_step if s == SSD_CHUNKS_PER_STEP - 1 else False)

    def one_chunk(xbc_ref, z_ref, dt_ref, y_ref, hs_ref, dyn_ref, bias_ref, alog_ref, dskip_ref, nw_ref,
                  dxbc_ref, dz_ref, ddt_ref, dnw_ref, dds_ref, dal_ref, dbi_ref, dh_scr, first):
        q = _ssd_common(xbc_ref, dt_ref, bias_ref, alog_ref)
        low = lax.broadcasted_iota(jnp.int32, (CHUNK, LANES), 1) < HEAD_DIM
        last_row = lax.broadcasted_iota(jnp.int32, (CHUNK, GROUP_W), 0) == CHUNK - 1
        xs, xg = q["xs"], q["xg"]
        xgb = xg.astype(bf16)
        wf = xg * q["dse"]
        wst = wf.astype(bf16)
        zz = z_ref[...]
        yy = y_ref[...]
        sz, dsz = _silu_and_grad(zz)
        y2, _, xhats, rs = _gate_norm(yy, zz, nw_ref[...], gate=sz)
        dyn_ = dyn_ref[...]
        dy2s, dnws = [], []
        for g in range(SSM_GROUPS):
            gl = slice(g * GROUP_W, (g + 1) * GROUP_W)
            gw = dyn_[:, gl] * nw_ref[:, gl]
            dy2s.append(rs[g] * (gw - xhats[g] * jnp.mean(gw * xhats[g], axis=-1, keepdims=True)))
            dnws.append(_rowsum8(dyn_[:, gl] * xhats[g]))
        dy2 = jnp.concatenate(dy2s, axis=1)
        dy = dy2 * sz
        dz_ref[...] = (dy2 * yy * dsz).astype(bf16)
        dnw_p = jnp.concatenate(dnws, axis=1)
        dds_p = _rowsum8(dy * xs)
        dyb = dy.astype(bf16)
        gfull = (dy * q["ecs"]).astype(bf16)
        dcs_c = jnp.zeros((CHUNK, CHUNK), f32)
        dcs_r = jnp.zeros((CHUNK, CHUNK), f32)
        dcs_e_parts, dxg_parts = [], []
        for g in range(SSM_GROUPS):
            gl = slice(g * GROUP_W, (g + 1) * GROUP_W)
            bsl = slice(SSM_INNER + g * D_STATE, SSM_INNER + (g + 1) * D_STATE)
            csl = slice(SSM_INNER + SSM_GROUPS * D_STATE + g * D_STATE, SSM_INNER + SSM_GROUPS * D_STATE + (g + 1) * D_STATE)
            bg = xbc_ref[:, bsl].astype(bf16)
            cg = xbc_ref[:, csl].astype(bf16)
            cb = _nt(cg, bg)
            hg = hs_ref[0, g]
            hgb = hg.astype(bf16)
            dhn = dh_scr[g]
            dhnb = dhn.astype(bf16)
            yoff = _nn(cg, hgb) * q["ecs"][:, gl]
            dw_ = _nn(bg, dhnb)
            r_e = dw_ * wf[:, gl]
            to_last = jnp.sum(r_e, axis=0, keepdims=True) + jnp.sum(dhn * hg, axis=0, keepdims=True) * q["cde"][:, gl]
            dcs_e_parts.append(dy[:, gl] * yoff - r_e + jnp.where(last_row, to_last, 0.0))
            dcb = jnp.zeros((CHUNK, CHUNK), f32)
            dxg_pairs = []
            for i in range(HEADS_PER_GROUP // 2):
                h0 = g * HEADS_PER_GROUP + 2 * i
                psl = slice(h0 * HEAD_DIM, (h0 + 2) * HEAD_DIM)
                xp = xgb[:, psl]
                dyp = dyb[:, psl]
                zero = jnp.zeros_like(dyp)
                tns = []
                for a in range(2):
                    h = h0 + a
                    lm = _decay_mat(q, h)
                    m = cb * lm
                    dm = _nt(jnp.where(low, dyp, zero) if a == 0 else jnp.where(low, zero, dyp), xp)
                    dcb = dcb + dm * lm
                    nmat = dm * m
                    dcs_c = dcs_c + jnp.where(q["cidx"] == h, jnp.sum(nmat, axis=1, keepdims=True), 0.0)
                    dcs_r = dcs_r + jnp.where(q["r"] == h, jnp.sum(nmat, axis=0, keepdims=True), 0.0)
                    tns.append(_tn(m.astype(bf16), dyp))
                dxg_pairs.append(jnp.where(low, tns[0], tns[1]))
            dxg_parts.append(jnp.concatenate(dxg_pairs, axis=1) + dw_ * q["dse"][:, gl])
            dcbb = dcb.astype(bf16)
            dxbc_ref[:, csl] = _nt(gfull[:, gl], hgb) + _nn(dcbb, bg)
            dxbc_ref[:, bsl] = _nt(wst[:, gl], dhnb) + _tn(dcbb, cg)
            dh_scr[g] = dhn * q["cde"][:, gl] + _tn(cg, gfull[:, gl])
        dxg = jnp.concatenate(dxg_parts, axis=1)
        dcs_e = jnp.concatenate(dcs_e_parts, axis=1)
        dxbc_ref[:, 0:SSM_INNER] = dskip_ref[...] * dy + dxg * q["dt_e"]
        dcs = dcs_c - dcs_r.T + _dot_exact(dcs_e, q["expand"], ((1,), (1,)))
        triu = (q["cidx"] >= q["r"]).astype(bf16)
        da = _dot_exact(dcs, triu, ((1,), (0,)), x_is_lhs=False)
        ddt = _dot_exact(dxg * xs, q["expand"], ((1,), (1,))) + da * q["a_neg"]
        ddtp = jnp.where(q["head_lane"], ddt * _sigmoid(q["dtp"]), 0.0)
        ddt_ref[...] = ddtp.astype(bf16)
        dal_p = _rowsum8(da * q["dt"]) * q["a_neg"]
        dbi_p = _rowsum8(ddtp)
        def accumulate():
            dnw_ref[...] += dnw_p
            dds_ref[...] += dds_p
            dal_ref[...] += dal_p
            dbi_ref[...] += dbi_p

        if first is False:
            accumulate()
        else:
            @pl.when(first)
            def _():
                dnw_ref[...] = dnw_p
                dds_ref[...] = dds_p
                dal_ref[...] = dal_p
                dbi_ref[...] = dbi_p

            pl.when(jnp.logical_not(first))(accumulate)

    def rows(w):
        return pl.BlockSpec((SSD_CHUNKS_PER_STEP * CHUNK, w), lambda b, c: (b * n_step + n_step - 1 - c, 0))

    def par(w):
        return pl.BlockSpec((1, w), lambda b, c: (0, 0))

    def acc(w):
        return pl.BlockSpec((SUBLANES, w), lambda b, c: (0, 0))

    return pl.pallas_call(
        body, name=name, grid=(t // SEQ, n_step),
        in_specs=[rows(CONV_CH), rows(SSM_INNER), rows(LANES), rows(SSM_INNER),
                  pl.BlockSpec((SSD_CHUNKS_PER_STEP, SSM_GROUPS, D_STATE, GROUP_W), lambda b, c: (b * n_step + n_step - 1 - c, 0, 0, 0)),
                  rows(SSM_INNER), par(LANES), par(LANES), par(SSM_INNER), par(SSM_INNER)],
        out_specs=[rows(CONV_CH), rows(SSM_INNER), rows(LANES), acc(SSM_INNER), acc(SSM_INNER), acc(LANES), acc(LANES)],
        out_shape=[jax.ShapeDtypeStruct((t, CONV_CH), f32), jax.ShapeDtypeStruct((t, SSM_INNER), bf16), jax.ShapeDtypeStruct((t, LANES), bf16),
                   jax.ShapeDtypeStruct((SUBLANES, SSM_INNER), f32), jax.ShapeDtypeStruct((SUBLANES, SSM_INNER), f32),
                   jax.ShapeDtypeStruct((SUBLANES, LANES), f32), jax.ShapeDtypeStruct((SUBLANES, LANES), f32)],
        scratch_shapes=[pltpu.VMEM((SSM_GROUPS, D_STATE, GROUP_W), f32)],
        compiler_params=_cparams(("arbitrary", "arbitrary")),
    )(xbc, z, dtp, y, hs, dyn, *params)


def _adamw_update(g, w, m, v):
    mm = ADAM_B1 * m + (1.0 - ADAM_B1) * g
    vv = ADAM_B2 * v + (1.0 - ADAM_B2) * (g * g)
    m_hat = mm / (1.0 - ADAM_B1 ** ADAM_STEP)
    v_hat = vv / (1.0 - ADAM_B2 ** ADAM_STEP)
    return -ADAM_LR * (m_hat / (jnp.sqrt(v_hat) + ADAM_EPS) + ADAM_WD * w), mm, vv


def _adamw(g_parts, w, m, v, name):
    rows, width = w.shape
    n = len(g_parts)
    tr = _row_tile(rows)

    def body(*refs):
        g_refs, (w_ref, m_ref, v_ref, g_out, d_out, m_out, v_out) = refs[:n], refs[n:]
        g = g_refs[0][...].astype(f32)
        for r in g_refs[1:]:
            g = g + r[...].astype(f32)
        g_out[...] = g
        d_out[...], m_out[...], v_out[...] = _adamw_update(g, w_ref[...], m_ref[...], v_ref[...])

    spec = pl.BlockSpec((tr, width), lambda i: (i, 0))
    return pl.pallas_call(
        body, name=name, grid=(rows // tr,), in_specs=[spec] * (n + 3), out_specs=[spec] * 4,
        out_shape=[jax.ShapeDtypeStruct((rows, width), f32)] * 4, compiler_params=_cparams(("parallel",)),
    )(*g_parts, w, m, v)


def _adamw_layers(landed, w, m, v, after, name, layers_on_columns=False):
    depth = len(landed)
    _, rows, width = landed[0].shape
    tr = _row_tile(rows)
    n_i = rows // tr
    at = (lambda ref: ref) if layers_on_columns else (lambda ref: ref.at[0])

    def body(*refs):
        part_refs, (w_ref, m_ref, v_ref, _, g_out, d_out, m_out, v_out) = refs[:depth * N_DEV], refs[depth * N_DEV:]
        for l in range(depth):
            @pl.when(pl.program_id(0) == l)
            def _(l=l):
                g = part_refs[l * N_DEV][0].astype(f32)
                for r in part_refs[l * N_DEV + 1:(l + 1) * N_DEV]:
                    g = g + r[0].astype(f32)
                at(g_out)[...] = g
                at(d_out)[...], at(m_out)[...], at(v_out)[...] = _adamw_update(g, at(w_ref)[...], at(m_ref)[...], at(v_ref)[...])

    def part_spec(l, p):
        return pl.BlockSpec((1, tr, width), lambda ll, i: (p, jnp.where(ll == l, i, jnp.where(ll < l, 0, n_i - 1)), 0))

    state = (pl.BlockSpec((tr, width), lambda ll, i: (i, ll)) if layers_on_columns
             else pl.BlockSpec((1, tr, width), lambda ll, i: (ll, i, 0)))
    return pl.pallas_call(
        body, name=name, grid=(depth, n_i),
        in_specs=[part_spec(l, p) for l in range(depth) for p in range(N_DEV)] + [state] * 3 + [ANY], out_specs=[state] * 4,
        out_shape=[jax.ShapeDtypeStruct(w.shape, f32)] * 4, compiler_params=_cparams(("arbitrary", "arbitrary")),
    )(*[landed[l] for l in range(depth) for _ in range(N_DEV)], w, m, v, after)


def _row_tile(rows, cap=512):
    for cand in range(min(rows, cap) // SUBLANES * SUBLANES, 0, -SUBLANES):
        if rows % cand == 0:
            return cand
    return rows


def _cols_from_devices(g, width, name):
    n_dev, depth, a, b = g.shape

    def body(g_ref, o_ref):
        for i in range(n_dev):
            o_ref[0, :, i * b:(i + 1) * b] = g_ref[i, 0]
        if width > n_dev * b:
            o_ref[0, :, n_dev * b:width] = jnp.zeros((a, width - n_dev * b), o_ref.dtype)

    return pl.pallas_call(
        body, name=name, grid=(depth,), in_specs=[pl.BlockSpec((n_dev, 1, a, b), lambda l: (0, l, 0, 0))],
        out_specs=pl.BlockSpec((1, a, width), lambda l: (l, 0, 0)), out_shape=jax.ShapeDtypeStruct((depth, a, width), g.dtype),
        compiler_params=_cparams(("parallel",)),
    )(g)


def _devices_from_cols(per_layer, b, name, tr=256):
    depth = len(per_layer)
    a, width = per_layer[0].shape

    def body(*refs):
        o_ref = refs[depth]
        for l in range(depth):
            for i in range(N_DEV):
                o_ref[i, l] = refs[l][:, i * b:(i + 1) * b]

    return pl.pallas_call(
        body, name=name, grid=(a // tr,), in_specs=[pl.BlockSpec((tr, width), lambda r: (r, 0))] * depth,
        out_specs=pl.BlockSpec((N_DEV, depth, tr, b), lambda r: (0, 0, r, 0)),
        out_shape=jax.ShapeDtypeStruct((N_DEV, depth, a, b), per_layer[0].dtype), compiler_params=_cparams(("parallel",)),
    )(*per_layer)


def _me():
    return lax.axis_index("x"), lax.axis_index("y"), lax.axis_index("c")


def _allgather_two_level(shards, name):
    n = len(shards)
    per = 7

    def body(*refs):
        ins, outs, token = refs[:n], refs[n:2 * n], refs[2 * n]
        send_sems, recv_sems, local_sems = refs[2 * n + 1:]
        token[...] = jnp.zeros_like(token)
        x, y, c = _me()
        me, sibling = (x, y, c), (x, y, 1 - c)
        chips = [(1 - x, y), (x, 1 - y), (1 - x, 1 - y)]

        def slot(a, p):
            return outs[a].at[4 * p[0] + 2 * p[1] + p[2]]

        def copy(a, k, block, to, src=None):
            return pltpu.make_async_remote_copy(
                src_ref=slot(a, block) if src is None else src, dst_ref=slot(a, block),
                send_sem=send_sems.at[a * per + k], recv_sem=recv_sems.at[a * per + k], device_id=to, device_id_type=MESH)

        mine = [pltpu.make_async_copy(ins[a], slot(a, me), local_sems.at[a]) for a in range(n)]
        for cp in mine:
            cp.start()
        first = []
        for a in range(n):
            first.append(copy(a, 0, me, sibling, src=ins[a]))
            first += [copy(a, 1 + j, me, (*chip, c), src=ins[a]) for j, chip in enumerate(chips)]
        for cp in first:
            cp.start()
        passed = []
        for j, chip in enumerate(chips):
            for a in range(n):
                copy(a, 1 + j, (*chip, c), me).wait_recv()
                fwd = copy(a, 4 + j, (*chip, c), sibling)
                fwd.start()
                passed.append(fwd)
        for a in range(n):
            copy(a, 0, sibling, me).wait_recv()
            for j, chip in enumerate(chips):
                copy(a, 4 + j, (*chip, 1 - c), me).wait_recv()
        for cp in first + passed:
            cp.wait_send()
        for cp in mine:
            cp.wait()

    outs = pl.pallas_call(
        body, name=name, in_specs=[ANY] * n, out_specs=[ANY] * n + [pl.BlockSpec(memory_space=pltpu.VMEM)],
        out_shape=[jax.ShapeDtypeStruct((N_DEV,) + s.shape, s.dtype) for s in shards] + [jax.ShapeDtypeStruct((SUBLANES, LANES), f32)],
        scratch_shapes=[pltpu.SemaphoreType.DMA((n * per,)), pltpu.SemaphoreType.DMA((n * per,)), pltpu.SemaphoreType.DMA((n,))],
    )(*shards)
    return outs[:n], outs[n]


def _allgather_direct(row, name):
    def body(in_ref, out_ref, send_sems, recv_sems, local_sem):
        x, y, c = _me()
        mine = out_ref.at[4 * x + 2 * y + c]
        local = pltpu.make_async_copy(in_ref, mine, local_sem)
        local.start()
        sends = []
        for k in range(1, N_DEV):
            px, py, pc = x ^ (k >> 2), y ^ ((k >> 1) & 1), c ^ (k & 1)
            sends.append(pltpu.make_async_remote_copy(
                src_ref=in_ref, dst_ref=mine, send_sem=send_sems.at[k - 1], recv_sem=recv_sems.at[k - 1],
                device_id=(px, py, pc), device_id_type=MESH))
        for cp in sends:
            cp.start()
        for k in range(1, N_DEV):
            px, py, pc = x ^ (k >> 2), y ^ ((k >> 1) & 1), c ^ (k & 1)
            theirs = out_ref.at[4 * px + 2 * py + pc]
            pltpu.make_async_remote_copy(
                src_ref=in_ref, dst_ref=theirs, send_sem=send_sems.at[k - 1], recv_sem=recv_sems.at[k - 1],
                device_id=(px, py, pc), device_id_type=MESH).wait_recv()
        for cp in sends:
            cp.wait_send()
        local.wait()

    return pl.pallas_call(
        body, name=name, in_specs=[ANY], out_specs=ANY, out_shape=jax.ShapeDtypeStruct((N_DEV,) + row.shape, row.dtype),
        scratch_shapes=[pltpu.SemaphoreType.DMA((N_DEV - 1,)), pltpu.SemaphoreType.DMA((N_DEV - 1,)), pltpu.SemaphoreType.DMA],
    )(row)


N_CHIP = N_DEV // 2
HBM = pl.BlockSpec(memory_space=pltpu.HBM)
SEM = pl.BlockSpec(memory_space=pltpu.SEMAPHORE)
EFFECT = pltpu.SideEffectType.DATAFLOW_SIDE_EFFECTING


def _peer(k):
    x, y, c = _me()
    return x ^ (k >> 2), y ^ ((k >> 1) & 1), c ^ (k & 1)


def _direct_copies(srcs, lands, send_sems, recv_sems, per_peer):
    x, y, c = _me()
    me = 4 * x + 2 * y + c
    copies = []
    for a in range(len(srcs)):
        for k in range(1, N_DEV):
            px, py, pc = _peer(k)
            piece = srcs[a].at[4 * px + 2 * py + pc] if per_peer else srcs[a]
            copies.append(pltpu.make_async_remote_copy(
                src_ref=piece, dst_ref=lands[a].at[me], send_sem=send_sems.at[a * (N_DEV - 1) + k - 1],
                recv_sem=recv_sems.at[a * (N_DEV - 1) + k - 1], device_id=(px, py, pc), device_id_type=MESH))
    return copies


def _direct_start(srcs, lands, per_peer, name):
    n = len(srcs)
    n_sem = n * (N_DEV - 1)

    def body(*refs):
        src_refs, land_refs = refs[:n], refs[n:2 * n]
        send_sems, recv_sems = refs[2 * n], refs[2 * n + 1]
        token = refs[-1]
        for cp in _direct_copies(src_refs, land_refs, send_sems, recv_sems, per_peer):
            cp.start()
        token[...] = jnp.zeros_like(token)

    outs = pl.pallas_call(
        body, name=name,
        out_shape=(pltpu.SemaphoreType.DMA((n_sem,)), pltpu.SemaphoreType.DMA((n_sem,)),
                   *[pltpu.HBM(s.shape, s.dtype) for s in srcs], *[pltpu.HBM(s.shape, s.dtype) for s in lands],
                   jax.ShapeDtypeStruct((SUBLANES, LANES), f32)),
        in_specs=[HBM] * (2 * n), out_specs=(SEM, SEM, *[HBM] * (2 * n), pl.BlockSpec(memory_space=pltpu.VMEM)),
        input_output_aliases={i: 2 + i for i in range(2 * n)},
        compiler_params=pltpu.CompilerParams(has_side_effects=EFFECT),
    )(*[pltpu.with_memory_space_constraint(s, pltpu.HBM) for s in srcs], *[pltpu.with_memory_space_constraint(s, pltpu.HBM) for s in lands])
    return outs[0], outs[1], outs[2:2 + n], outs[2 + n:2 + 2 * n], outs[-1]


def _direct_wait(send_sems, recv_sems, srcs, lands, after, per_peer, name):
    n = len(srcs)

    def body(*refs):
        src_refs, land_refs = refs[:n], refs[n:2 * n]
        s_sems, r_sems = refs[2 * n], refs[2 * n + 1]
        for cp in _direct_copies(src_refs, land_refs, s_sems, r_sems, per_peer):
            cp.wait_send()
            cp.wait_recv()

    outs = pl.pallas_call(
        body, name=name,
        out_shape=tuple(pltpu.HBM(s.shape, s.dtype) for s in list(srcs) + list(lands)),
        in_specs=[HBM] * (2 * n) + [SEM, SEM, ANY], out_specs=tuple([HBM] * (2 * n)),
        input_output_aliases={i: i for i in range(2 * n)},
        compiler_params=pltpu.CompilerParams(has_side_effects=EFFECT),
    )(*srcs, *lands, send_sems, recv_sems, after)
    return outs[n:]


def _row(v, width=None):
    v = v.reshape(1, -1).astype(f32)
    if width is not None and v.shape[1] < width:
        v = jnp.pad(v, ((0, 0), (0, width - v.shape[1])))
    return v


def _layer_params(p, l):
    return dict(
        norm_mix=_row(p["norm_mix"][l]), norm_ffn=_row(p["norm_ffn"][l]), conv_w=p["conv_w"][l], conv_b=_row(p["conv_b"][l]),
        ssd=(_row(p["dt_bias"][l], LANES), _row(p["a_log"][l], LANES), _row(jnp.repeat(p["d_skip"][l], HEAD_DIM)), _row(p["ssm_norm"][l])))


def _layer_fwd(h, w_in, rest, sp, tabs, l):
    tag = f"l{l}_"
    hn = _rmsnorm_fwd(h, sp["norm_mix"], tag + "norm_mix")
    qkv, z, xbc_pre = _in_proj(hn, w_in, (QKV_WIDTH, SSM_INNER, CONV_CH), tag + "proj")
    dtp = _matmul(hn, w_in, mode="nn", n_out=LANES, tn=LANES, b_off=DT_OFF // LANES, name=tag + "proj_dt")
    prep = _attn_prep(qkv, tabs, tag + "attn_prep")
    o, lse = _attn_fwd(prep, tag + "attn_fwd")
    xbc = _conv_fwd(xbc_pre, sp["conv_w"], sp["conv_b"], tag + "conv_fwd")
    yn, y, hs = _ssd_fwd(xbc, z, dtp, sp["ssd"], tag + "ssd_fwd")
    w_out, w_gate, w_up, w_down = rest(yn) if callable(rest) else rest
    h2 = _out_proj(o, yn, w_out, h, tag + "out_proj")
    hn2 = _rmsnorm_fwd(h2, sp["norm_ffn"], tag + "norm_ffn")
    g, u, act = _swiglu_fwd(hn2, w_gate, w_up, tag + "ffn_up")
    h3 = _matmul(act, w_down, mode="nn", tk=FFN_HIDDEN, add=h2, name=tag + "ffn_down")
    saved = dict(h=h, hn=hn, prep=prep, z=z, xbc_pre=xbc_pre, dtp=dtp, o=o, lse=lse, xbc=xbc, yn=yn, y=y, hs=hs, h2=h2, hn2=hn2, g=g, u=u, act=act,
                 rest=(w_out, w_gate, w_up, w_down))
    return h3, saved


def _layer_bwd(dh3_pair, s, big, sp, tabs, l, gd=f32, after_ffn=None):
    tag = f"l{l}_"
    dh3, dh3b = dh3_pair
    w_in, w_out, w_gate, w_up, w_down = big
    dg, du = _swiglu_bwd(dh3b, w_down, s["g"], s["u"], tag + "ffn_down_bwd")
    dw_down = _matmul(s["act"], dh3b, mode="tn", tm=1408, tn=512, tk=2048, out_dtype=gd, name=tag + "dw_down")
    dw_gate = _matmul(dg, s["hn2"], mode="tn", tm=1408, tn=512, tk=2048, out_dtype=gd, name=tag + "dw_gate")
    dw_up = _matmul(du, s["hn2"], mode="tn", tm=1408, tn=512, tk=2048, out_dtype=gd, name=tag + "dw_up")
    norm_ffn = sp["norm_ffn"] if after_ffn is None else sp["norm_ffn"] + after_ffn(dict(w_gate=dw_gate, w_up=dw_up, w_down=dw_down))
    dh2, dh2b, dnf = _nt_norm_bwd([(dg, w_gate), (du, w_up)], s["h2"], norm_ffn, dh3, tag + "ffn_up_bwd_norm", tk=1408, b_is_kd=True,
                                  vmem=VMEM_LIMIT_TWO_PAIRS)
    d_o = _matmul(dh2b, w_out, mode="nt", n_out=ATTN_WIDTH, tn=512, b_off=0, name=tag + "out_attn_bwd")
    dyn = _matmul(dh2b, w_out, mode="nt", n_out=SSM_INNER, tn=512, b_off=1, name=tag + "out_ssm_bwd")
    dw_out = jnp.concatenate([_matmul(s["o"], dh2b, mode="tn", tm=512, tn=512, tk=2048, out_dtype=gd, name=tag + "dw_out_attn"),
                              _matmul(s["yn"], dh2b, mode="tn", tm=512, tn=512, tk=2048, out_dtype=gd, name=tag + "dw_out_ssm")], axis=0)
    dxbc, dz, ddtp, dnw, dds, dal, dbi = _ssd_bwd(s["xbc"], s["z"], s["dtp"], s["y"], s["hs"], dyn, sp["ssd"], tag + "ssd_bwd")
    dxbc_pre, dconv_w, dconv_b = _conv_bwd(s["xbc_pre"], sp["conv_w"], sp["conv_b"], dxbc, tag + "conv_bwd")
    dq, dk, dv = _attn_bwd(s["prep"], tabs, s["o"], s["lse"], d_o, tag + "attn_bwd")
    dproj = jnp.concatenate([dq, dk, dv, dz, dxbc_pre, ddtp], axis=1)
    dw_in = _matmul(s["hn"], dproj, mode="tn", tm=512, tn=1152, tk=2048, out_dtype=gd, name=tag + "dw_in")
    res = _nt_norm_bwd([(dproj, w_in)], s["h"], sp["norm_mix"], dh2, tag + "proj_bwd_norm", tk=1152, bf16_copy=l > 0)
    dh, dhb, dnm = res if l > 0 else (res[0], None, res[1])
    grads = dict(
        norm_mix=dnm.sum(0), w_in=dw_in, conv_w=dconv_w, conv_b=dconv_b[0], dt_bias=dbi.sum(0)[:SSM_HEADS], a_log=dal.sum(0)[:SSM_HEADS],
        d_skip=dds.sum(0).reshape(SSM_HEADS, HEAD_DIM).sum(1), ssm_norm=dnw.sum(0), w_out=dw_out, norm_ffn=dnf.sum(0),
        w_gate=dw_gate, w_up=dw_up, w_down=dw_down)
    return (dh, dhb), grads


def _local_step(x, positions, target, p, bigs):
    tabs = _rope_tables(positions.reshape(-1, 1), "rope_tables")
    h = x
    saved, sps = [], []
    for l in range(DEPTH):
        sps.append(_layer_params(p, l))
        h, s = _layer_fwd(h, bigs[l][0], bigs[l][1:], sps[l], tabs, l)
        saved.append(s)
    dh, dhb, loss_parts, dfn = _final_loss(h, _row(p["final_norm"]), target, "final_loss")
    dh = (dh, dhb)
    layer_grads = [None] * DEPTH
    for l in reversed(range(DEPTH)):
        dh, layer_grads[l] = _layer_bwd(dh, saved[l], bigs[l], sps[l], tabs, l)
    grads = {k: [layer_grads[l][k] for l in range(DEPTH)] for k in layer_grads[0]}
    grads["final_norm"] = dfn.sum(0)
    return jnp.sum(loss_parts), dh[0], grads


BIG = ("w_in", "w_out", "w_gate", "w_up", "w_down")
REST = BIG[1:]
FFN = ("w_gate", "w_up", "w_down")
MIX = ("w_in", "w_out")
COL_SHARDED = ("w_in",)
TRANSPOSED = ("w_gate", "w_up")
SMALL = ("norm_mix", "conv_b", "dt_bias", "a_log", "d_skip", "ssm_norm", "norm_ffn", "final_norm")
WEIGHTS = ("norm_mix", "w_in", "conv_w", "conv_b", "dt_bias", "a_log", "d_skip", "ssm_norm", "w_out", "norm_ffn", "w_gate", "w_up", "w_down", "final_norm")
SMALL_ROWS = 88
CONVW_ROWS = 96
CONVW_SHARD_ROWS = 16


def _full_from_gathered(name, g, l):
    _, a, b = g.shape
    if name in COL_SHARDED:
        width = IN_PROJ_PAD if name == "w_in" else N_DEV * b
        return _cols_from_devices(g.reshape(N_DEV, 1, a, b), width, f"cols_l{l}_{name}").reshape(a, width)
    return g.reshape(N_DEV * a, b)


def _by_device(name, full, shard_shape, l):
    a, b = shard_shape
    if name in COL_SHARDED:
        return _devices_from_cols([full], b, f"devs_l{l}_{name}").reshape(N_CHIP, 2, a, b)
    return full.reshape(N_CHIP, 2, a, b)


def _pack_rows(parts, rows, width):
    flat = jnp.concatenate([q.reshape(-1) for q in parts])
    return jnp.pad(flat, (0, rows * width - flat.shape[0])).reshape(rows, width)


def _unpack(flat, like):
    out, off = [], 0
    for q in like:
        out.append(flat[off:off + q.size].reshape(q.shape))
        off += q.size
    return out


def kernel(x, positions, norm_mix, w_in, conv_w, conv_b, dt_bias, a_log, d_skip, ssm_norm, w_out, norm_ffn, w_gate, w_up, w_down, final_norm, loss_target, m_norm_mix, m_w_in, m_conv_w, m_conv_b, m_dt_bias, m_a_log, m_d_skip, m_ssm_norm, m_w_out, m_norm_ffn, m_w_gate, m_w_up, m_w_down, m_final_norm, v_norm_mix, v_w_in, v_conv_w, v_conv_b, v_dt_bias, v_a_log, v_d_skip, v_ssm_norm, v_w_out, v_norm_ffn, v_w_gate, v_w_up, v_w_down, v_final_norm):
    w = dict(norm_mix=norm_mix, w_in=w_in, conv_w=conv_w, conv_b=conv_b, dt_bias=dt_bias, a_log=a_log, d_skip=d_skip, ssm_norm=ssm_norm,
             w_out=w_out, norm_ffn=norm_ffn, w_gate=w_gate, w_up=w_up, w_down=w_down, final_norm=final_norm)
    m = dict(norm_mix=m_norm_mix, w_in=m_w_in, conv_w=m_conv_w, conv_b=m_conv_b, dt_bias=m_dt_bias, a_log=m_a_log, d_skip=m_d_skip,
             ssm_norm=m_ssm_norm, w_out=m_w_out, norm_ffn=m_norm_ffn, w_gate=m_w_gate, w_up=m_w_up, w_down=m_w_down, final_norm=m_final_norm)
    v = dict(norm_mix=v_norm_mix, w_in=v_w_in, conv_w=v_conv_w, conv_b=v_conv_b, dt_bias=v_dt_bias, a_log=v_a_log, d_skip=v_d_skip,
             ssm_norm=v_ssm_norm, w_out=v_w_out, norm_ffn=v_norm_ffn, w_gate=v_w_gate, w_up=v_w_up, w_down=v_w_down, final_norm=v_final_norm)
    ax, ay, ac = lax.axis_index("x"), lax.axis_index("y"), lax.axis_index("c")
    dev = 4 * ax + 2 * ay + ac

    assert DEPTH == 2
    t = x.shape[0] * x.shape[1]
    xf, target = x.reshape(t, D_MODEL), loss_target.reshape(t, D_MODEL)

    def own_slot(block):
        return lax.dynamic_update_slice(lax.empty((N_DEV,) + block.shape[1:], block.dtype), block, (dev,) + (0,) * (block.ndim - 1))

    def layer_shard(arr, k, l):
        return jnp.transpose(arr, (2, 0, 1))[:, l, :] if k in TRANSPOSED else arr[l]

    def gather_start(keys, l, tie, name):
        shards = [(layer_shard(w[keys[0]], keys[0], l) + tie).astype(bf16)] + [layer_shard(w[k], k, l).astype(bf16) for k in keys[1:]]
        return _direct_start(shards, [own_slot(s[None]) for s in shards], False, name)

    def scatter_start(keys, grads_l, l, name):
        shapes = [(w[k].shape[2], w[k].shape[1]) if k in TRANSPOSED else w[k].shape[1:] for k in keys]
        by_dev = [_by_device(k, grads_l[k], sh, l).reshape((N_DEV,) + sh) for k, sh in zip(keys, shapes)]
        return _direct_start(by_dev, [own_slot(lax.dynamic_slice_in_dim(g, dev, 1, 0)) for g in by_dev], True, name)

    (g_in0, conv_all), tie = _allgather_two_level([w["w_in"][0].astype(bf16), w["conv_w"]], "gather_l0_w_in")
    rest0_copy = gather_start(REST, 0, tie[0, 0], "gather_l0_rest_start")
    l1_copy = gather_start(BIG, 1, rest0_copy[4][0, 0], "gather_l1_start")
    p = {k: w[k] for k in SMALL}
    p["norm_mix"] = p["norm_mix"] + l1_copy[4][0, 0]
    p["conv_w"] = jnp.transpose(conv_all, (1, 2, 0, 3)).reshape(DEPTH, CONV_WIDTH, CONV_CH)
    sp0, sp1 = _layer_params(p, 0), _layer_params(p, 1)

    def rest0(after):
        lands = _direct_wait(*rest0_copy[:4], after, False, "gather_l0_rest_wait")
        return tuple(_full_from_gathered(k, g, 0) for k, g in zip(REST, lands))

    tabs = _rope_tables(positions.reshape(t, 1), "rope_tables")
    w_in0 = _full_from_gathered("w_in", g_in0, 0)
    h1, saved0 = _layer_fwd(xf, w_in0, rest0, sp0, tabs, 0)
    lands1 = _direct_wait(*l1_copy[:4], h1, False, "gather_l1_wait")
    bigs1 = tuple(_full_from_gathered(k, g, 1) for k, g in zip(BIG, lands1))
    h2, saved1 = _layer_fwd(h1, bigs1[0], bigs1[1:], sp1, tabs, 1)
    dh, dhb, loss_parts, dfn = _final_loss(h2, _row(p["final_norm"]), target, "final_loss")
    loss_local = jnp.sum(loss_parts)

    dh, grads1 = _layer_bwd((dh, dhb), saved1, bigs1, sp1, tabs, 1, gd=bf16)
    l1_grads = scatter_start(BIG, grads1, 1, "scatter_l1_start")
    w_out0, w_gate0, w_up0, w_down0 = saved0["rest"]
    bigs0 = (w_in0, w_out0, w_gate0, w_up0, w_down0 + l1_grads[4][0, 0].astype(bf16))
    ffn0_grads = []

    def after_ffn(grads_ffn):
        ffn0_grads.append(scatter_start(FFN, grads_ffn, 0, "scatter_l0_ffn_start"))
        return ffn0_grads[0][4][0, 0]

    (dx, _), grads0 = _layer_bwd(dh, saved0, bigs0, sp0, tabs, 0, gd=bf16, after_ffn=after_ffn)
    mix0_grads = scatter_start(MIX, grads0, 0, "scatter_l0_mix_start")
    landed = {(k, 1): g for k, g in zip(BIG, _direct_wait(*l1_grads[:4], dx, True, "scatter_l1_wait"))}
    landed.update({(k, 0): g for k, g in zip(FFN, _direct_wait(*ffn0_grads[0][:4], dx, True, "scatter_l0_ffn_wait"))})
    out_g, out_d, out_m, out_v = {}, {}, {}, {}

    def update(keys, after):
        for k in keys:
            parts = [landed[k, l] for l in range(DEPTH)]
            if k in TRANSPOSED:
                depth, a, b = w[k].shape
                state = [jnp.transpose(s, (2, 0, 1)).reshape(b, depth * a) for s in (w[k], m[k], v[k])]
                res = _adamw_layers(parts, *state, after, "adamw_" + k, layers_on_columns=True)
                res = [jnp.transpose(r.reshape(b, depth, a), (1, 2, 0)) for r in res]
            else:
                res = _adamw_layers(parts, w[k], m[k], v[k], after, "adamw_" + k)
            for dst, r in zip((out_g, out_d, out_m, out_v), res):
                dst[k] = r

    update(FFN, mix0_grads[4])
    grads = {k: [grads0[k], grads1[k]] for k in grads0 if k not in BIG}
    grads["final_norm"] = dfn.sum(0) + mix0_grads[4][0, 0]

    small_like = [w[k] for k in SMALL]
    small_grads = [jnp.stack(grads[k]) if k != "final_norm" else grads[k] for k in SMALL]
    small_pack = jnp.concatenate([_pack_rows(small_grads, SMALL_ROWS, LANES), _pack_rows([jnp.stack(grads["conv_w"])], CONVW_ROWS, LANES)], axis=0)
    parts = _allgather_direct(small_pack, "gather_small_grads")
    g_s, d_s, m_s, v_s = _adamw(
        [parts[i, :SMALL_ROWS] for i in range(N_DEV)], _pack_rows(small_like, SMALL_ROWS, LANES),
        _pack_rows([m[k] for k in SMALL], SMALL_ROWS, LANES), _pack_rows([v[k] for k in SMALL], SMALL_ROWS, LANES), "adamw_replicated")
    for dst, src in ((out_g, g_s), (out_d, d_s), (out_m, m_s), (out_v, v_s)):
        dst.update(zip(SMALL, _unpack(src.reshape(-1), small_like)))
    shard_w = conv_w.shape[-1]
    conv_parts = parts[:, SMALL_ROWS:].reshape(N_DEV, DEPTH, CONV_WIDTH, CONV_CH)
    conv_mine = lax.dynamic_slice_in_dim(conv_parts, dev * shard_w, shard_w, axis=3)
    g_c, d_c, m_c, v_c = _adamw(
        [_pack_rows([conv_mine[i]], CONVW_SHARD_ROWS, LANES) for i in range(N_DEV)], _pack_rows([conv_w], CONVW_SHARD_ROWS, LANES),
        _pack_rows([m["conv_w"]], CONVW_SHARD_ROWS, LANES), _pack_rows([v["conv_w"]], CONVW_SHARD_ROWS, LANES), "adamw_conv_w")
    for dst, src in ((out_g, g_c), (out_d, d_c), (out_m, m_c), (out_v, v_c)):
        dst["conv_w"] = src.reshape(-1)[:conv_w.size].reshape(conv_w.shape)

    landed.update({(k, 0): g for k, g in zip(MIX, _direct_wait(*mix0_grads[:4], v_c + out_v["w_down"][0, :CONVW_SHARD_ROWS, :LANES], True, "scatter_l0_mix_wait"))})
    update(MIX, v_c)

    loss = lax.psum(loss_local, ("x", "y", "c"))
    return (loss, dx.reshape(x.shape), *[out_g[k] for k in WEIGHTS], *[out_d[k] for k in WEIGHTS],
            *[out_m[k] for k in WEIGHTS], *[out_v[k] for k in WEIGHTS])
```

```python
import jax
import jax.numpy as jnp
import numpy as np
from jax import lax
from jax.experimental import pallas as pl
from jax.experimental.pallas import tpu as pltpu

f32 = jnp.float32
bf16 = jnp.bfloat16

D_MODEL = 1024
SEQ = 2048
DEPTH = 2
HEAD_DIM = 64
N_ATTN_HEADS = 8
N_KV_HEADS = 2
ATTN_WIDTH = 512
KV_WIDTH = 128
ROPE_DIM = 16
ROPE_THETA = 500000.0
DILATIONS = (1, 4, 16)
ATTN_BLOCK = 128
SSM_HEADS = 16
SSM_INNER = 1024
SSM_GROUPS = 2
D_STATE = 128
CONV_WIDTH = 4
CHUNK = 128
CONV_CH = 1536
MIX_WIDTH = 1536
QKV_WIDTH = ATTN_WIDTH + 2 * KV_WIDTH
DT_OFF = 3328
IN_PROJ = 3344
IN_PROJ_PAD = 3456
FFN_HIDDEN = 2816
EPS = 1e-5
N_DEV = 8
ADAM_LR = 0.001
ADAM_B1 = 0.9
ADAM_B2 = 0.999
ADAM_EPS = 1e-08
ADAM_WD = 0.01
ADAM_STEP = 10

LANES = 128
SUBLANES = 8
VMEM_LIMIT = 56 * 1024 * 1024
VMEM_LIMIT_TWO_PAIRS = 60 * 1024 * 1024

MESH = pl.DeviceIdType.MESH
ANY = pl.BlockSpec(memory_space=pl.ANY)


def _cparams(sem, vmem=None):
    return pltpu.CompilerParams(dimension_semantics=sem, vmem_limit_bytes=vmem or VMEM_LIMIT)


def _sigmoid(x):
    return 1.0 / (1.0 + jnp.exp(-x))


def _silu(x):
    return x * _sigmoid(x)


def _dsilu(x):
    s = _sigmoid(x)
    return s * (1.0 + x * (1.0 - s))


def _silu_and_grad(x):
    s = _sigmoid(x)
    return x * s, s * (1.0 + x * (1.0 - s))


def _softplus(x):
    return jnp.maximum(x, 0.0) + jnp.log(1.0 + jnp.exp(-jnp.abs(x)))


def _dot(a, b, dims, precision=None):
    return lax.dot_general(a, b, (dims, ((), ())), preferred_element_type=f32, precision=precision)


def _nn(a, b, precision=None):
    return _dot(a, b, ((1,), (0,)), precision)


def _nt(a, b):
    return _dot(a, b, ((1,), (1,)))


def _tn(a, b):
    return _dot(a, b, ((0,), (0,)))


def _rowsum8(t):
    n, w = t.shape
    return jnp.sum(t.reshape(n // SUBLANES, SUBLANES, w), axis=0)


def _matmul(a, b, *, mode, n_out=None, b_off=0, add=None, out_dtype=f32, tm=2048, tn=512, tk=1024, name):
    if mode == "tn":
        kk, m = a.shape
    else:
        m, kk = a.shape
    n = n_out if n_out is not None else (b.shape[0] if mode == "nt" else b.shape[1])
    tm, tn, tk = min(tm, m), min(tn, n), min(tk, kk)
    assert m % tm == 0 and n % tn == 0 and kk % tk == 0, (name, m, n, kk, tm, tn, tk)
    nk = kk // tk
    if mode == "nn":
        a_spec = pl.BlockSpec((tm, tk), lambda i, j, k: (i, k))
        b_spec = pl.BlockSpec((tk, tn), lambda i, j, k: (k, j + b_off))
        dims = ((1,), (0,))
    elif mode == "nt":
        a_spec = pl.BlockSpec((tm, tk), lambda i, j, k: (i, k))
        b_spec = pl.BlockSpec((tn, tk), lambda i, j, k: (j + b_off, k))
        dims = ((1,), (1,))
    else:
        a_spec = pl.BlockSpec((tk, tm), lambda i, j, k: (k, i))
        b_spec = pl.BlockSpec((tk, tn), lambda i, j, k: (k, j + b_off))
        dims = ((0,), (0,))
    o_spec = pl.BlockSpec((tm, tn), lambda i, j, k: (i, j))
    has_add = add is not None

    def body(*refs):
        if has_add:
            a_ref, b_ref, add_ref, o_ref, acc_ref = refs
        else:
            a_ref, b_ref, o_ref, acc_ref = refs
        k = pl.program_id(2)
        part = _dot(a_ref[...].astype(bf16), b_ref[...].astype(bf16), dims)

        @pl.when(k == 0)
        def _():
            acc_ref[...] = part

        @pl.when(k > 0)
        def _():
            acc_ref[...] += part

        @pl.when(k == nk - 1)
        def _():
            r = acc_ref[...]
            if has_add:
                r = r + add_ref[...]
            o_ref[...] = r.astype(out_dtype)

    in_specs = [a_spec, b_spec] + ([o_spec] if has_add else [])
    args = (a, b) + ((add,) if has_add else ())
    return pl.pallas_call(
        body, name=name, grid=(m // tm, n // tn, nk), in_specs=in_specs, out_specs=o_spec,
        out_shape=jax.ShapeDtypeStruct((m, n), out_dtype), scratch_shapes=[pltpu.VMEM((tm, tn), f32)],
        compiler_params=_cparams(("parallel", "parallel", "arbitrary")),
    )(*args)


def _in_proj(hn, w_in, widths, name, tm=2048, tn=256):
    m, k = hn.shape
    starts = [sum(widths[:i]) // tn for i in range(len(widths))]
    counts = [wd // tn for wd in widths]
    assert m % tm == 0 and all(wd % tn == 0 for wd in widths)
    n_out = len(widths)

    def body(a_ref, w_ref, *o_refs):
        j = pl.program_id(1)
        acc = _nn(a_ref[...], w_ref[...])
        for s, c, o_ref in zip(starts, counts, o_refs):
            @pl.when((j >= s) & (j < s + c))
            def _(o_ref=o_ref):
                o_ref[...] = acc

    def o_spec(s, c):
        return pl.BlockSpec((tm, tn), lambda i, j: (i, jnp.clip(j - s, 0, c - 1)))

    return pl.pallas_call(
        body, name=name, grid=(m // tm, sum(counts)),
        in_specs=[pl.BlockSpec((tm, k), lambda i, j: (i, 0)), pl.BlockSpec((k, tn), lambda i, j: (0, j))],
        out_specs=[o_spec(s, c) for s, c in zip(starts, counts)],
        out_shape=[jax.ShapeDtypeStruct((m, wd), f32) for wd in widths], compiler_params=_cparams(("parallel", "arbitrary")),
    )(hn, w_in)


def _out_proj(o, yn, w_out, h, name, tm=2048, tn=512):
    m, kb = o.shape
    n = w_out.shape[1]
    n_y = yn.shape[1] // kb
    assert yn.shape[1] % kb == 0 and w_out.shape[0] == kb * (1 + n_y) and m % tm == 0 and n % tn == 0

    def body(*refs):
        o_ref, y_refs, w_refs, h_ref, out_ref = refs[0], refs[1:1 + n_y], refs[1 + n_y:2 + 2 * n_y], refs[-2], refs[-1]
        acc = h_ref[...] + _nn(o_ref[...].astype(bf16), w_refs[0][...])
        for y_ref, w_ref in zip(y_refs, w_refs[1:]):
            acc = acc + _nn(y_ref[...], w_ref[...])
        out_ref[...] = acc

    res = pl.BlockSpec((tm, tn), lambda i, j: (i, j))

    def a_blk(c):
        return pl.BlockSpec((tm, kb), lambda i, j: (i, c))

    def w_blk(r):
        return pl.BlockSpec((kb, tn), lambda i, j: (r, j))

    return pl.pallas_call(
        body, name=name, grid=(m // tm, n // tn),
        in_specs=[a_blk(0)] + [a_blk(c) for c in range(n_y)] + [w_blk(r) for r in range(1 + n_y)] + [res],
        out_specs=res, out_shape=jax.ShapeDtypeStruct((m, n), f32), compiler_params=_cparams(("parallel", "parallel")),
    )(o, *[yn] * n_y, *[w_out] * (1 + n_y), h)


def _swiglu_fwd(hn, w_gate, w_up, name, tm=2048, tn=256):
    m, k = hn.shape
    n = w_gate.shape[0]
    assert m % tm == 0 and n % tn == 0, (name, m, n, tm, tn)

    def body(a_ref, wg_ref, wu_ref, g_ref, u_ref, act_ref):
        a = a_ref[...]
        g = _nt(a, wg_ref[...])
        u = _nt(a, wu_ref[...])
        sg, dsg = _silu_and_grad(g)
        g_ref[...] = (u * dsg).astype(bf16)
        u_ref[...] = sg.astype(bf16)
        act_ref[...] = (sg * u).astype(bf16)

    a_spec = pl.BlockSpec((tm, k), lambda i, j: (i, 0))
    w_spec = pl.BlockSpec((tn, k), lambda i, j: (j, 0))
    o_spec = pl.BlockSpec((tm, tn), lambda i, j: (i, j))
    return pl.pallas_call(
        body, name=name, grid=(m // tm, n // tn), in_specs=[a_spec, w_spec, w_spec], out_specs=[o_spec, o_spec, o_spec],
        out_shape=[jax.ShapeDtypeStruct((m, n), bf16)] * 3,
        compiler_params=_cparams(("parallel", "parallel")),
    )(hn, w_gate, w_up)


def _swiglu_bwd(dh, w_down, g, u, name, tm=2048, tn=256):
    m, k = dh.shape
    n = w_down.shape[0]
    assert m % tm == 0 and n % tn == 0, (name, m, n, tm, tn)

    def body(a_ref, w_ref, g_ref, u_ref, dg_ref, du_ref):
        dact = _nt(a_ref[...].astype(bf16), w_ref[...])
        dg_ref[...] = (dact * g_ref[...].astype(f32)).astype(bf16)
        du_ref[...] = (dact * u_ref[...].astype(f32)).astype(bf16)

    a_spec = pl.BlockSpec((tm, k), lambda i, j: (i, 0))
    w_spec = pl.BlockSpec((tn, k), lambda i, j: (j, 0))
    o_spec = pl.BlockSpec((tm, tn), lambda i, j: (i, j))
    return pl.pallas_call(
        body, name=name, grid=(m // tm, n // tn), in_specs=[a_spec, w_spec, o_spec, o_spec], out_specs=[o_spec, o_spec],
        out_shape=[jax.ShapeDtypeStruct((m, n), bf16), jax.ShapeDtypeStruct((m, n), bf16)],
        compiler_params=_cparams(("parallel", "parallel")),
    )(dh, w_down, g, u)


def _rmsnorm_fwd(h, w, name, tm=512):
    m, d = h.shape

    def body(h_ref, w_ref, o_ref):
        x = h_ref[...]
        r = lax.rsqrt(jnp.mean(x * x, axis=-1, keepdims=True) + EPS)
        o_ref[...] = (x * r * w_ref[...]).astype(bf16)

    return pl.pallas_call(
        body, name=name, grid=(m // tm,),
        in_specs=[pl.BlockSpec((tm, d), lambda i: (i, 0)), pl.BlockSpec((1, d), lambda i: (0, 0))],
        out_specs=pl.BlockSpec((tm, d), lambda i: (i, 0)), out_shape=jax.ShapeDtypeStruct((m, d), bf16),
        compiler_params=_cparams(("parallel",)),
    )(h, w)


def _nt_norm_bwd(pairs, h, w, dres, name, tk, b_is_kd=False, bf16_copy=True, tm=1024, vmem=None):
    m, d = h.shape
    contract = _nn if b_is_kd else _nt
    steps = [p[0].shape[1] // tk for p in pairs]
    assert all(p[0].shape[1] % tk == 0 for p in pairs), (name, tk)
    starts = [sum(steps[:i]) for i in range(len(pairs))]
    nk = sum(steps)
    n_p = len(pairs)

    def body(*refs):
        ab = refs[:2 * n_p]
        h_ref, w_ref, dres_ref, dh_ref = refs[2 * n_p:2 * n_p + 4]
        dhb_ref = refs[2 * n_p + 4] if bf16_copy else None
        dw_ref, acc_ref = refs[-2:]
        i, k = pl.program_id(0), pl.program_id(1)

        @pl.when(k == 0)
        def _():
            acc_ref[...] = jnp.zeros_like(acc_ref)

        for p in range(n_p):
            @pl.when((k >= starts[p]) & (k < starts[p] + steps[p]))
            def _(p=p):
                acc_ref[...] += contract(ab[2 * p][...], ab[2 * p + 1][...])

        @pl.when(k == nk - 1)
        def _():
            x = h_ref[...]
            r = lax.rsqrt(jnp.mean(x * x, axis=-1, keepdims=True) + EPS)
            xhat = x * r
            dy = acc_ref[...]
            gw = dy * w_ref[...]
            dh = dres_ref[...] + r * (gw - xhat * jnp.mean(gw * xhat, axis=-1, keepdims=True))
            dh_ref[...] = dh
            if bf16_copy:
                dhb_ref[...] = dh.astype(bf16)
            part = _rowsum8(dy * xhat)

            @pl.when(i == 0)
            def _():
                dw_ref[...] = part

            @pl.when(i > 0)
            def _():
                dw_ref[...] += part

    def clamp(k, p):
        return jnp.clip(k - starts[p], 0, steps[p] - 1)

    in_specs = []
    for p in range(n_p):
        b_spec = (pl.BlockSpec((tk, d), lambda i, k, p=p: (clamp(k, p), 0)) if b_is_kd
                  else pl.BlockSpec((d, tk), lambda i, k, p=p: (0, clamp(k, p))))
        in_specs += [pl.BlockSpec((tm, tk), lambda i, k, p=p: (i, clamp(k, p))), b_spec]
    row = pl.BlockSpec((tm, d), lambda i, k: (i, 0))
    in_specs += [row, pl.BlockSpec((1, d), lambda i, k: (0, 0)), row]
    return pl.pallas_call(
        body, name=name, grid=(m // tm, nk), in_specs=in_specs,
        out_specs=[row] + [row] * bf16_copy + [pl.BlockSpec((SUBLANES, d), lambda i, k: (0, 0))],
        out_shape=[jax.ShapeDtypeStruct((m, d), f32)] + [jax.ShapeDtypeStruct((m, d), bf16)] * bf16_copy + [jax.ShapeDtypeStruct((SUBLANES, d), f32)],
        scratch_shapes=[pltpu.VMEM((tm, d), f32)], compiler_params=_cparams(("arbitrary", "arbitrary"), vmem),
    )(*[t for p in pairs for t in p], h, w, dres)


def _final_loss(h, w, target, name, tm=512):
    m, d = h.shape

    def body(h_ref, w_ref, t_ref, dh_ref, dhb_ref, loss_ref, dw_ref):
        x = h_ref[...]
        r = lax.rsqrt(jnp.mean(x * x, axis=-1, keepdims=True) + EPS)
        xhat = x * r
        ww = w_ref[...]
        err = xhat * ww - t_ref[...]
        dy = err * (1.0 / d)
        gw = dy * ww
        dh = r * (gw - xhat * jnp.mean(gw * xhat, axis=-1, keepdims=True))
        dh_ref[...] = dh
        dhb_ref[...] = dh.astype(bf16)
        lpart = _rowsum8(err * err) * (0.5 / d)
        wpart = _rowsum8(dy * xhat)

        @pl.when(pl.program_id(0) == 0)
        def _():
            loss_ref[...] = lpart
            dw_ref[...] = wpart

        @pl.when(pl.program_id(0) > 0)
        def _():
            loss_ref[...] += lpart
            dw_ref[...] += wpart

    row = pl.BlockSpec((tm, d), lambda i: (i, 0))
    acc = pl.BlockSpec((SUBLANES, d), lambda i: (0, 0))
    return pl.pallas_call(
        body, name=name, grid=(m // tm,),
        in_specs=[row, pl.BlockSpec((1, d), lambda i: (0, 0)), row], out_specs=[row, row, acc, acc],
        out_shape=[jax.ShapeDtypeStruct((m, d), f32), jax.ShapeDtypeStruct((m, d), bf16),
                   jax.ShapeDtypeStruct((SUBLANES, d), f32), jax.ShapeDtypeStruct((SUBLANES, d), f32)],
        compiler_params=_cparams(("arbitrary",)),
    )(h, w, target)


def _lane_tables():
    f = np.arange(LANES) % HEAD_DIM
    inv = ROPE_THETA ** (-jnp.arange(0, ROPE_DIM, 2, dtype=f32) / ROPE_DIM)
    invf = jnp.where(f < ROPE_DIM, inv[f % (ROPE_DIM // 2)], 0.0).astype(f32)
    return invf.reshape(1, LANES)


def _rope_tables(pos_col, name):
    t = pos_col.shape[0]
    tm = SEQ

    def body(p_ref, f_ref, c_ref, s1_ref, s2_ref):
        ang = p_ref[...].astype(f32) * f_ref[...]
        co, si = jnp.cos(ang), jnp.sin(ang)
        f = lax.broadcasted_iota(jnp.int32, (tm, LANES), 1) % HEAD_DIM
        c_ref[...] = jnp.where(f < ROPE_DIM, co, 1.0)
        s1_ref[...] = jnp.where(f < ROPE_DIM // 2, -si, 0.0)
        s2_ref[...] = jnp.where((f >= ROPE_DIM // 2) & (f < ROPE_DIM), si, 0.0)

    row = pl.BlockSpec((tm, LANES), lambda i: (i, 0))
    return pl.pallas_call(
        body, name=name, grid=(t // tm,),
        in_specs=[pl.BlockSpec((tm, 1), lambda i: (i, 0)), pl.BlockSpec((1, LANES), lambda i: (0, 0))],
        out_specs=[row, row, row], out_shape=[jax.ShapeDtypeStruct((t, LANES), f32)] * 3,
        compiler_params=_cparams(("parallel",)),
    )(pos_col, _lane_tables())


def _rot(x, c, s1, s2):
    return x * c + pltpu.roll(x, LANES - ROPE_DIM // 2, 1) * s1 + pltpu.roll(x, ROPE_DIM // 2, 1) * s2


def _rot_t(g, c, s1, s2):
    return g * c + pltpu.roll(g * s1, ROPE_DIM // 2, 1) + pltpu.roll(g * s2, LANES - ROPE_DIM // 2, 1)


def _dup_head(x, kvh, low):
    a = jnp.where(kvh == 0, x, pltpu.roll(x, HEAD_DIM, 1))
    return jnp.where(low, a, pltpu.roll(a, HEAD_DIM, 1))


def _deinterleave(src_ref, dst_ref, d, dtype):
    length = SEQ // d
    if d == 1:
        dst_ref[...] = src_ref[...].astype(dtype)
    else:
        for r in range(d):
            dst_ref[pl.ds(r * length, length), :] = src_ref[pl.ds(r, length, stride=d), :].astype(dtype)


def _interleave_store(src_ref, dst_ref, d, accumulate):
    length = SEQ // d
    if d == 1:
        if accumulate:
            dst_ref[...] += src_ref[...]
        else:
            dst_ref[...] = src_ref[...]
    else:
        for r in range(d):
            blk = src_ref[pl.ds(r * length, length), :]
            if accumulate:
                dst_ref[pl.ds(r, length, stride=d), :] = dst_ref[pl.ds(r, length, stride=d), :] + blk
            else:
                dst_ref[pl.ds(r, length, stride=d), :] = blk


def _attn_masks():
    qi = lax.broadcasted_iota(jnp.int32, (ATTN_BLOCK, ATTN_BLOCK), 0)
    ki = lax.broadcasted_iota(jnp.int32, (ATTN_BLOCK, ATTN_BLOCK), 1)
    low = lax.broadcasted_iota(jnp.int32, (ATTN_BLOCK, LANES), 1) < HEAD_DIM
    return ki <= qi, ki >= qi, low


NEG_INF = float("-inf")


N_BRANCH = len(DILATIONS)


def _attn_prep(qkv, tabs, name):
    t = qkv.shape[0]
    nb = t // SEQ
    n_j = ATTN_WIDTH // LANES

    def q_body(q_ref, c_ref, s1_ref, s2_ref, out_ref, xr):
        xr[...] = _rot(q_ref[...], c_ref[...], s1_ref[...], s2_ref[...]) * (HEAD_DIM ** -0.5)
        for bi, d in enumerate(DILATIONS):
            _deinterleave(xr, out_ref.at[bi], d, bf16)

    def kv_body(x_ref, c_ref, s1_ref, s2_ref, out_ref, xr):
        lowfull = lax.broadcasted_iota(jnp.int32, (SEQ, LANES), 1) < HEAD_DIM
        x = x_ref[...]
        x = jnp.where(pl.program_id(1) == 0, _rot(x, c_ref[...], s1_ref[...], s2_ref[...]), x)
        for kvh in range(N_KV_HEADS):
            xr[...] = _dup_head(x, kvh, lowfull)
            for bi, d in enumerate(DILATIONS):
                length = SEQ // d
                for r in range(d):
                    rows = xr[...] if d == 1 else xr[pl.ds(r, length, stride=d), :]
                    out_ref[0, bi, pl.ds(r * length, length), kvh * LANES:(kvh + 1) * LANES] = rows.astype(bf16)

    tab = pl.BlockSpec((SEQ, LANES), lambda b, j: (b, 0))
    q = pl.pallas_call(
        q_body, name=name + "_q", grid=(nb, n_j),
        in_specs=[pl.BlockSpec((SEQ, LANES), lambda b, j: (b, j)), tab, tab, tab],
        out_specs=pl.BlockSpec((N_BRANCH, SEQ, LANES), lambda b, j: (0, b, j)),
        out_shape=jax.ShapeDtypeStruct((N_BRANCH, t, ATTN_WIDTH), bf16), scratch_shapes=[pltpu.VMEM((SEQ, LANES), f32)],
        compiler_params=_cparams(("parallel", "parallel")),
    )(qkv, *tabs)
    kv = pl.pallas_call(
        kv_body, name=name + "_kv", grid=(nb, 2),
        in_specs=[pl.BlockSpec((SEQ, LANES), lambda b, j: (b, n_j + j)), tab, tab, tab],
        out_specs=pl.BlockSpec((1, N_BRANCH, SEQ, N_KV_HEADS * LANES), lambda b, j: (j, 0, b, 0)),
        out_shape=jax.ShapeDtypeStruct((2, N_BRANCH, t, N_KV_HEADS * LANES), bf16), scratch_shapes=[pltpu.VMEM((SEQ, LANES), f32)],
        compiler_params=_cparams(("parallel", "parallel")),
    )(qkv, *tabs)
    return q, kv


def _attn_fwd(prep, name):
    q_all, kv_all = prep
    t = q_all.shape[1]
    nb = t // SEQ
    n_blk = SEQ // ATTN_BLOCK

    def body(q_ref, k_ref, v_ref, o_ref, o16_ref, lse_ref, ob, lb, o0, o1, o2, l0, l1, l2, ss):
        cur_ok, prev_ok, low = _attn_masks()
        onat, lnat = (o0, o1, o2), (l0, l1, l2)
        for bi, d in enumerate(DILATIONS):
            qd, kd, vd = q_ref.at[bi], k_ref.at[0, bi], v_ref.at[0, bi]
            per_res = n_blk // d

            def scores(n):
                cur, prev = pl.ds(n * ATTN_BLOCK, ATTN_BLOCK), pl.ds(max(n - 1, 0) * ATTN_BLOCK, ATTN_BLOCK)
                has_prev = n % per_res != 0
                qb = qd[cur, :]
                kc = kd[cur, :]
                if has_prev:
                    kp = kd[prev, :]
                for a in range(2):
                    qa = jnp.where(low if a == 0 else ~low, qb, jnp.zeros_like(qb))
                    ss[2 * n + a, :, 0:ATTN_BLOCK] = jnp.where(cur_ok, _nt(qa, kc), NEG_INF)
                    if has_prev:
                        ss[2 * n + a, :, ATTN_BLOCK:2 * ATTN_BLOCK] = jnp.where(prev_ok, _nt(qa, kp), NEG_INF)

            def softmax_pv(n):
                cur, prev = pl.ds(n * ATTN_BLOCK, ATTN_BLOCK), pl.ds(max(n - 1, 0) * ATTN_BLOCK, ATTN_BLOCK)
                has_prev = n % per_res != 0
                vc = vd[cur, :]
                if has_prev:
                    vp = vd[prev, :]
                outs, lses = [], []
                for a in range(2):
                    sc = ss[2 * n + a, :, 0:ATTN_BLOCK]
                    if has_prev:
                        sp = ss[2 * n + a, :, ATTN_BLOCK:2 * ATTN_BLOCK]
                        m = jnp.max(jnp.maximum(sc, sp), axis=1, keepdims=True)
                        pc, pp = jnp.exp(sc - m), jnp.exp(sp - m)
                        den = jnp.sum(pc + pp, axis=1, keepdims=True)
                        acc = _nn(pc.astype(bf16), vc) + _nn(pp.astype(bf16), vp)
                    else:
                        m = jnp.max(sc, axis=1, keepdims=True)
                        pc = jnp.exp(sc - m)
                        den = jnp.sum(pc, axis=1, keepdims=True)
                        acc = _nn(pc.astype(bf16), vc)
                    outs.append(acc * (1.0 / den))
                    lses.append(m + jnp.log(den))
                ob[cur, :] = jnp.where(low, outs[0], outs[1])
                lb[cur, :] = jnp.where(low, lses[0], lses[1])

            for n in range(n_blk):
                scores(n)
            for n in range(n_blk):
                softmax_pv(n)
            _interleave_store(ob, onat[bi], d, False)
            _interleave_store(lb, lnat[bi], d, False)
        la, lbb, lc = l0[...], l1[...], l2[...]
        lm = jnp.maximum(jnp.maximum(la, lbb), lc)
        wa, wb, wc = jnp.exp(la - lm), jnp.exp(lbb - lm), jnp.exp(lc - lm)
        ws = wa + wb + wc
        o = (wa * o0[...] + wb * o1[...] + wc * o2[...]) / ws
        o_ref[...] = o
        o16_ref[...] = o.astype(bf16)
        lse_ref[...] = lm + jnp.log(ws)

    def col(jj):
        return pl.BlockSpec((SEQ, LANES), lambda b, j: (b, jj if jj is not None else j))

    fs = pltpu.VMEM((SEQ, LANES), f32)
    return pl.pallas_call(
        body, name=name, grid=(nb, ATTN_WIDTH // LANES),
        in_specs=[pl.BlockSpec((N_BRANCH, SEQ, LANES), lambda b, j: (0, b, j)),
                  pl.BlockSpec((1, N_BRANCH, SEQ, LANES), lambda b, j: (0, 0, b, j // 2)),
                  pl.BlockSpec((1, N_BRANCH, SEQ, LANES), lambda b, j: (1, 0, b, j // 2))],
        out_specs=[col(None), col(None), col(None)],
        out_shape=[jax.ShapeDtypeStruct((t, ATTN_WIDTH), f32), jax.ShapeDtypeStruct((t, ATTN_WIDTH), bf16), jax.ShapeDtypeStruct((t, ATTN_WIDTH), f32)],
        scratch_shapes=[fs, fs, fs, fs, fs, fs, fs, fs, pltpu.VMEM((2 * n_blk, ATTN_BLOCK, 2 * ATTN_BLOCK), f32)],
        compiler_params=_cparams(("parallel", "parallel")),
    )(q_all, kv_all, kv_all)


def _attn_bwd(prep, tabs, o, lse, do, name):
    q_all, kv_all = prep
    t = q_all.shape[1]
    nb = t // SEQ
    n_blk = SEQ // ATTN_BLOCK
    n_j = ATTN_WIDTH // LANES

    def body(q_ref, k_ref, v_ref, c_ref, s1_ref, s2_ref, o_ref, lse_ref, do_ref, dq_ref, dk_ref, dv_ref,
             stat, dod, std, dqd, dkd, dvd, dqa, dka, dva, pb, dsb, dk_acc, dv_acc):
        j = pl.program_id(1)
        kvh = j // 2
        cur_ok, prev_ok, low = _attn_masks()
        lane = lax.broadcasted_iota(jnp.int32, (SEQ, LANES), 1)
        lowfull = lane < HEAD_DIM
        c, s1, s2 = c_ref[...], s1_ref[...], s2_ref[...]
        prod = do_ref[...] * o_ref[...]
        d_lo = jnp.sum(jnp.where(lowfull, prod, 0.0), axis=1, keepdims=True)
        d_hi = jnp.sum(jnp.where(lowfull, 0.0, prod), axis=1, keepdims=True)
        stat[...] = jnp.where(lane % HEAD_DIM < HEAD_DIM // 2, lse_ref[...], jnp.where(lowfull, d_lo, d_hi))
        dqa[...] = jnp.zeros_like(dqa)
        dka[...] = jnp.zeros_like(dka)
        dva[...] = jnp.zeros_like(dva)
        for bi, d in enumerate(DILATIONS):
            qd, kd, vd = q_ref.at[bi], k_ref.at[0, bi], v_ref.at[0, bi]
            _deinterleave(do_ref, dod, d, bf16)
            _deinterleave(stat, std, d, f32)
            per_res = n_blk // d
            curl, prevl = slice(0, ATTN_BLOCK), slice(ATTN_BLOCK, 2 * ATTN_BLOCK)

            def halves(x):
                zero = jnp.zeros_like(x)
                return jnp.where(low, x, zero), jnp.where(low, zero, x)

            def blk(n):
                return pl.ds(n * ATTN_BLOCK, ATTN_BLOCK)

            def has_prev(n):
                return n < n_blk and n % per_res != 0

            def probs(n):
                cur = blk(n)
                qas, doas = halves(qd[cur, :]), halves(dod[cur, :])
                kc, vc = kd[cur, :], vd[cur, :]
                if has_prev(n):
                    kp, vp = kd[blk(n - 1), :], vd[blk(n - 1), :]
                stb = std[cur, :]
                for a in range(2):
                    ls = stb[:, a * HEAD_DIM:a * HEAD_DIM + 1]
                    de = stb[:, a * HEAD_DIM + HEAD_DIM // 2:a * HEAD_DIM + HEAD_DIM // 2 + 1]
                    pc = jnp.exp(jnp.where(cur_ok, _nt(qas[a], kc), NEG_INF) - ls)
                    pb[2 * n + a, :, curl] = pc.astype(bf16)
                    dsb[2 * n + a, :, curl] = (pc * (_nt(doas[a], vc) - de)).astype(bf16)
                    if has_prev(n):
                        pp = jnp.exp(jnp.where(prev_ok, _nt(qas[a], kp), NEG_INF) - ls)
                        pb[2 * n + a, :, prevl] = pp.astype(bf16)
                        dsb[2 * n + a, :, prevl] = (pp * (_nt(doas[a], vp) - de)).astype(bf16)

            def grads(n):
                cur = blk(n)
                kc = kd[cur, :]
                dqs = [_nn(dsb[2 * n + a, :, curl], kc) for a in range(2)]
                q_rows, do_rows = list(halves(qd[cur, :])), list(halves(dod[cur, :]))
                ds_rows, p_rows = [dsb[2 * n + a, :, curl] for a in range(2)], [pb[2 * n + a, :, curl] for a in range(2)]
                if has_prev(n):
                    kp = kd[blk(n - 1), :]
                    dqs = [dqs[a] + _nn(dsb[2 * n + a, :, prevl], kp) for a in range(2)]
                if has_prev(n + 1):
                    q_rows += list(halves(qd[blk(n + 1), :]))
                    do_rows += list(halves(dod[blk(n + 1), :]))
                    ds_rows += [dsb[2 * n + 2 + a, :, prevl] for a in range(2)]
                    p_rows += [pb[2 * n + 2 + a, :, prevl] for a in range(2)]
                dqd[cur, :] = jnp.where(low, dqs[0], dqs[1])
                dkd[cur, :] = _tn(jnp.concatenate(ds_rows, axis=0), jnp.concatenate(q_rows, axis=0))
                dvd[cur, :] = _tn(jnp.concatenate(p_rows, axis=0), jnp.concatenate(do_rows, axis=0))

            for n in range(n_blk):
                probs(n)
            for n in range(n_blk):
                grads(n)
            _interleave_store(dqd, dqa, d, True)
            _interleave_store(dkd, dka, d, True)
            _interleave_store(dvd, dva, d, True)
        dq_ref[...] = _rot_t(dqa[...] * (HEAD_DIM ** -0.5), c, s1, s2).astype(bf16)
        dkf = dka[...]
        dkf = _rot_t(dkf + pltpu.roll(dkf, HEAD_DIM, 1), c, s1, s2)
        dvf = dva[...]
        dvf = dvf + pltpu.roll(dvf, HEAD_DIM, 1)
        mine = (lax.broadcasted_iota(jnp.int32, (SEQ, LANES), 1) // HEAD_DIM) == kvh
        dkc_, dvc_ = jnp.where(mine, dkf, 0.0), jnp.where(mine, dvf, 0.0)

        @pl.when(j == 0)
        def _():
            dk_acc[...] = dkc_
            dv_acc[...] = dvc_

        @pl.when(j > 0)
        def _():
            dk_acc[...] += dkc_
            dv_acc[...] += dvc_

        @pl.when(j == n_j - 1)
        def _():
            dk_ref[...] = dk_acc[...].astype(bf16)
            dv_ref[...] = dv_acc[...].astype(bf16)

    def col(jj):
        return pl.BlockSpec((SEQ, LANES), lambda b, j: (b, jj if jj is not None else j))

    tab = pl.BlockSpec((SEQ, LANES), lambda b, j: (b, 0))
    fs = pltpu.VMEM((SEQ, LANES), f32)
    hs = pltpu.VMEM((SEQ, LANES), bf16)
    return pl.pallas_call(
        body, name=name, grid=(nb, n_j),
        in_specs=[pl.BlockSpec((N_BRANCH, SEQ, LANES), lambda b, j: (0, b, j)),
                  pl.BlockSpec((1, N_BRANCH, SEQ, LANES), lambda b, j: (0, 0, b, j // 2)),
                  pl.BlockSpec((1, N_BRANCH, SEQ, LANES), lambda b, j: (1, 0, b, j // 2)),
                  tab, tab, tab, col(None), col(None), col(None)],
        out_specs=[col(None), tab, tab],
        out_shape=[jax.ShapeDtypeStruct((t, ATTN_WIDTH), bf16), jax.ShapeDtypeStruct((t, LANES), bf16), jax.ShapeDtypeStruct((t, LANES), bf16)],
        scratch_shapes=[fs, hs, fs, fs, fs, fs, fs, fs, fs,
                        pltpu.VMEM((2 * n_blk, ATTN_BLOCK, 2 * ATTN_BLOCK), bf16), pltpu.VMEM((2 * n_blk, ATTN_BLOCK, 2 * ATTN_BLOCK), bf16), fs, fs],
        compiler_params=_cparams(("parallel", "arbitrary")),
    )(q_all, kv_all, kv_all, *tabs, o, lse, do)


def _tap(w_ref, s):
    return w_ref[CONV_WIDTH - 1 - s:CONV_WIDTH - s, :]


def _conv_pre(x, w_ref, b_ref, row):
    shifted = [x] + [jnp.where(row >= s, pltpu.roll(x, s, 0), 0.0) for s in range(1, CONV_WIDTH)]
    pre = b_ref[...] + _tap(w_ref, 0) * x
    for s in range(1, CONV_WIDTH):
        pre = pre + _tap(w_ref, s) * shifted[s]
    return pre, shifted


def _conv_fwd(x, w, b, name, tc=512):
    t, ch = x.shape

    def body(x_ref, w_ref, b_ref, o_ref):
        row = lax.broadcasted_iota(jnp.int32, (SEQ, tc), 0)
        pre, _ = _conv_pre(x_ref[...], w_ref, b_ref, row)
        o_ref[...] = _silu(pre)

    xs = pl.BlockSpec((SEQ, tc), lambda i, j: (i, j))
    return pl.pallas_call(
        body, name=name, grid=(t // SEQ, ch // tc),
        in_specs=[xs, pl.BlockSpec((CONV_WIDTH, tc), lambda i, j: (0, j)), pl.BlockSpec((1, tc), lambda i, j: (0, j))],
        out_specs=xs, out_shape=jax.ShapeDtypeStruct((t, ch), f32),
        compiler_params=_cparams(("parallel", "parallel")),
    )(x, w, b)


def _conv_bwd(x, w, b, dact, name, tc=512):
    t, ch = x.shape

    def body(x_ref, w_ref, b_ref, d_ref, dx_ref, dw_ref, db_ref):
        row = lax.broadcasted_iota(jnp.int32, (SEQ, tc), 0)
        pre, shifted = _conv_pre(x_ref[...], w_ref, b_ref, row)
        dpre = d_ref[...] * _dsilu(pre)
        dx = _tap(w_ref, 0) * dpre
        for s in range(1, CONV_WIDTH):
            dx = dx + _tap(w_ref, s) * jnp.where(row < SEQ - s, pltpu.roll(dpre, SEQ - s, 0), 0.0)
        dx_ref[...] = dx.astype(bf16)
        first = pl.program_id(1) == 0
        parts = [jnp.sum(dpre * shifted[CONV_WIDTH - 1 - k], axis=0, keepdims=True) for k in range(CONV_WIDTH)]
        dbp = jnp.sum(dpre, axis=0, keepdims=True)

        @pl.when(first)
        def _():
            for k in range(CONV_WIDTH):
                dw_ref[k:k + 1, :] = parts[k]
            db_ref[...] = dbp

        @pl.when(jnp.logical_not(first))
        def _():
            for k in range(CONV_WIDTH):
                dw_ref[k:k + 1, :] += parts[k]
            db_ref[...] += dbp

    xs = pl.BlockSpec((SEQ, tc), lambda j, i: (i, j))
    ws = pl.BlockSpec((CONV_WIDTH, tc), lambda j, i: (0, j))
    bs = pl.BlockSpec((1, tc), lambda j, i: (0, j))
    return pl.pallas_call(
        body, name=name, grid=(ch // tc, t // SEQ),
        in_specs=[xs, ws, bs, xs], out_specs=[xs, ws, bs],
        out_shape=[jax.ShapeDtypeStruct((t, ch), bf16), jax.ShapeDtypeStruct((CONV_WIDTH, ch), f32), jax.ShapeDtypeStruct((1, ch), f32)],
        compiler_params=_cparams(("parallel", "arbitrary")),
    )(x, w, b, dact)


GROUP_W = SSM_INNER // SSM_GROUPS
HEADS_PER_GROUP = SSM_HEADS // SSM_GROUPS
SSD_CHUNKS_PER_STEP = 4


def _split3(x):
    hi = x.astype(bf16)
    r1 = x - hi.astype(f32)
    mid = r1.astype(bf16)
    lo = (r1 - mid.astype(f32)).astype(bf16)
    return hi, mid, lo


def _dot_exact(x, sel, dims, x_is_lhs=True):
    parts = _split3(x)
    if x_is_lhs:
        return _dot(parts[0], sel, dims) + _dot(parts[1], sel, dims) + _dot(parts[2], sel, dims)
    return _dot(sel, parts[0], dims) + _dot(sel, parts[1], dims) + _dot(sel, parts[2], dims)


def _ssd_common(xbc_ref, dt_ref, bias_ref, alog_ref):
    r = lax.broadcasted_iota(jnp.int32, (CHUNK, CHUNK), 0)
    cidx = lax.broadcasted_iota(jnp.int32, (CHUNK, CHUNK), 1)
    causal = r >= cidx
    tril = causal.astype(bf16)
    expand = (lax.broadcasted_iota(jnp.int32, (CHUNK, SSM_INNER), 0)
              == lax.broadcasted_iota(jnp.int32, (CHUNK, SSM_INNER), 1) // HEAD_DIM).astype(bf16)
    head_lane = cidx < SSM_HEADS
    dtp = dt_ref[...] + bias_ref[...]
    dt = jnp.where(head_lane, _softplus(dtp), 0.0)
    a_neg = -jnp.exp(alog_ref[...])
    a = dt * a_neg
    nn_dims = ((1,), (0,))
    cs = _dot_exact(a, tril, nn_dims, x_is_lhs=False)
    dt_e = _dot_exact(dt, expand, nn_dims)
    cs_e = _dot_exact(cs, expand, nn_dims)
    xs = xbc_ref[:, 0:SSM_INNER]
    xg = xs * dt_e
    ecs = jnp.exp(cs_e)
    cs_last = cs_e[CHUNK - 1:CHUNK, :]
    dse = jnp.exp(cs_last - cs_e)
    cde = jnp.exp(cs_last)
    return dict(r=r, cidx=cidx, causal=causal, tril=tril, expand=expand, head_lane=head_lane, dtp=dtp, dt=dt, a_neg=a_neg,
                cs=cs, cst=cs.T, dt_e=dt_e, cs_e=cs_e, xs=xs, xg=xg, ecs=ecs, dse=dse, cde=cde)


def _decay_mat(q, h):
    return jnp.exp(jnp.where(q["causal"], q["cs"][:, h:h + 1] - q["cst"][h:h + 1, :], NEG_INF))


def _gate_norm(y, z, nw, gate=None):
    y2 = y * (_silu(z) if gate is None else gate)
    outs, xhats, rs = [], [], []
    for g in range(SSM_GROUPS):
        sl = slice(g * GROUP_W, (g + 1) * GROUP_W)
        yg = y2[:, sl]
        r = lax.rsqrt(jnp.mean(yg * yg, axis=-1, keepdims=True) + EPS)
        xhats.append(yg * r)
        rs.append(r)
        outs.append(yg * r * nw[:, sl])
    return y2, outs, xhats, rs


def _ssd_fwd(xbc, z, dtp, params, name):
    t = xbc.shape[0]
    n_chunk = SEQ // CHUNK
    n_step = n_chunk // SSD_CHUNKS_PER_STEP

    def body(xbc_ref, z_ref, dt_ref, bias_ref, alog_ref, dskip_ref, nw_ref, yn_ref, y_ref, hs_ref, h_scr):
        @pl.when(pl.program_id(1) == 0)
        def _():
            h_scr[...] = jnp.zeros_like(h_scr)

        for s in range(SSD_CHUNKS_PER_STEP):
            r = pl.ds(s * CHUNK, CHUNK)
            one_chunk(xbc_ref.at[r], z_ref.at[r], dt_ref.at[r], bias_ref, alog_ref, dskip_ref, nw_ref,
                      yn_ref.at[r], y_ref.at[r], hs_ref.at[pl.ds(s, 1)], h_scr)

    def one_chunk(xbc_ref, z_ref, dt_ref, bias_ref, alog_ref, dskip_ref, nw_ref, yn_ref, y_ref, hs_ref, h_scr):
        q = _ssd_common(xbc_ref, dt_ref, bias_ref, alog_ref)
        low = lax.broadcasted_iota(jnp.int32, (CHUNK, LANES), 1) < HEAD_DIM
        xgb = q["xg"].astype(bf16)
        wst = (q["xg"] * q["dse"]).astype(bf16)
        hs_ref[0] = h_scr[...]
        ys = []
        for g in range(SSM_GROUPS):
            gl = slice(g * GROUP_W, (g + 1) * GROUP_W)
            bg = xbc_ref[:, SSM_INNER + g * D_STATE:SSM_INNER + (g + 1) * D_STATE].astype(bf16)
            cg = xbc_ref[:, SSM_INNER + SSM_GROUPS * D_STATE + g * D_STATE:SSM_INNER + SSM_GROUPS * D_STATE + (g + 1) * D_STATE].astype(bf16)
            cb = _nt(cg, bg)
            hg = h_scr[g]
            yoff = _nn(cg, hg.astype(bf16)) * q["ecs"][:, gl]
            pieces = []
            for i in range(HEADS_PER_GROUP // 2):
                h0 = g * HEADS_PER_GROUP + 2 * i
                xp = xgb[:, h0 * HEAD_DIM:(h0 + 2) * HEAD_DIM]
                m0 = (cb * _decay_mat(q, h0)).astype(bf16)
                m1 = (cb * _decay_mat(q, h0 + 1)).astype(bf16)
                zero = jnp.zeros_like(xp)
                pieces.append(_nn(m0, jnp.where(low, xp, zero)) + _nn(m1, jnp.where(low, zero, xp)))
            ys.append(jnp.concatenate(pieces, axis=1) + yoff + dskip_ref[:, gl] * q["xs"][:, gl])
            h_scr[g] = hg * q["cde"][:, gl] + _tn(bg, wst[:, gl])
        y = jnp.concatenate(ys, axis=1)
        y_ref[...] = y
        _, outs, _, _ = _gate_norm(y, z_ref[...], nw_ref[...])
        yn_ref[...] = jnp.concatenate(outs, axis=1).astype(bf16)

    def rows(w):
        return pl.BlockSpec((SSD_CHUNKS_PER_STEP * CHUNK, w), lambda b, c: (b * n_step + c, 0))

    def par(w):
        return pl.BlockSpec((1, w), lambda b, c: (0, 0))

    return pl.pallas_call(
        body, name=name, grid=(t // SEQ, n_step),
        in_specs=[rows(CONV_CH), rows(SSM_INNER), rows(LANES), par(LANES), par(LANES), par(SSM_INNER), par(SSM_INNER)],
        out_specs=[rows(SSM_INNER), rows(SSM_INNER),
                   pl.BlockSpec((SSD_CHUNKS_PER_STEP, SSM_GROUPS, D_STATE, GROUP_W), lambda b, c: (b * n_step + c, 0, 0, 0))],
        out_shape=[jax.ShapeDtypeStruct((t, SSM_INNER), bf16), jax.ShapeDtypeStruct((t, SSM_INNER), f32),
                   jax.ShapeDtypeStruct((t // CHUNK, SSM_GROUPS, D_STATE, GROUP_W), f32)],
        scratch_shapes=[pltpu.VMEM((SSM_GROUPS, D_STATE, GROUP_W), f32)],
        compiler_params=_cparams(("parallel", "arbitrary")),
    )(xbc, z, dtp, *params)


def _ssd_bwd(xbc, z, dtp, y, hs, dyn, params, name):
    t = xbc.shape[0]
    n_chunk = SEQ // CHUNK
    n_step = n_chunk // SSD_CHUNKS_PER_STEP

    def body(xbc_ref, z_ref, dt_ref, y_ref, hs_ref, dyn_ref, bias_ref, alog_ref, dskip_ref, nw_ref,
             dxbc_ref, dz_ref, ddt_ref, dnw_ref, dds_ref, dal_ref, dbi_ref, dh_scr):
        @pl.when(pl.program_id(1) == 0)
        def _():
            dh_scr[...] = jnp.zeros_like(dh_scr)

        first_step = (pl.program_id(0) == 0) & (pl.program_id(1) == 0)
        for s in reversed(range(SSD_CHUNKS_PER_STEP)):
            r = pl.ds(s * CHUNK, CHUNK)
            one_chunk(xbc_ref.at[r], z_ref.at[r], dt_ref.at[r], y_ref.at[r], hs_ref.at[pl.ds(s, 1)], dyn_ref.at[r],
                      bias_ref, alog_ref, dskip_ref, nw_ref, dxbc_ref.at[r], dz_ref.at[r], ddt_ref.at[r],
                      dnw_ref, dds_ref, dal_ref, dbi_ref, dh_scr, first_step if s == SSD_CHUNKS_PER_STEP - 1 else False)

    def one_chunk(xbc_ref, z_ref, dt_ref, y_ref, hs_ref, dyn_ref, bias_ref, alog_ref, dskip_ref, nw_ref,
                  dxbc_ref, dz_ref, ddt_ref, dnw_ref, dds_ref, dal_ref, dbi_ref, dh_scr, first):
        q = _ssd_common(xbc_ref, dt_ref, bias_ref, alog_ref)
        low = lax.broadcasted_iota(jnp.int32, (CHUNK, LANES), 1) < HEAD_DIM
        last_row = lax.broadcasted_iota(jnp.int32, (CHUNK, GROUP_W), 0) == CHUNK - 1
        xs, xg = q["xs"], q["xg"]
        xgb = xg.astype(bf16)
        wf = xg * q["dse"]
        wst = wf.astype(bf16)
        zz = z_ref[...]
        yy = y_ref[...]
        sz, dsz = _silu_and_grad(zz)
        y2, _, xhats, rs = _gate_norm(yy, zz, nw_ref[...], gate=sz)
        dyn_ = dyn_ref[...]
        dy2s, dnws = [], []
        for g in range(SSM_GROUPS):
            gl = slice(g * GROUP_W, (g + 1) * GROUP_W)
            gw = dyn_[:, gl] * nw_ref[:, gl]
            dy2s.append(rs[g] * (gw - xhats[g] * jnp.mean(gw * xhats[g], axis=-1, keepdims=True)))
            dnws.append(_rowsum8(dyn_[:, gl] * xhats[g]))
        dy2 = jnp.concatenate(dy2s, axis=1)
        dy = dy2 * sz
        dz_ref[...] = (dy2 * yy * dsz).astype(bf16)
        dnw_p = jnp.concatenate(dnws, axis=1)
        dds_p = _rowsum8(dy * xs)
        dyb = dy.astype(bf16)
        gfull = (dy * q["ecs"]).astype(bf16)
        dcs_c = jnp.zeros((CHUNK, CHUNK), f32)
        dcs_r = jnp.zeros((CHUNK, CHUNK), f32)
        dcs_e_parts, dxg_parts = [], []
        for g in range(SSM_GROUPS):
            gl = slice(g * GROUP_W, (g + 1) * GROUP_W)
            bsl = slice(SSM_INNER + g * D_STATE, SSM_INNER + (g + 1) * D_STATE)
            csl = slice(SSM_INNER + SSM_GROUPS * D_STATE + g * D_STATE, SSM_INNER + SSM_GROUPS * D_STATE + (g + 1) * D_STATE)
            bg = xbc_ref[:, bsl].astype(bf16)
            cg = xbc_ref[:, csl].astype(bf16)
            cb = _nt(cg, bg)
            hg = hs_ref[0, g]
            hgb = hg.astype(bf16)
            dhn = dh_scr[g]
            dhnb = dhn.astype(bf16)
            yoff = _nn(cg, hgb) * q["ecs"][:, gl]
            dw_ = _nn(bg, dhnb)
            r_e = dw_ * wf[:, gl]
            to_last = jnp.sum(r_e, axis=0, keepdims=True) + jnp.sum(dhn * hg, axis=0, keepdims=True) * q["cde"][:, gl]
            dcs_e_parts.append(dy[:, gl] * yoff - r_e + jnp.where(last_row, to_last, 0.0))
            dcb = jnp.zeros((CHUNK, CHUNK), f32)
            dxg_pairs = []
            for i in range(HEADS_PER_GROUP // 2):
                h0 = g * HEADS_PER_GROUP + 2 * i
                psl = slice(h0 * HEAD_DIM, (h0 + 2) * HEAD_DIM)
                xp = xgb[:, psl]
                dyp = dyb[:, psl]
                zero = jnp.zeros_like(dyp)
                tns = []
                for a in range(2):
                    h = h0 + a
                    lm = _decay_mat(q, h)
                    m = cb * lm
                    dm = _nt(jnp.where(low, dyp, zero) if a == 0 else jnp.where(low, zero, dyp), xp)
                    dcb = dcb + dm * lm
                    nmat = dm * m
                    dcs_c = dcs_c + jnp.where(q["cidx"] == h, jnp.sum(nmat, axis=1, keepdims=True), 0.0)
                    dcs_r = dcs_r + jnp.where(q["r"] == h, jnp.sum(nmat, axis=0, keepdims=True), 0.0)
                    tns.append(_tn(m.astype(bf16), dyp))
                dxg_pairs.append(jnp.where(low, tns[0], tns[1]))
            dxg_parts.append(jnp.concatenate(dxg_pairs, axis=1) + dw_ * q["dse"][:, gl])
            dcbb = dcb.astype(bf16)
            dxbc_ref[:, csl] = _nt(gfull[:, gl], hgb) + _nn(dcbb, bg)
            dxbc_ref[:, bsl] = _nt(wst[:, gl], dhnb) + _tn(dcbb, cg)
            dh_scr[g] = dhn * q["cde"][:, gl] + _tn(cg, gfull[:, gl])
        dxg = jnp.concatenate(dxg_parts, axis=1)
        dcs_e = jnp.concatenate(dcs_e_parts, axis=1)
        dxbc_ref[:, 0:SSM_INNER] = dskip_ref[...] * dy + dxg * q["dt_e"]
        dcs = dcs_c - dcs_r.T + _dot_exact(dcs_e, q["expand"], ((1,), (1,)))
        triu = (q["cidx"] >= q["r"]).astype(bf16)
        da = _dot_exact(dcs, triu, ((1,), (0,)), x_is_lhs=False)
        ddt = _dot_exact(dxg * xs, q["expand"], ((1,), (1,))) + da * q["a_neg"]
        ddtp = jnp.where(q["head_lane"], ddt * _sigmoid(q["dtp"]), 0.0)
        ddt_ref[...] = ddtp.astype(bf16)
        dal_p = _rowsum8(da * q["dt"]) * q["a_neg"]
        dbi_p = _rowsum8(ddtp)
        def accumulate():
            dnw_ref[...] += dnw_p
            dds_ref[...] += dds_p
            dal_ref[...] += dal_p
            dbi_ref[...] += dbi_p

        if first is False:
            accumulate()
        else:
            @pl.when(first)
            def _():
                dnw_ref[...] = dnw_p
                dds_ref[...] = dds_p
                dal_ref[...] = dal_p
                dbi_ref[...] = dbi_p

            pl.when(jnp.logical_not(first))(accumulate)

    def rows(w):
        return pl.BlockSpec((SSD_CHUNKS_PER_STEP * CHUNK, w), lambda b, c: (b * n_step + n_step - 1 - c, 0))

    def par(w):
        return pl.BlockSpec((1, w), lambda b, c: (0, 0))

    def acc(w):
        return pl.BlockSpec((SUBLANES, w), lambda b, c: (0, 0))

    return pl.pallas_call(
        body, name=name, grid=(t // SEQ, n_step),
        in_specs=[rows(CONV_CH), rows(SSM_INNER), rows(LANES), rows(SSM_INNER),
                  pl.BlockSpec((SSD_CHUNKS_PER_STEP, SSM_GROUPS, D_STATE, GROUP_W), lambda b, c: (b * n_step + n_step - 1 - c, 0, 0, 0)),
                  rows(SSM_INNER), par(LANES), par(LANES), par(SSM_INNER), par(SSM_INNER)],
        out_specs=[rows(CONV_CH), rows(SSM_INNER), rows(LANES), acc(SSM_INNER), acc(SSM_INNER), acc(LANES), acc(LANES)],
        out_shape=[jax.ShapeDtypeStruct((t, CONV_CH), f32), jax.ShapeDtypeStruct((t, SSM_INNER), bf16), jax.ShapeDtypeStruct((t, LANES), bf16),
                   jax.ShapeDtypeStruct((SUBLANES, SSM_INNER), f32), jax.ShapeDtypeStruct((SUBLANES, SSM_INNER), f32),
                   jax.ShapeDtypeStruct((SUBLANES, LANES), f32), jax.ShapeDtypeStruct((SUBLANES, LANES), f32)],
        scratch_shapes=[pltpu.VMEM((SSM_GROUPS, D_STATE, GROUP_W), f32)],
        compiler_params=_cparams(("arbitrary", "arbitrary")),
    )(xbc, z, dtp, y, hs, dyn, *params)


def _adamw_update(g, w, m, v):
    mm = ADAM_B1 * m + (1.0 - ADAM_B1) * g
    vv = ADAM_B2 * v + (1.0 - ADAM_B2) * (g * g)
    m_hat = mm / (1.0 - ADAM_B1 ** ADAM_STEP)
    v_hat = vv / (1.0 - ADAM_B2 ** ADAM_STEP)
    return -ADAM_LR * (m_hat / (jnp.sqrt(v_hat) + ADAM_EPS) + ADAM_WD * w), mm, vv


def _adamw(g_parts, w, m, v, name):
    rows, width = w.shape
    n = len(g_parts)
    tr = _row_tile(rows)

    def body(*refs):
        g_refs, (w_ref, m_ref, v_ref, g_out, d_out, m_out, v_out) = refs[:n], refs[n:]
        g = g_refs[0][...].astype(f32)
        for r in g_refs[1:]:
            g = g + r[...].astype(f32)
        g_out[...] = g
        d_out[...], m_out[...], v_out[...] = _adamw_update(g, w_ref[...], m_ref[...], v_ref[...])

    spec = pl.BlockSpec((tr, width), lambda i: (i, 0))
    return pl.pallas_call(
        body, name=name, grid=(rows // tr,), in_specs=[spec] * (n + 3), out_specs=[spec] * 4,
        out_shape=[jax.ShapeDtypeStruct((rows, width), f32)] * 4, compiler_params=_cparams(("parallel",)),
    )(*g_parts, w, m, v)


def _adamw_layers(landed, w, m, v, after, name, layers_on_columns=False):
    depth = len(landed)
    _, rows, width = landed[0].shape
    tr = _row_tile(rows)
    n_i = rows // tr
    at = (lambda ref: ref) if layers_on_columns else (lambda ref: ref.at[0])

    def body(*refs):
        part_refs, (w_ref, m_ref, v_ref, _, g_out, d_out, m_out, v_out) = refs[:depth * N_DEV], refs[depth * N_DEV:]
        for l in range(depth):
            @pl.when(pl.program_id(0) == l)
            def _(l=l):
                g = part_refs[l * N_DEV][0].astype(f32)
                for r in part_refs[l * N_DEV + 1:(l + 1) * N_DEV]:
                    g = g + r[0].astype(f32)
                at(g_out)[...] = g
                at(d_out)[...], at(m_out)[...], at(v_out)[...] = _adamw_update(g, at(w_ref)[...], at(m_ref)[...], at(v_ref)[...])

    def part_spec(l, p):
        return pl.BlockSpec((1, tr, width), lambda ll, i: (p, jnp.where(ll == l, i, jnp.where(ll < l, 0, n_i - 1)), 0))

    state = (pl.BlockSpec((tr, width), lambda ll, i: (i, ll)) if layers_on_columns
             else pl.BlockSpec((1, tr, width), lambda ll, i: (ll, i, 0)))
    return pl.pallas_call(
        body, name=name, grid=(depth, n_i),
        in_specs=[part_spec(l, p) for l in range(depth) for p in range(N_DEV)] + [state] * 3 + [ANY], out_specs=[state] * 4,
        out_shape=[jax.ShapeDtypeStruct(w.shape, f32)] * 4, compiler_params=_cparams(("arbitrary", "arbitrary")),
    )(*[landed[l] for l in range(depth) for _ in range(N_DEV)], w, m, v, after)


def _row_tile(rows, cap=512):
    for cand in range(min(rows, cap) // SUBLANES * SUBLANES, 0, -SUBLANES):
        if rows % cand == 0:
            return cand
    return rows


def _cols_from_devices(g, width, name):
    n_dev, depth, a, b = g.shape

    def body(g_ref, o_ref):
        for i in range(n_dev):
            o_ref[0, :, i * b:(i + 1) * b] = g_ref[i, 0]
        if width > n_dev * b:
            o_ref[0, :, n_dev * b:width] = jnp.zeros((a, width - n_dev * b), o_ref.dtype)

    return pl.pallas_call(
        body, name=name, grid=(depth,), in_specs=[pl.BlockSpec((n_dev, 1, a, b), lambda l: (0, l, 0, 0))],
        out_specs=pl.BlockSpec((1, a, width), lambda l: (l, 0, 0)), out_shape=jax.ShapeDtypeStruct((depth, a, width), g.dtype),
        compiler_params=_cparams(("parallel",)),
    )(g)


def _devices_from_cols(per_layer, b, name, tr=256):
    depth = len(per_layer)
    a, width = per_layer[0].shape

    def body(*refs):
        o_ref = refs[depth]
        for l in range(depth):
            for i in range(N_DEV):
                o_ref[i, l] = refs[l][:, i * b:(i + 1) * b]

    return pl.pallas_call(
        body, name=name, grid=(a // tr,), in_specs=[pl.BlockSpec((tr, width), lambda r: (r, 0))] * depth,
        out_specs=pl.BlockSpec((N_DEV, depth, tr, b), lambda r: (0, 0, r, 0)),
        out_shape=jax.ShapeDtypeStruct((N_DEV, depth, a, b), per_layer[0].dtype), compiler_params=_cparams(("parallel",)),
    )(*per_layer)


def _me():
    return lax.axis_index("x"), lax.axis_index("y"), lax.axis_index("c")


def _allgather_two_level(shards, name):
    n = len(shards)
    per = 7

    def body(*refs):
        ins, outs, token = refs[:n], refs[n:2 * n], refs[2 * n]
        send_sems, recv_sems, local_sems = refs[2 * n + 1:]
        token[...] = jnp.zeros_like(token)
        x, y, c = _me()
        me, sibling = (x, y, c), (x, y, 1 - c)
        chips = [(1 - x, y), (x, 1 - y), (1 - x, 1 - y)]

        def slot(a, p):
            return outs[a].at[4 * p[0] + 2 * p[1] + p[2]]

        def copy(a, k, block, to, src=None):
            return pltpu.make_async_remote_copy(
                src_ref=slot(a, block) if src is None else src, dst_ref=slot(a, block),
                send_sem=send_sems.at[a * per + k], recv_sem=recv_sems.at[a * per + k], device_id=to, device_id_type=MESH)

        mine = [pltpu.make_async_copy(ins[a], slot(a, me), local_sems.at[a]) for a in range(n)]
        for cp in mine:
            cp.start()
        first = []
        for a in range(n):
            first.append(copy(a, 0, me, sibling, src=ins[a]))
            first += [copy(a, 1 + j, me, (*chip, c), src=ins[a]) for j, chip in enumerate(chips)]
        for cp in first:
            cp.start()
        passed = []
        for j, chip in enumerate(chips):
            for a in range(n):
                copy(a, 1 + j, (*chip, c), me).wait_recv()
                fwd = copy(a, 4 + j, (*chip, c), sibling)
                fwd.start()
                passed.append(fwd)
        for a in range(n):
            copy(a, 0, sibling, me).wait_recv()
            for j, chip in enumerate(chips):
                copy(a, 4 + j, (*chip, 1 - c), me).wait_recv()
        for cp in first + passed:
            cp.wait_send()
        for cp in mine:
            cp.wait()

    outs = pl.pallas_call(
        body, name=name, in_specs=[ANY] * n, out_specs=[ANY] * n + [pl.BlockSpec(memory_space=pltpu.VMEM)],
        out_shape=[jax.ShapeDtypeStruct((N_DEV,) + s.shape, s.dtype) for s in shards] + [jax.ShapeDtypeStruct((SUBLANES, LANES), f32)],
        scratch_shapes=[pltpu.SemaphoreType.DMA((n * per,)), pltpu.SemaphoreType.DMA((n * per,)), pltpu.SemaphoreType.DMA((n,))],
    )(*shards)
    return outs[:n], outs[n]


def _allgather_direct(row, name):
    def body(in_ref, out_ref, send_sems, recv_sems, local_sem):
        x, y, c = _me()
        mine = out_ref.at[4 * x + 2 * y + c]
        local = pltpu.make_async_copy(in_ref, mine, local_sem)
        local.start()
        sends = []
        for k in range(1, N_DEV):
            px, py, pc = x ^ (k >> 2), y ^ ((k >> 1) & 1), c ^ (k & 1)
            sends.append(pltpu.make_async_remote_copy(
                src_ref=in_ref, dst_ref=mine, send_sem=send_sems.at[k - 1], recv_sem=recv_sems.at[k - 1],
                device_id=(px, py, pc), device_id_type=MESH))
        for cp in sends:
            cp.start()
        for k in range(1, N_DEV):
            px, py, pc = x ^ (k >> 2), y ^ ((k >> 1) & 1), c ^ (k & 1)
            theirs = out_ref.at[4 * px + 2 * py + pc]
            pltpu.make_async_remote_copy(
                src_ref=in_ref, dst_ref=theirs, send_sem=send_sems.at[k - 1], recv_sem=recv_sems.at[k - 1],
                device_id=(px, py, pc), device_id_type=MESH).wait_recv()
        for cp in sends:
            cp.wait_send()
        local.wait()

    return pl.pallas_call(
        body, name=name, in_specs=[ANY], out_specs=ANY, out_shape=jax.ShapeDtypeStruct((N_DEV,) + row.shape, row.dtype),
        scratch_shapes=[pltpu.SemaphoreType.DMA((N_DEV - 1,)), pltpu.SemaphoreType.DMA((N_DEV - 1,)), pltpu.SemaphoreType.DMA],
    )(row)


N_CHIP = N_DEV // 2
HBM = pl.BlockSpec(memory_space=pltpu.HBM)
SEM = pl.BlockSpec(memory_space=pltpu.SEMAPHORE)
EFFECT = pltpu.SideEffectType.DATAFLOW_SIDE_EFFECTING


def _peer(k):
    x, y, c = _me()
    return x ^ (k >> 2), y ^ ((k >> 1) & 1), c ^ (k & 1)


def _direct_copies(srcs, lands, send_sems, recv_sems, per_peer):
    x, y, c = _me()
    me = 4 * x + 2 * y + c
    copies = []
    for a in range(len(srcs)):
        for k in range(1, N_DEV):
            px, py, pc = _peer(k)
            piece = srcs[a].at[4 * px + 2 * py + pc] if per_peer else srcs[a]
            copies.append(pltpu.make_async_remote_copy(
                src_ref=piece, dst_ref=lands[a].at[me], send_sem=send_sems.at[a * (N_DEV - 1) + k - 1],
                recv_sem=recv_sems.at[a * (N_DEV - 1) + k - 1], device_id=(px, py, pc), device_id_type=MESH))
    return copies


def _direct_start(srcs, lands, per_peer, name):
    n = len(srcs)
    n_sem = n * (N_DEV - 1)

    def body(*refs):
        src_refs, land_refs = refs[:n], refs[n:2 * n]
        send_sems, recv_sems = refs[2 * n], refs[2 * n + 1]
        token = refs[-1]
        for cp in _direct_copies(src_refs, land_refs, send_sems, recv_sems, per_peer):
            cp.start()
        token[...] = jnp.zeros_like(token)

    outs = pl.pallas_call(
        body, name=name,
        out_shape=(pltpu.SemaphoreType.DMA((n_sem,)), pltpu.SemaphoreType.DMA((n_sem,)),
                   *[pltpu.HBM(s.shape, s.dtype) for s in srcs], *[pltpu.HBM(s.shape, s.dtype) for s in lands],
                   jax.ShapeDtypeStruct((SUBLANES, LANES), f32)),
        in_specs=[HBM] * (2 * n), out_specs=(SEM, SEM, *[HBM] * (2 * n), pl.BlockSpec(memory_space=pltpu.VMEM)),
        input_output_aliases={i: 2 + i for i in range(2 * n)},
        compiler_params=pltpu.CompilerParams(has_side_effects=EFFECT),
    )(*[pltpu.with_memory_space_constraint(s, pltpu.HBM) for s in srcs], *[pltpu.with_memory_space_constraint(s, pltpu.HBM) for s in lands])
    return outs[0], outs[1], outs[2:2 + n], outs[2 + n:2 + 2 * n], outs[-1]


def _direct_wait(send_sems, recv_sems, srcs, lands, after, per_peer, name):
    n = len(srcs)

    def body(*refs):
        src_refs, land_refs = refs[:n], refs[n:2 * n]
        s_sems, r_sems = refs[2 * n], refs[2 * n + 1]
        for cp in _direct_copies(src_refs, land_refs, s_sems, r_sems, per_peer):
            cp.wait_send()
            cp.wait_recv()

    outs = pl.pallas_call(
        body, name=name,
        out_shape=tuple(pltpu.HBM(s.shape, s.dtype) for s in list(srcs) + list(lands)),
        in_specs=[HBM] * (2 * n) + [SEM, SEM, ANY], out_specs=tuple([HBM] * (2 * n)),
        input_output_aliases={i: i for i in range(2 * n)},
        compiler_params=pltpu.CompilerParams(has_side_effects=EFFECT),
    )(*srcs, *lands, send_sems, recv_sems, after)
    return outs[n:]


def _row(v, width=None):
    v = v.reshape(1, -1).astype(f32)
    if width is not None and v.shape[1] < width:
        v = jnp.pad(v, ((0, 0), (0, width - v.shape[1])))
    return v


def _layer_params(p, l):
    return dict(
        norm_mix=_row(p["norm_mix"][l]), norm_ffn=_row(p["norm_ffn"][l]), conv_w=p["conv_w"][l], conv_b=_row(p["conv_b"][l]),
        ssd=(_row(p["dt_bias"][l], LANES), _row(p["a_log"][l], LANES), _row(jnp.repeat(p["d_skip"][l], HEAD_DIM)), _row(p["ssm_norm"][l])))


def _layer_fwd(h, w_in, rest, sp, tabs, l):
    tag = f"l{l}_"
    hn = _rmsnorm_fwd(h, sp["norm_mix"], tag + "norm_mix")
    qkv, z, xbc_pre = _in_proj(hn, w_in, (QKV_WIDTH, SSM_INNER, CONV_CH), tag + "proj")
    dtp = _matmul(hn, w_in, mode="nn", n_out=LANES, tn=LANES, b_off=DT_OFF // LANES, name=tag + "proj_dt")
    prep = _attn_prep(qkv, tabs, tag + "attn_prep")
    o, o16, lse = _attn_fwd(prep, tag + "attn_fwd")
    xbc = _conv_fwd(xbc_pre, sp["conv_w"], sp["conv_b"], tag + "conv_fwd")
    yn, y, hs = _ssd_fwd(xbc, z, dtp, sp["ssd"], tag + "ssd_fwd")
    w_out, w_gate, w_up, w_down = rest(yn) if callable(rest) else rest
    h2 = _out_proj(o16, yn, w_out, h, tag + "out_proj")
    hn2 = _rmsnorm_fwd(h2, sp["norm_ffn"], tag + "norm_ffn")
    g, u, act = _swiglu_fwd(hn2, w_gate, w_up, tag + "ffn_up")
    h3 = _matmul(act, w_down, mode="nn", tk=FFN_HIDDEN, add=h2, name=tag + "ffn_down")
    saved = dict(h=h, hn=hn, prep=prep, z=z, xbc_pre=xbc_pre, dtp=dtp, o=o, o16=o16, lse=lse, xbc=xbc, yn=yn, y=y, hs=hs, h2=h2, hn2=hn2, g=g, u=u, act=act,
                 rest=(w_out, w_gate, w_up, w_down))
    return h3, saved


def _layer_bwd(dh3_pair, s, big, sp, tabs, l, gd=f32, after_ffn=None):
    tag = f"l{l}_"
    dh3, dh3b = dh3_pair
    w_in, w_out, w_gate, w_up, w_down = big
    dg, du = _swiglu_bwd(dh3b, w_down, s["g"], s["u"], tag + "ffn_down_bwd")
    dw_down = _matmul(s["act"], dh3b, mode="tn", tm=1408, tn=512, tk=2048, out_dtype=gd, name=tag + "dw_down")
    dw_gate = _matmul(dg, s["hn2"], mode="tn", tm=1408, tn=512, tk=2048, out_dtype=gd, name=tag + "dw_gate")
    dw_up = _matmul(du, s["hn2"], mode="tn", tm=1408, tn=512, tk=2048, out_dtype=gd, name=tag + "dw_up")
    norm_ffn = sp["norm_ffn"] if after_ffn is None else sp["norm_ffn"] + after_ffn(dict(w_gate=dw_gate, w_up=dw_up, w_down=dw_down))
    dh2, dh2b, dnf = _nt_norm_bwd([(dg, w_gate), (du, w_up)], s["h2"], norm_ffn, dh3, tag + "ffn_up_bwd_norm", tk=1408, b_is_kd=True,
                                  vmem=VMEM_LIMIT_TWO_PAIRS)
    d_o = _matmul(dh2b, w_out, mode="nt", n_out=ATTN_WIDTH, tn=512, b_off=0, name=tag + "out_attn_bwd")
    dyn = _matmul(dh2b, w_out, mode="nt", n_out=SSM_INNER, tn=512, b_off=1, name=tag + "out_ssm_bwd")
    dw_out = jnp.concatenate([_matmul(s["o16"], dh2b, mode="tn", tm=512, tn=512, tk=2048, out_dtype=gd, name=tag + "dw_out_attn"),
                              _matmul(s["yn"], dh2b, mode="tn", tm=512, tn=512, tk=2048, out_dtype=gd, name=tag + "dw_out_ssm")], axis=0)
    dxbc, dz, ddtp, dnw, dds, dal, dbi = _ssd_bwd(s["xbc"], s["z"], s["dtp"], s["y"], s["hs"], dyn, sp["ssd"], tag + "ssd_bwd")
    dxbc_pre, dconv_w, dconv_b = _conv_bwd(s["xbc_pre"], sp["conv_w"], sp["conv_b"], dxbc, tag + "conv_bwd")
    dq, dk, dv = _attn_bwd(s["prep"], tabs, s["o"], s["lse"], d_o, tag + "attn_bwd")
    dproj = jnp.concatenate([dq, dk, dv, dz, dxbc_pre, ddtp], axis=1)
    dw_in = _matmul(s["hn"], dproj, mode="tn", tm=512, tn=1152, tk=2048, out_dtype=gd, name=tag + "dw_in")
    res = _nt_norm_bwd([(dproj, w_in)], s["h"], sp["norm_mix"], dh2, tag + "proj_bwd_norm", tk=1152, bf16_copy=l > 0)
    dh, dhb, dnm = res if l > 0 else (res[0], None, res[1])
    grads = dict(
        norm_mix=dnm.sum(0), w_in=dw_in, conv_w=dconv_w, conv_b=dconv_b[0], dt_bias=dbi.sum(0)[:SSM_HEADS], a_log=dal.sum(0)[:SSM_HEADS],
        d_skip=dds.sum(0).reshape(SSM_HEADS, HEAD_DIM).sum(1), ssm_norm=dnw.sum(0), w_out=dw_out, norm_ffn=dnf.sum(0),
        w_gate=dw_gate, w_up=dw_up, w_down=dw_down)
    return (dh, dhb), grads


def _local_step(x, positions, target, p, bigs):
    tabs = _rope_tables(positions.reshape(-1, 1), "rope_tables")
    h = x
    saved, sps = [], []
    for l in range(DEPTH):
        sps.append(_layer_params(p, l))
        h, s = _layer_fwd(h, bigs[l][0], bigs[l][1:], sps[l], tabs, l)
        saved.append(s)
    dh, dhb, loss_parts, dfn = _final_loss(h, _row(p["final_norm"]), target, "final_loss")
    dh = (dh, dhb)
    layer_grads = [None] * DEPTH
    for l in reversed(range(DEPTH)):
        dh, layer_grads[l] = _layer_bwd(dh, saved[l], bigs[l], sps[l], tabs, l)
    grads = {k: [layer_grads[l][k] for l in range(DEPTH)] for k in layer_grads[0]}
    grads["final_norm"] = dfn.sum(0)
    return jnp.sum(loss_parts), dh[0], grads


BIG = ("w_in", "w_out", "w_gate", "w_up", "w_down")
REST = BIG[1:]
FFN = ("w_gate", "w_up", "w_down")
MIX = ("w_in", "w_out")
COL_SHARDED = ("w_in",)
TRANSPOSED = ("w_gate", "w_up")
SMALL = ("norm_mix", "conv_b", "dt_bias", "a_log", "d_skip", "ssm_norm", "norm_ffn", "final_norm")
WEIGHTS = ("norm_mix", "w_in", "conv_w", "conv_b", "dt_bias", "a_log", "d_skip", "ssm_norm", "w_out", "norm_ffn", "w_gate", "w_up", "w_down", "final_norm")
SMALL_ROWS = 88
CONVW_ROWS = 96
CONVW_SHARD_ROWS = 16


def _full_from_gathered(name, g, l):
    _, a, b = g.shape
    if name in COL_SHARDED:
        width = IN_PROJ_PAD if name == "w_in" else N_DEV * b
        return _cols_from_devices(g.reshape(N_DEV, 1, a, b), width, f"cols_l{l}_{name}").reshape(a, width)
    return g.reshape(N_DEV * a, b)


def _by_device(name, full, shard_shape, l):
    a, b = shard_shape
    if name in COL_SHARDED:
        return _devices_from_cols([full], b, f"devs_l{l}_{name}").reshape(N_CHIP, 2, a, b)
    return full.reshape(N_CHIP, 2, a, b)


def _pack_rows(parts, rows, width):
    flat = jnp.concatenate([q.reshape(-1) for q in parts])
    return jnp.pad(flat, (0, rows * width - flat.shape[0])).reshape(rows, width)


def _unpack(flat, like):
    out, off = [], 0
    for q in like:
        out.append(flat[off:off + q.size].reshape(q.shape))
        off += q.size
    return out


def kernel(x, positions, norm_mix, w_in, conv_w, conv_b, dt_bias, a_log, d_skip, ssm_norm, w_out, norm_ffn, w_gate, w_up, w_down, final_norm, loss_target, m_norm_mix, m_w_in, m_conv_w, m_conv_b, m_dt_bias, m_a_log, m_d_skip, m_ssm_norm, m_w_out, m_norm_ffn, m_w_gate, m_w_up, m_w_down, m_final_norm, v_norm_mix, v_w_in, v_conv_w, v_conv_b, v_dt_bias, v_a_log, v_d_skip, v_ssm_norm, v_w_out, v_norm_ffn, v_w_gate, v_w_up, v_w_down, v_final_norm):
    w = dict(norm_mix=norm_mix, w_in=w_in, conv_w=conv_w, conv_b=conv_b, dt_bias=dt_bias, a_log=a_log, d_skip=d_skip, ssm_norm=ssm_norm,
             w_out=w_out, norm_ffn=norm_ffn, w_gate=w_gate, w_up=w_up, w_down=w_down, final_norm=final_norm)
    m = dict(norm_mix=m_norm_mix, w_in=m_w_in, conv_w=m_conv_w, conv_b=m_conv_b, dt_bias=m_dt_bias, a_log=m_a_log, d_skip=m_d_skip,
             ssm_norm=m_ssm_norm, w_out=m_w_out, norm_ffn=m_norm_ffn, w_gate=m_w_gate, w_up=m_w_up, w_down=m_w_down, final_norm=m_final_norm)
    v = dict(norm_mix=v_norm_mix, w_in=v_w_in, conv_w=v_conv_w, conv_b=v_conv_b, dt_bias=v_dt_bias, a_log=v_a_log, d_skip=v_d_skip,
             ssm_norm=v_ssm_norm, w_out=v_w_out, norm_ffn=v_norm_ffn, w_gate=v_w_gate, w_up=v_w_up, w_down=v_w_down, final_norm=v_final_norm)
    ax, ay, ac = lax.axis_index("x"), lax.axis_index("y"), lax.axis_index("c")
    dev = 4 * ax + 2 * ay + ac

    assert DEPTH == 2
    t = x.shape[0] * x.shape[1]
    xf, target = x.reshape(t, D_MODEL), loss_target.reshape(t, D_MODEL)

    def own_slot(block):
        return lax.dynamic_update_slice(lax.empty((N_DEV,) + block.shape[1:], block.dtype), block, (dev,) + (0,) * (block.ndim - 1))

    def layer_shard(arr, k, l):
        return jnp.transpose(arr, (2, 0, 1))[:, l, :] if k in TRANSPOSED else arr[l]

    def gather_start(keys, l, tie, name):
        shards = [(layer_shard(w[keys[0]], keys[0], l) + tie).astype(bf16)] + [layer_shard(w[k], k, l).astype(bf16) for k in keys[1:]]
        return _direct_start(shards, [own_slot(s[None]) for s in shards], False, name)

    def scatter_start(keys, grads_l, l, name):
        shapes = [(w[k].shape[2], w[k].shape[1]) if k in TRANSPOSED else w[k].shape[1:] for k in keys]
        by_dev = [_by_device(k, grads_l[k], sh, l).reshape((N_DEV,) + sh) for k, sh in zip(keys, shapes)]
        return _direct_start(by_dev, [own_slot(lax.dynamic_slice_in_dim(g, dev, 1, 0)) for g in by_dev], True, name)

    (g_in0, conv_all), tie = _allgather_two_level([w["w_in"][0].astype(bf16), w["conv_w"]], "gather_l0_w_in")
    rest0_copy = gather_start(REST, 0, tie[0, 0], "gather_l0_rest_start")
    l1_copy = gather_start(BIG, 1, rest0_copy[4][0, 0], "gather_l1_start")
    p = {k: w[k] for k in SMALL}
    p["norm_mix"] = p["norm_mix"] + l1_copy[4][0, 0]
    p["conv_w"] = jnp.transpose(conv_all, (1, 2, 0, 3)).reshape(DEPTH, CONV_WIDTH, CONV_CH)
    sp0, sp1 = _layer_params(p, 0), _layer_params(p, 1)

    def rest0(after):
        lands = _direct_wait(*rest0_copy[:4], after, False, "gather_l0_rest_wait")
        return tuple(_full_from_gathered(k, g, 0) for k, g in zip(REST, lands))

    tabs = _rope_tables(positions.reshape(t, 1), "rope_tables")
    w_in0 = _full_from_gathered("w_in", g_in0, 0)
    h1, saved0 = _layer_fwd(xf, w_in0, rest0, sp0, tabs, 0)
    lands1 = _direct_wait(*l1_copy[:4], h1, False, "gather_l1_wait")
    bigs1 = tuple(_full_from_gathered(k, g, 1) for k, g in zip(BIG, lands1))
    h2, saved1 = _layer_fwd(h1, bigs1[0], bigs1[1:], sp1, tabs, 1)
    dh, dhb, loss_parts, dfn = _final_loss(h2, _row(p["final_norm"]), target, "final_loss")
    loss_local = jnp.sum(loss_parts)

    dh, grads1 = _layer_bwd((dh, dhb), saved1, bigs1, sp1, tabs, 1, gd=bf16)
    l1_grads = scatter_start(BIG, grads1, 1, "scatter_l1_start")
    w_out0, w_gate0, w_up0, w_down0 = saved0["rest"]
    bigs0 = (w_in0, w_out0, w_gate0, w_up0, w_down0 + l1_grads[4][0, 0].astype(bf16))
    ffn0_grads = []

    def after_ffn(grads_ffn):
        ffn0_grads.append(scatter_start(FFN, grads_ffn, 0, "scatter_l0_ffn_start"))
        return ffn0_grads[0][4][0, 0]

    (dx, _), grads0 = _layer_bwd(dh, saved0, bigs0, sp0, tabs, 0, gd=bf16, after_ffn=after_ffn)
    mix0_grads = scatter_start(MIX, grads0, 0, "scatter_l0_mix_start")
    landed = {(k, 1): g for k, g in zip(BIG, _direct_wait(*l1_grads[:4], dx, True, "scatter_l1_wait"))}
    landed.update({(k, 0): g for k, g in zip(FFN, _direct_wait(*ffn0_grads[0][:4], dx, True, "scatter_l0_ffn_wait"))})
    out_g, out_d, out_m, out_v = {}, {}, {}, {}

    def update(keys, after):
        for k in keys:
            parts = [landed[k, l] for l in range(DEPTH)]
            if k in TRANSPOSED:
                depth, a, b = w[k].shape
                state = [jnp.transpose(s, (2, 0, 1)).reshape(b, depth * a) for s in (w[k], m[k], v[k])]
                res = _adamw_layers(parts, *state, after, "adamw_" + k, layers_on_columns=True)
                res = [jnp.transpose(r.reshape(b, depth, a), (1, 2, 0)) for r in res]
            else:
                res = _adamw_layers(parts, w[k], m[k], v[k], after, "adamw_" + k)
            for dst, r in zip((out_g, out_d, out_m, out_v), res):
                dst[k] = r

    update(FFN, mix0_grads[4])
    grads = {k: [grads0[k], grads1[k]] for k in grads0 if k not in BIG}
    grads["final_norm"] = dfn.sum(0) + mix0_grads[4][0, 0]

    small_like = [w[k] for k in SMALL]
    small_grads = [jnp.stack(grads[k]) if k != "final_norm" else grads[k] for k in SMALL]
    small_pack = jnp.concatenate([_pack_rows(small_grads, SMALL_ROWS, LANES), _pack_rows([jnp.stack(grads["conv_w"])], CONVW_ROWS, LANES)], axis=0)
    parts = _allgather_direct(small_pack, "gather_small_grads")
    g_s, d_s, m_s, v_s = _adamw(
        [parts[i, :SMALL_ROWS] for i in range(N_DEV)], _pack_rows(small_like, SMALL_ROWS, LANES),
        _pack_rows([m[k] for k in SMALL], SMALL_ROWS, LANES), _pack_rows([v[k] for k in SMALL], SMALL_ROWS, LANES), "adamw_replicated")
    for dst, src in ((out_g, g_s), (out_d, d_s), (out_m, m_s), (out_v, v_s)):
        dst.update(zip(SMALL, _unpack(src.reshape(-1), small_like)))
    shard_w = conv_w.shape[-1]
    conv_parts = parts[:, SMALL_ROWS:].reshape(N_DEV, DEPTH, CONV_WIDTH, CONV_CH)
    conv_mine = lax.dynamic_slice_in_dim(conv_parts, dev * shard_w, shard_w, axis=3)
    g_c, d_c, m_c, v_c = _adamw(
        [_pack_rows([conv_mine[i]], CONVW_SHARD_ROWS, LANES) for i in range(N_DEV)], _pack_rows([conv_w], CONVW_SHARD_ROWS, LANES),
        _pack_rows([m["conv_w"]], CONVW_SHARD_ROWS, LANES), _pack_rows([v["conv_w"]], CONVW_SHARD_ROWS, LANES), "adamw_conv_w")
    for dst, src in ((out_g, g_c), (out_d, d_c), (out_m, m_c), (out_v, v_c)):
        dst["conv_w"] = src.reshape(-1)[:conv_w.size].reshape(conv_w.shape)

    landed.update({(k, 0): g for k, g in zip(MIX, _direct_wait(*mix0_grads[:4], v_c + out_v["w_down"][0, :CONVW_SHARD_ROWS, :LANES], True, "scatter_l0_mix_wait"))})
    update(MIX, v_c)

    loss = lax.psum(loss_local, ("x", "y", "c"))
    return (loss, dx.reshape(x.shape), *[out_g[k] for k in WEIGHTS], *[out_d[k] for k in WEIGHTS],
            *[out_m[k] for k in WEIGHTS], *[out_v[k] for k in WEIGHTS])
```

```python
import jax
import jax.numpy as jnp
import numpy as np
from jax import lax
from jax.experimental import pallas as pl
from jax.experimental.pallas import tpu as pltpu

f32 = jnp.float32
bf16 = jnp.bfloat16

D_MODEL = 1024
SEQ = 2048
DEPTH = 2
HEAD_DIM = 64
N_ATTN_HEADS = 8
N_KV_HEADS = 2
ATTN_WIDTH = 512
KV_WIDTH = 128
ROPE_DIM = 16
ROPE_THETA = 500000.0
DILATIONS = (1, 4, 16)
ATTN_BLOCK = 128
SSM_HEADS = 16
SSM_INNER = 1024
SSM_GROUPS = 2
D_STATE = 128
CONV_WIDTH = 4
CHUNK = 128
CONV_CH = 1536
MIX_WIDTH = 1536
QKV_WIDTH = ATTN_WIDTH + 2 * KV_WIDTH
DT_OFF = 3328
IN_PROJ = 3344
IN_PROJ_PAD = 3456
FFN_HIDDEN = 2816
EPS = 1e-5
N_DEV = 8
ADAM_LR = 0.001
ADAM_B1 = 0.9
ADAM_B2 = 0.999
ADAM_EPS = 1e-08
ADAM_WD = 0.01
ADAM_STEP = 10

LANES = 128
SUBLANES = 8
VMEM_LIMIT = 56 * 1024 * 1024
VMEM_LIMIT_TWO_PAIRS = 60 * 1024 * 1024

MESH = pl.DeviceIdType.MESH
ANY = pl.BlockSpec(memory_space=pl.ANY)


def _cparams(sem, vmem=None):
    return pltpu.CompilerParams(dimension_semantics=sem, vmem_limit_bytes=vmem or VMEM_LIMIT)


def _sigmoid(x):
    return 1.0 / (1.0 + jnp.exp(-x))


def _silu(x):
    return x * _sigmoid(x)


def _dsilu(x):
    s = _sigmoid(x)
    return s * (1.0 + x * (1.0 - s))


def _silu_and_grad(x):
    s = _sigmoid(x)
    return x * s, s * (1.0 + x * (1.0 - s))


def _softplus(x):
    return jnp.maximum(x, 0.0) + jnp.log(1.0 + jnp.exp(-jnp.abs(x)))


def _dot(a, b, dims, precision=None):
    return lax.dot_general(a, b, (dims, ((), ())), preferred_element_type=f32, precision=precision)


def _nn(a, b, precision=None):
    return _dot(a, b, ((1,), (0,)), precision)


def _nt(a, b):
    return _dot(a, b, ((1,), (1,)))


def _tn(a, b):
    return _dot(a, b, ((0,), (0,)))


def _rowsum8(t):
    n, w = t.shape
    return jnp.sum(t.reshape(n // SUBLANES, SUBLANES, w), axis=0)


def _matmul(a, b, *, mode, n_out=None, b_off=0, add=None, out_dtype=f32, tm=2048, tn=512, tk=1024, name):
    if mode == "tn":
        kk, m = a.shape
    else:
        m, kk = a.shape
    n = n_out if n_out is not None else (b.shape[0] if mode == "nt" else b.shape[1])
    tm, tn, tk = min(tm, m), min(tn, n), min(tk, kk)
    assert m % tm == 0 and n % tn == 0 and kk % tk == 0, (name, m, n, kk, tm, tn, tk)
    nk = kk // tk
    if mode == "nn":
        a_spec = pl.BlockSpec((tm, tk), lambda i, j, k: (i, k))
        b_spec = pl.BlockSpec((tk, tn), lambda i, j, k: (k, j + b_off))
        dims = ((1,), (0,))
    elif mode == "nt":
        a_spec = pl.BlockSpec((tm, tk), lambda i, j, k: (i, k))
        b_spec = pl.BlockSpec((tn, tk), lambda i, j, k: (j + b_off, k))
        dims = ((1,), (1,))
    else:
        a_spec = pl.BlockSpec((tk, tm), lambda i, j, k: (k, i))
        b_spec = pl.BlockSpec((tk, tn), lambda i, j, k: (k, j + b_off))
        dims = ((0,), (0,))
    o_spec = pl.BlockSpec((tm, tn), lambda i, j, k: (i, j))
    has_add = add is not None

    def body(*refs):
        if has_add:
            a_ref, b_ref, add_ref, o_ref, acc_ref = refs
        else:
            a_ref, b_ref, o_ref, acc_ref = refs
        k = pl.program_id(2)
        part = _dot(a_ref[...].astype(bf16), b_ref[...].astype(bf16), dims)

        @pl.when(k == 0)
        def _():
            acc_ref[...] = part

        @pl.when(k > 0)
        def _():
            acc_ref[...] += part

        @pl.when(k == nk - 1)
        def _():
            r = acc_ref[...]
            if has_add:
                r = r + add_ref[...]
            o_ref[...] = r.astype(out_dtype)

    in_specs = [a_spec, b_spec] + ([o_spec] if has_add else [])
    args = (a, b) + ((add,) if has_add else ())
    return pl.pallas_call(
        body, name=name, grid=(m // tm, n // tn, nk), in_specs=in_specs, out_specs=o_spec,
        out_shape=jax.ShapeDtypeStruct((m, n), out_dtype), scratch_shapes=[pltpu.VMEM((tm, tn), f32)],
        compiler_params=_cparams(("parallel", "parallel", "arbitrary")),
    )(*args)


def _in_proj(hn, w_in, widths, name, tm=2048, tn=256):
    m, k = hn.shape
    starts = [sum(widths[:i]) // tn for i in range(len(widths))]
    counts = [wd // tn for wd in widths]
    assert m % tm == 0 and all(wd % tn == 0 for wd in widths)
    n_out = len(widths)

    def body(a_ref, w_ref, *o_refs):
        j = pl.program_id(1)
        acc = _nn(a_ref[...], w_ref[...])
        for s, c, o_ref in zip(starts, counts, o_refs):
            @pl.when((j >= s) & (j < s + c))
            def _(o_ref=o_ref):
                o_ref[...] = acc

    def o_spec(s, c):
        return pl.BlockSpec((tm, tn), lambda i, j: (i, jnp.clip(j - s, 0, c - 1)))

    return pl.pallas_call(
        body, name=name, grid=(m // tm, sum(counts)),
        in_specs=[pl.BlockSpec((tm, k), lambda i, j: (i, 0)), pl.BlockSpec((k, tn), lambda i, j: (0, j))],
        out_specs=[o_spec(s, c) for s, c in zip(starts, counts)],
        out_shape=[jax.ShapeDtypeStruct((m, wd), f32) for wd in widths], compiler_params=_cparams(("parallel", "arbitrary")),
    )(hn, w_in)


def _out_proj(o, yn, w_out, h, name, tm=2048, tn=512):
    m, kb = o.shape
    n = w_out.shape[1]
    n_y = yn.shape[1] // kb
    assert yn.shape[1] % kb == 0 and w_out.shape[0] == kb * (1 + n_y) and m % tm == 0 and n % tn == 0

    def body(*refs):
        o_ref, y_refs, w_refs, h_ref, out_ref = refs[0], refs[1:1 + n_y], refs[1 + n_y:2 + 2 * n_y], refs[-2], refs[-1]
        acc = h_ref[...] + _nn(o_ref[...].astype(bf16), w_refs[0][...])
        for y_ref, w_ref in zip(y_refs, w_refs[1:]):
            acc = acc + _nn(y_ref[...], w_ref[...])
        out_ref[...] = acc

    res = pl.BlockSpec((tm, tn), lambda i, j: (i, j))

    def a_blk(c):
        return pl.BlockSpec((tm, kb), lambda i, j: (i, c))

    def w_blk(r):
        return pl.BlockSpec((kb, tn), lambda i, j: (r, j))

    return pl.pallas_call(
        body, name=name, grid=(m // tm, n // tn),
        in_specs=[a_blk(0)] + [a_blk(c) for c in range(n_y)] + [w_blk(r) for r in range(1 + n_y)] + [res],
        out_specs=res, out_shape=jax.ShapeDtypeStruct((m, n), f32), compiler_params=_cparams(("parallel", "parallel")),
    )(o, *[yn] * n_y, *[w_out] * (1 + n_y), h)


def _swiglu_fwd(hn, w_gate, w_up, name, tm=2048, tn=256):
    m, k = hn.shape
    n = w_gate.shape[0]
    assert m % tm == 0 and n % tn == 0, (name, m, n, tm, tn)

    def body(a_ref, wg_ref, wu_ref, g_ref, u_ref, act_ref):
        a = a_ref[...]
        g = _nt(a, wg_ref[...])
        u = _nt(a, wu_ref[...])
        sg, dsg = _silu_and_grad(g)
        g_ref[...] = (u * dsg).astype(bf16)
        u_ref[...] = sg.astype(bf16)
        act_ref[...] = (sg * u).astype(bf16)

    a_spec = pl.BlockSpec((tm, k), lambda i, j: (i, 0))
    w_spec = pl.BlockSpec((tn, k), lambda i, j: (j, 0))
    o_spec = pl.BlockSpec((tm, tn), lambda i, j: (i, j))
    return pl.pallas_call(
        body, name=name, grid=(m // tm, n // tn), in_specs=[a_spec, w_spec, w_spec], out_specs=[o_spec, o_spec, o_spec],
        out_shape=[jax.ShapeDtypeStruct((m, n), bf16)] * 3,
        compiler_params=_cparams(("parallel", "parallel")),
    )(hn, w_gate, w_up)


def _swiglu_bwd(dh, w_down, g, u, name, tm=2048, tn=256):
    m, k = dh.shape
    n = w_down.shape[0]
    assert m % tm == 0 and n % tn == 0, (name, m, n, tm, tn)

    def body(a_ref, w_ref, g_ref, u_ref, dg_ref, du_ref):
        dact = _nt(a_ref[...].astype(bf16), w_ref[...])
        dg_ref[...] = (dact * g_ref[...].astype(f32)).astype(bf16)
        du_ref[...] = (dact * u_ref[...].astype(f32)).astype(bf16)

    a_spec = pl.BlockSpec((tm, k), lambda i, j: (i, 0))
    w_spec = pl.BlockSpec((tn, k), lambda i, j: (j, 0))
    o_spec = pl.BlockSpec((tm, tn), lambda i, j: (i, j))
    return pl.pallas_call(
        body, name=name, grid=(m // tm, n // tn), in_specs=[a_spec, w_spec, o_spec, o_spec], out_specs=[o_spec, o_spec],
        out_shape=[jax.ShapeDtypeStruct((m, n), bf16), jax.ShapeDtypeStruct((m, n), bf16)],
        compiler_params=_cparams(("parallel", "parallel")),
    )(dh, w_down, g, u)


def _rmsnorm_fwd(h, w, name, tm=1024):
    m, d = h.shape

    def body(h_ref, w_ref, o_ref):
        x = h_ref[...]
        r = lax.rsqrt(jnp.mean(x * x, axis=-1, keepdims=True) + EPS)
        o_ref[...] = (x * r * w_ref[...]).astype(bf16)

    return pl.pallas_call(
        body, name=name, grid=(m // tm,),
        in_specs=[pl.BlockSpec((tm, d), lambda i: (i, 0)), pl.BlockSpec((1, d), lambda i: (0, 0))],
        out_specs=pl.BlockSpec((tm, d), lambda i: (i, 0)), out_shape=jax.ShapeDtypeStruct((m, d), bf16),
        compiler_params=_cparams(("parallel",)),
    )(h, w)


def _nt_norm_bwd(pairs, h, w, dres, name, tk, b_is_kd=False, bf16_copy=True, tm=1024, vmem=None):
    m, d = h.shape
    contract = _nn if b_is_kd else _nt
    steps = [p[0].shape[1] // tk for p in pairs]
    assert all(p[0].shape[1] % tk == 0 for p in pairs), (name, tk)
    starts = [sum(steps[:i]) for i in range(len(pairs))]
    nk = sum(steps)
    n_p = len(pairs)

    def body(*refs):
        ab = refs[:2 * n_p]
        h_ref, w_ref, dres_ref, dh_ref = refs[2 * n_p:2 * n_p + 4]
        dhb_ref = refs[2 * n_p + 4] if bf16_copy else None
        dw_ref, acc_ref = refs[-2:]
        i, k = pl.program_id(0), pl.program_id(1)

        @pl.when(k == 0)
        def _():
            acc_ref[...] = jnp.zeros_like(acc_ref)

        for p in range(n_p):
            @pl.when((k >= starts[p]) & (k < starts[p] + steps[p]))
            def _(p=p):
                acc_ref[...] += contract(ab[2 * p][...], ab[2 * p + 1][...])

        @pl.when(k == nk - 1)
        def _():
            x = h_ref[...]
            r = lax.rsqrt(jnp.mean(x * x, axis=-1, keepdims=True) + EPS)
            xhat = x * r
            dy = acc_ref[...]
            gw = dy * w_ref[...]
            dh = dres_ref[...] + r * (gw - xhat * jnp.mean(gw * xhat, axis=-1, keepdims=True))
            dh_ref[...] = dh
            if bf16_copy:
                dhb_ref[...] = dh.astype(bf16)
            part = _rowsum8(dy * xhat)

            @pl.when(i == 0)
            def _():
                dw_ref[...] = part

            @pl.when(i > 0)
            def _():
                dw_ref[...] += part

    def clamp(k, p):
        return jnp.clip(k - starts[p], 0, steps[p] - 1)

    in_specs = []
    for p in range(n_p):
        b_spec = (pl.BlockSpec((tk, d), lambda i, k, p=p: (clamp(k, p), 0)) if b_is_kd
                  else pl.BlockSpec((d, tk), lambda i, k, p=p: (0, clamp(k, p))))
        in_specs += [pl.BlockSpec((tm, tk), lambda i, k, p=p: (i, clamp(k, p))), b_spec]
    row = pl.BlockSpec((tm, d), lambda i, k: (i, 0))
    in_specs += [row, pl.BlockSpec((1, d), lambda i, k: (0, 0)), row]
    return pl.pallas_call(
        body, name=name, grid=(m // tm, nk), in_specs=in_specs,
        out_specs=[row] + [row] * bf16_copy + [pl.BlockSpec((SUBLANES, d), lambda i, k: (0, 0))],
        out_shape=[jax.ShapeDtypeStruct((m, d), f32)] + [jax.ShapeDtypeStruct((m, d), bf16)] * bf16_copy + [jax.ShapeDtypeStruct((SUBLANES, d), f32)],
        scratch_shapes=[pltpu.VMEM((tm, d), f32)], compiler_params=_cparams(("arbitrary", "arbitrary"), vmem),
    )(*[t for p in pairs for t in p], h, w, dres)


def _final_loss(h, w, target, name, tm=1024):
    m, d = h.shape

    def body(h_ref, w_ref, t_ref, dh_ref, dhb_ref, loss_ref, dw_ref):
        x = h_ref[...]
        r = lax.rsqrt(jnp.mean(x * x, axis=-1, keepdims=True) + EPS)
        xhat = x * r
        ww = w_ref[...]
        err = xhat * ww - t_ref[...]
        dy = err * (1.0 / d)
        gw = dy * ww
        dh = r * (gw - xhat * jnp.mean(gw * xhat, axis=-1, keepdims=True))
        dh_ref[...] = dh
        dhb_ref[...] = dh.astype(bf16)
        lpart = _rowsum8(err * err) * (0.5 / d)
        wpart = _rowsum8(dy * xhat)

        @pl.when(pl.program_id(0) == 0)
        def _():
            loss_ref[...] = lpart
            dw_ref[...] = wpart

        @pl.when(pl.program_id(0) > 0)
        def _():
            loss_ref[...] += lpart
            dw_ref[...] += wpart

    row = pl.BlockSpec((tm, d), lambda i: (i, 0))
    acc = pl.BlockSpec((SUBLANES, d), lambda i: (0, 0))
    return pl.pallas_call(
        body, name=name, grid=(m // tm,),
        in_specs=[row, pl.BlockSpec((1, d), lambda i: (0, 0)), row], out_specs=[row, row, acc, acc],
        out_shape=[jax.ShapeDtypeStruct((m, d), f32), jax.ShapeDtypeStruct((m, d), bf16),
                   jax.ShapeDtypeStruct((SUBLANES, d), f32), jax.ShapeDtypeStruct((SUBLANES, d), f32)],
        compiler_params=_cparams(("arbitrary",)),
    )(h, w, target)


def _lane_tables():
    f = np.arange(LANES) % HEAD_DIM
    inv = ROPE_THETA ** (-jnp.arange(0, ROPE_DIM, 2, dtype=f32) / ROPE_DIM)
    invf = jnp.where(f < ROPE_DIM, inv[f % (ROPE_DIM // 2)], 0.0).astype(f32)
    return invf.reshape(1, LANES)


def _rope_tables(pos_col, name):
    t = pos_col.shape[0]
    tm = SEQ

    def body(p_ref, f_ref, c_ref, s1_ref, s2_ref):
        ang = p_ref[...].astype(f32) * f_ref[...]
        co, si = jnp.cos(ang), jnp.sin(ang)
        f = lax.broadcasted_iota(jnp.int32, (tm, LANES), 1) % HEAD_DIM
        c_ref[...] = jnp.where(f < ROPE_DIM, co, 1.0)
        s1_ref[...] = jnp.where(f < ROPE_DIM // 2, -si, 0.0)
        s2_ref[...] = jnp.where((f >= ROPE_DIM // 2) & (f < ROPE_DIM), si, 0.0)

    row = pl.BlockSpec((tm, LANES), lambda i: (i, 0))
    return pl.pallas_call(
        body, name=name, grid=(t // tm,),
        in_specs=[pl.BlockSpec((tm, 1), lambda i: (i, 0)), pl.BlockSpec((1, LANES), lambda i: (0, 0))],
        out_specs=[row, row, row], out_shape=[jax.ShapeDtypeStruct((t, LANES), f32)] * 3,
        compiler_params=_cparams(("parallel",)),
    )(pos_col, _lane_tables())


def _rot(x, c, s1, s2):
    return x * c + pltpu.roll(x, LANES - ROPE_DIM // 2, 1) * s1 + pltpu.roll(x, ROPE_DIM // 2, 1) * s2


def _rot_t(g, c, s1, s2):
    return g * c + pltpu.roll(g * s1, ROPE_DIM // 2, 1) + pltpu.roll(g * s2, LANES - ROPE_DIM // 2, 1)


def _dup_head(x, kvh, low):
    a = jnp.where(kvh == 0, x, pltpu.roll(x, HEAD_DIM, 1))
    return jnp.where(low, a, pltpu.roll(a, HEAD_DIM, 1))


def _deinterleave(src_ref, dst_ref, d, dtype):
    length = SEQ // d
    if d == 1:
        dst_ref[...] = src_ref[...].astype(dtype)
    else:
        for r in range(d):
            dst_ref[pl.ds(r * length, length), :] = src_ref[pl.ds(r, length, stride=d), :].astype(dtype)


def _interleave_store(src_ref, dst_ref, d, accumulate):
    length = SEQ // d
    if d == 1:
        if accumulate:
            dst_ref[...] += src_ref[...]
        else:
            dst_ref[...] = src_ref[...]
    else:
        for r in range(d):
            blk = src_ref[pl.ds(r * length, length), :]
            if accumulate:
                dst_ref[pl.ds(r, length, stride=d), :] = dst_ref[pl.ds(r, length, stride=d), :] + blk
            else:
                dst_ref[pl.ds(r, length, stride=d), :] = blk


def _attn_masks():
    qi = lax.broadcasted_iota(jnp.int32, (ATTN_BLOCK, ATTN_BLOCK), 0)
    ki = lax.broadcasted_iota(jnp.int32, (ATTN_BLOCK, ATTN_BLOCK), 1)
    low = lax.broadcasted_iota(jnp.int32, (ATTN_BLOCK, LANES), 1) < HEAD_DIM
    return ki <= qi, ki >= qi, low


NEG_INF = float("-inf")


N_BRANCH = len(DILATIONS)


def _attn_prep(qkv, tabs, name):
    t = qkv.shape[0]
    nb = t // SEQ
    n_j = ATTN_WIDTH // LANES

    def q_body(q_ref, c_ref, s1_ref, s2_ref, out_ref, xr):
        xr[...] = _rot(q_ref[...], c_ref[...], s1_ref[...], s2_ref[...]) * (HEAD_DIM ** -0.5)
        for bi, d in enumerate(DILATIONS):
            _deinterleave(xr, out_ref.at[bi], d, bf16)

    def kv_body(x_ref, c_ref, s1_ref, s2_ref, out_ref, xr):
        lowfull = lax.broadcasted_iota(jnp.int32, (SEQ, LANES), 1) < HEAD_DIM
        x = x_ref[...]
        x = jnp.where(pl.program_id(1) == 0, _rot(x, c_ref[...], s1_ref[...], s2_ref[...]), x)
        for kvh in range(N_KV_HEADS):
            xr[...] = _dup_head(x, kvh, lowfull)
            for bi, d in enumerate(DILATIONS):
                length = SEQ // d
                for r in range(d):
                    rows = xr[...] if d == 1 else xr[pl.ds(r, length, stride=d), :]
                    out_ref[0, bi, pl.ds(r * length, length), kvh * LANES:(kvh + 1) * LANES] = rows.astype(bf16)

    tab = pl.BlockSpec((SEQ, LANES), lambda b, j: (b, 0))
    q = pl.pallas_call(
        q_body, name=name + "_q", grid=(nb, n_j),
        in_specs=[pl.BlockSpec((SEQ, LANES), lambda b, j: (b, j)), tab, tab, tab],
        out_specs=pl.BlockSpec((N_BRANCH, SEQ, LANES), lambda b, j: (0, b, j)),
        out_shape=jax.ShapeDtypeStruct((N_BRANCH, t, ATTN_WIDTH), bf16), scratch_shapes=[pltpu.VMEM((SEQ, LANES), f32)],
        compiler_params=_cparams(("parallel", "parallel")),
    )(qkv, *tabs)
    kv = pl.pallas_call(
        kv_body, name=name + "_kv", grid=(nb, 2),
        in_specs=[pl.BlockSpec((SEQ, LANES), lambda b, j: (b, n_j + j)), tab, tab, tab],
        out_specs=pl.BlockSpec((1, N_BRANCH, SEQ, N_KV_HEADS * LANES), lambda b, j: (j, 0, b, 0)),
        out_shape=jax.ShapeDtypeStruct((2, N_BRANCH, t, N_KV_HEADS * LANES), bf16), scratch_shapes=[pltpu.VMEM((SEQ, LANES), f32)],
        compiler_params=_cparams(("parallel", "parallel")),
    )(qkv, *tabs)
    return q, kv


def _attn_fwd(prep, name):
    q_all, kv_all = prep
    t = q_all.shape[1]
    nb = t // SEQ
    n_blk = SEQ // ATTN_BLOCK

    def body(q_ref, k_ref, v_ref, o_ref, o16_ref, lse_ref, ob, lb, o0, o1, o2, l0, l1, l2, ss):
        cur_ok, prev_ok, low = _attn_masks()
        onat, lnat = (o0, o1, o2), (l0, l1, l2)
        for bi, d in enumerate(DILATIONS):
            qd, kd, vd = q_ref.at[bi], k_ref.at[0, bi], v_ref.at[0, bi]
            per_res = n_blk // d

            def scores(n):
                cur, prev = pl.ds(n * ATTN_BLOCK, ATTN_BLOCK), pl.ds(max(n - 1, 0) * ATTN_BLOCK, ATTN_BLOCK)
                has_prev = n % per_res != 0
                qb = qd[cur, :]
                kc = kd[cur, :]
                if has_prev:
                    kp = kd[prev, :]
                for a in range(2):
                    qa = jnp.where(low if a == 0 else ~low, qb, jnp.zeros_like(qb))
                    ss[2 * n + a, :, 0:ATTN_BLOCK] = jnp.where(cur_ok, _nt(qa, kc), NEG_INF)
                    if has_prev:
                        ss[2 * n + a, :, ATTN_BLOCK:2 * ATTN_BLOCK] = jnp.where(prev_ok, _nt(qa, kp), NEG_INF)

            def softmax_pv(n):
                cur, prev = pl.ds(n * ATTN_BLOCK, ATTN_BLOCK), pl.ds(max(n - 1, 0) * ATTN_BLOCK, ATTN_BLOCK)
                has_prev = n % per_res != 0
                vc = vd[cur, :]
                if has_prev:
                    vp = vd[prev, :]
                outs, lses = [], []
                for a in range(2):
                    sc = ss[2 * n + a, :, 0:ATTN_BLOCK]
                    if has_prev:
                        sp = ss[2 * n + a, :, ATTN_BLOCK:2 * ATTN_BLOCK]
                        m = jnp.max(jnp.maximum(sc, sp), axis=1, keepdims=True)
                        pc, pp = jnp.exp(sc - m), jnp.exp(sp - m)
                        den = jnp.sum(pc + pp, axis=1, keepdims=True)
                        acc = _nn(pc.astype(bf16), vc) + _nn(pp.astype(bf16), vp)
                    else:
                        m = jnp.max(sc, axis=1, keepdims=True)
                        pc = jnp.exp(sc - m)
                        den = jnp.sum(pc, axis=1, keepdims=True)
                        acc = _nn(pc.astype(bf16), vc)
                    outs.append(acc * (1.0 / den))
                    lses.append(m + jnp.log(den))
                ob[cur, :] = jnp.where(low, outs[0], outs[1])
                lb[cur, :] = jnp.where(low, lses[0], lses[1])

            for n in range(n_blk):
                scores(n)
            for n in range(n_blk):
                softmax_pv(n)
            _interleave_store(ob, onat[bi], d, False)
            _interleave_store(lb, lnat[bi], d, False)
        la, lbb, lc = l0[...], l1[...], l2[...]
        lm = jnp.maximum(jnp.maximum(la, lbb), lc)
        wa, wb, wc = jnp.exp(la - lm), jnp.exp(lbb - lm), jnp.exp(lc - lm)
        ws = wa + wb + wc
        o = (wa * o0[...] + wb * o1[...] + wc * o2[...]) / ws
        o_ref[...] = o
        o16_ref[...] = o.astype(bf16)
        lse_ref[...] = lm + jnp.log(ws)

    def col(jj):
        return pl.BlockSpec((SEQ, LANES), lambda b, j: (b, jj if jj is not None else j))

    fs = pltpu.VMEM((SEQ, LANES), f32)
    return pl.pallas_call(
        body, name=name, grid=(nb, ATTN_WIDTH // LANES),
        in_specs=[pl.BlockSpec((N_BRANCH, SEQ, LANES), lambda b, j: (0, b, j)),
                  pl.BlockSpec((1, N_BRANCH, SEQ, LANES), lambda b, j: (0, 0, b, j // 2)),
                  pl.BlockSpec((1, N_BRANCH, SEQ, LANES), lambda b, j: (1, 0, b, j // 2))],
        out_specs=[col(None), col(None), col(None)],
        out_shape=[jax.ShapeDtypeStruct((t, ATTN_WIDTH), f32), jax.ShapeDtypeStruct((t, ATTN_WIDTH), bf16), jax.ShapeDtypeStruct((t, ATTN_WIDTH), f32)],
        scratch_shapes=[fs, fs, fs, fs, fs, fs, fs, fs, pltpu.VMEM((2 * n_blk, ATTN_BLOCK, 2 * ATTN_BLOCK), f32)],
        compiler_params=_cparams(("parallel", "parallel")),
    )(q_all, kv_all, kv_all)


def _attn_bwd(prep, tabs, o, lse, do, name):
    q_all, kv_all = prep
    t = q_all.shape[1]
    nb = t // SEQ
    n_blk = SEQ // ATTN_BLOCK
    n_j = ATTN_WIDTH // LANES

    def body(q_ref, k_ref, v_ref, c_ref, s1_ref, s2_ref, o_ref, lse_ref, do_ref, dq_ref, dk_ref, dv_ref,
             stat, dod, std, dqd, dkd, dvd, dqa, dka, dva, pb, dsb, dk_acc, dv_acc):
        j = pl.program_id(1)
        kvh = j // 2
        cur_ok, prev_ok, low = _attn_masks()
        lane = lax.broadcasted_iota(jnp.int32, (SEQ, LANES), 1)
        lowfull = lane < HEAD_DIM
        c, s1, s2 = c_ref[...], s1_ref[...], s2_ref[...]
        prod = do_ref[...] * o_ref[...]
        d_lo = jnp.sum(jnp.where(lowfull, prod, 0.0), axis=1, keepdims=True)
        d_hi = jnp.sum(jnp.where(lowfull, 0.0, prod), axis=1, keepdims=True)
        stat[...] = jnp.where(lane % HEAD_DIM < HEAD_DIM // 2, lse_ref[...], jnp.where(lowfull, d_lo, d_hi))
        dqa[...] = jnp.zeros_like(dqa)
        dka[...] = jnp.zeros_like(dka)
        dva[...] = jnp.zeros_like(dva)
        for bi, d in enumerate(DILATIONS):
            qd, kd, vd = q_ref.at[bi], k_ref.at[0, bi], v_ref.at[0, bi]
            _deinterleave(do_ref, dod, d, bf16)
            _deinterleave(stat, std, d, f32)
            per_res = n_blk // d
            curl, prevl = slice(0, ATTN_BLOCK), slice(ATTN_BLOCK, 2 * ATTN_BLOCK)

            def halves(x):
                zero = jnp.zeros_like(x)
                return jnp.where(low, x, zero), jnp.where(low, zero, x)

            def blk(n):
                return pl.ds(n * ATTN_BLOCK, ATTN_BLOCK)

            def has_prev(n):
                return n < n_blk and n % per_res != 0

            def probs(n):
                cur = blk(n)
                qas, doas = halves(qd[cur, :]), halves(dod[cur, :])
                kc, vc = kd[cur, :], vd[cur, :]
                if has_prev(n):
                    kp, vp = kd[blk(n - 1), :], vd[blk(n - 1), :]
                stb = std[cur, :]
                for a in range(2):
                    ls = stb[:, a * HEAD_DIM:a * HEAD_DIM + 1]
                    de = stb[:, a * HEAD_DIM + HEAD_DIM // 2:a * HEAD_DIM + HEAD_DIM // 2 + 1]
                    pc = jnp.exp(jnp.where(cur_ok, _nt(qas[a], kc), NEG_INF) - ls)
                    pb[2 * n + a, :, curl] = pc.astype(bf16)
                    dsb[2 * n + a, :, curl] = (pc * (_nt(doas[a], vc) - de)).astype(bf16)
                    if has_prev(n):
                        pp = jnp.exp(jnp.where(prev_ok, _nt(qas[a], kp), NEG_INF) - ls)
                        pb[2 * n + a, :, prevl] = pp.astype(bf16)
                        dsb[2 * n + a, :, prevl] = (pp * (_nt(doas[a], vp) - de)).astype(bf16)

            def grads(n):
                cur = blk(n)
                kc = kd[cur, :]
                dqs = [_nn(dsb[2 * n + a, :, curl], kc) for a in range(2)]
                q_rows, do_rows = list(halves(qd[cur, :])), list(halves(dod[cur, :]))
                ds_rows, p_rows = [dsb[2 * n + a, :, curl] for a in range(2)], [pb[2 * n + a, :, curl] for a in range(2)]
                if has_prev(n):
                    kp = kd[blk(n - 1), :]
                    dqs = [dqs[a] + _nn(dsb[2 * n + a, :, prevl], kp) for a in range(2)]
                if has_prev(n + 1):
                    q_rows += list(halves(qd[blk(n + 1), :]))
                    do_rows += list(halves(dod[blk(n + 1), :]))
                    ds_rows += [dsb[2 * n + 2 + a, :, prevl] for a in range(2)]
                    p_rows += [pb[2 * n + 2 + a, :, prevl] for a in range(2)]
                dqd[cur, :] = jnp.where(low, dqs[0], dqs[1])
                dkd[cur, :] = _tn(jnp.concatenate(ds_rows, axis=0), jnp.concatenate(q_rows, axis=0))
                dvd[cur, :] = _tn(jnp.concatenate(p_rows, axis=0), jnp.concatenate(do_rows, axis=0))

            for n in range(n_blk):
                probs(n)
            for n in range(n_blk):
                grads(n)
            _interleave_store(dqd, dqa, d, True)
            _interleave_store(dkd, dka, d, True)
            _interleave_store(dvd, dva, d, True)
        dq_ref[...] = _rot_t(dqa[...] * (HEAD_DIM ** -0.5), c, s1, s2).astype(bf16)
        dkf = dka[...]
        dkf = _rot_t(dkf + pltpu.roll(dkf, HEAD_DIM, 1), c, s1, s2)
        dvf = dva[...]
        dvf = dvf + pltpu.roll(dvf, HEAD_DIM, 1)
        mine = (lax.broadcasted_iota(jnp.int32, (SEQ, LANES), 1) // HEAD_DIM) == kvh
        dkc_, dvc_ = jnp.where(mine, dkf, 0.0), jnp.where(mine, dvf, 0.0)

        @pl.when(j == 0)
        def _():
            dk_acc[...] = dkc_
            dv_acc[...] = dvc_

        @pl.when(j > 0)
        def _():
            dk_acc[...] += dkc_
            dv_acc[...] += dvc_

        @pl.when(j == n_j - 1)
        def _():
            dk_ref[...] = dk_acc[...].astype(bf16)
            dv_ref[...] = dv_acc[...].astype(bf16)

    def col(jj):
        return pl.BlockSpec((SEQ, LANES), lambda b, j: (b, jj if jj is not None else j))

    tab = pl.BlockSpec((SEQ, LANES), lambda b, j: (b, 0))
    fs = pltpu.VMEM((SEQ, LANES), f32)
    hs = pltpu.VMEM((SEQ, LANES), bf16)
    return pl.pallas_call(
        body, name=name, grid=(nb, n_j),
        in_specs=[pl.BlockSpec((N_BRANCH, SEQ, LANES), lambda b, j: (0, b, j)),
                  pl.BlockSpec((1, N_BRANCH, SEQ, LANES), lambda b, j: (0, 0, b, j // 2)),
                  pl.BlockSpec((1, N_BRANCH, SEQ, LANES), lambda b, j: (1, 0, b, j // 2)),
                  tab, tab, tab, col(None), col(None), col(None)],
        out_specs=[col(None), tab, tab],
        out_shape=[jax.ShapeDtypeStruct((t, ATTN_WIDTH), bf16), jax.ShapeDtypeStruct((t, LANES), bf16), jax.ShapeDtypeStruct((t, LANES), bf16)],
        scratch_shapes=[fs, hs, fs, fs, fs, fs, fs, fs, fs,
                        pltpu.VMEM((2 * n_blk, ATTN_BLOCK, 2 * ATTN_BLOCK), bf16), pltpu.VMEM((2 * n_blk, ATTN_BLOCK, 2 * ATTN_BLOCK), bf16), fs, fs],
        compiler_params=_cparams(("parallel", "arbitrary")),
    )(q_all, kv_all, kv_all, *tabs, o, lse, do)


def _tap(w_ref, s):
    return w_ref[CONV_WIDTH - 1 - s:CONV_WIDTH - s, :]


def _conv_pre(x, w_ref, b_ref, row):
    shifted = [x] + [jnp.where(row >= s, pltpu.roll(x, s, 0), 0.0) for s in range(1, CONV_WIDTH)]
    pre = b_ref[...] + _tap(w_ref, 0) * x
    for s in range(1, CONV_WIDTH):
        pre = pre + _tap(w_ref, s) * shifted[s]
    return pre, shifted


def _conv_fwd(x, w, b, name, tc=512):
    t, ch = x.shape

    def body(x_ref, w_ref, b_ref, o_ref):
        row = lax.broadcasted_iota(jnp.int32, (SEQ, tc), 0)
        pre, _ = _conv_pre(x_ref[...], w_ref, b_ref, row)
        o_ref[...] = _silu(pre)

    xs = pl.BlockSpec((SEQ, tc), lambda i, j: (i, j))
    return pl.pallas_call(
        body, name=name, grid=(t // SEQ, ch // tc),
        in_specs=[xs, pl.BlockSpec((CONV_WIDTH, tc), lambda i, j: (0, j)), pl.BlockSpec((1, tc), lambda i, j: (0, j))],
        out_specs=xs, out_shape=jax.ShapeDtypeStruct((t, ch), f32),
        compiler_params=_cparams(("parallel", "parallel")),
    )(x, w, b)


def _conv_bwd(x, w, b, dact, name, tc=512):
    t, ch = x.shape

    def body(x_ref, w_ref, b_ref, d_ref, dx_ref, dw_ref, db_ref):
        row = lax.broadcasted_iota(jnp.int32, (SEQ, tc), 0)
        pre, shifted = _conv_pre(x_ref[...], w_ref, b_ref, row)
        dpre = d_ref[...] * _dsilu(pre)
        dx = _tap(w_ref, 0) * dpre
        for s in range(1, CONV_WIDTH):
            dx = dx + _tap(w_ref, s) * jnp.where(row < SEQ - s, pltpu.roll(dpre, SEQ - s, 0), 0.0)
        dx_ref[...] = dx.astype(bf16)
        first = pl.program_id(1) == 0
        parts = [jnp.sum(dpre * shifted[CONV_WIDTH - 1 - k], axis=0, keepdims=True) for k in range(CONV_WIDTH)]
        dbp = jnp.sum(dpre, axis=0, keepdims=True)

        @pl.when(first)
        def _():
            for k in range(CONV_WIDTH):
                dw_ref[k:k + 1, :] = parts[k]
            db_ref[...] = dbp

        @pl.when(jnp.logical_not(first))
        def _():
            for k in range(CONV_WIDTH):
                dw_ref[k:k + 1, :] += parts[k]
            db_ref[...] += dbp

    xs = pl.BlockSpec((SEQ, tc), lambda j, i: (i, j))
    ws = pl.BlockSpec((CONV_WIDTH, tc), lambda j, i: (0, j))
    bs = pl.BlockSpec((1, tc), lambda j, i: (0, j))
    return pl.pallas_call(
        body, name=name, grid=(ch // tc, t // SEQ),
        in_specs=[xs, ws, bs, xs], out_specs=[xs, ws, bs],
        out_shape=[jax.ShapeDtypeStruct((t, ch), bf16), jax.ShapeDtypeStruct((CONV_WIDTH, ch), f32), jax.ShapeDtypeStruct((1, ch), f32)],
        compiler_params=_cparams(("parallel", "arbitrary")),
    )(x, w, b, dact)


GROUP_W = SSM_INNER // SSM_GROUPS
HEADS_PER_GROUP = SSM_HEADS // SSM_GROUPS
SSD_CHUNKS_PER_STEP = 4


def _split3(x):
    hi = x.astype(bf16)
    r1 = x - hi.astype(f32)
    mid = r1.astype(bf16)
    lo = (r1 - mid.astype(f32)).astype(bf16)
    return hi, mid, lo


def _dot_exact(x, sel, dims, x_is_lhs=True):
    parts = _split3(x)
    if x_is_lhs:
        return _dot(parts[0], sel, dims) + _dot(parts[1], sel, dims) + _dot(parts[2], sel, dims)
    return _dot(sel, parts[0], dims) + _dot(sel, parts[1], dims) + _dot(sel, parts[2], dims)


def _ssd_common(xbc_ref, dt_ref, bias_ref, alog_ref):
    r = lax.broadcasted_iota(jnp.int32, (CHUNK, CHUNK), 0)
    cidx = lax.broadcasted_iota(jnp.int32, (CHUNK, CHUNK), 1)
    causal = r >= cidx
    tril = causal.astype(bf16)
    expand = (lax.broadcasted_iota(jnp.int32, (CHUNK, SSM_INNER), 0)
              == lax.broadcasted_iota(jnp.int32, (CHUNK, SSM_INNER), 1) // HEAD_DIM).astype(bf16)
    head_lane = cidx < SSM_HEADS
    dtp = dt_ref[...] + bias_ref[...]
    dt = jnp.where(head_lane, _softplus(dtp), 0.0)
    a_neg = -jnp.exp(alog_ref[...])
    a = dt * a_neg
    nn_dims = ((1,), (0,))
    cs = _dot_exact(a, tril, nn_dims, x_is_lhs=False)
    dt_e = _dot_exact(dt, expand, nn_dims)
    cs_e = _dot_exact(cs, expand, nn_dims)
    xs = xbc_ref[:, 0:SSM_INNER]
    xg = xs * dt_e
    ecs = jnp.exp(cs_e)
    cs_last = cs_e[CHUNK - 1:CHUNK, :]
    dse = jnp.exp(cs_last - cs_e)
    cde = jnp.exp(cs_last)
    return dict(r=r, cidx=cidx, causal=causal, tril=tril, expand=expand, head_lane=head_lane, dtp=dtp, dt=dt, a_neg=a_neg,
                cs=cs, cst=cs.T, dt_e=dt_e, cs_e=cs_e, xs=xs, xg=xg, ecs=ecs, dse=dse, cde=cde)


def _decay_mat(q, h):
    return jnp.exp(jnp.where(q["causal"], q["cs"][:, h:h + 1] - q["cst"][h:h + 1, :], NEG_INF))


def _gate_norm(y, z, nw, gate=None):
    y2 = y * (_silu(z) if gate is None else gate)
    outs, xhats, rs = [], [], []
    for g in range(SSM_GROUPS):
        sl = slice(g * GROUP_W, (g + 1) * GROUP_W)
        yg = y2[:, sl]
        r = lax.rsqrt(jnp.mean(yg * yg, axis=-1, keepdims=True) + EPS)
        xhats.append(yg * r)
        rs.append(r)
        outs.append(yg * r * nw[:, sl])
    return y2, outs, xhats, rs


def _ssd_fwd(xbc, z, dtp, params, name):
    t = xbc.shape[0]
    n_chunk = SEQ // CHUNK
    n_step = n_chunk // SSD_CHUNKS_PER_STEP

    def body(xbc_ref, z_ref, dt_ref, bias_ref, alog_ref, dskip_ref, nw_ref, yn_ref, y_ref, hs_ref, h_scr):
        @pl.when(pl.program_id(1) == 0)
        def _():
            h_scr[...] = jnp.zeros_like(h_scr)

        for s in range(SSD_CHUNKS_PER_STEP):
            r = pl.ds(s * CHUNK, CHUNK)
            one_chunk(xbc_ref.at[r], z_ref.at[r], dt_ref.at[r], bias_ref, alog_ref, dskip_ref, nw_ref,
                      yn_ref.at[r], y_ref.at[r], hs_ref.at[pl.ds(s, 1)], h_scr)

    def one_chunk(xbc_ref, z_ref, dt_ref, bias_ref, alog_ref, dskip_ref, nw_ref, yn_ref, y_ref, hs_ref, h_scr):
        q = _ssd_common(xbc_ref, dt_ref, bias_ref, alog_ref)
        low = lax.broadcasted_iota(jnp.int32, (CHUNK, LANES), 1) < HEAD_DIM
        xgb = q["xg"].astype(bf16)
        wst = (q["xg"] * q["dse"]).astype(bf16)
        hs_ref[0] = h_scr[...]
        ys = []
        for g in range(SSM_GROUPS):
            gl = slice(g * GROUP_W, (g + 1) * GROUP_W)
            bg = xbc_ref[:, SSM_INNER + g * D_STATE:SSM_INNER + (g + 1) * D_STATE].astype(bf16)
            cg = xbc_ref[:, SSM_INNER + SSM_GROUPS * D_STATE + g * D_STATE:SSM_INNER + SSM_GROUPS * D_STATE + (g + 1) * D_STATE].astype(bf16)
            cb = _nt(cg, bg)
            hg = h_scr[g]
            yoff = _nn(cg, hg.astype(bf16)) * q["ecs"][:, gl]
            pieces = []
            for i in range(HEADS_PER_GROUP // 2):
                h0 = g * HEADS_PER_GROUP + 2 * i
                xp = xgb[:, h0 * HEAD_DIM:(h0 + 2) * HEAD_DIM]
                m0 = (cb * _decay_mat(q, h0)).astype(bf16)
                m1 = (cb * _decay_mat(q, h0 + 1)).astype(bf16)
                zero = jnp.zeros_like(xp)
                pieces.append(_nn(m0, jnp.where(low, xp, zero)) + _nn(m1, jnp.where(low, zero, xp)))
            ys.append(jnp.concatenate(pieces, axis=1) + yoff + dskip_ref[:, gl] * q["xs"][:, gl])
            h_scr[g] = hg * q["cde"][:, gl] + _tn(bg, wst[:, gl])
        y = jnp.concatenate(ys, axis=1)
        y_ref[...] = y
        _, outs, _, _ = _gate_norm(y, z_ref[...], nw_ref[...])
        yn_ref[...] = jnp.concatenate(outs, axis=1).astype(bf16)

    def rows(w):
        return pl.BlockSpec((SSD_CHUNKS_PER_STEP * CHUNK, w), lambda b, c: (b * n_step + c, 0))

    def par(w):
        return pl.BlockSpec((1, w), lambda b, c: (0, 0))

    return pl.pallas_call(
        body, name=name, grid=(t // SEQ, n_step),
        in_specs=[rows(CONV_CH), rows(SSM_INNER), rows(LANES), par(LANES), par(LANES), par(SSM_INNER), par(SSM_INNER)],
        out_specs=[rows(SSM_INNER), rows(SSM_INNER),
                   pl.BlockSpec((SSD_CHUNKS_PER_STEP, SSM_GROUPS, D_STATE, GROUP_W), lambda b, c: (b * n_step + c, 0, 0, 0))],
        out_shape=[jax.ShapeDtypeStruct((t, SSM_INNER), bf16), jax.ShapeDtypeStruct((t, SSM_INNER), f32),
                   jax.ShapeDtypeStruct((t // CHUNK, SSM_GROUPS, D_STATE, GROUP_W), f32)],
        scratch_shapes=[pltpu.VMEM((SSM_GROUPS, D_STATE, GROUP_W), f32)],
        compiler_params=_cparams(("parallel", "arbitrary")),
    )(xbc, z, dtp, *params)


def _ssd_bwd(xbc, z, dtp, y, hs, dyn, params, name):
    t = xbc.shape[0]
    n_chunk = SEQ // CHUNK
    n_step = n_chunk // SSD_CHUNKS_PER_STEP

    def body(xbc_ref, z_ref, dt_ref, y_ref, hs_ref, dyn_ref, bias_ref, alog_ref, dskip_ref, nw_ref,
             dxbc_ref, dz_ref, ddt_ref, dnw_ref, dds_ref, dal_ref, dbi_ref, dh_scr):
        @pl.when(pl.program_id(1) == 0)
        def _():
            dh_scr[...] = jnp.zeros_like(dh_scr)

        first_step = (pl.program_id(0) == 0) & (pl.program_id(1) == 0)
        for s in reversed(range(SSD_CHUNKS_PER_STEP)):
            r = pl.ds(s * CHUNK, CHUNK)
            one_chunk(xbc_ref.at[r], z_ref.at[r], dt_ref.at[r], y_ref.at[r], hs_ref.at[pl.ds(s, 1)], dyn_ref.at[r],
                      bias_ref, alog_ref, dskip_ref, nw_ref, dxbc_ref.at[r], dz_ref.at[r], ddt_ref.at[r],
                      dnw_ref, dds_ref, dal_ref, dbi_ref, dh_scr, first_step if s == SSD_CHUNKS_PER_STEP - 1 else False)

    def one_chunk(xbc_ref, z_ref, dt_ref, y_ref, hs_ref, dyn_ref, bias_ref, alog_ref, dskip_ref, nw_ref,
                  dxbc_ref, dz_ref, ddt_ref, dnw_ref, dds_ref, dal_ref, dbi_ref, dh_scr, first):
        q = _ssd_common(xbc_ref, dt_ref, bias_ref, alog_ref)
        low = lax.broadcasted_iota(jnp.int32, (CHUNK, LANES), 1) < HEAD_DIM
        last_row = lax.broadcasted_iota(jnp.int32, (CHUNK, GROUP_W), 0) == CHUNK - 1
        xs, xg = q["xs"], q["xg"]
        xgb = xg.astype(bf16)
        wf = xg * q["dse"]
        wst = wf.astype(bf16)
        zz = z_ref[...]
        yy = y_ref[...]
        sz, dsz = _silu_and_grad(zz)
        y2, _, xhats, rs = _gate_norm(yy, zz, nw_ref[...], gate=sz)
        dyn_ = dyn_ref[...]
        dy2s, dnws = [], []
        for g in range(SSM_GROUPS):
            gl = slice(g * GROUP_W, (g + 1) * GROUP_W)
            gw = dyn_[:, gl] * nw_ref[:, gl]
            dy2s.append(rs[g] * (gw - xhats[g] * jnp.mean(gw * xhats[g], axis=-1, keepdims=True)))
            dnws.append(_rowsum8(dyn_[:, gl] * xhats[g]))
        dy2 = jnp.concatenate(dy2s, axis=1)
        dy = dy2 * sz
        dz_ref[...] = (dy2 * yy * dsz).astype(bf16)
        dnw_p = jnp.concatenate(dnws, axis=1)
        dds_p = _rowsum8(dy * xs)
        dyb = dy.astype(bf16)
        gfull = (dy * q["ecs"]).astype(bf16)
        dcs_c = jnp.zeros((CHUNK, CHUNK), f32)
        dcs_r = jnp.zeros((CHUNK, CHUNK), f32)
        dcs_e_parts, dxg_parts = [], []
        for g in range(SSM_GROUPS):
            gl = slice(g * GROUP_W, (g + 1) * GROUP_W)
            bsl = slice(SSM_INNER + g * D_STATE, SSM_INNER + (g + 1) * D_STATE)
            csl = slice(SSM_INNER + SSM_GROUPS * D_STATE + g * D_STATE, SSM_INNER + SSM_GROUPS * D_STATE + (g + 1) * D_STATE)
            bg = xbc_ref[:, bsl].astype(bf16)
            cg = xbc_ref[:, csl].astype(bf16)
            cb = _nt(cg, bg)
            hg = hs_ref[0, g]
            hgb = hg.astype(bf16)
            dhn = dh_scr[g]
            dhnb = dhn.astype(bf16)
            yoff = _nn(cg, hgb) * q["ecs"][:, gl]
            dw_ = _nn(bg, dhnb)
            r_e = dw_ * wf[:, gl]
            to_last = jnp.sum(r_e, axis=0, keepdims=True) + jnp.sum(dhn * hg, axis=0, keepdims=True) * q["cde"][:, gl]
            dcs_e_parts.append(dy[:, gl] * yoff - r_e + jnp.where(last_row, to_last, 0.0))
            dcb = jnp.zeros((CHUNK, CHUNK), f32)
            dxg_pairs = []
            for i in range(HEADS_PER_GROUP // 2):
                h0 = g * HEADS_PER_GROUP + 2 * i
                psl = slice(h0 * HEAD_DIM, (h0 + 2) * HEAD_DIM)
                xp = xgb[:, psl]
                dyp = dyb[:, psl]
                zero = jnp.zeros_like(dyp)
                tns = []
                for a in range(2):
                    h = h0 + a
                    lm = _decay_mat(q, h)
                    m = cb * lm
                    dm = _nt(jnp.where(low, dyp, zero) if a == 0 else jnp.where(low, zero, dyp), xp)
                    dcb = dcb + dm * lm
                    nmat = dm * m
                    dcs_c = dcs_c + jnp.where(q["cidx"] == h, jnp.sum(nmat, axis=1, keepdims=True), 0.0)
                    dcs_r = dcs_r + jnp.where(q["r"] == h, jnp.sum(nmat, axis=0, keepdims=True), 0.0)
                    tns.append(_tn(m.astype(bf16), dyp))
                dxg_pairs.append(jnp.where(low, tns[0], tns[1]))
            dxg_parts.append(jnp.concatenate(dxg_pairs, axis=1) + dw_ * q["dse"][:, gl])
            dcbb = dcb.astype(bf16)
            dxbc_ref[:, csl] = _nt(gfull[:, gl], hgb) + _nn(dcbb, bg)
            dxbc_ref[:, bsl] = _nt(wst[:, gl], dhnb) + _tn(dcbb, cg)
            dh_scr[g] = dhn * q["cde"][:, gl] + _tn(cg, gfull[:, gl])
        dxg = jnp.concatenate(dxg_parts, axis=1)
        dcs_e = jnp.concatenate(dcs_e_parts, axis=1)
        dxbc_ref[:, 0:SSM_INNER] = dskip_ref[...] * dy + dxg * q["dt_e"]
        dcs = dcs_c - dcs_r.T + _dot_exact(dcs_e, q["expand"], ((1,), (1,)))
        triu = (q["cidx"] >= q["r"]).astype(bf16)
        da = _dot_exact(dcs, triu, ((1,), (0,)), x_is_lhs=False)
        ddt = _dot_exact(dxg * xs, q["expand"], ((1,), (1,))) + da * q["a_neg"]
        ddtp = jnp.where(q["head_lane"], ddt * _sigmoid(q["dtp"]), 0.0)
        ddt_ref[...] = ddtp.astype(bf16)
        dal_p = _rowsum8(da * q["dt"]) * q["a_neg"]
        dbi_p = _rowsum8(ddtp)
        def accumulate():
            dnw_ref[...] += dnw_p
            dds_ref[...] += dds_p
            dal_ref[...] += dal_p
            dbi_ref[...] += dbi_p

        if first is False:
            accumulate()
        else:
            @pl.when(first)
            def _():
                dnw_ref[...] = dnw_p
                dds_ref[...] = dds_p
                dal_ref[...] = dal_p
                dbi_ref[...] = dbi_p

            pl.when(jnp.logical_not(first))(accumulate)

    def rows(w):
        return pl.BlockSpec((SSD_CHUNKS_PER_STEP * CHUNK, w), lambda b, c: (b * n_step + n_step - 1 - c, 0))

    def par(w):
        return pl.BlockSpec((1, w), lambda b, c: (0, 0))

    def acc(w):
        return pl.BlockSpec((SUBLANES, w), lambda b, c: (0, 0))

    return pl.pallas_call(
        body, name=name, grid=(t // SEQ, n_step),
        in_specs=[rows(CONV_CH), rows(SSM_INNER), rows(LANES), rows(SSM_INNER),
                  pl.BlockSpec((SSD_CHUNKS_PER_STEP, SSM_GROUPS, D_STATE, GROUP_W), lambda b, c: (b * n_step + n_step - 1 - c, 0, 0, 0)),
                  rows(SSM_INNER), par(LANES), par(LANES), par(SSM_INNER), par(SSM_INNER)],
        out_specs=[rows(CONV_CH), rows(SSM_INNER), rows(LANES), acc(SSM_INNER), acc(SSM_INNER), acc(LANES), acc(LANES)],
        out_shape=[jax.ShapeDtypeStruct((t, CONV_CH), f32), jax.ShapeDtypeStruct((t, SSM_INNER), bf16), jax.ShapeDtypeStruct((t, LANES), bf16),
                   jax.ShapeDtypeStruct((SUBLANES, SSM_INNER), f32), jax.ShapeDtypeStruct((SUBLANES, SSM_INNER), f32),
                   jax.ShapeDtypeStruct((SUBLANES, LANES), f32), jax.ShapeDtypeStruct((SUBLANES, LANES), f32)],
        scratch_shapes=[pltpu.VMEM((SSM_GROUPS, D_STATE, GROUP_W), f32)],
        compiler_params=_cparams(("arbitrary", "arbitrary")),
    )(xbc, z, dtp, y, hs, dyn, *params)


def _adamw_update(g, w, m, v):
    mm = ADAM_B1 * m + (1.0 - ADAM_B1) * g
    vv = ADAM_B2 * v + (1.0 - ADAM_B2) * (g * g)
    m_hat = mm / (1.0 - ADAM_B1 ** ADAM_STEP)
    v_hat = vv / (1.0 - ADAM_B2 ** ADAM_STEP)
    return -ADAM_LR * (m_hat / (jnp.sqrt(v_hat) + ADAM_EPS) + ADAM_WD * w), mm, vv


def _adamw(g_parts, w, m, v, name):
    rows, width = w.shape
    n = len(g_parts)
    tr = _row_tile(rows)

    def body(*refs):
        g_refs, (w_ref, m_ref, v_ref, g_out, d_out, m_out, v_out) = refs[:n], refs[n:]
        g = g_refs[0][...].astype(f32)
        for r in g_refs[1:]:
            g = g + r[...].astype(f32)
        g_out[...] = g
        d_out[...], m_out[...], v_out[...] = _adamw_update(g, w_ref[...], m_ref[...], v_ref[...])

    spec = pl.BlockSpec((tr, width), lambda i: (i, 0))
    return pl.pallas_call(
        body, name=name, grid=(rows // tr,), in_specs=[spec] * (n + 3), out_specs=[spec] * 4,
        out_shape=[jax.ShapeDtypeStruct((rows, width), f32)] * 4, compiler_params=_cparams(("parallel",)),
    )(*g_parts, w, m, v)


def _adamw_layers(landed, w, m, v, after, name, layers_on_columns=False):
    depth = len(landed)
    _, rows, width = landed[0].shape
    tr = _row_tile(rows)
    n_i = rows // tr
    at = (lambda ref: ref) if layers_on_columns else (lambda ref: ref.at[0])

    def body(*refs):
        part_refs, (w_ref, m_ref, v_ref, _, g_out, d_out, m_out, v_out) = refs[:depth * N_DEV], refs[depth * N_DEV:]
        for l in range(depth):
            @pl.when(pl.program_id(0) == l)
            def _(l=l):
                g = part_refs[l * N_DEV][0].astype(f32)
                for r in part_refs[l * N_DEV + 1:(l + 1) * N_DEV]:
                    g = g + r[0].astype(f32)
                at(g_out)[...] = g
                at(d_out)[...], at(m_out)[...], at(v_out)[...] = _adamw_update(g, at(w_ref)[...], at(m_ref)[...], at(v_ref)[...])

    def part_spec(l, p):
        return pl.BlockSpec((1, tr, width), lambda ll, i: (p, jnp.where(ll == l, i, jnp.where(ll < l, 0, n_i - 1)), 0))

    state = (pl.BlockSpec((tr, width), lambda ll, i: (i, ll)) if layers_on_columns
             else pl.BlockSpec((1, tr, width), lambda ll, i: (ll, i, 0)))
    return pl.pallas_call(
        body, name=name, grid=(depth, n_i),
        in_specs=[part_spec(l, p) for l in range(depth) for p in range(N_DEV)] + [state] * 3 + [ANY], out_specs=[state] * 4,
        out_shape=[jax.ShapeDtypeStruct(w.shape, f32)] * 4, compiler_params=_cparams(("arbitrary", "arbitrary")),
    )(*[landed[l] for l in range(depth) for _ in range(N_DEV)], w, m, v, after)


def _row_tile(rows, cap=512):
    for cand in range(min(rows, cap) // SUBLANES * SUBLANES, 0, -SUBLANES):
        if rows % cand == 0:
            return cand
    return rows


def _cols_from_devices(g, width, name):
    n_dev, depth, a, b = g.shape

    def body(g_ref, o_ref):
        for i in range(n_dev):
            o_ref[0, :, i * b:(i + 1) * b] = g_ref[i, 0]
        if width > n_dev * b:
            o_ref[0, :, n_dev * b:width] = jnp.zeros((a, width - n_dev * b), o_ref.dtype)

    return pl.pallas_call(
        body, name=name, grid=(depth,), in_specs=[pl.BlockSpec((n_dev, 1, a, b), lambda l: (0, l, 0, 0))],
        out_specs=pl.BlockSpec((1, a, width), lambda l: (l, 0, 0)), out_shape=jax.ShapeDtypeStruct((depth, a, width), g.dtype),
        compiler_params=_cparams(("parallel",)),
    )(g)


def _devices_from_cols(per_layer, b, name, tr=256):
    depth = len(per_layer)
    a, width = per_layer[0].shape

    def body(*refs):
        o_ref = refs[depth]
        for l in range(depth):
            for i in range(N_DEV):
                o_ref[i, l] = refs[l][:, i * b:(i + 1) * b]

    return pl.pallas_call(
        body, name=name, grid=(a // tr,), in_specs=[pl.BlockSpec((tr, width), lambda r: (r, 0))] * depth,
        out_specs=pl.BlockSpec((N_DEV, depth, tr, b), lambda r: (0, 0, r, 0)),
        out_shape=jax.ShapeDtypeStruct((N_DEV, depth, a, b), per_layer[0].dtype), compiler_params=_cparams(("parallel",)),
    )(*per_layer)


def _me():
    return lax.axis_index("x"), lax.axis_index("y"), lax.axis_index("c")


def _allgather_two_level(shards, name):
    n = len(shards)
    per = 7

    def body(*refs):
        ins, outs, token = refs[:n], refs[n:2 * n], refs[2 * n]
        send_sems, recv_sems, local_sems = refs[2 * n + 1:]
        token[...] = jnp.zeros_like(token)
        x, y, c = _me()
        me, sibling = (x, y, c), (x, y, 1 - c)
        chips = [(1 - x, y), (x, 1 - y), (1 - x, 1 - y)]

        def slot(a, p):
            return outs[a].at[4 * p[0] + 2 * p[1] + p[2]]

        def copy(a, k, block, to, src=None):
            return pltpu.make_async_remote_copy(
                src_ref=slot(a, block) if src is None else src, dst_ref=slot(a, block),
                send_sem=send_sems.at[a * per + k], recv_sem=recv_sems.at[a * per + k], device_id=to, device_id_type=MESH)

        mine = [pltpu.make_async_copy(ins[a], slot(a, me), local_sems.at[a]) for a in range(n)]
        for cp in mine:
            cp.start()
        first = []
        for a in range(n):
            first.append(copy(a, 0, me, sibling, src=ins[a]))
            first += [copy(a, 1 + j, me, (*chip, c), src=ins[a]) for j, chip in enumerate(chips)]
        for cp in first:
            cp.start()
        passed = []
        for j, chip in enumerate(chips):
            for a in range(n):
                copy(a, 1 + j, (*chip, c), me).wait_recv()
                fwd = copy(a, 4 + j, (*chip, c), sibling)
                fwd.start()
                passed.append(fwd)
        for a in range(n):
            copy(a, 0, sibling, me).wait_recv()
            for j, chip in enumerate(chips):
                copy(a, 4 + j, (*chip, 1 - c), me).wait_recv()
        for cp in first + passed:
            cp.wait_send()
        for cp in mine:
            cp.wait()

    outs = pl.pallas_call(
        body, name=name, in_specs=[ANY] * n, out_specs=[ANY] * n + [pl.BlockSpec(memory_space=pltpu.VMEM)],
        out_shape=[jax.ShapeDtypeStruct((N_DEV,) + s.shape, s.dtype) for s in shards] + [jax.ShapeDtypeStruct((SUBLANES, LANES), f32)],
        scratch_shapes=[pltpu.SemaphoreType.DMA((n * per,)), pltpu.SemaphoreType.DMA((n * per,)), pltpu.SemaphoreType.DMA((n,))],
    )(*shards)
    return outs[:n], outs[n]


def _allgather_direct(row, name):
    def body(in_ref, out_ref, send_sems, recv_sems, local_sem):
        x, y, c = _me()
        mine = out_ref.at[4 * x + 2 * y + c]
        local = pltpu.make_async_copy(in_ref, mine, local_sem)
        local.start()
        sends = []
        for k in range(1, N_DEV):
            px, py, pc = x ^ (k >> 2), y ^ ((k >> 1) & 1), c ^ (k & 1)
            sends.append(pltpu.make_async_remote_copy(
                src_ref=in_ref, dst_ref=mine, send_sem=send_sems.at[k - 1], recv_sem=recv_sems.at[k - 1],
                device_id=(px, py, pc), device_id_type=MESH))
        for cp in sends:
            cp.start()
        for k in range(1, N_DEV):
            px, py, pc = x ^ (k >> 2), y ^ ((k >> 1) & 1), c ^ (k & 1)
            theirs = out_ref.at[4 * px + 2 * py + pc]
            pltpu.make_async_remote_copy(
                src_ref=in_ref, dst_ref=theirs, send_sem=send_sems.at[k - 1], recv_sem=recv_sems.at[k - 1],
                device_id=(px, py, pc), device_id_type=MESH).wait_recv()
        for cp in sends:
            cp.wait_send()
        local.wait()

    return pl.pallas_call(
        body, name=name, in_specs=[ANY], out_specs=ANY, out_shape=jax.ShapeDtypeStruct((N_DEV,) + row.shape, row.dtype),
        scratch_shapes=[pltpu.SemaphoreType.DMA((N_DEV - 1,)), pltpu.SemaphoreType.DMA((N_DEV - 1,)), pltpu.SemaphoreType.DMA],
    )(row)


N_CHIP = N_DEV // 2
HBM = pl.BlockSpec(memory_space=pltpu.HBM)
SEM = pl.BlockSpec(memory_space=pltpu.SEMAPHORE)
EFFECT = pltpu.SideEffectType.DATAFLOW_SIDE_EFFECTING


def _peer(k):
    x, y, c = _me()
    return x ^ (k >> 2), y ^ ((k >> 1) & 1), c ^ (k & 1)


def _direct_copies(srcs, lands, send_sems, recv_sems, per_peer):
    x, y, c = _me()
    me = 4 * x + 2 * y + c
    copies = []
    for a in range(len(srcs)):
        for k in range(1, N_DEV):
            px, py, pc = _peer(k)
            piece = srcs[a].at[4 * px + 2 * py + pc] if per_peer else srcs[a]
            copies.append(pltpu.make_async_remote_copy(
                src_ref=piece, dst_ref=lands[a].at[me], send_sem=send_sems.at[a * (N_DEV - 1) + k - 1],
                recv_sem=recv_sems.at[a * (N_DEV - 1) + k - 1], device_id=(px, py, pc), device_id_type=MESH))
    return copies


def _direct_start(srcs, lands, per_peer, name):
    n = len(srcs)
    n_sem = n * (N_DEV - 1)

    def body(*refs):
        src_refs, land_refs = refs[:n], refs[n:2 * n]
        send_sems, recv_sems = refs[2 * n], refs[2 * n + 1]
        token = refs[-1]
        for cp in _direct_copies(src_refs, land_refs, send_sems, recv_sems, per_peer):
            cp.start()
        token[...] = jnp.zeros_like(token)

    outs = pl.pallas_call(
        body, name=name,
        out_shape=(pltpu.SemaphoreType.DMA((n_sem,)), pltpu.SemaphoreType.DMA((n_sem,)),
                   *[pltpu.HBM(s.shape, s.dtype) for s in srcs], *[pltpu.HBM(s.shape, s.dtype) for s in lands],
                   jax.ShapeDtypeStruct((SUBLANES, LANES), f32)),
        in_specs=[HBM] * (2 * n), out_specs=(SEM, SEM, *[HBM] * (2 * n), pl.BlockSpec(memory_space=pltpu.VMEM)),
        input_output_aliases={i: 2 + i for i in range(2 * n)},
        compiler_params=pltpu.CompilerParams(has_side_effects=EFFECT),
    )(*[pltpu.with_memory_space_constraint(s, pltpu.HBM) for s in srcs], *[pltpu.with_memory_space_constraint(s, pltpu.HBM) for s in lands])
    return outs[0], outs[1], outs[2:2 + n], outs[2 + n:2 + 2 * n], outs[-1]


def _direct_wait(send_sems, recv_sems, srcs, lands, after, per_peer, name):
    n = len(srcs)

    def body(*refs):
        src_refs, land_refs = refs[:n], refs[n:2 * n]
        s_sems, r_sems = refs[2 * n], refs[2 * n + 1]
        for cp in _direct_copies(src_refs, land_refs, s_sems, r_sems, per_peer):
            cp.wait_send()
            cp.wait_recv()

    outs = pl.pallas_call(
        body, name=name,
        out_shape=tuple(pltpu.HBM(s.shape, s.dtype) for s in list(srcs) + list(lands)),
        in_specs=[HBM] * (2 * n) + [SEM, SEM, ANY], out_specs=tuple([HBM] * (2 * n)),
        input_output_aliases={i: i for i in range(2 * n)},
        compiler_params=pltpu.CompilerParams(has_side_effects=EFFECT),
    )(*srcs, *lands, send_sems, recv_sems, after)
    return outs[n:]


def _row(v, width=None):
    v = v.reshape(1, -1).astype(f32)
    if width is not None and v.shape[1] < width:
        v = jnp.pad(v, ((0, 0), (0, width - v.shape[1])))
    return v


def _layer_params(p, l):
    return dict(
        norm_mix=_row(p["norm_mix"][l]), norm_ffn=_row(p["norm_ffn"][l]), conv_w=p["conv_w"][l], conv_b=_row(p["conv_b"][l]),
        ssd=(_row(p["dt_bias"][l], LANES), _row(p["a_log"][l], LANES), _row(jnp.repeat(p["d_skip"][l], HEAD_DIM)), _row(p["ssm_norm"][l])))


def _layer_fwd(h, w_in, rest, sp, tabs, l):
    tag = f"l{l}_"
    hn = _rmsnorm_fwd(h, sp["norm_mix"], tag + "norm_mix")
    qkv, z, xbc_pre = _in_proj(hn, w_in, (QKV_WIDTH, SSM_INNER, CONV_CH), tag + "proj")
    dtp = _matmul(hn, w_in, mode="nn", n_out=LANES, tn=LANES, b_off=DT_OFF // LANES, name=tag + "proj_dt")
    prep = _attn_prep(qkv, tabs, tag + "attn_prep")
    o, o16, lse = _attn_fwd(prep, tag + "attn_fwd")
    xbc = _conv_fwd(xbc_pre, sp["conv_w"], sp["conv_b"], tag + "conv_fwd")
    yn, y, hs = _ssd_fwd(xbc, z, dtp, sp["ssd"], tag + "ssd_fwd")
    w_out, w_gate, w_up, w_down = rest(yn) if callable(rest) else rest
    h2 = _out_proj(o16, yn, w_out, h, tag + "out_proj")
    hn2 = _rmsnorm_fwd(h2, sp["norm_ffn"], tag + "norm_ffn")
    g, u, act = _swiglu_fwd(hn2, w_gate, w_up, tag + "ffn_up")
    h3 = _matmul(act, w_down, mode="nn", tk=FFN_HIDDEN, add=h2, name=tag + "ffn_down")
    saved = dict(h=h, hn=hn, prep=prep, z=z, xbc_pre=xbc_pre, dtp=dtp, o=o, o16=o16, lse=lse, xbc=xbc, yn=yn, y=y, hs=hs, h2=h2, hn2=hn2, g=g, u=u, act=act,
                 rest=(w_out, w_gate, w_up, w_down))
    return h3, saved


def _layer_bwd(dh3_pair, s, big, sp, tabs, l, gd=f32, after_ffn=None):
    tag = f"l{l}_"
    dh3, dh3b = dh3_pair
    w_in, w_out, w_gate, w_up, w_down = big
    dg, du = _swiglu_bwd(dh3b, w_down, s["g"], s["u"], tag + "ffn_down_bwd")
    dw_down = _matmul(s["act"], dh3b, mode="tn", tm=1408, tn=512, tk=2048, out_dtype=gd, name=tag + "dw_down")
    dw_gate = _matmul(dg, s["hn2"], mode="tn", tm=1408, tn=512, tk=2048, out_dtype=gd, name=tag + "dw_gate")
    dw_up = _matmul(du, s["hn2"], mode="tn", tm=1408, tn=512, tk=2048, out_dtype=gd, name=tag + "dw_up")
    norm_ffn = sp["norm_ffn"] if after_ffn is None else sp["norm_ffn"] + after_ffn(dict(w_gate=dw_gate, w_up=dw_up, w_down=dw_down))
    dh2, dh2b, dnf = _nt_norm_bwd([(dg, w_gate), (du, w_up)], s["h2"], norm_ffn, dh3, tag + "ffn_up_bwd_norm", tk=1408, b_is_kd=True,
                                  vmem=VMEM_LIMIT_TWO_PAIRS)
    d_o = _matmul(dh2b, w_out, mode="nt", n_out=ATTN_WIDTH, tn=512, b_off=0, name=tag + "out_attn_bwd")
    dyn = _matmul(dh2b, w_out, mode="nt", n_out=SSM_INNER, tn=512, b_off=1, name=tag + "out_ssm_bwd")
    dw_out = jnp.concatenate([_matmul(s["o16"], dh2b, mode="tn", tm=512, tn=512, tk=2048, out_dtype=gd, name=tag + "dw_out_attn"),
                              _matmul(s["yn"], dh2b, mode="tn", tm=512, tn=512, tk=2048, out_dtype=gd, name=tag + "dw_out_ssm")], axis=0)
    dxbc, dz, ddtp, dnw, dds, dal, dbi = _ssd_bwd(s["xbc"], s["z"], s["dtp"], s["y"], s["hs"], dyn, sp["ssd"], tag + "ssd_bwd")
    dxbc_pre, dconv_w, dconv_b = _conv_bwd(s["xbc_pre"], sp["conv_w"], sp["conv_b"], dxbc, tag + "conv_bwd")
    dq, dk, dv = _attn_bwd(s["prep"], tabs, s["o"], s["lse"], d_o, tag + "attn_bwd")
    dproj = jnp.concatenate([dq, dk, dv, dz, dxbc_pre, ddtp], axis=1)
    dw_in = _matmul(s["hn"], dproj, mode="tn", tm=512, tn=1152, tk=2048, out_dtype=gd, name=tag + "dw_in")
    res = _nt_norm_bwd([(dproj, w_in)], s["h"], sp["norm_mix"], dh2, tag + "proj_bwd_norm", tk=1152, bf16_copy=l > 0)
    dh, dhb, dnm = res if l > 0 else (res[0], None, res[1])
    grads = dict(
        norm_mix=dnm.sum(0), w_in=dw_in, conv_w=dconv_w, conv_b=dconv_b[0], dt_bias=dbi.sum(0)[:SSM_HEADS], a_log=dal.sum(0)[:SSM_HEADS],
        d_skip=dds.sum(0).reshape(SSM_HEADS, HEAD_DIM).sum(1), ssm_norm=dnw.sum(0), w_out=dw_out, norm_ffn=dnf.sum(0),
        w_gate=dw_gate, w_up=dw_up, w_down=dw_down)
    return (dh, dhb), grads


def _local_step(x, positions, target, p, bigs):
    tabs = _rope_tables(positions.reshape(-1, 1), "rope_tables")
    h = x
    saved, sps = [], []
    for l in range(DEPTH):
        sps.append(_layer_params(p, l))
        h, s = _layer_fwd(h, bigs[l][0], bigs[l][1:], sps[l], tabs, l)
        saved.append(s)
    dh, dhb, loss_parts, dfn = _final_loss(h, _row(p["final_norm"]), target, "final_loss")
    dh = (dh, dhb)
    layer_grads = [None] * DEPTH
    for l in reversed(range(DEPTH)):
        dh, layer_grads[l] = _layer_bwd(dh, saved[l], bigs[l], sps[l], tabs, l)
    grads = {k: [layer_grads[l][k] for l in range(DEPTH)] for k in layer_grads[0]}
    grads["final_norm"] = dfn.sum(0)
    return jnp.sum(loss_parts), dh[0], grads


BIG = ("w_in", "w_out", "w_gate", "w_up", "w_down")
REST = BIG[1:]
FFN = ("w_gate", "w_up", "w_down")
MIX = ("w_in", "w_out")
COL_SHARDED = ("w_in",)
TRANSPOSED = ("w_gate", "w_up")
SMALL = ("norm_mix", "conv_b", "dt_bias", "a_log", "d_skip", "ssm_norm", "norm_ffn", "final_norm")
WEIGHTS = ("norm_mix", "w_in", "conv_w", "conv_b", "dt_bias", "a_log", "d_skip", "ssm_norm", "w_out", "norm_ffn", "w_gate", "w_up", "w_down", "final_norm")
SMALL_ROWS = 88
CONVW_ROWS = 96
CONVW_SHARD_ROWS = 16


def _full_from_gathered(name, g, l):
    _, a, b = g.shape
    if name in COL_SHARDED:
        width = IN_PROJ_PAD if name == "w_in" else N_DEV * b
        return _cols_from_devices(g.reshape(N_DEV, 1, a, b), width, f"cols_l{l}_{name}").reshape(a, width)
    return g.reshape(N_DEV * a, b)


def _by_device(name, full, shard_shape, l):
    a, b = shard_shape
    if name in COL_SHARDED:
        return _devices_from_cols([full], b, f"devs_l{l}_{name}").reshape(N_CHIP, 2, a, b)
    return full.reshape(N_CHIP, 2, a, b)


def _pack_rows(parts, rows, width):
    flat = jnp.concatenate([q.reshape(-1) for q in parts])
    return jnp.pad(flat, (0, rows * width - flat.shape[0])).reshape(rows, width)


def _unpack(flat, like):
    out, off = [], 0
    for q in like:
        out.append(flat[off:off + q.size].reshape(q.shape))
        off += q.size
    return out


def kernel(x, positions, norm_mix, w_in, conv_w, conv_b, dt_bias, a_log, d_skip, ssm_norm, w_out, norm_ffn, w_gate, w_up, w_down, final_norm, loss_target, m_norm_mix, m_w_in, m_conv_w, m_conv_b, m_dt_bias, m_a_log, m_d_skip, m_ssm_norm, m_w_out, m_norm_ffn, m_w_gate, m_w_up, m_w_down, m_final_norm, v_norm_mix, v_w_in, v_conv_w, v_conv_b, v_dt_bias, v_a_log, v_d_skip, v_ssm_norm, v_w_out, v_norm_ffn, v_w_gate, v_w_up, v_w_down, v_final_norm):
    w = dict(norm_mix=norm_mix, w_in=w_in, conv_w=conv_w, conv_b=conv_b, dt_bias=dt_bias, a_log=a_log, d_skip=d_skip, ssm_norm=ssm_norm,
             w_out=w_out, norm_ffn=norm_ffn, w_gate=w_gate, w_up=w_up, w_down=w_down, final_norm=final_norm)
    m = dict(norm_mix=m_norm_mix, w_in=m_w_in, conv_w=m_conv_w, conv_b=m_conv_b, dt_bias=m_dt_bias, a_log=m_a_log, d_skip=m_d_skip,
             ssm_norm=m_ssm_norm, w_out=m_w_out, norm_ffn=m_norm_ffn, w_gate=m_w_gate, w_up=m_w_up, w_down=m_w_down, final_norm=m_final_norm)
    v = dict(norm_mix=v_norm_mix, w_in=v_w_in, conv_w=v_conv_w, conv_b=v_conv_b, dt_bias=v_dt_bias, a_log=v_a_log, d_skip=v_d_skip,
             ssm_norm=v_ssm_norm, w_out=v_w_out, norm_ffn=v_norm_ffn, w_gate=v_w_gate, w_up=v_w_up, w_down=v_w_down, final_norm=v_final_norm)
    ax, ay, ac = lax.axis_index("x"), lax.axis_index("y"), lax.axis_index("c")
    dev = 4 * ax + 2 * ay + ac

    assert DEPTH == 2
    t = x.shape[0] * x.shape[1]
    xf, target = x.reshape(t, D_MODEL), loss_target.reshape(t, D_MODEL)

    def own_slot(block):
        return lax.dynamic_update_slice(lax.empty((N_DEV,) + block.shape[1:], block.dtype), block, (dev,) + (0,) * (block.ndim - 1))

    def layer_shard(arr, k, l):
        return jnp.transpose(arr, (2, 0, 1))[:, l, :] if k in TRANSPOSED else arr[l]

    def gather_start(keys, l, tie, name):
        shards = [(layer_shard(w[keys[0]], keys[0], l) + tie).astype(bf16)] + [layer_shard(w[k], k, l).astype(bf16) for k in keys[1:]]
        return _direct_start(shards, [own_slot(s[None]) for s in shards], False, name)

    def scatter_start(keys, grads_l, l, name):
        shapes = [(w[k].shape[2], w[k].shape[1]) if k in TRANSPOSED else w[k].shape[1:] for k in keys]
        by_dev = [_by_device(k, grads_l[k], sh, l).reshape((N_DEV,) + sh) for k, sh in zip(keys, shapes)]
        return _direct_start(by_dev, [own_slot(lax.dynamic_slice_in_dim(g, dev, 1, 0)) for g in by_dev], True, name)

    (g_in0, conv_all), tie = _allgather_two_level([w["w_in"][0].astype(bf16), w["conv_w"]], "gather_l0_w_in")
    rest0_copy = gather_start(REST, 0, tie[0, 0], "gather_l0_rest_start")
    l1_copy = gather_start(BIG, 1, rest0_copy[4][0, 0], "gather_l1_start")
    p = {k: w[k] for k in SMALL}
    p["norm_mix"] = p["norm_mix"] + l1_copy[4][0, 0]
    p["conv_w"] = jnp.transpose(conv_all, (1, 2, 0, 3)).reshape(DEPTH, CONV_WIDTH, CONV_CH)
    sp0, sp1 = _layer_params(p, 0), _layer_params(p, 1)

    def rest0(after):
        lands = _direct_wait(*rest0_copy[:4], after, False, "gather_l0_rest_wait")
        return tuple(_full_from_gathered(k, g, 0) for k, g in zip(REST, lands))

    tabs = _rope_tables(positions.reshape(t, 1), "rope_tables")
    w_in0 = _full_from_gathered("w_in", g_in0, 0)
    h1, saved0 = _layer_fwd(xf, w_in0, rest0, sp0, tabs, 0)
    lands1 = _direct_wait(*l1_copy[:4], h1, False, "gather_l1_wait")
    bigs1 = tuple(_full_from_gathered(k, g, 1) for k, g in zip(BIG, lands1))
    h2, saved1 = _layer_fwd(h1, bigs1[0], bigs1[1:], sp1, tabs, 1)
    dh, dhb, loss_parts, dfn = _final_loss(h2, _row(p["final_norm"]), target, "final_loss")
    loss_local = jnp.sum(loss_parts)

    dh, grads1 = _layer_bwd((dh, dhb), saved1, bigs1, sp1, tabs, 1, gd=bf16)
    l1_grads = scatter_start(BIG, grads1, 1, "scatter_l1_start")
    w_out0, w_gate0, w_up0, w_down0 = saved0["rest"]
    bigs0 = (w_in0, w_out0, w_gate0, w_up0, w_down0 + l1_grads[4][0, 0].astype(bf16))
    ffn0_grads = []

    def after_ffn(grads_ffn):
        ffn0_grads.append(scatter_start(FFN, grads_ffn, 0, "scatter_l0_ffn_start"))
        return ffn0_grads[0][4][0, 0]

    (dx, _), grads0 = _layer_bwd(dh, saved0, bigs0, sp0, tabs, 0, gd=bf16, after_ffn=after_ffn)
    mix0_grads = scatter_start(MIX, grads0, 0, "scatter_l0_mix_start")
    landed = {(k, 1): g for k, g in zip(BIG, _direct_wait(*l1_grads[:4], dx, True, "scatter_l1_wait"))}
    landed.update({(k, 0): g for k, g in zip(FFN, _direct_wait(*ffn0_grads[0][:4], dx, True, "scatter_l0_ffn_wait"))})
    out_g, out_d, out_m, out_v = {}, {}, {}, {}

    def update(keys, after):
        for k in keys:
            parts = [landed[k, l] for l in range(DEPTH)]
            if k in TRANSPOSED:
                depth, a, b = w[k].shape
                state = [jnp.transpose(s, (2, 0, 1)).reshape(b, depth * a) for s in (w[k], m[k], v[k])]
                res = _adamw_layers(parts, *state, after, "adamw_" + k, layers_on_columns=True)
                res = [jnp.transpose(r.reshape(b, depth, a), (1, 2, 0)) for r in res]
            else:
                res = _adamw_layers(parts, w[k], m[k], v[k], after, "adamw_" + k)
            for dst, r in zip((out_g, out_d, out_m, out_v), res):
                dst[k] = r

    update(FFN, mix0_grads[4])
    grads = {k: [grads0[k], grads1[k]] for k in grads0 if k not in BIG}
    grads["final_norm"] = dfn.sum(0) + mix0_grads[4][0, 0]

    small_like = [w[k] for k in SMALL]
    small_grads = [jnp.stack(grads[k]) if k != "final_norm" else grads[k] for k in SMALL]
    small_pack = jnp.concatenate([_pack_rows(small_grads, SMALL_ROWS, LANES), _pack_rows([jnp.stack(grads["conv_w"])], CONVW_ROWS, LANES)], axis=0)
    parts = _allgather_direct(small_pack, "gather_small_grads")
    g_s, d_s, m_s, v_s = _adamw(
        [parts[i, :SMALL_ROWS] for i in range(N_DEV)], _pack_rows(small_like, SMALL_ROWS, LANES),
        _pack_rows([m[k] for k in SMALL], SMALL_ROWS, LANES), _pack_rows([v[k] for k in SMALL], SMALL_ROWS, LANES), "adamw_replicated")
    for dst, src in ((out_g, g_s), (out_d, d_s), (out_m, m_s), (out_v, v_s)):
        dst.update(zip(SMALL, _unpack(src.reshape(-1), small_like)))
    shard_w = conv_w.shape[-1]
    conv_parts = parts[:, SMALL_ROWS:].reshape(N_DEV, DEPTH, CONV_WIDTH, CONV_CH)
    conv_mine = lax.dynamic_slice_in_dim(conv_parts, dev * shard_w, shard_w, axis=3)
    g_c, d_c, m_c, v_c = _adamw(
        [_pack_rows([conv_mine[i]], CONVW_SHARD_ROWS, LANES) for i in range(N_DEV)], _pack_rows([conv_w], CONVW_SHARD_ROWS, LANES),
        _pack_rows([m["conv_w"]], CONVW_SHARD_ROWS, LANES), _pack_rows([v["conv_w"]], CONVW_SHARD_ROWS, LANES), "adamw_conv_w")
    for dst, src in ((out_g, g_c), (out_d, d_c), (out_m, m_c), (out_v, v_c)):
        dst["conv_w"] = src.reshape(-1)[:conv_w.size].reshape(conv_w.shape)

    landed.update({(k, 0): g for k, g in zip(MIX, _direct_wait(*mix0_grads[:4], v_c + out_v["w_down"][0, :CONVW_SHARD_ROWS, :LANES], True, "scatter_l0_mix_wait"))})
    update(MIX, v_c)

    loss = lax.psum(loss_local, ("x", "y", "c"))
    return (loss, dx.reshape(x.shape), *[out_g[k] for k in WEIGHTS], *[out_d[k] for k in WEIGHTS],
            *[out_m[k] for k in WEIGHTS], *[out_v[k] for k in WEIGHTS])
```

```python
import jax
import jax.numpy as jnp
import numpy as np
from jax import lax
from jax.experimental import pallas as pl
from jax.experimental.pallas import tpu as pltpu

f32 = jnp.float32
bf16 = jnp.bfloat16

D_MODEL = 1024
SEQ = 2048
DEPTH = 2
HEAD_DIM = 64
N_ATTN_HEADS = 8
N_KV_HEADS = 2
ATTN_WIDTH = 512
KV_WIDTH = 128
ROPE_DIM = 16
ROPE_THETA = 500000.0
DILATIONS = (1, 4, 16)
ATTN_BLOCK = 128
SSM_HEADS = 16
SSM_INNER = 1024
SSM_GROUPS = 2
D_STATE = 128
CONV_WIDTH = 4
CHUNK = 128
CONV_CH = 1536
MIX_WIDTH = 1536
QKV_WIDTH = ATTN_WIDTH + 2 * KV_WIDTH
DT_OFF = 3328
IN_PROJ = 3344
IN_PROJ_PAD = 3456
FFN_HIDDEN = 2816
EPS = 1e-5
N_DEV = 8
ADAM_LR = 0.001
ADAM_B1 = 0.9
ADAM_B2 = 0.999
ADAM_EPS = 1e-08
ADAM_WD = 0.01
ADAM_STEP = 10

LANES = 128
SUBLANES = 8
VMEM_LIMIT = 56 * 1024 * 1024
VMEM_LIMIT_TWO_PAIRS = 60 * 1024 * 1024

MESH = pl.DeviceIdType.MESH
ANY = pl.BlockSpec(memory_space=pl.ANY)


def _cparams(sem, vmem=None):
    return pltpu.CompilerParams(dimension_semantics=sem, vmem_limit_bytes=vmem or VMEM_LIMIT)


def _sigmoid(x):
    return 1.0 / (1.0 + jnp.exp(-x))


def _silu(x):
    return x * _sigmoid(x)


def _dsilu(x):
    s = _sigmoid(x)
    return s * (1.0 + x * (1.0 - s))


def _silu_and_grad(x):
    s = _sigmoid(x)
    return x * s, s * (1.0 + x * (1.0 - s))


def _softplus(x):
    return jnp.maximum(x, 0.0) + jnp.log(1.0 + jnp.exp(-jnp.abs(x)))


def _dot(a, b, dims, precision=None):
    return lax.dot_general(a, b, (dims, ((), ())), preferred_element_type=f32, precision=precision)


def _nn(a, b, precision=None):
    return _dot(a, b, ((1,), (0,)), precision)


def _nt(a, b):
    return _dot(a, b, ((1,), (1,)))


def _tn(a, b):
    return _dot(a, b, ((0,), (0,)))


def _rowsum8(t):
    n, w = t.shape
    return jnp.sum(t.reshape(n // SUBLANES, SUBLANES, w), axis=0)


def _matmul(a, b, *, mode, n_out=None, b_off=0, add=None, out_dtype=f32, tm=2048, tn=512, tk=1024, name):
    if mode == "tn":
        kk, m = a.shape
    else:
        m, kk = a.shape
    n = n_out if n_out is not None else (b.shape[0] if mode == "nt" else b.shape[1])
    tm, tn, tk = min(tm, m), min(tn, n), min(tk, kk)
    assert m % tm == 0 and n % tn == 0 and kk % tk == 0, (name, m, n, kk, tm, tn, tk)
    nk = kk // tk
    if mode == "nn":
        a_spec = pl.BlockSpec((tm, tk), lambda i, j, k: (i, k))
        b_spec = pl.BlockSpec((tk, tn), lambda i, j, k: (k, j + b_off))
        dims = ((1,), (0,))
    elif mode == "nt":
        a_spec = pl.BlockSpec((tm, tk), lambda i, j, k: (i, k))
        b_spec = pl.BlockSpec((tn, tk), lambda i, j, k: (j + b_off, k))
        dims = ((1,), (1,))
    else:
        a_spec = pl.BlockSpec((tk, tm), lambda i, j, k: (k, i))
        b_spec = pl.BlockSpec((tk, tn), lambda i, j, k: (k, j + b_off))
        dims = ((0,), (0,))
    o_spec = pl.BlockSpec((tm, tn), lambda i, j, k: (i, j))
    has_add = add is not None

    def body(*refs):
        if has_add:
            a_ref, b_ref, add_ref, o_ref, acc_ref = refs
        else:
            a_ref, b_ref, o_ref, acc_ref = refs
        k = pl.program_id(2)
        part = _dot(a_ref[...].astype(bf16), b_ref[...].astype(bf16), dims)

        @pl.when(k == 0)
        def _():
            acc_ref[...] = part

        @pl.when(k > 0)
        def _():
            acc_ref[...] += part

        @pl.when(k == nk - 1)
        def _():
            r = acc_ref[...]
            if has_add:
                r = r + add_ref[...]
            o_ref[...] = r.astype(out_dtype)

    in_specs = [a_spec, b_spec] + ([o_spec] if has_add else [])
    args = (a, b) + ((add,) if has_add else ())
    return pl.pallas_call(
        body, name=name, grid=(m // tm, n // tn, nk), in_specs=in_specs, out_specs=o_spec,
        out_shape=jax.ShapeDtypeStruct((m, n), out_dtype), scratch_shapes=[pltpu.VMEM((tm, tn), f32)],
        compiler_params=_cparams(("parallel", "parallel", "arbitrary")),
    )(*args)


def _in_proj(hn, w_in, widths, name, tm=2048, tn=256):
    m, k = hn.shape
    starts = [sum(widths[:i]) // tn for i in range(len(widths))]
    counts = [wd // tn for wd in widths]
    assert m % tm == 0 and all(wd % tn == 0 for wd in widths)
    n_out = len(widths)

    def body(a_ref, w_ref, *o_refs):
        j = pl.program_id(1)
        acc = _nn(a_ref[...], w_ref[...])
        for s, c, o_ref in zip(starts, counts, o_refs):
            @pl.when((j >= s) & (j < s + c))
            def _(o_ref=o_ref):
                o_ref[...] = acc

    def o_spec(s, c):
        return pl.BlockSpec((tm, tn), lambda i, j: (i, jnp.clip(j - s, 0, c - 1)))

    return pl.pallas_call(
        body, name=name, grid=(m // tm, sum(counts)),
        in_specs=[pl.BlockSpec((tm, k), lambda i, j: (i, 0)), pl.BlockSpec((k, tn), lambda i, j: (0, j))],
        out_specs=[o_spec(s, c) for s, c in zip(starts, counts)],
        out_shape=[jax.ShapeDtypeStruct((m, wd), f32) for wd in widths], compiler_params=_cparams(("parallel", "arbitrary")),
    )(hn, w_in)


def _out_proj(o, yn, w_out, h, name, tm=2048, tn=512):
    m, kb = o.shape
    n = w_out.shape[1]
    n_y = yn.shape[1] // kb
    assert yn.shape[1] % kb == 0 and w_out.shape[0] == kb * (1 + n_y) and m % tm == 0 and n % tn == 0

    def body(*refs):
        o_ref, y_refs, w_refs, h_ref, out_ref = refs[0], refs[1:1 + n_y], refs[1 + n_y:2 + 2 * n_y], refs[-2], refs[-1]
        acc = h_ref[...] + _nn(o_ref[...].astype(bf16), w_refs[0][...])
        for y_ref, w_ref in zip(y_refs, w_refs[1:]):
            acc = acc + _nn(y_ref[...], w_ref[...])
        out_ref[...] = acc

    res = pl.BlockSpec((tm, tn), lambda i, j: (i, j))

    def a_blk(c):
        return pl.BlockSpec((tm, kb), lambda i, j: (i, c))

    def w_blk(r):
        return pl.BlockSpec((kb, tn), lambda i, j: (r, j))

    return pl.pallas_call(
        body, name=name, grid=(m // tm, n // tn),
        in_specs=[a_blk(0)] + [a_blk(c) for c in range(n_y)] + [w_blk(r) for r in range(1 + n_y)] + [res],
        out_specs=res, out_shape=jax.ShapeDtypeStruct((m, n), f32), compiler_params=_cparams(("parallel", "parallel")),
    )(o, *[yn] * n_y, *[w_out] * (1 + n_y), h)


def _swiglu_fwd(hn, w_gate, w_up, name, tm=4096, tn=256):
    m, k = hn.shape
    n = w_gate.shape[0]
    assert m % tm == 0 and n % tn == 0, (name, m, n, tm, tn)

    def body(a_ref, wg_ref, wu_ref, g_ref, u_ref, act_ref):
        a = a_ref[...]
        g = _nt(a, wg_ref[...])
        u = _nt(a, wu_ref[...])
        sg, dsg = _silu_and_grad(g)
        g_ref[...] = (u * dsg).astype(bf16)
        u_ref[...] = sg.astype(bf16)
        act_ref[...] = (sg * u).astype(bf16)

    a_spec = pl.BlockSpec((tm, k), lambda i, j: (i, 0))
    w_spec = pl.BlockSpec((tn, k), lambda i, j: (j, 0))
    o_spec = pl.BlockSpec((tm, tn), lambda i, j: (i, j))
    return pl.pallas_call(
        body, name=name, grid=(m // tm, n // tn), in_specs=[a_spec, w_spec, w_spec], out_specs=[o_spec, o_spec, o_spec],
        out_shape=[jax.ShapeDtypeStruct((m, n), bf16)] * 3,
        compiler_params=_cparams(("parallel", "parallel")),
    )(hn, w_gate, w_up)


def _swiglu_bwd(dh, w_down, g, u, name, tm=4096, tn=256):
    m, k = dh.shape
    n = w_down.shape[0]
    assert m % tm == 0 and n % tn == 0, (name, m, n, tm, tn)

    def body(a_ref, w_ref, g_ref, u_ref, dg_ref, du_ref):
        dact = _nt(a_ref[...].astype(bf16), w_ref[...])
        dg_ref[...] = (dact * g_ref[...].astype(f32)).astype(bf16)
        du_ref[...] = (dact * u_ref[...].astype(f32)).astype(bf16)

    a_spec = pl.BlockSpec((tm, k), lambda i, j: (i, 0))
    w_spec = pl.BlockSpec((tn, k), lambda i, j: (j, 0))
    o_spec = pl.BlockSpec((tm, tn), lambda i, j: (i, j))
    return pl.pallas_call(
        body, name=name, grid=(m // tm, n // tn), in_specs=[a_spec, w_spec, o_spec, o_spec], out_specs=[o_spec, o_spec],
        out_shape=[jax.ShapeDtypeStruct((m, n), bf16), jax.ShapeDtypeStruct((m, n), bf16)],
        compiler_params=_cparams(("parallel", "parallel")),
    )(dh, w_down, g, u)


def _rmsnorm_fwd(h, w, name, tm=1024):
    m, d = h.shape

    def body(h_ref, w_ref, o_ref):
        x = h_ref[...]
        r = lax.rsqrt(jnp.mean(x * x, axis=-1, keepdims=True) + EPS)
        o_ref[...] = (x * r * w_ref[...]).astype(bf16)

    return pl.pallas_call(
        body, name=name, grid=(m // tm,),
        in_specs=[pl.BlockSpec((tm, d), lambda i: (i, 0)), pl.BlockSpec((1, d), lambda i: (0, 0))],
        out_specs=pl.BlockSpec((tm, d), lambda i: (i, 0)), out_shape=jax.ShapeDtypeStruct((m, d), bf16),
        compiler_params=_cparams(("parallel",)),
    )(h, w)


def _nt_norm_bwd(pairs, h, w, dres, name, tk, b_is_kd=False, bf16_copy=True, tm=1024, vmem=None):
    m, d = h.shape
    contract = _nn if b_is_kd else _nt
    steps = [p[0].shape[1] // tk for p in pairs]
    assert all(p[0].shape[1] % tk == 0 for p in pairs), (name, tk)
    starts = [sum(steps[:i]) for i in range(len(pairs))]
    nk = sum(steps)
    n_p = len(pairs)

    def body(*refs):
        ab = refs[:2 * n_p]
        h_ref, w_ref, dres_ref, dh_ref = refs[2 * n_p:2 * n_p + 4]
        dhb_ref = refs[2 * n_p + 4] if bf16_copy else None
        dw_ref, acc_ref = refs[-2:]
        i, k = pl.program_id(0), pl.program_id(1)

        @pl.when(k == 0)
        def _():
            acc_ref[...] = jnp.zeros_like(acc_ref)

        for p in range(n_p):
            @pl.when((k >= starts[p]) & (k < starts[p] + steps[p]))
            def _(p=p):
                acc_ref[...] += contract(ab[2 * p][...], ab[2 * p + 1][...])

        @pl.when(k == nk - 1)
        def _():
            x = h_ref[...]
            r = lax.rsqrt(jnp.mean(x * x, axis=-1, keepdims=True) + EPS)
            xhat = x * r
            dy = acc_ref[...]
            gw = dy * w_ref[...]
            dh = dres_ref[...] + r * (gw - xhat * jnp.mean(gw * xhat, axis=-1, keepdims=True))
            dh_ref[...] = dh
            if bf16_copy:
                dhb_ref[...] = dh.astype(bf16)
            part = _rowsum8(dy * xhat)

            @pl.when(i == 0)
            def _():
                dw_ref[...] = part

            @pl.when(i > 0)
            def _():
                dw_ref[...] += part

    def clamp(k, p):
        return jnp.clip(k - starts[p], 0, steps[p] - 1)

    in_specs = []
    for p in range(n_p):
        b_spec = (pl.BlockSpec((tk, d), lambda i, k, p=p: (clamp(k, p), 0)) if b_is_kd
                  else pl.BlockSpec((d, tk), lambda i, k, p=p: (0, clamp(k, p))))
        in_specs += [pl.BlockSpec((tm, tk), lambda i, k, p=p: (i, clamp(k, p))), b_spec]
    row = pl.BlockSpec((tm, d), lambda i, k: (i, 0))
    in_specs += [row, pl.BlockSpec((1, d), lambda i, k: (0, 0)), row]
    return pl.pallas_call(
        body, name=name, grid=(m // tm, nk), in_specs=in_specs,
        out_specs=[row] + [row] * bf16_copy + [pl.BlockSpec((SUBLANES, d), lambda i, k: (0, 0))],
        out_shape=[jax.ShapeDtypeStruct((m, d), f32)] + [jax.ShapeDtypeStruct((m, d), bf16)] * bf16_copy + [jax.ShapeDtypeStruct((SUBLANES, d), f32)],
        scratch_shapes=[pltpu.VMEM((tm, d), f32)], compiler_params=_cparams(("arbitrary", "arbitrary"), vmem),
    )(*[t for p in pairs for t in p], h, w, dres)


def _final_loss(h, w, target, name, tm=1024):
    m, d = h.shape

    def body(h_ref, w_ref, t_ref, dh_ref, dhb_ref, loss_ref, dw_ref):
        x = h_ref[...]
        r = lax.rsqrt(jnp.mean(x * x, axis=-1, keepdims=True) + EPS)
        xhat = x * r
        ww = w_ref[...]
        err = xhat * ww - t_ref[...]
        dy = err * (1.0 / d)
        gw = dy * ww
        dh = r * (gw - xhat * jnp.mean(gw * xhat, axis=-1, keepdims=True))
        dh_ref[...] = dh
        dhb_ref[...] = dh.astype(bf16)
        lpart = _rowsum8(err * err) * (0.5 / d)
        wpart = _rowsum8(dy * xhat)

        @pl.when(pl.program_id(0) == 0)
        def _():
            loss_ref[...] = lpart
            dw_ref[...] = wpart

        @pl.when(pl.program_id(0) > 0)
        def _():
            loss_ref[...] += lpart
            dw_ref[...] += wpart

    row = pl.BlockSpec((tm, d), lambda i: (i, 0))
    acc = pl.BlockSpec((SUBLANES, d), lambda i: (0, 0))
    return pl.pallas_call(
        body, name=name, grid=(m // tm,),
        in_specs=[row, pl.BlockSpec((1, d), lambda i: (0, 0)), row], out_specs=[row, row, acc, acc],
        out_shape=[jax.ShapeDtypeStruct((m, d), f32), jax.ShapeDtypeStruct((m, d), bf16),
                   jax.ShapeDtypeStruct((SUBLANES, d), f32), jax.ShapeDtypeStruct((SUBLANES, d), f32)],
        compiler_params=_cparams(("arbitrary",)),
    )(h, w, target)


def _lane_tables():
    f = np.arange(LANES) % HEAD_DIM
    inv = ROPE_THETA ** (-jnp.arange(0, ROPE_DIM, 2, dtype=f32) / ROPE_DIM)
    invf = jnp.where(f < ROPE_DIM, inv[f % (ROPE_DIM // 2)], 0.0).astype(f32)
    return invf.reshape(1, LANES)


def _rope_tables(pos_col, name):
    t = pos_col.shape[0]
    tm = SEQ

    def body(p_ref, f_ref, c_ref, s1_ref, s2_ref):
        ang = p_ref[...].astype(f32) * f_ref[...]
        co, si = jnp.cos(ang), jnp.sin(ang)
        f = lax.broadcasted_iota(jnp.int32, (tm, LANES), 1) % HEAD_DIM
        c_ref[...] = jnp.where(f < ROPE_DIM, co, 1.0)
        s1_ref[...] = jnp.where(f < ROPE_DIM // 2, -si, 0.0)
        s2_ref[...] = jnp.where((f >= ROPE_DIM // 2) & (f < ROPE_DIM), si, 0.0)

    row = pl.BlockSpec((tm, LANES), lambda i: (i, 0))
    return pl.pallas_call(
        body, name=name, grid=(t // tm,),
        in_specs=[pl.BlockSpec((tm, 1), lambda i: (i, 0)), pl.BlockSpec((1, LANES), lambda i: (0, 0))],
        out_specs=[row, row, row], out_shape=[jax.ShapeDtypeStruct((t, LANES), f32)] * 3,
        compiler_params=_cparams(("parallel",)),
    )(pos_col, _lane_tables())


def _rot(x, c, s1, s2):
    return x * c + pltpu.roll(x, LANES - ROPE_DIM // 2, 1) * s1 + pltpu.roll(x, ROPE_DIM // 2, 1) * s2


def _rot_t(g, c, s1, s2):
    return g * c + pltpu.roll(g * s1, ROPE_DIM // 2, 1) + pltpu.roll(g * s2, LANES - ROPE_DIM // 2, 1)


def _dup_head(x, kvh, low):
    a = jnp.where(kvh == 0, x, pltpu.roll(x, HEAD_DIM, 1))
    return jnp.where(low, a, pltpu.roll(a, HEAD_DIM, 1))


def _deinterleave(src_ref, dst_ref, d, dtype):
    length = SEQ // d
    if d == 1:
        dst_ref[...] = src_ref[...].astype(dtype)
    else:
        for r in range(d):
            dst_ref[pl.ds(r * length, length), :] = src_ref[pl.ds(r, length, stride=d), :].astype(dtype)


def _interleave_store(src_ref, dst_ref, d, accumulate):
    length = SEQ // d
    if d == 1:
        if accumulate:
            dst_ref[...] += src_ref[...]
        else:
            dst_ref[...] = src_ref[...]
    else:
        for r in range(d):
            blk = src_ref[pl.ds(r * length, length), :]
            if accumulate:
                dst_ref[pl.ds(r, length, stride=d), :] = dst_ref[pl.ds(r, length, stride=d), :] + blk
            else:
                dst_ref[pl.ds(r, length, stride=d), :] = blk


def _attn_masks():
    qi = lax.broadcasted_iota(jnp.int32, (ATTN_BLOCK, ATTN_BLOCK), 0)
    ki = lax.broadcasted_iota(jnp.int32, (ATTN_BLOCK, ATTN_BLOCK), 1)
    low = lax.broadcasted_iota(jnp.int32, (ATTN_BLOCK, LANES), 1) < HEAD_DIM
    return ki <= qi, ki >= qi, low


NEG_INF = float("-inf")


N_BRANCH = len(DILATIONS)


def _attn_prep(qkv, tabs, name):
    t = qkv.shape[0]
    nb = t // SEQ
    n_j = ATTN_WIDTH // LANES

    def q_body(q_ref, c_ref, s1_ref, s2_ref, out_ref, xr):
        xr[...] = _rot(q_ref[...], c_ref[...], s1_ref[...], s2_ref[...]) * (HEAD_DIM ** -0.5)
        for bi, d in enumerate(DILATIONS):
            _deinterleave(xr, out_ref.at[bi], d, bf16)

    def kv_body(x_ref, c_ref, s1_ref, s2_ref, out_ref, xr):
        lowfull = lax.broadcasted_iota(jnp.int32, (SEQ, LANES), 1) < HEAD_DIM
        x = x_ref[...]
        x = jnp.where(pl.program_id(1) == 0, _rot(x, c_ref[...], s1_ref[...], s2_ref[...]), x)
        for kvh in range(N_KV_HEADS):
            xr[...] = _dup_head(x, kvh, lowfull)
            for bi, d in enumerate(DILATIONS):
                length = SEQ // d
                for r in range(d):
                    rows = xr[...] if d == 1 else xr[pl.ds(r, length, stride=d), :]
                    out_ref[0, bi, pl.ds(r * length, length), kvh * LANES:(kvh + 1) * LANES] = rows.astype(bf16)

    tab = pl.BlockSpec((SEQ, LANES), lambda b, j: (b, 0))
    q = pl.pallas_call(
        q_body, name=name + "_q", grid=(nb, n_j),
        in_specs=[pl.BlockSpec((SEQ, LANES), lambda b, j: (b, j)), tab, tab, tab],
        out_specs=pl.BlockSpec((N_BRANCH, SEQ, LANES), lambda b, j: (0, b, j)),
        out_shape=jax.ShapeDtypeStruct((N_BRANCH, t, ATTN_WIDTH), bf16), scratch_shapes=[pltpu.VMEM((SEQ, LANES), f32)],
        compiler_params=_cparams(("parallel", "parallel")),
    )(qkv, *tabs)
    kv = pl.pallas_call(
        kv_body, name=name + "_kv", grid=(nb, 2),
        in_specs=[pl.BlockSpec((SEQ, LANES), lambda b, j: (b, n_j + j)), tab, tab, tab],
        out_specs=pl.BlockSpec((1, N_BRANCH, SEQ, N_KV_HEADS * LANES), lambda b, j: (j, 0, b, 0)),
        out_shape=jax.ShapeDtypeStruct((2, N_BRANCH, t, N_KV_HEADS * LANES), bf16), scratch_shapes=[pltpu.VMEM((SEQ, LANES), f32)],
        compiler_params=_cparams(("parallel", "parallel")),
    )(qkv, *tabs)
    return q, kv


def _attn_fwd(prep, name):
    q_all, kv_all = prep
    t = q_all.shape[1]
    nb = t // SEQ
    n_blk = SEQ // ATTN_BLOCK

    def body(q_ref, k_ref, v_ref, o_ref, o16_ref, lse_ref, ob, lb, o0, o1, o2, l0, l1, l2, ss):
        cur_ok, prev_ok, low = _attn_masks()
        onat, lnat = (o0, o1, o2), (l0, l1, l2)
        for bi, d in enumerate(DILATIONS):
            qd, kd, vd = q_ref.at[bi], k_ref.at[0, bi], v_ref.at[0, bi]
            per_res = n_blk // d

            def scores(n):
                cur, prev = pl.ds(n * ATTN_BLOCK, ATTN_BLOCK), pl.ds(max(n - 1, 0) * ATTN_BLOCK, ATTN_BLOCK)
                has_prev = n % per_res != 0
                qb = qd[cur, :]
                kc = kd[cur, :]
                if has_prev:
                    kp = kd[prev, :]
                for a in range(2):
                    qa = jnp.where(low if a == 0 else ~low, qb, jnp.zeros_like(qb))
                    ss[2 * n + a, :, 0:ATTN_BLOCK] = jnp.where(cur_ok, _nt(qa, kc), NEG_INF)
                    if has_prev:
                        ss[2 * n + a, :, ATTN_BLOCK:2 * ATTN_BLOCK] = jnp.where(prev_ok, _nt(qa, kp), NEG_INF)

            def softmax_pv(n):
                cur, prev = pl.ds(n * ATTN_BLOCK, ATTN_BLOCK), pl.ds(max(n - 1, 0) * ATTN_BLOCK, ATTN_BLOCK)
                has_prev = n % per_res != 0
                vc = vd[cur, :]
                if has_prev:
                    vp = vd[prev, :]
                outs, lses = [], []
                for a in range(2):
                    sc = ss[2 * n + a, :, 0:ATTN_BLOCK]
                    if has_prev:
                        sp = ss[2 * n + a, :, ATTN_BLOCK:2 * ATTN_BLOCK]
                        m = jnp.max(jnp.maximum(sc, sp), axis=1, keepdims=True)
                        pc, pp = jnp.exp(sc - m), jnp.exp(sp - m)
                        den = jnp.sum(pc + pp, axis=1, keepdims=True)
                        acc = _nn(pc.astype(bf16), vc) + _nn(pp.astype(bf16), vp)
                    else:
                        m = jnp.max(sc, axis=1, keepdims=True)
                        pc = jnp.exp(sc - m)
                        den = jnp.sum(pc, axis=1, keepdims=True)
                        acc = _nn(pc.astype(bf16), vc)
                    outs.append(acc * (1.0 / den))
                    lses.append(m + jnp.log(den))
                ob[cur, :] = jnp.where(low, outs[0], outs[1])
                lb[cur, :] = jnp.where(low, lses[0], lses[1])

            for n in range(n_blk):
                scores(n)
            for n in range(n_blk):
                softmax_pv(n)
            _interleave_store(ob, onat[bi], d, False)
            _interleave_store(lb, lnat[bi], d, False)
        la, lbb, lc = l0[...], l1[...], l2[...]
        lm = jnp.maximum(jnp.maximum(la, lbb), lc)
        wa, wb, wc = jnp.exp(la - lm), jnp.exp(lbb - lm), jnp.exp(lc - lm)
        ws = wa + wb + wc
        o = (wa * o0[...] + wb * o1[...] + wc * o2[...]) / ws
        o_ref[...] = o
        o16_ref[...] = o.astype(bf16)
        lse_ref[...] = lm + jnp.log(ws)

    def col(jj):
        return pl.BlockSpec((SEQ, LANES), lambda b, j: (b, jj if jj is not None else j))

    fs = pltpu.VMEM((SEQ, LANES), f32)
    return pl.pallas_call(
        body, name=name, grid=(nb, ATTN_WIDTH // LANES),
        in_specs=[pl.BlockSpec((N_BRANCH, SEQ, LANES), lambda b, j: (0, b, j)),
                  pl.BlockSpec((1, N_BRANCH, SEQ, LANES), lambda b, j: (0, 0, b, j // 2)),
                  pl.BlockSpec((1, N_BRANCH, SEQ, LANES), lambda b, j: (1, 0, b, j // 2))],
        out_specs=[col(None), col(None), col(None)],
        out_shape=[jax.ShapeDtypeStruct((t, ATTN_WIDTH), f32), jax.ShapeDtypeStruct((t, ATTN_WIDTH), bf16), jax.ShapeDtypeStruct((t, ATTN_WIDTH), f32)],
        scratch_shapes=[fs, fs, fs, fs, fs, fs, fs, fs, pltpu.VMEM((2 * n_blk, ATTN_BLOCK, 2 * ATTN_BLOCK), f32)],
        compiler_params=_cparams(("parallel", "parallel")),
    )(q_all, kv_all, kv_all)


def _attn_bwd(prep, tabs, o, lse, do, name):
    q_all, kv_all = prep
    t = q_all.shape[1]
    nb = t // SEQ
    n_blk = SEQ // ATTN_BLOCK
    n_j = ATTN_WIDTH // LANES

    def body(q_ref, k_ref, v_ref, c_ref, s1_ref, s2_ref, o_ref, lse_ref, do_ref, dq_ref, dk_ref, dv_ref,
             stat, dod, std, dqd, dkd, dvd, dqa, dka, dva, pb, dsb, dk_acc, dv_acc):
        j = pl.program_id(1)
        kvh = j // 2
        cur_ok, prev_ok, low = _attn_masks()
        lane = lax.broadcasted_iota(jnp.int32, (SEQ, LANES), 1)
        lowfull = lane < HEAD_DIM
        c, s1, s2 = c_ref[...], s1_ref[...], s2_ref[...]
        prod = do_ref[...] * o_ref[...]
        d_lo = jnp.sum(jnp.where(lowfull, prod, 0.0), axis=1, keepdims=True)
        d_hi = jnp.sum(jnp.where(lowfull, 0.0, prod), axis=1, keepdims=True)
        stat[...] = jnp.where(lane % HEAD_DIM < HEAD_DIM // 2, lse_ref[...], jnp.where(lowfull, d_lo, d_hi))
        dqa[...] = jnp.zeros_like(dqa)
        dka[...] = jnp.zeros_like(dka)
        dva[...] = jnp.zeros_like(dva)
        for bi, d in enumerate(DILATIONS):
            qd, kd, vd = q_ref.at[bi], k_ref.at[0, bi], v_ref.at[0, bi]
            _deinterleave(do_ref, dod, d, bf16)
            _deinterleave(stat, std, d, f32)
            per_res = n_blk // d
            curl, prevl = slice(0, ATTN_BLOCK), slice(ATTN_BLOCK, 2 * ATTN_BLOCK)

            def halves(x):
                zero = jnp.zeros_like(x)
                return jnp.where(low, x, zero), jnp.where(low, zero, x)

            def blk(n):
                return pl.ds(n * ATTN_BLOCK, ATTN_BLOCK)

            def has_prev(n):
                return n < n_blk and n % per_res != 0

            def probs(n):
                cur = blk(n)
                qas, doas = halves(qd[cur, :]), halves(dod[cur, :])
                kc, vc = kd[cur, :], vd[cur, :]
                if has_prev(n):
                    kp, vp = kd[blk(n - 1), :], vd[blk(n - 1), :]
                stb = std[cur, :]
                for a in range(2):
                    ls = stb[:, a * HEAD_DIM:a * HEAD_DIM + 1]
                    de = stb[:, a * HEAD_DIM + HEAD_DIM // 2:a * HEAD_DIM + HEAD_DIM // 2 + 1]
                    pc = jnp.exp(jnp.where(cur_ok, _nt(qas[a], kc), NEG_INF) - ls)
                    pb[2 * n + a, :, curl] = pc.astype(bf16)
                    dsb[2 * n + a, :, curl] = (pc * (_nt(doas[a], vc) - de)).astype(bf16)
                    if has_prev(n):
                        pp = jnp.exp(jnp.where(prev_ok, _nt(qas[a], kp), NEG_INF) - ls)
                        pb[2 * n + a, :, prevl] = pp.astype(bf16)
                        dsb[2 * n + a, :, prevl] = (pp * (_nt(doas[a], vp) - de)).astype(bf16)

            def grads(n):
                cur = blk(n)
                kc = kd[cur, :]
                dqs = [_nn(dsb[2 * n + a, :, curl], kc) for a in range(2)]
                q_rows, do_rows = list(halves(qd[cur, :])), list(halves(dod[cur, :]))
                ds_rows, p_rows = [dsb[2 * n + a, :, curl] for a in range(2)], [pb[2 * n + a, :, curl] for a in range(2)]
                if has_prev(n):
                    kp = kd[blk(n - 1), :]
                    dqs = [dqs[a] + _nn(dsb[2 * n + a, :, prevl], kp) for a in range(2)]
                if has_prev(n + 1):
                    q_rows += list(halves(qd[blk(n + 1), :]))
                    do_rows += list(halves(dod[blk(n + 1), :]))
                    ds_rows += [dsb[2 * n + 2 + a, :, prevl] for a in range(2)]
                    p_rows += [pb[2 * n + 2 + a, :, prevl] for a in range(2)]
                dqd[cur, :] = jnp.where(low, dqs[0], dqs[1])
                dkd[cur, :] = _tn(jnp.concatenate(ds_rows, axis=0), jnp.concatenate(q_rows, axis=0))
                dvd[cur, :] = _tn(jnp.concatenate(p_rows, axis=0), jnp.concatenate(do_rows, axis=0))

            for n in range(n_blk):
                probs(n)
            for n in range(n_blk):
                grads(n)
            _interleave_store(dqd, dqa, d, True)
            _interleave_store(dkd, dka, d, True)
            _interleave_store(dvd, dva, d, True)
        dq_ref[...] = _rot_t(dqa[...] * (HEAD_DIM ** -0.5), c, s1, s2).astype(bf16)
        dkf = dka[...]
        dkf = _rot_t(dkf + pltpu.roll(dkf, HEAD_DIM, 1), c, s1, s2)
        dvf = dva[...]
        dvf = dvf + pltpu.roll(dvf, HEAD_DIM, 1)
        mine = (lax.broadcasted_iota(jnp.int32, (SEQ, LANES), 1) // HEAD_DIM) == kvh
        dkc_, dvc_ = jnp.where(mine, dkf, 0.0), jnp.where(mine, dvf, 0.0)

        @pl.when(j == 0)
        def _():
            dk_acc[...] = dkc_
            dv_acc[...] = dvc_

        @pl.when(j > 0)
        def _():
            dk_acc[...] += dkc_
            dv_acc[...] += dvc_

        @pl.when(j == n_j - 1)
        def _():
            dk_ref[...] = dk_acc[...].astype(bf16)
            dv_ref[...] = dv_acc[...].astype(bf16)

    def col(jj):
        return pl.BlockSpec((SEQ, LANES), lambda b, j: (b, jj if jj is not None else j))

    tab = pl.BlockSpec((SEQ, LANES), lambda b, j: (b, 0))
    fs = pltpu.VMEM((SEQ, LANES), f32)
    hs = pltpu.VMEM((SEQ, LANES), bf16)
    return pl.pallas_call(
        body, name=name, grid=(nb, n_j),
        in_specs=[pl.BlockSpec((N_BRANCH, SEQ, LANES), lambda b, j: (0, b, j)),
                  pl.BlockSpec((1, N_BRANCH, SEQ, LANES), lambda b, j: (0, 0, b, j // 2)),
                  pl.BlockSpec((1, N_BRANCH, SEQ, LANES), lambda b, j: (1, 0, b, j // 2)),
                  tab, tab, tab, col(None), col(None), col(None)],
        out_specs=[col(None), tab, tab],
        out_shape=[jax.ShapeDtypeStruct((t, ATTN_WIDTH), bf16), jax.ShapeDtypeStruct((t, LANES), bf16), jax.ShapeDtypeStruct((t, LANES), bf16)],
        scratch_shapes=[fs, hs, fs, fs, fs, fs, fs, fs, fs,
                        pltpu.VMEM((2 * n_blk, ATTN_BLOCK, 2 * ATTN_BLOCK), bf16), pltpu.VMEM((2 * n_blk, ATTN_BLOCK, 2 * ATTN_BLOCK), bf16), fs, fs],
        compiler_params=_cparams(("parallel", "arbitrary")),
    )(q_all, kv_all, kv_all, *tabs, o, lse, do)


def _tap(w_ref, s):
    return w_ref[CONV_WIDTH - 1 - s:CONV_WIDTH - s, :]


def _conv_pre(x, w_ref, b_ref, row):
    shifted = [x] + [jnp.where(row >= s, pltpu.roll(x, s, 0), 0.0) for s in range(1, CONV_WIDTH)]
    pre = b_ref[...] + _tap(w_ref, 0) * x
    for s in range(1, CONV_WIDTH):
        pre = pre + _tap(w_ref, s) * shifted[s]
    return pre, shifted


def _conv_fwd(x, w, b, name, tc=512):
    t, ch = x.shape

    def body(x_ref, w_ref, b_ref, o_ref):
        row = lax.broadcasted_iota(jnp.int32, (SEQ, tc), 0)
        pre, _ = _conv_pre(x_ref[...], w_ref, b_ref, row)
        o_ref[...] = _silu(pre)

    xs = pl.BlockSpec((SEQ, tc), lambda i, j: (i, j))
    return pl.pallas_call(
        body, name=name, grid=(t // SEQ, ch // tc),
        in_specs=[xs, pl.BlockSpec((CONV_WIDTH, tc), lambda i, j: (0, j)), pl.BlockSpec((1, tc), lambda i, j: (0, j))],
        out_specs=xs, out_shape=jax.ShapeDtypeStruct((t, ch), f32),
        compiler_params=_cparams(("parallel", "parallel")),
    )(x, w, b)


def _conv_bwd(x, w, b, dact, name, tc=512):
    t, ch = x.shape

    def body(x_ref, w_ref, b_ref, d_ref, dx_ref, dw_ref, db_ref):
        row = lax.broadcasted_iota(jnp.int32, (SEQ, tc), 0)
        pre, shifted = _conv_pre(x_ref[...], w_ref, b_ref, row)
        dpre = d_ref[...] * _dsilu(pre)
        dx = _tap(w_ref, 0) * dpre
        for s in range(1, CONV_WIDTH):
            dx = dx + _tap(w_ref, s) * jnp.where(row < SEQ - s, pltpu.roll(dpre, SEQ - s, 0), 0.0)
        dx_ref[...] = dx.astype(bf16)
        first = pl.program_id(1) == 0
        parts = [jnp.sum(dpre * shifted[CONV_WIDTH - 1 - k], axis=0, keepdims=True) for k in range(CONV_WIDTH)]
        dbp = jnp.sum(dpre, axis=0, keepdims=True)

        @pl.when(first)
        def _():
            for k in range(CONV_WIDTH):
                dw_ref[k:k + 1, :] = parts[k]
            db_ref[...] = dbp

        @pl.when(jnp.logical_not(first))
        def _():
            for k in range(CONV_WIDTH):
                dw_ref[k:k + 1, :] += parts[k]
            db_ref[...] += dbp

    xs = pl.BlockSpec((SEQ, tc), lambda j, i: (i, j))
    ws = pl.BlockSpec((CONV_WIDTH, tc), lambda j, i: (0, j))
    bs = pl.BlockSpec((1, tc), lambda j, i: (0, j))
    return pl.pallas_call(
        body, name=name, grid=(ch // tc, t // SEQ),
        in_specs=[xs, ws, bs, xs], out_specs=[xs, ws, bs],
        out_shape=[jax.ShapeDtypeStruct((t, ch), bf16), jax.ShapeDtypeStruct((CONV_WIDTH, ch), f32), jax.ShapeDtypeStruct((1, ch), f32)],
        compiler_params=_cparams(("parallel", "arbitrary")),
    )(x, w, b, dact)


GROUP_W = SSM_INNER // SSM_GROUPS
HEADS_PER_GROUP = SSM_HEADS // SSM_GROUPS
SSD_CHUNKS_PER_STEP = 4


def _split3(x):
    hi = x.astype(bf16)
    r1 = x - hi.astype(f32)
    mid = r1.astype(bf16)
    lo = (r1 - mid.astype(f32)).astype(bf16)
    return hi, mid, lo


def _dot_exact(x, sel, dims, x_is_lhs=True):
    parts = _split3(x)
    if x_is_lhs:
        return _dot(parts[0], sel, dims) + _dot(parts[1], sel, dims) + _dot(parts[2], sel, dims)
    return _dot(sel, parts[0], dims) + _dot(sel, parts[1], dims) + _dot(sel, parts[2], dims)


def _ssd_common(xbc_ref, dt_ref, bias_ref, alog_ref):
    r = lax.broadcasted_iota(jnp.int32, (CHUNK, CHUNK), 0)
    cidx = lax.broadcasted_iota(jnp.int32, (CHUNK, CHUNK), 1)
    causal = r >= cidx
    tril = causal.astype(bf16)
    expand = (lax.broadcasted_iota(jnp.int32, (CHUNK, SSM_INNER), 0)
              == lax.broadcasted_iota(jnp.int32, (CHUNK, SSM_INNER), 1) // HEAD_DIM).astype(bf16)
    head_lane = cidx < SSM_HEADS
    dtp = dt_ref[...] + bias_ref[...]
    dt = jnp.where(head_lane, _softplus(dtp), 0.0)
    a_neg = -jnp.exp(alog_ref[...])
    a = dt * a_neg
    nn_dims = ((1,), (0,))
    cs = _dot_exact(a, tril, nn_dims, x_is_lhs=False)
    dt_e = _dot_exact(dt, expand, nn_dims)
    cs_e = _dot_exact(cs, expand, nn_dims)
    xs = xbc_ref[:, 0:SSM_INNER]
    xg = xs * dt_e
    ecs = jnp.exp(cs_e)
    cs_last = cs_e[CHUNK - 1:CHUNK, :]
    dse = jnp.exp(cs_last - cs_e)
    cde = jnp.exp(cs_last)
    return dict(r=r, cidx=cidx, causal=causal, tril=tril, expand=expand, head_lane=head_lane, dtp=dtp, dt=dt, a_neg=a_neg,
                cs=cs, cst=cs.T, dt_e=dt_e, cs_e=cs_e, xs=xs, xg=xg, ecs=ecs, dse=dse, cde=cde)


def _decay_mat(q, h):
    return jnp.exp(jnp.where(q["causal"], q["cs"][:, h:h + 1] - q["cst"][h:h + 1, :], NEG_INF))


def _gate_norm(y, z, nw, gate=None):
    y2 = y * (_silu(z) if gate is None else gate)
    outs, xhats, rs = [], [], []
    for g in range(SSM_GROUPS):
        sl = slice(g * GROUP_W, (g + 1) * GROUP_W)
        yg = y2[:, sl]
        r = lax.rsqrt(jnp.mean(yg * yg, axis=-1, keepdims=True) + EPS)
        xhats.append(yg * r)
        rs.append(r)
        outs.append(yg * r * nw[:, sl])
    return y2, outs, xhats, rs


def _ssd_fwd(xbc, z, dtp, params, name):
    t = xbc.shape[0]
    n_chunk = SEQ // CHUNK
    n_step = n_chunk // SSD_CHUNKS_PER_STEP

    def body(xbc_ref, z_ref, dt_ref, bias_ref, alog_ref, dskip_ref, nw_ref, yn_ref, y_ref, hs_ref, h_scr):
        @pl.when(pl.program_id(1) == 0)
        def _():
            h_scr[...] = jnp.zeros_like(h_scr)

        for s in range(SSD_CHUNKS_PER_STEP):
            r = pl.ds(s * CHUNK, CHUNK)
            one_chunk(xbc_ref.at[r], z_ref.at[r], dt_ref.at[r], bias_ref, alog_ref, dskip_ref, nw_ref,
                      yn_ref.at[r], y_ref.at[r], hs_ref.at[pl.ds(s, 1)], h_scr)

    def one_chunk(xbc_ref, z_ref, dt_ref, bias_ref, alog_ref, dskip_ref, nw_ref, yn_ref, y_ref, hs_ref, h_scr):
        q = _ssd_common(xbc_ref, dt_ref, bias_ref, alog_ref)
        low = lax.broadcasted_iota(jnp.int32, (CHUNK, LANES), 1) < HEAD_DIM
        xgb = q["xg"].astype(bf16)
        wst = (q["xg"] * q["dse"]).astype(bf16)
        hs_ref[0] = h_scr[...]
        ys = []
        for g in range(SSM_GROUPS):
            gl = slice(g * GROUP_W, (g + 1) * GROUP_W)
            bg = xbc_ref[:, SSM_INNER + g * D_STATE:SSM_INNER + (g + 1) * D_STATE].astype(bf16)
            cg = xbc_ref[:, SSM_INNER + SSM_GROUPS * D_STATE + g * D_STATE:SSM_INNER + SSM_GROUPS * D_STATE + (g + 1) * D_STATE].astype(bf16)
            cb = _nt(cg, bg)
            hg = h_scr[g]
            yoff = _nn(cg, hg.astype(bf16)) * q["ecs"][:, gl]
            pieces = []
            for i in range(HEADS_PER_GROUP // 2):
                h0 = g * HEADS_PER_GROUP + 2 * i
                xp = xgb[:, h0 * HEAD_DIM:(h0 + 2) * HEAD_DIM]
                m0 = (cb * _decay_mat(q, h0)).astype(bf16)
                m1 = (cb * _decay_mat(q, h0 + 1)).astype(bf16)
                zero = jnp.zeros_like(xp)
                pieces.append(_nn(m0, jnp.where(low, xp, zero)) + _nn(m1, jnp.where(low, zero, xp)))
            ys.append(jnp.concatenate(pieces, axis=1) + yoff + dskip_ref[:, gl] * q["xs"][:, gl])
            h_scr[g] = hg * q["cde"][:, gl] + _tn(bg, wst[:, gl])
        y = jnp.concatenate(ys, axis=1)
        y_ref[...] = y
        _, outs, _, _ = _gate_norm(y, z_ref[...], nw_ref[...])
        yn_ref[...] = jnp.concatenate(outs, axis=1).astype(bf16)

    def rows(w):
        return pl.BlockSpec((SSD_CHUNKS_PER_STEP * CHUNK, w), lambda b, c: (b * n_step + c, 0))

    def par(w):
        return pl.BlockSpec((1, w), lambda b, c: (0, 0))

    return pl.pallas_call(
        body, name=name, grid=(t // SEQ, n_step),
        in_specs=[rows(CONV_CH), rows(SSM_INNER), rows(LANES), par(LANES), par(LANES), par(SSM_INNER), par(SSM_INNER)],
        out_specs=[rows(SSM_INNER), rows(SSM_INNER),
                   pl.BlockSpec((SSD_CHUNKS_PER_STEP, SSM_GROUPS, D_STATE, GROUP_W), lambda b, c: (b * n_step + c, 0, 0, 0))],
        out_shape=[jax.ShapeDtypeStruct((t, SSM_INNER), bf16), jax.ShapeDtypeStruct((t, SSM_INNER), f32),
                   jax.ShapeDtypeStruct((t // CHUNK, SSM_GROUPS, D_STATE, GROUP_W), f32)],
        scratch_shapes=[pltpu.VMEM((SSM_GROUPS, D_STATE, GROUP_W), f32)],
        compiler_params=_cparams(("parallel", "arbitrary")),
    )(xbc, z, dtp, *params)


def _ssd_bwd(xbc, z, dtp, y, hs, dyn, params, name):
    t = xbc.shape[0]
    n_chunk = SEQ // CHUNK
    n_step = n_chunk // SSD_CHUNKS_PER_STEP

    def body(xbc_ref, z_ref, dt_ref, y_ref, hs_ref, dyn_ref, bias_ref, alog_ref, dskip_ref, nw_ref,
             dxbc_ref, dz_ref, ddt_ref, dnw_ref, dds_ref, dal_ref, dbi_ref, dh_scr):
        @pl.when(pl.program_id(1) == 0)
        def _():
            dh_scr[...] = jnp.zeros_like(dh_scr)

        first_step = (pl.program_id(0) == 0) & (pl.program_id(1) == 0)
        for s in reversed(range(SSD_CHUNKS_PER_STEP)):
            r = pl.ds(s * CHUNK, CHUNK)
            one_chunk(xbc_ref.at[r], z_ref.at[r], dt_ref.at[r], y_ref.at[r], hs_ref.at[pl.ds(s, 1)], dyn_ref.at[r],
                      bias_ref, alog_ref, dskip_ref, nw_ref, dxbc_ref.at[r], dz_ref.at[r], ddt_ref.at[r],
                      dnw_ref, dds_ref, dal_ref, dbi_ref, dh_scr, first_step if s == SSD_CHUNKS_PER_STEP - 1 else False)

    def one_chunk(xbc_ref, z_ref, dt_ref, y_ref, hs_ref, dyn_ref, bias_ref, alog_ref, dskip_ref, nw_ref,
                  dxbc_ref, dz_ref, ddt_ref, dnw_ref, dds_ref, dal_ref, dbi_ref, dh_scr, first):
        q = _ssd_common(xbc_ref, dt_ref, bias_ref, alog_ref)
        low = lax.broadcasted_iota(jnp.int32, (CHUNK, LANES), 1) < HEAD_DIM
        last_row = lax.broadcasted_iota(jnp.int32, (CHUNK, GROUP_W), 0) == CHUNK - 1
        xs, xg = q["xs"], q["xg"]
        xgb = xg.astype(bf16)
        wf = xg * q["dse"]
        wst = wf.astype(bf16)
        zz = z_ref[...]
        yy = y_ref[...]
        sz, dsz = _silu_and_grad(zz)
        y2, _, xhats, rs = _gate_norm(yy, zz, nw_ref[...], gate=sz)
        dyn_ = dyn_ref[...]
        dy2s, dnws = [], []
        for g in range(SSM_GROUPS):
            gl = slice(g * GROUP_W, (g + 1) * GROUP_W)
            gw = dyn_[:, gl] * nw_ref[:, gl]
            dy2s.append(rs[g] * (gw - xhats[g] * jnp.mean(gw * xhats[g], axis=-1, keepdims=True)))
            dnws.append(_rowsum8(dyn_[:, gl] * xhats[g]))
        dy2 = jnp.concatenate(dy2s, axis=1)
        dy = dy2 * sz
        dz_ref[...] = (dy2 * yy * dsz).astype(bf16)
        dnw_p = jnp.concatenate(dnws, axis=1)
        dds_p = _rowsum8(dy * xs)
        dyb = dy.astype(bf16)
        gfull = (dy * q["ecs"]).astype(bf16)
        dcs_c = jnp.zeros((CHUNK, CHUNK), f32)
        dcs_r = jnp.zeros((CHUNK, CHUNK), f32)
        dcs_e_parts, dxg_parts = [], []
        for g in range(SSM_GROUPS):
            gl = slice(g * GROUP_W, (g + 1) * GROUP_W)
            bsl = slice(SSM_INNER + g * D_STATE, SSM_INNER + (g + 1) * D_STATE)
            csl = slice(SSM_INNER + SSM_GROUPS * D_STATE + g * D_STATE, SSM_INNER + SSM_GROUPS * D_STATE + (g + 1) * D_STATE)
            bg = xbc_ref[:, bsl].astype(bf16)
            cg = xbc_ref[:, csl].astype(bf16)
            cb = _nt(cg, bg)
            hg = hs_ref[0, g]
            hgb = hg.astype(bf16)
            dhn = dh_scr[g]
            dhnb = dhn.astype(bf16)
            yoff = _nn(cg, hgb) * q["ecs"][:, gl]
            dw_ = _nn(bg, dhnb)
            r_e = dw_ * wf[:, gl]
            to_last = jnp.sum(r_e, axis=0, keepdims=True) + jnp.sum(dhn * hg, axis=0, keepdims=True) * q["cde"][:, gl]
            dcs_e_parts.append(dy[:, gl] * yoff - r_e + jnp.where(last_row, to_last, 0.0))
            dcb = jnp.zeros((CHUNK, CHUNK), f32)
            dxg_pairs = []
            for i in range(HEADS_PER_GROUP // 2):
                h0 = g * HEADS_PER_GROUP + 2 * i
                psl = slice(h0 * HEAD_DIM, (h0 + 2) * HEAD_DIM)
                xp = xgb[:, psl]
                dyp = dyb[:, psl]
                zero = jnp.zeros_like(dyp)
                tns = []
                for a in range(2):
                    h = h0 + a
                    lm = _decay_mat(q, h)
                    m = cb * lm
                    dm = _nt(jnp.where(low, dyp, zero) if a == 0 else jnp.where(low, zero, dyp), xp)
                    dcb = dcb + dm * lm
                    nmat = dm * m
                    dcs_c = dcs_c + jnp.where(q["cidx"] == h, jnp.sum(nmat, axis=1, keepdims=True), 0.0)
                    dcs_r = dcs_r + jnp.where(q["r"] == h, jnp.sum(nmat, axis=0, keepdims=True), 0.0)
                    tns.append(_tn(m.astype(bf16), dyp))
                dxg_pairs.append(jnp.where(low, tns[0], tns[1]))
            dxg_parts.append(jnp.concatenate(dxg_pairs, axis=1) + dw_ * q["dse"][:, gl])
            dcbb = dcb.astype(bf16)
            dxbc_ref[:, csl] = _nt(gfull[:, gl], hgb) + _nn(dcbb, bg)
            dxbc_ref[:, bsl] = _nt(wst[:, gl], dhnb) + _tn(dcbb, cg)
            dh_scr[g] = dhn * q["cde"][:, gl] + _tn(cg, gfull[:, gl])
        dxg = jnp.concatenate(dxg_parts, axis=1)
        dcs_e = jnp.concatenate(dcs_e_parts, axis=1)
        dxbc_ref[:, 0:SSM_INNER] = dskip_ref[...] * dy + dxg * q["dt_e"]
        dcs = dcs_c - dcs_r.T + _dot_exact(dcs_e, q["expand"], ((1,), (1,)))
        triu = (q["cidx"] >= q["r"]).astype(bf16)
        da = _dot_exact(dcs, triu, ((1,), (0,)), x_is_lhs=False)
        ddt = _dot_exact(dxg * xs, q["expand"], ((1,), (1,))) + da * q["a_neg"]
        ddtp = jnp.where(q["head_lane"], ddt * _sigmoid(q["dtp"]), 0.0)
        ddt_ref[...] = ddtp.astype(bf16)
        dal_p = _rowsum8(da * q["dt"]) * q["a_neg"]
        dbi_p = _rowsum8(ddtp)
        def accumulate():
            dnw_ref[...] += dnw_p
            dds_ref[...] += dds_p
            dal_ref[...] += dal_p
            dbi_ref[...] += dbi_p

        if first is False:
            accumulate()
        else:
            @pl.when(first)
            def _():
                dnw_ref[...] = dnw_p
                dds_ref[...] = dds_p
                dal_ref[...] = dal_p
                dbi_ref[...] = dbi_p

            pl.when(jnp.logical_not(first))(accumulate)

    def rows(w):
        return pl.BlockSpec((SSD_CHUNKS_PER_STEP * CHUNK, w), lambda b, c: (b * n_step + n_step - 1 - c, 0))

    def par(w):
        return pl.BlockSpec((1, w), lambda b, c: (0, 0))

    def acc(w):
        return pl.BlockSpec((SUBLANES, w), lambda b, c: (0, 0))

    return pl.pallas_call(
        body, name=name, grid=(t // SEQ, n_step),
        in_specs=[rows(CONV_CH), rows(SSM_INNER), rows(LANES), rows(SSM_INNER),
                  pl.BlockSpec((SSD_CHUNKS_PER_STEP, SSM_GROUPS, D_STATE, GROUP_W), lambda b, c: (b * n_step + n_step - 1 - c, 0, 0, 0)),
                  rows(SSM_INNER), par(LANES), par(LANES), par(SSM_INNER), par(SSM_INNER)],
        out_specs=[rows(CONV_CH), rows(SSM_INNER), rows(LANES), acc(SSM_INNER), acc(SSM_INNER), acc(LANES), acc(LANES)],
        out_shape=[jax.ShapeDtypeStruct((t, CONV_CH), f32), jax.ShapeDtypeStruct((t, SSM_INNER), bf16), jax.ShapeDtypeStruct((t, LANES), bf16),
                   jax.ShapeDtypeStruct((SUBLANES, SSM_INNER), f32), jax.ShapeDtypeStruct((SUBLANES, SSM_INNER), f32),
                   jax.ShapeDtypeStruct((SUBLANES, LANES), f32), jax.ShapeDtypeStruct((SUBLANES, LANES), f32)],
        scratch_shapes=[pltpu.VMEM((SSM_GROUPS, D_STATE, GROUP_W), f32)],
        compiler_params=_cparams(("arbitrary", "arbitrary")),
    )(xbc, z, dtp, y, hs, dyn, *params)


def _adamw_update(g, w, m, v):
    mm = ADAM_B1 * m + (1.0 - ADAM_B1) * g
    vv = ADAM_B2 * v + (1.0 - ADAM_B2) * (g * g)
    m_hat = mm / (1.0 - ADAM_B1 ** ADAM_STEP)
    v_hat = vv / (1.0 - ADAM_B2 ** ADAM_STEP)
    return -ADAM_LR * (m_hat / (jnp.sqrt(v_hat) + ADAM_EPS) + ADAM_WD * w), mm, vv


def _adamw(g_parts, w, m, v, name):
    rows, width = w.shape
    n = len(g_parts)
    tr = _row_tile(rows)

    def body(*refs):
        g_refs, (w_ref, m_ref, v_ref, g_out, d_out, m_out, v_out) = refs[:n], refs[n:]
        g = g_refs[0][...].astype(f32)
        for r in g_refs[1:]:
            g = g + r[...].astype(f32)
        g_out[...] = g
        d_out[...], m_out[...], v_out[...] = _adamw_update(g, w_ref[...], m_ref[...], v_ref[...])

    spec = pl.BlockSpec((tr, width), lambda i: (i, 0))
    return pl.pallas_call(
        body, name=name, grid=(rows // tr,), in_specs=[spec] * (n + 3), out_specs=[spec] * 4,
        out_shape=[jax.ShapeDtypeStruct((rows, width), f32)] * 4, compiler_params=_cparams(("parallel",)),
    )(*g_parts, w, m, v)


def _adamw_layers(landed, w, m, v, after, name, layers_on_columns=False):
    depth = len(landed)
    _, rows, width = landed[0].shape
    tr = _row_tile(rows)
    n_i = rows // tr
    at = (lambda ref: ref) if layers_on_columns else (lambda ref: ref.at[0])

    def body(*refs):
        part_refs, (w_ref, m_ref, v_ref, _, g_out, d_out, m_out, v_out) = refs[:depth * N_DEV], refs[depth * N_DEV:]
        for l in range(depth):
            @pl.when(pl.program_id(0) == l)
            def _(l=l):
                g = part_refs[l * N_DEV][0].astype(f32)
                for r in part_refs[l * N_DEV + 1:(l + 1) * N_DEV]:
                    g = g + r[0].astype(f32)
                at(g_out)[...] = g
                at(d_out)[...], at(m_out)[...], at(v_out)[...] = _adamw_update(g, at(w_ref)[...], at(m_ref)[...], at(v_ref)[...])

    def part_spec(l, p):
        return pl.BlockSpec((1, tr, width), lambda ll, i: (p, jnp.where(ll == l, i, jnp.where(ll < l, 0, n_i - 1)), 0))

    state = (pl.BlockSpec((tr, width), lambda ll, i: (i, ll)) if layers_on_columns
             else pl.BlockSpec((1, tr, width), lambda ll, i: (ll, i, 0)))
    return pl.pallas_call(
        body, name=name, grid=(depth, n_i),
        in_specs=[part_spec(l, p) for l in range(depth) for p in range(N_DEV)] + [state] * 3 + [ANY], out_specs=[state] * 4,
        out_shape=[jax.ShapeDtypeStruct(w.shape, f32)] * 4, compiler_params=_cparams(("arbitrary", "arbitrary")),
    )(*[landed[l] for l in range(depth) for _ in range(N_DEV)], w, m, v, after)


def _row_tile(rows, cap=512):
    for cand in range(min(rows, cap) // SUBLANES * SUBLANES, 0, -SUBLANES):
        if rows % cand == 0:
            return cand
    return rows


def _cols_from_devices(g, width, name):
    n_dev, depth, a, b = g.shape

    def body(g_ref, o_ref):
        for i in range(n_dev):
            o_ref[0, :, i * b:(i + 1) * b] = g_ref[i, 0]
        if width > n_dev * b:
            o_ref[0, :, n_dev * b:width] = jnp.zeros((a, width - n_dev * b), o_ref.dtype)

    return pl.pallas_call(
        body, name=name, grid=(depth,), in_specs=[pl.BlockSpec((n_dev, 1, a, b), lambda l: (0, l, 0, 0))],
        out_specs=pl.BlockSpec((1, a, width), lambda l: (l, 0, 0)), out_shape=jax.ShapeDtypeStruct((depth, a, width), g.dtype),
        compiler_params=_cparams(("parallel",)),
    )(g)


def _devices_from_cols(per_layer, b, name, tr=256):
    depth = len(per_layer)
    a, width = per_layer[0].shape

    def body(*refs):
        o_ref = refs[depth]
        for l in range(depth):
            for i in range(N_DEV):
                o_ref[i, l] = refs[l][:, i * b:(i + 1) * b]

    return pl.pallas_call(
        body, name=name, grid=(a // tr,), in_specs=[pl.BlockSpec((tr, width), lambda r: (r, 0))] * depth,
        out_specs=pl.BlockSpec((N_DEV, depth, tr, b), lambda r: (0, 0, r, 0)),
        out_shape=jax.ShapeDtypeStruct((N_DEV, depth, a, b), per_layer[0].dtype), compiler_params=_cparams(("parallel",)),
    )(*per_layer)


def _me():
    return lax.axis_index("x"), lax.axis_index("y"), lax.axis_index("c")


def _allgather_two_level(shards, name):
    n = len(shards)
    per = 7

    def body(*refs):
        ins, outs, token = refs[:n], refs[n:2 * n], refs[2 * n]
        send_sems, recv_sems, local_sems = refs[2 * n + 1:]
        token[...] = jnp.zeros_like(token)
        x, y, c = _me()
        me, sibling = (x, y, c), (x, y, 1 - c)
        chips = [(1 - x, y), (x, 1 - y), (1 - x, 1 - y)]

        def slot(a, p):
            return outs[a].at[4 * p[0] + 2 * p[1] + p[2]]

        def copy(a, k, block, to, src=None):
            return pltpu.make_async_remote_copy(
                src_ref=slot(a, block) if src is None else src, dst_ref=slot(a, block),
                send_sem=send_sems.at[a * per + k], recv_sem=recv_sems.at[a * per + k], device_id=to, device_id_type=MESH)

        mine = [pltpu.make_async_copy(ins[a], slot(a, me), local_sems.at[a]) for a in range(n)]
        for cp in mine:
            cp.start()
        first = []
        for a in range(n):
            first.append(copy(a, 0, me, sibling, src=ins[a]))
            first += [copy(a, 1 + j, me, (*chip, c), src=ins[a]) for j, chip in enumerate(chips)]
        for cp in first:
            cp.start()
        passed = []
        for j, chip in enumerate(chips):
            for a in range(n):
                copy(a, 1 + j, (*chip, c), me).wait_recv()
                fwd = copy(a, 4 + j, (*chip, c), sibling)
                fwd.start()
                passed.append(fwd)
        for a in range(n):
            copy(a, 0, sibling, me).wait_recv()
            for j, chip in enumerate(chips):
                copy(a, 4 + j, (*chip, 1 - c), me).wait_recv()
        for cp in first + passed:
            cp.wait_send()
        for cp in mine:
            cp.wait()

    outs = pl.pallas_call(
        body, name=name, in_specs=[ANY] * n, out_specs=[ANY] * n + [pl.BlockSpec(memory_space=pltpu.VMEM)],
        out_shape=[jax.ShapeDtypeStruct((N_DEV,) + s.shape, s.dtype) for s in shards] + [jax.ShapeDtypeStruct((SUBLANES, LANES), f32)],
        scratch_shapes=[pltpu.SemaphoreType.DMA((n * per,)), pltpu.SemaphoreType.DMA((n * per,)), pltpu.SemaphoreType.DMA((n,))],
    )(*shards)
    return outs[:n], outs[n]


def _allgather_direct(row, name):
    def body(in_ref, out_ref, send_sems, recv_sems, local_sem):
        x, y, c = _me()
        mine = out_ref.at[4 * x + 2 * y + c]
        local = pltpu.make_async_copy(in_ref, mine, local_sem)
        local.start()
        sends = []
        for k in range(1, N_DEV):
            px, py, pc = x ^ (k >> 2), y ^ ((k >> 1) & 1), c ^ (k & 1)
            sends.append(pltpu.make_async_remote_copy(
                src_ref=in_ref, dst_ref=mine, send_sem=send_sems.at[k - 1], recv_sem=recv_sems.at[k - 1],
                device_id=(px, py, pc), device_id_type=MESH))
        for cp in sends:
            cp.start()
        for k in range(1, N_DEV):
            px, py, pc = x ^ (k >> 2), y ^ ((k >> 1) & 1), c ^ (k & 1)
            theirs = out_ref.at[4 * px + 2 * py + pc]
            pltpu.make_async_remote_copy(
                src_ref=in_ref, dst_ref=theirs, send_sem=send_sems.at[k - 1], recv_sem=recv_sems.at[k - 1],
                device_id=(px, py, pc), device_id_type=MESH).wait_recv()
        for cp in sends:
            cp.wait_send()
        local.wait()

    return pl.pallas_call(
        body, name=name, in_specs=[ANY], out_specs=ANY, out_shape=jax.ShapeDtypeStruct((N_DEV,) + row.shape, row.dtype),
        scratch_shapes=[pltpu.SemaphoreType.DMA((N_DEV - 1,)), pltpu.SemaphoreType.DMA((N_DEV - 1,)), pltpu.SemaphoreType.DMA],
    )(row)


N_CHIP = N_DEV // 2
HBM = pl.BlockSpec(memory_space=pltpu.HBM)
SEM = pl.BlockSpec(memory_space=pltpu.SEMAPHORE)
EFFECT = pltpu.SideEffectType.DATAFLOW_SIDE_EFFECTING


def _peer(k):
    x, y, c = _me()
    return x ^ (k >> 2), y ^ ((k >> 1) & 1), c ^ (k & 1)


def _direct_copies(srcs, lands, send_sems, recv_sems, per_peer):
    x, y, c = _me()
    me = 4 * x + 2 * y + c
    copies = []
    for a in range(len(srcs)):
        for k in range(1, N_DEV):
            px, py, pc = _peer(k)
            piece = srcs[a].at[4 * px + 2 * py + pc] if per_peer else srcs[a]
            copies.append(pltpu.make_async_remote_copy(
                src_ref=piece, dst_ref=lands[a].at[me], send_sem=send_sems.at[a * (N_DEV - 1) + k - 1],
                recv_sem=recv_sems.at[a * (N_DEV - 1) + k - 1], device_id=(px, py, pc), device_id_type=MESH))
    return copies


def _direct_start(srcs, lands, per_peer, name):
    n = len(srcs)
    n_sem = n * (N_DEV - 1)

    def body(*refs):
        src_refs, land_refs = refs[:n], refs[n:2 * n]
        send_sems, recv_sems = refs[2 * n], refs[2 * n + 1]
        token = refs[-1]
        for cp in _direct_copies(src_refs, land_refs, send_sems, recv_sems, per_peer):
            cp.start()
        token[...] = jnp.zeros_like(token)

    outs = pl.pallas_call(
        body, name=name,
        out_shape=(pltpu.SemaphoreType.DMA((n_sem,)), pltpu.SemaphoreType.DMA((n_sem,)),
                   *[pltpu.HBM(s.shape, s.dtype) for s in srcs], *[pltpu.HBM(s.shape, s.dtype) for s in lands],
                   jax.ShapeDtypeStruct((SUBLANES, LANES), f32)),
        in_specs=[HBM] * (2 * n), out_specs=(SEM, SEM, *[HBM] * (2 * n), pl.BlockSpec(memory_space=pltpu.VMEM)),
        input_output_aliases={i: 2 + i for i in range(2 * n)},
        compiler_params=pltpu.CompilerParams(has_side_effects=EFFECT),
    )(*[pltpu.with_memory_space_constraint(s, pltpu.HBM) for s in srcs], *[pltpu.with_memory_space_constraint(s, pltpu.HBM) for s in lands])
    return outs[0], outs[1], outs[2:2 + n], outs[2 + n:2 + 2 * n], outs[-1]


def _direct_wait(send_sems, recv_sems, srcs, lands, after, per_peer, name):
    n = len(srcs)

    def body(*refs):
        src_refs, land_refs = refs[:n], refs[n:2 * n]
        s_sems, r_sems = refs[2 * n], refs[2 * n + 1]
        for cp in _direct_copies(src_refs, land_refs, s_sems, r_sems, per_peer):
            cp.wait_send()
            cp.wait_recv()

    outs = pl.pallas_call(
        body, name=name,
        out_shape=tuple(pltpu.HBM(s.shape, s.dtype) for s in list(srcs) + list(lands)),
        in_specs=[HBM] * (2 * n) + [SEM, SEM, ANY], out_specs=tuple([HBM] * (2 * n)),
        input_output_aliases={i: i for i in range(2 * n)},
        compiler_params=pltpu.CompilerParams(has_side_effects=EFFECT),
    )(*srcs, *lands, send_sems, recv_sems, after)
    return outs[n:]


def _row(v, width=None):
    v = v.reshape(1, -1).astype(f32)
    if width is not None and v.shape[1] < width:
        v = jnp.pad(v, ((0, 0), (0, width - v.shape[1])))
    return v


def _layer_params(p, l):
    return dict(
        norm_mix=_row(p["norm_mix"][l]), norm_ffn=_row(p["norm_ffn"][l]), conv_w=p["conv_w"][l], conv_b=_row(p["conv_b"][l]),
        ssd=(_row(p["dt_bias"][l], LANES), _row(p["a_log"][l], LANES), _row(jnp.repeat(p["d_skip"][l], HEAD_DIM)), _row(p["ssm_norm"][l])))


def _layer_fwd(h, w_in, rest, sp, tabs, l):
    tag = f"l{l}_"
    hn = _rmsnorm_fwd(h, sp["norm_mix"], tag + "norm_mix")
    qkv, z, xbc_pre = _in_proj(hn, w_in, (QKV_WIDTH, SSM_INNER, CONV_CH), tag + "proj")
    dtp = _matmul(hn, w_in, mode="nn", n_out=LANES, tn=LANES, b_off=DT_OFF // LANES, name=tag + "proj_dt")
    prep = _attn_prep(qkv, tabs, tag + "attn_prep")
    o, o16, lse = _attn_fwd(prep, tag + "attn_fwd")
    xbc = _conv_fwd(xbc_pre, sp["conv_w"], sp["conv_b"], tag + "conv_fwd")
    yn, y, hs = _ssd_fwd(xbc, z, dtp, sp["ssd"], tag + "ssd_fwd")
    w_out, w_gate, w_up, w_down = rest(yn) if callable(rest) else rest
    h2 = _out_proj(o16, yn, w_out, h, tag + "out_proj")
    hn2 = _rmsnorm_fwd(h2, sp["norm_ffn"], tag + "norm_ffn")
    g, u, act = _swiglu_fwd(hn2, w_gate, w_up, tag + "ffn_up")
    h3 = _matmul(act, w_down, mode="nn", tk=FFN_HIDDEN, add=h2, name=tag + "ffn_down")
    saved = dict(h=h, hn=hn, prep=prep, z=z, xbc_pre=xbc_pre, dtp=dtp, o=o, o16=o16, lse=lse, xbc=xbc, yn=yn, y=y, hs=hs, h2=h2, hn2=hn2, g=g, u=u, act=act,
                 rest=(w_out, w_gate, w_up, w_down))
    return h3, saved


def _layer_bwd(dh3_pair, s, big, sp, tabs, l, gd=f32, after_ffn=None):
    tag = f"l{l}_"
    dh3, dh3b = dh3_pair
    w_in, w_out, w_gate, w_up, w_down = big
    dg, du = _swiglu_bwd(dh3b, w_down, s["g"], s["u"], tag + "ffn_down_bwd")
    dw_down = _matmul(s["act"], dh3b, mode="tn", tm=1408, tn=512, tk=2048, out_dtype=gd, name=tag + "dw_down")
    dw_gate = _matmul(dg, s["hn2"], mode="tn", tm=1408, tn=512, tk=2048, out_dtype=gd, name=tag + "dw_gate")
    dw_up = _matmul(du, s["hn2"], mode="tn", tm=1408, tn=512, tk=2048, out_dtype=gd, name=tag + "dw_up")
    norm_ffn = sp["norm_ffn"] if after_ffn is None else sp["norm_ffn"] + after_ffn(dict(w_gate=dw_gate, w_up=dw_up, w_down=dw_down))
    dh2, dh2b, dnf = _nt_norm_bwd([(dg, w_gate), (du, w_up)], s["h2"], norm_ffn, dh3, tag + "ffn_up_bwd_norm", tk=1408, b_is_kd=True,
                                  vmem=VMEM_LIMIT_TWO_PAIRS)
    d_o = _matmul(dh2b, w_out, mode="nt", n_out=ATTN_WIDTH, tn=512, b_off=0, name=tag + "out_attn_bwd")
    dyn = _matmul(dh2b, w_out, mode="nt", n_out=SSM_INNER, tn=512, b_off=1, name=tag + "out_ssm_bwd")
    dw_out = jnp.concatenate([_matmul(s["o16"], dh2b, mode="tn", tm=512, tn=512, tk=2048, out_dtype=gd, name=tag + "dw_out_attn"),
                              _matmul(s["yn"], dh2b, mode="tn", tm=512, tn=512, tk=2048, out_dtype=gd, name=tag + "dw_out_ssm")], axis=0)
    dxbc, dz, ddtp, dnw, dds, dal, dbi = _ssd_bwd(s["xbc"], s["z"], s["dtp"], s["y"], s["hs"], dyn, sp["ssd"], tag + "ssd_bwd")
    dxbc_pre, dconv_w, dconv_b = _conv_bwd(s["xbc_pre"], sp["conv_w"], sp["conv_b"], dxbc, tag + "conv_bwd")
    dq, dk, dv = _attn_bwd(s["prep"], tabs, s["o"], s["lse"], d_o, tag + "attn_bwd")
    dproj = jnp.concatenate([dq, dk, dv, dz, dxbc_pre, ddtp], axis=1)
    dw_in = _matmul(s["hn"], dproj, mode="tn", tm=512, tn=1152, tk=2048, out_dtype=gd, name=tag + "dw_in")
    res = _nt_norm_bwd([(dproj, w_in)], s["h"], sp["norm_mix"], dh2, tag + "proj_bwd_norm", tk=1152, bf16_copy=l > 0)
    dh, dhb, dnm = res if l > 0 else (res[0], None, res[1])
    grads = dict(
        norm_mix=dnm.sum(0), w_in=dw_in, conv_w=dconv_w, conv_b=dconv_b[0], dt_bias=dbi.sum(0)[:SSM_HEADS], a_log=dal.sum(0)[:SSM_HEADS],
        d_skip=dds.sum(0).reshape(SSM_HEADS, HEAD_DIM).sum(1), ssm_norm=dnw.sum(0), w_out=dw_out, norm_ffn=dnf.sum(0),
        w_gate=dw_gate, w_up=dw_up, w_down=dw_down)
    return (dh, dhb), grads


def _local_step(x, positions, target, p, bigs):
    tabs = _rope_tables(positions.reshape(-1, 1), "rope_tables")
    h = x
    saved, sps = [], []
    for l in range(DEPTH):
        sps.append(_layer_params(p, l))
        h, s = _layer_fwd(h, bigs[l][0], bigs[l][1:], sps[l], tabs, l)
        saved.append(s)
    dh, dhb, loss_parts, dfn = _final_loss(h, _row(p["final_norm"]), target, "final_loss")
    dh = (dh, dhb)
    layer_grads = [None] * DEPTH
    for l in reversed(range(DEPTH)):
        dh, layer_grads[l] = _layer_bwd(dh, saved[l], bigs[l], sps[l], tabs, l)
    grads = {k: [layer_grads[l][k] for l in range(DEPTH)] for k in layer_grads[0]}
    grads["final_norm"] = dfn.sum(0)
    return jnp.sum(loss_parts), dh[0], grads


BIG = ("w_in", "w_out", "w_gate", "w_up", "w_down")
REST = BIG[1:]
FFN = ("w_gate", "w_up", "w_down")
MIX = ("w_in", "w_out")
COL_SHARDED = ("w_in",)
TRANSPOSED = ("w_gate", "w_up")
SMALL = ("norm_mix", "conv_b", "dt_bias", "a_log", "d_skip", "ssm_norm", "norm_ffn", "final_norm")
WEIGHTS = ("norm_mix", "w_in", "conv_w", "conv_b", "dt_bias", "a_log", "d_skip", "ssm_norm", "w_out", "norm_ffn", "w_gate", "w_up", "w_down", "final_norm")
SMALL_ROWS = 88
CONVW_ROWS = 96
CONVW_SHARD_ROWS = 16


def _full_from_gathered(name, g, l):
    _, a, b = g.shape
    if name in COL_SHARDED:
        width = IN_PROJ_PAD if name == "w_in" else N_DEV * b
        return _cols_from_devices(g.reshape(N_DEV, 1, a, b), width, f"cols_l{l}_{name}").reshape(a, width)
    return g.reshape(N_DEV * a, b)


def _by_device(name, full, shard_shape, l):
    a, b = shard_shape
    if name in COL_SHARDED:
        return _devices_from_cols([full], b, f"devs_l{l}_{name}").reshape(N_CHIP, 2, a, b)
    return full.reshape(N_CHIP, 2, a, b)


def _pack_rows(parts, rows, width):
    flat = jnp.concatenate([q.reshape(-1) for q in parts])
    return jnp.pad(flat, (0, rows * width - flat.shape[0])).reshape(rows, width)


def _unpack(flat, like):
    out, off = [], 0
    for q in like:
        out.append(flat[off:off + q.size].reshape(q.shape))
        off += q.size
    return out


def kernel(x, positions, norm_mix, w_in, conv_w, conv_b, dt_bias, a_log, d_skip, ssm_norm, w_out, norm_ffn, w_gate, w_up, w_down, final_norm, loss_target, m_norm_mix, m_w_in, m_conv_w, m_conv_b, m_dt_bias, m_a_log, m_d_skip, m_ssm_norm, m_w_out, m_norm_ffn, m_w_gate, m_w_up, m_w_down, m_final_norm, v_norm_mix, v_w_in, v_conv_w, v_conv_b, v_dt_bias, v_a_log, v_d_skip, v_ssm_norm, v_w_out, v_norm_ffn, v_w_gate, v_w_up, v_w_down, v_final_norm):
    w = dict(norm_mix=norm_mix, w_in=w_in, conv_w=conv_w, conv_b=conv_b, dt_bias=dt_bias, a_log=a_log, d_skip=d_skip, ssm_norm=ssm_norm,
             w_out=w_out, norm_ffn=norm_ffn, w_gate=w_gate, w_up=w_up, w_down=w_down, final_norm=final_norm)
    m = dict(norm_mix=m_norm_mix, w_in=m_w_in, conv_w=m_conv_w, conv_b=m_conv_b, dt_bias=m_dt_bias, a_log=m_a_log, d_skip=m_d_skip,
             ssm_norm=m_ssm_norm, w_out=m_w_out, norm_ffn=m_norm_ffn, w_gate=m_w_gate, w_up=m_w_up, w_down=m_w_down, final_norm=m_final_norm)
    v = dict(norm_mix=v_norm_mix, w_in=v_w_in, conv_w=v_conv_w, conv_b=v_conv_b, dt_bias=v_dt_bias, a_log=v_a_log, d_skip=v_d_skip,
             ssm_norm=v_ssm_norm, w_out=v_w_out, norm_ffn=v_norm_ffn, w_gate=v_w_gate, w_up=v_w_up, w_down=v_w_down, final_norm=v_final_norm)
    ax, ay, ac = lax.axis_index("x"), lax.axis_index("y"), lax.axis_index("c")
    dev = 4 * ax + 2 * ay + ac

    assert DEPTH == 2
    t = x.shape[0] * x.shape[1]
    xf, target = x.reshape(t, D_MODEL), loss_target.reshape(t, D_MODEL)

    def own_slot(block):
        return lax.dynamic_update_slice(lax.empty((N_DEV,) + block.shape[1:], block.dtype), block, (dev,) + (0,) * (block.ndim - 1))

    def layer_shard(arr, k, l):
        return jnp.transpose(arr, (2, 0, 1))[:, l, :] if k in TRANSPOSED else arr[l]

    def gather_start(keys, l, tie, name):
        shards = [(layer_shard(w[keys[0]], keys[0], l) + tie).astype(bf16)] + [layer_shard(w[k], k, l).astype(bf16) for k in keys[1:]]
        return _direct_start(shards, [own_slot(s[None]) for s in shards], False, name)

    def scatter_start(keys, grads_l, l, name):
        shapes = [(w[k].shape[2], w[k].shape[1]) if k in TRANSPOSED else w[k].shape[1:] for k in keys]
        by_dev = [_by_device(k, grads_l[k], sh, l).reshape((N_DEV,) + sh) for k, sh in zip(keys, shapes)]
        return _direct_start(by_dev, [own_slot(lax.dynamic_slice_in_dim(g, dev, 1, 0)) for g in by_dev], True, name)

    (g_in0, conv_all), tie = _allgather_two_level([w["w_in"][0].astype(bf16), w["conv_w"]], "gather_l0_w_in")
    rest0_copy = gather_start(REST, 0, tie[0, 0], "gather_l0_rest_start")
    l1_copy = gather_start(BIG, 1, rest0_copy[4][0, 0], "gather_l1_start")
    p = {k: w[k] for k in SMALL}
    p["norm_mix"] = p["norm_mix"] + l1_copy[4][0, 0]
    p["conv_w"] = jnp.transpose(conv_all, (1, 2, 0, 3)).reshape(DEPTH, CONV_WIDTH, CONV_CH)
    sp0, sp1 = _layer_params(p, 0), _layer_params(p, 1)

    def rest0(after):
        lands = _direct_wait(*rest0_copy[:4], after, False, "gather_l0_rest_wait")
        return tuple(_full_from_gathered(k, g, 0) for k, g in zip(REST, lands))

    tabs = _rope_tables(positions.reshape(t, 1), "rope_tables")
    w_in0 = _full_from_gathered("w_in", g_in0, 0)
    h1, saved0 = _layer_fwd(xf, w_in0, rest0, sp0, tabs, 0)
    lands1 = _direct_wait(*l1_copy[:4], h1, False, "gather_l1_wait")
    bigs1 = tuple(_full_from_gathered(k, g, 1) for k, g in zip(BIG, lands1))
    h2, saved1 = _layer_fwd(h1, bigs1[0], bigs1[1:], sp1, tabs, 1)
    dh, dhb, loss_parts, dfn = _final_loss(h2, _row(p["final_norm"]), target, "final_loss")
    loss_local = jnp.sum(loss_parts)

    dh, grads1 = _layer_bwd((dh, dhb), saved1, bigs1, sp1, tabs, 1, gd=bf16)
    l1_grads = scatter_start(BIG, grads1, 1, "scatter_l1_start")
    w_out0, w_gate0, w_up0, w_down0 = saved0["rest"]
    bigs0 = (w_in0, w_out0, w_gate0, w_up0, w_down0 + l1_grads[4][0, 0].astype(bf16))
    ffn0_grads = []

    def after_ffn(grads_ffn):
        ffn0_grads.append(scatter_start(FFN, grads_ffn, 0, "scatter_l0_ffn_start"))
        return ffn0_grads[0][4][0, 0]

    (dx, _), grads0 = _layer_bwd(dh, saved0, bigs0, sp0, tabs, 0, gd=bf16, after_ffn=after_ffn)
    mix0_grads = scatter_start(MIX, grads0, 0, "scatter_l0_mix_start")
    landed = {(k, 1): g for k, g in zip(BIG, _direct_wait(*l1_grads[:4], dx, True, "scatter_l1_wait"))}
    landed.update({(k, 0): g for k, g in zip(FFN, _direct_wait(*ffn0_grads[0][:4], dx, True, "scatter_l0_ffn_wait"))})
    out_g, out_d, out_m, out_v = {}, {}, {}, {}

    def update(keys, after):
        for k in keys:
            parts = [landed[k, l] for l in range(DEPTH)]
            if k in TRANSPOSED:
                depth, a, b = w[k].shape
                state = [jnp.transpose(s, (2, 0, 1)).reshape(b, depth * a) for s in (w[k], m[k], v[k])]
                res = _adamw_layers(parts, *state, after, "adamw_" + k, layers_on_columns=True)
                res = [jnp.transpose(r.reshape(b, depth, a), (1, 2, 0)) for r in res]
            else:
                res = _adamw_layers(parts, w[k], m[k], v[k], after, "adamw_" + k)
            for dst, r in zip((out_g, out_d, out_m, out_v), res):
                dst[k] = r

    update(FFN, mix0_grads[4])
    grads = {k: [grads0[k], grads1[k]] for k in grads0 if k not in BIG}
    grads["final_norm"] = dfn.sum(0) + mix0_grads[4][0, 0]

    small_like = [w[k] for k in SMALL]
    small_grads = [jnp.stack(grads[k]) if k != "final_norm" else grads[k] for k in SMALL]
    small_pack = jnp.concatenate([_pack_rows(small_grads, SMALL_ROWS, LANES), _pack_rows([jnp.stack(grads["conv_w"])], CONVW_ROWS, LANES)], axis=0)
    parts = _allgather_direct(small_pack, "gather_small_grads")
    g_s, d_s, m_s, v_s = _adamw(
        [parts[i, :SMALL_ROWS] for i in range(N_DEV)], _pack_rows(small_like, SMALL_ROWS, LANES),
        _pack_rows([m[k] for k in SMALL], SMALL_ROWS, LANES), _pack_rows([v[k] for k in SMALL], SMALL_ROWS, LANES), "adamw_replicated")
    for dst, src in ((out_g, g_s), (out_d, d_s), (out_m, m_s), (out_v, v_s)):
        dst.update(zip(SMALL, _unpack(src.reshape(-1), small_like)))
    shard_w = conv_w.shape[-1]
    conv_parts = parts[:, SMALL_ROWS:].reshape(N_DEV, DEPTH, CONV_WIDTH, CONV_CH)
    conv_mine = lax.dynamic_slice_in_dim(conv_parts, dev * shard_w, shard_w, axis=3)
    g_c, d_c, m_c, v_c = _adamw(
        [_pack_rows([conv_mine[i]], CONVW_SHARD_ROWS, LANES) for i in range(N_DEV)], _pack_rows([conv_w], CONVW_SHARD_ROWS, LANES),
        _pack_rows([m["conv_w"]], CONVW_SHARD_ROWS, LANES), _pack_rows([v["conv_w"]], CONVW_SHARD_ROWS, LANES), "adamw_conv_w")
    for dst, src in ((out_g, g_c), (out_d, d_c), (out_m, m_c), (out_v, v_c)):
        dst["conv_w"] = src.reshape(-1)[:conv_w.size].reshape(conv_w.shape)

    landed.update({(k, 0): g for k, g in zip(MIX, _direct_wait(*mix0_grads[:4], v_c + out_v["w_down"][0, :CONVW_SHARD_ROWS, :LANES], True, "scatter_l0_mix_wait"))})
    update(MIX, v_c)

    loss = lax.psum(loss_local, ("x", "y", "c"))
    return (loss, dx.reshape(x.shape), *[out_g[k] for k in WEIGHTS], *[out_d[k] for k in WEIGHTS],
            *[out_m[k] for k in WEIGHTS], *[out_v[k] for k in WEIGHTS])
```
